```python
import jax, jax.numpy as jnp
from jax import lax
import numpy as np

D_MODEL = 1024
BATCH = 8
SEQ = 4096
DEPTH = 1

MIX_WIDTH = D_MODEL
POOL_WIDTH = MIX_WIDTH // 2
POOL_WINDOWS = (2, 4, 8, 16)
N_POOL_GROUPS = len(POOL_WINDOWS)
POOL_GROUP_DIM = POOL_WIDTH // N_POOL_GROUPS
ATTN_WIDTH = MIX_WIDTH - POOL_WIDTH
HEAD_DIM = 64
N_Q_HEADS = ATTN_WIDTH // HEAD_DIM
N_KV_HEADS = 2
GQA_GROUP = N_Q_HEADS // N_KV_HEADS
WINDOW = 128
BLOCK = 128
N_BUCKETS = 32
MAX_EXACT = N_BUCKETS // 2
MAX_DISTANCE = 128
IN_WIDTH = POOL_WIDTH + N_Q_HEADS * HEAD_DIM + 2 * N_KV_HEADS * HEAD_DIM
D_FF = -(-(8 * D_MODEL) // (3 * 256)) * 256
EPS = 1e-6
NEG_INF = -1e30

kernel_name = "hybrid_pool_swa_sink_t5bias_block"


def rmsnorm(x, g):
    xf = x.astype(jnp.float32)
    xf = xf * lax.rsqrt(jnp.mean(xf * xf, axis=-1, keepdims=True) + EPS)
    return xf.astype(x.dtype) * g


def multiscale_pool(u, w_pool, pool_scale):
    b, s, _ = u.shape
    ug = u.reshape(b, s, N_POOL_GROUPS, POOL_GROUP_DIM)
    csum = jnp.cumsum(ug.astype(jnp.float32), axis=1)
    pos = jnp.arange(1, s + 1, dtype=jnp.float32)
    pooled = []
    for g, w in enumerate(POOL_WINDOWS):
        c = csum[:, :, g]
        lagged = jnp.pad(c, ((0, 0), (w, 0), (0, 0)))[:, :s]
        count = jnp.minimum(pos, float(w))[None, :, None]
        pooled.append((c - lagged) / count)
    pooled = jnp.stack(pooled, axis=2).astype(u.dtype) - ug
    mixed = jnp.einsum('bsgc,gcd->bsgd', pooled, w_pool)
    return mixed.reshape(b, s, POOL_WIDTH) * pool_scale


def relative_bias_band(rel_bias):
    qi = jnp.arange(BLOCK)[:, None]
    kj = jnp.arange(2 * BLOCK)[None, :]
    dist = qi + BLOCK - kj
    n = jnp.maximum(dist, 0)
    nf = jnp.maximum(n, 1).astype(jnp.float32)
    large = MAX_EXACT + (jnp.log(nf / MAX_EXACT) / np.float32(np.log(MAX_DISTANCE / MAX_EXACT))
                         * (N_BUCKETS - MAX_EXACT)).astype(jnp.int32)
    large = jnp.minimum(large, N_BUCKETS - 1)
    bucket = jnp.where(n < MAX_EXACT, n, large)
    bias = rel_bias.astype(jnp.float32)[bucket]
    bias = jnp.transpose(bias, (2, 0, 1)).reshape(N_KV_HEADS, GQA_GROUP, BLOCK, 2 * BLOCK)
    in_window = (dist >= 0) & (dist < WINDOW)
    return bias, in_window


def sliding_window_attention(q, k, v, rel_bias, sinks):
    b, s = q.shape[:2]
    nb = s // BLOCK
    qb = q.reshape(b, nb, BLOCK, N_KV_HEADS, GQA_GROUP, HEAD_DIM)

    def band(t):
        prev = jnp.pad(t, ((0, 0), (BLOCK, 0), (0, 0), (0, 0)))[:, :s]
        prev = prev.reshape(b, nb, BLOCK, N_KV_HEADS, HEAD_DIM)
        cur = t.reshape(b, nb, BLOCK, N_KV_HEADS, HEAD_DIM)
        return jnp.concatenate([prev, cur], axis=2)

    kb, vb = band(k), band(v)
    bias, in_window = relative_bias_band(rel_bias)
    key_exists = (jnp.arange(nb)[:, None] > 0) | (jnp.arange(2 * BLOCK)[None, :] >= BLOCK)
    mask = in_window[None] & key_exists[:, None, :]
    scale = 1.0 / np.sqrt(HEAD_DIM).astype(np.float32)
    logits = jnp.einsum('bnqhgd,bnkhd->bnhgqk', qb, kb).astype(jnp.float32) * scale
    logits = logits + bias[None, None]
    logits = jnp.where(mask[None, :, None, None], logits, NEG_INF)
    sink = sinks.astype(jnp.float32).reshape(N_KV_HEADS, GQA_GROUP)[None, None, :, :, None, None]
    m = jnp.maximum(jnp.max(logits, axis=-1, keepdims=True), sink)
    p = jnp.exp(logits - m)
    denom = jnp.sum(p, axis=-1, keepdims=True) + jnp.exp(sink - m)
    probs = (p / denom).astype(v.dtype)
    out = jnp.einsum('bnhgqk,bnkhd->bnqhgd', probs, vb)
    return out.reshape(b, s, ATTN_WIDTH)


def _fwd_setup_inputs(seed: int = 0) -> dict:
    key = jax.random.key(seed)
    ks = jax.random.split(key, 16)
    f32 = jnp.float32

    def w(k, shape, fan_in):
        return jax.random.normal(k, shape, f32) * (fan_in ** -0.5)

    def gain(k, shape):
        return 1.0 + 0.05 * jax.random.normal(k, shape, f32)

    return {
        "x": jax.random.normal(ks[0], (BATCH, SEQ, D_MODEL), f32),
        "g_pre_mix": gain(ks[1], (DEPTH, D_MODEL)),
        "w_in": w(ks[2], (DEPTH, D_MODEL, IN_WIDTH), D_MODEL),
        "w_pool": w(ks[3], (DEPTH, N_POOL_GROUPS, POOL_GROUP_DIM, POOL_GROUP_DIM), POOL_GROUP_DIM),
        "pool_scale": 1.0 + 0.1 * jax.random.normal(ks[4], (DEPTH, POOL_WIDTH), f32),
        "rel_bias": 0.5 * jax.random.normal(ks[5], (N_BUCKETS, N_Q_HEADS), f32),
        "sinks": 0.5 * jax.random.normal(ks[6], (DEPTH, N_Q_HEADS), f32),
        "w_out": w(ks[7], (DEPTH, MIX_WIDTH, D_MODEL), MIX_WIDTH),
        "g_post_mix": gain(ks[8], (DEPTH, D_MODEL)),
        "g_pre_ffn": gain(ks[9], (DEPTH, D_MODEL)),
        "w_gate": w(ks[10], (DEPTH, D_MODEL, D_FF), D_MODEL),
        "w_up": w(ks[11], (DEPTH, D_MODEL, D_FF), D_MODEL),
        "w_down": w(ks[12], (DEPTH, D_FF, D_MODEL), D_FF),
        "g_post_ffn": gain(ks[13], (DEPTH, D_MODEL)),
    }


def _fwd_reference(x, g_pre_mix, w_in, w_pool, pool_scale, rel_bias, sinks, w_out,
              g_post_mix, g_pre_ffn, w_gate, w_up, w_down, g_post_ffn):
    b, s, _ = x.shape
    q_end = POOL_WIDTH + N_Q_HEADS * HEAD_DIM
    k_end = q_end + N_KV_HEADS * HEAD_DIM
    for l in range(DEPTH):
        h = rmsnorm(x, g_pre_mix[l])
        proj = h @ w_in[l]
        u = proj[..., :POOL_WIDTH]
        q = proj[..., POOL_WIDTH:q_end].reshape(b, s, N_Q_HEADS, HEAD_DIM)
        k = proj[..., q_end:k_end].reshape(b, s, N_KV_HEADS, HEAD_DIM)
        v = proj[..., k_end:].reshape(b, s, N_KV_HEADS, HEAD_DIM)
        pool_out = multiscale_pool(u, w_pool[l], pool_scale[l])
        attn_out = sliding_window_attention(q, k, v, rel_bias, sinks[l])
        mix = jnp.concatenate([pool_out, attn_out], axis=-1) @ w_out[l]
        x = x + rmsnorm(mix, g_post_mix[l])
        h = rmsnorm(x, g_pre_ffn[l])
        f = (jax.nn.silu(h @ w_gate[l]) * (h @ w_up[l])) @ w_down[l]
        x = x + rmsnorm(f, g_post_ffn[l])
    return x


import jax as _jax
import jax.numpy as _jnp

TWIN_FORMAT = 'train_step'
FWD_PARAMS = ['x', 'g_pre_mix', 'w_in', 'w_pool', 'pool_scale', 'rel_bias', 'sinks', 'w_out', 'g_post_mix', 'g_pre_ffn', 'w_gate', 'w_up', 'w_down', 'g_post_ffn']
TWIN_WEIGHTS = ['g_pre_mix', 'w_in', 'w_pool', 'pool_scale', 'rel_bias', 'sinks', 'w_out', 'g_post_mix', 'g_pre_ffn', 'w_gate', 'w_up', 'w_down', 'g_post_ffn']
TWIN_DIFF_INPUT = 'x'
TWIN_INPUTS = ['x', 'g_pre_mix', 'w_in', 'w_pool', 'pool_scale', 'rel_bias', 'sinks', 'w_out', 'g_post_mix', 'g_pre_ffn', 'w_gate', 'w_up', 'w_down', 'g_post_ffn', 'loss_target', 'm_g_pre_mix', 'm_w_in', 'm_w_pool', 'm_pool_scale', 'm_rel_bias', 'm_sinks', 'm_w_out', 'm_g_post_mix', 'm_g_pre_ffn', 'm_w_gate', 'm_w_up', 'm_w_down', 'm_g_post_ffn', 'v_g_pre_mix', 'v_w_in', 'v_w_pool', 'v_pool_scale', 'v_rel_bias', 'v_sinks', 'v_w_out', 'v_g_post_mix', 'v_g_pre_ffn', 'v_w_gate', 'v_w_up', 'v_w_down', 'v_g_post_ffn']
TWIN_OUTPUTS = ['loss', 'grad_x', 'grad_g_pre_mix', 'grad_w_in', 'grad_w_pool', 'grad_pool_scale', 'grad_rel_bias', 'grad_sinks', 'grad_w_out', 'grad_g_post_mix', 'grad_g_pre_ffn', 'grad_w_gate', 'grad_w_up', 'grad_w_down', 'grad_g_post_ffn', 'delta_g_pre_mix', 'delta_w_in', 'delta_w_pool', 'delta_pool_scale', 'delta_rel_bias', 'delta_sinks', 'delta_w_out', 'delta_g_post_mix', 'delta_g_pre_ffn', 'delta_w_gate', 'delta_w_up', 'delta_w_down', 'delta_g_post_ffn', 'new_m_g_pre_mix', 'new_m_w_in', 'new_m_w_pool', 'new_m_pool_scale', 'new_m_rel_bias', 'new_m_sinks', 'new_m_w_out', 'new_m_g_post_mix', 'new_m_g_pre_ffn', 'new_m_w_gate', 'new_m_w_up', 'new_m_w_down', 'new_m_g_post_ffn', 'new_v_g_pre_mix', 'new_v_w_in', 'new_v_w_pool', 'new_v_pool_scale', 'new_v_rel_bias', 'new_v_sinks', 'new_v_w_out', 'new_v_g_post_mix', 'new_v_g_pre_ffn', 'new_v_w_gate', 'new_v_w_up', 'new_v_w_down', 'new_v_g_post_ffn']
TWIN_LEAF_KINDS = {'loss': 'loss', 'grad_x': 'grad_x', 'grad_g_pre_mix': 'grad_w', 'grad_w_in': 'grad_w', 'grad_w_pool': 'grad_w', 'grad_pool_scale': 'grad_w', 'grad_rel_bias': 'grad_w', 'grad_sinks': 'grad_w', 'grad_w_out': 'grad_w', 'grad_g_post_mix': 'grad_w', 'grad_g_pre_ffn': 'grad_w', 'grad_w_gate': 'grad_w', 'grad_w_up': 'grad_w', 'grad_w_down': 'grad_w', 'grad_g_post_ffn': 'grad_w', 'delta_g_pre_mix': 'delta_w', 'delta_w_in': 'delta_w', 'delta_w_pool': 'delta_w', 'delta_pool_scale': 'delta_w', 'delta_rel_bias': 'delta_w', 'delta_sinks': 'delta_w', 'delta_w_out': 'delta_w', 'delta_g_post_mix': 'delta_w', 'delta_g_pre_ffn': 'delta_w', 'delta_w_gate': 'delta_w', 'delta_w_up': 'delta_w', 'delta_w_down': 'delta_w', 'delta_g_post_ffn': 'delta_w', 'new_m_g_pre_mix': 'new_m', 'new_m_w_in': 'new_m', 'new_m_w_pool': 'new_m', 'new_m_pool_scale': 'new_m', 'new_m_rel_bias': 'new_m', 'new_m_sinks': 'new_m', 'new_m_w_out': 'new_m', 'new_m_g_post_mix': 'new_m', 'new_m_g_pre_ffn': 'new_m', 'new_m_w_gate': 'new_m', 'new_m_w_up': 'new_m', 'new_m_w_down': 'new_m', 'new_m_g_post_ffn': 'new_m', 'new_v_g_pre_mix': 'new_v', 'new_v_w_in': 'new_v', 'new_v_w_pool': 'new_v', 'new_v_pool_scale': 'new_v', 'new_v_rel_bias': 'new_v', 'new_v_sinks': 'new_v', 'new_v_w_out': 'new_v', 'new_v_g_post_mix': 'new_v', 'new_v_g_pre_ffn': 'new_v', 'new_v_w_gate': 'new_v', 'new_v_w_up': 'new_v', 'new_v_w_down': 'new_v', 'new_v_g_post_ffn': 'new_v'}


def _forward(args):
    return _fwd_reference(*[args[k] for k in FWD_PARAMS])


def _output_shape():
    out = _jax.eval_shape(lambda: _forward(_fwd_setup_inputs(0)))
    return out.shape, out.dtype

N_MICROBATCH = 1
ADAM_LR = 0.001
ADAM_B1 = 0.9
ADAM_B2 = 0.999
ADAM_EPS = 1e-08
ADAM_WD = 0.01
ADAM_STEP = 10
PER_EXAMPLE_BATCH_AXIS = {'x': 0, 'loss_target': 0}
SHARED_INPUTS = []
_WEIGHT_DTYPES = {'g_pre_mix': _jnp.float32, 'w_in': _jnp.float32, 'w_pool': _jnp.float32, 'pool_scale': _jnp.float32, 'rel_bias': _jnp.float32, 'sinks': _jnp.float32, 'w_out': _jnp.float32, 'g_post_mix': _jnp.float32, 'g_pre_ffn': _jnp.float32, 'w_gate': _jnp.float32, 'w_up': _jnp.float32, 'w_down': _jnp.float32, 'g_post_ffn': _jnp.float32}
MOMENT_SCALE = {'g_pre_mix': 7.595392e-01, 'w_in': 6.490750e-01, 'w_pool': 1.140898e+00, 'pool_scale': 1.467805e+00, 'rel_bias': 1.675682e-01, 'sinks': 8.593109e-02, 'w_out': 8.728253e-01, 'g_post_mix': 3.212627e+01, 'g_pre_ffn': 6.164312e-01, 'w_gate': 1.948190e-01, 'w_up': 3.419367e-01, 'w_down': 5.700159e-01, 'g_post_ffn': 3.209296e+01}


def _to_microbatches(a, axis):
    t = _jnp.moveaxis(a, axis, 0)
    t = t.reshape((N_MICROBATCH, t.shape[0] // N_MICROBATCH) + t.shape[1:])
    return _jnp.moveaxis(t, 1, axis + 1)


def setup_inputs(seed: int = 0) -> dict:
    inp = _fwd_setup_inputs(seed)
    key = _jax.random.fold_in(_jax.random.key(seed), 7919)
    shape, _ = _output_shape()
    out = dict(inp)
    out["loss_target"] = _jax.random.normal(_jax.random.fold_in(key, 0), shape, _jnp.float32)
    for i, name in enumerate(TWIN_WEIGHTS):
        w = inp[name].astype(_jnp.float32)
        if MOMENT_SCALE is None:
            s = _jnp.sqrt(_jnp.mean(_jnp.square(w)) + 1e-30)
        else:
            s = MOMENT_SCALE[name]
        km, kv = _jax.random.split(_jax.random.fold_in(key, i + 1))
        out[name] = w
        out["m_" + name] = s * _jax.random.normal(km, w.shape, _jnp.float32)
        out["v_" + name] = (s * s) * _jax.random.uniform(kv, w.shape, _jnp.float32, 0.5, 1.5)
    if N_MICROBATCH > 1:
        for name, axis in PER_EXAMPLE_BATCH_AXIS.items():
            out[name] = _to_microbatches(out[name], axis)
    return {'x': out['x'], 'g_pre_mix': out['g_pre_mix'], 'w_in': out['w_in'], 'w_pool': out['w_pool'], 'pool_scale': out['pool_scale'], 'rel_bias': out['rel_bias'], 'sinks': out['sinks'], 'w_out': out['w_out'], 'g_post_mix': out['g_post_mix'], 'g_pre_ffn': out['g_pre_ffn'], 'w_gate': out['w_gate'], 'w_up': out['w_up'], 'w_down': out['w_down'], 'g_post_ffn': out['g_post_ffn'], 'loss_target': out['loss_target'], 'm_g_pre_mix': out['m_g_pre_mix'], 'm_w_in': out['m_w_in'], 'm_w_pool': out['m_w_pool'], 'm_pool_scale': out['m_pool_scale'], 'm_rel_bias': out['m_rel_bias'], 'm_sinks': out['m_sinks'], 'm_w_out': out['m_w_out'], 'm_g_post_mix': out['m_g_post_mix'], 'm_g_pre_ffn': out['m_g_pre_ffn'], 'm_w_gate': out['m_w_gate'], 'm_w_up': out['m_w_up'], 'm_w_down': out['m_w_down'], 'm_g_post_ffn': out['m_g_post_ffn'], 'v_g_pre_mix': out['v_g_pre_mix'], 'v_w_in': out['v_w_in'], 'v_w_pool': out['v_w_pool'], 'v_pool_scale': out['v_pool_scale'], 'v_rel_bias': out['v_rel_bias'], 'v_sinks': out['v_sinks'], 'v_w_out': out['v_w_out'], 'v_g_post_mix': out['v_g_post_mix'], 'v_g_pre_ffn': out['v_g_pre_ffn'], 'v_w_gate': out['v_w_gate'], 'v_w_up': out['v_w_up'], 'v_w_down': out['v_w_down'], 'v_g_post_ffn': out['v_g_post_ffn']}


def _loss(weights, diff, rest, loss_target):
    with _jax.named_scope("forward"):
        args = {**rest, TWIN_DIFF_INPUT: diff, **{k: w.astype(_WEIGHT_DTYPES[k]) for k, w in weights.items()}}
        y = _forward(args)
    with _jax.named_scope("loss_head"):
        err = _jnp.square(y.astype(_jnp.float32) - loss_target)
        return 0.5 * _jnp.sum(_jnp.mean(err, axis=-1)) if err.ndim else 0.5 * err


def _adamw(w, g, m, v):
    m = ADAM_B1 * m + (1.0 - ADAM_B1) * g
    v = ADAM_B2 * v + (1.0 - ADAM_B2) * _jnp.square(g)
    m_hat = m / (1.0 - ADAM_B1 ** ADAM_STEP)
    v_hat = v / (1.0 - ADAM_B2 ** ADAM_STEP)
    delta = -ADAM_LR * (m_hat / (_jnp.sqrt(v_hat) + ADAM_EPS) + ADAM_WD * w)
    return delta, m, v


def reference(x, g_pre_mix, w_in, w_pool, pool_scale, rel_bias, sinks, w_out, g_post_mix, g_pre_ffn, w_gate, w_up, w_down, g_post_ffn, loss_target, m_g_pre_mix, m_w_in, m_w_pool, m_pool_scale, m_rel_bias, m_sinks, m_w_out, m_g_post_mix, m_g_pre_ffn, m_w_gate, m_w_up, m_w_down, m_g_post_ffn, v_g_pre_mix, v_w_in, v_w_pool, v_pool_scale, v_rel_bias, v_sinks, v_w_out, v_g_post_mix, v_g_pre_ffn, v_w_gate, v_w_up, v_w_down, v_g_post_ffn):
    given = dict(x=x, g_pre_mix=g_pre_mix, w_in=w_in, w_pool=w_pool, pool_scale=pool_scale, rel_bias=rel_bias, sinks=sinks, w_out=w_out, g_post_mix=g_post_mix, g_pre_ffn=g_pre_ffn, w_gate=w_gate, w_up=w_up, w_down=w_down, g_post_ffn=g_post_ffn, loss_target=loss_target, m_g_pre_mix=m_g_pre_mix, m_w_in=m_w_in, m_w_pool=m_w_pool, m_pool_scale=m_pool_scale, m_rel_bias=m_rel_bias, m_sinks=m_sinks, m_w_out=m_w_out, m_g_post_mix=m_g_post_mix, m_g_pre_ffn=m_g_pre_ffn, m_w_gate=m_w_gate, m_w_up=m_w_up, m_w_down=m_w_down, m_g_post_ffn=m_g_post_ffn, v_g_pre_mix=v_g_pre_mix, v_w_in=v_w_in, v_w_pool=v_w_pool, v_pool_scale=v_pool_scale, v_rel_bias=v_rel_bias, v_sinks=v_sinks, v_w_out=v_w_out, v_g_post_mix=v_g_post_mix, v_g_pre_ffn=v_g_pre_ffn, v_w_gate=v_w_gate, v_w_up=v_w_up, v_w_down=v_w_down, v_g_post_ffn=v_g_post_ffn)
    weights = {n: given[n] for n in TWIN_WEIGHTS}
    shared = {n: given[n] for n in SHARED_INPUTS}
    per_example = {n: given[n] for n in ['x']}
    grad_fn = _jax.value_and_grad(_loss, argnums=(0, 1))

    def one_microbatch(ex, loss_target):
        ex = dict(ex)
        diff = ex.pop(TWIN_DIFF_INPUT)
        return grad_fn(weights, diff, {**shared, **ex}, loss_target)

    if N_MICROBATCH == 1:
        loss, (grad_w, grad_x) = one_microbatch(per_example, given["loss_target"])
    else:
        def body(carry, xs):
            loss_sum, grad_sum = carry
            l_k, (gw_k, gx_k) = one_microbatch(xs[0], xs[1])
            with _jax.named_scope("update"):
                return (loss_sum + l_k, _jax.tree.map(_jnp.add, grad_sum, gw_k)), gx_k

        init = (_jnp.zeros((), _jnp.float32), _jax.tree.map(_jnp.zeros_like, weights))
        (loss, grad_w), grad_x = _jax.lax.scan(body, init, (per_example, given["loss_target"]))
    with _jax.named_scope("update"):
        delta_w, new_m, new_v = {}, {}, {}
        for n in TWIN_WEIGHTS:
            delta_w[n], new_m[n], new_v[n] = _adamw(weights[n], grad_w[n], given["m_" + n], given["v_" + n])
    return (loss, grad_x, *[grad_w[n] for n in TWIN_WEIGHTS], *[delta_w[n] for n in TWIN_WEIGHTS],
            *[new_m[n] for n in TWIN_WEIGHTS], *[new_v[n] for n in TWIN_WEIGHTS])
```

```python
import functools

import numpy as np
import jax
import jax.numpy as jnp
from jax import lax
from jax.experimental import pallas as pl
from jax.experimental.pallas import tpu as pltpu

F32 = jnp.float32
BF16 = jnp.bfloat16

N_DEV = 8
POOL_WIDTH = 512
POOL_WINDOWS = (2, 4, 8, 16)
POOL_GROUP_DIM = 128
HEAD_DIM = 64
N_Q_HEADS = 8
N_KV_HEADS = 2
GQA_GROUP = 4
BLOCK = 128
N_BUCKETS = 32
MAX_EXACT = 16
MAX_DISTANCE = 128
EPS = 1e-6
NEG_INF = -1e30
ATTN_SCALE = float(1.0 / np.sqrt(np.float32(HEAD_DIM)))

ADAM_LR = 0.001
ADAM_B1 = 0.9
ADAM_B2 = 0.999
ADAM_EPS = 1e-08
ADAM_WD = 0.01
ADAM_STEP = 10

TOKEN_TILE = 512
FF_TILE = 1408
VMEM_LIMIT = 56 * 1024 * 1024
MESH = pl.DeviceIdType.MESH


def _params(**kw):
    return pltpu.CompilerParams(vmem_limit_bytes=VMEM_LIMIT, **kw)


def _dot(a, b):
    return jnp.dot(a, b, preferred_element_type=F32)


def _dot_nt(a, b):
    return lax.dot_general(a, b, (((1,), (1,)), ((), ())), preferred_element_type=F32)


def _dot_tn(a, b):
    return lax.dot_general(a, b, (((0,), (0,)), ((), ())), preferred_element_type=F32)


def _dot_f32(a, b):
    return jnp.dot(a, b, preferred_element_type=F32, precision=lax.Precision.HIGHEST)


def _rstd(v):
    return lax.rsqrt(jnp.mean(v * v, axis=-1, keepdims=True) + EPS)


def _norm_bwd(dout, v, r, g):
    vn = v * r
    dn = dout * g
    dv = r * (dn - vn * jnp.mean(dn * vn, axis=-1, keepdims=True))
    return dv, dout * vn


def all_gather_rows(shard, name):
    r, c_ = shard.shape

    def body(x_ref, out_ref, send_sems, recv_sems, local_sem):
        x, y, c = lax.axis_index("x"), lax.axis_index("y"), lax.axis_index("c")
        me, sibling = (x, y, c), (x, y, 1 - c)
        chips = [(1 - x, y), (x, 1 - y), (1 - x, 1 - y)]

        def slot(px, py, pc):
            return out_ref.at[4 * px + 2 * py + pc]

        def copy(k, block, to, src=None):
            return pltpu.make_async_remote_copy(
                src_ref=slot(*block) if src is None else src, dst_ref=slot(*block),
                send_sem=send_sems.at[k], recv_sem=recv_sems.at[k], device_id=to, device_id_type=MESH)

        mine = pltpu.make_async_copy(x_ref, slot(*me), local_sem)
        mine.start()
        first = [copy(0, me, sibling, src=x_ref)]
        first += [copy(1 + j, me, (*chip, c), src=x_ref) for j, chip in enumerate(chips)]
        for cp in first:
            cp.start()
        passed = [copy(4 + j, (*chip, c), sibling) for j, chip in enumerate(chips)]
        for j, chip in enumerate(chips):
            copy(1 + j, (*chip, c), me).wait_recv()
            passed[j].start()
        copy(0, sibling, me).wait_recv()
        for j, chip in enumerate(chips):
            copy(4 + j, (*chip, 1 - c), me).wait_recv()
        for cp in first + passed:
            cp.wait_send()
        mine.wait()

    return pl.pallas_call(
        body, name=name,
        out_shape=jax.ShapeDtypeStruct((N_DEV, r, c_), shard.dtype),
        in_specs=[pl.BlockSpec(memory_space=pl.ANY)],
        out_specs=pl.BlockSpec(memory_space=pl.ANY),
        scratch_shapes=[pltpu.SemaphoreType.DMA((7,)), pltpu.SemaphoreType.DMA((7,)), pltpu.SemaphoreType.DMA],
    )(shard)


def exchange_partials(parts, name):
    _, r, c_ = parts.shape

    def body(p_ref, out_ref, send_sems, recv_sems, local_sem):
        x, y, c = lax.axis_index("x"), lax.axis_index("y"), lax.axis_index("c")
        my_id = 4 * x + 2 * y + c
        mine = pltpu.make_async_copy(p_ref.at[my_id], out_ref.at[my_id], local_sem)
        mine.start()
        copies = []
        for k in range(1, N_DEV):
            px, py, pc = x ^ (k >> 2), y ^ ((k >> 1) & 1), c ^ (k & 1)
            peer_id = 4 * px + 2 * py + pc
            cp = pltpu.make_async_remote_copy(
                src_ref=p_ref.at[peer_id], dst_ref=out_ref.at[my_id],
                send_sem=send_sems.at[k - 1], recv_sem=recv_sems.at[k - 1], device_id=(px, py, pc), device_id_type=MESH)
            cp.start()
            copies.append(cp)
        for cp in copies:
            cp.wait()
        mine.wait()

    return pl.pallas_call(
        body, name=name,
        out_shape=jax.ShapeDtypeStruct(parts.shape, parts.dtype),
        in_specs=[pl.BlockSpec(memory_space=pl.ANY)],
        out_specs=pl.BlockSpec(memory_space=pl.ANY),
        scratch_shapes=[pltpu.SemaphoreType.DMA((7,)), pltpu.SemaphoreType.DMA((7,)), pltpu.SemaphoreType.DMA],
    )(parts)


def _adamw(w, g, m, v):
    m2 = ADAM_B1 * m + (1.0 - ADAM_B1) * g
    v2 = ADAM_B2 * v + (1.0 - ADAM_B2) * (g * g)
    m_hat = m2 / (1.0 - ADAM_B1 ** ADAM_STEP)
    v_hat = v2 / (1.0 - ADAM_B2 ** ADAM_STEP)
    delta = -ADAM_LR * (m_hat / (jnp.sqrt(v_hat) + ADAM_EPS) + ADAM_WD * w)
    return delta, m2, v2


def small_allreduce_adamw(part, w, m, v):
    r, c_ = part.shape

    def body(p_ref, w_ref, m_ref, v_ref, g_ref, d_ref, m2_ref, v2_ref, gat_ref, send_sems, recv_sems):
        x, y, c = lax.axis_index("x"), lax.axis_index("y"), lax.axis_index("c")
        me, sibling = (x, y, c), (x, y, 1 - c)
        chips = [(1 - x, y), (x, 1 - y), (1 - x, 1 - y)]

        def slot(px, py, pc):
            return gat_ref.at[4 * px + 2 * py + pc]

        def copy(k, block, to, src=None):
            return pltpu.make_async_remote_copy(
                src_ref=slot(*block) if src is None else src, dst_ref=slot(*block),
                send_sem=send_sems.at[k], recv_sem=recv_sems.at[k], device_id=to, device_id_type=MESH)

        first = [copy(0, me, sibling, src=p_ref)]
        first += [copy(1 + j, me, (*chip, c), src=p_ref) for j, chip in enumerate(chips)]
        for cp in first:
            cp.start()
        passed = [copy(4 + j, (*chip, c), sibling) for j, chip in enumerate(chips)]
        for j, chip in enumerate(chips):
            copy(1 + j, (*chip, c), me).wait_recv()
            passed[j].start()
        copy(0, sibling, me).wait_recv()
        for j, chip in enumerate(chips):
            copy(4 + j, (*chip, 1 - c), me).wait_recv()
        for cp in first + passed:
            cp.wait_send()
        gat_ref[4 * x + 2 * y + c] = p_ref[...]
        total = gat_ref[0]
        for s in range(1, N_DEV):
            total = total + gat_ref[s]
        g_ref[...] = total
        delta, m2, v2 = _adamw(w_ref[...], total, m_ref[...], v_ref[...])
        d_ref[...] = delta
        m2_ref[...] = m2
        v2_ref[...] = v2

    vm = pl.BlockSpec(memory_space=pltpu.VMEM)
    return pl.pallas_call(
        body, name="small_allreduce_adamw",
        out_shape=[jax.ShapeDtypeStruct((r, c_), F32)] * 4,
        in_specs=[vm] * 4, out_specs=[vm] * 4,
        scratch_shapes=[pltpu.VMEM((N_DEV, r, c_), F32), pltpu.SemaphoreType.DMA((7,)), pltpu.SemaphoreType.DMA((7,))],
    )(part, w, m, v)


def sum_adamw(recv, w, m, v):
    _, r, c_ = recv.shape
    tr = 192

    def body(p_ref, w_ref, m_ref, v_ref, g_ref, d_ref, m2_ref, v2_ref):
        total = p_ref[0].astype(F32)
        for s in range(1, N_DEV):
            total = total + p_ref[s].astype(F32)
        g_ref[...] = total
        delta, m2, v2 = _adamw(w_ref[...], total, m_ref[...], v_ref[...])
        d_ref[...] = delta
        m2_ref[...] = m2
        v2_ref[...] = v2

    row = pl.BlockSpec((tr, c_), lambda i: (i, 0))
    return pl.pallas_call(
        body, name="sum_adamw", grid=(r // tr,),
        out_shape=[jax.ShapeDtypeStruct((r, c_), F32)] * 4,
        in_specs=[pl.BlockSpec((N_DEV, tr, c_), lambda i: (0, i, 0)), row, row, row],
        out_specs=[row] * 4,
        compiler_params=_params(),
    )(recv, w, m, v)


def norm_inproj(x, g, w):
    t, d = x.shape
    n = w.shape[1]
    tm = TOKEN_TILE

    def body(x_ref, g_ref, w_ref, proj_ref, h_ref):
        xv = x_ref[...]
        h = ((xv * _rstd(xv)) * g_ref[...]).astype(BF16)
        h_ref[...] = h
        proj_ref[...] = _dot(h, w_ref[...])

    return pl.pallas_call(
        body, name="norm_inproj", grid=(t // tm,),
        out_shape=[jax.ShapeDtypeStruct((t, n), F32), jax.ShapeDtypeStruct((t, d), BF16)],
        in_specs=[pl.BlockSpec((tm, d), lambda i: (i, 0)), pl.BlockSpec((1, d), lambda i: (0, 0)),
                  pl.BlockSpec((d, n), lambda i: (0, 0))],
        out_specs=[pl.BlockSpec((tm, n), lambda i: (i, 0)), pl.BlockSpec((tm, d), lambda i: (i, 0))],
        compiler_params=_params(),
    )(x, g, w)


def bias_band(bucket, in_window, rel_bias):
    def body(bk_ref, win_ref, rb_ref, out_ref):
        bk = bk_ref[...]
        keep = win_ref[...] > 0.5
        for h in range(N_Q_HEADS):
            acc = jnp.zeros(bk.shape, F32)
            for b in range(N_BUCKETS):
                acc = jnp.where(bk == float(b), rb_ref[b, h], acc)
            out_ref[h] = jnp.where(keep, acc, NEG_INF)

    vm = pl.BlockSpec(memory_space=pltpu.VMEM)
    return pl.pallas_call(
        body, name="bias_band",
        out_shape=jax.ShapeDtypeStruct((N_Q_HEADS, BLOCK, 2 * BLOCK), F32),
        in_specs=[vm, vm, pl.BlockSpec(memory_space=pltpu.SMEM)], out_specs=vm,
    )(bucket, in_window, rel_bias)


def bias_band_bwd(bucket, dbias):
    def body(bk_ref, db_ref, out_ref):
        bk = bk_ref[...]
        for h in range(N_Q_HEADS):
            db = db_ref[h]
            for b in range(N_BUCKETS):
                out_ref[b, h] = jnp.sum(jnp.where(bk == float(b), db, 0.0))

    vm = pl.BlockSpec(memory_space=pltpu.VMEM)
    return pl.pallas_call(
        body, name="bias_band_bwd",
        out_shape=jax.ShapeDtypeStruct((N_BUCKETS, N_Q_HEADS), F32),
        in_specs=[vm, vm], out_specs=pl.BlockSpec(memory_space=pltpu.SMEM),
    )(bucket, dbias)


def _pool_matrices(i, w, transposed):
    a = lax.broadcasted_iota(jnp.int32, (BLOCK, BLOCK), 0)
    b = lax.broadcasted_iota(jnp.int32, (BLOCK, BLOCK), 1)
    out_row, in_row = (b, a) if transposed else (a, b)
    d = out_row - in_row
    count = jnp.minimum(i * BLOCK + out_row + 1, w).astype(F32)
    cur = jnp.where((d >= 0) & (d < w), 1.0 / count, 0.0) - jnp.where(d == 0, 1.0, 0.0)
    prev = jnp.where((d + BLOCK < w) & (i > 0), 1.0 / w, 0.0)
    return cur, prev


def _pooled(i, g, w, u_cur, u_prev):
    cur, prev = _pool_matrices(i, w, False)
    cols = slice(g * POOL_GROUP_DIM, (g + 1) * POOL_GROUP_DIM)
    return _dot_f32(cur, u_cur[:, cols]) + _dot_f32(prev, u_prev[:, cols])


def _stack_heads(v, h, offset=0):
    return jnp.concatenate(
        [v[:, offset + (GQA_GROUP * h + g) * HEAD_DIM: offset + (GQA_GROUP * h + g + 1) * HEAD_DIM] for g in range(GQA_GROUP)],
        axis=0)


def _unstack_heads(parts):
    return jnp.concatenate(
        [parts[h][g * BLOCK:(g + 1) * BLOCK] for h in range(N_KV_HEADS) for g in range(GQA_GROUP)], axis=1)


def _attn_probs(i, h, qh, kh, biasm_ref, sinks_ref):
    s = _dot_nt(qh, kh) * ATTN_SCALE + biasm_ref[h]
    col = lax.broadcasted_iota(jnp.int32, s.shape, 1)
    s = jnp.where((i == 0) & (col < BLOCK), NEG_INF, s)
    sink = jnp.concatenate([jnp.full((BLOCK, 1), sinks_ref[0, GQA_GROUP * h + g], F32) for g in range(GQA_GROUP)], axis=0)
    m = jnp.maximum(jnp.max(s, axis=-1, keepdims=True), sink)
    p = jnp.exp(s - m)
    e_sink = jnp.exp(sink - m)
    inv = 1.0 / (jnp.sum(p, axis=-1, keepdims=True) + e_sink)
    return p * inv, e_sink * inv


def mixers_fwd(proj, biasm, sinks, w_pool, pool_scale):
    t = proj.shape[0]
    nb = t // BLOCK

    def body(uc_ref, up_ref, q_ref, kvc_ref, kvp_ref, biasm_ref, sinks_ref, wp_ref, sc_ref, out_ref):
        i = pl.program_id(0)
        u_cur, u_prev = uc_ref[...], up_ref[...]
        for g, w in enumerate(POOL_WINDOWS):
            pooled = _pooled(i, g, w, u_cur, u_prev)
            mixed = _dot(pooled.astype(BF16), wp_ref[g])
            cols = slice(g * POOL_GROUP_DIM, (g + 1) * POOL_GROUP_DIM)
            out_ref[:, cols] = (mixed * sc_ref[:, cols]).astype(BF16)
        q = q_ref[...]
        kv = jnp.concatenate([kvp_ref[...], kvc_ref[...]], axis=0)
        outs = []
        for h in range(N_KV_HEADS):
            kh = kv[:, h * HEAD_DIM:(h + 1) * HEAD_DIM].astype(BF16)
            vh = kv[:, (N_KV_HEADS + h) * HEAD_DIM:(N_KV_HEADS + h + 1) * HEAD_DIM].astype(BF16)
            qh = _stack_heads(q, h).astype(BF16)
            probs, _ = _attn_probs(i, h, qh, kh, biasm_ref, sinks_ref)
            outs.append(_dot(probs.astype(BF16), vh))
        out_ref[:, POOL_WIDTH:] = _unstack_heads(outs).astype(BF16)

    prev = lambda i: (jnp.maximum(i - 1, 0), 0)
    return pl.pallas_call(
        body, name="mixers_fwd", grid=(nb,),
        out_shape=jax.ShapeDtypeStruct((t, 2 * POOL_WIDTH), BF16),
        in_specs=[pl.BlockSpec((BLOCK, 512), lambda i: (i, 0)),
                  pl.BlockSpec((BLOCK, 512), prev),
                  pl.BlockSpec((BLOCK, 512), lambda i: (i, 1)),
                  pl.BlockSpec((BLOCK, 256), lambda i: (i, 4)),
                  pl.BlockSpec((BLOCK, 256), lambda i: (jnp.maximum(i - 1, 0), 4)),
                  pl.BlockSpec((N_KV_HEADS, GQA_GROUP * BLOCK, 2 * BLOCK), lambda i: (0, 0, 0)),
                  pl.BlockSpec(memory_space=pltpu.SMEM),
                  pl.BlockSpec((4, POOL_GROUP_DIM, POOL_GROUP_DIM), lambda i: (0, 0, 0)),
                  pl.BlockSpec((1, POOL_WIDTH), lambda i: (0, 0))],
        out_specs=pl.BlockSpec((BLOCK, 2 * POOL_WIDTH), lambda i: (i, 0)),
        compiler_params=_params(),
    )(proj, proj, proj, proj, proj, biasm, sinks, w_pool, pool_scale)


def outproj_norm(cat, w, x, g):
    t, d = x.shape
    tm = TOKEN_TILE

    def body(c_ref, w_ref, x_ref, g_ref, mix_ref, x1_ref):
        mix = _dot(c_ref[...], w_ref[...])
        mix_ref[...] = mix
        x1_ref[...] = x_ref[...] + (mix * _rstd(mix)) * g_ref[...]

    row = pl.BlockSpec((tm, d), lambda i: (i, 0))
    return pl.pallas_call(
        body, name="outproj_norm", grid=(t // tm,),
        out_shape=[jax.ShapeDtypeStruct((t, d), F32)] * 2,
        in_specs=[pl.BlockSpec((tm, cat.shape[1]), lambda i: (i, 0)), pl.BlockSpec(w.shape, lambda i: (0, 0)), row,
                  pl.BlockSpec((1, d), lambda i: (0, 0))],
        out_specs=[row, row],
        compiler_params=_params(),
    )(cat, w, x, g)


def ffn_up(x1, g, w_gate, w_up):
    t, d = x1.shape
    f = w_gate.shape[1]
    tm, tn = TOKEN_TILE, FF_TILE

    def body(x_ref, g_ref, wg_ref, wu_ref, h_ref, gate_ref, up_ref, a_ref):
        @pl.when(pl.program_id(1) == 0)
        def _():
            xv = x_ref[...]
            h_ref[...] = ((xv * _rstd(xv)) * g_ref[...]).astype(BF16)

        h = h_ref[...]
        gate = _dot(h, wg_ref[...])
        up = _dot(h, wu_ref[...])
        gate_ref[...] = gate.astype(BF16)
        up_ref[...] = up.astype(BF16)
        a_ref[...] = (gate * (1.0 / (1.0 + jnp.exp(-gate))) * up).astype(BF16)

    wide = pl.BlockSpec((tm, tn), lambda i, j: (i, j))
    return pl.pallas_call(
        body, name="ffn_up", grid=(t // tm, f // tn),
        out_shape=[jax.ShapeDtypeStruct((t, d), BF16)] + [jax.ShapeDtypeStruct((t, f), BF16)] * 3,
        in_specs=[pl.BlockSpec((tm, d), lambda i, j: (i, 0)), pl.BlockSpec((1, d), lambda i, j: (0, 0)),
                  pl.BlockSpec((d, tn), lambda i, j: (0, j)), pl.BlockSpec((d, tn), lambda i, j: (0, j))],
        out_specs=[pl.BlockSpec((tm, d), lambda i, j: (i, 0)), wide, wide, wide],
        compiler_params=_params(),
    )(x1, g, w_gate, w_up)


def ffn_down_loss(a, w_down, x1, g, target):
    t, d = x1.shape
    tm = TOKEN_TILE

    def body(a_ref, w_ref, x_ref, g_ref, t_ref, df_ref, dy_ref, dg_ref, loss_ref):
        @pl.when(pl.program_id(0) == 0)
        def _():
            dg_ref[...] = jnp.zeros_like(dg_ref)
            loss_ref[...] = jnp.zeros_like(loss_ref)

        f = _dot(a_ref[...], w_ref[...])
        r = _rstd(f)
        g = g_ref[...]
        err = x_ref[...] + (f * r) * g - t_ref[...]
        loss_ref[...] += 0.5 * jnp.sum(jnp.mean(err * err, axis=-1, keepdims=True))
        dy = err * (1.0 / d)
        dy_ref[...] = dy
        df, dg_rows = _norm_bwd(dy, f, r, g)
        df_ref[...] = df.astype(BF16)
        dg_ref[...] += jnp.sum(dg_rows, axis=0, keepdims=True)

    row = pl.BlockSpec((tm, d), lambda i: (i, 0))
    gain = pl.BlockSpec((1, d), lambda i: (0, 0))
    return pl.pallas_call(
        body, name="ffn_down_loss", grid=(t // tm,),
        out_shape=[jax.ShapeDtypeStruct((t, d), BF16), jax.ShapeDtypeStruct((t, d), F32),
                   jax.ShapeDtypeStruct((1, d), F32), jax.ShapeDtypeStruct((1, 128), F32)],
        in_specs=[pl.BlockSpec((tm, a.shape[1]), lambda i: (i, 0)), pl.BlockSpec(w_down.shape, lambda i: (0, 0)), row, gain, row],
        out_specs=[row, row, gain, pl.BlockSpec((1, 128), lambda i: (0, 0))],
        compiler_params=_params(),
    )(a, w_down, x1, g, target)


def ffn_down_bwd(df, w_down, gate, up):
    t, d = df.shape
    f = gate.shape[1]
    tm, tn = TOKEN_TILE, FF_TILE

    def body(df_ref, w_ref, gate_ref, up_ref, dgate_ref, dup_ref):
        da = _dot_nt(df_ref[...], w_ref[...])
        gate = gate_ref[...].astype(F32)
        sig = 1.0 / (1.0 + jnp.exp(-gate))
        dgate_ref[...] = (da * up_ref[...].astype(F32) * (sig * (1.0 + gate * (1.0 - sig)))).astype(BF16)
        dup_ref[...] = (da * (gate * sig)).astype(BF16)

    wide = pl.BlockSpec((tm, tn), lambda i, j: (i, j))
    return pl.pallas_call(
        body, name="ffn_down_bwd", grid=(t // tm, f // tn),
        out_shape=[jax.ShapeDtypeStruct((t, f), BF16)] * 2,
        in_specs=[pl.BlockSpec((tm, d), lambda i, j: (i, 0)), pl.BlockSpec((tn, d), lambda i, j: (j, 0)), wide, wide],
        out_specs=[wide, wide],
        compiler_params=_params(),
    )(df, w_down, gate, up)


def grad_weight(a, b, name, tm, tn):
    t, m = a.shape
    n = b.shape[1]
    tt = TOKEN_TILE

    def body(a_ref, b_ref, out_ref):
        @pl.when(pl.program_id(2) == 0)
        def _():
            out_ref[...] = jnp.zeros_like(out_ref)

        out_ref[...] += _dot_tn(a_ref[...], b_ref[...])

    return pl.pallas_call(
        body, name=name, grid=(m // tm, n // tn, t // tt),
        out_shape=jax.ShapeDtypeStruct((m, n), F32),
        in_specs=[pl.BlockSpec((tt, tm), lambda i, j, k: (k, i)), pl.BlockSpec((tt, tn), lambda i, j, k: (k, j))],
        out_specs=pl.BlockSpec((tm, tn), lambda i, j, k: (i, j)),
        compiler_params=_params(),
    )(a, b)


def ffn_up_bwd(dgate, dup, w_gate, w_up, x1, g_ffn, dy, mix, g_mix):
    t, d = x1.shape
    f = dgate.shape[1]
    tm = TOKEN_TILE // 2

    def body(dg_ref, du_ref, wg_ref, wu_ref, x_ref, gf_ref, dy_ref, mix_ref, gm_ref, dx1_ref, dmix_ref, dgf_ref, dgm_ref):
        @pl.when(pl.program_id(0) == 0)
        def _():
            dgf_ref[...] = jnp.zeros_like(dgf_ref)
            dgm_ref[...] = jnp.zeros_like(dgm_ref)

        dh = _dot_nt(dg_ref[...], wg_ref[...]) + _dot_nt(du_ref[...], wu_ref[...])
        x1 = x_ref[...]
        dx, dgf_rows = _norm_bwd(dh, x1, _rstd(x1), gf_ref[...])
        dx1 = dy_ref[...] + dx
        dx1_ref[...] = dx1
        dgf_ref[...] += jnp.sum(dgf_rows, axis=0, keepdims=True)
        mix = mix_ref[...]
        dmix, dgm_rows = _norm_bwd(dx1, mix, _rstd(mix), gm_ref[...])
        dmix_ref[...] = dmix.astype(BF16)
        dgm_ref[...] += jnp.sum(dgm_rows, axis=0, keepdims=True)

    row = pl.BlockSpec((tm, d), lambda i: (i, 0))
    wide = pl.BlockSpec((tm, f), lambda i: (i, 0))
    gain = pl.BlockSpec((1, d), lambda i: (0, 0))
    whole = pl.BlockSpec((d, f), lambda i: (0, 0))
    return pl.pallas_call(
        body, name="ffn_up_bwd", grid=(t // tm,),
        out_shape=[jax.ShapeDtypeStruct((t, d), F32), jax.ShapeDtypeStruct((t, d), BF16),
                   jax.ShapeDtypeStruct((1, d), F32), jax.ShapeDtypeStruct((1, d), F32)],
        in_specs=[wide, wide, whole, whole, row, gain, row, row, gain],
        out_specs=[row, row, gain, gain],
        compiler_params=_params(),
    )(dgate, dup, w_gate, w_up, x1, g_ffn, dy, mix, g_mix)


def outproj_bwd(dmix, w_out):
    t, d = dmix.shape
    tm = TOKEN_TILE

    def body(dm_ref, w_ref, out_ref):
        out_ref[...] = _dot_nt(dm_ref[...], w_ref[...])

    return pl.pallas_call(
        body, name="outproj_bwd", grid=(t // tm,),
        out_shape=jax.ShapeDtypeStruct((t, w_out.shape[0]), F32),
        in_specs=[pl.BlockSpec((tm, d), lambda i: (i, 0)), pl.BlockSpec(w_out.shape, lambda i: (0, 0))],
        out_specs=pl.BlockSpec((tm, w_out.shape[0]), lambda i: (i, 0)),
        compiler_params=_params(),
    )(dmix, w_out)


def mixers_bwd(proj, dcat, biasm, sinks, w_pool, pool_scale):
    t = proj.shape[0]
    nb = t // BLOCK

    def body(uc_ref, up_ref, q_ref, kvc_ref, kvp_ref, dcat_ref, biasm_ref, sinks_ref, wp_ref, sc_ref,
             dproj_ref, dbias_ref, dsink_ref, dwp_ref, dsc_ref, c_u, c_q, c_kv):
        i = pl.program_id(0)

        @pl.when(i == 0)
        def _():
            dbias_ref[...] = jnp.zeros_like(dbias_ref)
            dwp_ref[...] = jnp.zeros_like(dwp_ref)
            dsc_ref[...] = jnp.zeros_like(dsc_ref)
            for hq in range(N_Q_HEADS):
                dsink_ref[0, hq] = 0.0
            c_u[...] = jnp.zeros_like(c_u)
            c_q[...] = jnp.zeros_like(c_q)
            c_kv[...] = jnp.zeros_like(c_kv)

        @pl.when(i < nb)
        def _():
            u_cur, u_prev = uc_ref[...], up_ref[...]
            dcat = dcat_ref[...]
            du_cur, du_prev = [], []
            for g, w in enumerate(POOL_WINDOWS):
                cols = slice(g * POOL_GROUP_DIM, (g + 1) * POOL_GROUP_DIM)
                pooled = _pooled(i, g, w, u_cur, u_prev).astype(BF16)
                mixed = _dot(pooled, wp_ref[g])
                dout = dcat[:, cols]
                dsc_ref[:, cols] += jnp.sum(dout * mixed, axis=0, keepdims=True)
                dmixed = (dout * sc_ref[:, cols]).astype(BF16)
                dwp_ref[g] += _dot_tn(pooled, dmixed)
                dpooled = _dot_nt(dmixed, wp_ref[g])
                cur_t, prev_t = _pool_matrices(i, w, True)
                du_cur.append(_dot_f32(cur_t, dpooled))
                du_prev.append(_dot_f32(prev_t, dpooled))
            q = q_ref[...]
            kv = jnp.concatenate([kvp_ref[...], kvc_ref[...]], axis=0)
            dqs, dks, dvs = [], [], []
            for h in range(N_KV_HEADS):
                kh = kv[:, h * HEAD_DIM:(h + 1) * HEAD_DIM].astype(BF16)
                vh = kv[:, (N_KV_HEADS + h) * HEAD_DIM:(N_KV_HEADS + h + 1) * HEAD_DIM].astype(BF16)
                qh = _stack_heads(q, h).astype(BF16)
                do = _stack_heads(dcat, h, POOL_WIDTH).astype(BF16)
                probs, p_sink = _attn_probs(i, h, qh, kh, biasm_ref, sinks_ref)
                dp = _dot_nt(do, vh)
                delta = jnp.sum(probs * dp, axis=-1, keepdims=True)
                ds = probs * (dp - delta)
                dbias_ref[h] += ds
                dsink_rows = p_sink * delta
                for g in range(GQA_GROUP):
                    dsink_ref[0, GQA_GROUP * h + g] -= jnp.sum(dsink_rows[g * BLOCK:(g + 1) * BLOCK])
                dss = (ds * ATTN_SCALE).astype(BF16)
                dqs.append(_dot(dss, kh))
                dks.append(_dot_tn(dss, qh))
                dvs.append(_dot_tn(probs.astype(BF16), do))
            dkv = jnp.concatenate(dks + dvs, axis=1)
            dproj_ref[:, 0:POOL_WIDTH] = (c_u[...] + jnp.concatenate(du_prev, axis=1)).astype(BF16)
            dproj_ref[:, POOL_WIDTH:2 * POOL_WIDTH] = c_q[...].astype(BF16)
            dproj_ref[:, 2 * POOL_WIDTH:] = (c_kv[...] + dkv[0:BLOCK]).astype(BF16)
            c_u[...] = jnp.concatenate(du_cur, axis=1)
            c_q[...] = _unstack_heads(dqs)
            c_kv[...] = dkv[BLOCK:]

        @pl.when(i == nb)
        def _():
            dproj_ref[:, 0:POOL_WIDTH] = c_u[...].astype(BF16)
            dproj_ref[:, POOL_WIDTH:2 * POOL_WIDTH] = c_q[...].astype(BF16)
            dproj_ref[:, 2 * POOL_WIDTH:] = c_kv[...].astype(BF16)

    cur = lambda i: jnp.minimum(i, nb - 1)
    prv = lambda i: jnp.maximum(jnp.minimum(i, nb - 1) - 1, 0)
    return pl.pallas_call(
        body, name="mixers_bwd", grid=(nb + 1,),
        out_shape=[jax.ShapeDtypeStruct((t, proj.shape[1]), BF16),
                   jax.ShapeDtypeStruct((N_KV_HEADS, GQA_GROUP * BLOCK, 2 * BLOCK), F32),
                   jax.ShapeDtypeStruct((1, N_Q_HEADS), F32),
                   jax.ShapeDtypeStruct((4, POOL_GROUP_DIM, POOL_GROUP_DIM), F32),
                   jax.ShapeDtypeStruct((1, POOL_WIDTH), F32)],
        in_specs=[pl.BlockSpec((BLOCK, 512), lambda i: (cur(i), 0)),
                  pl.BlockSpec((BLOCK, 512), lambda i: (prv(i), 0)),
                  pl.BlockSpec((BLOCK, 512), lambda i: (cur(i), 1)),
                  pl.BlockSpec((BLOCK, 256), lambda i: (cur(i), 4)),
                  pl.BlockSpec((BLOCK, 256), lambda i: (prv(i), 4)),
                  pl.BlockSpec((BLOCK, 2 * POOL_WIDTH), lambda i: (cur(i), 0)),
                  pl.BlockSpec((N_KV_HEADS, GQA_GROUP * BLOCK, 2 * BLOCK), lambda i: (0, 0, 0)),
                  pl.BlockSpec(memory_space=pltpu.SMEM),
                  pl.BlockSpec((4, POOL_GROUP_DIM, POOL_GROUP_DIM), lambda i: (0, 0, 0)),
                  pl.BlockSpec((1, POOL_WIDTH), lambda i: (0, 0))],
        out_specs=[pl.BlockSpec((BLOCK, proj.shape[1]), lambda i: (jnp.maximum(i - 1, 0), 0)),
                   pl.BlockSpec((N_KV_HEADS, GQA_GROUP * BLOCK, 2 * BLOCK), lambda i: (0, 0, 0)),
                   pl.BlockSpec(memory_space=pltpu.SMEM),
                   pl.BlockSpec((4, POOL_GROUP_DIM, POOL_GROUP_DIM), lambda i: (0, 0, 0)),
                   pl.BlockSpec((1, POOL_WIDTH), lambda i: (0, 0))],
        scratch_shapes=[pltpu.VMEM((BLOCK, POOL_WIDTH), F32), pltpu.VMEM((BLOCK, POOL_WIDTH), F32),
                        pltpu.VMEM((BLOCK, 256), F32)],
        compiler_params=_params(),
    )(proj, proj, proj, proj, proj, dcat, biasm, sinks, w_pool, pool_scale)


def inproj_bwd(dproj, w_in, x, g, dx1):
    t, d = x.shape
    n = dproj.shape[1]
    tm = TOKEN_TILE

    def body(dp_ref, w_ref, x_ref, g_ref, dx1_ref, dx_ref, dg_ref):
        @pl.when(pl.program_id(0) == 0)
        def _():
            dg_ref[...] = jnp.zeros_like(dg_ref)

        dh = _dot_nt(dp_ref[...], w_ref[...])
        xv = x_ref[...]
        dx, dg_rows = _norm_bwd(dh, xv, _rstd(xv), g_ref[...])
        dx_ref[...] = dx1_ref[...] + dx
        dg_ref[...] += jnp.sum(dg_rows, axis=0, keepdims=True)

    row = pl.BlockSpec((tm, d), lambda i: (i, 0))
    gain = pl.BlockSpec((1, d), lambda i: (0, 0))
    return pl.pallas_call(
        body, name="inproj_bwd", grid=(t // tm,),
        out_shape=[jax.ShapeDtypeStruct((t, d), F32), jax.ShapeDtypeStruct((1, d), F32)],
        in_specs=[pl.BlockSpec((tm, n), lambda i: (i, 0)), pl.BlockSpec(w_in.shape, lambda i: (0, 0)), row, gain, row],
        out_specs=[row, gain],
        compiler_params=_params(),
    )(dproj, w_in, x, g, dx1)


def _bucket_band():
    qi = jnp.arange(BLOCK)[:, None]
    kj = jnp.arange(2 * BLOCK)[None, :]
    dist = qi + BLOCK - kj
    n = jnp.maximum(dist, 0)
    nf = jnp.maximum(n, 1).astype(F32)
    large = MAX_EXACT + (jnp.log(nf / MAX_EXACT) / np.float32(np.log(MAX_DISTANCE / MAX_EXACT))
                         * (N_BUCKETS - MAX_EXACT)).astype(jnp.int32)
    large = jnp.minimum(large, N_BUCKETS - 1)
    bucket = jnp.where(n < MAX_EXACT, n, large)
    in_window = (dist >= 0) & (dist < BLOCK)
    return bucket.astype(F32), in_window.astype(F32)


def _pack_rows(parts, width):
    return jnp.concatenate([p.reshape(-1, width) for p in parts], axis=0)


def _pad_row(v):
    flat = v.reshape(-1)
    pad = (-flat.shape[0]) % 128
    return jnp.pad(flat, (0, pad)).reshape(-1, 128)


def _split_cols(full):
    k, n = full.shape
    return full.reshape(k, N_DEV, n // N_DEV).transpose(1, 0, 2)


def _join_cols(shards):
    s, k, n = shards.shape
    return shards.transpose(1, 0, 2).reshape(k, s * n)


def kernel(x, g_pre_mix, w_in, w_pool, pool_scale, rel_bias, sinks, w_out, g_post_mix, g_pre_ffn, w_gate, w_up, w_down, g_post_ffn, loss_target, m_g_pre_mix, m_w_in, m_w_pool, m_pool_scale, m_rel_bias, m_sinks, m_w_out, m_g_post_mix, m_g_pre_ffn, m_w_gate, m_w_up, m_w_down, m_g_post_ffn, v_g_pre_mix, v_w_in, v_w_pool, v_pool_scale, v_rel_bias, v_sinks, v_w_out, v_g_post_mix, v_g_pre_ffn, v_w_gate, v_w_up, v_w_down, v_g_post_ffn):
    d = x.shape[-1]
    xs, target = x[0], loss_target[0]
    big = [w_in[0], w_out[0], w_gate[0], w_up[0], w_down[0]]
    rows = [b.size // d for b in big]
    starts = np.concatenate([[0], np.cumsum(rows)])

    gathered = all_gather_rows(_pack_rows([b.astype(BF16) for b in big], d), "gather_weights")
    seg = [gathered[:, starts[k]:starts[k + 1]] for k in range(5)]
    w_in_f = _join_cols(seg[0].reshape(N_DEV, *big[0].shape))
    w_out_f = seg[1].reshape(-1, d)
    w_gate_f = _join_cols(seg[2].reshape(N_DEV, *big[2].shape))
    w_up_f = _join_cols(seg[3].reshape(N_DEV, *big[3].shape))
    w_down_f = seg[4].reshape(-1, d)

    bucket, in_window = _bucket_band()
    biasm = bias_band(bucket, in_window, rel_bias).reshape(N_KV_HEADS, GQA_GROUP * BLOCK, 2 * BLOCK)
    w_pool_b = w_pool[0].astype(BF16)
    proj, h1 = norm_inproj(xs, g_pre_mix, w_in_f)
    cat = mixers_fwd(proj, biasm, sinks, w_pool_b, pool_scale)
    mix, x1 = outproj_norm(cat, w_out_f, xs, g_post_mix)
    h2, gate, up, act = ffn_up(x1, g_pre_ffn, w_gate_f, w_up_f)
    df, dy, dg_post_ffn, loss_part = ffn_down_loss(act, w_down_f, x1, g_post_ffn, target)

    dgate, dup = ffn_down_bwd(df, w_down_f, gate, up)
    dw_down = grad_weight(act, df, "grad_w_down", FF_TILE, d)
    dw_gate = grad_weight(h2, dgate, "grad_w_gate", d, FF_TILE)
    dw_up = grad_weight(h2, dup, "grad_w_up", d, FF_TILE)
    dx1, dmix, dg_pre_ffn, dg_post_mix = ffn_up_bwd(dgate, dup, w_gate_f, w_up_f, x1, g_pre_ffn, dy, mix, g_post_mix)
    dcat = outproj_bwd(dmix, w_out_f)
    dw_out = grad_weight(cat, dmix, "grad_w_out", d, d)
    dproj, dbias, dsinks, dw_pool, dpool_scale = mixers_bwd(proj, dcat, biasm, sinks, w_pool_b, pool_scale)
    drel_bias = bias_band_bwd(bucket, dbias.reshape(N_Q_HEADS, BLOCK, 2 * BLOCK))
    grad_x, dg_pre_mix = inproj_bwd(dproj, w_in_f, xs, g_pre_mix, dx1)
    dw_in = grad_weight(h1, dproj, "grad_w_in", d, dproj.shape[1])

    parts = jnp.concatenate([
        _split_cols(dw_in).astype(BF16).reshape(N_DEV, -1, d),
        dw_out.astype(BF16).reshape(N_DEV, -1, d),
        _split_cols(dw_gate).astype(BF16).reshape(N_DEV, -1, d),
        _split_cols(dw_up).astype(BF16).reshape(N_DEV, -1, d),
        dw_down.astype(BF16).reshape(N_DEV, -1, d)], axis=1)
    received = exchange_partials(parts, "exchange_grads")
    big_m = [m_w_in[0], m_w_out[0], m_w_gate[0], m_w_up[0], m_w_down[0]]
    big_v = [v_w_in[0], v_w_out[0], v_w_gate[0], v_w_up[0], v_w_down[0]]
    big_out = sum_adamw(received, _pack_rows(big, d), _pack_rows(big_m, d), _pack_rows(big_v, d))
    big_un = [[o[starts[k]:starts[k + 1]].reshape(big[k].shape)[None] for k in range(5)] for o in big_out]

    small_w = [g_pre_mix, pool_scale, rel_bias, sinks, g_post_mix, g_pre_ffn, g_post_ffn, w_pool]
    small_m = [m_g_pre_mix, m_pool_scale, m_rel_bias, m_sinks, m_g_post_mix, m_g_pre_ffn, m_g_post_ffn, m_w_pool]
    small_v = [v_g_pre_mix, v_pool_scale, v_rel_bias, v_sinks, v_g_post_mix, v_g_pre_ffn, v_g_post_ffn, v_w_pool]
    small_g = [dg_pre_mix, dpool_scale, drel_bias, dsinks, dg_post_mix, dg_pre_ffn, dg_post_ffn, dw_pool]
    zero = jnp.zeros((1,), F32)
    pack = lambda parts, last: jnp.concatenate([_pad_row(p) for p in parts] + [_pad_row(last)], axis=0)
    small_out = small_allreduce_adamw(pack(small_g, loss_part[0, :1]), pack(small_w, zero), pack(small_m, zero),
                                      pack(small_v, zero))
    small_rows = [_pad_row(p).shape[0] for p in small_w]
    s_starts = np.concatenate([[0], np.cumsum(small_rows)])

    def small_un(o):
        return [o[s_starts[k]:s_starts[k + 1]].reshape(-1)[:small_w[k].size].reshape(small_w[k].shape) for k in range(8)]

    loss = small_out[0][s_starts[8], 0]
    sm = [small_un(o) for o in small_out]

    def ordered(kind):
        s, b = sm[kind], big_un[kind]
        return [s[0], b[0], s[7], s[1], s[2], s[3], b[1], s[4], s[5], b[2], b[3], b[4], s[6]]

    return (loss, grad_x[None], *ordered(0), *ordered(1), *ordered(2), *ordered(3))
```

```python
import numpy as np
import jax
import jax.numpy as jnp
from jax import lax
from jax.experimental import pallas as pl
from jax.experimental.pallas import tpu as pltpu

F32 = jnp.float32
BF16 = jnp.bfloat16

N_DEV = 8
N_CHIP = 4
POOL_WIDTH = 512
POOL_WINDOWS = (2, 4, 8, 16)
POOL_GROUP_DIM = 128
HEAD_DIM = 64
N_Q_HEADS = 8
N_KV_HEADS = 2
GQA_GROUP = 4
BLOCK = 128
HALO = 16
N_BUCKETS = 32
MAX_EXACT = 16
MAX_DISTANCE = 128
EPS = 1e-6
NEG_INF = -1e30
ATTN_SCALE = float(1.0 / np.sqrt(np.float32(HEAD_DIM)))

ADAM_LR = 0.001
ADAM_B1 = 0.9
ADAM_B2 = 0.999
ADAM_EPS = 1e-08
ADAM_WD = 0.01
ADAM_STEP = 10

TOKEN_TILE = 512
FF_SHARDS_PER_TILE = 4
VMEM_LIMIT = 56 * 1024 * 1024
MESH = pl.DeviceIdType.MESH
ANY = pl.BlockSpec(memory_space=pl.ANY)
VMEM = pl.BlockSpec(memory_space=pltpu.VMEM)
SMEM = pl.BlockSpec(memory_space=pltpu.SMEM)


def _params(**kw):
    return pltpu.CompilerParams(vmem_limit_bytes=VMEM_LIMIT, **kw)


def _dot(a, b):
    return jnp.dot(a, b, preferred_element_type=F32)


def _dot_nt(a, b):
    return lax.dot_general(a, b, (((1,), (1,)), ((), ())), preferred_element_type=F32)


def _dot_tn(a, b):
    return lax.dot_general(a, b, (((0,), (0,)), ((), ())), preferred_element_type=F32)


def _rstd(v):
    return lax.rsqrt(jnp.mean(v * v, axis=-1, keepdims=True) + EPS)


def _norm_bwd(dout, v, r, g):
    vn = v * r
    dn = dout * g
    dv = r * (dn - vn * jnp.mean(dn * vn, axis=-1, keepdims=True))
    return dv, dout * vn


def _merge_rows(value):
    s, r, c_ = value.shape
    return value.reshape(s * r, c_)


def _gather_plan(srcs, outs, send_sems, recv_sems, local_sems):
    n = len(srcs)
    x, y, c = lax.axis_index("x"), lax.axis_index("y"), lax.axis_index("c")
    me, sibling = (x, y, c), (x, y, 1 - c)
    chips = [(1 - x, y), (x, 1 - y), (1 - x, 1 - y)]

    def slot(a, px, py, pc):
        return outs[a].at[4 * px + 2 * py + pc]

    def copy(a, k, block, to, from_src=False):
        return pltpu.make_async_remote_copy(
            src_ref=srcs[a] if from_src else slot(a, *block), dst_ref=slot(a, *block),
            send_sem=send_sems.at[k * n + a], recv_sem=recv_sems.at[k * n + a], device_id=to, device_id_type=MESH)

    def local(a):
        return pltpu.make_async_copy(srcs[a], slot(a, *me), local_sems.at[a])

    def first(a):
        return [copy(a, 0, me, sibling, True)] + [copy(a, 1 + j, me, (*chip, c), True) for j, chip in enumerate(chips)]

    def passed(a, j):
        return copy(a, 4 + j, (*chips[j], c), sibling)

    def start():
        for a in range(n):
            local(a).start()
            for cp in first(a):
                cp.start()

    def finish():
        for j, chip in enumerate(chips):
            for a in range(n):
                copy(a, 1 + j, (*chip, c), me).wait_recv()
                passed(a, j).start()
        for a in range(n):
            copy(a, 0, sibling, me).wait_recv()
            for j, chip in enumerate(chips):
                copy(a, 4 + j, (*chip, 1 - c), me).wait_recv()
        for a in range(n):
            for cp in first(a) + [passed(a, j) for j in range(3)]:
                cp.wait_send()
            local(a).wait()

    return start, finish


def _gather_scratch(n):
    return [pltpu.SemaphoreType.DMA((7 * n,)), pltpu.SemaphoreType.DMA((7 * n,)), pltpu.SemaphoreType.DMA((n,))]


def _chip_exchange_plan(srcs, outs, send_sems, recv_sems, local_sems):
    n = len(srcs)
    x, y, c = lax.axis_index("x"), lax.axis_index("y"), lax.axis_index("c")
    my_chip = 2 * x + y

    def copies():
        out = []
        for a in range(n):
            for k in range(1, N_CHIP):
                px, py = x ^ (k >> 1), y ^ (k & 1)
                out.append(pltpu.make_async_remote_copy(
                    src_ref=srcs[a].at[2 * px + py], dst_ref=outs[a].at[my_chip],
                    send_sem=send_sems.at[(k - 1) * n + a], recv_sem=recv_sems.at[(k - 1) * n + a],
                    device_id=(px, py, c), device_id_type=MESH))
        return out

    def local(a):
        return pltpu.make_async_copy(srcs[a].at[my_chip], outs[a].at[my_chip], local_sems.at[a])

    def start():
        for a in range(n):
            local(a).start()
        for cp in copies():
            cp.start()

    def finish():
        for cp in copies():
            cp.wait()
        for a in range(n):
            local(a).wait()

    return start, finish


def _chip_exchange_scratch(n):
    return [pltpu.SemaphoreType.DMA((3 * n,)), pltpu.SemaphoreType.DMA((3 * n,)), pltpu.SemaphoreType.DMA((n,))]


def gather_blocks(shards, name):
    def body(*refs):
        n = len(shards)
        start, finish = _gather_plan(refs[:n], refs[n:2 * n], *refs[2 * n:])
        start()
        finish()

    return pl.pallas_call(
        body, name=name,
        out_shape=[jax.ShapeDtypeStruct((N_DEV, *s.shape), s.dtype) for s in shards],
        in_specs=[ANY] * len(shards), out_specs=[ANY] * len(shards),
        scratch_shapes=_gather_scratch(len(shards)),
    )(*shards)


def exchange_partials(parts, name):
    n = len(parts)

    def body(*refs):
        p_refs, out_refs, send_sems, recv_sems, local_sems = refs[:n], refs[n:2 * n], *refs[2 * n:]
        x, y, c = lax.axis_index("x"), lax.axis_index("y"), lax.axis_index("c")
        my_id = 4 * x + 2 * y + c
        waits = []
        for a in range(n):
            mine = pltpu.make_async_copy(p_refs[a].at[my_id], out_refs[a].at[my_id], local_sems.at[a])
            mine.start()
            waits.append(mine)
            for k in range(1, N_DEV):
                px, py, pc = x ^ (k >> 2), y ^ ((k >> 1) & 1), c ^ (k & 1)
                cp = pltpu.make_async_remote_copy(
                    src_ref=p_refs[a].at[4 * px + 2 * py + pc], dst_ref=out_refs[a].at[my_id],
                    send_sem=send_sems.at[(k - 1) * n + a], recv_sem=recv_sems.at[(k - 1) * n + a],
                    device_id=(px, py, pc), device_id_type=MESH)
                cp.start()
                waits.append(cp)
        for cp in waits:
            cp.wait()

    return pl.pallas_call(
        body, name=name,
        out_shape=[jax.ShapeDtypeStruct(p.shape, p.dtype) for p in parts],
        in_specs=[ANY] * n, out_specs=[ANY] * n,
        scratch_shapes=[pltpu.SemaphoreType.DMA((7 * n,)), pltpu.SemaphoreType.DMA((7 * n,)), pltpu.SemaphoreType.DMA((n,))],
    )(*parts)


def pair_exchange(parts, name):
    n = len(parts)

    def body(*refs):
        p_refs, mine_refs, got_refs, send_sems, recv_sems, local_sems = refs[:n], refs[n:2 * n], refs[2 * n:3 * n], *refs[3 * n:]
        x, y, c = lax.axis_index("x"), lax.axis_index("y"), lax.axis_index("c")
        waits = []
        for a in range(n):
            keep = pltpu.make_async_copy(p_refs[a].at[c], mine_refs[a], local_sems.at[a])
            keep.start()
            cp = pltpu.make_async_remote_copy(
                src_ref=p_refs[a].at[1 - c], dst_ref=got_refs[a], send_sem=send_sems.at[a], recv_sem=recv_sems.at[a],
                device_id=(x, y, 1 - c), device_id_type=MESH)
            cp.start()
            waits += [keep, cp]
        for cp in waits:
            cp.wait()

    half = [jax.ShapeDtypeStruct(p.shape[1:], p.dtype) for p in parts]
    return pl.pallas_call(
        body, name=name, out_shape=half + half,
        in_specs=[ANY] * n, out_specs=[ANY] * (2 * n),
        scratch_shapes=[pltpu.SemaphoreType.DMA((n,)), pltpu.SemaphoreType.DMA((n,)), pltpu.SemaphoreType.DMA((n,))],
    )(*parts)


def pair_add(mine, got):
    n = len(mine)

    def body(*refs):
        for a in range(n):
            refs[2 * n + a][...] = (refs[a][...].astype(F32) + refs[n + a][...].astype(F32)).astype(BF16)

    def spec(p):
        zeros = (0,) * (p.ndim - 1)
        return pl.BlockSpec((1, *p.shape[1:]), lambda i: (i, *zeros))

    specs = [spec(p) for p in mine]
    return pl.pallas_call(
        body, name="pair_add", grid=(mine[0].shape[0],),
        out_shape=[jax.ShapeDtypeStruct(p.shape, BF16) for p in mine],
        in_specs=specs + specs, out_specs=specs,
        compiler_params=_params(),
    )(*mine, *got)


def sum_slots(slots, name):
    n = len(slots)

    def body(*refs):
        for a in range(n):
            total = refs[a][0].astype(F32)
            for s in range(1, slots[a].shape[0]):
                total = total + refs[a][s].astype(F32)
            refs[n + a][...] = total

    return pl.pallas_call(
        body, name=name,
        out_shape=[jax.ShapeDtypeStruct(p.shape[1:], F32) for p in slots],
        in_specs=[VMEM] * n, out_specs=[VMEM] * n,
        compiler_params=_params(),
    )(*slots)


def _adamw(w, g, m, v):
    m2 = ADAM_B1 * m + (1.0 - ADAM_B1) * g
    v2 = ADAM_B2 * v + (1.0 - ADAM_B2) * (g * g)
    m_hat = m2 / (1.0 - ADAM_B1 ** ADAM_STEP)
    v_hat = v2 / (1.0 - ADAM_B2 ** ADAM_STEP)
    delta = -ADAM_LR * (m_hat / (jnp.sqrt(v_hat) + ADAM_EPS) + ADAM_WD * w)
    return delta, m2, v2


def adamw_update(ws, gs, ms, vs, name):
    n = len(ws)

    def body(*refs):
        for a in range(n):
            delta, m2, v2 = _adamw(refs[a][...], refs[n + a][...], refs[2 * n + a][...], refs[3 * n + a][...])
            refs[4 * n + 3 * a][...] = delta
            refs[4 * n + 3 * a + 1][...] = m2
            refs[4 * n + 3 * a + 2][...] = v2

    out = pl.pallas_call(
        body, name=name,
        out_shape=[jax.ShapeDtypeStruct(w.shape, F32) for w in ws for _ in range(3)],
        in_specs=[VMEM] * (4 * n), out_specs=[VMEM] * (3 * n),
        compiler_params=_params(),
    )(*ws, *gs, *ms, *vs)
    return [out[3 * a:3 * a + 3] for a in range(n)]


def small_allreduce_adamw(part, w, m, v):
    r, c_ = part.shape

    def body(p_ref, w_ref, m_ref, v_ref, g_ref, d_ref, m2_ref, v2_ref, gat_ref, send_sems, recv_sems, local_sems):
        start, finish = _gather_plan([p_ref], [gat_ref], send_sems, recv_sems, local_sems)
        start()
        finish()
        total = gat_ref[0]
        for s in range(1, N_DEV):
            total = total + gat_ref[s]
        g_ref[...] = total
        delta, m2, v2 = _adamw(w_ref[...], total, m_ref[...], v_ref[...])
        d_ref[...] = delta
        m2_ref[...] = m2
        v2_ref[...] = v2

    return pl.pallas_call(
        body, name="small_allreduce_adamw",
        out_shape=[jax.ShapeDtypeStruct((r, c_), F32)] * 4,
        in_specs=[VMEM] * 4, out_specs=[VMEM] * 4,
        scratch_shapes=[pltpu.VMEM((N_DEV, r, c_), F32)] + _gather_scratch(1),
    )(part, w, m, v)


def norm_inproj(x, g, w_t):
    t, d = x.shape
    n = w_t.shape[0]
    tm = TOKEN_TILE

    def body(x_ref, g_ref, w_ref, proj_ref, h_ref):
        xv = x_ref[...]
        h = ((xv * _rstd(xv)) * g_ref[...]).astype(BF16)
        h_ref[...] = h
        proj_ref[...] = _dot_nt(h, w_ref[...])

    return pl.pallas_call(
        body, name="norm_inproj", grid=(t // tm,),
        out_shape=[jax.ShapeDtypeStruct((t, n), F32), jax.ShapeDtypeStruct((t, d), BF16)],
        in_specs=[pl.BlockSpec((tm, d), lambda i: (i, 0)), pl.BlockSpec((1, d), lambda i: (0, 0)),
                  pl.BlockSpec((n, d), lambda i: (0, 0))],
        out_specs=[pl.BlockSpec((tm, n), lambda i: (i, 0)), pl.BlockSpec((tm, d), lambda i: (i, 0))],
        compiler_params=_params(),
    )(x, g, w_t)


def bias_band(bucket, in_window, rel_bias):
    def body(bk_ref, win_ref, rb_ref, out_ref):
        bk = bk_ref[...]
        keep = win_ref[...] > 0.5
        for h in range(N_Q_HEADS):
            acc = jnp.zeros(bk.shape, F32)
            for b in range(N_BUCKETS):
                acc = jnp.where(bk == float(b), rb_ref[b, h], acc)
            out_ref[h] = jnp.where(keep, acc, NEG_INF)

    return pl.pallas_call(
        body, name="bias_band",
        out_shape=jax.ShapeDtypeStruct((N_Q_HEADS, BLOCK, 2 * BLOCK), F32),
        in_specs=[VMEM, VMEM, SMEM], out_specs=VMEM,
    )(bucket, in_window, rel_bias)


def bias_band_bwd(bucket, dbias):
    def body(bk_ref, db_ref, out_ref):
        bk = bk_ref[...]
        for h in range(N_Q_HEADS):
            db = db_ref[h]
            for b in range(N_BUCKETS):
                out_ref[b, h] = jnp.sum(jnp.where(bk == float(b), db, 0.0))

    return pl.pallas_call(
        body, name="bias_band_bwd",
        out_shape=jax.ShapeDtypeStruct((N_BUCKETS, N_Q_HEADS), F32),
        in_specs=[VMEM, VMEM], out_specs=SMEM,
    )(bucket, dbias)


def _window_sum(buf_ref, g, w, first):
    cols = slice(g * POOL_GROUP_DIM, (g + 1) * POOL_GROUP_DIM)
    acc = None
    for k in range(w):
        piece = buf_ref[first(k):first(k) + BLOCK, cols]
        acc = piece if acc is None else acc + piece
    return acc


def _inv_count(i, w):
    row = lax.broadcasted_iota(jnp.int32, (BLOCK, 1), 0)
    return 1.0 / jnp.minimum(i * BLOCK + row + 1, w).astype(F32)


def _fill_pool_input(i, ubuf, uc_ref, halo_ref):
    ubuf[0:HALO, :] = jnp.where(i > 0, halo_ref[...], 0.0)
    ubuf[HALO:, :] = uc_ref[...]


def _pooled(i, g, w, ubuf):
    cols = slice(g * POOL_GROUP_DIM, (g + 1) * POOL_GROUP_DIM)
    return _window_sum(ubuf, g, w, lambda k: HALO - k) * _inv_count(i, w) - ubuf[HALO:, cols]


def _head_variants(pair):
    low = lax.broadcasted_iota(jnp.int32, pair.shape, 1) < HEAD_DIM
    swapped = pltpu.roll(pair, HEAD_DIM, 1)
    zero = jnp.zeros_like(pair)
    pick = lambda c, a, b: jnp.where(c, a, b).astype(BF16)
    return [[pick(low, pair, zero), pick(low, zero, swapped)], [pick(low, swapped, zero), pick(low, zero, pair)]]


def _head_probs(i, hq, q2, k_var, biasm_ref, sinks_ref):
    s = _dot_nt(q2, k_var) * ATTN_SCALE + biasm_ref[hq]
    col = lax.broadcasted_iota(jnp.int32, s.shape, 1)
    s = jnp.where((i == 0) & (col < BLOCK), NEG_INF, s)
    sink = sinks_ref[0, hq]
    m = jnp.maximum(jnp.max(s, axis=-1, keepdims=True), sink)
    p = jnp.exp(s - m)
    e_sink = jnp.exp(sink - m)
    inv = 1.0 / (jnp.sum(p, axis=-1, keepdims=True) + e_sink)
    return p * inv, e_sink * inv


def _mixer_in_specs(cur, prv):
    return [pl.BlockSpec((BLOCK, 512), lambda i: (cur(i), 0)),
            pl.BlockSpec((HALO, 512), lambda i: (jnp.maximum(cur(i) * (BLOCK // HALO) - 1, 0), 0)),
            pl.BlockSpec((BLOCK, 512), lambda i: (cur(i), 1)),
            pl.BlockSpec((BLOCK, 256), lambda i: (cur(i), 4)),
            pl.BlockSpec((BLOCK, 256), lambda i: (prv(i), 4))]


def _mixer_param_specs():
    return [pl.BlockSpec((N_Q_HEADS, BLOCK, 2 * BLOCK), lambda i: (0, 0, 0)), SMEM,
            pl.BlockSpec((4, POOL_GROUP_DIM, POOL_GROUP_DIM), lambda i: (0, 0, 0)),
            pl.BlockSpec((1, POOL_WIDTH), lambda i: (0, 0))]


def mixers_fwd(proj, biasm, sinks, w_pool, pool_scale, gate_up_shard):
    t = proj.shape[0]
    nb = t // BLOCK

    def body(uc_ref, halo_ref, q_ref, kvc_ref, kvp_ref, biasm_ref, sinks_ref, wp_ref, sc_ref, shard_ref,
             out_ref, gathered_ref, ubuf, send_sems, recv_sems, local_sems):
        i = pl.program_id(0)
        start, finish = _gather_plan([shard_ref], [gathered_ref], send_sems, recv_sems, local_sems)
        pl.when(i == 0)(start)

        _fill_pool_input(i, ubuf, uc_ref, halo_ref)
        for g, w in enumerate(POOL_WINDOWS):
            mixed = _dot(_pooled(i, g, w, ubuf).astype(BF16), wp_ref[g])
            cols = slice(g * POOL_GROUP_DIM, (g + 1) * POOL_GROUP_DIM)
            out_ref[:, cols] = (mixed * sc_ref[:, cols]).astype(BF16)
        kv = jnp.concatenate([kvp_ref[...], kvc_ref[...]], axis=0)
        k_var = _head_variants(kv[:, 0:2 * HEAD_DIM])
        v_var = _head_variants(kv[:, 2 * HEAD_DIM:])
        for j in range(N_Q_HEADS // 2):
            q2 = q_ref[:, 2 * HEAD_DIM * j:2 * HEAD_DIM * (j + 1)].astype(BF16)
            acc = None
            for half in range(2):
                hq = 2 * j + half
                h = hq // GQA_GROUP
                probs, _ = _head_probs(i, hq, q2, k_var[h][half], biasm_ref, sinks_ref)
                o = _dot(probs.astype(BF16), v_var[h][half])
                acc = o if acc is None else acc + o
            out_ref[:, POOL_WIDTH + 2 * HEAD_DIM * j:POOL_WIDTH + 2 * HEAD_DIM * (j + 1)] = acc.astype(BF16)

        pl.when(i == nb - 1)(finish)

    return pl.pallas_call(
        body, name="mixers_fwd", grid=(nb,),
        out_shape=[jax.ShapeDtypeStruct((t, 2 * POOL_WIDTH), BF16),
                   jax.ShapeDtypeStruct((N_DEV, *gate_up_shard.shape), gate_up_shard.dtype)],
        in_specs=_mixer_in_specs(lambda i: i, lambda i: jnp.maximum(i - 1, 0)) + _mixer_param_specs() + [ANY],
        out_specs=[pl.BlockSpec((BLOCK, 2 * POOL_WIDTH), lambda i: (i, 0)), ANY],
        scratch_shapes=[pltpu.VMEM((HALO + BLOCK, POOL_WIDTH), F32)] + _gather_scratch(1),
        compiler_params=_params(),
    )(proj, proj, proj, proj, proj, biasm, sinks, w_pool, pool_scale, gate_up_shard)


def outproj_norm(cat, w, x, g):
    t, d = x.shape
    tm = TOKEN_TILE

    def body(c_ref, w_ref, x_ref, g_ref, mix_ref, x1_ref):
        mix = _dot(c_ref[...], w_ref[...])
        mix_ref[...] = mix
        x1_ref[...] = x_ref[...] + (mix * _rstd(mix)) * g_ref[...]

    row = pl.BlockSpec((tm, d), lambda i: (i, 0))
    return pl.pallas_call(
        body, name="outproj_norm", grid=(t // tm,),
        out_shape=[jax.ShapeDtypeStruct((t, d), F32)] * 2,
        in_specs=[pl.BlockSpec((tm, cat.shape[1]), lambda i: (i, 0)), pl.BlockSpec(w.shape, lambda i: (0, 0)), row,
                  pl.BlockSpec((1, d), lambda i: (0, 0))],
        out_specs=[row, row],
        compiler_params=_params(),
    )(cat, w, x, g)


def ffn_up(x1, g, gate_up, down_shard):
    t, d = x1.shape
    n = gate_up.shape[2]
    f = N_DEV * n
    tm, ts = TOKEN_TILE, FF_SHARDS_PER_TILE
    tn = ts * n
    steps = (t // tm, f // tn)

    def body(x_ref, g_ref, wg_ref, wu_ref, shard_ref, h_ref, gate_ref, up_ref, a_ref, gathered_ref,
             send_sems, recv_sems, local_sems):
        i, j = pl.program_id(0), pl.program_id(1)
        start, finish = _gather_plan([shard_ref], [gathered_ref], send_sems, recv_sems, local_sems)
        pl.when((i == 0) & (j == 0))(start)

        @pl.when(j == 0)
        def _():
            xv = x_ref[...]
            h_ref[...] = ((xv * _rstd(xv)) * g_ref[...]).astype(BF16)

        h = h_ref[...]
        gate = _dot_nt(h, _merge_rows(wg_ref[...]))
        up = _dot_nt(h, _merge_rows(wu_ref[...]))
        gate_ref[...] = gate.astype(BF16)
        up_ref[...] = up.astype(BF16)
        a_ref[...] = (gate * (1.0 / (1.0 + jnp.exp(-gate))) * up).astype(BF16)

        pl.when((i == steps[0] - 1) & (j == steps[1] - 1))(finish)

    wide = pl.BlockSpec((tm, tn), lambda i, j: (i, j))
    return pl.pallas_call(
        body, name="ffn_up", grid=steps,
        out_shape=[jax.ShapeDtypeStruct((t, d), BF16)] + [jax.ShapeDtypeStruct((t, f), BF16)] * 3
        + [jax.ShapeDtypeStruct((N_DEV, *down_shard.shape), down_shard.dtype)],
        in_specs=[pl.BlockSpec((tm, d), lambda i, j: (i, 0)), pl.BlockSpec((1, d), lambda i, j: (0, 0)),
                  pl.BlockSpec((ts, None, n, d), lambda i, j: (j, 0, 0, 0)),
                  pl.BlockSpec((ts, None, n, d), lambda i, j: (j, 1, 0, 0)), ANY],
        out_specs=[pl.BlockSpec((tm, d), lambda i, j: (i, 0)), wide, wide, wide, ANY],
        scratch_shapes=_gather_scratch(1),
        compiler_params=_params(),
    )(x1, g, gate_up, gate_up, down_shard)


def ffn_down_loss(a, w_down, x1, g, target):
    t, d = x1.shape
    tm = TOKEN_TILE

    def body(a_ref, w_ref, x_ref, g_ref, t_ref, df_ref, dy_ref, dg_ref, loss_ref):
        @pl.when(pl.program_id(0) == 0)
        def _():
            dg_ref[...] = jnp.zeros_like(dg_ref)
            loss_ref[...] = jnp.zeros_like(loss_ref)

        f = _dot(a_ref[...], _merge_rows(w_ref[...]))
        r = _rstd(f)
        g = g_ref[...]
        err = x_ref[...] + (f * r) * g - t_ref[...]
        loss_ref[...] += 0.5 * jnp.sum(jnp.mean(err * err, axis=-1, keepdims=True))
        dy = err * (1.0 / d)
        dy_ref[...] = dy
        df, dg_rows = _norm_bwd(dy, f, r, g)
        df_ref[...] = df.astype(BF16)
        dg_ref[...] += jnp.sum(dg_rows, axis=0, keepdims=True)

    row = pl.BlockSpec((tm, d), lambda i: (i, 0))
    gain = pl.BlockSpec((1, d), lambda i: (0, 0))
    return pl.pallas_call(
        body, name="ffn_down_loss", grid=(t // tm,),
        out_shape=[jax.ShapeDtypeStruct((t, d), BF16), jax.ShapeDtypeStruct((t, d), F32),
                   jax.ShapeDtypeStruct((1, d), F32), jax.ShapeDtypeStruct((1, 128), F32)],
        in_specs=[pl.BlockSpec((tm, a.shape[1]), lambda i: (i, 0)), pl.BlockSpec(w_down.shape, lambda i: (0, 0, 0)), row, gain, row],
        out_specs=[row, row, gain, pl.BlockSpec((1, 128), lambda i: (0, 0))],
        compiler_params=_params(),
    )(a, w_down, x1, g, target)


def ffn_down_bwd(df, w_down, gate, up):
    t, d = df.shape
    n = w_down.shape[1]
    f = gate.shape[1]
    tm, ts = TOKEN_TILE, FF_SHARDS_PER_TILE
    tn = ts * n

    def body(df_ref, w_ref, gate_ref, up_ref, dgate_ref, dup_ref):
        da = _dot_nt(df_ref[...], _merge_rows(w_ref[...]))
        gate = gate_ref[...].astype(F32)
        sig = 1.0 / (1.0 + jnp.exp(-gate))
        dgate_ref[...] = (da * up_ref[...].astype(F32) * (sig * (1.0 + gate * (1.0 - sig)))).astype(BF16)
        dup_ref[...] = (da * (gate * sig)).astype(BF16)

    wide = pl.BlockSpec((tm, tn), lambda i, j: (i, j))
    return pl.pallas_call(
        body, name="ffn_down_bwd", grid=(t // tm, f // tn),
        out_shape=[jax.ShapeDtypeStruct((t, f), BF16)] * 2,
        in_specs=[pl.BlockSpec((tm, d), lambda i, j: (i, 0)), pl.BlockSpec((ts, n, d), lambda i, j: (j, 0, 0)), wide, wide],
        out_specs=[wide, wide],
        compiler_params=_params(),
    )(df, w_down, gate, up)


def grad_rows(a, b, name):
    t, m = a.shape
    d = b.shape[1]
    tt = TOKEN_TILE
    last = t // tt - 1

    def body(a_ref, b_ref, out_ref, acc):
        k = pl.program_id(0)

        @pl.when(k == 0)
        def _():
            acc[...] = jnp.zeros_like(acc)

        acc[...] += _dot_tn(a_ref[...], b_ref[...])

        @pl.when(k == last)
        def _():
            out_ref[...] = acc[...].reshape(out_ref.shape).astype(BF16)

    return pl.pallas_call(
        body, name=name, grid=(t // tt,),
        out_shape=jax.ShapeDtypeStruct((N_DEV, m // N_DEV, d), BF16),
        in_specs=[pl.BlockSpec((tt, m), lambda k: (k, 0)), pl.BlockSpec((tt, d), lambda k: (k, 0))],
        out_specs=pl.BlockSpec((N_DEV, m // N_DEV, d), lambda k: (0, 0, 0)),
        scratch_shapes=[pltpu.VMEM((m, d), F32)],
        compiler_params=_params(),
    )(a, b)


def grad_ffn(lhs, b, name):
    t, f = lhs[0].shape
    d = b.shape[1]
    nw = len(lhs)
    n = f // N_DEV
    tt, ts = TOKEN_TILE, FF_SHARDS_PER_TILE
    tn = ts * n
    last = t // tt - 1

    def body(*refs):
        a_refs, b_ref, out_ref, acc = refs[:nw], refs[nw], refs[nw + 1], refs[nw + 2]
        k = pl.program_id(1)

        @pl.when(k == 0)
        def _():
            acc[...] = jnp.zeros_like(acc)

        for w in range(nw):
            acc[w] += _dot_tn(a_refs[w][...], b_ref[...])

        @pl.when(k == last)
        def _():
            for w in range(nw):
                blocks = acc[w].reshape(ts // 2, 2, n, d)
                for chip in range(ts // 2):
                    for core in range(2):
                        out_ref[core, chip, w] = blocks[chip, core].astype(BF16)

    return pl.pallas_call(
        body, name=name, grid=(f // tn, t // tt),
        out_shape=jax.ShapeDtypeStruct((2, N_CHIP, nw, n, d), BF16),
        in_specs=[pl.BlockSpec((tt, tn), lambda i, k: (k, i))] * nw + [pl.BlockSpec((tt, d), lambda i, k: (k, 0))],
        out_specs=pl.BlockSpec((2, ts // 2, nw, n, d), lambda i, k: (0, i, 0, 0, 0)),
        scratch_shapes=[pltpu.VMEM((nw, tn, d), F32)],
        compiler_params=_params(),
    )(*lhs, b)


def ffn_up_bwd(dgate, dup, gate_up, x1, g_ffn, dy, mix, g_mix):
    t, d = x1.shape
    n = gate_up.shape[2]
    tm, ts = TOKEN_TILE, FF_SHARDS_PER_TILE
    tk = ts * n
    ksteps = N_DEV // ts

    def body(dg_ref, du_ref, wg_ref, wu_ref, x_ref, gf_ref, dy_ref, mix_ref, gm_ref, dx1_ref, dmix_ref, dgf_ref, dgm_ref, acc):
        i, k = pl.program_id(0), pl.program_id(1)

        @pl.when((i == 0) & (k == 0))
        def _():
            dgf_ref[...] = jnp.zeros_like(dgf_ref)
            dgm_ref[...] = jnp.zeros_like(dgm_ref)

        part = _dot(dg_ref[...], _merge_rows(wg_ref[...])) + _dot(du_ref[...], _merge_rows(wu_ref[...]))

        @pl.when(k == 0)
        def _():
            acc[...] = part

        @pl.when(k > 0)
        def _():
            acc[...] += part

        @pl.when(k == ksteps - 1)
        def _():
            x1 = x_ref[...]
            dx, dgf_rows = _norm_bwd(acc[...], x1, _rstd(x1), gf_ref[...])
            dx1 = dy_ref[...] + dx
            dx1_ref[...] = dx1
            dgf_ref[...] += jnp.sum(dgf_rows, axis=0, keepdims=True)
            mix = mix_ref[...]
            dmix, dgm_rows = _norm_bwd(dx1, mix, _rstd(mix), gm_ref[...])
            dmix_ref[...] = dmix.astype(BF16)
            dgm_ref[...] += jnp.sum(dgm_rows, axis=0, keepdims=True)

    row = pl.BlockSpec((tm, d), lambda i, k: (i, 0))
    wide = pl.BlockSpec((tm, tk), lambda i, k: (i, k))
    gain = pl.BlockSpec((1, d), lambda i, k: (0, 0))
    return pl.pallas_call(
        body, name="ffn_up_bwd", grid=(t // tm, ksteps),
        out_shape=[jax.ShapeDtypeStruct((t, d), F32), jax.ShapeDtypeStruct((t, d), BF16),
                   jax.ShapeDtypeStruct((1, d), F32), jax.ShapeDtypeStruct((1, d), F32)],
        in_specs=[wide, wide, pl.BlockSpec((ts, None, n, d), lambda i, k: (k, 0, 0, 0)),
                  pl.BlockSpec((ts, None, n, d), lambda i, k: (k, 1, 0, 0)), row, gain, row, row, gain],
        out_specs=[row, row, gain, gain],
        scratch_shapes=[pltpu.VMEM((tm, d), F32)],
        compiler_params=_params(),
    )(dgate, dup, gate_up, gate_up, x1, g_ffn, dy, mix, g_mix)


def outproj_bwd(dmix, w_out):
    t, d = dmix.shape
    tm = TOKEN_TILE

    def body(dm_ref, w_ref, out_ref):
        out_ref[...] = _dot_nt(dm_ref[...], w_ref[...])

    return pl.pallas_call(
        body, name="outproj_bwd", grid=(t // tm,),
        out_shape=jax.ShapeDtypeStruct((t, w_out.shape[0]), F32),
        in_specs=[pl.BlockSpec((tm, d), lambda i: (i, 0)), pl.BlockSpec(w_out.shape, lambda i: (0, 0))],
        out_specs=pl.BlockSpec((tm, w_out.shape[0]), lambda i: (i, 0)),
        compiler_params=_params(),
    )(dmix, w_out)


def mixers_bwd(proj, dcat, biasm, sinks, w_pool, pool_scale, ffn_parts):
    t = proj.shape[0]
    nb = t // BLOCK
    na = len(ffn_parts)

    def body(*refs):
        (uc_ref, halo_ref, q_ref, kvc_ref, kvp_ref, dcat_ref, biasm_ref, sinks_ref, wp_ref, sc_ref) = refs[:10]
        part_refs = refs[10:10 + na]
        dproj_ref, dbias_ref, dsink_ref, dwp_ref, dsc_ref = refs[10 + na:15 + na]
        slot_refs = refs[15 + na:15 + 2 * na]
        ubuf, dbuf, c_u, c_q, c_kv, send_sems, recv_sems, local_sems = refs[15 + 2 * na:]
        i = pl.program_id(0)
        lane = lax.broadcasted_iota(jnp.int32, (1, 128), 1)
        start, finish = _chip_exchange_plan(part_refs, slot_refs, send_sems, recv_sems, local_sems)

        @pl.when(i == 0)
        def _():
            start()
            dbias_ref[...] = jnp.zeros_like(dbias_ref)
            dwp_ref[...] = jnp.zeros_like(dwp_ref)
            dsc_ref[...] = jnp.zeros_like(dsc_ref)
            dsink_ref[...] = jnp.zeros_like(dsink_ref)
            dbuf[...] = jnp.zeros_like(dbuf)
            c_u[...] = jnp.zeros_like(c_u)
            c_q[...] = jnp.zeros_like(c_q)
            c_kv[...] = jnp.zeros_like(c_kv)

        @pl.when(i < nb)
        def _():
            _fill_pool_input(i, ubuf, uc_ref, halo_ref)
            for g, w in enumerate(POOL_WINDOWS):
                cols = slice(g * POOL_GROUP_DIM, (g + 1) * POOL_GROUP_DIM)
                pooled = _pooled(i, g, w, ubuf).astype(BF16)
                mixed = _dot(pooled, wp_ref[g])
                dout = dcat_ref[:, cols]
                dsc_ref[:, cols] += jnp.sum(dout * mixed, axis=0, keepdims=True)
                dmixed = (dout * sc_ref[:, cols]).astype(BF16)
                dwp_ref[g] += _dot_tn(pooled, dmixed)
                dpooled = _dot_nt(dmixed, wp_ref[g])
                scaled = dpooled * _inv_count(i, w)
                dbuf[BLOCK:, cols] = scaled[0:HALO]
                dproj_ref[:, cols] = (_window_sum(dbuf, g, w, lambda k: k) + c_u[:, cols]).astype(BF16)
                dbuf[0:BLOCK, cols] = scaled
                c_u[:, cols] = -dpooled

            kv = jnp.concatenate([kvp_ref[...], kvc_ref[...]], axis=0)
            k_var = _head_variants(kv[:, 0:2 * HEAD_DIM])
            v_var = _head_variants(kv[:, 2 * HEAD_DIM:])
            q2s = [q_ref[:, 2 * HEAD_DIM * j:2 * HEAD_DIM * (j + 1)].astype(BF16) for j in range(N_Q_HEADS // 2)]
            do2s = [dcat_ref[:, POOL_WIDTH + 2 * HEAD_DIM * j:POOL_WIDTH + 2 * HEAD_DIM * (j + 1)].astype(BF16)
                    for j in range(N_Q_HEADS // 2)]
            dq2 = [None] * (N_Q_HEADS // 2)
            dss_of, pb_of = {}, {}
            dsink_row = jnp.zeros((1, 128), F32)
            for hq in range(N_Q_HEADS):
                j, half, h = hq // 2, hq % 2, hq // GQA_GROUP
                probs, p_sink = _head_probs(i, hq, q2s[j], k_var[h][half], biasm_ref, sinks_ref)
                dp = _dot_nt(do2s[j], v_var[h][half])
                delta = jnp.sum(probs * dp, axis=-1, keepdims=True)
                ds = probs * (dp - delta)
                dbias_ref[hq] += ds
                dsink_row = dsink_row - jnp.where(lane == hq, jnp.sum(p_sink * delta), 0.0)
                dss = (ds * ATTN_SCALE).astype(BF16)
                dss_of[hq], pb_of[hq] = dss, probs.astype(BF16)
                dq = _dot(dss, k_var[h][half])
                dq2[j] = dq if dq2[j] is None else dq2[j] + dq
            dsink_ref[...] += dsink_row
            low = lax.broadcasted_iota(jnp.int32, (2 * BLOCK, 2 * HEAD_DIM), 1) < HEAD_DIM
            dk_half, dv_half = [[None, None], [None, None]], [[None, None], [None, None]]
            for h in range(N_KV_HEADS):
                for half in range(2):
                    heads = [hq for hq in range(GQA_GROUP * h, GQA_GROUP * (h + 1)) if hq % 2 == half]
                    q_rows = jnp.concatenate([q2s[hq // 2] for hq in heads], axis=0)
                    do_rows = jnp.concatenate([do2s[hq // 2] for hq in heads], axis=0)
                    dk_half[h][half] = _dot_tn(jnp.concatenate([dss_of[hq] for hq in heads], axis=0), q_rows)
                    dv_half[h][half] = _dot_tn(jnp.concatenate([pb_of[hq] for hq in heads], axis=0), do_rows)

            def pair_of(halves):
                return jnp.where(low, halves[0][0] + pltpu.roll(halves[0][1], HEAD_DIM, 1),
                                 halves[1][1] + pltpu.roll(halves[1][0], HEAD_DIM, 1))

            dkv = jnp.concatenate([pair_of(dk_half), pair_of(dv_half)], axis=1)
            dproj_ref[:, POOL_WIDTH:2 * POOL_WIDTH] = c_q[...].astype(BF16)
            dproj_ref[:, 2 * POOL_WIDTH:] = (c_kv[...] + dkv[0:BLOCK]).astype(BF16)
            c_q[...] = jnp.concatenate(dq2, axis=1)
            c_kv[...] = dkv[BLOCK:]

        @pl.when(i == nb)
        def _():
            dbuf[BLOCK:, :] = jnp.zeros((HALO, POOL_WIDTH), F32)
            for g, w in enumerate(POOL_WINDOWS):
                cols = slice(g * POOL_GROUP_DIM, (g + 1) * POOL_GROUP_DIM)
                dproj_ref[:, cols] = (_window_sum(dbuf, g, w, lambda k: k) + c_u[:, cols]).astype(BF16)
            dproj_ref[:, POOL_WIDTH:2 * POOL_WIDTH] = c_q[...].astype(BF16)
            dproj_ref[:, 2 * POOL_WIDTH:] = c_kv[...].astype(BF16)
            finish()

    cur = lambda i: jnp.minimum(i, nb - 1)
    prv = lambda i: jnp.maximum(jnp.minimum(i, nb - 1) - 1, 0)
    out = pl.pallas_call(
        body, name="mixers_bwd", grid=(nb + 1,),
        out_shape=[jax.ShapeDtypeStruct((t, proj.shape[1]), BF16),
                   jax.ShapeDtypeStruct((N_Q_HEADS, BLOCK, 2 * BLOCK), F32),
                   jax.ShapeDtypeStruct((1, 128), F32),
                   jax.ShapeDtypeStruct((4, POOL_GROUP_DIM, POOL_GROUP_DIM), F32),
                   jax.ShapeDtypeStruct((1, POOL_WIDTH), F32)]
        + [jax.ShapeDtypeStruct(p.shape, p.dtype) for p in ffn_parts],
        in_specs=_mixer_in_specs(cur, prv) + [pl.BlockSpec((BLOCK, 2 * POOL_WIDTH), lambda i: (cur(i), 0))]
        + _mixer_param_specs() + [ANY] * na,
        out_specs=[pl.BlockSpec((BLOCK, proj.shape[1]), lambda i: (jnp.maximum(i - 1, 0), 0)),
                   pl.BlockSpec((N_Q_HEADS, BLOCK, 2 * BLOCK), lambda i: (0, 0, 0)),
                   pl.BlockSpec((1, 128), lambda i: (0, 0)),
                   pl.BlockSpec((4, POOL_GROUP_DIM, POOL_GROUP_DIM), lambda i: (0, 0, 0)),
                   pl.BlockSpec((1, POOL_WIDTH), lambda i: (0, 0))] + [ANY] * na,
        scratch_shapes=[pltpu.VMEM((HALO + BLOCK, POOL_WIDTH), F32), pltpu.VMEM((BLOCK + HALO, POOL_WIDTH), F32),
                        pltpu.VMEM((BLOCK, POOL_WIDTH), F32), pltpu.VMEM((BLOCK, POOL_WIDTH), F32),
                        pltpu.VMEM((BLOCK, 256), F32)] + _chip_exchange_scratch(na),
        compiler_params=_params(),
    )(proj, proj, proj, proj, proj, dcat, biasm, sinks, w_pool, pool_scale, *ffn_parts)
    return out[:5], out[5:]


def inproj_bwd(dproj, w_in_t, x, g, dx1):
    t, d = x.shape
    n = dproj.shape[1]
    tm = TOKEN_TILE

    def body(dp_ref, w_ref, x_ref, g_ref, dx1_ref, dx_ref, dg_ref):
        @pl.when(pl.program_id(0) == 0)
        def _():
            dg_ref[...] = jnp.zeros_like(dg_ref)

        dh = _dot(dp_ref[...], w_ref[...])
        xv = x_ref[...]
        dx, dg_rows = _norm_bwd(dh, xv, _rstd(xv), g_ref[...])
        dx_ref[...] = dx1_ref[...] + dx
        dg_ref[...] += jnp.sum(dg_rows, axis=0, keepdims=True)

    row = pl.BlockSpec((tm, d), lambda i: (i, 0))
    gain = pl.BlockSpec((1, d), lambda i: (0, 0))
    return pl.pallas_call(
        body, name="inproj_bwd", grid=(t // tm,),
        out_shape=[jax.ShapeDtypeStruct((t, d), F32), jax.ShapeDtypeStruct((1, d), F32)],
        in_specs=[pl.BlockSpec((tm, n), lambda i: (i, 0)), pl.BlockSpec(w_in_t.shape, lambda i: (0, 0)), row, gain, row],
        out_specs=[row, gain],
        compiler_params=_params(),
    )(dproj, w_in_t, x, g, dx1)


def _bucket_band():
    qi = jnp.arange(BLOCK)[:, None]
    kj = jnp.arange(2 * BLOCK)[None, :]
    dist = qi + BLOCK - kj
    n = jnp.maximum(dist, 0)
    nf = jnp.maximum(n, 1).astype(F32)
    large = MAX_EXACT + (jnp.log(nf / MAX_EXACT) / np.float32(np.log(MAX_DISTANCE / MAX_EXACT))
                         * (N_BUCKETS - MAX_EXACT)).astype(jnp.int32)
    large = jnp.minimum(large, N_BUCKETS - 1)
    bucket = jnp.where(n < MAX_EXACT, n, large)
    in_window = (dist >= 0) & (dist < BLOCK)
    return bucket.astype(F32), in_window.astype(F32)


def _pad_row(v):
    flat = v.reshape(-1)
    pad = (-flat.shape[0]) % 128
    return jnp.pad(flat, (0, pad)).reshape(-1, 128)


def kernel(x, g_pre_mix, w_in, w_pool, pool_scale, rel_bias, sinks, w_out, g_post_mix, g_pre_ffn, w_gate, w_up, w_down, g_post_ffn, loss_target, m_g_pre_mix, m_w_in, m_w_pool, m_pool_scale, m_rel_bias, m_sinks, m_w_out, m_g_post_mix, m_g_pre_ffn, m_w_gate, m_w_up, m_w_down, m_g_post_ffn, v_g_pre_mix, v_w_in, v_w_pool, v_pool_scale, v_rel_bias, v_sinks, v_w_out, v_g_post_mix, v_g_pre_ffn, v_w_gate, v_w_up, v_w_down, v_g_post_ffn):
    d = x.shape[-1]
    xs, target = x[0], loss_target[0]

    w_in_ts = w_in[0].T.astype(BF16)
    w_out_s = w_out[0].astype(BF16)
    gate_up_s = jnp.stack([w_gate[0].T, w_up[0].T]).astype(BF16)
    w_down_s = w_down[0].astype(BF16)
    w_in_t, w_out_f = gather_blocks([w_in_ts, w_out_s], "gather_mix_weights")
    w_in_t = w_in_t.reshape(-1, d)
    w_out_f = w_out_f.reshape(-1, d)

    bucket, in_window = _bucket_band()
    biasm = bias_band(bucket, in_window, rel_bias)
    w_pool_b = w_pool[0].astype(BF16)
    proj, h1 = norm_inproj(xs, g_pre_mix, w_in_t)
    cat, gate_up = mixers_fwd(proj, biasm, sinks, w_pool_b, pool_scale, gate_up_s)
    mix, x1 = outproj_norm(cat, w_out_f, xs, g_post_mix)
    h2, gate, up, act, w_down_f = ffn_up(x1, g_pre_ffn, gate_up, w_down_s)
    df, dy, dg_post_ffn, loss_part = ffn_down_loss(act, w_down_f, x1, g_post_ffn, target)

    dgate, dup = ffn_down_bwd(df, w_down_f, gate, up)
    d_gate_up = grad_ffn([dgate, dup], h2, "grad_w_gate_up")
    d_down = grad_ffn([act], df, "grad_w_down")
    pair = pair_exchange([d_gate_up, d_down], "pair_exchange_ffn")
    chip_parts = pair_add(pair[:2], pair[2:])
    dx1, dmix, dg_pre_ffn, dg_post_mix = ffn_up_bwd(dgate, dup, gate_up, x1, g_pre_ffn, dy, mix, g_post_mix)

    dcat = outproj_bwd(dmix, w_out_f)
    d_out = grad_rows(cat, dmix, "grad_w_out")
    (dproj, dbias, dsinks, dw_pool, dpool_scale), ffn_slots = mixers_bwd(
        proj, dcat, biasm, sinks, w_pool_b, pool_scale, chip_parts)
    drel_bias = bias_band_bwd(bucket, dbias)
    grad_x, dg_pre_mix = inproj_bwd(dproj, w_in_t, xs, g_pre_mix, dx1)
    d_in_t = grad_rows(dproj, h1, "grad_w_in")

    g_gate_up_t, g_down = sum_slots(ffn_slots, "sum_ffn")
    mix_slots = exchange_partials([d_in_t, d_out], "exchange_mix_grads")
    g_in_t, g_out = sum_slots(mix_slots, "sum_mix")
    big_w = [w_in[0], w_out[0], w_gate[0], w_up[0], w_down[0]]
    big_g = [g_in_t.T, g_out, g_gate_up_t[0].T, g_gate_up_t[1].T, g_down[0]]
    big_m = [m_w_in[0], m_w_out[0], m_w_gate[0], m_w_up[0], m_w_down[0]]
    big_v = [v_w_in[0], v_w_out[0], v_w_gate[0], v_w_up[0], v_w_down[0]]
    upd = adamw_update(big_w[:2], big_g[:2], big_m[:2], big_v[:2], "adamw_mix") \
        + adamw_update(big_w[2:], big_g[2:], big_m[2:], big_v[2:], "adamw_ffn")
    big_un = [[big_g[k][None] for k in range(5)]] + [[upd[k][q][None] for k in range(5)] for q in range(3)]

    small_w = [g_pre_mix, pool_scale, rel_bias, sinks, g_post_mix, g_pre_ffn, g_post_ffn, w_pool]
    small_m = [m_g_pre_mix, m_pool_scale, m_rel_bias, m_sinks, m_g_post_mix, m_g_pre_ffn, m_g_post_ffn, m_w_pool]
    small_v = [v_g_pre_mix, v_pool_scale, v_rel_bias, v_sinks, v_g_post_mix, v_g_pre_ffn, v_g_post_ffn, v_w_pool]
    small_g = [dg_pre_mix, dpool_scale, drel_bias, dsinks[:, :N_Q_HEADS], dg_post_mix, dg_pre_ffn, dg_post_ffn, dw_pool]
    zero = jnp.zeros((1,), F32)
    pack = lambda parts, last: jnp.concatenate([_pad_row(p) for p in parts] + [_pad_row(last)], axis=0)
    small_out = small_allreduce_adamw(pack(small_g, loss_part[0, :1]), pack(small_w, zero), pack(small_m, zero),
                                      pack(small_v, zero))
    small_rows = [_pad_row(p).shape[0] for p in small_w]
    s_starts = np.concatenate([[0], np.cumsum(small_rows)])

    def small_un(o):
        return [o[s_starts[k]:s_starts[k + 1]].reshape(-1)[:small_w[k].size].reshape(small_w[k].shape) for k in range(8)]

    loss = small_out[0][s_starts[8], 0]
    sm = [small_un(o) for o in small_out]

    def ordered(kind):
        s, b = sm[kind], big_un[kind]
        return [s[0], b[0], s[7], s[1], s[2], s[3], b[1], s[4], s[5], b[2], b[3], b[4], s[6]]

    return (loss, grad_x[None], *ordered(0), *ordered(1), *ordered(2), *ordered(3))
```

```python
import numpy as np
import jax
import jax.numpy as jnp
from jax import lax
from jax.experimental import pallas as pl
from jax.experimental.pallas import tpu as pltpu

F32 = jnp.float32
BF16 = jnp.bfloat16

N_DEV = 8
N_CHIP = 4
POOL_WIDTH = 512
POOL_WINDOWS = (2, 4, 8, 16)
POOL_GROUP_DIM = 128
HEAD_DIM = 64
N_Q_HEADS = 8
N_KV_HEADS = 2
GQA_GROUP = 4
BLOCK = 128
HALO = 16
N_BUCKETS = 32
MAX_EXACT = 16
MAX_DISTANCE = 128
EPS = 1e-6
NEG_INF = -1e30
ATTN_SCALE = float(1.0 / np.sqrt(np.float32(HEAD_DIM)))

ADAM_LR = 0.001
ADAM_B1 = 0.9
ADAM_B2 = 0.999
ADAM_EPS = 1e-08
ADAM_WD = 0.01
ADAM_STEP = 10

TOKEN_TILE = 512
FF_SHARDS_PER_TILE = 4
VMEM_LIMIT = 56 * 1024 * 1024
MESH = pl.DeviceIdType.MESH
ANY = pl.BlockSpec(memory_space=pl.ANY)
VMEM = pl.BlockSpec(memory_space=pltpu.VMEM)
SMEM = pl.BlockSpec(memory_space=pltpu.SMEM)


def _params(**kw):
    return pltpu.CompilerParams(vmem_limit_bytes=VMEM_LIMIT, **kw)


def _dot(a, b):
    return jnp.dot(a, b, preferred_element_type=F32)


def _dot_nt(a, b):
    return lax.dot_general(a, b, (((1,), (1,)), ((), ())), preferred_element_type=F32)


def _dot_tn(a, b):
    return lax.dot_general(a, b, (((0,), (0,)), ((), ())), preferred_element_type=F32)


def _rstd(v):
    return lax.rsqrt(jnp.mean(v * v, axis=-1, keepdims=True) + EPS)


def _norm_bwd(dout, v, r, g):
    vn = v * r
    dn = dout * g
    dv = r * (dn - vn * jnp.mean(dn * vn, axis=-1, keepdims=True))
    return dv, dout * vn


def _merge_rows(value):
    s, r, c_ = value.shape
    return value.reshape(s * r, c_)


def _gather_plan(srcs, outs, send_sems, recv_sems, local_sems):
    n = len(srcs)
    x, y, c = lax.axis_index("x"), lax.axis_index("y"), lax.axis_index("c")
    me, sibling = (x, y, c), (x, y, 1 - c)
    chips = [(1 - x, y), (x, 1 - y), (1 - x, 1 - y)]

    def slot(a, px, py, pc):
        return outs[a].at[4 * px + 2 * py + pc]

    def copy(a, k, block, to, from_src=False):
        return pltpu.make_async_remote_copy(
            src_ref=srcs[a] if from_src else slot(a, *block), dst_ref=slot(a, *block),
            send_sem=send_sems.at[k * n + a], recv_sem=recv_sems.at[k * n + a], device_id=to, device_id_type=MESH)

    def local(a):
        return pltpu.make_async_copy(srcs[a], slot(a, *me), local_sems.at[a])

    def first(a):
        return [copy(a, 0, me, sibling, True)] + [copy(a, 1 + j, me, (*chip, c), True) for j, chip in enumerate(chips)]

    def passed(a, j):
        return copy(a, 4 + j, (*chips[j], c), sibling)

    def start():
        for a in range(n):
            local(a).start()
            for cp in first(a):
                cp.start()

    def finish():
        for j, chip in enumerate(chips):
            for a in range(n):
                copy(a, 1 + j, (*chip, c), me).wait_recv()
                passed(a, j).start()
        for a in range(n):
            copy(a, 0, sibling, me).wait_recv()
            for j, chip in enumerate(chips):
                copy(a, 4 + j, (*chip, 1 - c), me).wait_recv()
        for a in range(n):
            for cp in first(a) + [passed(a, j) for j in range(3)]:
                cp.wait_send()
            local(a).wait()

    return start, finish


def _gather_scratch(n):
    return [pltpu.SemaphoreType.DMA((7 * n,)), pltpu.SemaphoreType.DMA((7 * n,)), pltpu.SemaphoreType.DMA((n,))]


def _chip_exchange_plan(srcs, outs, send_sems, recv_sems, local_sems):
    n = len(srcs)
    x, y, c = lax.axis_index("x"), lax.axis_index("y"), lax.axis_index("c")
    my_chip = 2 * x + y

    def copies():
        out = []
        for a in range(n):
            for k in range(1, N_CHIP):
                px, py = x ^ (k >> 1), y ^ (k & 1)
                out.append(pltpu.make_async_remote_copy(
                    src_ref=srcs[a].at[2 * px + py], dst_ref=outs[a].at[my_chip],
                    send_sem=send_sems.at[(k - 1) * n + a], recv_sem=recv_sems.at[(k - 1) * n + a],
                    device_id=(px, py, c), device_id_type=MESH))
        return out

    def local(a):
        return pltpu.make_async_copy(srcs[a].at[my_chip], outs[a].at[my_chip], local_sems.at[a])

    def start():
        for a in range(n):
            local(a).start()
        for cp in copies():
            cp.start()

    def finish():
        for cp in copies():
            cp.wait()
        for a in range(n):
            local(a).wait()

    return start, finish


def _chip_exchange_scratch(n):
    return [pltpu.SemaphoreType.DMA((3 * n,)), pltpu.SemaphoreType.DMA((3 * n,)), pltpu.SemaphoreType.DMA((n,))]


def gather_blocks(shards, name):
    def body(*refs):
        n = len(shards)
        start, finish = _gather_plan(refs[:n], refs[n:2 * n], *refs[2 * n:])
        start()
        finish()

    return pl.pallas_call(
        body, name=name,
        out_shape=[jax.ShapeDtypeStruct((N_DEV, *s.shape), s.dtype) for s in shards],
        in_specs=[ANY] * len(shards), out_specs=[ANY] * len(shards),
        scratch_shapes=_gather_scratch(len(shards)),
    )(*shards)


def exchange_partials(parts, name):
    n = len(parts)

    def body(*refs):
        p_refs, out_refs, send_sems, recv_sems, local_sems = refs[:n], refs[n:2 * n], *refs[2 * n:]
        x, y, c = lax.axis_index("x"), lax.axis_index("y"), lax.axis_index("c")
        my_id = 4 * x + 2 * y + c
        waits = []
        for a in range(n):
            mine = pltpu.make_async_copy(p_refs[a].at[my_id], out_refs[a].at[my_id], local_sems.at[a])
            mine.start()
            waits.append(mine)
            for k in range(1, N_DEV):
                px, py, pc = x ^ (k >> 2), y ^ ((k >> 1) & 1), c ^ (k & 1)
                cp = pltpu.make_async_remote_copy(
                    src_ref=p_refs[a].at[4 * px + 2 * py + pc], dst_ref=out_refs[a].at[my_id],
                    send_sem=send_sems.at[(k - 1) * n + a], recv_sem=recv_sems.at[(k - 1) * n + a],
                    device_id=(px, py, pc), device_id_type=MESH)
                cp.start()
                waits.append(cp)
        for cp in waits:
            cp.wait()

    return pl.pallas_call(
        body, name=name,
        out_shape=[jax.ShapeDtypeStruct(p.shape, p.dtype) for p in parts],
        in_specs=[ANY] * n, out_specs=[ANY] * n,
        scratch_shapes=[pltpu.SemaphoreType.DMA((7 * n,)), pltpu.SemaphoreType.DMA((7 * n,)), pltpu.SemaphoreType.DMA((n,))],
    )(*parts)


def pair_exchange(parts, name):
    n = len(parts)

    def body(*refs):
        p_refs, got_refs, send_sems, recv_sems = refs[:n], refs[n:2 * n], *refs[2 * n:]
        x, y, c = lax.axis_index("x"), lax.axis_index("y"), lax.axis_index("c")
        copies = [pltpu.make_async_remote_copy(
            src_ref=p_refs[a].at[1 - c], dst_ref=got_refs[a], send_sem=send_sems.at[a], recv_sem=recv_sems.at[a],
            device_id=(x, y, 1 - c), device_id_type=MESH) for a in range(n)]
        for cp in copies:
            cp.start()
        for cp in copies:
            cp.wait()

    return pl.pallas_call(
        body, name=name, out_shape=[jax.ShapeDtypeStruct(p.shape[1:], p.dtype) for p in parts],
        in_specs=[ANY] * n, out_specs=[ANY] * n,
        scratch_shapes=[pltpu.SemaphoreType.DMA((n,)), pltpu.SemaphoreType.DMA((n,))],
    )(*parts)


def pair_add(parts, got):
    n = len(parts)

    def body(core_ref, *refs):
        for a in range(n):
            refs[2 * n + a][...] = (refs[a][...].astype(F32) + refs[n + a][...].astype(F32)).astype(BF16)

    def own(p):
        zeros = (0,) * (p.ndim - 2)
        return pl.BlockSpec((None, 1, *p.shape[2:]), lambda i, core: (core[0], i, *zeros))

    def plain(p):
        zeros = (0,) * (p.ndim - 1)
        return pl.BlockSpec((1, *p.shape[1:]), lambda i, core: (i, *zeros))

    core = lax.axis_index("c").astype(jnp.int32).reshape(1)
    return pl.pallas_call(
        body, name="pair_add",
        grid_spec=pltpu.PrefetchScalarGridSpec(
            num_scalar_prefetch=1, grid=(got[0].shape[0],),
            in_specs=[own(p) for p in parts] + [plain(p) for p in got], out_specs=[plain(p) for p in got]),
        out_shape=[jax.ShapeDtypeStruct(p.shape, BF16) for p in got],
        compiler_params=_params(),
    )(core, *parts, *got)


def sum_slots(slots, name):
    n = len(slots)

    def body(*refs):
        for a in range(n):
            total = refs[a][0].astype(F32)
            for s in range(1, slots[a].shape[0]):
                total = total + refs[a][s].astype(F32)
            refs[n + a][...] = total

    return pl.pallas_call(
        body, name=name,
        out_shape=[jax.ShapeDtypeStruct(p.shape[1:], F32) for p in slots],
        in_specs=[VMEM] * n, out_specs=[VMEM] * n,
        compiler_params=_params(),
    )(*slots)


def _adamw(w, g, m, v):
    m2 = ADAM_B1 * m + (1.0 - ADAM_B1) * g
    v2 = ADAM_B2 * v + (1.0 - ADAM_B2) * (g * g)
    m_hat = m2 / (1.0 - ADAM_B1 ** ADAM_STEP)
    v_hat = v2 / (1.0 - ADAM_B2 ** ADAM_STEP)
    delta = -ADAM_LR * (m_hat / (jnp.sqrt(v_hat) + ADAM_EPS) + ADAM_WD * w)
    return delta, m2, v2


def adamw_update(ws, gs, ms, vs, name):
    n = len(ws)

    def body(*refs):
        for a in range(n):
            delta, m2, v2 = _adamw(refs[a][...], refs[n + a][...], refs[2 * n + a][...], refs[3 * n + a][...])
            refs[4 * n + 3 * a][...] = delta
            refs[4 * n + 3 * a + 1][...] = m2
            refs[4 * n + 3 * a + 2][...] = v2

    out = pl.pallas_call(
        body, name=name,
        out_shape=[jax.ShapeDtypeStruct(w.shape, F32) for w in ws for _ in range(3)],
        in_specs=[VMEM] * (4 * n), out_specs=[VMEM] * (3 * n),
        compiler_params=_params(),
    )(*ws, *gs, *ms, *vs)
    return [out[3 * a:3 * a + 3] for a in range(n)]


def small_allreduce_adamw(part, w, m, v):
    r, c_ = part.shape

    def body(p_ref, w_ref, m_ref, v_ref, g_ref, d_ref, m2_ref, v2_ref, gat_ref, send_sems, recv_sems, local_sems):
        start, finish = _gather_plan([p_ref], [gat_ref], send_sems, recv_sems, local_sems)
        start()
        finish()
        total = gat_ref[0]
        for s in range(1, N_DEV):
            total = total + gat_ref[s]
        g_ref[...] = total
        delta, m2, v2 = _adamw(w_ref[...], total, m_ref[...], v_ref[...])
        d_ref[...] = delta
        m2_ref[...] = m2
        v2_ref[...] = v2

    return pl.pallas_call(
        body, name="small_allreduce_adamw",
        out_shape=[jax.ShapeDtypeStruct((r, c_), F32)] * 4,
        in_specs=[VMEM] * 4, out_specs=[VMEM] * 4,
        scratch_shapes=[pltpu.VMEM((N_DEV, r, c_), F32)] + _gather_scratch(1),
    )(part, w, m, v)


def norm_inproj(x, g, w_t):
    t, d = x.shape
    n = w_t.shape[0]
    tm = TOKEN_TILE

    def body(x_ref, g_ref, w_ref, proj_ref, h_ref):
        xv = x_ref[...]
        h = ((xv * _rstd(xv)) * g_ref[...]).astype(BF16)
        h_ref[...] = h
        proj_ref[...] = _dot_nt(h, w_ref[...])

    return pl.pallas_call(
        body, name="norm_inproj", grid=(t // tm,),
        out_shape=[jax.ShapeDtypeStruct((t, n), F32), jax.ShapeDtypeStruct((t, d), BF16)],
        in_specs=[pl.BlockSpec((tm, d), lambda i: (i, 0)), pl.BlockSpec((1, d), lambda i: (0, 0)),
                  pl.BlockSpec((n, d), lambda i: (0, 0))],
        out_specs=[pl.BlockSpec((tm, n), lambda i: (i, 0)), pl.BlockSpec((tm, d), lambda i: (i, 0))],
        compiler_params=_params(),
    )(x, g, w_t)


def bias_band(bucket, in_window, rel_bias):
    def body(bk_ref, win_ref, rb_ref, out_ref):
        bk = bk_ref[...]
        keep = win_ref[...] > 0.5
        for h in range(N_Q_HEADS):
            acc = jnp.zeros(bk.shape, F32)
            for b in range(N_BUCKETS):
                acc = jnp.where(bk == float(b), rb_ref[b, h], acc)
            out_ref[h] = jnp.where(keep, acc, NEG_INF)

    return pl.pallas_call(
        body, name="bias_band",
        out_shape=jax.ShapeDtypeStruct((N_Q_HEADS, BLOCK, 2 * BLOCK), F32),
        in_specs=[VMEM, VMEM, SMEM], out_specs=VMEM,
    )(bucket, in_window, rel_bias)


def bias_band_bwd(bucket, dbias):
    def body(bk_ref, db_ref, out_ref):
        bk = bk_ref[...]
        for h in range(N_Q_HEADS):
            db = db_ref[h]
            for b in range(N_BUCKETS):
                out_ref[b, h] = jnp.sum(jnp.where(bk == float(b), db, 0.0))

    return pl.pallas_call(
        body, name="bias_band_bwd",
        out_shape=jax.ShapeDtypeStruct((N_BUCKETS, N_Q_HEADS), F32),
        in_specs=[VMEM, VMEM], out_specs=SMEM,
    )(bucket, dbias)


def _window_sum(buf_ref, g, w, first):
    cols = slice(g * POOL_GROUP_DIM, (g + 1) * POOL_GROUP_DIM)
    acc = None
    for k in range(w):
        piece = buf_ref[first(k):first(k) + BLOCK, cols]
        acc = piece if acc is None else acc + piece
    return acc


def _inv_count(i, w):
    row = lax.broadcasted_iota(jnp.int32, (BLOCK, 1), 0)
    return 1.0 / jnp.minimum(i * BLOCK + row + 1, w).astype(F32)


def _fill_pool_input(i, ubuf, uc_ref, halo_ref):
    ubuf[0:HALO, :] = jnp.where(i > 0, halo_ref[...], 0.0)
    ubuf[HALO:, :] = uc_ref[...]


def _pooled(i, g, w, ubuf):
    cols = slice(g * POOL_GROUP_DIM, (g + 1) * POOL_GROUP_DIM)
    return _window_sum(ubuf, g, w, lambda k: HALO - k) * _inv_count(i, w) - ubuf[HALO:, cols]


def _head_variants(pair):
    low = lax.broadcasted_iota(jnp.int32, pair.shape, 1) < HEAD_DIM
    swapped = pltpu.roll(pair, HEAD_DIM, 1)
    zero = jnp.zeros_like(pair)
    pick = lambda c, a, b: jnp.where(c, a, b).astype(BF16)
    return [[pick(low, pair, zero), pick(low, zero, swapped)], [pick(low, swapped, zero), pick(low, zero, pair)]]


def _head_probs(i, hq, q2, k_var, biasm_ref, sinks_ref):
    s = _dot_nt(q2, k_var) * ATTN_SCALE + biasm_ref[hq]
    col = lax.broadcasted_iota(jnp.int32, s.shape, 1)
    s = jnp.where((i == 0) & (col < BLOCK), NEG_INF, s)
    sink = sinks_ref[0, hq]
    m = jnp.maximum(jnp.max(s, axis=-1, keepdims=True), sink)
    p = jnp.exp(s - m)
    e_sink = jnp.exp(sink - m)
    inv = 1.0 / (jnp.sum(p, axis=-1, keepdims=True) + e_sink)
    return p * inv, e_sink * inv


def _mixer_in_specs(cur, prv):
    return [pl.BlockSpec((BLOCK, 512), lambda i: (cur(i), 0)),
            pl.BlockSpec((HALO, 512), lambda i: (jnp.maximum(cur(i) * (BLOCK // HALO) - 1, 0), 0)),
            pl.BlockSpec((BLOCK, 512), lambda i: (cur(i), 1)),
            pl.BlockSpec((BLOCK, 256), lambda i: (cur(i), 4)),
            pl.BlockSpec((BLOCK, 256), lambda i: (prv(i), 4))]


def _mixer_param_specs():
    return [pl.BlockSpec((N_Q_HEADS, BLOCK, 2 * BLOCK), lambda i: (0, 0, 0)), SMEM,
            pl.BlockSpec((4, POOL_GROUP_DIM, POOL_GROUP_DIM), lambda i: (0, 0, 0)),
            pl.BlockSpec((1, POOL_WIDTH), lambda i: (0, 0))]


def mixers_fwd(proj, biasm, sinks, w_pool, pool_scale, gate_up_shard):
    t = proj.shape[0]
    nb = t // BLOCK

    def body(uc_ref, halo_ref, q_ref, kvc_ref, kvp_ref, biasm_ref, sinks_ref, wp_ref, sc_ref, shard_ref,
             out_ref, gathered_ref, ubuf, send_sems, recv_sems, local_sems):
        i = pl.program_id(0)
        start, finish = _gather_plan([shard_ref], [gathered_ref], send_sems, recv_sems, local_sems)
        pl.when(i == 0)(start)

        _fill_pool_input(i, ubuf, uc_ref, halo_ref)
        for g, w in enumerate(POOL_WINDOWS):
            mixed = _dot(_pooled(i, g, w, ubuf).astype(BF16), wp_ref[g])
            cols = slice(g * POOL_GROUP_DIM, (g + 1) * POOL_GROUP_DIM)
            out_ref[:, cols] = (mixed * sc_ref[:, cols]).astype(BF16)
        kv = jnp.concatenate([kvp_ref[...], kvc_ref[...]], axis=0)
        k_var = _head_variants(kv[:, 0:2 * HEAD_DIM])
        v_var = _head_variants(kv[:, 2 * HEAD_DIM:])
        for j in range(N_Q_HEADS // 2):
            q2 = q_ref[:, 2 * HEAD_DIM * j:2 * HEAD_DIM * (j + 1)].astype(BF16)
            acc = None
            for half in range(2):
                hq = 2 * j + half
                h = hq // GQA_GROUP
                probs, _ = _head_probs(i, hq, q2, k_var[h][half], biasm_ref, sinks_ref)
                o = _dot(probs.astype(BF16), v_var[h][half])
                acc = o if acc is None else acc + o
            out_ref[:, POOL_WIDTH + 2 * HEAD_DIM * j:POOL_WIDTH + 2 * HEAD_DIM * (j + 1)] = acc.astype(BF16)

        pl.when(i == nb - 1)(finish)

    return pl.pallas_call(
        body, name="mixers_fwd", grid=(nb,),
        out_shape=[jax.ShapeDtypeStruct((t, 2 * POOL_WIDTH), BF16),
                   jax.ShapeDtypeStruct((N_DEV, *gate_up_shard.shape), gate_up_shard.dtype)],
        in_specs=_mixer_in_specs(lambda i: i, lambda i: jnp.maximum(i - 1, 0)) + _mixer_param_specs() + [ANY],
        out_specs=[pl.BlockSpec((BLOCK, 2 * POOL_WIDTH), lambda i: (i, 0)), ANY],
        scratch_shapes=[pltpu.VMEM((HALO + BLOCK, POOL_WIDTH), F32)] + _gather_scratch(1),
        compiler_params=_params(),
    )(proj, proj, proj, proj, proj, biasm, sinks, w_pool, pool_scale, gate_up_shard)


def outproj_norm(cat, w, x, g):
    t, d = x.shape
    tm = TOKEN_TILE

    def body(c_ref, w_ref, x_ref, g_ref, mix_ref, x1_ref):
        mix = _dot(c_ref[...], w_ref[...])
        mix_ref[...] = mix
        x1_ref[...] = x_ref[...] + (mix * _rstd(mix)) * g_ref[...]

    row = pl.BlockSpec((tm, d), lambda i: (i, 0))
    return pl.pallas_call(
        body, name="outproj_norm", grid=(t // tm,),
        out_shape=[jax.ShapeDtypeStruct((t, d), F32)] * 2,
        in_specs=[pl.BlockSpec((tm, cat.shape[1]), lambda i: (i, 0)), pl.BlockSpec(w.shape, lambda i: (0, 0)), row,
                  pl.BlockSpec((1, d), lambda i: (0, 0))],
        out_specs=[row, row],
        compiler_params=_params(),
    )(cat, w, x, g)


def ffn_up(x1, g, gate_up, down_shard):
    t, d = x1.shape
    n = gate_up.shape[2]
    f = N_DEV * n
    tm, ts = TOKEN_TILE, FF_SHARDS_PER_TILE
    tn = ts * n
    steps = (t // tm, f // tn)

    def body(x_ref, g_ref, wg_ref, wu_ref, shard_ref, h_ref, gate_ref, up_ref, a_ref, gathered_ref,
             send_sems, recv_sems, local_sems):
        i, j = pl.program_id(0), pl.program_id(1)
        start, finish = _gather_plan([shard_ref], [gathered_ref], send_sems, recv_sems, local_sems)
        pl.when((i == 0) & (j == 0))(start)

        @pl.when(j == 0)
        def _():
            xv = x_ref[...]
            h_ref[...] = ((xv * _rstd(xv)) * g_ref[...]).astype(BF16)

        h = h_ref[...]
        gate = _dot_nt(h, _merge_rows(wg_ref[...]))
        up = _dot_nt(h, _merge_rows(wu_ref[...]))
        gate_ref[...] = gate.astype(BF16)
        up_ref[...] = up.astype(BF16)
        a_ref[...] = (gate * (1.0 / (1.0 + jnp.exp(-gate))) * up).astype(BF16)

        pl.when((i == steps[0] - 1) & (j == steps[1] - 1))(finish)

    wide = pl.BlockSpec((tm, tn), lambda i, j: (i, j))
    return pl.pallas_call(
        body, name="ffn_up", grid=steps,
        out_shape=[jax.ShapeDtypeStruct((t, d), BF16)] + [jax.ShapeDtypeStruct((t, f), BF16)] * 3
        + [jax.ShapeDtypeStruct((N_DEV, *down_shard.shape), down_shard.dtype)],
        in_specs=[pl.BlockSpec((tm, d), lambda i, j: (i, 0)), pl.BlockSpec((1, d), lambda i, j: (0, 0)),
                  pl.BlockSpec((ts, None, n, d), lambda i, j: (j, 0, 0, 0)),
                  pl.BlockSpec((ts, None, n, d), lambda i, j: (j, 1, 0, 0)), ANY],
        out_specs=[pl.BlockSpec((tm, d), lambda i, j: (i, 0)), wide, wide, wide, ANY],
        scratch_shapes=_gather_scratch(1),
        compiler_params=_params(),
    )(x1, g, gate_up, gate_up, down_shard)


def ffn_down_loss(a, w_down, x1, g, target):
    t, d = x1.shape
    tm = TOKEN_TILE

    def body(a_ref, w_ref, x_ref, g_ref, t_ref, df_ref, dy_ref, dg_ref, loss_ref):
        @pl.when(pl.program_id(0) == 0)
        def _():
            dg_ref[...] = jnp.zeros_like(dg_ref)
            loss_ref[...] = jnp.zeros_like(loss_ref)

        f = _dot(a_ref[...], _merge_rows(w_ref[...]))
        r = _rstd(f)
        g = g_ref[...]
        err = x_ref[...] + (f * r) * g - t_ref[...]
        loss_ref[...] += 0.5 * jnp.sum(jnp.mean(err * err, axis=-1, keepdims=True))
        dy = err * (1.0 / d)
        dy_ref[...] = dy
        df, dg_rows = _norm_bwd(dy, f, r, g)
        df_ref[...] = df.astype(BF16)
        dg_ref[...] += jnp.sum(dg_rows, axis=0, keepdims=True)

    row = pl.BlockSpec((tm, d), lambda i: (i, 0))
    gain = pl.BlockSpec((1, d), lambda i: (0, 0))
    return pl.pallas_call(
        body, name="ffn_down_loss", grid=(t // tm,),
        out_shape=[jax.ShapeDtypeStruct((t, d), BF16), jax.ShapeDtypeStruct((t, d), F32),
                   jax.ShapeDtypeStruct((1, d), F32), jax.ShapeDtypeStruct((1, 128), F32)],
        in_specs=[pl.BlockSpec((tm, a.shape[1]), lambda i: (i, 0)), pl.BlockSpec(w_down.shape, lambda i: (0, 0, 0)), row, gain, row],
        out_specs=[row, row, gain, pl.BlockSpec((1, 128), lambda i: (0, 0))],
        compiler_params=_params(),
    )(a, w_down, x1, g, target)


def ffn_down_bwd(df, w_down, gate, up):
    t, d = df.shape
    n = w_down.shape[1]
    f = gate.shape[1]
    tm, ts = TOKEN_TILE, FF_SHARDS_PER_TILE
    tn = ts * n

    def body(df_ref, w_ref, gate_ref, up_ref, dgate_ref, dup_ref):
        da = _dot_nt(df_ref[...], _merge_rows(w_ref[...]))
        gate = gate_ref[...].astype(F32)
        sig = 1.0 / (1.0 + jnp.exp(-gate))
        dgate_ref[...] = (da * up_ref[...].astype(F32) * (sig * (1.0 + gate * (1.0 - sig)))).astype(BF16)
        dup_ref[...] = (da * (gate * sig)).astype(BF16)

    wide = pl.BlockSpec((tm, tn), lambda i, j: (i, j))
    return pl.pallas_call(
        body, name="ffn_down_bwd", grid=(t // tm, f // tn),
        out_shape=[jax.ShapeDtypeStruct((t, f), BF16)] * 2,
        in_specs=[pl.BlockSpec((tm, d), lambda i, j: (i, 0)), pl.BlockSpec((ts, n, d), lambda i, j: (j, 0, 0)), wide, wide],
        out_specs=[wide, wide],
        compiler_params=_params(),
    )(df, w_down, gate, up)


def grad_rows(a, b, name):
    t, m = a.shape
    d = b.shape[1]
    tt = TOKEN_TILE
    last = t // tt - 1

    def body(a_ref, b_ref, out_ref, acc):
        k = pl.program_id(0)

        @pl.when(k == 0)
        def _():
            acc[...] = jnp.zeros_like(acc)

        acc[...] += _dot_tn(a_ref[...], b_ref[...])

        @pl.when(k == last)
        def _():
            out_ref[...] = acc[...].reshape(out_ref.shape).astype(BF16)

    return pl.pallas_call(
        body, name=name, grid=(t // tt,),
        out_shape=jax.ShapeDtypeStruct((N_DEV, m // N_DEV, d), BF16),
        in_specs=[pl.BlockSpec((tt, m), lambda k: (k, 0)), pl.BlockSpec((tt, d), lambda k: (k, 0))],
        out_specs=pl.BlockSpec((N_DEV, m // N_DEV, d), lambda k: (0, 0, 0)),
        scratch_shapes=[pltpu.VMEM((m, d), F32)],
        compiler_params=_params(),
    )(a, b)


def grad_ffn(lhs, b, name):
    t, f = lhs[0].shape
    d = b.shape[1]
    nw = len(lhs)
    n = f // N_DEV
    tt, ts = TOKEN_TILE, FF_SHARDS_PER_TILE
    tn = ts * n
    last = t // tt - 1

    def body(*refs):
        a_refs, b_ref, out_ref, acc = refs[:nw], refs[nw], refs[nw + 1], refs[nw + 2]
        k = pl.program_id(1)

        @pl.when(k == 0)
        def _():
            acc[...] = jnp.zeros_like(acc)

        for w in range(nw):
            acc[w] += _dot_tn(a_refs[w][...], b_ref[...])

        @pl.when(k == last)
        def _():
            for w in range(nw):
                blocks = acc[w].reshape(ts // 2, 2, n, d)
                for chip in range(ts // 2):
                    for core in range(2):
                        out_ref[core, chip, w] = blocks[chip, core].astype(BF16)

    return pl.pallas_call(
        body, name=name, grid=(f // tn, t // tt),
        out_shape=jax.ShapeDtypeStruct((2, N_CHIP, nw, n, d), BF16),
        in_specs=[pl.BlockSpec((tt, tn), lambda i, k: (k, i))] * nw + [pl.BlockSpec((tt, d), lambda i, k: (k, 0))],
        out_specs=pl.BlockSpec((2, ts // 2, nw, n, d), lambda i, k: (0, i, 0, 0, 0)),
        scratch_shapes=[pltpu.VMEM((nw, tn, d), F32)],
        compiler_params=_params(),
    )(*lhs, b)


def ffn_up_bwd(dgate, dup, gate_up, x1, g_ffn, dy, mix, g_mix):
    t, d = x1.shape
    n = gate_up.shape[2]
    tm, ts = TOKEN_TILE, FF_SHARDS_PER_TILE
    tk = ts * n
    ksteps = N_DEV // ts

    def body(dg_ref, du_ref, wg_ref, wu_ref, x_ref, gf_ref, dy_ref, mix_ref, gm_ref, dx1_ref, dmix_ref, dgf_ref, dgm_ref, acc):
        i, k = pl.program_id(0), pl.program_id(1)

        @pl.when((i == 0) & (k == 0))
        def _():
            dgf_ref[...] = jnp.zeros_like(dgf_ref)
            dgm_ref[...] = jnp.zeros_like(dgm_ref)

        part = _dot(dg_ref[...], _merge_rows(wg_ref[...])) + _dot(du_ref[...], _merge_rows(wu_ref[...]))

        @pl.when(k == 0)
        def _():
            acc[...] = part

        @pl.when(k > 0)
        def _():
            acc[...] += part

        @pl.when(k == ksteps - 1)
        def _():
            x1 = x_ref[...]
            dx, dgf_rows = _norm_bwd(acc[...], x1, _rstd(x1), gf_ref[...])
            dx1 = dy_ref[...] + dx
            dx1_ref[...] = dx1
            dgf_ref[...] += jnp.sum(dgf_rows, axis=0, keepdims=True)
            mix = mix_ref[...]
            dmix, dgm_rows = _norm_bwd(dx1, mix, _rstd(mix), gm_ref[...])
            dmix_ref[...] = dmix.astype(BF16)
            dgm_ref[...] += jnp.sum(dgm_rows, axis=0, keepdims=True)

    row = pl.BlockSpec((tm, d), lambda i, k: (i, 0))
    wide = pl.BlockSpec((tm, tk), lambda i, k: (i, k))
    gain = pl.BlockSpec((1, d), lambda i, k: (0, 0))
    return pl.pallas_call(
        body, name="ffn_up_bwd", grid=(t // tm, ksteps),
        out_shape=[jax.ShapeDtypeStruct((t, d), F32), jax.ShapeDtypeStruct((t, d), BF16),
                   jax.ShapeDtypeStruct((1, d), F32), jax.ShapeDtypeStruct((1, d), F32)],
        in_specs=[wide, wide, pl.BlockSpec((ts, None, n, d), lambda i, k: (k, 0, 0, 0)),
                  pl.BlockSpec((ts, None, n, d), lambda i, k: (k, 1, 0, 0)), row, gain, row, row, gain],
        out_specs=[row, row, gain, gain],
        scratch_shapes=[pltpu.VMEM((tm, d), F32)],
        compiler_params=_params(),
    )(dgate, dup, gate_up, gate_up, x1, g_ffn, dy, mix, g_mix)


def outproj_bwd(dmix, w_out):
    t, d = dmix.shape
    tm = TOKEN_TILE

    def body(dm_ref, w_ref, out_ref):
        out_ref[...] = _dot_nt(dm_ref[...], w_ref[...])

    return pl.pallas_call(
        body, name="outproj_bwd", grid=(t // tm,),
        out_shape=jax.ShapeDtypeStruct((t, w_out.shape[0]), F32),
        in_specs=[pl.BlockSpec((tm, d), lambda i: (i, 0)), pl.BlockSpec(w_out.shape, lambda i: (0, 0))],
        out_specs=pl.BlockSpec((tm, w_out.shape[0]), lambda i: (i, 0)),
        compiler_params=_params(),
    )(dmix, w_out)


def mixers_bwd(proj, dcat, biasm, sinks, w_pool, pool_scale, ffn_parts):
    t = proj.shape[0]
    nb = t // BLOCK
    na = len(ffn_parts)

    def body(*refs):
        (uc_ref, halo_ref, q_ref, kvc_ref, kvp_ref, dcat_ref, biasm_ref, sinks_ref, wp_ref, sc_ref) = refs[:10]
        part_refs = refs[10:10 + na]
        dproj_ref, dbias_ref, dsink_ref, dwp_ref, dsc_ref = refs[10 + na:15 + na]
        slot_refs = refs[15 + na:15 + 2 * na]
        ubuf, dbuf, c_u, c_q, c_kv, send_sems, recv_sems, local_sems = refs[15 + 2 * na:]
        i = pl.program_id(0)
        lane = lax.broadcasted_iota(jnp.int32, (1, 128), 1)
        start, finish = _chip_exchange_plan(part_refs, slot_refs, send_sems, recv_sems, local_sems)

        @pl.when(i == 0)
        def _():
            start()
            dbias_ref[...] = jnp.zeros_like(dbias_ref)
            dwp_ref[...] = jnp.zeros_like(dwp_ref)
            dsc_ref[...] = jnp.zeros_like(dsc_ref)
            dsink_ref[...] = jnp.zeros_like(dsink_ref)
            dbuf[...] = jnp.zeros_like(dbuf)
            c_u[...] = jnp.zeros_like(c_u)
            c_q[...] = jnp.zeros_like(c_q)
            c_kv[...] = jnp.zeros_like(c_kv)

        @pl.when(i < nb)
        def _():
            _fill_pool_input(i, ubuf, uc_ref, halo_ref)
            for g, w in enumerate(POOL_WINDOWS):
                cols = slice(g * POOL_GROUP_DIM, (g + 1) * POOL_GROUP_DIM)
                pooled = _pooled(i, g, w, ubuf).astype(BF16)
                mixed = _dot(pooled, wp_ref[g])
                dout = dcat_ref[:, cols]
                dsc_ref[:, cols] += jnp.sum(dout * mixed, axis=0, keepdims=True)
                dmixed = (dout * sc_ref[:, cols]).astype(BF16)
                dwp_ref[g] += _dot_tn(pooled, dmixed)
                dpooled = _dot_nt(dmixed, wp_ref[g])
                scaled = dpooled * _inv_count(i, w)
                dbuf[BLOCK:, cols] = scaled[0:HALO]
                dproj_ref[:, cols] = (_window_sum(dbuf, g, w, lambda k: k) + c_u[:, cols]).astype(BF16)
                dbuf[0:BLOCK, cols] = scaled
                c_u[:, cols] = -dpooled

            kv = jnp.concatenate([kvp_ref[...], kvc_ref[...]], axis=0)
            k_var = _head_variants(kv[:, 0:2 * HEAD_DIM])
            v_var = _head_variants(kv[:, 2 * HEAD_DIM:])
            q2s = [q_ref[:, 2 * HEAD_DIM * j:2 * HEAD_DIM * (j + 1)].astype(BF16) for j in range(N_Q_HEADS // 2)]
            do2s = [dcat_ref[:, POOL_WIDTH + 2 * HEAD_DIM * j:POOL_WIDTH + 2 * HEAD_DIM * (j + 1)].astype(BF16)
                    for j in range(N_Q_HEADS // 2)]
            dq2 = [None] * (N_Q_HEADS // 2)
            dss_of, pb_of = {}, {}
            dsink_row = jnp.zeros((1, 128), F32)
            for hq in range(N_Q_HEADS):
                j, half, h = hq // 2, hq % 2, hq // GQA_GROUP
                probs, p_sink = _head_probs(i, hq, q2s[j], k_var[h][half], biasm_ref, sinks_ref)
                dp = _dot_nt(do2s[j], v_var[h][half])
                delta = jnp.sum(probs * dp, axis=-1, keepdims=True)
                ds = probs * (dp - delta)
                dbias_ref[hq] += ds
                dsink_row = dsink_row - jnp.where(lane == hq, jnp.sum(p_sink * delta), 0.0)
                dss = (ds * ATTN_SCALE).astype(BF16)
                dss_of[hq], pb_of[hq] = dss, probs.astype(BF16)
                dq = _dot(dss, k_var[h][half])
                dq2[j] = dq if dq2[j] is None else dq2[j] + dq
            dsink_ref[...] += dsink_row
            low = lax.broadcasted_iota(jnp.int32, (2 * BLOCK, 2 * HEAD_DIM), 1) < HEAD_DIM
            dk_half, dv_half = [[None, None], [None, None]], [[None, None], [None, None]]
            for h in range(N_KV_HEADS):
                for half in range(2):
                    heads = [hq for hq in range(GQA_GROUP * h, GQA_GROUP * (h + 1)) if hq % 2 == half]
                    q_rows = jnp.concatenate([q2s[hq // 2] for hq in heads], axis=0)
                    do_rows = jnp.concatenate([do2s[hq // 2] for hq in heads], axis=0)
                    dk_half[h][half] = _dot_tn(jnp.concatenate([dss_of[hq] for hq in heads], axis=0), q_rows)
                    dv_half[h][half] = _dot_tn(jnp.concatenate([pb_of[hq] for hq in heads], axis=0), do_rows)

            def pair_of(halves):
                return jnp.where(low, halves[0][0] + pltpu.roll(halves[0][1], HEAD_DIM, 1),
                                 halves[1][1] + pltpu.roll(halves[1][0], HEAD_DIM, 1))

            dkv = jnp.concatenate([pair_of(dk_half), pair_of(dv_half)], axis=1)
            dproj_ref[:, POOL_WIDTH:2 * POOL_WIDTH] = c_q[...].astype(BF16)
            dproj_ref[:, 2 * POOL_WIDTH:] = (c_kv[...] + dkv[0:BLOCK]).astype(BF16)
            c_q[...] = jnp.concatenate(dq2, axis=1)
            c_kv[...] = dkv[BLOCK:]

        @pl.when(i == nb)
        def _():
            dbuf[BLOCK:, :] = jnp.zeros((HALO, POOL_WIDTH), F32)
            for g, w in enumerate(POOL_WINDOWS):
                cols = slice(g * POOL_GROUP_DIM, (g + 1) * POOL_GROUP_DIM)
                dproj_ref[:, cols] = (_window_sum(dbuf, g, w, lambda k: k) + c_u[:, cols]).astype(BF16)
            dproj_ref[:, POOL_WIDTH:2 * POOL_WIDTH] = c_q[...].astype(BF16)
            dproj_ref[:, 2 * POOL_WIDTH:] = c_kv[...].astype(BF16)
            finish()

    cur = lambda i: jnp.minimum(i, nb - 1)
    prv = lambda i: jnp.maximum(jnp.minimum(i, nb - 1) - 1, 0)
    out = pl.pallas_call(
        body, name="mixers_bwd", grid=(nb + 1,),
        out_shape=[jax.ShapeDtypeStruct((t, proj.shape[1]), BF16),
                   jax.ShapeDtypeStruct((N_Q_HEADS, BLOCK, 2 * BLOCK), F32),
                   jax.ShapeDtypeStruct((1, 128), F32),
                   jax.ShapeDtypeStruct((4, POOL_GROUP_DIM, POOL_GROUP_DIM), F32),
                   jax.ShapeDtypeStruct((1, POOL_WIDTH), F32)]
        + [jax.ShapeDtypeStruct(p.shape, p.dtype) for p in ffn_parts],
        in_specs=_mixer_in_specs(cur, prv) + [pl.BlockSpec((BLOCK, 2 * POOL_WIDTH), lambda i: (cur(i), 0))]
        + _mixer_param_specs() + [ANY] * na,
        out_specs=[pl.BlockSpec((BLOCK, proj.shape[1]), lambda i: (jnp.maximum(i - 1, 0), 0)),
                   pl.BlockSpec((N_Q_HEADS, BLOCK, 2 * BLOCK), lambda i: (0, 0, 0)),
                   pl.BlockSpec((1, 128), lambda i: (0, 0)),
                   pl.BlockSpec((4, POOL_GROUP_DIM, POOL_GROUP_DIM), lambda i: (0, 0, 0)),
                   pl.BlockSpec((1, POOL_WIDTH), lambda i: (0, 0))] + [ANY] * na,
        scratch_shapes=[pltpu.VMEM((HALO + BLOCK, POOL_WIDTH), F32), pltpu.VMEM((BLOCK + HALO, POOL_WIDTH), F32),
                        pltpu.VMEM((BLOCK, POOL_WIDTH), F32), pltpu.VMEM((BLOCK, POOL_WIDTH), F32),
                        pltpu.VMEM((BLOCK, 256), F32)] + _chip_exchange_scratch(na),
        compiler_params=_params(),
    )(proj, proj, proj, proj, proj, dcat, biasm, sinks, w_pool, pool_scale, *ffn_parts)
    return out[:5], out[5:]


def inproj_bwd(dproj, w_in_t, x, g, dx1):
    t, d = x.shape
    n = dproj.shape[1]
    tm = TOKEN_TILE

    def body(dp_ref, w_ref, x_ref, g_ref, dx1_ref, dx_ref, dg_ref):
        @pl.when(pl.program_id(0) == 0)
        def _():
            dg_ref[...] = jnp.zeros_like(dg_ref)

        dh = _dot(dp_ref[...], w_ref[...])
        xv = x_ref[...]
        dx, dg_rows = _norm_bwd(dh, xv, _rstd(xv), g_ref[...])
        dx_ref[...] = dx1_ref[...] + dx
        dg_ref[...] += jnp.sum(dg_rows, axis=0, keepdims=True)

    row = pl.BlockSpec((tm, d), lambda i: (i, 0))
    gain = pl.BlockSpec((1, d), lambda i: (0, 0))
    return pl.pallas_call(
        body, name="inproj_bwd", grid=(t // tm,),
        out_shape=[jax.ShapeDtypeStruct((t, d), F32), jax.ShapeDtypeStruct((1, d), F32)],
        in_specs=[pl.BlockSpec((tm, n), lambda i: (i, 0)), pl.BlockSpec(w_in_t.shape, lambda i: (0, 0)), row, gain, row],
        out_specs=[row, gain],
        compiler_params=_params(),
    )(dproj, w_in_t, x, g, dx1)


def _bucket_band():
    qi = jnp.arange(BLOCK)[:, None]
    kj = jnp.arange(2 * BLOCK)[None, :]
    dist = qi + BLOCK - kj
    n = jnp.maximum(dist, 0)
    nf = jnp.maximum(n, 1).astype(F32)
    large = MAX_EXACT + (jnp.log(nf / MAX_EXACT) / np.float32(np.log(MAX_DISTANCE / MAX_EXACT))
                         * (N_BUCKETS - MAX_EXACT)).astype(jnp.int32)
    large = jnp.minimum(large, N_BUCKETS - 1)
    bucket = jnp.where(n < MAX_EXACT, n, large)
    in_window = (dist >= 0) & (dist < BLOCK)
    return bucket.astype(F32), in_window.astype(F32)


def _pad_row(v):
    flat = v.reshape(-1)
    pad = (-flat.shape[0]) % 128
    return jnp.pad(flat, (0, pad)).reshape(-1, 128)


def kernel(x, g_pre_mix, w_in, w_pool, pool_scale, rel_bias, sinks, w_out, g_post_mix, g_pre_ffn, w_gate, w_up, w_down, g_post_ffn, loss_target, m_g_pre_mix, m_w_in, m_w_pool, m_pool_scale, m_rel_bias, m_sinks, m_w_out, m_g_post_mix, m_g_pre_ffn, m_w_gate, m_w_up, m_w_down, m_g_post_ffn, v_g_pre_mix, v_w_in, v_w_pool, v_pool_scale, v_rel_bias, v_sinks, v_w_out, v_g_post_mix, v_g_pre_ffn, v_w_gate, v_w_up, v_w_down, v_g_post_ffn):
    d = x.shape[-1]
    xs, target = x[0], loss_target[0]

    w_in_ts = w_in[0].T.astype(BF16)
    w_out_s = w_out[0].astype(BF16)
    gate_up_s = jnp.stack([w_gate[0].T, w_up[0].T]).astype(BF16)
    w_down_s = w_down[0].astype(BF16)
    w_in_t, w_out_f = gather_blocks([w_in_ts, w_out_s], "gather_mix_weights")
    w_in_t = w_in_t.reshape(-1, d)
    w_out_f = w_out_f.reshape(-1, d)

    bucket, in_window = _bucket_band()
    biasm = bias_band(bucket, in_window, rel_bias)
    w_pool_b = w_pool[0].astype(BF16)
    proj, h1 = norm_inproj(xs, g_pre_mix, w_in_t)
    cat, gate_up = mixers_fwd(proj, biasm, sinks, w_pool_b, pool_scale, gate_up_s)
    mix, x1 = outproj_norm(cat, w_out_f, xs, g_post_mix)
    h2, gate, up, act, w_down_f = ffn_up(x1, g_pre_ffn, gate_up, w_down_s)
    df, dy, dg_post_ffn, loss_part = ffn_down_loss(act, w_down_f, x1, g_post_ffn, target)

    dgate, dup = ffn_down_bwd(df, w_down_f, gate, up)
    d_gate_up = grad_ffn([dgate, dup], h2, "grad_w_gate_up")
    d_down = grad_ffn([act], df, "grad_w_down")
    chip_parts = pair_add([d_gate_up, d_down], pair_exchange([d_gate_up, d_down], "pair_exchange_ffn"))
    dx1, dmix, dg_pre_ffn, dg_post_mix = ffn_up_bwd(dgate, dup, gate_up, x1, g_pre_ffn, dy, mix, g_post_mix)

    dcat = outproj_bwd(dmix, w_out_f)
    d_out = grad_rows(cat, dmix, "grad_w_out")
    (dproj, dbias, dsinks, dw_pool, dpool_scale), ffn_slots = mixers_bwd(
        proj, dcat, biasm, sinks, w_pool_b, pool_scale, chip_parts)
    drel_bias = bias_band_bwd(bucket, dbias)
    grad_x, dg_pre_mix = inproj_bwd(dproj, w_in_t, xs, g_pre_mix, dx1)
    d_in_t = grad_rows(dproj, h1, "grad_w_in")

    g_gate_up_t, g_down = sum_slots(ffn_slots, "sum_ffn")
    mix_slots = exchange_partials([d_in_t, d_out], "exchange_mix_grads")
    g_in_t, g_out = sum_slots(mix_slots, "sum_mix")
    big_w = [w_in[0], w_out[0], w_gate[0], w_up[0], w_down[0]]
    big_g = [g_in_t.T, g_out, g_gate_up_t[0].T, g_gate_up_t[1].T, g_down[0]]
    big_m = [m_w_in[0], m_w_out[0], m_w_gate[0], m_w_up[0], m_w_down[0]]
    big_v = [v_w_in[0], v_w_out[0], v_w_gate[0], v_w_up[0], v_w_down[0]]
    upd = adamw_update(big_w[:2], big_g[:2], big_m[:2], big_v[:2], "adamw_mix") \
        + adamw_update(big_w[2:], big_g[2:], big_m[2:], big_v[2:], "adamw_ffn")
    big_un = [[big_g[k][None] for k in range(5)]] + [[upd[k][q][None] for k in range(5)] for q in range(3)]

    small_w = [g_pre_mix, pool_scale, rel_bias, sinks, g_post_mix, g_pre_ffn, g_post_ffn, w_pool]
    small_m = [m_g_pre_mix, m_pool_scale, m_rel_bias, m_sinks, m_g_post_mix, m_g_pre_ffn, m_g_post_ffn, m_w_pool]
    small_v = [v_g_pre_mix, v_pool_scale, v_rel_bias, v_sinks, v_g_post_mix, v_g_pre_ffn, v_g_post_ffn, v_w_pool]
    small_g = [dg_pre_mix, dpool_scale, drel_bias, dsinks[:, :N_Q_HEADS], dg_post_mix, dg_pre_ffn, dg_post_ffn, dw_pool]
    zero = jnp.zeros((1,), F32)
    pack = lambda parts, last: jnp.concatenate([_pad_row(p) for p in parts] + [_pad_row(last)], axis=0)
    small_out = small_allreduce_adamw(pack(small_g, loss_part[0, :1]), pack(small_w, zero), pack(small_m, zero),
                                      pack(small_v, zero))
    small_rows = [_pad_row(p).shape[0] for p in small_w]
    s_starts = np.concatenate([[0], np.cumsum(small_rows)])

    def small_un(o):
        return [o[s_starts[k]:s_starts[k + 1]].reshape(-1)[:small_w[k].size].reshape(small_w[k].shape) for k in range(8)]

    loss = small_out[0][s_starts[8], 0]
    sm = [small_un(o) for o in small_out]

    def ordered(kind):
        s, b = sm[kind], big_un[kind]
        return [s[0], b[0], s[7], s[1], s[2], s[3], b[1], s[4], s[5], b[2], b[3], b[4], s[6]]

    return (loss, grad_x[None], *ordered(0), *ordered(1), *ordered(2), *ordered(3))
```

```python
import numpy as np
import jax
import jax.numpy as jnp
from jax import lax
from jax.experimental import pallas as pl
from jax.experimental.pallas import tpu as pltpu

F32 = jnp.float32
BF16 = jnp.bfloat16

N_DEV = 8
N_CHIP = 4
POOL_WIDTH = 512
POOL_WINDOWS = (2, 4, 8, 16)
POOL_GROUP_DIM = 128
HEAD_DIM = 64
N_Q_HEADS = 8
N_KV_HEADS = 2
GQA_GROUP = 4
BLOCK = 128
HALO = 16
N_BUCKETS = 32
MAX_EXACT = 16
MAX_DISTANCE = 128
EPS = 1e-6
NEG_INF = -1e30
ATTN_SCALE = float(1.0 / np.sqrt(np.float32(HEAD_DIM)))

ADAM_LR = 0.001
ADAM_B1 = 0.9
ADAM_B2 = 0.999
ADAM_EPS = 1e-08
ADAM_WD = 0.01
ADAM_STEP = 10

TOKEN_TILE = 512
FF_SHARDS_PER_TILE = 4
VMEM_LIMIT = 56 * 1024 * 1024
MESH = pl.DeviceIdType.MESH
ANY = pl.BlockSpec(memory_space=pl.ANY)
VMEM = pl.BlockSpec(memory_space=pltpu.VMEM)
SMEM = pl.BlockSpec(memory_space=pltpu.SMEM)


def _params(**kw):
    return pltpu.CompilerParams(vmem_limit_bytes=VMEM_LIMIT, **kw)


def _dot(a, b):
    return jnp.dot(a, b, preferred_element_type=F32)


def _dot_nt(a, b):
    return lax.dot_general(a, b, (((1,), (1,)), ((), ())), preferred_element_type=F32)


def _dot_tn(a, b):
    return lax.dot_general(a, b, (((0,), (0,)), ((), ())), preferred_element_type=F32)


def _rstd(v):
    return lax.rsqrt(jnp.mean(v * v, axis=-1, keepdims=True) + EPS)


def _norm_bwd(dout, v, r, g):
    vn = v * r
    dn = dout * g
    dv = r * (dn - vn * jnp.mean(dn * vn, axis=-1, keepdims=True))
    return dv, dout * vn


def _as_rows(v):
    return jnp.concatenate([v[:, k:k + 128] for k in range(0, v.shape[1], 128)], axis=0)


def _as_lanes(rows):
    return jnp.concatenate([rows[k:k + 1, :] for k in range(rows.shape[0])], axis=1)


def _merge_rows(value):
    s, r, c_ = value.shape
    return value.reshape(s * r, c_)


def _gather_plan(srcs, outs, send_sems, recv_sems, local_sems=None, bounce=None):
    n = len(srcs)
    x, y, c = lax.axis_index("x"), lax.axis_index("y"), lax.axis_index("c")
    me, sibling = (x, y, c), (x, y, 1 - c)
    chips = [(1 - x, y), (x, 1 - y), (1 - x, 1 - y)]

    def slot(a, px, py, pc):
        return outs[a].at[4 * px + 2 * py + pc]

    def copy(a, k, block, to, from_src=False):
        return pltpu.make_async_remote_copy(
            src_ref=srcs[a] if from_src else slot(a, *block), dst_ref=slot(a, *block),
            send_sem=send_sems.at[k * n + a], recv_sem=recv_sems.at[k * n + a], device_id=to, device_id_type=MESH)

    def own_in(a):
        return pltpu.make_async_copy(srcs[a], bounce[a], local_sems.at[a])

    def own_out(a):
        return pltpu.make_async_copy(bounce[a], slot(a, *me), local_sems.at[a])

    def first(a):
        return [copy(a, 0, me, sibling, True)] + [copy(a, 1 + j, me, (*chip, c), True) for j, chip in enumerate(chips)]

    def passed(a, j):
        return copy(a, 4 + j, (*chips[j], c), sibling)

    def start():
        for a in range(n):
            if bounce is not None:
                own_in(a).start()
            for cp in first(a):
                cp.start()

    def finish():
        if bounce is not None:
            for a in range(n):
                own_in(a).wait()
                own_out(a).start()
        for j, chip in enumerate(chips):
            for a in range(n):
                copy(a, 1 + j, (*chip, c), me).wait_recv()
                passed(a, j).start()
        for a in range(n):
            copy(a, 0, sibling, me).wait_recv()
            for j, chip in enumerate(chips):
                copy(a, 4 + j, (*chip, 1 - c), me).wait_recv()
        for a in range(n):
            for cp in first(a) + [passed(a, j) for j in range(3)]:
                cp.wait_send()
            if bounce is not None:
                own_out(a).wait()

    return start, finish


def _gather_scratch(shards):
    n = len(shards)
    return [pltpu.SemaphoreType.DMA((7 * n,)), pltpu.SemaphoreType.DMA((7 * n,)), pltpu.SemaphoreType.DMA((n,))] \
        + [pltpu.VMEM(s.shape, s.dtype) for s in shards]


def _chip_exchange_plan(srcs, outs, send_sems, recv_sems, local_sems, bounce):
    n = len(srcs)
    x, y, c = lax.axis_index("x"), lax.axis_index("y"), lax.axis_index("c")
    my_chip = 2 * x + y

    def copies():
        out = []
        for a in range(n):
            for k in range(1, N_CHIP):
                px, py = x ^ (k >> 1), y ^ (k & 1)
                out.append(pltpu.make_async_remote_copy(
                    src_ref=srcs[a].at[2 * px + py], dst_ref=outs[a].at[my_chip],
                    send_sem=send_sems.at[(k - 1) * n + a], recv_sem=recv_sems.at[(k - 1) * n + a],
                    device_id=(px, py, c), device_id_type=MESH))
        return out

    def own_in(a):
        return pltpu.make_async_copy(srcs[a].at[my_chip], bounce[a], local_sems.at[a])

    def own_out(a):
        return pltpu.make_async_copy(bounce[a], outs[a].at[my_chip], local_sems.at[a])

    def start():
        for a in range(n):
            own_in(a).start()
        for cp in copies():
            cp.start()

    def finish():
        for a in range(n):
            own_in(a).wait()
            own_out(a).start()
        for cp in copies():
            cp.wait()
        for a in range(n):
            own_out(a).wait()

    return start, finish


def _chip_exchange_scratch(parts):
    n = len(parts)
    return [pltpu.SemaphoreType.DMA((3 * n,)), pltpu.SemaphoreType.DMA((3 * n,)), pltpu.SemaphoreType.DMA((n,))] \
        + [pltpu.VMEM(p.shape[1:], p.dtype) for p in parts]


def gather_blocks(shards, name):
    def body(*refs):
        n = len(shards)
        start, finish = _gather_plan(refs[:n], refs[n:2 * n], *refs[2 * n:2 * n + 3], bounce=refs[2 * n + 3:])
        start()
        finish()

    return pl.pallas_call(
        body, name=name,
        out_shape=[jax.ShapeDtypeStruct((N_DEV, *s.shape), s.dtype) for s in shards],
        in_specs=[ANY] * len(shards), out_specs=[ANY] * len(shards),
        scratch_shapes=_gather_scratch(shards),
    )(*shards)


def pair_exchange(parts, name):
    n = len(parts)

    def body(*refs):
        p_refs, got_refs, send_sems, recv_sems = refs[:n], refs[n:2 * n], *refs[2 * n:]
        x, y, c = lax.axis_index("x"), lax.axis_index("y"), lax.axis_index("c")
        copies = [pltpu.make_async_remote_copy(
            src_ref=p_refs[a].at[1 - c], dst_ref=got_refs[a], send_sem=send_sems.at[a], recv_sem=recv_sems.at[a],
            device_id=(x, y, 1 - c), device_id_type=MESH) for a in range(n)]
        for cp in copies:
            cp.start()
        for cp in copies:
            cp.wait()

    return pl.pallas_call(
        body, name=name, out_shape=[jax.ShapeDtypeStruct(p.shape[1:], p.dtype) for p in parts],
        in_specs=[ANY] * n, out_specs=[ANY] * n,
        scratch_shapes=[pltpu.SemaphoreType.DMA((n,)), pltpu.SemaphoreType.DMA((n,))],
    )(*parts)


def pair_add(parts, got, name):
    n = len(parts)

    def body(core_ref, *refs):
        for a in range(n):
            refs[2 * n + a][...] = (refs[a][...].astype(F32) + refs[n + a][...].astype(F32)).astype(BF16)

    def own(p):
        zeros = (0,) * (p.ndim - 2)
        return pl.BlockSpec((None, 1, *p.shape[2:]), lambda i, core: (core[0], i, *zeros))

    def plain(p):
        zeros = (0,) * (p.ndim - 1)
        return pl.BlockSpec((1, *p.shape[1:]), lambda i, core: (i, *zeros))

    core = lax.axis_index("c").astype(jnp.int32).reshape(1)
    return pl.pallas_call(
        body, name=name,
        grid_spec=pltpu.PrefetchScalarGridSpec(
            num_scalar_prefetch=1, grid=(got[0].shape[0],),
            in_specs=[own(p) for p in parts] + [plain(p) for p in got], out_specs=[plain(p) for p in got]),
        out_shape=[jax.ShapeDtypeStruct(p.shape, BF16) for p in got],
        compiler_params=_params(),
    )(core, *parts, *got)


def sum_slots(slots, name):
    n = len(slots)

    def body(*refs):
        for a in range(n):
            total = refs[a][0].astype(F32)
            for s in range(1, slots[a].shape[0]):
                total = total + refs[a][s].astype(F32)
            refs[n + a][...] = total

    return pl.pallas_call(
        body, name=name,
        out_shape=[jax.ShapeDtypeStruct(p.shape[1:], F32) for p in slots],
        in_specs=[VMEM] * n, out_specs=[VMEM] * n,
        compiler_params=_params(),
    )(*slots)


def _adamw(w, g, m, v):
    m2 = ADAM_B1 * m + (1.0 - ADAM_B1) * g
    v2 = ADAM_B2 * v + (1.0 - ADAM_B2) * (g * g)
    m_hat = m2 / (1.0 - ADAM_B1 ** ADAM_STEP)
    v_hat = v2 / (1.0 - ADAM_B2 ** ADAM_STEP)
    delta = -ADAM_LR * (m_hat / (jnp.sqrt(v_hat) + ADAM_EPS) + ADAM_WD * w)
    return delta, m2, v2


def adamw_update(ws, gs, ms, vs, name):
    n = len(ws)

    def body(*refs):
        for a in range(n):
            delta, m2, v2 = _adamw(refs[a][...], refs[n + a][...], refs[2 * n + a][...], refs[3 * n + a][...])
            refs[4 * n + 3 * a][...] = delta
            refs[4 * n + 3 * a + 1][...] = m2
            refs[4 * n + 3 * a + 2][...] = v2

    out = pl.pallas_call(
        body, name=name,
        out_shape=[jax.ShapeDtypeStruct(w.shape, F32) for w in ws for _ in range(3)],
        in_specs=[VMEM] * (4 * n), out_specs=[VMEM] * (3 * n),
        compiler_params=_params(),
    )(*ws, *gs, *ms, *vs)
    return [out[3 * a:3 * a + 3] for a in range(n)]


GAIN_ROWS = 8
ROW_POOL_SCALE = 4 * GAIN_ROWS
ROW_SINKS = ROW_POOL_SCALE + 4
ROW_LOSS = ROW_SINKS + 1
ROW_W_POOL = 40
SMALL_ROWS = ROW_W_POOL + 4 * POOL_GROUP_DIM


def tail_reduce(d_in_t, gains, dpool_scale, dsinks, loss_part, dw_pool, drel_bias, small_w, small_m, small_v):
    n_small = len(small_w)

    def body(*refs):
        d_in_ref, g_refs, (dsc_ref, dsink_ref, loss_ref, dwp_ref, drb_ref) = refs[0], refs[1:5], refs[5:10]
        w_refs, m_refs, v_refs = (refs[10 + k * n_small:10 + (k + 1) * n_small] for k in range(3))
        outs = refs[10 + 3 * n_small:]
        slots_ref, loss_out = outs[0], outs[1]
        result = outs[2:2 + 4 * n_small]
        stage, gat, gat_rb, g_send, g_recv, x_send, x_recv, own_sem, bounce = outs[2 + 4 * n_small:]
        x, y, c = lax.axis_index("x"), lax.axis_index("y"), lax.axis_index("c")
        my_id = 4 * x + 2 * y + c

        for k in range(4):
            stage[GAIN_ROWS * k:GAIN_ROWS * (k + 1), :] = g_refs[k][...]
        stage[ROW_POOL_SCALE:ROW_SINKS, :] = dsc_ref[...]
        stage[ROW_SINKS:ROW_LOSS, :] = dsink_ref[...]
        stage[ROW_LOSS:ROW_LOSS + 1, :] = loss_ref[...]
        stage[ROW_LOSS + 1:ROW_W_POOL, :] = jnp.zeros((ROW_W_POOL - ROW_LOSS - 1, 128), F32)
        stage[ROW_W_POOL:, :] = dwp_ref[...].reshape(4 * POOL_GROUP_DIM, POOL_GROUP_DIM)
        gat[my_id] = stage[...]
        gat_rb[my_id] = drb_ref[...]
        start, finish = _gather_plan([stage, drb_ref], [gat, gat_rb], g_send, g_recv)
        start()

        own_in = pltpu.make_async_copy(d_in_ref.at[my_id], bounce, own_sem)
        own_in.start()
        copies = []
        for k in range(1, N_DEV):
            px, py, pc = x ^ (k >> 2), y ^ ((k >> 1) & 1), c ^ (k & 1)
            copies.append(pltpu.make_async_remote_copy(
                src_ref=d_in_ref.at[4 * px + 2 * py + pc], dst_ref=slots_ref.at[my_id],
                send_sem=x_send.at[k - 1], recv_sem=x_recv.at[k - 1], device_id=(px, py, pc), device_id_type=MESH))
        for cp in copies:
            cp.start()

        finish()
        total, total_rb = gat[0], gat_rb[0]
        for s in range(1, N_DEV):
            total, total_rb = total + gat[s], total_rb + gat_rb[s]
        loss_out[...] = total[ROW_LOSS:ROW_LOSS + 1, :]
        grads = [_as_lanes(total[GAIN_ROWS * k:GAIN_ROWS * (k + 1), :]) for k in range(4)]
        grads.append(_as_lanes(total[ROW_POOL_SCALE:ROW_SINKS, :]))
        grads.append(total[ROW_SINKS:ROW_LOSS, 0:N_Q_HEADS])
        grads.append(total[ROW_W_POOL:, :].reshape(w_refs[6].shape))
        grads.append(total_rb)
        for k in range(n_small):
            delta, m2, v2 = _adamw(w_refs[k][...], grads[k], m_refs[k][...], v_refs[k][...])
            result[4 * k][...] = grads[k]
            result[4 * k + 1][...] = delta
            result[4 * k + 2][...] = m2
            result[4 * k + 3][...] = v2

        own_in.wait()
        own_out = pltpu.make_async_copy(bounce, slots_ref.at[my_id], own_sem)
        own_out.start()
        for cp in copies:
            cp.wait()
        own_out.wait()

    n_in = 10 + 3 * n_small
    out = pl.pallas_call(
        body, name="tail_reduce",
        out_shape=[jax.ShapeDtypeStruct(d_in_t.shape, d_in_t.dtype), jax.ShapeDtypeStruct((1, 128), F32)]
        + [jax.ShapeDtypeStruct(w.shape, F32) for w in small_w for _ in range(4)],
        in_specs=[ANY] + [VMEM] * (n_in - 1), out_specs=[ANY] + [VMEM] * (1 + 4 * n_small),
        scratch_shapes=[pltpu.VMEM((SMALL_ROWS, 128), F32), pltpu.VMEM((N_DEV, SMALL_ROWS, 128), F32),
                        pltpu.VMEM((N_DEV, *drel_bias.shape), F32),
                        pltpu.SemaphoreType.DMA((14,)), pltpu.SemaphoreType.DMA((14,)),
                        pltpu.SemaphoreType.DMA((7,)), pltpu.SemaphoreType.DMA((7,)), pltpu.SemaphoreType.DMA,
                        pltpu.VMEM(d_in_t.shape[1:], d_in_t.dtype)],
        compiler_params=_params(),
    )(d_in_t, *gains, dpool_scale, dsinks, loss_part, dw_pool, drel_bias, *small_w, *small_m, *small_v)
    return out[0], out[1], [out[2 + 4 * k:6 + 4 * k] for k in range(n_small)]


def norm_inproj(x, g, w_t):
    t, d = x.shape
    n = w_t.shape[0]
    tm = TOKEN_TILE

    def body(x_ref, g_ref, w_ref, proj_ref, h_ref):
        xv = x_ref[...]
        h = ((xv * _rstd(xv)) * g_ref[...]).astype(BF16)
        h_ref[...] = h
        proj_ref[...] = _dot_nt(h, w_ref[...])

    return pl.pallas_call(
        body, name="norm_inproj", grid=(t // tm,),
        out_shape=[jax.ShapeDtypeStruct((t, n), F32), jax.ShapeDtypeStruct((t, d), BF16)],
        in_specs=[pl.BlockSpec((tm, d), lambda i: (i, 0)), pl.BlockSpec((1, d), lambda i: (0, 0)),
                  pl.BlockSpec((n, d), lambda i: (0, 0))],
        out_specs=[pl.BlockSpec((tm, n), lambda i: (i, 0)), pl.BlockSpec((tm, d), lambda i: (i, 0))],
        compiler_params=_params(),
    )(x, g, w_t)


def bias_band(bucket, in_window, rel_bias):
    def body(bk_ref, win_ref, rb_ref, out_ref):
        bk = bk_ref[...]
        keep = win_ref[...] > 0.5
        for h in range(N_Q_HEADS):
            acc = jnp.zeros(bk.shape, F32)
            for b in range(N_BUCKETS):
                acc = jnp.where(bk == float(b), rb_ref[b, h], acc)
            out_ref[h] = jnp.where(keep, acc, NEG_INF)

    return pl.pallas_call(
        body, name="bias_band",
        out_shape=jax.ShapeDtypeStruct((N_Q_HEADS, BLOCK, 2 * BLOCK), F32),
        in_specs=[VMEM, VMEM, SMEM], out_specs=VMEM,
    )(bucket, in_window, rel_bias)


def bias_band_bwd(bucket, dbias):
    def body(bk_ref, db_ref, out_ref):
        bk = bk_ref[...]
        for h in range(N_Q_HEADS):
            db = db_ref[h]
            for b in range(N_BUCKETS):
                out_ref[b, h] = jnp.sum(jnp.where(bk == float(b), db, 0.0))

    return pl.pallas_call(
        body, name="bias_band_bwd",
        out_shape=jax.ShapeDtypeStruct((N_BUCKETS, N_Q_HEADS), F32),
        in_specs=[VMEM, VMEM], out_specs=SMEM,
    )(bucket, dbias)


def _window_sum(buf_ref, g, w, first):
    cols = slice(g * POOL_GROUP_DIM, (g + 1) * POOL_GROUP_DIM)
    acc = None
    for k in range(w):
        piece = buf_ref[first(k):first(k) + BLOCK, cols]
        acc = piece if acc is None else acc + piece
    return acc


def _inv_count(i, w):
    row = lax.broadcasted_iota(jnp.int32, (BLOCK, 1), 0)
    return 1.0 / jnp.minimum(i * BLOCK + row + 1, w).astype(F32)


def _fill_pool_input(i, ubuf, uc_ref, halo_ref):
    ubuf[0:HALO, :] = jnp.where(i > 0, halo_ref[...], 0.0)
    ubuf[HALO:, :] = uc_ref[...]


def _pooled(i, g, w, ubuf):
    cols = slice(g * POOL_GROUP_DIM, (g + 1) * POOL_GROUP_DIM)
    return _window_sum(ubuf, g, w, lambda k: HALO - k) * _inv_count(i, w) - ubuf[HALO:, cols]


def _head_variants(pair):
    low = lax.broadcasted_iota(jnp.int32, pair.shape, 1) < HEAD_DIM
    swapped = pltpu.roll(pair, HEAD_DIM, 1)
    zero = jnp.zeros_like(pair)
    pick = lambda c, a, b: jnp.where(c, a, b).astype(BF16)
    return [[pick(low, pair, zero), pick(low, zero, swapped)], [pick(low, swapped, zero), pick(low, zero, pair)]]


def _head_probs(i, hq, q2, k_var, biasm_ref, sinks_ref):
    s = _dot_nt(q2, k_var) * ATTN_SCALE + biasm_ref[hq]
    col = lax.broadcasted_iota(jnp.int32, s.shape, 1)
    s = jnp.where((i == 0) & (col < BLOCK), NEG_INF, s)
    sink = sinks_ref[0, hq]
    m = jnp.maximum(jnp.max(s, axis=-1, keepdims=True), sink)
    p = jnp.exp(s - m)
    e_sink = jnp.exp(sink - m)
    inv = 1.0 / (jnp.sum(p, axis=-1, keepdims=True) + e_sink)
    return p * inv, e_sink * inv


def _mixer_in_specs(cur, prv):
    return [pl.BlockSpec((BLOCK, 512), lambda i: (cur(i), 0)),
            pl.BlockSpec((HALO, 512), lambda i: (jnp.maximum(cur(i) * (BLOCK // HALO) - 1, 0), 0)),
            pl.BlockSpec((BLOCK, 512), lambda i: (cur(i), 1)),
            pl.BlockSpec((BLOCK, 256), lambda i: (cur(i), 4)),
            pl.BlockSpec((BLOCK, 256), lambda i: (prv(i), 4))]


def _mixer_param_specs():
    return [pl.BlockSpec((N_Q_HEADS, BLOCK, 2 * BLOCK), lambda i: (0, 0, 0)), SMEM,
            pl.BlockSpec((4, POOL_GROUP_DIM, POOL_GROUP_DIM), lambda i: (0, 0, 0)),
            pl.BlockSpec((1, POOL_WIDTH), lambda i: (0, 0))]


def mixers_fwd(proj, biasm, sinks, w_pool, pool_scale, gate_up_shard):
    t = proj.shape[0]
    nb = t // BLOCK

    def body(uc_ref, halo_ref, q_ref, kvc_ref, kvp_ref, biasm_ref, sinks_ref, wp_ref, sc_ref, shard_ref,
             out_ref, gathered_ref, ubuf, send_sems, recv_sems, local_sems, bounce):
        i = pl.program_id(0)
        start, finish = _gather_plan([shard_ref], [gathered_ref], send_sems, recv_sems, local_sems, [bounce])
        pl.when(i == 0)(start)

        _fill_pool_input(i, ubuf, uc_ref, halo_ref)
        for g, w in enumerate(POOL_WINDOWS):
            mixed = _dot(_pooled(i, g, w, ubuf).astype(BF16), wp_ref[g])
            cols = slice(g * POOL_GROUP_DIM, (g + 1) * POOL_GROUP_DIM)
            out_ref[:, cols] = (mixed * sc_ref[:, cols]).astype(BF16)
        kv = jnp.concatenate([kvp_ref[...], kvc_ref[...]], axis=0)
        k_var = _head_variants(kv[:, 0:2 * HEAD_DIM])
        v_var = _head_variants(kv[:, 2 * HEAD_DIM:])
        for j in range(N_Q_HEADS // 2):
            q2 = q_ref[:, 2 * HEAD_DIM * j:2 * HEAD_DIM * (j + 1)].astype(BF16)
            acc = None
            for half in range(2):
                hq = 2 * j + half
                h = hq // GQA_GROUP
                probs, _ = _head_probs(i, hq, q2, k_var[h][half], biasm_ref, sinks_ref)
                o = _dot(probs.astype(BF16), v_var[h][half])
                acc = o if acc is None else acc + o
            out_ref[:, POOL_WIDTH + 2 * HEAD_DIM * j:POOL_WIDTH + 2 * HEAD_DIM * (j + 1)] = acc.astype(BF16)

        pl.when(i == nb - 1)(finish)

    return pl.pallas_call(
        body, name="mixers_fwd", grid=(nb,),
        out_shape=[jax.ShapeDtypeStruct((t, 2 * POOL_WIDTH), BF16),
                   jax.ShapeDtypeStruct((N_DEV, *gate_up_shard.shape), gate_up_shard.dtype)],
        in_specs=_mixer_in_specs(lambda i: i, lambda i: jnp.maximum(i - 1, 0)) + _mixer_param_specs() + [ANY],
        out_specs=[pl.BlockSpec((BLOCK, 2 * POOL_WIDTH), lambda i: (i, 0)), ANY],
        scratch_shapes=[pltpu.VMEM((HALO + BLOCK, POOL_WIDTH), F32)] + _gather_scratch([gate_up_shard]),
        compiler_params=_params(),
    )(proj, proj, proj, proj, proj, biasm, sinks, w_pool, pool_scale, gate_up_shard)


def outproj_norm(cat, w, x, g):
    t, d = x.shape
    tm = TOKEN_TILE

    def body(c_ref, w_ref, x_ref, g_ref, mix_ref, x1_ref):
        mix = _dot(c_ref[...], w_ref[...])
        mix_ref[...] = mix
        x1_ref[...] = x_ref[...] + (mix * _rstd(mix)) * g_ref[...]

    row = pl.BlockSpec((tm, d), lambda i: (i, 0))
    return pl.pallas_call(
        body, name="outproj_norm", grid=(t // tm,),
        out_shape=[jax.ShapeDtypeStruct((t, d), F32)] * 2,
        in_specs=[pl.BlockSpec((tm, cat.shape[1]), lambda i: (i, 0)), pl.BlockSpec(w.shape, lambda i: (0, 0)), row,
                  pl.BlockSpec((1, d), lambda i: (0, 0))],
        out_specs=[row, row],
        compiler_params=_params(),
    )(cat, w, x, g)


def ffn_up(x1, g, gate_up, down_shard):
    t, d = x1.shape
    n = gate_up.shape[2]
    f = N_DEV * n
    tm, ts = TOKEN_TILE, FF_SHARDS_PER_TILE
    tn = ts * n
    steps = (t // tm, f // tn)

    def body(x_ref, g_ref, wg_ref, wu_ref, shard_ref, h_ref, gate_ref, up_ref, a_ref, gathered_ref,
             send_sems, recv_sems, local_sems, bounce):
        i, j = pl.program_id(0), pl.program_id(1)
        start, finish = _gather_plan([shard_ref], [gathered_ref], send_sems, recv_sems, local_sems, [bounce])
        pl.when((i == 0) & (j == 0))(start)

        @pl.when(j == 0)
        def _():
            xv = x_ref[...]
            h_ref[...] = ((xv * _rstd(xv)) * g_ref[...]).astype(BF16)

        h = h_ref[...]
        gate = _dot_nt(h, _merge_rows(wg_ref[...]))
        up = _dot_nt(h, _merge_rows(wu_ref[...]))
        gate_ref[...] = gate.astype(BF16)
        up_ref[...] = up.astype(BF16)
        a_ref[...] = (gate * (1.0 / (1.0 + jnp.exp(-gate))) * up).astype(BF16)

        pl.when((i == steps[0] - 1) & (j == steps[1] - 1))(finish)

    wide = pl.BlockSpec((tm, tn), lambda i, j: (i, j))
    return pl.pallas_call(
        body, name="ffn_up", grid=steps,
        out_shape=[jax.ShapeDtypeStruct((t, d), BF16)] + [jax.ShapeDtypeStruct((t, f), BF16)] * 3
        + [jax.ShapeDtypeStruct((N_DEV, *down_shard.shape), down_shard.dtype)],
        in_specs=[pl.BlockSpec((tm, d), lambda i, j: (i, 0)), pl.BlockSpec((1, d), lambda i, j: (0, 0)),
                  pl.BlockSpec((ts, None, n, d), lambda i, j: (j, 0, 0, 0)),
                  pl.BlockSpec((ts, None, n, d), lambda i, j: (j, 1, 0, 0)), ANY],
        out_specs=[pl.BlockSpec((tm, d), lambda i, j: (i, 0)), wide, wide, wide, ANY],
        scratch_shapes=_gather_scratch([down_shard]),
        compiler_params=_params(),
    )(x1, g, gate_up, gate_up, down_shard)


def ffn_down_loss(a, w_down, x1, g, target):
    t, d = x1.shape
    tm = TOKEN_TILE

    def body(a_ref, w_ref, x_ref, g_ref, t_ref, df_ref, dy_ref, dg_ref, loss_ref):
        @pl.when(pl.program_id(0) == 0)
        def _():
            dg_ref[...] = jnp.zeros_like(dg_ref)
            loss_ref[...] = jnp.zeros_like(loss_ref)

        f = _dot(a_ref[...], _merge_rows(w_ref[...]))
        r = _rstd(f)
        g = g_ref[...]
        err = x_ref[...] + (f * r) * g - t_ref[...]
        loss_ref[...] += 0.5 * jnp.sum(jnp.mean(err * err, axis=-1, keepdims=True))
        dy = err * (1.0 / d)
        dy_ref[...] = dy
        df, dg_rows = _norm_bwd(dy, f, r, g)
        df_ref[...] = df.astype(BF16)
        dg_ref[...] += _as_rows(jnp.sum(dg_rows, axis=0, keepdims=True))

    row = pl.BlockSpec((tm, d), lambda i: (i, 0))
    gain = pl.BlockSpec((1, d), lambda i: (0, 0))
    return pl.pallas_call(
        body, name="ffn_down_loss", grid=(t // tm,),
        out_shape=[jax.ShapeDtypeStruct((t, d), BF16), jax.ShapeDtypeStruct((t, d), F32),
                   jax.ShapeDtypeStruct((d // 128, 128), F32), jax.ShapeDtypeStruct((1, 128), F32)],
        in_specs=[pl.BlockSpec((tm, a.shape[1]), lambda i: (i, 0)), pl.BlockSpec(w_down.shape, lambda i: (0, 0, 0)), row, gain, row],
        out_specs=[row, row, pl.BlockSpec((d // 128, 128), lambda i: (0, 0)), pl.BlockSpec((1, 128), lambda i: (0, 0))],
        compiler_params=_params(),
    )(a, w_down, x1, g, target)


def ffn_down_bwd(df, w_down, gate, up):
    t, d = df.shape
    n = w_down.shape[1]
    f = gate.shape[1]
    tm, ts = TOKEN_TILE, FF_SHARDS_PER_TILE
    tn = ts * n

    def body(df_ref, w_ref, gate_ref, up_ref, dgate_ref, dup_ref):
        da = _dot_nt(df_ref[...], _merge_rows(w_ref[...]))
        gate = gate_ref[...].astype(F32)
        sig = 1.0 / (1.0 + jnp.exp(-gate))
        dgate_ref[...] = (da * up_ref[...].astype(F32) * (sig * (1.0 + gate * (1.0 - sig)))).astype(BF16)
        dup_ref[...] = (da * (gate * sig)).astype(BF16)

    wide = pl.BlockSpec((tm, tn), lambda i, j: (i, j))
    return pl.pallas_call(
        body, name="ffn_down_bwd", grid=(t // tm, f // tn),
        out_shape=[jax.ShapeDtypeStruct((t, f), BF16)] * 2,
        in_specs=[pl.BlockSpec((tm, d), lambda i, j: (i, 0)), pl.BlockSpec((ts, n, d), lambda i, j: (j, 0, 0)), wide, wide],
        out_specs=[wide, wide],
        compiler_params=_params(),
    )(df, w_down, gate, up)


def grad_rows(a, b, name, by_core=False):
    t, m = a.shape
    d = b.shape[1]
    r = m // N_DEV
    tt = TOKEN_TILE
    last = t // tt - 1
    out_shape = (2, N_CHIP, r, d) if by_core else (N_DEV, r, d)

    def body(a_ref, b_ref, out_ref, acc):
        k = pl.program_id(0)

        @pl.when(k == 0)
        def _():
            acc[...] = jnp.zeros_like(acc)

        acc[...] += _dot_tn(a_ref[...], b_ref[...])

        @pl.when(k == last)
        def _():
            if by_core:
                blocks = acc[...].reshape(N_CHIP, 2, r, d)
                for chip in range(N_CHIP):
                    for core in range(2):
                        out_ref[core, chip] = blocks[chip, core].astype(BF16)
            else:
                out_ref[...] = acc[...].reshape(out_shape).astype(BF16)

    return pl.pallas_call(
        body, name=name, grid=(t // tt,),
        out_shape=jax.ShapeDtypeStruct(out_shape, BF16),
        in_specs=[pl.BlockSpec((tt, m), lambda k: (k, 0)), pl.BlockSpec((tt, d), lambda k: (k, 0))],
        out_specs=pl.BlockSpec(out_shape, lambda k: (0,) * len(out_shape)),
        scratch_shapes=[pltpu.VMEM((m, d), F32)],
        compiler_params=_params(),
    )(a, b)


def grad_ffn(lhs, b, name):
    t, f = lhs[0].shape
    d = b.shape[1]
    nw = len(lhs)
    n = f // N_DEV
    tt, ts = TOKEN_TILE, FF_SHARDS_PER_TILE
    tn = ts * n
    last = t // tt - 1

    def body(*refs):
        a_refs, b_ref, out_ref, acc = refs[:nw], refs[nw], refs[nw + 1], refs[nw + 2]
        k = pl.program_id(1)

        @pl.when(k == 0)
        def _():
            acc[...] = jnp.zeros_like(acc)

        for w in range(nw):
            acc[w] += _dot_tn(a_refs[w][...], b_ref[...])

        @pl.when(k == last)
        def _():
            for w in range(nw):
                blocks = acc[w].reshape(ts // 2, 2, n, d)
                for chip in range(ts // 2):
                    for core in range(2):
                        out_ref[core, chip, w] = blocks[chip, core].astype(BF16)

    return pl.pallas_call(
        body, name=name, grid=(f // tn, t // tt),
        out_shape=jax.ShapeDtypeStruct((2, N_CHIP, nw, n, d), BF16),
        in_specs=[pl.BlockSpec((tt, tn), lambda i, k: (k, i))] * nw + [pl.BlockSpec((tt, d), lambda i, k: (k, 0))],
        out_specs=pl.BlockSpec((2, ts // 2, nw, n, d), lambda i, k: (0, i, 0, 0, 0)),
        scratch_shapes=[pltpu.VMEM((nw, tn, d), F32)],
        compiler_params=_params(),
    )(*lhs, b)


def ffn_up_bwd(dgate, dup, gate_up, x1, g_ffn, dy, mix, g_mix):
    t, d = x1.shape
    n = gate_up.shape[2]
    tm, ts = TOKEN_TILE, FF_SHARDS_PER_TILE
    tk = ts * n
    ksteps = N_DEV // ts

    def body(dg_ref, du_ref, wg_ref, wu_ref, x_ref, gf_ref, dy_ref, mix_ref, gm_ref, dx1_ref, dmix_ref, dgf_ref, dgm_ref, acc):
        i, k = pl.program_id(0), pl.program_id(1)

        @pl.when((i == 0) & (k == 0))
        def _():
            dgf_ref[...] = jnp.zeros_like(dgf_ref)
            dgm_ref[...] = jnp.zeros_like(dgm_ref)

        part = _dot(dg_ref[...], _merge_rows(wg_ref[...])) + _dot(du_ref[...], _merge_rows(wu_ref[...]))

        @pl.when(k == 0)
        def _():
            acc[...] = part

        @pl.when(k > 0)
        def _():
            acc[...] += part

        @pl.when(k == ksteps - 1)
        def _():
            x1 = x_ref[...]
            dx, dgf_rows = _norm_bwd(acc[...], x1, _rstd(x1), gf_ref[...])
            dx1 = dy_ref[...] + dx
            dx1_ref[...] = dx1
            dgf_ref[...] += _as_rows(jnp.sum(dgf_rows, axis=0, keepdims=True))
            mix = mix_ref[...]
            dmix, dgm_rows = _norm_bwd(dx1, mix, _rstd(mix), gm_ref[...])
            dmix_ref[...] = dmix.astype(BF16)
            dgm_ref[...] += _as_rows(jnp.sum(dgm_rows, axis=0, keepdims=True))

    row = pl.BlockSpec((tm, d), lambda i, k: (i, 0))
    wide = pl.BlockSpec((tm, tk), lambda i, k: (i, k))
    gain = pl.BlockSpec((1, d), lambda i, k: (0, 0))
    gain_rows = pl.BlockSpec((d // 128, 128), lambda i, k: (0, 0))
    return pl.pallas_call(
        body, name="ffn_up_bwd", grid=(t // tm, ksteps),
        out_shape=[jax.ShapeDtypeStruct((t, d), F32), jax.ShapeDtypeStruct((t, d), BF16),
                   jax.ShapeDtypeStruct((d // 128, 128), F32), jax.ShapeDtypeStruct((d // 128, 128), F32)],
        in_specs=[wide, wide, pl.BlockSpec((ts, None, n, d), lambda i, k: (k, 0, 0, 0)),
                  pl.BlockSpec((ts, None, n, d), lambda i, k: (k, 1, 0, 0)), row, gain, row, row, gain],
        out_specs=[row, row, gain_rows, gain_rows],
        scratch_shapes=[pltpu.VMEM((tm, d), F32)],
        compiler_params=_params(),
    )(dgate, dup, gate_up, gate_up, x1, g_ffn, dy, mix, g_mix)


def outproj_bwd(dmix, w_out):
    t, d = dmix.shape
    tm = TOKEN_TILE

    def body(dm_ref, w_ref, out_ref):
        out_ref[...] = _dot_nt(dm_ref[...], w_ref[...])

    return pl.pallas_call(
        body, name="outproj_bwd", grid=(t // tm,),
        out_shape=jax.ShapeDtypeStruct((t, w_out.shape[0]), F32),
        in_specs=[pl.BlockSpec((tm, d), lambda i: (i, 0)), pl.BlockSpec(w_out.shape, lambda i: (0, 0))],
        out_specs=pl.BlockSpec((tm, w_out.shape[0]), lambda i: (i, 0)),
        compiler_params=_params(),
    )(dmix, w_out)


def mixers_bwd(proj, dcat, biasm, sinks, w_pool, pool_scale, ffn_parts):
    t = proj.shape[0]
    nb = t // BLOCK
    na = len(ffn_parts)

    def body(*refs):
        (uc_ref, halo_ref, q_ref, kvc_ref, kvp_ref, dcat_ref, biasm_ref, sinks_ref, wp_ref, sc_ref) = refs[:10]
        part_refs = refs[10:10 + na]
        dproj_ref, dbias_ref, dsink_ref, dwp_ref, dsc_ref = refs[10 + na:15 + na]
        slot_refs = refs[15 + na:15 + 2 * na]
        ubuf, dbuf, c_u, c_q, c_kv, send_sems, recv_sems, local_sems = refs[15 + 2 * na:23 + 2 * na]
        bounce = refs[23 + 2 * na:]
        i = pl.program_id(0)
        lane = lax.broadcasted_iota(jnp.int32, (1, 128), 1)
        start, finish = _chip_exchange_plan(part_refs, slot_refs, send_sems, recv_sems, local_sems, bounce)

        @pl.when(i == 0)
        def _():
            start()
            dbias_ref[...] = jnp.zeros_like(dbias_ref)
            dwp_ref[...] = jnp.zeros_like(dwp_ref)
            dsc_ref[...] = jnp.zeros_like(dsc_ref)
            dsink_ref[...] = jnp.zeros_like(dsink_ref)
            dbuf[...] = jnp.zeros_like(dbuf)
            c_u[...] = jnp.zeros_like(c_u)
            c_q[...] = jnp.zeros_like(c_q)
            c_kv[...] = jnp.zeros_like(c_kv)

        @pl.when(i < nb)
        def _():
            _fill_pool_input(i, ubuf, uc_ref, halo_ref)
            for g, w in enumerate(POOL_WINDOWS):
                cols = slice(g * POOL_GROUP_DIM, (g + 1) * POOL_GROUP_DIM)
                pooled = _pooled(i, g, w, ubuf).astype(BF16)
                mixed = _dot(pooled, wp_ref[g])
                dout = dcat_ref[:, cols]
                dsc_ref[g:g + 1, :] += jnp.sum(dout * mixed, axis=0, keepdims=True)
                dmixed = (dout * sc_ref[:, cols]).astype(BF16)
                dwp_ref[g] += _dot_tn(pooled, dmixed)
                dpooled = _dot_nt(dmixed, wp_ref[g])
                scaled = dpooled * _inv_count(i, w)
                dbuf[BLOCK:, cols] = scaled[0:HALO]
                dproj_ref[:, cols] = (_window_sum(dbuf, g, w, lambda k: k) + c_u[:, cols]).astype(BF16)
                dbuf[0:BLOCK, cols] = scaled
                c_u[:, cols] = -dpooled

            kv = jnp.concatenate([kvp_ref[...], kvc_ref[...]], axis=0)
            k_var = _head_variants(kv[:, 0:2 * HEAD_DIM])
            v_var = _head_variants(kv[:, 2 * HEAD_DIM:])
            q2s = [q_ref[:, 2 * HEAD_DIM * j:2 * HEAD_DIM * (j + 1)].astype(BF16) for j in range(N_Q_HEADS // 2)]
            do2s = [dcat_ref[:, POOL_WIDTH + 2 * HEAD_DIM * j:POOL_WIDTH + 2 * HEAD_DIM * (j + 1)].astype(BF16)
                    for j in range(N_Q_HEADS // 2)]
            dq2 = [None] * (N_Q_HEADS // 2)
            dss_of, pb_of = {}, {}
            dsink_row = jnp.zeros((1, 128), F32)
            for hq in range(N_Q_HEADS):
                j, half, h = hq // 2, hq % 2, hq // GQA_GROUP
                probs, p_sink = _head_probs(i, hq, q2s[j], k_var[h][half], biasm_ref, sinks_ref)
                dp = _dot_nt(do2s[j], v_var[h][half])
                delta = jnp.sum(probs * dp, axis=-1, keepdims=True)
                ds = probs * (dp - delta)
                dbias_ref[hq] += ds
                dsink_row = dsink_row - jnp.where(lane == hq, jnp.sum(p_sink * delta), 0.0)
                dss = (ds * ATTN_SCALE).astype(BF16)
                dss_of[hq], pb_of[hq] = dss, probs.astype(BF16)
                dq = _dot(dss, k_var[h][half])
                dq2[j] = dq if dq2[j] is None else dq2[j] + dq
            dsink_ref[...] += dsink_row
            low = lax.broadcasted_iota(jnp.int32, (2 * BLOCK, 2 * HEAD_DIM), 1) < HEAD_DIM
            dk_half, dv_half = [[None, None], [None, None]], [[None, None], [None, None]]
            for h in range(N_KV_HEADS):
                for half in range(2):
                    heads = [hq for hq in range(GQA_GROUP * h, GQA_GROUP * (h + 1)) if hq % 2 == half]
                    q_rows = jnp.concatenate([q2s[hq // 2] for hq in heads], axis=0)
                    do_rows = jnp.concatenate([do2s[hq // 2] for hq in heads], axis=0)
                    dk_half[h][half] = _dot_tn(jnp.concatenate([dss_of[hq] for hq in heads], axis=0), q_rows)
                    dv_half[h][half] = _dot_tn(jnp.concatenate([pb_of[hq] for hq in heads], axis=0), do_rows)

            def pair_of(halves):
                return jnp.where(low, halves[0][0] + pltpu.roll(halves[0][1], HEAD_DIM, 1),
                                 halves[1][1] + pltpu.roll(halves[1][0], HEAD_DIM, 1))

            dkv = jnp.concatenate([pair_of(dk_half), pair_of(dv_half)], axis=1)
            dproj_ref[:, POOL_WIDTH:2 * POOL_WIDTH] = c_q[...].astype(BF16)
            dproj_ref[:, 2 * POOL_WIDTH:] = (c_kv[...] + dkv[0:BLOCK]).astype(BF16)
            c_q[...] = jnp.concatenate(dq2, axis=1)
            c_kv[...] = dkv[BLOCK:]

        @pl.when(i == nb)
        def _():
            dbuf[BLOCK:, :] = jnp.zeros((HALO, POOL_WIDTH), F32)
            for g, w in enumerate(POOL_WINDOWS):
                cols = slice(g * POOL_GROUP_DIM, (g + 1) * POOL_GROUP_DIM)
                dproj_ref[:, cols] = (_window_sum(dbuf, g, w, lambda k: k) + c_u[:, cols]).astype(BF16)
            dproj_ref[:, POOL_WIDTH:2 * POOL_WIDTH] = c_q[...].astype(BF16)
            dproj_ref[:, 2 * POOL_WIDTH:] = c_kv[...].astype(BF16)
            finish()

    cur = lambda i: jnp.minimum(i, nb - 1)
    prv = lambda i: jnp.maximum(jnp.minimum(i, nb - 1) - 1, 0)
    out = pl.pallas_call(
        body, name="mixers_bwd", grid=(nb + 1,),
        out_shape=[jax.ShapeDtypeStruct((t, proj.shape[1]), BF16),
                   jax.ShapeDtypeStruct((N_Q_HEADS, BLOCK, 2 * BLOCK), F32),
                   jax.ShapeDtypeStruct((1, 128), F32),
                   jax.ShapeDtypeStruct((4, POOL_GROUP_DIM, POOL_GROUP_DIM), F32),
                   jax.ShapeDtypeStruct((len(POOL_WINDOWS), POOL_GROUP_DIM), F32)]
        + [jax.ShapeDtypeStruct(p.shape, p.dtype) for p in ffn_parts],
        in_specs=_mixer_in_specs(cur, prv) + [pl.BlockSpec((BLOCK, 2 * POOL_WIDTH), lambda i: (cur(i), 0))]
        + _mixer_param_specs() + [ANY] * na,
        out_specs=[pl.BlockSpec((BLOCK, proj.shape[1]), lambda i: (jnp.maximum(i - 1, 0), 0)),
                   pl.BlockSpec((N_Q_HEADS, BLOCK, 2 * BLOCK), lambda i: (0, 0, 0)),
                   pl.BlockSpec((1, 128), lambda i: (0, 0)),
                   pl.BlockSpec((4, POOL_GROUP_DIM, POOL_GROUP_DIM), lambda i: (0, 0, 0)),
                   pl.BlockSpec((len(POOL_WINDOWS), POOL_GROUP_DIM), lambda i: (0, 0))] + [ANY] * na,
        scratch_shapes=[pltpu.VMEM((HALO + BLOCK, POOL_WIDTH), F32), pltpu.VMEM((BLOCK + HALO, POOL_WIDTH), F32),
                        pltpu.VMEM((BLOCK, POOL_WIDTH), F32), pltpu.VMEM((BLOCK, POOL_WIDTH), F32),
                        pltpu.VMEM((BLOCK, 256), F32)] + _chip_exchange_scratch(ffn_parts),
        compiler_params=_params(),
    )(proj, proj, proj, proj, proj, dcat, biasm, sinks, w_pool, pool_scale, *ffn_parts)
    return out[:5], out[5:]


def inproj_bwd(dproj, w_in_t, x, g, dx1):
    t, d = x.shape
    n = dproj.shape[1]
    tm = TOKEN_TILE

    def body(dp_ref, w_ref, x_ref, g_ref, dx1_ref, dx_ref, dg_ref):
        @pl.when(pl.program_id(0) == 0)
        def _():
            dg_ref[...] = jnp.zeros_like(dg_ref)

        dh = _dot(dp_ref[...], w_ref[...])
        xv = x_ref[...]
        dx, dg_rows = _norm_bwd(dh, xv, _rstd(xv), g_ref[...])
        dx_ref[...] = dx1_ref[...] + dx
        dg_ref[...] += _as_rows(jnp.sum(dg_rows, axis=0, keepdims=True))

    row = pl.BlockSpec((tm, d), lambda i: (i, 0))
    gain = pl.BlockSpec((1, d), lambda i: (0, 0))
    return pl.pallas_call(
        body, name="inproj_bwd", grid=(t // tm,),
        out_shape=[jax.ShapeDtypeStruct((t, d), F32), jax.ShapeDtypeStruct((d // 128, 128), F32)],
        in_specs=[pl.BlockSpec((tm, n), lambda i: (i, 0)), pl.BlockSpec(w_in_t.shape, lambda i: (0, 0)), row, gain, row],
        out_specs=[row, pl.BlockSpec((d // 128, 128), lambda i: (0, 0))],
        compiler_params=_params(),
    )(dproj, w_in_t, x, g, dx1)


def _bucket_band():
    qi = jnp.arange(BLOCK)[:, None]
    kj = jnp.arange(2 * BLOCK)[None, :]
    dist = qi + BLOCK - kj
    n = jnp.maximum(dist, 0)
    nf = jnp.maximum(n, 1).astype(F32)
    large = MAX_EXACT + (jnp.log(nf / MAX_EXACT) / np.float32(np.log(MAX_DISTANCE / MAX_EXACT))
                         * (N_BUCKETS - MAX_EXACT)).astype(jnp.int32)
    large = jnp.minimum(large, N_BUCKETS - 1)
    bucket = jnp.where(n < MAX_EXACT, n, large)
    in_window = (dist >= 0) & (dist < BLOCK)
    return bucket.astype(F32), in_window.astype(F32)


def kernel(x, g_pre_mix, w_in, w_pool, pool_scale, rel_bias, sinks, w_out, g_post_mix, g_pre_ffn, w_gate, w_up, w_down, g_post_ffn, loss_target, m_g_pre_mix, m_w_in, m_w_pool, m_pool_scale, m_rel_bias, m_sinks, m_w_out, m_g_post_mix, m_g_pre_ffn, m_w_gate, m_w_up, m_w_down, m_g_post_ffn, v_g_pre_mix, v_w_in, v_w_pool, v_pool_scale, v_rel_bias, v_sinks, v_w_out, v_g_post_mix, v_g_pre_ffn, v_w_gate, v_w_up, v_w_down, v_g_post_ffn):
    d = x.shape[-1]
    xs, target = x[0], loss_target[0]

    w_in_ts = w_in[0].T.astype(BF16)
    w_out_s = w_out[0].astype(BF16)
    gate_up_s = jnp.stack([w_gate[0].T, w_up[0].T]).astype(BF16)
    w_down_s = w_down[0].astype(BF16)
    w_in_t, w_out_f = gather_blocks([w_in_ts, w_out_s], "gather_mix_weights")
    w_in_t = w_in_t.reshape(-1, d)
    w_out_f = w_out_f.reshape(-1, d)

    bucket, in_window = _bucket_band()
    biasm = bias_band(bucket, in_window, rel_bias)
    w_pool_b = w_pool[0].astype(BF16)
    proj, h1 = norm_inproj(xs, g_pre_mix, w_in_t)
    cat, gate_up = mixers_fwd(proj, biasm, sinks, w_pool_b, pool_scale, gate_up_s)
    mix, x1 = outproj_norm(cat, w_out_f, xs, g_post_mix)
    h2, gate, up, act, w_down_f = ffn_up(x1, g_pre_ffn, gate_up, w_down_s)
    df, dy, dg_post_ffn, loss_part = ffn_down_loss(act, w_down_f, x1, g_post_ffn, target)

    dgate, dup = ffn_down_bwd(df, w_down_f, gate, up)
    d_gate_up = grad_ffn([dgate, dup], h2, "grad_w_gate_up")
    d_down = grad_ffn([act], df, "grad_w_down")
    chip_parts = pair_add([d_gate_up, d_down], pair_exchange([d_gate_up, d_down], "pair_exchange_ffn"), "pair_add_ffn")
    dx1, dmix, dg_pre_ffn, dg_post_mix = ffn_up_bwd(dgate, dup, gate_up, x1, g_pre_ffn, dy, mix, g_post_mix)

    dcat = outproj_bwd(dmix, w_out_f)
    d_out = grad_rows(cat, dmix, "grad_w_out", by_core=True)
    chip_parts += pair_add([d_out], pair_exchange([d_out], "pair_exchange_out"), "pair_add_out")
    (dproj, dbias, dsinks, dw_pool, dpool_scale), slots = mixers_bwd(
        proj, dcat, biasm, sinks, w_pool_b, pool_scale, chip_parts)
    drel_bias = bias_band_bwd(bucket, dbias)
    grad_x, dg_pre_mix = inproj_bwd(dproj, w_in_t, xs, g_pre_mix, dx1)
    d_in_t = grad_rows(dproj, h1, "grad_w_in")

    small_w = [g_pre_mix, g_post_mix, g_pre_ffn, g_post_ffn, pool_scale, sinks, w_pool, rel_bias]
    small_m = [m_g_pre_mix, m_g_post_mix, m_g_pre_ffn, m_g_post_ffn, m_pool_scale, m_sinks, m_w_pool, m_rel_bias]
    small_v = [v_g_pre_mix, v_g_post_mix, v_g_pre_ffn, v_g_post_ffn, v_pool_scale, v_sinks, v_w_pool, v_rel_bias]
    in_slots, loss_row, sm = tail_reduce(
        d_in_t, [dg_pre_mix, dg_post_mix, dg_pre_ffn, dg_post_ffn], dpool_scale, dsinks, loss_part, dw_pool, drel_bias,
        small_w, small_m, small_v)
    g_gate_up_t, g_down, g_out = sum_slots(slots, "sum_ffn")
    g_in_t, = sum_slots([in_slots], "sum_in")
    big_w = [w_in[0], w_out[0], w_gate[0], w_up[0], w_down[0]]
    big_g = [g_in_t.T, g_out, g_gate_up_t[0].T, g_gate_up_t[1].T, g_down[0]]
    big_m = [m_w_in[0], m_w_out[0], m_w_gate[0], m_w_up[0], m_w_down[0]]
    big_v = [v_w_in[0], v_w_out[0], v_w_gate[0], v_w_up[0], v_w_down[0]]
    upd = adamw_update(big_w[:2], big_g[:2], big_m[:2], big_v[:2], "adamw_mix") \
        + adamw_update(big_w[2:], big_g[2:], big_m[2:], big_v[2:], "adamw_ffn")
    big = [[big_g[k][None], *(u[None] for u in upd[k])] for k in range(5)]

    def ordered(kind):
        s, b = [p[kind] for p in sm], [p[kind] for p in big]
        return [s[0], b[0], s[6], s[4], s[7], s[5], b[1], s[1], s[2], b[2], b[3], b[4], s[3]]

    return (loss_row[0, 0], grad_x[None], *ordered(0), *ordered(1), *ordered(2), *ordered(3))
```

```python
import numpy as np
import jax
import jax.numpy as jnp
from jax import lax
from jax.experimental import pallas as pl
from jax.experimental.pallas import tpu as pltpu

F32 = jnp.float32
BF16 = jnp.bfloat16

N_DEV = 8
N_CHIP = 4
POOL_WIDTH = 512
POOL_WINDOWS = (2, 4, 8, 16)
POOL_GROUP_DIM = 128
HEAD_DIM = 64
N_Q_HEADS = 8
N_KV_HEADS = 2
GQA_GROUP = 4
BLOCK = 128
HALO = 16
N_BUCKETS = 32
MAX_EXACT = 16
MAX_DISTANCE = 128
EPS = 1e-6
NEG_INF = -1e30
ATTN_SCALE = float(1.0 / np.sqrt(np.float32(HEAD_DIM)))

ADAM_LR = 0.001
ADAM_B1 = 0.9
ADAM_B2 = 0.999
ADAM_EPS = 1e-08
ADAM_WD = 0.01
ADAM_STEP = 10

TOKEN_TILE = 512
FFN_TOKEN_TILE = 1024
FF_SHARDS_PER_TILE = 4
VMEM_LIMIT = 56 * 1024 * 1024
MESH = pl.DeviceIdType.MESH
ANY = pl.BlockSpec(memory_space=pl.ANY)
VMEM = pl.BlockSpec(memory_space=pltpu.VMEM)
SMEM = pl.BlockSpec(memory_space=pltpu.SMEM)


def _params(**kw):
    return pltpu.CompilerParams(vmem_limit_bytes=VMEM_LIMIT, **kw)


def _dot(a, b):
    return jnp.dot(a, b, preferred_element_type=F32)


def _dot_nt(a, b):
    return lax.dot_general(a, b, (((1,), (1,)), ((), ())), preferred_element_type=F32)


def _dot_tn(a, b):
    return lax.dot_general(a, b, (((0,), (0,)), ((), ())), preferred_element_type=F32)


def _rstd(v):
    return lax.rsqrt(jnp.mean(v * v, axis=-1, keepdims=True) + EPS)


def _norm_bwd(dout, v, r, g):
    vn = v * r
    dn = dout * g
    dv = r * (dn - vn * jnp.mean(dn * vn, axis=-1, keepdims=True))
    return dv, dout * vn


def _as_rows(v):
    return jnp.concatenate([v[:, k:k + 128] for k in range(0, v.shape[1], 128)], axis=0)


def _as_lanes(rows):
    return jnp.concatenate([rows[k:k + 1, :] for k in range(rows.shape[0])], axis=1)


def _merge_rows(value):
    s, r, c_ = value.shape
    return value.reshape(s * r, c_)


def _gather_plan(srcs, outs, send_sems, recv_sems, local_sems=None, bounce=None):
    n = len(srcs)
    x, y, c = lax.axis_index("x"), lax.axis_index("y"), lax.axis_index("c")
    me, sibling = (x, y, c), (x, y, 1 - c)
    chips = [(1 - x, y), (x, 1 - y), (1 - x, 1 - y)]

    def slot(a, px, py, pc):
        return outs[a].at[4 * px + 2 * py + pc]

    def copy(a, k, block, to, from_src=False):
        return pltpu.make_async_remote_copy(
            src_ref=srcs[a] if from_src else slot(a, *block), dst_ref=slot(a, *block),
            send_sem=send_sems.at[k * n + a], recv_sem=recv_sems.at[k * n + a], device_id=to, device_id_type=MESH)

    def own_in(a):
        return pltpu.make_async_copy(srcs[a], bounce[a], local_sems.at[a])

    def own_out(a):
        return pltpu.make_async_copy(bounce[a], slot(a, *me), local_sems.at[a])

    def first(a):
        return [copy(a, 0, me, sibling, True)] + [copy(a, 1 + j, me, (*chip, c), True) for j, chip in enumerate(chips)]

    def passed(a, j):
        return copy(a, 4 + j, (*chips[j], c), sibling)

    def start():
        for a in range(n):
            if bounce is not None:
                own_in(a).start()
            for cp in first(a):
                cp.start()

    def finish():
        if bounce is not None:
            for a in range(n):
                own_in(a).wait()
                own_out(a).start()
        for j, chip in enumerate(chips):
            for a in range(n):
                copy(a, 1 + j, (*chip, c), me).wait_recv()
                passed(a, j).start()
        for a in range(n):
            copy(a, 0, sibling, me).wait_recv()
            for j, chip in enumerate(chips):
                copy(a, 4 + j, (*chip, 1 - c), me).wait_recv()
        for a in range(n):
            for cp in first(a) + [passed(a, j) for j in range(3)]:
                cp.wait_send()
            if bounce is not None:
                own_out(a).wait()

    return start, finish


def _gather_scratch(shards):
    n = len(shards)
    return [pltpu.SemaphoreType.DMA((7 * n,)), pltpu.SemaphoreType.DMA((7 * n,)), pltpu.SemaphoreType.DMA((n,))] \
        + [pltpu.VMEM(s.shape, s.dtype) for s in shards]


def _chip_exchange_plan(srcs, outs, send_sems, recv_sems, local_sems, bounce):
    n = len(srcs)
    x, y, c = lax.axis_index("x"), lax.axis_index("y"), lax.axis_index("c")
    my_chip = 2 * x + y

    def copies():
        out = []
        for a in range(n):
            for k in range(1, N_CHIP):
                px, py = x ^ (k >> 1), y ^ (k & 1)
                out.append(pltpu.make_async_remote_copy(
                    src_ref=srcs[a].at[2 * px + py], dst_ref=outs[a].at[my_chip],
                    send_sem=send_sems.at[(k - 1) * n + a], recv_sem=recv_sems.at[(k - 1) * n + a],
                    device_id=(px, py, c), device_id_type=MESH))
        return out

    def own_in(a):
        return pltpu.make_async_copy(srcs[a].at[my_chip], bounce[a], local_sems.at[a])

    def own_out(a):
        return pltpu.make_async_copy(bounce[a], outs[a].at[my_chip], local_sems.at[a])

    def start():
        for a in range(n):
            own_in(a).start()
        for cp in copies():
            cp.start()

    def finish():
        for a in range(n):
            own_in(a).wait()
            own_out(a).start()
        for cp in copies():
            cp.wait()
        for a in range(n):
            own_out(a).wait()

    return start, finish


def _chip_exchange_scratch(parts):
    n = len(parts)
    return [pltpu.SemaphoreType.DMA((3 * n,)), pltpu.SemaphoreType.DMA((3 * n,)), pltpu.SemaphoreType.DMA((n,))] \
        + [pltpu.VMEM(p.shape[1:], p.dtype) for p in parts]


def gather_blocks(shards, name):
    def body(*refs):
        n = len(shards)
        start, finish = _gather_plan(refs[:n], refs[n:2 * n], *refs[2 * n:2 * n + 3], bounce=refs[2 * n + 3:])
        start()
        finish()

    return pl.pallas_call(
        body, name=name,
        out_shape=[jax.ShapeDtypeStruct((N_DEV, *s.shape), s.dtype) for s in shards],
        in_specs=[ANY] * len(shards), out_specs=[ANY] * len(shards),
        scratch_shapes=_gather_scratch(shards),
    )(*shards)


def pair_exchange(parts, name):
    n = len(parts)

    def body(*refs):
        p_refs, got_refs, send_sems, recv_sems = refs[:n], refs[n:2 * n], *refs[2 * n:]
        x, y, c = lax.axis_index("x"), lax.axis_index("y"), lax.axis_index("c")
        copies = [pltpu.make_async_remote_copy(
            src_ref=p_refs[a].at[1 - c], dst_ref=got_refs[a], send_sem=send_sems.at[a], recv_sem=recv_sems.at[a],
            device_id=(x, y, 1 - c), device_id_type=MESH) for a in range(n)]
        for cp in copies:
            cp.start()
        for cp in copies:
            cp.wait()

    return pl.pallas_call(
        body, name=name, out_shape=[jax.ShapeDtypeStruct(p.shape[1:], p.dtype) for p in parts],
        in_specs=[ANY] * n, out_specs=[ANY] * n,
        scratch_shapes=[pltpu.SemaphoreType.DMA((n,)), pltpu.SemaphoreType.DMA((n,))],
    )(*parts)


def pair_add(parts, got, name):
    n = len(parts)

    def body(core_ref, *refs):
        for a in range(n):
            refs[2 * n + a][...] = (refs[a][...].astype(F32) + refs[n + a][...].astype(F32)).astype(BF16)

    def own(p):
        zeros = (0,) * (p.ndim - 2)
        return pl.BlockSpec((None, 1, *p.shape[2:]), lambda i, core: (core[0], i, *zeros))

    def plain(p):
        zeros = (0,) * (p.ndim - 1)
        return pl.BlockSpec((1, *p.shape[1:]), lambda i, core: (i, *zeros))

    core = lax.axis_index("c").astype(jnp.int32).reshape(1)
    return pl.pallas_call(
        body, name=name,
        grid_spec=pltpu.PrefetchScalarGridSpec(
            num_scalar_prefetch=1, grid=(got[0].shape[0],),
            in_specs=[own(p) for p in parts] + [plain(p) for p in got], out_specs=[plain(p) for p in got]),
        out_shape=[jax.ShapeDtypeStruct(p.shape, BF16) for p in got],
        compiler_params=_params(),
    )(core, *parts, *got)


def sum_slots(slots, name):
    n = len(slots)

    def body(*refs):
        for a in range(n):
            total = refs[a][0].astype(F32)
            for s in range(1, slots[a].shape[0]):
                total = total + refs[a][s].astype(F32)
            refs[n + a][...] = total

    return pl.pallas_call(
        body, name=name,
        out_shape=[jax.ShapeDtypeStruct(p.shape[1:], F32) for p in slots],
        in_specs=[VMEM] * n, out_specs=[VMEM] * n,
        compiler_params=_params(),
    )(*slots)


def _adamw(w, g, m, v):
    m2 = ADAM_B1 * m + (1.0 - ADAM_B1) * g
    v2 = ADAM_B2 * v + (1.0 - ADAM_B2) * (g * g)
    m_hat = m2 / (1.0 - ADAM_B1 ** ADAM_STEP)
    v_hat = v2 / (1.0 - ADAM_B2 ** ADAM_STEP)
    delta = -ADAM_LR * (m_hat / (jnp.sqrt(v_hat) + ADAM_EPS) + ADAM_WD * w)
    return delta, m2, v2


def adamw_update(ws, gs, ms, vs, name):
    n = len(ws)

    def body(*refs):
        for a in range(n):
            delta, m2, v2 = _adamw(refs[a][...], refs[n + a][...], refs[2 * n + a][...], refs[3 * n + a][...])
            refs[4 * n + 3 * a][...] = delta
            refs[4 * n + 3 * a + 1][...] = m2
            refs[4 * n + 3 * a + 2][...] = v2

    out = pl.pallas_call(
        body, name=name,
        out_shape=[jax.ShapeDtypeStruct(w.shape, F32) for w in ws for _ in range(3)],
        in_specs=[VMEM] * (4 * n), out_specs=[VMEM] * (3 * n),
        compiler_params=_params(),
    )(*ws, *gs, *ms, *vs)
    return [out[3 * a:3 * a + 3] for a in range(n)]


GAIN_ROWS = 8
ROW_POOL_SCALE = 4 * GAIN_ROWS
ROW_SINKS = ROW_POOL_SCALE + 4
ROW_LOSS = ROW_SINKS + 1
ROW_W_POOL = 40
SMALL_ROWS = ROW_W_POOL + 4 * POOL_GROUP_DIM


def tail_reduce(d_in_t, gains, dpool_scale, dsinks, loss_part, dw_pool, drel_bias, small_w, small_m, small_v):
    n_small = len(small_w)

    def body(*refs):
        d_in_ref, g_refs, (dsc_ref, dsink_ref, loss_ref, dwp_ref, drb_ref) = refs[0], refs[1:5], refs[5:10]
        w_refs, m_refs, v_refs = (refs[10 + k * n_small:10 + (k + 1) * n_small] for k in range(3))
        outs = refs[10 + 3 * n_small:]
        g_in_ref, loss_out = outs[0], outs[1]
        result = outs[2:2 + 4 * n_small]
        stage, gat, gat_rb, g_send, g_recv, pair_got, chip_part, chip_got, p_send, p_recv, x_send, x_recv = outs[2 + 4 * n_small:]
        x, y, c = lax.axis_index("x"), lax.axis_index("y"), lax.axis_index("c")
        my_id, my_chip = 4 * x + 2 * y + c, 2 * x + y

        for k in range(4):
            stage[GAIN_ROWS * k:GAIN_ROWS * (k + 1), :] = g_refs[k][...]
        stage[ROW_POOL_SCALE:ROW_SINKS, :] = dsc_ref[...]
        stage[ROW_SINKS:ROW_LOSS, :] = dsink_ref[...]
        stage[ROW_LOSS:ROW_LOSS + 1, :] = loss_ref[...]
        stage[ROW_LOSS + 1:ROW_W_POOL, :] = jnp.zeros((ROW_W_POOL - ROW_LOSS - 1, 128), F32)
        stage[ROW_W_POOL:, :] = dwp_ref[...].reshape(4 * POOL_GROUP_DIM, POOL_GROUP_DIM)
        gat[my_id] = stage[...]
        gat_rb[my_id] = drb_ref[...]
        start, finish = _gather_plan([stage, drb_ref], [gat, gat_rb], g_send, g_recv)
        start()

        pair = pltpu.make_async_remote_copy(
            src_ref=d_in_ref.at[1 - c], dst_ref=pair_got, send_sem=p_send, recv_sem=p_recv,
            device_id=(x, y, 1 - c), device_id_type=MESH)
        pair.start()
        pair.wait()
        chip_part[...] = (d_in_ref[c].astype(F32) + pair_got[...].astype(F32)).astype(BF16)
        copies = []
        for k in range(1, N_CHIP):
            px, py = x ^ (k >> 1), y ^ (k & 1)
            copies.append(pltpu.make_async_remote_copy(
                src_ref=chip_part.at[2 * px + py], dst_ref=chip_got.at[my_chip],
                send_sem=x_send.at[k - 1], recv_sem=x_recv.at[k - 1], device_id=(px, py, c), device_id_type=MESH))
        for cp in copies:
            cp.start()
        chip_got[my_chip] = chip_part[my_chip]

        finish()
        total, total_rb = gat[0], gat_rb[0]
        for s in range(1, N_DEV):
            total, total_rb = total + gat[s], total_rb + gat_rb[s]
        loss_out[...] = total[ROW_LOSS:ROW_LOSS + 1, :]
        grads = [_as_lanes(total[GAIN_ROWS * k:GAIN_ROWS * (k + 1), :]) for k in range(4)]
        grads.append(_as_lanes(total[ROW_POOL_SCALE:ROW_SINKS, :]))
        grads.append(total[ROW_SINKS:ROW_LOSS, 0:N_Q_HEADS])
        grads.append(total[ROW_W_POOL:, :].reshape(w_refs[6].shape))
        grads.append(total_rb)
        for k in range(n_small):
            delta, m2, v2 = _adamw(w_refs[k][...], grads[k], m_refs[k][...], v_refs[k][...])
            result[4 * k][...] = grads[k]
            result[4 * k + 1][...] = delta
            result[4 * k + 2][...] = m2
            result[4 * k + 3][...] = v2

        for cp in copies:
            cp.wait()
        g_in = chip_got[0].astype(F32)
        for s in range(1, N_CHIP):
            g_in = g_in + chip_got[s].astype(F32)
        g_in_ref[...] = g_in

    n_in = 10 + 3 * n_small
    per_core = d_in_t.shape[1:]
    out = pl.pallas_call(
        body, name="tail_reduce",
        out_shape=[jax.ShapeDtypeStruct(d_in_t.shape[2:], F32), jax.ShapeDtypeStruct((1, 128), F32)]
        + [jax.ShapeDtypeStruct(w.shape, F32) for w in small_w for _ in range(4)],
        in_specs=[VMEM] * n_in, out_specs=[VMEM] * (2 + 4 * n_small),
        scratch_shapes=[pltpu.VMEM((SMALL_ROWS, 128), F32), pltpu.VMEM((N_DEV, SMALL_ROWS, 128), F32),
                        pltpu.VMEM((N_DEV, *drel_bias.shape), F32),
                        pltpu.SemaphoreType.DMA((14,)), pltpu.SemaphoreType.DMA((14,)),
                        pltpu.VMEM(per_core, d_in_t.dtype), pltpu.VMEM(per_core, d_in_t.dtype),
                        pltpu.VMEM(per_core, d_in_t.dtype),
                        pltpu.SemaphoreType.DMA, pltpu.SemaphoreType.DMA,
                        pltpu.SemaphoreType.DMA((3,)), pltpu.SemaphoreType.DMA((3,))],
        compiler_params=_params(),
    )(d_in_t, *gains, dpool_scale, dsinks, loss_part, dw_pool, drel_bias, *small_w, *small_m, *small_v)
    return out[0], out[1], [out[2 + 4 * k:6 + 4 * k] for k in range(n_small)]


def norm_inproj(x, g, w_t, out_shard):
    t, d = x.shape
    n = w_t.shape[0]
    tm = TOKEN_TILE
    last = t // tm - 1

    def body(x_ref, g_ref, w_ref, shard_ref, proj_ref, h_ref, gathered_ref, send_sems, recv_sems, local_sems, bounce):
        i = pl.program_id(0)
        start, finish = _gather_plan([shard_ref], [gathered_ref], send_sems, recv_sems, local_sems, [bounce])
        pl.when(i == 0)(start)
        xv = x_ref[...]
        h = ((xv * _rstd(xv)) * g_ref[...]).astype(BF16)
        h_ref[...] = h
        proj_ref[...] = _dot_nt(h, w_ref[...])
        pl.when(i == last)(finish)

    return pl.pallas_call(
        body, name="norm_inproj", grid=(t // tm,),
        out_shape=[jax.ShapeDtypeStruct((t, n), F32), jax.ShapeDtypeStruct((t, d), BF16),
                   jax.ShapeDtypeStruct((N_DEV, *out_shard.shape), out_shard.dtype)],
        in_specs=[pl.BlockSpec((tm, d), lambda i: (i, 0)), pl.BlockSpec((1, d), lambda i: (0, 0)),
                  pl.BlockSpec((n, d), lambda i: (0, 0)), ANY],
        out_specs=[pl.BlockSpec((tm, n), lambda i: (i, 0)), pl.BlockSpec((tm, d), lambda i: (i, 0)), ANY],
        scratch_shapes=_gather_scratch([out_shard]),
        compiler_params=_params(),
    )(x, g, w_t, out_shard)


def bias_band(bucket, in_window, rel_bias):
    def body(bk_ref, win_ref, rb_ref, out_ref):
        bk = bk_ref[...]
        keep = win_ref[...] > 0.5
        for h in range(N_Q_HEADS):
            acc = jnp.zeros(bk.shape, F32)
            for b in range(N_BUCKETS):
                acc = jnp.where(bk == float(b), rb_ref[b, h], acc)
            out_ref[h] = jnp.where(keep, acc, NEG_INF)

    return pl.pallas_call(
        body, name="bias_band",
        out_shape=jax.ShapeDtypeStruct((N_Q_HEADS, BLOCK, 2 * BLOCK), F32),
        in_specs=[VMEM, VMEM, SMEM], out_specs=VMEM,
    )(bucket, in_window, rel_bias)


def bias_band_bwd(bucket, dbias):
    def body(bk_ref, db_ref, out_ref):
        bk = bk_ref[...]
        for h in range(N_Q_HEADS):
            db = db_ref[h]
            for b in range(N_BUCKETS):
                out_ref[b, h] = jnp.sum(jnp.where(bk == float(b), db, 0.0))

    return pl.pallas_call(
        body, name="bias_band_bwd",
        out_shape=jax.ShapeDtypeStruct((N_BUCKETS, N_Q_HEADS), F32),
        in_specs=[VMEM, VMEM], out_specs=SMEM,
    )(bucket, dbias)


def _window_sum(buf_ref, g, w, first):
    cols = slice(g * POOL_GROUP_DIM, (g + 1) * POOL_GROUP_DIM)
    acc = None
    for k in range(w):
        piece = buf_ref[first(k):first(k) + BLOCK, cols]
        acc = piece if acc is None else acc + piece
    return acc


def _inv_count(i, w):
    row = lax.broadcasted_iota(jnp.int32, (BLOCK, 1), 0)
    return 1.0 / jnp.minimum(i * BLOCK + row + 1, w).astype(F32)


def _fill_pool_input(i, ubuf, uc_ref, halo_ref):
    ubuf[0:HALO, :] = jnp.where(i > 0, halo_ref[...], 0.0)
    ubuf[HALO:, :] = uc_ref[...]


def _pooled(i, g, w, ubuf):
    cols = slice(g * POOL_GROUP_DIM, (g + 1) * POOL_GROUP_DIM)
    return _window_sum(ubuf, g, w, lambda k: HALO - k) * _inv_count(i, w) - ubuf[HALO:, cols]


def _head_variants(pair):
    low = lax.broadcasted_iota(jnp.int32, pair.shape, 1) < HEAD_DIM
    swapped = pltpu.roll(pair, HEAD_DIM, 1)
    zero = jnp.zeros_like(pair)
    pick = lambda c, a, b: jnp.where(c, a, b).astype(BF16)
    return [[pick(low, pair, zero), pick(low, zero, swapped)], [pick(low, swapped, zero), pick(low, zero, pair)]]


def _head_probs(i, hq, q2, k_var, biasm_ref, sinks_ref):
    s = _dot_nt(q2, k_var) * ATTN_SCALE + biasm_ref[hq]
    col = lax.broadcasted_iota(jnp.int32, s.shape, 1)
    s = jnp.where((i == 0) & (col < BLOCK), NEG_INF, s)
    sink = sinks_ref[0, hq]
    m = jnp.maximum(jnp.max(s, axis=-1, keepdims=True), sink)
    p = jnp.exp(s - m)
    e_sink = jnp.exp(sink - m)
    inv = 1.0 / (jnp.sum(p, axis=-1, keepdims=True) + e_sink)
    return p * inv, e_sink * inv


def _mixer_in_specs(cur, prv):
    return [pl.BlockSpec((BLOCK, 512), lambda i: (cur(i), 0)),
            pl.BlockSpec((HALO, 512), lambda i: (jnp.maximum(cur(i) * (BLOCK // HALO) - 1, 0), 0)),
            pl.BlockSpec((BLOCK, 512), lambda i: (cur(i), 1)),
            pl.BlockSpec((BLOCK, 256), lambda i: (cur(i), 4)),
            pl.BlockSpec((BLOCK, 256), lambda i: (prv(i), 4))]


def _mixer_param_specs():
    return [pl.BlockSpec((N_Q_HEADS, BLOCK, 2 * BLOCK), lambda i: (0, 0, 0)), SMEM,
            pl.BlockSpec((4, POOL_GROUP_DIM, POOL_GROUP_DIM), lambda i: (0, 0, 0)),
            pl.BlockSpec((1, POOL_WIDTH), lambda i: (0, 0))]


def mixers_fwd(proj, biasm, sinks, w_pool, pool_scale, gate_up_shard):
    t = proj.shape[0]
    nb = t // BLOCK

    def body(uc_ref, halo_ref, q_ref, kvc_ref, kvp_ref, biasm_ref, sinks_ref, wp_ref, sc_ref, shard_ref,
             out_ref, gathered_ref, ubuf, send_sems, recv_sems, local_sems, bounce):
        i = pl.program_id(0)
        start, finish = _gather_plan([shard_ref], [gathered_ref], send_sems, recv_sems, local_sems, [bounce])
        pl.when(i == 0)(start)

        _fill_pool_input(i, ubuf, uc_ref, halo_ref)
        for g, w in enumerate(POOL_WINDOWS):
            mixed = _dot(_pooled(i, g, w, ubuf).astype(BF16), wp_ref[g])
            cols = slice(g * POOL_GROUP_DIM, (g + 1) * POOL_GROUP_DIM)
            out_ref[:, cols] = (mixed * sc_ref[:, cols]).astype(BF16)
        kv = jnp.concatenate([kvp_ref[...], kvc_ref[...]], axis=0)
        k_var = _head_variants(kv[:, 0:2 * HEAD_DIM])
        v_var = _head_variants(kv[:, 2 * HEAD_DIM:])
        for j in range(N_Q_HEADS // 2):
            q2 = q_ref[:, 2 * HEAD_DIM * j:2 * HEAD_DIM * (j + 1)].astype(BF16)
            acc = None
            for half in range(2):
                hq = 2 * j + half
                h = hq // GQA_GROUP
                probs, _ = _head_probs(i, hq, q2, k_var[h][half], biasm_ref, sinks_ref)
                o = _dot(probs.astype(BF16), v_var[h][half])
                acc = o if acc is None else acc + o
            out_ref[:, POOL_WIDTH + 2 * HEAD_DIM * j:POOL_WIDTH + 2 * HEAD_DIM * (j + 1)] = acc.astype(BF16)

        pl.when(i == nb - 1)(finish)

    return pl.pallas_call(
        body, name="mixers_fwd", grid=(nb,),
        out_shape=[jax.ShapeDtypeStruct((t, 2 * POOL_WIDTH), BF16),
                   jax.ShapeDtypeStruct((N_DEV, *gate_up_shard.shape), gate_up_shard.dtype)],
        in_specs=_mixer_in_specs(lambda i: i, lambda i: jnp.maximum(i - 1, 0)) + _mixer_param_specs() + [ANY],
        out_specs=[pl.BlockSpec((BLOCK, 2 * POOL_WIDTH), lambda i: (i, 0)), ANY],
        scratch_shapes=[pltpu.VMEM((HALO + BLOCK, POOL_WIDTH), F32)] + _gather_scratch([gate_up_shard]),
        compiler_params=_params(),
    )(proj, proj, proj, proj, proj, biasm, sinks, w_pool, pool_scale, gate_up_shard)


def outproj_norm(cat, w, x, g, g_next):
    t, d = x.shape
    tm = TOKEN_TILE

    def body(c_ref, w_ref, x_ref, g_ref, gn_ref, mix_ref, x1_ref, h2_ref):
        mix = _dot(c_ref[...], w_ref[...])
        mix_ref[...] = mix
        x1 = x_ref[...] + (mix * _rstd(mix)) * g_ref[...]
        x1_ref[...] = x1
        h2_ref[...] = ((x1 * _rstd(x1)) * gn_ref[...]).astype(BF16)

    row = pl.BlockSpec((tm, d), lambda i: (i, 0))
    gain = pl.BlockSpec((1, d), lambda i: (0, 0))
    return pl.pallas_call(
        body, name="outproj_norm", grid=(t // tm,),
        out_shape=[jax.ShapeDtypeStruct((t, d), F32), jax.ShapeDtypeStruct((t, d), F32), jax.ShapeDtypeStruct((t, d), BF16)],
        in_specs=[pl.BlockSpec((tm, cat.shape[1]), lambda i: (i, 0)), pl.BlockSpec(w.shape, lambda i: (0, 0)), row, gain, gain],
        out_specs=[row, row, row],
        compiler_params=_params(),
    )(cat, w, x, g, g_next)


def ffn_up(h, gate_up, down_shard):
    t, d = h.shape
    n = gate_up.shape[2]
    f = N_DEV * n
    tm, ts = FFN_TOKEN_TILE, FF_SHARDS_PER_TILE
    tn = ts * n
    steps = (f // tn, t // tm)

    def body(h_ref, wg_ref, wu_ref, shard_ref, gate_ref, up_ref, a_ref, gathered_ref,
             send_sems, recv_sems, local_sems, bounce):
        j, i = pl.program_id(0), pl.program_id(1)
        start, finish = _gather_plan([shard_ref], [gathered_ref], send_sems, recv_sems, local_sems, [bounce])
        pl.when((i == 0) & (j == 0))(start)

        hv = h_ref[...]
        gate = _dot_nt(hv, _merge_rows(wg_ref[...]))
        up = _dot_nt(hv, _merge_rows(wu_ref[...]))
        gate_ref[...] = gate.astype(BF16)
        up_ref[...] = up.astype(BF16)
        a_ref[...] = (gate * (1.0 / (1.0 + jnp.exp(-gate))) * up).astype(BF16)

        pl.when((j == steps[0] - 1) & (i == steps[1] - 1))(finish)

    wide = pl.BlockSpec((tm, tn), lambda j, i: (i, j))
    return pl.pallas_call(
        body, name="ffn_up", grid=steps,
        out_shape=[jax.ShapeDtypeStruct((t, f), BF16)] * 3
        + [jax.ShapeDtypeStruct((N_DEV, *down_shard.shape), down_shard.dtype)],
        in_specs=[pl.BlockSpec((tm, d), lambda j, i: (i, 0)),
                  pl.BlockSpec((ts, None, n, d), lambda j, i: (j, 0, 0, 0)),
                  pl.BlockSpec((ts, None, n, d), lambda j, i: (j, 1, 0, 0)), ANY],
        out_specs=[wide, wide, wide, ANY],
        scratch_shapes=_gather_scratch([down_shard]),
        compiler_params=_params(),
    )(h, gate_up, gate_up, down_shard)


def ffn_down_loss(a, w_down, x1, g, target):
    t, d = x1.shape
    tm = TOKEN_TILE

    def body(a_ref, w_ref, x_ref, g_ref, t_ref, df_ref, dy_ref, dg_ref, loss_ref):
        @pl.when(pl.program_id(0) == 0)
        def _():
            dg_ref[...] = jnp.zeros_like(dg_ref)
            loss_ref[...] = jnp.zeros_like(loss_ref)

        f = _dot(a_ref[...], _merge_rows(w_ref[...]))
        r = _rstd(f)
        g = g_ref[...]
        err = x_ref[...] + (f * r) * g - t_ref[...]
        loss_ref[...] += 0.5 * jnp.sum(jnp.mean(err * err, axis=-1, keepdims=True))
        dy = err * (1.0 / d)
        dy_ref[...] = dy
        df, dg_rows = _norm_bwd(dy, f, r, g)
        df_ref[...] = df.astype(BF16)
        dg_ref[...] += _as_rows(jnp.sum(dg_rows, axis=0, keepdims=True))

    row = pl.BlockSpec((tm, d), lambda i: (i, 0))
    gain = pl.BlockSpec((1, d), lambda i: (0, 0))
    return pl.pallas_call(
        body, name="ffn_down_loss", grid=(t // tm,),
        out_shape=[jax.ShapeDtypeStruct((t, d), BF16), jax.ShapeDtypeStruct((t, d), F32),
                   jax.ShapeDtypeStruct((d // 128, 128), F32), jax.ShapeDtypeStruct((1, 128), F32)],
        in_specs=[pl.BlockSpec((tm, a.shape[1]), lambda i: (i, 0)), pl.BlockSpec(w_down.shape, lambda i: (0, 0, 0)), row, gain, row],
        out_specs=[row, row, pl.BlockSpec((d // 128, 128), lambda i: (0, 0)), pl.BlockSpec((1, 128), lambda i: (0, 0))],
        compiler_params=_params(),
    )(a, w_down, x1, g, target)


def ffn_down_bwd(df, w_down, gate, up):
    t, d = df.shape
    n = w_down.shape[1]
    f = gate.shape[1]
    tm, ts = FFN_TOKEN_TILE, FF_SHARDS_PER_TILE
    tn = ts * n

    def body(df_ref, w_ref, gate_ref, up_ref, dgate_ref, dup_ref):
        da = _dot_nt(df_ref[...], _merge_rows(w_ref[...]))
        gate = gate_ref[...].astype(F32)
        sig = 1.0 / (1.0 + jnp.exp(-gate))
        dgate_ref[...] = (da * up_ref[...].astype(F32) * (sig * (1.0 + gate * (1.0 - sig)))).astype(BF16)
        dup_ref[...] = (da * (gate * sig)).astype(BF16)

    wide = pl.BlockSpec((tm, tn), lambda j, i: (i, j))
    return pl.pallas_call(
        body, name="ffn_down_bwd", grid=(f // tn, t // tm),
        out_shape=[jax.ShapeDtypeStruct((t, f), BF16)] * 2,
        in_specs=[pl.BlockSpec((tm, d), lambda j, i: (i, 0)), pl.BlockSpec((ts, n, d), lambda j, i: (j, 0, 0)), wide, wide],
        out_specs=[wide, wide],
        compiler_params=_params(),
    )(df, w_down, gate, up)


def grad_rows(a, b, name, by_core=False):
    t, m = a.shape
    d = b.shape[1]
    r = m // N_DEV
    tt = TOKEN_TILE
    last = t // tt - 1
    out_shape = (2, N_CHIP, r, d) if by_core else (N_DEV, r, d)

    def body(a_ref, b_ref, out_ref, acc):
        k = pl.program_id(0)

        @pl.when(k == 0)
        def _():
            acc[...] = jnp.zeros_like(acc)

        acc[...] += _dot_tn(a_ref[...], b_ref[...])

        @pl.when(k == last)
        def _():
            if by_core:
                blocks = acc[...].reshape(N_CHIP, 2, r, d)
                for chip in range(N_CHIP):
                    for core in range(2):
                        out_ref[core, chip] = blocks[chip, core].astype(BF16)
            else:
                out_ref[...] = acc[...].reshape(out_shape).astype(BF16)

    return pl.pallas_call(
        body, name=name, grid=(t // tt,),
        out_shape=jax.ShapeDtypeStruct(out_shape, BF16),
        in_specs=[pl.BlockSpec((tt, m), lambda k: (k, 0)), pl.BlockSpec((tt, d), lambda k: (k, 0))],
        out_specs=pl.BlockSpec(out_shape, lambda k: (0,) * len(out_shape)),
        scratch_shapes=[pltpu.VMEM((m, d), F32)],
        compiler_params=_params(),
    )(a, b)


def grad_ffn(lhs, b, name):
    t, f = lhs[0].shape
    d = b.shape[1]
    nw = len(lhs)
    n = f // N_DEV
    tt, ts = TOKEN_TILE, FF_SHARDS_PER_TILE
    tn = ts * n
    last = t // tt - 1

    def body(*refs):
        a_refs, b_ref, out_ref, acc = refs[:nw], refs[nw], refs[nw + 1], refs[nw + 2]
        k = pl.program_id(1)

        @pl.when(k == 0)
        def _():
            acc[...] = jnp.zeros_like(acc)

        for w in range(nw):
            acc[w] += _dot_tn(a_refs[w][...], b_ref[...])

        @pl.when(k == last)
        def _():
            for w in range(nw):
                blocks = acc[w].reshape(ts // 2, 2, n, d)
                for chip in range(ts // 2):
                    for core in range(2):
                        out_ref[core, chip, w] = blocks[chip, core].astype(BF16)

    return pl.pallas_call(
        body, name=name, grid=(f // tn, t // tt),
        out_shape=jax.ShapeDtypeStruct((2, N_CHIP, nw, n, d), BF16),
        in_specs=[pl.BlockSpec((tt, tn), lambda i, k: (k, i))] * nw + [pl.BlockSpec((tt, d), lambda i, k: (k, 0))],
        out_specs=pl.BlockSpec((2, ts // 2, nw, n, d), lambda i, k: (0, i, 0, 0, 0)),
        scratch_shapes=[pltpu.VMEM((nw, tn, d), F32)],
        compiler_params=_params(),
    )(*lhs, b)


def ffn_up_bwd(dgate, dup, gate_up, x1, g_ffn, dy, mix, g_mix):
    t, d = x1.shape
    n = gate_up.shape[2]
    f = N_DEV * n
    tm = TOKEN_TILE

    def body(dg_ref, du_ref, wg_ref, wu_ref, x_ref, gf_ref, dy_ref, mix_ref, gm_ref, dx1_ref, dmix_ref, dgf_ref, dgm_ref):
        @pl.when(pl.program_id(0) == 0)
        def _():
            dgf_ref[...] = jnp.zeros_like(dgf_ref)
            dgm_ref[...] = jnp.zeros_like(dgm_ref)

        dh = _dot(dg_ref[...], _merge_rows(wg_ref[...])) + _dot(du_ref[...], _merge_rows(wu_ref[...]))
        x1 = x_ref[...]
        dx, dgf_rows = _norm_bwd(dh, x1, _rstd(x1), gf_ref[...])
        dx1 = dy_ref[...] + dx
        dx1_ref[...] = dx1
        dgf_ref[...] += _as_rows(jnp.sum(dgf_rows, axis=0, keepdims=True))
        mix = mix_ref[...]
        dmix, dgm_rows = _norm_bwd(dx1, mix, _rstd(mix), gm_ref[...])
        dmix_ref[...] = dmix.astype(BF16)
        dgm_ref[...] += _as_rows(jnp.sum(dgm_rows, axis=0, keepdims=True))

    row = pl.BlockSpec((tm, d), lambda i: (i, 0))
    wide = pl.BlockSpec((tm, f), lambda i: (i, 0))
    gain = pl.BlockSpec((1, d), lambda i: (0, 0))
    gain_rows = pl.BlockSpec((d // 128, 128), lambda i: (0, 0))
    once = pl.Buffered(1)
    return pl.pallas_call(
        body, name="ffn_up_bwd", grid=(t // tm,),
        out_shape=[jax.ShapeDtypeStruct((t, d), F32), jax.ShapeDtypeStruct((t, d), BF16),
                   jax.ShapeDtypeStruct((d // 128, 128), F32), jax.ShapeDtypeStruct((d // 128, 128), F32)],
        in_specs=[wide, wide, pl.BlockSpec((N_DEV, None, n, d), lambda i: (0, 0, 0, 0), pipeline_mode=once),
                  pl.BlockSpec((N_DEV, None, n, d), lambda i: (0, 1, 0, 0), pipeline_mode=once), row, gain, row, row, gain],
        out_specs=[row, row, gain_rows, gain_rows],
        compiler_params=_params(),
    )(dgate, dup, gate_up, gate_up, x1, g_ffn, dy, mix, g_mix)


def outproj_bwd(dmix, w_out):
    t, d = dmix.shape
    tm = TOKEN_TILE

    def body(dm_ref, w_ref, out_ref):
        out_ref[...] = _dot_nt(dm_ref[...], w_ref[...])

    return pl.pallas_call(
        body, name="outproj_bwd", grid=(t // tm,),
        out_shape=jax.ShapeDtypeStruct((t, w_out.shape[0]), F32),
        in_specs=[pl.BlockSpec((tm, d), lambda i: (i, 0)), pl.BlockSpec(w_out.shape, lambda i: (0, 0))],
        out_specs=pl.BlockSpec((tm, w_out.shape[0]), lambda i: (i, 0)),
        compiler_params=_params(),
    )(dmix, w_out)


def mixers_bwd(proj, dcat, biasm, sinks, w_pool, pool_scale, ffn_parts):
    t = proj.shape[0]
    nb = t // BLOCK
    na = len(ffn_parts)

    def body(*refs):
        (uc_ref, halo_ref, q_ref, kvc_ref, kvp_ref, dcat_ref, biasm_ref, sinks_ref, wp_ref, sc_ref) = refs[:10]
        part_refs = refs[10:10 + na]
        dproj_ref, dbias_ref, dsink_ref, dwp_ref, dsc_ref = refs[10 + na:15 + na]
        slot_refs = refs[15 + na:15 + 2 * na]
        ubuf, dbuf, c_u, c_q, c_kv, send_sems, recv_sems, local_sems = refs[15 + 2 * na:23 + 2 * na]
        bounce = refs[23 + 2 * na:]
        i = pl.program_id(0)
        lane = lax.broadcasted_iota(jnp.int32, (1, 128), 1)
        start, finish = _chip_exchange_plan(part_refs, slot_refs, send_sems, recv_sems, local_sems, bounce)

        @pl.when(i == 0)
        def _():
            start()
            dbias_ref[...] = jnp.zeros_like(dbias_ref)
            dwp_ref[...] = jnp.zeros_like(dwp_ref)
            dsc_ref[...] = jnp.zeros_like(dsc_ref)
            dsink_ref[...] = jnp.zeros_like(dsink_ref)
            dbuf[...] = jnp.zeros_like(dbuf)
            c_u[...] = jnp.zeros_like(c_u)
            c_q[...] = jnp.zeros_like(c_q)
            c_kv[...] = jnp.zeros_like(c_kv)

        @pl.when(i < nb)
        def _():
            _fill_pool_input(i, ubuf, uc_ref, halo_ref)
            for g, w in enumerate(POOL_WINDOWS):
                cols = slice(g * POOL_GROUP_DIM, (g + 1) * POOL_GROUP_DIM)
                pooled = _pooled(i, g, w, ubuf).astype(BF16)
                mixed = _dot(pooled, wp_ref[g])
                dout = dcat_ref[:, cols]
                dsc_ref[g:g + 1, :] += jnp.sum(dout * mixed, axis=0, keepdims=True)
                dmixed = (dout * sc_ref[:, cols]).astype(BF16)
                dwp_ref[g] += _dot_tn(pooled, dmixed)
                dpooled = _dot_nt(dmixed, wp_ref[g])
                scaled = dpooled * _inv_count(i, w)
                dbuf[BLOCK:, cols] = scaled[0:HALO]
                dproj_ref[:, cols] = (_window_sum(dbuf, g, w, lambda k: k) + c_u[:, cols]).astype(BF16)
                dbuf[0:BLOCK, cols] = scaled
                c_u[:, cols] = -dpooled

            kv = jnp.concatenate([kvp_ref[...], kvc_ref[...]], axis=0)
            k_var = _head_variants(kv[:, 0:2 * HEAD_DIM])
            v_var = _head_variants(kv[:, 2 * HEAD_DIM:])
            q2s = [q_ref[:, 2 * HEAD_DIM * j:2 * HEAD_DIM * (j + 1)].astype(BF16) for j in range(N_Q_HEADS // 2)]
            do2s = [dcat_ref[:, POOL_WIDTH + 2 * HEAD_DIM * j:POOL_WIDTH + 2 * HEAD_DIM * (j + 1)].astype(BF16)
                    for j in range(N_Q_HEADS // 2)]
            dq2 = [None] * (N_Q_HEADS // 2)
            dss_of, pb_of = {}, {}
            dsink_row = jnp.zeros((1, 128), F32)
            for hq in range(N_Q_HEADS):
                j, half, h = hq // 2, hq % 2, hq // GQA_GROUP
                probs, p_sink = _head_probs(i, hq, q2s[j], k_var[h][half], biasm_ref, sinks_ref)
                dp = _dot_nt(do2s[j], v_var[h][half])
                delta = jnp.sum(probs * dp, axis=-1, keepdims=True)
                ds = probs * (dp - delta)
                dbias_ref[hq] += ds
                dsink_row = dsink_row - jnp.where(lane == hq, jnp.sum(p_sink * delta), 0.0)
                dss = (ds * ATTN_SCALE).astype(BF16)
                dss_of[hq], pb_of[hq] = dss, probs.astype(BF16)
                dq = _dot(dss, k_var[h][half])
                dq2[j] = dq if dq2[j] is None else dq2[j] + dq
            dsink_ref[...] += dsink_row
            low = lax.broadcasted_iota(jnp.int32, (2 * BLOCK, 2 * HEAD_DIM), 1) < HEAD_DIM
            dk_half, dv_half = [[None, None], [None, None]], [[None, None], [None, None]]
            for h in range(N_KV_HEADS):
                for half in range(2):
                    heads = [hq for hq in range(GQA_GROUP * h, GQA_GROUP * (h + 1)) if hq % 2 == half]
                    q_rows = jnp.concatenate([q2s[hq // 2] for hq in heads], axis=0)
                    do_rows = jnp.concatenate([do2s[hq // 2] for hq in heads], axis=0)
                    dk_half[h][half] = _dot_tn(jnp.concatenate([dss_of[hq] for hq in heads], axis=0), q_rows)
                    dv_half[h][half] = _dot_tn(jnp.concatenate([pb_of[hq] for hq in heads], axis=0), do_rows)

            def pair_of(halves):
                return jnp.where(low, halves[0][0] + pltpu.roll(halves[0][1], HEAD_DIM, 1),
                                 halves[1][1] + pltpu.roll(halves[1][0], HEAD_DIM, 1))

            dkv = jnp.concatenate([pair_of(dk_half), pair_of(dv_half)], axis=1)
            dproj_ref[:, POOL_WIDTH:2 * POOL_WIDTH] = c_q[...].astype(BF16)
            dproj_ref[:, 2 * POOL_WIDTH:] = (c_kv[...] + dkv[0:BLOCK]).astype(BF16)
            c_q[...] = jnp.concatenate(dq2, axis=1)
            c_kv[...] = dkv[BLOCK:]

        @pl.when(i == nb)
        def _():
            dbuf[BLOCK:, :] = jnp.zeros((HALO, POOL_WIDTH), F32)
            for g, w in enumerate(POOL_WINDOWS):
                cols = slice(g * POOL_GROUP_DIM, (g + 1) * POOL_GROUP_DIM)
                dproj_ref[:, cols] = (_window_sum(dbuf, g, w, lambda k: k) + c_u[:, cols]).astype(BF16)
            dproj_ref[:, POOL_WIDTH:2 * POOL_WIDTH] = c_q[...].astype(BF16)
            dproj_ref[:, 2 * POOL_WIDTH:] = c_kv[...].astype(BF16)
            finish()

    cur = lambda i: jnp.minimum(i, nb - 1)
    prv = lambda i: jnp.maximum(jnp.minimum(i, nb - 1) - 1, 0)
    out = pl.pallas_call(
        body, name="mixers_bwd", grid=(nb + 1,),
        out_shape=[jax.ShapeDtypeStruct((t, proj.shape[1]), BF16),
                   jax.ShapeDtypeStruct((N_Q_HEADS, BLOCK, 2 * BLOCK), F32),
                   jax.ShapeDtypeStruct((1, 128), F32),
                   jax.ShapeDtypeStruct((4, POOL_GROUP_DIM, POOL_GROUP_DIM), F32),
                   jax.ShapeDtypeStruct((len(POOL_WINDOWS), POOL_GROUP_DIM), F32)]
        + [jax.ShapeDtypeStruct(p.shape, p.dtype) for p in ffn_parts],
        in_specs=_mixer_in_specs(cur, prv) + [pl.BlockSpec((BLOCK, 2 * POOL_WIDTH), lambda i: (cur(i), 0))]
        + _mixer_param_specs() + [ANY] * na,
        out_specs=[pl.BlockSpec((BLOCK, proj.shape[1]), lambda i: (jnp.maximum(i - 1, 0), 0)),
                   pl.BlockSpec((N_Q_HEADS, BLOCK, 2 * BLOCK), lambda i: (0, 0, 0)),
                   pl.BlockSpec((1, 128), lambda i: (0, 0)),
                   pl.BlockSpec((4, POOL_GROUP_DIM, POOL_GROUP_DIM), lambda i: (0, 0, 0)),
                   pl.BlockSpec((len(POOL_WINDOWS), POOL_GROUP_DIM), lambda i: (0, 0))] + [ANY] * na,
        scratch_shapes=[pltpu.VMEM((HALO + BLOCK, POOL_WIDTH), F32), pltpu.VMEM((BLOCK + HALO, POOL_WIDTH), F32),
                        pltpu.VMEM((BLOCK, POOL_WIDTH), F32), pltpu.VMEM((BLOCK, POOL_WIDTH), F32),
                        pltpu.VMEM((BLOCK, 256), F32)] + _chip_exchange_scratch(ffn_parts),
        compiler_params=_params(),
    )(proj, proj, proj, proj, proj, dcat, biasm, sinks, w_pool, pool_scale, *ffn_parts)
    return out[:5], out[5:]


def inproj_bwd(dproj, w_in_t, x, g, dx1):
    t, d = x.shape
    n = dproj.shape[1]
    tm = TOKEN_TILE

    def body(dp_ref, w_ref, x_ref, g_ref, dx1_ref, dx_ref, dg_ref):
        @pl.when(pl.program_id(0) == 0)
        def _():
            dg_ref[...] = jnp.zeros_like(dg_ref)

        dh = _dot(dp_ref[...], w_ref[...])
        xv = x_ref[...]
        dx, dg_rows = _norm_bwd(dh, xv, _rstd(xv), g_ref[...])
        dx_ref[...] = dx1_ref[...] + dx
        dg_ref[...] += _as_rows(jnp.sum(dg_rows, axis=0, keepdims=True))

    row = pl.BlockSpec((tm, d), lambda i: (i, 0))
    gain = pl.BlockSpec((1, d), lambda i: (0, 0))
    return pl.pallas_call(
        body, name="inproj_bwd", grid=(t // tm,),
        out_shape=[jax.ShapeDtypeStruct((t, d), F32), jax.ShapeDtypeStruct((d // 128, 128), F32)],
        in_specs=[pl.BlockSpec((tm, n), lambda i: (i, 0)), pl.BlockSpec(w_in_t.shape, lambda i: (0, 0)), row, gain, row],
        out_specs=[row, pl.BlockSpec((d // 128, 128), lambda i: (0, 0))],
        compiler_params=_params(),
    )(dproj, w_in_t, x, g, dx1)


def _bucket_band():
    qi = jnp.arange(BLOCK)[:, None]
    kj = jnp.arange(2 * BLOCK)[None, :]
    dist = qi + BLOCK - kj
    n = jnp.maximum(dist, 0)
    nf = jnp.maximum(n, 1).astype(F32)
    large = MAX_EXACT + (jnp.log(nf / MAX_EXACT) / np.float32(np.log(MAX_DISTANCE / MAX_EXACT))
                         * (N_BUCKETS - MAX_EXACT)).astype(jnp.int32)
    large = jnp.minimum(large, N_BUCKETS - 1)
    bucket = jnp.where(n < MAX_EXACT, n, large)
    in_window = (dist >= 0) & (dist < BLOCK)
    return bucket.astype(F32), in_window.astype(F32)


def kernel(x, g_pre_mix, w_in, w_pool, pool_scale, rel_bias, sinks, w_out, g_post_mix, g_pre_ffn, w_gate, w_up, w_down, g_post_ffn, loss_target, m_g_pre_mix, m_w_in, m_w_pool, m_pool_scale, m_rel_bias, m_sinks, m_w_out, m_g_post_mix, m_g_pre_ffn, m_w_gate, m_w_up, m_w_down, m_g_post_ffn, v_g_pre_mix, v_w_in, v_w_pool, v_pool_scale, v_rel_bias, v_sinks, v_w_out, v_g_post_mix, v_g_pre_ffn, v_w_gate, v_w_up, v_w_down, v_g_post_ffn):
    d = x.shape[-1]
    xs, target = x[0], loss_target[0]

    w_in_ts = w_in[0].T.astype(BF16)
    w_out_s = w_out[0].astype(BF16)
    gate_up_s = jnp.stack([w_gate[0].T, w_up[0].T]).astype(BF16)
    w_down_s = w_down[0].astype(BF16)
    w_in_t, = gather_blocks([w_in_ts], "gather_w_in")
    w_in_t = w_in_t.reshape(-1, d)

    bucket, in_window = _bucket_band()
    biasm = bias_band(bucket, in_window, rel_bias)
    w_pool_b = w_pool[0].astype(BF16)
    proj, h1, w_out_f = norm_inproj(xs, g_pre_mix, w_in_t, w_out_s)
    w_out_f = w_out_f.reshape(-1, d)
    cat, gate_up = mixers_fwd(proj, biasm, sinks, w_pool_b, pool_scale, gate_up_s)
    mix, x1, h2 = outproj_norm(cat, w_out_f, xs, g_post_mix, g_pre_ffn)
    gate, up, act, w_down_f = ffn_up(h2, gate_up, w_down_s)
    df, dy, dg_post_ffn, loss_part = ffn_down_loss(act, w_down_f, x1, g_post_ffn, target)

    dgate, dup = ffn_down_bwd(df, w_down_f, gate, up)
    d_gate_up = grad_ffn([dgate, dup], h2, "grad_w_gate_up")
    d_down = grad_ffn([act], df, "grad_w_down")
    chip_parts = pair_add([d_gate_up, d_down], pair_exchange([d_gate_up, d_down], "pair_exchange_ffn"), "pair_add_ffn")
    dx1, dmix, dg_pre_ffn, dg_post_mix = ffn_up_bwd(dgate, dup, gate_up, x1, g_pre_ffn, dy, mix, g_post_mix)

    dcat = outproj_bwd(dmix, w_out_f)
    d_out = grad_rows(cat, dmix, "grad_w_out", by_core=True)
    chip_parts += pair_add([d_out], pair_exchange([d_out], "pair_exchange_out"), "pair_add_out")
    (dproj, dbias, dsinks, dw_pool, dpool_scale), slots = mixers_bwd(
        proj, dcat, biasm, sinks, w_pool_b, pool_scale, chip_parts)
    drel_bias = bias_band_bwd(bucket, dbias)
    grad_x, dg_pre_mix = inproj_bwd(dproj, w_in_t, xs, g_pre_mix, dx1)
    d_in_t = grad_rows(dproj, h1, "grad_w_in", by_core=True)

    small_w = [g_pre_mix, g_post_mix, g_pre_ffn, g_post_ffn, pool_scale, sinks, w_pool, rel_bias]
    small_m = [m_g_pre_mix, m_g_post_mix, m_g_pre_ffn, m_g_post_ffn, m_pool_scale, m_sinks, m_w_pool, m_rel_bias]
    small_v = [v_g_pre_mix, v_g_post_mix, v_g_pre_ffn, v_g_post_ffn, v_pool_scale, v_sinks, v_w_pool, v_rel_bias]
    g_in_t, loss_row, sm = tail_reduce(
        d_in_t, [dg_pre_mix, dg_post_mix, dg_pre_ffn, dg_post_ffn], dpool_scale, dsinks, loss_part, dw_pool, drel_bias,
        small_w, small_m, small_v)
    g_gate_up_t, g_down, g_out = sum_slots(slots, "sum_ffn")
    big_w = [w_in[0], w_out[0], w_gate[0], w_up[0], w_down[0]]
    big_g = [g_in_t.T, g_out, g_gate_up_t[0].T, g_gate_up_t[1].T, g_down[0]]
    big_m = [m_w_in[0], m_w_out[0], m_w_gate[0], m_w_up[0], m_w_down[0]]
    big_v = [v_w_in[0], v_w_out[0], v_w_gate[0], v_w_up[0], v_w_down[0]]
    upd = adamw_update(big_w[:2], big_g[:2], big_m[:2], big_v[:2], "adamw_mix") \
        + adamw_update(big_w[2:], big_g[2:], big_m[2:], big_v[2:], "adamw_ffn")
    big = [[big_g[k][None], *(u[None] for u in upd[k])] for k in range(5)]

    def ordered(kind):
        s, b = [p[kind] for p in sm], [p[kind] for p in big]
        return [s[0], b[0], s[6], s[4], s[7], s[5], b[1], s[1], s[2], b[2], b[3], b[4], s[3]]

    return (loss_row[0, 0], grad_x[None], *ordered(0), *ordered(1), *ordered(2), *ordered(3))
```

```python
import numpy as np
import jax
import jax.numpy as jnp
from jax import lax
from jax.experimental import pallas as pl
from jax.experimental.pallas import tpu as pltpu

F32 = jnp.float32
BF16 = jnp.bfloat16

N_DEV = 8
N_CHIP = 4
POOL_WIDTH = 512
POOL_WINDOWS = (2, 4, 8, 16)
POOL_GROUP_DIM = 128
HEAD_DIM = 64
N_Q_HEADS = 8
N_KV_HEADS = 2
GQA_GROUP = 4
BLOCK = 128
HALO = 16
ROW_CHUNK = 32
N_BUCKETS = 32
MAX_EXACT = 16
MAX_DISTANCE = 128
EPS = 1e-6
NEG_INF = -1e30
ATTN_SCALE = float(1.0 / np.sqrt(np.float32(HEAD_DIM)))

ADAM_LR = 0.001
ADAM_B1 = 0.9
ADAM_B2 = 0.999
ADAM_EPS = 1e-08
ADAM_WD = 0.01
ADAM_STEP = 10

TOKEN_TILE = 512
FFN_TOKEN_TILE = 1024
FF_SHARDS_PER_TILE = 4
VMEM_LIMIT = 56 * 1024 * 1024
MESH = pl.DeviceIdType.MESH
ANY = pl.BlockSpec(memory_space=pl.ANY)
VMEM = pl.BlockSpec(memory_space=pltpu.VMEM)
SMEM = pl.BlockSpec(memory_space=pltpu.SMEM)


def _params(**kw):
    return pltpu.CompilerParams(vmem_limit_bytes=VMEM_LIMIT, **kw)


def _dot(a, b):
    return jnp.dot(a, b, preferred_element_type=F32)


def _dot_nt(a, b):
    return lax.dot_general(a, b, (((1,), (1,)), ((), ())), preferred_element_type=F32)


def _dot_tn(a, b):
    return lax.dot_general(a, b, (((0,), (0,)), ((), ())), preferred_element_type=F32)


def _rstd(v):
    return lax.rsqrt(jnp.mean(v * v, axis=-1, keepdims=True) + EPS)


def _norm_bwd(dout, v, r, g):
    vn = v * r
    dn = dout * g
    dv = r * (dn - vn * jnp.mean(dn * vn, axis=-1, keepdims=True))
    return dv, dout * vn


def _as_rows(v):
    return jnp.concatenate([v[:, k:k + 128] for k in range(0, v.shape[1], 128)], axis=0)


def _as_lanes(rows):
    return jnp.concatenate([rows[k:k + 1, :] for k in range(rows.shape[0])], axis=1)


def _merge_rows(value):
    s, r, c_ = value.shape
    return value.reshape(s * r, c_)


def _gather_plan(srcs, outs, send_sems, recv_sems, local_sems=None, bounce=None):
    n = len(srcs)
    x, y, c = lax.axis_index("x"), lax.axis_index("y"), lax.axis_index("c")
    me, sibling = (x, y, c), (x, y, 1 - c)
    chips = [(1 - x, y), (x, 1 - y), (1 - x, 1 - y)]

    def slot(a, px, py, pc):
        return outs[a].at[4 * px + 2 * py + pc]

    def copy(a, k, block, to, from_src=False):
        return pltpu.make_async_remote_copy(
            src_ref=srcs[a] if from_src else slot(a, *block), dst_ref=slot(a, *block),
            send_sem=send_sems.at[k * n + a], recv_sem=recv_sems.at[k * n + a], device_id=to, device_id_type=MESH)

    def own_in(a):
        return pltpu.make_async_copy(srcs[a], bounce[a], local_sems.at[a])

    def own_out(a):
        return pltpu.make_async_copy(bounce[a], slot(a, *me), local_sems.at[a])

    def first(a):
        return [copy(a, 0, me, sibling, True)] + [copy(a, 1 + j, me, (*chip, c), True) for j, chip in enumerate(chips)]

    def passed(a, j):
        return copy(a, 4 + j, (*chips[j], c), sibling)

    def start():
        for a in range(n):
            if bounce is not None:
                own_in(a).start()
            for cp in first(a):
                cp.start()

    def finish():
        if bounce is not None:
            for a in range(n):
                own_in(a).wait()
                own_out(a).start()
        for j, chip in enumerate(chips):
            for a in range(n):
                copy(a, 1 + j, (*chip, c), me).wait_recv()
                passed(a, j).start()
        for a in range(n):
            copy(a, 0, sibling, me).wait_recv()
            for j, chip in enumerate(chips):
                copy(a, 4 + j, (*chip, 1 - c), me).wait_recv()
        for a in range(n):
            for cp in first(a) + [passed(a, j) for j in range(3)]:
                cp.wait_send()
            if bounce is not None:
                own_out(a).wait()

    return start, finish


def _gather_scratch(shards):
    n = len(shards)
    return [pltpu.SemaphoreType.DMA((7 * n,)), pltpu.SemaphoreType.DMA((7 * n,)), pltpu.SemaphoreType.DMA((n,))] \
        + [pltpu.VMEM(s.shape, s.dtype) for s in shards]


def _chip_exchange_plan(srcs, outs, send_sems, recv_sems, local_sems, bounce):
    n = len(srcs)
    x, y, c = lax.axis_index("x"), lax.axis_index("y"), lax.axis_index("c")
    my_chip = 2 * x + y

    def copies():
        out = []
        for a in range(n):
            for k in range(1, N_CHIP):
                px, py = x ^ (k >> 1), y ^ (k & 1)
                out.append(pltpu.make_async_remote_copy(
                    src_ref=srcs[a].at[2 * px + py], dst_ref=outs[a].at[my_chip],
                    send_sem=send_sems.at[(k - 1) * n + a], recv_sem=recv_sems.at[(k - 1) * n + a],
                    device_id=(px, py, c), device_id_type=MESH))
        return out

    def own_in(a):
        return pltpu.make_async_copy(srcs[a].at[my_chip], bounce[a], local_sems.at[a])

    def own_out(a):
        return pltpu.make_async_copy(bounce[a], outs[a].at[my_chip], local_sems.at[a])

    def start():
        for a in range(n):
            own_in(a).start()
        for cp in copies():
            cp.start()

    def finish():
        for a in range(n):
            own_in(a).wait()
            own_out(a).start()
        for cp in copies():
            cp.wait()
        for a in range(n):
            own_out(a).wait()

    return start, finish


def _chip_exchange_scratch(parts):
    n = len(parts)
    return [pltpu.SemaphoreType.DMA((3 * n,)), pltpu.SemaphoreType.DMA((3 * n,)), pltpu.SemaphoreType.DMA((n,))] \
        + [pltpu.VMEM(p.shape[1:], p.dtype) for p in parts]


def gather_blocks(shards, name):
    def body(*refs):
        n = len(shards)
        start, finish = _gather_plan(refs[:n], refs[n:2 * n], *refs[2 * n:2 * n + 3], bounce=refs[2 * n + 3:])
        start()
        finish()

    return pl.pallas_call(
        body, name=name,
        out_shape=[jax.ShapeDtypeStruct((N_DEV, *s.shape), s.dtype) for s in shards],
        in_specs=[ANY] * len(shards), out_specs=[ANY] * len(shards),
        scratch_shapes=_gather_scratch(shards),
    )(*shards)


def pair_exchange(parts, name):
    n = len(parts)

    def body(*refs):
        p_refs, got_refs, send_sems, recv_sems = refs[:n], refs[n:2 * n], *refs[2 * n:]
        x, y, c = lax.axis_index("x"), lax.axis_index("y"), lax.axis_index("c")
        copies = [pltpu.make_async_remote_copy(
            src_ref=p_refs[a].at[1 - c], dst_ref=got_refs[a], send_sem=send_sems.at[a], recv_sem=recv_sems.at[a],
            device_id=(x, y, 1 - c), device_id_type=MESH) for a in range(n)]
        for cp in copies:
            cp.start()
        for cp in copies:
            cp.wait()

    return pl.pallas_call(
        body, name=name, out_shape=[jax.ShapeDtypeStruct(p.shape[1:], p.dtype) for p in parts],
        in_specs=[ANY] * n, out_specs=[ANY] * n,
        scratch_shapes=[pltpu.SemaphoreType.DMA((n,)), pltpu.SemaphoreType.DMA((n,))],
    )(*parts)


def pair_add(parts, got, name):
    n = len(parts)

    def body(core_ref, *refs):
        for a in range(n):
            refs[2 * n + a][...] = (refs[a][...].astype(F32) + refs[n + a][...].astype(F32)).astype(BF16)

    def own(p):
        zeros = (0,) * (p.ndim - 2)
        return pl.BlockSpec((None, 1, *p.shape[2:]), lambda i, core: (core[0], i, *zeros))

    def plain(p):
        zeros = (0,) * (p.ndim - 1)
        return pl.BlockSpec((1, *p.shape[1:]), lambda i, core: (i, *zeros))

    core = lax.axis_index("c").astype(jnp.int32).reshape(1)
    return pl.pallas_call(
        body, name=name,
        grid_spec=pltpu.PrefetchScalarGridSpec(
            num_scalar_prefetch=1, grid=(got[0].shape[0],),
            in_specs=[own(p) for p in parts] + [plain(p) for p in got], out_specs=[plain(p) for p in got]),
        out_shape=[jax.ShapeDtypeStruct(p.shape, BF16) for p in got],
        compiler_params=_params(),
    )(core, *parts, *got)


def sum_slots(slots, name):
    n = len(slots)

    def body(*refs):
        for a in range(n):
            total = refs[a][0].astype(F32)
            for s in range(1, slots[a].shape[0]):
                total = total + refs[a][s].astype(F32)
            refs[n + a][...] = total

    return pl.pallas_call(
        body, name=name,
        out_shape=[jax.ShapeDtypeStruct(p.shape[1:], F32) for p in slots],
        in_specs=[VMEM] * n, out_specs=[VMEM] * n,
        compiler_params=_params(),
    )(*slots)


def _adamw(w, g, m, v):
    m2 = ADAM_B1 * m + (1.0 - ADAM_B1) * g
    v2 = ADAM_B2 * v + (1.0 - ADAM_B2) * (g * g)
    m_hat = m2 / (1.0 - ADAM_B1 ** ADAM_STEP)
    v_hat = v2 / (1.0 - ADAM_B2 ** ADAM_STEP)
    delta = -ADAM_LR * (m_hat / (jnp.sqrt(v_hat) + ADAM_EPS) + ADAM_WD * w)
    return delta, m2, v2


def adamw_update(ws, gs, ms, vs, name):
    n = len(ws)

    def body(*refs):
        for a in range(n):
            delta, m2, v2 = _adamw(refs[a][...], refs[n + a][...], refs[2 * n + a][...], refs[3 * n + a][...])
            refs[4 * n + 3 * a][...] = delta
            refs[4 * n + 3 * a + 1][...] = m2
            refs[4 * n + 3 * a + 2][...] = v2

    out = pl.pallas_call(
        body, name=name,
        out_shape=[jax.ShapeDtypeStruct(w.shape, F32) for w in ws for _ in range(3)],
        in_specs=[VMEM] * (4 * n), out_specs=[VMEM] * (3 * n),
        compiler_params=_params(),
    )(*ws, *gs, *ms, *vs)
    return [out[3 * a:3 * a + 3] for a in range(n)]


GAIN_ROWS = 8
ROW_POOL_SCALE = 4 * GAIN_ROWS
ROW_SINKS = ROW_POOL_SCALE + 4
ROW_LOSS = ROW_SINKS + 1
ROW_W_POOL = 40
SMALL_ROWS = ROW_W_POOL + 4 * POOL_GROUP_DIM


def tail_reduce(d_in_t, gains, dpool_scale, dsinks, loss_part, dw_pool, drel_bias, small_w, small_m, small_v):
    n_small = len(small_w)

    def body(*refs):
        d_in_ref, g_refs, (dsc_ref, dsink_ref, loss_ref, dwp_ref, drb_ref) = refs[0], refs[1:5], refs[5:10]
        w_refs, m_refs, v_refs = (refs[10 + k * n_small:10 + (k + 1) * n_small] for k in range(3))
        outs = refs[10 + 3 * n_small:]
        g_in_ref, loss_out = outs[0], outs[1]
        result = outs[2:2 + 4 * n_small]
        stage, gat, gat_rb, g_send, g_recv, pair_got, chip_part, chip_got, p_send, p_recv, x_send, x_recv = outs[2 + 4 * n_small:]
        x, y, c = lax.axis_index("x"), lax.axis_index("y"), lax.axis_index("c")
        my_id, my_chip = 4 * x + 2 * y + c, 2 * x + y

        for k in range(4):
            stage[GAIN_ROWS * k:GAIN_ROWS * (k + 1), :] = g_refs[k][...]
        stage[ROW_POOL_SCALE:ROW_SINKS, :] = dsc_ref[...]
        stage[ROW_SINKS:ROW_LOSS, :] = dsink_ref[...]
        stage[ROW_LOSS:ROW_LOSS + 1, :] = loss_ref[...]
        stage[ROW_LOSS + 1:ROW_W_POOL, :] = jnp.zeros((ROW_W_POOL - ROW_LOSS - 1, 128), F32)
        stage[ROW_W_POOL:, :] = dwp_ref[...].reshape(4 * POOL_GROUP_DIM, POOL_GROUP_DIM)
        gat[my_id] = stage[...]
        gat_rb[my_id] = drb_ref[...]
        start, finish = _gather_plan([stage, drb_ref], [gat, gat_rb], g_send, g_recv)
        start()

        pair = pltpu.make_async_remote_copy(
            src_ref=d_in_ref.at[1 - c], dst_ref=pair_got, send_sem=p_send, recv_sem=p_recv,
            device_id=(x, y, 1 - c), device_id_type=MESH)
        pair.start()
        pair.wait()
        chip_part[...] = (d_in_ref[c].astype(F32) + pair_got[...].astype(F32)).astype(BF16)
        copies = []
        for k in range(1, N_CHIP):
            px, py = x ^ (k >> 1), y ^ (k & 1)
            copies.append(pltpu.make_async_remote_copy(
                src_ref=chip_part.at[2 * px + py], dst_ref=chip_got.at[my_chip],
                send_sem=x_send.at[k - 1], recv_sem=x_recv.at[k - 1], device_id=(px, py, c), device_id_type=MESH))
        for cp in copies:
            cp.start()
        chip_got[my_chip] = chip_part[my_chip]

        finish()
        total, total_rb = gat[0], gat_rb[0]
        for s in range(1, N_DEV):
            total, total_rb = total + gat[s], total_rb + gat_rb[s]
        loss_out[...] = total[ROW_LOSS:ROW_LOSS + 1, :]
        grads = [_as_lanes(total[GAIN_ROWS * k:GAIN_ROWS * (k + 1), :]) for k in range(4)]
        grads.append(_as_lanes(total[ROW_POOL_SCALE:ROW_SINKS, :]))
        grads.append(total[ROW_SINKS:ROW_LOSS, 0:N_Q_HEADS])
        grads.append(total[ROW_W_POOL:, :].reshape(w_refs[6].shape))
        grads.append(total_rb)
        for k in range(n_small):
            delta, m2, v2 = _adamw(w_refs[k][...], grads[k], m_refs[k][...], v_refs[k][...])
            result[4 * k][...] = grads[k]
            result[4 * k + 1][...] = delta
            result[4 * k + 2][...] = m2
            result[4 * k + 3][...] = v2

        for cp in copies:
            cp.wait()
        g_in = chip_got[0].astype(F32)
        for s in range(1, N_CHIP):
            g_in = g_in + chip_got[s].astype(F32)
        g_in_ref[...] = g_in

    n_in = 10 + 3 * n_small
    per_core = d_in_t.shape[1:]
    out = pl.pallas_call(
        body, name="tail_reduce",
        out_shape=[jax.ShapeDtypeStruct(d_in_t.shape[2:], F32), jax.ShapeDtypeStruct((1, 128), F32)]
        + [jax.ShapeDtypeStruct(w.shape, F32) for w in small_w for _ in range(4)],
        in_specs=[VMEM] * n_in, out_specs=[VMEM] * (2 + 4 * n_small),
        scratch_shapes=[pltpu.VMEM((SMALL_ROWS, 128), F32), pltpu.VMEM((N_DEV, SMALL_ROWS, 128), F32),
                        pltpu.VMEM((N_DEV, *drel_bias.shape), F32),
                        pltpu.SemaphoreType.DMA((14,)), pltpu.SemaphoreType.DMA((14,)),
                        pltpu.VMEM(per_core, d_in_t.dtype), pltpu.VMEM(per_core, d_in_t.dtype),
                        pltpu.VMEM(per_core, d_in_t.dtype),
                        pltpu.SemaphoreType.DMA, pltpu.SemaphoreType.DMA,
                        pltpu.SemaphoreType.DMA((3,)), pltpu.SemaphoreType.DMA((3,))],
        compiler_params=_params(),
    )(d_in_t, *gains, dpool_scale, dsinks, loss_part, dw_pool, drel_bias, *small_w, *small_m, *small_v)
    return out[0], out[1], [out[2 + 4 * k:6 + 4 * k] for k in range(n_small)]


def norm_inproj(x, g, w_t, out_shard):
    t, d = x.shape
    n = w_t.shape[0]
    tm = TOKEN_TILE
    last = t // tm - 1

    def body(x_ref, g_ref, w_ref, shard_ref, proj_ref, h_ref, gathered_ref, send_sems, recv_sems, local_sems, bounce):
        i = pl.program_id(0)
        start, finish = _gather_plan([shard_ref], [gathered_ref], send_sems, recv_sems, local_sems, [bounce])
        pl.when(i == 0)(start)
        xv = x_ref[...]
        h = ((xv * _rstd(xv)) * g_ref[...]).astype(BF16)
        h_ref[...] = h
        proj_ref[...] = _dot_nt(h, w_ref[...])
        pl.when(i == last)(finish)

    return pl.pallas_call(
        body, name="norm_inproj", grid=(t // tm,),
        out_shape=[jax.ShapeDtypeStruct((t, n), F32), jax.ShapeDtypeStruct((t, d), BF16),
                   jax.ShapeDtypeStruct((N_DEV, *out_shard.shape), out_shard.dtype)],
        in_specs=[pl.BlockSpec((tm, d), lambda i: (i, 0)), pl.BlockSpec((1, d), lambda i: (0, 0)),
                  pl.BlockSpec((n, d), lambda i: (0, 0)), ANY],
        out_specs=[pl.BlockSpec((tm, n), lambda i: (i, 0)), pl.BlockSpec((tm, d), lambda i: (i, 0)), ANY],
        scratch_shapes=_gather_scratch([out_shard]),
        compiler_params=_params(),
    )(x, g, w_t, out_shard)


def bias_band(bucket, in_window, rel_bias):
    def body(bk_ref, win_ref, rb_ref, out_ref):
        bk = bk_ref[...]
        keep = win_ref[...] > 0.5
        for h in range(N_Q_HEADS):
            acc = jnp.zeros(bk.shape, F32)
            for b in range(N_BUCKETS):
                acc = jnp.where(bk == float(b), rb_ref[b, h], acc)
            out_ref[h] = jnp.where(keep, acc, NEG_INF)

    return pl.pallas_call(
        body, name="bias_band",
        out_shape=jax.ShapeDtypeStruct((N_Q_HEADS, BLOCK, 2 * BLOCK), F32),
        in_specs=[VMEM, VMEM, SMEM], out_specs=VMEM,
    )(bucket, in_window, rel_bias)


def bias_band_bwd(bucket, dbias):
    def body(bk_ref, db_ref, out_ref):
        bk = bk_ref[...]
        for h in range(N_Q_HEADS):
            db = db_ref[h]
            for b in range(N_BUCKETS):
                out_ref[b, h] = jnp.sum(jnp.where(bk == float(b), db, 0.0))

    return pl.pallas_call(
        body, name="bias_band_bwd",
        out_shape=jax.ShapeDtypeStruct((N_BUCKETS, N_Q_HEADS), F32),
        in_specs=[VMEM, VMEM], out_specs=SMEM,
    )(bucket, dbias)


def _window_sum(buf_ref, g, w, first):
    cols = slice(g * POOL_GROUP_DIM, (g + 1) * POOL_GROUP_DIM)
    acc = None
    for k in range(w):
        piece = buf_ref[first(k):first(k) + BLOCK, cols]
        acc = piece if acc is None else acc + piece
    return acc


def _inv_count(i, w):
    row = lax.broadcasted_iota(jnp.int32, (BLOCK, 1), 0)
    return 1.0 / jnp.minimum(i * BLOCK + row + 1, w).astype(F32)


def _fill_pool_input(i, ubuf, uc_ref, halo_ref):
    ubuf[0:HALO, :] = jnp.where(i > 0, halo_ref[...], 0.0)
    ubuf[HALO:, :] = uc_ref[...]


def _pooled(i, g, w, ubuf):
    cols = slice(g * POOL_GROUP_DIM, (g + 1) * POOL_GROUP_DIM)
    return _window_sum(ubuf, g, w, lambda k: HALO - k) * _inv_count(i, w) - ubuf[HALO:, cols]


def _head_variants(pair):
    low = lax.broadcasted_iota(jnp.int32, pair.shape, 1) < HEAD_DIM
    swapped = pltpu.roll(pair, HEAD_DIM, 1)
    zero = jnp.zeros_like(pair)
    pick = lambda c, a, b: jnp.where(c, a, b).astype(BF16)
    return [[pick(low, pair, zero), pick(low, zero, swapped)], [pick(low, swapped, zero), pick(low, zero, pair)]]


def _head_probs(i, hq, rows, s_ref, biasm_ref, sinks_ref):
    s = s_ref[hq, rows, :] * ATTN_SCALE + biasm_ref[hq, rows, :]
    col = lax.broadcasted_iota(jnp.int32, s.shape, 1)
    s = jnp.where((i == 0) & (col < BLOCK), NEG_INF, s)
    sink = sinks_ref[0, hq]
    m = jnp.maximum(jnp.max(s, axis=-1, keepdims=True), sink)
    p = jnp.exp(s - m)
    e_sink = jnp.exp(sink - m)
    inv = 1.0 / (jnp.sum(p, axis=-1, keepdims=True) + e_sink)
    return p * inv, e_sink * inv


def _mixer_in_specs(cur, prv):
    return [pl.BlockSpec((BLOCK, 512), lambda i: (cur(i), 0)),
            pl.BlockSpec((HALO, 512), lambda i: (jnp.maximum(cur(i) * (BLOCK // HALO) - 1, 0), 0)),
            pl.BlockSpec((BLOCK, 512), lambda i: (cur(i), 1)),
            pl.BlockSpec((BLOCK, 256), lambda i: (cur(i), 4)),
            pl.BlockSpec((BLOCK, 256), lambda i: (prv(i), 4))]


def _mixer_param_specs():
    return [pl.BlockSpec((N_Q_HEADS, BLOCK, 2 * BLOCK), lambda i: (0, 0, 0)), SMEM,
            pl.BlockSpec((4, POOL_GROUP_DIM, POOL_GROUP_DIM), lambda i: (0, 0, 0)),
            pl.BlockSpec((1, POOL_WIDTH), lambda i: (0, 0))]


def mixers_fwd(proj, biasm, sinks, w_pool, pool_scale, gate_up_shard):
    t = proj.shape[0]
    nb = t // BLOCK

    def body(uc_ref, halo_ref, q_ref, kvc_ref, kvp_ref, biasm_ref, sinks_ref, wp_ref, sc_ref, shard_ref,
             out_ref, gathered_ref, ubuf, s_all, p_all, send_sems, recv_sems, local_sems, bounce):
        i = pl.program_id(0)
        start, finish = _gather_plan([shard_ref], [gathered_ref], send_sems, recv_sems, local_sems, [bounce])
        pl.when(i == 0)(start)

        _fill_pool_input(i, ubuf, uc_ref, halo_ref)
        for g, w in enumerate(POOL_WINDOWS):
            mixed = _dot(_pooled(i, g, w, ubuf).astype(BF16), wp_ref[g])
            cols = slice(g * POOL_GROUP_DIM, (g + 1) * POOL_GROUP_DIM)
            out_ref[:, cols] = (mixed * sc_ref[:, cols]).astype(BF16)
        kv = jnp.concatenate([kvp_ref[...], kvc_ref[...]], axis=0)
        k_var = _head_variants(kv[:, 0:2 * HEAD_DIM])
        v_var = _head_variants(kv[:, 2 * HEAD_DIM:])
        for hq in range(N_Q_HEADS):
            j, half, h = hq // 2, hq % 2, hq // GQA_GROUP
            q2 = q_ref[:, 2 * HEAD_DIM * j:2 * HEAD_DIM * (j + 1)].astype(BF16)
            s_all[hq] = _dot_nt(q2, k_var[h][half])
        for hq in range(N_Q_HEADS):
            for r in range(0, BLOCK, ROW_CHUNK):
                rows = slice(r, r + ROW_CHUNK)
                probs, _ = _head_probs(i, hq, rows, s_all, biasm_ref, sinks_ref)
                p_all[hq, rows, :] = probs.astype(BF16)
        for j in range(N_Q_HEADS // 2):
            h = 2 * j // GQA_GROUP
            acc = _dot(p_all[2 * j], v_var[h][0]) + _dot(p_all[2 * j + 1], v_var[h][1])
            out_ref[:, POOL_WIDTH + 2 * HEAD_DIM * j:POOL_WIDTH + 2 * HEAD_DIM * (j + 1)] = acc.astype(BF16)

        pl.when(i == nb - 1)(finish)

    return pl.pallas_call(
        body, name="mixers_fwd", grid=(nb,),
        out_shape=[jax.ShapeDtypeStruct((t, 2 * POOL_WIDTH), BF16),
                   jax.ShapeDtypeStruct((N_DEV, *gate_up_shard.shape), gate_up_shard.dtype)],
        in_specs=_mixer_in_specs(lambda i: i, lambda i: jnp.maximum(i - 1, 0)) + _mixer_param_specs() + [ANY],
        out_specs=[pl.BlockSpec((BLOCK, 2 * POOL_WIDTH), lambda i: (i, 0)), ANY],
        scratch_shapes=[pltpu.VMEM((HALO + BLOCK, POOL_WIDTH), F32), pltpu.VMEM((N_Q_HEADS, BLOCK, 2 * BLOCK), F32),
                        pltpu.VMEM((N_Q_HEADS, BLOCK, 2 * BLOCK), BF16)] + _gather_scratch([gate_up_shard]),
        compiler_params=_params(),
    )(proj, proj, proj, proj, proj, biasm, sinks, w_pool, pool_scale, gate_up_shard)


def outproj_norm(cat, w, x, g, g_next):
    t, d = x.shape
    tm = TOKEN_TILE

    def body(c_ref, w_ref, x_ref, g_ref, gn_ref, mix_ref, x1_ref, h2_ref):
        mix = _dot(c_ref[...], w_ref[...])
        mix_ref[...] = mix
        x1 = x_ref[...] + (mix * _rstd(mix)) * g_ref[...]
        x1_ref[...] = x1
        h2_ref[...] = ((x1 * _rstd(x1)) * gn_ref[...]).astype(BF16)

    row = pl.BlockSpec((tm, d), lambda i: (i, 0))
    gain = pl.BlockSpec((1, d), lambda i: (0, 0))
    return pl.pallas_call(
        body, name="outproj_norm", grid=(t // tm,),
        out_shape=[jax.ShapeDtypeStruct((t, d), F32), jax.ShapeDtypeStruct((t, d), F32), jax.ShapeDtypeStruct((t, d), BF16)],
        in_specs=[pl.BlockSpec((tm, cat.shape[1]), lambda i: (i, 0)), pl.BlockSpec(w.shape, lambda i: (0, 0)), row, gain, gain],
        out_specs=[row, row, row],
        compiler_params=_params(),
    )(cat, w, x, g, g_next)


def ffn_up(h, gate_up, down_shard):
    t, d = h.shape
    n = gate_up.shape[2]
    f = N_DEV * n
    tm, ts = FFN_TOKEN_TILE, FF_SHARDS_PER_TILE
    tn = ts * n
    steps = (f // tn, t // tm)

    def body(h_ref, wg_ref, wu_ref, shard_ref, gate_ref, up_ref, a_ref, gathered_ref,
             send_sems, recv_sems, local_sems, bounce):
        j, i = pl.program_id(0), pl.program_id(1)
        start, finish = _gather_plan([shard_ref], [gathered_ref], send_sems, recv_sems, local_sems, [bounce])
        pl.when((i == 0) & (j == 0))(start)

        hv = h_ref[...]
        gate = _dot_nt(hv, _merge_rows(wg_ref[...]))
        up = _dot_nt(hv, _merge_rows(wu_ref[...]))
        gate_ref[...] = gate.astype(BF16)
        up_ref[...] = up.astype(BF16)
        a_ref[...] = (gate * (1.0 / (1.0 + jnp.exp(-gate))) * up).astype(BF16)

        pl.when((j == steps[0] - 1) & (i == steps[1] - 1))(finish)

    wide = pl.BlockSpec((tm, tn), lambda j, i: (i, j))
    return pl.pallas_call(
        body, name="ffn_up", grid=steps,
        out_shape=[jax.ShapeDtypeStruct((t, f), BF16)] * 3
        + [jax.ShapeDtypeStruct((N_DEV, *down_shard.shape), down_shard.dtype)],
        in_specs=[pl.BlockSpec((tm, d), lambda j, i: (i, 0)),
                  pl.BlockSpec((ts, None, n, d), lambda j, i: (j, 0, 0, 0)),
                  pl.BlockSpec((ts, None, n, d), lambda j, i: (j, 1, 0, 0)), ANY],
        out_specs=[wide, wide, wide, ANY],
        scratch_shapes=_gather_scratch([down_shard]),
        compiler_params=_params(),
    )(h, gate_up, gate_up, down_shard)


def ffn_down_loss(a, w_down, x1, g, target):
    t, d = x1.shape
    tm = TOKEN_TILE

    def body(a_ref, w_ref, x_ref, g_ref, t_ref, df_ref, dy_ref, dg_ref, loss_ref):
        @pl.when(pl.program_id(0) == 0)
        def _():
            dg_ref[...] = jnp.zeros_like(dg_ref)
            loss_ref[...] = jnp.zeros_like(loss_ref)

        f = _dot(a_ref[...], _merge_rows(w_ref[...]))
        r = _rstd(f)
        g = g_ref[...]
        err = x_ref[...] + (f * r) * g - t_ref[...]
        loss_ref[...] += 0.5 * jnp.sum(jnp.mean(err * err, axis=-1, keepdims=True))
        dy = err * (1.0 / d)
        dy_ref[...] = dy
        df, dg_rows = _norm_bwd(dy, f, r, g)
        df_ref[...] = df.astype(BF16)
        dg_ref[...] += _as_rows(jnp.sum(dg_rows, axis=0, keepdims=True))

    row = pl.BlockSpec((tm, d), lambda i: (i, 0))
    gain = pl.BlockSpec((1, d), lambda i: (0, 0))
    return pl.pallas_call(
        body, name="ffn_down_loss", grid=(t // tm,),
        out_shape=[jax.ShapeDtypeStruct((t, d), BF16), jax.ShapeDtypeStruct((t, d), F32),
                   jax.ShapeDtypeStruct((d // 128, 128), F32), jax.ShapeDtypeStruct((1, 128), F32)],
        in_specs=[pl.BlockSpec((tm, a.shape[1]), lambda i: (i, 0)), pl.BlockSpec(w_down.shape, lambda i: (0, 0, 0)), row, gain, row],
        out_specs=[row, row, pl.BlockSpec((d // 128, 128), lambda i: (0, 0)), pl.BlockSpec((1, 128), lambda i: (0, 0))],
        compiler_params=_params(),
    )(a, w_down, x1, g, target)


def ffn_down_bwd(df, w_down, gate, up):
    t, d = df.shape
    n = w_down.shape[1]
    f = gate.shape[1]
    tm, ts = FFN_TOKEN_TILE, FF_SHARDS_PER_TILE
    tn = ts * n

    def body(df_ref, w_ref, gate_ref, up_ref, dgate_ref, dup_ref):
        da = _dot_nt(df_ref[...], _merge_rows(w_ref[...]))
        gate = gate_ref[...].astype(F32)
        sig = 1.0 / (1.0 + jnp.exp(-gate))
        dgate_ref[...] = (da * up_ref[...].astype(F32) * (sig * (1.0 + gate * (1.0 - sig)))).astype(BF16)
        dup_ref[...] = (da * (gate * sig)).astype(BF16)

    wide = pl.BlockSpec((tm, tn), lambda j, i: (i, j))
    return pl.pallas_call(
        body, name="ffn_down_bwd", grid=(f // tn, t // tm),
        out_shape=[jax.ShapeDtypeStruct((t, f), BF16)] * 2,
        in_specs=[pl.BlockSpec((tm, d), lambda j, i: (i, 0)), pl.BlockSpec((ts, n, d), lambda j, i: (j, 0, 0)), wide, wide],
        out_specs=[wide, wide],
        compiler_params=_params(),
    )(df, w_down, gate, up)


def grad_rows(a, b, name, by_core=False):
    t, m = a.shape
    d = b.shape[1]
    r = m // N_DEV
    tt = TOKEN_TILE
    last = t // tt - 1
    out_shape = (2, N_CHIP, r, d) if by_core else (N_DEV, r, d)

    def body(a_ref, b_ref, out_ref, acc):
        k = pl.program_id(0)

        @pl.when(k == 0)
        def _():
            acc[...] = jnp.zeros_like(acc)

        acc[...] += _dot_tn(a_ref[...], b_ref[...])

        @pl.when(k == last)
        def _():
            if by_core:
                blocks = acc[...].reshape(N_CHIP, 2, r, d)
                for chip in range(N_CHIP):
                    for core in range(2):
                        out_ref[core, chip] = blocks[chip, core].astype(BF16)
            else:
                out_ref[...] = acc[...].reshape(out_shape).astype(BF16)

    return pl.pallas_call(
        body, name=name, grid=(t // tt,),
        out_shape=jax.ShapeDtypeStruct(out_shape, BF16),
        in_specs=[pl.BlockSpec((tt, m), lambda k: (k, 0)), pl.BlockSpec((tt, d), lambda k: (k, 0))],
        out_specs=pl.BlockSpec(out_shape, lambda k: (0,) * len(out_shape)),
        scratch_shapes=[pltpu.VMEM((m, d), F32)],
        compiler_params=_params(),
    )(a, b)


def grad_ffn(lhs, b, name):
    t, f = lhs[0].shape
    d = b.shape[1]
    nw = len(lhs)
    n = f // N_DEV
    tt, ts = TOKEN_TILE, FF_SHARDS_PER_TILE
    tn = ts * n
    last = t // tt - 1

    def body(*refs):
        a_refs, b_ref, out_ref, acc = refs[:nw], refs[nw], refs[nw + 1], refs[nw + 2]
        k = pl.program_id(1)

        @pl.when(k == 0)
        def _():
            acc[...] = jnp.zeros_like(acc)

        for w in range(nw):
            acc[w] += _dot_tn(a_refs[w][...], b_ref[...])

        @pl.when(k == last)
        def _():
            for w in range(nw):
                blocks = acc[w].reshape(ts // 2, 2, n, d)
                for chip in range(ts // 2):
                    for core in range(2):
                        out_ref[core, chip, w] = blocks[chip, core].astype(BF16)

    return pl.pallas_call(
        body, name=name, grid=(f // tn, t // tt),
        out_shape=jax.ShapeDtypeStruct((2, N_CHIP, nw, n, d), BF16),
        in_specs=[pl.BlockSpec((tt, tn), lambda i, k: (k, i))] * nw + [pl.BlockSpec((tt, d), lambda i, k: (k, 0))],
        out_specs=pl.BlockSpec((2, ts // 2, nw, n, d), lambda i, k: (0, i, 0, 0, 0)),
        scratch_shapes=[pltpu.VMEM((nw, tn, d), F32)],
        compiler_params=_params(),
    )(*lhs, b)


def ffn_up_bwd(dgate, dup, gate_up, x1, g_ffn, dy, mix, g_mix):
    t, d = x1.shape
    n = gate_up.shape[2]
    f = N_DEV * n
    tm = TOKEN_TILE

    def body(dg_ref, du_ref, wg_ref, wu_ref, x_ref, gf_ref, dy_ref, mix_ref, gm_ref, dx1_ref, dmix_ref, dgf_ref, dgm_ref):
        @pl.when(pl.program_id(0) == 0)
        def _():
            dgf_ref[...] = jnp.zeros_like(dgf_ref)
            dgm_ref[...] = jnp.zeros_like(dgm_ref)

        dh = _dot(dg_ref[...], _merge_rows(wg_ref[...])) + _dot(du_ref[...], _merge_rows(wu_ref[...]))
        x1 = x_ref[...]
        dx, dgf_rows = _norm_bwd(dh, x1, _rstd(x1), gf_ref[...])
        dx1 = dy_ref[...] + dx
        dx1_ref[...] = dx1
        dgf_ref[...] += _as_rows(jnp.sum(dgf_rows, axis=0, keepdims=True))
        mix = mix_ref[...]
        dmix, dgm_rows = _norm_bwd(dx1, mix, _rstd(mix), gm_ref[...])
        dmix_ref[...] = dmix.astype(BF16)
        dgm_ref[...] += _as_rows(jnp.sum(dgm_rows, axis=0, keepdims=True))

    row = pl.BlockSpec((tm, d), lambda i: (i, 0))
    wide = pl.BlockSpec((tm, f), lambda i: (i, 0))
    gain = pl.BlockSpec((1, d), lambda i: (0, 0))
    gain_rows = pl.BlockSpec((d // 128, 128), lambda i: (0, 0))
    once = pl.Buffered(1)
    return pl.pallas_call(
        body, name="ffn_up_bwd", grid=(t // tm,),
        out_shape=[jax.ShapeDtypeStruct((t, d), F32), jax.ShapeDtypeStruct((t, d), BF16),
                   jax.ShapeDtypeStruct((d // 128, 128), F32), jax.ShapeDtypeStruct((d // 128, 128), F32)],
        in_specs=[wide, wide, pl.BlockSpec((N_DEV, None, n, d), lambda i: (0, 0, 0, 0), pipeline_mode=once),
                  pl.BlockSpec((N_DEV, None, n, d), lambda i: (0, 1, 0, 0), pipeline_mode=once), row, gain, row, row, gain],
        out_specs=[row, row, gain_rows, gain_rows],
        compiler_params=_params(),
    )(dgate, dup, gate_up, gate_up, x1, g_ffn, dy, mix, g_mix)


def outproj_bwd(dmix, w_out):
    t, d = dmix.shape
    tm = TOKEN_TILE

    def body(dm_ref, w_ref, out_ref):
        out_ref[...] = _dot_nt(dm_ref[...], w_ref[...])

    return pl.pallas_call(
        body, name="outproj_bwd", grid=(t // tm,),
        out_shape=jax.ShapeDtypeStruct((t, w_out.shape[0]), F32),
        in_specs=[pl.BlockSpec((tm, d), lambda i: (i, 0)), pl.BlockSpec(w_out.shape, lambda i: (0, 0))],
        out_specs=pl.BlockSpec((tm, w_out.shape[0]), lambda i: (i, 0)),
        compiler_params=_params(),
    )(dmix, w_out)


def mixers_bwd(proj, dcat, biasm, sinks, w_pool, pool_scale, ffn_parts):
    t = proj.shape[0]
    nb = t // BLOCK
    na = len(ffn_parts)

    def body(*refs):
        (uc_ref, halo_ref, q_ref, kvc_ref, kvp_ref, dcat_ref, biasm_ref, sinks_ref, wp_ref, sc_ref) = refs[:10]
        part_refs = refs[10:10 + na]
        dproj_ref, dbias_ref, dsink_ref, dwp_ref, dsc_ref = refs[10 + na:15 + na]
        slot_refs = refs[15 + na:15 + 2 * na]
        ubuf, dbuf, c_u, c_q, c_kv, s_all, dp_all, ds_all, p_all = refs[15 + 2 * na:24 + 2 * na]
        send_sems, recv_sems, local_sems = refs[24 + 2 * na:27 + 2 * na]
        bounce = refs[27 + 2 * na:]
        i = pl.program_id(0)
        lane = lax.broadcasted_iota(jnp.int32, (1, 128), 1)
        start, finish = _chip_exchange_plan(part_refs, slot_refs, send_sems, recv_sems, local_sems, bounce)

        @pl.when(i == 0)
        def _():
            start()
            dbias_ref[...] = jnp.zeros_like(dbias_ref)
            dwp_ref[...] = jnp.zeros_like(dwp_ref)
            dsc_ref[...] = jnp.zeros_like(dsc_ref)
            dsink_ref[...] = jnp.zeros_like(dsink_ref)
            dbuf[...] = jnp.zeros_like(dbuf)
            c_u[...] = jnp.zeros_like(c_u)
            c_q[...] = jnp.zeros_like(c_q)
            c_kv[...] = jnp.zeros_like(c_kv)

        @pl.when(i < nb)
        def _():
            _fill_pool_input(i, ubuf, uc_ref, halo_ref)
            for g, w in enumerate(POOL_WINDOWS):
                cols = slice(g * POOL_GROUP_DIM, (g + 1) * POOL_GROUP_DIM)
                pooled = _pooled(i, g, w, ubuf).astype(BF16)
                mixed = _dot(pooled, wp_ref[g])
                dout = dcat_ref[:, cols]
                dsc_ref[g:g + 1, :] += jnp.sum(dout * mixed, axis=0, keepdims=True)
                dmixed = (dout * sc_ref[:, cols]).astype(BF16)
                dwp_ref[g] += _dot_tn(pooled, dmixed)
                dpooled = _dot_nt(dmixed, wp_ref[g])
                scaled = dpooled * _inv_count(i, w)
                dbuf[BLOCK:, cols] = scaled[0:HALO]
                dproj_ref[:, cols] = (_window_sum(dbuf, g, w, lambda k: k) + c_u[:, cols]).astype(BF16)
                dbuf[0:BLOCK, cols] = scaled
                c_u[:, cols] = -dpooled

            kv = jnp.concatenate([kvp_ref[...], kvc_ref[...]], axis=0)
            k_var = _head_variants(kv[:, 0:2 * HEAD_DIM])
            v_var = _head_variants(kv[:, 2 * HEAD_DIM:])
            q2s = [q_ref[:, 2 * HEAD_DIM * j:2 * HEAD_DIM * (j + 1)].astype(BF16) for j in range(N_Q_HEADS // 2)]
            do2s = [dcat_ref[:, POOL_WIDTH + 2 * HEAD_DIM * j:POOL_WIDTH + 2 * HEAD_DIM * (j + 1)].astype(BF16)
                    for j in range(N_Q_HEADS // 2)]
            slot = lambda hq: 4 * (hq // GQA_GROUP) + 2 * (hq % 2) + (hq % GQA_GROUP) // 2
            for hq in range(N_Q_HEADS):
                j, half, h = hq // 2, hq % 2, hq // GQA_GROUP
                s_all[hq] = _dot_nt(q2s[j], k_var[h][half])
                dp_all[hq] = _dot_nt(do2s[j], v_var[h][half])
            dsink_row = jnp.zeros((1, 128), F32)
            for hq in range(N_Q_HEADS):
                dsink = 0.0
                for r in range(0, BLOCK, ROW_CHUNK):
                    rows = slice(r, r + ROW_CHUNK)
                    probs, p_sink = _head_probs(i, hq, rows, s_all, biasm_ref, sinks_ref)
                    dp = dp_all[hq, rows, :]
                    delta = jnp.sum(probs * dp, axis=-1, keepdims=True)
                    ds = probs * (dp - delta)
                    dbias_ref[hq, rows, :] += ds
                    dsink = dsink + jnp.sum(p_sink * delta)
                    ds_all[slot(hq), rows, :] = (ds * ATTN_SCALE).astype(BF16)
                    p_all[slot(hq), rows, :] = probs.astype(BF16)
                dsink_row = dsink_row - jnp.where(lane == hq, dsink, 0.0)
            dsink_ref[...] += dsink_row
            dq2 = [None] * (N_Q_HEADS // 2)
            for hq in range(N_Q_HEADS):
                j, half, h = hq // 2, hq % 2, hq // GQA_GROUP
                dq = _dot(ds_all[slot(hq)], k_var[h][half])
                dq2[j] = dq if dq2[j] is None else dq2[j] + dq
            low = lax.broadcasted_iota(jnp.int32, (2 * BLOCK, 2 * HEAD_DIM), 1) < HEAD_DIM
            dk_half, dv_half = [[None, None], [None, None]], [[None, None], [None, None]]
            for h in range(N_KV_HEADS):
                for half in range(2):
                    heads = [hq for hq in range(GQA_GROUP * h, GQA_GROUP * (h + 1)) if hq % 2 == half]
                    base = slot(heads[0])
                    q_rows = jnp.concatenate([q2s[hq // 2] for hq in heads], axis=0)
                    do_rows = jnp.concatenate([do2s[hq // 2] for hq in heads], axis=0)
                    dk_half[h][half] = _dot_tn(_merge_rows(ds_all[base:base + 2]), q_rows)
                    dv_half[h][half] = _dot_tn(_merge_rows(p_all[base:base + 2]), do_rows)

            def pair_of(halves):
                return jnp.where(low, halves[0][0] + pltpu.roll(halves[0][1], HEAD_DIM, 1),
                                 halves[1][1] + pltpu.roll(halves[1][0], HEAD_DIM, 1))

            dkv = jnp.concatenate([pair_of(dk_half), pair_of(dv_half)], axis=1)
            dproj_ref[:, POOL_WIDTH:2 * POOL_WIDTH] = c_q[...].astype(BF16)
            dproj_ref[:, 2 * POOL_WIDTH:] = (c_kv[...] + dkv[0:BLOCK]).astype(BF16)
            c_q[...] = jnp.concatenate(dq2, axis=1)
            c_kv[...] = dkv[BLOCK:]

        @pl.when(i == nb)
        def _():
            dbuf[BLOCK:, :] = jnp.zeros((HALO, POOL_WIDTH), F32)
            for g, w in enumerate(POOL_WINDOWS):
                cols = slice(g * POOL_GROUP_DIM, (g + 1) * POOL_GROUP_DIM)
                dproj_ref[:, cols] = (_window_sum(dbuf, g, w, lambda k: k) + c_u[:, cols]).astype(BF16)
            dproj_ref[:, POOL_WIDTH:2 * POOL_WIDTH] = c_q[...].astype(BF16)
            dproj_ref[:, 2 * POOL_WIDTH:] = c_kv[...].astype(BF16)
            finish()

    cur = lambda i: jnp.minimum(i, nb - 1)
    prv = lambda i: jnp.maximum(jnp.minimum(i, nb - 1) - 1, 0)
    out = pl.pallas_call(
        body, name="mixers_bwd", grid=(nb + 1,),
        out_shape=[jax.ShapeDtypeStruct((t, proj.shape[1]), BF16),
                   jax.ShapeDtypeStruct((N_Q_HEADS, BLOCK, 2 * BLOCK), F32),
                   jax.ShapeDtypeStruct((1, 128), F32),
                   jax.ShapeDtypeStruct((4, POOL_GROUP_DIM, POOL_GROUP_DIM), F32),
                   jax.ShapeDtypeStruct((len(POOL_WINDOWS), POOL_GROUP_DIM), F32)]
        + [jax.ShapeDtypeStruct(p.shape, p.dtype) for p in ffn_parts],
        in_specs=_mixer_in_specs(cur, prv) + [pl.BlockSpec((BLOCK, 2 * POOL_WIDTH), lambda i: (cur(i), 0))]
        + _mixer_param_specs() + [ANY] * na,
        out_specs=[pl.BlockSpec((BLOCK, proj.shape[1]), lambda i: (jnp.maximum(i - 1, 0), 0)),
                   pl.BlockSpec((N_Q_HEADS, BLOCK, 2 * BLOCK), lambda i: (0, 0, 0)),
                   pl.BlockSpec((1, 128), lambda i: (0, 0)),
                   pl.BlockSpec((4, POOL_GROUP_DIM, POOL_GROUP_DIM), lambda i: (0, 0, 0)),
                   pl.BlockSpec((len(POOL_WINDOWS), POOL_GROUP_DIM), lambda i: (0, 0))] + [ANY] * na,
        scratch_shapes=[pltpu.VMEM((HALO + BLOCK, POOL_WIDTH), F32), pltpu.VMEM((BLOCK + HALO, POOL_WIDTH), F32),
                        pltpu.VMEM((BLOCK, POOL_WIDTH), F32), pltpu.VMEM((BLOCK, POOL_WIDTH), F32),
                        pltpu.VMEM((BLOCK, 256), F32),
                        pltpu.VMEM((N_Q_HEADS, BLOCK, 2 * BLOCK), F32), pltpu.VMEM((N_Q_HEADS, BLOCK, 2 * BLOCK), F32),
                        pltpu.VMEM((N_Q_HEADS, BLOCK, 2 * BLOCK), BF16), pltpu.VMEM((N_Q_HEADS, BLOCK, 2 * BLOCK), BF16)]
        + _chip_exchange_scratch(ffn_parts),
        compiler_params=_params(),
    )(proj, proj, proj, proj, proj, dcat, biasm, sinks, w_pool, pool_scale, *ffn_parts)
    return out[:5], out[5:]


def inproj_bwd(dproj, w_in_t, x, g, dx1):
    t, d = x.shape
    n = dproj.shape[1]
    tm = TOKEN_TILE

    def body(dp_ref, w_ref, x_ref, g_ref, dx1_ref, dx_ref, dg_ref):
        @pl.when(pl.program_id(0) == 0)
        def _():
            dg_ref[...] = jnp.zeros_like(dg_ref)

        dh = _dot(dp_ref[...], w_ref[...])
        xv = x_ref[...]
        dx, dg_rows = _norm_bwd(dh, xv, _rstd(xv), g_ref[...])
        dx_ref[...] = dx1_ref[...] + dx
        dg_ref[...] += _as_rows(jnp.sum(dg_rows, axis=0, keepdims=True))

    row = pl.BlockSpec((tm, d), lambda i: (i, 0))
    gain = pl.BlockSpec((1, d), lambda i: (0, 0))
    return pl.pallas_call(
        body, name="inproj_bwd", grid=(t // tm,),
        out_shape=[jax.ShapeDtypeStruct((t, d), F32), jax.ShapeDtypeStruct((d // 128, 128), F32)],
        in_specs=[pl.BlockSpec((tm, n), lambda i: (i, 0)), pl.BlockSpec(w_in_t.shape, lambda i: (0, 0)), row, gain, row],
        out_specs=[row, pl.BlockSpec((d // 128, 128), lambda i: (0, 0))],
        compiler_params=_params(),
    )(dproj, w_in_t, x, g, dx1)


def _bucket_band():
    qi = jnp.arange(BLOCK)[:, None]
    kj = jnp.arange(2 * BLOCK)[None, :]
    dist = qi + BLOCK - kj
    n = jnp.maximum(dist, 0)
    nf = jnp.maximum(n, 1).astype(F32)
    large = MAX_EXACT + (jnp.log(nf / MAX_EXACT) / np.float32(np.log(MAX_DISTANCE / MAX_EXACT))
                         * (N_BUCKETS - MAX_EXACT)).astype(jnp.int32)
    large = jnp.minimum(large, N_BUCKETS - 1)
    bucket = jnp.where(n < MAX_EXACT, n, large)
    in_window = (dist >= 0) & (dist < BLOCK)
    return bucket.astype(F32), in_window.astype(F32)


def kernel(x, g_pre_mix, w_in, w_pool, pool_scale, rel_bias, sinks, w_out, g_post_mix, g_pre_ffn, w_gate, w_up, w_down, g_post_ffn, loss_target, m_g_pre_mix, m_w_in, m_w_pool, m_pool_scale, m_rel_bias, m_sinks, m_w_out, m_g_post_mix, m_g_pre_ffn, m_w_gate, m_w_up, m_w_down, m_g_post_ffn, v_g_pre_mix, v_w_in, v_w_pool, v_pool_scale, v_rel_bias, v_sinks, v_w_out, v_g_post_mix, v_g_pre_ffn, v_w_gate, v_w_up, v_w_down, v_g_post_ffn):
    d = x.shape[-1]
    xs, target = x[0], loss_target[0]

    w_in_ts = w_in[0].T.astype(BF16)
    w_out_s = w_out[0].astype(BF16)
    gate_up_s = jnp.stack([w_gate[0].T, w_up[0].T]).astype(BF16)
    w_down_s = w_down[0].astype(BF16)
    w_in_t, = gather_blocks([w_in_ts], "gather_w_in")
    w_in_t = w_in_t.reshape(-1, d)

    bucket, in_window = _bucket_band()
    biasm = bias_band(bucket, in_window, rel_bias)
    w_pool_b = w_pool[0].astype(BF16)
    proj, h1, w_out_f = norm_inproj(xs, g_pre_mix, w_in_t, w_out_s)
    w_out_f = w_out_f.reshape(-1, d)
    cat, gate_up = mixers_fwd(proj, biasm, sinks, w_pool_b, pool_scale, gate_up_s)
    mix, x1, h2 = outproj_norm(cat, w_out_f, xs, g_post_mix, g_pre_ffn)
    gate, up, act, w_down_f = ffn_up(h2, gate_up, w_down_s)
    df, dy, dg_post_ffn, loss_part = ffn_down_loss(act, w_down_f, x1, g_post_ffn, target)

    dgate, dup = ffn_down_bwd(df, w_down_f, gate, up)
    d_gate_up = grad_ffn([dgate, dup], h2, "grad_w_gate_up")
    d_down = grad_ffn([act], df, "grad_w_down")
    chip_parts = pair_add([d_gate_up, d_down], pair_exchange([d_gate_up, d_down], "pair_exchange_ffn"), "pair_add_ffn")
    dx1, dmix, dg_pre_ffn, dg_post_mix = ffn_up_bwd(dgate, dup, gate_up, x1, g_pre_ffn, dy, mix, g_post_mix)

    dcat = outproj_bwd(dmix, w_out_f)
    d_out = grad_rows(cat, dmix, "grad_w_out", by_core=True)
    chip_parts += pair_add([d_out], pair_exchange([d_out], "pair_exchange_out"), "pair_add_out")
    (dproj, dbias, dsinks, dw_pool, dpool_scale), slots = mixers_bwd(
        proj, dcat, biasm, sinks, w_pool_b, pool_scale, chip_parts)
    drel_bias = bias_band_bwd(bucket, dbias)
    grad_x, dg_pre_mix = inproj_bwd(dproj, w_in_t, xs, g_pre_mix, dx1)
    d_in_t = grad_rows(dproj, h1, "grad_w_in", by_core=True)

    small_w = [g_pre_mix, g_post_mix, g_pre_ffn, g_post_ffn, pool_scale, sinks, w_pool, rel_bias]
    small_m = [m_g_pre_mix, m_g_post_mix, m_g_pre_ffn, m_g_post_ffn, m_pool_scale, m_sinks, m_w_pool, m_rel_bias]
    small_v = [v_g_pre_mix, v_g_post_mix, v_g_pre_ffn, v_g_post_ffn, v_pool_scale, v_sinks, v_w_pool, v_rel_bias]
    g_in_t, loss_row, sm = tail_reduce(
        d_in_t, [dg_pre_mix, dg_post_mix, dg_pre_ffn, dg_post_ffn], dpool_scale, dsinks, loss_part, dw_pool, drel_bias,
        small_w, small_m, small_v)
    g_gate_up_t, g_down, g_out = sum_slots(slots, "sum_ffn")
    big_w = [w_in[0], w_out[0], w_gate[0], w_up[0], w_down[0]]
    big_g = [g_in_t.T, g_out, g_gate_up_t[0].T, g_gate_up_t[1].T, g_down[0]]
    big_m = [m_w_in[0], m_w_out[0], m_w_gate[0], m_w_up[0], m_w_down[0]]
    big_v = [v_w_in[0], v_w_out[0], v_w_gate[0], v_w_up[0], v_w_down[0]]
    upd = adamw_update(big_w[:2], big_g[:2], big_m[:2], big_v[:2], "adamw_mix") \
        + adamw_update(big_w[2:], big_g[2:], big_m[2:], big_v[2:], "adamw_ffn")
    big = [[big_g[k][None], *(u[None] for u in upd[k])] for k in range(5)]

    def ordered(kind):
        s, b = [p[kind] for p in sm], [p[kind] for p in big]
        return [s[0], b[0], s[6], s[4], s[7], s[5], b[1], s[1], s[2], b[2], b[3], b[4], s[3]]

    return (loss_row[0, 0], grad_x[None], *ordered(0), *ordered(1), *ordered(2), *ordered(3))
```

```python
import numpy as np
import jax
import jax.numpy as jnp
from jax import lax
from jax.experimental import pallas as pl
from jax.experimental.pallas import tpu as pltpu

F32 = jnp.float32
BF16 = jnp.bfloat16

N_DEV = 8
N_CHIP = 4
POOL_WIDTH = 512
POOL_WINDOWS = (2, 4, 8, 16)
POOL_GROUP_DIM = 128
HEAD_DIM = 64
N_Q_HEADS = 8
N_KV_HEADS = 2
GQA_GROUP = 4
BLOCK = 128
HALO = 16
ROW_CHUNK = 32
N_BUCKETS = 32
MAX_EXACT = 16
MAX_DISTANCE = 128
EPS = 1e-6
NEG_INF = -1e30
ATTN_SCALE = float(1.0 / np.sqrt(np.float32(HEAD_DIM)))

ADAM_LR = 0.001
ADAM_B1 = 0.9
ADAM_B2 = 0.999
ADAM_EPS = 1e-08
ADAM_WD = 0.01
ADAM_STEP = 10

TOKEN_TILE = 512
FFN_TOKEN_TILE = 1024
FF_SHARDS_PER_TILE = 4
VMEM_LIMIT = 56 * 1024 * 1024
MESH = pl.DeviceIdType.MESH
ANY = pl.BlockSpec(memory_space=pl.ANY)
VMEM = pl.BlockSpec(memory_space=pltpu.VMEM)
SMEM = pl.BlockSpec(memory_space=pltpu.SMEM)


def _params(**kw):
    return pltpu.CompilerParams(vmem_limit_bytes=VMEM_LIMIT, **kw)


def _dot(a, b):
    return jnp.dot(a, b, preferred_element_type=F32)


def _dot_nt(a, b):
    return lax.dot_general(a, b, (((1,), (1,)), ((), ())), preferred_element_type=F32)


def _dot_tn(a, b):
    return lax.dot_general(a, b, (((0,), (0,)), ((), ())), preferred_element_type=F32)


def _rstd(v):
    return lax.rsqrt(jnp.mean(v * v, axis=-1, keepdims=True) + EPS)


def _norm_bwd(dout, v, r, g):
    vn = v * r
    dn = dout * g
    dv = r * (dn - vn * jnp.mean(dn * vn, axis=-1, keepdims=True))
    return dv, dout * vn


def _as_rows(v):
    return jnp.concatenate([v[:, k:k + 128] for k in range(0, v.shape[1], 128)], axis=0)


def _as_lanes(rows):
    return jnp.concatenate([rows[k:k + 1, :] for k in range(rows.shape[0])], axis=1)


def _merge_rows(value):
    s, r, c_ = value.shape
    return value.reshape(s * r, c_)


def _gather_plan(srcs, outs, send_sems, recv_sems, local_sems=None, bounce=None):
    n = len(srcs)
    x, y, c = lax.axis_index("x"), lax.axis_index("y"), lax.axis_index("c")
    me, sibling = (x, y, c), (x, y, 1 - c)
    chips = [(1 - x, y), (x, 1 - y), (1 - x, 1 - y)]

    def slot(a, px, py, pc):
        return outs[a].at[4 * px + 2 * py + pc]

    def copy(a, k, block, to, from_src=False):
        return pltpu.make_async_remote_copy(
            src_ref=srcs[a] if from_src else slot(a, *block), dst_ref=slot(a, *block),
            send_sem=send_sems.at[k * n + a], recv_sem=recv_sems.at[k * n + a], device_id=to, device_id_type=MESH)

    def own_in(a):
        return pltpu.make_async_copy(srcs[a], bounce[a], local_sems.at[a])

    def own_out(a):
        return pltpu.make_async_copy(bounce[a], slot(a, *me), local_sems.at[a])

    def first(a):
        return [copy(a, 0, me, sibling, True)] + [copy(a, 1 + j, me, (*chip, c), True) for j, chip in enumerate(chips)]

    def passed(a, j):
        return copy(a, 4 + j, (*chips[j], c), sibling)

    def start():
        for a in range(n):
            if bounce is not None:
                own_in(a).start()
            for cp in first(a):
                cp.start()

    def finish():
        if bounce is not None:
            for a in range(n):
                own_in(a).wait()
                own_out(a).start()
        for j, chip in enumerate(chips):
            for a in range(n):
                copy(a, 1 + j, (*chip, c), me).wait_recv()
                passed(a, j).start()
        for a in range(n):
            copy(a, 0, sibling, me).wait_recv()
            for j, chip in enumerate(chips):
                copy(a, 4 + j, (*chip, 1 - c), me).wait_recv()
        for a in range(n):
            for cp in first(a) + [passed(a, j) for j in range(3)]:
                cp.wait_send()
            if bounce is not None:
                own_out(a).wait()

    return start, finish


def _gather_scratch(shards):
    n = len(shards)
    return [pltpu.SemaphoreType.DMA((7 * n,)), pltpu.SemaphoreType.DMA((7 * n,)), pltpu.SemaphoreType.DMA((n,))] \
        + [pltpu.VMEM(s.shape, s.dtype) for s in shards]


def _chip_exchange_plan(srcs, outs, send_sems, recv_sems, local_sems, bounce):
    n = len(srcs)
    x, y, c = lax.axis_index("x"), lax.axis_index("y"), lax.axis_index("c")
    my_chip = 2 * x + y

    def copies():
        out = []
        for a in range(n):
            for k in range(1, N_CHIP):
                px, py = x ^ (k >> 1), y ^ (k & 1)
                out.append(pltpu.make_async_remote_copy(
                    src_ref=srcs[a].at[2 * px + py], dst_ref=outs[a].at[my_chip],
                    send_sem=send_sems.at[(k - 1) * n + a], recv_sem=recv_sems.at[(k - 1) * n + a],
                    device_id=(px, py, c), device_id_type=MESH))
        return out

    def own_in(a):
        return pltpu.make_async_copy(srcs[a].at[my_chip], bounce[a], local_sems.at[a])

    def own_out(a):
        return pltpu.make_async_copy(bounce[a], outs[a].at[my_chip], local_sems.at[a])

    def start():
        for a in range(n):
            own_in(a).start()
        for cp in copies():
            cp.start()

    def finish():
        for a in range(n):
            own_in(a).wait()
            own_out(a).start()
        for cp in copies():
            cp.wait()
        for a in range(n):
            own_out(a).wait()

    return start, finish


def _chip_exchange_scratch(parts):
    n = len(parts)
    return [pltpu.SemaphoreType.DMA((3 * n,)), pltpu.SemaphoreType.DMA((3 * n,)), pltpu.SemaphoreType.DMA((n,))] \
        + [pltpu.VMEM(p.shape[1:], p.dtype) for p in parts]


def gather_blocks(shards, name):
    def body(*refs):
        n = len(shards)
        start, finish = _gather_plan(refs[:n], refs[n:2 * n], *refs[2 * n:2 * n + 3], bounce=refs[2 * n + 3:])
        start()
        finish()

    return pl.pallas_call(
        body, name=name,
        out_shape=[jax.ShapeDtypeStruct((N_DEV, *s.shape), s.dtype) for s in shards],
        in_specs=[ANY] * len(shards), out_specs=[ANY] * len(shards),
        scratch_shapes=_gather_scratch(shards),
    )(*shards)


def pair_exchange(parts, name):
    n = len(parts)

    def body(*refs):
        p_refs, got_refs, send_sems, recv_sems = refs[:n], refs[n:2 * n], *refs[2 * n:]
        x, y, c = lax.axis_index("x"), lax.axis_index("y"), lax.axis_index("c")
        copies = [pltpu.make_async_remote_copy(
            src_ref=p_refs[a].at[1 - c], dst_ref=got_refs[a], send_sem=send_sems.at[a], recv_sem=recv_sems.at[a],
            device_id=(x, y, 1 - c), device_id_type=MESH) for a in range(n)]
        for cp in copies:
            cp.start()
        for cp in copies:
            cp.wait()

    return pl.pallas_call(
        body, name=name, out_shape=[jax.ShapeDtypeStruct(p.shape[1:], p.dtype) for p in parts],
        in_specs=[ANY] * n, out_specs=[ANY] * n,
        scratch_shapes=[pltpu.SemaphoreType.DMA((n,)), pltpu.SemaphoreType.DMA((n,))],
    )(*parts)


def pair_add(parts, got, name):
    n = len(parts)

    def body(core_ref, *refs):
        for a in range(n):
            refs[2 * n + a][...] = (refs[a][...].astype(F32) + refs[n + a][...].astype(F32)).astype(BF16)

    def own(p):
        zeros = (0,) * (p.ndim - 2)
        return pl.BlockSpec((None, 1, *p.shape[2:]), lambda i, core: (core[0], i, *zeros))

    def plain(p):
        zeros = (0,) * (p.ndim - 1)
        return pl.BlockSpec((1, *p.shape[1:]), lambda i, core: (i, *zeros))

    core = lax.axis_index("c").astype(jnp.int32).reshape(1)
    return pl.pallas_call(
        body, name=name,
        grid_spec=pltpu.PrefetchScalarGridSpec(
            num_scalar_prefetch=1, grid=(got[0].shape[0],),
            in_specs=[own(p) for p in parts] + [plain(p) for p in got], out_specs=[plain(p) for p in got]),
        out_shape=[jax.ShapeDtypeStruct(p.shape, BF16) for p in got],
        compiler_params=_params(),
    )(core, *parts, *got)


def sum_slots(slots, name):
    n = len(slots)

    def body(*refs):
        for a in range(n):
            total = refs[a][0].astype(F32)
            for s in range(1, slots[a].shape[0]):
                total = total + refs[a][s].astype(F32)
            refs[n + a][...] = total

    return pl.pallas_call(
        body, name=name,
        out_shape=[jax.ShapeDtypeStruct(p.shape[1:], F32) for p in slots],
        in_specs=[VMEM] * n, out_specs=[VMEM] * n,
        compiler_params=_params(),
    )(*slots)


def _adamw(w, g, m, v):
    m2 = ADAM_B1 * m + (1.0 - ADAM_B1) * g
    v2 = ADAM_B2 * v + (1.0 - ADAM_B2) * (g * g)
    m_hat = m2 / (1.0 - ADAM_B1 ** ADAM_STEP)
    v_hat = v2 / (1.0 - ADAM_B2 ** ADAM_STEP)
    delta = -ADAM_LR * (m_hat / (jnp.sqrt(v_hat) + ADAM_EPS) + ADAM_WD * w)
    return delta, m2, v2


def adamw_update(ws, gs, ms, vs, name):
    n = len(ws)

    def body(*refs):
        for a in range(n):
            delta, m2, v2 = _adamw(refs[a][...], refs[n + a][...], refs[2 * n + a][...], refs[3 * n + a][...])
            refs[4 * n + 3 * a][...] = delta
            refs[4 * n + 3 * a + 1][...] = m2
            refs[4 * n + 3 * a + 2][...] = v2

    out = pl.pallas_call(
        body, name=name,
        out_shape=[jax.ShapeDtypeStruct(w.shape, F32) for w in ws for _ in range(3)],
        in_specs=[VMEM] * (4 * n), out_specs=[VMEM] * (3 * n),
        compiler_params=_params(),
    )(*ws, *gs, *ms, *vs)
    return [out[3 * a:3 * a + 3] for a in range(n)]


GAIN_ROWS = 8
ROW_POOL_SCALE = 4 * GAIN_ROWS
ROW_SINKS = ROW_POOL_SCALE + 4
ROW_LOSS = ROW_SINKS + 1
ROW_W_POOL = 40
SMALL_ROWS = ROW_W_POOL + 4 * POOL_GROUP_DIM


def tail_reduce(d_in_t, gains, dpool_scale, dsinks, loss_part, dw_pool, drel_bias, small_w, small_m, small_v):
    n_small = len(small_w)

    def body(*refs):
        d_in_ref, g_refs, (dsc_ref, dsink_ref, loss_ref, dwp_ref, drb_ref) = refs[0], refs[1:5], refs[5:10]
        w_refs, m_refs, v_refs = (refs[10 + k * n_small:10 + (k + 1) * n_small] for k in range(3))
        outs = refs[10 + 3 * n_small:]
        g_in_ref, loss_out = outs[0], outs[1]
        result = outs[2:2 + 4 * n_small]
        stage, gat, gat_rb, g_send, g_recv, pair_got, chip_part, chip_got, p_send, p_recv, x_send, x_recv = outs[2 + 4 * n_small:]
        x, y, c = lax.axis_index("x"), lax.axis_index("y"), lax.axis_index("c")
        my_id, my_chip = 4 * x + 2 * y + c, 2 * x + y

        for k in range(4):
            stage[GAIN_ROWS * k:GAIN_ROWS * (k + 1), :] = g_refs[k][...]
        stage[ROW_POOL_SCALE:ROW_SINKS, :] = dsc_ref[...]
        stage[ROW_SINKS:ROW_LOSS, :] = dsink_ref[...]
        stage[ROW_LOSS:ROW_LOSS + 1, :] = loss_ref[...]
        stage[ROW_LOSS + 1:ROW_W_POOL, :] = jnp.zeros((ROW_W_POOL - ROW_LOSS - 1, 128), F32)
        stage[ROW_W_POOL:, :] = dwp_ref[...].reshape(4 * POOL_GROUP_DIM, POOL_GROUP_DIM)
        gat[my_id] = stage[...]
        gat_rb[my_id] = drb_ref[...]
        start, finish = _gather_plan([stage, drb_ref], [gat, gat_rb], g_send, g_recv)
        start()

        pair = pltpu.make_async_remote_copy(
            src_ref=d_in_ref.at[1 - c], dst_ref=pair_got, send_sem=p_send, recv_sem=p_recv,
            device_id=(x, y, 1 - c), device_id_type=MESH)
        pair.start()
        pair.wait()
        chip_part[...] = (d_in_ref[c].astype(F32) + pair_got[...].astype(F32)).astype(BF16)
        copies = []
        for k in range(1, N_CHIP):
            px, py = x ^ (k >> 1), y ^ (k & 1)
            copies.append(pltpu.make_async_remote_copy(
                src_ref=chip_part.at[2 * px + py], dst_ref=chip_got.at[my_chip],
                send_sem=x_send.at[k - 1], recv_sem=x_recv.at[k - 1], device_id=(px, py, c), device_id_type=MESH))
        for cp in copies:
            cp.start()
        chip_got[my_chip] = chip_part[my_chip]

        finish()
        total, total_rb = gat[0], gat_rb[0]
        for s in range(1, N_DEV):
            total, total_rb = total + gat[s], total_rb + gat_rb[s]
        loss_out[...] = total[ROW_LOSS:ROW_LOSS + 1, :]
        grads = [_as_lanes(total[GAIN_ROWS * k:GAIN_ROWS * (k + 1), :]) for k in range(4)]
        grads.append(_as_lanes(total[ROW_POOL_SCALE:ROW_SINKS, :]))
        grads.append(total[ROW_SINKS:ROW_LOSS, 0:N_Q_HEADS])
        grads.append(total[ROW_W_POOL:, :].reshape(w_refs[6].shape))
        grads.append(total_rb)
        for k in range(n_small):
            delta, m2, v2 = _adamw(w_refs[k][...], grads[k], m_refs[k][...], v_refs[k][...])
            result[4 * k][...] = grads[k]
            result[4 * k + 1][...] = delta
            result[4 * k + 2][...] = m2
            result[4 * k + 3][...] = v2

        for cp in copies:
            cp.wait()
        g_in = chip_got[0].astype(F32)
        for s in range(1, N_CHIP):
            g_in = g_in + chip_got[s].astype(F32)
        g_in_ref[...] = g_in

    n_in = 10 + 3 * n_small
    per_core = d_in_t.shape[1:]
    out = pl.pallas_call(
        body, name="tail_reduce",
        out_shape=[jax.ShapeDtypeStruct(d_in_t.shape[2:], F32), jax.ShapeDtypeStruct((1, 128), F32)]
        + [jax.ShapeDtypeStruct(w.shape, F32) for w in small_w for _ in range(4)],
        in_specs=[VMEM] * n_in, out_specs=[VMEM] * (2 + 4 * n_small),
        scratch_shapes=[pltpu.VMEM((SMALL_ROWS, 128), F32), pltpu.VMEM((N_DEV, SMALL_ROWS, 128), F32),
                        pltpu.VMEM((N_DEV, *drel_bias.shape), F32),
                        pltpu.SemaphoreType.DMA((14,)), pltpu.SemaphoreType.DMA((14,)),
                        pltpu.VMEM(per_core, d_in_t.dtype), pltpu.VMEM(per_core, d_in_t.dtype),
                        pltpu.VMEM(per_core, d_in_t.dtype),
                        pltpu.SemaphoreType.DMA, pltpu.SemaphoreType.DMA,
                        pltpu.SemaphoreType.DMA((3,)), pltpu.SemaphoreType.DMA((3,))],
        compiler_params=_params(),
    )(d_in_t, *gains, dpool_scale, dsinks, loss_part, dw_pool, drel_bias, *small_w, *small_m, *small_v)
    return out[0], out[1], [out[2 + 4 * k:6 + 4 * k] for k in range(n_small)]


def norm_inproj(x, g, w_t, out_shard):
    t, d = x.shape
    n = w_t.shape[0]
    tm = TOKEN_TILE
    last = t // tm - 1

    def body(x_ref, g_ref, w_ref, shard_ref, proj_ref, h_ref, gathered_ref, send_sems, recv_sems, local_sems, bounce):
        i = pl.program_id(0)
        start, finish = _gather_plan([shard_ref], [gathered_ref], send_sems, recv_sems, local_sems, [bounce])
        pl.when(i == 0)(start)
        xv = x_ref[...]
        h = ((xv * _rstd(xv)) * g_ref[...]).astype(BF16)
        h_ref[...] = h
        proj_ref[...] = _dot_nt(h, w_ref[...])
        pl.when(i == last)(finish)

    return pl.pallas_call(
        body, name="norm_inproj", grid=(t // tm,),
        out_shape=[jax.ShapeDtypeStruct((t, n), F32), jax.ShapeDtypeStruct((t, d), BF16),
                   jax.ShapeDtypeStruct((N_DEV, *out_shard.shape), out_shard.dtype)],
        in_specs=[pl.BlockSpec((tm, d), lambda i: (i, 0)), pl.BlockSpec((1, d), lambda i: (0, 0)),
                  pl.BlockSpec((n, d), lambda i: (0, 0)), ANY],
        out_specs=[pl.BlockSpec((tm, n), lambda i: (i, 0)), pl.BlockSpec((tm, d), lambda i: (i, 0)), ANY],
        scratch_shapes=_gather_scratch([out_shard]),
        compiler_params=_params(),
    )(x, g, w_t, out_shard)


def bias_band(bucket, in_window, rel_bias):
    def body(bk_ref, win_ref, rb_ref, out_ref):
        bk = bk_ref[...]
        keep = win_ref[...] > 0.5
        for h in range(N_Q_HEADS):
            acc = jnp.zeros(bk.shape, F32)
            for b in range(N_BUCKETS):
                acc = jnp.where(bk == float(b), rb_ref[b, h], acc)
            out_ref[h] = jnp.where(keep, acc, NEG_INF)

    return pl.pallas_call(
        body, name="bias_band",
        out_shape=jax.ShapeDtypeStruct((N_Q_HEADS, BLOCK, 2 * BLOCK), F32),
        in_specs=[VMEM, VMEM, SMEM], out_specs=VMEM,
    )(bucket, in_window, rel_bias)


def bias_band_bwd(bucket, dbias):
    def body(bk_ref, db_ref, out_ref):
        bk = bk_ref[...]
        for h in range(N_Q_HEADS):
            db = db_ref[h]
            for b in range(N_BUCKETS):
                out_ref[b, h] = jnp.sum(jnp.where(bk == float(b), db, 0.0))

    return pl.pallas_call(
        body, name="bias_band_bwd",
        out_shape=jax.ShapeDtypeStruct((N_BUCKETS, N_Q_HEADS), F32),
        in_specs=[VMEM, VMEM], out_specs=SMEM,
    )(bucket, dbias)


def _window_sum(buf_ref, g, w, first):
    cols = slice(g * POOL_GROUP_DIM, (g + 1) * POOL_GROUP_DIM)
    acc = None
    for k in range(w):
        piece = buf_ref[first(k):first(k) + BLOCK, cols]
        acc = piece if acc is None else acc + piece
    return acc


def _inv_count(i, w):
    row = lax.broadcasted_iota(jnp.int32, (BLOCK, 1), 0)
    return 1.0 / jnp.minimum(i * BLOCK + row + 1, w).astype(F32)


def _fill_pool_input(i, ubuf, uc_ref, halo_ref):
    ubuf[0:HALO, :] = jnp.where(i > 0, halo_ref[...], 0.0)
    ubuf[HALO:, :] = uc_ref[...]


def _pooled(i, g, w, ubuf):
    cols = slice(g * POOL_GROUP_DIM, (g + 1) * POOL_GROUP_DIM)
    return _window_sum(ubuf, g, w, lambda k: HALO - k) * _inv_count(i, w) - ubuf[HALO:, cols]


def _head_variants(pair):
    low = lax.broadcasted_iota(jnp.int32, pair.shape, 1) < HEAD_DIM
    swapped = pltpu.roll(pair, HEAD_DIM, 1)
    zero = jnp.zeros_like(pair)
    pick = lambda c, a, b: jnp.where(c, a, b).astype(BF16)
    return [[pick(low, pair, zero), pick(low, zero, swapped)], [pick(low, swapped, zero), pick(low, zero, pair)]]


def _head_probs(i, hq, rows, s_ref, biasm_ref, sinks_ref):
    s = s_ref[hq, rows, :] * ATTN_SCALE + biasm_ref[hq, rows, :]
    col = lax.broadcasted_iota(jnp.int32, s.shape, 1)
    s = jnp.where((i == 0) & (col < BLOCK), NEG_INF, s)
    sink = sinks_ref[0, hq]
    m = jnp.maximum(jnp.max(s, axis=-1, keepdims=True), sink)
    p = jnp.exp(s - m)
    e_sink = jnp.exp(sink - m)
    inv = 1.0 / (jnp.sum(p, axis=-1, keepdims=True) + e_sink)
    return p * inv, e_sink * inv


def _mixer_in_specs(cur, prv):
    return [pl.BlockSpec((BLOCK, 512), lambda i: (cur(i), 0)),
            pl.BlockSpec((HALO, 512), lambda i: (jnp.maximum(cur(i) * (BLOCK // HALO) - 1, 0), 0)),
            pl.BlockSpec((BLOCK, 512), lambda i: (cur(i), 1)),
            pl.BlockSpec((BLOCK, 256), lambda i: (cur(i), 4)),
            pl.BlockSpec((BLOCK, 256), lambda i: (prv(i), 4))]


def _mixer_param_specs():
    return [pl.BlockSpec((N_Q_HEADS, BLOCK, 2 * BLOCK), lambda i: (0, 0, 0)), SMEM,
            pl.BlockSpec((4, POOL_GROUP_DIM, POOL_GROUP_DIM), lambda i: (0, 0, 0)),
            pl.BlockSpec((1, POOL_WIDTH), lambda i: (0, 0))]


def mixers_fwd(proj, biasm, sinks, w_pool, pool_scale, shards):
    t = proj.shape[0]
    nb = t // BLOCK
    ns = len(shards)

    def body(*refs):
        uc_ref, halo_ref, q_ref, kvc_ref, kvp_ref, biasm_ref, sinks_ref, wp_ref, sc_ref = refs[:9]
        shard_refs, out_ref, gathered_refs = refs[9:9 + ns], refs[9 + ns], refs[10 + ns:10 + 2 * ns]
        ubuf, s_all, p_all, send_sems, recv_sems, local_sems = refs[10 + 2 * ns:16 + 2 * ns]
        i = pl.program_id(0)
        start, finish = _gather_plan(shard_refs, gathered_refs, send_sems, recv_sems, local_sems, refs[16 + 2 * ns:])
        pl.when(i == 0)(start)

        _fill_pool_input(i, ubuf, uc_ref, halo_ref)
        for g, w in enumerate(POOL_WINDOWS):
            mixed = _dot(_pooled(i, g, w, ubuf).astype(BF16), wp_ref[g])
            cols = slice(g * POOL_GROUP_DIM, (g + 1) * POOL_GROUP_DIM)
            out_ref[:, cols] = (mixed * sc_ref[:, cols]).astype(BF16)
        kv = jnp.concatenate([kvp_ref[...], kvc_ref[...]], axis=0)
        k_var = _head_variants(kv[:, 0:2 * HEAD_DIM])
        v_var = _head_variants(kv[:, 2 * HEAD_DIM:])
        for hq in range(N_Q_HEADS):
            j, half, h = hq // 2, hq % 2, hq // GQA_GROUP
            q2 = q_ref[:, 2 * HEAD_DIM * j:2 * HEAD_DIM * (j + 1)].astype(BF16)
            s_all[hq] = _dot_nt(q2, k_var[h][half])
        for hq in range(N_Q_HEADS):
            for r in range(0, BLOCK, ROW_CHUNK):
                rows = slice(r, r + ROW_CHUNK)
                probs, _ = _head_probs(i, hq, rows, s_all, biasm_ref, sinks_ref)
                p_all[hq, rows, :] = probs.astype(BF16)
        for j in range(N_Q_HEADS // 2):
            h = 2 * j // GQA_GROUP
            acc = _dot(p_all[2 * j], v_var[h][0]) + _dot(p_all[2 * j + 1], v_var[h][1])
            out_ref[:, POOL_WIDTH + 2 * HEAD_DIM * j:POOL_WIDTH + 2 * HEAD_DIM * (j + 1)] = acc.astype(BF16)

        pl.when(i == nb - 1)(finish)

    return pl.pallas_call(
        body, name="mixers_fwd", grid=(nb,),
        out_shape=[jax.ShapeDtypeStruct((t, 2 * POOL_WIDTH), BF16)]
        + [jax.ShapeDtypeStruct((N_DEV, *sh.shape), sh.dtype) for sh in shards],
        in_specs=_mixer_in_specs(lambda i: i, lambda i: jnp.maximum(i - 1, 0)) + _mixer_param_specs() + [ANY] * ns,
        out_specs=[pl.BlockSpec((BLOCK, 2 * POOL_WIDTH), lambda i: (i, 0))] + [ANY] * ns,
        scratch_shapes=[pltpu.VMEM((HALO + BLOCK, POOL_WIDTH), F32), pltpu.VMEM((N_Q_HEADS, BLOCK, 2 * BLOCK), F32),
                        pltpu.VMEM((N_Q_HEADS, BLOCK, 2 * BLOCK), BF16)] + _gather_scratch(shards),
        compiler_params=_params(),
    )(proj, proj, proj, proj, proj, biasm, sinks, w_pool, pool_scale, *shards)


def outproj_norm(cat, w, x, g, g_next):
    t, d = x.shape
    tm = TOKEN_TILE

    def body(c_ref, w_ref, x_ref, g_ref, gn_ref, mix_ref, x1_ref, h2_ref):
        mix = _dot(c_ref[...], w_ref[...])
        mix_ref[...] = mix
        x1 = x_ref[...] + (mix * _rstd(mix)) * g_ref[...]
        x1_ref[...] = x1
        h2_ref[...] = ((x1 * _rstd(x1)) * gn_ref[...]).astype(BF16)

    row = pl.BlockSpec((tm, d), lambda i: (i, 0))
    gain = pl.BlockSpec((1, d), lambda i: (0, 0))
    return pl.pallas_call(
        body, name="outproj_norm", grid=(t // tm,),
        out_shape=[jax.ShapeDtypeStruct((t, d), F32), jax.ShapeDtypeStruct((t, d), F32), jax.ShapeDtypeStruct((t, d), BF16)],
        in_specs=[pl.BlockSpec((tm, cat.shape[1]), lambda i: (i, 0)), pl.BlockSpec(w.shape, lambda i: (0, 0)), row, gain, gain],
        out_specs=[row, row, row],
        compiler_params=_params(),
    )(cat, w, x, g, g_next)


def ffn_up(h, gate_t, up_t, down_shard):
    t, d = h.shape
    n = gate_t.shape[1]
    f = N_DEV * n
    tm, ts = FFN_TOKEN_TILE, FF_SHARDS_PER_TILE
    tn = ts * n
    steps = (f // tn, t // tm)

    def body(h_ref, wg_ref, wu_ref, shard_ref, gate_ref, up_ref, a_ref, gathered_ref,
             send_sems, recv_sems, local_sems, bounce):
        j, i = pl.program_id(0), pl.program_id(1)
        start, finish = _gather_plan([shard_ref], [gathered_ref], send_sems, recv_sems, local_sems, [bounce])
        pl.when((i == 0) & (j == 0))(start)

        hv = h_ref[...]
        gate = _dot_nt(hv, _merge_rows(wg_ref[...]))
        up = _dot_nt(hv, _merge_rows(wu_ref[...]))
        gate_ref[...] = gate.astype(BF16)
        up_ref[...] = up.astype(BF16)
        a_ref[...] = (gate * (1.0 / (1.0 + jnp.exp(-gate))) * up).astype(BF16)

        pl.when((j == steps[0] - 1) & (i == steps[1] - 1))(finish)

    wide = pl.BlockSpec((tm, tn), lambda j, i: (i, j))
    return pl.pallas_call(
        body, name="ffn_up", grid=steps,
        out_shape=[jax.ShapeDtypeStruct((t, f), BF16)] * 3
        + [jax.ShapeDtypeStruct((N_DEV, *down_shard.shape), down_shard.dtype)],
        in_specs=[pl.BlockSpec((tm, d), lambda j, i: (i, 0)),
                  pl.BlockSpec((ts, n, d), lambda j, i: (j, 0, 0)),
                  pl.BlockSpec((ts, n, d), lambda j, i: (j, 0, 0)), ANY],
        out_specs=[wide, wide, wide, ANY],
        scratch_shapes=_gather_scratch([down_shard]),
        compiler_params=_params(),
    )(h, gate_t, up_t, down_shard)


def ffn_down_loss(a, w_down, x1, g, target):
    t, d = x1.shape
    tm = TOKEN_TILE

    def body(a_ref, w_ref, x_ref, g_ref, t_ref, df_ref, dy_ref, dg_ref, loss_ref):
        @pl.when(pl.program_id(0) == 0)
        def _():
            dg_ref[...] = jnp.zeros_like(dg_ref)
            loss_ref[...] = jnp.zeros_like(loss_ref)

        f = _dot(a_ref[...], _merge_rows(w_ref[...]))
        r = _rstd(f)
        g = g_ref[...]
        err = x_ref[...] + (f * r) * g - t_ref[...]
        loss_ref[...] += 0.5 * jnp.sum(jnp.mean(err * err, axis=-1, keepdims=True))
        dy = err * (1.0 / d)
        dy_ref[...] = dy
        df, dg_rows = _norm_bwd(dy, f, r, g)
        df_ref[...] = df.astype(BF16)
        dg_ref[...] += _as_rows(jnp.sum(dg_rows, axis=0, keepdims=True))

    row = pl.BlockSpec((tm, d), lambda i: (i, 0))
    gain = pl.BlockSpec((1, d), lambda i: (0, 0))
    return pl.pallas_call(
        body, name="ffn_down_loss", grid=(t // tm,),
        out_shape=[jax.ShapeDtypeStruct((t, d), BF16), jax.ShapeDtypeStruct((t, d), F32),
                   jax.ShapeDtypeStruct((d // 128, 128), F32), jax.ShapeDtypeStruct((1, 128), F32)],
        in_specs=[pl.BlockSpec((tm, a.shape[1]), lambda i: (i, 0)), pl.BlockSpec(w_down.shape, lambda i: (0, 0, 0)), row, gain, row],
        out_specs=[row, row, pl.BlockSpec((d // 128, 128), lambda i: (0, 0)), pl.BlockSpec((1, 128), lambda i: (0, 0))],
        compiler_params=_params(),
    )(a, w_down, x1, g, target)


def ffn_down_bwd(df, w_down, gate, up):
    t, d = df.shape
    n = w_down.shape[1]
    f = gate.shape[1]
    tm, ts = FFN_TOKEN_TILE, FF_SHARDS_PER_TILE
    tn = ts * n

    def body(df_ref, w_ref, gate_ref, up_ref, dgate_ref, dup_ref):
        da = _dot_nt(df_ref[...], _merge_rows(w_ref[...]))
        gate = gate_ref[...].astype(F32)
        sig = 1.0 / (1.0 + jnp.exp(-gate))
        dgate_ref[...] = (da * up_ref[...].astype(F32) * (sig * (1.0 + gate * (1.0 - sig)))).astype(BF16)
        dup_ref[...] = (da * (gate * sig)).astype(BF16)

    wide = pl.BlockSpec((tm, tn), lambda j, i: (i, j))
    return pl.pallas_call(
        body, name="ffn_down_bwd", grid=(f // tn, t // tm),
        out_shape=[jax.ShapeDtypeStruct((t, f), BF16)] * 2,
        in_specs=[pl.BlockSpec((tm, d), lambda j, i: (i, 0)), pl.BlockSpec((ts, n, d), lambda j, i: (j, 0, 0)), wide, wide],
        out_specs=[wide, wide],
        compiler_params=_params(),
    )(df, w_down, gate, up)


def grad_rows(a, b, name, by_core=False):
    t, m = a.shape
    d = b.shape[1]
    r = m // N_DEV
    tt = TOKEN_TILE
    last = t // tt - 1
    out_shape = (2, N_CHIP, r, d) if by_core else (N_DEV, r, d)

    def body(a_ref, b_ref, out_ref, acc):
        k = pl.program_id(0)

        @pl.when(k == 0)
        def _():
            acc[...] = jnp.zeros_like(acc)

        acc[...] += _dot_tn(a_ref[...], b_ref[...])

        @pl.when(k == last)
        def _():
            if by_core:
                blocks = acc[...].reshape(N_CHIP, 2, r, d)
                for chip in range(N_CHIP):
                    for core in range(2):
                        out_ref[core, chip] = blocks[chip, core].astype(BF16)
            else:
                out_ref[...] = acc[...].reshape(out_shape).astype(BF16)

    return pl.pallas_call(
        body, name=name, grid=(t // tt,),
        out_shape=jax.ShapeDtypeStruct(out_shape, BF16),
        in_specs=[pl.BlockSpec((tt, m), lambda k: (k, 0)), pl.BlockSpec((tt, d), lambda k: (k, 0))],
        out_specs=pl.BlockSpec(out_shape, lambda k: (0,) * len(out_shape)),
        scratch_shapes=[pltpu.VMEM((m, d), F32)],
        compiler_params=_params(),
    )(a, b)


def grad_ffn(lhs, b, name):
    t, f = lhs[0].shape
    d = b.shape[1]
    nw = len(lhs)
    n = f // N_DEV
    tt, ts = TOKEN_TILE, FF_SHARDS_PER_TILE
    tn = ts * n
    last = t // tt - 1

    def body(*refs):
        a_refs, b_ref, out_ref, acc = refs[:nw], refs[nw], refs[nw + 1], refs[nw + 2]
        k = pl.program_id(1)

        @pl.when(k == 0)
        def _():
            acc[...] = jnp.zeros_like(acc)

        for w in range(nw):
            acc[w] += _dot_tn(a_refs[w][...], b_ref[...])

        @pl.when(k == last)
        def _():
            for w in range(nw):
                blocks = acc[w].reshape(ts // 2, 2, n, d)
                for chip in range(ts // 2):
                    for core in range(2):
                        out_ref[core, chip, w] = blocks[chip, core].astype(BF16)

    return pl.pallas_call(
        body, name=name, grid=(f // tn, t // tt),
        out_shape=jax.ShapeDtypeStruct((2, N_CHIP, nw, n, d), BF16),
        in_specs=[pl.BlockSpec((tt, tn), lambda i, k: (k, i))] * nw + [pl.BlockSpec((tt, d), lambda i, k: (k, 0))],
        out_specs=pl.BlockSpec((2, ts // 2, nw, n, d), lambda i, k: (0, i, 0, 0, 0)),
        scratch_shapes=[pltpu.VMEM((nw, tn, d), F32)],
        compiler_params=_params(),
    )(*lhs, b)


def ffn_up_bwd(dgate, dup, gate_t, up_t, x1, g_ffn, dy, mix, g_mix, chip_parts):
    t, d = x1.shape
    n = gate_t.shape[1]
    f = N_DEV * n
    tm = TOKEN_TILE
    na = len(chip_parts)
    last = t // tm - 1

    def body(*refs):
        dg_ref, du_ref, wg_ref, wu_ref, x_ref, gf_ref, dy_ref, mix_ref, gm_ref = refs[:9]
        part_refs = refs[9:9 + na]
        dx1_ref, dmix_ref, dgf_ref, dgm_ref = refs[9 + na:13 + na]
        slot_refs = refs[13 + na:13 + 2 * na]
        send_sems, recv_sems, local_sems = refs[13 + 2 * na:16 + 2 * na]
        i = pl.program_id(0)
        start, finish = _chip_exchange_plan(part_refs, slot_refs, send_sems, recv_sems, local_sems, refs[16 + 2 * na:])

        @pl.when(i == 0)
        def _():
            start()
            dgf_ref[...] = jnp.zeros_like(dgf_ref)
            dgm_ref[...] = jnp.zeros_like(dgm_ref)

        dh = _dot(dg_ref[...], _merge_rows(wg_ref[...])) + _dot(du_ref[...], _merge_rows(wu_ref[...]))
        x1 = x_ref[...]
        dx, dgf_rows = _norm_bwd(dh, x1, _rstd(x1), gf_ref[...])
        dx1 = dy_ref[...] + dx
        dx1_ref[...] = dx1
        dgf_ref[...] += _as_rows(jnp.sum(dgf_rows, axis=0, keepdims=True))
        mix = mix_ref[...]
        dmix, dgm_rows = _norm_bwd(dx1, mix, _rstd(mix), gm_ref[...])
        dmix_ref[...] = dmix.astype(BF16)
        dgm_ref[...] += _as_rows(jnp.sum(dgm_rows, axis=0, keepdims=True))
        pl.when(i == last)(finish)

    row = pl.BlockSpec((tm, d), lambda i: (i, 0))
    wide = pl.BlockSpec((tm, f), lambda i: (i, 0))
    gain = pl.BlockSpec((1, d), lambda i: (0, 0))
    gain_rows = pl.BlockSpec((d // 128, 128), lambda i: (0, 0))
    whole = pl.BlockSpec((N_DEV, n, d), lambda i: (0, 0, 0), pipeline_mode=pl.Buffered(1))
    out = pl.pallas_call(
        body, name="ffn_up_bwd", grid=(t // tm,),
        out_shape=[jax.ShapeDtypeStruct((t, d), F32), jax.ShapeDtypeStruct((t, d), BF16),
                   jax.ShapeDtypeStruct((d // 128, 128), F32), jax.ShapeDtypeStruct((d // 128, 128), F32)]
        + [jax.ShapeDtypeStruct(p.shape, p.dtype) for p in chip_parts],
        in_specs=[wide, wide, whole, whole, row, gain, row, row, gain] + [ANY] * na,
        out_specs=[row, row, gain_rows, gain_rows] + [ANY] * na,
        scratch_shapes=_chip_exchange_scratch(chip_parts),
        compiler_params=_params(),
    )(dgate, dup, gate_t, up_t, x1, g_ffn, dy, mix, g_mix, *chip_parts)
    return out[:4], out[4:]


def outproj_bwd(dmix, w_out):
    t, d = dmix.shape
    tm = TOKEN_TILE

    def body(dm_ref, w_ref, out_ref):
        out_ref[...] = _dot_nt(dm_ref[...], w_ref[...])

    return pl.pallas_call(
        body, name="outproj_bwd", grid=(t // tm,),
        out_shape=jax.ShapeDtypeStruct((t, w_out.shape[0]), F32),
        in_specs=[pl.BlockSpec((tm, d), lambda i: (i, 0)), pl.BlockSpec(w_out.shape, lambda i: (0, 0))],
        out_specs=pl.BlockSpec((tm, w_out.shape[0]), lambda i: (i, 0)),
        compiler_params=_params(),
    )(dmix, w_out)


def mixers_bwd(proj, dcat, biasm, sinks, w_pool, pool_scale, ffn_parts):
    t = proj.shape[0]
    nb = t // BLOCK
    na = len(ffn_parts)

    def body(*refs):
        (uc_ref, halo_ref, q_ref, kvc_ref, kvp_ref, dcat_ref, biasm_ref, sinks_ref, wp_ref, sc_ref) = refs[:10]
        part_refs = refs[10:10 + na]
        dproj_ref, dbias_ref, dsink_ref, dwp_ref, dsc_ref = refs[10 + na:15 + na]
        slot_refs = refs[15 + na:15 + 2 * na]
        ubuf, dbuf, c_u, c_q, c_kv, s_all, dp_all, ds_all, p_all = refs[15 + 2 * na:24 + 2 * na]
        send_sems, recv_sems, local_sems = refs[24 + 2 * na:27 + 2 * na]
        bounce = refs[27 + 2 * na:]
        i = pl.program_id(0)
        lane = lax.broadcasted_iota(jnp.int32, (1, 128), 1)
        start, finish = _chip_exchange_plan(part_refs, slot_refs, send_sems, recv_sems, local_sems, bounce)

        @pl.when(i == 0)
        def _():
            start()
            dbias_ref[...] = jnp.zeros_like(dbias_ref)
            dwp_ref[...] = jnp.zeros_like(dwp_ref)
            dsc_ref[...] = jnp.zeros_like(dsc_ref)
            dsink_ref[...] = jnp.zeros_like(dsink_ref)
            dbuf[...] = jnp.zeros_like(dbuf)
            c_u[...] = jnp.zeros_like(c_u)
            c_q[...] = jnp.zeros_like(c_q)
            c_kv[...] = jnp.zeros_like(c_kv)

        @pl.when(i < nb)
        def _():
            _fill_pool_input(i, ubuf, uc_ref, halo_ref)
            for g, w in enumerate(POOL_WINDOWS):
                cols = slice(g * POOL_GROUP_DIM, (g + 1) * POOL_GROUP_DIM)
                pooled = _pooled(i, g, w, ubuf).astype(BF16)
                mixed = _dot(pooled, wp_ref[g])
                dout = dcat_ref[:, cols]
                dsc_ref[g:g + 1, :] += jnp.sum(dout * mixed, axis=0, keepdims=True)
                dmixed = (dout * sc_ref[:, cols]).astype(BF16)
                dwp_ref[g] += _dot_tn(pooled, dmixed)
                dpooled = _dot_nt(dmixed, wp_ref[g])
                scaled = dpooled * _inv_count(i, w)
                dbuf[BLOCK:, cols] = scaled[0:HALO]
                dproj_ref[:, cols] = (_window_sum(dbuf, g, w, lambda k: k) + c_u[:, cols]).astype(BF16)
                dbuf[0:BLOCK, cols] = scaled
                c_u[:, cols] = -dpooled

            kv = jnp.concatenate([kvp_ref[...], kvc_ref[...]], axis=0)
            k_var = _head_variants(kv[:, 0:2 * HEAD_DIM])
            v_var = _head_variants(kv[:, 2 * HEAD_DIM:])
            q2s = [q_ref[:, 2 * HEAD_DIM * j:2 * HEAD_DIM * (j + 1)].astype(BF16) for j in range(N_Q_HEADS // 2)]
            do2s = [dcat_ref[:, POOL_WIDTH + 2 * HEAD_DIM * j:POOL_WIDTH + 2 * HEAD_DIM * (j + 1)].astype(BF16)
                    for j in range(N_Q_HEADS // 2)]
            slot = lambda hq: 4 * (hq // GQA_GROUP) + 2 * (hq % 2) + (hq % GQA_GROUP) // 2
            for hq in range(N_Q_HEADS):
                j, half, h = hq // 2, hq % 2, hq // GQA_GROUP
                s_all[hq] = _dot_nt(q2s[j], k_var[h][half])
                dp_all[hq] = _dot_nt(do2s[j], v_var[h][half])
            dsink_row = jnp.zeros((1, 128), F32)
            for hq in range(N_Q_HEADS):
                dsink = 0.0
                for r in range(0, BLOCK, ROW_CHUNK):
                    rows = slice(r, r + ROW_CHUNK)
                    probs, p_sink = _head_probs(i, hq, rows, s_all, biasm_ref, sinks_ref)
                    dp = dp_all[hq, rows, :]
                    delta = jnp.sum(probs * dp, axis=-1, keepdims=True)
                    ds = probs * (dp - delta)
                    dbias_ref[hq, rows, :] += ds
                    dsink = dsink + jnp.sum(p_sink * delta)
                    ds_all[slot(hq), rows, :] = (ds * ATTN_SCALE).astype(BF16)
                    p_all[slot(hq), rows, :] = probs.astype(BF16)
                dsink_row = dsink_row - jnp.where(lane == hq, dsink, 0.0)
            dsink_ref[...] += dsink_row
            dq2 = [None] * (N_Q_HEADS // 2)
            for hq in range(N_Q_HEADS):
                j, half, h = hq // 2, hq % 2, hq // GQA_GROUP
                dq = _dot(ds_all[slot(hq)], k_var[h][half])
                dq2[j] = dq if dq2[j] is None else dq2[j] + dq
            low = lax.broadcasted_iota(jnp.int32, (2 * BLOCK, 2 * HEAD_DIM), 1) < HEAD_DIM
            dk_half, dv_half = [[None, None], [None, None]], [[None, None], [None, None]]
            for h in range(N_KV_HEADS):
                for half in range(2):
                    heads = [hq for hq in range(GQA_GROUP * h, GQA_GROUP * (h + 1)) if hq % 2 == half]
                    base = slot(heads[0])
                    q_rows = jnp.concatenate([q2s[hq // 2] for hq in heads], axis=0)
                    do_rows = jnp.concatenate([do2s[hq // 2] for hq in heads], axis=0)
                    dk_half[h][half] = _dot_tn(_merge_rows(ds_all[base:base + 2]), q_rows)
                    dv_half[h][half] = _dot_tn(_merge_rows(p_all[base:base + 2]), do_rows)

            def pair_of(halves):
                return jnp.where(low, halves[0][0] + pltpu.roll(halves[0][1], HEAD_DIM, 1),
                                 halves[1][1] + pltpu.roll(halves[1][0], HEAD_DIM, 1))

            dkv = jnp.concatenate([pair_of(dk_half), pair_of(dv_half)], axis=1)
            dproj_ref[:, POOL_WIDTH:2 * POOL_WIDTH] = c_q[...].astype(BF16)
            dproj_ref[:, 2 * POOL_WIDTH:] = (c_kv[...] + dkv[0:BLOCK]).astype(BF16)
            c_q[...] = jnp.concatenate(dq2, axis=1)
            c_kv[...] = dkv[BLOCK:]

        @pl.when(i == nb)
        def _():
            dbuf[BLOCK:, :] = jnp.zeros((HALO, POOL_WIDTH), F32)
            for g, w in enumerate(POOL_WINDOWS):
                cols = slice(g * POOL_GROUP_DIM, (g + 1) * POOL_GROUP_DIM)
                dproj_ref[:, cols] = (_window_sum(dbuf, g, w, lambda k: k) + c_u[:, cols]).astype(BF16)
            dproj_ref[:, POOL_WIDTH:2 * POOL_WIDTH] = c_q[...].astype(BF16)
            dproj_ref[:, 2 * POOL_WIDTH:] = c_kv[...].astype(BF16)
            finish()

    cur = lambda i: jnp.minimum(i, nb - 1)
    prv = lambda i: jnp.maximum(jnp.minimum(i, nb - 1) - 1, 0)
    out = pl.pallas_call(
        body, name="mixers_bwd", grid=(nb + 1,),
        out_shape=[jax.ShapeDtypeStruct((t, proj.shape[1]), BF16),
                   jax.ShapeDtypeStruct((N_Q_HEADS, BLOCK, 2 * BLOCK), F32),
                   jax.ShapeDtypeStruct((1, 128), F32),
                   jax.ShapeDtypeStruct((4, POOL_GROUP_DIM, POOL_GROUP_DIM), F32),
                   jax.ShapeDtypeStruct((len(POOL_WINDOWS), POOL_GROUP_DIM), F32)]
        + [jax.ShapeDtypeStruct(p.shape, p.dtype) for p in ffn_parts],
        in_specs=_mixer_in_specs(cur, prv) + [pl.BlockSpec((BLOCK, 2 * POOL_WIDTH), lambda i: (cur(i), 0))]
        + _mixer_param_specs() + [ANY] * na,
        out_specs=[pl.BlockSpec((BLOCK, proj.shape[1]), lambda i: (jnp.maximum(i - 1, 0), 0)),
                   pl.BlockSpec((N_Q_HEADS, BLOCK, 2 * BLOCK), lambda i: (0, 0, 0)),
                   pl.BlockSpec((1, 128), lambda i: (0, 0)),
                   pl.BlockSpec((4, POOL_GROUP_DIM, POOL_GROUP_DIM), lambda i: (0, 0, 0)),
                   pl.BlockSpec((len(POOL_WINDOWS), POOL_GROUP_DIM), lambda i: (0, 0))] + [ANY] * na,
        scratch_shapes=[pltpu.VMEM((HALO + BLOCK, POOL_WIDTH), F32), pltpu.VMEM((BLOCK + HALO, POOL_WIDTH), F32),
                        pltpu.VMEM((BLOCK, POOL_WIDTH), F32), pltpu.VMEM((BLOCK, POOL_WIDTH), F32),
                        pltpu.VMEM((BLOCK, 256), F32),
                        pltpu.VMEM((N_Q_HEADS, BLOCK, 2 * BLOCK), F32), pltpu.VMEM((N_Q_HEADS, BLOCK, 2 * BLOCK), F32),
                        pltpu.VMEM((N_Q_HEADS, BLOCK, 2 * BLOCK), BF16), pltpu.VMEM((N_Q_HEADS, BLOCK, 2 * BLOCK), BF16)]
        + _chip_exchange_scratch(ffn_parts),
        compiler_params=_params(),
    )(proj, proj, proj, proj, proj, dcat, biasm, sinks, w_pool, pool_scale, *ffn_parts)
    return out[:5], out[5:]


def inproj_bwd(dproj, w_in_t, x, g, dx1):
    t, d = x.shape
    n = dproj.shape[1]
    tm = TOKEN_TILE

    def body(dp_ref, w_ref, x_ref, g_ref, dx1_ref, dx_ref, dg_ref):
        @pl.when(pl.program_id(0) == 0)
        def _():
            dg_ref[...] = jnp.zeros_like(dg_ref)

        dh = _dot(dp_ref[...], w_ref[...])
        xv = x_ref[...]
        dx, dg_rows = _norm_bwd(dh, xv, _rstd(xv), g_ref[...])
        dx_ref[...] = dx1_ref[...] + dx
        dg_ref[...] += _as_rows(jnp.sum(dg_rows, axis=0, keepdims=True))

    row = pl.BlockSpec((tm, d), lambda i: (i, 0))
    gain = pl.BlockSpec((1, d), lambda i: (0, 0))
    return pl.pallas_call(
        body, name="inproj_bwd", grid=(t // tm,),
        out_shape=[jax.ShapeDtypeStruct((t, d), F32), jax.ShapeDtypeStruct((d // 128, 128), F32)],
        in_specs=[pl.BlockSpec((tm, n), lambda i: (i, 0)), pl.BlockSpec(w_in_t.shape, lambda i: (0, 0)), row, gain, row],
        out_specs=[row, pl.BlockSpec((d // 128, 128), lambda i: (0, 0))],
        compiler_params=_params(),
    )(dproj, w_in_t, x, g, dx1)


def _bucket_band():
    qi = jnp.arange(BLOCK)[:, None]
    kj = jnp.arange(2 * BLOCK)[None, :]
    dist = qi + BLOCK - kj
    n = jnp.maximum(dist, 0)
    nf = jnp.maximum(n, 1).astype(F32)
    large = MAX_EXACT + (jnp.log(nf / MAX_EXACT) / np.float32(np.log(MAX_DISTANCE / MAX_EXACT))
                         * (N_BUCKETS - MAX_EXACT)).astype(jnp.int32)
    large = jnp.minimum(large, N_BUCKETS - 1)
    bucket = jnp.where(n < MAX_EXACT, n, large)
    in_window = (dist >= 0) & (dist < BLOCK)
    return bucket.astype(F32), in_window.astype(F32)


def kernel(x, g_pre_mix, w_in, w_pool, pool_scale, rel_bias, sinks, w_out, g_post_mix, g_pre_ffn, w_gate, w_up, w_down, g_post_ffn, loss_target, m_g_pre_mix, m_w_in, m_w_pool, m_pool_scale, m_rel_bias, m_sinks, m_w_out, m_g_post_mix, m_g_pre_ffn, m_w_gate, m_w_up, m_w_down, m_g_post_ffn, v_g_pre_mix, v_w_in, v_w_pool, v_pool_scale, v_rel_bias, v_sinks, v_w_out, v_g_post_mix, v_g_pre_ffn, v_w_gate, v_w_up, v_w_down, v_g_post_ffn):
    d = x.shape[-1]
    xs, target = x[0], loss_target[0]

    w_in_ts = w_in[0].T.astype(BF16)
    w_out_s = w_out[0].astype(BF16)
    gate_ts = w_gate[0].T.astype(BF16)
    up_ts = w_up[0].T.astype(BF16)
    w_down_s = w_down[0].astype(BF16)
    w_in_t, = gather_blocks([w_in_ts], "gather_w_in")
    w_in_t = w_in_t.reshape(-1, d)

    bucket, in_window = _bucket_band()
    biasm = bias_band(bucket, in_window, rel_bias)
    w_pool_b = w_pool[0].astype(BF16)
    proj, h1, up_t = norm_inproj(xs, g_pre_mix, w_in_t, up_ts)
    cat, gate_t, w_out_f = mixers_fwd(proj, biasm, sinks, w_pool_b, pool_scale, [gate_ts, w_out_s])
    w_out_f = w_out_f.reshape(-1, d)
    mix, x1, h2 = outproj_norm(cat, w_out_f, xs, g_post_mix, g_pre_ffn)
    gate, up, act, w_down_f = ffn_up(h2, gate_t, up_t, w_down_s)
    df, dy, dg_post_ffn, loss_part = ffn_down_loss(act, w_down_f, x1, g_post_ffn, target)

    dgate, dup = ffn_down_bwd(df, w_down_f, gate, up)
    d_gate_up = grad_ffn([dgate, dup], h2, "grad_w_gate_up")
    d_down = grad_ffn([act], df, "grad_w_down")
    q_gate_up, q_down = pair_add([d_gate_up, d_down], pair_exchange([d_gate_up, d_down], "pair_exchange_ffn"), "pair_add_ffn")
    (dx1, dmix, dg_pre_ffn, dg_post_mix), gate_up_slots = ffn_up_bwd(
        dgate, dup, gate_t, up_t, x1, g_pre_ffn, dy, mix, g_post_mix, [q_gate_up])

    dcat = outproj_bwd(dmix, w_out_f)
    d_out = grad_rows(cat, dmix, "grad_w_out", by_core=True)
    q_out = pair_add([d_out], pair_exchange([d_out], "pair_exchange_out"), "pair_add_out")
    (dproj, dbias, dsinks, dw_pool, dpool_scale), slots = mixers_bwd(
        proj, dcat, biasm, sinks, w_pool_b, pool_scale, [q_down, *q_out])
    drel_bias = bias_band_bwd(bucket, dbias)
    grad_x, dg_pre_mix = inproj_bwd(dproj, w_in_t, xs, g_pre_mix, dx1)
    d_in_t = grad_rows(dproj, h1, "grad_w_in", by_core=True)

    small_w = [g_pre_mix, g_post_mix, g_pre_ffn, g_post_ffn, pool_scale, sinks, w_pool, rel_bias]
    small_m = [m_g_pre_mix, m_g_post_mix, m_g_pre_ffn, m_g_post_ffn, m_pool_scale, m_sinks, m_w_pool, m_rel_bias]
    small_v = [v_g_pre_mix, v_g_post_mix, v_g_pre_ffn, v_g_post_ffn, v_pool_scale, v_sinks, v_w_pool, v_rel_bias]
    g_in_t, loss_row, sm = tail_reduce(
        d_in_t, [dg_pre_mix, dg_post_mix, dg_pre_ffn, dg_post_ffn], dpool_scale, dsinks, loss_part, dw_pool, drel_bias,
        small_w, small_m, small_v)
    g_gate_up_t, g_down, g_out = sum_slots([*gate_up_slots, *slots], "sum_ffn")
    big_w = [w_in[0], w_out[0], w_gate[0], w_up[0], w_down[0]]
    big_g = [g_in_t.T, g_out, g_gate_up_t[0].T, g_gate_up_t[1].T, g_down[0]]
    big_m = [m_w_in[0], m_w_out[0], m_w_gate[0], m_w_up[0], m_w_down[0]]
    big_v = [v_w_in[0], v_w_out[0], v_w_gate[0], v_w_up[0], v_w_down[0]]
    upd = adamw_update(big_w[:2], big_g[:2], big_m[:2], big_v[:2], "adamw_mix") \
        + adamw_update(big_w[2:], big_g[2:], big_m[2:], big_v[2:], "adamw_ffn")
    big = [[big_g[k][None], *(u[None] for u in upd[k])] for k in range(5)]

    def ordered(kind):
        s, b = [p[kind] for p in sm], [p[kind] for p in big]
        return [s[0], b[0], s[6], s[4], s[7], s[5], b[1], s[1], s[2], b[2], b[3], b[4], s[3]]

    return (loss_row[0, 0], grad_x[None], *ordered(0), *ordered(1), *ordered(2), *ordered(3))
```

```python
import numpy as np
import jax
import jax.numpy as jnp
from jax import lax
from jax.experimental import pallas as pl
from jax.experimental.pallas import tpu as pltpu

F32 = jnp.float32
BF16 = jnp.bfloat16

N_DEV = 8
N_CHIP = 4
POOL_WIDTH = 512
POOL_WINDOWS = (2, 4, 8, 16)
POOL_GROUP_DIM = 128
HEAD_DIM = 64
N_Q_HEADS = 8
N_KV_HEADS = 2
GQA_GROUP = 4
BLOCK = 128
HALO = 16
ROW_CHUNK = 32
N_BUCKETS = 32
MAX_EXACT = 16
MAX_DISTANCE = 128
EPS = 1e-6
NEG_INF = -1e30
ATTN_SCALE = float(1.0 / np.sqrt(np.float32(HEAD_DIM)))

ADAM_LR = 0.001
ADAM_B1 = 0.9
ADAM_B2 = 0.999
ADAM_EPS = 1e-08
ADAM_WD = 0.01
ADAM_STEP = 10

TOKEN_TILE = 512
FFN_TOKEN_TILE = 1024
FF_SHARDS_PER_TILE = 4
VMEM_LIMIT = 56 * 1024 * 1024
MESH = pl.DeviceIdType.MESH
ANY = pl.BlockSpec(memory_space=pl.ANY)
VMEM = pl.BlockSpec(memory_space=pltpu.VMEM)
SMEM = pl.BlockSpec(memory_space=pltpu.SMEM)


def _params(**kw):
    return pltpu.CompilerParams(vmem_limit_bytes=VMEM_LIMIT, **kw)


def _dot(a, b):
    return jnp.dot(a, b, preferred_element_type=F32)


def _dot_nt(a, b):
    return lax.dot_general(a, b, (((1,), (1,)), ((), ())), preferred_element_type=F32)


def _dot_tn(a, b):
    return lax.dot_general(a, b, (((0,), (0,)), ((), ())), preferred_element_type=F32)


def _rstd(v):
    return lax.rsqrt(jnp.mean(v * v, axis=-1, keepdims=True) + EPS)


def _norm_bwd(dout, v, r, g):
    vn = v * r
    dn = dout * g
    dv = r * (dn - vn * jnp.mean(dn * vn, axis=-1, keepdims=True))
    return dv, dout * vn


def _as_rows(v):
    return jnp.concatenate([v[:, k:k + 128] for k in range(0, v.shape[1], 128)], axis=0)


def _as_lanes(rows):
    return jnp.concatenate([rows[k:k + 1, :] for k in range(rows.shape[0])], axis=1)


def _merge_rows(value):
    s, r, c_ = value.shape
    return value.reshape(s * r, c_)


def _gather_plan(srcs, outs, send_sems, recv_sems, local_sems=None, bounce=None):
    n = len(srcs)
    x, y, c = lax.axis_index("x"), lax.axis_index("y"), lax.axis_index("c")
    me, sibling = (x, y, c), (x, y, 1 - c)
    chips = [(1 - x, y), (x, 1 - y), (1 - x, 1 - y)]

    def slot(a, px, py, pc):
        return outs[a].at[4 * px + 2 * py + pc]

    def copy(a, k, block, to, from_src=False):
        return pltpu.make_async_remote_copy(
            src_ref=srcs[a] if from_src else slot(a, *block), dst_ref=slot(a, *block),
            send_sem=send_sems.at[k * n + a], recv_sem=recv_sems.at[k * n + a], device_id=to, device_id_type=MESH)

    def own_in(a):
        return pltpu.make_async_copy(srcs[a], bounce[a], local_sems.at[a])

    def own_out(a):
        return pltpu.make_async_copy(bounce[a], slot(a, *me), local_sems.at[a])

    def first(a):
        return [copy(a, 0, me, sibling, True)] + [copy(a, 1 + j, me, (*chip, c), True) for j, chip in enumerate(chips)]

    def passed(a, j):
        return copy(a, 4 + j, (*chips[j], c), sibling)

    def start():
        for a in range(n):
            if bounce is not None:
                own_in(a).start()
            for cp in first(a):
                cp.start()

    def finish():
        if bounce is not None:
            for a in range(n):
                own_in(a).wait()
                own_out(a).start()
        for j, chip in enumerate(chips):
            for a in range(n):
                copy(a, 1 + j, (*chip, c), me).wait_recv()
                passed(a, j).start()
        for a in range(n):
            copy(a, 0, sibling, me).wait_recv()
            for j, chip in enumerate(chips):
                copy(a, 4 + j, (*chip, 1 - c), me).wait_recv()
        for a in range(n):
            for cp in first(a) + [passed(a, j) for j in range(3)]:
                cp.wait_send()
            if bounce is not None:
                own_out(a).wait()

    return start, finish


def _gather_scratch(shards):
    n = len(shards)
    return [pltpu.SemaphoreType.DMA((7 * n,)), pltpu.SemaphoreType.DMA((7 * n,)), pltpu.SemaphoreType.DMA((n,))] \
        + [pltpu.VMEM(s.shape, s.dtype) for s in shards]


def _chip_exchange_plan(srcs, outs, send_sems, recv_sems, local_sems, bounce):
    n = len(srcs)
    x, y, c = lax.axis_index("x"), lax.axis_index("y"), lax.axis_index("c")
    my_chip = 2 * x + y

    def copies():
        out = []
        for a in range(n):
            for k in range(1, N_CHIP):
                px, py = x ^ (k >> 1), y ^ (k & 1)
                out.append(pltpu.make_async_remote_copy(
                    src_ref=srcs[a].at[2 * px + py], dst_ref=outs[a].at[my_chip],
                    send_sem=send_sems.at[(k - 1) * n + a], recv_sem=recv_sems.at[(k - 1) * n + a],
                    device_id=(px, py, c), device_id_type=MESH))
        return out

    def own_in(a):
        return pltpu.make_async_copy(srcs[a].at[my_chip], bounce[a], local_sems.at[a])

    def own_out(a):
        return pltpu.make_async_copy(bounce[a], outs[a].at[my_chip], local_sems.at[a])

    def start():
        for a in range(n):
            own_in(a).start()
        for cp in copies():
            cp.start()

    def finish():
        for a in range(n):
            own_in(a).wait()
            own_out(a).start()
        for cp in copies():
            cp.wait()
        for a in range(n):
            own_out(a).wait()

    return start, finish


def _chip_exchange_scratch(parts):
    n = len(parts)
    return [pltpu.SemaphoreType.DMA((3 * n,)), pltpu.SemaphoreType.DMA((3 * n,)), pltpu.SemaphoreType.DMA((n,))] \
        + [pltpu.VMEM(p.shape[1:], p.dtype) for p in parts]


def gather_blocks(shards, name):
    def body(*refs):
        n = len(shards)
        start, finish = _gather_plan(refs[:n], refs[n:2 * n], *refs[2 * n:2 * n + 3], bounce=refs[2 * n + 3:])
        start()
        finish()

    return pl.pallas_call(
        body, name=name,
        out_shape=[jax.ShapeDtypeStruct((N_DEV, *s.shape), s.dtype) for s in shards],
        in_specs=[ANY] * len(shards), out_specs=[ANY] * len(shards),
        scratch_shapes=_gather_scratch(shards),
    )(*shards)


def pair_exchange(parts, name):
    n = len(parts)

    def body(*refs):
        p_refs, got_refs, send_sems, recv_sems = refs[:n], refs[n:2 * n], *refs[2 * n:]
        x, y, c = lax.axis_index("x"), lax.axis_index("y"), lax.axis_index("c")
        copies = [pltpu.make_async_remote_copy(
            src_ref=p_refs[a].at[1 - c], dst_ref=got_refs[a], send_sem=send_sems.at[a], recv_sem=recv_sems.at[a],
            device_id=(x, y, 1 - c), device_id_type=MESH) for a in range(n)]
        for cp in copies:
            cp.start()
        for cp in copies:
            cp.wait()

    return pl.pallas_call(
        body, name=name, out_shape=[jax.ShapeDtypeStruct(p.shape[1:], p.dtype) for p in parts],
        in_specs=[ANY] * n, out_specs=[ANY] * n,
        scratch_shapes=[pltpu.SemaphoreType.DMA((n,)), pltpu.SemaphoreType.DMA((n,))],
    )(*parts)


def pair_add(parts, got, name):
    n = len(parts)

    def body(core_ref, *refs):
        for a in range(n):
            refs[2 * n + a][...] = (refs[a][...].astype(F32) + refs[n + a][...].astype(F32)).astype(BF16)

    def own(p):
        zeros = (0,) * (p.ndim - 2)
        return pl.BlockSpec((None, 1, *p.shape[2:]), lambda i, core: (core[0], i, *zeros))

    def plain(p):
        zeros = (0,) * (p.ndim - 1)
        return pl.BlockSpec((1, *p.shape[1:]), lambda i, core: (i, *zeros))

    core = lax.axis_index("c").astype(jnp.int32).reshape(1)
    return pl.pallas_call(
        body, name=name,
        grid_spec=pltpu.PrefetchScalarGridSpec(
            num_scalar_prefetch=1, grid=(got[0].shape[0],),
            in_specs=[own(p) for p in parts] + [plain(p) for p in got], out_specs=[plain(p) for p in got]),
        out_shape=[jax.ShapeDtypeStruct(p.shape, BF16) for p in got],
        compiler_params=_params(),
    )(core, *parts, *got)


def sum_slots(slots, name):
    n = len(slots)

    def body(*refs):
        for a in range(n):
            total = refs[a][0].astype(F32)
            for s in range(1, slots[a].shape[0]):
                total = total + refs[a][s].astype(F32)
            refs[n + a][...] = total

    return pl.pallas_call(
        body, name=name,
        out_shape=[jax.ShapeDtypeStruct(p.shape[1:], F32) for p in slots],
        in_specs=[VMEM] * n, out_specs=[VMEM] * n,
        compiler_params=_params(),
    )(*slots)


def _adamw(w, g, m, v):
    m2 = ADAM_B1 * m + (1.0 - ADAM_B1) * g
    v2 = ADAM_B2 * v + (1.0 - ADAM_B2) * (g * g)
    m_hat = m2 / (1.0 - ADAM_B1 ** ADAM_STEP)
    v_hat = v2 / (1.0 - ADAM_B2 ** ADAM_STEP)
    delta = -ADAM_LR * (m_hat / (jnp.sqrt(v_hat) + ADAM_EPS) + ADAM_WD * w)
    return delta, m2, v2


def adamw_update(ws, gs, ms, vs, name):
    n = len(ws)

    def body(*refs):
        for a in range(n):
            delta, m2, v2 = _adamw(refs[a][...], refs[n + a][...], refs[2 * n + a][...], refs[3 * n + a][...])
            refs[4 * n + 3 * a][...] = delta
            refs[4 * n + 3 * a + 1][...] = m2
            refs[4 * n + 3 * a + 2][...] = v2

    out = pl.pallas_call(
        body, name=name,
        out_shape=[jax.ShapeDtypeStruct(w.shape, F32) for w in ws for _ in range(3)],
        in_specs=[VMEM] * (4 * n), out_specs=[VMEM] * (3 * n),
        compiler_params=_params(),
    )(*ws, *gs, *ms, *vs)
    return [out[3 * a:3 * a + 3] for a in range(n)]


GAIN_ROWS = 8
ROW_POOL_SCALE = 4 * GAIN_ROWS
ROW_SINKS = ROW_POOL_SCALE + 4
ROW_LOSS = ROW_SINKS + 1
ROW_W_POOL = 40
SMALL_ROWS = ROW_W_POOL + 4 * POOL_GROUP_DIM


def tail_reduce(gains, dpool_scale, dsinks, loss_part, dw_pool, drel_bias, small_w, small_m, small_v):
    n_small = len(small_w)

    def body(*refs):
        g_refs, (dsc_ref, dsink_ref, loss_ref, dwp_ref, drb_ref) = refs[0:4], refs[4:9]
        w_refs, m_refs, v_refs = (refs[9 + k * n_small:9 + (k + 1) * n_small] for k in range(3))
        outs = refs[9 + 3 * n_small:]
        loss_out = outs[0]
        result = outs[1:1 + 4 * n_small]
        stage, gat, gat_rb, g_send, g_recv = outs[1 + 4 * n_small:]
        x, y, c = lax.axis_index("x"), lax.axis_index("y"), lax.axis_index("c")
        my_id = 4 * x + 2 * y + c

        for k in range(4):
            stage[GAIN_ROWS * k:GAIN_ROWS * (k + 1), :] = g_refs[k][...]
        stage[ROW_POOL_SCALE:ROW_SINKS, :] = dsc_ref[...]
        stage[ROW_SINKS:ROW_LOSS, :] = dsink_ref[...]
        stage[ROW_LOSS:ROW_LOSS + 1, :] = loss_ref[...]
        stage[ROW_LOSS + 1:ROW_W_POOL, :] = jnp.zeros((ROW_W_POOL - ROW_LOSS - 1, 128), F32)
        stage[ROW_W_POOL:, :] = dwp_ref[...].reshape(4 * POOL_GROUP_DIM, POOL_GROUP_DIM)
        gat[my_id] = stage[...]
        gat_rb[my_id] = drb_ref[...]
        start, finish = _gather_plan([stage, drb_ref], [gat, gat_rb], g_send, g_recv)
        start()
        finish()
        total, total_rb = gat[0], gat_rb[0]
        for s in range(1, N_DEV):
            total, total_rb = total + gat[s], total_rb + gat_rb[s]
        loss_out[...] = total[ROW_LOSS:ROW_LOSS + 1, :]
        grads = [_as_lanes(total[GAIN_ROWS * k:GAIN_ROWS * (k + 1), :]) for k in range(4)]
        grads.append(_as_lanes(total[ROW_POOL_SCALE:ROW_SINKS, :]))
        grads.append(total[ROW_SINKS:ROW_LOSS, 0:N_Q_HEADS])
        grads.append(total[ROW_W_POOL:, :].reshape(w_refs[6].shape))
        grads.append(total_rb)
        for k in range(n_small):
            delta, m2, v2 = _adamw(w_refs[k][...], grads[k], m_refs[k][...], v_refs[k][...])
            result[4 * k][...] = grads[k]
            result[4 * k + 1][...] = delta
            result[4 * k + 2][...] = m2
            result[4 * k + 3][...] = v2

    n_in = 9 + 3 * n_small
    out = pl.pallas_call(
        body, name="tail_reduce",
        out_shape=[jax.ShapeDtypeStruct((1, 128), F32)]
        + [jax.ShapeDtypeStruct(w.shape, F32) for w in small_w for _ in range(4)],
        in_specs=[VMEM] * n_in, out_specs=[VMEM] * (1 + 4 * n_small),
        scratch_shapes=[pltpu.VMEM((SMALL_ROWS, 128), F32), pltpu.VMEM((N_DEV, SMALL_ROWS, 128), F32),
                        pltpu.VMEM((N_DEV, *drel_bias.shape), F32),
                        pltpu.SemaphoreType.DMA((14,)), pltpu.SemaphoreType.DMA((14,))],
        compiler_params=_params(),
    )(*gains, dpool_scale, dsinks, loss_part, dw_pool, drel_bias, *small_w, *small_m, *small_v)
    return out[0], [out[1 + 4 * k:5 + 4 * k] for k in range(n_small)]


def norm_inproj(x, g, w_t, out_shard):
    t, d = x.shape
    n = w_t.shape[0]
    tm = TOKEN_TILE
    last = t // tm - 1

    def body(x_ref, g_ref, w_ref, shard_ref, proj_ref, h_ref, gathered_ref, send_sems, recv_sems, local_sems, bounce):
        i = pl.program_id(0)
        start, finish = _gather_plan([shard_ref], [gathered_ref], send_sems, recv_sems, local_sems, [bounce])
        pl.when(i == 0)(start)
        xv = x_ref[...]
        h = ((xv * _rstd(xv)) * g_ref[...]).astype(BF16)
        h_ref[...] = h
        proj_ref[...] = _dot_nt(h, w_ref[...])
        pl.when(i == last)(finish)

    return pl.pallas_call(
        body, name="norm_inproj", grid=(t // tm,),
        out_shape=[jax.ShapeDtypeStruct((t, n), F32), jax.ShapeDtypeStruct((t, d), BF16),
                   jax.ShapeDtypeStruct((N_DEV, *out_shard.shape), out_shard.dtype)],
        in_specs=[pl.BlockSpec((tm, d), lambda i: (i, 0)), pl.BlockSpec((1, d), lambda i: (0, 0)),
                  pl.BlockSpec((n, d), lambda i: (0, 0)), ANY],
        out_specs=[pl.BlockSpec((tm, n), lambda i: (i, 0)), pl.BlockSpec((tm, d), lambda i: (i, 0)), ANY],
        scratch_shapes=_gather_scratch([out_shard]),
        compiler_params=_params(),
    )(x, g, w_t, out_shard)


def bias_band(bucket, in_window, rel_bias):
    def body(bk_ref, win_ref, rb_ref, out_ref):
        bk = bk_ref[...]
        keep = win_ref[...] > 0.5
        for h in range(N_Q_HEADS):
            acc = jnp.zeros(bk.shape, F32)
            for b in range(N_BUCKETS):
                acc = jnp.where(bk == float(b), rb_ref[b, h], acc)
            out_ref[h] = jnp.where(keep, acc, NEG_INF)

    return pl.pallas_call(
        body, name="bias_band",
        out_shape=jax.ShapeDtypeStruct((N_Q_HEADS, BLOCK, 2 * BLOCK), F32),
        in_specs=[VMEM, VMEM, SMEM], out_specs=VMEM,
    )(bucket, in_window, rel_bias)


def bias_band_bwd(bucket, dbias):
    def body(bk_ref, db_ref, out_ref):
        bk = bk_ref[...]
        for h in range(N_Q_HEADS):
            db = db_ref[h]
            for b in range(N_BUCKETS):
                out_ref[b, h] = jnp.sum(jnp.where(bk == float(b), db, 0.0))

    return pl.pallas_call(
        body, name="bias_band_bwd",
        out_shape=jax.ShapeDtypeStruct((N_BUCKETS, N_Q_HEADS), F32),
        in_specs=[VMEM, VMEM], out_specs=SMEM,
    )(bucket, dbias)


def _window_sum(buf_ref, g, w, first):
    cols = slice(g * POOL_GROUP_DIM, (g + 1) * POOL_GROUP_DIM)
    acc = None
    for k in range(w):
        piece = buf_ref[first(k):first(k) + BLOCK, cols]
        acc = piece if acc is None else acc + piece
    return acc


def _inv_count(i, w):
    row = lax.broadcasted_iota(jnp.int32, (BLOCK, 1), 0)
    return 1.0 / jnp.minimum(i * BLOCK + row + 1, w).astype(F32)


def _fill_pool_input(i, ubuf, uc_ref, halo_ref):
    ubuf[0:HALO, :] = jnp.where(i > 0, halo_ref[...], 0.0)
    ubuf[HALO:, :] = uc_ref[...]


def _pooled(i, g, w, ubuf):
    cols = slice(g * POOL_GROUP_DIM, (g + 1) * POOL_GROUP_DIM)
    return _window_sum(ubuf, g, w, lambda k: HALO - k) * _inv_count(i, w) - ubuf[HALO:, cols]


def _head_variants(pair):
    low = lax.broadcasted_iota(jnp.int32, pair.shape, 1) < HEAD_DIM
    swapped = pltpu.roll(pair, HEAD_DIM, 1)
    zero = jnp.zeros_like(pair)
    pick = lambda c, a, b: jnp.where(c, a, b).astype(BF16)
    return [[pick(low, pair, zero), pick(low, zero, swapped)], [pick(low, swapped, zero), pick(low, zero, pair)]]


def _head_probs(i, hq, rows, s_ref, biasm_ref, sinks_ref):
    s = s_ref[hq, rows, :] * ATTN_SCALE + biasm_ref[hq, rows, :]
    col = lax.broadcasted_iota(jnp.int32, s.shape, 1)
    s = jnp.where((i == 0) & (col < BLOCK), NEG_INF, s)
    sink = sinks_ref[0, hq]
    m = jnp.maximum(jnp.max(s, axis=-1, keepdims=True), sink)
    p = jnp.exp(s - m)
    e_sink = jnp.exp(sink - m)
    inv = 1.0 / (jnp.sum(p, axis=-1, keepdims=True) + e_sink)
    return p * inv, e_sink * inv


def _mixer_in_specs(cur, prv):
    return [pl.BlockSpec((BLOCK, 512), lambda i: (cur(i), 0)),
            pl.BlockSpec((HALO, 512), lambda i: (jnp.maximum(cur(i) * (BLOCK // HALO) - 1, 0), 0)),
            pl.BlockSpec((BLOCK, 512), lambda i: (cur(i), 1)),
            pl.BlockSpec((BLOCK, 256), lambda i: (cur(i), 4)),
            pl.BlockSpec((BLOCK, 256), lambda i: (prv(i), 4))]


def _mixer_param_specs():
    return [pl.BlockSpec((N_Q_HEADS, BLOCK, 2 * BLOCK), lambda i: (0, 0, 0)), SMEM,
            pl.BlockSpec((4, POOL_GROUP_DIM, POOL_GROUP_DIM), lambda i: (0, 0, 0)),
            pl.BlockSpec((1, POOL_WIDTH), lambda i: (0, 0))]


def mixers_fwd(proj, biasm, sinks, w_pool, pool_scale, shards):
    t = proj.shape[0]
    nb = t // BLOCK
    ns = len(shards)

    def body(*refs):
        uc_ref, halo_ref, q_ref, kvc_ref, kvp_ref, biasm_ref, sinks_ref, wp_ref, sc_ref = refs[:9]
        shard_refs, out_ref, gathered_refs = refs[9:9 + ns], refs[9 + ns], refs[10 + ns:10 + 2 * ns]
        ubuf, s_all, p_all, send_sems, recv_sems, local_sems = refs[10 + 2 * ns:16 + 2 * ns]
        i = pl.program_id(0)
        start, finish = _gather_plan(shard_refs, gathered_refs, send_sems, recv_sems, local_sems, refs[16 + 2 * ns:])
        pl.when(i == 0)(start)

        _fill_pool_input(i, ubuf, uc_ref, halo_ref)
        for g, w in enumerate(POOL_WINDOWS):
            mixed = _dot(_pooled(i, g, w, ubuf).astype(BF16), wp_ref[g])
            cols = slice(g * POOL_GROUP_DIM, (g + 1) * POOL_GROUP_DIM)
            out_ref[:, cols] = (mixed * sc_ref[:, cols]).astype(BF16)
        kv = jnp.concatenate([kvp_ref[...], kvc_ref[...]], axis=0)
        k_var = _head_variants(kv[:, 0:2 * HEAD_DIM])
        v_var = _head_variants(kv[:, 2 * HEAD_DIM:])
        for hq in range(N_Q_HEADS):
            j, half, h = hq // 2, hq % 2, hq // GQA_GROUP
            q2 = q_ref[:, 2 * HEAD_DIM * j:2 * HEAD_DIM * (j + 1)].astype(BF16)
            s_all[hq] = _dot_nt(q2, k_var[h][half])
        for hq in range(N_Q_HEADS):
            for r in range(0, BLOCK, ROW_CHUNK):
                rows = slice(r, r + ROW_CHUNK)
                probs, _ = _head_probs(i, hq, rows, s_all, biasm_ref, sinks_ref)
                p_all[hq, rows, :] = probs.astype(BF16)
        for j in range(N_Q_HEADS // 2):
            h = 2 * j // GQA_GROUP
            acc = _dot(p_all[2 * j], v_var[h][0]) + _dot(p_all[2 * j + 1], v_var[h][1])
            out_ref[:, POOL_WIDTH + 2 * HEAD_DIM * j:POOL_WIDTH + 2 * HEAD_DIM * (j + 1)] = acc.astype(BF16)

        pl.when(i == nb - 1)(finish)

    return pl.pallas_call(
        body, name="mixers_fwd", grid=(nb,),
        out_shape=[jax.ShapeDtypeStruct((t, 2 * POOL_WIDTH), BF16)]
        + [jax.ShapeDtypeStruct((N_DEV, *sh.shape), sh.dtype) for sh in shards],
        in_specs=_mixer_in_specs(lambda i: i, lambda i: jnp.maximum(i - 1, 0)) + _mixer_param_specs() + [ANY] * ns,
        out_specs=[pl.BlockSpec((BLOCK, 2 * POOL_WIDTH), lambda i: (i, 0))] + [ANY] * ns,
        scratch_shapes=[pltpu.VMEM((HALO + BLOCK, POOL_WIDTH), F32), pltpu.VMEM((N_Q_HEADS, BLOCK, 2 * BLOCK), F32),
                        pltpu.VMEM((N_Q_HEADS, BLOCK, 2 * BLOCK), BF16)] + _gather_scratch(shards),
        compiler_params=_params(),
    )(proj, proj, proj, proj, proj, biasm, sinks, w_pool, pool_scale, *shards)


def outproj_norm(cat, w, x, g, g_next, shard):
    t, d = x.shape
    tm = TOKEN_TILE
    last = t // tm - 1

    def body(c_ref, w_ref, x_ref, g_ref, gn_ref, shard_ref, mix_ref, x1_ref, h2_ref, gathered_ref,
             send_sems, recv_sems, local_sems, bounce):
        i = pl.program_id(0)
        start, finish = _gather_plan([shard_ref], [gathered_ref], send_sems, recv_sems, local_sems, [bounce])
        pl.when(i == 0)(start)
        mix = _dot(c_ref[...], w_ref[...])
        mix_ref[...] = mix
        x1 = x_ref[...] + (mix * _rstd(mix)) * g_ref[...]
        x1_ref[...] = x1
        h2_ref[...] = ((x1 * _rstd(x1)) * gn_ref[...]).astype(BF16)
        pl.when(i == last)(finish)

    row = pl.BlockSpec((tm, d), lambda i: (i, 0))
    gain = pl.BlockSpec((1, d), lambda i: (0, 0))
    return pl.pallas_call(
        body, name="outproj_norm", grid=(t // tm,),
        out_shape=[jax.ShapeDtypeStruct((t, d), F32), jax.ShapeDtypeStruct((t, d), F32), jax.ShapeDtypeStruct((t, d), BF16),
                   jax.ShapeDtypeStruct((N_DEV, *shard.shape), shard.dtype)],
        in_specs=[pl.BlockSpec((tm, cat.shape[1]), lambda i: (i, 0)), pl.BlockSpec(w.shape, lambda i: (0, 0)), row, gain, gain, ANY],
        out_specs=[row, row, row, ANY],
        scratch_shapes=_gather_scratch([shard]),
        compiler_params=_params(),
    )(cat, w, x, g, g_next, shard)


def ffn_up(h, gate_t, up_t, down_shard):
    t, d = h.shape
    n = gate_t.shape[1]
    f = N_DEV * n
    tm, ts = FFN_TOKEN_TILE, FF_SHARDS_PER_TILE
    tn = ts * n
    steps = (f // tn, t // tm)

    def body(h_ref, wg_ref, wu_ref, shard_ref, gate_ref, up_ref, a_ref, gathered_ref,
             send_sems, recv_sems, local_sems, bounce):
        j, i = pl.program_id(0), pl.program_id(1)
        start, finish = _gather_plan([shard_ref], [gathered_ref], send_sems, recv_sems, local_sems, [bounce])
        pl.when((i == 0) & (j == 0))(start)

        hv = h_ref[...]
        gate = _dot_nt(hv, _merge_rows(wg_ref[...]))
        up = _dot_nt(hv, _merge_rows(wu_ref[...]))
        gate_ref[...] = gate.astype(BF16)
        up_ref[...] = up.astype(BF16)
        a_ref[...] = (gate * (1.0 / (1.0 + jnp.exp(-gate))) * up).astype(BF16)

        pl.when((j == steps[0] - 1) & (i == steps[1] - 1))(finish)

    wide = pl.BlockSpec((tm, tn), lambda j, i: (i, j))
    return pl.pallas_call(
        body, name="ffn_up", grid=steps,
        out_shape=[jax.ShapeDtypeStruct((t, f), BF16)] * 3
        + [jax.ShapeDtypeStruct((N_DEV, *down_shard.shape), down_shard.dtype)],
        in_specs=[pl.BlockSpec((tm, d), lambda j, i: (i, 0)),
                  pl.BlockSpec((ts, n, d), lambda j, i: (j, 0, 0)),
                  pl.BlockSpec((ts, n, d), lambda j, i: (j, 0, 0)), ANY],
        out_specs=[wide, wide, wide, ANY],
        scratch_shapes=_gather_scratch([down_shard]),
        compiler_params=_params(),
    )(h, gate_t, up_t, down_shard)


def ffn_down_loss(a, w_down, x1, g, target):
    t, d = x1.shape
    tm = TOKEN_TILE

    def body(a_ref, w_ref, x_ref, g_ref, t_ref, df_ref, dy_ref, dg_ref, loss_ref):
        @pl.when(pl.program_id(0) == 0)
        def _():
            dg_ref[...] = jnp.zeros_like(dg_ref)
            loss_ref[...] = jnp.zeros_like(loss_ref)

        f = _dot(a_ref[...], _merge_rows(w_ref[...]))
        r = _rstd(f)
        g = g_ref[...]
        err = x_ref[...] + (f * r) * g - t_ref[...]
        loss_ref[...] += 0.5 * jnp.sum(jnp.mean(err * err, axis=-1, keepdims=True))
        dy = err * (1.0 / d)
        dy_ref[...] = dy
        df, dg_rows = _norm_bwd(dy, f, r, g)
        df_ref[...] = df.astype(BF16)
        dg_ref[...] += _as_rows(jnp.sum(dg_rows, axis=0, keepdims=True))

    row = pl.BlockSpec((tm, d), lambda i: (i, 0))
    gain = pl.BlockSpec((1, d), lambda i: (0, 0))
    return pl.pallas_call(
        body, name="ffn_down_loss", grid=(t // tm,),
        out_shape=[jax.ShapeDtypeStruct((t, d), BF16), jax.ShapeDtypeStruct((t, d), F32),
                   jax.ShapeDtypeStruct((d // 128, 128), F32), jax.ShapeDtypeStruct((1, 128), F32)],
        in_specs=[pl.BlockSpec((tm, a.shape[1]), lambda i: (i, 0)), pl.BlockSpec(w_down.shape, lambda i: (0, 0, 0)), row, gain, row],
        out_specs=[row, row, pl.BlockSpec((d // 128, 128), lambda i: (0, 0)), pl.BlockSpec((1, 128), lambda i: (0, 0))],
        compiler_params=_params(),
    )(a, w_down, x1, g, target)


def ffn_down_bwd(df, w_down, gate, up):
    t, d = df.shape
    n = w_down.shape[1]
    f = gate.shape[1]
    tm, ts = FFN_TOKEN_TILE, FF_SHARDS_PER_TILE
    tn = ts * n

    def body(df_ref, w_ref, gate_ref, up_ref, dgate_ref, dup_ref):
        da = _dot_nt(df_ref[...], _merge_rows(w_ref[...]))
        gate = gate_ref[...].astype(F32)
        sig = 1.0 / (1.0 + jnp.exp(-gate))
        dgate_ref[...] = (da * up_ref[...].astype(F32) * (sig * (1.0 + gate * (1.0 - sig)))).astype(BF16)
        dup_ref[...] = (da * (gate * sig)).astype(BF16)

    wide = pl.BlockSpec((tm, tn), lambda j, i: (i, j))
    return pl.pallas_call(
        body, name="ffn_down_bwd", grid=(f // tn, t // tm),
        out_shape=[jax.ShapeDtypeStruct((t, f), BF16)] * 2,
        in_specs=[pl.BlockSpec((tm, d), lambda j, i: (i, 0)), pl.BlockSpec((ts, n, d), lambda j, i: (j, 0, 0)), wide, wide],
        out_specs=[wide, wide],
        compiler_params=_params(),
    )(df, w_down, gate, up)


def grad_rows(a, b, name, by_core=False):
    t, m = a.shape
    d = b.shape[1]
    r = m // N_DEV
    tt = TOKEN_TILE
    last = t // tt - 1
    out_shape = (2, N_CHIP, r, d) if by_core else (N_DEV, r, d)

    def body(a_ref, b_ref, out_ref, acc):
        k = pl.program_id(0)

        @pl.when(k == 0)
        def _():
            acc[...] = jnp.zeros_like(acc)

        acc[...] += _dot_tn(a_ref[...], b_ref[...])

        @pl.when(k == last)
        def _():
            if by_core:
                blocks = acc[...].reshape(N_CHIP, 2, r, d)
                for chip in range(N_CHIP):
                    for core in range(2):
                        out_ref[core, chip] = blocks[chip, core].astype(BF16)
            else:
                out_ref[...] = acc[...].reshape(out_shape).astype(BF16)

    return pl.pallas_call(
        body, name=name, grid=(t // tt,),
        out_shape=jax.ShapeDtypeStruct(out_shape, BF16),
        in_specs=[pl.BlockSpec((tt, m), lambda k: (k, 0)), pl.BlockSpec((tt, d), lambda k: (k, 0))],
        out_specs=pl.BlockSpec(out_shape, lambda k: (0,) * len(out_shape)),
        scratch_shapes=[pltpu.VMEM((m, d), F32)],
        compiler_params=_params(),
    )(a, b)


def grad_ffn(lhs, b, name, chip_parts=()):
    t, f = lhs[0].shape
    d = b.shape[1]
    nw = len(lhs)
    na = len(chip_parts)
    n = f // N_DEV
    tt, ts = TOKEN_TILE, FF_SHARDS_PER_TILE
    tn = ts * n
    steps = (f // tn, t // tt)

    def body(*refs):
        a_refs, b_ref, part_refs = refs[:nw], refs[nw], refs[nw + 1:nw + 1 + na]
        out_refs = refs[nw + 1 + na:2 * nw + 1 + na]
        slot_refs = refs[2 * nw + 1 + na:2 * nw + 1 + 2 * na]
        acc = refs[2 * nw + 1 + 2 * na]
        i, k = pl.program_id(0), pl.program_id(1)
        if na:
            send_sems, recv_sems, local_sems = refs[2 * nw + 2 + 2 * na:2 * nw + 5 + 2 * na]
            start, finish = _chip_exchange_plan(part_refs, slot_refs, send_sems, recv_sems, local_sems,
                                                refs[2 * nw + 5 + 2 * na:])
            pl.when((i == 0) & (k == 0))(start)

        @pl.when(k == 0)
        def _():
            acc[...] = jnp.zeros_like(acc)

        for w in range(nw):
            acc[w] += _dot_tn(a_refs[w][...], b_ref[...])

        @pl.when(k == steps[1] - 1)
        def _():
            for w in range(nw):
                blocks = acc[w].reshape(ts // 2, 2, n, d)
                for chip in range(ts // 2):
                    for core in range(2):
                        out_refs[w][core, chip] = blocks[chip, core].astype(BF16)

        if na:
            pl.when((i == steps[0] - 1) & (k == steps[1] - 1))(finish)

    out = pl.pallas_call(
        body, name=name, grid=steps,
        out_shape=[jax.ShapeDtypeStruct((2, N_CHIP, n, d), BF16)] * nw
        + [jax.ShapeDtypeStruct(p.shape, p.dtype) for p in chip_parts],
        in_specs=[pl.BlockSpec((tt, tn), lambda i, k: (k, i))] * nw + [pl.BlockSpec((tt, d), lambda i, k: (k, 0))] + [ANY] * na,
        out_specs=[pl.BlockSpec((2, ts // 2, n, d), lambda i, k: (0, i, 0, 0))] * nw + [ANY] * na,
        scratch_shapes=[pltpu.VMEM((nw, tn, d), F32)] + (_chip_exchange_scratch(chip_parts) if na else []),
        compiler_params=_params(),
    )(*lhs, b, *chip_parts)
    return out[:nw], out[nw:]


def ffn_up_bwd(dgate, dup, gate_t, up_t, x1, g_ffn, dy, mix, g_mix, chip_parts):
    t, d = x1.shape
    n = gate_t.shape[1]
    f = N_DEV * n
    tm = TOKEN_TILE
    na = len(chip_parts)
    last = t // tm - 1

    def body(*refs):
        dg_ref, du_ref, wg_ref, wu_ref, x_ref, gf_ref, dy_ref, mix_ref, gm_ref = refs[:9]
        part_refs = refs[9:9 + na]
        dx1_ref, dmix_ref, dgf_ref, dgm_ref = refs[9 + na:13 + na]
        slot_refs = refs[13 + na:13 + 2 * na]
        send_sems, recv_sems, local_sems = refs[13 + 2 * na:16 + 2 * na]
        i = pl.program_id(0)
        start, finish = _chip_exchange_plan(part_refs, slot_refs, send_sems, recv_sems, local_sems, refs[16 + 2 * na:])

        @pl.when(i == 0)
        def _():
            start()
            dgf_ref[...] = jnp.zeros_like(dgf_ref)
            dgm_ref[...] = jnp.zeros_like(dgm_ref)

        dh = _dot(dg_ref[...], _merge_rows(wg_ref[...])) + _dot(du_ref[...], _merge_rows(wu_ref[...]))
        x1 = x_ref[...]
        dx, dgf_rows = _norm_bwd(dh, x1, _rstd(x1), gf_ref[...])
        dx1 = dy_ref[...] + dx
        dx1_ref[...] = dx1
        dgf_ref[...] += _as_rows(jnp.sum(dgf_rows, axis=0, keepdims=True))
        mix = mix_ref[...]
        dmix, dgm_rows = _norm_bwd(dx1, mix, _rstd(mix), gm_ref[...])
        dmix_ref[...] = dmix.astype(BF16)
        dgm_ref[...] += _as_rows(jnp.sum(dgm_rows, axis=0, keepdims=True))
        pl.when(i == last)(finish)

    row = pl.BlockSpec((tm, d), lambda i: (i, 0))
    wide = pl.BlockSpec((tm, f), lambda i: (i, 0))
    gain = pl.BlockSpec((1, d), lambda i: (0, 0))
    gain_rows = pl.BlockSpec((d // 128, 128), lambda i: (0, 0))
    whole = pl.BlockSpec((N_DEV, n, d), lambda i: (0, 0, 0), pipeline_mode=pl.Buffered(1))
    out = pl.pallas_call(
        body, name="ffn_up_bwd", grid=(t // tm,),
        out_shape=[jax.ShapeDtypeStruct((t, d), F32), jax.ShapeDtypeStruct((t, d), BF16),
                   jax.ShapeDtypeStruct((d // 128, 128), F32), jax.ShapeDtypeStruct((d // 128, 128), F32)]
        + [jax.ShapeDtypeStruct(p.shape, p.dtype) for p in chip_parts],
        in_specs=[wide, wide, whole, whole, row, gain, row, row, gain] + [ANY] * na,
        out_specs=[row, row, gain_rows, gain_rows] + [ANY] * na,
        scratch_shapes=_chip_exchange_scratch(chip_parts),
        compiler_params=_params(),
    )(dgate, dup, gate_t, up_t, x1, g_ffn, dy, mix, g_mix, *chip_parts)
    return out[:4], out[4:]


def outproj_bwd(dmix, w_out):
    t, d = dmix.shape
    tm = TOKEN_TILE

    def body(dm_ref, w_ref, out_ref):
        out_ref[...] = _dot_nt(dm_ref[...], w_ref[...])

    return pl.pallas_call(
        body, name="outproj_bwd", grid=(t // tm,),
        out_shape=jax.ShapeDtypeStruct((t, w_out.shape[0]), F32),
        in_specs=[pl.BlockSpec((tm, d), lambda i: (i, 0)), pl.BlockSpec(w_out.shape, lambda i: (0, 0))],
        out_specs=pl.BlockSpec((tm, w_out.shape[0]), lambda i: (i, 0)),
        compiler_params=_params(),
    )(dmix, w_out)


def mixers_bwd(proj, dcat, biasm, sinks, w_pool, pool_scale, ffn_parts):
    t = proj.shape[0]
    nb = t // BLOCK
    na = len(ffn_parts)

    def body(*refs):
        (uc_ref, halo_ref, q_ref, kvc_ref, kvp_ref, dcat_ref, biasm_ref, sinks_ref, wp_ref, sc_ref) = refs[:10]
        part_refs = refs[10:10 + na]
        dproj_ref, dbias_ref, dsink_ref, dwp_ref, dsc_ref = refs[10 + na:15 + na]
        slot_refs = refs[15 + na:15 + 2 * na]
        ubuf, dbuf, c_u, c_q, c_kv, s_all, dp_all, ds_all, p_all = refs[15 + 2 * na:24 + 2 * na]
        send_sems, recv_sems, local_sems = refs[24 + 2 * na:27 + 2 * na]
        bounce = refs[27 + 2 * na:]
        i = pl.program_id(0)
        lane = lax.broadcasted_iota(jnp.int32, (1, 128), 1)
        start, finish = _chip_exchange_plan(part_refs, slot_refs, send_sems, recv_sems, local_sems, bounce)

        @pl.when(i == 0)
        def _():
            start()
            dbias_ref[...] = jnp.zeros_like(dbias_ref)
            dwp_ref[...] = jnp.zeros_like(dwp_ref)
            dsc_ref[...] = jnp.zeros_like(dsc_ref)
            dsink_ref[...] = jnp.zeros_like(dsink_ref)
            dbuf[...] = jnp.zeros_like(dbuf)
            c_u[...] = jnp.zeros_like(c_u)
            c_q[...] = jnp.zeros_like(c_q)
            c_kv[...] = jnp.zeros_like(c_kv)

        @pl.when(i < nb)
        def _():
            _fill_pool_input(i, ubuf, uc_ref, halo_ref)
            for g, w in enumerate(POOL_WINDOWS):
                cols = slice(g * POOL_GROUP_DIM, (g + 1) * POOL_GROUP_DIM)
                pooled = _pooled(i, g, w, ubuf).astype(BF16)
                mixed = _dot(pooled, wp_ref[g])
                dout = dcat_ref[:, cols]
                dsc_ref[g:g + 1, :] += jnp.sum(dout * mixed, axis=0, keepdims=True)
                dmixed = (dout * sc_ref[:, cols]).astype(BF16)
                dwp_ref[g] += _dot_tn(pooled, dmixed)
                dpooled = _dot_nt(dmixed, wp_ref[g])
                scaled = dpooled * _inv_count(i, w)
                dbuf[BLOCK:, cols] = scaled[0:HALO]
                dproj_ref[:, cols] = (_window_sum(dbuf, g, w, lambda k: k) + c_u[:, cols]).astype(BF16)
                dbuf[0:BLOCK, cols] = scaled
                c_u[:, cols] = -dpooled

            kv = jnp.concatenate([kvp_ref[...], kvc_ref[...]], axis=0)
            k_var = _head_variants(kv[:, 0:2 * HEAD_DIM])
            v_var = _head_variants(kv[:, 2 * HEAD_DIM:])
            q2s = [q_ref[:, 2 * HEAD_DIM * j:2 * HEAD_DIM * (j + 1)].astype(BF16) for j in range(N_Q_HEADS // 2)]
            do2s = [dcat_ref[:, POOL_WIDTH + 2 * HEAD_DIM * j:POOL_WIDTH + 2 * HEAD_DIM * (j + 1)].astype(BF16)
                    for j in range(N_Q_HEADS // 2)]
            slot = lambda hq: 4 * (hq // GQA_GROUP) + 2 * (hq % 2) + (hq % GQA_GROUP) // 2
            for hq in range(N_Q_HEADS):
                j, half, h = hq // 2, hq % 2, hq // GQA_GROUP
                s_all[hq] = _dot_nt(q2s[j], k_var[h][half])
                dp_all[hq] = _dot_nt(do2s[j], v_var[h][half])
            dsink_row = jnp.zeros((1, 128), F32)
            for hq in range(N_Q_HEADS):
                dsink = 0.0
                for r in range(0, BLOCK, ROW_CHUNK):
                    rows = slice(r, r + ROW_CHUNK)
                    probs, p_sink = _head_probs(i, hq, rows, s_all, biasm_ref, sinks_ref)
                    dp = dp_all[hq, rows, :]
                    delta = jnp.sum(probs * dp, axis=-1, keepdims=True)
                    ds = probs * (dp - delta)
                    dbias_ref[hq, rows, :] += ds
                    dsink = dsink + jnp.sum(p_sink * delta)
                    ds_all[slot(hq), rows, :] = (ds * ATTN_SCALE).astype(BF16)
                    p_all[slot(hq), rows, :] = probs.astype(BF16)
                dsink_row = dsink_row - jnp.where(lane == hq, dsink, 0.0)
            dsink_ref[...] += dsink_row
            dq2 = [None] * (N_Q_HEADS // 2)
            for hq in range(N_Q_HEADS):
                j, half, h = hq // 2, hq % 2, hq // GQA_GROUP
                dq = _dot(ds_all[slot(hq)], k_var[h][half])
                dq2[j] = dq if dq2[j] is None else dq2[j] + dq
            low = lax.broadcasted_iota(jnp.int32, (2 * BLOCK, 2 * HEAD_DIM), 1) < HEAD_DIM
            dk_half, dv_half = [[None, None], [None, None]], [[None, None], [None, None]]
            for h in range(N_KV_HEADS):
                for half in range(2):
                    heads = [hq for hq in range(GQA_GROUP * h, GQA_GROUP * (h + 1)) if hq % 2 == half]
                    base = slot(heads[0])
                    q_rows = jnp.concatenate([q2s[hq // 2] for hq in heads], axis=0)
                    do_rows = jnp.concatenate([do2s[hq // 2] for hq in heads], axis=0)
                    dk_half[h][half] = _dot_tn(_merge_rows(ds_all[base:base + 2]), q_rows)
                    dv_half[h][half] = _dot_tn(_merge_rows(p_all[base:base + 2]), do_rows)

            def pair_of(halves):
                return jnp.where(low, halves[0][0] + pltpu.roll(halves[0][1], HEAD_DIM, 1),
                                 halves[1][1] + pltpu.roll(halves[1][0], HEAD_DIM, 1))

            dkv = jnp.concatenate([pair_of(dk_half), pair_of(dv_half)], axis=1)
            dproj_ref[:, POOL_WIDTH:2 * POOL_WIDTH] = c_q[...].astype(BF16)
            dproj_ref[:, 2 * POOL_WIDTH:] = (c_kv[...] + dkv[0:BLOCK]).astype(BF16)
            c_q[...] = jnp.concatenate(dq2, axis=1)
            c_kv[...] = dkv[BLOCK:]

        @pl.when(i == nb)
        def _():
            dbuf[BLOCK:, :] = jnp.zeros((HALO, POOL_WIDTH), F32)
            for g, w in enumerate(POOL_WINDOWS):
                cols = slice(g * POOL_GROUP_DIM, (g + 1) * POOL_GROUP_DIM)
                dproj_ref[:, cols] = (_window_sum(dbuf, g, w, lambda k: k) + c_u[:, cols]).astype(BF16)
            dproj_ref[:, POOL_WIDTH:2 * POOL_WIDTH] = c_q[...].astype(BF16)
            dproj_ref[:, 2 * POOL_WIDTH:] = c_kv[...].astype(BF16)
            finish()

    cur = lambda i: jnp.minimum(i, nb - 1)
    prv = lambda i: jnp.maximum(jnp.minimum(i, nb - 1) - 1, 0)
    out = pl.pallas_call(
        body, name="mixers_bwd", grid=(nb + 1,),
        out_shape=[jax.ShapeDtypeStruct((t, proj.shape[1]), BF16),
                   jax.ShapeDtypeStruct((N_Q_HEADS, BLOCK, 2 * BLOCK), F32),
                   jax.ShapeDtypeStruct((1, 128), F32),
                   jax.ShapeDtypeStruct((4, POOL_GROUP_DIM, POOL_GROUP_DIM), F32),
                   jax.ShapeDtypeStruct((len(POOL_WINDOWS), POOL_GROUP_DIM), F32)]
        + [jax.ShapeDtypeStruct(p.shape, p.dtype) for p in ffn_parts],
        in_specs=_mixer_in_specs(cur, prv) + [pl.BlockSpec((BLOCK, 2 * POOL_WIDTH), lambda i: (cur(i), 0))]
        + _mixer_param_specs() + [ANY] * na,
        out_specs=[pl.BlockSpec((BLOCK, proj.shape[1]), lambda i: (jnp.maximum(i - 1, 0), 0)),
                   pl.BlockSpec((N_Q_HEADS, BLOCK, 2 * BLOCK), lambda i: (0, 0, 0)),
                   pl.BlockSpec((1, 128), lambda i: (0, 0)),
                   pl.BlockSpec((4, POOL_GROUP_DIM, POOL_GROUP_DIM), lambda i: (0, 0, 0)),
                   pl.BlockSpec((len(POOL_WINDOWS), POOL_GROUP_DIM), lambda i: (0, 0))] + [ANY] * na,
        scratch_shapes=[pltpu.VMEM((HALO + BLOCK, POOL_WIDTH), F32), pltpu.VMEM((BLOCK + HALO, POOL_WIDTH), F32),
                        pltpu.VMEM((BLOCK, POOL_WIDTH), F32), pltpu.VMEM((BLOCK, POOL_WIDTH), F32),
                        pltpu.VMEM((BLOCK, 256), F32),
                        pltpu.VMEM((N_Q_HEADS, BLOCK, 2 * BLOCK), F32), pltpu.VMEM((N_Q_HEADS, BLOCK, 2 * BLOCK), F32),
                        pltpu.VMEM((N_Q_HEADS, BLOCK, 2 * BLOCK), BF16), pltpu.VMEM((N_Q_HEADS, BLOCK, 2 * BLOCK), BF16)]
        + _chip_exchange_scratch(ffn_parts),
        compiler_params=_params(),
    )(proj, proj, proj, proj, proj, dcat, biasm, sinks, w_pool, pool_scale, *ffn_parts)
    return out[:5], out[5:]


def inproj_bwd(dproj, w_in_t, x, g, dx1, chip_parts):
    t, d = x.shape
    n = dproj.shape[1]
    tm = TOKEN_TILE
    na = len(chip_parts)
    last = t // tm - 1

    def body(*refs):
        dp_ref, w_ref, x_ref, g_ref, dx1_ref = refs[:5]
        part_refs = refs[5:5 + na]
        dx_ref, dg_ref = refs[5 + na:7 + na]
        slot_refs = refs[7 + na:7 + 2 * na]
        send_sems, recv_sems, local_sems = refs[7 + 2 * na:10 + 2 * na]
        i = pl.program_id(0)
        start, finish = _chip_exchange_plan(part_refs, slot_refs, send_sems, recv_sems, local_sems, refs[10 + 2 * na:])

        @pl.when(i == 0)
        def _():
            start()
            dg_ref[...] = jnp.zeros_like(dg_ref)

        dh = _dot(dp_ref[...], w_ref[...])
        xv = x_ref[...]
        dx, dg_rows = _norm_bwd(dh, xv, _rstd(xv), g_ref[...])
        dx_ref[...] = dx1_ref[...] + dx
        dg_ref[...] += _as_rows(jnp.sum(dg_rows, axis=0, keepdims=True))
        pl.when(i == last)(finish)

    row = pl.BlockSpec((tm, d), lambda i: (i, 0))
    gain = pl.BlockSpec((1, d), lambda i: (0, 0))
    out = pl.pallas_call(
        body, name="inproj_bwd", grid=(t // tm,),
        out_shape=[jax.ShapeDtypeStruct((t, d), F32), jax.ShapeDtypeStruct((d // 128, 128), F32)]
        + [jax.ShapeDtypeStruct(p.shape, p.dtype) for p in chip_parts],
        in_specs=[pl.BlockSpec((tm, n), lambda i: (i, 0)), pl.BlockSpec(w_in_t.shape, lambda i: (0, 0)), row, gain, row]
        + [ANY] * na,
        out_specs=[row, pl.BlockSpec((d // 128, 128), lambda i: (0, 0))] + [ANY] * na,
        scratch_shapes=_chip_exchange_scratch(chip_parts),
        compiler_params=_params(),
    )(dproj, w_in_t, x, g, dx1, *chip_parts)
    return out[0], out[1], out[2:]


def _bucket_band():
    qi = jnp.arange(BLOCK)[:, None]
    kj = jnp.arange(2 * BLOCK)[None, :]
    dist = qi + BLOCK - kj
    n = jnp.maximum(dist, 0)
    nf = jnp.maximum(n, 1).astype(F32)
    large = MAX_EXACT + (jnp.log(nf / MAX_EXACT) / np.float32(np.log(MAX_DISTANCE / MAX_EXACT))
                         * (N_BUCKETS - MAX_EXACT)).astype(jnp.int32)
    large = jnp.minimum(large, N_BUCKETS - 1)
    bucket = jnp.where(n < MAX_EXACT, n, large)
    in_window = (dist >= 0) & (dist < BLOCK)
    return bucket.astype(F32), in_window.astype(F32)


def kernel(x, g_pre_mix, w_in, w_pool, pool_scale, rel_bias, sinks, w_out, g_post_mix, g_pre_ffn, w_gate, w_up, w_down, g_post_ffn, loss_target, m_g_pre_mix, m_w_in, m_w_pool, m_pool_scale, m_rel_bias, m_sinks, m_w_out, m_g_post_mix, m_g_pre_ffn, m_w_gate, m_w_up, m_w_down, m_g_post_ffn, v_g_pre_mix, v_w_in, v_w_pool, v_pool_scale, v_rel_bias, v_sinks, v_w_out, v_g_post_mix, v_g_pre_ffn, v_w_gate, v_w_up, v_w_down, v_g_post_ffn):
    d = x.shape[-1]
    xs, target = x[0], loss_target[0]

    w_in_ts = w_in[0].T.astype(BF16)
    w_out_s = w_out[0].astype(BF16)
    gate_ts = w_gate[0].T.astype(BF16)
    up_ts = w_up[0].T.astype(BF16)
    w_down_s = w_down[0].astype(BF16)
    w_in_t, = gather_blocks([w_in_ts], "gather_w_in")
    w_in_t = w_in_t.reshape(-1, d)

    bucket, in_window = _bucket_band()
    biasm = bias_band(bucket, in_window, rel_bias)
    w_pool_b = w_pool[0].astype(BF16)
    proj, h1, w_out_f = norm_inproj(xs, g_pre_mix, w_in_t, w_out_s)
    w_out_f = w_out_f.reshape(-1, d)
    cat, gate_t = mixers_fwd(proj, biasm, sinks, w_pool_b, pool_scale, [gate_ts])
    mix, x1, h2, up_t = outproj_norm(cat, w_out_f, xs, g_post_mix, g_pre_ffn, up_ts)
    gate, up, act, w_down_f = ffn_up(h2, gate_t, up_t, w_down_s)
    df, dy, dg_post_ffn, loss_part = ffn_down_loss(act, w_down_f, x1, g_post_ffn, target)

    def pair_sum(parts, tag):
        return pair_add(parts, pair_exchange(parts, "pair_exchange_" + tag), "pair_add_" + tag)

    dgate, dup = ffn_down_bwd(df, w_down_f, gate, up)
    (d_gate, d_up), _ = grad_ffn([dgate, dup], h2, "grad_w_gate_up")
    q_gate, q_up = pair_sum([d_gate, d_up], "gate_up")
    (d_down,), gate_slots = grad_ffn([act], df, "grad_w_down", [q_gate])
    q_down, = pair_sum([d_down], "down")
    (dx1, dmix, dg_pre_ffn, dg_post_mix), up_slots = ffn_up_bwd(
        dgate, dup, gate_t, up_t, x1, g_pre_ffn, dy, mix, g_post_mix, [q_up])
    dcat = outproj_bwd(dmix, w_out_f)
    d_out = grad_rows(cat, dmix, "grad_w_out", by_core=True)
    q_out, = pair_sum([d_out], "out")
    (dproj, dbias, dsinks, dw_pool, dpool_scale), down_out_slots = mixers_bwd(
        proj, dcat, biasm, sinks, w_pool_b, pool_scale, [q_down, q_out])
    drel_bias = bias_band_bwd(bucket, dbias)
    d_in_t = grad_rows(dproj, h1, "grad_w_in", by_core=True)
    q_in, = pair_sum([d_in_t], "in")
    grad_x, dg_pre_mix, in_slots = inproj_bwd(dproj, w_in_t, xs, g_pre_mix, dx1, [q_in])

    small_w = [g_pre_mix, g_post_mix, g_pre_ffn, g_post_ffn, pool_scale, sinks, w_pool, rel_bias]
    small_m = [m_g_pre_mix, m_g_post_mix, m_g_pre_ffn, m_g_post_ffn, m_pool_scale, m_sinks, m_w_pool, m_rel_bias]
    small_v = [v_g_pre_mix, v_g_post_mix, v_g_pre_ffn, v_g_post_ffn, v_pool_scale, v_sinks, v_w_pool, v_rel_bias]
    loss_row, sm = tail_reduce(
        [dg_pre_mix, dg_post_mix, dg_pre_ffn, dg_post_ffn], dpool_scale, dsinks, loss_part, dw_pool, drel_bias,
        small_w, small_m, small_v)
    g_gate_t, g_up_t, g_down, g_out, g_in_t = sum_slots(
        [*gate_slots, *up_slots, *down_out_slots, *in_slots], "sum_slots")
    big_w = [w_in[0], w_out[0], w_gate[0], w_up[0], w_down[0]]
    big_g = [g_in_t.T, g_out, g_gate_t.T, g_up_t.T, g_down]
    big_m = [m_w_in[0], m_w_out[0], m_w_gate[0], m_w_up[0], m_w_down[0]]
    big_v = [v_w_in[0], v_w_out[0], v_w_gate[0], v_w_up[0], v_w_down[0]]
    upd = adamw_update(big_w[:2], big_g[:2], big_m[:2], big_v[:2], "adamw_mix") \
        + adamw_update(big_w[2:], big_g[2:], big_m[2:], big_v[2:], "adamw_ffn")
    big = [[big_g[k][None], *(u[None] for u in upd[k])] for k in range(5)]

    def ordered(kind):
        s, b = [p[kind] for p in sm], [p[kind] for p in big]
        return [s[0], b[0], s[6], s[4], s[7], s[5], b[1], s[1], s[2], b[2], b[3], b[4], s[3]]

    return (loss_row[0, 0], grad_x[None], *ordered(0), *ordered(1), *ordered(2), *ordered(3))
```

```python
import numpy as np
import jax
import jax.numpy as jnp
from jax import lax
from jax.experimental import pallas as pl
from jax.experimental.pallas import tpu as pltpu

F32 = jnp.float32
BF16 = jnp.bfloat16

N_DEV = 8
N_CHIP = 4
POOL_WIDTH = 512
POOL_WINDOWS = (2, 4, 8, 16)
POOL_GROUP_DIM = 128
HEAD_DIM = 64
N_Q_HEADS = 8
N_KV_HEADS = 2
GQA_GROUP = 4
BLOCK = 128
HALO = 16
ROW_CHUNK = 32
N_BUCKETS = 32
MAX_EXACT = 16
MAX_DISTANCE = 128
EPS = 1e-6
NEG_INF = -1e30
ATTN_SCALE = float(1.0 / np.sqrt(np.float32(HEAD_DIM)))

ADAM_LR = 0.001
ADAM_B1 = 0.9
ADAM_B2 = 0.999
ADAM_EPS = 1e-08
ADAM_WD = 0.01
ADAM_STEP = 10

TOKEN_TILE = 512
FFN_TOKEN_TILE = 1024
FF_SHARDS_PER_TILE = 4
VMEM_LIMIT = 56 * 1024 * 1024
MESH = pl.DeviceIdType.MESH
ANY = pl.BlockSpec(memory_space=pl.ANY)
VMEM = pl.BlockSpec(memory_space=pltpu.VMEM)
SMEM = pl.BlockSpec(memory_space=pltpu.SMEM)


def _params(**kw):
    return pltpu.CompilerParams(vmem_limit_bytes=VMEM_LIMIT, **kw)


def _dot(a, b):
    return jnp.dot(a, b, preferred_element_type=F32)


def _dot_nt(a, b):
    return lax.dot_general(a, b, (((1,), (1,)), ((), ())), preferred_element_type=F32)


def _dot_tn(a, b):
    return lax.dot_general(a, b, (((0,), (0,)), ((), ())), preferred_element_type=F32)


def _rstd(v):
    return lax.rsqrt(jnp.mean(v * v, axis=-1, keepdims=True) + EPS)


def _norm_bwd(dout, v, r, g):
    vn = v * r
    dn = dout * g
    dv = r * (dn - vn * jnp.mean(dn * vn, axis=-1, keepdims=True))
    return dv, dout * vn


def _as_rows(v):
    return jnp.concatenate([v[:, k:k + 128] for k in range(0, v.shape[1], 128)], axis=0)


def _as_lanes(rows):
    return jnp.concatenate([rows[k:k + 1, :] for k in range(rows.shape[0])], axis=1)


def _merge_rows(value):
    s, r, c_ = value.shape
    return value.reshape(s * r, c_)


def _gather_plan(srcs, outs, send_sems, recv_sems, local_sems=None, bounce=None):
    n = len(srcs)
    x, y, c = lax.axis_index("x"), lax.axis_index("y"), lax.axis_index("c")
    me, sibling = (x, y, c), (x, y, 1 - c)
    chips = [(1 - x, y), (x, 1 - y), (1 - x, 1 - y)]

    def slot(a, px, py, pc):
        return outs[a].at[4 * px + 2 * py + pc]

    def copy(a, k, block, to, from_src=False):
        return pltpu.make_async_remote_copy(
            src_ref=srcs[a] if from_src else slot(a, *block), dst_ref=slot(a, *block),
            send_sem=send_sems.at[k * n + a], recv_sem=recv_sems.at[k * n + a], device_id=to, device_id_type=MESH)

    def own_in(a):
        return pltpu.make_async_copy(srcs[a], bounce[a], local_sems.at[a])

    def own_out(a):
        return pltpu.make_async_copy(bounce[a], slot(a, *me), local_sems.at[a])

    def first(a):
        return [copy(a, 0, me, sibling, True)] + [copy(a, 1 + j, me, (*chip, c), True) for j, chip in enumerate(chips)]

    def passed(a, j):
        return copy(a, 4 + j, (*chips[j], c), sibling)

    def start():
        for a in range(n):
            if bounce is not None:
                own_in(a).start()
            for cp in first(a):
                cp.start()

    def finish():
        if bounce is not None:
            for a in range(n):
                own_in(a).wait()
                own_out(a).start()
        for j, chip in enumerate(chips):
            for a in range(n):
                copy(a, 1 + j, (*chip, c), me).wait_recv()
                passed(a, j).start()
        for a in range(n):
            copy(a, 0, sibling, me).wait_recv()
            for j, chip in enumerate(chips):
                copy(a, 4 + j, (*chip, 1 - c), me).wait_recv()
        for a in range(n):
            for cp in first(a) + [passed(a, j) for j in range(3)]:
                cp.wait_send()
            if bounce is not None:
                own_out(a).wait()

    return start, finish


def _gather_scratch(shards):
    n = len(shards)
    return [pltpu.SemaphoreType.DMA((7 * n,)), pltpu.SemaphoreType.DMA((7 * n,)), pltpu.SemaphoreType.DMA((n,))] \
        + [pltpu.VMEM(s.shape, s.dtype) for s in shards]


def _chip_exchange_plan(srcs, outs, send_sems, recv_sems, local_sems, bounce):
    n = len(srcs)
    x, y, c = lax.axis_index("x"), lax.axis_index("y"), lax.axis_index("c")
    my_chip = 2 * x + y

    def copies():
        out = []
        for a in range(n):
            for k in range(1, N_CHIP):
                px, py = x ^ (k >> 1), y ^ (k & 1)
                out.append(pltpu.make_async_remote_copy(
                    src_ref=srcs[a].at[2 * px + py], dst_ref=outs[a].at[my_chip],
                    send_sem=send_sems.at[(k - 1) * n + a], recv_sem=recv_sems.at[(k - 1) * n + a],
                    device_id=(px, py, c), device_id_type=MESH))
        return out

    def own_in(a):
        return pltpu.make_async_copy(srcs[a].at[my_chip], bounce[a], local_sems.at[a])

    def own_out(a):
        return pltpu.make_async_copy(bounce[a], outs[a].at[my_chip], local_sems.at[a])

    def start():
        for a in range(n):
            own_in(a).start()
        for cp in copies():
            cp.start()

    def finish():
        for a in range(n):
            own_in(a).wait()
            own_out(a).start()
        for cp in copies():
            cp.wait()
        for a in range(n):
            own_out(a).wait()

    return start, finish


def _chip_exchange_scratch(parts):
    n = len(parts)
    return [pltpu.SemaphoreType.DMA((3 * n,)), pltpu.SemaphoreType.DMA((3 * n,)), pltpu.SemaphoreType.DMA((n,))] \
        + [pltpu.VMEM(p.shape[1:], p.dtype) for p in parts]


def gather_blocks(shards, name):
    def body(*refs):
        n = len(shards)
        start, finish = _gather_plan(refs[:n], refs[n:2 * n], *refs[2 * n:2 * n + 3], bounce=refs[2 * n + 3:])
        start()
        finish()

    return pl.pallas_call(
        body, name=name,
        out_shape=[jax.ShapeDtypeStruct((N_DEV, *s.shape), s.dtype) for s in shards],
        in_specs=[ANY] * len(shards), out_specs=[ANY] * len(shards),
        scratch_shapes=_gather_scratch(shards),
    )(*shards)


def pair_exchange(parts, name):
    n = len(parts)

    def body(*refs):
        p_refs, got_refs, send_sems, recv_sems = refs[:n], refs[n:2 * n], *refs[2 * n:]
        x, y, c = lax.axis_index("x"), lax.axis_index("y"), lax.axis_index("c")
        copies = [pltpu.make_async_remote_copy(
            src_ref=p_refs[a].at[1 - c], dst_ref=got_refs[a], send_sem=send_sems.at[a], recv_sem=recv_sems.at[a],
            device_id=(x, y, 1 - c), device_id_type=MESH) for a in range(n)]
        for cp in copies:
            cp.start()
        for cp in copies:
            cp.wait()

    return pl.pallas_call(
        body, name=name, out_shape=[jax.ShapeDtypeStruct(p.shape[1:], p.dtype) for p in parts],
        in_specs=[ANY] * n, out_specs=[ANY] * n,
        scratch_shapes=[pltpu.SemaphoreType.DMA((n,)), pltpu.SemaphoreType.DMA((n,))],
    )(*parts)


def pair_add(parts, got, name):
    n = len(parts)

    def body(core_ref, *refs):
        for a in range(n):
            refs[2 * n + a][...] = (refs[a][...].astype(F32) + refs[n + a][...].astype(F32)).astype(BF16)

    def own(p):
        zeros = (0,) * (p.ndim - 2)
        return pl.BlockSpec((None, 1, *p.shape[2:]), lambda i, core: (core[0], i, *zeros))

    def plain(p):
        zeros = (0,) * (p.ndim - 1)
        return pl.BlockSpec((1, *p.shape[1:]), lambda i, core: (i, *zeros))

    core = lax.axis_index("c").astype(jnp.int32).reshape(1)
    return pl.pallas_call(
        body, name=name,
        grid_spec=pltpu.PrefetchScalarGridSpec(
            num_scalar_prefetch=1, grid=(got[0].shape[0],),
            in_specs=[own(p) for p in parts] + [plain(p) for p in got], out_specs=[plain(p) for p in got]),
        out_shape=[jax.ShapeDtypeStruct(p.shape, BF16) for p in got],
        compiler_params=_params(),
    )(core, *parts, *got)


def sum_slots(slots, name):
    n = len(slots)

    def body(*refs):
        for a in range(n):
            total = refs[a][0].astype(F32)
            for s in range(1, slots[a].shape[0]):
                total = total + refs[a][s].astype(F32)
            refs[n + a][...] = total

    return pl.pallas_call(
        body, name=name,
        out_shape=[jax.ShapeDtypeStruct(p.shape[1:], F32) for p in slots],
        in_specs=[VMEM] * n, out_specs=[VMEM] * n,
        compiler_params=_params(),
    )(*slots)


def _adamw(w, g, m, v):
    m2 = ADAM_B1 * m + (1.0 - ADAM_B1) * g
    v2 = ADAM_B2 * v + (1.0 - ADAM_B2) * (g * g)
    m_hat = m2 / (1.0 - ADAM_B1 ** ADAM_STEP)
    v_hat = v2 / (1.0 - ADAM_B2 ** ADAM_STEP)
    delta = -ADAM_LR * (m_hat / (jnp.sqrt(v_hat) + ADAM_EPS) + ADAM_WD * w)
    return delta, m2, v2


def adamw_update(ws, gs, ms, vs, name):
    n = len(ws)

    def body(*refs):
        for a in range(n):
            delta, m2, v2 = _adamw(refs[a][...], refs[n + a][...], refs[2 * n + a][...], refs[3 * n + a][...])
            refs[4 * n + 3 * a][...] = delta
            refs[4 * n + 3 * a + 1][...] = m2
            refs[4 * n + 3 * a + 2][...] = v2

    out = pl.pallas_call(
        body, name=name,
        out_shape=[jax.ShapeDtypeStruct(w.shape, F32) for w in ws for _ in range(3)],
        in_specs=[VMEM] * (4 * n), out_specs=[VMEM] * (3 * n),
        compiler_params=_params(),
    )(*ws, *gs, *ms, *vs)
    return [out[3 * a:3 * a + 3] for a in range(n)]


GAIN_ROWS = 8
ROW_POOL_SCALE = 4 * GAIN_ROWS
ROW_SINKS = ROW_POOL_SCALE + 4
ROW_LOSS = ROW_SINKS + 1
ROW_W_POOL = 40
SMALL_ROWS = ROW_W_POOL + 4 * POOL_GROUP_DIM


def tail_reduce(d_in_t, gains, dpool_scale, dsinks, loss_part, dw_pool, drel_bias):
    def body(d_in_ref, g0, g1, g2, g3, dsc_ref, dsink_ref, loss_ref, dwp_ref, drb_ref, g_in_ref, total_ref, total_rb_ref,
             stage, gat, gat_rb, g_send, g_recv, pair_got, chip_part, chip_got, p_send, p_recv, x_send, x_recv):
        x, y, c = lax.axis_index("x"), lax.axis_index("y"), lax.axis_index("c")
        my_id, my_chip = 4 * x + 2 * y + c, 2 * x + y

        for k, g_ref in enumerate((g0, g1, g2, g3)):
            stage[GAIN_ROWS * k:GAIN_ROWS * (k + 1), :] = g_ref[...]
        stage[ROW_POOL_SCALE:ROW_SINKS, :] = dsc_ref[...]
        stage[ROW_SINKS:ROW_LOSS, :] = dsink_ref[...]
        stage[ROW_LOSS:ROW_LOSS + 1, :] = loss_ref[...]
        stage[ROW_LOSS + 1:ROW_W_POOL, :] = jnp.zeros((ROW_W_POOL - ROW_LOSS - 1, 128), F32)
        stage[ROW_W_POOL:, :] = dwp_ref[...].reshape(4 * POOL_GROUP_DIM, POOL_GROUP_DIM)
        gat[my_id] = stage[...]
        gat_rb[my_id] = drb_ref[...]
        start, finish = _gather_plan([stage, drb_ref], [gat, gat_rb], g_send, g_recv)
        start()

        pair = pltpu.make_async_remote_copy(
            src_ref=d_in_ref.at[1 - c], dst_ref=pair_got, send_sem=p_send, recv_sem=p_recv,
            device_id=(x, y, 1 - c), device_id_type=MESH)
        pair.start()
        pair.wait()
        chip_part[...] = (d_in_ref[c].astype(F32) + pair_got[...].astype(F32)).astype(BF16)
        copies = []
        for k in range(1, N_CHIP):
            px, py = x ^ (k >> 1), y ^ (k & 1)
            copies.append(pltpu.make_async_remote_copy(
                src_ref=chip_part.at[2 * px + py], dst_ref=chip_got.at[my_chip],
                send_sem=x_send.at[k - 1], recv_sem=x_recv.at[k - 1], device_id=(px, py, c), device_id_type=MESH))
        for cp in copies:
            cp.start()
        chip_got[my_chip] = chip_part[my_chip]

        finish()
        total, total_rb = gat[0], gat_rb[0]
        for s in range(1, N_DEV):
            total, total_rb = total + gat[s], total_rb + gat_rb[s]
        total_ref[...] = total
        total_rb_ref[...] = total_rb

        for cp in copies:
            cp.wait()
        g_in = chip_got[0].astype(F32)
        for s in range(1, N_CHIP):
            g_in = g_in + chip_got[s].astype(F32)
        g_in_ref[...] = g_in

    per_core = d_in_t.shape[1:]
    return pl.pallas_call(
        body, name="tail_reduce",
        out_shape=[jax.ShapeDtypeStruct(d_in_t.shape[2:], F32), jax.ShapeDtypeStruct((SMALL_ROWS, 128), F32),
                   jax.ShapeDtypeStruct(drel_bias.shape, F32)],
        in_specs=[VMEM] * 10, out_specs=[VMEM] * 3,
        scratch_shapes=[pltpu.VMEM((SMALL_ROWS, 128), F32), pltpu.VMEM((N_DEV, SMALL_ROWS, 128), F32),
                        pltpu.VMEM((N_DEV, *drel_bias.shape), F32),
                        pltpu.SemaphoreType.DMA((14,)), pltpu.SemaphoreType.DMA((14,)),
                        pltpu.VMEM(per_core, d_in_t.dtype), pltpu.VMEM(per_core, d_in_t.dtype),
                        pltpu.VMEM(per_core, d_in_t.dtype),
                        pltpu.SemaphoreType.DMA, pltpu.SemaphoreType.DMA,
                        pltpu.SemaphoreType.DMA((3,)), pltpu.SemaphoreType.DMA((3,))],
        compiler_params=_params(),
    )(d_in_t, *gains, dpool_scale, dsinks, loss_part, dw_pool, drel_bias)


def small_adamw(total, total_rb, small_w, small_m, small_v):
    n_small = len(small_w)

    def body(*refs):
        total_ref, rb_ref = refs[:2]
        w_refs, m_refs, v_refs = (refs[2 + k * n_small:2 + (k + 1) * n_small] for k in range(3))
        loss_out = refs[2 + 3 * n_small]
        result = refs[3 + 3 * n_small:]
        total = total_ref[...]
        loss_out[...] = total[ROW_LOSS:ROW_LOSS + 1, :]
        grads = [_as_lanes(total[GAIN_ROWS * k:GAIN_ROWS * (k + 1), :]) for k in range(4)]
        grads.append(_as_lanes(total[ROW_POOL_SCALE:ROW_SINKS, :]))
        grads.append(total[ROW_SINKS:ROW_LOSS, 0:N_Q_HEADS])
        grads.append(total[ROW_W_POOL:, :].reshape(w_refs[6].shape))
        grads.append(rb_ref[...])
        for k in range(n_small):
            delta, m2, v2 = _adamw(w_refs[k][...], grads[k], m_refs[k][...], v_refs[k][...])
            result[4 * k][...] = grads[k]
            result[4 * k + 1][...] = delta
            result[4 * k + 2][...] = m2
            result[4 * k + 3][...] = v2

    out = pl.pallas_call(
        body, name="small_adamw",
        out_shape=[jax.ShapeDtypeStruct((1, 128), F32)] + [jax.ShapeDtypeStruct(w.shape, F32) for w in small_w for _ in range(4)],
        in_specs=[VMEM] * (2 + 3 * n_small), out_specs=[VMEM] * (1 + 4 * n_small),
        compiler_params=_params(),
    )(total, total_rb, *small_w, *small_m, *small_v)
    return out[0], [out[1 + 4 * k:5 + 4 * k] for k in range(n_small)]


def norm_inproj(x, g, w_t, out_shard):
    t, d = x.shape
    n = w_t.shape[0]
    tm = TOKEN_TILE
    last = t // tm - 1

    def body(x_ref, g_ref, w_ref, shard_ref, proj_ref, h_ref, gathered_ref, send_sems, recv_sems, local_sems, bounce):
        i = pl.program_id(0)
        start, finish = _gather_plan([shard_ref], [gathered_ref], send_sems, recv_sems, local_sems, [bounce])
        pl.when(i == 0)(start)
        xv = x_ref[...]
        h = ((xv * _rstd(xv)) * g_ref[...]).astype(BF16)
        h_ref[...] = h
        proj_ref[...] = _dot_nt(h, w_ref[...])
        pl.when(i == last)(finish)

    return pl.pallas_call(
        body, name="norm_inproj", grid=(t // tm,),
        out_shape=[jax.ShapeDtypeStruct((t, n), F32), jax.ShapeDtypeStruct((t, d), BF16),
                   jax.ShapeDtypeStruct((N_DEV, *out_shard.shape), out_shard.dtype)],
        in_specs=[pl.BlockSpec((tm, d), lambda i: (i, 0)), pl.BlockSpec((1, d), lambda i: (0, 0)),
                  pl.BlockSpec((n, d), lambda i: (0, 0)), ANY],
        out_specs=[pl.BlockSpec((tm, n), lambda i: (i, 0)), pl.BlockSpec((tm, d), lambda i: (i, 0)), ANY],
        scratch_shapes=_gather_scratch([out_shard]),
        compiler_params=_params(),
    )(x, g, w_t, out_shard)


def bias_band(bucket, in_window, rel_bias):
    def body(bk_ref, win_ref, rb_ref, out_ref):
        bk = bk_ref[...]
        keep = win_ref[...] > 0.5
        for h in range(N_Q_HEADS):
            acc = jnp.zeros(bk.shape, F32)
            for b in range(N_BUCKETS):
                acc = jnp.where(bk == float(b), rb_ref[b, h], acc)
            out_ref[h] = jnp.where(keep, acc, NEG_INF)

    return pl.pallas_call(
        body, name="bias_band",
        out_shape=jax.ShapeDtypeStruct((N_Q_HEADS, BLOCK, 2 * BLOCK), F32),
        in_specs=[VMEM, VMEM, SMEM], out_specs=VMEM,
    )(bucket, in_window, rel_bias)


def bias_band_bwd(bucket, dbias):
    def body(bk_ref, db_ref, out_ref):
        bk = bk_ref[...]
        for h in range(N_Q_HEADS):
            db = db_ref[h]
            for b in range(N_BUCKETS):
                out_ref[b, h] = jnp.sum(jnp.where(bk == float(b), db, 0.0))

    return pl.pallas_call(
        body, name="bias_band_bwd",
        out_shape=jax.ShapeDtypeStruct((N_BUCKETS, N_Q_HEADS), F32),
        in_specs=[VMEM, VMEM], out_specs=SMEM,
    )(bucket, dbias)


def _window_sum(buf_ref, g, w, first):
    cols = slice(g * POOL_GROUP_DIM, (g + 1) * POOL_GROUP_DIM)
    acc = None
    for k in range(w):
        piece = buf_ref[first(k):first(k) + BLOCK, cols]
        acc = piece if acc is None else acc + piece
    return acc


def _inv_count(i, w):
    row = lax.broadcasted_iota(jnp.int32, (BLOCK, 1), 0)
    return 1.0 / jnp.minimum(i * BLOCK + row + 1, w).astype(F32)


def _fill_pool_input(i, ubuf, uc_ref, halo_ref):
    ubuf[0:HALO, :] = jnp.where(i > 0, halo_ref[...], 0.0)
    ubuf[HALO:, :] = uc_ref[...]


def _pooled(i, g, w, ubuf):
    cols = slice(g * POOL_GROUP_DIM, (g + 1) * POOL_GROUP_DIM)
    return _window_sum(ubuf, g, w, lambda k: HALO - k) * _inv_count(i, w) - ubuf[HALO:, cols]


def _head_variants(pair):
    low = lax.broadcasted_iota(jnp.int32, pair.shape, 1) < HEAD_DIM
    swapped = pltpu.roll(pair, HEAD_DIM, 1)
    zero = jnp.zeros_like(pair)
    pick = lambda c, a, b: jnp.where(c, a, b).astype(BF16)
    return [[pick(low, pair, zero), pick(low, zero, swapped)], [pick(low, swapped, zero), pick(low, zero, pair)]]


def _head_probs(i, hq, rows, s_ref, biasm_ref, sinks_ref):
    s = s_ref[hq, rows, :] * ATTN_SCALE + biasm_ref[hq, rows, :]
    col = lax.broadcasted_iota(jnp.int32, s.shape, 1)
    s = jnp.where((i == 0) & (col < BLOCK), NEG_INF, s)
    sink = sinks_ref[0, hq]
    m = jnp.maximum(jnp.max(s, axis=-1, keepdims=True), sink)
    p = jnp.exp(s - m)
    e_sink = jnp.exp(sink - m)
    inv = 1.0 / (jnp.sum(p, axis=-1, keepdims=True) + e_sink)
    return p * inv, e_sink * inv


def _mixer_in_specs(cur, prv):
    return [pl.BlockSpec((BLOCK, 512), lambda i: (cur(i), 0)),
            pl.BlockSpec((HALO, 512), lambda i: (jnp.maximum(cur(i) * (BLOCK // HALO) - 1, 0), 0)),
            pl.BlockSpec((BLOCK, 512), lambda i: (cur(i), 1)),
            pl.BlockSpec((BLOCK, 256), lambda i: (cur(i), 4)),
            pl.BlockSpec((BLOCK, 256), lambda i: (prv(i), 4))]


def _mixer_param_specs():
    return [pl.BlockSpec((N_Q_HEADS, BLOCK, 2 * BLOCK), lambda i: (0, 0, 0)), SMEM,
            pl.BlockSpec((4, POOL_GROUP_DIM, POOL_GROUP_DIM), lambda i: (0, 0, 0)),
            pl.BlockSpec((1, POOL_WIDTH), lambda i: (0, 0))]


def mixers_fwd(proj, biasm, sinks, w_pool, pool_scale, shards):
    t = proj.shape[0]
    nb = t // BLOCK
    ns = len(shards)

    def body(*refs):
        uc_ref, halo_ref, q_ref, kvc_ref, kvp_ref, biasm_ref, sinks_ref, wp_ref, sc_ref = refs[:9]
        shard_refs, out_ref, gathered_refs = refs[9:9 + ns], refs[9 + ns], refs[10 + ns:10 + 2 * ns]
        ubuf, s_all, p_all, send_sems, recv_sems, local_sems = refs[10 + 2 * ns:16 + 2 * ns]
        i = pl.program_id(0)
        start, finish = _gather_plan(shard_refs, gathered_refs, send_sems, recv_sems, local_sems, refs[16 + 2 * ns:])
        pl.when(i == 0)(start)

        _fill_pool_input(i, ubuf, uc_ref, halo_ref)
        for g, w in enumerate(POOL_WINDOWS):
            mixed = _dot(_pooled(i, g, w, ubuf).astype(BF16), wp_ref[g])
            cols = slice(g * POOL_GROUP_DIM, (g + 1) * POOL_GROUP_DIM)
            out_ref[:, cols] = (mixed * sc_ref[:, cols]).astype(BF16)
        kv = jnp.concatenate([kvp_ref[...], kvc_ref[...]], axis=0)
        k_var = _head_variants(kv[:, 0:2 * HEAD_DIM])
        v_var = _head_variants(kv[:, 2 * HEAD_DIM:])
        for hq in range(N_Q_HEADS):
            j, half, h = hq // 2, hq % 2, hq // GQA_GROUP
            q2 = q_ref[:, 2 * HEAD_DIM * j:2 * HEAD_DIM * (j + 1)].astype(BF16)
            s_all[hq] = _dot_nt(q2, k_var[h][half])
        for hq in range(N_Q_HEADS):
            for r in range(0, BLOCK, ROW_CHUNK):
                rows = slice(r, r + ROW_CHUNK)
                probs, _ = _head_probs(i, hq, rows, s_all, biasm_ref, sinks_ref)
                p_all[hq, rows, :] = probs.astype(BF16)
        for j in range(N_Q_HEADS // 2):
            h = 2 * j // GQA_GROUP
            acc = _dot(p_all[2 * j], v_var[h][0]) + _dot(p_all[2 * j + 1], v_var[h][1])
            out_ref[:, POOL_WIDTH + 2 * HEAD_DIM * j:POOL_WIDTH + 2 * HEAD_DIM * (j + 1)] = acc.astype(BF16)

        pl.when(i == nb - 1)(finish)

    return pl.pallas_call(
        body, name="mixers_fwd", grid=(nb,),
        out_shape=[jax.ShapeDtypeStruct((t, 2 * POOL_WIDTH), BF16)]
        + [jax.ShapeDtypeStruct((N_DEV, *sh.shape), sh.dtype) for sh in shards],
        in_specs=_mixer_in_specs(lambda i: i, lambda i: jnp.maximum(i - 1, 0)) + _mixer_param_specs() + [ANY] * ns,
        out_specs=[pl.BlockSpec((BLOCK, 2 * POOL_WIDTH), lambda i: (i, 0))] + [ANY] * ns,
        scratch_shapes=[pltpu.VMEM((HALO + BLOCK, POOL_WIDTH), F32), pltpu.VMEM((N_Q_HEADS, BLOCK, 2 * BLOCK), F32),
                        pltpu.VMEM((N_Q_HEADS, BLOCK, 2 * BLOCK), BF16)] + _gather_scratch(shards),
        compiler_params=_params(),
    )(proj, proj, proj, proj, proj, biasm, sinks, w_pool, pool_scale, *shards)


def outproj_norm(cat, w, x, g, g_next, shard):
    t, d = x.shape
    tm = TOKEN_TILE
    last = t // tm - 1

    def body(c_ref, w_ref, x_ref, g_ref, gn_ref, shard_ref, mix_ref, x1_ref, h2_ref, gathered_ref,
             send_sems, recv_sems, local_sems, bounce):
        i = pl.program_id(0)
        start, finish = _gather_plan([shard_ref], [gathered_ref], send_sems, recv_sems, local_sems, [bounce])
        pl.when(i == 0)(start)
        mix = _dot(c_ref[...], w_ref[...])
        mix_ref[...] = mix
        x1 = x_ref[...] + (mix * _rstd(mix)) * g_ref[...]
        x1_ref[...] = x1
        h2_ref[...] = ((x1 * _rstd(x1)) * gn_ref[...]).astype(BF16)
        pl.when(i == last)(finish)

    row = pl.BlockSpec((tm, d), lambda i: (i, 0))
    gain = pl.BlockSpec((1, d), lambda i: (0, 0))
    return pl.pallas_call(
        body, name="outproj_norm", grid=(t // tm,),
        out_shape=[jax.ShapeDtypeStruct((t, d), F32), jax.ShapeDtypeStruct((t, d), F32), jax.ShapeDtypeStruct((t, d), BF16),
                   jax.ShapeDtypeStruct((N_DEV, *shard.shape), shard.dtype)],
        in_specs=[pl.BlockSpec((tm, cat.shape[1]), lambda i: (i, 0)), pl.BlockSpec(w.shape, lambda i: (0, 0)), row, gain, gain, ANY],
        out_specs=[row, row, row, ANY],
        scratch_shapes=_gather_scratch([shard]),
        compiler_params=_params(),
    )(cat, w, x, g, g_next, shard)


def ffn_up(h, gate_t, up_t, down_shard):
    t, d = h.shape
    n = gate_t.shape[1]
    f = N_DEV * n
    tm, ts = FFN_TOKEN_TILE, FF_SHARDS_PER_TILE
    tn = ts * n
    steps = (f // tn, t // tm)

    def body(h_ref, wg_ref, wu_ref, shard_ref, gate_ref, up_ref, a_ref, gathered_ref,
             send_sems, recv_sems, local_sems, bounce):
        j, i = pl.program_id(0), pl.program_id(1)
        start, finish = _gather_plan([shard_ref], [gathered_ref], send_sems, recv_sems, local_sems, [bounce])
        pl.when((i == 0) & (j == 0))(start)

        hv = h_ref[...]
        gate = _dot_nt(hv, _merge_rows(wg_ref[...]))
        up = _dot_nt(hv, _merge_rows(wu_ref[...]))
        gate_ref[...] = gate.astype(BF16)
        up_ref[...] = up.astype(BF16)
        a_ref[...] = (gate * (1.0 / (1.0 + jnp.exp(-gate))) * up).astype(BF16)

        pl.when((j == steps[0] - 1) & (i == steps[1] - 1))(finish)

    wide = pl.BlockSpec((tm, tn), lambda j, i: (i, j))
    return pl.pallas_call(
        body, name="ffn_up", grid=steps,
        out_shape=[jax.ShapeDtypeStruct((t, f), BF16)] * 3
        + [jax.ShapeDtypeStruct((N_DEV, *down_shard.shape), down_shard.dtype)],
        in_specs=[pl.BlockSpec((tm, d), lambda j, i: (i, 0)),
                  pl.BlockSpec((ts, n, d), lambda j, i: (j, 0, 0)),
                  pl.BlockSpec((ts, n, d), lambda j, i: (j, 0, 0)), ANY],
        out_specs=[wide, wide, wide, ANY],
        scratch_shapes=_gather_scratch([down_shard]),
        compiler_params=_params(),
    )(h, gate_t, up_t, down_shard)


def ffn_down_loss(a, w_down, x1, g, target):
    t, d = x1.shape
    tm = TOKEN_TILE

    def body(a_ref, w_ref, x_ref, g_ref, t_ref, df_ref, dy_ref, dg_ref, loss_ref):
        @pl.when(pl.program_id(0) == 0)
        def _():
            dg_ref[...] = jnp.zeros_like(dg_ref)
            loss_ref[...] = jnp.zeros_like(loss_ref)

        f = _dot(a_ref[...], _merge_rows(w_ref[...]))
        r = _rstd(f)
        g = g_ref[...]
        err = x_ref[...] + (f * r) * g - t_ref[...]
        loss_ref[...] += 0.5 * jnp.sum(jnp.mean(err * err, axis=-1, keepdims=True))
        dy = err * (1.0 / d)
        dy_ref[...] = dy
        df, dg_rows = _norm_bwd(dy, f, r, g)
        df_ref[...] = df.astype(BF16)
        dg_ref[...] += _as_rows(jnp.sum(dg_rows, axis=0, keepdims=True))

    row = pl.BlockSpec((tm, d), lambda i: (i, 0))
    gain = pl.BlockSpec((1, d), lambda i: (0, 0))
    return pl.pallas_call(
        body, name="ffn_down_loss", grid=(t // tm,),
        out_shape=[jax.ShapeDtypeStruct((t, d), BF16), jax.ShapeDtypeStruct((t, d), F32),
                   jax.ShapeDtypeStruct((d // 128, 128), F32), jax.ShapeDtypeStruct((1, 128), F32)],
        in_specs=[pl.BlockSpec((tm, a.shape[1]), lambda i: (i, 0)), pl.BlockSpec(w_down.shape, lambda i: (0, 0, 0)), row, gain, row],
        out_specs=[row, row, pl.BlockSpec((d // 128, 128), lambda i: (0, 0)), pl.BlockSpec((1, 128), lambda i: (0, 0))],
        compiler_params=_params(),
    )(a, w_down, x1, g, target)


def ffn_down_bwd(df, w_down, gate, up):
    t, d = df.shape
    n = w_down.shape[1]
    f = gate.shape[1]
    tm, ts = FFN_TOKEN_TILE, FF_SHARDS_PER_TILE
    tn = ts * n

    def body(df_ref, w_ref, gate_ref, up_ref, dgate_ref, dup_ref):
        da = _dot_nt(df_ref[...], _merge_rows(w_ref[...]))
        gate = gate_ref[...].astype(F32)
        sig = 1.0 / (1.0 + jnp.exp(-gate))
        dgate_ref[...] = (da * up_ref[...].astype(F32) * (sig * (1.0 + gate * (1.0 - sig)))).astype(BF16)
        dup_ref[...] = (da * (gate * sig)).astype(BF16)

    wide = pl.BlockSpec((tm, tn), lambda j, i: (i, j))
    return pl.pallas_call(
        body, name="ffn_down_bwd", grid=(f // tn, t // tm),
        out_shape=[jax.ShapeDtypeStruct((t, f), BF16)] * 2,
        in_specs=[pl.BlockSpec((tm, d), lambda j, i: (i, 0)), pl.BlockSpec((ts, n, d), lambda j, i: (j, 0, 0)), wide, wide],
        out_specs=[wide, wide],
        compiler_params=_params(),
    )(df, w_down, gate, up)


def grad_rows(a, b, name, by_core=False):
    t, m = a.shape
    d = b.shape[1]
    r = m // N_DEV
    tt = TOKEN_TILE
    last = t // tt - 1
    out_shape = (2, N_CHIP, r, d) if by_core else (N_DEV, r, d)

    def body(a_ref, b_ref, out_ref, acc):
        k = pl.program_id(0)

        @pl.when(k == 0)
        def _():
            acc[...] = jnp.zeros_like(acc)

        acc[...] += _dot_tn(a_ref[...], b_ref[...])

        @pl.when(k == last)
        def _():
            if by_core:
                blocks = acc[...].reshape(N_CHIP, 2, r, d)
                for chip in range(N_CHIP):
                    for core in range(2):
                        out_ref[core, chip] = blocks[chip, core].astype(BF16)
            else:
                out_ref[...] = acc[...].reshape(out_shape).astype(BF16)

    return pl.pallas_call(
        body, name=name, grid=(t // tt,),
        out_shape=jax.ShapeDtypeStruct(out_shape, BF16),
        in_specs=[pl.BlockSpec((tt, m), lambda k: (k, 0)), pl.BlockSpec((tt, d), lambda k: (k, 0))],
        out_specs=pl.BlockSpec(out_shape, lambda k: (0,) * len(out_shape)),
        scratch_shapes=[pltpu.VMEM((m, d), F32)],
        compiler_params=_params(),
    )(a, b)


def grad_ffn(lhs, b, name, chip_parts=()):
    t, f = lhs[0].shape
    d = b.shape[1]
    nw = len(lhs)
    na = len(chip_parts)
    n = f // N_DEV
    tt, ts = TOKEN_TILE, FF_SHARDS_PER_TILE
    tn = ts * n
    steps = (f // tn, t // tt)

    def body(*refs):
        a_refs, b_ref, part_refs = refs[:nw], refs[nw], refs[nw + 1:nw + 1 + na]
        out_refs = refs[nw + 1 + na:2 * nw + 1 + na]
        slot_refs = refs[2 * nw + 1 + na:2 * nw + 1 + 2 * na]
        acc = refs[2 * nw + 1 + 2 * na]
        i, k = pl.program_id(0), pl.program_id(1)
        if na:
            send_sems, recv_sems, local_sems = refs[2 * nw + 2 + 2 * na:2 * nw + 5 + 2 * na]
            start, finish = _chip_exchange_plan(part_refs, slot_refs, send_sems, recv_sems, local_sems,
                                                refs[2 * nw + 5 + 2 * na:])
            pl.when((i == 0) & (k == 0))(start)

        @pl.when(k == 0)
        def _():
            acc[...] = jnp.zeros_like(acc)

        for w in range(nw):
            acc[w] += _dot_tn(a_refs[w][...], b_ref[...])

        @pl.when(k == steps[1] - 1)
        def _():
            for w in range(nw):
                blocks = acc[w].reshape(ts // 2, 2, n, d)
                for chip in range(ts // 2):
                    for core in range(2):
                        out_refs[w][core, chip] = blocks[chip, core].astype(BF16)

        if na:
            pl.when((i == steps[0] - 1) & (k == steps[1] - 1))(finish)

    out = pl.pallas_call(
        body, name=name, grid=steps,
        out_shape=[jax.ShapeDtypeStruct((2, N_CHIP, n, d), BF16)] * nw
        + [jax.ShapeDtypeStruct(p.shape, p.dtype) for p in chip_parts],
        in_specs=[pl.BlockSpec((tt, tn), lambda i, k: (k, i))] * nw + [pl.BlockSpec((tt, d), lambda i, k: (k, 0))] + [ANY] * na,
        out_specs=[pl.BlockSpec((2, ts // 2, n, d), lambda i, k: (0, i, 0, 0))] * nw + [ANY] * na,
        scratch_shapes=[pltpu.VMEM((nw, tn, d), F32)] + (_chip_exchange_scratch(chip_parts) if na else []),
        compiler_params=_params(),
    )(*lhs, b, *chip_parts)
    return out[:nw], out[nw:]


def ffn_up_bwd(dgate, dup, gate_t, up_t, x1, g_ffn, dy, mix, g_mix, chip_parts):
    t, d = x1.shape
    n = gate_t.shape[1]
    f = N_DEV * n
    tm = TOKEN_TILE
    na = len(chip_parts)
    last = t // tm - 1

    def body(*refs):
        dg_ref, du_ref, wg_ref, wu_ref, x_ref, gf_ref, dy_ref, mix_ref, gm_ref = refs[:9]
        part_refs = refs[9:9 + na]
        dx1_ref, dmix_ref, dgf_ref, dgm_ref = refs[9 + na:13 + na]
        slot_refs = refs[13 + na:13 + 2 * na]
        send_sems, recv_sems, local_sems = refs[13 + 2 * na:16 + 2 * na]
        i = pl.program_id(0)
        start, finish = _chip_exchange_plan(part_refs, slot_refs, send_sems, recv_sems, local_sems, refs[16 + 2 * na:])

        @pl.when(i == 0)
        def _():
            start()
            dgf_ref[...] = jnp.zeros_like(dgf_ref)
            dgm_ref[...] = jnp.zeros_like(dgm_ref)

        dh = _dot(dg_ref[...], _merge_rows(wg_ref[...])) + _dot(du_ref[...], _merge_rows(wu_ref[...]))
        x1 = x_ref[...]
        dx, dgf_rows = _norm_bwd(dh, x1, _rstd(x1), gf_ref[...])
        dx1 = dy_ref[...] + dx
        dx1_ref[...] = dx1
        dgf_ref[...] += _as_rows(jnp.sum(dgf_rows, axis=0, keepdims=True))
        mix = mix_ref[...]
        dmix, dgm_rows = _norm_bwd(dx1, mix, _rstd(mix), gm_ref[...])
        dmix_ref[...] = dmix.astype(BF16)
        dgm_ref[...] += _as_rows(jnp.sum(dgm_rows, axis=0, keepdims=True))
        pl.when(i == last)(finish)

    row = pl.BlockSpec((tm, d), lambda i: (i, 0))
    wide = pl.BlockSpec((tm, f), lambda i: (i, 0))
    gain = pl.BlockSpec((1, d), lambda i: (0, 0))
    gain_rows = pl.BlockSpec((d // 128, 128), lambda i: (0, 0))
    whole = pl.BlockSpec((N_DEV, n, d), lambda i: (0, 0, 0), pipeline_mode=pl.Buffered(1))
    out = pl.pallas_call(
        body, name="ffn_up_bwd", grid=(t // tm,),
        out_shape=[jax.ShapeDtypeStruct((t, d), F32), jax.ShapeDtypeStruct((t, d), BF16),
                   jax.ShapeDtypeStruct((d // 128, 128), F32), jax.ShapeDtypeStruct((d // 128, 128), F32)]
        + [jax.ShapeDtypeStruct(p.shape, p.dtype) for p in chip_parts],
        in_specs=[wide, wide, whole, whole, row, gain, row, row, gain] + [ANY] * na,
        out_specs=[row, row, gain_rows, gain_rows] + [ANY] * na,
        scratch_shapes=_chip_exchange_scratch(chip_parts),
        compiler_params=_params(),
    )(dgate, dup, gate_t, up_t, x1, g_ffn, dy, mix, g_mix, *chip_parts)
    return out[:4], out[4:]


def outproj_bwd(dmix, w_out):
    t, d = dmix.shape
    tm = TOKEN_TILE

    def body(dm_ref, w_ref, out_ref):
        out_ref[...] = _dot_nt(dm_ref[...], w_ref[...])

    return pl.pallas_call(
        body, name="outproj_bwd", grid=(t // tm,),
        out_shape=jax.ShapeDtypeStruct((t, w_out.shape[0]), F32),
        in_specs=[pl.BlockSpec((tm, d), lambda i: (i, 0)), pl.BlockSpec(w_out.shape, lambda i: (0, 0))],
        out_specs=pl.BlockSpec((tm, w_out.shape[0]), lambda i: (i, 0)),
        compiler_params=_params(),
    )(dmix, w_out)


def mixers_bwd(proj, dcat, biasm, sinks, w_pool, pool_scale, ffn_parts):
    t = proj.shape[0]
    nb = t // BLOCK
    na = len(ffn_parts)

    def body(*refs):
        (uc_ref, halo_ref, q_ref, kvc_ref, kvp_ref, dcat_ref, biasm_ref, sinks_ref, wp_ref, sc_ref) = refs[:10]
        part_refs = refs[10:10 + na]
        dproj_ref, dbias_ref, dsink_ref, dwp_ref, dsc_ref = refs[10 + na:15 + na]
        slot_refs = refs[15 + na:15 + 2 * na]
        ubuf, dbuf, c_u, c_q, c_kv, s_all, dp_all, ds_all, p_all = refs[15 + 2 * na:24 + 2 * na]
        send_sems, recv_sems, local_sems = refs[24 + 2 * na:27 + 2 * na]
        bounce = refs[27 + 2 * na:]
        i = pl.program_id(0)
        lane = lax.broadcasted_iota(jnp.int32, (1, 128), 1)
        start, finish = _chip_exchange_plan(part_refs, slot_refs, send_sems, recv_sems, local_sems, bounce)

        @pl.when(i == 0)
        def _():
            start()
            dbias_ref[...] = jnp.zeros_like(dbias_ref)
            dwp_ref[...] = jnp.zeros_like(dwp_ref)
            dsc_ref[...] = jnp.zeros_like(dsc_ref)
            dsink_ref[...] = jnp.zeros_like(dsink_ref)
            dbuf[...] = jnp.zeros_like(dbuf)
            c_u[...] = jnp.zeros_like(c_u)
            c_q[...] = jnp.zeros_like(c_q)
            c_kv[...] = jnp.zeros_like(c_kv)

        @pl.when(i < nb)
        def _():
            _fill_pool_input(i, ubuf, uc_ref, halo_ref)
            for g, w in enumerate(POOL_WINDOWS):
                cols = slice(g * POOL_GROUP_DIM, (g + 1) * POOL_GROUP_DIM)
                pooled = _pooled(i, g, w, ubuf).astype(BF16)
                mixed = _dot(pooled, wp_ref[g])
                dout = dcat_ref[:, cols]
                dsc_ref[g:g + 1, :] += jnp.sum(dout * mixed, axis=0, keepdims=True)
                dmixed = (dout * sc_ref[:, cols]).astype(BF16)
                dwp_ref[g] += _dot_tn(pooled, dmixed)
                dpooled = _dot_nt(dmixed, wp_ref[g])
                scaled = dpooled * _inv_count(i, w)
                dbuf[BLOCK:, cols] = scaled[0:HALO]
                dproj_ref[:, cols] = (_window_sum(dbuf, g, w, lambda k: k) + c_u[:, cols]).astype(BF16)
                dbuf[0:BLOCK, cols] = scaled
                c_u[:, cols] = -dpooled

            kv = jnp.concatenate([kvp_ref[...], kvc_ref[...]], axis=0)
            k_var = _head_variants(kv[:, 0:2 * HEAD_DIM])
            v_var = _head_variants(kv[:, 2 * HEAD_DIM:])
            q2s = [q_ref[:, 2 * HEAD_DIM * j:2 * HEAD_DIM * (j + 1)].astype(BF16) for j in range(N_Q_HEADS // 2)]
            do2s = [dcat_ref[:, POOL_WIDTH + 2 * HEAD_DIM * j:POOL_WIDTH + 2 * HEAD_DIM * (j + 1)].astype(BF16)
                    for j in range(N_Q_HEADS // 2)]
            slot = lambda hq: 4 * (hq // GQA_GROUP) + 2 * (hq % 2) + (hq % GQA_GROUP) // 2
            for hq in range(N_Q_HEADS):
                j, half, h = hq // 2, hq % 2, hq // GQA_GROUP
                s_all[hq] = _dot_nt(q2s[j], k_var[h][half])
                dp_all[hq] = _dot_nt(do2s[j], v_var[h][half])
            dsink_row = jnp.zeros((1, 128), F32)
            for hq in range(N_Q_HEADS):
                dsink = 0.0
                for r in range(0, BLOCK, ROW_CHUNK):
                    rows = slice(r, r + ROW_CHUNK)
                    probs, p_sink = _head_probs(i, hq, rows, s_all, biasm_ref, sinks_ref)
                    dp = dp_all[hq, rows, :]
                    delta = jnp.sum(probs * dp, axis=-1, keepdims=True)
                    ds = probs * (dp - delta)
                    dbias_ref[hq, rows, :] += ds
                    dsink = dsink + jnp.sum(p_sink * delta)
                    ds_all[slot(hq), rows, :] = (ds * ATTN_SCALE).astype(BF16)
                    p_all[slot(hq), rows, :] = probs.astype(BF16)
                dsink_row = dsink_row - jnp.where(lane == hq, dsink, 0.0)
            dsink_ref[...] += dsink_row
            dq2 = [None] * (N_Q_HEADS // 2)
            for hq in range(N_Q_HEADS):
                j, half, h = hq // 2, hq % 2, hq // GQA_GROUP
                dq = _dot(ds_all[slot(hq)], k_var[h][half])
                dq2[j] = dq if dq2[j] is None else dq2[j] + dq
            low = lax.broadcasted_iota(jnp.int32, (2 * BLOCK, 2 * HEAD_DIM), 1) < HEAD_DIM
            dk_half, dv_half = [[None, None], [None, None]], [[None, None], [None, None]]
            for h in range(N_KV_HEADS):
                for half in range(2):
                    heads = [hq for hq in range(GQA_GROUP * h, GQA_GROUP * (h + 1)) if hq % 2 == half]
                    base = slot(heads[0])
                    q_rows = jnp.concatenate([q2s[hq // 2] for hq in heads], axis=0)
                    do_rows = jnp.concatenate([do2s[hq // 2] for hq in heads], axis=0)
                    dk_half[h][half] = _dot_tn(_merge_rows(ds_all[base:base + 2]), q_rows)
                    dv_half[h][half] = _dot_tn(_merge_rows(p_all[base:base + 2]), do_rows)

            def pair_of(halves):
                return jnp.where(low, halves[0][0] + pltpu.roll(halves[0][1], HEAD_DIM, 1),
                                 halves[1][1] + pltpu.roll(halves[1][0], HEAD_DIM, 1))

            dkv = jnp.concatenate([pair_of(dk_half), pair_of(dv_half)], axis=1)
            dproj_ref[:, POOL_WIDTH:2 * POOL_WIDTH] = c_q[...].astype(BF16)
            dproj_ref[:, 2 * POOL_WIDTH:] = (c_kv[...] + dkv[0:BLOCK]).astype(BF16)
            c_q[...] = jnp.concatenate(dq2, axis=1)
            c_kv[...] = dkv[BLOCK:]

        @pl.when(i == nb)
        def _():
            dbuf[BLOCK:, :] = jnp.zeros((HALO, POOL_WIDTH), F32)
            for g, w in enumerate(POOL_WINDOWS):
                cols = slice(g * POOL_GROUP_DIM, (g + 1) * POOL_GROUP_DIM)
                dproj_ref[:, cols] = (_window_sum(dbuf, g, w, lambda k: k) + c_u[:, cols]).astype(BF16)
            dproj_ref[:, POOL_WIDTH:2 * POOL_WIDTH] = c_q[...].astype(BF16)
            dproj_ref[:, 2 * POOL_WIDTH:] = c_kv[...].astype(BF16)
            finish()

    cur = lambda i: jnp.minimum(i, nb - 1)
    prv = lambda i: jnp.maximum(jnp.minimum(i, nb - 1) - 1, 0)
    out = pl.pallas_call(
        body, name="mixers_bwd", grid=(nb + 1,),
        out_shape=[jax.ShapeDtypeStruct((t, proj.shape[1]), BF16),
                   jax.ShapeDtypeStruct((N_Q_HEADS, BLOCK, 2 * BLOCK), F32),
                   jax.ShapeDtypeStruct((1, 128), F32),
                   jax.ShapeDtypeStruct((4, POOL_GROUP_DIM, POOL_GROUP_DIM), F32),
                   jax.ShapeDtypeStruct((len(POOL_WINDOWS), POOL_GROUP_DIM), F32)]
        + [jax.ShapeDtypeStruct(p.shape, p.dtype) for p in ffn_parts],
        in_specs=_mixer_in_specs(cur, prv) + [pl.BlockSpec((BLOCK, 2 * POOL_WIDTH), lambda i: (cur(i), 0))]
        + _mixer_param_specs() + [ANY] * na,
        out_specs=[pl.BlockSpec((BLOCK, proj.shape[1]), lambda i: (jnp.maximum(i - 1, 0), 0)),
                   pl.BlockSpec((N_Q_HEADS, BLOCK, 2 * BLOCK), lambda i: (0, 0, 0)),
                   pl.BlockSpec((1, 128), lambda i: (0, 0)),
                   pl.BlockSpec((4, POOL_GROUP_DIM, POOL_GROUP_DIM), lambda i: (0, 0, 0)),
                   pl.BlockSpec((len(POOL_WINDOWS), POOL_GROUP_DIM), lambda i: (0, 0))] + [ANY] * na,
        scratch_shapes=[pltpu.VMEM((HALO + BLOCK, POOL_WIDTH), F32), pltpu.VMEM((BLOCK + HALO, POOL_WIDTH), F32),
                        pltpu.VMEM((BLOCK, POOL_WIDTH), F32), pltpu.VMEM((BLOCK, POOL_WIDTH), F32),
                        pltpu.VMEM((BLOCK, 256), F32),
                        pltpu.VMEM((N_Q_HEADS, BLOCK, 2 * BLOCK), F32), pltpu.VMEM((N_Q_HEADS, BLOCK, 2 * BLOCK), F32),
                        pltpu.VMEM((N_Q_HEADS, BLOCK, 2 * BLOCK), BF16), pltpu.VMEM((N_Q_HEADS, BLOCK, 2 * BLOCK), BF16)]
        + _chip_exchange_scratch(ffn_parts),
        compiler_params=_params(),
    )(proj, proj, proj, proj, proj, dcat, biasm, sinks, w_pool, pool_scale, *ffn_parts)
    return out[:5], out[5:]


def inproj_bwd(dproj, w_in_t, x, g, dx1):
    t, d = x.shape
    n = dproj.shape[1]
    tm = TOKEN_TILE

    def body(dp_ref, w_ref, x_ref, g_ref, dx1_ref, dx_ref, dg_ref):
        @pl.when(pl.program_id(0) == 0)
        def _():
            dg_ref[...] = jnp.zeros_like(dg_ref)

        dh = _dot(dp_ref[...], w_ref[...])
        xv = x_ref[...]
        dx, dg_rows = _norm_bwd(dh, xv, _rstd(xv), g_ref[...])
        dx_ref[...] = dx1_ref[...] + dx
        dg_ref[...] += _as_rows(jnp.sum(dg_rows, axis=0, keepdims=True))

    row = pl.BlockSpec((tm, d), lambda i: (i, 0))
    gain = pl.BlockSpec((1, d), lambda i: (0, 0))
    return pl.pallas_call(
        body, name="inproj_bwd", grid=(t // tm,),
        out_shape=[jax.ShapeDtypeStruct((t, d), F32), jax.ShapeDtypeStruct((d // 128, 128), F32)],
        in_specs=[pl.BlockSpec((tm, n), lambda i: (i, 0)), pl.BlockSpec(w_in_t.shape, lambda i: (0, 0)), row, gain, row],
        out_specs=[row, pl.BlockSpec((d // 128, 128), lambda i: (0, 0))],
        compiler_params=_params(),
    )(dproj, w_in_t, x, g, dx1)


def _bucket_band():
    qi = jnp.arange(BLOCK)[:, None]
    kj = jnp.arange(2 * BLOCK)[None, :]
    dist = qi + BLOCK - kj
    n = jnp.maximum(dist, 0)
    nf = jnp.maximum(n, 1).astype(F32)
    large = MAX_EXACT + (jnp.log(nf / MAX_EXACT) / np.float32(np.log(MAX_DISTANCE / MAX_EXACT))
                         * (N_BUCKETS - MAX_EXACT)).astype(jnp.int32)
    large = jnp.minimum(large, N_BUCKETS - 1)
    bucket = jnp.where(n < MAX_EXACT, n, large)
    in_window = (dist >= 0) & (dist < BLOCK)
    return bucket.astype(F32), in_window.astype(F32)


def kernel(x, g_pre_mix, w_in, w_pool, pool_scale, rel_bias, sinks, w_out, g_post_mix, g_pre_ffn, w_gate, w_up, w_down, g_post_ffn, loss_target, m_g_pre_mix, m_w_in, m_w_pool, m_pool_scale, m_rel_bias, m_sinks, m_w_out, m_g_post_mix, m_g_pre_ffn, m_w_gate, m_w_up, m_w_down, m_g_post_ffn, v_g_pre_mix, v_w_in, v_w_pool, v_pool_scale, v_rel_bias, v_sinks, v_w_out, v_g_post_mix, v_g_pre_ffn, v_w_gate, v_w_up, v_w_down, v_g_post_ffn):
    d = x.shape[-1]
    xs, target = x[0], loss_target[0]

    w_in_ts = w_in[0].T.astype(BF16)
    w_out_s = w_out[0].astype(BF16)
    gate_ts = w_gate[0].T.astype(BF16)
    up_ts = w_up[0].T.astype(BF16)
    w_down_s = w_down[0].astype(BF16)
    w_in_t, = gather_blocks([w_in_ts], "gather_w_in")
    w_in_t = w_in_t.reshape(-1, d)

    bucket, in_window = _bucket_band()
    biasm = bias_band(bucket, in_window, rel_bias)
    w_pool_b = w_pool[0].astype(BF16)
    proj, h1, w_out_f = norm_inproj(xs, g_pre_mix, w_in_t, w_out_s)
    w_out_f = w_out_f.reshape(-1, d)
    cat, gate_t = mixers_fwd(proj, biasm, sinks, w_pool_b, pool_scale, [gate_ts])
    mix, x1, h2, up_t = outproj_norm(cat, w_out_f, xs, g_post_mix, g_pre_ffn, up_ts)
    gate, up, act, w_down_f = ffn_up(h2, gate_t, up_t, w_down_s)
    df, dy, dg_post_ffn, loss_part = ffn_down_loss(act, w_down_f, x1, g_post_ffn, target)

    def pair_sum(parts, tag):
        return pair_add(parts, pair_exchange(parts, "pair_exchange_" + tag), "pair_add_" + tag)

    dgate, dup = ffn_down_bwd(df, w_down_f, gate, up)
    (d_gate, d_up), _ = grad_ffn([dgate, dup], h2, "grad_w_gate_up")
    (d_down,), _ = grad_ffn([act], df, "grad_w_down")
    q_gate, q_up, q_down = pair_sum([d_gate, d_up, d_down], "ffn")
    (dx1, dmix, dg_pre_ffn, dg_post_mix), gate_up_slots = ffn_up_bwd(
        dgate, dup, gate_t, up_t, x1, g_pre_ffn, dy, mix, g_post_mix, [q_gate, q_up])
    dcat = outproj_bwd(dmix, w_out_f)
    d_out = grad_rows(cat, dmix, "grad_w_out", by_core=True)
    q_out, = pair_sum([d_out], "out")
    (dproj, dbias, dsinks, dw_pool, dpool_scale), down_out_slots = mixers_bwd(
        proj, dcat, biasm, sinks, w_pool_b, pool_scale, [q_down, q_out])
    drel_bias = bias_band_bwd(bucket, dbias)
    grad_x, dg_pre_mix = inproj_bwd(dproj, w_in_t, xs, g_pre_mix, dx1)
    d_in_t = grad_rows(dproj, h1, "grad_w_in", by_core=True)

    small_w = [g_pre_mix, g_post_mix, g_pre_ffn, g_post_ffn, pool_scale, sinks, w_pool, rel_bias]
    small_m = [m_g_pre_mix, m_g_post_mix, m_g_pre_ffn, m_g_post_ffn, m_pool_scale, m_sinks, m_w_pool, m_rel_bias]
    small_v = [v_g_pre_mix, v_g_post_mix, v_g_pre_ffn, v_g_post_ffn, v_pool_scale, v_sinks, v_w_pool, v_rel_bias]
    g_in_t, total, total_rb = tail_reduce(
        d_in_t, [dg_pre_mix, dg_post_mix, dg_pre_ffn, dg_post_ffn], dpool_scale, dsinks, loss_part, dw_pool, drel_bias)
    loss_row, sm = small_adamw(total, total_rb, small_w, small_m, small_v)
    g_gate_t, g_up_t, g_down, g_out = sum_slots([*gate_up_slots, *down_out_slots], "sum_slots")
    big_w = [w_in[0], w_out[0], w_gate[0], w_up[0], w_down[0]]
    big_g = [g_in_t.T, g_out, g_gate_t.T, g_up_t.T, g_down]
    big_m = [m_w_in[0], m_w_out[0], m_w_gate[0], m_w_up[0], m_w_down[0]]
    big_v = [v_w_in[0], v_w_out[0], v_w_gate[0], v_w_up[0], v_w_down[0]]
    upd = adamw_update(big_w[:2], big_g[:2], big_m[:2], big_v[:2], "adamw_mix") \
        + adamw_update(big_w[2:], big_g[2:], big_m[2:], big_v[2:], "adamw_ffn")
    big = [[big_g[k][None], *(u[None] for u in upd[k])] for k in range(5)]

    def ordered(kind):
        s, b = [p[kind] for p in sm], [p[kind] for p in big]
        return [s[0], b[0], s[6], s[4], s[7], s[5], b[1], s[1], s[2], b[2], b[3], b[4], s[3]]

    return (loss_row[0, 0], grad_x[None], *ordered(0), *ordered(1), *ordered(2), *ordered(3))
```

```python
import numpy as np
import jax
import jax.numpy as jnp
from jax import lax
from jax.experimental import pallas as pl
from jax.experimental.pallas import tpu as pltpu

F32 = jnp.float32
BF16 = jnp.bfloat16

N_DEV = 8
N_CHIP = 4
POOL_WIDTH = 512
POOL_WINDOWS = (2, 4, 8, 16)
POOL_GROUP_DIM = 128
HEAD_DIM = 64
N_Q_HEADS = 8
N_KV_HEADS = 2
GQA_GROUP = 4
BLOCK = 128
HALO = 16
ROW_CHUNK = 32
N_BUCKETS = 32
MAX_EXACT = 16
MAX_DISTANCE = 128
EPS = 1e-6
NEG_INF = -1e30
ATTN_SCALE = float(1.0 / np.sqrt(np.float32(HEAD_DIM)))

ADAM_LR = 0.001
ADAM_B1 = 0.9
ADAM_B2 = 0.999
ADAM_EPS = 1e-08
ADAM_WD = 0.01
ADAM_STEP = 10

TOKEN_TILE = 512
FFN_TOKEN_TILE = 1024
FF_SHARDS_PER_TILE = 4
VMEM_LIMIT = 56 * 1024 * 1024
MESH = pl.DeviceIdType.MESH
ANY = pl.BlockSpec(memory_space=pl.ANY)
VMEM = pl.BlockSpec(memory_space=pltpu.VMEM)
SMEM = pl.BlockSpec(memory_space=pltpu.SMEM)


def _params(**kw):
    return pltpu.CompilerParams(vmem_limit_bytes=VMEM_LIMIT, **kw)


def _dot(a, b):
    return jnp.dot(a, b, preferred_element_type=F32)


def _dot_nt(a, b):
    return lax.dot_general(a, b, (((1,), (1,)), ((), ())), preferred_element_type=F32)


def _dot_tn(a, b):
    return lax.dot_general(a, b, (((0,), (0,)), ((), ())), preferred_element_type=F32)


def _rstd(v):
    return lax.rsqrt(jnp.mean(v * v, axis=-1, keepdims=True) + EPS)


def _norm_bwd(dout, v, r, g):
    vn = v * r
    dn = dout * g
    dv = r * (dn - vn * jnp.mean(dn * vn, axis=-1, keepdims=True))
    return dv, dout * vn


def _as_rows(v):
    return jnp.concatenate([v[:, k:k + 128] for k in range(0, v.shape[1], 128)], axis=0)


def _as_lanes(rows):
    return jnp.concatenate([rows[k:k + 1, :] for k in range(rows.shape[0])], axis=1)


def _merge_rows(value):
    s, r, c_ = value.shape
    return value.reshape(s * r, c_)


def _gather_plan(srcs, outs, send_sems, recv_sems, local_sems=None, bounce=None, rows=None):
    n = len(srcs)
    x, y, c = lax.axis_index("x"), lax.axis_index("y"), lax.axis_index("c")
    me, sibling = (x, y, c), (x, y, 1 - c)
    chips = [(1 - x, y), (x, 1 - y), (1 - x, 1 - y)]

    def slot(a, px, py, pc):
        whole = outs[a].at[4 * px + 2 * py + pc]
        return whole if rows is None or rows[a] is None else whole.at[pl.ds(*rows[a])]

    def copy(a, k, block, to, from_src=False):
        return pltpu.make_async_remote_copy(
            src_ref=srcs[a] if from_src else slot(a, *block), dst_ref=slot(a, *block),
            send_sem=send_sems.at[k * n + a], recv_sem=recv_sems.at[k * n + a], device_id=to, device_id_type=MESH)

    def own_in(a):
        return pltpu.make_async_copy(srcs[a], bounce[a], local_sems.at[a])

    def own_out(a):
        return pltpu.make_async_copy(bounce[a], slot(a, *me), local_sems.at[a])

    def first(a):
        return [copy(a, 0, me, sibling, True)] + [copy(a, 1 + j, me, (*chip, c), True) for j, chip in enumerate(chips)]

    def passed(a, j):
        return copy(a, 4 + j, (*chips[j], c), sibling)

    def start():
        for a in range(n):
            if bounce is not None:
                own_in(a).start()
            for cp in first(a):
                cp.start()

    def finish():
        if bounce is not None:
            for a in range(n):
                own_in(a).wait()
                own_out(a).start()
        for j, chip in enumerate(chips):
            for a in range(n):
                copy(a, 1 + j, (*chip, c), me).wait_recv()
                passed(a, j).start()
        for a in range(n):
            copy(a, 0, sibling, me).wait_recv()
            for j, chip in enumerate(chips):
                copy(a, 4 + j, (*chip, 1 - c), me).wait_recv()
        for a in range(n):
            for cp in first(a) + [passed(a, j) for j in range(3)]:
                cp.wait_send()
            if bounce is not None:
                own_out(a).wait()

    return start, finish


def _gather_scratch(shards):
    n = len(shards)
    return [pltpu.SemaphoreType.DMA((7 * n,)), pltpu.SemaphoreType.DMA((7 * n,)), pltpu.SemaphoreType.DMA((n,))] \
        + [pltpu.VMEM(s.shape, s.dtype) for s in shards]


def _chip_exchange_plan(srcs, outs, send_sems, recv_sems, local_sems, bounce):
    n = len(srcs)
    x, y, c = lax.axis_index("x"), lax.axis_index("y"), lax.axis_index("c")
    my_chip = 2 * x + y

    def copies():
        out = []
        for a in range(n):
            for k in range(1, N_CHIP):
                px, py = x ^ (k >> 1), y ^ (k & 1)
                out.append(pltpu.make_async_remote_copy(
                    src_ref=srcs[a].at[2 * px + py], dst_ref=outs[a].at[my_chip],
                    send_sem=send_sems.at[(k - 1) * n + a], recv_sem=recv_sems.at[(k - 1) * n + a],
                    device_id=(px, py, c), device_id_type=MESH))
        return out

    def own_in(a):
        return pltpu.make_async_copy(srcs[a].at[my_chip], bounce[a], local_sems.at[a])

    def own_out(a):
        return pltpu.make_async_copy(bounce[a], outs[a].at[my_chip], local_sems.at[a])

    def start():
        for a in range(n):
            own_in(a).start()
        for cp in copies():
            cp.start()

    def finish():
        for a in range(n):
            own_in(a).wait()
            own_out(a).start()
        for cp in copies():
            cp.wait()
        for a in range(n):
            own_out(a).wait()

    return start, finish


def _chip_exchange_scratch(parts):
    n = len(parts)
    return [pltpu.SemaphoreType.DMA((3 * n,)), pltpu.SemaphoreType.DMA((3 * n,)), pltpu.SemaphoreType.DMA((n,))] \
        + [pltpu.VMEM(p.shape[1:], p.dtype) for p in parts]


def gather_blocks(shards, name):
    def body(*refs):
        n = len(shards)
        start, finish = _gather_plan(refs[:n], refs[n:2 * n], *refs[2 * n:2 * n + 3], bounce=refs[2 * n + 3:])
        start()
        finish()

    return pl.pallas_call(
        body, name=name,
        out_shape=[jax.ShapeDtypeStruct((N_DEV, *s.shape), s.dtype) for s in shards],
        in_specs=[ANY] * len(shards), out_specs=[ANY] * len(shards),
        scratch_shapes=_gather_scratch(shards),
    )(*shards)


def pair_exchange(parts, name):
    n = len(parts)

    def body(*refs):
        p_refs, got_refs, send_sems, recv_sems = refs[:n], refs[n:2 * n], *refs[2 * n:]
        x, y, c = lax.axis_index("x"), lax.axis_index("y"), lax.axis_index("c")
        copies = [pltpu.make_async_remote_copy(
            src_ref=p_refs[a].at[1 - c], dst_ref=got_refs[a], send_sem=send_sems.at[a], recv_sem=recv_sems.at[a],
            device_id=(x, y, 1 - c), device_id_type=MESH) for a in range(n)]
        for cp in copies:
            cp.start()
        for cp in copies:
            cp.wait()

    return pl.pallas_call(
        body, name=name, out_shape=[jax.ShapeDtypeStruct(p.shape[1:], p.dtype) for p in parts],
        in_specs=[ANY] * n, out_specs=[ANY] * n,
        scratch_shapes=[pltpu.SemaphoreType.DMA((n,)), pltpu.SemaphoreType.DMA((n,))],
    )(*parts)


def pair_add(parts, got, name):
    n = len(parts)

    def body(core_ref, *refs):
        for a in range(n):
            refs[2 * n + a][...] = (refs[a][...].astype(F32) + refs[n + a][...].astype(F32)).astype(BF16)

    def own(p):
        zeros = (0,) * (p.ndim - 2)
        return pl.BlockSpec((None, 1, *p.shape[2:]), lambda i, core: (core[0], i, *zeros))

    def plain(p):
        zeros = (0,) * (p.ndim - 1)
        return pl.BlockSpec((1, *p.shape[1:]), lambda i, core: (i, *zeros))

    core = lax.axis_index("c").astype(jnp.int32).reshape(1)
    return pl.pallas_call(
        body, name=name,
        grid_spec=pltpu.PrefetchScalarGridSpec(
            num_scalar_prefetch=1, grid=(got[0].shape[0],),
            in_specs=[own(p) for p in parts] + [plain(p) for p in got], out_specs=[plain(p) for p in got]),
        out_shape=[jax.ShapeDtypeStruct(p.shape, BF16) for p in got],
        compiler_params=_params(),
    )(core, *parts, *got)


def sum_slots(slots, name):
    n = len(slots)

    def body(*refs):
        for a in range(n):
            total = refs[a][0].astype(F32)
            for s in range(1, slots[a].shape[0]):
                total = total + refs[a][s].astype(F32)
            refs[n + a][...] = total

    return pl.pallas_call(
        body, name=name,
        out_shape=[jax.ShapeDtypeStruct(p.shape[1:], F32) for p in slots],
        in_specs=[VMEM] * n, out_specs=[VMEM] * n,
        compiler_params=_params(),
    )(*slots)


def _adamw(w, g, m, v):
    m2 = ADAM_B1 * m + (1.0 - ADAM_B1) * g
    v2 = ADAM_B2 * v + (1.0 - ADAM_B2) * (g * g)
    m_hat = m2 / (1.0 - ADAM_B1 ** ADAM_STEP)
    v_hat = v2 / (1.0 - ADAM_B2 ** ADAM_STEP)
    delta = -ADAM_LR * (m_hat / (jnp.sqrt(v_hat) + ADAM_EPS) + ADAM_WD * w)
    return delta, m2, v2


def adamw_update(ws, gs, ms, vs, name):
    n = len(ws)

    def body(*refs):
        for a in range(n):
            delta, m2, v2 = _adamw(refs[a][...], refs[n + a][...], refs[2 * n + a][...], refs[3 * n + a][...])
            refs[4 * n + 3 * a][...] = delta
            refs[4 * n + 3 * a + 1][...] = m2
            refs[4 * n + 3 * a + 2][...] = v2

    out = pl.pallas_call(
        body, name=name,
        out_shape=[jax.ShapeDtypeStruct(w.shape, F32) for w in ws for _ in range(3)],
        in_specs=[VMEM] * (4 * n), out_specs=[VMEM] * (3 * n),
        compiler_params=_params(),
    )(*ws, *gs, *ms, *vs)
    return [out[3 * a:3 * a + 3] for a in range(n)]


GAIN_ROWS = 8
ROW_POOL_SCALE = 4 * GAIN_ROWS
ROW_SINKS = ROW_POOL_SCALE + 4
ROW_LOSS = ROW_SINKS + 1
ROW_W_POOL = 40
SMALL_ROWS = ROW_W_POOL + 4 * POOL_GROUP_DIM


def tail_reduce(d_in_t, gains, dpool_scale, dsinks, loss_part, dw_pool, drel_bias):
    def body(d_in_ref, g0, g1, g2, g3, dsc_ref, dsink_ref, loss_ref, dwp_ref, drb_ref, g_in_ref, total_ref, total_rb_ref,
             stage, gat, gat_rb, g_send, g_recv, pair_got, chip_part, chip_got, p_send, p_recv, x_send, x_recv):
        x, y, c = lax.axis_index("x"), lax.axis_index("y"), lax.axis_index("c")
        my_id, my_chip = 4 * x + 2 * y + c, 2 * x + y

        for k, g_ref in enumerate((g0, g1, g2, g3)):
            stage[GAIN_ROWS * k:GAIN_ROWS * (k + 1), :] = g_ref[...]
        stage[ROW_POOL_SCALE:ROW_SINKS, :] = dsc_ref[...]
        stage[ROW_SINKS:ROW_LOSS, :] = dsink_ref[...]
        stage[ROW_LOSS:ROW_LOSS + 1, :] = loss_ref[...]
        stage[ROW_LOSS + 1:ROW_W_POOL, :] = jnp.zeros((ROW_W_POOL - ROW_LOSS - 1, 128), F32)
        stage[ROW_W_POOL:, :] = dwp_ref[...].reshape(4 * POOL_GROUP_DIM, POOL_GROUP_DIM)
        gat[my_id] = stage[...]
        gat_rb[my_id] = drb_ref[...]
        start, finish = _gather_plan([stage, drb_ref], [gat, gat_rb], g_send, g_recv)
        start()

        pair = pltpu.make_async_remote_copy(
            src_ref=d_in_ref.at[1 - c], dst_ref=pair_got, send_sem=p_send, recv_sem=p_recv,
            device_id=(x, y, 1 - c), device_id_type=MESH)
        pair.start()
        pair.wait()
        chip_part[...] = (d_in_ref[c].astype(F32) + pair_got[...].astype(F32)).astype(BF16)
        copies = []
        for k in range(1, N_CHIP):
            px, py = x ^ (k >> 1), y ^ (k & 1)
            copies.append(pltpu.make_async_remote_copy(
                src_ref=chip_part.at[2 * px + py], dst_ref=chip_got.at[my_chip],
                send_sem=x_send.at[k - 1], recv_sem=x_recv.at[k - 1], device_id=(px, py, c), device_id_type=MESH))
        for cp in copies:
            cp.start()
        chip_got[my_chip] = chip_part[my_chip]

        finish()
        total, total_rb = gat[0], gat_rb[0]
        for s in range(1, N_DEV):
            total, total_rb = total + gat[s], total_rb + gat_rb[s]
        total_ref[...] = total
        total_rb_ref[...] = total_rb

        for cp in copies:
            cp.wait()
        g_in = chip_got[0].astype(F32)
        for s in range(1, N_CHIP):
            g_in = g_in + chip_got[s].astype(F32)
        g_in_ref[...] = g_in

    per_core = d_in_t.shape[1:]
    return pl.pallas_call(
        body, name="tail_reduce",
        out_shape=[jax.ShapeDtypeStruct(d_in_t.shape[2:], F32), jax.ShapeDtypeStruct((SMALL_ROWS, 128), F32),
                   jax.ShapeDtypeStruct(drel_bias.shape, F32)],
        in_specs=[VMEM] * 10, out_specs=[VMEM] * 3,
        scratch_shapes=[pltpu.VMEM((SMALL_ROWS, 128), F32), pltpu.VMEM((N_DEV, SMALL_ROWS, 128), F32),
                        pltpu.VMEM((N_DEV, *drel_bias.shape), F32),
                        pltpu.SemaphoreType.DMA((14,)), pltpu.SemaphoreType.DMA((14,)),
                        pltpu.VMEM(per_core, d_in_t.dtype), pltpu.VMEM(per_core, d_in_t.dtype),
                        pltpu.VMEM(per_core, d_in_t.dtype),
                        pltpu.SemaphoreType.DMA, pltpu.SemaphoreType.DMA,
                        pltpu.SemaphoreType.DMA((3,)), pltpu.SemaphoreType.DMA((3,))],
        compiler_params=_params(),
    )(d_in_t, *gains, dpool_scale, dsinks, loss_part, dw_pool, drel_bias)


def small_adamw(total, total_rb, small_w, small_m, small_v):
    n_small = len(small_w)

    def body(*refs):
        total_ref, rb_ref = refs[:2]
        w_refs, m_refs, v_refs = (refs[2 + k * n_small:2 + (k + 1) * n_small] for k in range(3))
        loss_out = refs[2 + 3 * n_small]
        result = refs[3 + 3 * n_small:]
        total = total_ref[...]
        loss_out[...] = total[ROW_LOSS:ROW_LOSS + 1, :]
        grads = [_as_lanes(total[GAIN_ROWS * k:GAIN_ROWS * (k + 1), :]) for k in range(4)]
        grads.append(_as_lanes(total[ROW_POOL_SCALE:ROW_SINKS, :]))
        grads.append(total[ROW_SINKS:ROW_LOSS, 0:N_Q_HEADS])
        grads.append(total[ROW_W_POOL:, :].reshape(w_refs[6].shape))
        grads.append(rb_ref[...])
        for k in range(n_small):
            delta, m2, v2 = _adamw(w_refs[k][...], grads[k], m_refs[k][...], v_refs[k][...])
            result[4 * k][...] = grads[k]
            result[4 * k + 1][...] = delta
            result[4 * k + 2][...] = m2
            result[4 * k + 3][...] = v2

    out = pl.pallas_call(
        body, name="small_adamw",
        out_shape=[jax.ShapeDtypeStruct((1, 128), F32)] + [jax.ShapeDtypeStruct(w.shape, F32) for w in small_w for _ in range(4)],
        in_specs=[VMEM] * (2 + 3 * n_small), out_specs=[VMEM] * (1 + 4 * n_small),
        compiler_params=_params(),
    )(total, total_rb, *small_w, *small_m, *small_v)
    return out[0], [out[1 + 4 * k:5 + 4 * k] for k in range(n_small)]


def norm_inproj(x, g, w_t, shard, shard_rows):
    t, d = x.shape
    n = w_t.shape[0]
    tm = TOKEN_TILE
    last = t // tm - 1
    rows = [(0, shard.shape[0])]

    def body(x_ref, g_ref, w_ref, shard_ref, proj_ref, h_ref, gathered_ref, send_sems, recv_sems, local_sems, bounce):
        i = pl.program_id(0)
        start, finish = _gather_plan([shard_ref], [gathered_ref], send_sems, recv_sems, local_sems, [bounce], rows)
        pl.when(i == 0)(start)
        xv = x_ref[...]
        h = ((xv * _rstd(xv)) * g_ref[...]).astype(BF16)
        h_ref[...] = h
        proj_ref[...] = _dot_nt(h, w_ref[...])
        pl.when(i == last)(finish)

    return pl.pallas_call(
        body, name="norm_inproj", grid=(t // tm,),
        out_shape=[jax.ShapeDtypeStruct((t, n), F32), jax.ShapeDtypeStruct((t, d), BF16),
                   jax.ShapeDtypeStruct((N_DEV, shard_rows, d), shard.dtype)],
        in_specs=[pl.BlockSpec((tm, d), lambda i: (i, 0)), pl.BlockSpec((1, d), lambda i: (0, 0)),
                  pl.BlockSpec((n, d), lambda i: (0, 0)), ANY],
        out_specs=[pl.BlockSpec((tm, n), lambda i: (i, 0)), pl.BlockSpec((tm, d), lambda i: (i, 0)), ANY],
        scratch_shapes=_gather_scratch([shard]),
        compiler_params=_params(),
    )(x, g, w_t, shard)


def bias_band(bucket, in_window, rel_bias):
    def body(bk_ref, win_ref, rb_ref, out_ref):
        bk = bk_ref[...]
        keep = win_ref[...] > 0.5
        for h in range(N_Q_HEADS):
            acc = jnp.zeros(bk.shape, F32)
            for b in range(N_BUCKETS):
                acc = jnp.where(bk == float(b), rb_ref[b, h], acc)
            out_ref[h] = jnp.where(keep, acc, NEG_INF)

    return pl.pallas_call(
        body, name="bias_band",
        out_shape=jax.ShapeDtypeStruct((N_Q_HEADS, BLOCK, 2 * BLOCK), F32),
        in_specs=[VMEM, VMEM, SMEM], out_specs=VMEM,
    )(bucket, in_window, rel_bias)


def bias_band_bwd(bucket, dbias):
    def body(bk_ref, db_ref, out_ref):
        bk = bk_ref[...]
        for h in range(N_Q_HEADS):
            db = db_ref[h]
            for b in range(N_BUCKETS):
                out_ref[b, h] = jnp.sum(jnp.where(bk == float(b), db, 0.0))

    return pl.pallas_call(
        body, name="bias_band_bwd",
        out_shape=jax.ShapeDtypeStruct((N_BUCKETS, N_Q_HEADS), F32),
        in_specs=[VMEM, VMEM], out_specs=SMEM,
    )(bucket, dbias)


def _window_sum(buf_ref, g, w, first):
    cols = slice(g * POOL_GROUP_DIM, (g + 1) * POOL_GROUP_DIM)
    acc = None
    for k in range(w):
        piece = buf_ref[first(k):first(k) + BLOCK, cols]
        acc = piece if acc is None else acc + piece
    return acc


def _inv_count(i, w):
    row = lax.broadcasted_iota(jnp.int32, (BLOCK, 1), 0)
    return 1.0 / jnp.minimum(i * BLOCK + row + 1, w).astype(F32)


def _fill_pool_input(i, ubuf, uc_ref, halo_ref):
    ubuf[0:HALO, :] = jnp.where(i > 0, halo_ref[...], 0.0)
    ubuf[HALO:, :] = uc_ref[...]


def _pooled(i, g, w, ubuf):
    cols = slice(g * POOL_GROUP_DIM, (g + 1) * POOL_GROUP_DIM)
    return _window_sum(ubuf, g, w, lambda k: HALO - k) * _inv_count(i, w) - ubuf[HALO:, cols]


def _head_variants(pair):
    low = lax.broadcasted_iota(jnp.int32, pair.shape, 1) < HEAD_DIM
    swapped = pltpu.roll(pair, HEAD_DIM, 1)
    zero = jnp.zeros_like(pair)
    pick = lambda c, a, b: jnp.where(c, a, b).astype(BF16)
    return [[pick(low, pair, zero), pick(low, zero, swapped)], [pick(low, swapped, zero), pick(low, zero, pair)]]


def _head_probs(i, hq, rows, s_ref, biasm_ref, sinks_ref):
    s = s_ref[hq, rows, :] * ATTN_SCALE + biasm_ref[hq, rows, :]
    col = lax.broadcasted_iota(jnp.int32, s.shape, 1)
    s = jnp.where((i == 0) & (col < BLOCK), NEG_INF, s)
    sink = sinks_ref[0, hq]
    m = jnp.maximum(jnp.max(s, axis=-1, keepdims=True), sink)
    p = jnp.exp(s - m)
    e_sink = jnp.exp(sink - m)
    inv = 1.0 / (jnp.sum(p, axis=-1, keepdims=True) + e_sink)
    return p * inv, e_sink * inv


def _mixer_in_specs(cur, prv):
    return [pl.BlockSpec((BLOCK, 512), lambda i: (cur(i), 0)),
            pl.BlockSpec((HALO, 512), lambda i: (jnp.maximum(cur(i) * (BLOCK // HALO) - 1, 0), 0)),
            pl.BlockSpec((BLOCK, 512), lambda i: (cur(i), 1)),
            pl.BlockSpec((BLOCK, 256), lambda i: (cur(i), 4)),
            pl.BlockSpec((BLOCK, 256), lambda i: (prv(i), 4))]


def _mixer_param_specs():
    return [pl.BlockSpec((N_Q_HEADS, BLOCK, 2 * BLOCK), lambda i: (0, 0, 0)), SMEM,
            pl.BlockSpec((4, POOL_GROUP_DIM, POOL_GROUP_DIM), lambda i: (0, 0, 0)),
            pl.BlockSpec((1, POOL_WIDTH), lambda i: (0, 0))]


def mixers_fwd(proj, biasm, sinks, w_pool, pool_scale, shards):
    t = proj.shape[0]
    nb = t // BLOCK
    ns = len(shards)

    def body(*refs):
        uc_ref, halo_ref, q_ref, kvc_ref, kvp_ref, biasm_ref, sinks_ref, wp_ref, sc_ref = refs[:9]
        shard_refs, out_ref, gathered_refs = refs[9:9 + ns], refs[9 + ns], refs[10 + ns:10 + 2 * ns]
        ubuf, s_all, p_all, send_sems, recv_sems, local_sems = refs[10 + 2 * ns:16 + 2 * ns]
        i = pl.program_id(0)
        start, finish = _gather_plan(shard_refs, gathered_refs, send_sems, recv_sems, local_sems, refs[16 + 2 * ns:])
        pl.when(i == 0)(start)

        _fill_pool_input(i, ubuf, uc_ref, halo_ref)
        for g, w in enumerate(POOL_WINDOWS):
            mixed = _dot(_pooled(i, g, w, ubuf).astype(BF16), wp_ref[g])
            cols = slice(g * POOL_GROUP_DIM, (g + 1) * POOL_GROUP_DIM)
            out_ref[:, cols] = (mixed * sc_ref[:, cols]).astype(BF16)
        kv = jnp.concatenate([kvp_ref[...], kvc_ref[...]], axis=0)
        k_var = _head_variants(kv[:, 0:2 * HEAD_DIM])
        v_var = _head_variants(kv[:, 2 * HEAD_DIM:])
        for hq in range(N_Q_HEADS):
            j, half, h = hq // 2, hq % 2, hq // GQA_GROUP
            q2 = q_ref[:, 2 * HEAD_DIM * j:2 * HEAD_DIM * (j + 1)].astype(BF16)
            s_all[hq] = _dot_nt(q2, k_var[h][half])
        for hq in range(N_Q_HEADS):
            for r in range(0, BLOCK, ROW_CHUNK):
                rows = slice(r, r + ROW_CHUNK)
                probs, _ = _head_probs(i, hq, rows, s_all, biasm_ref, sinks_ref)
                p_all[hq, rows, :] = probs.astype(BF16)
        for j in range(N_Q_HEADS // 2):
            h = 2 * j // GQA_GROUP
            acc = _dot(p_all[2 * j], v_var[h][0]) + _dot(p_all[2 * j + 1], v_var[h][1])
            out_ref[:, POOL_WIDTH + 2 * HEAD_DIM * j:POOL_WIDTH + 2 * HEAD_DIM * (j + 1)] = acc.astype(BF16)

        pl.when(i == nb - 1)(finish)

    return pl.pallas_call(
        body, name="mixers_fwd", grid=(nb,),
        out_shape=[jax.ShapeDtypeStruct((t, 2 * POOL_WIDTH), BF16)]
        + [jax.ShapeDtypeStruct((N_DEV, *sh.shape), sh.dtype) for sh in shards],
        in_specs=_mixer_in_specs(lambda i: i, lambda i: jnp.maximum(i - 1, 0)) + _mixer_param_specs() + [ANY] * ns,
        out_specs=[pl.BlockSpec((BLOCK, 2 * POOL_WIDTH), lambda i: (i, 0))] + [ANY] * ns,
        scratch_shapes=[pltpu.VMEM((HALO + BLOCK, POOL_WIDTH), F32), pltpu.VMEM((N_Q_HEADS, BLOCK, 2 * BLOCK), F32),
                        pltpu.VMEM((N_Q_HEADS, BLOCK, 2 * BLOCK), BF16)] + _gather_scratch(shards),
        compiler_params=_params(),
    )(proj, proj, proj, proj, proj, biasm, sinks, w_pool, pool_scale, *shards)


def outproj_norm(cat, w, x, g, g_next, shard, partial):
    t, d = x.shape
    tm = TOKEN_TILE
    last = t // tm - 1
    rows = [(partial.shape[1] - shard.shape[0], shard.shape[0])]

    def body(c_ref, w_ref, x_ref, g_ref, gn_ref, shard_ref, partial_ref, mix_ref, x1_ref, h2_ref, gathered_ref,
             send_sems, recv_sems, local_sems, bounce):
        i = pl.program_id(0)
        start, finish = _gather_plan([shard_ref], [gathered_ref], send_sems, recv_sems, local_sems, [bounce], rows)
        pl.when(i == 0)(start)
        mix = _dot(c_ref[...], w_ref[...])
        mix_ref[...] = mix
        x1 = x_ref[...] + (mix * _rstd(mix)) * g_ref[...]
        x1_ref[...] = x1
        h2_ref[...] = ((x1 * _rstd(x1)) * gn_ref[...]).astype(BF16)
        pl.when(i == last)(finish)

    row = pl.BlockSpec((tm, d), lambda i: (i, 0))
    gain = pl.BlockSpec((1, d), lambda i: (0, 0))
    return pl.pallas_call(
        body, name="outproj_norm", grid=(t // tm,),
        out_shape=[jax.ShapeDtypeStruct((t, d), F32), jax.ShapeDtypeStruct((t, d), F32), jax.ShapeDtypeStruct((t, d), BF16),
                   jax.ShapeDtypeStruct(partial.shape, partial.dtype)],
        in_specs=[pl.BlockSpec((tm, cat.shape[1]), lambda i: (i, 0)), pl.BlockSpec(w.shape, lambda i: (0, 0)), row, gain, gain,
                  ANY, ANY],
        out_specs=[row, row, row, ANY],
        input_output_aliases={6: 3},
        scratch_shapes=_gather_scratch([shard]),
        compiler_params=_params(),
    )(cat, w, x, g, g_next, shard, partial)


def ffn_up(h, gate_t, up_t, down_shard):
    t, d = h.shape
    n = gate_t.shape[1]
    f = N_DEV * n
    tm, ts = FFN_TOKEN_TILE, FF_SHARDS_PER_TILE
    tn = ts * n
    steps = (f // tn, t // tm)

    def body(h_ref, wg_ref, wu_ref, shard_ref, gate_ref, up_ref, a_ref, gathered_ref,
             send_sems, recv_sems, local_sems, bounce):
        j, i = pl.program_id(0), pl.program_id(1)
        start, finish = _gather_plan([shard_ref], [gathered_ref], send_sems, recv_sems, local_sems, [bounce])
        pl.when((i == 0) & (j == 0))(start)

        hv = h_ref[...]
        gate = _dot_nt(hv, _merge_rows(wg_ref[...]))
        up = _dot_nt(hv, _merge_rows(wu_ref[...]))
        gate_ref[...] = gate.astype(BF16)
        up_ref[...] = up.astype(BF16)
        a_ref[...] = (gate * (1.0 / (1.0 + jnp.exp(-gate))) * up).astype(BF16)

        pl.when((j == steps[0] - 1) & (i == steps[1] - 1))(finish)

    wide = pl.BlockSpec((tm, tn), lambda j, i: (i, j))
    return pl.pallas_call(
        body, name="ffn_up", grid=steps,
        out_shape=[jax.ShapeDtypeStruct((t, f), BF16)] * 3
        + [jax.ShapeDtypeStruct((N_DEV, *down_shard.shape), down_shard.dtype)],
        in_specs=[pl.BlockSpec((tm, d), lambda j, i: (i, 0)),
                  pl.BlockSpec((ts, n, d), lambda j, i: (j, 0, 0)),
                  pl.BlockSpec((ts, n, d), lambda j, i: (j, 0, 0)), ANY],
        out_specs=[wide, wide, wide, ANY],
        scratch_shapes=_gather_scratch([down_shard]),
        compiler_params=_params(),
    )(h, gate_t, up_t, down_shard)


def ffn_down_loss(a, w_down, x1, g, target):
    t, d = x1.shape
    tm = TOKEN_TILE

    def body(a_ref, w_ref, x_ref, g_ref, t_ref, df_ref, dy_ref, dg_ref, loss_ref):
        @pl.when(pl.program_id(0) == 0)
        def _():
            dg_ref[...] = jnp.zeros_like(dg_ref)
            loss_ref[...] = jnp.zeros_like(loss_ref)

        f = _dot(a_ref[...], _merge_rows(w_ref[...]))
        r = _rstd(f)
        g = g_ref[...]
        err = x_ref[...] + (f * r) * g - t_ref[...]
        loss_ref[...] += 0.5 * jnp.sum(jnp.mean(err * err, axis=-1, keepdims=True))
        dy = err * (1.0 / d)
        dy_ref[...] = dy
        df, dg_rows = _norm_bwd(dy, f, r, g)
        df_ref[...] = df.astype(BF16)
        dg_ref[...] += _as_rows(jnp.sum(dg_rows, axis=0, keepdims=True))

    row = pl.BlockSpec((tm, d), lambda i: (i, 0))
    gain = pl.BlockSpec((1, d), lambda i: (0, 0))
    return pl.pallas_call(
        body, name="ffn_down_loss", grid=(t // tm,),
        out_shape=[jax.ShapeDtypeStruct((t, d), BF16), jax.ShapeDtypeStruct((t, d), F32),
                   jax.ShapeDtypeStruct((d // 128, 128), F32), jax.ShapeDtypeStruct((1, 128), F32)],
        in_specs=[pl.BlockSpec((tm, a.shape[1]), lambda i: (i, 0)), pl.BlockSpec(w_down.shape, lambda i: (0, 0, 0)), row, gain, row],
        out_specs=[row, row, pl.BlockSpec((d // 128, 128), lambda i: (0, 0)), pl.BlockSpec((1, 128), lambda i: (0, 0))],
        compiler_params=_params(),
    )(a, w_down, x1, g, target)


def ffn_down_bwd(df, w_down, gate, up):
    t, d = df.shape
    n = w_down.shape[1]
    f = gate.shape[1]
    tm, ts = FFN_TOKEN_TILE, FF_SHARDS_PER_TILE
    tn = ts * n

    def body(df_ref, w_ref, gate_ref, up_ref, dgate_ref, dup_ref):
        da = _dot_nt(df_ref[...], _merge_rows(w_ref[...]))
        gate = gate_ref[...].astype(F32)
        sig = 1.0 / (1.0 + jnp.exp(-gate))
        dgate_ref[...] = (da * up_ref[...].astype(F32) * (sig * (1.0 + gate * (1.0 - sig)))).astype(BF16)
        dup_ref[...] = (da * (gate * sig)).astype(BF16)

    wide = pl.BlockSpec((tm, tn), lambda j, i: (i, j))
    return pl.pallas_call(
        body, name="ffn_down_bwd", grid=(f // tn, t // tm),
        out_shape=[jax.ShapeDtypeStruct((t, f), BF16)] * 2,
        in_specs=[pl.BlockSpec((tm, d), lambda j, i: (i, 0)), pl.BlockSpec((ts, n, d), lambda j, i: (j, 0, 0)), wide, wide],
        out_specs=[wide, wide],
        compiler_params=_params(),
    )(df, w_down, gate, up)


def grad_rows(a, b, name, by_core=False):
    t, m = a.shape
    d = b.shape[1]
    r = m // N_DEV
    tt = TOKEN_TILE
    last = t // tt - 1
    out_shape = (2, N_CHIP, r, d) if by_core else (N_DEV, r, d)

    def body(a_ref, b_ref, out_ref, acc):
        k = pl.program_id(0)

        @pl.when(k == 0)
        def _():
            acc[...] = jnp.zeros_like(acc)

        acc[...] += _dot_tn(a_ref[...], b_ref[...])

        @pl.when(k == last)
        def _():
            if by_core:
                blocks = acc[...].reshape(N_CHIP, 2, r, d)
                for chip in range(N_CHIP):
                    for core in range(2):
                        out_ref[core, chip] = blocks[chip, core].astype(BF16)
            else:
                out_ref[...] = acc[...].reshape(out_shape).astype(BF16)

    return pl.pallas_call(
        body, name=name, grid=(t // tt,),
        out_shape=jax.ShapeDtypeStruct(out_shape, BF16),
        in_specs=[pl.BlockSpec((tt, m), lambda k: (k, 0)), pl.BlockSpec((tt, d), lambda k: (k, 0))],
        out_specs=pl.BlockSpec(out_shape, lambda k: (0,) * len(out_shape)),
        scratch_shapes=[pltpu.VMEM((m, d), F32)],
        compiler_params=_params(),
    )(a, b)


def grad_ffn(lhs, b, name, chip_parts=()):
    t, f = lhs[0].shape
    d = b.shape[1]
    nw = len(lhs)
    na = len(chip_parts)
    n = f // N_DEV
    tt, ts = TOKEN_TILE, FF_SHARDS_PER_TILE
    tn = ts * n
    steps = (f // tn, t // tt)

    def body(*refs):
        a_refs, b_ref, part_refs = refs[:nw], refs[nw], refs[nw + 1:nw + 1 + na]
        out_refs = refs[nw + 1 + na:2 * nw + 1 + na]
        slot_refs = refs[2 * nw + 1 + na:2 * nw + 1 + 2 * na]
        acc = refs[2 * nw + 1 + 2 * na]
        i, k = pl.program_id(0), pl.program_id(1)
        if na:
            send_sems, recv_sems, local_sems = refs[2 * nw + 2 + 2 * na:2 * nw + 5 + 2 * na]
            start, finish = _chip_exchange_plan(part_refs, slot_refs, send_sems, recv_sems, local_sems,
                                                refs[2 * nw + 5 + 2 * na:])
            pl.when((i == 0) & (k == 0))(start)

        @pl.when(k == 0)
        def _():
            acc[...] = jnp.zeros_like(acc)

        for w in range(nw):
            acc[w] += _dot_tn(a_refs[w][...], b_ref[...])

        @pl.when(k == steps[1] - 1)
        def _():
            for w in range(nw):
                blocks = acc[w].reshape(ts // 2, 2, n, d)
                for chip in range(ts // 2):
                    for core in range(2):
                        out_refs[w][core, chip] = blocks[chip, core].astype(BF16)

        if na:
            pl.when((i == steps[0] - 1) & (k == steps[1] - 1))(finish)

    out = pl.pallas_call(
        body, name=name, grid=steps,
        out_shape=[jax.ShapeDtypeStruct((2, N_CHIP, n, d), BF16)] * nw
        + [jax.ShapeDtypeStruct(p.shape, p.dtype) for p in chip_parts],
        in_specs=[pl.BlockSpec((tt, tn), lambda i, k: (k, i))] * nw + [pl.BlockSpec((tt, d), lambda i, k: (k, 0))] + [ANY] * na,
        out_specs=[pl.BlockSpec((2, ts // 2, n, d), lambda i, k: (0, i, 0, 0))] * nw + [ANY] * na,
        scratch_shapes=[pltpu.VMEM((nw, tn, d), F32)] + (_chip_exchange_scratch(chip_parts) if na else []),
        compiler_params=_params(),
    )(*lhs, b, *chip_parts)
    return out[:nw], out[nw:]


def ffn_up_bwd(dgate, dup, gate_t, up_t, x1, g_ffn, dy, mix, g_mix, chip_parts):
    t, d = x1.shape
    n = gate_t.shape[1]
    f = N_DEV * n
    tm = TOKEN_TILE
    na = len(chip_parts)
    last = t // tm - 1

    def body(*refs):
        dg_ref, du_ref, wg_ref, wu_ref, x_ref, gf_ref, dy_ref, mix_ref, gm_ref = refs[:9]
        part_refs = refs[9:9 + na]
        dx1_ref, dmix_ref, dgf_ref, dgm_ref = refs[9 + na:13 + na]
        slot_refs = refs[13 + na:13 + 2 * na]
        send_sems, recv_sems, local_sems = refs[13 + 2 * na:16 + 2 * na]
        i = pl.program_id(0)
        start, finish = _chip_exchange_plan(part_refs, slot_refs, send_sems, recv_sems, local_sems, refs[16 + 2 * na:])

        @pl.when(i == 0)
        def _():
            start()
            dgf_ref[...] = jnp.zeros_like(dgf_ref)
            dgm_ref[...] = jnp.zeros_like(dgm_ref)

        dh = _dot(dg_ref[...], _merge_rows(wg_ref[...])) + _dot(du_ref[...], _merge_rows(wu_ref[...]))
        x1 = x_ref[...]
        dx, dgf_rows = _norm_bwd(dh, x1, _rstd(x1), gf_ref[...])
        dx1 = dy_ref[...] + dx
        dx1_ref[...] = dx1
        dgf_ref[...] += _as_rows(jnp.sum(dgf_rows, axis=0, keepdims=True))
        mix = mix_ref[...]
        dmix, dgm_rows = _norm_bwd(dx1, mix, _rstd(mix), gm_ref[...])
        dmix_ref[...] = dmix.astype(BF16)
        dgm_ref[...] += _as_rows(jnp.sum(dgm_rows, axis=0, keepdims=True))
        pl.when(i == last)(finish)

    row = pl.BlockSpec((tm, d), lambda i: (i, 0))
    wide = pl.BlockSpec((tm, f), lambda i: (i, 0))
    gain = pl.BlockSpec((1, d), lambda i: (0, 0))
    gain_rows = pl.BlockSpec((d // 128, 128), lambda i: (0, 0))
    whole = pl.BlockSpec((N_DEV, n, d), lambda i: (0, 0, 0), pipeline_mode=pl.Buffered(1))
    out = pl.pallas_call(
        body, name="ffn_up_bwd", grid=(t // tm,),
        out_shape=[jax.ShapeDtypeStruct((t, d), F32), jax.ShapeDtypeStruct((t, d), BF16),
                   jax.ShapeDtypeStruct((d // 128, 128), F32), jax.ShapeDtypeStruct((d // 128, 128), F32)]
        + [jax.ShapeDtypeStruct(p.shape, p.dtype) for p in chip_parts],
        in_specs=[wide, wide, whole, whole, row, gain, row, row, gain] + [ANY] * na,
        out_specs=[row, row, gain_rows, gain_rows] + [ANY] * na,
        scratch_shapes=_chip_exchange_scratch(chip_parts),
        compiler_params=_params(),
    )(dgate, dup, gate_t, up_t, x1, g_ffn, dy, mix, g_mix, *chip_parts)
    return out[:4], out[4:]


def outproj_bwd(dmix, w_out):
    t, d = dmix.shape
    tm = TOKEN_TILE

    def body(dm_ref, w_ref, out_ref):
        out_ref[...] = _dot_nt(dm_ref[...], w_ref[...])

    return pl.pallas_call(
        body, name="outproj_bwd", grid=(t // tm,),
        out_shape=jax.ShapeDtypeStruct((t, w_out.shape[0]), F32),
        in_specs=[pl.BlockSpec((tm, d), lambda i: (i, 0)), pl.BlockSpec(w_out.shape, lambda i: (0, 0))],
        out_specs=pl.BlockSpec((tm, w_out.shape[0]), lambda i: (i, 0)),
        compiler_params=_params(),
    )(dmix, w_out)


def mixers_bwd(proj, dcat, biasm, sinks, w_pool, pool_scale, ffn_parts):
    t = proj.shape[0]
    nb = t // BLOCK
    na = len(ffn_parts)

    def body(*refs):
        (uc_ref, halo_ref, q_ref, kvc_ref, kvp_ref, dcat_ref, biasm_ref, sinks_ref, wp_ref, sc_ref) = refs[:10]
        part_refs = refs[10:10 + na]
        dproj_ref, dbias_ref, dsink_ref, dwp_ref, dsc_ref = refs[10 + na:15 + na]
        slot_refs = refs[15 + na:15 + 2 * na]
        ubuf, dbuf, c_u, c_q, c_kv, s_all, dp_all, ds_all, p_all = refs[15 + 2 * na:24 + 2 * na]
        send_sems, recv_sems, local_sems = refs[24 + 2 * na:27 + 2 * na]
        bounce = refs[27 + 2 * na:]
        i = pl.program_id(0)
        lane = lax.broadcasted_iota(jnp.int32, (1, 128), 1)
        start, finish = _chip_exchange_plan(part_refs, slot_refs, send_sems, recv_sems, local_sems, bounce)

        @pl.when(i == 0)
        def _():
            start()
            dbias_ref[...] = jnp.zeros_like(dbias_ref)
            dwp_ref[...] = jnp.zeros_like(dwp_ref)
            dsc_ref[...] = jnp.zeros_like(dsc_ref)
            dsink_ref[...] = jnp.zeros_like(dsink_ref)
            dbuf[...] = jnp.zeros_like(dbuf)
            c_u[...] = jnp.zeros_like(c_u)
            c_q[...] = jnp.zeros_like(c_q)
            c_kv[...] = jnp.zeros_like(c_kv)

        @pl.when(i < nb)
        def _():
            _fill_pool_input(i, ubuf, uc_ref, halo_ref)
            for g, w in enumerate(POOL_WINDOWS):
                cols = slice(g * POOL_GROUP_DIM, (g + 1) * POOL_GROUP_DIM)
                pooled = _pooled(i, g, w, ubuf).astype(BF16)
                mixed = _dot(pooled, wp_ref[g])
                dout = dcat_ref[:, cols]
                dsc_ref[g:g + 1, :] += jnp.sum(dout * mixed, axis=0, keepdims=True)
                dmixed = (dout * sc_ref[:, cols]).astype(BF16)
                dwp_ref[g] += _dot_tn(pooled, dmixed)
                dpooled = _dot_nt(dmixed, wp_ref[g])
                scaled = dpooled * _inv_count(i, w)
                dbuf[BLOCK:, cols] = scaled[0:HALO]
                dproj_ref[:, cols] = (_window_sum(dbuf, g, w, lambda k: k) + c_u[:, cols]).astype(BF16)
                dbuf[0:BLOCK, cols] = scaled
                c_u[:, cols] = -dpooled

            kv = jnp.concatenate([kvp_ref[...], kvc_ref[...]], axis=0)
            k_var = _head_variants(kv[:, 0:2 * HEAD_DIM])
            v_var = _head_variants(kv[:, 2 * HEAD_DIM:])
            q2s = [q_ref[:, 2 * HEAD_DIM * j:2 * HEAD_DIM * (j + 1)].astype(BF16) for j in range(N_Q_HEADS // 2)]
            do2s = [dcat_ref[:, POOL_WIDTH + 2 * HEAD_DIM * j:POOL_WIDTH + 2 * HEAD_DIM * (j + 1)].astype(BF16)
                    for j in range(N_Q_HEADS // 2)]
            slot = lambda hq: 4 * (hq // GQA_GROUP) + 2 * (hq % 2) + (hq % GQA_GROUP) // 2
            for hq in range(N_Q_HEADS):
                j, half, h = hq // 2, hq % 2, hq // GQA_GROUP
                s_all[hq] = _dot_nt(q2s[j], k_var[h][half])
                dp_all[hq] = _dot_nt(do2s[j], v_var[h][half])
            dsink_row = jnp.zeros((1, 128), F32)
            for hq in range(N_Q_HEADS):
                dsink = 0.0
                for r in range(0, BLOCK, ROW_CHUNK):
                    rows = slice(r, r + ROW_CHUNK)
                    probs, p_sink = _head_probs(i, hq, rows, s_all, biasm_ref, sinks_ref)
                    dp = dp_all[hq, rows, :]
                    delta = jnp.sum(probs * dp, axis=-1, keepdims=True)
                    ds = probs * (dp - delta)
                    dbias_ref[hq, rows, :] += ds
                    dsink = dsink + jnp.sum(p_sink * delta)
                    ds_all[slot(hq), rows, :] = (ds * ATTN_SCALE).astype(BF16)
                    p_all[slot(hq), rows, :] = probs.astype(BF16)
                dsink_row = dsink_row - jnp.where(lane == hq, dsink, 0.0)
            dsink_ref[...] += dsink_row
            dq2 = [None] * (N_Q_HEADS // 2)
            for hq in range(N_Q_HEADS):
                j, half, h = hq // 2, hq % 2, hq // GQA_GROUP
                dq = _dot(ds_all[slot(hq)], k_var[h][half])
                dq2[j] = dq if dq2[j] is None else dq2[j] + dq
            low = lax.broadcasted_iota(jnp.int32, (2 * BLOCK, 2 * HEAD_DIM), 1) < HEAD_DIM
            dk_half, dv_half = [[None, None], [None, None]], [[None, None], [None, None]]
            for h in range(N_KV_HEADS):
                for half in range(2):
                    heads = [hq for hq in range(GQA_GROUP * h, GQA_GROUP * (h + 1)) if hq % 2 == half]
                    base = slot(heads[0])
                    q_rows = jnp.concatenate([q2s[hq // 2] for hq in heads], axis=0)
                    do_rows = jnp.concatenate([do2s[hq // 2] for hq in heads], axis=0)
                    dk_half[h][half] = _dot_tn(_merge_rows(ds_all[base:base + 2]), q_rows)
                    dv_half[h][half] = _dot_tn(_merge_rows(p_all[base:base + 2]), do_rows)

            def pair_of(halves):
                return jnp.where(low, halves[0][0] + pltpu.roll(halves[0][1], HEAD_DIM, 1),
                                 halves[1][1] + pltpu.roll(halves[1][0], HEAD_DIM, 1))

            dkv = jnp.concatenate([pair_of(dk_half), pair_of(dv_half)], axis=1)
            dproj_ref[:, POOL_WIDTH:2 * POOL_WIDTH] = c_q[...].astype(BF16)
            dproj_ref[:, 2 * POOL_WIDTH:] = (c_kv[...] + dkv[0:BLOCK]).astype(BF16)
            c_q[...] = jnp.concatenate(dq2, axis=1)
            c_kv[...] = dkv[BLOCK:]

        @pl.when(i == nb)
        def _():
            dbuf[BLOCK:, :] = jnp.zeros((HALO, POOL_WIDTH), F32)
            for g, w in enumerate(POOL_WINDOWS):
                cols = slice(g * POOL_GROUP_DIM, (g + 1) * POOL_GROUP_DIM)
                dproj_ref[:, cols] = (_window_sum(dbuf, g, w, lambda k: k) + c_u[:, cols]).astype(BF16)
            dproj_ref[:, POOL_WIDTH:2 * POOL_WIDTH] = c_q[...].astype(BF16)
            dproj_ref[:, 2 * POOL_WIDTH:] = c_kv[...].astype(BF16)
            finish()

    cur = lambda i: jnp.minimum(i, nb - 1)
    prv = lambda i: jnp.maximum(jnp.minimum(i, nb - 1) - 1, 0)
    out = pl.pallas_call(
        body, name="mixers_bwd", grid=(nb + 1,),
        out_shape=[jax.ShapeDtypeStruct((t, proj.shape[1]), BF16),
                   jax.ShapeDtypeStruct((N_Q_HEADS, BLOCK, 2 * BLOCK), F32),
                   jax.ShapeDtypeStruct((1, 128), F32),
                   jax.ShapeDtypeStruct((4, POOL_GROUP_DIM, POOL_GROUP_DIM), F32),
                   jax.ShapeDtypeStruct((len(POOL_WINDOWS), POOL_GROUP_DIM), F32)]
        + [jax.ShapeDtypeStruct(p.shape, p.dtype) for p in ffn_parts],
        in_specs=_mixer_in_specs(cur, prv) + [pl.BlockSpec((BLOCK, 2 * POOL_WIDTH), lambda i: (cur(i), 0))]
        + _mixer_param_specs() + [ANY] * na,
        out_specs=[pl.BlockSpec((BLOCK, proj.shape[1]), lambda i: (jnp.maximum(i - 1, 0), 0)),
                   pl.BlockSpec((N_Q_HEADS, BLOCK, 2 * BLOCK), lambda i: (0, 0, 0)),
                   pl.BlockSpec((1, 128), lambda i: (0, 0)),
                   pl.BlockSpec((4, POOL_GROUP_DIM, POOL_GROUP_DIM), lambda i: (0, 0, 0)),
                   pl.BlockSpec((len(POOL_WINDOWS), POOL_GROUP_DIM), lambda i: (0, 0))] + [ANY] * na,
        scratch_shapes=[pltpu.VMEM((HALO + BLOCK, POOL_WIDTH), F32), pltpu.VMEM((BLOCK + HALO, POOL_WIDTH), F32),
                        pltpu.VMEM((BLOCK, POOL_WIDTH), F32), pltpu.VMEM((BLOCK, POOL_WIDTH), F32),
                        pltpu.VMEM((BLOCK, 256), F32),
                        pltpu.VMEM((N_Q_HEADS, BLOCK, 2 * BLOCK), F32), pltpu.VMEM((N_Q_HEADS, BLOCK, 2 * BLOCK), F32),
                        pltpu.VMEM((N_Q_HEADS, BLOCK, 2 * BLOCK), BF16), pltpu.VMEM((N_Q_HEADS, BLOCK, 2 * BLOCK), BF16)]
        + _chip_exchange_scratch(ffn_parts),
        compiler_params=_params(),
    )(proj, proj, proj, proj, proj, dcat, biasm, sinks, w_pool, pool_scale, *ffn_parts)
    return out[:5], out[5:]


def inproj_bwd(dproj, w_in_t, x, g, dx1):
    t, d = x.shape
    n = dproj.shape[1]
    tm = TOKEN_TILE

    def body(dp_ref, w_ref, x_ref, g_ref, dx1_ref, dx_ref, dg_ref):
        @pl.when(pl.program_id(0) == 0)
        def _():
            dg_ref[...] = jnp.zeros_like(dg_ref)

        dh = _dot(dp_ref[...], w_ref[...])
        xv = x_ref[...]
        dx, dg_rows = _norm_bwd(dh, xv, _rstd(xv), g_ref[...])
        dx_ref[...] = dx1_ref[...] + dx
        dg_ref[...] += _as_rows(jnp.sum(dg_rows, axis=0, keepdims=True))

    row = pl.BlockSpec((tm, d), lambda i: (i, 0))
    gain = pl.BlockSpec((1, d), lambda i: (0, 0))
    return pl.pallas_call(
        body, name="inproj_bwd", grid=(t // tm,),
        out_shape=[jax.ShapeDtypeStruct((t, d), F32), jax.ShapeDtypeStruct((d // 128, 128), F32)],
        in_specs=[pl.BlockSpec((tm, n), lambda i: (i, 0)), pl.BlockSpec(w_in_t.shape, lambda i: (0, 0)), row, gain, row],
        out_specs=[row, pl.BlockSpec((d // 128, 128), lambda i: (0, 0))],
        compiler_params=_params(),
    )(dproj, w_in_t, x, g, dx1)


def _bucket_band():
    qi = jnp.arange(BLOCK)[:, None]
    kj = jnp.arange(2 * BLOCK)[None, :]
    dist = qi + BLOCK - kj
    n = jnp.maximum(dist, 0)
    nf = jnp.maximum(n, 1).astype(F32)
    large = MAX_EXACT + (jnp.log(nf / MAX_EXACT) / np.float32(np.log(MAX_DISTANCE / MAX_EXACT))
                         * (N_BUCKETS - MAX_EXACT)).astype(jnp.int32)
    large = jnp.minimum(large, N_BUCKETS - 1)
    bucket = jnp.where(n < MAX_EXACT, n, large)
    in_window = (dist >= 0) & (dist < BLOCK)
    return bucket.astype(F32), in_window.astype(F32)


def kernel(x, g_pre_mix, w_in, w_pool, pool_scale, rel_bias, sinks, w_out, g_post_mix, g_pre_ffn, w_gate, w_up, w_down, g_post_ffn, loss_target, m_g_pre_mix, m_w_in, m_w_pool, m_pool_scale, m_rel_bias, m_sinks, m_w_out, m_g_post_mix, m_g_pre_ffn, m_w_gate, m_w_up, m_w_down, m_g_post_ffn, v_g_pre_mix, v_w_in, v_w_pool, v_pool_scale, v_rel_bias, v_sinks, v_w_out, v_g_post_mix, v_g_pre_ffn, v_w_gate, v_w_up, v_w_down, v_g_post_ffn):
    d = x.shape[-1]
    xs, target = x[0], loss_target[0]

    w_in_ts = w_in[0].T.astype(BF16)
    w_out_s = w_out[0].astype(BF16)
    gate_ts = w_gate[0].T.astype(BF16)
    up_ts = w_up[0].T.astype(BF16)
    w_down_s = w_down[0].astype(BF16)
    w_in_t, = gather_blocks([w_in_ts], "gather_w_in")
    w_in_t = w_in_t.reshape(-1, d)

    bucket, in_window = _bucket_band()
    biasm = bias_band(bucket, in_window, rel_bias)
    w_pool_b = w_pool[0].astype(BF16)
    half = up_ts.shape[0] // 2
    proj, h1, up_t = norm_inproj(xs, g_pre_mix, w_in_t, up_ts[:half], up_ts.shape[0])
    cat, gate_t, w_out_f = mixers_fwd(proj, biasm, sinks, w_pool_b, pool_scale, [gate_ts, w_out_s])
    w_out_f = w_out_f.reshape(-1, d)
    mix, x1, h2, up_t = outproj_norm(cat, w_out_f, xs, g_post_mix, g_pre_ffn, up_ts[half:], up_t)
    gate, up, act, w_down_f = ffn_up(h2, gate_t, up_t, w_down_s)
    df, dy, dg_post_ffn, loss_part = ffn_down_loss(act, w_down_f, x1, g_post_ffn, target)

    def pair_sum(parts, tag):
        return pair_add(parts, pair_exchange(parts, "pair_exchange_" + tag), "pair_add_" + tag)

    dgate, dup = ffn_down_bwd(df, w_down_f, gate, up)
    (d_gate, d_up), _ = grad_ffn([dgate, dup], h2, "grad_w_gate_up")
    (d_down,), _ = grad_ffn([act], df, "grad_w_down")
    q_gate, q_up, q_down = pair_sum([d_gate, d_up, d_down], "ffn")
    (dx1, dmix, dg_pre_ffn, dg_post_mix), gate_up_slots = ffn_up_bwd(
        dgate, dup, gate_t, up_t, x1, g_pre_ffn, dy, mix, g_post_mix, [q_gate, q_up])
    dcat = outproj_bwd(dmix, w_out_f)
    d_out = grad_rows(cat, dmix, "grad_w_out", by_core=True)
    q_out, = pair_sum([d_out], "out")
    (dproj, dbias, dsinks, dw_pool, dpool_scale), down_out_slots = mixers_bwd(
        proj, dcat, biasm, sinks, w_pool_b, pool_scale, [q_down, q_out])
    drel_bias = bias_band_bwd(bucket, dbias)
    grad_x, dg_pre_mix = inproj_bwd(dproj, w_in_t, xs, g_pre_mix, dx1)
    d_in_t = grad_rows(dproj, h1, "grad_w_in", by_core=True)

    small_w = [g_pre_mix, g_post_mix, g_pre_ffn, g_post_ffn, pool_scale, sinks, w_pool, rel_bias]
    small_m = [m_g_pre_mix, m_g_post_mix, m_g_pre_ffn, m_g_post_ffn, m_pool_scale, m_sinks, m_w_pool, m_rel_bias]
    small_v = [v_g_pre_mix, v_g_post_mix, v_g_pre_ffn, v_g_post_ffn, v_pool_scale, v_sinks, v_w_pool, v_rel_bias]
    g_in_t, total, total_rb = tail_reduce(
        d_in_t, [dg_pre_mix, dg_post_mix, dg_pre_ffn, dg_post_ffn], dpool_scale, dsinks, loss_part, dw_pool, drel_bias)
    loss_row, sm = small_adamw(total, total_rb, small_w, small_m, small_v)
    g_gate_t, g_up_t, g_down, g_out = sum_slots([*gate_up_slots, *down_out_slots], "sum_slots")
    big_w = [w_in[0], w_out[0], w_gate[0], w_up[0], w_down[0]]
    big_g = [g_in_t.T, g_out, g_gate_t.T, g_up_t.T, g_down]
    big_m = [m_w_in[0], m_w_out[0], m_w_gate[0], m_w_up[0], m_w_down[0]]
    big_v = [v_w_in[0], v_w_out[0], v_w_gate[0], v_w_up[0], v_w_down[0]]
    upd = adamw_update(big_w[:2], big_g[:2], big_m[:2], big_v[:2], "adamw_mix") \
        + adamw_update(big_w[2:], big_g[2:], big_m[2:], big_v[2:], "adamw_ffn")
    big = [[big_g[k][None], *(u[None] for u in upd[k])] for k in range(5)]

    def ordered(kind):
        s, b = [p[kind] for p in sm], [p[kind] for p in big]
        return [s[0], b[0], s[6], s[4], s[7], s[5], b[1], s[1], s[2], b[2], b[3], b[4], s[3]]

    return (loss_row[0, 0], grad_x[None], *ordered(0), *ordered(1), *ordered(2), *ordered(3))
```

```python
import numpy as np
import jax
import jax.numpy as jnp
from jax import lax
from jax.experimental import pallas as pl
from jax.experimental.pallas import tpu as pltpu

F32 = jnp.float32
BF16 = jnp.bfloat16

N_DEV = 8
N_CHIP = 4
POOL_WIDTH = 512
POOL_WINDOWS = (2, 4, 8, 16)
POOL_GROUP_DIM = 128
HEAD_DIM = 64
N_Q_HEADS = 8
N_KV_HEADS = 2
GQA_GROUP = 4
BLOCK = 128
HALO = 16
ROW_CHUNK = 32
N_BUCKETS = 32
MAX_EXACT = 16
MAX_DISTANCE = 128
EPS = 1e-6
NEG_INF = -1e30
ATTN_SCALE = float(1.0 / np.sqrt(np.float32(HEAD_DIM)))

ADAM_LR = 0.001
ADAM_B1 = 0.9
ADAM_B2 = 0.999
ADAM_EPS = 1e-08
ADAM_WD = 0.01
ADAM_STEP = 10

TOKEN_TILE = 512
FFN_TOKEN_TILE = 1024
FF_SHARDS_PER_TILE = 4
VMEM_LIMIT = 56 * 1024 * 1024
MESH = pl.DeviceIdType.MESH
ANY = pl.BlockSpec(memory_space=pl.ANY)
VMEM = pl.BlockSpec(memory_space=pltpu.VMEM)
SMEM = pl.BlockSpec(memory_space=pltpu.SMEM)


def _params(**kw):
    return pltpu.CompilerParams(vmem_limit_bytes=VMEM_LIMIT, **kw)


def _dot(a, b):
    return jnp.dot(a, b, preferred_element_type=F32)


def _dot_nt(a, b):
    return lax.dot_general(a, b, (((1,), (1,)), ((), ())), preferred_element_type=F32)


def _dot_tn(a, b):
    return lax.dot_general(a, b, (((0,), (0,)), ((), ())), preferred_element_type=F32)


def _rstd(v):
    return lax.rsqrt(jnp.mean(v * v, axis=-1, keepdims=True) + EPS)


def _norm_bwd(dout, v, r, g):
    vn = v * r
    dn = dout * g
    dv = r * (dn - vn * jnp.mean(dn * vn, axis=-1, keepdims=True))
    return dv, dout * vn


def _as_rows(v):
    return jnp.concatenate([v[:, k:k + 128] for k in range(0, v.shape[1], 128)], axis=0)


def _as_lanes(rows):
    return jnp.concatenate([rows[k:k + 1, :] for k in range(rows.shape[0])], axis=1)


def _merge_rows(value):
    s, r, c_ = value.shape
    return value.reshape(s * r, c_)


def _gather_plan(srcs, outs, send_sems, recv_sems, local_sems=None, bounce=None, rows=None):
    n = len(srcs)
    x, y, c = lax.axis_index("x"), lax.axis_index("y"), lax.axis_index("c")
    me, sibling = (x, y, c), (x, y, 1 - c)
    chips = [(1 - x, y), (x, 1 - y), (1 - x, 1 - y)]

    def slot(a, px, py, pc):
        whole = outs[a].at[4 * px + 2 * py + pc]
        return whole if rows is None or rows[a] is None else whole.at[pl.ds(*rows[a])]

    def copy(a, k, block, to, from_src=False):
        return pltpu.make_async_remote_copy(
            src_ref=srcs[a] if from_src else slot(a, *block), dst_ref=slot(a, *block),
            send_sem=send_sems.at[k * n + a], recv_sem=recv_sems.at[k * n + a], device_id=to, device_id_type=MESH)

    def own_in(a):
        return pltpu.make_async_copy(srcs[a], bounce[a], local_sems.at[a])

    def own_out(a):
        return pltpu.make_async_copy(bounce[a], slot(a, *me), local_sems.at[a])

    def first(a):
        return [copy(a, 0, me, sibling, True)] + [copy(a, 1 + j, me, (*chip, c), True) for j, chip in enumerate(chips)]

    def passed(a, j):
        return copy(a, 4 + j, (*chips[j], c), sibling)

    def start():
        for a in range(n):
            if bounce is not None:
                own_in(a).start()
            for cp in first(a):
                cp.start()

    def finish():
        if bounce is not None:
            for a in range(n):
                own_in(a).wait()
                own_out(a).start()
        for j, chip in enumerate(chips):
            for a in range(n):
                copy(a, 1 + j, (*chip, c), me).wait_recv()
                passed(a, j).start()
        for a in range(n):
            copy(a, 0, sibling, me).wait_recv()
            for j, chip in enumerate(chips):
                copy(a, 4 + j, (*chip, 1 - c), me).wait_recv()
        for a in range(n):
            for cp in first(a) + [passed(a, j) for j in range(3)]:
                cp.wait_send()
            if bounce is not None:
                own_out(a).wait()

    return start, finish


def _gather_scratch(shards):
    n = len(shards)
    return [pltpu.SemaphoreType.DMA((7 * n,)), pltpu.SemaphoreType.DMA((7 * n,)), pltpu.SemaphoreType.DMA((n,))] \
        + [pltpu.VMEM(s.shape, s.dtype) for s in shards]


def _chip_exchange_plan(srcs, outs, send_sems, recv_sems, local_sems, bounce):
    n = len(srcs)
    x, y, c = lax.axis_index("x"), lax.axis_index("y"), lax.axis_index("c")
    my_chip = 2 * x + y

    def copies():
        out = []
        for a in range(n):
            for k in range(1, N_CHIP):
                px, py = x ^ (k >> 1), y ^ (k & 1)
                out.append(pltpu.make_async_remote_copy(
                    src_ref=srcs[a].at[2 * px + py], dst_ref=outs[a].at[my_chip],
                    send_sem=send_sems.at[(k - 1) * n + a], recv_sem=recv_sems.at[(k - 1) * n + a],
                    device_id=(px, py, c), device_id_type=MESH))
        return out

    def own_in(a):
        return pltpu.make_async_copy(srcs[a].at[my_chip], bounce[a], local_sems.at[a])

    def own_out(a):
        return pltpu.make_async_copy(bounce[a], outs[a].at[my_chip], local_sems.at[a])

    def start():
        for a in range(n):
            own_in(a).start()
        for cp in copies():
            cp.start()

    def finish():
        for a in range(n):
            own_in(a).wait()
            own_out(a).start()
        for cp in copies():
            cp.wait()
        for a in range(n):
            own_out(a).wait()

    return start, finish


def _chip_exchange_scratch(parts):
    n = len(parts)
    return [pltpu.SemaphoreType.DMA((3 * n,)), pltpu.SemaphoreType.DMA((3 * n,)), pltpu.SemaphoreType.DMA((n,))] \
        + [pltpu.VMEM(p.shape[1:], p.dtype) for p in parts]


def gather_blocks(shards, name):
    def body(*refs):
        n = len(shards)
        start, finish = _gather_plan(refs[:n], refs[n:2 * n], *refs[2 * n:2 * n + 3], bounce=refs[2 * n + 3:])
        start()
        finish()

    return pl.pallas_call(
        body, name=name,
        out_shape=[jax.ShapeDtypeStruct((N_DEV, *s.shape), s.dtype) for s in shards],
        in_specs=[ANY] * len(shards), out_specs=[ANY] * len(shards),
        scratch_shapes=_gather_scratch(shards),
    )(*shards)


def pair_exchange(parts, name):
    n = len(parts)

    def body(*refs):
        p_refs, got_refs, send_sems, recv_sems = refs[:n], refs[n:2 * n], *refs[2 * n:]
        x, y, c = lax.axis_index("x"), lax.axis_index("y"), lax.axis_index("c")
        copies = [pltpu.make_async_remote_copy(
            src_ref=p_refs[a].at[1 - c], dst_ref=got_refs[a], send_sem=send_sems.at[a], recv_sem=recv_sems.at[a],
            device_id=(x, y, 1 - c), device_id_type=MESH) for a in range(n)]
        for cp in copies:
            cp.start()
        for cp in copies:
            cp.wait()

    return pl.pallas_call(
        body, name=name, out_shape=[jax.ShapeDtypeStruct(p.shape[1:], p.dtype) for p in parts],
        in_specs=[ANY] * n, out_specs=[ANY] * n,
        scratch_shapes=[pltpu.SemaphoreType.DMA((n,)), pltpu.SemaphoreType.DMA((n,))],
    )(*parts)


def pair_add(parts, got, name):
    n = len(parts)

    def body(core_ref, *refs):
        for a in range(n):
            refs[2 * n + a][...] = (refs[a][...].astype(F32) + refs[n + a][...].astype(F32)).astype(BF16)

    def own(p):
        zeros = (0,) * (p.ndim - 2)
        return pl.BlockSpec((None, 1, *p.shape[2:]), lambda i, core: (core[0], i, *zeros))

    def plain(p):
        zeros = (0,) * (p.ndim - 1)
        return pl.BlockSpec((1, *p.shape[1:]), lambda i, core: (i, *zeros))

    core = lax.axis_index("c").astype(jnp.int32).reshape(1)
    return pl.pallas_call(
        body, name=name,
        grid_spec=pltpu.PrefetchScalarGridSpec(
            num_scalar_prefetch=1, grid=(got[0].shape[0],),
            in_specs=[own(p) for p in parts] + [plain(p) for p in got], out_specs=[plain(p) for p in got]),
        out_shape=[jax.ShapeDtypeStruct(p.shape, BF16) for p in got],
        compiler_params=_params(),
    )(core, *parts, *got)


def sum_slots(slots, name):
    n = len(slots)

    def body(*refs):
        for a in range(n):
            total = refs[a][0].astype(F32)
            for s in range(1, slots[a].shape[0]):
                total = total + refs[a][s].astype(F32)
            refs[n + a][...] = total

    return pl.pallas_call(
        body, name=name,
        out_shape=[jax.ShapeDtypeStruct(p.shape[1:], F32) for p in slots],
        in_specs=[VMEM] * n, out_specs=[VMEM] * n,
        compiler_params=_params(),
    )(*slots)


def _adamw(w, g, m, v):
    m2 = ADAM_B1 * m + (1.0 - ADAM_B1) * g
    v2 = ADAM_B2 * v + (1.0 - ADAM_B2) * (g * g)
    m_hat = m2 / (1.0 - ADAM_B1 ** ADAM_STEP)
    v_hat = v2 / (1.0 - ADAM_B2 ** ADAM_STEP)
    delta = -ADAM_LR * (m_hat / (jnp.sqrt(v_hat) + ADAM_EPS) + ADAM_WD * w)
    return delta, m2, v2


def adamw_update(ws, gs, ms, vs, name):
    n = len(ws)

    def body(*refs):
        for a in range(n):
            delta, m2, v2 = _adamw(refs[a][...], refs[n + a][...], refs[2 * n + a][...], refs[3 * n + a][...])
            refs[4 * n + 3 * a][...] = delta
            refs[4 * n + 3 * a + 1][...] = m2
            refs[4 * n + 3 * a + 2][...] = v2

    out = pl.pallas_call(
        body, name=name,
        out_shape=[jax.ShapeDtypeStruct(w.shape, F32) for w in ws for _ in range(3)],
        in_specs=[VMEM] * (4 * n), out_specs=[VMEM] * (3 * n),
        compiler_params=_params(),
    )(*ws, *gs, *ms, *vs)
    return [out[3 * a:3 * a + 3] for a in range(n)]


GAIN_ROWS = 8
ROW_POOL_SCALE = 4 * GAIN_ROWS
ROW_SINKS = ROW_POOL_SCALE + 4
ROW_LOSS = ROW_SINKS + 1
ROW_W_POOL = 40
SMALL_ROWS = ROW_W_POOL + 4 * POOL_GROUP_DIM


def tail_reduce(d_in_t, gains, dpool_scale, dsinks, loss_part, dw_pool, drel_bias):
    def body(d_in_ref, g0, g1, g2, g3, dsc_ref, dsink_ref, loss_ref, dwp_ref, drb_ref, g_in_ref, total_ref, total_rb_ref,
             stage, gat, gat_rb, g_send, g_recv, pair_got, chip_part, chip_got, p_send, p_recv, x_send, x_recv):
        x, y, c = lax.axis_index("x"), lax.axis_index("y"), lax.axis_index("c")
        my_id, my_chip = 4 * x + 2 * y + c, 2 * x + y

        for k, g_ref in enumerate((g0, g1, g2, g3)):
            stage[GAIN_ROWS * k:GAIN_ROWS * (k + 1), :] = g_ref[...]
        stage[ROW_POOL_SCALE:ROW_SINKS, :] = dsc_ref[...]
        stage[ROW_SINKS:ROW_LOSS, :] = dsink_ref[...]
        stage[ROW_LOSS:ROW_LOSS + 1, :] = loss_ref[...]
        stage[ROW_LOSS + 1:ROW_W_POOL, :] = jnp.zeros((ROW_W_POOL - ROW_LOSS - 1, 128), F32)
        stage[ROW_W_POOL:, :] = dwp_ref[...].reshape(4 * POOL_GROUP_DIM, POOL_GROUP_DIM)
        gat[my_id] = stage[...]
        gat_rb[my_id] = drb_ref[...]
        start, finish = _gather_plan([stage, drb_ref], [gat, gat_rb], g_send, g_recv)
        start()

        pair = pltpu.make_async_remote_copy(
            src_ref=d_in_ref.at[1 - c], dst_ref=pair_got, send_sem=p_send, recv_sem=p_recv,
            device_id=(x, y, 1 - c), device_id_type=MESH)
        pair.start()
        pair.wait()
        chip_part[...] = (d_in_ref[c].astype(F32) + pair_got[...].astype(F32)).astype(BF16)
        copies = []
        for k in range(1, N_CHIP):
            px, py = x ^ (k >> 1), y ^ (k & 1)
            copies.append(pltpu.make_async_remote_copy(
                src_ref=chip_part.at[2 * px + py], dst_ref=chip_got.at[my_chip],
                send_sem=x_send.at[k - 1], recv_sem=x_recv.at[k - 1], device_id=(px, py, c), device_id_type=MESH))
        for cp in copies:
            cp.start()
        chip_got[my_chip] = chip_part[my_chip]

        finish()
        total, total_rb = gat[0], gat_rb[0]
        for s in range(1, N_DEV):
            total, total_rb = total + gat[s], total_rb + gat_rb[s]
        total_ref[...] = total
        total_rb_ref[...] = total_rb

        for cp in copies:
            cp.wait()
        g_in = chip_got[0].astype(F32)
        for s in range(1, N_CHIP):
            g_in = g_in + chip_got[s].astype(F32)
        g_in_ref[...] = g_in

    per_core = d_in_t.shape[1:]
    return pl.pallas_call(
        body, name="tail_reduce",
        out_shape=[jax.ShapeDtypeStruct(d_in_t.shape[2:], F32), jax.ShapeDtypeStruct((SMALL_ROWS, 128), F32),
                   jax.ShapeDtypeStruct(drel_bias.shape, F32)],
        in_specs=[VMEM] * 10, out_specs=[VMEM] * 3,
        scratch_shapes=[pltpu.VMEM((SMALL_ROWS, 128), F32), pltpu.VMEM((N_DEV, SMALL_ROWS, 128), F32),
                        pltpu.VMEM((N_DEV, *drel_bias.shape), F32),
                        pltpu.SemaphoreType.DMA((14,)), pltpu.SemaphoreType.DMA((14,)),
                        pltpu.VMEM(per_core, d_in_t.dtype), pltpu.VMEM(per_core, d_in_t.dtype),
                        pltpu.VMEM(per_core, d_in_t.dtype),
                        pltpu.SemaphoreType.DMA, pltpu.SemaphoreType.DMA,
                        pltpu.SemaphoreType.DMA((3,)), pltpu.SemaphoreType.DMA((3,))],
        compiler_params=_params(),
    )(d_in_t, *gains, dpool_scale, dsinks, loss_part, dw_pool, drel_bias)


def small_adamw(total, total_rb, small_w, small_m, small_v):
    n_small = len(small_w)

    def body(*refs):
        total_ref, rb_ref = refs[:2]
        w_refs, m_refs, v_refs = (refs[2 + k * n_small:2 + (k + 1) * n_small] for k in range(3))
        loss_out = refs[2 + 3 * n_small]
        result = refs[3 + 3 * n_small:]
        total = total_ref[...]
        loss_out[...] = total[ROW_LOSS:ROW_LOSS + 1, :]
        grads = [_as_lanes(total[GAIN_ROWS * k:GAIN_ROWS * (k + 1), :]) for k in range(4)]
        grads.append(_as_lanes(total[ROW_POOL_SCALE:ROW_SINKS, :]))
        grads.append(total[ROW_SINKS:ROW_LOSS, 0:N_Q_HEADS])
        grads.append(total[ROW_W_POOL:, :].reshape(w_refs[6].shape))
        grads.append(rb_ref[...])
        for k in range(n_small):
            delta, m2, v2 = _adamw(w_refs[k][...], grads[k], m_refs[k][...], v_refs[k][...])
            result[4 * k][...] = grads[k]
            result[4 * k + 1][...] = delta
            result[4 * k + 2][...] = m2
            result[4 * k + 3][...] = v2

    out = pl.pallas_call(
        body, name="small_adamw",
        out_shape=[jax.ShapeDtypeStruct((1, 128), F32)] + [jax.ShapeDtypeStruct(w.shape, F32) for w in small_w for _ in range(4)],
        in_specs=[VMEM] * (2 + 3 * n_small), out_specs=[VMEM] * (1 + 4 * n_small),
        compiler_params=_params(),
    )(total, total_rb, *small_w, *small_m, *small_v)
    return out[0], [out[1 + 4 * k:5 + 4 * k] for k in range(n_small)]


def norm_inproj(x, g, w_t, shard, shard_rows):
    t, d = x.shape
    n = w_t.shape[0]
    tm = TOKEN_TILE
    last = t // tm - 1
    rows = [(0, shard.shape[0])]

    def body(x_ref, g_ref, w_ref, shard_ref, proj_ref, h_ref, gathered_ref, send_sems, recv_sems, local_sems, bounce):
        i = pl.program_id(0)
        start, finish = _gather_plan([shard_ref], [gathered_ref], send_sems, recv_sems, local_sems, [bounce], rows)
        pl.when(i == 0)(start)
        xv = x_ref[...]
        h = ((xv * _rstd(xv)) * g_ref[...]).astype(BF16)
        h_ref[...] = h
        proj_ref[...] = _dot_nt(h, w_ref[...])
        pl.when(i == last)(finish)

    return pl.pallas_call(
        body, name="norm_inproj", grid=(t // tm,),
        out_shape=[jax.ShapeDtypeStruct((t, n), F32), jax.ShapeDtypeStruct((t, d), BF16),
                   jax.ShapeDtypeStruct((N_DEV, shard_rows, d), shard.dtype)],
        in_specs=[pl.BlockSpec((tm, d), lambda i: (i, 0)), pl.BlockSpec((1, d), lambda i: (0, 0)),
                  pl.BlockSpec((n, d), lambda i: (0, 0)), ANY],
        out_specs=[pl.BlockSpec((tm, n), lambda i: (i, 0)), pl.BlockSpec((tm, d), lambda i: (i, 0)), ANY],
        scratch_shapes=_gather_scratch([shard]),
        compiler_params=_params(),
    )(x, g, w_t, shard)


def bias_band(bucket, in_window, rel_bias):
    def body(bk_ref, win_ref, rb_ref, out_ref):
        bk = bk_ref[...]
        keep = win_ref[...] > 0.5
        for h in range(N_Q_HEADS):
            acc = jnp.zeros(bk.shape, F32)
            for b in range(N_BUCKETS):
                acc = jnp.where(bk == float(b), rb_ref[b, h], acc)
            out_ref[h] = jnp.where(keep, acc, NEG_INF)

    return pl.pallas_call(
        body, name="bias_band",
        out_shape=jax.ShapeDtypeStruct((N_Q_HEADS, BLOCK, 2 * BLOCK), F32),
        in_specs=[VMEM, VMEM, SMEM], out_specs=VMEM,
    )(bucket, in_window, rel_bias)


def bias_band_bwd(bucket, dbias):
    def body(bk_ref, db_ref, out_ref):
        bk = bk_ref[...]
        for h in range(N_Q_HEADS):
            db = db_ref[h]
            for b in range(N_BUCKETS):
                out_ref[h, b] = jnp.sum(jnp.where(bk == float(b), db, 0.0))

    return pl.pallas_call(
        body, name="bias_band_bwd",
        out_shape=jax.ShapeDtypeStruct((N_Q_HEADS, N_BUCKETS), F32),
        in_specs=[VMEM, VMEM], out_specs=SMEM,
    )(bucket, dbias)


def _window_sum(buf_ref, g, w, first):
    cols = slice(g * POOL_GROUP_DIM, (g + 1) * POOL_GROUP_DIM)
    acc = None
    for k in range(w):
        piece = buf_ref[first(k):first(k) + BLOCK, cols]
        acc = piece if acc is None else acc + piece
    return acc


def _inv_count(i, w):
    row = lax.broadcasted_iota(jnp.int32, (BLOCK, 1), 0)
    return 1.0 / jnp.minimum(i * BLOCK + row + 1, w).astype(F32)


def _fill_pool_input(i, ubuf, uc_ref, halo_ref):
    ubuf[0:HALO, :] = jnp.where(i > 0, halo_ref[...], 0.0)
    ubuf[HALO:, :] = uc_ref[...]


def _pooled(i, g, w, ubuf):
    cols = slice(g * POOL_GROUP_DIM, (g + 1) * POOL_GROUP_DIM)
    return _window_sum(ubuf, g, w, lambda k: HALO - k) * _inv_count(i, w) - ubuf[HALO:, cols]


def _head_variants(pair):
    low = lax.broadcasted_iota(jnp.int32, pair.shape, 1) < HEAD_DIM
    swapped = pltpu.roll(pair, HEAD_DIM, 1)
    zero = jnp.zeros_like(pair)
    pick = lambda c, a, b: jnp.where(c, a, b).astype(BF16)
    return [[pick(low, pair, zero), pick(low, zero, swapped)], [pick(low, swapped, zero), pick(low, zero, pair)]]


def _head_probs(i, hq, rows, s_ref, biasm_ref, sinks_ref):
    s = s_ref[hq, rows, :] * ATTN_SCALE + biasm_ref[hq, rows, :]
    col = lax.broadcasted_iota(jnp.int32, s.shape, 1)
    s = jnp.where((i == 0) & (col < BLOCK), NEG_INF, s)
    sink = sinks_ref[0, hq]
    m = jnp.maximum(jnp.max(s, axis=-1, keepdims=True), sink)
    p = jnp.exp(s - m)
    e_sink = jnp.exp(sink - m)
    inv = 1.0 / (jnp.sum(p, axis=-1, keepdims=True) + e_sink)
    return p * inv, e_sink * inv


def _mixer_in_specs(cur, prv):
    return [pl.BlockSpec((BLOCK, 512), lambda i: (cur(i), 0)),
            pl.BlockSpec((HALO, 512), lambda i: (jnp.maximum(cur(i) * (BLOCK // HALO) - 1, 0), 0)),
            pl.BlockSpec((BLOCK, 512), lambda i: (cur(i), 1)),
            pl.BlockSpec((BLOCK, 256), lambda i: (cur(i), 4)),
            pl.BlockSpec((BLOCK, 256), lambda i: (prv(i), 4))]


def _mixer_param_specs():
    return [pl.BlockSpec((N_Q_HEADS, BLOCK, 2 * BLOCK), lambda i: (0, 0, 0)), SMEM,
            pl.BlockSpec((4, POOL_GROUP_DIM, POOL_GROUP_DIM), lambda i: (0, 0, 0)),
            pl.BlockSpec((1, POOL_WIDTH), lambda i: (0, 0))]


def mixers_fwd(proj, biasm, sinks, w_pool, pool_scale, shards):
    t = proj.shape[0]
    nb = t // BLOCK
    ns = len(shards)

    def body(*refs):
        uc_ref, halo_ref, q_ref, kvc_ref, kvp_ref, biasm_ref, sinks_ref, wp_ref, sc_ref = refs[:9]
        shard_refs, out_ref, gathered_refs = refs[9:9 + ns], refs[9 + ns], refs[10 + ns:10 + 2 * ns]
        ubuf, s_all, p_all, send_sems, recv_sems, local_sems = refs[10 + 2 * ns:16 + 2 * ns]
        i = pl.program_id(0)
        start, finish = _gather_plan(shard_refs, gathered_refs, send_sems, recv_sems, local_sems, refs[16 + 2 * ns:])
        pl.when(i == 0)(start)

        _fill_pool_input(i, ubuf, uc_ref, halo_ref)
        for g, w in enumerate(POOL_WINDOWS):
            mixed = _dot(_pooled(i, g, w, ubuf).astype(BF16), wp_ref[g])
            cols = slice(g * POOL_GROUP_DIM, (g + 1) * POOL_GROUP_DIM)
            out_ref[:, cols] = (mixed * sc_ref[:, cols]).astype(BF16)
        kv = jnp.concatenate([kvp_ref[...], kvc_ref[...]], axis=0)
        k_var = _head_variants(kv[:, 0:2 * HEAD_DIM])
        v_var = _head_variants(kv[:, 2 * HEAD_DIM:])
        for hq in range(N_Q_HEADS):
            j, half, h = hq // 2, hq % 2, hq // GQA_GROUP
            q2 = q_ref[:, 2 * HEAD_DIM * j:2 * HEAD_DIM * (j + 1)].astype(BF16)
            s_all[hq] = _dot_nt(q2, k_var[h][half])
        for hq in range(N_Q_HEADS):
            for r in range(0, BLOCK, ROW_CHUNK):
                rows = slice(r, r + ROW_CHUNK)
                probs, _ = _head_probs(i, hq, rows, s_all, biasm_ref, sinks_ref)
                p_all[hq, rows, :] = probs.astype(BF16)
        for j in range(N_Q_HEADS // 2):
            h = 2 * j // GQA_GROUP
            acc = _dot(p_all[2 * j], v_var[h][0]) + _dot(p_all[2 * j + 1], v_var[h][1])
            out_ref[:, POOL_WIDTH + 2 * HEAD_DIM * j:POOL_WIDTH + 2 * HEAD_DIM * (j + 1)] = acc.astype(BF16)

        pl.when(i == nb - 1)(finish)

    return pl.pallas_call(
        body, name="mixers_fwd", grid=(nb,),
        out_shape=[jax.ShapeDtypeStruct((t, 2 * POOL_WIDTH), BF16)]
        + [jax.ShapeDtypeStruct((N_DEV, *sh.shape), sh.dtype) for sh in shards],
        in_specs=_mixer_in_specs(lambda i: i, lambda i: jnp.maximum(i - 1, 0)) + _mixer_param_specs() + [ANY] * ns,
        out_specs=[pl.BlockSpec((BLOCK, 2 * POOL_WIDTH), lambda i: (i, 0))] + [ANY] * ns,
        scratch_shapes=[pltpu.VMEM((HALO + BLOCK, POOL_WIDTH), F32), pltpu.VMEM((N_Q_HEADS, BLOCK, 2 * BLOCK), F32),
                        pltpu.VMEM((N_Q_HEADS, BLOCK, 2 * BLOCK), BF16)] + _gather_scratch(shards),
        compiler_params=_params(),
    )(proj, proj, proj, proj, proj, biasm, sinks, w_pool, pool_scale, *shards)


def outproj_norm(cat, w, x, g, g_next, shard, partial):
    t, d = x.shape
    tm = TOKEN_TILE
    last = t // tm - 1
    rows = [(partial.shape[1] - shard.shape[0], shard.shape[0])]

    def body(c_ref, w_ref, x_ref, g_ref, gn_ref, shard_ref, partial_ref, mix_ref, x1_ref, h2_ref, gathered_ref,
             send_sems, recv_sems, local_sems, bounce):
        i = pl.program_id(0)
        start, finish = _gather_plan([shard_ref], [gathered_ref], send_sems, recv_sems, local_sems, [bounce], rows)
        pl.when(i == 0)(start)
        mix = _dot(c_ref[...], w_ref[...])
        mix_ref[...] = mix
        x1 = x_ref[...] + (mix * _rstd(mix)) * g_ref[...]
        x1_ref[...] = x1
        h2_ref[...] = ((x1 * _rstd(x1)) * gn_ref[...]).astype(BF16)
        pl.when(i == last)(finish)

    row = pl.BlockSpec((tm, d), lambda i: (i, 0))
    gain = pl.BlockSpec((1, d), lambda i: (0, 0))
    return pl.pallas_call(
        body, name="outproj_norm", grid=(t // tm,),
        out_shape=[jax.ShapeDtypeStruct((t, d), F32), jax.ShapeDtypeStruct((t, d), F32), jax.ShapeDtypeStruct((t, d), BF16),
                   jax.ShapeDtypeStruct(partial.shape, partial.dtype)],
        in_specs=[pl.BlockSpec((tm, cat.shape[1]), lambda i: (i, 0)), pl.BlockSpec(w.shape, lambda i: (0, 0)), row, gain, gain,
                  ANY, ANY],
        out_specs=[row, row, row, ANY],
        input_output_aliases={6: 3},
        scratch_shapes=_gather_scratch([shard]),
        compiler_params=_params(),
    )(cat, w, x, g, g_next, shard, partial)


def ffn_up(h, gate_t, up_t, down_shard):
    t, d = h.shape
    n = gate_t.shape[1]
    f = N_DEV * n
    tm, ts = FFN_TOKEN_TILE, FF_SHARDS_PER_TILE
    tn = ts * n
    steps = (f // tn, t // tm)

    def body(h_ref, wg_ref, wu_ref, shard_ref, gate_ref, up_ref, a_ref, gathered_ref,
             send_sems, recv_sems, local_sems, bounce):
        j, i = pl.program_id(0), pl.program_id(1)
        start, finish = _gather_plan([shard_ref], [gathered_ref], send_sems, recv_sems, local_sems, [bounce])
        pl.when((i == 0) & (j == 0))(start)

        hv = h_ref[...]
        gate = _dot_nt(hv, _merge_rows(wg_ref[...]))
        up = _dot_nt(hv, _merge_rows(wu_ref[...]))
        gate_ref[...] = gate.astype(BF16)
        up_ref[...] = up.astype(BF16)
        a_ref[...] = (gate * (1.0 / (1.0 + jnp.exp(-gate))) * up).astype(BF16)

        pl.when((j == steps[0] - 1) & (i == steps[1] - 1))(finish)

    wide = pl.BlockSpec((tm, tn), lambda j, i: (i, j))
    return pl.pallas_call(
        body, name="ffn_up", grid=steps,
        out_shape=[jax.ShapeDtypeStruct((t, f), BF16)] * 3
        + [jax.ShapeDtypeStruct((N_DEV, *down_shard.shape), down_shard.dtype)],
        in_specs=[pl.BlockSpec((tm, d), lambda j, i: (i, 0)),
                  pl.BlockSpec((ts, n, d), lambda j, i: (j, 0, 0)),
                  pl.BlockSpec((ts, n, d), lambda j, i: (j, 0, 0)), ANY],
        out_specs=[wide, wide, wide, ANY],
        scratch_shapes=_gather_scratch([down_shard]),
        compiler_params=_params(),
    )(h, gate_t, up_t, down_shard)


def ffn_down_loss(a, w_down, x1, g, target):
    t, d = x1.shape
    tm = TOKEN_TILE

    def body(a_ref, w_ref, x_ref, g_ref, t_ref, df_ref, dy_ref, dg_ref, loss_ref):
        @pl.when(pl.program_id(0) == 0)
        def _():
            dg_ref[...] = jnp.zeros_like(dg_ref)
            loss_ref[...] = jnp.zeros_like(loss_ref)

        f = _dot(a_ref[...], _merge_rows(w_ref[...]))
        r = _rstd(f)
        g = g_ref[...]
        err = x_ref[...] + (f * r) * g - t_ref[...]
        loss_ref[...] += 0.5 * jnp.sum(jnp.mean(err * err, axis=-1, keepdims=True))
        dy = err * (1.0 / d)
        dy_ref[...] = dy
        df, dg_rows = _norm_bwd(dy, f, r, g)
        df_ref[...] = df.astype(BF16)
        dg_ref[...] += _as_rows(jnp.sum(dg_rows, axis=0, keepdims=True))

    row = pl.BlockSpec((tm, d), lambda i: (i, 0))
    gain = pl.BlockSpec((1, d), lambda i: (0, 0))
    return pl.pallas_call(
        body, name="ffn_down_loss", grid=(t // tm,),
        out_shape=[jax.ShapeDtypeStruct((t, d), BF16), jax.ShapeDtypeStruct((t, d), F32),
                   jax.ShapeDtypeStruct((d // 128, 128), F32), jax.ShapeDtypeStruct((1, 128), F32)],
        in_specs=[pl.BlockSpec((tm, a.shape[1]), lambda i: (i, 0)), pl.BlockSpec(w_down.shape, lambda i: (0, 0, 0)), row, gain, row],
        out_specs=[row, row, pl.BlockSpec((d // 128, 128), lambda i: (0, 0)), pl.BlockSpec((1, 128), lambda i: (0, 0))],
        compiler_params=_params(),
    )(a, w_down, x1, g, target)


def ffn_down_bwd(df, w_down, gate, up):
    t, d = df.shape
    n = w_down.shape[1]
    f = gate.shape[1]
    tm, ts = FFN_TOKEN_TILE, FF_SHARDS_PER_TILE
    tn = ts * n

    def body(df_ref, w_ref, gate_ref, up_ref, dgate_ref, dup_ref):
        da = _dot_nt(df_ref[...], _merge_rows(w_ref[...]))
        gate = gate_ref[...].astype(F32)
        sig = 1.0 / (1.0 + jnp.exp(-gate))
        dgate_ref[...] = (da * up_ref[...].astype(F32) * (sig * (1.0 + gate * (1.0 - sig)))).astype(BF16)
        dup_ref[...] = (da * (gate * sig)).astype(BF16)

    wide = pl.BlockSpec((tm, tn), lambda j, i: (i, j))
    return pl.pallas_call(
        body, name="ffn_down_bwd", grid=(f // tn, t // tm),
        out_shape=[jax.ShapeDtypeStruct((t, f), BF16)] * 2,
        in_specs=[pl.BlockSpec((tm, d), lambda j, i: (i, 0)), pl.BlockSpec((ts, n, d), lambda j, i: (j, 0, 0)), wide, wide],
        out_specs=[wide, wide],
        compiler_params=_params(),
    )(df, w_down, gate, up)


def grad_rows(a, b, name, by_core=False):
    t, m = a.shape
    d = b.shape[1]
    r = m // N_DEV
    tt = TOKEN_TILE
    last = t // tt - 1
    out_shape = (2, N_CHIP, r, d) if by_core else (N_DEV, r, d)

    def body(a_ref, b_ref, out_ref, acc):
        k = pl.program_id(0)

        @pl.when(k == 0)
        def _():
            acc[...] = jnp.zeros_like(acc)

        acc[...] += _dot_tn(a_ref[...], b_ref[...])

        @pl.when(k == last)
        def _():
            if by_core:
                blocks = acc[...].reshape(N_CHIP, 2, r, d)
                for chip in range(N_CHIP):
                    for core in range(2):
                        out_ref[core, chip] = blocks[chip, core].astype(BF16)
            else:
                out_ref[...] = acc[...].reshape(out_shape).astype(BF16)

    return pl.pallas_call(
        body, name=name, grid=(t // tt,),
        out_shape=jax.ShapeDtypeStruct(out_shape, BF16),
        in_specs=[pl.BlockSpec((tt, m), lambda k: (k, 0)), pl.BlockSpec((tt, d), lambda k: (k, 0))],
        out_specs=pl.BlockSpec(out_shape, lambda k: (0,) * len(out_shape)),
        scratch_shapes=[pltpu.VMEM((m, d), F32)],
        compiler_params=_params(),
    )(a, b)


def grad_ffn(lhs, b, name, chip_parts=()):
    t, f = lhs[0].shape
    d = b.shape[1]
    nw = len(lhs)
    na = len(chip_parts)
    n = f // N_DEV
    tt, ts = TOKEN_TILE, FF_SHARDS_PER_TILE
    tn = ts * n
    steps = (f // tn, t // tt)

    def body(*refs):
        a_refs, b_ref, part_refs = refs[:nw], refs[nw], refs[nw + 1:nw + 1 + na]
        out_refs = refs[nw + 1 + na:2 * nw + 1 + na]
        slot_refs = refs[2 * nw + 1 + na:2 * nw + 1 + 2 * na]
        acc = refs[2 * nw + 1 + 2 * na]
        i, k = pl.program_id(0), pl.program_id(1)
        if na:
            send_sems, recv_sems, local_sems = refs[2 * nw + 2 + 2 * na:2 * nw + 5 + 2 * na]
            start, finish = _chip_exchange_plan(part_refs, slot_refs, send_sems, recv_sems, local_sems,
                                                refs[2 * nw + 5 + 2 * na:])
            pl.when((i == 0) & (k == 0))(start)

        @pl.when(k == 0)
        def _():
            acc[...] = jnp.zeros_like(acc)

        for w in range(nw):
            acc[w] += _dot_tn(a_refs[w][...], b_ref[...])

        @pl.when(k == steps[1] - 1)
        def _():
            for w in range(nw):
                blocks = acc[w].reshape(ts // 2, 2, n, d)
                for chip in range(ts // 2):
                    for core in range(2):
                        out_refs[w][core, chip] = blocks[chip, core].astype(BF16)

        if na:
            pl.when((i == steps[0] - 1) & (k == steps[1] - 1))(finish)

    out = pl.pallas_call(
        body, name=name, grid=steps,
        out_shape=[jax.ShapeDtypeStruct((2, N_CHIP, n, d), BF16)] * nw
        + [jax.ShapeDtypeStruct(p.shape, p.dtype) for p in chip_parts],
        in_specs=[pl.BlockSpec((tt, tn), lambda i, k: (k, i))] * nw + [pl.BlockSpec((tt, d), lambda i, k: (k, 0))] + [ANY] * na,
        out_specs=[pl.BlockSpec((2, ts // 2, n, d), lambda i, k: (0, i, 0, 0))] * nw + [ANY] * na,
        scratch_shapes=[pltpu.VMEM((nw, tn, d), F32)] + (_chip_exchange_scratch(chip_parts) if na else []),
        compiler_params=_params(),
    )(*lhs, b, *chip_parts)
    return out[:nw], out[nw:]


def ffn_up_bwd(dgate, dup, gate_t, up_t, x1, g_ffn, dy, mix, g_mix, chip_parts):
    t, d = x1.shape
    n = gate_t.shape[1]
    f = N_DEV * n
    tm = TOKEN_TILE
    na = len(chip_parts)
    last = t // tm - 1

    def body(*refs):
        dg_ref, du_ref, wg_ref, wu_ref, x_ref, gf_ref, dy_ref, mix_ref, gm_ref = refs[:9]
        part_refs = refs[9:9 + na]
        dx1_ref, dmix_ref, dgf_ref, dgm_ref = refs[9 + na:13 + na]
        slot_refs = refs[13 + na:13 + 2 * na]
        send_sems, recv_sems, local_sems = refs[13 + 2 * na:16 + 2 * na]
        i = pl.program_id(0)
        start, finish = _chip_exchange_plan(part_refs, slot_refs, send_sems, recv_sems, local_sems, refs[16 + 2 * na:])

        @pl.when(i == 0)
        def _():
            start()
            dgf_ref[...] = jnp.zeros_like(dgf_ref)
            dgm_ref[...] = jnp.zeros_like(dgm_ref)

        dh = _dot(dg_ref[...], _merge_rows(wg_ref[...])) + _dot(du_ref[...], _merge_rows(wu_ref[...]))
        x1 = x_ref[...]
        dx, dgf_rows = _norm_bwd(dh, x1, _rstd(x1), gf_ref[...])
        dx1 = dy_ref[...] + dx
        dx1_ref[...] = dx1
        dgf_ref[...] += _as_rows(jnp.sum(dgf_rows, axis=0, keepdims=True))
        mix = mix_ref[...]
        dmix, dgm_rows = _norm_bwd(dx1, mix, _rstd(mix), gm_ref[...])
        dmix_ref[...] = dmix.astype(BF16)
        dgm_ref[...] += _as_rows(jnp.sum(dgm_rows, axis=0, keepdims=True))
        pl.when(i == last)(finish)

    row = pl.BlockSpec((tm, d), lambda i: (i, 0))
    wide = pl.BlockSpec((tm, f), lambda i: (i, 0))
    gain = pl.BlockSpec((1, d), lambda i: (0, 0))
    gain_rows = pl.BlockSpec((d // 128, 128), lambda i: (0, 0))
    whole = pl.BlockSpec((N_DEV, n, d), lambda i: (0, 0, 0), pipeline_mode=pl.Buffered(1))
    out = pl.pallas_call(
        body, name="ffn_up_bwd", grid=(t // tm,),
        out_shape=[jax.ShapeDtypeStruct((t, d), F32), jax.ShapeDtypeStruct((t, d), BF16),
                   jax.ShapeDtypeStruct((d // 128, 128), F32), jax.ShapeDtypeStruct((d // 128, 128), F32)]
        + [jax.ShapeDtypeStruct(p.shape, p.dtype) for p in chip_parts],
        in_specs=[wide, wide, whole, whole, row, gain, row, row, gain] + [ANY] * na,
        out_specs=[row, row, gain_rows, gain_rows] + [ANY] * na,
        scratch_shapes=_chip_exchange_scratch(chip_parts),
        compiler_params=_params(),
    )(dgate, dup, gate_t, up_t, x1, g_ffn, dy, mix, g_mix, *chip_parts)
    return out[:4], out[4:]


def outproj_bwd(dmix, w_out):
    t, d = dmix.shape
    tm = TOKEN_TILE

    def body(dm_ref, w_ref, out_ref):
        out_ref[...] = _dot_nt(dm_ref[...], w_ref[...])

    return pl.pallas_call(
        body, name="outproj_bwd", grid=(t // tm,),
        out_shape=jax.ShapeDtypeStruct((t, w_out.shape[0]), F32),
        in_specs=[pl.BlockSpec((tm, d), lambda i: (i, 0)), pl.BlockSpec(w_out.shape, lambda i: (0, 0))],
        out_specs=pl.BlockSpec((tm, w_out.shape[0]), lambda i: (i, 0)),
        compiler_params=_params(),
    )(dmix, w_out)


def mixers_bwd(proj, dcat, biasm, sinks, w_pool, pool_scale, ffn_parts):
    t = proj.shape[0]
    nb = t // BLOCK
    na = len(ffn_parts)

    def body(*refs):
        (uc_ref, halo_ref, q_ref, kvc_ref, kvp_ref, dcat_ref, biasm_ref, sinks_ref, wp_ref, sc_ref) = refs[:10]
        part_refs = refs[10:10 + na]
        dproj_ref, dbias_ref, dsink_ref, dwp_ref, dsc_ref = refs[10 + na:15 + na]
        slot_refs = refs[15 + na:15 + 2 * na]
        ubuf, dbuf, c_u, c_q, c_kv, s_all, dp_all, ds_all, p_all = refs[15 + 2 * na:24 + 2 * na]
        send_sems, recv_sems, local_sems = refs[24 + 2 * na:27 + 2 * na]
        bounce = refs[27 + 2 * na:]
        i = pl.program_id(0)
        lane = lax.broadcasted_iota(jnp.int32, (1, 128), 1)
        start, finish = _chip_exchange_plan(part_refs, slot_refs, send_sems, recv_sems, local_sems, bounce)

        @pl.when(i == 0)
        def _():
            start()
            dbias_ref[...] = jnp.zeros_like(dbias_ref)
            dwp_ref[...] = jnp.zeros_like(dwp_ref)
            dsc_ref[...] = jnp.zeros_like(dsc_ref)
            dsink_ref[...] = jnp.zeros_like(dsink_ref)
            dbuf[...] = jnp.zeros_like(dbuf)
            c_u[...] = jnp.zeros_like(c_u)
            c_q[...] = jnp.zeros_like(c_q)
            c_kv[...] = jnp.zeros_like(c_kv)

        @pl.when(i < nb)
        def _():
            _fill_pool_input(i, ubuf, uc_ref, halo_ref)
            for g, w in enumerate(POOL_WINDOWS):
                cols = slice(g * POOL_GROUP_DIM, (g + 1) * POOL_GROUP_DIM)
                pooled = _pooled(i, g, w, ubuf).astype(BF16)
                mixed = _dot(pooled, wp_ref[g])
                dout = dcat_ref[:, cols]
                dsc_ref[g:g + 1, :] += jnp.sum(dout * mixed, axis=0, keepdims=True)
                dmixed = (dout * sc_ref[:, cols]).astype(BF16)
                dwp_ref[g] += _dot_tn(pooled, dmixed)
                dpooled = _dot_nt(dmixed, wp_ref[g])
                scaled = dpooled * _inv_count(i, w)
                dbuf[BLOCK:, cols] = scaled[0:HALO]
                dproj_ref[:, cols] = (_window_sum(dbuf, g, w, lambda k: k) + c_u[:, cols]).astype(BF16)
                dbuf[0:BLOCK, cols] = scaled
                c_u[:, cols] = -dpooled

            kv = jnp.concatenate([kvp_ref[...], kvc_ref[...]], axis=0)
            k_var = _head_variants(kv[:, 0:2 * HEAD_DIM])
            v_var = _head_variants(kv[:, 2 * HEAD_DIM:])
            q2s = [q_ref[:, 2 * HEAD_DIM * j:2 * HEAD_DIM * (j + 1)].astype(BF16) for j in range(N_Q_HEADS // 2)]
            do2s = [dcat_ref[:, POOL_WIDTH + 2 * HEAD_DIM * j:POOL_WIDTH + 2 * HEAD_DIM * (j + 1)].astype(BF16)
                    for j in range(N_Q_HEADS // 2)]
            slot = lambda hq: 4 * (hq // GQA_GROUP) + 2 * (hq % 2) + (hq % GQA_GROUP) // 2
            for hq in range(N_Q_HEADS):
                j, half, h = hq // 2, hq % 2, hq // GQA_GROUP
                s_all[hq] = _dot_nt(q2s[j], k_var[h][half])
                dp_all[hq] = _dot_nt(do2s[j], v_var[h][half])
            dsink_row = jnp.zeros((1, 128), F32)
            for hq in range(N_Q_HEADS):
                dsink = 0.0
                for r in range(0, BLOCK, ROW_CHUNK):
                    rows = slice(r, r + ROW_CHUNK)
                    probs, p_sink = _head_probs(i, hq, rows, s_all, biasm_ref, sinks_ref)
                    dp = dp_all[hq, rows, :]
                    delta = jnp.sum(probs * dp, axis=-1, keepdims=True)
                    ds = probs * (dp - delta)
                    dbias_ref[hq, rows, :] += ds
                    dsink = dsink + jnp.sum(p_sink * delta)
                    ds_all[slot(hq), rows, :] = (ds * ATTN_SCALE).astype(BF16)
                    p_all[slot(hq), rows, :] = probs.astype(BF16)
                dsink_row = dsink_row - jnp.where(lane == hq, dsink, 0.0)
            dsink_ref[...] += dsink_row
            dq2 = [None] * (N_Q_HEADS // 2)
            for hq in range(N_Q_HEADS):
                j, half, h = hq // 2, hq % 2, hq // GQA_GROUP
                dq = _dot(ds_all[slot(hq)], k_var[h][half])
                dq2[j] = dq if dq2[j] is None else dq2[j] + dq
            low = lax.broadcasted_iota(jnp.int32, (2 * BLOCK, 2 * HEAD_DIM), 1) < HEAD_DIM
            dk_half, dv_half = [[None, None], [None, None]], [[None, None], [None, None]]
            for h in range(N_KV_HEADS):
                for half in range(2):
                    heads = [hq for hq in range(GQA_GROUP * h, GQA_GROUP * (h + 1)) if hq % 2 == half]
                    base = slot(heads[0])
                    q_rows = jnp.concatenate([q2s[hq // 2] for hq in heads], axis=0)
                    do_rows = jnp.concatenate([do2s[hq // 2] for hq in heads], axis=0)
                    dk_half[h][half] = _dot_tn(_merge_rows(ds_all[base:base + 2]), q_rows)
                    dv_half[h][half] = _dot_tn(_merge_rows(p_all[base:base + 2]), do_rows)

            def pair_of(halves):
                return jnp.where(low, halves[0][0] + pltpu.roll(halves[0][1], HEAD_DIM, 1),
                                 halves[1][1] + pltpu.roll(halves[1][0], HEAD_DIM, 1))

            dkv = jnp.concatenate([pair_of(dk_half), pair_of(dv_half)], axis=1)
            dproj_ref[:, POOL_WIDTH:2 * POOL_WIDTH] = c_q[...].astype(BF16)
            dproj_ref[:, 2 * POOL_WIDTH:] = (c_kv[...] + dkv[0:BLOCK]).astype(BF16)
            c_q[...] = jnp.concatenate(dq2, axis=1)
            c_kv[...] = dkv[BLOCK:]

        @pl.when(i == nb)
        def _():
            dbuf[BLOCK:, :] = jnp.zeros((HALO, POOL_WIDTH), F32)
            for g, w in enumerate(POOL_WINDOWS):
                cols = slice(g * POOL_GROUP_DIM, (g + 1) * POOL_GROUP_DIM)
                dproj_ref[:, cols] = (_window_sum(dbuf, g, w, lambda k: k) + c_u[:, cols]).astype(BF16)
            dproj_ref[:, POOL_WIDTH:2 * POOL_WIDTH] = c_q[...].astype(BF16)
            dproj_ref[:, 2 * POOL_WIDTH:] = c_kv[...].astype(BF16)
            finish()

    cur = lambda i: jnp.minimum(i, nb - 1)
    prv = lambda i: jnp.maximum(jnp.minimum(i, nb - 1) - 1, 0)
    out = pl.pallas_call(
        body, name="mixers_bwd", grid=(nb + 1,),
        out_shape=[jax.ShapeDtypeStruct((t, proj.shape[1]), BF16),
                   jax.ShapeDtypeStruct((N_Q_HEADS, BLOCK, 2 * BLOCK), F32),
                   jax.ShapeDtypeStruct((1, 128), F32),
                   jax.ShapeDtypeStruct((4, POOL_GROUP_DIM, POOL_GROUP_DIM), F32),
                   jax.ShapeDtypeStruct((len(POOL_WINDOWS), POOL_GROUP_DIM), F32)]
        + [jax.ShapeDtypeStruct(p.shape, p.dtype) for p in ffn_parts],
        in_specs=_mixer_in_specs(cur, prv) + [pl.BlockSpec((BLOCK, 2 * POOL_WIDTH), lambda i: (cur(i), 0))]
        + _mixer_param_specs() + [ANY] * na,
        out_specs=[pl.BlockSpec((BLOCK, proj.shape[1]), lambda i: (jnp.maximum(i - 1, 0), 0)),
                   pl.BlockSpec((N_Q_HEADS, BLOCK, 2 * BLOCK), lambda i: (0, 0, 0)),
                   pl.BlockSpec((1, 128), lambda i: (0, 0)),
                   pl.BlockSpec((4, POOL_GROUP_DIM, POOL_GROUP_DIM), lambda i: (0, 0, 0)),
                   pl.BlockSpec((len(POOL_WINDOWS), POOL_GROUP_DIM), lambda i: (0, 0))] + [ANY] * na,
        scratch_shapes=[pltpu.VMEM((HALO + BLOCK, POOL_WIDTH), F32), pltpu.VMEM((BLOCK + HALO, POOL_WIDTH), F32),
                        pltpu.VMEM((BLOCK, POOL_WIDTH), F32), pltpu.VMEM((BLOCK, POOL_WIDTH), F32),
                        pltpu.VMEM((BLOCK, 256), F32),
                        pltpu.VMEM((N_Q_HEADS, BLOCK, 2 * BLOCK), F32), pltpu.VMEM((N_Q_HEADS, BLOCK, 2 * BLOCK), F32),
                        pltpu.VMEM((N_Q_HEADS, BLOCK, 2 * BLOCK), BF16), pltpu.VMEM((N_Q_HEADS, BLOCK, 2 * BLOCK), BF16)]
        + _chip_exchange_scratch(ffn_parts),
        compiler_params=_params(),
    )(proj, proj, proj, proj, proj, dcat, biasm, sinks, w_pool, pool_scale, *ffn_parts)
    return out[:5], out[5:]


def inproj_bwd(dproj, w_in_t, x, g, dx1):
    t, d = x.shape
    n = dproj.shape[1]
    tm = TOKEN_TILE

    def body(dp_ref, w_ref, x_ref, g_ref, dx1_ref, dx_ref, dg_ref):
        @pl.when(pl.program_id(0) == 0)
        def _():
            dg_ref[...] = jnp.zeros_like(dg_ref)

        dh = _dot(dp_ref[...], w_ref[...])
        xv = x_ref[...]
        dx, dg_rows = _norm_bwd(dh, xv, _rstd(xv), g_ref[...])
        dx_ref[...] = dx1_ref[...] + dx
        dg_ref[...] += _as_rows(jnp.sum(dg_rows, axis=0, keepdims=True))

    row = pl.BlockSpec((tm, d), lambda i: (i, 0))
    gain = pl.BlockSpec((1, d), lambda i: (0, 0))
    return pl.pallas_call(
        body, name="inproj_bwd", grid=(t // tm,),
        out_shape=[jax.ShapeDtypeStruct((t, d), F32), jax.ShapeDtypeStruct((d // 128, 128), F32)],
        in_specs=[pl.BlockSpec((tm, n), lambda i: (i, 0)), pl.BlockSpec(w_in_t.shape, lambda i: (0, 0)), row, gain, row],
        out_specs=[row, pl.BlockSpec((d // 128, 128), lambda i: (0, 0))],
        compiler_params=_params(),
    )(dproj, w_in_t, x, g, dx1)


def _bucket_band():
    qi = jnp.arange(BLOCK)[:, None]
    kj = jnp.arange(2 * BLOCK)[None, :]
    dist = qi + BLOCK - kj
    n = jnp.maximum(dist, 0)
    nf = jnp.maximum(n, 1).astype(F32)
    large = MAX_EXACT + (jnp.log(nf / MAX_EXACT) / np.float32(np.log(MAX_DISTANCE / MAX_EXACT))
                         * (N_BUCKETS - MAX_EXACT)).astype(jnp.int32)
    large = jnp.minimum(large, N_BUCKETS - 1)
    bucket = jnp.where(n < MAX_EXACT, n, large)
    in_window = (dist >= 0) & (dist < BLOCK)
    return bucket.astype(F32), in_window.astype(F32)


def kernel(x, g_pre_mix, w_in, w_pool, pool_scale, rel_bias, sinks, w_out, g_post_mix, g_pre_ffn, w_gate, w_up, w_down, g_post_ffn, loss_target, m_g_pre_mix, m_w_in, m_w_pool, m_pool_scale, m_rel_bias, m_sinks, m_w_out, m_g_post_mix, m_g_pre_ffn, m_w_gate, m_w_up, m_w_down, m_g_post_ffn, v_g_pre_mix, v_w_in, v_w_pool, v_pool_scale, v_rel_bias, v_sinks, v_w_out, v_g_post_mix, v_g_pre_ffn, v_w_gate, v_w_up, v_w_down, v_g_post_ffn):
    d = x.shape[-1]
    xs, target = x[0], loss_target[0]

    w_in_ts = w_in[0].T.astype(BF16)
    w_out_s = w_out[0].astype(BF16)
    gate_ts = w_gate[0].T.astype(BF16)
    up_ts = w_up[0].T.astype(BF16)
    w_down_s = w_down[0].astype(BF16)
    w_in_t, = gather_blocks([w_in_ts], "gather_w_in")
    w_in_t = w_in_t.reshape(-1, d)

    bucket, in_window = _bucket_band()
    biasm = bias_band(bucket, in_window, rel_bias)
    w_pool_b = w_pool[0].astype(BF16)
    half = up_ts.shape[0] // 2
    proj, h1, up_t = norm_inproj(xs, g_pre_mix, w_in_t, up_ts[:half], up_ts.shape[0])
    cat, gate_t, w_out_f = mixers_fwd(proj, biasm, sinks, w_pool_b, pool_scale, [gate_ts, w_out_s])
    w_out_f = w_out_f.reshape(-1, d)
    mix, x1, h2, up_t = outproj_norm(cat, w_out_f, xs, g_post_mix, g_pre_ffn, up_ts[half:], up_t)
    gate, up, act, w_down_f = ffn_up(h2, gate_t, up_t, w_down_s)
    df, dy, dg_post_ffn, loss_part = ffn_down_loss(act, w_down_f, x1, g_post_ffn, target)

    def pair_sum(parts, tag):
        return pair_add(parts, pair_exchange(parts, "pair_exchange_" + tag), "pair_add_" + tag)

    dgate, dup = ffn_down_bwd(df, w_down_f, gate, up)
    (d_gate, d_up), _ = grad_ffn([dgate, dup], h2, "grad_w_gate_up")
    (d_down,), _ = grad_ffn([act], df, "grad_w_down")
    q_gate, q_up, q_down = pair_sum([d_gate, d_up, d_down], "ffn")
    (dx1, dmix, dg_pre_ffn, dg_post_mix), gate_up_slots = ffn_up_bwd(
        dgate, dup, gate_t, up_t, x1, g_pre_ffn, dy, mix, g_post_mix, [q_gate, q_up])
    dcat = outproj_bwd(dmix, w_out_f)
    d_out = grad_rows(cat, dmix, "grad_w_out", by_core=True)
    q_out, = pair_sum([d_out], "out")
    (dproj, dbias, dsinks, dw_pool, dpool_scale), down_out_slots = mixers_bwd(
        proj, dcat, biasm, sinks, w_pool_b, pool_scale, [q_down, q_out])
    drel_bias = bias_band_bwd(bucket, dbias)
    grad_x, dg_pre_mix = inproj_bwd(dproj, w_in_t, xs, g_pre_mix, dx1)
    d_in_t = grad_rows(dproj, h1, "grad_w_in", by_core=True)

    small_w = [g_pre_mix, g_post_mix, g_pre_ffn, g_post_ffn, pool_scale, sinks, w_pool, rel_bias.T]
    small_m = [m_g_pre_mix, m_g_post_mix, m_g_pre_ffn, m_g_post_ffn, m_pool_scale, m_sinks, m_w_pool, m_rel_bias.T]
    small_v = [v_g_pre_mix, v_g_post_mix, v_g_pre_ffn, v_g_post_ffn, v_pool_scale, v_sinks, v_w_pool, v_rel_bias.T]
    g_in_t, total, total_rb = tail_reduce(
        d_in_t, [dg_pre_mix, dg_post_mix, dg_pre_ffn, dg_post_ffn], dpool_scale, dsinks, loss_part, dw_pool, drel_bias)
    loss_row, sm = small_adamw(total, total_rb, small_w, small_m, small_v)
    sm[7] = [r.T for r in sm[7]]
    g_gate_t, g_up_t, g_down, g_out = sum_slots([*gate_up_slots, *down_out_slots], "sum_slots")
    big_w = [w_in[0].T, w_out[0], w_gate[0].T, w_up[0].T, w_down[0]]
    big_g = [g_in_t, g_out, g_gate_t, g_up_t, g_down]
    big_m = [m_w_in[0].T, m_w_out[0], m_w_gate[0].T, m_w_up[0].T, m_w_down[0]]
    big_v = [v_w_in[0].T, v_w_out[0], v_w_gate[0].T, v_w_up[0].T, v_w_down[0]]
    upd = adamw_update(big_w[:2], big_g[:2], big_m[:2], big_v[:2], "adamw_mix") \
        + adamw_update(big_w[2:], big_g[2:], big_m[2:], big_v[2:], "adamw_ffn")
    back = lambda k, a: (a.T if k in (0, 2, 3) else a)[None]
    big = [[back(k, big_g[k]), *(back(k, u) for u in upd[k])] for k in range(5)]

    def ordered(kind):
        s, b = [p[kind] for p in sm], [p[kind] for p in big]
        return [s[0], b[0], s[6], s[4], s[7], s[5], b[1], s[1], s[2], b[2], b[3], b[4], s[3]]

    return (loss_row[0, 0], grad_x[None], *ordered(0), *ordered(1), *ordered(2), *ordered(3))
```

```python
import numpy as np
import jax
import jax.numpy as jnp
from jax import lax
from jax.experimental import pallas as pl
from jax.experimental.pallas import tpu as pltpu

F32 = jnp.float32
BF16 = jnp.bfloat16

N_DEV = 8
N_CHIP = 4
POOL_WIDTH = 512
POOL_WINDOWS = (2, 4, 8, 16)
POOL_GROUP_DIM = 128
HEAD_DIM = 64
N_Q_HEADS = 8
N_KV_HEADS = 2
GQA_GROUP = 4
BLOCK = 128
HALO = 16
ROW_CHUNK = 32
N_BUCKETS = 32
MAX_EXACT = 16
MAX_DISTANCE = 128
EPS = 1e-6
NEG_INF = -1e30
ATTN_SCALE = float(1.0 / np.sqrt(np.float32(HEAD_DIM)))

ADAM_LR = 0.001
ADAM_B1 = 0.9
ADAM_B2 = 0.999
ADAM_EPS = 1e-08
ADAM_WD = 0.01
ADAM_STEP = 10

TOKEN_TILE = 512
FFN_TOKEN_TILE = 1024
FF_SHARDS_PER_TILE = 4
VMEM_LIMIT = 56 * 1024 * 1024
MESH = pl.DeviceIdType.MESH
ANY = pl.BlockSpec(memory_space=pl.ANY)
VMEM = pl.BlockSpec(memory_space=pltpu.VMEM)
SMEM = pl.BlockSpec(memory_space=pltpu.SMEM)


def _params(**kw):
    return pltpu.CompilerParams(vmem_limit_bytes=VMEM_LIMIT, **kw)


def _dot(a, b):
    return jnp.dot(a, b, preferred_element_type=F32)


def _dot_nt(a, b):
    return lax.dot_general(a, b, (((1,), (1,)), ((), ())), preferred_element_type=F32)


def _dot_tn(a, b):
    return lax.dot_general(a, b, (((0,), (0,)), ((), ())), preferred_element_type=F32)


def _rstd(v):
    return lax.rsqrt(jnp.mean(v * v, axis=-1, keepdims=True) + EPS)


def _norm_bwd(dout, v, r, g):
    vn = v * r
    dn = dout * g
    dv = r * (dn - vn * jnp.mean(dn * vn, axis=-1, keepdims=True))
    return dv, dout * vn


def _as_rows(v):
    return jnp.concatenate([v[:, k:k + 128] for k in range(0, v.shape[1], 128)], axis=0)


def _as_lanes(rows):
    return jnp.concatenate([rows[k:k + 1, :] for k in range(rows.shape[0])], axis=1)


def _merge_rows(value):
    s, r, c_ = value.shape
    return value.reshape(s * r, c_)


def _gather_plan(srcs, outs, send_sems, recv_sems, local_sems=None, bounce=None, rows=None):
    n = len(srcs)
    x, y, c = lax.axis_index("x"), lax.axis_index("y"), lax.axis_index("c")
    me, sibling = (x, y, c), (x, y, 1 - c)
    chips = [(1 - x, y), (x, 1 - y), (1 - x, 1 - y)]

    def slot(a, px, py, pc):
        whole = outs[a].at[4 * px + 2 * py + pc]
        return whole if rows is None or rows[a] is None else whole.at[pl.ds(*rows[a])]

    def copy(a, k, block, to, from_src=False):
        return pltpu.make_async_remote_copy(
            src_ref=srcs[a] if from_src else slot(a, *block), dst_ref=slot(a, *block),
            send_sem=send_sems.at[k * n + a], recv_sem=recv_sems.at[k * n + a], device_id=to, device_id_type=MESH)

    def own_in(a):
        return pltpu.make_async_copy(srcs[a], bounce[a], local_sems.at[a])

    def own_out(a):
        return pltpu.make_async_copy(bounce[a], slot(a, *me), local_sems.at[a])

    def first(a):
        return [copy(a, 0, me, sibling, True)] + [copy(a, 1 + j, me, (*chip, c), True) for j, chip in enumerate(chips)]

    def passed(a, j):
        return copy(a, 4 + j, (*chips[j], c), sibling)

    def start():
        for a in range(n):
            if bounce is not None:
                own_in(a).start()
            for cp in first(a):
                cp.start()

    def finish():
        if bounce is not None:
            for a in range(n):
                own_in(a).wait()
                own_out(a).start()
        for j, chip in enumerate(chips):
            for a in range(n):
                copy(a, 1 + j, (*chip, c), me).wait_recv()
                passed(a, j).start()
        for a in range(n):
            copy(a, 0, sibling, me).wait_recv()
            for j, chip in enumerate(chips):
                copy(a, 4 + j, (*chip, 1 - c), me).wait_recv()
        for a in range(n):
            for cp in first(a) + [passed(a, j) for j in range(3)]:
                cp.wait_send()
            if bounce is not None:
                own_out(a).wait()

    return start, finish


def _gather_scratch(shards):
    n = len(shards)
    return [pltpu.SemaphoreType.DMA((7 * n,)), pltpu.SemaphoreType.DMA((7 * n,)), pltpu.SemaphoreType.DMA((n,))] \
        + [pltpu.VMEM(s.shape, s.dtype) for s in shards]


def _chip_exchange_plan(srcs, outs, send_sems, recv_sems, local_sems, bounce):
    n = len(srcs)
    x, y, c = lax.axis_index("x"), lax.axis_index("y"), lax.axis_index("c")
    my_chip = 2 * x + y

    def copies():
        out = []
        for a in range(n):
            for k in range(1, N_CHIP):
                px, py = x ^ (k >> 1), y ^ (k & 1)
                out.append(pltpu.make_async_remote_copy(
                    src_ref=srcs[a].at[2 * px + py], dst_ref=outs[a].at[my_chip],
                    send_sem=send_sems.at[(k - 1) * n + a], recv_sem=recv_sems.at[(k - 1) * n + a],
                    device_id=(px, py, c), device_id_type=MESH))
        return out

    def own_in(a):
        return pltpu.make_async_copy(srcs[a].at[my_chip], bounce[a], local_sems.at[a])

    def own_out(a):
        return pltpu.make_async_copy(bounce[a], outs[a].at[my_chip], local_sems.at[a])

    def start():
        for a in range(n):
            own_in(a).start()
        for cp in copies():
            cp.start()

    def finish():
        for a in range(n):
            own_in(a).wait()
            own_out(a).start()
        for cp in copies():
            cp.wait()
        for a in range(n):
            own_out(a).wait()

    return start, finish


def _chip_exchange_scratch(parts):
    n = len(parts)
    return [pltpu.SemaphoreType.DMA((3 * n,)), pltpu.SemaphoreType.DMA((3 * n,)), pltpu.SemaphoreType.DMA((n,))] \
        + [pltpu.VMEM(p.shape[1:], p.dtype) for p in parts]


def _pair_plan(srcs, outs, send_sems, recv_sems):
    x, y, c = lax.axis_index("x"), lax.axis_index("y"), lax.axis_index("c")

    def copies():
        return [pltpu.make_async_remote_copy(
            src_ref=srcs[a].at[1 - c], dst_ref=outs[a], send_sem=send_sems.at[a], recv_sem=recv_sems.at[a],
            device_id=(x, y, 1 - c), device_id_type=MESH) for a in range(len(srcs))]

    def start():
        for cp in copies():
            cp.start()

    def finish():
        for cp in copies():
            cp.wait()

    return start, finish


def gather_blocks(shards, name):
    def body(*refs):
        n = len(shards)
        start, finish = _gather_plan(refs[:n], refs[n:2 * n], *refs[2 * n:2 * n + 3], bounce=refs[2 * n + 3:])
        start()
        finish()

    return pl.pallas_call(
        body, name=name,
        out_shape=[jax.ShapeDtypeStruct((N_DEV, *s.shape), s.dtype) for s in shards],
        in_specs=[ANY] * len(shards), out_specs=[ANY] * len(shards),
        scratch_shapes=_gather_scratch(shards),
    )(*shards)


def pair_exchange(parts, name):
    n = len(parts)

    def body(*refs):
        start, finish = _pair_plan(refs[:n], refs[n:2 * n], *refs[2 * n:])
        start()
        finish()

    return pl.pallas_call(
        body, name=name, out_shape=[jax.ShapeDtypeStruct(p.shape[1:], p.dtype) for p in parts],
        in_specs=[ANY] * n, out_specs=[ANY] * n,
        scratch_shapes=[pltpu.SemaphoreType.DMA((n,)), pltpu.SemaphoreType.DMA((n,))],
    )(*parts)


def pair_add(parts, got, name):
    n = len(parts)

    def body(core_ref, *refs):
        for a in range(n):
            refs[2 * n + a][...] = (refs[a][...].astype(F32) + refs[n + a][...].astype(F32)).astype(BF16)

    def own(p):
        zeros = (0,) * (p.ndim - 2)
        return pl.BlockSpec((None, 1, *p.shape[2:]), lambda i, core: (core[0], i, *zeros))

    def plain(p):
        zeros = (0,) * (p.ndim - 1)
        return pl.BlockSpec((1, *p.shape[1:]), lambda i, core: (i, *zeros))

    core = lax.axis_index("c").astype(jnp.int32).reshape(1)
    return pl.pallas_call(
        body, name=name,
        grid_spec=pltpu.PrefetchScalarGridSpec(
            num_scalar_prefetch=1, grid=(got[0].shape[0],),
            in_specs=[own(p) for p in parts] + [plain(p) for p in got], out_specs=[plain(p) for p in got]),
        out_shape=[jax.ShapeDtypeStruct(p.shape, BF16) for p in got],
        compiler_params=_params(),
    )(core, *parts, *got)


def sum_slots(slots, name):
    n = len(slots)

    def body(*refs):
        for a in range(n):
            total = refs[a][0].astype(F32)
            for s in range(1, slots[a].shape[0]):
                total = total + refs[a][s].astype(F32)
            refs[n + a][...] = total

    return pl.pallas_call(
        body, name=name,
        out_shape=[jax.ShapeDtypeStruct(p.shape[1:], F32) for p in slots],
        in_specs=[VMEM] * n, out_specs=[VMEM] * n,
        compiler_params=_params(),
    )(*slots)


def _adamw(w, g, m, v):
    m2 = ADAM_B1 * m + (1.0 - ADAM_B1) * g
    v2 = ADAM_B2 * v + (1.0 - ADAM_B2) * (g * g)
    m_hat = m2 / (1.0 - ADAM_B1 ** ADAM_STEP)
    v_hat = v2 / (1.0 - ADAM_B2 ** ADAM_STEP)
    delta = -ADAM_LR * (m_hat / (jnp.sqrt(v_hat) + ADAM_EPS) + ADAM_WD * w)
    return delta, m2, v2


def adamw_update(ws, gs, ms, vs, name):
    n = len(ws)

    def body(*refs):
        for a in range(n):
            delta, m2, v2 = _adamw(refs[a][...], refs[n + a][...], refs[2 * n + a][...], refs[3 * n + a][...])
            refs[4 * n + 3 * a][...] = delta
            refs[4 * n + 3 * a + 1][...] = m2
            refs[4 * n + 3 * a + 2][...] = v2

    out = pl.pallas_call(
        body, name=name,
        out_shape=[jax.ShapeDtypeStruct(w.shape, F32) for w in ws for _ in range(3)],
        in_specs=[VMEM] * (4 * n), out_specs=[VMEM] * (3 * n),
        compiler_params=_params(),
    )(*ws, *gs, *ms, *vs)
    return [out[3 * a:3 * a + 3] for a in range(n)]


GAIN_ROWS = 8
ROW_POOL_SCALE = 4 * GAIN_ROWS
ROW_SINKS = ROW_POOL_SCALE + 4
ROW_LOSS = ROW_SINKS + 1
ROW_W_POOL = 40
SMALL_ROWS = ROW_W_POOL + 4 * POOL_GROUP_DIM


def grad_w_in_small_reduce(a, b, gains, dpool_scale, dsinks, loss_part, dw_pool, drel_bias):
    t, m = a.shape
    d = b.shape[1]
    r = m // N_DEV
    tt = TOKEN_TILE
    last = t // tt - 1

    def body(a_ref, b_ref, g0, g1, g2, g3, dsc_ref, dsink_ref, loss_ref, dwp_ref, drb_ref, out_ref, total_ref, total_rb_ref,
             acc, stage, gat, gat_rb, g_send, g_recv):
        k = pl.program_id(0)
        x, y, c = lax.axis_index("x"), lax.axis_index("y"), lax.axis_index("c")
        start, finish = _gather_plan([stage, drb_ref], [gat, gat_rb], g_send, g_recv)

        @pl.when(k == 0)
        def _():
            for q, g_ref in enumerate((g0, g1, g2, g3)):
                stage[GAIN_ROWS * q:GAIN_ROWS * (q + 1), :] = g_ref[...]
            stage[ROW_POOL_SCALE:ROW_SINKS, :] = dsc_ref[...]
            stage[ROW_SINKS:ROW_LOSS, :] = dsink_ref[...]
            stage[ROW_LOSS:ROW_LOSS + 1, :] = loss_ref[...]
            stage[ROW_LOSS + 1:ROW_W_POOL, :] = jnp.zeros((ROW_W_POOL - ROW_LOSS - 1, 128), F32)
            stage[ROW_W_POOL:, :] = dwp_ref[...].reshape(4 * POOL_GROUP_DIM, POOL_GROUP_DIM)
            gat[4 * x + 2 * y + c] = stage[...]
            gat_rb[4 * x + 2 * y + c] = drb_ref[...]
            start()
            acc[...] = jnp.zeros_like(acc)

        acc[...] += _dot_tn(a_ref[...], b_ref[...])

        @pl.when(k == last)
        def _():
            blocks = acc[...].reshape(N_CHIP, 2, r, d)
            for chip in range(N_CHIP):
                for core in range(2):
                    out_ref[core, chip] = blocks[chip, core].astype(BF16)
            finish()
            total, total_rb = gat[0], gat_rb[0]
            for s in range(1, N_DEV):
                total, total_rb = total + gat[s], total_rb + gat_rb[s]
            total_ref[...] = total
            total_rb_ref[...] = total_rb

    out_shape = (2, N_CHIP, r, d)
    return pl.pallas_call(
        body, name="grad_w_in", grid=(t // tt,),
        out_shape=[jax.ShapeDtypeStruct(out_shape, BF16), jax.ShapeDtypeStruct((SMALL_ROWS, 128), F32),
                   jax.ShapeDtypeStruct(drel_bias.shape, F32)],
        in_specs=[pl.BlockSpec((tt, m), lambda k: (k, 0)), pl.BlockSpec((tt, d), lambda k: (k, 0))] + [VMEM] * 9,
        out_specs=[pl.BlockSpec(out_shape, lambda k: (0,) * len(out_shape)), VMEM, VMEM],
        scratch_shapes=[pltpu.VMEM((m, d), F32), pltpu.VMEM((SMALL_ROWS, 128), F32),
                        pltpu.VMEM((N_DEV, SMALL_ROWS, 128), F32), pltpu.VMEM((N_DEV, *drel_bias.shape), F32),
                        pltpu.SemaphoreType.DMA((14,)), pltpu.SemaphoreType.DMA((14,))],
        compiler_params=_params(),
    )(a, b, *gains, dpool_scale, dsinks, loss_part, dw_pool, drel_bias)


def reduce_w_in(d_in_t):
    def body(d_in_ref, g_in_ref, pair_got, chip_part, chip_got, p_send, p_recv, x_send, x_recv):
        x, y, c = lax.axis_index("x"), lax.axis_index("y"), lax.axis_index("c")
        my_chip = 2 * x + y
        pair = pltpu.make_async_remote_copy(
            src_ref=d_in_ref.at[1 - c], dst_ref=pair_got, send_sem=p_send, recv_sem=p_recv,
            device_id=(x, y, 1 - c), device_id_type=MESH)
        pair.start()
        pair.wait()
        chip_part[...] = (d_in_ref[c].astype(F32) + pair_got[...].astype(F32)).astype(BF16)
        copies = []
        for k in range(1, N_CHIP):
            px, py = x ^ (k >> 1), y ^ (k & 1)
            copies.append(pltpu.make_async_remote_copy(
                src_ref=chip_part.at[2 * px + py], dst_ref=chip_got.at[my_chip],
                send_sem=x_send.at[k - 1], recv_sem=x_recv.at[k - 1], device_id=(px, py, c), device_id_type=MESH))
        for cp in copies:
            cp.start()
        chip_got[my_chip] = chip_part[my_chip]
        for cp in copies:
            cp.wait()
        g_in = chip_got[0].astype(F32)
        for s in range(1, N_CHIP):
            g_in = g_in + chip_got[s].astype(F32)
        g_in_ref[...] = g_in

    per_core = d_in_t.shape[1:]
    return pl.pallas_call(
        body, name="reduce_w_in",
        out_shape=jax.ShapeDtypeStruct(d_in_t.shape[2:], F32),
        in_specs=[VMEM], out_specs=VMEM,
        scratch_shapes=[pltpu.VMEM(per_core, d_in_t.dtype), pltpu.VMEM(per_core, d_in_t.dtype),
                        pltpu.VMEM(per_core, d_in_t.dtype),
                        pltpu.SemaphoreType.DMA, pltpu.SemaphoreType.DMA,
                        pltpu.SemaphoreType.DMA((3,)), pltpu.SemaphoreType.DMA((3,))],
        compiler_params=_params(),
    )(d_in_t)


def small_adamw(total, total_rb, small_w, small_m, small_v):
    n_small = len(small_w)

    def body(*refs):
        total_ref, rb_ref = refs[:2]
        w_refs, m_refs, v_refs = (refs[2 + k * n_small:2 + (k + 1) * n_small] for k in range(3))
        loss_out = refs[2 + 3 * n_small]
        result = refs[3 + 3 * n_small:]
        total = total_ref[...]
        loss_out[...] = total[ROW_LOSS:ROW_LOSS + 1, :]
        grads = [_as_lanes(total[GAIN_ROWS * k:GAIN_ROWS * (k + 1), :]) for k in range(4)]
        grads.append(_as_lanes(total[ROW_POOL_SCALE:ROW_SINKS, :]))
        grads.append(total[ROW_SINKS:ROW_LOSS, 0:N_Q_HEADS])
        grads.append(total[ROW_W_POOL:, :].reshape(w_refs[6].shape))
        grads.append(rb_ref[...])
        for k in range(n_small):
            delta, m2, v2 = _adamw(w_refs[k][...], grads[k], m_refs[k][...], v_refs[k][...])
            result[4 * k][...] = grads[k]
            result[4 * k + 1][...] = delta
            result[4 * k + 2][...] = m2
            result[4 * k + 3][...] = v2

    out = pl.pallas_call(
        body, name="small_adamw",
        out_shape=[jax.ShapeDtypeStruct((1, 128), F32)] + [jax.ShapeDtypeStruct(w.shape, F32) for w in small_w for _ in range(4)],
        in_specs=[VMEM] * (2 + 3 * n_small), out_specs=[VMEM] * (1 + 4 * n_small),
        compiler_params=_params(),
    )(total, total_rb, *small_w, *small_m, *small_v)
    return out[0], [out[1 + 4 * k:5 + 4 * k] for k in range(n_small)]


def norm_inproj(x, g, w_t, shard, shard_rows):
    t, d = x.shape
    n = w_t.shape[0]
    tm = TOKEN_TILE
    last = t // tm - 1
    rows = [(0, shard.shape[0])]

    def body(x_ref, g_ref, w_ref, shard_ref, proj_ref, h_ref, gathered_ref, send_sems, recv_sems, local_sems, bounce):
        i = pl.program_id(0)
        start, finish = _gather_plan([shard_ref], [gathered_ref], send_sems, recv_sems, local_sems, [bounce], rows)
        pl.when(i == 0)(start)
        xv = x_ref[...]
        h = ((xv * _rstd(xv)) * g_ref[...]).astype(BF16)
        h_ref[...] = h
        proj_ref[...] = _dot_nt(h, w_ref[...])
        pl.when(i == last)(finish)

    return pl.pallas_call(
        body, name="norm_inproj", grid=(t // tm,),
        out_shape=[jax.ShapeDtypeStruct((t, n), F32), jax.ShapeDtypeStruct((t, d), BF16),
                   jax.ShapeDtypeStruct((N_DEV, shard_rows, d), shard.dtype)],
        in_specs=[pl.BlockSpec((tm, d), lambda i: (i, 0)), pl.BlockSpec((1, d), lambda i: (0, 0)),
                  pl.BlockSpec((n, d), lambda i: (0, 0)), ANY],
        out_specs=[pl.BlockSpec((tm, n), lambda i: (i, 0)), pl.BlockSpec((tm, d), lambda i: (i, 0)), ANY],
        scratch_shapes=_gather_scratch([shard]),
        compiler_params=_params(),
    )(x, g, w_t, shard)


def bias_band(bucket, in_window, rel_bias):
    def body(bk_ref, win_ref, rb_ref, out_ref):
        bk = bk_ref[...]
        keep = win_ref[...] > 0.5
        for h in range(N_Q_HEADS):
            acc = jnp.zeros(bk.shape, F32)
            for b in range(N_BUCKETS):
                acc = jnp.where(bk == float(b), rb_ref[b, h], acc)
            out_ref[h] = jnp.where(keep, acc, NEG_INF)

    return pl.pallas_call(
        body, name="bias_band",
        out_shape=jax.ShapeDtypeStruct((N_Q_HEADS, BLOCK, 2 * BLOCK), F32),
        in_specs=[VMEM, VMEM, SMEM], out_specs=VMEM,
    )(bucket, in_window, rel_bias)


def bias_band_bwd(bucket, dbias):
    def body(bk_ref, db_ref, out_ref):
        bk = bk_ref[...]
        for h in range(N_Q_HEADS):
            db = db_ref[h]
            for b in range(N_BUCKETS):
                out_ref[h, b] = jnp.sum(jnp.where(bk == float(b), db, 0.0))

    return pl.pallas_call(
        body, name="bias_band_bwd",
        out_shape=jax.ShapeDtypeStruct((N_Q_HEADS, N_BUCKETS), F32),
        in_specs=[VMEM, VMEM], out_specs=SMEM,
    )(bucket, dbias)


def _window_sum(buf_ref, g, w, first):
    cols = slice(g * POOL_GROUP_DIM, (g + 1) * POOL_GROUP_DIM)
    acc = None
    for k in range(w):
        piece = buf_ref[first(k):first(k) + BLOCK, cols]
        acc = piece if acc is None else acc + piece
    return acc


def _inv_count(i, w):
    row = lax.broadcasted_iota(jnp.int32, (BLOCK, 1), 0)
    return 1.0 / jnp.minimum(i * BLOCK + row + 1, w).astype(F32)


def _fill_pool_input(i, ubuf, uc_ref, halo_ref):
    ubuf[0:HALO, :] = jnp.where(i > 0, halo_ref[...], 0.0)
    ubuf[HALO:, :] = uc_ref[...]


def _pooled(i, g, w, ubuf):
    cols = slice(g * POOL_GROUP_DIM, (g + 1) * POOL_GROUP_DIM)
    return _window_sum(ubuf, g, w, lambda k: HALO - k) * _inv_count(i, w) - ubuf[HALO:, cols]


def _head_variants(pair):
    low = lax.broadcasted_iota(jnp.int32, pair.shape, 1) < HEAD_DIM
    swapped = pltpu.roll(pair, HEAD_DIM, 1)
    zero = jnp.zeros_like(pair)
    pick = lambda c, a, b: jnp.where(c, a, b).astype(BF16)
    return [[pick(low, pair, zero), pick(low, zero, swapped)], [pick(low, swapped, zero), pick(low, zero, pair)]]


def _head_probs(i, hq, rows, s_ref, biasm_ref, sinks_ref):
    s = s_ref[hq, rows, :] * ATTN_SCALE + biasm_ref[hq, rows, :]
    col = lax.broadcasted_iota(jnp.int32, s.shape, 1)
    s = jnp.where((i == 0) & (col < BLOCK), NEG_INF, s)
    sink = sinks_ref[0, hq]
    m = jnp.maximum(jnp.max(s, axis=-1, keepdims=True), sink)
    p = jnp.exp(s - m)
    e_sink = jnp.exp(sink - m)
    inv = 1.0 / (jnp.sum(p, axis=-1, keepdims=True) + e_sink)
    return p * inv, e_sink * inv


def _mixer_in_specs(cur, prv):
    return [pl.BlockSpec((BLOCK, 512), lambda i: (cur(i), 0)),
            pl.BlockSpec((HALO, 512), lambda i: (jnp.maximum(cur(i) * (BLOCK // HALO) - 1, 0), 0)),
            pl.BlockSpec((BLOCK, 512), lambda i: (cur(i), 1)),
            pl.BlockSpec((BLOCK, 256), lambda i: (cur(i), 4)),
            pl.BlockSpec((BLOCK, 256), lambda i: (prv(i), 4))]


def _mixer_param_specs():
    return [pl.BlockSpec((N_Q_HEADS, BLOCK, 2 * BLOCK), lambda i: (0, 0, 0)), SMEM,
            pl.BlockSpec((4, POOL_GROUP_DIM, POOL_GROUP_DIM), lambda i: (0, 0, 0)),
            pl.BlockSpec((1, POOL_WIDTH), lambda i: (0, 0))]


def mixers_fwd(proj, biasm, sinks, w_pool, pool_scale, shards):
    t = proj.shape[0]
    nb = t // BLOCK
    ns = len(shards)

    def body(*refs):
        uc_ref, halo_ref, q_ref, kvc_ref, kvp_ref, biasm_ref, sinks_ref, wp_ref, sc_ref = refs[:9]
        shard_refs, out_ref, gathered_refs = refs[9:9 + ns], refs[9 + ns], refs[10 + ns:10 + 2 * ns]
        ubuf, s_all, p_all, send_sems, recv_sems, local_sems = refs[10 + 2 * ns:16 + 2 * ns]
        i = pl.program_id(0)
        start, finish = _gather_plan(shard_refs, gathered_refs, send_sems, recv_sems, local_sems, refs[16 + 2 * ns:])
        pl.when(i == 0)(start)

        _fill_pool_input(i, ubuf, uc_ref, halo_ref)
        for g, w in enumerate(POOL_WINDOWS):
            mixed = _dot(_pooled(i, g, w, ubuf).astype(BF16), wp_ref[g])
            cols = slice(g * POOL_GROUP_DIM, (g + 1) * POOL_GROUP_DIM)
            out_ref[:, cols] = (mixed * sc_ref[:, cols]).astype(BF16)
        kv = jnp.concatenate([kvp_ref[...], kvc_ref[...]], axis=0)
        k_var = _head_variants(kv[:, 0:2 * HEAD_DIM])
        v_var = _head_variants(kv[:, 2 * HEAD_DIM:])
        for hq in range(N_Q_HEADS):
            j, half, h = hq // 2, hq % 2, hq // GQA_GROUP
            q2 = q_ref[:, 2 * HEAD_DIM * j:2 * HEAD_DIM * (j + 1)].astype(BF16)
            s_all[hq] = _dot_nt(q2, k_var[h][half])
        for hq in range(N_Q_HEADS):
            for r in range(0, BLOCK, ROW_CHUNK):
                rows = slice(r, r + ROW_CHUNK)
                probs, _ = _head_probs(i, hq, rows, s_all, biasm_ref, sinks_ref)
                p_all[hq, rows, :] = probs.astype(BF16)
        for j in range(N_Q_HEADS // 2):
            h = 2 * j // GQA_GROUP
            acc = _dot(p_all[2 * j], v_var[h][0]) + _dot(p_all[2 * j + 1], v_var[h][1])
            out_ref[:, POOL_WIDTH + 2 * HEAD_DIM * j:POOL_WIDTH + 2 * HEAD_DIM * (j + 1)] = acc.astype(BF16)

        pl.when(i == nb - 1)(finish)

    return pl.pallas_call(
        body, name="mixers_fwd", grid=(nb,),
        out_shape=[jax.ShapeDtypeStruct((t, 2 * POOL_WIDTH), BF16)]
        + [jax.ShapeDtypeStruct((N_DEV, *sh.shape), sh.dtype) for sh in shards],
        in_specs=_mixer_in_specs(lambda i: i, lambda i: jnp.maximum(i - 1, 0)) + _mixer_param_specs() + [ANY] * ns,
        out_specs=[pl.BlockSpec((BLOCK, 2 * POOL_WIDTH), lambda i: (i, 0))] + [ANY] * ns,
        scratch_shapes=[pltpu.VMEM((HALO + BLOCK, POOL_WIDTH), F32), pltpu.VMEM((N_Q_HEADS, BLOCK, 2 * BLOCK), F32),
                        pltpu.VMEM((N_Q_HEADS, BLOCK, 2 * BLOCK), BF16)] + _gather_scratch(shards),
        compiler_params=_params(),
    )(proj, proj, proj, proj, proj, biasm, sinks, w_pool, pool_scale, *shards)


def outproj_norm(cat, w, x, g, g_next, shard, partial):
    t, d = x.shape
    tm = TOKEN_TILE
    last = t // tm - 1
    rows = [(partial.shape[1] - shard.shape[0], shard.shape[0])]

    def body(c_ref, w_ref, x_ref, g_ref, gn_ref, shard_ref, partial_ref, mix_ref, x1_ref, h2_ref, gathered_ref,
             send_sems, recv_sems, local_sems, bounce):
        i = pl.program_id(0)
        start, finish = _gather_plan([shard_ref], [gathered_ref], send_sems, recv_sems, local_sems, [bounce], rows)
        pl.when(i == 0)(start)
        mix = _dot(c_ref[...], w_ref[...])
        mix_ref[...] = mix
        x1 = x_ref[...] + (mix * _rstd(mix)) * g_ref[...]
        x1_ref[...] = x1
        h2_ref[...] = ((x1 * _rstd(x1)) * gn_ref[...]).astype(BF16)
        pl.when(i == last)(finish)

    row = pl.BlockSpec((tm, d), lambda i: (i, 0))
    gain = pl.BlockSpec((1, d), lambda i: (0, 0))
    return pl.pallas_call(
        body, name="outproj_norm", grid=(t // tm,),
        out_shape=[jax.ShapeDtypeStruct((t, d), F32), jax.ShapeDtypeStruct((t, d), F32), jax.ShapeDtypeStruct((t, d), BF16),
                   jax.ShapeDtypeStruct(partial.shape, partial.dtype)],
        in_specs=[pl.BlockSpec((tm, cat.shape[1]), lambda i: (i, 0)), pl.BlockSpec(w.shape, lambda i: (0, 0)), row, gain, gain,
                  ANY, ANY],
        out_specs=[row, row, row, ANY],
        input_output_aliases={6: 3},
        scratch_shapes=_gather_scratch([shard]),
        compiler_params=_params(),
    )(cat, w, x, g, g_next, shard, partial)


def ffn_up(h, gate_t, up_t, down_shard):
    t, d = h.shape
    n = gate_t.shape[1]
    f = N_DEV * n
    tm, ts = FFN_TOKEN_TILE, FF_SHARDS_PER_TILE
    tn = ts * n
    steps = (f // tn, t // tm)

    def body(h_ref, wg_ref, wu_ref, shard_ref, gate_ref, up_ref, a_ref, gathered_ref,
             send_sems, recv_sems, local_sems, bounce):
        j, i = pl.program_id(0), pl.program_id(1)
        start, finish = _gather_plan([shard_ref], [gathered_ref], send_sems, recv_sems, local_sems, [bounce])
        pl.when((i == 0) & (j == 0))(start)

        hv = h_ref[...]
        gate = _dot_nt(hv, _merge_rows(wg_ref[...]))
        up = _dot_nt(hv, _merge_rows(wu_ref[...]))
        gate_ref[...] = gate.astype(BF16)
        up_ref[...] = up.astype(BF16)
        a_ref[...] = (gate * (1.0 / (1.0 + jnp.exp(-gate))) * up).astype(BF16)

        pl.when((j == steps[0] - 1) & (i == steps[1] - 1))(finish)

    wide = pl.BlockSpec((tm, tn), lambda j, i: (i, j))
    return pl.pallas_call(
        body, name="ffn_up", grid=steps,
        out_shape=[jax.ShapeDtypeStruct((t, f), BF16)] * 3
        + [jax.ShapeDtypeStruct((N_DEV, *down_shard.shape), down_shard.dtype)],
        in_specs=[pl.BlockSpec((tm, d), lambda j, i: (i, 0)),
                  pl.BlockSpec((ts, n, d), lambda j, i: (j, 0, 0)),
                  pl.BlockSpec((ts, n, d), lambda j, i: (j, 0, 0)), ANY],
        out_specs=[wide, wide, wide, ANY],
        scratch_shapes=_gather_scratch([down_shard]),
        compiler_params=_params(),
    )(h, gate_t, up_t, down_shard)


def ffn_down_loss(a, w_down, x1, g, target):
    t, d = x1.shape
    tm = TOKEN_TILE

    def body(a_ref, w_ref, x_ref, g_ref, t_ref, df_ref, dy_ref, dg_ref, loss_ref):
        @pl.when(pl.program_id(0) == 0)
        def _():
            dg_ref[...] = jnp.zeros_like(dg_ref)
            loss_ref[...] = jnp.zeros_like(loss_ref)

        f = _dot(a_ref[...], _merge_rows(w_ref[...]))
        r = _rstd(f)
        g = g_ref[...]
        err = x_ref[...] + (f * r) * g - t_ref[...]
        loss_ref[...] += 0.5 * jnp.sum(jnp.mean(err * err, axis=-1, keepdims=True))
        dy = err * (1.0 / d)
        dy_ref[...] = dy
        df, dg_rows = _norm_bwd(dy, f, r, g)
        df_ref[...] = df.astype(BF16)
        dg_ref[...] += _as_rows(jnp.sum(dg_rows, axis=0, keepdims=True))

    row = pl.BlockSpec((tm, d), lambda i: (i, 0))
    gain = pl.BlockSpec((1, d), lambda i: (0, 0))
    return pl.pallas_call(
        body, name="ffn_down_loss", grid=(t // tm,),
        out_shape=[jax.ShapeDtypeStruct((t, d), BF16), jax.ShapeDtypeStruct((t, d), F32),
                   jax.ShapeDtypeStruct((d // 128, 128), F32), jax.ShapeDtypeStruct((1, 128), F32)],
        in_specs=[pl.BlockSpec((tm, a.shape[1]), lambda i: (i, 0)), pl.BlockSpec(w_down.shape, lambda i: (0, 0, 0)), row, gain, row],
        out_specs=[row, row, pl.BlockSpec((d // 128, 128), lambda i: (0, 0)), pl.BlockSpec((1, 128), lambda i: (0, 0))],
        compiler_params=_params(),
    )(a, w_down, x1, g, target)


def ffn_down_bwd(df, w_down, gate, up):
    t, d = df.shape
    n = w_down.shape[1]
    f = gate.shape[1]
    tm, ts = FFN_TOKEN_TILE, FF_SHARDS_PER_TILE
    tn = ts * n

    def body(df_ref, w_ref, gate_ref, up_ref, dgate_ref, dup_ref):
        da = _dot_nt(df_ref[...], _merge_rows(w_ref[...]))
        gate = gate_ref[...].astype(F32)
        sig = 1.0 / (1.0 + jnp.exp(-gate))
        dgate_ref[...] = (da * up_ref[...].astype(F32) * (sig * (1.0 + gate * (1.0 - sig)))).astype(BF16)
        dup_ref[...] = (da * (gate * sig)).astype(BF16)

    wide = pl.BlockSpec((tm, tn), lambda j, i: (i, j))
    return pl.pallas_call(
        body, name="ffn_down_bwd", grid=(f // tn, t // tm),
        out_shape=[jax.ShapeDtypeStruct((t, f), BF16)] * 2,
        in_specs=[pl.BlockSpec((tm, d), lambda j, i: (i, 0)), pl.BlockSpec((ts, n, d), lambda j, i: (j, 0, 0)), wide, wide],
        out_specs=[wide, wide],
        compiler_params=_params(),
    )(df, w_down, gate, up)


def grad_rows(a, b, name, by_core=False):
    t, m = a.shape
    d = b.shape[1]
    r = m // N_DEV
    tt = TOKEN_TILE
    last = t // tt - 1
    out_shape = (2, N_CHIP, r, d) if by_core else (N_DEV, r, d)

    def body(a_ref, b_ref, out_ref, acc):
        k = pl.program_id(0)

        @pl.when(k == 0)
        def _():
            acc[...] = jnp.zeros_like(acc)

        acc[...] += _dot_tn(a_ref[...], b_ref[...])

        @pl.when(k == last)
        def _():
            if by_core:
                blocks = acc[...].reshape(N_CHIP, 2, r, d)
                for chip in range(N_CHIP):
                    for core in range(2):
                        out_ref[core, chip] = blocks[chip, core].astype(BF16)
            else:
                out_ref[...] = acc[...].reshape(out_shape).astype(BF16)

    return pl.pallas_call(
        body, name=name, grid=(t // tt,),
        out_shape=jax.ShapeDtypeStruct(out_shape, BF16),
        in_specs=[pl.BlockSpec((tt, m), lambda k: (k, 0)), pl.BlockSpec((tt, d), lambda k: (k, 0))],
        out_specs=pl.BlockSpec(out_shape, lambda k: (0,) * len(out_shape)),
        scratch_shapes=[pltpu.VMEM((m, d), F32)],
        compiler_params=_params(),
    )(a, b)


def grad_ffn(lhs, b, name, pair_parts=()):
    t, f = lhs[0].shape
    d = b.shape[1]
    nw = len(lhs)
    na = len(pair_parts)
    n = f // N_DEV
    tt, ts = TOKEN_TILE, FF_SHARDS_PER_TILE
    tn = ts * n
    steps = (f // tn, t // tt)

    def body(*refs):
        a_refs, b_ref, part_refs = refs[:nw], refs[nw], refs[nw + 1:nw + 1 + na]
        out_refs = refs[nw + 1 + na:2 * nw + 1 + na]
        got_refs = refs[2 * nw + 1 + na:2 * nw + 1 + 2 * na]
        acc = refs[2 * nw + 1 + 2 * na]
        i, k = pl.program_id(0), pl.program_id(1)
        if na:
            start, finish = _pair_plan(part_refs, got_refs, *refs[2 * nw + 2 + 2 * na:])
            pl.when((i == 0) & (k == 0))(start)

        @pl.when(k == 0)
        def _():
            acc[...] = jnp.zeros_like(acc)

        for w in range(nw):
            acc[w] += _dot_tn(a_refs[w][...], b_ref[...])

        @pl.when(k == steps[1] - 1)
        def _():
            for w in range(nw):
                blocks = acc[w].reshape(ts // 2, 2, n, d)
                for chip in range(ts // 2):
                    for core in range(2):
                        out_refs[w][core, chip] = blocks[chip, core].astype(BF16)

        if na:
            pl.when((i == steps[0] - 1) & (k == steps[1] - 1))(finish)

    out = pl.pallas_call(
        body, name=name, grid=steps,
        out_shape=[jax.ShapeDtypeStruct((2, N_CHIP, n, d), BF16)] * nw
        + [jax.ShapeDtypeStruct(p.shape[1:], p.dtype) for p in pair_parts],
        in_specs=[pl.BlockSpec((tt, tn), lambda i, k: (k, i))] * nw + [pl.BlockSpec((tt, d), lambda i, k: (k, 0))] + [ANY] * na,
        out_specs=[pl.BlockSpec((2, ts // 2, n, d), lambda i, k: (0, i, 0, 0))] * nw + [ANY] * na,
        scratch_shapes=[pltpu.VMEM((nw, tn, d), F32)]
        + ([pltpu.SemaphoreType.DMA((na,)), pltpu.SemaphoreType.DMA((na,))] if na else []),
        compiler_params=_params(),
    )(*lhs, b, *pair_parts)
    return out[:nw], out[nw:]


def ffn_up_bwd(dgate, dup, gate_t, up_t, x1, g_ffn, dy, mix, g_mix, chip_parts):
    t, d = x1.shape
    n = gate_t.shape[1]
    f = N_DEV * n
    tm = TOKEN_TILE
    na = len(chip_parts)
    last = t // tm - 1

    def body(*refs):
        dg_ref, du_ref, wg_ref, wu_ref, x_ref, gf_ref, dy_ref, mix_ref, gm_ref = refs[:9]
        part_refs = refs[9:9 + na]
        dx1_ref, dmix_ref, dgf_ref, dgm_ref = refs[9 + na:13 + na]
        slot_refs = refs[13 + na:13 + 2 * na]
        send_sems, recv_sems, local_sems = refs[13 + 2 * na:16 + 2 * na]
        i = pl.program_id(0)
        start, finish = _chip_exchange_plan(part_refs, slot_refs, send_sems, recv_sems, local_sems, refs[16 + 2 * na:])

        @pl.when(i == 0)
        def _():
            start()
            dgf_ref[...] = jnp.zeros_like(dgf_ref)
            dgm_ref[...] = jnp.zeros_like(dgm_ref)

        dh = _dot(dg_ref[...], _merge_rows(wg_ref[...])) + _dot(du_ref[...], _merge_rows(wu_ref[...]))
        x1 = x_ref[...]
        dx, dgf_rows = _norm_bwd(dh, x1, _rstd(x1), gf_ref[...])
        dx1 = dy_ref[...] + dx
        dx1_ref[...] = dx1
        dgf_ref[...] += _as_rows(jnp.sum(dgf_rows, axis=0, keepdims=True))
        mix = mix_ref[...]
        dmix, dgm_rows = _norm_bwd(dx1, mix, _rstd(mix), gm_ref[...])
        dmix_ref[...] = dmix.astype(BF16)
        dgm_ref[...] += _as_rows(jnp.sum(dgm_rows, axis=0, keepdims=True))
        pl.when(i == last)(finish)

    row = pl.BlockSpec((tm, d), lambda i: (i, 0))
    wide = pl.BlockSpec((tm, f), lambda i: (i, 0))
    gain = pl.BlockSpec((1, d), lambda i: (0, 0))
    gain_rows = pl.BlockSpec((d // 128, 128), lambda i: (0, 0))
    whole = pl.BlockSpec((N_DEV, n, d), lambda i: (0, 0, 0), pipeline_mode=pl.Buffered(1))
    out = pl.pallas_call(
        body, name="ffn_up_bwd", grid=(t // tm,),
        out_shape=[jax.ShapeDtypeStruct((t, d), F32), jax.ShapeDtypeStruct((t, d), BF16),
                   jax.ShapeDtypeStruct((d // 128, 128), F32), jax.ShapeDtypeStruct((d // 128, 128), F32)]
        + [jax.ShapeDtypeStruct(p.shape, p.dtype) for p in chip_parts],
        in_specs=[wide, wide, whole, whole, row, gain, row, row, gain] + [ANY] * na,
        out_specs=[row, row, gain_rows, gain_rows] + [ANY] * na,
        scratch_shapes=_chip_exchange_scratch(chip_parts),
        compiler_params=_params(),
    )(dgate, dup, gate_t, up_t, x1, g_ffn, dy, mix, g_mix, *chip_parts)
    return out[:4], out[4:]


def outproj_bwd(dmix, w_out):
    t, d = dmix.shape
    tm = TOKEN_TILE

    def body(dm_ref, w_ref, out_ref):
        out_ref[...] = _dot_nt(dm_ref[...], w_ref[...])

    return pl.pallas_call(
        body, name="outproj_bwd", grid=(t // tm,),
        out_shape=jax.ShapeDtypeStruct((t, w_out.shape[0]), F32),
        in_specs=[pl.BlockSpec((tm, d), lambda i: (i, 0)), pl.BlockSpec(w_out.shape, lambda i: (0, 0))],
        out_specs=pl.BlockSpec((tm, w_out.shape[0]), lambda i: (i, 0)),
        compiler_params=_params(),
    )(dmix, w_out)


def mixers_bwd(proj, dcat, biasm, sinks, w_pool, pool_scale, ffn_parts):
    t = proj.shape[0]
    nb = t // BLOCK
    na = len(ffn_parts)

    def body(*refs):
        (uc_ref, halo_ref, q_ref, kvc_ref, kvp_ref, dcat_ref, biasm_ref, sinks_ref, wp_ref, sc_ref) = refs[:10]
        part_refs = refs[10:10 + na]
        dproj_ref, dbias_ref, dsink_ref, dwp_ref, dsc_ref = refs[10 + na:15 + na]
        slot_refs = refs[15 + na:15 + 2 * na]
        ubuf, dbuf, c_u, c_q, c_kv, s_all, dp_all, ds_all, p_all = refs[15 + 2 * na:24 + 2 * na]
        send_sems, recv_sems, local_sems = refs[24 + 2 * na:27 + 2 * na]
        bounce = refs[27 + 2 * na:]
        i = pl.program_id(0)
        lane = lax.broadcasted_iota(jnp.int32, (1, 128), 1)
        start, finish = _chip_exchange_plan(part_refs, slot_refs, send_sems, recv_sems, local_sems, bounce)

        @pl.when(i == 0)
        def _():
            start()
            dbias_ref[...] = jnp.zeros_like(dbias_ref)
            dwp_ref[...] = jnp.zeros_like(dwp_ref)
            dsc_ref[...] = jnp.zeros_like(dsc_ref)
            dsink_ref[...] = jnp.zeros_like(dsink_ref)
            dbuf[...] = jnp.zeros_like(dbuf)
            c_u[...] = jnp.zeros_like(c_u)
            c_q[...] = jnp.zeros_like(c_q)
            c_kv[...] = jnp.zeros_like(c_kv)

        @pl.when(i < nb)
        def _():
            _fill_pool_input(i, ubuf, uc_ref, halo_ref)
            for g, w in enumerate(POOL_WINDOWS):
                cols = slice(g * POOL_GROUP_DIM, (g + 1) * POOL_GROUP_DIM)
                pooled = _pooled(i, g, w, ubuf).astype(BF16)
                mixed = _dot(pooled, wp_ref[g])
                dout = dcat_ref[:, cols]
                dsc_ref[g:g + 1, :] += jnp.sum(dout * mixed, axis=0, keepdims=True)
                dmixed = (dout * sc_ref[:, cols]).astype(BF16)
                dwp_ref[g] += _dot_tn(pooled, dmixed)
                dpooled = _dot_nt(dmixed, wp_ref[g])
                scaled = dpooled * _inv_count(i, w)
                dbuf[BLOCK:, cols] = scaled[0:HALO]
                dproj_ref[:, cols] = (_window_sum(dbuf, g, w, lambda k: k) + c_u[:, cols]).astype(BF16)
                dbuf[0:BLOCK, cols] = scaled
                c_u[:, cols] = -dpooled

            kv = jnp.concatenate([kvp_ref[...], kvc_ref[...]], axis=0)
            k_var = _head_variants(kv[:, 0:2 * HEAD_DIM])
            v_var = _head_variants(kv[:, 2 * HEAD_DIM:])
            q2s = [q_ref[:, 2 * HEAD_DIM * j:2 * HEAD_DIM * (j + 1)].astype(BF16) for j in range(N_Q_HEADS // 2)]
            do2s = [dcat_ref[:, POOL_WIDTH + 2 * HEAD_DIM * j:POOL_WIDTH + 2 * HEAD_DIM * (j + 1)].astype(BF16)
                    for j in range(N_Q_HEADS // 2)]
            slot = lambda hq: 4 * (hq // GQA_GROUP) + 2 * (hq % 2) + (hq % GQA_GROUP) // 2
            for hq in range(N_Q_HEADS):
                j, half, h = hq // 2, hq % 2, hq // GQA_GROUP
                s_all[hq] = _dot_nt(q2s[j], k_var[h][half])
                dp_all[hq] = _dot_nt(do2s[j], v_var[h][half])
            dsink_row = jnp.zeros((1, 128), F32)
            for hq in range(N_Q_HEADS):
                dsink = 0.0
                for r in range(0, BLOCK, ROW_CHUNK):
                    rows = slice(r, r + ROW_CHUNK)
                    probs, p_sink = _head_probs(i, hq, rows, s_all, biasm_ref, sinks_ref)
                    dp = dp_all[hq, rows, :]
                    delta = jnp.sum(probs * dp, axis=-1, keepdims=True)
                    ds = probs * (dp - delta)
                    dbias_ref[hq, rows, :] += ds
                    dsink = dsink + jnp.sum(p_sink * delta)
                    ds_all[slot(hq), rows, :] = (ds * ATTN_SCALE).astype(BF16)
                    p_all[slot(hq), rows, :] = probs.astype(BF16)
                dsink_row = dsink_row - jnp.where(lane == hq, dsink, 0.0)
            dsink_ref[...] += dsink_row
            dq2 = [None] * (N_Q_HEADS // 2)
            for hq in range(N_Q_HEADS):
                j, half, h = hq // 2, hq % 2, hq // GQA_GROUP
                dq = _dot(ds_all[slot(hq)], k_var[h][half])
                dq2[j] = dq if dq2[j] is None else dq2[j] + dq
            low = lax.broadcasted_iota(jnp.int32, (2 * BLOCK, 2 * HEAD_DIM), 1) < HEAD_DIM
            dk_half, dv_half = [[None, None], [None, None]], [[None, None], [None, None]]
            for h in range(N_KV_HEADS):
                for half in range(2):
                    heads = [hq for hq in range(GQA_GROUP * h, GQA_GROUP * (h + 1)) if hq % 2 == half]
                    base = slot(heads[0])
                    q_rows = jnp.concatenate([q2s[hq // 2] for hq in heads], axis=0)
                    do_rows = jnp.concatenate([do2s[hq // 2] for hq in heads], axis=0)
                    dk_half[h][half] = _dot_tn(_merge_rows(ds_all[base:base + 2]), q_rows)
                    dv_half[h][half] = _dot_tn(_merge_rows(p_all[base:base + 2]), do_rows)

            def pair_of(halves):
                return jnp.where(low, halves[0][0] + pltpu.roll(halves[0][1], HEAD_DIM, 1),
                                 halves[1][1] + pltpu.roll(halves[1][0], HEAD_DIM, 1))

            dkv = jnp.concatenate([pair_of(dk_half), pair_of(dv_half)], axis=1)
            dproj_ref[:, POOL_WIDTH:2 * POOL_WIDTH] = c_q[...].astype(BF16)
            dproj_ref[:, 2 * POOL_WIDTH:] = (c_kv[...] + dkv[0:BLOCK]).astype(BF16)
            c_q[...] = jnp.concatenate(dq2, axis=1)
            c_kv[...] = dkv[BLOCK:]

        @pl.when(i == nb)
        def _():
            dbuf[BLOCK:, :] = jnp.zeros((HALO, POOL_WIDTH), F32)
            for g, w in enumerate(POOL_WINDOWS):
                cols = slice(g * POOL_GROUP_DIM, (g + 1) * POOL_GROUP_DIM)
                dproj_ref[:, cols] = (_window_sum(dbuf, g, w, lambda k: k) + c_u[:, cols]).astype(BF16)
            dproj_ref[:, POOL_WIDTH:2 * POOL_WIDTH] = c_q[...].astype(BF16)
            dproj_ref[:, 2 * POOL_WIDTH:] = c_kv[...].astype(BF16)
            finish()

    cur = lambda i: jnp.minimum(i, nb - 1)
    prv = lambda i: jnp.maximum(jnp.minimum(i, nb - 1) - 1, 0)
    out = pl.pallas_call(
        body, name="mixers_bwd", grid=(nb + 1,),
        out_shape=[jax.ShapeDtypeStruct((t, proj.shape[1]), BF16),
                   jax.ShapeDtypeStruct((N_Q_HEADS, BLOCK, 2 * BLOCK), F32),
                   jax.ShapeDtypeStruct((1, 128), F32),
                   jax.ShapeDtypeStruct((4, POOL_GROUP_DIM, POOL_GROUP_DIM), F32),
                   jax.ShapeDtypeStruct((len(POOL_WINDOWS), POOL_GROUP_DIM), F32)]
        + [jax.ShapeDtypeStruct(p.shape, p.dtype) for p in ffn_parts],
        in_specs=_mixer_in_specs(cur, prv) + [pl.BlockSpec((BLOCK, 2 * POOL_WIDTH), lambda i: (cur(i), 0))]
        + _mixer_param_specs() + [ANY] * na,
        out_specs=[pl.BlockSpec((BLOCK, proj.shape[1]), lambda i: (jnp.maximum(i - 1, 0), 0)),
                   pl.BlockSpec((N_Q_HEADS, BLOCK, 2 * BLOCK), lambda i: (0, 0, 0)),
                   pl.BlockSpec((1, 128), lambda i: (0, 0)),
                   pl.BlockSpec((4, POOL_GROUP_DIM, POOL_GROUP_DIM), lambda i: (0, 0, 0)),
                   pl.BlockSpec((len(POOL_WINDOWS), POOL_GROUP_DIM), lambda i: (0, 0))] + [ANY] * na,
        scratch_shapes=[pltpu.VMEM((HALO + BLOCK, POOL_WIDTH), F32), pltpu.VMEM((BLOCK + HALO, POOL_WIDTH), F32),
                        pltpu.VMEM((BLOCK, POOL_WIDTH), F32), pltpu.VMEM((BLOCK, POOL_WIDTH), F32),
                        pltpu.VMEM((BLOCK, 256), F32),
                        pltpu.VMEM((N_Q_HEADS, BLOCK, 2 * BLOCK), F32), pltpu.VMEM((N_Q_HEADS, BLOCK, 2 * BLOCK), F32),
                        pltpu.VMEM((N_Q_HEADS, BLOCK, 2 * BLOCK), BF16), pltpu.VMEM((N_Q_HEADS, BLOCK, 2 * BLOCK), BF16)]
        + _chip_exchange_scratch(ffn_parts),
        compiler_params=_params(),
    )(proj, proj, proj, proj, proj, dcat, biasm, sinks, w_pool, pool_scale, *ffn_parts)
    return out[:5], out[5:]


def inproj_bwd(dproj, w_in_t, x, g, dx1):
    t, d = x.shape
    n = dproj.shape[1]
    tm = TOKEN_TILE

    def body(dp_ref, w_ref, x_ref, g_ref, dx1_ref, dx_ref, dg_ref):
        @pl.when(pl.program_id(0) == 0)
        def _():
            dg_ref[...] = jnp.zeros_like(dg_ref)

        dh = _dot(dp_ref[...], w_ref[...])
        xv = x_ref[...]
        dx, dg_rows = _norm_bwd(dh, xv, _rstd(xv), g_ref[...])
        dx_ref[...] = dx1_ref[...] + dx
        dg_ref[...] += _as_rows(jnp.sum(dg_rows, axis=0, keepdims=True))

    row = pl.BlockSpec((tm, d), lambda i: (i, 0))
    gain = pl.BlockSpec((1, d), lambda i: (0, 0))
    return pl.pallas_call(
        body, name="inproj_bwd", grid=(t // tm,),
        out_shape=[jax.ShapeDtypeStruct((t, d), F32), jax.ShapeDtypeStruct((d // 128, 128), F32)],
        in_specs=[pl.BlockSpec((tm, n), lambda i: (i, 0)), pl.BlockSpec(w_in_t.shape, lambda i: (0, 0)), row, gain, row],
        out_specs=[row, pl.BlockSpec((d // 128, 128), lambda i: (0, 0))],
        compiler_params=_params(),
    )(dproj, w_in_t, x, g, dx1)


def _bucket_band():
    qi = jnp.arange(BLOCK)[:, None]
    kj = jnp.arange(2 * BLOCK)[None, :]
    dist = qi + BLOCK - kj
    n = jnp.maximum(dist, 0)
    nf = jnp.maximum(n, 1).astype(F32)
    large = MAX_EXACT + (jnp.log(nf / MAX_EXACT) / np.float32(np.log(MAX_DISTANCE / MAX_EXACT))
                         * (N_BUCKETS - MAX_EXACT)).astype(jnp.int32)
    large = jnp.minimum(large, N_BUCKETS - 1)
    bucket = jnp.where(n < MAX_EXACT, n, large)
    in_window = (dist >= 0) & (dist < BLOCK)
    return bucket.astype(F32), in_window.astype(F32)


def kernel(x, g_pre_mix, w_in, w_pool, pool_scale, rel_bias, sinks, w_out, g_post_mix, g_pre_ffn, w_gate, w_up, w_down, g_post_ffn, loss_target, m_g_pre_mix, m_w_in, m_w_pool, m_pool_scale, m_rel_bias, m_sinks, m_w_out, m_g_post_mix, m_g_pre_ffn, m_w_gate, m_w_up, m_w_down, m_g_post_ffn, v_g_pre_mix, v_w_in, v_w_pool, v_pool_scale, v_rel_bias, v_sinks, v_w_out, v_g_post_mix, v_g_pre_ffn, v_w_gate, v_w_up, v_w_down, v_g_post_ffn):
    d = x.shape[-1]
    xs, target = x[0], loss_target[0]

    w_in_ts = w_in[0].T.astype(BF16)
    w_out_s = w_out[0].astype(BF16)
    gate_ts = w_gate[0].T.astype(BF16)
    up_ts = w_up[0].T.astype(BF16)
    w_down_s = w_down[0].astype(BF16)
    w_in_t, = gather_blocks([w_in_ts], "gather_w_in")
    w_in_t = w_in_t.reshape(-1, d)

    bucket, in_window = _bucket_band()
    biasm = bias_band(bucket, in_window, rel_bias)
    w_pool_b = w_pool[0].astype(BF16)
    half = up_ts.shape[0] // 2
    proj, h1, up_t = norm_inproj(xs, g_pre_mix, w_in_t, up_ts[:half], up_ts.shape[0])
    cat, gate_t, w_out_f = mixers_fwd(proj, biasm, sinks, w_pool_b, pool_scale, [gate_ts, w_out_s])
    w_out_f = w_out_f.reshape(-1, d)
    mix, x1, h2, up_t = outproj_norm(cat, w_out_f, xs, g_post_mix, g_pre_ffn, up_ts[half:], up_t)
    gate, up, act, w_down_f = ffn_up(h2, gate_t, up_t, w_down_s)
    df, dy, dg_post_ffn, loss_part = ffn_down_loss(act, w_down_f, x1, g_post_ffn, target)

    def pair_sum(parts, tag):
        return pair_add(parts, pair_exchange(parts, "pair_exchange_" + tag), "pair_add_" + tag)

    dgate, dup = ffn_down_bwd(df, w_down_f, gate, up)
    (d_gate, d_up), _ = grad_ffn([dgate, dup], h2, "grad_w_gate_up")
    (d_down,), got_gate_up = grad_ffn([act], df, "grad_w_down", [d_gate, d_up])
    q_gate, q_up, q_down = pair_add(
        [d_gate, d_up, d_down], [*got_gate_up, *pair_exchange([d_down], "pair_exchange_down")], "pair_add_ffn")
    (dx1, dmix, dg_pre_ffn, dg_post_mix), gate_down_slots = ffn_up_bwd(
        dgate, dup, gate_t, up_t, x1, g_pre_ffn, dy, mix, g_post_mix, [q_gate, q_down])
    dcat = outproj_bwd(dmix, w_out_f)
    d_out = grad_rows(cat, dmix, "grad_w_out", by_core=True)
    q_out, = pair_sum([d_out], "out")
    (dproj, dbias, dsinks, dw_pool, dpool_scale), up_out_slots = mixers_bwd(
        proj, dcat, biasm, sinks, w_pool_b, pool_scale, [q_up, q_out])
    drel_bias = bias_band_bwd(bucket, dbias)
    grad_x, dg_pre_mix = inproj_bwd(dproj, w_in_t, xs, g_pre_mix, dx1)

    small_w = [g_pre_mix, g_post_mix, g_pre_ffn, g_post_ffn, pool_scale, sinks, w_pool, rel_bias.T]
    small_m = [m_g_pre_mix, m_g_post_mix, m_g_pre_ffn, m_g_post_ffn, m_pool_scale, m_sinks, m_w_pool, m_rel_bias.T]
    small_v = [v_g_pre_mix, v_g_post_mix, v_g_pre_ffn, v_g_post_ffn, v_pool_scale, v_sinks, v_w_pool, v_rel_bias.T]
    d_in_t, total, total_rb = grad_w_in_small_reduce(
        dproj, h1, [dg_pre_mix, dg_post_mix, dg_pre_ffn, dg_post_ffn], dpool_scale, dsinks, loss_part, dw_pool, drel_bias)
    g_in_t = reduce_w_in(d_in_t)
    loss_row, sm = small_adamw(total, total_rb, small_w, small_m, small_v)
    sm[7] = [r.T for r in sm[7]]
    g_gate_t, g_down, g_up_t, g_out = sum_slots([*gate_down_slots, *up_out_slots], "sum_slots")
    big_w = [w_in[0].T, w_out[0], w_gate[0].T, w_up[0].T, w_down[0]]
    big_g = [g_in_t, g_out, g_gate_t, g_up_t, g_down]
    big_m = [m_w_in[0].T, m_w_out[0], m_w_gate[0].T, m_w_up[0].T, m_w_down[0]]
    big_v = [v_w_in[0].T, v_w_out[0], v_w_gate[0].T, v_w_up[0].T, v_w_down[0]]
    upd = adamw_update(big_w[:2], big_g[:2], big_m[:2], big_v[:2], "adamw_mix") \
        + adamw_update(big_w[2:], big_g[2:], big_m[2:], big_v[2:], "adamw_ffn")
    back = lambda k, a: (a.T if k in (0, 2, 3) else a)[None]
    big = [[back(k, big_g[k]), *(back(k, u) for u in upd[k])] for k in range(5)]

    def ordered(kind):
        s, b = [p[kind] for p in sm], [p[kind] for p in big]
        return [s[0], b[0], s[6], s[4], s[7], s[5], b[1], s[1], s[2], b[2], b[3], b[4], s[3]]

    return (loss_row[0, 0], grad_x[None], *ordered(0), *ordered(1), *ordered(2), *ordered(3))
```

```python
import numpy as np
import jax
import jax.numpy as jnp
from jax import lax
from jax.experimental import pallas as pl
from jax.experimental.pallas import tpu as pltpu

F32 = jnp.float32
BF16 = jnp.bfloat16

N_DEV = 8
N_CHIP = 4
POOL_WIDTH = 512
POOL_WINDOWS = (2, 4, 8, 16)
POOL_GROUP_DIM = 128
HEAD_DIM = 64
N_Q_HEADS = 8
N_KV_HEADS = 2
GQA_GROUP = 4
BLOCK = 128
HALO = 16
ROW_CHUNK = 32
N_BUCKETS = 32
MAX_EXACT = 16
MAX_DISTANCE = 128
EPS = 1e-6
NEG_INF = -1e30
ATTN_SCALE = float(1.0 / np.sqrt(np.float32(HEAD_DIM)))

ADAM_LR = 0.001
ADAM_B1 = 0.9
ADAM_B2 = 0.999
ADAM_EPS = 1e-08
ADAM_WD = 0.01
ADAM_STEP = 10

TOKEN_TILE = 512
FFN_TOKEN_TILE = 1024
FF_SHARDS_PER_TILE = 4
VMEM_LIMIT = 56 * 1024 * 1024
MESH = pl.DeviceIdType.MESH
ANY = pl.BlockSpec(memory_space=pl.ANY)
VMEM = pl.BlockSpec(memory_space=pltpu.VMEM)
SMEM = pl.BlockSpec(memory_space=pltpu.SMEM)


def _params(**kw):
    return pltpu.CompilerParams(vmem_limit_bytes=VMEM_LIMIT, **kw)


def _dot(a, b):
    return jnp.dot(a, b, preferred_element_type=F32)


def _dot_nt(a, b):
    return lax.dot_general(a, b, (((1,), (1,)), ((), ())), preferred_element_type=F32)


def _dot_tn(a, b):
    return lax.dot_general(a, b, (((0,), (0,)), ((), ())), preferred_element_type=F32)


def _rstd(v):
    return lax.rsqrt(jnp.mean(v * v, axis=-1, keepdims=True) + EPS)


def _norm_bwd(dout, v, r, g):
    vn = v * r
    dn = dout * g
    dv = r * (dn - vn * jnp.mean(dn * vn, axis=-1, keepdims=True))
    return dv, dout * vn


def _as_rows(v):
    return jnp.concatenate([v[:, k:k + 128] for k in range(0, v.shape[1], 128)], axis=0)


def _as_lanes(rows):
    return jnp.concatenate([rows[k:k + 1, :] for k in range(rows.shape[0])], axis=1)


def _merge_rows(value):
    s, r, c_ = value.shape
    return value.reshape(s * r, c_)


def _gather_plan(srcs, outs, send_sems, recv_sems, local_sems=None, bounce=None, rows=None):
    n = len(srcs)
    x, y, c = lax.axis_index("x"), lax.axis_index("y"), lax.axis_index("c")
    me, sibling = (x, y, c), (x, y, 1 - c)
    chips = [(1 - x, y), (x, 1 - y), (1 - x, 1 - y)]

    def slot(a, px, py, pc):
        whole = outs[a].at[4 * px + 2 * py + pc]
        return whole if rows is None or rows[a] is None else whole.at[pl.ds(*rows[a])]

    def copy(a, k, block, to, from_src=False):
        return pltpu.make_async_remote_copy(
            src_ref=srcs[a] if from_src else slot(a, *block), dst_ref=slot(a, *block),
            send_sem=send_sems.at[k * n + a], recv_sem=recv_sems.at[k * n + a], device_id=to, device_id_type=MESH)

    def own_in(a):
        return pltpu.make_async_copy(srcs[a], bounce[a], local_sems.at[a])

    def own_out(a):
        return pltpu.make_async_copy(bounce[a], slot(a, *me), local_sems.at[a])

    def first(a):
        return [copy(a, 0, me, sibling, True)] + [copy(a, 1 + j, me, (*chip, c), True) for j, chip in enumerate(chips)]

    def passed(a, j):
        return copy(a, 4 + j, (*chips[j], c), sibling)

    def start():
        for a in range(n):
            if bounce is not None:
                own_in(a).start()
            for cp in first(a):
                cp.start()

    def finish():
        if bounce is not None:
            for a in range(n):
                own_in(a).wait()
                own_out(a).start()
        for j, chip in enumerate(chips):
            for a in range(n):
                copy(a, 1 + j, (*chip, c), me).wait_recv()
                passed(a, j).start()
        for a in range(n):
            copy(a, 0, sibling, me).wait_recv()
            for j, chip in enumerate(chips):
                copy(a, 4 + j, (*chip, 1 - c), me).wait_recv()
        for a in range(n):
            for cp in first(a) + [passed(a, j) for j in range(3)]:
                cp.wait_send()
            if bounce is not None:
                own_out(a).wait()

    return start, finish


def _gather_scratch(shards):
    n = len(shards)
    return [pltpu.SemaphoreType.DMA((7 * n,)), pltpu.SemaphoreType.DMA((7 * n,)), pltpu.SemaphoreType.DMA((n,))] \
        + [pltpu.VMEM(s.shape, s.dtype) for s in shards]


def _chip_exchange_plan(srcs, outs, send_sems, recv_sems, local_sems, bounce):
    n = len(srcs)
    x, y, c = lax.axis_index("x"), lax.axis_index("y"), lax.axis_index("c")
    my_chip = 2 * x + y

    def copies():
        out = []
        for a in range(n):
            for k in range(1, N_CHIP):
                px, py = x ^ (k >> 1), y ^ (k & 1)
                out.append(pltpu.make_async_remote_copy(
                    src_ref=srcs[a].at[2 * px + py], dst_ref=outs[a].at[my_chip],
                    send_sem=send_sems.at[(k - 1) * n + a], recv_sem=recv_sems.at[(k - 1) * n + a],
                    device_id=(px, py, c), device_id_type=MESH))
        return out

    def own_in(a):
        return pltpu.make_async_copy(srcs[a].at[my_chip], bounce[a], local_sems.at[a])

    def own_out(a):
        return pltpu.make_async_copy(bounce[a], outs[a].at[my_chip], local_sems.at[a])

    def start():
        for a in range(n):
            own_in(a).start()
        for cp in copies():
            cp.start()

    def finish():
        for a in range(n):
            own_in(a).wait()
            own_out(a).start()
        for cp in copies():
            cp.wait()
        for a in range(n):
            own_out(a).wait()

    return start, finish


def _chip_exchange_scratch(parts):
    n = len(parts)
    return [pltpu.SemaphoreType.DMA((3 * n,)), pltpu.SemaphoreType.DMA((3 * n,)), pltpu.SemaphoreType.DMA((n,))] \
        + [pltpu.VMEM(p.shape[1:], p.dtype) for p in parts]


def _pair_plan(srcs, outs, send_sems, recv_sems):
    x, y, c = lax.axis_index("x"), lax.axis_index("y"), lax.axis_index("c")

    def copies():
        return [pltpu.make_async_remote_copy(
            src_ref=srcs[a].at[1 - c], dst_ref=outs[a], send_sem=send_sems.at[a], recv_sem=recv_sems.at[a],
            device_id=(x, y, 1 - c), device_id_type=MESH) for a in range(len(srcs))]

    def start():
        for cp in copies():
            cp.start()

    def finish():
        for cp in copies():
            cp.wait()

    return start, finish


def pair_exchange(parts, name):
    n = len(parts)

    def body(*refs):
        start, finish = _pair_plan(refs[:n], refs[n:2 * n], *refs[2 * n:])
        start()
        finish()

    return pl.pallas_call(
        body, name=name, out_shape=[jax.ShapeDtypeStruct(p.shape[1:], p.dtype) for p in parts],
        in_specs=[ANY] * n, out_specs=[ANY] * n,
        scratch_shapes=[pltpu.SemaphoreType.DMA((n,)), pltpu.SemaphoreType.DMA((n,))],
    )(*parts)


def pair_add(parts, got, name):
    n = len(parts)

    def body(core_ref, *refs):
        for a in range(n):
            refs[2 * n + a][...] = (refs[a][...].astype(F32) + refs[n + a][...].astype(F32)).astype(BF16)

    def own(p):
        zeros = (0,) * (p.ndim - 2)
        return pl.BlockSpec((None, 1, *p.shape[2:]), lambda i, core: (core[0], i, *zeros))

    def plain(p):
        zeros = (0,) * (p.ndim - 1)
        return pl.BlockSpec((1, *p.shape[1:]), lambda i, core: (i, *zeros))

    core = lax.axis_index("c").astype(jnp.int32).reshape(1)
    return pl.pallas_call(
        body, name=name,
        grid_spec=pltpu.PrefetchScalarGridSpec(
            num_scalar_prefetch=1, grid=(got[0].shape[0],),
            in_specs=[own(p) for p in parts] + [plain(p) for p in got], out_specs=[plain(p) for p in got]),
        out_shape=[jax.ShapeDtypeStruct(p.shape, BF16) for p in got],
        compiler_params=_params(),
    )(core, *parts, *got)


def _adamw(w, g, m, v):
    m2 = ADAM_B1 * m + (1.0 - ADAM_B1) * g
    v2 = ADAM_B2 * v + (1.0 - ADAM_B2) * (g * g)
    m_hat = m2 / (1.0 - ADAM_B1 ** ADAM_STEP)
    v_hat = v2 / (1.0 - ADAM_B2 ** ADAM_STEP)
    delta = -ADAM_LR * (m_hat / (jnp.sqrt(v_hat) + ADAM_EPS) + ADAM_WD * w)
    return delta, m2, v2


def sum_adamw(slots, ws, ms, vs, name):
    n = len(ws)
    halves = 2

    def body(*refs):
        for a in range(n):
            total = refs[a][0].astype(F32)
            for s in range(1, slots[a].shape[0]):
                total = total + refs[a][s].astype(F32)
            delta, m2, v2 = _adamw(refs[n + a][...], total, refs[2 * n + a][...], refs[3 * n + a][...])
            for q, val in enumerate((total, delta, m2, v2)):
                refs[4 * n + 4 * a + q][...] = val

    def rows(w):
        return pl.BlockSpec((w.shape[0] // halves, w.shape[1]), lambda i: (i, 0))

    def slot_rows(p):
        return pl.BlockSpec((p.shape[0], p.shape[1] // halves, p.shape[2]), lambda i: (0, i, 0))

    out = pl.pallas_call(
        body, name=name, grid=(halves,),
        out_shape=[jax.ShapeDtypeStruct(w.shape, F32) for w in ws for _ in range(4)],
        in_specs=[slot_rows(p) for p in slots] + [rows(w) for w in ws] * 3,
        out_specs=[rows(w) for w in ws for _ in range(4)],
        compiler_params=_params(),
    )(*slots, *ws, *ms, *vs)
    return [out[4 * a:4 * a + 4] for a in range(n)]


def adamw_update(ws, gs, ms, vs, name):
    n = len(ws)

    def body(*refs):
        for a in range(n):
            delta, m2, v2 = _adamw(refs[a][...], refs[n + a][...], refs[2 * n + a][...], refs[3 * n + a][...])
            refs[4 * n + 3 * a][...] = delta
            refs[4 * n + 3 * a + 1][...] = m2
            refs[4 * n + 3 * a + 2][...] = v2

    out = pl.pallas_call(
        body, name=name,
        out_shape=[jax.ShapeDtypeStruct(w.shape, F32) for w in ws for _ in range(3)],
        in_specs=[VMEM] * (4 * n), out_specs=[VMEM] * (3 * n),
        compiler_params=_params(),
    )(*ws, *gs, *ms, *vs)
    return [out[3 * a:3 * a + 3] for a in range(n)]


GAIN_ROWS = 8
ROW_POOL_SCALE = 4 * GAIN_ROWS
ROW_SINKS = ROW_POOL_SCALE + 4
ROW_LOSS = ROW_SINKS + 1
ROW_W_POOL = 40
SMALL_ROWS = ROW_W_POOL + 4 * POOL_GROUP_DIM


def grad_w_in_small_reduce(a, b, gains, dpool_scale, dsinks, loss_part, dw_pool, drel_bias):
    t, m = a.shape
    d = b.shape[1]
    r = m // N_DEV
    tt = TOKEN_TILE
    last = t // tt - 1

    def body(a_ref, b_ref, g0, g1, g2, g3, dsc_ref, dsink_ref, loss_ref, dwp_ref, drb_ref, out_ref, total_ref, total_rb_ref,
             acc, stage, gat, gat_rb, g_send, g_recv):
        k = pl.program_id(0)
        x, y, c = lax.axis_index("x"), lax.axis_index("y"), lax.axis_index("c")
        start, finish = _gather_plan([stage, drb_ref], [gat, gat_rb], g_send, g_recv)

        @pl.when(k == 0)
        def _():
            for q, g_ref in enumerate((g0, g1, g2, g3)):
                stage[GAIN_ROWS * q:GAIN_ROWS * (q + 1), :] = g_ref[...]
            stage[ROW_POOL_SCALE:ROW_SINKS, :] = dsc_ref[...]
            stage[ROW_SINKS:ROW_LOSS, :] = dsink_ref[...]
            stage[ROW_LOSS:ROW_LOSS + 1, :] = loss_ref[...]
            stage[ROW_LOSS + 1:ROW_W_POOL, :] = jnp.zeros((ROW_W_POOL - ROW_LOSS - 1, 128), F32)
            stage[ROW_W_POOL:, :] = dwp_ref[...].reshape(4 * POOL_GROUP_DIM, POOL_GROUP_DIM)
            gat[4 * x + 2 * y + c] = stage[...]
            gat_rb[4 * x + 2 * y + c] = drb_ref[...]
            start()
            acc[...] = jnp.zeros_like(acc)

        acc[...] += _dot_tn(a_ref[...], b_ref[...])

        @pl.when(k == last)
        def _():
            blocks = acc[...].reshape(N_CHIP, 2, r, d)
            for chip in range(N_CHIP):
                for core in range(2):
                    out_ref[core, chip] = blocks[chip, core].astype(BF16)
            finish()
            total, total_rb = gat[0], gat_rb[0]
            for s in range(1, N_DEV):
                total, total_rb = total + gat[s], total_rb + gat_rb[s]
            total_ref[...] = total
            total_rb_ref[...] = total_rb

    out_shape = (2, N_CHIP, r, d)
    return pl.pallas_call(
        body, name="grad_w_in", grid=(t // tt,),
        out_shape=[jax.ShapeDtypeStruct(out_shape, BF16), jax.ShapeDtypeStruct((SMALL_ROWS, 128), F32),
                   jax.ShapeDtypeStruct(drel_bias.shape, F32)],
        in_specs=[pl.BlockSpec((tt, m), lambda k: (k, 0)), pl.BlockSpec((tt, d), lambda k: (k, 0))] + [VMEM] * 9,
        out_specs=[pl.BlockSpec(out_shape, lambda k: (0,) * len(out_shape)), VMEM, VMEM],
        scratch_shapes=[pltpu.VMEM((m, d), F32), pltpu.VMEM((SMALL_ROWS, 128), F32),
                        pltpu.VMEM((N_DEV, SMALL_ROWS, 128), F32), pltpu.VMEM((N_DEV, *drel_bias.shape), F32),
                        pltpu.SemaphoreType.DMA((14,)), pltpu.SemaphoreType.DMA((14,))],
        compiler_params=_params(),
    )(a, b, *gains, dpool_scale, dsinks, loss_part, dw_pool, drel_bias)


def reduce_w_in(d_in_t):
    def body(d_in_ref, g_in_ref, pair_got, chip_part, chip_got, p_send, p_recv, x_send, x_recv):
        x, y, c = lax.axis_index("x"), lax.axis_index("y"), lax.axis_index("c")
        my_chip = 2 * x + y
        pair = pltpu.make_async_remote_copy(
            src_ref=d_in_ref.at[1 - c], dst_ref=pair_got, send_sem=p_send, recv_sem=p_recv,
            device_id=(x, y, 1 - c), device_id_type=MESH)
        pair.start()
        pair.wait()
        chip_part[...] = (d_in_ref[c].astype(F32) + pair_got[...].astype(F32)).astype(BF16)
        copies = []
        for k in range(1, N_CHIP):
            px, py = x ^ (k >> 1), y ^ (k & 1)
            copies.append(pltpu.make_async_remote_copy(
                src_ref=chip_part.at[2 * px + py], dst_ref=chip_got.at[my_chip],
                send_sem=x_send.at[k - 1], recv_sem=x_recv.at[k - 1], device_id=(px, py, c), device_id_type=MESH))
        for cp in copies:
            cp.start()
        chip_got[my_chip] = chip_part[my_chip]
        for cp in copies:
            cp.wait()
        g_in = chip_got[0].astype(F32)
        for s in range(1, N_CHIP):
            g_in = g_in + chip_got[s].astype(F32)
        g_in_ref[...] = g_in

    per_core = d_in_t.shape[1:]
    return pl.pallas_call(
        body, name="reduce_w_in",
        out_shape=jax.ShapeDtypeStruct(d_in_t.shape[2:], F32),
        in_specs=[VMEM], out_specs=VMEM,
        scratch_shapes=[pltpu.VMEM(per_core, d_in_t.dtype), pltpu.VMEM(per_core, d_in_t.dtype),
                        pltpu.VMEM(per_core, d_in_t.dtype),
                        pltpu.SemaphoreType.DMA, pltpu.SemaphoreType.DMA,
                        pltpu.SemaphoreType.DMA((3,)), pltpu.SemaphoreType.DMA((3,))],
        compiler_params=_params(),
    )(d_in_t)


def small_adamw(total, total_rb, small_w, small_m, small_v):
    n_small = len(small_w)

    def body(*refs):
        total_ref, rb_ref = refs[:2]
        w_refs, m_refs, v_refs = (refs[2 + k * n_small:2 + (k + 1) * n_small] for k in range(3))
        loss_out = refs[2 + 3 * n_small]
        result = refs[3 + 3 * n_small:]
        total = total_ref[...]
        loss_out[...] = total[ROW_LOSS:ROW_LOSS + 1, :]
        grads = [_as_lanes(total[GAIN_ROWS * k:GAIN_ROWS * (k + 1), :]) for k in range(4)]
        grads.append(_as_lanes(total[ROW_POOL_SCALE:ROW_SINKS, :]))
        grads.append(total[ROW_SINKS:ROW_LOSS, 0:N_Q_HEADS])
        grads.append(total[ROW_W_POOL:, :].reshape(w_refs[6].shape))
        grads.append(rb_ref[...])
        for k in range(n_small):
            delta, m2, v2 = _adamw(w_refs[k][...], grads[k], m_refs[k][...], v_refs[k][...])
            result[4 * k][...] = grads[k]
            result[4 * k + 1][...] = delta
            result[4 * k + 2][...] = m2
            result[4 * k + 3][...] = v2

    out = pl.pallas_call(
        body, name="small_adamw",
        out_shape=[jax.ShapeDtypeStruct((1, 128), F32)] + [jax.ShapeDtypeStruct(w.shape, F32) for w in small_w for _ in range(4)],
        in_specs=[VMEM] * (2 + 3 * n_small), out_specs=[VMEM] * (1 + 4 * n_small),
        compiler_params=_params(),
    )(total, total_rb, *small_w, *small_m, *small_v)
    return out[0], [out[1 + 4 * k:5 + 4 * k] for k in range(n_small)]


def norm_inproj(x, g, w_shard, shard, shard_rows):
    t, d = x.shape
    r = w_shard.shape[0]
    tm = TOKEN_TILE
    nt = t // tm

    def body(x_ref, g_ref, w_shard_ref, shard_ref, proj_ref, h_ref, w_ref, gathered_ref, h_all, w_all, w_sem,
             send_w, recv_w, local_w, bounce_w, send_sems, recv_sems, local_sems, bounce):
        i = pl.program_id(0)
        start_w, finish_w = _gather_plan([w_shard_ref], [w_ref], send_w, recv_w, local_w, [bounce_w])
        start, finish = _gather_plan([shard_ref], [gathered_ref], send_sems, recv_sems, local_sems, [bounce],
                                     [(0, shard.shape[0])])

        @pl.when(i == 0)
        def _():
            start_w()
            start()

        @pl.when(i < nt)
        def _():
            xv = x_ref[...]
            h = ((xv * _rstd(xv)) * g_ref[...]).astype(BF16)
            h_ref[...] = h
            h_all[pl.ds(pl.multiple_of(i * tm, tm), tm), :] = h

        @pl.when(i == nt - 1)
        def _():
            finish_w()
            landed = pltpu.make_async_copy(w_ref, w_all, w_sem)
            landed.start()
            landed.wait()

        @pl.when(i >= nt)
        def _():
            rows = pl.ds(pl.multiple_of((i - nt) * tm, tm), tm)
            proj_ref[...] = _dot_nt(h_all[rows, :], _merge_rows(w_all[...]))

        pl.when(i == 2 * nt - 1)(finish)

    first = lambda i: (jnp.minimum(i, nt - 1), 0)
    return pl.pallas_call(
        body, name="norm_inproj", grid=(2 * nt,),
        out_shape=[jax.ShapeDtypeStruct((t, N_DEV * r), F32), jax.ShapeDtypeStruct((t, d), BF16),
                   jax.ShapeDtypeStruct((N_DEV, r, d), w_shard.dtype),
                   jax.ShapeDtypeStruct((N_DEV, shard_rows, d), shard.dtype)],
        in_specs=[pl.BlockSpec((tm, d), first), pl.BlockSpec((1, d), lambda i: (0, 0)), ANY, ANY],
        out_specs=[pl.BlockSpec((tm, N_DEV * r), lambda i: (jnp.maximum(i - nt, 0), 0)), pl.BlockSpec((tm, d), first),
                   ANY, ANY],
        scratch_shapes=[pltpu.VMEM((t, d), BF16), pltpu.VMEM((N_DEV, r, d), w_shard.dtype), pltpu.SemaphoreType.DMA]
        + _gather_scratch([w_shard]) + _gather_scratch([shard]),
        compiler_params=_params(),
    )(x, g, w_shard, shard)


def bias_band(bucket, in_window, rel_bias):
    def body(bk_ref, win_ref, rb_ref, out_ref):
        bk = bk_ref[...]
        keep = win_ref[...] > 0.5
        for h in range(N_Q_HEADS):
            acc = jnp.zeros(bk.shape, F32)
            for b in range(N_BUCKETS):
                acc = jnp.where(bk == float(b), rb_ref[b, h], acc)
            out_ref[h] = jnp.where(keep, acc, NEG_INF)

    return pl.pallas_call(
        body, name="bias_band",
        out_shape=jax.ShapeDtypeStruct((N_Q_HEADS, BLOCK, 2 * BLOCK), F32),
        in_specs=[VMEM, VMEM, SMEM], out_specs=VMEM,
    )(bucket, in_window, rel_bias)


def bias_band_bwd(bucket, dbias):
    def body(bk_ref, db_ref, out_ref):
        bk = bk_ref[...]
        for h in range(N_Q_HEADS):
            db = db_ref[h]
            for b in range(N_BUCKETS):
                out_ref[h, b] = jnp.sum(jnp.where(bk == float(b), db, 0.0))

    return pl.pallas_call(
        body, name="bias_band_bwd",
        out_shape=jax.ShapeDtypeStruct((N_Q_HEADS, N_BUCKETS), F32),
        in_specs=[VMEM, VMEM], out_specs=SMEM,
    )(bucket, dbias)


def _window_sum(buf_ref, g, w, first):
    cols = slice(g * POOL_GROUP_DIM, (g + 1) * POOL_GROUP_DIM)
    acc = None
    for k in range(w):
        piece = buf_ref[first(k):first(k) + BLOCK, cols]
        acc = piece if acc is None else acc + piece
    return acc


def _inv_count(i, w):
    row = lax.broadcasted_iota(jnp.int32, (BLOCK, 1), 0)
    return 1.0 / jnp.minimum(i * BLOCK + row + 1, w).astype(F32)


def _fill_pool_input(i, ubuf, uc_ref, halo_ref):
    ubuf[0:HALO, :] = jnp.where(i > 0, halo_ref[...], 0.0)
    ubuf[HALO:, :] = uc_ref[...]


def _pooled(i, g, w, ubuf):
    cols = slice(g * POOL_GROUP_DIM, (g + 1) * POOL_GROUP_DIM)
    return _window_sum(ubuf, g, w, lambda k: HALO - k) * _inv_count(i, w) - ubuf[HALO:, cols]


def _head_variants(pair):
    low = lax.broadcasted_iota(jnp.int32, pair.shape, 1) < HEAD_DIM
    swapped = pltpu.roll(pair, HEAD_DIM, 1)
    zero = jnp.zeros_like(pair)
    pick = lambda c, a, b: jnp.where(c, a, b).astype(BF16)
    return [[pick(low, pair, zero), pick(low, zero, swapped)], [pick(low, swapped, zero), pick(low, zero, pair)]]


def _head_probs(i, hq, rows, s_ref, biasm_ref, sinks_ref):
    s = s_ref[hq, rows, :] * ATTN_SCALE + biasm_ref[hq, rows, :]
    col = lax.broadcasted_iota(jnp.int32, s.shape, 1)
    s = jnp.where((i == 0) & (col < BLOCK), NEG_INF, s)
    sink = sinks_ref[0, hq]
    m = jnp.maximum(jnp.max(s, axis=-1, keepdims=True), sink)
    p = jnp.exp(s - m)
    e_sink = jnp.exp(sink - m)
    inv = 1.0 / (jnp.sum(p, axis=-1, keepdims=True) + e_sink)
    return p * inv, e_sink * inv


def _mixer_in_specs(cur, prv):
    return [pl.BlockSpec((BLOCK, 512), lambda i: (cur(i), 0)),
            pl.BlockSpec((HALO, 512), lambda i: (jnp.maximum(cur(i) * (BLOCK // HALO) - 1, 0), 0)),
            pl.BlockSpec((BLOCK, 512), lambda i: (cur(i), 1)),
            pl.BlockSpec((BLOCK, 256), lambda i: (cur(i), 4)),
            pl.BlockSpec((BLOCK, 256), lambda i: (prv(i), 4))]


def _mixer_param_specs():
    return [pl.BlockSpec((N_Q_HEADS, BLOCK, 2 * BLOCK), lambda i: (0, 0, 0)), SMEM,
            pl.BlockSpec((4, POOL_GROUP_DIM, POOL_GROUP_DIM), lambda i: (0, 0, 0)),
            pl.BlockSpec((1, POOL_WIDTH), lambda i: (0, 0))]


def mixers_fwd(proj, biasm, sinks, w_pool, pool_scale, shards):
    t = proj.shape[0]
    nb = t // BLOCK
    ns = len(shards)

    def body(*refs):
        uc_ref, halo_ref, q_ref, kvc_ref, kvp_ref, biasm_ref, sinks_ref, wp_ref, sc_ref = refs[:9]
        shard_refs, out_ref, gathered_refs = refs[9:9 + ns], refs[9 + ns], refs[10 + ns:10 + 2 * ns]
        ubuf, s_all, p_all, send_sems, recv_sems, local_sems = refs[10 + 2 * ns:16 + 2 * ns]
        i = pl.program_id(0)
        start, finish = _gather_plan(shard_refs, gathered_refs, send_sems, recv_sems, local_sems, refs[16 + 2 * ns:])
        pl.when(i == 0)(start)

        _fill_pool_input(i, ubuf, uc_ref, halo_ref)
        for g, w in enumerate(POOL_WINDOWS):
            mixed = _dot(_pooled(i, g, w, ubuf).astype(BF16), wp_ref[g])
            cols = slice(g * POOL_GROUP_DIM, (g + 1) * POOL_GROUP_DIM)
            out_ref[:, cols] = (mixed * sc_ref[:, cols]).astype(BF16)
        kv = jnp.concatenate([kvp_ref[...], kvc_ref[...]], axis=0)
        k_var = _head_variants(kv[:, 0:2 * HEAD_DIM])
        v_var = _head_variants(kv[:, 2 * HEAD_DIM:])
        for hq in range(N_Q_HEADS):
            j, half, h = hq // 2, hq % 2, hq // GQA_GROUP
            q2 = q_ref[:, 2 * HEAD_DIM * j:2 * HEAD_DIM * (j + 1)].astype(BF16)
            s_all[hq] = _dot_nt(q2, k_var[h][half])
        for hq in range(N_Q_HEADS):
            for r in range(0, BLOCK, ROW_CHUNK):
                rows = slice(r, r + ROW_CHUNK)
                probs, _ = _head_probs(i, hq, rows, s_all, biasm_ref, sinks_ref)
                p_all[hq, rows, :] = probs.astype(BF16)
        for j in range(N_Q_HEADS // 2):
            h = 2 * j // GQA_GROUP
            acc = _dot(p_all[2 * j], v_var[h][0]) + _dot(p_all[2 * j + 1], v_var[h][1])
            out_ref[:, POOL_WIDTH + 2 * HEAD_DIM * j:POOL_WIDTH + 2 * HEAD_DIM * (j + 1)] = acc.astype(BF16)

        pl.when(i == nb - 1)(finish)

    return pl.pallas_call(
        body, name="mixers_fwd", grid=(nb,),
        out_shape=[jax.ShapeDtypeStruct((t, 2 * POOL_WIDTH), BF16)]
        + [jax.ShapeDtypeStruct((N_DEV, *sh.shape), sh.dtype) for sh in shards],
        in_specs=_mixer_in_specs(lambda i: i, lambda i: jnp.maximum(i - 1, 0)) + _mixer_param_specs() + [ANY] * ns,
        out_specs=[pl.BlockSpec((BLOCK, 2 * POOL_WIDTH), lambda i: (i, 0))] + [ANY] * ns,
        scratch_shapes=[pltpu.VMEM((HALO + BLOCK, POOL_WIDTH), F32), pltpu.VMEM((N_Q_HEADS, BLOCK, 2 * BLOCK), F32),
                        pltpu.VMEM((N_Q_HEADS, BLOCK, 2 * BLOCK), BF16)] + _gather_scratch(shards),
        compiler_params=_params(),
    )(proj, proj, proj, proj, proj, biasm, sinks, w_pool, pool_scale, *shards)


def outproj_norm(cat, w, x, g, g_next, shard, partial):
    t, d = x.shape
    tm = TOKEN_TILE
    last = t // tm - 1
    rows = [(partial.shape[1] - shard.shape[0], shard.shape[0])]

    def body(c_ref, w_ref, x_ref, g_ref, gn_ref, shard_ref, partial_ref, mix_ref, x1_ref, h2_ref, gathered_ref,
             send_sems, recv_sems, local_sems, bounce):
        i = pl.program_id(0)
        start, finish = _gather_plan([shard_ref], [gathered_ref], send_sems, recv_sems, local_sems, [bounce], rows)
        pl.when(i == 0)(start)
        mix = _dot(c_ref[...], w_ref[...])
        mix_ref[...] = mix
        x1 = x_ref[...] + (mix * _rstd(mix)) * g_ref[...]
        x1_ref[...] = x1
        h2_ref[...] = ((x1 * _rstd(x1)) * gn_ref[...]).astype(BF16)
        pl.when(i == last)(finish)

    row = pl.BlockSpec((tm, d), lambda i: (i, 0))
    gain = pl.BlockSpec((1, d), lambda i: (0, 0))
    return pl.pallas_call(
        body, name="outproj_norm", grid=(t // tm,),
        out_shape=[jax.ShapeDtypeStruct((t, d), F32), jax.ShapeDtypeStruct((t, d), F32), jax.ShapeDtypeStruct((t, d), BF16),
                   jax.ShapeDtypeStruct(partial.shape, partial.dtype)],
        in_specs=[pl.BlockSpec((tm, cat.shape[1]), lambda i: (i, 0)), pl.BlockSpec(w.shape, lambda i: (0, 0)), row, gain, gain,
                  ANY, ANY],
        out_specs=[row, row, row, ANY],
        input_output_aliases={6: 3},
        scratch_shapes=_gather_scratch([shard]),
        compiler_params=_params(),
    )(cat, w, x, g, g_next, shard, partial)


def ffn_up(h, gate_t, up_t, down_shard):
    t, d = h.shape
    n = gate_t.shape[1]
    f = N_DEV * n
    tm, ts = FFN_TOKEN_TILE, FF_SHARDS_PER_TILE
    tn = ts * n
    steps = (f // tn, t // tm)

    def body(h_ref, wg_ref, wu_ref, shard_ref, gate_ref, up_ref, a_ref, gathered_ref,
             send_sems, recv_sems, local_sems, bounce):
        j, i = pl.program_id(0), pl.program_id(1)
        start, finish = _gather_plan([shard_ref], [gathered_ref], send_sems, recv_sems, local_sems, [bounce])
        pl.when((i == 0) & (j == 0))(start)

        hv = h_ref[...]
        gate = _dot_nt(hv, _merge_rows(wg_ref[...]))
        up = _dot_nt(hv, _merge_rows(wu_ref[...]))
        gate_ref[...] = gate.astype(BF16)
        up_ref[...] = up.astype(BF16)
        a_ref[...] = (gate * (1.0 / (1.0 + jnp.exp(-gate))) * up).astype(BF16)

        pl.when((j == steps[0] - 1) & (i == steps[1] - 1))(finish)

    wide = pl.BlockSpec((tm, tn), lambda j, i: (i, j))
    return pl.pallas_call(
        body, name="ffn_up", grid=steps,
        out_shape=[jax.ShapeDtypeStruct((t, f), BF16)] * 3
        + [jax.ShapeDtypeStruct((N_DEV, *down_shard.shape), down_shard.dtype)],
        in_specs=[pl.BlockSpec((tm, d), lambda j, i: (i, 0)),
                  pl.BlockSpec((ts, n, d), lambda j, i: (j, 0, 0)),
                  pl.BlockSpec((ts, n, d), lambda j, i: (j, 0, 0)), ANY],
        out_specs=[wide, wide, wide, ANY],
        scratch_shapes=_gather_scratch([down_shard]),
        compiler_params=_params(),
    )(h, gate_t, up_t, down_shard)


def ffn_down_loss(a, w_down, x1, g, target):
    t, d = x1.shape
    tm = TOKEN_TILE

    def body(a_ref, w_ref, x_ref, g_ref, t_ref, df_ref, dy_ref, dg_ref, loss_ref):
        @pl.when(pl.program_id(0) == 0)
        def _():
            dg_ref[...] = jnp.zeros_like(dg_ref)
            loss_ref[...] = jnp.zeros_like(loss_ref)

        f = _dot(a_ref[...], _merge_rows(w_ref[...]))
        r = _rstd(f)
        g = g_ref[...]
        err = x_ref[...] + (f * r) * g - t_ref[...]
        loss_ref[...] += 0.5 * jnp.sum(jnp.mean(err * err, axis=-1, keepdims=True))
        dy = err * (1.0 / d)
        dy_ref[...] = dy
        df, dg_rows = _norm_bwd(dy, f, r, g)
        df_ref[...] = df.astype(BF16)
        dg_ref[...] += _as_rows(jnp.sum(dg_rows, axis=0, keepdims=True))

    row = pl.BlockSpec((tm, d), lambda i: (i, 0))
    gain = pl.BlockSpec((1, d), lambda i: (0, 0))
    return pl.pallas_call(
        body, name="ffn_down_loss", grid=(t // tm,),
        out_shape=[jax.ShapeDtypeStruct((t, d), BF16), jax.ShapeDtypeStruct((t, d), F32),
                   jax.ShapeDtypeStruct((d // 128, 128), F32), jax.ShapeDtypeStruct((1, 128), F32)],
        in_specs=[pl.BlockSpec((tm, a.shape[1]), lambda i: (i, 0)), pl.BlockSpec(w_down.shape, lambda i: (0, 0, 0)), row, gain, row],
        out_specs=[row, row, pl.BlockSpec((d // 128, 128), lambda i: (0, 0)), pl.BlockSpec((1, 128), lambda i: (0, 0))],
        compiler_params=_params(),
    )(a, w_down, x1, g, target)


def ffn_down_bwd(df, w_down, gate, up):
    t, d = df.shape
    n = w_down.shape[1]
    f = gate.shape[1]
    tm, ts = FFN_TOKEN_TILE, FF_SHARDS_PER_TILE
    tn = ts * n

    def body(df_ref, w_ref, gate_ref, up_ref, dgate_ref, dup_ref):
        da = _dot_nt(df_ref[...], _merge_rows(w_ref[...]))
        gate = gate_ref[...].astype(F32)
        sig = 1.0 / (1.0 + jnp.exp(-gate))
        dgate_ref[...] = (da * up_ref[...].astype(F32) * (sig * (1.0 + gate * (1.0 - sig)))).astype(BF16)
        dup_ref[...] = (da * (gate * sig)).astype(BF16)

    wide = pl.BlockSpec((tm, tn), lambda j, i: (i, j))
    return pl.pallas_call(
        body, name="ffn_down_bwd", grid=(f // tn, t // tm),
        out_shape=[jax.ShapeDtypeStruct((t, f), BF16)] * 2,
        in_specs=[pl.BlockSpec((tm, d), lambda j, i: (i, 0)), pl.BlockSpec((ts, n, d), lambda j, i: (j, 0, 0)), wide, wide],
        out_specs=[wide, wide],
        compiler_params=_params(),
    )(df, w_down, gate, up)


def grad_rows(a, b, name, by_core=False):
    t, m = a.shape
    d = b.shape[1]
    r = m // N_DEV
    tt = TOKEN_TILE
    last = t // tt - 1
    out_shape = (2, N_CHIP, r, d) if by_core else (N_DEV, r, d)

    def body(a_ref, b_ref, out_ref, acc):
        k = pl.program_id(0)

        @pl.when(k == 0)
        def _():
            acc[...] = jnp.zeros_like(acc)

        acc[...] += _dot_tn(a_ref[...], b_ref[...])

        @pl.when(k == last)
        def _():
            if by_core:
                blocks = acc[...].reshape(N_CHIP, 2, r, d)
                for chip in range(N_CHIP):
                    for core in range(2):
                        out_ref[core, chip] = blocks[chip, core].astype(BF16)
            else:
                out_ref[...] = acc[...].reshape(out_shape).astype(BF16)

    return pl.pallas_call(
        body, name=name, grid=(t // tt,),
        out_shape=jax.ShapeDtypeStruct(out_shape, BF16),
        in_specs=[pl.BlockSpec((tt, m), lambda k: (k, 0)), pl.BlockSpec((tt, d), lambda k: (k, 0))],
        out_specs=pl.BlockSpec(out_shape, lambda k: (0,) * len(out_shape)),
        scratch_shapes=[pltpu.VMEM((m, d), F32)],
        compiler_params=_params(),
    )(a, b)


def grad_ffn(lhs, b, name, pair_parts=()):
    t, f = lhs[0].shape
    d = b.shape[1]
    nw = len(lhs)
    na = len(pair_parts)
    n = f // N_DEV
    tt, ts = TOKEN_TILE, FF_SHARDS_PER_TILE
    tn = ts * n
    steps = (f // tn, t // tt)

    def body(*refs):
        a_refs, b_ref, part_refs = refs[:nw], refs[nw], refs[nw + 1:nw + 1 + na]
        out_refs = refs[nw + 1 + na:2 * nw + 1 + na]
        got_refs = refs[2 * nw + 1 + na:2 * nw + 1 + 2 * na]
        acc = refs[2 * nw + 1 + 2 * na]
        i, k = pl.program_id(0), pl.program_id(1)
        if na:
            start, finish = _pair_plan(part_refs, got_refs, *refs[2 * nw + 2 + 2 * na:])
            pl.when((i == 0) & (k == 0))(start)

        @pl.when(k == 0)
        def _():
            acc[...] = jnp.zeros_like(acc)

        for w in range(nw):
            acc[w] += _dot_tn(a_refs[w][...], b_ref[...])

        @pl.when(k == steps[1] - 1)
        def _():
            for w in range(nw):
                blocks = acc[w].reshape(ts // 2, 2, n, d)
                for chip in range(ts // 2):
                    for core in range(2):
                        out_refs[w][core, chip] = blocks[chip, core].astype(BF16)

        if na:
            pl.when((i == steps[0] - 1) & (k == steps[1] - 1))(finish)

    out = pl.pallas_call(
        body, name=name, grid=steps,
        out_shape=[jax.ShapeDtypeStruct((2, N_CHIP, n, d), BF16)] * nw
        + [jax.ShapeDtypeStruct(p.shape[1:], p.dtype) for p in pair_parts],
        in_specs=[pl.BlockSpec((tt, tn), lambda i, k: (k, i))] * nw + [pl.BlockSpec((tt, d), lambda i, k: (k, 0))] + [ANY] * na,
        out_specs=[pl.BlockSpec((2, ts // 2, n, d), lambda i, k: (0, i, 0, 0))] * nw + [ANY] * na,
        scratch_shapes=[pltpu.VMEM((nw, tn, d), F32)]
        + ([pltpu.SemaphoreType.DMA((na,)), pltpu.SemaphoreType.DMA((na,))] if na else []),
        compiler_params=_params(),
    )(*lhs, b, *pair_parts)
    return out[:nw], out[nw:]


def ffn_up_bwd(dgate, dup, gate_t, up_t, x1, g_ffn, dy, mix, g_mix, chip_parts):
    t, d = x1.shape
    n = gate_t.shape[1]
    f = N_DEV * n
    tm = TOKEN_TILE
    na = len(chip_parts)
    last = t // tm - 1

    def body(*refs):
        dg_ref, du_ref, wg_ref, wu_ref, x_ref, gf_ref, dy_ref, mix_ref, gm_ref = refs[:9]
        part_refs = refs[9:9 + na]
        dx1_ref, dmix_ref, dgf_ref, dgm_ref = refs[9 + na:13 + na]
        slot_refs = refs[13 + na:13 + 2 * na]
        send_sems, recv_sems, local_sems = refs[13 + 2 * na:16 + 2 * na]
        i = pl.program_id(0)
        start, finish = _chip_exchange_plan(part_refs, slot_refs, send_sems, recv_sems, local_sems, refs[16 + 2 * na:])

        @pl.when(i == 0)
        def _():
            start()
            dgf_ref[...] = jnp.zeros_like(dgf_ref)
            dgm_ref[...] = jnp.zeros_like(dgm_ref)

        dh = _dot(dg_ref[...], _merge_rows(wg_ref[...])) + _dot(du_ref[...], _merge_rows(wu_ref[...]))
        x1 = x_ref[...]
        dx, dgf_rows = _norm_bwd(dh, x1, _rstd(x1), gf_ref[...])
        dx1 = dy_ref[...] + dx
        dx1_ref[...] = dx1
        dgf_ref[...] += _as_rows(jnp.sum(dgf_rows, axis=0, keepdims=True))
        mix = mix_ref[...]
        dmix, dgm_rows = _norm_bwd(dx1, mix, _rstd(mix), gm_ref[...])
        dmix_ref[...] = dmix.astype(BF16)
        dgm_ref[...] += _as_rows(jnp.sum(dgm_rows, axis=0, keepdims=True))
        pl.when(i == last)(finish)

    row = pl.BlockSpec((tm, d), lambda i: (i, 0))
    wide = pl.BlockSpec((tm, f), lambda i: (i, 0))
    gain = pl.BlockSpec((1, d), lambda i: (0, 0))
    gain_rows = pl.BlockSpec((d // 128, 128), lambda i: (0, 0))
    whole = pl.BlockSpec((N_DEV, n, d), lambda i: (0, 0, 0), pipeline_mode=pl.Buffered(1))
    out = pl.pallas_call(
        body, name="ffn_up_bwd", grid=(t // tm,),
        out_shape=[jax.ShapeDtypeStruct((t, d), F32), jax.ShapeDtypeStruct((t, d), BF16),
                   jax.ShapeDtypeStruct((d // 128, 128), F32), jax.ShapeDtypeStruct((d // 128, 128), F32)]
        + [jax.ShapeDtypeStruct(p.shape, p.dtype) for p in chip_parts],
        in_specs=[wide, wide, whole, whole, row, gain, row, row, gain] + [ANY] * na,
        out_specs=[row, row, gain_rows, gain_rows] + [ANY] * na,
        scratch_shapes=_chip_exchange_scratch(chip_parts),
        compiler_params=_params(),
    )(dgate, dup, gate_t, up_t, x1, g_ffn, dy, mix, g_mix, *chip_parts)
    return out[:4], out[4:]


def outproj_bwd(dmix, w_out):
    t, d = dmix.shape
    tm = TOKEN_TILE

    def body(dm_ref, w_ref, out_ref):
        out_ref[...] = _dot_nt(dm_ref[...], w_ref[...])

    return pl.pallas_call(
        body, name="outproj_bwd", grid=(t // tm,),
        out_shape=jax.ShapeDtypeStruct((t, w_out.shape[0]), F32),
        in_specs=[pl.BlockSpec((tm, d), lambda i: (i, 0)), pl.BlockSpec(w_out.shape, lambda i: (0, 0))],
        out_specs=pl.BlockSpec((tm, w_out.shape[0]), lambda i: (i, 0)),
        compiler_params=_params(),
    )(dmix, w_out)


def mixers_bwd(proj, dcat, biasm, sinks, w_pool, pool_scale, ffn_parts):
    t = proj.shape[0]
    nb = t // BLOCK
    na = len(ffn_parts)

    def body(*refs):
        (uc_ref, halo_ref, q_ref, kvc_ref, kvp_ref, dcat_ref, biasm_ref, sinks_ref, wp_ref, sc_ref) = refs[:10]
        part_refs = refs[10:10 + na]
        dproj_ref, dbias_ref, dsink_ref, dwp_ref, dsc_ref = refs[10 + na:15 + na]
        slot_refs = refs[15 + na:15 + 2 * na]
        ubuf, dbuf, c_u, c_q, c_kv, s_all, dp_all, ds_all, p_all = refs[15 + 2 * na:24 + 2 * na]
        send_sems, recv_sems, local_sems = refs[24 + 2 * na:27 + 2 * na]
        bounce = refs[27 + 2 * na:]
        i = pl.program_id(0)
        lane = lax.broadcasted_iota(jnp.int32, (1, 128), 1)
        start, finish = _chip_exchange_plan(part_refs, slot_refs, send_sems, recv_sems, local_sems, bounce)

        @pl.when(i == 0)
        def _():
            start()
            dbias_ref[...] = jnp.zeros_like(dbias_ref)
            dwp_ref[...] = jnp.zeros_like(dwp_ref)
            dsc_ref[...] = jnp.zeros_like(dsc_ref)
            dsink_ref[...] = jnp.zeros_like(dsink_ref)
            dbuf[...] = jnp.zeros_like(dbuf)
            c_u[...] = jnp.zeros_like(c_u)
            c_q[...] = jnp.zeros_like(c_q)
            c_kv[...] = jnp.zeros_like(c_kv)

        @pl.when(i < nb)
        def _():
            _fill_pool_input(i, ubuf, uc_ref, halo_ref)
            for g, w in enumerate(POOL_WINDOWS):
                cols = slice(g * POOL_GROUP_DIM, (g + 1) * POOL_GROUP_DIM)
                pooled = _pooled(i, g, w, ubuf).astype(BF16)
                mixed = _dot(pooled, wp_ref[g])
                dout = dcat_ref[:, cols]
                dsc_ref[g:g + 1, :] += jnp.sum(dout * mixed, axis=0, keepdims=True)
                dmixed = (dout * sc_ref[:, cols]).astype(BF16)
                dwp_ref[g] += _dot_tn(pooled, dmixed)
                dpooled = _dot_nt(dmixed, wp_ref[g])
                scaled = dpooled * _inv_count(i, w)
                dbuf[BLOCK:, cols] = scaled[0:HALO]
                dproj_ref[:, cols] = (_window_sum(dbuf, g, w, lambda k: k) + c_u[:, cols]).astype(BF16)
                dbuf[0:BLOCK, cols] = scaled
                c_u[:, cols] = -dpooled

            kv = jnp.concatenate([kvp_ref[...], kvc_ref[...]], axis=0)
            k_var = _head_variants(kv[:, 0:2 * HEAD_DIM])
            v_var = _head_variants(kv[:, 2 * HEAD_DIM:])
            q2s = [q_ref[:, 2 * HEAD_DIM * j:2 * HEAD_DIM * (j + 1)].astype(BF16) for j in range(N_Q_HEADS // 2)]
            do2s = [dcat_ref[:, POOL_WIDTH + 2 * HEAD_DIM * j:POOL_WIDTH + 2 * HEAD_DIM * (j + 1)].astype(BF16)
                    for j in range(N_Q_HEADS // 2)]
            slot = lambda hq: 4 * (hq // GQA_GROUP) + 2 * (hq % 2) + (hq % GQA_GROUP) // 2
            for hq in range(N_Q_HEADS):
                j, half, h = hq // 2, hq % 2, hq // GQA_GROUP
                s_all[hq] = _dot_nt(q2s[j], k_var[h][half])
                dp_all[hq] = _dot_nt(do2s[j], v_var[h][half])
            dsink_row = jnp.zeros((1, 128), F32)
            for hq in range(N_Q_HEADS):
                dsink = 0.0
                for r in range(0, BLOCK, ROW_CHUNK):
                    rows = slice(r, r + ROW_CHUNK)
                    probs, p_sink = _head_probs(i, hq, rows, s_all, biasm_ref, sinks_ref)
                    dp = dp_all[hq, rows, :]
                    delta = jnp.sum(probs * dp, axis=-1, keepdims=True)
                    ds = probs * (dp - delta)
                    dbias_ref[hq, rows, :] += ds
                    dsink = dsink + jnp.sum(p_sink * delta)
                    ds_all[slot(hq), rows, :] = (ds * ATTN_SCALE).astype(BF16)
                    p_all[slot(hq), rows, :] = probs.astype(BF16)
                dsink_row = dsink_row - jnp.where(lane == hq, dsink, 0.0)
            dsink_ref[...] += dsink_row
            dq2 = [None] * (N_Q_HEADS // 2)
            for hq in range(N_Q_HEADS):
                j, half, h = hq // 2, hq % 2, hq // GQA_GROUP
                dq = _dot(ds_all[slot(hq)], k_var[h][half])
                dq2[j] = dq if dq2[j] is None else dq2[j] + dq
            low = lax.broadcasted_iota(jnp.int32, (2 * BLOCK, 2 * HEAD_DIM), 1) < HEAD_DIM
            dk_half, dv_half = [[None, None], [None, None]], [[None, None], [None, None]]
            for h in range(N_KV_HEADS):
                for half in range(2):
                    heads = [hq for hq in range(GQA_GROUP * h, GQA_GROUP * (h + 1)) if hq % 2 == half]
                    base = slot(heads[0])
                    q_rows = jnp.concatenate([q2s[hq // 2] for hq in heads], axis=0)
                    do_rows = jnp.concatenate([do2s[hq // 2] for hq in heads], axis=0)
                    dk_half[h][half] = _dot_tn(_merge_rows(ds_all[base:base + 2]), q_rows)
                    dv_half[h][half] = _dot_tn(_merge_rows(p_all[base:base + 2]), do_rows)

            def pair_of(halves):
                return jnp.where(low, halves[0][0] + pltpu.roll(halves[0][1], HEAD_DIM, 1),
                                 halves[1][1] + pltpu.roll(halves[1][0], HEAD_DIM, 1))

            dkv = jnp.concatenate([pair_of(dk_half), pair_of(dv_half)], axis=1)
            dproj_ref[:, POOL_WIDTH:2 * POOL_WIDTH] = c_q[...].astype(BF16)
            dproj_ref[:, 2 * POOL_WIDTH:] = (c_kv[...] + dkv[0:BLOCK]).astype(BF16)
            c_q[...] = jnp.concatenate(dq2, axis=1)
            c_kv[...] = dkv[BLOCK:]

        @pl.when(i == nb)
        def _():
            dbuf[BLOCK:, :] = jnp.zeros((HALO, POOL_WIDTH), F32)
            for g, w in enumerate(POOL_WINDOWS):
                cols = slice(g * POOL_GROUP_DIM, (g + 1) * POOL_GROUP_DIM)
                dproj_ref[:, cols] = (_window_sum(dbuf, g, w, lambda k: k) + c_u[:, cols]).astype(BF16)
            dproj_ref[:, POOL_WIDTH:2 * POOL_WIDTH] = c_q[...].astype(BF16)
            dproj_ref[:, 2 * POOL_WIDTH:] = c_kv[...].astype(BF16)
            finish()

    cur = lambda i: jnp.minimum(i, nb - 1)
    prv = lambda i: jnp.maximum(jnp.minimum(i, nb - 1) - 1, 0)
    out = pl.pallas_call(
        body, name="mixers_bwd", grid=(nb + 1,),
        out_shape=[jax.ShapeDtypeStruct((t, proj.shape[1]), BF16),
                   jax.ShapeDtypeStruct((N_Q_HEADS, BLOCK, 2 * BLOCK), F32),
                   jax.ShapeDtypeStruct((1, 128), F32),
                   jax.ShapeDtypeStruct((4, POOL_GROUP_DIM, POOL_GROUP_DIM), F32),
                   jax.ShapeDtypeStruct((len(POOL_WINDOWS), POOL_GROUP_DIM), F32)]
        + [jax.ShapeDtypeStruct(p.shape, p.dtype) for p in ffn_parts],
        in_specs=_mixer_in_specs(cur, prv) + [pl.BlockSpec((BLOCK, 2 * POOL_WIDTH), lambda i: (cur(i), 0))]
        + _mixer_param_specs() + [ANY] * na,
        out_specs=[pl.BlockSpec((BLOCK, proj.shape[1]), lambda i: (jnp.maximum(i - 1, 0), 0)),
                   pl.BlockSpec((N_Q_HEADS, BLOCK, 2 * BLOCK), lambda i: (0, 0, 0)),
                   pl.BlockSpec((1, 128), lambda i: (0, 0)),
                   pl.BlockSpec((4, POOL_GROUP_DIM, POOL_GROUP_DIM), lambda i: (0, 0, 0)),
                   pl.BlockSpec((len(POOL_WINDOWS), POOL_GROUP_DIM), lambda i: (0, 0))] + [ANY] * na,
        scratch_shapes=[pltpu.VMEM((HALO + BLOCK, POOL_WIDTH), F32), pltpu.VMEM((BLOCK + HALO, POOL_WIDTH), F32),
                        pltpu.VMEM((BLOCK, POOL_WIDTH), F32), pltpu.VMEM((BLOCK, POOL_WIDTH), F32),
                        pltpu.VMEM((BLOCK, 256), F32),
                        pltpu.VMEM((N_Q_HEADS, BLOCK, 2 * BLOCK), F32), pltpu.VMEM((N_Q_HEADS, BLOCK, 2 * BLOCK), F32),
                        pltpu.VMEM((N_Q_HEADS, BLOCK, 2 * BLOCK), BF16), pltpu.VMEM((N_Q_HEADS, BLOCK, 2 * BLOCK), BF16)]
        + _chip_exchange_scratch(ffn_parts),
        compiler_params=_params(),
    )(proj, proj, proj, proj, proj, dcat, biasm, sinks, w_pool, pool_scale, *ffn_parts)
    return out[:5], out[5:]


def inproj_bwd(dproj, w_in_t, x, g, dx1):
    t, d = x.shape
    n = dproj.shape[1]
    tm = TOKEN_TILE

    def body(dp_ref, w_ref, x_ref, g_ref, dx1_ref, dx_ref, dg_ref):
        @pl.when(pl.program_id(0) == 0)
        def _():
            dg_ref[...] = jnp.zeros_like(dg_ref)

        dh = _dot(dp_ref[...], w_ref[...])
        xv = x_ref[...]
        dx, dg_rows = _norm_bwd(dh, xv, _rstd(xv), g_ref[...])
        dx_ref[...] = dx1_ref[...] + dx
        dg_ref[...] += _as_rows(jnp.sum(dg_rows, axis=0, keepdims=True))

    row = pl.BlockSpec((tm, d), lambda i: (i, 0))
    gain = pl.BlockSpec((1, d), lambda i: (0, 0))
    return pl.pallas_call(
        body, name="inproj_bwd", grid=(t // tm,),
        out_shape=[jax.ShapeDtypeStruct((t, d), F32), jax.ShapeDtypeStruct((d // 128, 128), F32)],
        in_specs=[pl.BlockSpec((tm, n), lambda i: (i, 0)), pl.BlockSpec(w_in_t.shape, lambda i: (0, 0)), row, gain, row],
        out_specs=[row, pl.BlockSpec((d // 128, 128), lambda i: (0, 0))],
        compiler_params=_params(),
    )(dproj, w_in_t, x, g, dx1)


def _bucket_band():
    qi = jnp.arange(BLOCK)[:, None]
    kj = jnp.arange(2 * BLOCK)[None, :]
    dist = qi + BLOCK - kj
    n = jnp.maximum(dist, 0)
    nf = jnp.maximum(n, 1).astype(F32)
    large = MAX_EXACT + (jnp.log(nf / MAX_EXACT) / np.float32(np.log(MAX_DISTANCE / MAX_EXACT))
                         * (N_BUCKETS - MAX_EXACT)).astype(jnp.int32)
    large = jnp.minimum(large, N_BUCKETS - 1)
    bucket = jnp.where(n < MAX_EXACT, n, large)
    in_window = (dist >= 0) & (dist < BLOCK)
    return bucket.astype(F32), in_window.astype(F32)


def kernel(x, g_pre_mix, w_in, w_pool, pool_scale, rel_bias, sinks, w_out, g_post_mix, g_pre_ffn, w_gate, w_up, w_down, g_post_ffn, loss_target, m_g_pre_mix, m_w_in, m_w_pool, m_pool_scale, m_rel_bias, m_sinks, m_w_out, m_g_post_mix, m_g_pre_ffn, m_w_gate, m_w_up, m_w_down, m_g_post_ffn, v_g_pre_mix, v_w_in, v_w_pool, v_pool_scale, v_rel_bias, v_sinks, v_w_out, v_g_post_mix, v_g_pre_ffn, v_w_gate, v_w_up, v_w_down, v_g_post_ffn):
    d = x.shape[-1]
    xs, target = x[0], loss_target[0]

    w_in_ts = w_in[0].T.astype(BF16)
    w_out_s = w_out[0].astype(BF16)
    gate_ts = w_gate[0].T.astype(BF16)
    up_ts = w_up[0].T.astype(BF16)
    w_down_s = w_down[0].astype(BF16)

    bucket, in_window = _bucket_band()
    biasm = bias_band(bucket, in_window, rel_bias)
    w_pool_b = w_pool[0].astype(BF16)
    half = up_ts.shape[0] // 2
    proj, h1, w_in_t, up_t = norm_inproj(xs, g_pre_mix, w_in_ts, up_ts[:half], up_ts.shape[0])
    w_in_t = w_in_t.reshape(-1, d)
    cat, gate_t, w_out_f = mixers_fwd(proj, biasm, sinks, w_pool_b, pool_scale, [gate_ts, w_out_s])
    w_out_f = w_out_f.reshape(-1, d)
    mix, x1, h2, up_t = outproj_norm(cat, w_out_f, xs, g_post_mix, g_pre_ffn, up_ts[half:], up_t)
    gate, up, act, w_down_f = ffn_up(h2, gate_t, up_t, w_down_s)
    df, dy, dg_post_ffn, loss_part = ffn_down_loss(act, w_down_f, x1, g_post_ffn, target)

    def pair_sum(parts, tag):
        return pair_add(parts, pair_exchange(parts, "pair_exchange_" + tag), "pair_add_" + tag)

    dgate, dup = ffn_down_bwd(df, w_down_f, gate, up)
    (d_gate, d_up), _ = grad_ffn([dgate, dup], h2, "grad_w_gate_up")
    (d_down,), got_gate_up = grad_ffn([act], df, "grad_w_down", [d_gate, d_up])
    q_gate, q_up, q_down = pair_add(
        [d_gate, d_up, d_down], [*got_gate_up, *pair_exchange([d_down], "pair_exchange_down")], "pair_add_ffn")
    (dx1, dmix, dg_pre_ffn, dg_post_mix), gate_down_slots = ffn_up_bwd(
        dgate, dup, gate_t, up_t, x1, g_pre_ffn, dy, mix, g_post_mix, [q_gate, q_down])
    dcat = outproj_bwd(dmix, w_out_f)
    d_out = grad_rows(cat, dmix, "grad_w_out", by_core=True)
    q_out, = pair_sum([d_out], "out")
    (dproj, dbias, dsinks, dw_pool, dpool_scale), up_out_slots = mixers_bwd(
        proj, dcat, biasm, sinks, w_pool_b, pool_scale, [q_up, q_out])
    drel_bias = bias_band_bwd(bucket, dbias)
    grad_x, dg_pre_mix = inproj_bwd(dproj, w_in_t, xs, g_pre_mix, dx1)

    small_w = [g_pre_mix, g_post_mix, g_pre_ffn, g_post_ffn, pool_scale, sinks, w_pool, rel_bias.T]
    small_m = [m_g_pre_mix, m_g_post_mix, m_g_pre_ffn, m_g_post_ffn, m_pool_scale, m_sinks, m_w_pool, m_rel_bias.T]
    small_v = [v_g_pre_mix, v_g_post_mix, v_g_pre_ffn, v_g_post_ffn, v_pool_scale, v_sinks, v_w_pool, v_rel_bias.T]
    d_in_t, total, total_rb = grad_w_in_small_reduce(
        dproj, h1, [dg_pre_mix, dg_post_mix, dg_pre_ffn, dg_post_ffn], dpool_scale, dsinks, loss_part, dw_pool, drel_bias)
    g_in_t = reduce_w_in(d_in_t)
    loss_row, sm = small_adamw(total, total_rb, small_w, small_m, small_v)
    sm[7] = [r.T for r in sm[7]]
    big_w = [w_in[0].T, w_out[0], w_gate[0].T, w_up[0].T, w_down[0]]
    big_m = [m_w_in[0].T, m_w_out[0], m_w_gate[0].T, m_w_up[0].T, m_w_down[0]]
    big_v = [v_w_in[0].T, v_w_out[0], v_w_gate[0].T, v_w_up[0].T, v_w_down[0]]
    (gate_slots, down_slots), (up_slots, out_slots) = gate_down_slots, up_out_slots
    upd = sum_adamw([out_slots, gate_slots, up_slots, down_slots], big_w[1:], big_m[1:], big_v[1:], "sum_adamw")
    upd = [[g_in_t, *adamw_update(big_w[:1], [g_in_t], big_m[:1], big_v[:1], "adamw_in")[0]], *upd]
    back = lambda k, a: (a.T if k in (0, 2, 3) else a)[None]
    big = [[back(k, u) for u in upd[k]] for k in range(5)]

    def ordered(kind):
        s, b = [p[kind] for p in sm], [p[kind] for p in big]
        return [s[0], b[0], s[6], s[4], s[7], s[5], b[1], s[1], s[2], b[2], b[3], b[4], s[3]]

    return (loss_row[0, 0], grad_x[None], *ordered(0), *ordered(1), *ordered(2), *ordered(3))
```

```python
import numpy as np
import jax
import jax.numpy as jnp
from jax import lax
from jax.experimental import pallas as pl
from jax.experimental.pallas import tpu as pltpu

F32 = jnp.float32
BF16 = jnp.bfloat16

N_DEV = 8
N_CHIP = 4
POOL_WIDTH = 512
POOL_WINDOWS = (2, 4, 8, 16)
POOL_GROUP_DIM = 128
HEAD_DIM = 64
N_Q_HEADS = 8
N_KV_HEADS = 2
GQA_GROUP = 4
BLOCK = 128
HALO = 16
ROW_CHUNK = 32
N_BUCKETS = 32
MAX_EXACT = 16
MAX_DISTANCE = 128
EPS = 1e-6
NEG_INF = -1e30
ATTN_SCALE = float(1.0 / np.sqrt(np.float32(HEAD_DIM)))

ADAM_LR = 0.001
ADAM_B1 = 0.9
ADAM_B2 = 0.999
ADAM_EPS = 1e-08
ADAM_WD = 0.01
ADAM_STEP = 10

TOKEN_TILE = 1024
WIDE_K_TOKEN_TILE = 512
FFN_TOKEN_TILE = 1024
FF_SHARDS_PER_TILE = 4
VMEM_LIMIT = 56 * 1024 * 1024
MESH = pl.DeviceIdType.MESH
ANY = pl.BlockSpec(memory_space=pl.ANY)
VMEM = pl.BlockSpec(memory_space=pltpu.VMEM)
SMEM = pl.BlockSpec(memory_space=pltpu.SMEM)


def _params(**kw):
    return pltpu.CompilerParams(vmem_limit_bytes=VMEM_LIMIT, **kw)


def _dot(a, b):
    return jnp.dot(a, b, preferred_element_type=F32)


def _dot_nt(a, b):
    return lax.dot_general(a, b, (((1,), (1,)), ((), ())), preferred_element_type=F32)


def _dot_tn(a, b):
    return lax.dot_general(a, b, (((0,), (0,)), ((), ())), preferred_element_type=F32)


def _rstd(v):
    return lax.rsqrt(jnp.mean(v * v, axis=-1, keepdims=True) + EPS)


def _norm_bwd(dout, v, r, g):
    vn = v * r
    dn = dout * g
    dv = r * (dn - vn * jnp.mean(dn * vn, axis=-1, keepdims=True))
    return dv, dout * vn


def _as_rows(v):
    return jnp.concatenate([v[:, k:k + 128] for k in range(0, v.shape[1], 128)], axis=0)


def _as_lanes(rows):
    return jnp.concatenate([rows[k:k + 1, :] for k in range(rows.shape[0])], axis=1)


def _merge_rows(value):
    s, r, c_ = value.shape
    return value.reshape(s * r, c_)


def _gather_plan(srcs, outs, send_sems, recv_sems, local_sems=None, bounce=None, rows=None):
    n = len(srcs)
    x, y, c = lax.axis_index("x"), lax.axis_index("y"), lax.axis_index("c")
    me, sibling = (x, y, c), (x, y, 1 - c)
    chips = [(1 - x, y), (x, 1 - y), (1 - x, 1 - y)]

    def slot(a, px, py, pc):
        whole = outs[a].at[4 * px + 2 * py + pc]
        return whole if rows is None or rows[a] is None else whole.at[pl.ds(*rows[a])]

    def copy(a, k, block, to, from_src=False):
        return pltpu.make_async_remote_copy(
            src_ref=srcs[a] if from_src else slot(a, *block), dst_ref=slot(a, *block),
            send_sem=send_sems.at[k * n + a], recv_sem=recv_sems.at[k * n + a], device_id=to, device_id_type=MESH)

    def own_in(a):
        return pltpu.make_async_copy(srcs[a], bounce[a], local_sems.at[a])

    def own_out(a):
        return pltpu.make_async_copy(bounce[a], slot(a, *me), local_sems.at[a])

    def first(a):
        return [copy(a, 0, me, sibling, True)] + [copy(a, 1 + j, me, (*chip, c), True) for j, chip in enumerate(chips)]

    def passed(a, j):
        return copy(a, 4 + j, (*chips[j], c), sibling)

    def start():
        for a in range(n):
            if bounce is not None:
                own_in(a).start()
            for cp in first(a):
                cp.start()

    def finish():
        if bounce is not None:
            for a in range(n):
                own_in(a).wait()
                own_out(a).start()
        for j, chip in enumerate(chips):
            for a in range(n):
                copy(a, 1 + j, (*chip, c), me).wait_recv()
                passed(a, j).start()
        for a in range(n):
            copy(a, 0, sibling, me).wait_recv()
            for j, chip in enumerate(chips):
                copy(a, 4 + j, (*chip, 1 - c), me).wait_recv()
        for a in range(n):
            for cp in first(a) + [passed(a, j) for j in range(3)]:
                cp.wait_send()
            if bounce is not None:
                own_out(a).wait()

    return start, finish


def _gather_scratch(shards):
    n = len(shards)
    return [pltpu.SemaphoreType.DMA((7 * n,)), pltpu.SemaphoreType.DMA((7 * n,)), pltpu.SemaphoreType.DMA((n,))] \
        + [pltpu.VMEM(s.shape, s.dtype) for s in shards]


def _chip_exchange_plan(srcs, outs, send_sems, recv_sems, local_sems, bounce):
    n = len(srcs)
    x, y, c = lax.axis_index("x"), lax.axis_index("y"), lax.axis_index("c")
    my_chip = 2 * x + y

    def copies():
        out = []
        for a in range(n):
            for k in range(1, N_CHIP):
                px, py = x ^ (k >> 1), y ^ (k & 1)
                out.append(pltpu.make_async_remote_copy(
                    src_ref=srcs[a].at[2 * px + py], dst_ref=outs[a].at[my_chip],
                    send_sem=send_sems.at[(k - 1) * n + a], recv_sem=recv_sems.at[(k - 1) * n + a],
                    device_id=(px, py, c), device_id_type=MESH))
        return out

    def own_in(a):
        return pltpu.make_async_copy(srcs[a].at[my_chip], bounce[a], local_sems.at[a])

    def own_out(a):
        return pltpu.make_async_copy(bounce[a], outs[a].at[my_chip], local_sems.at[a])

    def start():
        for a in range(n):
            own_in(a).start()
        for cp in copies():
            cp.start()

    def finish():
        for a in range(n):
            own_in(a).wait()
            own_out(a).start()
        for cp in copies():
            cp.wait()
        for a in range(n):
            own_out(a).wait()

    return start, finish


def _chip_exchange_scratch(parts):
    n = len(parts)
    return [pltpu.SemaphoreType.DMA((3 * n,)), pltpu.SemaphoreType.DMA((3 * n,)), pltpu.SemaphoreType.DMA((n,))] \
        + [pltpu.VMEM(p.shape[1:], p.dtype) for p in parts]


def _pair_plan(srcs, outs, send_sems, recv_sems):
    x, y, c = lax.axis_index("x"), lax.axis_index("y"), lax.axis_index("c")

    def copies():
        return [pltpu.make_async_remote_copy(
            src_ref=srcs[a].at[1 - c], dst_ref=outs[a], send_sem=send_sems.at[a], recv_sem=recv_sems.at[a],
            device_id=(x, y, 1 - c), device_id_type=MESH) for a in range(len(srcs))]

    def start():
        for cp in copies():
            cp.start()

    def finish():
        for cp in copies():
            cp.wait()

    return start, finish


def pair_exchange(parts, name):
    n = len(parts)

    def body(*refs):
        start, finish = _pair_plan(refs[:n], refs[n:2 * n], *refs[2 * n:])
        start()
        finish()

    return pl.pallas_call(
        body, name=name, out_shape=[jax.ShapeDtypeStruct(p.shape[1:], p.dtype) for p in parts],
        in_specs=[ANY] * n, out_specs=[ANY] * n,
        scratch_shapes=[pltpu.SemaphoreType.DMA((n,)), pltpu.SemaphoreType.DMA((n,))],
    )(*parts)


def pair_add(parts, got, name):
    n = len(parts)

    def body(core_ref, *refs):
        for a in range(n):
            refs[2 * n + a][...] = (refs[a][...].astype(F32) + refs[n + a][...].astype(F32)).astype(BF16)

    def own(p):
        zeros = (0,) * (p.ndim - 2)
        return pl.BlockSpec((None, 1, *p.shape[2:]), lambda i, core: (core[0], i, *zeros))

    def plain(p):
        zeros = (0,) * (p.ndim - 1)
        return pl.BlockSpec((1, *p.shape[1:]), lambda i, core: (i, *zeros))

    core = lax.axis_index("c").astype(jnp.int32).reshape(1)
    return pl.pallas_call(
        body, name=name,
        grid_spec=pltpu.PrefetchScalarGridSpec(
            num_scalar_prefetch=1, grid=(got[0].shape[0],),
            in_specs=[own(p) for p in parts] + [plain(p) for p in got], out_specs=[plain(p) for p in got]),
        out_shape=[jax.ShapeDtypeStruct(p.shape, BF16) for p in got],
        compiler_params=_params(),
    )(core, *parts, *got)


def _adamw(w, g, m, v):
    m2 = ADAM_B1 * m + (1.0 - ADAM_B1) * g
    v2 = ADAM_B2 * v + (1.0 - ADAM_B2) * (g * g)
    m_hat = m2 / (1.0 - ADAM_B1 ** ADAM_STEP)
    v_hat = v2 / (1.0 - ADAM_B2 ** ADAM_STEP)
    delta = -ADAM_LR * (m_hat / (jnp.sqrt(v_hat) + ADAM_EPS) + ADAM_WD * w)
    return delta, m2, v2


def sum_adamw(slots, ws, ms, vs, name):
    n = len(ws)
    halves = 2

    def body(*refs):
        for a in range(n):
            total = refs[a][0].astype(F32)
            for s in range(1, slots[a].shape[0]):
                total = total + refs[a][s].astype(F32)
            delta, m2, v2 = _adamw(refs[n + a][...], total, refs[2 * n + a][...], refs[3 * n + a][...])
            for q, val in enumerate((total, delta, m2, v2)):
                refs[4 * n + 4 * a + q][...] = val

    def rows(w):
        return pl.BlockSpec((w.shape[0] // halves, w.shape[1]), lambda i: (i, 0))

    def slot_rows(p):
        return pl.BlockSpec((p.shape[0], p.shape[1] // halves, p.shape[2]), lambda i: (0, i, 0))

    out = pl.pallas_call(
        body, name=name, grid=(halves,),
        out_shape=[jax.ShapeDtypeStruct(w.shape, F32) for w in ws for _ in range(4)],
        in_specs=[slot_rows(p) for p in slots] + [rows(w) for w in ws] * 3,
        out_specs=[rows(w) for w in ws for _ in range(4)],
        compiler_params=_params(),
    )(*slots, *ws, *ms, *vs)
    return [out[4 * a:4 * a + 4] for a in range(n)]


def adamw_update(ws, gs, ms, vs, name):
    n = len(ws)

    def body(*refs):
        for a in range(n):
            delta, m2, v2 = _adamw(refs[a][...], refs[n + a][...], refs[2 * n + a][...], refs[3 * n + a][...])
            refs[4 * n + 3 * a][...] = delta
            refs[4 * n + 3 * a + 1][...] = m2
            refs[4 * n + 3 * a + 2][...] = v2

    out = pl.pallas_call(
        body, name=name,
        out_shape=[jax.ShapeDtypeStruct(w.shape, F32) for w in ws for _ in range(3)],
        in_specs=[VMEM] * (4 * n), out_specs=[VMEM] * (3 * n),
        compiler_params=_params(),
    )(*ws, *gs, *ms, *vs)
    return [out[3 * a:3 * a + 3] for a in range(n)]


GAIN_ROWS = 8
ROW_POOL_SCALE = 4 * GAIN_ROWS
ROW_SINKS = ROW_POOL_SCALE + 4
ROW_LOSS = ROW_SINKS + 1
ROW_W_POOL = 40
SMALL_ROWS = ROW_W_POOL + 4 * POOL_GROUP_DIM


def grad_w_in_small_reduce(a, b, gains, dpool_scale, dsinks, loss_part, dw_pool, drel_bias):
    t, m = a.shape
    d = b.shape[1]
    r = m // N_DEV
    tt = TOKEN_TILE
    last = t // tt - 1

    def body(a_ref, b_ref, g0, g1, g2, g3, dsc_ref, dsink_ref, loss_ref, dwp_ref, drb_ref, out_ref, total_ref, total_rb_ref,
             acc, stage, gat, gat_rb, g_send, g_recv):
        k = pl.program_id(0)
        x, y, c = lax.axis_index("x"), lax.axis_index("y"), lax.axis_index("c")
        start, finish = _gather_plan([stage, drb_ref], [gat, gat_rb], g_send, g_recv)

        @pl.when(k == 0)
        def _():
            for q, g_ref in enumerate((g0, g1, g2, g3)):
                stage[GAIN_ROWS * q:GAIN_ROWS * (q + 1), :] = g_ref[...]
            stage[ROW_POOL_SCALE:ROW_SINKS, :] = dsc_ref[...]
            stage[ROW_SINKS:ROW_LOSS, :] = dsink_ref[...]
            stage[ROW_LOSS:ROW_LOSS + 1, :] = loss_ref[...]
            stage[ROW_LOSS + 1:ROW_W_POOL, :] = jnp.zeros((ROW_W_POOL - ROW_LOSS - 1, 128), F32)
            stage[ROW_W_POOL:, :] = dwp_ref[...].reshape(4 * POOL_GROUP_DIM, POOL_GROUP_DIM)
            gat[4 * x + 2 * y + c] = stage[...]
            gat_rb[4 * x + 2 * y + c] = drb_ref[...]
            start()
            acc[...] = jnp.zeros_like(acc)

        acc[...] += _dot_tn(a_ref[...], b_ref[...])

        @pl.when(k == last)
        def _():
            blocks = acc[...].reshape(N_CHIP, 2, r, d)
            for chip in range(N_CHIP):
                for core in range(2):
                    out_ref[core, chip] = blocks[chip, core].astype(BF16)
            finish()
            total, total_rb = gat[0], gat_rb[0]
            for s in range(1, N_DEV):
                total, total_rb = total + gat[s], total_rb + gat_rb[s]
            total_ref[...] = total
            total_rb_ref[...] = total_rb

    out_shape = (2, N_CHIP, r, d)
    return pl.pallas_call(
        body, name="grad_w_in", grid=(t // tt,),
        out_shape=[jax.ShapeDtypeStruct(out_shape, BF16), jax.ShapeDtypeStruct((SMALL_ROWS, 128), F32),
                   jax.ShapeDtypeStruct(drel_bias.shape, F32)],
        in_specs=[pl.BlockSpec((tt, m), lambda k: (k, 0)), pl.BlockSpec((tt, d), lambda k: (k, 0))] + [VMEM] * 9,
        out_specs=[pl.BlockSpec(out_shape, lambda k: (0,) * len(out_shape)), VMEM, VMEM],
        scratch_shapes=[pltpu.VMEM((m, d), F32), pltpu.VMEM((SMALL_ROWS, 128), F32),
                        pltpu.VMEM((N_DEV, SMALL_ROWS, 128), F32), pltpu.VMEM((N_DEV, *drel_bias.shape), F32),
                        pltpu.SemaphoreType.DMA((14,)), pltpu.SemaphoreType.DMA((14,))],
        compiler_params=_params(),
    )(a, b, *gains, dpool_scale, dsinks, loss_part, dw_pool, drel_bias)


def reduce_w_in(d_in_t):
    def body(d_in_ref, g_in_ref, pair_got, chip_part, chip_got, p_send, p_recv, x_send, x_recv):
        x, y, c = lax.axis_index("x"), lax.axis_index("y"), lax.axis_index("c")
        my_chip = 2 * x + y
        pair = pltpu.make_async_remote_copy(
            src_ref=d_in_ref.at[1 - c], dst_ref=pair_got, send_sem=p_send, recv_sem=p_recv,
            device_id=(x, y, 1 - c), device_id_type=MESH)
        pair.start()
        pair.wait()
        chip_part[...] = (d_in_ref[c].astype(F32) + pair_got[...].astype(F32)).astype(BF16)
        copies = []
        for k in range(1, N_CHIP):
            px, py = x ^ (k >> 1), y ^ (k & 1)
            copies.append(pltpu.make_async_remote_copy(
                src_ref=chip_part.at[2 * px + py], dst_ref=chip_got.at[my_chip],
                send_sem=x_send.at[k - 1], recv_sem=x_recv.at[k - 1], device_id=(px, py, c), device_id_type=MESH))
        for cp in copies:
            cp.start()
        chip_got[my_chip] = chip_part[my_chip]
        for cp in copies:
            cp.wait()
        g_in = chip_got[0].astype(F32)
        for s in range(1, N_CHIP):
            g_in = g_in + chip_got[s].astype(F32)
        g_in_ref[...] = g_in

    per_core = d_in_t.shape[1:]
    return pl.pallas_call(
        body, name="reduce_w_in",
        out_shape=jax.ShapeDtypeStruct(d_in_t.shape[2:], F32),
        in_specs=[VMEM], out_specs=VMEM,
        scratch_shapes=[pltpu.VMEM(per_core, d_in_t.dtype), pltpu.VMEM(per_core, d_in_t.dtype),
                        pltpu.VMEM(per_core, d_in_t.dtype),
                        pltpu.SemaphoreType.DMA, pltpu.SemaphoreType.DMA,
                        pltpu.SemaphoreType.DMA((3,)), pltpu.SemaphoreType.DMA((3,))],
        compiler_params=_params(),
    )(d_in_t)


def small_adamw(total, total_rb, small_w, small_m, small_v):
    n_small = len(small_w)

    def body(*refs):
        total_ref, rb_ref = refs[:2]
        w_refs, m_refs, v_refs = (refs[2 + k * n_small:2 + (k + 1) * n_small] for k in range(3))
        loss_out = refs[2 + 3 * n_small]
        result = refs[3 + 3 * n_small:]
        total = total_ref[...]
        loss_out[...] = total[ROW_LOSS:ROW_LOSS + 1, :]
        grads = [_as_lanes(total[GAIN_ROWS * k:GAIN_ROWS * (k + 1), :]) for k in range(4)]
        grads.append(_as_lanes(total[ROW_POOL_SCALE:ROW_SINKS, :]))
        grads.append(total[ROW_SINKS:ROW_LOSS, 0:N_Q_HEADS])
        grads.append(total[ROW_W_POOL:, :].reshape(w_refs[6].shape))
        grads.append(rb_ref[...])
        for k in range(n_small):
            delta, m2, v2 = _adamw(w_refs[k][...], grads[k], m_refs[k][...], v_refs[k][...])
            result[4 * k][...] = grads[k]
            result[4 * k + 1][...] = delta
            result[4 * k + 2][...] = m2
            result[4 * k + 3][...] = v2

    out = pl.pallas_call(
        body, name="small_adamw",
        out_shape=[jax.ShapeDtypeStruct((1, 128), F32)] + [jax.ShapeDtypeStruct(w.shape, F32) for w in small_w for _ in range(4)],
        in_specs=[VMEM] * (2 + 3 * n_small), out_specs=[VMEM] * (1 + 4 * n_small),
        compiler_params=_params(),
    )(total, total_rb, *small_w, *small_m, *small_v)
    return out[0], [out[1 + 4 * k:5 + 4 * k] for k in range(n_small)]


def norm_inproj(x, g, w_shard, shard, shard_rows):
    t, d = x.shape
    r = w_shard.shape[0]
    tm = TOKEN_TILE
    nt = t // tm

    def body(x_ref, g_ref, w_shard_ref, shard_ref, proj_ref, h_ref, w_ref, gathered_ref, h_all, w_all, w_sem,
             send_w, recv_w, local_w, bounce_w, send_sems, recv_sems, local_sems, bounce):
        i = pl.program_id(0)
        start_w, finish_w = _gather_plan([w_shard_ref], [w_ref], send_w, recv_w, local_w, [bounce_w])
        start, finish = _gather_plan([shard_ref], [gathered_ref], send_sems, recv_sems, local_sems, [bounce],
                                     [(0, shard.shape[0])])

        @pl.when(i == 0)
        def _():
            start_w()
            start()

        @pl.when(i < nt)
        def _():
            xv = x_ref[...]
            h = ((xv * _rstd(xv)) * g_ref[...]).astype(BF16)
            h_ref[...] = h
            h_all[pl.ds(pl.multiple_of(i * tm, tm), tm), :] = h

        @pl.when(i == nt - 1)
        def _():
            finish_w()
            landed = pltpu.make_async_copy(w_ref, w_all, w_sem)
            landed.start()
            landed.wait()

        @pl.when(i >= nt)
        def _():
            rows = pl.ds(pl.multiple_of((i - nt) * tm, tm), tm)
            proj_ref[...] = _dot_nt(h_all[rows, :], _merge_rows(w_all[...]))

        pl.when(i == 2 * nt - 1)(finish)

    first = lambda i: (jnp.minimum(i, nt - 1), 0)
    return pl.pallas_call(
        body, name="norm_inproj", grid=(2 * nt,),
        out_shape=[jax.ShapeDtypeStruct((t, N_DEV * r), F32), jax.ShapeDtypeStruct((t, d), BF16),
                   jax.ShapeDtypeStruct((N_DEV, r, d), w_shard.dtype),
                   jax.ShapeDtypeStruct((N_DEV, shard_rows, d), shard.dtype)],
        in_specs=[pl.BlockSpec((tm, d), first), pl.BlockSpec((1, d), lambda i: (0, 0)), ANY, ANY],
        out_specs=[pl.BlockSpec((tm, N_DEV * r), lambda i: (jnp.maximum(i - nt, 0), 0)), pl.BlockSpec((tm, d), first),
                   ANY, ANY],
        scratch_shapes=[pltpu.VMEM((t, d), BF16), pltpu.VMEM((N_DEV, r, d), w_shard.dtype), pltpu.SemaphoreType.DMA]
        + _gather_scratch([w_shard]) + _gather_scratch([shard]),
        compiler_params=_params(),
    )(x, g, w_shard, shard)


def bias_band(bucket, in_window, rel_bias):
    def body(bk_ref, win_ref, rb_ref, out_ref):
        bk = bk_ref[...]
        keep = win_ref[...] > 0.5
        for h in range(N_Q_HEADS):
            acc = jnp.zeros(bk.shape, F32)
            for b in range(N_BUCKETS):
                acc = jnp.where(bk == float(b), rb_ref[b, h], acc)
            out_ref[h] = jnp.where(keep, acc, NEG_INF)

    return pl.pallas_call(
        body, name="bias_band",
        out_shape=jax.ShapeDtypeStruct((N_Q_HEADS, BLOCK, 2 * BLOCK), F32),
        in_specs=[VMEM, VMEM, SMEM], out_specs=VMEM,
    )(bucket, in_window, rel_bias)


def bias_band_bwd(bucket, dbias):
    def body(bk_ref, db_ref, out_ref):
        bk = bk_ref[...]
        for h in range(N_Q_HEADS):
            db = db_ref[h]
            for b in range(N_BUCKETS):
                out_ref[h, b] = jnp.sum(jnp.where(bk == float(b), db, 0.0))

    return pl.pallas_call(
        body, name="bias_band_bwd",
        out_shape=jax.ShapeDtypeStruct((N_Q_HEADS, N_BUCKETS), F32),
        in_specs=[VMEM, VMEM], out_specs=SMEM,
    )(bucket, dbias)


def _window_sum(buf_ref, g, w, first):
    cols = slice(g * POOL_GROUP_DIM, (g + 1) * POOL_GROUP_DIM)
    acc = None
    for k in range(w):
        piece = buf_ref[first(k):first(k) + BLOCK, cols]
        acc = piece if acc is None else acc + piece
    return acc


def _inv_count(i, w):
    row = lax.broadcasted_iota(jnp.int32, (BLOCK, 1), 0)
    return 1.0 / jnp.minimum(i * BLOCK + row + 1, w).astype(F32)


def _fill_pool_input(i, ubuf, uc_ref, halo_ref):
    ubuf[0:HALO, :] = jnp.where(i > 0, halo_ref[...], 0.0)
    ubuf[HALO:, :] = uc_ref[...]


def _pooled(i, g, w, ubuf):
    cols = slice(g * POOL_GROUP_DIM, (g + 1) * POOL_GROUP_DIM)
    return _window_sum(ubuf, g, w, lambda k: HALO - k) * _inv_count(i, w) - ubuf[HALO:, cols]


def _head_variants(pair):
    low = lax.broadcasted_iota(jnp.int32, pair.shape, 1) < HEAD_DIM
    swapped = pltpu.roll(pair, HEAD_DIM, 1)
    zero = jnp.zeros_like(pair)
    pick = lambda c, a, b: jnp.where(c, a, b).astype(BF16)
    return [[pick(low, pair, zero), pick(low, zero, swapped)], [pick(low, swapped, zero), pick(low, zero, pair)]]


def _head_probs(i, hq, rows, s_ref, biasm_ref, sinks_ref):
    s = s_ref[hq, rows, :] * ATTN_SCALE + biasm_ref[hq, rows, :]
    col = lax.broadcasted_iota(jnp.int32, s.shape, 1)
    s = jnp.where((i == 0) & (col < BLOCK), NEG_INF, s)
    sink = sinks_ref[0, hq]
    m = jnp.maximum(jnp.max(s, axis=-1, keepdims=True), sink)
    p = jnp.exp(s - m)
    e_sink = jnp.exp(sink - m)
    inv = 1.0 / (jnp.sum(p, axis=-1, keepdims=True) + e_sink)
    return p * inv, e_sink * inv


def _mixer_in_specs(cur, prv):
    return [pl.BlockSpec((BLOCK, 512), lambda i: (cur(i), 0)),
            pl.BlockSpec((HALO, 512), lambda i: (jnp.maximum(cur(i) * (BLOCK // HALO) - 1, 0), 0)),
            pl.BlockSpec((BLOCK, 512), lambda i: (cur(i), 1)),
            pl.BlockSpec((BLOCK, 256), lambda i: (cur(i), 4)),
            pl.BlockSpec((BLOCK, 256), lambda i: (prv(i), 4))]


def _mixer_param_specs():
    return [pl.BlockSpec((N_Q_HEADS, BLOCK, 2 * BLOCK), lambda i: (0, 0, 0)), SMEM,
            pl.BlockSpec((4, POOL_GROUP_DIM, POOL_GROUP_DIM), lambda i: (0, 0, 0)),
            pl.BlockSpec((1, POOL_WIDTH), lambda i: (0, 0))]


def mixers_fwd(proj, biasm, sinks, w_pool, pool_scale, shards):
    t = proj.shape[0]
    nb = t // BLOCK
    ns = len(shards)

    def body(*refs):
        uc_ref, halo_ref, q_ref, kvc_ref, kvp_ref, biasm_ref, sinks_ref, wp_ref, sc_ref = refs[:9]
        shard_refs, out_ref, gathered_refs = refs[9:9 + ns], refs[9 + ns], refs[10 + ns:10 + 2 * ns]
        ubuf, s_all, p_all, send_sems, recv_sems, local_sems = refs[10 + 2 * ns:16 + 2 * ns]
        i = pl.program_id(0)
        start, finish = _gather_plan(shard_refs, gathered_refs, send_sems, recv_sems, local_sems, refs[16 + 2 * ns:])
        pl.when(i == 0)(start)

        _fill_pool_input(i, ubuf, uc_ref, halo_ref)
        for g, w in enumerate(POOL_WINDOWS):
            mixed = _dot(_pooled(i, g, w, ubuf).astype(BF16), wp_ref[g])
            cols = slice(g * POOL_GROUP_DIM, (g + 1) * POOL_GROUP_DIM)
            out_ref[:, cols] = (mixed * sc_ref[:, cols]).astype(BF16)
        kv = jnp.concatenate([kvp_ref[...], kvc_ref[...]], axis=0)
        k_var = _head_variants(kv[:, 0:2 * HEAD_DIM])
        v_var = _head_variants(kv[:, 2 * HEAD_DIM:])
        for hq in range(N_Q_HEADS):
            j, half, h = hq // 2, hq % 2, hq // GQA_GROUP
            q2 = q_ref[:, 2 * HEAD_DIM * j:2 * HEAD_DIM * (j + 1)].astype(BF16)
            s_all[hq] = _dot_nt(q2, k_var[h][half])
        for hq in range(N_Q_HEADS):
            for r in range(0, BLOCK, ROW_CHUNK):
                rows = slice(r, r + ROW_CHUNK)
                probs, _ = _head_probs(i, hq, rows, s_all, biasm_ref, sinks_ref)
                p_all[hq, rows, :] = probs.astype(BF16)
        for j in range(N_Q_HEADS // 2):
            h = 2 * j // GQA_GROUP
            acc = _dot(p_all[2 * j], v_var[h][0]) + _dot(p_all[2 * j + 1], v_var[h][1])
            out_ref[:, POOL_WIDTH + 2 * HEAD_DIM * j:POOL_WIDTH + 2 * HEAD_DIM * (j + 1)] = acc.astype(BF16)

        pl.when(i == nb - 1)(finish)

    return pl.pallas_call(
        body, name="mixers_fwd", grid=(nb,),
        out_shape=[jax.ShapeDtypeStruct((t, 2 * POOL_WIDTH), BF16)]
        + [jax.ShapeDtypeStruct((N_DEV, *sh.shape), sh.dtype) for sh in shards],
        in_specs=_mixer_in_specs(lambda i: i, lambda i: jnp.maximum(i - 1, 0)) + _mixer_param_specs() + [ANY] * ns,
        out_specs=[pl.BlockSpec((BLOCK, 2 * POOL_WIDTH), lambda i: (i, 0))] + [ANY] * ns,
        scratch_shapes=[pltpu.VMEM((HALO + BLOCK, POOL_WIDTH), F32), pltpu.VMEM((N_Q_HEADS, BLOCK, 2 * BLOCK), F32),
                        pltpu.VMEM((N_Q_HEADS, BLOCK, 2 * BLOCK), BF16)] + _gather_scratch(shards),
        compiler_params=_params(),
    )(proj, proj, proj, proj, proj, biasm, sinks, w_pool, pool_scale, *shards)


def outproj_norm(cat, w, x, g, g_next, shard, partial):
    t, d = x.shape
    tm = TOKEN_TILE
    last = t // tm - 1
    rows = [(partial.shape[1] - shard.shape[0], shard.shape[0])]

    def body(c_ref, w_ref, x_ref, g_ref, gn_ref, shard_ref, partial_ref, mix_ref, x1_ref, h2_ref, gathered_ref,
             send_sems, recv_sems, local_sems, bounce):
        i = pl.program_id(0)
        start, finish = _gather_plan([shard_ref], [gathered_ref], send_sems, recv_sems, local_sems, [bounce], rows)
        pl.when(i == 0)(start)
        mix = _dot(c_ref[...], w_ref[...])
        mix_ref[...] = mix
        x1 = x_ref[...] + (mix * _rstd(mix)) * g_ref[...]
        x1_ref[...] = x1
        h2_ref[...] = ((x1 * _rstd(x1)) * gn_ref[...]).astype(BF16)
        pl.when(i == last)(finish)

    row = pl.BlockSpec((tm, d), lambda i: (i, 0))
    gain = pl.BlockSpec((1, d), lambda i: (0, 0))
    return pl.pallas_call(
        body, name="outproj_norm", grid=(t // tm,),
        out_shape=[jax.ShapeDtypeStruct((t, d), F32), jax.ShapeDtypeStruct((t, d), F32), jax.ShapeDtypeStruct((t, d), BF16),
                   jax.ShapeDtypeStruct(partial.shape, partial.dtype)],
        in_specs=[pl.BlockSpec((tm, cat.shape[1]), lambda i: (i, 0)), pl.BlockSpec(w.shape, lambda i: (0, 0)), row, gain, gain,
                  ANY, ANY],
        out_specs=[row, row, row, ANY],
        input_output_aliases={6: 3},
        scratch_shapes=_gather_scratch([shard]),
        compiler_params=_params(),
    )(cat, w, x, g, g_next, shard, partial)


def ffn_up(h, gate_t, up_t, down_shard):
    t, d = h.shape
    n = gate_t.shape[1]
    f = N_DEV * n
    tm, ts = FFN_TOKEN_TILE, FF_SHARDS_PER_TILE
    tn = ts * n
    steps = (f // tn, t // tm)

    def body(h_ref, wg_ref, wu_ref, shard_ref, gate_ref, up_ref, a_ref, gathered_ref,
             send_sems, recv_sems, local_sems, bounce):
        j, i = pl.program_id(0), pl.program_id(1)
        start, finish = _gather_plan([shard_ref], [gathered_ref], send_sems, recv_sems, local_sems, [bounce])
        pl.when((i == 0) & (j == 0))(start)

        hv = h_ref[...]
        gate = _dot_nt(hv, _merge_rows(wg_ref[...]))
        up = _dot_nt(hv, _merge_rows(wu_ref[...]))
        gate_ref[...] = gate.astype(BF16)
        up_ref[...] = up.astype(BF16)
        a_ref[...] = (gate * (1.0 / (1.0 + jnp.exp(-gate))) * up).astype(BF16)

        pl.when((j == steps[0] - 1) & (i == steps[1] - 1))(finish)

    wide = pl.BlockSpec((tm, tn), lambda j, i: (i, j))
    return pl.pallas_call(
        body, name="ffn_up", grid=steps,
        out_shape=[jax.ShapeDtypeStruct((t, f), BF16)] * 3
        + [jax.ShapeDtypeStruct((N_DEV, *down_shard.shape), down_shard.dtype)],
        in_specs=[pl.BlockSpec((tm, d), lambda j, i: (i, 0)),
                  pl.BlockSpec((ts, n, d), lambda j, i: (j, 0, 0)),
                  pl.BlockSpec((ts, n, d), lambda j, i: (j, 0, 0)), ANY],
        out_specs=[wide, wide, wide, ANY],
        scratch_shapes=_gather_scratch([down_shard]),
        compiler_params=_params(),
    )(h, gate_t, up_t, down_shard)


def ffn_down_loss(a, w_down, x1, g, target):
    t, d = x1.shape
    tm = WIDE_K_TOKEN_TILE

    def body(a_ref, w_ref, x_ref, g_ref, t_ref, df_ref, dy_ref, dg_ref, loss_ref):
        @pl.when(pl.program_id(0) == 0)
        def _():
            dg_ref[...] = jnp.zeros_like(dg_ref)
            loss_ref[...] = jnp.zeros_like(loss_ref)

        f = _dot(a_ref[...], _merge_rows(w_ref[...]))
        r = _rstd(f)
        g = g_ref[...]
        err = x_ref[...] + (f * r) * g - t_ref[...]
        loss_ref[...] += 0.5 * jnp.sum(jnp.mean(err * err, axis=-1, keepdims=True))
        dy = err * (1.0 / d)
        dy_ref[...] = dy
        df, dg_rows = _norm_bwd(dy, f, r, g)
        df_ref[...] = df.astype(BF16)
        dg_ref[...] += _as_rows(jnp.sum(dg_rows, axis=0, keepdims=True))

    row = pl.BlockSpec((tm, d), lambda i: (i, 0))
    gain = pl.BlockSpec((1, d), lambda i: (0, 0))
    return pl.pallas_call(
        body, name="ffn_down_loss", grid=(t // tm,),
        out_shape=[jax.ShapeDtypeStruct((t, d), BF16), jax.ShapeDtypeStruct((t, d), F32),
                   jax.ShapeDtypeStruct((d // 128, 128), F32), jax.ShapeDtypeStruct((1, 128), F32)],
        in_specs=[pl.BlockSpec((tm, a.shape[1]), lambda i: (i, 0)), pl.BlockSpec(w_down.shape, lambda i: (0, 0, 0)), row, gain, row],
        out_specs=[row, row, pl.BlockSpec((d // 128, 128), lambda i: (0, 0)), pl.BlockSpec((1, 128), lambda i: (0, 0))],
        compiler_params=_params(),
    )(a, w_down, x1, g, target)


def ffn_down_bwd(df, w_down, gate, up):
    t, d = df.shape
    n = w_down.shape[1]
    f = gate.shape[1]
    tm, ts = FFN_TOKEN_TILE, FF_SHARDS_PER_TILE
    tn = ts * n

    def body(df_ref, w_ref, gate_ref, up_ref, dgate_ref, dup_ref):
        da = _dot_nt(df_ref[...], _merge_rows(w_ref[...]))
        gate = gate_ref[...].astype(F32)
        sig = 1.0 / (1.0 + jnp.exp(-gate))
        dgate_ref[...] = (da * up_ref[...].astype(F32) * (sig * (1.0 + gate * (1.0 - sig)))).astype(BF16)
        dup_ref[...] = (da * (gate * sig)).astype(BF16)

    wide = pl.BlockSpec((tm, tn), lambda j, i: (i, j))
    return pl.pallas_call(
        body, name="ffn_down_bwd", grid=(f // tn, t // tm),
        out_shape=[jax.ShapeDtypeStruct((t, f), BF16)] * 2,
        in_specs=[pl.BlockSpec((tm, d), lambda j, i: (i, 0)), pl.BlockSpec((ts, n, d), lambda j, i: (j, 0, 0)), wide, wide],
        out_specs=[wide, wide],
        compiler_params=_params(),
    )(df, w_down, gate, up)


def grad_rows(a, b, name, by_core=False):
    t, m = a.shape
    d = b.shape[1]
    r = m // N_DEV
    tt = TOKEN_TILE
    last = t // tt - 1
    out_shape = (2, N_CHIP, r, d) if by_core else (N_DEV, r, d)

    def body(a_ref, b_ref, out_ref, acc):
        k = pl.program_id(0)

        @pl.when(k == 0)
        def _():
            acc[...] = jnp.zeros_like(acc)

        acc[...] += _dot_tn(a_ref[...], b_ref[...])

        @pl.when(k == last)
        def _():
            if by_core:
                blocks = acc[...].reshape(N_CHIP, 2, r, d)
                for chip in range(N_CHIP):
                    for core in range(2):
                        out_ref[core, chip] = blocks[chip, core].astype(BF16)
            else:
                out_ref[...] = acc[...].reshape(out_shape).astype(BF16)

    return pl.pallas_call(
        body, name=name, grid=(t // tt,),
        out_shape=jax.ShapeDtypeStruct(out_shape, BF16),
        in_specs=[pl.BlockSpec((tt, m), lambda k: (k, 0)), pl.BlockSpec((tt, d), lambda k: (k, 0))],
        out_specs=pl.BlockSpec(out_shape, lambda k: (0,) * len(out_shape)),
        scratch_shapes=[pltpu.VMEM((m, d), F32)],
        compiler_params=_params(),
    )(a, b)


def grad_ffn(lhs, b, name, pair_parts=()):
    t, f = lhs[0].shape
    d = b.shape[1]
    nw = len(lhs)
    na = len(pair_parts)
    n = f // N_DEV
    tt, ts = TOKEN_TILE, FF_SHARDS_PER_TILE
    tn = ts * n
    steps = (f // tn, t // tt)

    def body(*refs):
        a_refs, b_ref, part_refs = refs[:nw], refs[nw], refs[nw + 1:nw + 1 + na]
        out_refs = refs[nw + 1 + na:2 * nw + 1 + na]
        got_refs = refs[2 * nw + 1 + na:2 * nw + 1 + 2 * na]
        acc = refs[2 * nw + 1 + 2 * na]
        i, k = pl.program_id(0), pl.program_id(1)
        if na:
            start, finish = _pair_plan(part_refs, got_refs, *refs[2 * nw + 2 + 2 * na:])
            pl.when((i == 0) & (k == 0))(start)

        @pl.when(k == 0)
        def _():
            acc[...] = jnp.zeros_like(acc)

        for w in range(nw):
            acc[w] += _dot_tn(a_refs[w][...], b_ref[...])

        @pl.when(k == steps[1] - 1)
        def _():
            for w in range(nw):
                blocks = acc[w].reshape(ts // 2, 2, n, d)
                for chip in range(ts // 2):
                    for core in range(2):
                        out_refs[w][core, chip] = blocks[chip, core].astype(BF16)

        if na:
            pl.when((i == steps[0] - 1) & (k == steps[1] - 1))(finish)

    out = pl.pallas_call(
        body, name=name, grid=steps,
        out_shape=[jax.ShapeDtypeStruct((2, N_CHIP, n, d), BF16)] * nw
        + [jax.ShapeDtypeStruct(p.shape[1:], p.dtype) for p in pair_parts],
        in_specs=[pl.BlockSpec((tt, tn), lambda i, k: (k, i))] * nw + [pl.BlockSpec((tt, d), lambda i, k: (k, 0))] + [ANY] * na,
        out_specs=[pl.BlockSpec((2, ts // 2, n, d), lambda i, k: (0, i, 0, 0))] * nw + [ANY] * na,
        scratch_shapes=[pltpu.VMEM((nw, tn, d), F32)]
        + ([pltpu.SemaphoreType.DMA((na,)), pltpu.SemaphoreType.DMA((na,))] if na else []),
        compiler_params=_params(),
    )(*lhs, b, *pair_parts)
    return out[:nw], out[nw:]


def ffn_up_bwd(dgate, dup, gate_t, up_t, x1, g_ffn, dy, mix, g_mix, chip_parts):
    t, d = x1.shape
    n = gate_t.shape[1]
    f = N_DEV * n
    tm = WIDE_K_TOKEN_TILE
    na = len(chip_parts)
    last = t // tm - 1

    def body(*refs):
        dg_ref, du_ref, wg_ref, wu_ref, x_ref, gf_ref, dy_ref, mix_ref, gm_ref = refs[:9]
        part_refs = refs[9:9 + na]
        dx1_ref, dmix_ref, dgf_ref, dgm_ref = refs[9 + na:13 + na]
        slot_refs = refs[13 + na:13 + 2 * na]
        send_sems, recv_sems, local_sems = refs[13 + 2 * na:16 + 2 * na]
        i = pl.program_id(0)
        start, finish = _chip_exchange_plan(part_refs, slot_refs, send_sems, recv_sems, local_sems, refs[16 + 2 * na:])

        @pl.when(i == 0)
        def _():
            start()
            dgf_ref[...] = jnp.zeros_like(dgf_ref)
            dgm_ref[...] = jnp.zeros_like(dgm_ref)

        dh = _dot(dg_ref[...], _merge_rows(wg_ref[...])) + _dot(du_ref[...], _merge_rows(wu_ref[...]))
        x1 = x_ref[...]
        dx, dgf_rows = _norm_bwd(dh, x1, _rstd(x1), gf_ref[...])
        dx1 = dy_ref[...] + dx
        dx1_ref[...] = dx1
        dgf_ref[...] += _as_rows(jnp.sum(dgf_rows, axis=0, keepdims=True))
        mix = mix_ref[...]
        dmix, dgm_rows = _norm_bwd(dx1, mix, _rstd(mix), gm_ref[...])
        dmix_ref[...] = dmix.astype(BF16)
        dgm_ref[...] += _as_rows(jnp.sum(dgm_rows, axis=0, keepdims=True))
        pl.when(i == last)(finish)

    row = pl.BlockSpec((tm, d), lambda i: (i, 0))
    wide = pl.BlockSpec((tm, f), lambda i: (i, 0))
    gain = pl.BlockSpec((1, d), lambda i: (0, 0))
    gain_rows = pl.BlockSpec((d // 128, 128), lambda i: (0, 0))
    whole = pl.BlockSpec((N_DEV, n, d), lambda i: (0, 0, 0), pipeline_mode=pl.Buffered(1))
    out = pl.pallas_call(
        body, name="ffn_up_bwd", grid=(t // tm,),
        out_shape=[jax.ShapeDtypeStruct((t, d), F32), jax.ShapeDtypeStruct((t, d), BF16),
                   jax.ShapeDtypeStruct((d // 128, 128), F32), jax.ShapeDtypeStruct((d // 128, 128), F32)]
        + [jax.ShapeDtypeStruct(p.shape, p.dtype) for p in chip_parts],
        in_specs=[wide, wide, whole, whole, row, gain, row, row, gain] + [ANY] * na,
        out_specs=[row, row, gain_rows, gain_rows] + [ANY] * na,
        scratch_shapes=_chip_exchange_scratch(chip_parts),
        compiler_params=_params(),
    )(dgate, dup, gate_t, up_t, x1, g_ffn, dy, mix, g_mix, *chip_parts)
    return out[:4], out[4:]


def outproj_bwd(dmix, w_out):
    t, d = dmix.shape
    tm = TOKEN_TILE

    def body(dm_ref, w_ref, out_ref):
        out_ref[...] = _dot_nt(dm_ref[...], w_ref[...])

    return pl.pallas_call(
        body, name="outproj_bwd", grid=(t // tm,),
        out_shape=jax.ShapeDtypeStruct((t, w_out.shape[0]), F32),
        in_specs=[pl.BlockSpec((tm, d), lambda i: (i, 0)), pl.BlockSpec(w_out.shape, lambda i: (0, 0))],
        out_specs=pl.BlockSpec((tm, w_out.shape[0]), lambda i: (i, 0)),
        compiler_params=_params(),
    )(dmix, w_out)


def mixers_bwd(proj, dcat, biasm, sinks, w_pool, pool_scale, ffn_parts):
    t = proj.shape[0]
    nb = t // BLOCK
    na = len(ffn_parts)

    def body(*refs):
        (uc_ref, halo_ref, q_ref, kvc_ref, kvp_ref, dcat_ref, biasm_ref, sinks_ref, wp_ref, sc_ref) = refs[:10]
        part_refs = refs[10:10 + na]
        dproj_ref, dbias_ref, dsink_ref, dwp_ref, dsc_ref = refs[10 + na:15 + na]
        slot_refs = refs[15 + na:15 + 2 * na]
        ubuf, dbuf, c_u, c_q, c_kv, s_all, dp_all, ds_all, p_all = refs[15 + 2 * na:24 + 2 * na]
        send_sems, recv_sems, local_sems = refs[24 + 2 * na:27 + 2 * na]
        bounce = refs[27 + 2 * na:]
        i = pl.program_id(0)
        lane = lax.broadcasted_iota(jnp.int32, (1, 128), 1)
        start, finish = _chip_exchange_plan(part_refs, slot_refs, send_sems, recv_sems, local_sems, bounce)

        @pl.when(i == 0)
        def _():
            start()
            dbias_ref[...] = jnp.zeros_like(dbias_ref)
            dwp_ref[...] = jnp.zeros_like(dwp_ref)
            dsc_ref[...] = jnp.zeros_like(dsc_ref)
            dsink_ref[...] = jnp.zeros_like(dsink_ref)
            dbuf[...] = jnp.zeros_like(dbuf)
            c_u[...] = jnp.zeros_like(c_u)
            c_q[...] = jnp.zeros_like(c_q)
            c_kv[...] = jnp.zeros_like(c_kv)

        @pl.when(i < nb)
        def _():
            _fill_pool_input(i, ubuf, uc_ref, halo_ref)
            for g, w in enumerate(POOL_WINDOWS):
                cols = slice(g * POOL_GROUP_DIM, (g + 1) * POOL_GROUP_DIM)
                pooled = _pooled(i, g, w, ubuf).astype(BF16)
                mixed = _dot(pooled, wp_ref[g])
                dout = dcat_ref[:, cols]
                dsc_ref[g:g + 1, :] += jnp.sum(dout * mixed, axis=0, keepdims=True)
                dmixed = (dout * sc_ref[:, cols]).astype(BF16)
                dwp_ref[g] += _dot_tn(pooled, dmixed)
                dpooled = _dot_nt(dmixed, wp_ref[g])
                scaled = dpooled * _inv_count(i, w)
                dbuf[BLOCK:, cols] = scaled[0:HALO]
                dproj_ref[:, cols] = (_window_sum(dbuf, g, w, lambda k: k) + c_u[:, cols]).astype(BF16)
                dbuf[0:BLOCK, cols] = scaled
                c_u[:, cols] = -dpooled

            kv = jnp.concatenate([kvp_ref[...], kvc_ref[...]], axis=0)
            k_var = _head_variants(kv[:, 0:2 * HEAD_DIM])
            v_var = _head_variants(kv[:, 2 * HEAD_DIM:])
            q2s = [q_ref[:, 2 * HEAD_DIM * j:2 * HEAD_DIM * (j + 1)].astype(BF16) for j in range(N_Q_HEADS // 2)]
            do2s = [dcat_ref[:, POOL_WIDTH + 2 * HEAD_DIM * j:POOL_WIDTH + 2 * HEAD_DIM * (j + 1)].astype(BF16)
                    for j in range(N_Q_HEADS // 2)]
            slot = lambda hq: 4 * (hq // GQA_GROUP) + 2 * (hq % 2) + (hq % GQA_GROUP) // 2
            for hq in range(N_Q_HEADS):
                j, half, h = hq // 2, hq % 2, hq // GQA_GROUP
                s_all[hq] = _dot_nt(q2s[j], k_var[h][half])
                dp_all[hq] = _dot_nt(do2s[j], v_var[h][half])
            dsink_row = jnp.zeros((1, 128), F32)
            for hq in range(N_Q_HEADS):
                dsink = 0.0
                for r in range(0, BLOCK, ROW_CHUNK):
                    rows = slice(r, r + ROW_CHUNK)
                    probs, p_sink = _head_probs(i, hq, rows, s_all, biasm_ref, sinks_ref)
                    dp = dp_all[hq, rows, :]
                    delta = jnp.sum(probs * dp, axis=-1, keepdims=True)
                    ds = probs * (dp - delta)
                    dbias_ref[hq, rows, :] += ds
                    dsink = dsink + jnp.sum(p_sink * delta)
                    ds_all[slot(hq), rows, :] = (ds * ATTN_SCALE).astype(BF16)
                    p_all[slot(hq), rows, :] = probs.astype(BF16)
                dsink_row = dsink_row - jnp.where(lane == hq, dsink, 0.0)
            dsink_ref[...] += dsink_row
            dq2 = [None] * (N_Q_HEADS // 2)
            for hq in range(N_Q_HEADS):
                j, half, h = hq // 2, hq % 2, hq // GQA_GROUP
                dq = _dot(ds_all[slot(hq)], k_var[h][half])
                dq2[j] = dq if dq2[j] is None else dq2[j] + dq
            low = lax.broadcasted_iota(jnp.int32, (2 * BLOCK, 2 * HEAD_DIM), 1) < HEAD_DIM
            dk_half, dv_half = [[None, None], [None, None]], [[None, None], [None, None]]
            for h in range(N_KV_HEADS):
                for half in range(2):
                    heads = [hq for hq in range(GQA_GROUP * h, GQA_GROUP * (h + 1)) if hq % 2 == half]
                    base = slot(heads[0])
                    q_rows = jnp.concatenate([q2s[hq // 2] for hq in heads], axis=0)
                    do_rows = jnp.concatenate([do2s[hq // 2] for hq in heads], axis=0)
                    dk_half[h][half] = _dot_tn(_merge_rows(ds_all[base:base + 2]), q_rows)
                    dv_half[h][half] = _dot_tn(_merge_rows(p_all[base:base + 2]), do_rows)

            def pair_of(halves):
                return jnp.where(low, halves[0][0] + pltpu.roll(halves[0][1], HEAD_DIM, 1),
                                 halves[1][1] + pltpu.roll(halves[1][0], HEAD_DIM, 1))

            dkv = jnp.concatenate([pair_of(dk_half), pair_of(dv_half)], axis=1)
            dproj_ref[:, POOL_WIDTH:2 * POOL_WIDTH] = c_q[...].astype(BF16)
            dproj_ref[:, 2 * POOL_WIDTH:] = (c_kv[...] + dkv[0:BLOCK]).astype(BF16)
            c_q[...] = jnp.concatenate(dq2, axis=1)
            c_kv[...] = dkv[BLOCK:]

        @pl.when(i == nb)
        def _():
            dbuf[BLOCK:, :] = jnp.zeros((HALO, POOL_WIDTH), F32)
            for g, w in enumerate(POOL_WINDOWS):
                cols = slice(g * POOL_GROUP_DIM, (g + 1) * POOL_GROUP_DIM)
                dproj_ref[:, cols] = (_window_sum(dbuf, g, w, lambda k: k) + c_u[:, cols]).astype(BF16)
            dproj_ref[:, POOL_WIDTH:2 * POOL_WIDTH] = c_q[...].astype(BF16)
            dproj_ref[:, 2 * POOL_WIDTH:] = c_kv[...].astype(BF16)
            finish()

    cur = lambda i: jnp.minimum(i, nb - 1)
    prv = lambda i: jnp.maximum(jnp.minimum(i, nb - 1) - 1, 0)
    out = pl.pallas_call(
        body, name="mixers_bwd", grid=(nb + 1,),
        out_shape=[jax.ShapeDtypeStruct((t, proj.shape[1]), BF16),
                   jax.ShapeDtypeStruct((N_Q_HEADS, BLOCK, 2 * BLOCK), F32),
                   jax.ShapeDtypeStruct((1, 128), F32),
                   jax.ShapeDtypeStruct((4, POOL_GROUP_DIM, POOL_GROUP_DIM), F32),
                   jax.ShapeDtypeStruct((len(POOL_WINDOWS), POOL_GROUP_DIM), F32)]
        + [jax.ShapeDtypeStruct(p.shape, p.dtype) for p in ffn_parts],
        in_specs=_mixer_in_specs(cur, prv) + [pl.BlockSpec((BLOCK, 2 * POOL_WIDTH), lambda i: (cur(i), 0))]
        + _mixer_param_specs() + [ANY] * na,
        out_specs=[pl.BlockSpec((BLOCK, proj.shape[1]), lambda i: (jnp.maximum(i - 1, 0), 0)),
                   pl.BlockSpec((N_Q_HEADS, BLOCK, 2 * BLOCK), lambda i: (0, 0, 0)),
                   pl.BlockSpec((1, 128), lambda i: (0, 0)),
                   pl.BlockSpec((4, POOL_GROUP_DIM, POOL_GROUP_DIM), lambda i: (0, 0, 0)),
                   pl.BlockSpec((len(POOL_WINDOWS), POOL_GROUP_DIM), lambda i: (0, 0))] + [ANY] * na,
        scratch_shapes=[pltpu.VMEM((HALO + BLOCK, POOL_WIDTH), F32), pltpu.VMEM((BLOCK + HALO, POOL_WIDTH), F32),
                        pltpu.VMEM((BLOCK, POOL_WIDTH), F32), pltpu.VMEM((BLOCK, POOL_WIDTH), F32),
                        pltpu.VMEM((BLOCK, 256), F32),
                        pltpu.VMEM((N_Q_HEADS, BLOCK, 2 * BLOCK), F32), pltpu.VMEM((N_Q_HEADS, BLOCK, 2 * BLOCK), F32),
                        pltpu.VMEM((N_Q_HEADS, BLOCK, 2 * BLOCK), BF16), pltpu.VMEM((N_Q_HEADS, BLOCK, 2 * BLOCK), BF16)]
        + _chip_exchange_scratch(ffn_parts),
        compiler_params=_params(),
    )(proj, proj, proj, proj, proj, dcat, biasm, sinks, w_pool, pool_scale, *ffn_parts)
    return out[:5], out[5:]


def inproj_bwd(dproj, w_in_t, x, g, dx1):
    t, d = x.shape
    n = dproj.shape[1]
    tm = TOKEN_TILE

    def body(dp_ref, w_ref, x_ref, g_ref, dx1_ref, dx_ref, dg_ref):
        @pl.when(pl.program_id(0) == 0)
        def _():
            dg_ref[...] = jnp.zeros_like(dg_ref)

        dh = _dot(dp_ref[...], w_ref[...])
        xv = x_ref[...]
        dx, dg_rows = _norm_bwd(dh, xv, _rstd(xv), g_ref[...])
        dx_ref[...] = dx1_ref[...] + dx
        dg_ref[...] += _as_rows(jnp.sum(dg_rows, axis=0, keepdims=True))

    row = pl.BlockSpec((tm, d), lambda i: (i, 0))
    gain = pl.BlockSpec((1, d), lambda i: (0, 0))
    return pl.pallas_call(
        body, name="inproj_bwd", grid=(t // tm,),
        out_shape=[jax.ShapeDtypeStruct((t, d), F32), jax.ShapeDtypeStruct((d // 128, 128), F32)],
        in_specs=[pl.BlockSpec((tm, n), lambda i: (i, 0)), pl.BlockSpec(w_in_t.shape, lambda i: (0, 0)), row, gain, row],
        out_specs=[row, pl.BlockSpec((d // 128, 128), lambda i: (0, 0))],
        compiler_params=_params(),
    )(dproj, w_in_t, x, g, dx1)


def _bucket_band():
    qi = jnp.arange(BLOCK)[:, None]
    kj = jnp.arange(2 * BLOCK)[None, :]
    dist = qi + BLOCK - kj
    n = jnp.maximum(dist, 0)
    nf = jnp.maximum(n, 1).astype(F32)
    large = MAX_EXACT + (jnp.log(nf / MAX_EXACT) / np.float32(np.log(MAX_DISTANCE / MAX_EXACT))
                         * (N_BUCKETS - MAX_EXACT)).astype(jnp.int32)
    large = jnp.minimum(large, N_BUCKETS - 1)
    bucket = jnp.where(n < MAX_EXACT, n, large)
    in_window = (dist >= 0) & (dist < BLOCK)
    return bucket.astype(F32), in_window.astype(F32)


def kernel(x, g_pre_mix, w_in, w_pool, pool_scale, rel_bias, sinks, w_out, g_post_mix, g_pre_ffn, w_gate, w_up, w_down, g_post_ffn, loss_target, m_g_pre_mix, m_w_in, m_w_pool, m_pool_scale, m_rel_bias, m_sinks, m_w_out, m_g_post_mix, m_g_pre_ffn, m_w_gate, m_w_up, m_w_down, m_g_post_ffn, v_g_pre_mix, v_w_in, v_w_pool, v_pool_scale, v_rel_bias, v_sinks, v_w_out, v_g_post_mix, v_g_pre_ffn, v_w_gate, v_w_up, v_w_down, v_g_post_ffn):
    d = x.shape[-1]
    xs, target = x[0], loss_target[0]

    w_in_ts = w_in[0].T.astype(BF16)
    w_out_s = w_out[0].astype(BF16)
    gate_ts = w_gate[0].T.astype(BF16)
    up_ts = w_up[0].T.astype(BF16)
    w_down_s = w_down[0].astype(BF16)

    bucket, in_window = _bucket_band()
    biasm = bias_band(bucket, in_window, rel_bias)
    w_pool_b = w_pool[0].astype(BF16)
    half = up_ts.shape[0] // 2
    proj, h1, w_in_t, up_t = norm_inproj(xs, g_pre_mix, w_in_ts, up_ts[:half], up_ts.shape[0])
    w_in_t = w_in_t.reshape(-1, d)
    cat, gate_t, w_out_f = mixers_fwd(proj, biasm, sinks, w_pool_b, pool_scale, [gate_ts, w_out_s])
    w_out_f = w_out_f.reshape(-1, d)
    mix, x1, h2, up_t = outproj_norm(cat, w_out_f, xs, g_post_mix, g_pre_ffn, up_ts[half:], up_t)
    gate, up, act, w_down_f = ffn_up(h2, gate_t, up_t, w_down_s)
    df, dy, dg_post_ffn, loss_part = ffn_down_loss(act, w_down_f, x1, g_post_ffn, target)

    def pair_sum(parts, tag):
        return pair_add(parts, pair_exchange(parts, "pair_exchange_" + tag), "pair_add_" + tag)

    dgate, dup = ffn_down_bwd(df, w_down_f, gate, up)
    (d_gate, d_up), _ = grad_ffn([dgate, dup], h2, "grad_w_gate_up")
    (d_down,), got_gate_up = grad_ffn([act], df, "grad_w_down", [d_gate, d_up])
    q_gate, q_up, q_down = pair_add(
        [d_gate, d_up, d_down], [*got_gate_up, *pair_exchange([d_down], "pair_exchange_down")], "pair_add_ffn")
    (dx1, dmix, dg_pre_ffn, dg_post_mix), gate_down_slots = ffn_up_bwd(
        dgate, dup, gate_t, up_t, x1, g_pre_ffn, dy, mix, g_post_mix, [q_gate, q_down])
    dcat = outproj_bwd(dmix, w_out_f)
    d_out = grad_rows(cat, dmix, "grad_w_out", by_core=True)
    q_out, = pair_sum([d_out], "out")
    (dproj, dbias, dsinks, dw_pool, dpool_scale), up_out_slots = mixers_bwd(
        proj, dcat, biasm, sinks, w_pool_b, pool_scale, [q_up, q_out])
    drel_bias = bias_band_bwd(bucket, dbias)
    grad_x, dg_pre_mix = inproj_bwd(dproj, w_in_t, xs, g_pre_mix, dx1)

    small_w = [g_pre_mix, g_post_mix, g_pre_ffn, g_post_ffn, pool_scale, sinks, w_pool, rel_bias.T]
    small_m = [m_g_pre_mix, m_g_post_mix, m_g_pre_ffn, m_g_post_ffn, m_pool_scale, m_sinks, m_w_pool, m_rel_bias.T]
    small_v = [v_g_pre_mix, v_g_post_mix, v_g_pre_ffn, v_g_post_ffn, v_pool_scale, v_sinks, v_w_pool, v_rel_bias.T]
    d_in_t, total, total_rb = grad_w_in_small_reduce(
        dproj, h1, [dg_pre_mix, dg_post_mix, dg_pre_ffn, dg_post_ffn], dpool_scale, dsinks, loss_part, dw_pool, drel_bias)
    g_in_t = reduce_w_in(d_in_t)
    loss_row, sm = small_adamw(total, total_rb, small_w, small_m, small_v)
    sm[7] = [r.T for r in sm[7]]
    big_w = [w_in[0].T, w_out[0], w_gate[0].T, w_up[0].T, w_down[0]]
    big_m = [m_w_in[0].T, m_w_out[0], m_w_gate[0].T, m_w_up[0].T, m_w_down[0]]
    big_v = [v_w_in[0].T, v_w_out[0], v_w_gate[0].T, v_w_up[0].T, v_w_down[0]]
    (gate_slots, down_slots), (up_slots, out_slots) = gate_down_slots, up_out_slots
    upd = sum_adamw([out_slots, gate_slots, up_slots, down_slots], big_w[1:], big_m[1:], big_v[1:], "sum_adamw")
    upd = [[g_in_t, *adamw_update(big_w[:1], [g_in_t], big_m[:1], big_v[:1], "adamw_in")[0]], *upd]
    back = lambda k, a: (a.T if k in (0, 2, 3) else a)[None]
    big = [[back(k, u) for u in upd[k]] for k in range(5)]

    def ordered(kind):
        s, b = [p[kind] for p in sm], [p[kind] for p in big]
        return [s[0], b[0], s[6], s[4], s[7], s[5], b[1], s[1], s[2], b[2], b[3], b[4], s[3]]

    return (loss_row[0, 0], grad_x[None], *ordered(0), *ordered(1), *ordered(2), *ordered(3))
```

```python
import numpy as np
import jax
import jax.numpy as jnp
from jax import lax
from jax.experimental import pallas as pl
from jax.experimental.pallas import tpu as pltpu

F32 = jnp.float32
BF16 = jnp.bfloat16

N_DEV = 8
N_CHIP = 4
POOL_WIDTH = 512
POOL_WINDOWS = (2, 4, 8, 16)
POOL_GROUP_DIM = 128
HEAD_DIM = 64
N_Q_HEADS = 8
N_KV_HEADS = 2
GQA_GROUP = 4
BLOCK = 128
HALO = 16
ROW_CHUNK = 32
N_BUCKETS = 32
MAX_EXACT = 16
MAX_DISTANCE = 128
EPS = 1e-6
NEG_INF = -1e30
ATTN_SCALE = float(1.0 / np.sqrt(np.float32(HEAD_DIM)))

ADAM_LR = 0.001
ADAM_B1 = 0.9
ADAM_B2 = 0.999
ADAM_EPS = 1e-08
ADAM_WD = 0.01
ADAM_STEP = 10

TOKEN_TILE = 1024
WIDE_K_TOKEN_TILE = 512
FFN_TOKEN_TILE = 1024
FF_SHARDS_PER_TILE = 4
VMEM_LIMIT = 56 * 1024 * 1024
MESH = pl.DeviceIdType.MESH
ANY = pl.BlockSpec(memory_space=pl.ANY)
VMEM = pl.BlockSpec(memory_space=pltpu.VMEM)
SMEM = pl.BlockSpec(memory_space=pltpu.SMEM)


def _params(**kw):
    return pltpu.CompilerParams(vmem_limit_bytes=VMEM_LIMIT, **kw)


def _dot(a, b):
    return jnp.dot(a, b, preferred_element_type=F32)


def _dot_nt(a, b):
    return lax.dot_general(a, b, (((1,), (1,)), ((), ())), preferred_element_type=F32)


def _dot_tn(a, b):
    return lax.dot_general(a, b, (((0,), (0,)), ((), ())), preferred_element_type=F32)


def _rstd(v):
    return lax.rsqrt(jnp.mean(v * v, axis=-1, keepdims=True) + EPS)


def _norm_bwd(dout, v, r, g):
    vn = v * r
    dn = dout * g
    dv = r * (dn - vn * jnp.mean(dn * vn, axis=-1, keepdims=True))
    return dv, dout * vn


def _as_rows(v):
    return jnp.concatenate([v[:, k:k + 128] for k in range(0, v.shape[1], 128)], axis=0)


def _as_lanes(rows):
    return jnp.concatenate([rows[k:k + 1, :] for k in range(rows.shape[0])], axis=1)


def _merge_rows(value):
    s, r, c_ = value.shape
    return value.reshape(s * r, c_)


def _gather_plan(srcs, outs, send_sems, recv_sems, local_sems=None, bounce=None, rows=None):
    n = len(srcs)
    x, y, c = lax.axis_index("x"), lax.axis_index("y"), lax.axis_index("c")
    me, sibling = (x, y, c), (x, y, 1 - c)
    chips = [(1 - x, y), (x, 1 - y), (1 - x, 1 - y)]

    def slot(a, px, py, pc):
        whole = outs[a].at[4 * px + 2 * py + pc]
        return whole if rows is None or rows[a] is None else whole.at[pl.ds(*rows[a])]

    def copy(a, k, block, to, from_src=False):
        return pltpu.make_async_remote_copy(
            src_ref=srcs[a] if from_src else slot(a, *block), dst_ref=slot(a, *block),
            send_sem=send_sems.at[k * n + a], recv_sem=recv_sems.at[k * n + a], device_id=to, device_id_type=MESH)

    def own_in(a):
        return pltpu.make_async_copy(srcs[a], bounce[a], local_sems.at[a])

    def own_out(a):
        return pltpu.make_async_copy(bounce[a], slot(a, *me), local_sems.at[a])

    def first(a):
        return [copy(a, 0, me, sibling, True)] + [copy(a, 1 + j, me, (*chip, c), True) for j, chip in enumerate(chips)]

    def passed(a, j):
        return copy(a, 4 + j, (*chips[j], c), sibling)

    def start():
        for a in range(n):
            if bounce is not None:
                own_in(a).start()
            for cp in first(a):
                cp.start()

    def finish():
        if bounce is not None:
            for a in range(n):
                own_in(a).wait()
                own_out(a).start()
        for j, chip in enumerate(chips):
            for a in range(n):
                copy(a, 1 + j, (*chip, c), me).wait_recv()
                passed(a, j).start()
        for a in range(n):
            copy(a, 0, sibling, me).wait_recv()
            for j, chip in enumerate(chips):
                copy(a, 4 + j, (*chip, 1 - c), me).wait_recv()
        for a in range(n):
            for cp in first(a) + [passed(a, j) for j in range(3)]:
                cp.wait_send()
            if bounce is not None:
                own_out(a).wait()

    return start, finish


def _gather_scratch(shards):
    n = len(shards)
    return [pltpu.SemaphoreType.DMA((7 * n,)), pltpu.SemaphoreType.DMA((7 * n,)), pltpu.SemaphoreType.DMA((n,))] \
        + [pltpu.VMEM(s.shape, s.dtype) for s in shards]


def _chip_exchange_plan(srcs, outs, send_sems, recv_sems, local_sems, bounce):
    n = len(srcs)
    x, y, c = lax.axis_index("x"), lax.axis_index("y"), lax.axis_index("c")
    my_chip = 2 * x + y

    def copies():
        out = []
        for a in range(n):
            for k in range(1, N_CHIP):
                px, py = x ^ (k >> 1), y ^ (k & 1)
                out.append(pltpu.make_async_remote_copy(
                    src_ref=srcs[a].at[2 * px + py], dst_ref=outs[a].at[my_chip],
                    send_sem=send_sems.at[(k - 1) * n + a], recv_sem=recv_sems.at[(k - 1) * n + a],
                    device_id=(px, py, c), device_id_type=MESH))
        return out

    def own_in(a):
        return pltpu.make_async_copy(srcs[a].at[my_chip], bounce[a], local_sems.at[a])

    def own_out(a):
        return pltpu.make_async_copy(bounce[a], outs[a].at[my_chip], local_sems.at[a])

    def start():
        for a in range(n):
            own_in(a).start()
        for cp in copies():
            cp.start()

    def finish():
        for a in range(n):
            own_in(a).wait()
            own_out(a).start()
        for cp in copies():
            cp.wait()
        for a in range(n):
            own_out(a).wait()

    return start, finish


def _chip_exchange_scratch(parts):
    n = len(parts)
    return [pltpu.SemaphoreType.DMA((3 * n,)), pltpu.SemaphoreType.DMA((3 * n,)), pltpu.SemaphoreType.DMA((n,))] \
        + [pltpu.VMEM(p.shape[1:], p.dtype) for p in parts]


def _pair_plan(srcs, outs, send_sems, recv_sems):
    x, y, c = lax.axis_index("x"), lax.axis_index("y"), lax.axis_index("c")

    def copies():
        return [pltpu.make_async_remote_copy(
            src_ref=srcs[a].at[1 - c], dst_ref=outs[a], send_sem=send_sems.at[a], recv_sem=recv_sems.at[a],
            device_id=(x, y, 1 - c), device_id_type=MESH) for a in range(len(srcs))]

    def start():
        for cp in copies():
            cp.start()

    def finish():
        for cp in copies():
            cp.wait()

    return start, finish


def pair_exchange(parts, name):
    n = len(parts)

    def body(*refs):
        start, finish = _pair_plan(refs[:n], refs[n:2 * n], *refs[2 * n:])
        start()
        finish()

    return pl.pallas_call(
        body, name=name, out_shape=[jax.ShapeDtypeStruct(p.shape[1:], p.dtype) for p in parts],
        in_specs=[ANY] * n, out_specs=[ANY] * n,
        scratch_shapes=[pltpu.SemaphoreType.DMA((n,)), pltpu.SemaphoreType.DMA((n,))],
    )(*parts)


def pair_add(parts, got, name, split_last=False):
    n = len(parts)

    def body(core_ref, *refs):
        for a in range(n):
            total = (refs[a][...].astype(F32) + refs[n + a][...].astype(F32)).astype(BF16)
            if split_last and a == n - 1:
                half = total.shape[1] // 2
                refs[2 * n + a][...] = total[:, :half]
                refs[2 * n + a + 1][...] = total[:, half:]
            else:
                refs[2 * n + a][...] = total

    def own(p):
        zeros = (0,) * (p.ndim - 2)
        return pl.BlockSpec((None, 1, *p.shape[2:]), lambda i, core: (core[0], i, *zeros))

    def plain(p):
        zeros = (0,) * (p.ndim - 1)
        return pl.BlockSpec((1, *p.shape[1:]), lambda i, core: (i, *zeros))

    outs = [jax.ShapeDtypeStruct(p.shape, BF16) for p in got]
    if split_last:
        s0, r, d = got[-1].shape
        outs[-1:] = [jax.ShapeDtypeStruct((s0, r // 2, d), BF16)] * 2
    core = lax.axis_index("c").astype(jnp.int32).reshape(1)
    return pl.pallas_call(
        body, name=name,
        grid_spec=pltpu.PrefetchScalarGridSpec(
            num_scalar_prefetch=1, grid=(got[0].shape[0],),
            in_specs=[own(p) for p in parts] + [plain(p) for p in got], out_specs=[plain(p) for p in outs]),
        out_shape=outs,
        compiler_params=_params(),
    )(core, *parts, *got)


def _adamw(w, g, m, v):
    m2 = ADAM_B1 * m + (1.0 - ADAM_B1) * g
    v2 = ADAM_B2 * v + (1.0 - ADAM_B2) * (g * g)
    m_hat = m2 / (1.0 - ADAM_B1 ** ADAM_STEP)
    v_hat = v2 / (1.0 - ADAM_B2 ** ADAM_STEP)
    delta = -ADAM_LR * (m_hat / (jnp.sqrt(v_hat) + ADAM_EPS) + ADAM_WD * w)
    return delta, m2, v2


def sum_adamw(slots, ws, ms, vs, name):
    n = len(ws)
    halves = 2
    flat = [p for entry in slots for p in (entry if isinstance(entry, (tuple, list)) else (entry,))]
    first = np.cumsum([0] + [len(e) if isinstance(e, (tuple, list)) else 1 for e in slots])
    nf = len(flat)

    def body(*refs):
        i = pl.program_id(0)

        def slot_sum(ref):
            total = ref[0].astype(F32)
            for s in range(1, ref.shape[0]):
                total = total + ref[s].astype(F32)
            return total

        for a in range(n):
            total = slot_sum(refs[first[a]])
            for h in range(1, first[a + 1] - first[a]):
                total = jnp.where(i == h, slot_sum(refs[first[a] + h]), total)
            delta, m2, v2 = _adamw(refs[nf + a][...], total, refs[nf + n + a][...], refs[nf + 2 * n + a][...])
            for q, val in enumerate((total, delta, m2, v2)):
                refs[nf + 3 * n + 4 * a + q][...] = val

    def rows(w):
        return pl.BlockSpec((w.shape[0] // halves, w.shape[1]), lambda i: (i, 0))

    def slot_spec(entry):
        if isinstance(entry, (tuple, list)):
            return [pl.BlockSpec(p.shape, lambda i: (0, 0, 0)) for p in entry]
        return [pl.BlockSpec((entry.shape[0], entry.shape[1] // halves, entry.shape[2]), lambda i: (0, i, 0))]

    out = pl.pallas_call(
        body, name=name, grid=(halves,),
        out_shape=[jax.ShapeDtypeStruct(w.shape, F32) for w in ws for _ in range(4)],
        in_specs=[sp for e in slots for sp in slot_spec(e)] + [rows(w) for w in ws] * 3,
        out_specs=[rows(w) for w in ws for _ in range(4)],
        compiler_params=_params(),
    )(*flat, *ws, *ms, *vs)
    return [out[4 * a:4 * a + 4] for a in range(n)]


def adamw_update(ws, gs, ms, vs, name):
    n = len(ws)

    def body(*refs):
        for a in range(n):
            delta, m2, v2 = _adamw(refs[a][...], refs[n + a][...], refs[2 * n + a][...], refs[3 * n + a][...])
            refs[4 * n + 3 * a][...] = delta
            refs[4 * n + 3 * a + 1][...] = m2
            refs[4 * n + 3 * a + 2][...] = v2

    out = pl.pallas_call(
        body, name=name,
        out_shape=[jax.ShapeDtypeStruct(w.shape, F32) for w in ws for _ in range(3)],
        in_specs=[VMEM] * (4 * n), out_specs=[VMEM] * (3 * n),
        compiler_params=_params(),
    )(*ws, *gs, *ms, *vs)
    return [out[3 * a:3 * a + 3] for a in range(n)]


GAIN_ROWS = 8
ROW_POOL_SCALE = 4 * GAIN_ROWS
ROW_SINKS = ROW_POOL_SCALE + 4
ROW_LOSS = ROW_SINKS + 1
ROW_W_POOL = 40
SMALL_ROWS = ROW_W_POOL + 4 * POOL_GROUP_DIM


def grad_w_in_small_reduce(a, b, gains, dpool_scale, dsinks, loss_part, dw_pool, drel_bias):
    t, m = a.shape
    d = b.shape[1]
    r = m // N_DEV
    tt = TOKEN_TILE
    last = t // tt - 1

    def body(a_ref, b_ref, g0, g1, g2, g3, dsc_ref, dsink_ref, loss_ref, dwp_ref, drb_ref, out_ref, total_ref, total_rb_ref,
             acc, stage, gat, gat_rb, g_send, g_recv):
        k = pl.program_id(0)
        x, y, c = lax.axis_index("x"), lax.axis_index("y"), lax.axis_index("c")
        start, finish = _gather_plan([stage, drb_ref], [gat, gat_rb], g_send, g_recv)

        @pl.when(k == 0)
        def _():
            for q, g_ref in enumerate((g0, g1, g2, g3)):
                stage[GAIN_ROWS * q:GAIN_ROWS * (q + 1), :] = g_ref[...]
            stage[ROW_POOL_SCALE:ROW_SINKS, :] = dsc_ref[...]
            stage[ROW_SINKS:ROW_LOSS, :] = dsink_ref[...]
            stage[ROW_LOSS:ROW_LOSS + 1, :] = loss_ref[...]
            stage[ROW_LOSS + 1:ROW_W_POOL, :] = jnp.zeros((ROW_W_POOL - ROW_LOSS - 1, 128), F32)
            stage[ROW_W_POOL:, :] = dwp_ref[...].reshape(4 * POOL_GROUP_DIM, POOL_GROUP_DIM)
            gat[4 * x + 2 * y + c] = stage[...]
            gat_rb[4 * x + 2 * y + c] = drb_ref[...]
            start()
            acc[...] = jnp.zeros_like(acc)

        acc[...] += _dot_tn(a_ref[...], b_ref[...])

        @pl.when(k == last)
        def _():
            blocks = acc[...].reshape(N_CHIP, 2, r, d)
            for chip in range(N_CHIP):
                for core in range(2):
                    out_ref[core, chip] = blocks[chip, core].astype(BF16)
            finish()
            total, total_rb = gat[0], gat_rb[0]
            for s in range(1, N_DEV):
                total, total_rb = total + gat[s], total_rb + gat_rb[s]
            total_ref[...] = total
            total_rb_ref[...] = total_rb

    out_shape = (2, N_CHIP, r, d)
    return pl.pallas_call(
        body, name="grad_w_in", grid=(t // tt,),
        out_shape=[jax.ShapeDtypeStruct(out_shape, BF16), jax.ShapeDtypeStruct((SMALL_ROWS, 128), F32),
                   jax.ShapeDtypeStruct(drel_bias.shape, F32)],
        in_specs=[pl.BlockSpec((tt, m), lambda k: (k, 0)), pl.BlockSpec((tt, d), lambda k: (k, 0))] + [VMEM] * 9,
        out_specs=[pl.BlockSpec(out_shape, lambda k: (0,) * len(out_shape)), VMEM, VMEM],
        scratch_shapes=[pltpu.VMEM((m, d), F32), pltpu.VMEM((SMALL_ROWS, 128), F32),
                        pltpu.VMEM((N_DEV, SMALL_ROWS, 128), F32), pltpu.VMEM((N_DEV, *drel_bias.shape), F32),
                        pltpu.SemaphoreType.DMA((14,)), pltpu.SemaphoreType.DMA((14,))],
        compiler_params=_params(),
    )(a, b, *gains, dpool_scale, dsinks, loss_part, dw_pool, drel_bias)


def reduce_w_in(d_in_t):
    def body(d_in_ref, g_in_ref, pair_got, chip_part, chip_got, p_send, p_recv, x_send, x_recv):
        x, y, c = lax.axis_index("x"), lax.axis_index("y"), lax.axis_index("c")
        my_chip = 2 * x + y
        pair = pltpu.make_async_remote_copy(
            src_ref=d_in_ref.at[1 - c], dst_ref=pair_got, send_sem=p_send, recv_sem=p_recv,
            device_id=(x, y, 1 - c), device_id_type=MESH)
        pair.start()
        pair.wait()
        chip_part[...] = (d_in_ref[c].astype(F32) + pair_got[...].astype(F32)).astype(BF16)
        copies = []
        for k in range(1, N_CHIP):
            px, py = x ^ (k >> 1), y ^ (k & 1)
            copies.append(pltpu.make_async_remote_copy(
                src_ref=chip_part.at[2 * px + py], dst_ref=chip_got.at[my_chip],
                send_sem=x_send.at[k - 1], recv_sem=x_recv.at[k - 1], device_id=(px, py, c), device_id_type=MESH))
        for cp in copies:
            cp.start()
        chip_got[my_chip] = chip_part[my_chip]
        for cp in copies:
            cp.wait()
        g_in = chip_got[0].astype(F32)
        for s in range(1, N_CHIP):
            g_in = g_in + chip_got[s].astype(F32)
        g_in_ref[...] = g_in

    per_core = d_in_t.shape[1:]
    return pl.pallas_call(
        body, name="reduce_w_in",
        out_shape=jax.ShapeDtypeStruct(d_in_t.shape[2:], F32),
        in_specs=[VMEM], out_specs=VMEM,
        scratch_shapes=[pltpu.VMEM(per_core, d_in_t.dtype), pltpu.VMEM(per_core, d_in_t.dtype),
                        pltpu.VMEM(per_core, d_in_t.dtype),
                        pltpu.SemaphoreType.DMA, pltpu.SemaphoreType.DMA,
                        pltpu.SemaphoreType.DMA((3,)), pltpu.SemaphoreType.DMA((3,))],
        compiler_params=_params(),
    )(d_in_t)


def small_adamw(total, total_rb, small_w, small_m, small_v):
    n_small = len(small_w)

    def body(*refs):
        total_ref, rb_ref = refs[:2]
        w_refs, m_refs, v_refs = (refs[2 + k * n_small:2 + (k + 1) * n_small] for k in range(3))
        loss_out = refs[2 + 3 * n_small]
        result = refs[3 + 3 * n_small:]
        total = total_ref[...]
        loss_out[...] = total[ROW_LOSS:ROW_LOSS + 1, :]
        grads = [_as_lanes(total[GAIN_ROWS * k:GAIN_ROWS * (k + 1), :]) for k in range(4)]
        grads.append(_as_lanes(total[ROW_POOL_SCALE:ROW_SINKS, :]))
        grads.append(total[ROW_SINKS:ROW_LOSS, 0:N_Q_HEADS])
        grads.append(total[ROW_W_POOL:, :].reshape(w_refs[6].shape))
        grads.append(rb_ref[...])
        for k in range(n_small):
            delta, m2, v2 = _adamw(w_refs[k][...], grads[k], m_refs[k][...], v_refs[k][...])
            result[4 * k][...] = grads[k]
            result[4 * k + 1][...] = delta
            result[4 * k + 2][...] = m2
            result[4 * k + 3][...] = v2

    out = pl.pallas_call(
        body, name="small_adamw",
        out_shape=[jax.ShapeDtypeStruct((1, 128), F32)] + [jax.ShapeDtypeStruct(w.shape, F32) for w in small_w for _ in range(4)],
        in_specs=[VMEM] * (2 + 3 * n_small), out_specs=[VMEM] * (1 + 4 * n_small),
        compiler_params=_params(),
    )(total, total_rb, *small_w, *small_m, *small_v)
    return out[0], [out[1 + 4 * k:5 + 4 * k] for k in range(n_small)]


def norm_inproj(x, g, w_shard, shard, shard_rows):
    t, d = x.shape
    r = w_shard.shape[0]
    tm = TOKEN_TILE
    nt = t // tm

    def body(x_ref, g_ref, w_shard_ref, shard_ref, proj_ref, h_ref, w_ref, gathered_ref, h_all, w_all, w_sem,
             send_w, recv_w, local_w, bounce_w, send_sems, recv_sems, local_sems, bounce):
        i = pl.program_id(0)
        start_w, finish_w = _gather_plan([w_shard_ref], [w_ref], send_w, recv_w, local_w, [bounce_w])
        start, finish = _gather_plan([shard_ref], [gathered_ref], send_sems, recv_sems, local_sems, [bounce],
                                     [(0, shard.shape[0])])

        @pl.when(i == 0)
        def _():
            start_w()
            start()

        @pl.when(i < nt)
        def _():
            xv = x_ref[...]
            h = ((xv * _rstd(xv)) * g_ref[...]).astype(BF16)
            h_ref[...] = h
            h_all[pl.ds(pl.multiple_of(i * tm, tm), tm), :] = h

        @pl.when(i == nt - 1)
        def _():
            finish_w()
            landed = pltpu.make_async_copy(w_ref, w_all, w_sem)
            landed.start()
            landed.wait()

        @pl.when(i >= nt)
        def _():
            rows = pl.ds(pl.multiple_of((i - nt) * tm, tm), tm)
            proj_ref[...] = _dot_nt(h_all[rows, :], _merge_rows(w_all[...]))

        pl.when(i == 2 * nt - 1)(finish)

    first = lambda i: (jnp.minimum(i, nt - 1), 0)
    return pl.pallas_call(
        body, name="norm_inproj", grid=(2 * nt,),
        out_shape=[jax.ShapeDtypeStruct((t, N_DEV * r), F32), jax.ShapeDtypeStruct((t, d), BF16),
                   jax.ShapeDtypeStruct((N_DEV, r, d), w_shard.dtype),
                   jax.ShapeDtypeStruct((N_DEV, shard_rows, d), shard.dtype)],
        in_specs=[pl.BlockSpec((tm, d), first), pl.BlockSpec((1, d), lambda i: (0, 0)), ANY, ANY],
        out_specs=[pl.BlockSpec((tm, N_DEV * r), lambda i: (jnp.maximum(i - nt, 0), 0)), pl.BlockSpec((tm, d), first),
                   ANY, ANY],
        scratch_shapes=[pltpu.VMEM((t, d), BF16), pltpu.VMEM((N_DEV, r, d), w_shard.dtype), pltpu.SemaphoreType.DMA]
        + _gather_scratch([w_shard]) + _gather_scratch([shard]),
        compiler_params=_params(),
    )(x, g, w_shard, shard)


def bias_band(bucket, in_window, rel_bias):
    def body(bk_ref, win_ref, rb_ref, out_ref):
        bk = bk_ref[...]
        keep = win_ref[...] > 0.5
        for h in range(N_Q_HEADS):
            acc = jnp.zeros(bk.shape, F32)
            for b in range(N_BUCKETS):
                acc = jnp.where(bk == float(b), rb_ref[b, h], acc)
            out_ref[h] = jnp.where(keep, acc, NEG_INF)

    return pl.pallas_call(
        body, name="bias_band",
        out_shape=jax.ShapeDtypeStruct((N_Q_HEADS, BLOCK, 2 * BLOCK), F32),
        in_specs=[VMEM, VMEM, SMEM], out_specs=VMEM,
    )(bucket, in_window, rel_bias)


def bias_band_bwd(bucket, dbias):
    def body(bk_ref, db_ref, out_ref):
        bk = bk_ref[...]
        for h in range(N_Q_HEADS):
            db = db_ref[h]
            for b in range(N_BUCKETS):
                out_ref[h, b] = jnp.sum(jnp.where(bk == float(b), db, 0.0))

    return pl.pallas_call(
        body, name="bias_band_bwd",
        out_shape=jax.ShapeDtypeStruct((N_Q_HEADS, N_BUCKETS), F32),
        in_specs=[VMEM, VMEM], out_specs=SMEM,
    )(bucket, dbias)


def _window_sum(buf_ref, g, w, first):
    cols = slice(g * POOL_GROUP_DIM, (g + 1) * POOL_GROUP_DIM)
    acc = None
    for k in range(w):
        piece = buf_ref[first(k):first(k) + BLOCK, cols]
        acc = piece if acc is None else acc + piece
    return acc


def _inv_count(i, w):
    row = lax.broadcasted_iota(jnp.int32, (BLOCK, 1), 0)
    return 1.0 / jnp.minimum(i * BLOCK + row + 1, w).astype(F32)


def _fill_pool_input(i, ubuf, uc_ref, halo_ref):
    ubuf[0:HALO, :] = jnp.where(i > 0, halo_ref[...], 0.0)
    ubuf[HALO:, :] = uc_ref[...]


def _pooled(i, g, w, ubuf):
    cols = slice(g * POOL_GROUP_DIM, (g + 1) * POOL_GROUP_DIM)
    return _window_sum(ubuf, g, w, lambda k: HALO - k) * _inv_count(i, w) - ubuf[HALO:, cols]


def _head_variants(pair):
    low = lax.broadcasted_iota(jnp.int32, pair.shape, 1) < HEAD_DIM
    swapped = pltpu.roll(pair, HEAD_DIM, 1)
    zero = jnp.zeros_like(pair)
    pick = lambda c, a, b: jnp.where(c, a, b).astype(BF16)
    return [[pick(low, pair, zero), pick(low, zero, swapped)], [pick(low, swapped, zero), pick(low, zero, pair)]]


def _head_probs(i, hq, rows, s_ref, biasm_ref, sinks_ref):
    s = s_ref[hq, rows, :] * ATTN_SCALE + biasm_ref[hq, rows, :]
    col = lax.broadcasted_iota(jnp.int32, s.shape, 1)
    s = jnp.where((i == 0) & (col < BLOCK), NEG_INF, s)
    sink = sinks_ref[0, hq]
    m = jnp.maximum(jnp.max(s, axis=-1, keepdims=True), sink)
    p = jnp.exp(s - m)
    e_sink = jnp.exp(sink - m)
    inv = 1.0 / (jnp.sum(p, axis=-1, keepdims=True) + e_sink)
    return p * inv, e_sink * inv


def _mixer_in_specs(cur, prv):
    return [pl.BlockSpec((BLOCK, 512), lambda i: (cur(i), 0)),
            pl.BlockSpec((HALO, 512), lambda i: (jnp.maximum(cur(i) * (BLOCK // HALO) - 1, 0), 0)),
            pl.BlockSpec((BLOCK, 512), lambda i: (cur(i), 1)),
            pl.BlockSpec((BLOCK, 256), lambda i: (cur(i), 4)),
            pl.BlockSpec((BLOCK, 256), lambda i: (prv(i), 4))]


def _mixer_param_specs():
    return [pl.BlockSpec((N_Q_HEADS, BLOCK, 2 * BLOCK), lambda i: (0, 0, 0)), SMEM,
            pl.BlockSpec((4, POOL_GROUP_DIM, POOL_GROUP_DIM), lambda i: (0, 0, 0)),
            pl.BlockSpec((1, POOL_WIDTH), lambda i: (0, 0))]


def mixers_fwd(proj, biasm, sinks, w_pool, pool_scale, shards):
    t = proj.shape[0]
    nb = t // BLOCK
    ns = len(shards)

    def body(*refs):
        uc_ref, halo_ref, q_ref, kvc_ref, kvp_ref, biasm_ref, sinks_ref, wp_ref, sc_ref = refs[:9]
        shard_refs, out_ref, gathered_refs = refs[9:9 + ns], refs[9 + ns], refs[10 + ns:10 + 2 * ns]
        ubuf, s_all, p_all, send_sems, recv_sems, local_sems = refs[10 + 2 * ns:16 + 2 * ns]
        i = pl.program_id(0)
        start, finish = _gather_plan(shard_refs, gathered_refs, send_sems, recv_sems, local_sems, refs[16 + 2 * ns:])
        pl.when(i == 0)(start)

        _fill_pool_input(i, ubuf, uc_ref, halo_ref)
        for g, w in enumerate(POOL_WINDOWS):
            mixed = _dot(_pooled(i, g, w, ubuf).astype(BF16), wp_ref[g])
            cols = slice(g * POOL_GROUP_DIM, (g + 1) * POOL_GROUP_DIM)
            out_ref[:, cols] = (mixed * sc_ref[:, cols]).astype(BF16)
        kv = jnp.concatenate([kvp_ref[...], kvc_ref[...]], axis=0)
        k_var = _head_variants(kv[:, 0:2 * HEAD_DIM])
        v_var = _head_variants(kv[:, 2 * HEAD_DIM:])
        for hq in range(N_Q_HEADS):
            j, half, h = hq // 2, hq % 2, hq // GQA_GROUP
            q2 = q_ref[:, 2 * HEAD_DIM * j:2 * HEAD_DIM * (j + 1)].astype(BF16)
            s_all[hq] = _dot_nt(q2, k_var[h][half])
        for hq in range(N_Q_HEADS):
            for r in range(0, BLOCK, ROW_CHUNK):
                rows = slice(r, r + ROW_CHUNK)
                probs, _ = _head_probs(i, hq, rows, s_all, biasm_ref, sinks_ref)
                p_all[hq, rows, :] = probs.astype(BF16)
        for j in range(N_Q_HEADS // 2):
            h = 2 * j // GQA_GROUP
            acc = _dot(p_all[2 * j], v_var[h][0]) + _dot(p_all[2 * j + 1], v_var[h][1])
            out_ref[:, POOL_WIDTH + 2 * HEAD_DIM * j:POOL_WIDTH + 2 * HEAD_DIM * (j + 1)] = acc.astype(BF16)

        pl.when(i == nb - 1)(finish)

    return pl.pallas_call(
        body, name="mixers_fwd", grid=(nb,),
        out_shape=[jax.ShapeDtypeStruct((t, 2 * POOL_WIDTH), BF16)]
        + [jax.ShapeDtypeStruct((N_DEV, *sh.shape), sh.dtype) for sh in shards],
        in_specs=_mixer_in_specs(lambda i: i, lambda i: jnp.maximum(i - 1, 0)) + _mixer_param_specs() + [ANY] * ns,
        out_specs=[pl.BlockSpec((BLOCK, 2 * POOL_WIDTH), lambda i: (i, 0))] + [ANY] * ns,
        scratch_shapes=[pltpu.VMEM((HALO + BLOCK, POOL_WIDTH), F32), pltpu.VMEM((N_Q_HEADS, BLOCK, 2 * BLOCK), F32),
                        pltpu.VMEM((N_Q_HEADS, BLOCK, 2 * BLOCK), BF16)] + _gather_scratch(shards),
        compiler_params=_params(),
    )(proj, proj, proj, proj, proj, biasm, sinks, w_pool, pool_scale, *shards)


def outproj_norm(cat, w, x, g, g_next, shard, partial):
    t, d = x.shape
    tm = TOKEN_TILE
    last = t // tm - 1
    rows = [(partial.shape[1] - shard.shape[0], shard.shape[0])]

    def body(c_ref, w_ref, x_ref, g_ref, gn_ref, shard_ref, partial_ref, mix_ref, x1_ref, h2_ref, gathered_ref,
             send_sems, recv_sems, local_sems, bounce):
        i = pl.program_id(0)
        start, finish = _gather_plan([shard_ref], [gathered_ref], send_sems, recv_sems, local_sems, [bounce], rows)
        pl.when(i == 0)(start)
        mix = _dot(c_ref[...], w_ref[...])
        mix_ref[...] = mix
        x1 = x_ref[...] + (mix * _rstd(mix)) * g_ref[...]
        x1_ref[...] = x1
        h2_ref[...] = ((x1 * _rstd(x1)) * gn_ref[...]).astype(BF16)
        pl.when(i == last)(finish)

    row = pl.BlockSpec((tm, d), lambda i: (i, 0))
    gain = pl.BlockSpec((1, d), lambda i: (0, 0))
    return pl.pallas_call(
        body, name="outproj_norm", grid=(t // tm,),
        out_shape=[jax.ShapeDtypeStruct((t, d), F32), jax.ShapeDtypeStruct((t, d), F32), jax.ShapeDtypeStruct((t, d), BF16),
                   jax.ShapeDtypeStruct(partial.shape, partial.dtype)],
        in_specs=[pl.BlockSpec((tm, cat.shape[1]), lambda i: (i, 0)), pl.BlockSpec(w.shape, lambda i: (0, 0)), row, gain, gain,
                  ANY, ANY],
        out_specs=[row, row, row, ANY],
        input_output_aliases={6: 3},
        scratch_shapes=_gather_scratch([shard]),
        compiler_params=_params(),
    )(cat, w, x, g, g_next, shard, partial)


def ffn_up(h, gate_t, up_t, down_shard):
    t, d = h.shape
    n = gate_t.shape[1]
    f = N_DEV * n
    tm, ts = FFN_TOKEN_TILE, FF_SHARDS_PER_TILE
    tn = ts * n
    steps = (f // tn, t // tm)

    def body(h_ref, wg_ref, wu_ref, shard_ref, gate_ref, up_ref, a_ref, gathered_ref,
             send_sems, recv_sems, local_sems, bounce):
        j, i = pl.program_id(0), pl.program_id(1)
        start, finish = _gather_plan([shard_ref], [gathered_ref], send_sems, recv_sems, local_sems, [bounce])
        pl.when((i == 0) & (j == 0))(start)

        hv = h_ref[...]
        gate = _dot_nt(hv, _merge_rows(wg_ref[...]))
        up = _dot_nt(hv, _merge_rows(wu_ref[...]))
        gate_ref[...] = gate.astype(BF16)
        up_ref[...] = up.astype(BF16)
        a_ref[...] = (gate * (1.0 / (1.0 + jnp.exp(-gate))) * up).astype(BF16)

        pl.when((j == steps[0] - 1) & (i == steps[1] - 1))(finish)

    wide = pl.BlockSpec((tm, tn), lambda j, i: (i, j))
    return pl.pallas_call(
        body, name="ffn_up", grid=steps,
        out_shape=[jax.ShapeDtypeStruct((t, f), BF16)] * 3
        + [jax.ShapeDtypeStruct((N_DEV, *down_shard.shape), down_shard.dtype)],
        in_specs=[pl.BlockSpec((tm, d), lambda j, i: (i, 0)),
                  pl.BlockSpec((ts, n, d), lambda j, i: (j, 0, 0)),
                  pl.BlockSpec((ts, n, d), lambda j, i: (j, 0, 0)), ANY],
        out_specs=[wide, wide, wide, ANY],
        scratch_shapes=_gather_scratch([down_shard]),
        compiler_params=_params(),
    )(h, gate_t, up_t, down_shard)


def ffn_down_loss(a, w_down, x1, g, target):
    t, d = x1.shape
    tm = WIDE_K_TOKEN_TILE

    def body(a_ref, w_ref, x_ref, g_ref, t_ref, df_ref, dy_ref, dg_ref, loss_ref):
        @pl.when(pl.program_id(0) == 0)
        def _():
            dg_ref[...] = jnp.zeros_like(dg_ref)
            loss_ref[...] = jnp.zeros_like(loss_ref)

        f = _dot(a_ref[...], _merge_rows(w_ref[...]))
        r = _rstd(f)
        g = g_ref[...]
        err = x_ref[...] + (f * r) * g - t_ref[...]
        loss_ref[...] += 0.5 * jnp.sum(jnp.mean(err * err, axis=-1, keepdims=True))
        dy = err * (1.0 / d)
        dy_ref[...] = dy
        df, dg_rows = _norm_bwd(dy, f, r, g)
        df_ref[...] = df.astype(BF16)
        dg_ref[...] += _as_rows(jnp.sum(dg_rows, axis=0, keepdims=True))

    row = pl.BlockSpec((tm, d), lambda i: (i, 0))
    gain = pl.BlockSpec((1, d), lambda i: (0, 0))
    return pl.pallas_call(
        body, name="ffn_down_loss", grid=(t // tm,),
        out_shape=[jax.ShapeDtypeStruct((t, d), BF16), jax.ShapeDtypeStruct((t, d), F32),
                   jax.ShapeDtypeStruct((d // 128, 128), F32), jax.ShapeDtypeStruct((1, 128), F32)],
        in_specs=[pl.BlockSpec((tm, a.shape[1]), lambda i: (i, 0)), pl.BlockSpec(w_down.shape, lambda i: (0, 0, 0)), row, gain, row],
        out_specs=[row, row, pl.BlockSpec((d // 128, 128), lambda i: (0, 0)), pl.BlockSpec((1, 128), lambda i: (0, 0))],
        compiler_params=_params(),
    )(a, w_down, x1, g, target)


def ffn_down_bwd(df, w_down, gate, up):
    t, d = df.shape
    n = w_down.shape[1]
    f = gate.shape[1]
    tm, ts = FFN_TOKEN_TILE, FF_SHARDS_PER_TILE
    tn = ts * n

    def body(df_ref, w_ref, gate_ref, up_ref, dgate_ref, dup_ref):
        da = _dot_nt(df_ref[...], _merge_rows(w_ref[...]))
        gate = gate_ref[...].astype(F32)
        sig = 1.0 / (1.0 + jnp.exp(-gate))
        dgate_ref[...] = (da * up_ref[...].astype(F32) * (sig * (1.0 + gate * (1.0 - sig)))).astype(BF16)
        dup_ref[...] = (da * (gate * sig)).astype(BF16)

    wide = pl.BlockSpec((tm, tn), lambda j, i: (i, j))
    return pl.pallas_call(
        body, name="ffn_down_bwd", grid=(f // tn, t // tm),
        out_shape=[jax.ShapeDtypeStruct((t, f), BF16)] * 2,
        in_specs=[pl.BlockSpec((tm, d), lambda j, i: (i, 0)), pl.BlockSpec((ts, n, d), lambda j, i: (j, 0, 0)), wide, wide],
        out_specs=[wide, wide],
        compiler_params=_params(),
    )(df, w_down, gate, up)


def grad_rows(a, b, name, by_core=False):
    t, m = a.shape
    d = b.shape[1]
    r = m // N_DEV
    tt = TOKEN_TILE
    last = t // tt - 1
    out_shape = (2, N_CHIP, r, d) if by_core else (N_DEV, r, d)

    def body(a_ref, b_ref, out_ref, acc):
        k = pl.program_id(0)

        @pl.when(k == 0)
        def _():
            acc[...] = jnp.zeros_like(acc)

        acc[...] += _dot_tn(a_ref[...], b_ref[...])

        @pl.when(k == last)
        def _():
            if by_core:
                blocks = acc[...].reshape(N_CHIP, 2, r, d)
                for chip in range(N_CHIP):
                    for core in range(2):
                        out_ref[core, chip] = blocks[chip, core].astype(BF16)
            else:
                out_ref[...] = acc[...].reshape(out_shape).astype(BF16)

    return pl.pallas_call(
        body, name=name, grid=(t // tt,),
        out_shape=jax.ShapeDtypeStruct(out_shape, BF16),
        in_specs=[pl.BlockSpec((tt, m), lambda k: (k, 0)), pl.BlockSpec((tt, d), lambda k: (k, 0))],
        out_specs=pl.BlockSpec(out_shape, lambda k: (0,) * len(out_shape)),
        scratch_shapes=[pltpu.VMEM((m, d), F32)],
        compiler_params=_params(),
    )(a, b)


def grad_ffn(lhs, b, name, pair_parts=()):
    t, f = lhs[0].shape
    d = b.shape[1]
    nw = len(lhs)
    na = len(pair_parts)
    n = f // N_DEV
    tt, ts = TOKEN_TILE, FF_SHARDS_PER_TILE
    tn = ts * n
    steps = (f // tn, t // tt)

    def body(*refs):
        a_refs, b_ref, part_refs = refs[:nw], refs[nw], refs[nw + 1:nw + 1 + na]
        out_refs = refs[nw + 1 + na:2 * nw + 1 + na]
        got_refs = refs[2 * nw + 1 + na:2 * nw + 1 + 2 * na]
        acc = refs[2 * nw + 1 + 2 * na]
        i, k = pl.program_id(0), pl.program_id(1)
        if na:
            start, finish = _pair_plan(part_refs, got_refs, *refs[2 * nw + 2 + 2 * na:])
            pl.when((i == 0) & (k == 0))(start)

        @pl.when(k == 0)
        def _():
            acc[...] = jnp.zeros_like(acc)

        for w in range(nw):
            acc[w] += _dot_tn(a_refs[w][...], b_ref[...])

        @pl.when(k == steps[1] - 1)
        def _():
            for w in range(nw):
                blocks = acc[w].reshape(ts // 2, 2, n, d)
                for chip in range(ts // 2):
                    for core in range(2):
                        out_refs[w][core, chip] = blocks[chip, core].astype(BF16)

        if na:
            pl.when((i == steps[0] - 1) & (k == steps[1] - 1))(finish)

    out = pl.pallas_call(
        body, name=name, grid=steps,
        out_shape=[jax.ShapeDtypeStruct((2, N_CHIP, n, d), BF16)] * nw
        + [jax.ShapeDtypeStruct(p.shape[1:], p.dtype) for p in pair_parts],
        in_specs=[pl.BlockSpec((tt, tn), lambda i, k: (k, i))] * nw + [pl.BlockSpec((tt, d), lambda i, k: (k, 0))] + [ANY] * na,
        out_specs=[pl.BlockSpec((2, ts // 2, n, d), lambda i, k: (0, i, 0, 0))] * nw + [ANY] * na,
        scratch_shapes=[pltpu.VMEM((nw, tn, d), F32)]
        + ([pltpu.SemaphoreType.DMA((na,)), pltpu.SemaphoreType.DMA((na,))] if na else []),
        compiler_params=_params(),
    )(*lhs, b, *pair_parts)
    return out[:nw], out[nw:]


def ffn_up_bwd(dgate, dup, gate_t, up_t, x1, g_ffn, dy, mix, g_mix, chip_parts):
    t, d = x1.shape
    n = gate_t.shape[1]
    f = N_DEV * n
    tm = WIDE_K_TOKEN_TILE
    na = len(chip_parts)
    last = t // tm - 1

    def body(*refs):
        dg_ref, du_ref, wg_ref, wu_ref, x_ref, gf_ref, dy_ref, mix_ref, gm_ref = refs[:9]
        part_refs = refs[9:9 + na]
        dx1_ref, dmix_ref, dgf_ref, dgm_ref = refs[9 + na:13 + na]
        slot_refs = refs[13 + na:13 + 2 * na]
        send_sems, recv_sems, local_sems = refs[13 + 2 * na:16 + 2 * na]
        i = pl.program_id(0)
        start, finish = _chip_exchange_plan(part_refs, slot_refs, send_sems, recv_sems, local_sems, refs[16 + 2 * na:])

        @pl.when(i == 0)
        def _():
            start()
            dgf_ref[...] = jnp.zeros_like(dgf_ref)
            dgm_ref[...] = jnp.zeros_like(dgm_ref)

        dh = _dot(dg_ref[...], _merge_rows(wg_ref[...])) + _dot(du_ref[...], _merge_rows(wu_ref[...]))
        x1 = x_ref[...]
        dx, dgf_rows = _norm_bwd(dh, x1, _rstd(x1), gf_ref[...])
        dx1 = dy_ref[...] + dx
        dx1_ref[...] = dx1
        dgf_ref[...] += _as_rows(jnp.sum(dgf_rows, axis=0, keepdims=True))
        mix = mix_ref[...]
        dmix, dgm_rows = _norm_bwd(dx1, mix, _rstd(mix), gm_ref[...])
        dmix_ref[...] = dmix.astype(BF16)
        dgm_ref[...] += _as_rows(jnp.sum(dgm_rows, axis=0, keepdims=True))
        pl.when(i == last)(finish)

    row = pl.BlockSpec((tm, d), lambda i: (i, 0))
    wide = pl.BlockSpec((tm, f), lambda i: (i, 0))
    gain = pl.BlockSpec((1, d), lambda i: (0, 0))
    gain_rows = pl.BlockSpec((d // 128, 128), lambda i: (0, 0))
    whole = pl.BlockSpec((N_DEV, n, d), lambda i: (0, 0, 0), pipeline_mode=pl.Buffered(1))
    out = pl.pallas_call(
        body, name="ffn_up_bwd", grid=(t // tm,),
        out_shape=[jax.ShapeDtypeStruct((t, d), F32), jax.ShapeDtypeStruct((t, d), BF16),
                   jax.ShapeDtypeStruct((d // 128, 128), F32), jax.ShapeDtypeStruct((d // 128, 128), F32)]
        + [jax.ShapeDtypeStruct(p.shape, p.dtype) for p in chip_parts],
        in_specs=[wide, wide, whole, whole, row, gain, row, row, gain] + [ANY] * na,
        out_specs=[row, row, gain_rows, gain_rows] + [ANY] * na,
        scratch_shapes=_chip_exchange_scratch(chip_parts),
        compiler_params=_params(),
    )(dgate, dup, gate_t, up_t, x1, g_ffn, dy, mix, g_mix, *chip_parts)
    return out[:4], out[4:]


def outproj_bwd(dmix, w_out):
    t, d = dmix.shape
    tm = TOKEN_TILE

    def body(dm_ref, w_ref, out_ref):
        out_ref[...] = _dot_nt(dm_ref[...], w_ref[...])

    return pl.pallas_call(
        body, name="outproj_bwd", grid=(t // tm,),
        out_shape=jax.ShapeDtypeStruct((t, w_out.shape[0]), F32),
        in_specs=[pl.BlockSpec((tm, d), lambda i: (i, 0)), pl.BlockSpec(w_out.shape, lambda i: (0, 0))],
        out_specs=pl.BlockSpec((tm, w_out.shape[0]), lambda i: (i, 0)),
        compiler_params=_params(),
    )(dmix, w_out)


def mixers_bwd(proj, dcat, biasm, sinks, w_pool, pool_scale, ffn_parts):
    t = proj.shape[0]
    nb = t // BLOCK
    na = len(ffn_parts)

    def body(*refs):
        (uc_ref, halo_ref, q_ref, kvc_ref, kvp_ref, dcat_ref, biasm_ref, sinks_ref, wp_ref, sc_ref) = refs[:10]
        part_refs = refs[10:10 + na]
        dproj_ref, dbias_ref, dsink_ref, dwp_ref, dsc_ref = refs[10 + na:15 + na]
        slot_refs = refs[15 + na:15 + 2 * na]
        ubuf, dbuf, c_u, c_q, c_kv, s_all, dp_all, ds_all, p_all = refs[15 + 2 * na:24 + 2 * na]
        send_sems, recv_sems, local_sems = refs[24 + 2 * na:27 + 2 * na]
        bounce = refs[27 + 2 * na:]
        i = pl.program_id(0)
        lane = lax.broadcasted_iota(jnp.int32, (1, 128), 1)
        start, finish = _chip_exchange_plan(part_refs, slot_refs, send_sems, recv_sems, local_sems, bounce)

        @pl.when(i == 0)
        def _():
            start()
            dbias_ref[...] = jnp.zeros_like(dbias_ref)
            dwp_ref[...] = jnp.zeros_like(dwp_ref)
            dsc_ref[...] = jnp.zeros_like(dsc_ref)
            dsink_ref[...] = jnp.zeros_like(dsink_ref)
            dbuf[...] = jnp.zeros_like(dbuf)
            c_u[...] = jnp.zeros_like(c_u)
            c_q[...] = jnp.zeros_like(c_q)
            c_kv[...] = jnp.zeros_like(c_kv)

        @pl.when(i < nb)
        def _():
            _fill_pool_input(i, ubuf, uc_ref, halo_ref)
            for g, w in enumerate(POOL_WINDOWS):
                cols = slice(g * POOL_GROUP_DIM, (g + 1) * POOL_GROUP_DIM)
                pooled = _pooled(i, g, w, ubuf).astype(BF16)
                mixed = _dot(pooled, wp_ref[g])
                dout = dcat_ref[:, cols]
                dsc_ref[g:g + 1, :] += jnp.sum(dout * mixed, axis=0, keepdims=True)
                dmixed = (dout * sc_ref[:, cols]).astype(BF16)
                dwp_ref[g] += _dot_tn(pooled, dmixed)
                dpooled = _dot_nt(dmixed, wp_ref[g])
                scaled = dpooled * _inv_count(i, w)
                dbuf[BLOCK:, cols] = scaled[0:HALO]
                dproj_ref[:, cols] = (_window_sum(dbuf, g, w, lambda k: k) + c_u[:, cols]).astype(BF16)
                dbuf[0:BLOCK, cols] = scaled
                c_u[:, cols] = -dpooled

            kv = jnp.concatenate([kvp_ref[...], kvc_ref[...]], axis=0)
            k_var = _head_variants(kv[:, 0:2 * HEAD_DIM])
            v_var = _head_variants(kv[:, 2 * HEAD_DIM:])
            q2s = [q_ref[:, 2 * HEAD_DIM * j:2 * HEAD_DIM * (j + 1)].astype(BF16) for j in range(N_Q_HEADS // 2)]
            do2s = [dcat_ref[:, POOL_WIDTH + 2 * HEAD_DIM * j:POOL_WIDTH + 2 * HEAD_DIM * (j + 1)].astype(BF16)
                    for j in range(N_Q_HEADS // 2)]
            slot = lambda hq: 4 * (hq // GQA_GROUP) + 2 * (hq % 2) + (hq % GQA_GROUP) // 2
            for hq in range(N_Q_HEADS):
                j, half, h = hq // 2, hq % 2, hq // GQA_GROUP
                s_all[hq] = _dot_nt(q2s[j], k_var[h][half])
                dp_all[hq] = _dot_nt(do2s[j], v_var[h][half])
            dsink_row = jnp.zeros((1, 128), F32)
            for hq in range(N_Q_HEADS):
                dsink = 0.0
                for r in range(0, BLOCK, ROW_CHUNK):
                    rows = slice(r, r + ROW_CHUNK)
                    probs, p_sink = _head_probs(i, hq, rows, s_all, biasm_ref, sinks_ref)
                    dp = dp_all[hq, rows, :]
                    delta = jnp.sum(probs * dp, axis=-1, keepdims=True)
                    ds = probs * (dp - delta)
                    dbias_ref[hq, rows, :] += ds
                    dsink = dsink + jnp.sum(p_sink * delta)
                    ds_all[slot(hq), rows, :] = (ds * ATTN_SCALE).astype(BF16)
                    p_all[slot(hq), rows, :] = probs.astype(BF16)
                dsink_row = dsink_row - jnp.where(lane == hq, dsink, 0.0)
            dsink_ref[...] += dsink_row
            dq2 = [None] * (N_Q_HEADS // 2)
            for hq in range(N_Q_HEADS):
                j, half, h = hq // 2, hq % 2, hq // GQA_GROUP
                dq = _dot(ds_all[slot(hq)], k_var[h][half])
                dq2[j] = dq if dq2[j] is None else dq2[j] + dq
            low = lax.broadcasted_iota(jnp.int32, (2 * BLOCK, 2 * HEAD_DIM), 1) < HEAD_DIM
            dk_half, dv_half = [[None, None], [None, None]], [[None, None], [None, None]]
            for h in range(N_KV_HEADS):
                for half in range(2):
                    heads = [hq for hq in range(GQA_GROUP * h, GQA_GROUP * (h + 1)) if hq % 2 == half]
                    base = slot(heads[0])
                    q_rows = jnp.concatenate([q2s[hq // 2] for hq in heads], axis=0)
                    do_rows = jnp.concatenate([do2s[hq // 2] for hq in heads], axis=0)
                    dk_half[h][half] = _dot_tn(_merge_rows(ds_all[base:base + 2]), q_rows)
                    dv_half[h][half] = _dot_tn(_merge_rows(p_all[base:base + 2]), do_rows)

            def pair_of(halves):
                return jnp.where(low, halves[0][0] + pltpu.roll(halves[0][1], HEAD_DIM, 1),
                                 halves[1][1] + pltpu.roll(halves[1][0], HEAD_DIM, 1))

            dkv = jnp.concatenate([pair_of(dk_half), pair_of(dv_half)], axis=1)
            dproj_ref[:, POOL_WIDTH:2 * POOL_WIDTH] = c_q[...].astype(BF16)
            dproj_ref[:, 2 * POOL_WIDTH:] = (c_kv[...] + dkv[0:BLOCK]).astype(BF16)
            c_q[...] = jnp.concatenate(dq2, axis=1)
            c_kv[...] = dkv[BLOCK:]

        @pl.when(i == nb)
        def _():
            dbuf[BLOCK:, :] = jnp.zeros((HALO, POOL_WIDTH), F32)
            for g, w in enumerate(POOL_WINDOWS):
                cols = slice(g * POOL_GROUP_DIM, (g + 1) * POOL_GROUP_DIM)
                dproj_ref[:, cols] = (_window_sum(dbuf, g, w, lambda k: k) + c_u[:, cols]).astype(BF16)
            dproj_ref[:, POOL_WIDTH:2 * POOL_WIDTH] = c_q[...].astype(BF16)
            dproj_ref[:, 2 * POOL_WIDTH:] = c_kv[...].astype(BF16)
            finish()

    cur = lambda i: jnp.minimum(i, nb - 1)
    prv = lambda i: jnp.maximum(jnp.minimum(i, nb - 1) - 1, 0)
    out = pl.pallas_call(
        body, name="mixers_bwd", grid=(nb + 1,),
        out_shape=[jax.ShapeDtypeStruct((t, proj.shape[1]), BF16),
                   jax.ShapeDtypeStruct((N_Q_HEADS, BLOCK, 2 * BLOCK), F32),
                   jax.ShapeDtypeStruct((1, 128), F32),
                   jax.ShapeDtypeStruct((4, POOL_GROUP_DIM, POOL_GROUP_DIM), F32),
                   jax.ShapeDtypeStruct((len(POOL_WINDOWS), POOL_GROUP_DIM), F32)]
        + [jax.ShapeDtypeStruct(p.shape, p.dtype) for p in ffn_parts],
        in_specs=_mixer_in_specs(cur, prv) + [pl.BlockSpec((BLOCK, 2 * POOL_WIDTH), lambda i: (cur(i), 0))]
        + _mixer_param_specs() + [ANY] * na,
        out_specs=[pl.BlockSpec((BLOCK, proj.shape[1]), lambda i: (jnp.maximum(i - 1, 0), 0)),
                   pl.BlockSpec((N_Q_HEADS, BLOCK, 2 * BLOCK), lambda i: (0, 0, 0)),
                   pl.BlockSpec((1, 128), lambda i: (0, 0)),
                   pl.BlockSpec((4, POOL_GROUP_DIM, POOL_GROUP_DIM), lambda i: (0, 0, 0)),
                   pl.BlockSpec((len(POOL_WINDOWS), POOL_GROUP_DIM), lambda i: (0, 0))] + [ANY] * na,
        scratch_shapes=[pltpu.VMEM((HALO + BLOCK, POOL_WIDTH), F32), pltpu.VMEM((BLOCK + HALO, POOL_WIDTH), F32),
                        pltpu.VMEM((BLOCK, POOL_WIDTH), F32), pltpu.VMEM((BLOCK, POOL_WIDTH), F32),
                        pltpu.VMEM((BLOCK, 256), F32),
                        pltpu.VMEM((N_Q_HEADS, BLOCK, 2 * BLOCK), F32), pltpu.VMEM((N_Q_HEADS, BLOCK, 2 * BLOCK), F32),
                        pltpu.VMEM((N_Q_HEADS, BLOCK, 2 * BLOCK), BF16), pltpu.VMEM((N_Q_HEADS, BLOCK, 2 * BLOCK), BF16)]
        + _chip_exchange_scratch(ffn_parts),
        compiler_params=_params(),
    )(proj, proj, proj, proj, proj, dcat, biasm, sinks, w_pool, pool_scale, *ffn_parts)
    return out[:5], out[5:]


def inproj_bwd(dproj, w_in_t, x, g, dx1):
    t, d = x.shape
    n = dproj.shape[1]
    tm = TOKEN_TILE

    def body(dp_ref, w_ref, x_ref, g_ref, dx1_ref, dx_ref, dg_ref):
        @pl.when(pl.program_id(0) == 0)
        def _():
            dg_ref[...] = jnp.zeros_like(dg_ref)

        dh = _dot(dp_ref[...], w_ref[...])
        xv = x_ref[...]
        dx, dg_rows = _norm_bwd(dh, xv, _rstd(xv), g_ref[...])
        dx_ref[...] = dx1_ref[...] + dx
        dg_ref[...] += _as_rows(jnp.sum(dg_rows, axis=0, keepdims=True))

    row = pl.BlockSpec((tm, d), lambda i: (i, 0))
    gain = pl.BlockSpec((1, d), lambda i: (0, 0))
    return pl.pallas_call(
        body, name="inproj_bwd", grid=(t // tm,),
        out_shape=[jax.ShapeDtypeStruct((t, d), F32), jax.ShapeDtypeStruct((d // 128, 128), F32)],
        in_specs=[pl.BlockSpec((tm, n), lambda i: (i, 0)), pl.BlockSpec(w_in_t.shape, lambda i: (0, 0)), row, gain, row],
        out_specs=[row, pl.BlockSpec((d // 128, 128), lambda i: (0, 0))],
        compiler_params=_params(),
    )(dproj, w_in_t, x, g, dx1)


def _bucket_band():
    qi = jnp.arange(BLOCK)[:, None]
    kj = jnp.arange(2 * BLOCK)[None, :]
    dist = qi + BLOCK - kj
    n = jnp.maximum(dist, 0)
    nf = jnp.maximum(n, 1).astype(F32)
    large = MAX_EXACT + (jnp.log(nf / MAX_EXACT) / np.float32(np.log(MAX_DISTANCE / MAX_EXACT))
                         * (N_BUCKETS - MAX_EXACT)).astype(jnp.int32)
    large = jnp.minimum(large, N_BUCKETS - 1)
    bucket = jnp.where(n < MAX_EXACT, n, large)
    in_window = (dist >= 0) & (dist < BLOCK)
    return bucket.astype(F32), in_window.astype(F32)


def kernel(x, g_pre_mix, w_in, w_pool, pool_scale, rel_bias, sinks, w_out, g_post_mix, g_pre_ffn, w_gate, w_up, w_down, g_post_ffn, loss_target, m_g_pre_mix, m_w_in, m_w_pool, m_pool_scale, m_rel_bias, m_sinks, m_w_out, m_g_post_mix, m_g_pre_ffn, m_w_gate, m_w_up, m_w_down, m_g_post_ffn, v_g_pre_mix, v_w_in, v_w_pool, v_pool_scale, v_rel_bias, v_sinks, v_w_out, v_g_post_mix, v_g_pre_ffn, v_w_gate, v_w_up, v_w_down, v_g_post_ffn):
    d = x.shape[-1]
    xs, target = x[0], loss_target[0]

    w_in_ts = w_in[0].T.astype(BF16)
    w_out_s = w_out[0].astype(BF16)
    gate_ts = w_gate[0].T.astype(BF16)
    up_ts = w_up[0].T.astype(BF16)
    w_down_s = w_down[0].astype(BF16)

    bucket, in_window = _bucket_band()
    biasm = bias_band(bucket, in_window, rel_bias)
    w_pool_b = w_pool[0].astype(BF16)
    half = up_ts.shape[0] // 2
    proj, h1, w_in_t, up_t = norm_inproj(xs, g_pre_mix, w_in_ts, up_ts[:half], up_ts.shape[0])
    w_in_t = w_in_t.reshape(-1, d)
    cat, gate_t, w_out_f = mixers_fwd(proj, biasm, sinks, w_pool_b, pool_scale, [gate_ts, w_out_s])
    w_out_f = w_out_f.reshape(-1, d)
    mix, x1, h2, up_t = outproj_norm(cat, w_out_f, xs, g_post_mix, g_pre_ffn, up_ts[half:], up_t)
    gate, up, act, w_down_f = ffn_up(h2, gate_t, up_t, w_down_s)
    df, dy, dg_post_ffn, loss_part = ffn_down_loss(act, w_down_f, x1, g_post_ffn, target)

    def pair_sum(parts, tag):
        return pair_add(parts, pair_exchange(parts, "pair_exchange_" + tag), "pair_add_" + tag)

    dgate, dup = ffn_down_bwd(df, w_down_f, gate, up)
    (d_gate, d_up), _ = grad_ffn([dgate, dup], h2, "grad_w_gate_up")
    (d_down,), got_gate_up = grad_ffn([act], df, "grad_w_down", [d_gate, d_up])
    q_gate, q_up, q_down_a, q_down_b = pair_add(
        [d_gate, d_up, d_down], [*got_gate_up, *pair_exchange([d_down], "pair_exchange_down")], "pair_add_ffn",
        split_last=True)
    (dx1, dmix, dg_pre_ffn, dg_post_mix), (gate_slots, down_a_slots) = ffn_up_bwd(
        dgate, dup, gate_t, up_t, x1, g_pre_ffn, dy, mix, g_post_mix, [q_gate, q_down_a])
    dcat = outproj_bwd(dmix, w_out_f)
    d_out = grad_rows(cat, dmix, "grad_w_out", by_core=True)
    q_out, = pair_sum([d_out], "out")
    (dproj, dbias, dsinks, dw_pool, dpool_scale), (up_slots, out_slots, down_b_slots) = mixers_bwd(
        proj, dcat, biasm, sinks, w_pool_b, pool_scale, [q_up, q_out, q_down_b])
    drel_bias = bias_band_bwd(bucket, dbias)
    grad_x, dg_pre_mix = inproj_bwd(dproj, w_in_t, xs, g_pre_mix, dx1)

    small_w = [g_pre_mix, g_post_mix, g_pre_ffn, g_post_ffn, pool_scale, sinks, w_pool, rel_bias.T]
    small_m = [m_g_pre_mix, m_g_post_mix, m_g_pre_ffn, m_g_post_ffn, m_pool_scale, m_sinks, m_w_pool, m_rel_bias.T]
    small_v = [v_g_pre_mix, v_g_post_mix, v_g_pre_ffn, v_g_post_ffn, v_pool_scale, v_sinks, v_w_pool, v_rel_bias.T]
    d_in_t, total, total_rb = grad_w_in_small_reduce(
        dproj, h1, [dg_pre_mix, dg_post_mix, dg_pre_ffn, dg_post_ffn], dpool_scale, dsinks, loss_part, dw_pool, drel_bias)
    g_in_t = reduce_w_in(d_in_t)
    loss_row, sm = small_adamw(total, total_rb, small_w, small_m, small_v)
    sm[7] = [r.T for r in sm[7]]
    big_w = [w_in[0].T, w_out[0], w_gate[0].T, w_up[0].T, w_down[0]]
    big_m = [m_w_in[0].T, m_w_out[0], m_w_gate[0].T, m_w_up[0].T, m_w_down[0]]
    big_v = [v_w_in[0].T, v_w_out[0], v_w_gate[0].T, v_w_up[0].T, v_w_down[0]]
    upd = sum_adamw([out_slots, gate_slots, up_slots, (down_a_slots, down_b_slots)], big_w[1:], big_m[1:], big_v[1:],
                    "sum_adamw")
    upd = [[g_in_t, *adamw_update(big_w[:1], [g_in_t], big_m[:1], big_v[:1], "adamw_in")[0]], *upd]
    back = lambda k, a: (a.T if k in (0, 2, 3) else a)[None]
    big = [[back(k, u) for u in upd[k]] for k in range(5)]

    def ordered(kind):
        s, b = [p[kind] for p in sm], [p[kind] for p in big]
        return [s[0], b[0], s[6], s[4], s[7], s[5], b[1], s[1], s[2], b[2], b[3], b[4], s[3]]

    return (loss_row[0, 0], grad_x[None], *ordered(0), *ordered(1), *ordered(2), *ordered(3))
```

```python
import numpy as np
import jax
import jax.numpy as jnp
from jax import lax
from jax.experimental import pallas as pl
from jax.experimental.pallas import tpu as pltpu

F32 = jnp.float32
BF16 = jnp.bfloat16

N_DEV = 8
N_CHIP = 4
POOL_WIDTH = 512
POOL_WINDOWS = (2, 4, 8, 16)
POOL_GROUP_DIM = 128
HEAD_DIM = 64
N_Q_HEADS = 8
N_KV_HEADS = 2
GQA_GROUP = 4
BLOCK = 128
HALO = 16
ROW_CHUNK = 32
N_BUCKETS = 32
MAX_EXACT = 16
MAX_DISTANCE = 128
EPS = 1e-6
NEG_INF = -1e30
ATTN_SCALE = float(1.0 / np.sqrt(np.float32(HEAD_DIM)))

ADAM_LR = 0.001
ADAM_B1 = 0.9
ADAM_B2 = 0.999
ADAM_EPS = 1e-08
ADAM_WD = 0.01
ADAM_STEP = 10

TOKEN_TILE = 1024
WIDE_K_TOKEN_TILE = 512
FFN_TOKEN_TILE = 1024
FF_SHARDS_PER_TILE = 4
VMEM_LIMIT = 56 * 1024 * 1024
MESH = pl.DeviceIdType.MESH
ANY = pl.BlockSpec(memory_space=pl.ANY)
VMEM = pl.BlockSpec(memory_space=pltpu.VMEM)
SMEM = pl.BlockSpec(memory_space=pltpu.SMEM)


def _params(**kw):
    return pltpu.CompilerParams(vmem_limit_bytes=VMEM_LIMIT, **kw)


def _dot(a, b):
    return jnp.dot(a, b, preferred_element_type=F32)


def _dot_nt(a, b):
    return lax.dot_general(a, b, (((1,), (1,)), ((), ())), preferred_element_type=F32)


def _dot_tn(a, b):
    return lax.dot_general(a, b, (((0,), (0,)), ((), ())), preferred_element_type=F32)


def _rstd(v):
    return lax.rsqrt(jnp.mean(v * v, axis=-1, keepdims=True) + EPS)


def _norm_bwd(dout, v, r, g):
    vn = v * r
    dn = dout * g
    dv = r * (dn - vn * jnp.mean(dn * vn, axis=-1, keepdims=True))
    return dv, dout * vn


def _as_rows(v):
    return jnp.concatenate([v[:, k:k + 128] for k in range(0, v.shape[1], 128)], axis=0)


def _as_lanes(rows):
    return jnp.concatenate([rows[k:k + 1, :] for k in range(rows.shape[0])], axis=1)


def _merge_rows(value):
    s, r, c_ = value.shape
    return value.reshape(s * r, c_)


def _gather_plan(srcs, outs, send_sems, recv_sems, local_sems=None, bounce=None, rows=None):
    n = len(srcs)
    x, y, c = lax.axis_index("x"), lax.axis_index("y"), lax.axis_index("c")
    me, sibling = (x, y, c), (x, y, 1 - c)
    chips = [(1 - x, y), (x, 1 - y), (1 - x, 1 - y)]

    def slot(a, px, py, pc):
        whole = outs[a].at[4 * px + 2 * py + pc]
        return whole if rows is None or rows[a] is None else whole.at[pl.ds(*rows[a])]

    def copy(a, k, block, to, from_src=False):
        return pltpu.make_async_remote_copy(
            src_ref=srcs[a] if from_src else slot(a, *block), dst_ref=slot(a, *block),
            send_sem=send_sems.at[k * n + a], recv_sem=recv_sems.at[k * n + a], device_id=to, device_id_type=MESH)

    def own_in(a):
        return pltpu.make_async_copy(srcs[a], bounce[a], local_sems.at[a])

    def own_out(a):
        return pltpu.make_async_copy(bounce[a], slot(a, *me), local_sems.at[a])

    def first(a):
        return [copy(a, 0, me, sibling, True)] + [copy(a, 1 + j, me, (*chip, c), True) for j, chip in enumerate(chips)]

    def passed(a, j):
        return copy(a, 4 + j, (*chips[j], c), sibling)

    def start():
        for a in range(n):
            if bounce is not None:
                own_in(a).start()
            for cp in first(a):
                cp.start()

    def forward():
        if bounce is not None:
            for a in range(n):
                own_in(a).wait()
                own_out(a).start()
        for j, chip in enumerate(chips):
            for a in range(n):
                copy(a, 1 + j, (*chip, c), me).wait_recv()
                passed(a, j).start()

    def complete():
        for a in range(n):
            copy(a, 0, sibling, me).wait_recv()
            for j, chip in enumerate(chips):
                copy(a, 4 + j, (*chip, 1 - c), me).wait_recv()
        for a in range(n):
            for cp in first(a) + [passed(a, j) for j in range(3)]:
                cp.wait_send()
            if bounce is not None:
                own_out(a).wait()

    def finish():
        forward()
        complete()

    finish.forward, finish.complete = forward, complete
    return start, finish


def _gather_scratch(shards):
    n = len(shards)
    return [pltpu.SemaphoreType.DMA((7 * n,)), pltpu.SemaphoreType.DMA((7 * n,)), pltpu.SemaphoreType.DMA((n,))] \
        + [pltpu.VMEM(s.shape, s.dtype) for s in shards]


def _chip_exchange_plan(srcs, outs, send_sems, recv_sems, local_sems, bounce):
    n = len(srcs)
    x, y, c = lax.axis_index("x"), lax.axis_index("y"), lax.axis_index("c")
    my_chip = 2 * x + y

    def copies():
        out = []
        for a in range(n):
            for k in range(1, N_CHIP):
                px, py = x ^ (k >> 1), y ^ (k & 1)
                out.append(pltpu.make_async_remote_copy(
                    src_ref=srcs[a].at[2 * px + py], dst_ref=outs[a].at[my_chip],
                    send_sem=send_sems.at[(k - 1) * n + a], recv_sem=recv_sems.at[(k - 1) * n + a],
                    device_id=(px, py, c), device_id_type=MESH))
        return out

    def own_in(a):
        return pltpu.make_async_copy(srcs[a].at[my_chip], bounce[a], local_sems.at[a])

    def own_out(a):
        return pltpu.make_async_copy(bounce[a], outs[a].at[my_chip], local_sems.at[a])

    def start():
        for a in range(n):
            own_in(a).start()
        for cp in copies():
            cp.start()

    def finish():
        for a in range(n):
            own_in(a).wait()
            own_out(a).start()
        for cp in copies():
            cp.wait()
        for a in range(n):
            own_out(a).wait()

    return start, finish


def _chip_exchange_scratch(parts):
    n = len(parts)
    return [pltpu.SemaphoreType.DMA((3 * n,)), pltpu.SemaphoreType.DMA((3 * n,)), pltpu.SemaphoreType.DMA((n,))] \
        + [pltpu.VMEM(p.shape[1:], p.dtype) for p in parts]


def _pair_plan(srcs, outs, send_sems, recv_sems):
    x, y, c = lax.axis_index("x"), lax.axis_index("y"), lax.axis_index("c")

    def copies():
        return [pltpu.make_async_remote_copy(
            src_ref=srcs[a].at[1 - c], dst_ref=outs[a], send_sem=send_sems.at[a], recv_sem=recv_sems.at[a],
            device_id=(x, y, 1 - c), device_id_type=MESH) for a in range(len(srcs))]

    def start():
        for cp in copies():
            cp.start()

    def finish():
        for cp in copies():
            cp.wait()

    return start, finish


def pair_exchange(parts, name):
    n = len(parts)

    def body(*refs):
        start, finish = _pair_plan(refs[:n], refs[n:2 * n], *refs[2 * n:])
        start()
        finish()

    return pl.pallas_call(
        body, name=name, out_shape=[jax.ShapeDtypeStruct(p.shape[1:], p.dtype) for p in parts],
        in_specs=[ANY] * n, out_specs=[ANY] * n,
        scratch_shapes=[pltpu.SemaphoreType.DMA((n,)), pltpu.SemaphoreType.DMA((n,))],
    )(*parts)


def pair_add(parts, got, name, split_last=False):
    n = len(parts)

    def body(core_ref, *refs):
        for a in range(n):
            total = (refs[a][...].astype(F32) + refs[n + a][...].astype(F32)).astype(BF16)
            if split_last and a == n - 1:
                half = total.shape[1] // 2
                refs[2 * n + a][...] = total[:, :half]
                refs[2 * n + a + 1][...] = total[:, half:]
            else:
                refs[2 * n + a][...] = total

    def own(p):
        zeros = (0,) * (p.ndim - 2)
        return pl.BlockSpec((None, 1, *p.shape[2:]), lambda i, core: (core[0], i, *zeros))

    def plain(p):
        zeros = (0,) * (p.ndim - 1)
        return pl.BlockSpec((1, *p.shape[1:]), lambda i, core: (i, *zeros))

    outs = [jax.ShapeDtypeStruct(p.shape, BF16) for p in got]
    if split_last:
        s0, r, d = got[-1].shape
        outs[-1:] = [jax.ShapeDtypeStruct((s0, r // 2, d), BF16)] * 2
    core = lax.axis_index("c").astype(jnp.int32).reshape(1)
    return pl.pallas_call(
        body, name=name,
        grid_spec=pltpu.PrefetchScalarGridSpec(
            num_scalar_prefetch=1, grid=(got[0].shape[0],),
            in_specs=[own(p) for p in parts] + [plain(p) for p in got], out_specs=[plain(p) for p in outs]),
        out_shape=outs,
        compiler_params=_params(),
    )(core, *parts, *got)


def _adamw(w, g, m, v):
    m2 = ADAM_B1 * m + (1.0 - ADAM_B1) * g
    v2 = ADAM_B2 * v + (1.0 - ADAM_B2) * (g * g)
    m_hat = m2 / (1.0 - ADAM_B1 ** ADAM_STEP)
    v_hat = v2 / (1.0 - ADAM_B2 ** ADAM_STEP)
    delta = -ADAM_LR * (m_hat / (jnp.sqrt(v_hat) + ADAM_EPS) + ADAM_WD * w)
    return delta, m2, v2


def sum_adamw(slots, ws, ms, vs, name):
    n = len(ws)
    halves = 2
    flat = [p for entry in slots for p in (entry if isinstance(entry, (tuple, list)) else (entry,))]
    first = np.cumsum([0] + [len(e) if isinstance(e, (tuple, list)) else 1 for e in slots])
    nf = len(flat)

    def body(*refs):
        i = pl.program_id(0)

        def slot_sum(ref):
            total = ref[0].astype(F32)
            for s in range(1, ref.shape[0]):
                total = total + ref[s].astype(F32)
            return total

        for a in range(n):
            total = slot_sum(refs[first[a]])
            for h in range(1, first[a + 1] - first[a]):
                total = jnp.where(i == h, slot_sum(refs[first[a] + h]), total)
            delta, m2, v2 = _adamw(refs[nf + a][...], total, refs[nf + n + a][...], refs[nf + 2 * n + a][...])
            for q, val in enumerate((total, delta, m2, v2)):
                refs[nf + 3 * n + 4 * a + q][...] = val

    def rows(w):
        return pl.BlockSpec((w.shape[0] // halves, w.shape[1]), lambda i: (i, 0))

    def slot_spec(entry):
        if isinstance(entry, (tuple, list)):
            return [pl.BlockSpec(p.shape, lambda i: (0, 0, 0)) for p in entry]
        return [pl.BlockSpec((entry.shape[0], entry.shape[1] // halves, entry.shape[2]), lambda i: (0, i, 0))]

    out = pl.pallas_call(
        body, name=name, grid=(halves,),
        out_shape=[jax.ShapeDtypeStruct(w.shape, F32) for w in ws for _ in range(4)],
        in_specs=[sp for e in slots for sp in slot_spec(e)] + [rows(w) for w in ws] * 3,
        out_specs=[rows(w) for w in ws for _ in range(4)],
        compiler_params=_params(),
    )(*flat, *ws, *ms, *vs)
    return [out[4 * a:4 * a + 4] for a in range(n)]


def adamw_update(ws, gs, ms, vs, name):
    n = len(ws)

    def body(*refs):
        for a in range(n):
            delta, m2, v2 = _adamw(refs[a][...], refs[n + a][...], refs[2 * n + a][...], refs[3 * n + a][...])
            refs[4 * n + 3 * a][...] = delta
            refs[4 * n + 3 * a + 1][...] = m2
            refs[4 * n + 3 * a + 2][...] = v2

    out = pl.pallas_call(
        body, name=name,
        out_shape=[jax.ShapeDtypeStruct(w.shape, F32) for w in ws for _ in range(3)],
        in_specs=[VMEM] * (4 * n), out_specs=[VMEM] * (3 * n),
        compiler_params=_params(),
    )(*ws, *gs, *ms, *vs)
    return [out[3 * a:3 * a + 3] for a in range(n)]


GAIN_ROWS = 8
ROW_POOL_SCALE = 4 * GAIN_ROWS
ROW_SINKS = ROW_POOL_SCALE + 4
ROW_LOSS = ROW_SINKS + 1
ROW_W_POOL = 40
SMALL_ROWS = ROW_W_POOL + 4 * POOL_GROUP_DIM


def grad_w_in_small_reduce(a, b, gains, dpool_scale, dsinks, loss_part, dw_pool, drel_bias):
    t, m = a.shape
    d = b.shape[1]
    r = m // N_DEV
    tt = TOKEN_TILE
    last = t // tt - 1

    def body(a_ref, b_ref, g0, g1, g2, g3, dsc_ref, dsink_ref, loss_ref, dwp_ref, drb_ref, out_ref, total_ref, total_rb_ref,
             acc, stage, gat, gat_rb, g_send, g_recv):
        k = pl.program_id(0)
        x, y, c = lax.axis_index("x"), lax.axis_index("y"), lax.axis_index("c")
        start, finish = _gather_plan([stage, drb_ref], [gat, gat_rb], g_send, g_recv)

        @pl.when(k == 0)
        def _():
            for q, g_ref in enumerate((g0, g1, g2, g3)):
                stage[GAIN_ROWS * q:GAIN_ROWS * (q + 1), :] = g_ref[...]
            stage[ROW_POOL_SCALE:ROW_SINKS, :] = dsc_ref[...]
            stage[ROW_SINKS:ROW_LOSS, :] = dsink_ref[...]
            stage[ROW_LOSS:ROW_LOSS + 1, :] = loss_ref[...]
            stage[ROW_LOSS + 1:ROW_W_POOL, :] = jnp.zeros((ROW_W_POOL - ROW_LOSS - 1, 128), F32)
            stage[ROW_W_POOL:, :] = dwp_ref[...].reshape(4 * POOL_GROUP_DIM, POOL_GROUP_DIM)
            gat[4 * x + 2 * y + c] = stage[...]
            gat_rb[4 * x + 2 * y + c] = drb_ref[...]
            start()
            acc[...] = jnp.zeros_like(acc)

        acc[...] += _dot_tn(a_ref[...], b_ref[...])

        @pl.when(k == last)
        def _():
            blocks = acc[...].reshape(N_CHIP, 2, r, d)
            for chip in range(N_CHIP):
                for core in range(2):
                    out_ref[core, chip] = blocks[chip, core].astype(BF16)
            finish()
            total, total_rb = gat[0], gat_rb[0]
            for s in range(1, N_DEV):
                total, total_rb = total + gat[s], total_rb + gat_rb[s]
            total_ref[...] = total
            total_rb_ref[...] = total_rb

    out_shape = (2, N_CHIP, r, d)
    return pl.pallas_call(
        body, name="grad_w_in", grid=(t // tt,),
        out_shape=[jax.ShapeDtypeStruct(out_shape, BF16), jax.ShapeDtypeStruct((SMALL_ROWS, 128), F32),
                   jax.ShapeDtypeStruct(drel_bias.shape, F32)],
        in_specs=[pl.BlockSpec((tt, m), lambda k: (k, 0)), pl.BlockSpec((tt, d), lambda k: (k, 0))] + [VMEM] * 9,
        out_specs=[pl.BlockSpec(out_shape, lambda k: (0,) * len(out_shape)), VMEM, VMEM],
        scratch_shapes=[pltpu.VMEM((m, d), F32), pltpu.VMEM((SMALL_ROWS, 128), F32),
                        pltpu.VMEM((N_DEV, SMALL_ROWS, 128), F32), pltpu.VMEM((N_DEV, *drel_bias.shape), F32),
                        pltpu.SemaphoreType.DMA((14,)), pltpu.SemaphoreType.DMA((14,))],
        compiler_params=_params(),
    )(a, b, *gains, dpool_scale, dsinks, loss_part, dw_pool, drel_bias)


def reduce_w_in(d_in_t):
    def body(d_in_ref, g_in_ref, pair_got, chip_part, chip_got, p_send, p_recv, x_send, x_recv):
        x, y, c = lax.axis_index("x"), lax.axis_index("y"), lax.axis_index("c")
        my_chip = 2 * x + y
        pair = pltpu.make_async_remote_copy(
            src_ref=d_in_ref.at[1 - c], dst_ref=pair_got, send_sem=p_send, recv_sem=p_recv,
            device_id=(x, y, 1 - c), device_id_type=MESH)
        pair.start()
        pair.wait()
        chip_part[...] = (d_in_ref[c].astype(F32) + pair_got[...].astype(F32)).astype(BF16)
        copies = []
        for k in range(1, N_CHIP):
            px, py = x ^ (k >> 1), y ^ (k & 1)
            copies.append(pltpu.make_async_remote_copy(
                src_ref=chip_part.at[2 * px + py], dst_ref=chip_got.at[my_chip],
                send_sem=x_send.at[k - 1], recv_sem=x_recv.at[k - 1], device_id=(px, py, c), device_id_type=MESH))
        for cp in copies:
            cp.start()
        chip_got[my_chip] = chip_part[my_chip]
        for cp in copies:
            cp.wait()
        g_in = chip_got[0].astype(F32)
        for s in range(1, N_CHIP):
            g_in = g_in + chip_got[s].astype(F32)
        g_in_ref[...] = g_in

    per_core = d_in_t.shape[1:]
    return pl.pallas_call(
        body, name="reduce_w_in",
        out_shape=jax.ShapeDtypeStruct(d_in_t.shape[2:], F32),
        in_specs=[VMEM], out_specs=VMEM,
        scratch_shapes=[pltpu.VMEM(per_core, d_in_t.dtype), pltpu.VMEM(per_core, d_in_t.dtype),
                        pltpu.VMEM(per_core, d_in_t.dtype),
                        pltpu.SemaphoreType.DMA, pltpu.SemaphoreType.DMA,
                        pltpu.SemaphoreType.DMA((3,)), pltpu.SemaphoreType.DMA((3,))],
        compiler_params=_params(),
    )(d_in_t)


def small_adamw(total, total_rb, small_w, small_m, small_v):
    n_small = len(small_w)

    def body(*refs):
        total_ref, rb_ref = refs[:2]
        w_refs, m_refs, v_refs = (refs[2 + k * n_small:2 + (k + 1) * n_small] for k in range(3))
        loss_out = refs[2 + 3 * n_small]
        result = refs[3 + 3 * n_small:]
        total = total_ref[...]
        loss_out[...] = total[ROW_LOSS:ROW_LOSS + 1, :]
        grads = [_as_lanes(total[GAIN_ROWS * k:GAIN_ROWS * (k + 1), :]) for k in range(4)]
        grads.append(_as_lanes(total[ROW_POOL_SCALE:ROW_SINKS, :]))
        grads.append(total[ROW_SINKS:ROW_LOSS, 0:N_Q_HEADS])
        grads.append(total[ROW_W_POOL:, :].reshape(w_refs[6].shape))
        grads.append(rb_ref[...])
        for k in range(n_small):
            delta, m2, v2 = _adamw(w_refs[k][...], grads[k], m_refs[k][...], v_refs[k][...])
            result[4 * k][...] = grads[k]
            result[4 * k + 1][...] = delta
            result[4 * k + 2][...] = m2
            result[4 * k + 3][...] = v2

    out = pl.pallas_call(
        body, name="small_adamw",
        out_shape=[jax.ShapeDtypeStruct((1, 128), F32)] + [jax.ShapeDtypeStruct(w.shape, F32) for w in small_w for _ in range(4)],
        in_specs=[VMEM] * (2 + 3 * n_small), out_specs=[VMEM] * (1 + 4 * n_small),
        compiler_params=_params(),
    )(total, total_rb, *small_w, *small_m, *small_v)
    return out[0], [out[1 + 4 * k:5 + 4 * k] for k in range(n_small)]


def norm_inproj(x, g, w_shard, shard, shard_rows):
    t, d = x.shape
    r = w_shard.shape[0]
    tm = TOKEN_TILE
    nt = t // tm

    def body(x_ref, g_ref, w_shard_ref, shard_ref, proj_ref, h_ref, w_ref, gathered_ref, h_all, w_all, w_sem,
             send_w, recv_w, local_w, bounce_w, send_sems, recv_sems, local_sems, bounce):
        i = pl.program_id(0)
        start_w, finish_w = _gather_plan([w_shard_ref], [w_ref], send_w, recv_w, local_w, [bounce_w])
        start, finish = _gather_plan([shard_ref], [gathered_ref], send_sems, recv_sems, local_sems, [bounce],
                                     [(0, shard.shape[0])])

        @pl.when(i == 0)
        def _():
            start_w()
            start()

        @pl.when(i < nt)
        def _():
            xv = x_ref[...]
            h = ((xv * _rstd(xv)) * g_ref[...]).astype(BF16)
            h_ref[...] = h
            h_all[pl.ds(pl.multiple_of(i * tm, tm), tm), :] = h

        @pl.when(i == nt - 1)
        def _():
            finish_w()
            landed = pltpu.make_async_copy(w_ref, w_all, w_sem)
            landed.start()
            landed.wait()

        @pl.when(i >= nt)
        def _():
            rows = pl.ds(pl.multiple_of((i - nt) * tm, tm), tm)
            proj_ref[...] = _dot_nt(h_all[rows, :], _merge_rows(w_all[...]))

        pl.when(i == 2 * nt - 2)(finish.forward)
        pl.when(i == 2 * nt - 1)(finish.complete)

    first = lambda i: (jnp.minimum(i, nt - 1), 0)
    return pl.pallas_call(
        body, name="norm_inproj", grid=(2 * nt,),
        out_shape=[jax.ShapeDtypeStruct((t, N_DEV * r), F32), jax.ShapeDtypeStruct((t, d), BF16),
                   jax.ShapeDtypeStruct((N_DEV, r, d), w_shard.dtype),
                   jax.ShapeDtypeStruct((N_DEV, shard_rows, d), shard.dtype)],
        in_specs=[pl.BlockSpec((tm, d), first), pl.BlockSpec((1, d), lambda i: (0, 0)), ANY, ANY],
        out_specs=[pl.BlockSpec((tm, N_DEV * r), lambda i: (jnp.maximum(i - nt, 0), 0)), pl.BlockSpec((tm, d), first),
                   ANY, ANY],
        scratch_shapes=[pltpu.VMEM((t, d), BF16), pltpu.VMEM((N_DEV, r, d), w_shard.dtype), pltpu.SemaphoreType.DMA]
        + _gather_scratch([w_shard]) + _gather_scratch([shard]),
        compiler_params=_params(),
    )(x, g, w_shard, shard)


def bias_band(bucket, in_window, rel_bias):
    def body(bk_ref, win_ref, rb_ref, out_ref):
        bk = bk_ref[...]
        keep = win_ref[...] > 0.5
        for h in range(N_Q_HEADS):
            acc = jnp.zeros(bk.shape, F32)
            for b in range(N_BUCKETS):
                acc = jnp.where(bk == float(b), rb_ref[b, h], acc)
            out_ref[h] = jnp.where(keep, acc, NEG_INF)

    return pl.pallas_call(
        body, name="bias_band",
        out_shape=jax.ShapeDtypeStruct((N_Q_HEADS, BLOCK, 2 * BLOCK), F32),
        in_specs=[VMEM, VMEM, SMEM], out_specs=VMEM,
    )(bucket, in_window, rel_bias)


def bias_band_bwd(bucket, dbias):
    def body(bk_ref, db_ref, out_ref):
        bk = bk_ref[...]
        for h in range(N_Q_HEADS):
            db = db_ref[h]
            for b in range(N_BUCKETS):
                out_ref[h, b] = jnp.sum(jnp.where(bk == float(b), db, 0.0))

    return pl.pallas_call(
        body, name="bias_band_bwd",
        out_shape=jax.ShapeDtypeStruct((N_Q_HEADS, N_BUCKETS), F32),
        in_specs=[VMEM, VMEM], out_specs=SMEM,
    )(bucket, dbias)


def _window_sum(buf_ref, g, w, first):
    cols = slice(g * POOL_GROUP_DIM, (g + 1) * POOL_GROUP_DIM)
    acc = None
    for k in range(w):
        piece = buf_ref[first(k):first(k) + BLOCK, cols]
        acc = piece if acc is None else acc + piece
    return acc


def _inv_count(i, w):
    row = lax.broadcasted_iota(jnp.int32, (BLOCK, 1), 0)
    return 1.0 / jnp.minimum(i * BLOCK + row + 1, w).astype(F32)


def _fill_pool_input(i, ubuf, uc_ref, halo_ref):
    ubuf[0:HALO, :] = jnp.where(i > 0, halo_ref[...], 0.0)
    ubuf[HALO:, :] = uc_ref[...]


def _pooled(i, g, w, ubuf):
    cols = slice(g * POOL_GROUP_DIM, (g + 1) * POOL_GROUP_DIM)
    return _window_sum(ubuf, g, w, lambda k: HALO - k) * _inv_count(i, w) - ubuf[HALO:, cols]


def _head_variants(pair):
    low = lax.broadcasted_iota(jnp.int32, pair.shape, 1) < HEAD_DIM
    swapped = pltpu.roll(pair, HEAD_DIM, 1)
    zero = jnp.zeros_like(pair)
    pick = lambda c, a, b: jnp.where(c, a, b).astype(BF16)
    return [[pick(low, pair, zero), pick(low, zero, swapped)], [pick(low, swapped, zero), pick(low, zero, pair)]]


def _head_probs(i, hq, rows, s_ref, biasm_ref, sinks_ref):
    s = s_ref[hq, rows, :] * ATTN_SCALE + biasm_ref[hq, rows, :]
    col = lax.broadcasted_iota(jnp.int32, s.shape, 1)
    s = jnp.where((i == 0) & (col < BLOCK), NEG_INF, s)
    sink = sinks_ref[0, hq]
    m = jnp.maximum(jnp.max(s, axis=-1, keepdims=True), sink)
    p = jnp.exp(s - m)
    e_sink = jnp.exp(sink - m)
    inv = 1.0 / (jnp.sum(p, axis=-1, keepdims=True) + e_sink)
    return p * inv, e_sink * inv


def _mixer_in_specs(cur, prv):
    return [pl.BlockSpec((BLOCK, 512), lambda i: (cur(i), 0)),
            pl.BlockSpec((HALO, 512), lambda i: (jnp.maximum(cur(i) * (BLOCK // HALO) - 1, 0), 0)),
            pl.BlockSpec((BLOCK, 512), lambda i: (cur(i), 1)),
            pl.BlockSpec((BLOCK, 256), lambda i: (cur(i), 4)),
            pl.BlockSpec((BLOCK, 256), lambda i: (prv(i), 4))]


def _mixer_param_specs():
    return [pl.BlockSpec((N_Q_HEADS, BLOCK, 2 * BLOCK), lambda i: (0, 0, 0)), SMEM,
            pl.BlockSpec((4, POOL_GROUP_DIM, POOL_GROUP_DIM), lambda i: (0, 0, 0)),
            pl.BlockSpec((1, POOL_WIDTH), lambda i: (0, 0))]


def mixers_fwd(proj, biasm, sinks, w_pool, pool_scale, shards):
    t = proj.shape[0]
    nb = t // BLOCK
    ns = len(shards)

    def body(*refs):
        uc_ref, halo_ref, q_ref, kvc_ref, kvp_ref, biasm_ref, sinks_ref, wp_ref, sc_ref = refs[:9]
        shard_refs, out_ref, gathered_refs = refs[9:9 + ns], refs[9 + ns], refs[10 + ns:10 + 2 * ns]
        ubuf, s_all, p_all, send_sems, recv_sems, local_sems = refs[10 + 2 * ns:16 + 2 * ns]
        i = pl.program_id(0)
        start, finish = _gather_plan(shard_refs, gathered_refs, send_sems, recv_sems, local_sems, refs[16 + 2 * ns:])
        pl.when(i == 0)(start)

        _fill_pool_input(i, ubuf, uc_ref, halo_ref)
        for g, w in enumerate(POOL_WINDOWS):
            mixed = _dot(_pooled(i, g, w, ubuf).astype(BF16), wp_ref[g])
            cols = slice(g * POOL_GROUP_DIM, (g + 1) * POOL_GROUP_DIM)
            out_ref[:, cols] = (mixed * sc_ref[:, cols]).astype(BF16)
        kv = jnp.concatenate([kvp_ref[...], kvc_ref[...]], axis=0)
        k_var = _head_variants(kv[:, 0:2 * HEAD_DIM])
        v_var = _head_variants(kv[:, 2 * HEAD_DIM:])
        for hq in range(N_Q_HEADS):
            j, half, h = hq // 2, hq % 2, hq // GQA_GROUP
            q2 = q_ref[:, 2 * HEAD_DIM * j:2 * HEAD_DIM * (j + 1)].astype(BF16)
            s_all[hq] = _dot_nt(q2, k_var[h][half])
        for hq in range(N_Q_HEADS):
            for r in range(0, BLOCK, ROW_CHUNK):
                rows = slice(r, r + ROW_CHUNK)
                probs, _ = _head_probs(i, hq, rows, s_all, biasm_ref, sinks_ref)
                p_all[hq, rows, :] = probs.astype(BF16)
        for j in range(N_Q_HEADS // 2):
            h = 2 * j // GQA_GROUP
            acc = _dot(p_all[2 * j], v_var[h][0]) + _dot(p_all[2 * j + 1], v_var[h][1])
            out_ref[:, POOL_WIDTH + 2 * HEAD_DIM * j:POOL_WIDTH + 2 * HEAD_DIM * (j + 1)] = acc.astype(BF16)

        pl.when(i == max(nb - 4, 0))(finish.forward)
        pl.when(i == nb - 1)(finish.complete)

    return pl.pallas_call(
        body, name="mixers_fwd", grid=(nb,),
        out_shape=[jax.ShapeDtypeStruct((t, 2 * POOL_WIDTH), BF16)]
        + [jax.ShapeDtypeStruct((N_DEV, *sh.shape), sh.dtype) for sh in shards],
        in_specs=_mixer_in_specs(lambda i: i, lambda i: jnp.maximum(i - 1, 0)) + _mixer_param_specs() + [ANY] * ns,
        out_specs=[pl.BlockSpec((BLOCK, 2 * POOL_WIDTH), lambda i: (i, 0))] + [ANY] * ns,
        scratch_shapes=[pltpu.VMEM((HALO + BLOCK, POOL_WIDTH), F32), pltpu.VMEM((N_Q_HEADS, BLOCK, 2 * BLOCK), F32),
                        pltpu.VMEM((N_Q_HEADS, BLOCK, 2 * BLOCK), BF16)] + _gather_scratch(shards),
        compiler_params=_params(),
    )(proj, proj, proj, proj, proj, biasm, sinks, w_pool, pool_scale, *shards)


def outproj_norm(cat, w, x, g, g_next, shard, partial):
    t, d = x.shape
    tm = TOKEN_TILE
    last = t // tm - 1
    rows = [(partial.shape[1] - shard.shape[0], shard.shape[0])]

    def body(c_ref, w_ref, x_ref, g_ref, gn_ref, shard_ref, partial_ref, mix_ref, x1_ref, h2_ref, gathered_ref,
             send_sems, recv_sems, local_sems, bounce):
        i = pl.program_id(0)
        start, finish = _gather_plan([shard_ref], [gathered_ref], send_sems, recv_sems, local_sems, [bounce], rows)
        pl.when(i == 0)(start)
        mix = _dot(c_ref[...], w_ref[...])
        mix_ref[...] = mix
        x1 = x_ref[...] + (mix * _rstd(mix)) * g_ref[...]
        x1_ref[...] = x1
        h2_ref[...] = ((x1 * _rstd(x1)) * gn_ref[...]).astype(BF16)
        pl.when(i == max(last - 1, 0))(finish.forward)
        pl.when(i == last)(finish.complete)

    row = pl.BlockSpec((tm, d), lambda i: (i, 0))
    gain = pl.BlockSpec((1, d), lambda i: (0, 0))
    return pl.pallas_call(
        body, name="outproj_norm", grid=(t // tm,),
        out_shape=[jax.ShapeDtypeStruct((t, d), F32), jax.ShapeDtypeStruct((t, d), F32), jax.ShapeDtypeStruct((t, d), BF16),
                   jax.ShapeDtypeStruct(partial.shape, partial.dtype)],
        in_specs=[pl.BlockSpec((tm, cat.shape[1]), lambda i: (i, 0)), pl.BlockSpec(w.shape, lambda i: (0, 0)), row, gain, gain,
                  ANY, ANY],
        out_specs=[row, row, row, ANY],
        input_output_aliases={6: 3},
        scratch_shapes=_gather_scratch([shard]),
        compiler_params=_params(),
    )(cat, w, x, g, g_next, shard, partial)


def ffn_up(h, gate_t, up_t, down_shard):
    t, d = h.shape
    n = gate_t.shape[1]
    f = N_DEV * n
    tm, ts = FFN_TOKEN_TILE, FF_SHARDS_PER_TILE
    tn = ts * n
    steps = (f // tn, t // tm)

    def body(h_ref, wg_ref, wu_ref, shard_ref, gate_ref, up_ref, a_ref, gathered_ref,
             send_sems, recv_sems, local_sems, bounce):
        j, i = pl.program_id(0), pl.program_id(1)
        start, finish = _gather_plan([shard_ref], [gathered_ref], send_sems, recv_sems, local_sems, [bounce])
        pl.when((i == 0) & (j == 0))(start)

        hv = h_ref[...]
        gate = _dot_nt(hv, _merge_rows(wg_ref[...]))
        up = _dot_nt(hv, _merge_rows(wu_ref[...]))
        gate_ref[...] = gate.astype(BF16)
        up_ref[...] = up.astype(BF16)
        a_ref[...] = (gate * (1.0 / (1.0 + jnp.exp(-gate))) * up).astype(BF16)

        pl.when((j == steps[0] - 1) & (i == max(steps[1] - 2, 0)))(finish.forward)
        pl.when((j == steps[0] - 1) & (i == steps[1] - 1))(finish.complete)

    wide = pl.BlockSpec((tm, tn), lambda j, i: (i, j))
    return pl.pallas_call(
        body, name="ffn_up", grid=steps,
        out_shape=[jax.ShapeDtypeStruct((t, f), BF16)] * 3
        + [jax.ShapeDtypeStruct((N_DEV, *down_shard.shape), down_shard.dtype)],
        in_specs=[pl.BlockSpec((tm, d), lambda j, i: (i, 0)),
                  pl.BlockSpec((ts, n, d), lambda j, i: (j, 0, 0)),
                  pl.BlockSpec((ts, n, d), lambda j, i: (j, 0, 0)), ANY],
        out_specs=[wide, wide, wide, ANY],
        scratch_shapes=_gather_scratch([down_shard]),
        compiler_params=_params(),
    )(h, gate_t, up_t, down_shard)


def ffn_down_loss(a, w_down, x1, g, target):
    t, d = x1.shape
    tm = WIDE_K_TOKEN_TILE

    def body(a_ref, w_ref, x_ref, g_ref, t_ref, df_ref, dy_ref, dg_ref, loss_ref):
        @pl.when(pl.program_id(0) == 0)
        def _():
            dg_ref[...] = jnp.zeros_like(dg_ref)
            loss_ref[...] = jnp.zeros_like(loss_ref)

        f = _dot(a_ref[...], _merge_rows(w_ref[...]))
        r = _rstd(f)
        g = g_ref[...]
        err = x_ref[...] + (f * r) * g - t_ref[...]
        loss_ref[...] += 0.5 * jnp.sum(jnp.mean(err * err, axis=-1, keepdims=True))
        dy = err * (1.0 / d)
        dy_ref[...] = dy
        df, dg_rows = _norm_bwd(dy, f, r, g)
        df_ref[...] = df.astype(BF16)
        dg_ref[...] += _as_rows(jnp.sum(dg_rows, axis=0, keepdims=True))

    row = pl.BlockSpec((tm, d), lambda i: (i, 0))
    gain = pl.BlockSpec((1, d), lambda i: (0, 0))
    return pl.pallas_call(
        body, name="ffn_down_loss", grid=(t // tm,),
        out_shape=[jax.ShapeDtypeStruct((t, d), BF16), jax.ShapeDtypeStruct((t, d), F32),
                   jax.ShapeDtypeStruct((d // 128, 128), F32), jax.ShapeDtypeStruct((1, 128), F32)],
        in_specs=[pl.BlockSpec((tm, a.shape[1]), lambda i: (i, 0)), pl.BlockSpec(w_down.shape, lambda i: (0, 0, 0)), row, gain, row],
        out_specs=[row, row, pl.BlockSpec((d // 128, 128), lambda i: (0, 0)), pl.BlockSpec((1, 128), lambda i: (0, 0))],
        compiler_params=_params(),
    )(a, w_down, x1, g, target)


def ffn_down_bwd(df, w_down, gate, up):
    t, d = df.shape
    n = w_down.shape[1]
    f = gate.shape[1]
    tm, ts = FFN_TOKEN_TILE, FF_SHARDS_PER_TILE
    tn = ts * n

    def body(df_ref, w_ref, gate_ref, up_ref, dgate_ref, dup_ref):
        da = _dot_nt(df_ref[...], _merge_rows(w_ref[...]))
        gate = gate_ref[...].astype(F32)
        sig = 1.0 / (1.0 + jnp.exp(-gate))
        dgate_ref[...] = (da * up_ref[...].astype(F32) * (sig * (1.0 + gate * (1.0 - sig)))).astype(BF16)
        dup_ref[...] = (da * (gate * sig)).astype(BF16)

    wide = pl.BlockSpec((tm, tn), lambda j, i: (i, j))
    return pl.pallas_call(
        body, name="ffn_down_bwd", grid=(f // tn, t // tm),
        out_shape=[jax.ShapeDtypeStruct((t, f), BF16)] * 2,
        in_specs=[pl.BlockSpec((tm, d), lambda j, i: (i, 0)), pl.BlockSpec((ts, n, d), lambda j, i: (j, 0, 0)), wide, wide],
        out_specs=[wide, wide],
        compiler_params=_params(),
    )(df, w_down, gate, up)


def grad_rows(a, b, name, by_core=False):
    t, m = a.shape
    d = b.shape[1]
    r = m // N_DEV
    tt = TOKEN_TILE
    last = t // tt - 1
    out_shape = (2, N_CHIP, r, d) if by_core else (N_DEV, r, d)

    def body(a_ref, b_ref, out_ref, acc):
        k = pl.program_id(0)

        @pl.when(k == 0)
        def _():
            acc[...] = jnp.zeros_like(acc)

        acc[...] += _dot_tn(a_ref[...], b_ref[...])

        @pl.when(k == last)
        def _():
            if by_core:
                blocks = acc[...].reshape(N_CHIP, 2, r, d)
                for chip in range(N_CHIP):
                    for core in range(2):
                        out_ref[core, chip] = blocks[chip, core].astype(BF16)
            else:
                out_ref[...] = acc[...].reshape(out_shape).astype(BF16)

    return pl.pallas_call(
        body, name=name, grid=(t // tt,),
        out_shape=jax.ShapeDtypeStruct(out_shape, BF16),
        in_specs=[pl.BlockSpec((tt, m), lambda k: (k, 0)), pl.BlockSpec((tt, d), lambda k: (k, 0))],
        out_specs=pl.BlockSpec(out_shape, lambda k: (0,) * len(out_shape)),
        scratch_shapes=[pltpu.VMEM((m, d), F32)],
        compiler_params=_params(),
    )(a, b)


def grad_ffn(lhs, b, name, pair_parts=()):
    t, f = lhs[0].shape
    d = b.shape[1]
    nw = len(lhs)
    na = len(pair_parts)
    n = f // N_DEV
    tt, ts = TOKEN_TILE, FF_SHARDS_PER_TILE
    tn = ts * n
    steps = (f // tn, t // tt)

    def body(*refs):
        a_refs, b_ref, part_refs = refs[:nw], refs[nw], refs[nw + 1:nw + 1 + na]
        out_refs = refs[nw + 1 + na:2 * nw + 1 + na]
        got_refs = refs[2 * nw + 1 + na:2 * nw + 1 + 2 * na]
        acc = refs[2 * nw + 1 + 2 * na]
        i, k = pl.program_id(0), pl.program_id(1)
        if na:
            start, finish = _pair_plan(part_refs, got_refs, *refs[2 * nw + 2 + 2 * na:])
            pl.when((i == 0) & (k == 0))(start)

        @pl.when(k == 0)
        def _():
            acc[...] = jnp.zeros_like(acc)

        for w in range(nw):
            acc[w] += _dot_tn(a_refs[w][...], b_ref[...])

        @pl.when(k == steps[1] - 1)
        def _():
            for w in range(nw):
                blocks = acc[w].reshape(ts // 2, 2, n, d)
                for chip in range(ts // 2):
                    for core in range(2):
                        out_refs[w][core, chip] = blocks[chip, core].astype(BF16)

        if na:
            pl.when((i == steps[0] - 1) & (k == steps[1] - 1))(finish)

    out = pl.pallas_call(
        body, name=name, grid=steps,
        out_shape=[jax.ShapeDtypeStruct((2, N_CHIP, n, d), BF16)] * nw
        + [jax.ShapeDtypeStruct(p.shape[1:], p.dtype) for p in pair_parts],
        in_specs=[pl.BlockSpec((tt, tn), lambda i, k: (k, i))] * nw + [pl.BlockSpec((tt, d), lambda i, k: (k, 0))] + [ANY] * na,
        out_specs=[pl.BlockSpec((2, ts // 2, n, d), lambda i, k: (0, i, 0, 0))] * nw + [ANY] * na,
        scratch_shapes=[pltpu.VMEM((nw, tn, d), F32)]
        + ([pltpu.SemaphoreType.DMA((na,)), pltpu.SemaphoreType.DMA((na,))] if na else []),
        compiler_params=_params(),
    )(*lhs, b, *pair_parts)
    return out[:nw], out[nw:]


def ffn_up_bwd(dgate, dup, gate_t, up_t, x1, g_ffn, dy, mix, g_mix, chip_parts):
    t, d = x1.shape
    n = gate_t.shape[1]
    f = N_DEV * n
    tm = WIDE_K_TOKEN_TILE
    na = len(chip_parts)
    last = t // tm - 1

    def body(*refs):
        dg_ref, du_ref, wg_ref, wu_ref, x_ref, gf_ref, dy_ref, mix_ref, gm_ref = refs[:9]
        part_refs = refs[9:9 + na]
        dx1_ref, dmix_ref, dgf_ref, dgm_ref = refs[9 + na:13 + na]
        slot_refs = refs[13 + na:13 + 2 * na]
        send_sems, recv_sems, local_sems = refs[13 + 2 * na:16 + 2 * na]
        i = pl.program_id(0)
        start, finish = _chip_exchange_plan(part_refs, slot_refs, send_sems, recv_sems, local_sems, refs[16 + 2 * na:])

        @pl.when(i == 0)
        def _():
            start()
            dgf_ref[...] = jnp.zeros_like(dgf_ref)
            dgm_ref[...] = jnp.zeros_like(dgm_ref)

        dh = _dot(dg_ref[...], _merge_rows(wg_ref[...])) + _dot(du_ref[...], _merge_rows(wu_ref[...]))
        x1 = x_ref[...]
        dx, dgf_rows = _norm_bwd(dh, x1, _rstd(x1), gf_ref[...])
        dx1 = dy_ref[...] + dx
        dx1_ref[...] = dx1
        dgf_ref[...] += _as_rows(jnp.sum(dgf_rows, axis=0, keepdims=True))
        mix = mix_ref[...]
        dmix, dgm_rows = _norm_bwd(dx1, mix, _rstd(mix), gm_ref[...])
        dmix_ref[...] = dmix.astype(BF16)
        dgm_ref[...] += _as_rows(jnp.sum(dgm_rows, axis=0, keepdims=True))
        pl.when(i == last)(finish)

    row = pl.BlockSpec((tm, d), lambda i: (i, 0))
    wide = pl.BlockSpec((tm, f), lambda i: (i, 0))
    gain = pl.BlockSpec((1, d), lambda i: (0, 0))
    gain_rows = pl.BlockSpec((d // 128, 128), lambda i: (0, 0))
    whole = pl.BlockSpec((N_DEV, n, d), lambda i: (0, 0, 0), pipeline_mode=pl.Buffered(1))
    out = pl.pallas_call(
        body, name="ffn_up_bwd", grid=(t // tm,),
        out_shape=[jax.ShapeDtypeStruct((t, d), F32), jax.ShapeDtypeStruct((t, d), BF16),
                   jax.ShapeDtypeStruct((d // 128, 128), F32), jax.ShapeDtypeStruct((d // 128, 128), F32)]
        + [jax.ShapeDtypeStruct(p.shape, p.dtype) for p in chip_parts],
        in_specs=[wide, wide, whole, whole, row, gain, row, row, gain] + [ANY] * na,
        out_specs=[row, row, gain_rows, gain_rows] + [ANY] * na,
        scratch_shapes=_chip_exchange_scratch(chip_parts),
        compiler_params=_params(),
    )(dgate, dup, gate_t, up_t, x1, g_ffn, dy, mix, g_mix, *chip_parts)
    return out[:4], out[4:]


def outproj_bwd(dmix, w_out):
    t, d = dmix.shape
    tm = TOKEN_TILE

    def body(dm_ref, w_ref, out_ref):
        out_ref[...] = _dot_nt(dm_ref[...], w_ref[...])

    return pl.pallas_call(
        body, name="outproj_bwd", grid=(t // tm,),
        out_shape=jax.ShapeDtypeStruct((t, w_out.shape[0]), F32),
        in_specs=[pl.BlockSpec((tm, d), lambda i: (i, 0)), pl.BlockSpec(w_out.shape, lambda i: (0, 0))],
        out_specs=pl.BlockSpec((tm, w_out.shape[0]), lambda i: (i, 0)),
        compiler_params=_params(),
    )(dmix, w_out)


def mixers_bwd(proj, dcat, biasm, sinks, w_pool, pool_scale, ffn_parts):
    t = proj.shape[0]
    nb = t // BLOCK
    na = len(ffn_parts)

    def body(*refs):
        (uc_ref, halo_ref, q_ref, kvc_ref, kvp_ref, dcat_ref, biasm_ref, sinks_ref, wp_ref, sc_ref) = refs[:10]
        part_refs = refs[10:10 + na]
        dproj_ref, dbias_ref, dsink_ref, dwp_ref, dsc_ref = refs[10 + na:15 + na]
        slot_refs = refs[15 + na:15 + 2 * na]
        ubuf, dbuf, c_u, c_q, c_kv, s_all, dp_all, ds_all, p_all = refs[15 + 2 * na:24 + 2 * na]
        send_sems, recv_sems, local_sems = refs[24 + 2 * na:27 + 2 * na]
        bounce = refs[27 + 2 * na:]
        i = pl.program_id(0)
        lane = lax.broadcasted_iota(jnp.int32, (1, 128), 1)
        start, finish = _chip_exchange_plan(part_refs, slot_refs, send_sems, recv_sems, local_sems, bounce)

        @pl.when(i == 0)
        def _():
            start()
            dbias_ref[...] = jnp.zeros_like(dbias_ref)
            dwp_ref[...] = jnp.zeros_like(dwp_ref)
            dsc_ref[...] = jnp.zeros_like(dsc_ref)
            dsink_ref[...] = jnp.zeros_like(dsink_ref)
            dbuf[...] = jnp.zeros_like(dbuf)
            c_u[...] = jnp.zeros_like(c_u)
            c_q[...] = jnp.zeros_like(c_q)
            c_kv[...] = jnp.zeros_like(c_kv)

        @pl.when(i < nb)
        def _():
            _fill_pool_input(i, ubuf, uc_ref, halo_ref)
            for g, w in enumerate(POOL_WINDOWS):
                cols = slice(g * POOL_GROUP_DIM, (g + 1) * POOL_GROUP_DIM)
                pooled = _pooled(i, g, w, ubuf).astype(BF16)
                mixed = _dot(pooled, wp_ref[g])
                dout = dcat_ref[:, cols]
                dsc_ref[g:g + 1, :] += jnp.sum(dout * mixed, axis=0, keepdims=True)
                dmixed = (dout * sc_ref[:, cols]).astype(BF16)
                dwp_ref[g] += _dot_tn(pooled, dmixed)
                dpooled = _dot_nt(dmixed, wp_ref[g])
                scaled = dpooled * _inv_count(i, w)
                dbuf[BLOCK:, cols] = scaled[0:HALO]
                dproj_ref[:, cols] = (_window_sum(dbuf, g, w, lambda k: k) + c_u[:, cols]).astype(BF16)
                dbuf[0:BLOCK, cols] = scaled
                c_u[:, cols] = -dpooled

            kv = jnp.concatenate([kvp_ref[...], kvc_ref[...]], axis=0)
            k_var = _head_variants(kv[:, 0:2 * HEAD_DIM])
            v_var = _head_variants(kv[:, 2 * HEAD_DIM:])
            q2s = [q_ref[:, 2 * HEAD_DIM * j:2 * HEAD_DIM * (j + 1)].astype(BF16) for j in range(N_Q_HEADS // 2)]
            do2s = [dcat_ref[:, POOL_WIDTH + 2 * HEAD_DIM * j:POOL_WIDTH + 2 * HEAD_DIM * (j + 1)].astype(BF16)
                    for j in range(N_Q_HEADS // 2)]
            slot = lambda hq: 4 * (hq // GQA_GROUP) + 2 * (hq % 2) + (hq % GQA_GROUP) // 2
            for hq in range(N_Q_HEADS):
                j, half, h = hq // 2, hq % 2, hq // GQA_GROUP
                s_all[hq] = _dot_nt(q2s[j], k_var[h][half])
                dp_all[hq] = _dot_nt(do2s[j], v_var[h][half])
            dsink_row = jnp.zeros((1, 128), F32)
            for hq in range(N_Q_HEADS):
                dsink = 0.0
                for r in range(0, BLOCK, ROW_CHUNK):
                    rows = slice(r, r + ROW_CHUNK)
                    probs, p_sink = _head_probs(i, hq, rows, s_all, biasm_ref, sinks_ref)
                    dp = dp_all[hq, rows, :]
                    delta = jnp.sum(probs * dp, axis=-1, keepdims=True)
                    ds = probs * (dp - delta)
                    dbias_ref[hq, rows, :] += ds
                    dsink = dsink + jnp.sum(p_sink * delta)
                    ds_all[slot(hq), rows, :] = (ds * ATTN_SCALE).astype(BF16)
                    p_all[slot(hq), rows, :] = probs.astype(BF16)
                dsink_row = dsink_row - jnp.where(lane == hq, dsink, 0.0)
            dsink_ref[...] += dsink_row
            dq2 = [None] * (N_Q_HEADS // 2)
            for hq in range(N_Q_HEADS):
                j, half, h = hq // 2, hq % 2, hq // GQA_GROUP
                dq = _dot(ds_all[slot(hq)], k_var[h][half])
                dq2[j] = dq if dq2[j] is None else dq2[j] + dq
            low = lax.broadcasted_iota(jnp.int32, (2 * BLOCK, 2 * HEAD_DIM), 1) < HEAD_DIM
            dk_half, dv_half = [[None, None], [None, None]], [[None, None], [None, None]]
            for h in range(N_KV_HEADS):
                for half in range(2):
                    heads = [hq for hq in range(GQA_GROUP * h, GQA_GROUP * (h + 1)) if hq % 2 == half]
                    base = slot(heads[0])
                    q_rows = jnp.concatenate([q2s[hq // 2] for hq in heads], axis=0)
                    do_rows = jnp.concatenate([do2s[hq // 2] for hq in heads], axis=0)
                    dk_half[h][half] = _dot_tn(_merge_rows(ds_all[base:base + 2]), q_rows)
                    dv_half[h][half] = _dot_tn(_merge_rows(p_all[base:base + 2]), do_rows)

            def pair_of(halves):
                return jnp.where(low, halves[0][0] + pltpu.roll(halves[0][1], HEAD_DIM, 1),
                                 halves[1][1] + pltpu.roll(halves[1][0], HEAD_DIM, 1))

            dkv = jnp.concatenate([pair_of(dk_half), pair_of(dv_half)], axis=1)
            dproj_ref[:, POOL_WIDTH:2 * POOL_WIDTH] = c_q[...].astype(BF16)
            dproj_ref[:, 2 * POOL_WIDTH:] = (c_kv[...] + dkv[0:BLOCK]).astype(BF16)
            c_q[...] = jnp.concatenate(dq2, axis=1)
            c_kv[...] = dkv[BLOCK:]

        @pl.when(i == nb)
        def _():
            dbuf[BLOCK:, :] = jnp.zeros((HALO, POOL_WIDTH), F32)
            for g, w in enumerate(POOL_WINDOWS):
                cols = slice(g * POOL_GROUP_DIM, (g + 1) * POOL_GROUP_DIM)
                dproj_ref[:, cols] = (_window_sum(dbuf, g, w, lambda k: k) + c_u[:, cols]).astype(BF16)
            dproj_ref[:, POOL_WIDTH:2 * POOL_WIDTH] = c_q[...].astype(BF16)
            dproj_ref[:, 2 * POOL_WIDTH:] = c_kv[...].astype(BF16)
            finish()

    cur = lambda i: jnp.minimum(i, nb - 1)
    prv = lambda i: jnp.maximum(jnp.minimum(i, nb - 1) - 1, 0)
    out = pl.pallas_call(
        body, name="mixers_bwd", grid=(nb + 1,),
        out_shape=[jax.ShapeDtypeStruct((t, proj.shape[1]), BF16),
                   jax.ShapeDtypeStruct((N_Q_HEADS, BLOCK, 2 * BLOCK), F32),
                   jax.ShapeDtypeStruct((1, 128), F32),
                   jax.ShapeDtypeStruct((4, POOL_GROUP_DIM, POOL_GROUP_DIM), F32),
                   jax.ShapeDtypeStruct((len(POOL_WINDOWS), POOL_GROUP_DIM), F32)]
        + [jax.ShapeDtypeStruct(p.shape, p.dtype) for p in ffn_parts],
        in_specs=_mixer_in_specs(cur, prv) + [pl.BlockSpec((BLOCK, 2 * POOL_WIDTH), lambda i: (cur(i), 0))]
        + _mixer_param_specs() + [ANY] * na,
        out_specs=[pl.BlockSpec((BLOCK, proj.shape[1]), lambda i: (jnp.maximum(i - 1, 0), 0)),
                   pl.BlockSpec((N_Q_HEADS, BLOCK, 2 * BLOCK), lambda i: (0, 0, 0)),
                   pl.BlockSpec((1, 128), lambda i: (0, 0)),
                   pl.BlockSpec((4, POOL_GROUP_DIM, POOL_GROUP_DIM), lambda i: (0, 0, 0)),
                   pl.BlockSpec((len(POOL_WINDOWS), POOL_GROUP_DIM), lambda i: (0, 0))] + [ANY] * na,
        scratch_shapes=[pltpu.VMEM((HALO + BLOCK, POOL_WIDTH), F32), pltpu.VMEM((BLOCK + HALO, POOL_WIDTH), F32),
                        pltpu.VMEM((BLOCK, POOL_WIDTH), F32), pltpu.VMEM((BLOCK, POOL_WIDTH), F32),
                        pltpu.VMEM((BLOCK, 256), F32),
                        pltpu.VMEM((N_Q_HEADS, BLOCK, 2 * BLOCK), F32), pltpu.VMEM((N_Q_HEADS, BLOCK, 2 * BLOCK), F32),
                        pltpu.VMEM((N_Q_HEADS, BLOCK, 2 * BLOCK), BF16), pltpu.VMEM((N_Q_HEADS, BLOCK, 2 * BLOCK), BF16)]
        + _chip_exchange_scratch(ffn_parts),
        compiler_params=_params(),
    )(proj, proj, proj, proj, proj, dcat, biasm, sinks, w_pool, pool_scale, *ffn_parts)
    return out[:5], out[5:]


def inproj_bwd(dproj, w_in_t, x, g, dx1):
    t, d = x.shape
    n = dproj.shape[1]
    tm = TOKEN_TILE

    def body(dp_ref, w_ref, x_ref, g_ref, dx1_ref, dx_ref, dg_ref):
        @pl.when(pl.program_id(0) == 0)
        def _():
            dg_ref[...] = jnp.zeros_like(dg_ref)

        dh = _dot(dp_ref[...], w_ref[...])
        xv = x_ref[...]
        dx, dg_rows = _norm_bwd(dh, xv, _rstd(xv), g_ref[...])
        dx_ref[...] = dx1_ref[...] + dx
        dg_ref[...] += _as_rows(jnp.sum(dg_rows, axis=0, keepdims=True))

    row = pl.BlockSpec((tm, d), lambda i: (i, 0))
    gain = pl.BlockSpec((1, d), lambda i: (0, 0))
    return pl.pallas_call(
        body, name="inproj_bwd", grid=(t // tm,),
        out_shape=[jax.ShapeDtypeStruct((t, d), F32), jax.ShapeDtypeStruct((d // 128, 128), F32)],
        in_specs=[pl.BlockSpec((tm, n), lambda i: (i, 0)), pl.BlockSpec(w_in_t.shape, lambda i: (0, 0)), row, gain, row],
        out_specs=[row, pl.BlockSpec((d // 128, 128), lambda i: (0, 0))],
        compiler_params=_params(),
    )(dproj, w_in_t, x, g, dx1)


def _bucket_band():
    qi = jnp.arange(BLOCK)[:, None]
    kj = jnp.arange(2 * BLOCK)[None, :]
    dist = qi + BLOCK - kj
    n = jnp.maximum(dist, 0)
    nf = jnp.maximum(n, 1).astype(F32)
    large = MAX_EXACT + (jnp.log(nf / MAX_EXACT) / np.float32(np.log(MAX_DISTANCE / MAX_EXACT))
                         * (N_BUCKETS - MAX_EXACT)).astype(jnp.int32)
    large = jnp.minimum(large, N_BUCKETS - 1)
    bucket = jnp.where(n < MAX_EXACT, n, large)
    in_window = (dist >= 0) & (dist < BLOCK)
    return bucket.astype(F32), in_window.astype(F32)


def kernel(x, g_pre_mix, w_in, w_pool, pool_scale, rel_bias, sinks, w_out, g_post_mix, g_pre_ffn, w_gate, w_up, w_down, g_post_ffn, loss_target, m_g_pre_mix, m_w_in, m_w_pool, m_pool_scale, m_rel_bias, m_sinks, m_w_out, m_g_post_mix, m_g_pre_ffn, m_w_gate, m_w_up, m_w_down, m_g_post_ffn, v_g_pre_mix, v_w_in, v_w_pool, v_pool_scale, v_rel_bias, v_sinks, v_w_out, v_g_post_mix, v_g_pre_ffn, v_w_gate, v_w_up, v_w_down, v_g_post_ffn):
    d = x.shape[-1]
    xs, target = x[0], loss_target[0]

    w_in_ts = w_in[0].T.astype(BF16)
    w_out_s = w_out[0].astype(BF16)
    gate_ts = w_gate[0].T.astype(BF16)
    up_ts = w_up[0].T.astype(BF16)
    w_down_s = w_down[0].astype(BF16)

    bucket, in_window = _bucket_band()
    biasm = bias_band(bucket, in_window, rel_bias)
    w_pool_b = w_pool[0].astype(BF16)
    half = up_ts.shape[0] // 2
    proj, h1, w_in_t, up_t = norm_inproj(xs, g_pre_mix, w_in_ts, up_ts[:half], up_ts.shape[0])
    w_in_t = w_in_t.reshape(-1, d)
    cat, gate_t, w_out_f = mixers_fwd(proj, biasm, sinks, w_pool_b, pool_scale, [gate_ts, w_out_s])
    w_out_f = w_out_f.reshape(-1, d)
    mix, x1, h2, up_t = outproj_norm(cat, w_out_f, xs, g_post_mix, g_pre_ffn, up_ts[half:], up_t)
    gate, up, act, w_down_f = ffn_up(h2, gate_t, up_t, w_down_s)
    df, dy, dg_post_ffn, loss_part = ffn_down_loss(act, w_down_f, x1, g_post_ffn, target)

    def pair_sum(parts, tag):
        return pair_add(parts, pair_exchange(parts, "pair_exchange_" + tag), "pair_add_" + tag)

    dgate, dup = ffn_down_bwd(df, w_down_f, gate, up)
    (d_gate, d_up), _ = grad_ffn([dgate, dup], h2, "grad_w_gate_up")
    (d_down,), got_gate_up = grad_ffn([act], df, "grad_w_down", [d_gate, d_up])
    q_gate, q_up, q_down_a, q_down_b = pair_add(
        [d_gate, d_up, d_down], [*got_gate_up, *pair_exchange([d_down], "pair_exchange_down")], "pair_add_ffn",
        split_last=True)
    (dx1, dmix, dg_pre_ffn, dg_post_mix), (gate_slots, down_a_slots) = ffn_up_bwd(
        dgate, dup, gate_t, up_t, x1, g_pre_ffn, dy, mix, g_post_mix, [q_gate, q_down_a])
    dcat = outproj_bwd(dmix, w_out_f)
    d_out = grad_rows(cat, dmix, "grad_w_out", by_core=True)
    q_out, = pair_sum([d_out], "out")
    (dproj, dbias, dsinks, dw_pool, dpool_scale), (up_slots, out_slots, down_b_slots) = mixers_bwd(
        proj, dcat, biasm, sinks, w_pool_b, pool_scale, [q_up, q_out, q_down_b])
    drel_bias = bias_band_bwd(bucket, dbias)
    grad_x, dg_pre_mix = inproj_bwd(dproj, w_in_t, xs, g_pre_mix, dx1)

    small_w = [g_pre_mix, g_post_mix, g_pre_ffn, g_post_ffn, pool_scale, sinks, w_pool, rel_bias.T]
    small_m = [m_g_pre_mix, m_g_post_mix, m_g_pre_ffn, m_g_post_ffn, m_pool_scale, m_sinks, m_w_pool, m_rel_bias.T]
    small_v = [v_g_pre_mix, v_g_post_mix, v_g_pre_ffn, v_g_post_ffn, v_pool_scale, v_sinks, v_w_pool, v_rel_bias.T]
    d_in_t, total, total_rb = grad_w_in_small_reduce(
        dproj, h1, [dg_pre_mix, dg_post_mix, dg_pre_ffn, dg_post_ffn], dpool_scale, dsinks, loss_part, dw_pool, drel_bias)
    g_in_t = reduce_w_in(d_in_t)
    loss_row, sm = small_adamw(total, total_rb, small_w, small_m, small_v)
    sm[7] = [r.T for r in sm[7]]
    big_w = [w_in[0].T, w_out[0], w_gate[0].T, w_up[0].T, w_down[0]]
    big_m = [m_w_in[0].T, m_w_out[0], m_w_gate[0].T, m_w_up[0].T, m_w_down[0]]
    big_v = [v_w_in[0].T, v_w_out[0], v_w_gate[0].T, v_w_up[0].T, v_w_down[0]]
    upd = sum_adamw([out_slots, gate_slots, up_slots, (down_a_slots, down_b_slots)], big_w[1:], big_m[1:], big_v[1:],
                    "sum_adamw")
    upd = [[g_in_t, *adamw_update(big_w[:1], [g_in_t], big_m[:1], big_v[:1], "adamw_in")[0]], *upd]
    back = lambda k, a: (a.T if k in (0, 2, 3) else a)[None]
    big = [[back(k, u) for u in upd[k]] for k in range(5)]

    def ordered(kind):
        s, b = [p[kind] for p in sm], [p[kind] for p in big]
        return [s[0], b[0], s[6], s[4], s[7], s[5], b[1], s[1], s[2], b[2], b[3], b[4], s[3]]

    return (loss_row[0, 0], grad_x[None], *ordered(0), *ordered(1), *ordered(2), *ordered(3))
```

```python
import numpy as np
import jax
import jax.numpy as jnp
from jax import lax
from jax.experimental import pallas as pl
from jax.experimental.pallas import tpu as pltpu

F32 = jnp.float32
BF16 = jnp.bfloat16

N_DEV = 8
N_CHIP = 4
POOL_WIDTH = 512
POOL_WINDOWS = (2, 4, 8, 16)
POOL_GROUP_DIM = 128
HEAD_DIM = 64
N_Q_HEADS = 8
N_KV_HEADS = 2
GQA_GROUP = 4
BLOCK = 128
HALO = 16
ROW_CHUNK = 32
N_BUCKETS = 32
MAX_EXACT = 16
MAX_DISTANCE = 128
EPS = 1e-6
NEG_INF = -1e30
ATTN_SCALE = float(1.0 / np.sqrt(np.float32(HEAD_DIM)))

ADAM_LR = 0.001
ADAM_B1 = 0.9
ADAM_B2 = 0.999
ADAM_EPS = 1e-08
ADAM_WD = 0.01
ADAM_STEP = 10

TOKEN_TILE = 1024
WIDE_K_TOKEN_TILE = 512
FFN_TOKEN_TILE = 1024
FF_SHARDS_PER_TILE = 4
VMEM_LIMIT = 56 * 1024 * 1024
MESH = pl.DeviceIdType.MESH
ANY = pl.BlockSpec(memory_space=pl.ANY)
VMEM = pl.BlockSpec(memory_space=pltpu.VMEM)
SMEM = pl.BlockSpec(memory_space=pltpu.SMEM)


def _params(**kw):
    return pltpu.CompilerParams(vmem_limit_bytes=VMEM_LIMIT, **kw)


def _dot(a, b):
    return jnp.dot(a, b, preferred_element_type=F32)


def _dot_nt(a, b):
    return lax.dot_general(a, b, (((1,), (1,)), ((), ())), preferred_element_type=F32)


def _dot_tn(a, b):
    return lax.dot_general(a, b, (((0,), (0,)), ((), ())), preferred_element_type=F32)


def _rstd(v):
    return lax.rsqrt(jnp.mean(v * v, axis=-1, keepdims=True) + EPS)


def _norm_bwd(dout, v, r, g):
    vn = v * r
    dn = dout * g
    dv = r * (dn - vn * jnp.mean(dn * vn, axis=-1, keepdims=True))
    return dv, dout * vn


def _as_rows(v):
    return jnp.concatenate([v[:, k:k + 128] for k in range(0, v.shape[1], 128)], axis=0)


def _as_lanes(rows):
    return jnp.concatenate([rows[k:k + 1, :] for k in range(rows.shape[0])], axis=1)


def _merge_rows(value):
    s, r, c_ = value.shape
    return value.reshape(s * r, c_)


def _gather_plan(srcs, outs, send_sems, recv_sems, local_sems=None, bounce=None, rows=None):
    n = len(srcs)
    x, y, c = lax.axis_index("x"), lax.axis_index("y"), lax.axis_index("c")
    me, sibling = (x, y, c), (x, y, 1 - c)
    chips = [(1 - x, y), (x, 1 - y), (1 - x, 1 - y)]

    def slot(a, px, py, pc):
        whole = outs[a].at[4 * px + 2 * py + pc]
        return whole if rows is None or rows[a] is None else whole.at[pl.ds(*rows[a])]

    def copy(a, k, block, to, from_src=False):
        return pltpu.make_async_remote_copy(
            src_ref=srcs[a] if from_src else slot(a, *block), dst_ref=slot(a, *block),
            send_sem=send_sems.at[k * n + a], recv_sem=recv_sems.at[k * n + a], device_id=to, device_id_type=MESH)

    def own_in(a):
        return pltpu.make_async_copy(srcs[a], bounce[a], local_sems.at[a])

    def own_out(a):
        return pltpu.make_async_copy(bounce[a], slot(a, *me), local_sems.at[a])

    def first(a):
        return [copy(a, 0, me, sibling, True)] + [copy(a, 1 + j, me, (*chip, c), True) for j, chip in enumerate(chips)]

    def passed(a, j):
        return copy(a, 4 + j, (*chips[j], c), sibling)

    def start():
        for a in range(n):
            if bounce is not None:
                own_in(a).start()
            for cp in first(a):
                cp.start()

    def forward():
        if bounce is not None:
            for a in range(n):
                own_in(a).wait()
                own_out(a).start()
        for j, chip in enumerate(chips):
            for a in range(n):
                copy(a, 1 + j, (*chip, c), me).wait_recv()
                passed(a, j).start()

    def complete():
        for a in range(n):
            copy(a, 0, sibling, me).wait_recv()
            for j, chip in enumerate(chips):
                copy(a, 4 + j, (*chip, 1 - c), me).wait_recv()
        for a in range(n):
            for cp in first(a) + [passed(a, j) for j in range(3)]:
                cp.wait_send()
            if bounce is not None:
                own_out(a).wait()

    def finish():
        forward()
        complete()

    finish.forward, finish.complete = forward, complete
    return start, finish


def _gather_scratch(shards):
    n = len(shards)
    return [pltpu.SemaphoreType.DMA((7 * n,)), pltpu.SemaphoreType.DMA((7 * n,)), pltpu.SemaphoreType.DMA((n,))] \
        + [pltpu.VMEM(s.shape, s.dtype) for s in shards]


def _chip_exchange_plan(srcs, outs, send_sems, recv_sems, local_sems, bounce):
    n = len(srcs)
    x, y, c = lax.axis_index("x"), lax.axis_index("y"), lax.axis_index("c")
    my_chip = 2 * x + y

    def copies():
        out = []
        for a in range(n):
            for k in range(1, N_CHIP):
                px, py = x ^ (k >> 1), y ^ (k & 1)
                out.append(pltpu.make_async_remote_copy(
                    src_ref=srcs[a].at[2 * px + py], dst_ref=outs[a].at[my_chip],
                    send_sem=send_sems.at[(k - 1) * n + a], recv_sem=recv_sems.at[(k - 1) * n + a],
                    device_id=(px, py, c), device_id_type=MESH))
        return out

    def own_in(a):
        return pltpu.make_async_copy(srcs[a].at[my_chip], bounce[a], local_sems.at[a])

    def own_out(a):
        return pltpu.make_async_copy(bounce[a], outs[a].at[my_chip], local_sems.at[a])

    def start():
        for a in range(n):
            own_in(a).start()
        for cp in copies():
            cp.start()

    def finish():
        for a in range(n):
            own_in(a).wait()
            own_out(a).start()
        for cp in copies():
            cp.wait()
        for a in range(n):
            own_out(a).wait()

    return start, finish


def _chip_exchange_scratch(parts):
    n = len(parts)
    return [pltpu.SemaphoreType.DMA((3 * n,)), pltpu.SemaphoreType.DMA((3 * n,)), pltpu.SemaphoreType.DMA((n,))] \
        + [pltpu.VMEM(p.shape[1:], p.dtype) for p in parts]


def _pair_plan(srcs, outs, send_sems, recv_sems):
    x, y, c = lax.axis_index("x"), lax.axis_index("y"), lax.axis_index("c")

    def copies():
        return [pltpu.make_async_remote_copy(
            src_ref=srcs[a].at[1 - c], dst_ref=outs[a], send_sem=send_sems.at[a], recv_sem=recv_sems.at[a],
            device_id=(x, y, 1 - c), device_id_type=MESH) for a in range(len(srcs))]

    def start():
        for cp in copies():
            cp.start()

    def finish():
        for cp in copies():
            cp.wait()

    return start, finish


def pair_exchange(parts, name):
    n = len(parts)

    def body(*refs):
        start, finish = _pair_plan(refs[:n], refs[n:2 * n], *refs[2 * n:])
        start()
        finish()

    return pl.pallas_call(
        body, name=name, out_shape=[jax.ShapeDtypeStruct(p.shape[1:], p.dtype) for p in parts],
        in_specs=[ANY] * n, out_specs=[ANY] * n,
        scratch_shapes=[pltpu.SemaphoreType.DMA((n,)), pltpu.SemaphoreType.DMA((n,))],
    )(*parts)


def pair_add(parts, got, name):
    n = len(parts)

    def body(core_ref, *refs):
        for a in range(n):
            refs[2 * n + a][...] = (refs[a][...].astype(F32) + refs[n + a][...].astype(F32)).astype(BF16)

    def own(p):
        zeros = (0,) * (p.ndim - 2)
        return pl.BlockSpec((None, 1, *p.shape[2:]), lambda i, core: (core[0], i, *zeros))

    def plain(p):
        zeros = (0,) * (p.ndim - 1)
        return pl.BlockSpec((1, *p.shape[1:]), lambda i, core: (i, *zeros))

    core = lax.axis_index("c").astype(jnp.int32).reshape(1)
    return pl.pallas_call(
        body, name=name,
        grid_spec=pltpu.PrefetchScalarGridSpec(
            num_scalar_prefetch=1, grid=(got[0].shape[0],),
            in_specs=[own(p) for p in parts] + [plain(p) for p in got], out_specs=[plain(p) for p in got]),
        out_shape=[jax.ShapeDtypeStruct(p.shape, BF16) for p in got],
        compiler_params=_params(),
    )(core, *parts, *got)


def _adamw(w, g, m, v):
    m2 = ADAM_B1 * m + (1.0 - ADAM_B1) * g
    v2 = ADAM_B2 * v + (1.0 - ADAM_B2) * (g * g)
    m_hat = m2 / (1.0 - ADAM_B1 ** ADAM_STEP)
    v_hat = v2 / (1.0 - ADAM_B2 ** ADAM_STEP)
    delta = -ADAM_LR * (m_hat / (jnp.sqrt(v_hat) + ADAM_EPS) + ADAM_WD * w)
    return delta, m2, v2


def sum_adamw(slots, ws, ms, vs, name):
    n = len(ws)
    halves = 2

    def body(*refs):
        for a in range(n):
            total = refs[a][0].astype(F32)
            for s in range(1, slots[a].shape[0]):
                total = total + refs[a][s].astype(F32)
            delta, m2, v2 = _adamw(refs[n + a][...], total, refs[2 * n + a][...], refs[3 * n + a][...])
            for q, val in enumerate((total, delta, m2, v2)):
                refs[4 * n + 4 * a + q][...] = val

    def rows(w):
        return pl.BlockSpec((w.shape[0] // halves, w.shape[1]), lambda i: (i, 0))

    def slot_rows(p):
        return pl.BlockSpec((p.shape[0], p.shape[1] // halves, p.shape[2]), lambda i: (0, i, 0))

    out = pl.pallas_call(
        body, name=name, grid=(halves,),
        out_shape=[jax.ShapeDtypeStruct(w.shape, F32) for w in ws for _ in range(4)],
        in_specs=[slot_rows(p) for p in slots] + [rows(w) for w in ws] * 3,
        out_specs=[rows(w) for w in ws for _ in range(4)],
        compiler_params=_params(),
    )(*slots, *ws, *ms, *vs)
    return [out[4 * a:4 * a + 4] for a in range(n)]


def adamw_update(ws, gs, ms, vs, name):
    n = len(ws)

    def body(*refs):
        for a in range(n):
            delta, m2, v2 = _adamw(refs[a][...], refs[n + a][...], refs[2 * n + a][...], refs[3 * n + a][...])
            refs[4 * n + 3 * a][...] = delta
            refs[4 * n + 3 * a + 1][...] = m2
            refs[4 * n + 3 * a + 2][...] = v2

    out = pl.pallas_call(
        body, name=name,
        out_shape=[jax.ShapeDtypeStruct(w.shape, F32) for w in ws for _ in range(3)],
        in_specs=[VMEM] * (4 * n), out_specs=[VMEM] * (3 * n),
        compiler_params=_params(),
    )(*ws, *gs, *ms, *vs)
    return [out[3 * a:3 * a + 3] for a in range(n)]


GAIN_ROWS = 8
ROW_POOL_SCALE = 4 * GAIN_ROWS
ROW_SINKS = ROW_POOL_SCALE + 4
ROW_LOSS = ROW_SINKS + 1
ROW_W_POOL = 40
SMALL_ROWS = ROW_W_POOL + 4 * POOL_GROUP_DIM


def grad_w_in_small_reduce(a, b, gains, dpool_scale, dsinks, loss_part, dw_pool, drel_bias):
    t, m = a.shape
    d = b.shape[1]
    r = m // N_DEV
    tt = TOKEN_TILE
    last = t // tt - 1

    def body(a_ref, b_ref, g0, g1, g2, g3, dsc_ref, dsink_ref, loss_ref, dwp_ref, drb_ref, out_ref, total_ref, total_rb_ref,
             acc, stage, gat, gat_rb, g_send, g_recv):
        k = pl.program_id(0)
        x, y, c = lax.axis_index("x"), lax.axis_index("y"), lax.axis_index("c")
        start, finish = _gather_plan([stage, drb_ref], [gat, gat_rb], g_send, g_recv)

        @pl.when(k == 0)
        def _():
            for q, g_ref in enumerate((g0, g1, g2, g3)):
                stage[GAIN_ROWS * q:GAIN_ROWS * (q + 1), :] = g_ref[...]
            stage[ROW_POOL_SCALE:ROW_SINKS, :] = dsc_ref[...]
            stage[ROW_SINKS:ROW_LOSS, :] = dsink_ref[...]
            stage[ROW_LOSS:ROW_LOSS + 1, :] = loss_ref[...]
            stage[ROW_LOSS + 1:ROW_W_POOL, :] = jnp.zeros((ROW_W_POOL - ROW_LOSS - 1, 128), F32)
            stage[ROW_W_POOL:, :] = dwp_ref[...].reshape(4 * POOL_GROUP_DIM, POOL_GROUP_DIM)
            gat[4 * x + 2 * y + c] = stage[...]
            gat_rb[4 * x + 2 * y + c] = drb_ref[...]
            start()
            acc[...] = jnp.zeros_like(acc)

        acc[...] += _dot_tn(a_ref[...], b_ref[...])

        @pl.when(k == last)
        def _():
            blocks = acc[...].reshape(N_CHIP, 2, r, d)
            for chip in range(N_CHIP):
                for core in range(2):
                    out_ref[core, chip] = blocks[chip, core].astype(BF16)
            finish()
            total, total_rb = gat[0], gat_rb[0]
            for s in range(1, N_DEV):
                total, total_rb = total + gat[s], total_rb + gat_rb[s]
            total_ref[...] = total
            total_rb_ref[...] = total_rb

    out_shape = (2, N_CHIP, r, d)
    return pl.pallas_call(
        body, name="grad_w_in", grid=(t // tt,),
        out_shape=[jax.ShapeDtypeStruct(out_shape, BF16), jax.ShapeDtypeStruct((SMALL_ROWS, 128), F32),
                   jax.ShapeDtypeStruct(drel_bias.shape, F32)],
        in_specs=[pl.BlockSpec((tt, m), lambda k: (k, 0)), pl.BlockSpec((tt, d), lambda k: (k, 0))] + [VMEM] * 9,
        out_specs=[pl.BlockSpec(out_shape, lambda k: (0,) * len(out_shape)), VMEM, VMEM],
        scratch_shapes=[pltpu.VMEM((m, d), F32), pltpu.VMEM((SMALL_ROWS, 128), F32),
                        pltpu.VMEM((N_DEV, SMALL_ROWS, 128), F32), pltpu.VMEM((N_DEV, *drel_bias.shape), F32),
                        pltpu.SemaphoreType.DMA((14,)), pltpu.SemaphoreType.DMA((14,))],
        compiler_params=_params(),
    )(a, b, *gains, dpool_scale, dsinks, loss_part, dw_pool, drel_bias)


def reduce_w_in(d_in_t):
    def body(d_in_ref, g_in_ref, pair_got, chip_part, chip_got, p_send, p_recv, x_send, x_recv):
        x, y, c = lax.axis_index("x"), lax.axis_index("y"), lax.axis_index("c")
        my_chip = 2 * x + y
        pair = pltpu.make_async_remote_copy(
            src_ref=d_in_ref.at[1 - c], dst_ref=pair_got, send_sem=p_send, recv_sem=p_recv,
            device_id=(x, y, 1 - c), device_id_type=MESH)
        pair.start()
        pair.wait()
        chip_part[...] = (d_in_ref[c].astype(F32) + pair_got[...].astype(F32)).astype(BF16)
        copies = []
        for k in range(1, N_CHIP):
            px, py = x ^ (k >> 1), y ^ (k & 1)
            copies.append(pltpu.make_async_remote_copy(
                src_ref=chip_part.at[2 * px + py], dst_ref=chip_got.at[my_chip],
                send_sem=x_send.at[k - 1], recv_sem=x_recv.at[k - 1], device_id=(px, py, c), device_id_type=MESH))
        for cp in copies:
            cp.start()
        chip_got[my_chip] = chip_part[my_chip]
        for cp in copies:
            cp.wait()
        g_in = chip_got[0].astype(F32)
        for s in range(1, N_CHIP):
            g_in = g_in + chip_got[s].astype(F32)
        g_in_ref[...] = g_in

    per_core = d_in_t.shape[1:]
    return pl.pallas_call(
        body, name="reduce_w_in",
        out_shape=jax.ShapeDtypeStruct(d_in_t.shape[2:], F32),
        in_specs=[VMEM], out_specs=VMEM,
        scratch_shapes=[pltpu.VMEM(per_core, d_in_t.dtype), pltpu.VMEM(per_core, d_in_t.dtype),
                        pltpu.VMEM(per_core, d_in_t.dtype),
                        pltpu.SemaphoreType.DMA, pltpu.SemaphoreType.DMA,
                        pltpu.SemaphoreType.DMA((3,)), pltpu.SemaphoreType.DMA((3,))],
        compiler_params=_params(),
    )(d_in_t)


def small_adamw(total, total_rb, small_w, small_m, small_v):
    n_small = len(small_w)

    def body(*refs):
        total_ref, rb_ref = refs[:2]
        w_refs, m_refs, v_refs = (refs[2 + k * n_small:2 + (k + 1) * n_small] for k in range(3))
        loss_out = refs[2 + 3 * n_small]
        result = refs[3 + 3 * n_small:]
        total = total_ref[...]
        loss_out[...] = total[ROW_LOSS:ROW_LOSS + 1, :]
        grads = [_as_lanes(total[GAIN_ROWS * k:GAIN_ROWS * (k + 1), :]) for k in range(4)]
        grads.append(_as_lanes(total[ROW_POOL_SCALE:ROW_SINKS, :]))
        grads.append(total[ROW_SINKS:ROW_LOSS, 0:N_Q_HEADS])
        grads.append(total[ROW_W_POOL:, :].reshape(w_refs[6].shape))
        grads.append(rb_ref[...])
        for k in range(n_small):
            delta, m2, v2 = _adamw(w_refs[k][...], grads[k], m_refs[k][...], v_refs[k][...])
            result[4 * k][...] = grads[k]
            result[4 * k + 1][...] = delta
            result[4 * k + 2][...] = m2
            result[4 * k + 3][...] = v2

    out = pl.pallas_call(
        body, name="small_adamw",
        out_shape=[jax.ShapeDtypeStruct((1, 128), F32)] + [jax.ShapeDtypeStruct(w.shape, F32) for w in small_w for _ in range(4)],
        in_specs=[VMEM] * (2 + 3 * n_small), out_specs=[VMEM] * (1 + 4 * n_small),
        compiler_params=_params(),
    )(total, total_rb, *small_w, *small_m, *small_v)
    return out[0], [out[1 + 4 * k:5 + 4 * k] for k in range(n_small)]


def norm_inproj(x, g, w_shard, shard, shard_rows):
    t, d = x.shape
    r = w_shard.shape[0]
    tm = TOKEN_TILE
    nt = t // tm

    def body(x_ref, g_ref, w_shard_ref, shard_ref, proj_ref, h_ref, w_ref, gathered_ref, h_all, w_all, w_sem,
             send_w, recv_w, local_w, bounce_w, send_sems, recv_sems, local_sems, bounce):
        i = pl.program_id(0)
        start_w, finish_w = _gather_plan([w_shard_ref], [w_ref], send_w, recv_w, local_w, [bounce_w])
        start, finish = _gather_plan([shard_ref], [gathered_ref], send_sems, recv_sems, local_sems, [bounce],
                                     [(0, shard.shape[0])])

        @pl.when(i == 0)
        def _():
            start_w()
            start()

        @pl.when(i < nt)
        def _():
            xv = x_ref[...]
            h = ((xv * _rstd(xv)) * g_ref[...]).astype(BF16)
            h_ref[...] = h
            h_all[pl.ds(pl.multiple_of(i * tm, tm), tm), :] = h

        @pl.when(i == nt - 1)
        def _():
            finish_w()
            landed = pltpu.make_async_copy(w_ref, w_all, w_sem)
            landed.start()
            landed.wait()

        @pl.when(i >= nt)
        def _():
            rows = pl.ds(pl.multiple_of((i - nt) * tm, tm), tm)
            proj_ref[...] = _dot_nt(h_all[rows, :], _merge_rows(w_all[...]))

        pl.when(i == 2 * nt - 2)(finish.forward)
        pl.when(i == 2 * nt - 1)(finish.complete)

    first = lambda i: (jnp.minimum(i, nt - 1), 0)
    return pl.pallas_call(
        body, name="norm_inproj", grid=(2 * nt,),
        out_shape=[jax.ShapeDtypeStruct((t, N_DEV * r), F32), jax.ShapeDtypeStruct((t, d), BF16),
                   jax.ShapeDtypeStruct((N_DEV, r, d), w_shard.dtype),
                   jax.ShapeDtypeStruct((N_DEV, shard_rows, d), shard.dtype)],
        in_specs=[pl.BlockSpec((tm, d), first), pl.BlockSpec((1, d), lambda i: (0, 0)), ANY, ANY],
        out_specs=[pl.BlockSpec((tm, N_DEV * r), lambda i: (jnp.maximum(i - nt, 0), 0)), pl.BlockSpec((tm, d), first),
                   ANY, ANY],
        scratch_shapes=[pltpu.VMEM((t, d), BF16), pltpu.VMEM((N_DEV, r, d), w_shard.dtype), pltpu.SemaphoreType.DMA]
        + _gather_scratch([w_shard]) + _gather_scratch([shard]),
        compiler_params=_params(),
    )(x, g, w_shard, shard)


def bias_band(bucket, in_window, rel_bias):
    def body(bk_ref, win_ref, rb_ref, out_ref):
        bk = bk_ref[...]
        keep = win_ref[...] > 0.5
        for h in range(N_Q_HEADS):
            acc = jnp.zeros(bk.shape, F32)
            for b in range(N_BUCKETS):
                acc = jnp.where(bk == float(b), rb_ref[b, h], acc)
            out_ref[h] = jnp.where(keep, acc, NEG_INF)

    return pl.pallas_call(
        body, name="bias_band",
        out_shape=jax.ShapeDtypeStruct((N_Q_HEADS, BLOCK, 2 * BLOCK), F32),
        in_specs=[VMEM, VMEM, SMEM], out_specs=VMEM,
    )(bucket, in_window, rel_bias)


def bias_band_bwd(bucket, dbias):
    def body(bk_ref, db_ref, out_ref):
        bk = bk_ref[...]
        for h in range(N_Q_HEADS):
            db = db_ref[h]
            for b in range(N_BUCKETS):
                out_ref[h, b] = jnp.sum(jnp.where(bk == float(b), db, 0.0))

    return pl.pallas_call(
        body, name="bias_band_bwd",
        out_shape=jax.ShapeDtypeStruct((N_Q_HEADS, N_BUCKETS), F32),
        in_specs=[VMEM, VMEM], out_specs=SMEM,
    )(bucket, dbias)


def _window_sum(buf_ref, g, w, first):
    cols = slice(g * POOL_GROUP_DIM, (g + 1) * POOL_GROUP_DIM)
    acc = None
    for k in range(w):
        piece = buf_ref[first(k):first(k) + BLOCK, cols]
        acc = piece if acc is None else acc + piece
    return acc


def _inv_count(i, w):
    row = lax.broadcasted_iota(jnp.int32, (BLOCK, 1), 0)
    return 1.0 / jnp.minimum(i * BLOCK + row + 1, w).astype(F32)


def _fill_pool_input(i, ubuf, uc_ref, halo_ref):
    ubuf[0:HALO, :] = jnp.where(i > 0, halo_ref[...], 0.0)
    ubuf[HALO:, :] = uc_ref[...]


def _pooled(i, g, w, ubuf):
    cols = slice(g * POOL_GROUP_DIM, (g + 1) * POOL_GROUP_DIM)
    return _window_sum(ubuf, g, w, lambda k: HALO - k) * _inv_count(i, w) - ubuf[HALO:, cols]


def _head_variants(pair):
    low = lax.broadcasted_iota(jnp.int32, pair.shape, 1) < HEAD_DIM
    swapped = pltpu.roll(pair, HEAD_DIM, 1)
    zero = jnp.zeros_like(pair)
    pick = lambda c, a, b: jnp.where(c, a, b).astype(BF16)
    return [[pick(low, pair, zero), pick(low, zero, swapped)], [pick(low, swapped, zero), pick(low, zero, pair)]]


def _head_probs(i, hq, rows, s_ref, biasm_ref, sinks_ref):
    s = s_ref[hq, rows, :] * ATTN_SCALE + biasm_ref[hq, rows, :]
    col = lax.broadcasted_iota(jnp.int32, s.shape, 1)
    s = jnp.where((i == 0) & (col < BLOCK), NEG_INF, s)
    sink = sinks_ref[0, hq]
    m = jnp.maximum(jnp.max(s, axis=-1, keepdims=True), sink)
    p = jnp.exp(s - m)
    e_sink = jnp.exp(sink - m)
    inv = 1.0 / (jnp.sum(p, axis=-1, keepdims=True) + e_sink)
    return p * inv, e_sink * inv


def _head_slot(hq):
    return 4 * (hq // GQA_GROUP) + 2 * (hq % 2) + (hq % GQA_GROUP) // 2


def _mixer_in_specs(cur, prv):
    return [pl.BlockSpec((BLOCK, 512), lambda i: (cur(i), 0)),
            pl.BlockSpec((HALO, 512), lambda i: (jnp.maximum(cur(i) * (BLOCK // HALO) - 1, 0), 0)),
            pl.BlockSpec((BLOCK, 512), lambda i: (cur(i), 1)),
            pl.BlockSpec((BLOCK, 256), lambda i: (cur(i), 4)),
            pl.BlockSpec((BLOCK, 256), lambda i: (prv(i), 4))]


def _mixer_param_specs():
    return [pl.BlockSpec((N_Q_HEADS, BLOCK, 2 * BLOCK), lambda i: (0, 0, 0)), SMEM,
            pl.BlockSpec((4, POOL_GROUP_DIM, POOL_GROUP_DIM), lambda i: (0, 0, 0)),
            pl.BlockSpec((1, POOL_WIDTH), lambda i: (0, 0))]


def mixers_fwd(proj, biasm, sinks, w_pool, pool_scale, shards):
    t = proj.shape[0]
    nb = t // BLOCK
    ns = len(shards)

    def body(*refs):
        uc_ref, halo_ref, q_ref, kvc_ref, kvp_ref, biasm_ref, sinks_ref, wp_ref, sc_ref = refs[:9]
        shard_refs = refs[9:9 + ns]
        out_ref, pooled_ref, p_all, psink_ref = refs[9 + ns:13 + ns]
        gathered_refs = refs[13 + ns:13 + 2 * ns]
        ubuf, s_all, send_sems, recv_sems, local_sems = refs[13 + 2 * ns:18 + 2 * ns]
        i = pl.program_id(0)
        start, finish = _gather_plan(shard_refs, gathered_refs, send_sems, recv_sems, local_sems, refs[18 + 2 * ns:])
        pl.when(i == 0)(start)

        _fill_pool_input(i, ubuf, uc_ref, halo_ref)
        for g, w in enumerate(POOL_WINDOWS):
            cols = slice(g * POOL_GROUP_DIM, (g + 1) * POOL_GROUP_DIM)
            pooled = _pooled(i, g, w, ubuf).astype(BF16)
            pooled_ref[:, cols] = pooled
            out_ref[:, cols] = (_dot(pooled, wp_ref[g]) * sc_ref[:, cols]).astype(BF16)
        kv = jnp.concatenate([kvp_ref[...], kvc_ref[...]], axis=0)
        k_var = _head_variants(kv[:, 0:2 * HEAD_DIM])
        v_var = _head_variants(kv[:, 2 * HEAD_DIM:])
        for hq in range(N_Q_HEADS):
            j, half, h = hq // 2, hq % 2, hq // GQA_GROUP
            q2 = q_ref[:, 2 * HEAD_DIM * j:2 * HEAD_DIM * (j + 1)].astype(BF16)
            s_all[hq] = _dot_nt(q2, k_var[h][half])
        psink_ref[...] = jnp.zeros_like(psink_ref)
        for hq in range(N_Q_HEADS):
            for r in range(0, BLOCK, ROW_CHUNK):
                rows = slice(r, r + ROW_CHUNK)
                probs, p_sink = _head_probs(i, hq, rows, s_all, biasm_ref, sinks_ref)
                p_all[_head_slot(hq), rows, :] = probs.astype(BF16)
                psink_ref[rows, hq:hq + 1] = p_sink
        for j in range(N_Q_HEADS // 2):
            h = 2 * j // GQA_GROUP
            acc = _dot(p_all[_head_slot(2 * j)], v_var[h][0]) + _dot(p_all[_head_slot(2 * j + 1)], v_var[h][1])
            out_ref[:, POOL_WIDTH + 2 * HEAD_DIM * j:POOL_WIDTH + 2 * HEAD_DIM * (j + 1)] = acc.astype(BF16)

        pl.when(i == max(nb - 4, 0))(finish.forward)
        pl.when(i == nb - 1)(finish.complete)

    return pl.pallas_call(
        body, name="mixers_fwd", grid=(nb,),
        out_shape=[jax.ShapeDtypeStruct((t, 2 * POOL_WIDTH), BF16), jax.ShapeDtypeStruct((t, POOL_WIDTH), BF16),
                   jax.ShapeDtypeStruct((N_Q_HEADS, t, 2 * BLOCK), BF16), jax.ShapeDtypeStruct((t, 128), F32)]
        + [jax.ShapeDtypeStruct((N_DEV, *sh.shape), sh.dtype) for sh in shards],
        in_specs=_mixer_in_specs(lambda i: i, lambda i: jnp.maximum(i - 1, 0)) + _mixer_param_specs() + [ANY] * ns,
        out_specs=[pl.BlockSpec((BLOCK, 2 * POOL_WIDTH), lambda i: (i, 0)), pl.BlockSpec((BLOCK, POOL_WIDTH), lambda i: (i, 0)),
                   pl.BlockSpec((N_Q_HEADS, BLOCK, 2 * BLOCK), lambda i: (0, i, 0)), pl.BlockSpec((BLOCK, 128), lambda i: (i, 0))]
        + [ANY] * ns,
        scratch_shapes=[pltpu.VMEM((HALO + BLOCK, POOL_WIDTH), F32), pltpu.VMEM((N_Q_HEADS, BLOCK, 2 * BLOCK), F32)]
        + _gather_scratch(shards),
        compiler_params=_params(),
    )(proj, proj, proj, proj, proj, biasm, sinks, w_pool, pool_scale, *shards)


def outproj_norm(cat, w, x, g, g_next, shard, partial):
    t, d = x.shape
    tm = TOKEN_TILE
    last = t // tm - 1
    rows = [(partial.shape[1] - shard.shape[0], shard.shape[0])]

    def body(c_ref, w_ref, x_ref, g_ref, gn_ref, shard_ref, partial_ref, mix_ref, x1_ref, h2_ref, gathered_ref,
             send_sems, recv_sems, local_sems, bounce):
        i = pl.program_id(0)
        start, finish = _gather_plan([shard_ref], [gathered_ref], send_sems, recv_sems, local_sems, [bounce], rows)
        pl.when(i == 0)(start)
        mix = _dot(c_ref[...], w_ref[...])
        mix_ref[...] = mix
        x1 = x_ref[...] + (mix * _rstd(mix)) * g_ref[...]
        x1_ref[...] = x1
        h2_ref[...] = ((x1 * _rstd(x1)) * gn_ref[...]).astype(BF16)
        pl.when(i == max(last - 1, 0))(finish.forward)
        pl.when(i == last)(finish.complete)

    row = pl.BlockSpec((tm, d), lambda i: (i, 0))
    gain = pl.BlockSpec((1, d), lambda i: (0, 0))
    return pl.pallas_call(
        body, name="outproj_norm", grid=(t // tm,),
        out_shape=[jax.ShapeDtypeStruct((t, d), F32), jax.ShapeDtypeStruct((t, d), F32), jax.ShapeDtypeStruct((t, d), BF16),
                   jax.ShapeDtypeStruct(partial.shape, partial.dtype)],
        in_specs=[pl.BlockSpec((tm, cat.shape[1]), lambda i: (i, 0)), pl.BlockSpec(w.shape, lambda i: (0, 0)), row, gain, gain,
                  ANY, ANY],
        out_specs=[row, row, row, ANY],
        input_output_aliases={6: 3},
        scratch_shapes=_gather_scratch([shard]),
        compiler_params=_params(),
    )(cat, w, x, g, g_next, shard, partial)


def ffn_up(h, gate_t, up_t, down_shard):
    t, d = h.shape
    n = gate_t.shape[1]
    f = N_DEV * n
    tm, ts = FFN_TOKEN_TILE, FF_SHARDS_PER_TILE
    tn = ts * n
    steps = (f // tn, t // tm)

    def body(h_ref, wg_ref, wu_ref, shard_ref, gate_ref, up_ref, a_ref, gathered_ref,
             send_sems, recv_sems, local_sems, bounce):
        j, i = pl.program_id(0), pl.program_id(1)
        start, finish = _gather_plan([shard_ref], [gathered_ref], send_sems, recv_sems, local_sems, [bounce])
        pl.when((i == 0) & (j == 0))(start)

        hv = h_ref[...]
        gate = _dot_nt(hv, _merge_rows(wg_ref[...]))
        up = _dot_nt(hv, _merge_rows(wu_ref[...]))
        gate_ref[...] = gate.astype(BF16)
        up_ref[...] = up.astype(BF16)
        a_ref[...] = (gate * (1.0 / (1.0 + jnp.exp(-gate))) * up).astype(BF16)

        pl.when((j == steps[0] - 1) & (i == max(steps[1] - 2, 0)))(finish.forward)
        pl.when((j == steps[0] - 1) & (i == steps[1] - 1))(finish.complete)

    wide = pl.BlockSpec((tm, tn), lambda j, i: (i, j))
    return pl.pallas_call(
        body, name="ffn_up", grid=steps,
        out_shape=[jax.ShapeDtypeStruct((t, f), BF16)] * 3
        + [jax.ShapeDtypeStruct((N_DEV, *down_shard.shape), down_shard.dtype)],
        in_specs=[pl.BlockSpec((tm, d), lambda j, i: (i, 0)),
                  pl.BlockSpec((ts, n, d), lambda j, i: (j, 0, 0)),
                  pl.BlockSpec((ts, n, d), lambda j, i: (j, 0, 0)), ANY],
        out_specs=[wide, wide, wide, ANY],
        scratch_shapes=_gather_scratch([down_shard]),
        compiler_params=_params(),
    )(h, gate_t, up_t, down_shard)


def ffn_down_loss(a, w_down, x1, g, target):
    t, d = x1.shape
    tm = WIDE_K_TOKEN_TILE

    def body(a_ref, w_ref, x_ref, g_ref, t_ref, df_ref, dy_ref, dg_ref, loss_ref):
        @pl.when(pl.program_id(0) == 0)
        def _():
            dg_ref[...] = jnp.zeros_like(dg_ref)
            loss_ref[...] = jnp.zeros_like(loss_ref)

        f = _dot(a_ref[...], _merge_rows(w_ref[...]))
        r = _rstd(f)
        g = g_ref[...]
        err = x_ref[...] + (f * r) * g - t_ref[...]
        loss_ref[...] += 0.5 * jnp.sum(jnp.mean(err * err, axis=-1, keepdims=True))
        dy = err * (1.0 / d)
        dy_ref[...] = dy
        df, dg_rows = _norm_bwd(dy, f, r, g)
        df_ref[...] = df.astype(BF16)
        dg_ref[...] += _as_rows(jnp.sum(dg_rows, axis=0, keepdims=True))

    row = pl.BlockSpec((tm, d), lambda i: (i, 0))
    gain = pl.BlockSpec((1, d), lambda i: (0, 0))
    return pl.pallas_call(
        body, name="ffn_down_loss", grid=(t // tm,),
        out_shape=[jax.ShapeDtypeStruct((t, d), BF16), jax.ShapeDtypeStruct((t, d), F32),
                   jax.ShapeDtypeStruct((d // 128, 128), F32), jax.ShapeDtypeStruct((1, 128), F32)],
        in_specs=[pl.BlockSpec((tm, a.shape[1]), lambda i: (i, 0)), pl.BlockSpec(w_down.shape, lambda i: (0, 0, 0)), row, gain, row],
        out_specs=[row, row, pl.BlockSpec((d // 128, 128), lambda i: (0, 0)), pl.BlockSpec((1, 128), lambda i: (0, 0))],
        compiler_params=_params(),
    )(a, w_down, x1, g, target)


def ffn_down_bwd(df, w_down, gate, up):
    t, d = df.shape
    n = w_down.shape[1]
    f = gate.shape[1]
    tm, ts = FFN_TOKEN_TILE, FF_SHARDS_PER_TILE
    tn = ts * n

    def body(df_ref, w_ref, gate_ref, up_ref, dgate_ref, dup_ref):
        da = _dot_nt(df_ref[...], _merge_rows(w_ref[...]))
        gate = gate_ref[...].astype(F32)
        sig = 1.0 / (1.0 + jnp.exp(-gate))
        dgate_ref[...] = (da * up_ref[...].astype(F32) * (sig * (1.0 + gate * (1.0 - sig)))).astype(BF16)
        dup_ref[...] = (da * (gate * sig)).astype(BF16)

    wide = pl.BlockSpec((tm, tn), lambda j, i: (i, j))
    return pl.pallas_call(
        body, name="ffn_down_bwd", grid=(f // tn, t // tm),
        out_shape=[jax.ShapeDtypeStruct((t, f), BF16)] * 2,
        in_specs=[pl.BlockSpec((tm, d), lambda j, i: (i, 0)), pl.BlockSpec((ts, n, d), lambda j, i: (j, 0, 0)), wide, wide],
        out_specs=[wide, wide],
        compiler_params=_params(),
    )(df, w_down, gate, up)


def grad_rows(a, b, name, by_core=False):
    t, m = a.shape
    d = b.shape[1]
    r = m // N_DEV
    tt = TOKEN_TILE
    last = t // tt - 1
    out_shape = (2, N_CHIP, r, d) if by_core else (N_DEV, r, d)

    def body(a_ref, b_ref, out_ref, acc):
        k = pl.program_id(0)

        @pl.when(k == 0)
        def _():
            acc[...] = jnp.zeros_like(acc)

        acc[...] += _dot_tn(a_ref[...], b_ref[...])

        @pl.when(k == last)
        def _():
            if by_core:
                blocks = acc[...].reshape(N_CHIP, 2, r, d)
                for chip in range(N_CHIP):
                    for core in range(2):
                        out_ref[core, chip] = blocks[chip, core].astype(BF16)
            else:
                out_ref[...] = acc[...].reshape(out_shape).astype(BF16)

    return pl.pallas_call(
        body, name=name, grid=(t // tt,),
        out_shape=jax.ShapeDtypeStruct(out_shape, BF16),
        in_specs=[pl.BlockSpec((tt, m), lambda k: (k, 0)), pl.BlockSpec((tt, d), lambda k: (k, 0))],
        out_specs=pl.BlockSpec(out_shape, lambda k: (0,) * len(out_shape)),
        scratch_shapes=[pltpu.VMEM((m, d), F32)],
        compiler_params=_params(),
    )(a, b)


def grad_ffn(lhs, b, name, pair_parts=()):
    t, f = lhs[0].shape
    d = b.shape[1]
    nw = len(lhs)
    na = len(pair_parts)
    n = f // N_DEV
    tt, ts = TOKEN_TILE, FF_SHARDS_PER_TILE
    tn = ts * n
    steps = (f // tn, t // tt)

    def body(*refs):
        a_refs, b_ref, part_refs = refs[:nw], refs[nw], refs[nw + 1:nw + 1 + na]
        out_refs = refs[nw + 1 + na:2 * nw + 1 + na]
        got_refs = refs[2 * nw + 1 + na:2 * nw + 1 + 2 * na]
        acc = refs[2 * nw + 1 + 2 * na]
        i, k = pl.program_id(0), pl.program_id(1)
        if na:
            start, finish = _pair_plan(part_refs, got_refs, *refs[2 * nw + 2 + 2 * na:])
            pl.when((i == 0) & (k == 0))(start)

        @pl.when(k == 0)
        def _():
            acc[...] = jnp.zeros_like(acc)

        for w in range(nw):
            acc[w] += _dot_tn(a_refs[w][...], b_ref[...])

        @pl.when(k == steps[1] - 1)
        def _():
            for w in range(nw):
                blocks = acc[w].reshape(ts // 2, 2, n, d)
                for chip in range(ts // 2):
                    for core in range(2):
                        out_refs[w][core, chip] = blocks[chip, core].astype(BF16)

        if na:
            pl.when((i == steps[0] - 1) & (k == steps[1] - 1))(finish)

    out = pl.pallas_call(
        body, name=name, grid=steps,
        out_shape=[jax.ShapeDtypeStruct((2, N_CHIP, n, d), BF16)] * nw
        + [jax.ShapeDtypeStruct(p.shape[1:], p.dtype) for p in pair_parts],
        in_specs=[pl.BlockSpec((tt, tn), lambda i, k: (k, i))] * nw + [pl.BlockSpec((tt, d), lambda i, k: (k, 0))] + [ANY] * na,
        out_specs=[pl.BlockSpec((2, ts // 2, n, d), lambda i, k: (0, i, 0, 0))] * nw + [ANY] * na,
        scratch_shapes=[pltpu.VMEM((nw, tn, d), F32)]
        + ([pltpu.SemaphoreType.DMA((na,)), pltpu.SemaphoreType.DMA((na,))] if na else []),
        compiler_params=_params(),
    )(*lhs, b, *pair_parts)
    return out[:nw], out[nw:]


def ffn_up_bwd(dgate, dup, gate_t, up_t, x1, g_ffn, dy, mix, g_mix, chip_parts):
    t, d = x1.shape
    n = gate_t.shape[1]
    f = N_DEV * n
    tm = WIDE_K_TOKEN_TILE
    na = len(chip_parts)
    last = t // tm - 1

    def body(*refs):
        dg_ref, du_ref, wg_ref, wu_ref, x_ref, gf_ref, dy_ref, mix_ref, gm_ref = refs[:9]
        part_refs = refs[9:9 + na]
        dx1_ref, dmix_ref, dgf_ref, dgm_ref = refs[9 + na:13 + na]
        slot_refs = refs[13 + na:13 + 2 * na]
        send_sems, recv_sems, local_sems = refs[13 + 2 * na:16 + 2 * na]
        i = pl.program_id(0)
        start, finish = _chip_exchange_plan(part_refs, slot_refs, send_sems, recv_sems, local_sems, refs[16 + 2 * na:])

        @pl.when(i == 0)
        def _():
            start()
            dgf_ref[...] = jnp.zeros_like(dgf_ref)
            dgm_ref[...] = jnp.zeros_like(dgm_ref)

        dh = _dot(dg_ref[...], _merge_rows(wg_ref[...])) + _dot(du_ref[...], _merge_rows(wu_ref[...]))
        x1 = x_ref[...]
        dx, dgf_rows = _norm_bwd(dh, x1, _rstd(x1), gf_ref[...])
        dx1 = dy_ref[...] + dx
        dx1_ref[...] = dx1
        dgf_ref[...] += _as_rows(jnp.sum(dgf_rows, axis=0, keepdims=True))
        mix = mix_ref[...]
        dmix, dgm_rows = _norm_bwd(dx1, mix, _rstd(mix), gm_ref[...])
        dmix_ref[...] = dmix.astype(BF16)
        dgm_ref[...] += _as_rows(jnp.sum(dgm_rows, axis=0, keepdims=True))
        pl.when(i == last)(finish)

    row = pl.BlockSpec((tm, d), lambda i: (i, 0))
    wide = pl.BlockSpec((tm, f), lambda i: (i, 0))
    gain = pl.BlockSpec((1, d), lambda i: (0, 0))
    gain_rows = pl.BlockSpec((d // 128, 128), lambda i: (0, 0))
    whole = pl.BlockSpec((N_DEV, n, d), lambda i: (0, 0, 0), pipeline_mode=pl.Buffered(1))
    out = pl.pallas_call(
        body, name="ffn_up_bwd", grid=(t // tm,),
        out_shape=[jax.ShapeDtypeStruct((t, d), F32), jax.ShapeDtypeStruct((t, d), BF16),
                   jax.ShapeDtypeStruct((d // 128, 128), F32), jax.ShapeDtypeStruct((d // 128, 128), F32)]
        + [jax.ShapeDtypeStruct(p.shape, p.dtype) for p in chip_parts],
        in_specs=[wide, wide, whole, whole, row, gain, row, row, gain] + [ANY] * na,
        out_specs=[row, row, gain_rows, gain_rows] + [ANY] * na,
        scratch_shapes=_chip_exchange_scratch(chip_parts),
        compiler_params=_params(),
    )(dgate, dup, gate_t, up_t, x1, g_ffn, dy, mix, g_mix, *chip_parts)
    return out[:4], out[4:]


def outproj_bwd(dmix, w_out):
    t, d = dmix.shape
    tm = TOKEN_TILE

    def body(dm_ref, w_ref, out_ref):
        out_ref[...] = _dot_nt(dm_ref[...], w_ref[...])

    return pl.pallas_call(
        body, name="outproj_bwd", grid=(t // tm,),
        out_shape=jax.ShapeDtypeStruct((t, w_out.shape[0]), F32),
        in_specs=[pl.BlockSpec((tm, d), lambda i: (i, 0)), pl.BlockSpec(w_out.shape, lambda i: (0, 0))],
        out_specs=pl.BlockSpec((tm, w_out.shape[0]), lambda i: (i, 0)),
        compiler_params=_params(),
    )(dmix, w_out)


def mixers_bwd(proj, dcat, pooled, probs, p_sinks, w_pool, pool_scale, ffn_parts):
    t = proj.shape[0]
    nb = t // BLOCK
    na = len(ffn_parts)

    def body(*refs):
        (q_ref, kvc_ref, kvp_ref, dcat_ref, pooled_ref, p_all, psink_ref, wp_ref, sc_ref) = refs[:9]
        part_refs = refs[9:9 + na]
        dproj_ref, dbias_ref, dsink_ref, dwp_ref, dsc_ref = refs[9 + na:14 + na]
        slot_refs = refs[14 + na:14 + 2 * na]
        dbuf, c_u, c_q, c_kv, dp_all, ds_all, sink_acc = refs[14 + 2 * na:21 + 2 * na]
        send_sems, recv_sems, local_sems = refs[21 + 2 * na:24 + 2 * na]
        bounce = refs[24 + 2 * na:]
        i = pl.program_id(0)
        lane = lax.broadcasted_iota(jnp.int32, (1, 128), 1)
        start, finish = _chip_exchange_plan(part_refs, slot_refs, send_sems, recv_sems, local_sems, bounce)

        @pl.when(i == 0)
        def _():
            start()
            dbias_ref[...] = jnp.zeros_like(dbias_ref)
            dwp_ref[...] = jnp.zeros_like(dwp_ref)
            dsc_ref[...] = jnp.zeros_like(dsc_ref)
            dsink_ref[...] = jnp.zeros_like(dsink_ref)
            dbuf[...] = jnp.zeros_like(dbuf)
            c_u[...] = jnp.zeros_like(c_u)
            c_q[...] = jnp.zeros_like(c_q)
            c_kv[...] = jnp.zeros_like(c_kv)

        @pl.when(i < nb)
        def _():
            for g, w in enumerate(POOL_WINDOWS):
                cols = slice(g * POOL_GROUP_DIM, (g + 1) * POOL_GROUP_DIM)
                pooled = pooled_ref[:, cols]
                mixed = _dot(pooled, wp_ref[g])
                dout = dcat_ref[:, cols]
                dsc_ref[g:g + 1, :] += jnp.sum(dout * mixed, axis=0, keepdims=True)
                dmixed = (dout * sc_ref[:, cols]).astype(BF16)
                dwp_ref[g] += _dot_tn(pooled, dmixed)
                dpooled = _dot_nt(dmixed, wp_ref[g])
                scaled = dpooled * _inv_count(i, w)
                dbuf[BLOCK:, cols] = scaled[0:HALO]
                dproj_ref[:, cols] = (_window_sum(dbuf, g, w, lambda k: k) + c_u[:, cols]).astype(BF16)
                dbuf[0:BLOCK, cols] = scaled
                c_u[:, cols] = -dpooled

            kv = jnp.concatenate([kvp_ref[...], kvc_ref[...]], axis=0)
            k_var = _head_variants(kv[:, 0:2 * HEAD_DIM])
            v_var = _head_variants(kv[:, 2 * HEAD_DIM:])
            q2s = [q_ref[:, 2 * HEAD_DIM * j:2 * HEAD_DIM * (j + 1)].astype(BF16) for j in range(N_Q_HEADS // 2)]
            do2s = [dcat_ref[:, POOL_WIDTH + 2 * HEAD_DIM * j:POOL_WIDTH + 2 * HEAD_DIM * (j + 1)].astype(BF16)
                    for j in range(N_Q_HEADS // 2)]
            slot = _head_slot
            for hq in range(N_Q_HEADS):
                j, half, h = hq // 2, hq % 2, hq // GQA_GROUP
                dp_all[hq] = _dot_nt(do2s[j], v_var[h][half])
            sink_acc[...] = jnp.zeros_like(sink_acc)
            for hq in range(N_Q_HEADS):
                for r in range(0, BLOCK, ROW_CHUNK):
                    rows = slice(r, r + ROW_CHUNK)
                    probs = p_all[slot(hq), rows, :].astype(F32)
                    dp = dp_all[hq, rows, :]
                    delta = jnp.sum(probs * dp, axis=-1, keepdims=True)
                    ds = probs * (dp - delta)
                    dbias_ref[hq, rows, :] += ds
                    sink_acc[rows, :] += jnp.where(lane == hq, psink_ref[rows, :], 0.0) * delta
                    ds_all[slot(hq), rows, :] = (ds * ATTN_SCALE).astype(BF16)
            dsink_ref[...] -= jnp.sum(sink_acc[...], axis=0, keepdims=True)
            dq2 = [None] * (N_Q_HEADS // 2)
            for hq in range(N_Q_HEADS):
                j, half, h = hq // 2, hq % 2, hq // GQA_GROUP
                dq = _dot(ds_all[slot(hq)], k_var[h][half])
                dq2[j] = dq if dq2[j] is None else dq2[j] + dq
            low = lax.broadcasted_iota(jnp.int32, (2 * BLOCK, 2 * HEAD_DIM), 1) < HEAD_DIM
            dk_half, dv_half = [[None, None], [None, None]], [[None, None], [None, None]]
            for h in range(N_KV_HEADS):
                for half in range(2):
                    heads = [hq for hq in range(GQA_GROUP * h, GQA_GROUP * (h + 1)) if hq % 2 == half]
                    base = slot(heads[0])
                    q_rows = jnp.concatenate([q2s[hq // 2] for hq in heads], axis=0)
                    do_rows = jnp.concatenate([do2s[hq // 2] for hq in heads], axis=0)
                    dk_half[h][half] = _dot_tn(_merge_rows(ds_all[base:base + 2]), q_rows)
                    dv_half[h][half] = _dot_tn(_merge_rows(p_all[base:base + 2]), do_rows)

            def pair_of(halves):
                return jnp.where(low, halves[0][0] + pltpu.roll(halves[0][1], HEAD_DIM, 1),
                                 halves[1][1] + pltpu.roll(halves[1][0], HEAD_DIM, 1))

            dkv = jnp.concatenate([pair_of(dk_half), pair_of(dv_half)], axis=1)
            dproj_ref[:, POOL_WIDTH:2 * POOL_WIDTH] = c_q[...].astype(BF16)
            dproj_ref[:, 2 * POOL_WIDTH:] = (c_kv[...] + dkv[0:BLOCK]).astype(BF16)
            c_q[...] = jnp.concatenate(dq2, axis=1)
            c_kv[...] = dkv[BLOCK:]

        @pl.when(i == nb)
        def _():
            dbuf[BLOCK:, :] = jnp.zeros((HALO, POOL_WIDTH), F32)
            for g, w in enumerate(POOL_WINDOWS):
                cols = slice(g * POOL_GROUP_DIM, (g + 1) * POOL_GROUP_DIM)
                dproj_ref[:, cols] = (_window_sum(dbuf, g, w, lambda k: k) + c_u[:, cols]).astype(BF16)
            dproj_ref[:, POOL_WIDTH:2 * POOL_WIDTH] = c_q[...].astype(BF16)
            dproj_ref[:, 2 * POOL_WIDTH:] = c_kv[...].astype(BF16)
            finish()

    cur = lambda i: jnp.minimum(i, nb - 1)
    prv = lambda i: jnp.maximum(jnp.minimum(i, nb - 1) - 1, 0)
    out = pl.pallas_call(
        body, name="mixers_bwd", grid=(nb + 1,),
        out_shape=[jax.ShapeDtypeStruct((t, proj.shape[1]), BF16),
                   jax.ShapeDtypeStruct((N_Q_HEADS, BLOCK, 2 * BLOCK), F32),
                   jax.ShapeDtypeStruct((1, 128), F32),
                   jax.ShapeDtypeStruct((4, POOL_GROUP_DIM, POOL_GROUP_DIM), F32),
                   jax.ShapeDtypeStruct((len(POOL_WINDOWS), POOL_GROUP_DIM), F32)]
        + [jax.ShapeDtypeStruct(p.shape, p.dtype) for p in ffn_parts],
        in_specs=_mixer_in_specs(cur, prv)[2:]
        + [pl.BlockSpec((BLOCK, 2 * POOL_WIDTH), lambda i: (cur(i), 0)), pl.BlockSpec((BLOCK, POOL_WIDTH), lambda i: (cur(i), 0)),
           pl.BlockSpec((N_Q_HEADS, BLOCK, 2 * BLOCK), lambda i: (0, cur(i), 0)), pl.BlockSpec((BLOCK, 128), lambda i: (cur(i), 0))]
        + _mixer_param_specs()[2:] + [ANY] * na,
        out_specs=[pl.BlockSpec((BLOCK, proj.shape[1]), lambda i: (jnp.maximum(i - 1, 0), 0)),
                   pl.BlockSpec((N_Q_HEADS, BLOCK, 2 * BLOCK), lambda i: (0, 0, 0)),
                   pl.BlockSpec((1, 128), lambda i: (0, 0)),
                   pl.BlockSpec((4, POOL_GROUP_DIM, POOL_GROUP_DIM), lambda i: (0, 0, 0)),
                   pl.BlockSpec((len(POOL_WINDOWS), POOL_GROUP_DIM), lambda i: (0, 0))] + [ANY] * na,
        scratch_shapes=[pltpu.VMEM((BLOCK + HALO, POOL_WIDTH), F32),
                        pltpu.VMEM((BLOCK, POOL_WIDTH), F32), pltpu.VMEM((BLOCK, POOL_WIDTH), F32),
                        pltpu.VMEM((BLOCK, 256), F32),
                        pltpu.VMEM((N_Q_HEADS, BLOCK, 2 * BLOCK), F32), pltpu.VMEM((N_Q_HEADS, BLOCK, 2 * BLOCK), BF16),
                        pltpu.VMEM((BLOCK, 128), F32)]
        + _chip_exchange_scratch(ffn_parts),
        compiler_params=_params(),
    )(proj, proj, proj, dcat, pooled, probs, p_sinks, w_pool, pool_scale, *ffn_parts)
    return out[:5], out[5:]


def inproj_bwd(dproj, w_in_t, x, g, dx1):
    t, d = x.shape
    n = dproj.shape[1]
    tm = TOKEN_TILE

    def body(dp_ref, w_ref, x_ref, g_ref, dx1_ref, dx_ref, dg_ref):
        @pl.when(pl.program_id(0) == 0)
        def _():
            dg_ref[...] = jnp.zeros_like(dg_ref)

        dh = _dot(dp_ref[...], w_ref[...])
        xv = x_ref[...]
        dx, dg_rows = _norm_bwd(dh, xv, _rstd(xv), g_ref[...])
        dx_ref[...] = dx1_ref[...] + dx
        dg_ref[...] += _as_rows(jnp.sum(dg_rows, axis=0, keepdims=True))

    row = pl.BlockSpec((tm, d), lambda i: (i, 0))
    gain = pl.BlockSpec((1, d), lambda i: (0, 0))
    return pl.pallas_call(
        body, name="inproj_bwd", grid=(t // tm,),
        out_shape=[jax.ShapeDtypeStruct((t, d), F32), jax.ShapeDtypeStruct((d // 128, 128), F32)],
        in_specs=[pl.BlockSpec((tm, n), lambda i: (i, 0)), pl.BlockSpec(w_in_t.shape, lambda i: (0, 0)), row, gain, row],
        out_specs=[row, pl.BlockSpec((d // 128, 128), lambda i: (0, 0))],
        compiler_params=_params(),
    )(dproj, w_in_t, x, g, dx1)


def _bucket_band():
    qi = jnp.arange(BLOCK)[:, None]
    kj = jnp.arange(2 * BLOCK)[None, :]
    dist = qi + BLOCK - kj
    n = jnp.maximum(dist, 0)
    nf = jnp.maximum(n, 1).astype(F32)
    large = MAX_EXACT + (jnp.log(nf / MAX_EXACT) / np.float32(np.log(MAX_DISTANCE / MAX_EXACT))
                         * (N_BUCKETS - MAX_EXACT)).astype(jnp.int32)
    large = jnp.minimum(large, N_BUCKETS - 1)
    bucket = jnp.where(n < MAX_EXACT, n, large)
    in_window = (dist >= 0) & (dist < BLOCK)
    return bucket.astype(F32), in_window.astype(F32)


def kernel(x, g_pre_mix, w_in, w_pool, pool_scale, rel_bias, sinks, w_out, g_post_mix, g_pre_ffn, w_gate, w_up, w_down, g_post_ffn, loss_target, m_g_pre_mix, m_w_in, m_w_pool, m_pool_scale, m_rel_bias, m_sinks, m_w_out, m_g_post_mix, m_g_pre_ffn, m_w_gate, m_w_up, m_w_down, m_g_post_ffn, v_g_pre_mix, v_w_in, v_w_pool, v_pool_scale, v_rel_bias, v_sinks, v_w_out, v_g_post_mix, v_g_pre_ffn, v_w_gate, v_w_up, v_w_down, v_g_post_ffn):
    d = x.shape[-1]
    xs, target = x[0], loss_target[0]

    w_in_ts = w_in[0].T.astype(BF16)
    w_out_s = w_out[0].astype(BF16)
    gate_ts = w_gate[0].T.astype(BF16)
    up_ts = w_up[0].T.astype(BF16)
    w_down_s = w_down[0].astype(BF16)

    bucket, in_window = _bucket_band()
    biasm = bias_band(bucket, in_window, rel_bias)
    w_pool_b = w_pool[0].astype(BF16)
    half = up_ts.shape[0] // 2
    proj, h1, w_in_t, up_t = norm_inproj(xs, g_pre_mix, w_in_ts, up_ts[:half], up_ts.shape[0])
    w_in_t = w_in_t.reshape(-1, d)
    cat, pooled, probs, p_sinks, gate_t, w_out_f = mixers_fwd(proj, biasm, sinks, w_pool_b, pool_scale, [gate_ts, w_out_s])
    w_out_f = w_out_f.reshape(-1, d)
    mix, x1, h2, up_t = outproj_norm(cat, w_out_f, xs, g_post_mix, g_pre_ffn, up_ts[half:], up_t)
    gate, up, act, w_down_f = ffn_up(h2, gate_t, up_t, w_down_s)
    df, dy, dg_post_ffn, loss_part = ffn_down_loss(act, w_down_f, x1, g_post_ffn, target)

    def pair_sum(parts, tag):
        return pair_add(parts, pair_exchange(parts, "pair_exchange_" + tag), "pair_add_" + tag)

    dgate, dup = ffn_down_bwd(df, w_down_f, gate, up)
    (d_gate, d_up), _ = grad_ffn([dgate, dup], h2, "grad_w_gate_up")
    (d_down,), got_gate_up = grad_ffn([act], df, "grad_w_down", [d_gate, d_up])
    q_gate, q_up, q_down = pair_add(
        [d_gate, d_up, d_down], [*got_gate_up, *pair_exchange([d_down], "pair_exchange_down")], "pair_add_ffn")
    (dx1, dmix, dg_pre_ffn, dg_post_mix), (gate_slots, down_slots) = ffn_up_bwd(
        dgate, dup, gate_t, up_t, x1, g_pre_ffn, dy, mix, g_post_mix, [q_gate, q_down])
    dcat = outproj_bwd(dmix, w_out_f)
    d_out = grad_rows(cat, dmix, "grad_w_out", by_core=True)
    q_out, = pair_sum([d_out], "out")
    (dproj, dbias, dsinks, dw_pool, dpool_scale), (up_slots, out_slots) = mixers_bwd(
        proj, dcat, pooled, probs, p_sinks, w_pool_b, pool_scale, [q_up, q_out])
    drel_bias = bias_band_bwd(bucket, dbias)
    grad_x, dg_pre_mix = inproj_bwd(dproj, w_in_t, xs, g_pre_mix, dx1)

    small_w = [g_pre_mix, g_post_mix, g_pre_ffn, g_post_ffn, pool_scale, sinks, w_pool, rel_bias.T]
    small_m = [m_g_pre_mix, m_g_post_mix, m_g_pre_ffn, m_g_post_ffn, m_pool_scale, m_sinks, m_w_pool, m_rel_bias.T]
    small_v = [v_g_pre_mix, v_g_post_mix, v_g_pre_ffn, v_g_post_ffn, v_pool_scale, v_sinks, v_w_pool, v_rel_bias.T]
    d_in_t, total, total_rb = grad_w_in_small_reduce(
        dproj, h1, [dg_pre_mix, dg_post_mix, dg_pre_ffn, dg_post_ffn], dpool_scale, dsinks, loss_part, dw_pool, drel_bias)
    g_in_t = reduce_w_in(d_in_t)
    loss_row, sm = small_adamw(total, total_rb, small_w, small_m, small_v)
    sm[7] = [r.T for r in sm[7]]
    big_w = [w_in[0].T, w_out[0], w_gate[0].T, w_up[0].T, w_down[0]]
    big_m = [m_w_in[0].T, m_w_out[0], m_w_gate[0].T, m_w_up[0].T, m_w_down[0]]
    big_v = [v_w_in[0].T, v_w_out[0], v_w_gate[0].T, v_w_up[0].T, v_w_down[0]]
    upd = sum_adamw([out_slots, gate_slots, up_slots, down_slots], big_w[1:], big_m[1:], big_v[1:], "sum_adamw")
    upd = [[g_in_t, *adamw_update(big_w[:1], [g_in_t], big_m[:1], big_v[:1], "adamw_in")[0]], *upd]
    back = lambda k, a: (a.T if k in (0, 2, 3) else a)[None]
    big = [[back(k, u) for u in upd[k]] for k in range(5)]

    def ordered(kind):
        s, b = [p[kind] for p in sm], [p[kind] for p in big]
        return [s[0], b[0], s[6], s[4], s[7], s[5], b[1], s[1], s[2], b[2], b[3], b[4], s[3]]

    return (loss_row[0, 0], grad_x[None], *ordered(0), *ordered(1), *ordered(2), *ordered(3))
```

```python
import numpy as np
import jax
import jax.numpy as jnp
from jax import lax
from jax.experimental import pallas as pl
from jax.experimental.pallas import tpu as pltpu

F32 = jnp.float32
BF16 = jnp.bfloat16

N_DEV = 8
N_CHIP = 4
POOL_WIDTH = 512
POOL_WINDOWS = (2, 4, 8, 16)
POOL_GROUP_DIM = 128
HEAD_DIM = 64
N_Q_HEADS = 8
N_KV_HEADS = 2
GQA_GROUP = 4
BLOCK = 128
HALO = 16
ROW_CHUNK = 32
N_BUCKETS = 32
MAX_EXACT = 16
MAX_DISTANCE = 128
EPS = 1e-6
NEG_INF = -1e30
ATTN_SCALE = float(1.0 / np.sqrt(np.float32(HEAD_DIM)))

ADAM_LR = 0.001
ADAM_B1 = 0.9
ADAM_B2 = 0.999
ADAM_EPS = 1e-08
ADAM_WD = 0.01
ADAM_STEP = 10

TOKEN_TILE = 1024
WIDE_K_TOKEN_TILE = 512
FFN_TOKEN_TILE = 1024
FF_SHARDS_PER_TILE = 4
VMEM_LIMIT = 56 * 1024 * 1024
MESH = pl.DeviceIdType.MESH
PAIR_COLLECTIVE_ID = 0
ANY = pl.BlockSpec(memory_space=pl.ANY)
VMEM = pl.BlockSpec(memory_space=pltpu.VMEM)
SMEM = pl.BlockSpec(memory_space=pltpu.SMEM)


def _params(**kw):
    return pltpu.CompilerParams(vmem_limit_bytes=VMEM_LIMIT, **kw)


def _dot(a, b):
    return jnp.dot(a, b, preferred_element_type=F32)


def _dot_nt(a, b):
    return lax.dot_general(a, b, (((1,), (1,)), ((), ())), preferred_element_type=F32)


def _dot_tn(a, b):
    return lax.dot_general(a, b, (((0,), (0,)), ((), ())), preferred_element_type=F32)


def _rstd(v):
    return lax.rsqrt(jnp.mean(v * v, axis=-1, keepdims=True) + EPS)


def _norm_bwd(dout, v, r, g):
    vn = v * r
    dn = dout * g
    dv = r * (dn - vn * jnp.mean(dn * vn, axis=-1, keepdims=True))
    return dv, dout * vn


def _as_rows(v):
    return jnp.concatenate([v[:, k:k + 128] for k in range(0, v.shape[1], 128)], axis=0)


def _as_lanes(rows):
    return jnp.concatenate([rows[k:k + 1, :] for k in range(rows.shape[0])], axis=1)


def _merge_rows(value):
    s, r, c_ = value.shape
    return value.reshape(s * r, c_)


def _gather_plan(srcs, outs, send_sems, recv_sems, local_sems=None, bounce=None, rows=None):
    n = len(srcs)
    x, y, c = lax.axis_index("x"), lax.axis_index("y"), lax.axis_index("c")
    me, sibling = (x, y, c), (x, y, 1 - c)
    chips = [(1 - x, y), (x, 1 - y), (1 - x, 1 - y)]

    def slot(a, px, py, pc):
        whole = outs[a].at[4 * px + 2 * py + pc]
        return whole if rows is None or rows[a] is None else whole.at[pl.ds(*rows[a])]

    def copy(a, k, block, to, from_src=False):
        return pltpu.make_async_remote_copy(
            src_ref=srcs[a] if from_src else slot(a, *block), dst_ref=slot(a, *block),
            send_sem=send_sems.at[k * n + a], recv_sem=recv_sems.at[k * n + a], device_id=to, device_id_type=MESH)

    def own_in(a):
        return pltpu.make_async_copy(srcs[a], bounce[a], local_sems.at[a])

    def own_out(a):
        return pltpu.make_async_copy(bounce[a], slot(a, *me), local_sems.at[a])

    def first(a):
        return [copy(a, 0, me, sibling, True)] + [copy(a, 1 + j, me, (*chip, c), True) for j, chip in enumerate(chips)]

    def passed(a, j):
        return copy(a, 4 + j, (*chips[j], c), sibling)

    def start():
        for a in range(n):
            if bounce is not None:
                own_in(a).start()
            for cp in first(a):
                cp.start()

    def forward():
        if bounce is not None:
            for a in range(n):
                own_in(a).wait()
                own_out(a).start()
        for j, chip in enumerate(chips):
            for a in range(n):
                copy(a, 1 + j, (*chip, c), me).wait_recv()
                passed(a, j).start()

    def complete():
        for a in range(n):
            copy(a, 0, sibling, me).wait_recv()
            for j, chip in enumerate(chips):
                copy(a, 4 + j, (*chip, 1 - c), me).wait_recv()
        for a in range(n):
            for cp in first(a) + [passed(a, j) for j in range(3)]:
                cp.wait_send()
            if bounce is not None:
                own_out(a).wait()

    def finish():
        forward()
        complete()

    finish.forward, finish.complete = forward, complete
    return start, finish


def _gather_scratch(shards):
    n = len(shards)
    return [pltpu.SemaphoreType.DMA((7 * n,)), pltpu.SemaphoreType.DMA((7 * n,)), pltpu.SemaphoreType.DMA((n,))] \
        + [pltpu.VMEM(s.shape, s.dtype) for s in shards]


def _chip_exchange_plan(srcs, outs, send_sems, recv_sems, local_sems, bounce):
    n = len(srcs)
    x, y, c = lax.axis_index("x"), lax.axis_index("y"), lax.axis_index("c")
    my_chip = 2 * x + y

    def copies():
        out = []
        for a in range(n):
            for k in range(1, N_CHIP):
                px, py = x ^ (k >> 1), y ^ (k & 1)
                out.append(pltpu.make_async_remote_copy(
                    src_ref=srcs[a].at[2 * px + py], dst_ref=outs[a].at[my_chip],
                    send_sem=send_sems.at[(k - 1) * n + a], recv_sem=recv_sems.at[(k - 1) * n + a],
                    device_id=(px, py, c), device_id_type=MESH))
        return out

    def own_in(a):
        return pltpu.make_async_copy(srcs[a].at[my_chip], bounce[a], local_sems.at[a])

    def own_out(a):
        return pltpu.make_async_copy(bounce[a], outs[a].at[my_chip], local_sems.at[a])

    def start():
        for a in range(n):
            own_in(a).start()
        for cp in copies():
            cp.start()

    def finish():
        for a in range(n):
            own_in(a).wait()
            own_out(a).start()
        for cp in copies():
            cp.wait()
        for a in range(n):
            own_out(a).wait()

    return start, finish


def _chip_exchange_scratch(parts):
    n = len(parts)
    return [pltpu.SemaphoreType.DMA((3 * n,)), pltpu.SemaphoreType.DMA((3 * n,)), pltpu.SemaphoreType.DMA((n,))] \
        + [pltpu.VMEM(p.shape[1:], p.dtype) for p in parts]


def _pair_plan(srcs, outs, send_sems, recv_sems):
    x, y, c = lax.axis_index("x"), lax.axis_index("y"), lax.axis_index("c")

    def copies():
        return [pltpu.make_async_remote_copy(
            src_ref=srcs[a].at[1 - c], dst_ref=outs[a], send_sem=send_sems.at[a], recv_sem=recv_sems.at[a],
            device_id=(x, y, 1 - c), device_id_type=MESH) for a in range(len(srcs))]

    def start():
        for cp in copies():
            cp.start()

    def finish():
        for cp in copies():
            cp.wait()

    return start, finish


def pair_exchange(parts, name):
    n = len(parts)

    def body(*refs):
        x, y, c = lax.axis_index("x"), lax.axis_index("y"), lax.axis_index("c")
        barrier = pltpu.get_barrier_semaphore()
        pl.semaphore_signal(barrier, inc=1, device_id=(x, y, 1 - c), device_id_type=MESH)
        pl.semaphore_wait(barrier, 1)
        start, finish = _pair_plan(refs[:n], refs[n:2 * n], *refs[2 * n:])
        start()
        finish()

    return pl.pallas_call(
        body, name=name, out_shape=[jax.ShapeDtypeStruct(p.shape[1:], p.dtype) for p in parts],
        in_specs=[ANY] * n, out_specs=[ANY] * n,
        scratch_shapes=[pltpu.SemaphoreType.DMA((n,)), pltpu.SemaphoreType.DMA((n,))],
        compiler_params=pltpu.CompilerParams(collective_id=PAIR_COLLECTIVE_ID),
    )(*parts)


def pair_add(parts, got, name):
    n = len(parts)

    def body(core_ref, *refs):
        for a in range(n):
            refs[2 * n + a][...] = (refs[a][...].astype(F32) + refs[n + a][...].astype(F32)).astype(BF16)

    def own(p):
        zeros = (0,) * (p.ndim - 2)
        return pl.BlockSpec((None, 1, *p.shape[2:]), lambda i, core: (core[0], i, *zeros))

    def plain(p):
        zeros = (0,) * (p.ndim - 1)
        return pl.BlockSpec((1, *p.shape[1:]), lambda i, core: (i, *zeros))

    core = lax.axis_index("c").astype(jnp.int32).reshape(1)
    return pl.pallas_call(
        body, name=name,
        grid_spec=pltpu.PrefetchScalarGridSpec(
            num_scalar_prefetch=1, grid=(got[0].shape[0],),
            in_specs=[own(p) for p in parts] + [plain(p) for p in got], out_specs=[plain(p) for p in got]),
        out_shape=[jax.ShapeDtypeStruct(p.shape, BF16) for p in got],
        compiler_params=_params(),
    )(core, *parts, *got)


def _adamw(w, g, m, v):
    m2 = ADAM_B1 * m + (1.0 - ADAM_B1) * g
    v2 = ADAM_B2 * v + (1.0 - ADAM_B2) * (g * g)
    m_hat = m2 / (1.0 - ADAM_B1 ** ADAM_STEP)
    v_hat = v2 / (1.0 - ADAM_B2 ** ADAM_STEP)
    delta = -ADAM_LR * (m_hat / (jnp.sqrt(v_hat) + ADAM_EPS) + ADAM_WD * w)
    return delta, m2, v2


def sum_adamw(slots, ws, ms, vs, name):
    n = len(ws)
    halves = 2

    def body(*refs):
        for a in range(n):
            total = refs[a][0].astype(F32)
            for s in range(1, slots[a].shape[0]):
                total = total + refs[a][s].astype(F32)
            delta, m2, v2 = _adamw(refs[n + a][...], total, refs[2 * n + a][...], refs[3 * n + a][...])
            for q, val in enumerate((total, delta, m2, v2)):
                refs[4 * n + 4 * a + q][...] = val

    def rows(w):
        return pl.BlockSpec((w.shape[0] // halves, w.shape[1]), lambda i: (i, 0))

    def slot_rows(p):
        return pl.BlockSpec((p.shape[0], p.shape[1] // halves, p.shape[2]), lambda i: (0, i, 0))

    out = pl.pallas_call(
        body, name=name, grid=(halves,),
        out_shape=[jax.ShapeDtypeStruct(w.shape, F32) for w in ws for _ in range(4)],
        in_specs=[slot_rows(p) for p in slots] + [rows(w) for w in ws] * 3,
        out_specs=[rows(w) for w in ws for _ in range(4)],
        compiler_params=_params(),
    )(*slots, *ws, *ms, *vs)
    return [out[4 * a:4 * a + 4] for a in range(n)]


def adamw_update(ws, gs, ms, vs, name):
    n = len(ws)

    def body(*refs):
        for a in range(n):
            delta, m2, v2 = _adamw(refs[a][...], refs[n + a][...], refs[2 * n + a][...], refs[3 * n + a][...])
            refs[4 * n + 3 * a][...] = delta
            refs[4 * n + 3 * a + 1][...] = m2
            refs[4 * n + 3 * a + 2][...] = v2

    out = pl.pallas_call(
        body, name=name,
        out_shape=[jax.ShapeDtypeStruct(w.shape, F32) for w in ws for _ in range(3)],
        in_specs=[VMEM] * (4 * n), out_specs=[VMEM] * (3 * n),
        compiler_params=_params(),
    )(*ws, *gs, *ms, *vs)
    return [out[3 * a:3 * a + 3] for a in range(n)]


GAIN_ROWS = 8
ROW_POOL_SCALE = 4 * GAIN_ROWS
ROW_SINKS = ROW_POOL_SCALE + 4
ROW_LOSS = ROW_SINKS + 1
ROW_W_POOL = 40
SMALL_ROWS = ROW_W_POOL + 4 * POOL_GROUP_DIM


def grad_w_in_small_reduce(a, b, gains, dpool_scale, dsinks, loss_part, dw_pool, drel_bias):
    t, m = a.shape
    d = b.shape[1]
    r = m // N_DEV
    tt = TOKEN_TILE
    last = t // tt - 1

    def body(a_ref, b_ref, g0, g1, g2, g3, dsc_ref, dsink_ref, loss_ref, dwp_ref, drb_ref, out_ref, total_ref, total_rb_ref,
             acc, stage, gat, gat_rb, g_send, g_recv):
        k = pl.program_id(0)
        x, y, c = lax.axis_index("x"), lax.axis_index("y"), lax.axis_index("c")
        start, finish = _gather_plan([stage, drb_ref], [gat, gat_rb], g_send, g_recv)

        @pl.when(k == 0)
        def _():
            for q, g_ref in enumerate((g0, g1, g2, g3)):
                stage[GAIN_ROWS * q:GAIN_ROWS * (q + 1), :] = g_ref[...]
            stage[ROW_POOL_SCALE:ROW_SINKS, :] = dsc_ref[...]
            stage[ROW_SINKS:ROW_LOSS, :] = dsink_ref[...]
            stage[ROW_LOSS:ROW_LOSS + 1, :] = loss_ref[...]
            stage[ROW_LOSS + 1:ROW_W_POOL, :] = jnp.zeros((ROW_W_POOL - ROW_LOSS - 1, 128), F32)
            stage[ROW_W_POOL:, :] = dwp_ref[...].reshape(4 * POOL_GROUP_DIM, POOL_GROUP_DIM)
            gat[4 * x + 2 * y + c] = stage[...]
            gat_rb[4 * x + 2 * y + c] = drb_ref[...]
            start()
            acc[...] = jnp.zeros_like(acc)

        acc[...] += _dot_tn(a_ref[...], b_ref[...])

        @pl.when(k == last)
        def _():
            blocks = acc[...].reshape(N_CHIP, 2, r, d)
            for chip in range(N_CHIP):
                for core in range(2):
                    out_ref[core, chip] = blocks[chip, core].astype(BF16)
            finish()
            total, total_rb = gat[0], gat_rb[0]
            for s in range(1, N_DEV):
                total, total_rb = total + gat[s], total_rb + gat_rb[s]
            total_ref[...] = total
            total_rb_ref[...] = total_rb

    out_shape = (2, N_CHIP, r, d)
    return pl.pallas_call(
        body, name="grad_w_in", grid=(t // tt,),
        out_shape=[jax.ShapeDtypeStruct(out_shape, BF16), jax.ShapeDtypeStruct((SMALL_ROWS, 128), F32),
                   jax.ShapeDtypeStruct(drel_bias.shape, F32)],
        in_specs=[pl.BlockSpec((tt, m), lambda k: (k, 0)), pl.BlockSpec((tt, d), lambda k: (k, 0))] + [VMEM] * 9,
        out_specs=[pl.BlockSpec(out_shape, lambda k: (0,) * len(out_shape)), VMEM, VMEM],
        scratch_shapes=[pltpu.VMEM((m, d), F32), pltpu.VMEM((SMALL_ROWS, 128), F32),
                        pltpu.VMEM((N_DEV, SMALL_ROWS, 128), F32), pltpu.VMEM((N_DEV, *drel_bias.shape), F32),
                        pltpu.SemaphoreType.DMA((14,)), pltpu.SemaphoreType.DMA((14,))],
        compiler_params=_params(),
    )(a, b, *gains, dpool_scale, dsinks, loss_part, dw_pool, drel_bias)


def reduce_w_in(d_in_t):
    def body(d_in_ref, g_in_ref, pair_got, chip_part, chip_got, p_send, p_recv, x_send, x_recv):
        x, y, c = lax.axis_index("x"), lax.axis_index("y"), lax.axis_index("c")
        my_chip = 2 * x + y
        pair = pltpu.make_async_remote_copy(
            src_ref=d_in_ref.at[1 - c], dst_ref=pair_got, send_sem=p_send, recv_sem=p_recv,
            device_id=(x, y, 1 - c), device_id_type=MESH)
        pair.start()
        pair.wait()
        chip_part[...] = (d_in_ref[c].astype(F32) + pair_got[...].astype(F32)).astype(BF16)
        copies = []
        for k in range(1, N_CHIP):
            px, py = x ^ (k >> 1), y ^ (k & 1)
            copies.append(pltpu.make_async_remote_copy(
                src_ref=chip_part.at[2 * px + py], dst_ref=chip_got.at[my_chip],
                send_sem=x_send.at[k - 1], recv_sem=x_recv.at[k - 1], device_id=(px, py, c), device_id_type=MESH))
        for cp in copies:
            cp.start()
        chip_got[my_chip] = chip_part[my_chip]
        for cp in copies:
            cp.wait()
        g_in = chip_got[0].astype(F32)
        for s in range(1, N_CHIP):
            g_in = g_in + chip_got[s].astype(F32)
        g_in_ref[...] = g_in

    per_core = d_in_t.shape[1:]
    return pl.pallas_call(
        body, name="reduce_w_in",
        out_shape=jax.ShapeDtypeStruct(d_in_t.shape[2:], F32),
        in_specs=[VMEM], out_specs=VMEM,
        scratch_shapes=[pltpu.VMEM(per_core, d_in_t.dtype), pltpu.VMEM(per_core, d_in_t.dtype),
                        pltpu.VMEM(per_core, d_in_t.dtype),
                        pltpu.SemaphoreType.DMA, pltpu.SemaphoreType.DMA,
                        pltpu.SemaphoreType.DMA((3,)), pltpu.SemaphoreType.DMA((3,))],
        compiler_params=_params(),
    )(d_in_t)


def small_adamw(total, total_rb, small_w, small_m, small_v):
    n_small = len(small_w)

    def body(*refs):
        total_ref, rb_ref = refs[:2]
        w_refs, m_refs, v_refs = (refs[2 + k * n_small:2 + (k + 1) * n_small] for k in range(3))
        loss_out = refs[2 + 3 * n_small]
        result = refs[3 + 3 * n_small:]
        total = total_ref[...]
        loss_out[...] = total[ROW_LOSS:ROW_LOSS + 1, :]
        grads = [_as_lanes(total[GAIN_ROWS * k:GAIN_ROWS * (k + 1), :]) for k in range(4)]
        grads.append(_as_lanes(total[ROW_POOL_SCALE:ROW_SINKS, :]))
        grads.append(total[ROW_SINKS:ROW_LOSS, 0:N_Q_HEADS])
        grads.append(total[ROW_W_POOL:, :].reshape(w_refs[6].shape))
        grads.append(rb_ref[...])
        for k in range(n_small):
            delta, m2, v2 = _adamw(w_refs[k][...], grads[k], m_refs[k][...], v_refs[k][...])
            result[4 * k][...] = grads[k]
            result[4 * k + 1][...] = delta
            result[4 * k + 2][...] = m2
            result[4 * k + 3][...] = v2

    out = pl.pallas_call(
        body, name="small_adamw",
        out_shape=[jax.ShapeDtypeStruct((1, 128), F32)] + [jax.ShapeDtypeStruct(w.shape, F32) for w in small_w for _ in range(4)],
        in_specs=[VMEM] * (2 + 3 * n_small), out_specs=[VMEM] * (1 + 4 * n_small),
        compiler_params=_params(),
    )(total, total_rb, *small_w, *small_m, *small_v)
    return out[0], [out[1 + 4 * k:5 + 4 * k] for k in range(n_small)]


def norm_inproj(x, g, w_shard, shard, shard_rows):
    t, d = x.shape
    r = w_shard.shape[0]
    tm = TOKEN_TILE
    nt = t // tm

    def body(x_ref, g_ref, w_shard_ref, shard_ref, proj_ref, h_ref, w_ref, gathered_ref, h_all, w_all, w_sem,
             send_w, recv_w, local_w, bounce_w, send_sems, recv_sems, local_sems, bounce):
        i = pl.program_id(0)
        start_w, finish_w = _gather_plan([w_shard_ref], [w_ref], send_w, recv_w, local_w, [bounce_w])
        start, finish = _gather_plan([shard_ref], [gathered_ref], send_sems, recv_sems, local_sems, [bounce],
                                     [(0, shard.shape[0])])

        @pl.when(i == 0)
        def _():
            start_w()
            start()

        @pl.when(i < nt)
        def _():
            xv = x_ref[...]
            h = ((xv * _rstd(xv)) * g_ref[...]).astype(BF16)
            h_ref[...] = h
            h_all[pl.ds(pl.multiple_of(i * tm, tm), tm), :] = h

        @pl.when(i == nt - 1)
        def _():
            finish_w()
            landed = pltpu.make_async_copy(w_ref, w_all, w_sem)
            landed.start()
            landed.wait()

        @pl.when(i >= nt)
        def _():
            rows = pl.ds(pl.multiple_of((i - nt) * tm, tm), tm)
            proj_ref[...] = _dot_nt(h_all[rows, :], _merge_rows(w_all[...]))

        pl.when(i == 2 * nt - 2)(finish.forward)
        pl.when(i == 2 * nt - 1)(finish.complete)

    first = lambda i: (jnp.minimum(i, nt - 1), 0)
    return pl.pallas_call(
        body, name="norm_inproj", grid=(2 * nt,),
        out_shape=[jax.ShapeDtypeStruct((t, N_DEV * r), F32), jax.ShapeDtypeStruct((t, d), BF16),
                   jax.ShapeDtypeStruct((N_DEV, r, d), w_shard.dtype),
                   jax.ShapeDtypeStruct((N_DEV, shard_rows, d), shard.dtype)],
        in_specs=[pl.BlockSpec((tm, d), first), pl.BlockSpec((1, d), lambda i: (0, 0)), ANY, ANY],
        out_specs=[pl.BlockSpec((tm, N_DEV * r), lambda i: (jnp.maximum(i - nt, 0), 0)), pl.BlockSpec((tm, d), first),
                   ANY, ANY],
        scratch_shapes=[pltpu.VMEM((t, d), BF16), pltpu.VMEM((N_DEV, r, d), w_shard.dtype), pltpu.SemaphoreType.DMA]
        + _gather_scratch([w_shard]) + _gather_scratch([shard]),
        compiler_params=_params(),
    )(x, g, w_shard, shard)


def bias_band(bucket, in_window, rel_bias):
    def body(bk_ref, win_ref, rb_ref, out_ref):
        bk = bk_ref[...]
        keep = win_ref[...] > 0.5
        for h in range(N_Q_HEADS):
            acc = jnp.zeros(bk.shape, F32)
            for b in range(N_BUCKETS):
                acc = jnp.where(bk == float(b), rb_ref[b, h], acc)
            out_ref[h] = jnp.where(keep, acc, NEG_INF)

    return pl.pallas_call(
        body, name="bias_band",
        out_shape=jax.ShapeDtypeStruct((N_Q_HEADS, BLOCK, 2 * BLOCK), F32),
        in_specs=[VMEM, VMEM, SMEM], out_specs=VMEM,
    )(bucket, in_window, rel_bias)


def bias_band_bwd(bucket, dbias):
    def body(bk_ref, db_ref, out_ref):
        bk = bk_ref[...]
        for h in range(N_Q_HEADS):
            db = db_ref[h]
            for b in range(N_BUCKETS):
                out_ref[h, b] = jnp.sum(jnp.where(bk == float(b), db, 0.0))

    return pl.pallas_call(
        body, name="bias_band_bwd",
        out_shape=jax.ShapeDtypeStruct((N_Q_HEADS, N_BUCKETS), F32),
        in_specs=[VMEM, VMEM], out_specs=SMEM,
    )(bucket, dbias)


def _window_sum(buf_ref, g, w, first):
    cols = slice(g * POOL_GROUP_DIM, (g + 1) * POOL_GROUP_DIM)
    acc = None
    for k in range(w):
        piece = buf_ref[first(k):first(k) + BLOCK, cols]
        acc = piece if acc is None else acc + piece
    return acc


def _inv_count(i, w):
    row = lax.broadcasted_iota(jnp.int32, (BLOCK, 1), 0)
    return 1.0 / jnp.minimum(i * BLOCK + row + 1, w).astype(F32)


def _fill_pool_input(i, ubuf, uc_ref, halo_ref):
    ubuf[0:HALO, :] = jnp.where(i > 0, halo_ref[...], 0.0)
    ubuf[HALO:, :] = uc_ref[...]


def _pooled(i, g, w, ubuf):
    cols = slice(g * POOL_GROUP_DIM, (g + 1) * POOL_GROUP_DIM)
    return _window_sum(ubuf, g, w, lambda k: HALO - k) * _inv_count(i, w) - ubuf[HALO:, cols]


def _head_variants(pair):
    low = lax.broadcasted_iota(jnp.int32, pair.shape, 1) < HEAD_DIM
    swapped = pltpu.roll(pair, HEAD_DIM, 1)
    zero = jnp.zeros_like(pair)
    pick = lambda c, a, b: jnp.where(c, a, b).astype(BF16)
    return [[pick(low, pair, zero), pick(low, zero, swapped)], [pick(low, swapped, zero), pick(low, zero, pair)]]


def _head_probs(i, hq, rows, s_ref, biasm_ref, sinks_ref):
    s = s_ref[hq, rows, :] * ATTN_SCALE + biasm_ref[hq, rows, :]
    col = lax.broadcasted_iota(jnp.int32, s.shape, 1)
    s = jnp.where((i == 0) & (col < BLOCK), NEG_INF, s)
    sink = sinks_ref[0, hq]
    m = jnp.maximum(jnp.max(s, axis=-1, keepdims=True), sink)
    p = jnp.exp(s - m)
    e_sink = jnp.exp(sink - m)
    inv = 1.0 / (jnp.sum(p, axis=-1, keepdims=True) + e_sink)
    return p * inv, e_sink * inv


def _head_slot(hq):
    return 4 * (hq // GQA_GROUP) + 2 * (hq % 2) + (hq % GQA_GROUP) // 2


def _mixer_in_specs(cur, prv):
    return [pl.BlockSpec((BLOCK, 512), lambda i: (cur(i), 0)),
            pl.BlockSpec((HALO, 512), lambda i: (jnp.maximum(cur(i) * (BLOCK // HALO) - 1, 0), 0)),
            pl.BlockSpec((BLOCK, 512), lambda i: (cur(i), 1)),
            pl.BlockSpec((BLOCK, 256), lambda i: (cur(i), 4)),
            pl.BlockSpec((BLOCK, 256), lambda i: (prv(i), 4))]


def _mixer_param_specs():
    return [pl.BlockSpec((N_Q_HEADS, BLOCK, 2 * BLOCK), lambda i: (0, 0, 0)), SMEM,
            pl.BlockSpec((4, POOL_GROUP_DIM, POOL_GROUP_DIM), lambda i: (0, 0, 0)),
            pl.BlockSpec((1, POOL_WIDTH), lambda i: (0, 0))]


def mixers_fwd(proj, biasm, sinks, w_pool, pool_scale, shards):
    t = proj.shape[0]
    nb = t // BLOCK
    ns = len(shards)

    def body(*refs):
        uc_ref, halo_ref, q_ref, kvc_ref, kvp_ref, biasm_ref, sinks_ref, wp_ref, sc_ref = refs[:9]
        shard_refs = refs[9:9 + ns]
        out_ref, pooled_ref, p_all, psink_ref = refs[9 + ns:13 + ns]
        gathered_refs = refs[13 + ns:13 + 2 * ns]
        ubuf, s_all, send_sems, recv_sems, local_sems = refs[13 + 2 * ns:18 + 2 * ns]
        i = pl.program_id(0)
        start, finish = _gather_plan(shard_refs, gathered_refs, send_sems, recv_sems, local_sems, refs[18 + 2 * ns:])
        pl.when(i == 0)(start)

        _fill_pool_input(i, ubuf, uc_ref, halo_ref)
        for g, w in enumerate(POOL_WINDOWS):
            cols = slice(g * POOL_GROUP_DIM, (g + 1) * POOL_GROUP_DIM)
            pooled = _pooled(i, g, w, ubuf).astype(BF16)
            pooled_ref[:, cols] = pooled
            out_ref[:, cols] = (_dot(pooled, wp_ref[g]) * sc_ref[:, cols]).astype(BF16)
        kv = jnp.concatenate([kvp_ref[...], kvc_ref[...]], axis=0)
        k_var = _head_variants(kv[:, 0:2 * HEAD_DIM])
        v_var = _head_variants(kv[:, 2 * HEAD_DIM:])
        for hq in range(N_Q_HEADS):
            j, half, h = hq // 2, hq % 2, hq // GQA_GROUP
            q2 = q_ref[:, 2 * HEAD_DIM * j:2 * HEAD_DIM * (j + 1)].astype(BF16)
            s_all[hq] = _dot_nt(q2, k_var[h][half])
        psink_ref[...] = jnp.zeros_like(psink_ref)
        for hq in range(N_Q_HEADS):
            for r in range(0, BLOCK, ROW_CHUNK):
                rows = slice(r, r + ROW_CHUNK)
                probs, p_sink = _head_probs(i, hq, rows, s_all, biasm_ref, sinks_ref)
                p_all[_head_slot(hq), rows, :] = probs.astype(BF16)
                psink_ref[rows, hq:hq + 1] = p_sink
        for j in range(N_Q_HEADS // 2):
            h = 2 * j // GQA_GROUP
            acc = _dot(p_all[_head_slot(2 * j)], v_var[h][0]) + _dot(p_all[_head_slot(2 * j + 1)], v_var[h][1])
            out_ref[:, POOL_WIDTH + 2 * HEAD_DIM * j:POOL_WIDTH + 2 * HEAD_DIM * (j + 1)] = acc.astype(BF16)

        pl.when(i == max(nb - 4, 0))(finish.forward)
        pl.when(i == nb - 1)(finish.complete)

    return pl.pallas_call(
        body, name="mixers_fwd", grid=(nb,),
        out_shape=[jax.ShapeDtypeStruct((t, 2 * POOL_WIDTH), BF16), jax.ShapeDtypeStruct((t, POOL_WIDTH), BF16),
                   jax.ShapeDtypeStruct((N_Q_HEADS, t, 2 * BLOCK), BF16), jax.ShapeDtypeStruct((t, 128), F32)]
        + [jax.ShapeDtypeStruct((N_DEV, *sh.shape), sh.dtype) for sh in shards],
        in_specs=_mixer_in_specs(lambda i: i, lambda i: jnp.maximum(i - 1, 0)) + _mixer_param_specs() + [ANY] * ns,
        out_specs=[pl.BlockSpec((BLOCK, 2 * POOL_WIDTH), lambda i: (i, 0)), pl.BlockSpec((BLOCK, POOL_WIDTH), lambda i: (i, 0)),
                   pl.BlockSpec((N_Q_HEADS, BLOCK, 2 * BLOCK), lambda i: (0, i, 0)), pl.BlockSpec((BLOCK, 128), lambda i: (i, 0))]
        + [ANY] * ns,
        scratch_shapes=[pltpu.VMEM((HALO + BLOCK, POOL_WIDTH), F32), pltpu.VMEM((N_Q_HEADS, BLOCK, 2 * BLOCK), F32)]
        + _gather_scratch(shards),
        compiler_params=_params(),
    )(proj, proj, proj, proj, proj, biasm, sinks, w_pool, pool_scale, *shards)


def outproj_norm(cat, w, x, g, g_next, shard, partial):
    t, d = x.shape
    tm = TOKEN_TILE
    last = t // tm - 1
    rows = [(partial.shape[1] - shard.shape[0], shard.shape[0])]

    def body(c_ref, w_ref, x_ref, g_ref, gn_ref, shard_ref, partial_ref, mix_ref, x1_ref, h2_ref, gathered_ref,
             send_sems, recv_sems, local_sems, bounce):
        i = pl.program_id(0)
        start, finish = _gather_plan([shard_ref], [gathered_ref], send_sems, recv_sems, local_sems, [bounce], rows)
        pl.when(i == 0)(start)
        mix = _dot(c_ref[...], w_ref[...])
        mix_ref[...] = mix
        x1 = x_ref[...] + (mix * _rstd(mix)) * g_ref[...]
        x1_ref[...] = x1
        h2_ref[...] = ((x1 * _rstd(x1)) * gn_ref[...]).astype(BF16)
        pl.when(i == max(last - 1, 0))(finish.forward)
        pl.when(i == last)(finish.complete)

    row = pl.BlockSpec((tm, d), lambda i: (i, 0))
    gain = pl.BlockSpec((1, d), lambda i: (0, 0))
    return pl.pallas_call(
        body, name="outproj_norm", grid=(t // tm,),
        out_shape=[jax.ShapeDtypeStruct((t, d), F32), jax.ShapeDtypeStruct((t, d), F32), jax.ShapeDtypeStruct((t, d), BF16),
                   jax.ShapeDtypeStruct(partial.shape, partial.dtype)],
        in_specs=[pl.BlockSpec((tm, cat.shape[1]), lambda i: (i, 0)), pl.BlockSpec(w.shape, lambda i: (0, 0)), row, gain, gain,
                  ANY, ANY],
        out_specs=[row, row, row, ANY],
        input_output_aliases={6: 3},
        scratch_shapes=_gather_scratch([shard]),
        compiler_params=_params(),
    )(cat, w, x, g, g_next, shard, partial)


def ffn_up(h, gate_t, up_t, down_shard):
    t, d = h.shape
    n = gate_t.shape[1]
    f = N_DEV * n
    tm, ts = FFN_TOKEN_TILE, FF_SHARDS_PER_TILE
    tn = ts * n
    steps = (f // tn, t // tm)

    def body(h_ref, wg_ref, wu_ref, shard_ref, gate_ref, up_ref, a_ref, gathered_ref,
             send_sems, recv_sems, local_sems, bounce):
        j, i = pl.program_id(0), pl.program_id(1)
        start, finish = _gather_plan([shard_ref], [gathered_ref], send_sems, recv_sems, local_sems, [bounce])
        pl.when((i == 0) & (j == 0))(start)

        hv = h_ref[...]
        gate = _dot_nt(hv, _merge_rows(wg_ref[...]))
        up = _dot_nt(hv, _merge_rows(wu_ref[...]))
        gate_ref[...] = gate.astype(BF16)
        up_ref[...] = up.astype(BF16)
        a_ref[...] = (gate * (1.0 / (1.0 + jnp.exp(-gate))) * up).astype(BF16)

        pl.when((j == steps[0] - 1) & (i == max(steps[1] - 2, 0)))(finish.forward)
        pl.when((j == steps[0] - 1) & (i == steps[1] - 1))(finish.complete)

    wide = pl.BlockSpec((tm, tn), lambda j, i: (i, j))
    return pl.pallas_call(
        body, name="ffn_up", grid=steps,
        out_shape=[jax.ShapeDtypeStruct((t, f), BF16)] * 3
        + [jax.ShapeDtypeStruct((N_DEV, *down_shard.shape), down_shard.dtype)],
        in_specs=[pl.BlockSpec((tm, d), lambda j, i: (i, 0)),
                  pl.BlockSpec((ts, n, d), lambda j, i: (j, 0, 0)),
                  pl.BlockSpec((ts, n, d), lambda j, i: (j, 0, 0)), ANY],
        out_specs=[wide, wide, wide, ANY],
        scratch_shapes=_gather_scratch([down_shard]),
        compiler_params=_params(),
    )(h, gate_t, up_t, down_shard)


def ffn_down_loss(a, w_down, x1, g, target):
    t, d = x1.shape
    tm = WIDE_K_TOKEN_TILE

    def body(a_ref, w_ref, x_ref, g_ref, t_ref, df_ref, dy_ref, dg_ref, loss_ref):
        @pl.when(pl.program_id(0) == 0)
        def _():
            dg_ref[...] = jnp.zeros_like(dg_ref)
            loss_ref[...] = jnp.zeros_like(loss_ref)

        f = _dot(a_ref[...], _merge_rows(w_ref[...]))
        r = _rstd(f)
        g = g_ref[...]
        err = x_ref[...] + (f * r) * g - t_ref[...]
        loss_ref[...] += 0.5 * jnp.sum(jnp.mean(err * err, axis=-1, keepdims=True))
        dy = err * (1.0 / d)
        dy_ref[...] = dy
        df, dg_rows = _norm_bwd(dy, f, r, g)
        df_ref[...] = df.astype(BF16)
        dg_ref[...] += _as_rows(jnp.sum(dg_rows, axis=0, keepdims=True))

    row = pl.BlockSpec((tm, d), lambda i: (i, 0))
    gain = pl.BlockSpec((1, d), lambda i: (0, 0))
    return pl.pallas_call(
        body, name="ffn_down_loss", grid=(t // tm,),
        out_shape=[jax.ShapeDtypeStruct((t, d), BF16), jax.ShapeDtypeStruct((t, d), F32),
                   jax.ShapeDtypeStruct((d // 128, 128), F32), jax.ShapeDtypeStruct((1, 128), F32)],
        in_specs=[pl.BlockSpec((tm, a.shape[1]), lambda i: (i, 0)), pl.BlockSpec(w_down.shape, lambda i: (0, 0, 0)), row, gain, row],
        out_specs=[row, row, pl.BlockSpec((d // 128, 128), lambda i: (0, 0)), pl.BlockSpec((1, 128), lambda i: (0, 0))],
        compiler_params=_params(),
    )(a, w_down, x1, g, target)


def ffn_down_bwd(df, w_down, gate, up):
    t, d = df.shape
    n = w_down.shape[1]
    f = gate.shape[1]
    tm, ts = FFN_TOKEN_TILE, FF_SHARDS_PER_TILE
    tn = ts * n

    def body(df_ref, w_ref, gate_ref, up_ref, dgate_ref, dup_ref):
        da = _dot_nt(df_ref[...], _merge_rows(w_ref[...]))
        gate = gate_ref[...].astype(F32)
        sig = 1.0 / (1.0 + jnp.exp(-gate))
        dgate_ref[...] = (da * up_ref[...].astype(F32) * (sig * (1.0 + gate * (1.0 - sig)))).astype(BF16)
        dup_ref[...] = (da * (gate * sig)).astype(BF16)

    wide = pl.BlockSpec((tm, tn), lambda j, i: (i, j))
    return pl.pallas_call(
        body, name="ffn_down_bwd", grid=(f // tn, t // tm),
        out_shape=[jax.ShapeDtypeStruct((t, f), BF16)] * 2,
        in_specs=[pl.BlockSpec((tm, d), lambda j, i: (i, 0)), pl.BlockSpec((ts, n, d), lambda j, i: (j, 0, 0)), wide, wide],
        out_specs=[wide, wide],
        compiler_params=_params(),
    )(df, w_down, gate, up)


def grad_rows(a, b, name, by_core=False):
    t, m = a.shape
    d = b.shape[1]
    r = m // N_DEV
    tt = TOKEN_TILE
    last = t // tt - 1
    out_shape = (2, N_CHIP, r, d) if by_core else (N_DEV, r, d)

    def body(a_ref, b_ref, out_ref, acc):
        k = pl.program_id(0)

        @pl.when(k == 0)
        def _():
            acc[...] = jnp.zeros_like(acc)

        acc[...] += _dot_tn(a_ref[...], b_ref[...])

        @pl.when(k == last)
        def _():
            if by_core:
                blocks = acc[...].reshape(N_CHIP, 2, r, d)
                for chip in range(N_CHIP):
                    for core in range(2):
                        out_ref[core, chip] = blocks[chip, core].astype(BF16)
            else:
                out_ref[...] = acc[...].reshape(out_shape).astype(BF16)

    return pl.pallas_call(
        body, name=name, grid=(t // tt,),
        out_shape=jax.ShapeDtypeStruct(out_shape, BF16),
        in_specs=[pl.BlockSpec((tt, m), lambda k: (k, 0)), pl.BlockSpec((tt, d), lambda k: (k, 0))],
        out_specs=pl.BlockSpec(out_shape, lambda k: (0,) * len(out_shape)),
        scratch_shapes=[pltpu.VMEM((m, d), F32)],
        compiler_params=_params(),
    )(a, b)


def grad_ffn(lhs, b, name, pair_parts=()):
    t, f = lhs[0].shape
    d = b.shape[1]
    nw = len(lhs)
    na = len(pair_parts)
    n = f // N_DEV
    tt, ts = TOKEN_TILE, FF_SHARDS_PER_TILE
    tn = ts * n
    steps = (f // tn, t // tt)

    def body(*refs):
        a_refs, b_ref, part_refs = refs[:nw], refs[nw], refs[nw + 1:nw + 1 + na]
        out_refs = refs[nw + 1 + na:2 * nw + 1 + na]
        got_refs = refs[2 * nw + 1 + na:2 * nw + 1 + 2 * na]
        acc = refs[2 * nw + 1 + 2 * na]
        i, k = pl.program_id(0), pl.program_id(1)
        if na:
            start, finish = _pair_plan(part_refs, got_refs, *refs[2 * nw + 2 + 2 * na:])
            pl.when((i == 0) & (k == 0))(start)

        @pl.when(k == 0)
        def _():
            acc[...] = jnp.zeros_like(acc)

        for w in range(nw):
            acc[w] += _dot_tn(a_refs[w][...], b_ref[...])

        @pl.when(k == steps[1] - 1)
        def _():
            for w in range(nw):
                blocks = acc[w].reshape(ts // 2, 2, n, d)
                for chip in range(ts // 2):
                    for core in range(2):
                        out_refs[w][core, chip] = blocks[chip, core].astype(BF16)

        if na:
            pl.when((i == steps[0] - 1) & (k == steps[1] - 1))(finish)

    out = pl.pallas_call(
        body, name=name, grid=steps,
        out_shape=[jax.ShapeDtypeStruct((2, N_CHIP, n, d), BF16)] * nw
        + [jax.ShapeDtypeStruct(p.shape[1:], p.dtype) for p in pair_parts],
        in_specs=[pl.BlockSpec((tt, tn), lambda i, k: (k, i))] * nw + [pl.BlockSpec((tt, d), lambda i, k: (k, 0))] + [ANY] * na,
        out_specs=[pl.BlockSpec((2, ts // 2, n, d), lambda i, k: (0, i, 0, 0))] * nw + [ANY] * na,
        scratch_shapes=[pltpu.VMEM((nw, tn, d), F32)]
        + ([pltpu.SemaphoreType.DMA((na,)), pltpu.SemaphoreType.DMA((na,))] if na else []),
        compiler_params=_params(),
    )(*lhs, b, *pair_parts)
    return out[:nw], out[nw:]


def ffn_up_bwd(dgate, dup, gate_t, up_t, x1, g_ffn, dy, mix, g_mix, chip_parts):
    t, d = x1.shape
    n = gate_t.shape[1]
    f = N_DEV * n
    tm = WIDE_K_TOKEN_TILE
    na = len(chip_parts)
    last = t // tm - 1

    def body(*refs):
        dg_ref, du_ref, wg_ref, wu_ref, x_ref, gf_ref, dy_ref, mix_ref, gm_ref = refs[:9]
        part_refs = refs[9:9 + na]
        dx1_ref, dmix_ref, dgf_ref, dgm_ref = refs[9 + na:13 + na]
        slot_refs = refs[13 + na:13 + 2 * na]
        send_sems, recv_sems, local_sems = refs[13 + 2 * na:16 + 2 * na]
        i = pl.program_id(0)
        start, finish = _chip_exchange_plan(part_refs, slot_refs, send_sems, recv_sems, local_sems, refs[16 + 2 * na:])

        @pl.when(i == 0)
        def _():
            start()
            dgf_ref[...] = jnp.zeros_like(dgf_ref)
            dgm_ref[...] = jnp.zeros_like(dgm_ref)

        dh = _dot(dg_ref[...], _merge_rows(wg_ref[...])) + _dot(du_ref[...], _merge_rows(wu_ref[...]))
        x1 = x_ref[...]
        dx, dgf_rows = _norm_bwd(dh, x1, _rstd(x1), gf_ref[...])
        dx1 = dy_ref[...] + dx
        dx1_ref[...] = dx1
        dgf_ref[...] += _as_rows(jnp.sum(dgf_rows, axis=0, keepdims=True))
        mix = mix_ref[...]
        dmix, dgm_rows = _norm_bwd(dx1, mix, _rstd(mix), gm_ref[...])
        dmix_ref[...] = dmix.astype(BF16)
        dgm_ref[...] += _as_rows(jnp.sum(dgm_rows, axis=0, keepdims=True))
        pl.when(i == last)(finish)

    row = pl.BlockSpec((tm, d), lambda i: (i, 0))
    wide = pl.BlockSpec((tm, f), lambda i: (i, 0))
    gain = pl.BlockSpec((1, d), lambda i: (0, 0))
    gain_rows = pl.BlockSpec((d // 128, 128), lambda i: (0, 0))
    whole = pl.BlockSpec((N_DEV, n, d), lambda i: (0, 0, 0), pipeline_mode=pl.Buffered(1))
    out = pl.pallas_call(
        body, name="ffn_up_bwd", grid=(t // tm,),
        out_shape=[jax.ShapeDtypeStruct((t, d), F32), jax.ShapeDtypeStruct((t, d), BF16),
                   jax.ShapeDtypeStruct((d // 128, 128), F32), jax.ShapeDtypeStruct((d // 128, 128), F32)]
        + [jax.ShapeDtypeStruct(p.shape, p.dtype) for p in chip_parts],
        in_specs=[wide, wide, whole, whole, row, gain, row, row, gain] + [ANY] * na,
        out_specs=[row, row, gain_rows, gain_rows] + [ANY] * na,
        scratch_shapes=_chip_exchange_scratch(chip_parts),
        compiler_params=_params(),
    )(dgate, dup, gate_t, up_t, x1, g_ffn, dy, mix, g_mix, *chip_parts)
    return out[:4], out[4:]


def outproj_bwd(dmix, w_out):
    t, d = dmix.shape
    tm = TOKEN_TILE

    def body(dm_ref, w_ref, out_ref):
        out_ref[...] = _dot_nt(dm_ref[...], w_ref[...])

    return pl.pallas_call(
        body, name="outproj_bwd", grid=(t // tm,),
        out_shape=jax.ShapeDtypeStruct((t, w_out.shape[0]), F32),
        in_specs=[pl.BlockSpec((tm, d), lambda i: (i, 0)), pl.BlockSpec(w_out.shape, lambda i: (0, 0))],
        out_specs=pl.BlockSpec((tm, w_out.shape[0]), lambda i: (i, 0)),
        compiler_params=_params(),
    )(dmix, w_out)


def mixers_bwd(proj, dcat, pooled, probs, p_sinks, w_pool, pool_scale, ffn_parts):
    t = proj.shape[0]
    nb = t // BLOCK
    na = len(ffn_parts)

    def body(*refs):
        (q_ref, kvc_ref, kvp_ref, dcat_ref, pooled_ref, p_all, psink_ref, wp_ref, sc_ref) = refs[:9]
        part_refs = refs[9:9 + na]
        dproj_ref, dbias_ref, dsink_ref, dwp_ref, dsc_ref = refs[9 + na:14 + na]
        slot_refs = refs[14 + na:14 + 2 * na]
        dbuf, c_u, c_q, c_kv, dp_all, ds_all, sink_acc = refs[14 + 2 * na:21 + 2 * na]
        send_sems, recv_sems, local_sems = refs[21 + 2 * na:24 + 2 * na]
        bounce = refs[24 + 2 * na:]
        i = pl.program_id(0)
        lane = lax.broadcasted_iota(jnp.int32, (1, 128), 1)
        start, finish = _chip_exchange_plan(part_refs, slot_refs, send_sems, recv_sems, local_sems, bounce)

        @pl.when(i == 0)
        def _():
            start()
            dbias_ref[...] = jnp.zeros_like(dbias_ref)
            dwp_ref[...] = jnp.zeros_like(dwp_ref)
            dsc_ref[...] = jnp.zeros_like(dsc_ref)
            dsink_ref[...] = jnp.zeros_like(dsink_ref)
            dbuf[...] = jnp.zeros_like(dbuf)
            c_u[...] = jnp.zeros_like(c_u)
            c_q[...] = jnp.zeros_like(c_q)
            c_kv[...] = jnp.zeros_like(c_kv)

        @pl.when(i < nb)
        def _():
            for g, w in enumerate(POOL_WINDOWS):
                cols = slice(g * POOL_GROUP_DIM, (g + 1) * POOL_GROUP_DIM)
                pooled = pooled_ref[:, cols]
                mixed = _dot(pooled, wp_ref[g])
                dout = dcat_ref[:, cols]
                dsc_ref[g:g + 1, :] += jnp.sum(dout * mixed, axis=0, keepdims=True)
                dmixed = (dout * sc_ref[:, cols]).astype(BF16)
                dwp_ref[g] += _dot_tn(pooled, dmixed)
                dpooled = _dot_nt(dmixed, wp_ref[g])
                scaled = dpooled * _inv_count(i, w)
                dbuf[BLOCK:, cols] = scaled[0:HALO]
                dproj_ref[:, cols] = (_window_sum(dbuf, g, w, lambda k: k) + c_u[:, cols]).astype(BF16)
                dbuf[0:BLOCK, cols] = scaled
                c_u[:, cols] = -dpooled

            kv = jnp.concatenate([kvp_ref[...], kvc_ref[...]], axis=0)
            k_var = _head_variants(kv[:, 0:2 * HEAD_DIM])
            v_var = _head_variants(kv[:, 2 * HEAD_DIM:])
            q2s = [q_ref[:, 2 * HEAD_DIM * j:2 * HEAD_DIM * (j + 1)].astype(BF16) for j in range(N_Q_HEADS // 2)]
            do2s = [dcat_ref[:, POOL_WIDTH + 2 * HEAD_DIM * j:POOL_WIDTH + 2 * HEAD_DIM * (j + 1)].astype(BF16)
                    for j in range(N_Q_HEADS // 2)]
            slot = _head_slot
            for hq in range(N_Q_HEADS):
                j, half, h = hq // 2, hq % 2, hq // GQA_GROUP
                dp_all[hq] = _dot_nt(do2s[j], v_var[h][half])
            sink_acc[...] = jnp.zeros_like(sink_acc)
            for hq in range(N_Q_HEADS):
                for r in range(0, BLOCK, ROW_CHUNK):
                    rows = slice(r, r + ROW_CHUNK)
                    probs = p_all[slot(hq), rows, :].astype(F32)
                    dp = dp_all[hq, rows, :]
                    delta = jnp.sum(probs * dp, axis=-1, keepdims=True)
                    ds = probs * (dp - delta)
                    dbias_ref[hq, rows, :] += ds
                    sink_acc[rows, :] += jnp.where(lane == hq, psink_ref[rows, :], 0.0) * delta
                    ds_all[slot(hq), rows, :] = (ds * ATTN_SCALE).astype(BF16)
            dsink_ref[...] -= jnp.sum(sink_acc[...], axis=0, keepdims=True)
            dq2 = [None] * (N_Q_HEADS // 2)
            for hq in range(N_Q_HEADS):
                j, half, h = hq // 2, hq % 2, hq // GQA_GROUP
                dq = _dot(ds_all[slot(hq)], k_var[h][half])
                dq2[j] = dq if dq2[j] is None else dq2[j] + dq
            low = lax.broadcasted_iota(jnp.int32, (2 * BLOCK, 2 * HEAD_DIM), 1) < HEAD_DIM
            dk_half, dv_half = [[None, None], [None, None]], [[None, None], [None, None]]
            for h in range(N_KV_HEADS):
                for half in range(2):
                    heads = [hq for hq in range(GQA_GROUP * h, GQA_GROUP * (h + 1)) if hq % 2 == half]
                    base = slot(heads[0])
                    q_rows = jnp.concatenate([q2s[hq // 2] for hq in heads], axis=0)
                    do_rows = jnp.concatenate([do2s[hq // 2] for hq in heads], axis=0)
                    dk_half[h][half] = _dot_tn(_merge_rows(ds_all[base:base + 2]), q_rows)
                    dv_half[h][half] = _dot_tn(_merge_rows(p_all[base:base + 2]), do_rows)

            def pair_of(halves):
                return jnp.where(low, halves[0][0] + pltpu.roll(halves[0][1], HEAD_DIM, 1),
                                 halves[1][1] + pltpu.roll(halves[1][0], HEAD_DIM, 1))

            dkv = jnp.concatenate([pair_of(dk_half), pair_of(dv_half)], axis=1)
            dproj_ref[:, POOL_WIDTH:2 * POOL_WIDTH] = c_q[...].astype(BF16)
            dproj_ref[:, 2 * POOL_WIDTH:] = (c_kv[...] + dkv[0:BLOCK]).astype(BF16)
            c_q[...] = jnp.concatenate(dq2, axis=1)
            c_kv[...] = dkv[BLOCK:]

        @pl.when(i == nb)
        def _():
            dbuf[BLOCK:, :] = jnp.zeros((HALO, POOL_WIDTH), F32)
            for g, w in enumerate(POOL_WINDOWS):
                cols = slice(g * POOL_GROUP_DIM, (g + 1) * POOL_GROUP_DIM)
                dproj_ref[:, cols] = (_window_sum(dbuf, g, w, lambda k: k) + c_u[:, cols]).astype(BF16)
            dproj_ref[:, POOL_WIDTH:2 * POOL_WIDTH] = c_q[...].astype(BF16)
            dproj_ref[:, 2 * POOL_WIDTH:] = c_kv[...].astype(BF16)
            finish()

    cur = lambda i: jnp.minimum(i, nb - 1)
    prv = lambda i: jnp.maximum(jnp.minimum(i, nb - 1) - 1, 0)
    out = pl.pallas_call(
        body, name="mixers_bwd", grid=(nb + 1,),
        out_shape=[jax.ShapeDtypeStruct((t, proj.shape[1]), BF16),
                   jax.ShapeDtypeStruct((N_Q_HEADS, BLOCK, 2 * BLOCK), F32),
                   jax.ShapeDtypeStruct((1, 128), F32),
                   jax.ShapeDtypeStruct((4, POOL_GROUP_DIM, POOL_GROUP_DIM), F32),
                   jax.ShapeDtypeStruct((len(POOL_WINDOWS), POOL_GROUP_DIM), F32)]
        + [jax.ShapeDtypeStruct(p.shape, p.dtype) for p in ffn_parts],
        in_specs=_mixer_in_specs(cur, prv)[2:]
        + [pl.BlockSpec((BLOCK, 2 * POOL_WIDTH), lambda i: (cur(i), 0)), pl.BlockSpec((BLOCK, POOL_WIDTH), lambda i: (cur(i), 0)),
           pl.BlockSpec((N_Q_HEADS, BLOCK, 2 * BLOCK), lambda i: (0, cur(i), 0)), pl.BlockSpec((BLOCK, 128), lambda i: (cur(i), 0))]
        + _mixer_param_specs()[2:] + [ANY] * na,
        out_specs=[pl.BlockSpec((BLOCK, proj.shape[1]), lambda i: (jnp.maximum(i - 1, 0), 0)),
                   pl.BlockSpec((N_Q_HEADS, BLOCK, 2 * BLOCK), lambda i: (0, 0, 0)),
                   pl.BlockSpec((1, 128), lambda i: (0, 0)),
                   pl.BlockSpec((4, POOL_GROUP_DIM, POOL_GROUP_DIM), lambda i: (0, 0, 0)),
                   pl.BlockSpec((len(POOL_WINDOWS), POOL_GROUP_DIM), lambda i: (0, 0))] + [ANY] * na,
        scratch_shapes=[pltpu.VMEM((BLOCK + HALO, POOL_WIDTH), F32),
                        pltpu.VMEM((BLOCK, POOL_WIDTH), F32), pltpu.VMEM((BLOCK, POOL_WIDTH), F32),
                        pltpu.VMEM((BLOCK, 256), F32),
                        pltpu.VMEM((N_Q_HEADS, BLOCK, 2 * BLOCK), F32), pltpu.VMEM((N_Q_HEADS, BLOCK, 2 * BLOCK), BF16),
                        pltpu.VMEM((BLOCK, 128), F32)]
        + _chip_exchange_scratch(ffn_parts),
        compiler_params=_params(),
    )(proj, proj, proj, dcat, pooled, probs, p_sinks, w_pool, pool_scale, *ffn_parts)
    return out[:5], out[5:]


def inproj_bwd(dproj, w_in_t, x, g, dx1):
    t, d = x.shape
    n = dproj.shape[1]
    tm = TOKEN_TILE

    def body(dp_ref, w_ref, x_ref, g_ref, dx1_ref, dx_ref, dg_ref):
        @pl.when(pl.program_id(0) == 0)
        def _():
            dg_ref[...] = jnp.zeros_like(dg_ref)

        dh = _dot(dp_ref[...], w_ref[...])
        xv = x_ref[...]
        dx, dg_rows = _norm_bwd(dh, xv, _rstd(xv), g_ref[...])
        dx_ref[...] = dx1_ref[...] + dx
        dg_ref[...] += _as_rows(jnp.sum(dg_rows, axis=0, keepdims=True))

    row = pl.BlockSpec((tm, d), lambda i: (i, 0))
    gain = pl.BlockSpec((1, d), lambda i: (0, 0))
    return pl.pallas_call(
        body, name="inproj_bwd", grid=(t // tm,),
        out_shape=[jax.ShapeDtypeStruct((t, d), F32), jax.ShapeDtypeStruct((d // 128, 128), F32)],
        in_specs=[pl.BlockSpec((tm, n), lambda i: (i, 0)), pl.BlockSpec(w_in_t.shape, lambda i: (0, 0)), row, gain, row],
        out_specs=[row, pl.BlockSpec((d // 128, 128), lambda i: (0, 0))],
        compiler_params=_params(),
    )(dproj, w_in_t, x, g, dx1)


def _bucket_band():
    qi = jnp.arange(BLOCK)[:, None]
    kj = jnp.arange(2 * BLOCK)[None, :]
    dist = qi + BLOCK - kj
    n = jnp.maximum(dist, 0)
    nf = jnp.maximum(n, 1).astype(F32)
    large = MAX_EXACT + (jnp.log(nf / MAX_EXACT) / np.float32(np.log(MAX_DISTANCE / MAX_EXACT))
                         * (N_BUCKETS - MAX_EXACT)).astype(jnp.int32)
    large = jnp.minimum(large, N_BUCKETS - 1)
    bucket = jnp.where(n < MAX_EXACT, n, large)
    in_window = (dist >= 0) & (dist < BLOCK)
    return bucket.astype(F32), in_window.astype(F32)


def kernel(x, g_pre_mix, w_in, w_pool, pool_scale, rel_bias, sinks, w_out, g_post_mix, g_pre_ffn, w_gate, w_up, w_down, g_post_ffn, loss_target, m_g_pre_mix, m_w_in, m_w_pool, m_pool_scale, m_rel_bias, m_sinks, m_w_out, m_g_post_mix, m_g_pre_ffn, m_w_gate, m_w_up, m_w_down, m_g_post_ffn, v_g_pre_mix, v_w_in, v_w_pool, v_pool_scale, v_rel_bias, v_sinks, v_w_out, v_g_post_mix, v_g_pre_ffn, v_w_gate, v_w_up, v_w_down, v_g_post_ffn):
    d = x.shape[-1]
    xs, target = x[0], loss_target[0]

    w_in_ts = w_in[0].T.astype(BF16)
    w_out_s = w_out[0].astype(BF16)
    gate_ts = w_gate[0].T.astype(BF16)
    up_ts = w_up[0].T.astype(BF16)
    w_down_s = w_down[0].astype(BF16)

    bucket, in_window = _bucket_band()
    biasm = bias_band(bucket, in_window, rel_bias)
    w_pool_b = w_pool[0].astype(BF16)
    half = up_ts.shape[0] // 2
    proj, h1, w_in_t, up_t = norm_inproj(xs, g_pre_mix, w_in_ts, up_ts[:half], up_ts.shape[0])
    w_in_t = w_in_t.reshape(-1, d)
    cat, pooled, probs, p_sinks, gate_t, w_out_f = mixers_fwd(proj, biasm, sinks, w_pool_b, pool_scale, [gate_ts, w_out_s])
    w_out_f = w_out_f.reshape(-1, d)
    mix, x1, h2, up_t = outproj_norm(cat, w_out_f, xs, g_post_mix, g_pre_ffn, up_ts[half:], up_t)
    gate, up, act, w_down_f = ffn_up(h2, gate_t, up_t, w_down_s)
    df, dy, dg_post_ffn, loss_part = ffn_down_loss(act, w_down_f, x1, g_post_ffn, target)

    def pair_sum(parts, tag):
        return pair_add(parts, pair_exchange(parts, "pair_exchange_" + tag), "pair_add_" + tag)

    dgate, dup = ffn_down_bwd(df, w_down_f, gate, up)
    (d_gate, d_up), _ = grad_ffn([dgate, dup], h2, "grad_w_gate_up")
    (d_down,), got_gate_up = grad_ffn([act], df, "grad_w_down", [d_gate, d_up])
    q_gate, q_up, q_down = pair_add(
        [d_gate, d_up, d_down], [*got_gate_up, *pair_exchange([d_down], "pair_exchange_down")], "pair_add_ffn")
    (dx1, dmix, dg_pre_ffn, dg_post_mix), (gate_slots, down_slots) = ffn_up_bwd(
        dgate, dup, gate_t, up_t, x1, g_pre_ffn, dy, mix, g_post_mix, [q_gate, q_down])
    dcat = outproj_bwd(dmix, w_out_f)
    d_out = grad_rows(cat, dmix, "grad_w_out", by_core=True)
    q_out, = pair_sum([d_out], "out")
    (dproj, dbias, dsinks, dw_pool, dpool_scale), (up_slots, out_slots) = mixers_bwd(
        proj, dcat, pooled, probs, p_sinks, w_pool_b, pool_scale, [q_up, q_out])
    drel_bias = bias_band_bwd(bucket, dbias)
    grad_x, dg_pre_mix = inproj_bwd(dproj, w_in_t, xs, g_pre_mix, dx1)

    small_w = [g_pre_mix, g_post_mix, g_pre_ffn, g_post_ffn, pool_scale, sinks, w_pool, rel_bias.T]
    small_m = [m_g_pre_mix, m_g_post_mix, m_g_pre_ffn, m_g_post_ffn, m_pool_scale, m_sinks, m_w_pool, m_rel_bias.T]
    small_v = [v_g_pre_mix, v_g_post_mix, v_g_pre_ffn, v_g_post_ffn, v_pool_scale, v_sinks, v_w_pool, v_rel_bias.T]
    d_in_t, total, total_rb = grad_w_in_small_reduce(
        dproj, h1, [dg_pre_mix, dg_post_mix, dg_pre_ffn, dg_post_ffn], dpool_scale, dsinks, loss_part, dw_pool, drel_bias)
    g_in_t = reduce_w_in(d_in_t)
    loss_row, sm = small_adamw(total, total_rb, small_w, small_m, small_v)
    sm[7] = [r.T for r in sm[7]]
    big_w = [w_in[0].T, w_out[0], w_gate[0].T, w_up[0].T, w_down[0]]
    big_m = [m_w_in[0].T, m_w_out[0], m_w_gate[0].T, m_w_up[0].T, m_w_down[0]]
    big_v = [v_w_in[0].T, v_w_out[0], v_w_gate[0].T, v_w_up[0].T, v_w_down[0]]
    upd = sum_adamw([out_slots, gate_slots, up_slots, down_slots], big_w[1:], big_m[1:], big_v[1:], "sum_adamw")
    upd = [[g_in_t, *adamw_update(big_w[:1], [g_in_t], big_m[:1], big_v[:1], "adamw_in")[0]], *upd]
    back = lambda k, a: (a.T if k in (0, 2, 3) else a)[None]
    big = [[back(k, u) for u in upd[k]] for k in range(5)]

    def ordered(kind):
        s, b = [p[kind] for p in sm], [p[kind] for p in big]
        return [s[0], b[0], s[6], s[4], s[7], s[5], b[1], s[1], s[2], b[2], b[3], b[4], s[3]]

    return (loss_row[0, 0], grad_x[None], *ordered(0), *ordered(1), *ordered(2), *ordered(3))
```

```python
import numpy as np
import jax
import jax.numpy as jnp
from jax import lax
from jax.experimental import pallas as pl
from jax.experimental.pallas import tpu as pltpu

F32 = jnp.float32
BF16 = jnp.bfloat16

N_DEV = 8
N_CHIP = 4
POOL_WIDTH = 512
POOL_WINDOWS = (2, 4, 8, 16)
POOL_GROUP_DIM = 128
HEAD_DIM = 64
N_Q_HEADS = 8
N_KV_HEADS = 2
GQA_GROUP = 4
BLOCK = 128
HALO = 16
ROW_CHUNK = 32
N_BUCKETS = 32
MAX_EXACT = 16
MAX_DISTANCE = 128
EPS = 1e-6
NEG_INF = -1e30
ATTN_SCALE = float(1.0 / np.sqrt(np.float32(HEAD_DIM)))

ADAM_LR = 0.001
ADAM_B1 = 0.9
ADAM_B2 = 0.999
ADAM_EPS = 1e-08
ADAM_WD = 0.01
ADAM_STEP = 10

TOKEN_TILE = 1024
WIDE_K_TOKEN_TILE = 512
FFN_TOKEN_TILE = 1024
FF_SHARDS_PER_TILE = 4
VMEM_LIMIT = 56 * 1024 * 1024
MESH = pl.DeviceIdType.MESH
PAIR_COLLECTIVE_ID = 0
GATHER_COLLECTIVE_ID = 1
CHIP_COLLECTIVE_ID = 2
ANY = pl.BlockSpec(memory_space=pl.ANY)
VMEM = pl.BlockSpec(memory_space=pltpu.VMEM)
SMEM = pl.BlockSpec(memory_space=pltpu.SMEM)


def _params(**kw):
    return pltpu.CompilerParams(vmem_limit_bytes=VMEM_LIMIT, **kw)


def _dot(a, b):
    return jnp.dot(a, b, preferred_element_type=F32)


def _dot_nt(a, b):
    return lax.dot_general(a, b, (((1,), (1,)), ((), ())), preferred_element_type=F32)


def _dot_tn(a, b):
    return lax.dot_general(a, b, (((0,), (0,)), ((), ())), preferred_element_type=F32)


def _rstd(v):
    return lax.rsqrt(jnp.mean(v * v, axis=-1, keepdims=True) + EPS)


def _norm_bwd(dout, v, r, g):
    vn = v * r
    dn = dout * g
    dv = r * (dn - vn * jnp.mean(dn * vn, axis=-1, keepdims=True))
    return dv, dout * vn


def _as_rows(v):
    return jnp.concatenate([v[:, k:k + 128] for k in range(0, v.shape[1], 128)], axis=0)


def _as_lanes(rows):
    return jnp.concatenate([rows[k:k + 1, :] for k in range(rows.shape[0])], axis=1)


def _handshake(peers):
    barrier = pltpu.get_barrier_semaphore()
    for peer in peers:
        pl.semaphore_signal(barrier, inc=1, device_id=peer, device_id_type=MESH)
    pl.semaphore_wait(barrier, len(peers))


def _merge_rows(value):
    s, r, c_ = value.shape
    return value.reshape(s * r, c_)


def _gather_plan(srcs, outs, send_sems, recv_sems, local_sems=None, bounce=None, rows=None, shake=True):
    n = len(srcs)
    x, y, c = lax.axis_index("x"), lax.axis_index("y"), lax.axis_index("c")
    me, sibling = (x, y, c), (x, y, 1 - c)
    chips = [(1 - x, y), (x, 1 - y), (1 - x, 1 - y)]

    def slot(a, px, py, pc):
        whole = outs[a].at[4 * px + 2 * py + pc]
        return whole if rows is None or rows[a] is None else whole.at[pl.ds(*rows[a])]

    def copy(a, k, block, to, from_src=False):
        return pltpu.make_async_remote_copy(
            src_ref=srcs[a] if from_src else slot(a, *block), dst_ref=slot(a, *block),
            send_sem=send_sems.at[k * n + a], recv_sem=recv_sems.at[k * n + a], device_id=to, device_id_type=MESH)

    def own_in(a):
        return pltpu.make_async_copy(srcs[a], bounce[a], local_sems.at[a])

    def own_out(a):
        return pltpu.make_async_copy(bounce[a], slot(a, *me), local_sems.at[a])

    def first(a):
        return [copy(a, 0, me, sibling, True)] + [copy(a, 1 + j, me, (*chip, c), True) for j, chip in enumerate(chips)]

    def passed(a, j):
        return copy(a, 4 + j, (*chips[j], c), sibling)

    def start():
        if shake:
            _handshake([sibling] + [(*chip, c) for chip in chips])
        for a in range(n):
            if bounce is not None:
                own_in(a).start()
            for cp in first(a):
                cp.start()

    def forward():
        if bounce is not None:
            for a in range(n):
                own_in(a).wait()
                own_out(a).start()
        for j, chip in enumerate(chips):
            for a in range(n):
                copy(a, 1 + j, (*chip, c), me).wait_recv()
                passed(a, j).start()

    def complete():
        for a in range(n):
            copy(a, 0, sibling, me).wait_recv()
            for j, chip in enumerate(chips):
                copy(a, 4 + j, (*chip, 1 - c), me).wait_recv()
        for a in range(n):
            for cp in first(a) + [passed(a, j) for j in range(3)]:
                cp.wait_send()
            if bounce is not None:
                own_out(a).wait()

    def finish():
        forward()
        complete()

    finish.forward, finish.complete = forward, complete
    return start, finish


def _gather_scratch(shards):
    n = len(shards)
    return [pltpu.SemaphoreType.DMA((7 * n,)), pltpu.SemaphoreType.DMA((7 * n,)), pltpu.SemaphoreType.DMA((n,))] \
        + [pltpu.VMEM(s.shape, s.dtype) for s in shards]


def _chip_exchange_plan(srcs, outs, send_sems, recv_sems, local_sems, bounce):
    n = len(srcs)
    x, y, c = lax.axis_index("x"), lax.axis_index("y"), lax.axis_index("c")
    my_chip = 2 * x + y

    def copies():
        out = []
        for a in range(n):
            for k in range(1, N_CHIP):
                px, py = x ^ (k >> 1), y ^ (k & 1)
                out.append(pltpu.make_async_remote_copy(
                    src_ref=srcs[a].at[2 * px + py], dst_ref=outs[a].at[my_chip],
                    send_sem=send_sems.at[(k - 1) * n + a], recv_sem=recv_sems.at[(k - 1) * n + a],
                    device_id=(px, py, c), device_id_type=MESH))
        return out

    def own_in(a):
        return pltpu.make_async_copy(srcs[a].at[my_chip], bounce[a], local_sems.at[a])

    def own_out(a):
        return pltpu.make_async_copy(bounce[a], outs[a].at[my_chip], local_sems.at[a])

    def start():
        _handshake([(x ^ (k >> 1), y ^ (k & 1), c) for k in range(1, N_CHIP)])
        for a in range(n):
            own_in(a).start()
        for cp in copies():
            cp.start()

    def finish():
        for a in range(n):
            own_in(a).wait()
            own_out(a).start()
        for cp in copies():
            cp.wait()
        for a in range(n):
            own_out(a).wait()

    return start, finish


def _chip_exchange_scratch(parts):
    n = len(parts)
    return [pltpu.SemaphoreType.DMA((3 * n,)), pltpu.SemaphoreType.DMA((3 * n,)), pltpu.SemaphoreType.DMA((n,))] \
        + [pltpu.VMEM(p.shape[1:], p.dtype) for p in parts]


def _pair_plan(srcs, outs, send_sems, recv_sems):
    x, y, c = lax.axis_index("x"), lax.axis_index("y"), lax.axis_index("c")

    def copies():
        return [pltpu.make_async_remote_copy(
            src_ref=srcs[a].at[1 - c], dst_ref=outs[a], send_sem=send_sems.at[a], recv_sem=recv_sems.at[a],
            device_id=(x, y, 1 - c), device_id_type=MESH) for a in range(len(srcs))]

    def start():
        _handshake([(x, y, 1 - c)])
        for cp in copies():
            cp.start()

    def finish():
        for cp in copies():
            cp.wait()

    return start, finish


def pair_exchange(parts, name):
    n = len(parts)

    def body(*refs):
        start, finish = _pair_plan(refs[:n], refs[n:2 * n], *refs[2 * n:])
        start()
        finish()

    return pl.pallas_call(
        body, name=name, out_shape=[jax.ShapeDtypeStruct(p.shape[1:], p.dtype) for p in parts],
        in_specs=[ANY] * n, out_specs=[ANY] * n,
        scratch_shapes=[pltpu.SemaphoreType.DMA((n,)), pltpu.SemaphoreType.DMA((n,))],
        compiler_params=_params(collective_id=PAIR_COLLECTIVE_ID),
    )(*parts)


def pair_add(parts, got, name):
    n = len(parts)

    def body(core_ref, *refs):
        for a in range(n):
            refs[2 * n + a][...] = (refs[a][...].astype(F32) + refs[n + a][...].astype(F32)).astype(BF16)

    def own(p):
        zeros = (0,) * (p.ndim - 2)
        return pl.BlockSpec((None, 1, *p.shape[2:]), lambda i, core: (core[0], i, *zeros))

    def plain(p):
        zeros = (0,) * (p.ndim - 1)
        return pl.BlockSpec((1, *p.shape[1:]), lambda i, core: (i, *zeros))

    core = lax.axis_index("c").astype(jnp.int32).reshape(1)
    return pl.pallas_call(
        body, name=name,
        grid_spec=pltpu.PrefetchScalarGridSpec(
            num_scalar_prefetch=1, grid=(got[0].shape[0],),
            in_specs=[own(p) for p in parts] + [plain(p) for p in got], out_specs=[plain(p) for p in got]),
        out_shape=[jax.ShapeDtypeStruct(p.shape, BF16) for p in got],
        compiler_params=_params(),
    )(core, *parts, *got)


def _adamw(w, g, m, v):
    m2 = ADAM_B1 * m + (1.0 - ADAM_B1) * g
    v2 = ADAM_B2 * v + (1.0 - ADAM_B2) * (g * g)
    m_hat = m2 / (1.0 - ADAM_B1 ** ADAM_STEP)
    v_hat = v2 / (1.0 - ADAM_B2 ** ADAM_STEP)
    delta = -ADAM_LR * (m_hat / (jnp.sqrt(v_hat) + ADAM_EPS) + ADAM_WD * w)
    return delta, m2, v2


def sum_adamw(slots, ws, ms, vs, name):
    n = len(ws)
    halves = 2

    def body(*refs):
        for a in range(n):
            total = refs[a][0].astype(F32)
            for s in range(1, slots[a].shape[0]):
                total = total + refs[a][s].astype(F32)
            delta, m2, v2 = _adamw(refs[n + a][...], total, refs[2 * n + a][...], refs[3 * n + a][...])
            for q, val in enumerate((total, delta, m2, v2)):
                refs[4 * n + 4 * a + q][...] = val

    def rows(w):
        return pl.BlockSpec((w.shape[0] // halves, w.shape[1]), lambda i: (i, 0))

    def slot_rows(p):
        return pl.BlockSpec((p.shape[0], p.shape[1] // halves, p.shape[2]), lambda i: (0, i, 0))

    out = pl.pallas_call(
        body, name=name, grid=(halves,),
        out_shape=[jax.ShapeDtypeStruct(w.shape, F32) for w in ws for _ in range(4)],
        in_specs=[slot_rows(p) for p in slots] + [rows(w) for w in ws] * 3,
        out_specs=[rows(w) for w in ws for _ in range(4)],
        compiler_params=_params(),
    )(*slots, *ws, *ms, *vs)
    return [out[4 * a:4 * a + 4] for a in range(n)]


def adamw_update(ws, gs, ms, vs, name):
    n = len(ws)

    def body(*refs):
        for a in range(n):
            delta, m2, v2 = _adamw(refs[a][...], refs[n + a][...], refs[2 * n + a][...], refs[3 * n + a][...])
            refs[4 * n + 3 * a][...] = delta
            refs[4 * n + 3 * a + 1][...] = m2
            refs[4 * n + 3 * a + 2][...] = v2

    out = pl.pallas_call(
        body, name=name,
        out_shape=[jax.ShapeDtypeStruct(w.shape, F32) for w in ws for _ in range(3)],
        in_specs=[VMEM] * (4 * n), out_specs=[VMEM] * (3 * n),
        compiler_params=_params(),
    )(*ws, *gs, *ms, *vs)
    return [out[3 * a:3 * a + 3] for a in range(n)]


GAIN_ROWS = 8
ROW_POOL_SCALE = 4 * GAIN_ROWS
ROW_SINKS = ROW_POOL_SCALE + 4
ROW_LOSS = ROW_SINKS + 1
ROW_W_POOL = 40
SMALL_ROWS = ROW_W_POOL + 4 * POOL_GROUP_DIM


def grad_w_in_small_reduce(a, b, gains, dpool_scale, dsinks, loss_part, dw_pool, drel_bias):
    t, m = a.shape
    d = b.shape[1]
    r = m // N_DEV
    tt = TOKEN_TILE
    last = t // tt - 1

    def body(a_ref, b_ref, g0, g1, g2, g3, dsc_ref, dsink_ref, loss_ref, dwp_ref, drb_ref, out_ref, total_ref, total_rb_ref,
             acc, stage, gat, gat_rb, g_send, g_recv):
        k = pl.program_id(0)
        x, y, c = lax.axis_index("x"), lax.axis_index("y"), lax.axis_index("c")
        start, finish = _gather_plan([stage, drb_ref], [gat, gat_rb], g_send, g_recv)

        @pl.when(k == 0)
        def _():
            for q, g_ref in enumerate((g0, g1, g2, g3)):
                stage[GAIN_ROWS * q:GAIN_ROWS * (q + 1), :] = g_ref[...]
            stage[ROW_POOL_SCALE:ROW_SINKS, :] = dsc_ref[...]
            stage[ROW_SINKS:ROW_LOSS, :] = dsink_ref[...]
            stage[ROW_LOSS:ROW_LOSS + 1, :] = loss_ref[...]
            stage[ROW_LOSS + 1:ROW_W_POOL, :] = jnp.zeros((ROW_W_POOL - ROW_LOSS - 1, 128), F32)
            stage[ROW_W_POOL:, :] = dwp_ref[...].reshape(4 * POOL_GROUP_DIM, POOL_GROUP_DIM)
            gat[4 * x + 2 * y + c] = stage[...]
            gat_rb[4 * x + 2 * y + c] = drb_ref[...]
            start()
            acc[...] = jnp.zeros_like(acc)

        acc[...] += _dot_tn(a_ref[...], b_ref[...])

        @pl.when(k == last)
        def _():
            blocks = acc[...].reshape(N_CHIP, 2, r, d)
            for chip in range(N_CHIP):
                for core in range(2):
                    out_ref[core, chip] = blocks[chip, core].astype(BF16)
            finish()
            total, total_rb = gat[0], gat_rb[0]
            for s in range(1, N_DEV):
                total, total_rb = total + gat[s], total_rb + gat_rb[s]
            total_ref[...] = total
            total_rb_ref[...] = total_rb

    out_shape = (2, N_CHIP, r, d)
    return pl.pallas_call(
        body, name="grad_w_in", grid=(t // tt,),
        out_shape=[jax.ShapeDtypeStruct(out_shape, BF16), jax.ShapeDtypeStruct((SMALL_ROWS, 128), F32),
                   jax.ShapeDtypeStruct(drel_bias.shape, F32)],
        in_specs=[pl.BlockSpec((tt, m), lambda k: (k, 0)), pl.BlockSpec((tt, d), lambda k: (k, 0))] + [VMEM] * 9,
        out_specs=[pl.BlockSpec(out_shape, lambda k: (0,) * len(out_shape)), VMEM, VMEM],
        scratch_shapes=[pltpu.VMEM((m, d), F32), pltpu.VMEM((SMALL_ROWS, 128), F32),
                        pltpu.VMEM((N_DEV, SMALL_ROWS, 128), F32), pltpu.VMEM((N_DEV, *drel_bias.shape), F32),
                        pltpu.SemaphoreType.DMA((14,)), pltpu.SemaphoreType.DMA((14,))],
        compiler_params=_params(collective_id=GATHER_COLLECTIVE_ID),
    )(a, b, *gains, dpool_scale, dsinks, loss_part, dw_pool, drel_bias)


def reduce_w_in(d_in_t):
    def body(d_in_ref, g_in_ref, pair_got, chip_part, chip_got, p_send, p_recv, x_send, x_recv):
        x, y, c = lax.axis_index("x"), lax.axis_index("y"), lax.axis_index("c")
        my_chip = 2 * x + y
        _handshake([(x, y, 1 - c)] + [(x ^ (k >> 1), y ^ (k & 1), c) for k in range(1, N_CHIP)])
        pair = pltpu.make_async_remote_copy(
            src_ref=d_in_ref.at[1 - c], dst_ref=pair_got, send_sem=p_send, recv_sem=p_recv,
            device_id=(x, y, 1 - c), device_id_type=MESH)
        pair.start()
        pair.wait()
        chip_part[...] = (d_in_ref[c].astype(F32) + pair_got[...].astype(F32)).astype(BF16)
        copies = []
        for k in range(1, N_CHIP):
            px, py = x ^ (k >> 1), y ^ (k & 1)
            copies.append(pltpu.make_async_remote_copy(
                src_ref=chip_part.at[2 * px + py], dst_ref=chip_got.at[my_chip],
                send_sem=x_send.at[k - 1], recv_sem=x_recv.at[k - 1], device_id=(px, py, c), device_id_type=MESH))
        for cp in copies:
            cp.start()
        chip_got[my_chip] = chip_part[my_chip]
        for cp in copies:
            cp.wait()
        g_in = chip_got[0].astype(F32)
        for s in range(1, N_CHIP):
            g_in = g_in + chip_got[s].astype(F32)
        g_in_ref[...] = g_in

    per_core = d_in_t.shape[1:]
    return pl.pallas_call(
        body, name="reduce_w_in",
        out_shape=jax.ShapeDtypeStruct(d_in_t.shape[2:], F32),
        in_specs=[VMEM], out_specs=VMEM,
        scratch_shapes=[pltpu.VMEM(per_core, d_in_t.dtype), pltpu.VMEM(per_core, d_in_t.dtype),
                        pltpu.VMEM(per_core, d_in_t.dtype),
                        pltpu.SemaphoreType.DMA, pltpu.SemaphoreType.DMA,
                        pltpu.SemaphoreType.DMA((3,)), pltpu.SemaphoreType.DMA((3,))],
        compiler_params=_params(collective_id=GATHER_COLLECTIVE_ID),
    )(d_in_t)


def small_adamw(total, total_rb, small_w, small_m, small_v):
    n_small = len(small_w)

    def body(*refs):
        total_ref, rb_ref = refs[:2]
        w_refs, m_refs, v_refs = (refs[2 + k * n_small:2 + (k + 1) * n_small] for k in range(3))
        loss_out = refs[2 + 3 * n_small]
        result = refs[3 + 3 * n_small:]
        total = total_ref[...]
        loss_out[...] = total[ROW_LOSS:ROW_LOSS + 1, :]
        grads = [_as_lanes(total[GAIN_ROWS * k:GAIN_ROWS * (k + 1), :]) for k in range(4)]
        grads.append(_as_lanes(total[ROW_POOL_SCALE:ROW_SINKS, :]))
        grads.append(total[ROW_SINKS:ROW_LOSS, 0:N_Q_HEADS])
        grads.append(total[ROW_W_POOL:, :].reshape(w_refs[6].shape))
        grads.append(rb_ref[...])
        for k in range(n_small):
            delta, m2, v2 = _adamw(w_refs[k][...], grads[k], m_refs[k][...], v_refs[k][...])
            result[4 * k][...] = grads[k]
            result[4 * k + 1][...] = delta
            result[4 * k + 2][...] = m2
            result[4 * k + 3][...] = v2

    out = pl.pallas_call(
        body, name="small_adamw",
        out_shape=[jax.ShapeDtypeStruct((1, 128), F32)] + [jax.ShapeDtypeStruct(w.shape, F32) for w in small_w for _ in range(4)],
        in_specs=[VMEM] * (2 + 3 * n_small), out_specs=[VMEM] * (1 + 4 * n_small),
        compiler_params=_params(),
    )(total, total_rb, *small_w, *small_m, *small_v)
    return out[0], [out[1 + 4 * k:5 + 4 * k] for k in range(n_small)]


def norm_inproj(x, g, w_shard, shard, shard_rows):
    t, d = x.shape
    r = w_shard.shape[0]
    tm = TOKEN_TILE
    nt = t // tm

    def body(x_ref, g_ref, w_shard_ref, shard_ref, proj_ref, h_ref, w_ref, gathered_ref, h_all, w_all, w_sem,
             send_w, recv_w, local_w, bounce_w, send_sems, recv_sems, local_sems, bounce):
        i = pl.program_id(0)
        start_w, finish_w = _gather_plan([w_shard_ref], [w_ref], send_w, recv_w, local_w, [bounce_w])
        start, finish = _gather_plan([shard_ref], [gathered_ref], send_sems, recv_sems, local_sems, [bounce],
                                     [(0, shard.shape[0])], shake=False)

        @pl.when(i == 0)
        def _():
            start_w()
            start()

        @pl.when(i < nt)
        def _():
            xv = x_ref[...]
            h = ((xv * _rstd(xv)) * g_ref[...]).astype(BF16)
            h_ref[...] = h
            h_all[pl.ds(pl.multiple_of(i * tm, tm), tm), :] = h

        @pl.when(i == nt - 1)
        def _():
            finish_w()
            landed = pltpu.make_async_copy(w_ref, w_all, w_sem)
            landed.start()
            landed.wait()

        @pl.when(i >= nt)
        def _():
            rows = pl.ds(pl.multiple_of((i - nt) * tm, tm), tm)
            proj_ref[...] = _dot_nt(h_all[rows, :], _merge_rows(w_all[...]))

        pl.when(i == 2 * nt - 2)(finish.forward)
        pl.when(i == 2 * nt - 1)(finish.complete)

    first = lambda i: (jnp.minimum(i, nt - 1), 0)
    return pl.pallas_call(
        body, name="norm_inproj", grid=(2 * nt,),
        out_shape=[jax.ShapeDtypeStruct((t, N_DEV * r), F32), jax.ShapeDtypeStruct((t, d), BF16),
                   jax.ShapeDtypeStruct((N_DEV, r, d), w_shard.dtype),
                   jax.ShapeDtypeStruct((N_DEV, shard_rows, d), shard.dtype)],
        in_specs=[pl.BlockSpec((tm, d), first), pl.BlockSpec((1, d), lambda i: (0, 0)), ANY, ANY],
        out_specs=[pl.BlockSpec((tm, N_DEV * r), lambda i: (jnp.maximum(i - nt, 0), 0)), pl.BlockSpec((tm, d), first),
                   ANY, ANY],
        scratch_shapes=[pltpu.VMEM((t, d), BF16), pltpu.VMEM((N_DEV, r, d), w_shard.dtype), pltpu.SemaphoreType.DMA]
        + _gather_scratch([w_shard]) + _gather_scratch([shard]),
        compiler_params=_params(collective_id=GATHER_COLLECTIVE_ID),
    )(x, g, w_shard, shard)


def bias_band(bucket, in_window, rel_bias):
    def body(bk_ref, win_ref, rb_ref, out_ref):
        bk = bk_ref[...]
        keep = win_ref[...] > 0.5
        for h in range(N_Q_HEADS):
            acc = jnp.zeros(bk.shape, F32)
            for b in range(N_BUCKETS):
                acc = jnp.where(bk == float(b), rb_ref[b, h], acc)
            out_ref[h] = jnp.where(keep, acc, NEG_INF)

    return pl.pallas_call(
        body, name="bias_band",
        out_shape=jax.ShapeDtypeStruct((N_Q_HEADS, BLOCK, 2 * BLOCK), F32),
        in_specs=[VMEM, VMEM, SMEM], out_specs=VMEM,
    )(bucket, in_window, rel_bias)


def bias_band_bwd(bucket, dbias):
    def body(bk_ref, db_ref, out_ref):
        bk = bk_ref[...]
        for h in range(N_Q_HEADS):
            db = db_ref[h]
            for b in range(N_BUCKETS):
                out_ref[h, b] = jnp.sum(jnp.where(bk == float(b), db, 0.0))

    return pl.pallas_call(
        body, name="bias_band_bwd",
        out_shape=jax.ShapeDtypeStruct((N_Q_HEADS, N_BUCKETS), F32),
        in_specs=[VMEM, VMEM], out_specs=SMEM,
    )(bucket, dbias)


def _window_sum(buf_ref, g, w, first):
    cols = slice(g * POOL_GROUP_DIM, (g + 1) * POOL_GROUP_DIM)
    acc = None
    for k in range(w):
        piece = buf_ref[first(k):first(k) + BLOCK, cols]
        acc = piece if acc is None else acc + piece
    return acc


def _inv_count(i, w):
    row = lax.broadcasted_iota(jnp.int32, (BLOCK, 1), 0)
    return 1.0 / jnp.minimum(i * BLOCK + row + 1, w).astype(F32)


def _fill_pool_input(i, ubuf, uc_ref, halo_ref):
    ubuf[0:HALO, :] = jnp.where(i > 0, halo_ref[...], 0.0)
    ubuf[HALO:, :] = uc_ref[...]


def _pooled(i, g, w, ubuf):
    cols = slice(g * POOL_GROUP_DIM, (g + 1) * POOL_GROUP_DIM)
    return _window_sum(ubuf, g, w, lambda k: HALO - k) * _inv_count(i, w) - ubuf[HALO:, cols]


def _head_variants(pair):
    low = lax.broadcasted_iota(jnp.int32, pair.shape, 1) < HEAD_DIM
    swapped = pltpu.roll(pair, HEAD_DIM, 1)
    zero = jnp.zeros_like(pair)
    pick = lambda c, a, b: jnp.where(c, a, b).astype(BF16)
    return [[pick(low, pair, zero), pick(low, zero, swapped)], [pick(low, swapped, zero), pick(low, zero, pair)]]


def _head_probs(i, hq, rows, s_ref, biasm_ref, sinks_ref):
    s = s_ref[hq, rows, :] * ATTN_SCALE + biasm_ref[hq, rows, :]
    col = lax.broadcasted_iota(jnp.int32, s.shape, 1)
    s = jnp.where((i == 0) & (col < BLOCK), NEG_INF, s)
    sink = sinks_ref[0, hq]
    m = jnp.maximum(jnp.max(s, axis=-1, keepdims=True), sink)
    p = jnp.exp(s - m)
    e_sink = jnp.exp(sink - m)
    inv = 1.0 / (jnp.sum(p, axis=-1, keepdims=True) + e_sink)
    return p * inv, e_sink * inv


def _head_slot(hq):
    return 4 * (hq // GQA_GROUP) + 2 * (hq % 2) + (hq % GQA_GROUP) // 2


def _mixer_in_specs(cur, prv):
    return [pl.BlockSpec((BLOCK, 512), lambda i: (cur(i), 0)),
            pl.BlockSpec((HALO, 512), lambda i: (jnp.maximum(cur(i) * (BLOCK // HALO) - 1, 0), 0)),
            pl.BlockSpec((BLOCK, 512), lambda i: (cur(i), 1)),
            pl.BlockSpec((BLOCK, 256), lambda i: (cur(i), 4)),
            pl.BlockSpec((BLOCK, 256), lambda i: (prv(i), 4))]


def _mixer_param_specs():
    return [pl.BlockSpec((N_Q_HEADS, BLOCK, 2 * BLOCK), lambda i: (0, 0, 0)), SMEM,
            pl.BlockSpec((4, POOL_GROUP_DIM, POOL_GROUP_DIM), lambda i: (0, 0, 0)),
            pl.BlockSpec((1, POOL_WIDTH), lambda i: (0, 0))]


def mixers_fwd(proj, biasm, sinks, w_pool, pool_scale, shards):
    t = proj.shape[0]
    nb = t // BLOCK
    ns = len(shards)

    def body(*refs):
        uc_ref, halo_ref, q_ref, kvc_ref, kvp_ref, biasm_ref, sinks_ref, wp_ref, sc_ref = refs[:9]
        shard_refs = refs[9:9 + ns]
        out_ref, pooled_ref, p_all, psink_ref = refs[9 + ns:13 + ns]
        gathered_refs = refs[13 + ns:13 + 2 * ns]
        ubuf, s_all, send_sems, recv_sems, local_sems = refs[13 + 2 * ns:18 + 2 * ns]
        i = pl.program_id(0)
        start, finish = _gather_plan(shard_refs, gathered_refs, send_sems, recv_sems, local_sems, refs[18 + 2 * ns:])
        pl.when(i == 0)(start)

        _fill_pool_input(i, ubuf, uc_ref, halo_ref)
        for g, w in enumerate(POOL_WINDOWS):
            cols = slice(g * POOL_GROUP_DIM, (g + 1) * POOL_GROUP_DIM)
            pooled = _pooled(i, g, w, ubuf).astype(BF16)
            pooled_ref[:, cols] = pooled
            out_ref[:, cols] = (_dot(pooled, wp_ref[g]) * sc_ref[:, cols]).astype(BF16)
        kv = jnp.concatenate([kvp_ref[...], kvc_ref[...]], axis=0)
        k_var = _head_variants(kv[:, 0:2 * HEAD_DIM])
        v_var = _head_variants(kv[:, 2 * HEAD_DIM:])
        for hq in range(N_Q_HEADS):
            j, half, h = hq // 2, hq % 2, hq // GQA_GROUP
            q2 = q_ref[:, 2 * HEAD_DIM * j:2 * HEAD_DIM * (j + 1)].astype(BF16)
            s_all[hq] = _dot_nt(q2, k_var[h][half])
        psink_ref[...] = jnp.zeros_like(psink_ref)
        for hq in range(N_Q_HEADS):
            for r in range(0, BLOCK, ROW_CHUNK):
                rows = slice(r, r + ROW_CHUNK)
                probs, p_sink = _head_probs(i, hq, rows, s_all, biasm_ref, sinks_ref)
                p_all[_head_slot(hq), rows, :] = probs.astype(BF16)
                psink_ref[rows, hq:hq + 1] = p_sink
        for j in range(N_Q_HEADS // 2):
            h = 2 * j // GQA_GROUP
            acc = _dot(p_all[_head_slot(2 * j)], v_var[h][0]) + _dot(p_all[_head_slot(2 * j + 1)], v_var[h][1])
            out_ref[:, POOL_WIDTH + 2 * HEAD_DIM * j:POOL_WIDTH + 2 * HEAD_DIM * (j + 1)] = acc.astype(BF16)

        pl.when(i == max(nb - 4, 0))(finish.forward)
        pl.when(i == nb - 1)(finish.complete)

    return pl.pallas_call(
        body, name="mixers_fwd", grid=(nb,),
        out_shape=[jax.ShapeDtypeStruct((t, 2 * POOL_WIDTH), BF16), jax.ShapeDtypeStruct((t, POOL_WIDTH), BF16),
                   jax.ShapeDtypeStruct((N_Q_HEADS, t, 2 * BLOCK), BF16), jax.ShapeDtypeStruct((t, 128), F32)]
        + [jax.ShapeDtypeStruct((N_DEV, *sh.shape), sh.dtype) for sh in shards],
        in_specs=_mixer_in_specs(lambda i: i, lambda i: jnp.maximum(i - 1, 0)) + _mixer_param_specs() + [ANY] * ns,
        out_specs=[pl.BlockSpec((BLOCK, 2 * POOL_WIDTH), lambda i: (i, 0)), pl.BlockSpec((BLOCK, POOL_WIDTH), lambda i: (i, 0)),
                   pl.BlockSpec((N_Q_HEADS, BLOCK, 2 * BLOCK), lambda i: (0, i, 0)), pl.BlockSpec((BLOCK, 128), lambda i: (i, 0))]
        + [ANY] * ns,
        scratch_shapes=[pltpu.VMEM((HALO + BLOCK, POOL_WIDTH), F32), pltpu.VMEM((N_Q_HEADS, BLOCK, 2 * BLOCK), F32)]
        + _gather_scratch(shards),
        compiler_params=_params(collective_id=GATHER_COLLECTIVE_ID),
    )(proj, proj, proj, proj, proj, biasm, sinks, w_pool, pool_scale, *shards)


def outproj_norm(cat, w, x, g, g_next, shard, partial):
    t, d = x.shape
    tm = TOKEN_TILE
    last = t // tm - 1
    rows = [(partial.shape[1] - shard.shape[0], shard.shape[0])]

    def body(c_ref, w_ref, x_ref, g_ref, gn_ref, shard_ref, partial_ref, mix_ref, x1_ref, h2_ref, gathered_ref,
             send_sems, recv_sems, local_sems, bounce):
        i = pl.program_id(0)
        start, finish = _gather_plan([shard_ref], [gathered_ref], send_sems, recv_sems, local_sems, [bounce], rows)
        pl.when(i == 0)(start)
        mix = _dot(c_ref[...], w_ref[...])
        mix_ref[...] = mix
        x1 = x_ref[...] + (mix * _rstd(mix)) * g_ref[...]
        x1_ref[...] = x1
        h2_ref[...] = ((x1 * _rstd(x1)) * gn_ref[...]).astype(BF16)
        pl.when(i == max(last - 1, 0))(finish.forward)
        pl.when(i == last)(finish.complete)

    row = pl.BlockSpec((tm, d), lambda i: (i, 0))
    gain = pl.BlockSpec((1, d), lambda i: (0, 0))
    return pl.pallas_call(
        body, name="outproj_norm", grid=(t // tm,),
        out_shape=[jax.ShapeDtypeStruct((t, d), F32), jax.ShapeDtypeStruct((t, d), F32), jax.ShapeDtypeStruct((t, d), BF16),
                   jax.ShapeDtypeStruct(partial.shape, partial.dtype)],
        in_specs=[pl.BlockSpec((tm, cat.shape[1]), lambda i: (i, 0)), pl.BlockSpec(w.shape, lambda i: (0, 0)), row, gain, gain,
                  ANY, ANY],
        out_specs=[row, row, row, ANY],
        input_output_aliases={6: 3},
        scratch_shapes=_gather_scratch([shard]),
        compiler_params=_params(collective_id=GATHER_COLLECTIVE_ID),
    )(cat, w, x, g, g_next, shard, partial)


def ffn_up(h, gate_t, up_t, down_shard):
    t, d = h.shape
    n = gate_t.shape[1]
    f = N_DEV * n
    tm, ts = FFN_TOKEN_TILE, FF_SHARDS_PER_TILE
    tn = ts * n
    steps = (f // tn, t // tm)

    def body(h_ref, wg_ref, wu_ref, shard_ref, gate_ref, up_ref, a_ref, gathered_ref,
             send_sems, recv_sems, local_sems, bounce):
        j, i = pl.program_id(0), pl.program_id(1)
        start, finish = _gather_plan([shard_ref], [gathered_ref], send_sems, recv_sems, local_sems, [bounce])
        pl.when((i == 0) & (j == 0))(start)

        hv = h_ref[...]
        gate = _dot_nt(hv, _merge_rows(wg_ref[...]))
        up = _dot_nt(hv, _merge_rows(wu_ref[...]))
        gate_ref[...] = gate.astype(BF16)
        up_ref[...] = up.astype(BF16)
        a_ref[...] = (gate * (1.0 / (1.0 + jnp.exp(-gate))) * up).astype(BF16)

        pl.when((j == steps[0] - 1) & (i == max(steps[1] - 2, 0)))(finish.forward)
        pl.when((j == steps[0] - 1) & (i == steps[1] - 1))(finish.complete)

    wide = pl.BlockSpec((tm, tn), lambda j, i: (i, j))
    return pl.pallas_call(
        body, name="ffn_up", grid=steps,
        out_shape=[jax.ShapeDtypeStruct((t, f), BF16)] * 3
        + [jax.ShapeDtypeStruct((N_DEV, *down_shard.shape), down_shard.dtype)],
        in_specs=[pl.BlockSpec((tm, d), lambda j, i: (i, 0)),
                  pl.BlockSpec((ts, n, d), lambda j, i: (j, 0, 0)),
                  pl.BlockSpec((ts, n, d), lambda j, i: (j, 0, 0)), ANY],
        out_specs=[wide, wide, wide, ANY],
        scratch_shapes=_gather_scratch([down_shard]),
        compiler_params=_params(collective_id=GATHER_COLLECTIVE_ID),
    )(h, gate_t, up_t, down_shard)


def ffn_down_loss(a, w_down, x1, g, target):
    t, d = x1.shape
    tm = WIDE_K_TOKEN_TILE

    def body(a_ref, w_ref, x_ref, g_ref, t_ref, df_ref, dy_ref, dg_ref, loss_ref):
        @pl.when(pl.program_id(0) == 0)
        def _():
            dg_ref[...] = jnp.zeros_like(dg_ref)
            loss_ref[...] = jnp.zeros_like(loss_ref)

        f = _dot(a_ref[...], _merge_rows(w_ref[...]))
        r = _rstd(f)
        g = g_ref[...]
        err = x_ref[...] + (f * r) * g - t_ref[...]
        loss_ref[...] += 0.5 * jnp.sum(jnp.mean(err * err, axis=-1, keepdims=True))
        dy = err * (1.0 / d)
        dy_ref[...] = dy
        df, dg_rows = _norm_bwd(dy, f, r, g)
        df_ref[...] = df.astype(BF16)
        dg_ref[...] += _as_rows(jnp.sum(dg_rows, axis=0, keepdims=True))

    row = pl.BlockSpec((tm, d), lambda i: (i, 0))
    gain = pl.BlockSpec((1, d), lambda i: (0, 0))
    return pl.pallas_call(
        body, name="ffn_down_loss", grid=(t // tm,),
        out_shape=[jax.ShapeDtypeStruct((t, d), BF16), jax.ShapeDtypeStruct((t, d), F32),
                   jax.ShapeDtypeStruct((d // 128, 128), F32), jax.ShapeDtypeStruct((1, 128), F32)],
        in_specs=[pl.BlockSpec((tm, a.shape[1]), lambda i: (i, 0)), pl.BlockSpec(w_down.shape, lambda i: (0, 0, 0)), row, gain, row],
        out_specs=[row, row, pl.BlockSpec((d // 128, 128), lambda i: (0, 0)), pl.BlockSpec((1, 128), lambda i: (0, 0))],
        compiler_params=_params(),
    )(a, w_down, x1, g, target)


def ffn_down_bwd(df, w_down, gate, up):
    t, d = df.shape
    n = w_down.shape[1]
    f = gate.shape[1]
    tm, ts = FFN_TOKEN_TILE, FF_SHARDS_PER_TILE
    tn = ts * n

    def body(df_ref, w_ref, gate_ref, up_ref, dgate_ref, dup_ref):
        da = _dot_nt(df_ref[...], _merge_rows(w_ref[...]))
        gate = gate_ref[...].astype(F32)
        sig = 1.0 / (1.0 + jnp.exp(-gate))
        dgate_ref[...] = (da * up_ref[...].astype(F32) * (sig * (1.0 + gate * (1.0 - sig)))).astype(BF16)
        dup_ref[...] = (da * (gate * sig)).astype(BF16)

    wide = pl.BlockSpec((tm, tn), lambda j, i: (i, j))
    return pl.pallas_call(
        body, name="ffn_down_bwd", grid=(f // tn, t // tm),
        out_shape=[jax.ShapeDtypeStruct((t, f), BF16)] * 2,
        in_specs=[pl.BlockSpec((tm, d), lambda j, i: (i, 0)), pl.BlockSpec((ts, n, d), lambda j, i: (j, 0, 0)), wide, wide],
        out_specs=[wide, wide],
        compiler_params=_params(),
    )(df, w_down, gate, up)


def grad_rows(a, b, name, by_core=False):
    t, m = a.shape
    d = b.shape[1]
    r = m // N_DEV
    tt = TOKEN_TILE
    last = t // tt - 1
    out_shape = (2, N_CHIP, r, d) if by_core else (N_DEV, r, d)

    def body(a_ref, b_ref, out_ref, acc):
        k = pl.program_id(0)

        @pl.when(k == 0)
        def _():
            acc[...] = jnp.zeros_like(acc)

        acc[...] += _dot_tn(a_ref[...], b_ref[...])

        @pl.when(k == last)
        def _():
            if by_core:
                blocks = acc[...].reshape(N_CHIP, 2, r, d)
                for chip in range(N_CHIP):
                    for core in range(2):
                        out_ref[core, chip] = blocks[chip, core].astype(BF16)
            else:
                out_ref[...] = acc[...].reshape(out_shape).astype(BF16)

    return pl.pallas_call(
        body, name=name, grid=(t // tt,),
        out_shape=jax.ShapeDtypeStruct(out_shape, BF16),
        in_specs=[pl.BlockSpec((tt, m), lambda k: (k, 0)), pl.BlockSpec((tt, d), lambda k: (k, 0))],
        out_specs=pl.BlockSpec(out_shape, lambda k: (0,) * len(out_shape)),
        scratch_shapes=[pltpu.VMEM((m, d), F32)],
        compiler_params=_params(),
    )(a, b)


def grad_ffn(lhs, b, name, pair_parts=()):
    t, f = lhs[0].shape
    d = b.shape[1]
    nw = len(lhs)
    na = len(pair_parts)
    n = f // N_DEV
    tt, ts = TOKEN_TILE, FF_SHARDS_PER_TILE
    tn = ts * n
    steps = (f // tn, t // tt)

    def body(*refs):
        a_refs, b_ref, part_refs = refs[:nw], refs[nw], refs[nw + 1:nw + 1 + na]
        out_refs = refs[nw + 1 + na:2 * nw + 1 + na]
        got_refs = refs[2 * nw + 1 + na:2 * nw + 1 + 2 * na]
        acc = refs[2 * nw + 1 + 2 * na]
        i, k = pl.program_id(0), pl.program_id(1)
        if na:
            start, finish = _pair_plan(part_refs, got_refs, *refs[2 * nw + 2 + 2 * na:])
            pl.when((i == 0) & (k == 0))(start)

        @pl.when(k == 0)
        def _():
            acc[...] = jnp.zeros_like(acc)

        for w in range(nw):
            acc[w] += _dot_tn(a_refs[w][...], b_ref[...])

        @pl.when(k == steps[1] - 1)
        def _():
            for w in range(nw):
                blocks = acc[w].reshape(ts // 2, 2, n, d)
                for chip in range(ts // 2):
                    for core in range(2):
                        out_refs[w][core, chip] = blocks[chip, core].astype(BF16)

        if na:
            pl.when((i == steps[0] - 1) & (k == steps[1] - 1))(finish)

    out = pl.pallas_call(
        body, name=name, grid=steps,
        out_shape=[jax.ShapeDtypeStruct((2, N_CHIP, n, d), BF16)] * nw
        + [jax.ShapeDtypeStruct(p.shape[1:], p.dtype) for p in pair_parts],
        in_specs=[pl.BlockSpec((tt, tn), lambda i, k: (k, i))] * nw + [pl.BlockSpec((tt, d), lambda i, k: (k, 0))] + [ANY] * na,
        out_specs=[pl.BlockSpec((2, ts // 2, n, d), lambda i, k: (0, i, 0, 0))] * nw + [ANY] * na,
        scratch_shapes=[pltpu.VMEM((nw, tn, d), F32)]
        + ([pltpu.SemaphoreType.DMA((na,)), pltpu.SemaphoreType.DMA((na,))] if na else []),
        compiler_params=_params(collective_id=PAIR_COLLECTIVE_ID) if na else _params(),
    )(*lhs, b, *pair_parts)
    return out[:nw], out[nw:]


def ffn_up_bwd(dgate, dup, gate_t, up_t, x1, g_ffn, dy, mix, g_mix, chip_parts):
    t, d = x1.shape
    n = gate_t.shape[1]
    f = N_DEV * n
    tm = WIDE_K_TOKEN_TILE
    na = len(chip_parts)
    last = t // tm - 1

    def body(*refs):
        dg_ref, du_ref, wg_ref, wu_ref, x_ref, gf_ref, dy_ref, mix_ref, gm_ref = refs[:9]
        part_refs = refs[9:9 + na]
        dx1_ref, dmix_ref, dgf_ref, dgm_ref = refs[9 + na:13 + na]
        slot_refs = refs[13 + na:13 + 2 * na]
        send_sems, recv_sems, local_sems = refs[13 + 2 * na:16 + 2 * na]
        i = pl.program_id(0)
        start, finish = _chip_exchange_plan(part_refs, slot_refs, send_sems, recv_sems, local_sems, refs[16 + 2 * na:])

        @pl.when(i == 0)
        def _():
            start()
            dgf_ref[...] = jnp.zeros_like(dgf_ref)
            dgm_ref[...] = jnp.zeros_like(dgm_ref)

        dh = _dot(dg_ref[...], _merge_rows(wg_ref[...])) + _dot(du_ref[...], _merge_rows(wu_ref[...]))
        x1 = x_ref[...]
        dx, dgf_rows = _norm_bwd(dh, x1, _rstd(x1), gf_ref[...])
        dx1 = dy_ref[...] + dx
        dx1_ref[...] = dx1
        dgf_ref[...] += _as_rows(jnp.sum(dgf_rows, axis=0, keepdims=True))
        mix = mix_ref[...]
        dmix, dgm_rows = _norm_bwd(dx1, mix, _rstd(mix), gm_ref[...])
        dmix_ref[...] = dmix.astype(BF16)
        dgm_ref[...] += _as_rows(jnp.sum(dgm_rows, axis=0, keepdims=True))
        pl.when(i == last)(finish)

    row = pl.BlockSpec((tm, d), lambda i: (i, 0))
    wide = pl.BlockSpec((tm, f), lambda i: (i, 0))
    gain = pl.BlockSpec((1, d), lambda i: (0, 0))
    gain_rows = pl.BlockSpec((d // 128, 128), lambda i: (0, 0))
    whole = pl.BlockSpec((N_DEV, n, d), lambda i: (0, 0, 0), pipeline_mode=pl.Buffered(1))
    out = pl.pallas_call(
        body, name="ffn_up_bwd", grid=(t // tm,),
        out_shape=[jax.ShapeDtypeStruct((t, d), F32), jax.ShapeDtypeStruct((t, d), BF16),
                   jax.ShapeDtypeStruct((d // 128, 128), F32), jax.ShapeDtypeStruct((d // 128, 128), F32)]
        + [jax.ShapeDtypeStruct(p.shape, p.dtype) for p in chip_parts],
        in_specs=[wide, wide, whole, whole, row, gain, row, row, gain] + [ANY] * na,
        out_specs=[row, row, gain_rows, gain_rows] + [ANY] * na,
        scratch_shapes=_chip_exchange_scratch(chip_parts),
        compiler_params=_params(collective_id=CHIP_COLLECTIVE_ID),
    )(dgate, dup, gate_t, up_t, x1, g_ffn, dy, mix, g_mix, *chip_parts)
    return out[:4], out[4:]


def outproj_bwd(dmix, w_out):
    t, d = dmix.shape
    tm = TOKEN_TILE

    def body(dm_ref, w_ref, out_ref):
        out_ref[...] = _dot_nt(dm_ref[...], w_ref[...])

    return pl.pallas_call(
        body, name="outproj_bwd", grid=(t // tm,),
        out_shape=jax.ShapeDtypeStruct((t, w_out.shape[0]), F32),
        in_specs=[pl.BlockSpec((tm, d), lambda i: (i, 0)), pl.BlockSpec(w_out.shape, lambda i: (0, 0))],
        out_specs=pl.BlockSpec((tm, w_out.shape[0]), lambda i: (i, 0)),
        compiler_params=_params(),
    )(dmix, w_out)


def mixers_bwd(proj, dcat, pooled, probs, p_sinks, w_pool, pool_scale, ffn_parts):
    t = proj.shape[0]
    nb = t // BLOCK
    na = len(ffn_parts)

    def body(*refs):
        (q_ref, kvc_ref, kvp_ref, dcat_ref, pooled_ref, p_all, psink_ref, wp_ref, sc_ref) = refs[:9]
        part_refs = refs[9:9 + na]
        dproj_ref, dbias_ref, dsink_ref, dwp_ref, dsc_ref = refs[9 + na:14 + na]
        slot_refs = refs[14 + na:14 + 2 * na]
        dbuf, c_u, c_q, c_kv, dp_all, ds_all, sink_acc = refs[14 + 2 * na:21 + 2 * na]
        send_sems, recv_sems, local_sems = refs[21 + 2 * na:24 + 2 * na]
        bounce = refs[24 + 2 * na:]
        i = pl.program_id(0)
        lane = lax.broadcasted_iota(jnp.int32, (1, 128), 1)
        start, finish = _chip_exchange_plan(part_refs, slot_refs, send_sems, recv_sems, local_sems, bounce)

        @pl.when(i == 0)
        def _():
            start()
            dbias_ref[...] = jnp.zeros_like(dbias_ref)
            dwp_ref[...] = jnp.zeros_like(dwp_ref)
            dsc_ref[...] = jnp.zeros_like(dsc_ref)
            dsink_ref[...] = jnp.zeros_like(dsink_ref)
            dbuf[...] = jnp.zeros_like(dbuf)
            c_u[...] = jnp.zeros_like(c_u)
            c_q[...] = jnp.zeros_like(c_q)
            c_kv[...] = jnp.zeros_like(c_kv)

        @pl.when(i < nb)
        def _():
            for g, w in enumerate(POOL_WINDOWS):
                cols = slice(g * POOL_GROUP_DIM, (g + 1) * POOL_GROUP_DIM)
                pooled = pooled_ref[:, cols]
                mixed = _dot(pooled, wp_ref[g])
                dout = dcat_ref[:, cols]
                dsc_ref[g:g + 1, :] += jnp.sum(dout * mixed, axis=0, keepdims=True)
                dmixed = (dout * sc_ref[:, cols]).astype(BF16)
                dwp_ref[g] += _dot_tn(pooled, dmixed)
                dpooled = _dot_nt(dmixed, wp_ref[g])
                scaled = dpooled * _inv_count(i, w)
                dbuf[BLOCK:, cols] = scaled[0:HALO]
                dproj_ref[:, cols] = (_window_sum(dbuf, g, w, lambda k: k) + c_u[:, cols]).astype(BF16)
                dbuf[0:BLOCK, cols] = scaled
                c_u[:, cols] = -dpooled

            kv = jnp.concatenate([kvp_ref[...], kvc_ref[...]], axis=0)
            k_var = _head_variants(kv[:, 0:2 * HEAD_DIM])
            v_var = _head_variants(kv[:, 2 * HEAD_DIM:])
            q2s = [q_ref[:, 2 * HEAD_DIM * j:2 * HEAD_DIM * (j + 1)].astype(BF16) for j in range(N_Q_HEADS // 2)]
            do2s = [dcat_ref[:, POOL_WIDTH + 2 * HEAD_DIM * j:POOL_WIDTH + 2 * HEAD_DIM * (j + 1)].astype(BF16)
                    for j in range(N_Q_HEADS // 2)]
            slot = _head_slot
            for hq in range(N_Q_HEADS):
                j, half, h = hq // 2, hq % 2, hq // GQA_GROUP
                dp_all[hq] = _dot_nt(do2s[j], v_var[h][half])
            sink_acc[...] = jnp.zeros_like(sink_acc)
            for hq in range(N_Q_HEADS):
                for r in range(0, BLOCK, ROW_CHUNK):
                    rows = slice(r, r + ROW_CHUNK)
                    probs = p_all[slot(hq), rows, :].astype(F32)
                    dp = dp_all[hq, rows, :]
                    delta = jnp.sum(probs * dp, axis=-1, keepdims=True)
                    ds = probs * (dp - delta)
                    dbias_ref[hq, rows, :] += ds
                    sink_acc[rows, :] += jnp.where(lane == hq, psink_ref[rows, :], 0.0) * delta
                    ds_all[slot(hq), rows, :] = (ds * ATTN_SCALE).astype(BF16)
            dsink_ref[...] -= jnp.sum(sink_acc[...], axis=0, keepdims=True)
            dq2 = [None] * (N_Q_HEADS // 2)
            for hq in range(N_Q_HEADS):
                j, half, h = hq // 2, hq % 2, hq // GQA_GROUP
                dq = _dot(ds_all[slot(hq)], k_var[h][half])
                dq2[j] = dq if dq2[j] is None else dq2[j] + dq
            low = lax.broadcasted_iota(jnp.int32, (2 * BLOCK, 2 * HEAD_DIM), 1) < HEAD_DIM
            dk_half, dv_half = [[None, None], [None, None]], [[None, None], [None, None]]
            for h in range(N_KV_HEADS):
                for half in range(2):
                    heads = [hq for hq in range(GQA_GROUP * h, GQA_GROUP * (h + 1)) if hq % 2 == half]
                    base = slot(heads[0])
                    q_rows = jnp.concatenate([q2s[hq // 2] for hq in heads], axis=0)
                    do_rows = jnp.concatenate([do2s[hq // 2] for hq in heads], axis=0)
                    dk_half[h][half] = _dot_tn(_merge_rows(ds_all[base:base + 2]), q_rows)
                    dv_half[h][half] = _dot_tn(_merge_rows(p_all[base:base + 2]), do_rows)

            def pair_of(halves):
                return jnp.where(low, halves[0][0] + pltpu.roll(halves[0][1], HEAD_DIM, 1),
                                 halves[1][1] + pltpu.roll(halves[1][0], HEAD_DIM, 1))

            dkv = jnp.concatenate([pair_of(dk_half), pair_of(dv_half)], axis=1)
            dproj_ref[:, POOL_WIDTH:2 * POOL_WIDTH] = c_q[...].astype(BF16)
            dproj_ref[:, 2 * POOL_WIDTH:] = (c_kv[...] + dkv[0:BLOCK]).astype(BF16)
            c_q[...] = jnp.concatenate(dq2, axis=1)
            c_kv[...] = dkv[BLOCK:]

        @pl.when(i == nb)
        def _():
            dbuf[BLOCK:, :] = jnp.zeros((HALO, POOL_WIDTH), F32)
            for g, w in enumerate(POOL_WINDOWS):
                cols = slice(g * POOL_GROUP_DIM, (g + 1) * POOL_GROUP_DIM)
                dproj_ref[:, cols] = (_window_sum(dbuf, g, w, lambda k: k) + c_u[:, cols]).astype(BF16)
            dproj_ref[:, POOL_WIDTH:2 * POOL_WIDTH] = c_q[...].astype(BF16)
            dproj_ref[:, 2 * POOL_WIDTH:] = c_kv[...].astype(BF16)
            finish()

    cur = lambda i: jnp.minimum(i, nb - 1)
    prv = lambda i: jnp.maximum(jnp.minimum(i, nb - 1) - 1, 0)
    out = pl.pallas_call(
        body, name="mixers_bwd", grid=(nb + 1,),
        out_shape=[jax.ShapeDtypeStruct((t, proj.shape[1]), BF16),
                   jax.ShapeDtypeStruct((N_Q_HEADS, BLOCK, 2 * BLOCK), F32),
                   jax.ShapeDtypeStruct((1, 128), F32),
                   jax.ShapeDtypeStruct((4, POOL_GROUP_DIM, POOL_GROUP_DIM), F32),
                   jax.ShapeDtypeStruct((len(POOL_WINDOWS), POOL_GROUP_DIM), F32)]
        + [jax.ShapeDtypeStruct(p.shape, p.dtype) for p in ffn_parts],
        in_specs=_mixer_in_specs(cur, prv)[2:]
        + [pl.BlockSpec((BLOCK, 2 * POOL_WIDTH), lambda i: (cur(i), 0)), pl.BlockSpec((BLOCK, POOL_WIDTH), lambda i: (cur(i), 0)),
           pl.BlockSpec((N_Q_HEADS, BLOCK, 2 * BLOCK), lambda i: (0, cur(i), 0)), pl.BlockSpec((BLOCK, 128), lambda i: (cur(i), 0))]
        + _mixer_param_specs()[2:] + [ANY] * na,
        out_specs=[pl.BlockSpec((BLOCK, proj.shape[1]), lambda i: (jnp.maximum(i - 1, 0), 0)),
                   pl.BlockSpec((N_Q_HEADS, BLOCK, 2 * BLOCK), lambda i: (0, 0, 0)),
                   pl.BlockSpec((1, 128), lambda i: (0, 0)),
                   pl.BlockSpec((4, POOL_GROUP_DIM, POOL_GROUP_DIM), lambda i: (0, 0, 0)),
                   pl.BlockSpec((len(POOL_WINDOWS), POOL_GROUP_DIM), lambda i: (0, 0))] + [ANY] * na,
        scratch_shapes=[pltpu.VMEM((BLOCK + HALO, POOL_WIDTH), F32),
                        pltpu.VMEM((BLOCK, POOL_WIDTH), F32), pltpu.VMEM((BLOCK, POOL_WIDTH), F32),
                        pltpu.VMEM((BLOCK, 256), F32),
                        pltpu.VMEM((N_Q_HEADS, BLOCK, 2 * BLOCK), F32), pltpu.VMEM((N_Q_HEADS, BLOCK, 2 * BLOCK), BF16),
                        pltpu.VMEM((BLOCK, 128), F32)]
        + _chip_exchange_scratch(ffn_parts),
        compiler_params=_params(collective_id=CHIP_COLLECTIVE_ID),
    )(proj, proj, proj, dcat, pooled, probs, p_sinks, w_pool, pool_scale, *ffn_parts)
    return out[:5], out[5:]


def inproj_bwd(dproj, w_in_t, x, g, dx1):
    t, d = x.shape
    n = dproj.shape[1]
    tm = TOKEN_TILE

    def body(dp_ref, w_ref, x_ref, g_ref, dx1_ref, dx_ref, dg_ref):
        @pl.when(pl.program_id(0) == 0)
        def _():
            dg_ref[...] = jnp.zeros_like(dg_ref)

        dh = _dot(dp_ref[...], w_ref[...])
        xv = x_ref[...]
        dx, dg_rows = _norm_bwd(dh, xv, _rstd(xv), g_ref[...])
        dx_ref[...] = dx1_ref[...] + dx
        dg_ref[...] += _as_rows(jnp.sum(dg_rows, axis=0, keepdims=True))

    row = pl.BlockSpec((tm, d), lambda i: (i, 0))
    gain = pl.BlockSpec((1, d), lambda i: (0, 0))
    return pl.pallas_call(
        body, name="inproj_bwd", grid=(t // tm,),
        out_shape=[jax.ShapeDtypeStruct((t, d), F32), jax.ShapeDtypeStruct((d // 128, 128), F32)],
        in_specs=[pl.BlockSpec((tm, n), lambda i: (i, 0)), pl.BlockSpec(w_in_t.shape, lambda i: (0, 0)), row, gain, row],
        out_specs=[row, pl.BlockSpec((d // 128, 128), lambda i: (0, 0))],
        compiler_params=_params(),
    )(dproj, w_in_t, x, g, dx1)


def _bucket_band():
    qi = jnp.arange(BLOCK)[:, None]
    kj = jnp.arange(2 * BLOCK)[None, :]
    dist = qi + BLOCK - kj
    n = jnp.maximum(dist, 0)
    nf = jnp.maximum(n, 1).astype(F32)
    large = MAX_EXACT + (jnp.log(nf / MAX_EXACT) / np.float32(np.log(MAX_DISTANCE / MAX_EXACT))
                         * (N_BUCKETS - MAX_EXACT)).astype(jnp.int32)
    large = jnp.minimum(large, N_BUCKETS - 1)
    bucket = jnp.where(n < MAX_EXACT, n, large)
    in_window = (dist >= 0) & (dist < BLOCK)
    return bucket.astype(F32), in_window.astype(F32)


def kernel(x, g_pre_mix, w_in, w_pool, pool_scale, rel_bias, sinks, w_out, g_post_mix, g_pre_ffn, w_gate, w_up, w_down, g_post_ffn, loss_target, m_g_pre_mix, m_w_in, m_w_pool, m_pool_scale, m_rel_bias, m_sinks, m_w_out, m_g_post_mix, m_g_pre_ffn, m_w_gate, m_w_up, m_w_down, m_g_post_ffn, v_g_pre_mix, v_w_in, v_w_pool, v_pool_scale, v_rel_bias, v_sinks, v_w_out, v_g_post_mix, v_g_pre_ffn, v_w_gate, v_w_up, v_w_down, v_g_post_ffn):
    d = x.shape[-1]
    xs, target = x[0], loss_target[0]

    w_in_ts = w_in[0].T.astype(BF16)
    w_out_s = w_out[0].astype(BF16)
    gate_ts = w_gate[0].T.astype(BF16)
    up_ts = w_up[0].T.astype(BF16)
    w_down_s = w_down[0].astype(BF16)

    bucket, in_window = _bucket_band()
    biasm = bias_band(bucket, in_window, rel_bias)
    w_pool_b = w_pool[0].astype(BF16)
    half = up_ts.shape[0] // 2
    proj, h1, w_in_t, up_t = norm_inproj(xs, g_pre_mix, w_in_ts, up_ts[:half], up_ts.shape[0])
    w_in_t = w_in_t.reshape(-1, d)
    cat, pooled, probs, p_sinks, gate_t, w_out_f = mixers_fwd(proj, biasm, sinks, w_pool_b, pool_scale, [gate_ts, w_out_s])
    w_out_f = w_out_f.reshape(-1, d)
    mix, x1, h2, up_t = outproj_norm(cat, w_out_f, xs, g_post_mix, g_pre_ffn, up_ts[half:], up_t)
    gate, up, act, w_down_f = ffn_up(h2, gate_t, up_t, w_down_s)
    df, dy, dg_post_ffn, loss_part = ffn_down_loss(act, w_down_f, x1, g_post_ffn, target)

    def pair_sum(parts, tag):
        return pair_add(parts, pair_exchange(parts, "pair_exchange_" + tag), "pair_add_" + tag)

    dgate, dup = ffn_down_bwd(df, w_down_f, gate, up)
    (d_gate, d_up), _ = grad_ffn([dgate, dup], h2, "grad_w_gate_up")
    (d_down,), got_gate_up = grad_ffn([act], df, "grad_w_down", [d_gate, d_up])
    q_gate, q_up, q_down = pair_add(
        [d_gate, d_up, d_down], [*got_gate_up, *pair_exchange([d_down], "pair_exchange_down")], "pair_add_ffn")
    (dx1, dmix, dg_pre_ffn, dg_post_mix), (gate_slots, down_slots) = ffn_up_bwd(
        dgate, dup, gate_t, up_t, x1, g_pre_ffn, dy, mix, g_post_mix, [q_gate, q_down])
    dcat = outproj_bwd(dmix, w_out_f)
    d_out = grad_rows(cat, dmix, "grad_w_out", by_core=True)
    q_out, = pair_sum([d_out], "out")
    (dproj, dbias, dsinks, dw_pool, dpool_scale), (up_slots, out_slots) = mixers_bwd(
        proj, dcat, pooled, probs, p_sinks, w_pool_b, pool_scale, [q_up, q_out])
    drel_bias = bias_band_bwd(bucket, dbias)
    grad_x, dg_pre_mix = inproj_bwd(dproj, w_in_t, xs, g_pre_mix, dx1)

    small_w = [g_pre_mix, g_post_mix, g_pre_ffn, g_post_ffn, pool_scale, sinks, w_pool, rel_bias.T]
    small_m = [m_g_pre_mix, m_g_post_mix, m_g_pre_ffn, m_g_post_ffn, m_pool_scale, m_sinks, m_w_pool, m_rel_bias.T]
    small_v = [v_g_pre_mix, v_g_post_mix, v_g_pre_ffn, v_g_post_ffn, v_pool_scale, v_sinks, v_w_pool, v_rel_bias.T]
    d_in_t, total, total_rb = grad_w_in_small_reduce(
        dproj, h1, [dg_pre_mix, dg_post_mix, dg_pre_ffn, dg_post_ffn], dpool_scale, dsinks, loss_part, dw_pool, drel_bias)
    g_in_t = reduce_w_in(d_in_t)
    loss_row, sm = small_adamw(total, total_rb, small_w, small_m, small_v)
    sm[7] = [r.T for r in sm[7]]
    big_w = [w_in[0].T, w_out[0], w_gate[0].T, w_up[0].T, w_down[0]]
    big_m = [m_w_in[0].T, m_w_out[0], m_w_gate[0].T, m_w_up[0].T, m_w_down[0]]
    big_v = [v_w_in[0].T, v_w_out[0], v_w_gate[0].T, v_w_up[0].T, v_w_down[0]]
    upd = sum_adamw([out_slots, gate_slots, up_slots, down_slots], big_w[1:], big_m[1:], big_v[1:], "sum_adamw")
    upd = [[g_in_t, *adamw_update(big_w[:1], [g_in_t], big_m[:1], big_v[:1], "adamw_in")[0]], *upd]
    back = lambda k, a: (a.T if k in (0, 2, 3) else a)[None]
    big = [[back(k, u) for u in upd[k]] for k in range(5)]

    def ordered(kind):
        s, b = [p[kind] for p in sm], [p[kind] for p in big]
        return [s[0], b[0], s[6], s[4], s[7], s[5], b[1], s[1], s[2], b[2], b[3], b[4], s[3]]

    return (loss_row[0, 0], grad_x[None], *ordered(0), *ordered(1), *ordered(2), *ordered(3))
```

```python
import numpy as np
import jax
import jax.numpy as jnp
from jax import lax
from jax.experimental import pallas as pl
from jax.experimental.pallas import tpu as pltpu

F32 = jnp.float32
BF16 = jnp.bfloat16

N_DEV = 8
N_CHIP = 4
POOL_WIDTH = 512
POOL_WINDOWS = (2, 4, 8, 16)
POOL_GROUP_DIM = 128
HEAD_DIM = 64
N_Q_HEADS = 8
N_KV_HEADS = 2
GQA_GROUP = 4
BLOCK = 128
HALO = 16
ROW_CHUNK = 32
N_BUCKETS = 32
MAX_EXACT = 16
MAX_DISTANCE = 128
EPS = 1e-6
NEG_INF = -1e30
ATTN_SCALE = float(1.0 / np.sqrt(np.float32(HEAD_DIM)))

ADAM_LR = 0.001
ADAM_B1 = 0.9
ADAM_B2 = 0.999
ADAM_EPS = 1e-08
ADAM_WD = 0.01
ADAM_STEP = 10

TOKEN_TILE = 1024
WIDE_K_TOKEN_TILE = 512
FFN_TOKEN_TILE = 1024
FF_SHARDS_PER_TILE = 4
VMEM_LIMIT = 56 * 1024 * 1024
MESH = pl.DeviceIdType.MESH
PAIR_COLLECTIVE_ID = 0
GATHER_COLLECTIVE_ID = 1
CHIP_COLLECTIVE_ID = 2
ANY = pl.BlockSpec(memory_space=pl.ANY)
VMEM = pl.BlockSpec(memory_space=pltpu.VMEM)
SMEM = pl.BlockSpec(memory_space=pltpu.SMEM)


def _params(**kw):
    return pltpu.CompilerParams(vmem_limit_bytes=VMEM_LIMIT, **kw)


def _dot(a, b):
    return jnp.dot(a, b, preferred_element_type=F32)


def _dot_nt(a, b):
    return lax.dot_general(a, b, (((1,), (1,)), ((), ())), preferred_element_type=F32)


def _dot_tn(a, b):
    return lax.dot_general(a, b, (((0,), (0,)), ((), ())), preferred_element_type=F32)


def _rstd(v):
    return lax.rsqrt(jnp.mean(v * v, axis=-1, keepdims=True) + EPS)


def _norm_bwd(dout, v, r, g):
    vn = v * r
    dn = dout * g
    dv = r * (dn - vn * jnp.mean(dn * vn, axis=-1, keepdims=True))
    return dv, dout * vn


def _as_rows(v):
    return jnp.concatenate([v[:, k:k + 128] for k in range(0, v.shape[1], 128)], axis=0)


def _as_lanes(rows):
    return jnp.concatenate([rows[k:k + 1, :] for k in range(rows.shape[0])], axis=1)


def _handshake(peers):
    barrier = pltpu.get_barrier_semaphore()
    for peer in peers:
        pl.semaphore_signal(barrier, inc=1, device_id=peer, device_id_type=MESH)
    pl.semaphore_wait(barrier, len(peers))


def _merge_rows(value):
    s, r, c_ = value.shape
    return value.reshape(s * r, c_)


def _gather_plan(srcs, outs, send_sems, recv_sems, local_sems=None, bounce=None, rows=None, shake=True):
    n = len(srcs)
    x, y, c = lax.axis_index("x"), lax.axis_index("y"), lax.axis_index("c")
    me, sibling = (x, y, c), (x, y, 1 - c)
    chips = [(1 - x, y), (x, 1 - y), (1 - x, 1 - y)]

    def slot(a, px, py, pc):
        whole = outs[a].at[4 * px + 2 * py + pc]
        return whole if rows is None or rows[a] is None else whole.at[pl.ds(*rows[a])]

    def copy(a, k, block, to, from_src=False):
        return pltpu.make_async_remote_copy(
            src_ref=srcs[a] if from_src else slot(a, *block), dst_ref=slot(a, *block),
            send_sem=send_sems.at[k * n + a], recv_sem=recv_sems.at[k * n + a], device_id=to, device_id_type=MESH)

    def own_in(a):
        return pltpu.make_async_copy(srcs[a], bounce[a], local_sems.at[a])

    def own_out(a):
        return pltpu.make_async_copy(bounce[a], slot(a, *me), local_sems.at[a])

    def first(a):
        return [copy(a, 0, me, sibling, True)] + [copy(a, 1 + j, me, (*chip, c), True) for j, chip in enumerate(chips)]

    def passed(a, j):
        return copy(a, 4 + j, (*chips[j], c), sibling)

    def start():
        if shake:
            _handshake([sibling] + [(*chip, c) for chip in chips])
        for a in range(n):
            if bounce is not None:
                own_in(a).start()
            for cp in first(a):
                cp.start()

    def forward():
        if bounce is not None:
            for a in range(n):
                own_in(a).wait()
                own_out(a).start()
        for j, chip in enumerate(chips):
            for a in range(n):
                copy(a, 1 + j, (*chip, c), me).wait_recv()
                passed(a, j).start()

    def complete():
        for a in range(n):
            copy(a, 0, sibling, me).wait_recv()
            for j, chip in enumerate(chips):
                copy(a, 4 + j, (*chip, 1 - c), me).wait_recv()
        for a in range(n):
            for cp in first(a) + [passed(a, j) for j in range(3)]:
                cp.wait_send()
            if bounce is not None:
                own_out(a).wait()

    def finish():
        forward()
        complete()

    finish.forward, finish.complete = forward, complete
    return start, finish


def _gather_scratch(shards):
    n = len(shards)
    return [pltpu.SemaphoreType.DMA((7 * n,)), pltpu.SemaphoreType.DMA((7 * n,)), pltpu.SemaphoreType.DMA((n,))] \
        + [pltpu.VMEM(s.shape, s.dtype) for s in shards]


def _chip_exchange_plan(srcs, outs, send_sems, recv_sems, local_sems, bounce):
    n = len(srcs)
    x, y, c = lax.axis_index("x"), lax.axis_index("y"), lax.axis_index("c")
    my_chip = 2 * x + y

    def copies():
        out = []
        for a in range(n):
            for k in range(1, N_CHIP):
                px, py = x ^ (k >> 1), y ^ (k & 1)
                out.append(pltpu.make_async_remote_copy(
                    src_ref=srcs[a].at[2 * px + py], dst_ref=outs[a].at[my_chip],
                    send_sem=send_sems.at[(k - 1) * n + a], recv_sem=recv_sems.at[(k - 1) * n + a],
                    device_id=(px, py, c), device_id_type=MESH))
        return out

    def own_in(a):
        return pltpu.make_async_copy(srcs[a].at[my_chip], bounce[a], local_sems.at[a])

    def own_out(a):
        return pltpu.make_async_copy(bounce[a], outs[a].at[my_chip], local_sems.at[a])

    def start():
        _handshake([(x ^ (k >> 1), y ^ (k & 1), c) for k in range(1, N_CHIP)])
        for a in range(n):
            own_in(a).start()
        for cp in copies():
            cp.start()

    def finish():
        for a in range(n):
            own_in(a).wait()
            own_out(a).start()
        for cp in copies():
            cp.wait()
        for a in range(n):
            own_out(a).wait()

    return start, finish


def _chip_exchange_scratch(parts):
    n = len(parts)
    return [pltpu.SemaphoreType.DMA((3 * n,)), pltpu.SemaphoreType.DMA((3 * n,)), pltpu.SemaphoreType.DMA((n,))] \
        + [pltpu.VMEM(p.shape[1:], p.dtype) for p in parts]


def _pair_plan(srcs, outs, send_sems, recv_sems):
    x, y, c = lax.axis_index("x"), lax.axis_index("y"), lax.axis_index("c")

    def copies():
        return [pltpu.make_async_remote_copy(
            src_ref=srcs[a].at[1 - c], dst_ref=outs[a], send_sem=send_sems.at[a], recv_sem=recv_sems.at[a],
            device_id=(x, y, 1 - c), device_id_type=MESH) for a in range(len(srcs))]

    def start():
        _handshake([(x, y, 1 - c)])
        for cp in copies():
            cp.start()

    def finish():
        for cp in copies():
            cp.wait()

    return start, finish


def pair_exchange(parts, name):
    n = len(parts)

    def body(*refs):
        start, finish = _pair_plan(refs[:n], refs[n:2 * n], *refs[2 * n:])
        start()
        finish()

    return pl.pallas_call(
        body, name=name, out_shape=[jax.ShapeDtypeStruct(p.shape[1:], p.dtype) for p in parts],
        in_specs=[ANY] * n, out_specs=[ANY] * n,
        scratch_shapes=[pltpu.SemaphoreType.DMA((n,)), pltpu.SemaphoreType.DMA((n,))],
        compiler_params=_params(collective_id=PAIR_COLLECTIVE_ID),
    )(*parts)


def pair_add(parts, got, name):
    n = len(parts)

    def body(core_ref, *refs):
        for a in range(n):
            refs[2 * n + a][...] = (refs[a][...].astype(F32) + refs[n + a][...].astype(F32)).astype(BF16)

    def own(p):
        zeros = (0,) * (p.ndim - 2)
        return pl.BlockSpec((None, 1, *p.shape[2:]), lambda i, core: (core[0], i, *zeros))

    def plain(p):
        zeros = (0,) * (p.ndim - 1)
        return pl.BlockSpec((1, *p.shape[1:]), lambda i, core: (i, *zeros))

    core = lax.axis_index("c").astype(jnp.int32).reshape(1)
    return pl.pallas_call(
        body, name=name,
        grid_spec=pltpu.PrefetchScalarGridSpec(
            num_scalar_prefetch=1, grid=(got[0].shape[0],),
            in_specs=[own(p) for p in parts] + [plain(p) for p in got], out_specs=[plain(p) for p in got]),
        out_shape=[jax.ShapeDtypeStruct(p.shape, BF16) for p in got],
        compiler_params=_params(),
    )(core, *parts, *got)


def _adamw(w, g, m, v):
    m2 = ADAM_B1 * m + (1.0 - ADAM_B1) * g
    v2 = ADAM_B2 * v + (1.0 - ADAM_B2) * (g * g)
    m_hat = m2 / (1.0 - ADAM_B1 ** ADAM_STEP)
    v_hat = v2 / (1.0 - ADAM_B2 ** ADAM_STEP)
    delta = -ADAM_LR * (m_hat / (jnp.sqrt(v_hat) + ADAM_EPS) + ADAM_WD * w)
    return delta, m2, v2


def sum_adamw(slots, ws, ms, vs, name):
    n = len(ws)
    halves = 2

    def body(*refs):
        for a in range(n):
            total = refs[a][0].astype(F32)
            for s in range(1, slots[a].shape[0]):
                total = total + refs[a][s].astype(F32)
            delta, m2, v2 = _adamw(refs[n + a][...], total, refs[2 * n + a][...], refs[3 * n + a][...])
            for q, val in enumerate((total, delta, m2, v2)):
                refs[4 * n + 4 * a + q][...] = val

    def rows(w):
        return pl.BlockSpec((w.shape[0] // halves, w.shape[1]), lambda i: (i, 0))

    def slot_rows(p):
        return pl.BlockSpec((p.shape[0], p.shape[1] // halves, p.shape[2]), lambda i: (0, i, 0))

    out = pl.pallas_call(
        body, name=name, grid=(halves,),
        out_shape=[jax.ShapeDtypeStruct(w.shape, F32) for w in ws for _ in range(4)],
        in_specs=[slot_rows(p) for p in slots] + [rows(w) for w in ws] * 3,
        out_specs=[rows(w) for w in ws for _ in range(4)],
        compiler_params=_params(),
    )(*slots, *ws, *ms, *vs)
    return [out[4 * a:4 * a + 4] for a in range(n)]


def adamw_update(ws, gs, ms, vs, name):
    n = len(ws)

    def body(*refs):
        for a in range(n):
            delta, m2, v2 = _adamw(refs[a][...], refs[n + a][...], refs[2 * n + a][...], refs[3 * n + a][...])
            refs[4 * n + 3 * a][...] = delta
            refs[4 * n + 3 * a + 1][...] = m2
            refs[4 * n + 3 * a + 2][...] = v2

    out = pl.pallas_call(
        body, name=name,
        out_shape=[jax.ShapeDtypeStruct(w.shape, F32) for w in ws for _ in range(3)],
        in_specs=[VMEM] * (4 * n), out_specs=[VMEM] * (3 * n),
        compiler_params=_params(),
    )(*ws, *gs, *ms, *vs)
    return [out[3 * a:3 * a + 3] for a in range(n)]


GAIN_ROWS = 8
ROW_POOL_SCALE = 4 * GAIN_ROWS
ROW_SINKS = ROW_POOL_SCALE + 4
ROW_LOSS = ROW_SINKS + 1
ROW_W_POOL = 40
SMALL_ROWS = ROW_W_POOL + 4 * POOL_GROUP_DIM


def grad_w_in_small_reduce(a, b, gains, dpool_scale, dsinks, loss_part, dw_pool, drel_bias):
    t, m = a.shape
    d = b.shape[1]
    r = m // N_DEV
    tt = TOKEN_TILE
    last = t // tt - 1

    def body(a_ref, b_ref, g0, g1, g2, g3, dsc_ref, dsink_ref, loss_ref, dwp_ref, drb_ref, out_ref, total_ref, total_rb_ref,
             acc, stage, gat, gat_rb, g_send, g_recv):
        k = pl.program_id(0)
        x, y, c = lax.axis_index("x"), lax.axis_index("y"), lax.axis_index("c")
        start, finish = _gather_plan([stage, drb_ref], [gat, gat_rb], g_send, g_recv)

        @pl.when(k == 0)
        def _():
            for q, g_ref in enumerate((g0, g1, g2, g3)):
                stage[GAIN_ROWS * q:GAIN_ROWS * (q + 1), :] = g_ref[...]
            stage[ROW_POOL_SCALE:ROW_SINKS, :] = dsc_ref[...]
            stage[ROW_SINKS:ROW_LOSS, :] = dsink_ref[...]
            stage[ROW_LOSS:ROW_LOSS + 1, :] = loss_ref[...]
            stage[ROW_LOSS + 1:ROW_W_POOL, :] = jnp.zeros((ROW_W_POOL - ROW_LOSS - 1, 128), F32)
            stage[ROW_W_POOL:, :] = dwp_ref[...].reshape(4 * POOL_GROUP_DIM, POOL_GROUP_DIM)
            gat[4 * x + 2 * y + c] = stage[...]
            gat_rb[4 * x + 2 * y + c] = drb_ref[...]
            start()
            acc[...] = jnp.zeros_like(acc)

        acc[...] += _dot_tn(a_ref[...], b_ref[...])

        @pl.when(k == last)
        def _():
            blocks = acc[...].reshape(N_CHIP, 2, r, d)
            for chip in range(N_CHIP):
                for core in range(2):
                    out_ref[core, chip] = blocks[chip, core].astype(BF16)
            finish()
            total, total_rb = gat[0], gat_rb[0]
            for s in range(1, N_DEV):
                total, total_rb = total + gat[s], total_rb + gat_rb[s]
            total_ref[...] = total
            total_rb_ref[...] = total_rb

    out_shape = (2, N_CHIP, r, d)
    return pl.pallas_call(
        body, name="grad_w_in", grid=(t // tt,),
        out_shape=[jax.ShapeDtypeStruct(out_shape, BF16), jax.ShapeDtypeStruct((SMALL_ROWS, 128), F32),
                   jax.ShapeDtypeStruct(drel_bias.shape, F32)],
        in_specs=[pl.BlockSpec((tt, m), lambda k: (k, 0)), pl.BlockSpec((tt, d), lambda k: (k, 0))] + [VMEM] * 9,
        out_specs=[pl.BlockSpec(out_shape, lambda k: (0,) * len(out_shape)), VMEM, VMEM],
        scratch_shapes=[pltpu.VMEM((m, d), F32), pltpu.VMEM((SMALL_ROWS, 128), F32),
                        pltpu.VMEM((N_DEV, SMALL_ROWS, 128), F32), pltpu.VMEM((N_DEV, *drel_bias.shape), F32),
                        pltpu.SemaphoreType.DMA((14,)), pltpu.SemaphoreType.DMA((14,))],
        compiler_params=_params(collective_id=GATHER_COLLECTIVE_ID),
    )(a, b, *gains, dpool_scale, dsinks, loss_part, dw_pool, drel_bias)


def reduce_w_in(d_in_t):
    def body(d_in_ref, g_in_ref, pair_got, chip_part, chip_got, p_send, p_recv, x_send, x_recv):
        x, y, c = lax.axis_index("x"), lax.axis_index("y"), lax.axis_index("c")
        my_chip = 2 * x + y
        _handshake([(x, y, 1 - c)] + [(x ^ (k >> 1), y ^ (k & 1), c) for k in range(1, N_CHIP)])
        pair = pltpu.make_async_remote_copy(
            src_ref=d_in_ref.at[1 - c], dst_ref=pair_got, send_sem=p_send, recv_sem=p_recv,
            device_id=(x, y, 1 - c), device_id_type=MESH)
        pair.start()
        pair.wait()
        chip_part[...] = (d_in_ref[c].astype(F32) + pair_got[...].astype(F32)).astype(BF16)
        copies = []
        for k in range(1, N_CHIP):
            px, py = x ^ (k >> 1), y ^ (k & 1)
            copies.append(pltpu.make_async_remote_copy(
                src_ref=chip_part.at[2 * px + py], dst_ref=chip_got.at[my_chip],
                send_sem=x_send.at[k - 1], recv_sem=x_recv.at[k - 1], device_id=(px, py, c), device_id_type=MESH))
        for cp in copies:
            cp.start()
        chip_got[my_chip] = chip_part[my_chip]
        for cp in copies:
            cp.wait()
        g_in = chip_got[0].astype(F32)
        for s in range(1, N_CHIP):
            g_in = g_in + chip_got[s].astype(F32)
        g_in_ref[...] = g_in

    per_core = d_in_t.shape[1:]
    return pl.pallas_call(
        body, name="reduce_w_in",
        out_shape=jax.ShapeDtypeStruct(d_in_t.shape[2:], F32),
        in_specs=[VMEM], out_specs=VMEM,
        scratch_shapes=[pltpu.VMEM(per_core, d_in_t.dtype), pltpu.VMEM(per_core, d_in_t.dtype),
                        pltpu.VMEM(per_core, d_in_t.dtype),
                        pltpu.SemaphoreType.DMA, pltpu.SemaphoreType.DMA,
                        pltpu.SemaphoreType.DMA((3,)), pltpu.SemaphoreType.DMA((3,))],
        compiler_params=_params(collective_id=GATHER_COLLECTIVE_ID),
    )(d_in_t)


def small_adamw(total, total_rb, small_w, small_m, small_v):
    n_small = len(small_w)

    def body(*refs):
        total_ref, rb_ref = refs[:2]
        w_refs, m_refs, v_refs = (refs[2 + k * n_small:2 + (k + 1) * n_small] for k in range(3))
        loss_out = refs[2 + 3 * n_small]
        result = refs[3 + 3 * n_small:]
        total = total_ref[...]
        loss_out[...] = total[ROW_LOSS:ROW_LOSS + 1, :]
        grads = [_as_lanes(total[GAIN_ROWS * k:GAIN_ROWS * (k + 1), :]) for k in range(4)]
        grads.append(_as_lanes(total[ROW_POOL_SCALE:ROW_SINKS, :]))
        grads.append(total[ROW_SINKS:ROW_LOSS, 0:N_Q_HEADS])
        grads.append(total[ROW_W_POOL:, :].reshape(w_refs[6].shape))
        grads.append(rb_ref[...])
        for k in range(n_small):
            delta, m2, v2 = _adamw(w_refs[k][...], grads[k], m_refs[k][...], v_refs[k][...])
            result[4 * k][...] = grads[k]
            result[4 * k + 1][...] = delta
            result[4 * k + 2][...] = m2
            result[4 * k + 3][...] = v2

    out = pl.pallas_call(
        body, name="small_adamw",
        out_shape=[jax.ShapeDtypeStruct((1, 128), F32)] + [jax.ShapeDtypeStruct(w.shape, F32) for w in small_w for _ in range(4)],
        in_specs=[VMEM] * (2 + 3 * n_small), out_specs=[VMEM] * (1 + 4 * n_small),
        compiler_params=_params(),
    )(total, total_rb, *small_w, *small_m, *small_v)
    return out[0], [out[1 + 4 * k:5 + 4 * k] for k in range(n_small)]


def norm_inproj(x, g, w_shard, shard, shard_rows):
    t, d = x.shape
    r = w_shard.shape[0]
    tm = TOKEN_TILE
    nt = t // tm

    def body(x_ref, g_ref, w_shard_ref, shard_ref, proj_ref, h_ref, w_ref, gathered_ref, h_all, w_all, w_sem,
             send_w, recv_w, local_w, bounce_w, send_sems, recv_sems, local_sems, bounce):
        i = pl.program_id(0)
        start_w, finish_w = _gather_plan([w_shard_ref], [w_ref], send_w, recv_w, local_w, [bounce_w])
        start, finish = _gather_plan([shard_ref], [gathered_ref], send_sems, recv_sems, local_sems, [bounce],
                                     [(0, shard.shape[0])], shake=False)

        @pl.when(i == 0)
        def _():
            start_w()
            start()

        @pl.when(i < nt)
        def _():
            xv = x_ref[...]
            h = ((xv * _rstd(xv)) * g_ref[...]).astype(BF16)
            h_ref[...] = h
            h_all[pl.ds(pl.multiple_of(i * tm, tm), tm), :] = h

        @pl.when(i == nt - 1)
        def _():
            finish_w()
            landed = pltpu.make_async_copy(w_ref, w_all, w_sem)
            landed.start()
            landed.wait()

        @pl.when(i >= nt)
        def _():
            rows = pl.ds(pl.multiple_of((i - nt) * tm, tm), tm)
            proj_ref[...] = _dot_nt(h_all[rows, :], _merge_rows(w_all[...]))

        pl.when(i == 2 * nt - 2)(finish.forward)
        pl.when(i == 2 * nt - 1)(finish.complete)

    first = lambda i: (jnp.minimum(i, nt - 1), 0)
    return pl.pallas_call(
        body, name="norm_inproj", grid=(2 * nt,),
        out_shape=[jax.ShapeDtypeStruct((t, N_DEV * r), F32), jax.ShapeDtypeStruct((t, d), BF16),
                   jax.ShapeDtypeStruct((N_DEV, r, d), w_shard.dtype),
                   jax.ShapeDtypeStruct((N_DEV, shard_rows, d), shard.dtype)],
        in_specs=[pl.BlockSpec((tm, d), first), pl.BlockSpec((1, d), lambda i: (0, 0)), ANY, ANY],
        out_specs=[pl.BlockSpec((tm, N_DEV * r), lambda i: (jnp.maximum(i - nt, 0), 0)), pl.BlockSpec((tm, d), first),
                   ANY, ANY],
        scratch_shapes=[pltpu.VMEM((t, d), BF16), pltpu.VMEM((N_DEV, r, d), w_shard.dtype), pltpu.SemaphoreType.DMA]
        + _gather_scratch([w_shard]) + _gather_scratch([shard]),
        compiler_params=_params(collective_id=GATHER_COLLECTIVE_ID),
    )(x, g, w_shard, shard)


def bias_band(bucket, in_window, rel_bias):
    def body(bk_ref, win_ref, rb_ref, out_ref):
        bk = bk_ref[...]
        keep = win_ref[...] > 0.5
        for h in range(N_Q_HEADS):
            acc = jnp.zeros(bk.shape, F32)
            for b in range(N_BUCKETS):
                acc = jnp.where(bk == float(b), rb_ref[b, h], acc)
            out_ref[h] = jnp.where(keep, acc, NEG_INF)

    return pl.pallas_call(
        body, name="bias_band",
        out_shape=jax.ShapeDtypeStruct((N_Q_HEADS, BLOCK, 2 * BLOCK), F32),
        in_specs=[VMEM, VMEM, SMEM], out_specs=VMEM,
    )(bucket, in_window, rel_bias)


def _window_sum(buf_ref, g, w, first):
    cols = slice(g * POOL_GROUP_DIM, (g + 1) * POOL_GROUP_DIM)
    acc = None
    for k in range(w):
        piece = buf_ref[first(k):first(k) + BLOCK, cols]
        acc = piece if acc is None else acc + piece
    return acc


def _inv_count(i, w):
    row = lax.broadcasted_iota(jnp.int32, (BLOCK, 1), 0)
    return 1.0 / jnp.minimum(i * BLOCK + row + 1, w).astype(F32)


def _fill_pool_input(i, ubuf, uc_ref, halo_ref):
    ubuf[0:HALO, :] = jnp.where(i > 0, halo_ref[...], 0.0)
    ubuf[HALO:, :] = uc_ref[...]


def _pooled(i, g, w, ubuf):
    cols = slice(g * POOL_GROUP_DIM, (g + 1) * POOL_GROUP_DIM)
    return _window_sum(ubuf, g, w, lambda k: HALO - k) * _inv_count(i, w) - ubuf[HALO:, cols]


def _head_variants(pair):
    low = lax.broadcasted_iota(jnp.int32, pair.shape, 1) < HEAD_DIM
    swapped = pltpu.roll(pair, HEAD_DIM, 1)
    zero = jnp.zeros_like(pair)
    pick = lambda c, a, b: jnp.where(c, a, b).astype(BF16)
    return [[pick(low, pair, zero), pick(low, zero, swapped)], [pick(low, swapped, zero), pick(low, zero, pair)]]


def _head_probs(i, hq, rows, s_ref, biasm_ref, sinks_ref):
    s = s_ref[hq, rows, :] * ATTN_SCALE + biasm_ref[hq, rows, :]
    col = lax.broadcasted_iota(jnp.int32, s.shape, 1)
    s = jnp.where((i == 0) & (col < BLOCK), NEG_INF, s)
    sink = sinks_ref[0, hq]
    m = jnp.maximum(jnp.max(s, axis=-1, keepdims=True), sink)
    p = jnp.exp(s - m)
    e_sink = jnp.exp(sink - m)
    inv = 1.0 / (jnp.sum(p, axis=-1, keepdims=True) + e_sink)
    return p * inv, e_sink * inv


def _head_slot(hq):
    return 4 * (hq // GQA_GROUP) + 2 * (hq % 2) + (hq % GQA_GROUP) // 2


def _mixer_in_specs(cur, prv):
    return [pl.BlockSpec((BLOCK, 512), lambda i: (cur(i), 0)),
            pl.BlockSpec((HALO, 512), lambda i: (jnp.maximum(cur(i) * (BLOCK // HALO) - 1, 0), 0)),
            pl.BlockSpec((BLOCK, 512), lambda i: (cur(i), 1)),
            pl.BlockSpec((BLOCK, 256), lambda i: (cur(i), 4)),
            pl.BlockSpec((BLOCK, 256), lambda i: (prv(i), 4))]


def _mixer_param_specs():
    return [pl.BlockSpec((N_Q_HEADS, BLOCK, 2 * BLOCK), lambda i: (0, 0, 0)), SMEM,
            pl.BlockSpec((4, POOL_GROUP_DIM, POOL_GROUP_DIM), lambda i: (0, 0, 0)),
            pl.BlockSpec((1, POOL_WIDTH), lambda i: (0, 0))]


def mixers_fwd(proj, biasm, sinks, w_pool, pool_scale, shards):
    t = proj.shape[0]
    nb = t // BLOCK
    ns = len(shards)

    def body(*refs):
        uc_ref, halo_ref, q_ref, kvc_ref, kvp_ref, biasm_ref, sinks_ref, wp_ref, sc_ref = refs[:9]
        shard_refs = refs[9:9 + ns]
        out_ref, pooled_ref, p_all, psink_ref = refs[9 + ns:13 + ns]
        gathered_refs = refs[13 + ns:13 + 2 * ns]
        ubuf, s_all, send_sems, recv_sems, local_sems = refs[13 + 2 * ns:18 + 2 * ns]
        i = pl.program_id(0)
        start, finish = _gather_plan(shard_refs, gathered_refs, send_sems, recv_sems, local_sems, refs[18 + 2 * ns:])
        pl.when(i == 0)(start)

        _fill_pool_input(i, ubuf, uc_ref, halo_ref)
        for g, w in enumerate(POOL_WINDOWS):
            cols = slice(g * POOL_GROUP_DIM, (g + 1) * POOL_GROUP_DIM)
            pooled = _pooled(i, g, w, ubuf).astype(BF16)
            pooled_ref[:, cols] = pooled
            out_ref[:, cols] = (_dot(pooled, wp_ref[g]) * sc_ref[:, cols]).astype(BF16)
        kv = jnp.concatenate([kvp_ref[...], kvc_ref[...]], axis=0)
        k_var = _head_variants(kv[:, 0:2 * HEAD_DIM])
        v_var = _head_variants(kv[:, 2 * HEAD_DIM:])
        for hq in range(N_Q_HEADS):
            j, half, h = hq // 2, hq % 2, hq // GQA_GROUP
            q2 = q_ref[:, 2 * HEAD_DIM * j:2 * HEAD_DIM * (j + 1)].astype(BF16)
            s_all[hq] = _dot_nt(q2, k_var[h][half])
        psink_ref[...] = jnp.zeros_like(psink_ref)
        for hq in range(N_Q_HEADS):
            for r in range(0, BLOCK, ROW_CHUNK):
                rows = slice(r, r + ROW_CHUNK)
                probs, p_sink = _head_probs(i, hq, rows, s_all, biasm_ref, sinks_ref)
                p_all[_head_slot(hq), rows, :] = probs.astype(BF16)
                psink_ref[rows, hq:hq + 1] = p_sink
        for j in range(N_Q_HEADS // 2):
            h = 2 * j // GQA_GROUP
            acc = _dot(p_all[_head_slot(2 * j)], v_var[h][0]) + _dot(p_all[_head_slot(2 * j + 1)], v_var[h][1])
            out_ref[:, POOL_WIDTH + 2 * HEAD_DIM * j:POOL_WIDTH + 2 * HEAD_DIM * (j + 1)] = acc.astype(BF16)

        pl.when(i == max(nb - 4, 0))(finish.forward)
        pl.when(i == nb - 1)(finish.complete)

    return pl.pallas_call(
        body, name="mixers_fwd", grid=(nb,),
        out_shape=[jax.ShapeDtypeStruct((t, 2 * POOL_WIDTH), BF16), jax.ShapeDtypeStruct((t, POOL_WIDTH), BF16),
                   jax.ShapeDtypeStruct((N_Q_HEADS, t, 2 * BLOCK), BF16), jax.ShapeDtypeStruct((t, 128), F32)]
        + [jax.ShapeDtypeStruct((N_DEV, *sh.shape), sh.dtype) for sh in shards],
        in_specs=_mixer_in_specs(lambda i: i, lambda i: jnp.maximum(i - 1, 0)) + _mixer_param_specs() + [ANY] * ns,
        out_specs=[pl.BlockSpec((BLOCK, 2 * POOL_WIDTH), lambda i: (i, 0)), pl.BlockSpec((BLOCK, POOL_WIDTH), lambda i: (i, 0)),
                   pl.BlockSpec((N_Q_HEADS, BLOCK, 2 * BLOCK), lambda i: (0, i, 0)), pl.BlockSpec((BLOCK, 128), lambda i: (i, 0))]
        + [ANY] * ns,
        scratch_shapes=[pltpu.VMEM((HALO + BLOCK, POOL_WIDTH), F32), pltpu.VMEM((N_Q_HEADS, BLOCK, 2 * BLOCK), F32)]
        + _gather_scratch(shards),
        compiler_params=_params(collective_id=GATHER_COLLECTIVE_ID),
    )(proj, proj, proj, proj, proj, biasm, sinks, w_pool, pool_scale, *shards)


def outproj_norm(cat, w, x, g, g_next, shard, partial):
    t, d = x.shape
    tm = TOKEN_TILE
    last = t // tm - 1
    rows = [(partial.shape[1] - shard.shape[0], shard.shape[0])]

    def body(c_ref, w_ref, x_ref, g_ref, gn_ref, shard_ref, partial_ref, mix_ref, x1_ref, h2_ref, gathered_ref,
             send_sems, recv_sems, local_sems, bounce):
        i = pl.program_id(0)
        start, finish = _gather_plan([shard_ref], [gathered_ref], send_sems, recv_sems, local_sems, [bounce], rows)
        pl.when(i == 0)(start)
        mix = _dot(c_ref[...], w_ref[...])
        mix_ref[...] = mix
        x1 = x_ref[...] + (mix * _rstd(mix)) * g_ref[...]
        x1_ref[...] = x1
        h2_ref[...] = ((x1 * _rstd(x1)) * gn_ref[...]).astype(BF16)
        pl.when(i == max(last - 1, 0))(finish.forward)
        pl.when(i == last)(finish.complete)

    row = pl.BlockSpec((tm, d), lambda i: (i, 0))
    gain = pl.BlockSpec((1, d), lambda i: (0, 0))
    return pl.pallas_call(
        body, name="outproj_norm", grid=(t // tm,),
        out_shape=[jax.ShapeDtypeStruct((t, d), F32), jax.ShapeDtypeStruct((t, d), F32), jax.ShapeDtypeStruct((t, d), BF16),
                   jax.ShapeDtypeStruct(partial.shape, partial.dtype)],
        in_specs=[pl.BlockSpec((tm, cat.shape[1]), lambda i: (i, 0)), pl.BlockSpec(w.shape, lambda i: (0, 0)), row, gain, gain,
                  ANY, ANY],
        out_specs=[row, row, row, ANY],
        input_output_aliases={6: 3},
        scratch_shapes=_gather_scratch([shard]),
        compiler_params=_params(collective_id=GATHER_COLLECTIVE_ID),
    )(cat, w, x, g, g_next, shard, partial)


def ffn_up(h, gate_t, up_t, down_shard):
    t, d = h.shape
    n = gate_t.shape[1]
    f = N_DEV * n
    tm, ts = FFN_TOKEN_TILE, FF_SHARDS_PER_TILE
    tn = ts * n
    steps = (f // tn, t // tm)

    def body(h_ref, wg_ref, wu_ref, shard_ref, gate_ref, up_ref, a_ref, gathered_ref,
             send_sems, recv_sems, local_sems, bounce):
        j, i = pl.program_id(0), pl.program_id(1)
        start, finish = _gather_plan([shard_ref], [gathered_ref], send_sems, recv_sems, local_sems, [bounce])
        pl.when((i == 0) & (j == 0))(start)

        hv = h_ref[...]
        gate = _dot_nt(hv, _merge_rows(wg_ref[...]))
        up = _dot_nt(hv, _merge_rows(wu_ref[...]))
        gate_ref[...] = gate.astype(BF16)
        up_ref[...] = up.astype(BF16)
        a_ref[...] = (gate * (1.0 / (1.0 + jnp.exp(-gate))) * up).astype(BF16)

        pl.when((j == steps[0] - 1) & (i == max(steps[1] - 2, 0)))(finish.forward)
        pl.when((j == steps[0] - 1) & (i == steps[1] - 1))(finish.complete)

    wide = pl.BlockSpec((tm, tn), lambda j, i: (i, j))
    return pl.pallas_call(
        body, name="ffn_up", grid=steps,
        out_shape=[jax.ShapeDtypeStruct((t, f), BF16)] * 3
        + [jax.ShapeDtypeStruct((N_DEV, *down_shard.shape), down_shard.dtype)],
        in_specs=[pl.BlockSpec((tm, d), lambda j, i: (i, 0)),
                  pl.BlockSpec((ts, n, d), lambda j, i: (j, 0, 0)),
                  pl.BlockSpec((ts, n, d), lambda j, i: (j, 0, 0)), ANY],
        out_specs=[wide, wide, wide, ANY],
        scratch_shapes=_gather_scratch([down_shard]),
        compiler_params=_params(collective_id=GATHER_COLLECTIVE_ID),
    )(h, gate_t, up_t, down_shard)


def ffn_down_loss(a, w_down, x1, g, target):
    t, d = x1.shape
    tm = WIDE_K_TOKEN_TILE

    def body(a_ref, w_ref, x_ref, g_ref, t_ref, df_ref, dy_ref, dg_ref, loss_ref):
        @pl.when(pl.program_id(0) == 0)
        def _():
            dg_ref[...] = jnp.zeros_like(dg_ref)
            loss_ref[...] = jnp.zeros_like(loss_ref)

        f = _dot(a_ref[...], _merge_rows(w_ref[...]))
        r = _rstd(f)
        g = g_ref[...]
        err = x_ref[...] + (f * r) * g - t_ref[...]
        loss_ref[...] += 0.5 * jnp.sum(jnp.mean(err * err, axis=-1, keepdims=True))
        dy = err * (1.0 / d)
        dy_ref[...] = dy
        df, dg_rows = _norm_bwd(dy, f, r, g)
        df_ref[...] = df.astype(BF16)
        dg_ref[...] += _as_rows(jnp.sum(dg_rows, axis=0, keepdims=True))

    row = pl.BlockSpec((tm, d), lambda i: (i, 0))
    gain = pl.BlockSpec((1, d), lambda i: (0, 0))
    return pl.pallas_call(
        body, name="ffn_down_loss", grid=(t // tm,),
        out_shape=[jax.ShapeDtypeStruct((t, d), BF16), jax.ShapeDtypeStruct((t, d), F32),
                   jax.ShapeDtypeStruct((d // 128, 128), F32), jax.ShapeDtypeStruct((1, 128), F32)],
        in_specs=[pl.BlockSpec((tm, a.shape[1]), lambda i: (i, 0)), pl.BlockSpec(w_down.shape, lambda i: (0, 0, 0)), row, gain, row],
        out_specs=[row, row, pl.BlockSpec((d // 128, 128), lambda i: (0, 0)), pl.BlockSpec((1, 128), lambda i: (0, 0))],
        compiler_params=_params(),
    )(a, w_down, x1, g, target)


def ffn_down_bwd(df, w_down, gate, up):
    t, d = df.shape
    n = w_down.shape[1]
    f = gate.shape[1]
    tm, ts = FFN_TOKEN_TILE, FF_SHARDS_PER_TILE
    tn = ts * n

    def body(df_ref, w_ref, gate_ref, up_ref, dgate_ref, dup_ref):
        da = _dot_nt(df_ref[...], _merge_rows(w_ref[...]))
        gate = gate_ref[...].astype(F32)
        sig = 1.0 / (1.0 + jnp.exp(-gate))
        dgate_ref[...] = (da * up_ref[...].astype(F32) * (sig * (1.0 + gate * (1.0 - sig)))).astype(BF16)
        dup_ref[...] = (da * (gate * sig)).astype(BF16)

    wide = pl.BlockSpec((tm, tn), lambda j, i: (i, j))
    return pl.pallas_call(
        body, name="ffn_down_bwd", grid=(f // tn, t // tm),
        out_shape=[jax.ShapeDtypeStruct((t, f), BF16)] * 2,
        in_specs=[pl.BlockSpec((tm, d), lambda j, i: (i, 0)), pl.BlockSpec((ts, n, d), lambda j, i: (j, 0, 0)), wide, wide],
        out_specs=[wide, wide],
        compiler_params=_params(),
    )(df, w_down, gate, up)


def grad_rows(a, b, name, by_core=False):
    t, m = a.shape
    d = b.shape[1]
    r = m // N_DEV
    tt = TOKEN_TILE
    last = t // tt - 1
    out_shape = (2, N_CHIP, r, d) if by_core else (N_DEV, r, d)

    def body(a_ref, b_ref, out_ref, acc):
        k = pl.program_id(0)

        @pl.when(k == 0)
        def _():
            acc[...] = jnp.zeros_like(acc)

        acc[...] += _dot_tn(a_ref[...], b_ref[...])

        @pl.when(k == last)
        def _():
            if by_core:
                blocks = acc[...].reshape(N_CHIP, 2, r, d)
                for chip in range(N_CHIP):
                    for core in range(2):
                        out_ref[core, chip] = blocks[chip, core].astype(BF16)
            else:
                out_ref[...] = acc[...].reshape(out_shape).astype(BF16)

    return pl.pallas_call(
        body, name=name, grid=(t // tt,),
        out_shape=jax.ShapeDtypeStruct(out_shape, BF16),
        in_specs=[pl.BlockSpec((tt, m), lambda k: (k, 0)), pl.BlockSpec((tt, d), lambda k: (k, 0))],
        out_specs=pl.BlockSpec(out_shape, lambda k: (0,) * len(out_shape)),
        scratch_shapes=[pltpu.VMEM((m, d), F32)],
        compiler_params=_params(),
    )(a, b)


def grad_ffn(lhs, b, name, pair_parts=()):
    t, f = lhs[0].shape
    d = b.shape[1]
    nw = len(lhs)
    na = len(pair_parts)
    n = f // N_DEV
    tt, ts = TOKEN_TILE, FF_SHARDS_PER_TILE
    tn = ts * n
    steps = (f // tn, t // tt)

    def body(*refs):
        a_refs, b_ref, part_refs = refs[:nw], refs[nw], refs[nw + 1:nw + 1 + na]
        out_refs = refs[nw + 1 + na:2 * nw + 1 + na]
        got_refs = refs[2 * nw + 1 + na:2 * nw + 1 + 2 * na]
        acc = refs[2 * nw + 1 + 2 * na]
        i, k = pl.program_id(0), pl.program_id(1)
        if na:
            start, finish = _pair_plan(part_refs, got_refs, *refs[2 * nw + 2 + 2 * na:])
            pl.when((i == 0) & (k == 0))(start)

        @pl.when(k == 0)
        def _():
            acc[...] = jnp.zeros_like(acc)

        for w in range(nw):
            acc[w] += _dot_tn(a_refs[w][...], b_ref[...])

        @pl.when(k == steps[1] - 1)
        def _():
            for w in range(nw):
                blocks = acc[w].reshape(ts // 2, 2, n, d)
                for chip in range(ts // 2):
                    for core in range(2):
                        out_refs[w][core, chip] = blocks[chip, core].astype(BF16)

        if na:
            pl.when((i == steps[0] - 1) & (k == steps[1] - 1))(finish)

    out = pl.pallas_call(
        body, name=name, grid=steps,
        out_shape=[jax.ShapeDtypeStruct((2, N_CHIP, n, d), BF16)] * nw
        + [jax.ShapeDtypeStruct(p.shape[1:], p.dtype) for p in pair_parts],
        in_specs=[pl.BlockSpec((tt, tn), lambda i, k: (k, i))] * nw + [pl.BlockSpec((tt, d), lambda i, k: (k, 0))] + [ANY] * na,
        out_specs=[pl.BlockSpec((2, ts // 2, n, d), lambda i, k: (0, i, 0, 0))] * nw + [ANY] * na,
        scratch_shapes=[pltpu.VMEM((nw, tn, d), F32)]
        + ([pltpu.SemaphoreType.DMA((na,)), pltpu.SemaphoreType.DMA((na,))] if na else []),
        compiler_params=_params(collective_id=PAIR_COLLECTIVE_ID) if na else _params(),
    )(*lhs, b, *pair_parts)
    return out[:nw], out[nw:]


def ffn_up_bwd(dgate, dup, gate_t, up_t, x1, g_ffn, dy, mix, g_mix, chip_parts):
    t, d = x1.shape
    n = gate_t.shape[1]
    f = N_DEV * n
    tm = WIDE_K_TOKEN_TILE
    na = len(chip_parts)
    last = t // tm - 1

    def body(*refs):
        dg_ref, du_ref, wg_ref, wu_ref, x_ref, gf_ref, dy_ref, mix_ref, gm_ref = refs[:9]
        part_refs = refs[9:9 + na]
        dx1_ref, dmix_ref, dgf_ref, dgm_ref = refs[9 + na:13 + na]
        slot_refs = refs[13 + na:13 + 2 * na]
        send_sems, recv_sems, local_sems = refs[13 + 2 * na:16 + 2 * na]
        i = pl.program_id(0)
        start, finish = _chip_exchange_plan(part_refs, slot_refs, send_sems, recv_sems, local_sems, refs[16 + 2 * na:])

        @pl.when(i == 0)
        def _():
            start()
            dgf_ref[...] = jnp.zeros_like(dgf_ref)
            dgm_ref[...] = jnp.zeros_like(dgm_ref)

        dh = _dot(dg_ref[...], _merge_rows(wg_ref[...])) + _dot(du_ref[...], _merge_rows(wu_ref[...]))
        x1 = x_ref[...]
        dx, dgf_rows = _norm_bwd(dh, x1, _rstd(x1), gf_ref[...])
        dx1 = dy_ref[...] + dx
        dx1_ref[...] = dx1
        dgf_ref[...] += _as_rows(jnp.sum(dgf_rows, axis=0, keepdims=True))
        mix = mix_ref[...]
        dmix, dgm_rows = _norm_bwd(dx1, mix, _rstd(mix), gm_ref[...])
        dmix_ref[...] = dmix.astype(BF16)
        dgm_ref[...] += _as_rows(jnp.sum(dgm_rows, axis=0, keepdims=True))
        pl.when(i == last)(finish)

    row = pl.BlockSpec((tm, d), lambda i: (i, 0))
    wide = pl.BlockSpec((tm, f), lambda i: (i, 0))
    gain = pl.BlockSpec((1, d), lambda i: (0, 0))
    gain_rows = pl.BlockSpec((d // 128, 128), lambda i: (0, 0))
    whole = pl.BlockSpec((N_DEV, n, d), lambda i: (0, 0, 0), pipeline_mode=pl.Buffered(1))
    out = pl.pallas_call(
        body, name="ffn_up_bwd", grid=(t // tm,),
        out_shape=[jax.ShapeDtypeStruct((t, d), F32), jax.ShapeDtypeStruct((t, d), BF16),
                   jax.ShapeDtypeStruct((d // 128, 128), F32), jax.ShapeDtypeStruct((d // 128, 128), F32)]
        + [jax.ShapeDtypeStruct(p.shape, p.dtype) for p in chip_parts],
        in_specs=[wide, wide, whole, whole, row, gain, row, row, gain] + [ANY] * na,
        out_specs=[row, row, gain_rows, gain_rows] + [ANY] * na,
        scratch_shapes=_chip_exchange_scratch(chip_parts),
        compiler_params=_params(collective_id=CHIP_COLLECTIVE_ID),
    )(dgate, dup, gate_t, up_t, x1, g_ffn, dy, mix, g_mix, *chip_parts)
    return out[:4], out[4:]


def outproj_bwd(dmix, w_out):
    t, d = dmix.shape
    tm = TOKEN_TILE

    def body(dm_ref, w_ref, out_ref):
        out_ref[...] = _dot_nt(dm_ref[...], w_ref[...])

    return pl.pallas_call(
        body, name="outproj_bwd", grid=(t // tm,),
        out_shape=jax.ShapeDtypeStruct((t, w_out.shape[0]), F32),
        in_specs=[pl.BlockSpec((tm, d), lambda i: (i, 0)), pl.BlockSpec(w_out.shape, lambda i: (0, 0))],
        out_specs=pl.BlockSpec((tm, w_out.shape[0]), lambda i: (i, 0)),
        compiler_params=_params(),
    )(dmix, w_out)


def mixers_bwd(proj, dcat, pooled, probs, p_sinks, w_pool, pool_scale, bucket, ffn_parts):
    t = proj.shape[0]
    nb = t // BLOCK
    na = len(ffn_parts)

    def body(*refs):
        (q_ref, kvc_ref, kvp_ref, dcat_ref, pooled_ref, p_all, psink_ref, wp_ref, sc_ref, bk_ref) = refs[:10]
        part_refs = refs[10:10 + na]
        dproj_ref, dbias_ref, dsink_ref, dwp_ref, dsc_ref, drb_ref = refs[10 + na:16 + na]
        slot_refs = refs[16 + na:16 + 2 * na]
        dbuf, c_u, c_q, c_kv, dp_all, ds_all, sink_acc = refs[16 + 2 * na:23 + 2 * na]
        send_sems, recv_sems, local_sems = refs[23 + 2 * na:26 + 2 * na]
        bounce = refs[26 + 2 * na:]
        i = pl.program_id(0)
        lane = lax.broadcasted_iota(jnp.int32, (1, 128), 1)
        start, finish = _chip_exchange_plan(part_refs, slot_refs, send_sems, recv_sems, local_sems, bounce)

        @pl.when(i == 0)
        def _():
            start()
            dbias_ref[...] = jnp.zeros_like(dbias_ref)
            dwp_ref[...] = jnp.zeros_like(dwp_ref)
            dsc_ref[...] = jnp.zeros_like(dsc_ref)
            dsink_ref[...] = jnp.zeros_like(dsink_ref)
            dbuf[...] = jnp.zeros_like(dbuf)
            c_u[...] = jnp.zeros_like(c_u)
            c_q[...] = jnp.zeros_like(c_q)
            c_kv[...] = jnp.zeros_like(c_kv)

        @pl.when(i < nb)
        def _():
            for g, w in enumerate(POOL_WINDOWS):
                cols = slice(g * POOL_GROUP_DIM, (g + 1) * POOL_GROUP_DIM)
                pooled = pooled_ref[:, cols]
                mixed = _dot(pooled, wp_ref[g])
                dout = dcat_ref[:, cols]
                dsc_ref[g:g + 1, :] += jnp.sum(dout * mixed, axis=0, keepdims=True)
                dmixed = (dout * sc_ref[:, cols]).astype(BF16)
                dwp_ref[g] += _dot_tn(pooled, dmixed)
                dpooled = _dot_nt(dmixed, wp_ref[g])
                scaled = dpooled * _inv_count(i, w)
                dbuf[BLOCK:, cols] = scaled[0:HALO]
                dproj_ref[:, cols] = (_window_sum(dbuf, g, w, lambda k: k) + c_u[:, cols]).astype(BF16)
                dbuf[0:BLOCK, cols] = scaled
                c_u[:, cols] = -dpooled

            kv = jnp.concatenate([kvp_ref[...], kvc_ref[...]], axis=0)
            k_var = _head_variants(kv[:, 0:2 * HEAD_DIM])
            v_var = _head_variants(kv[:, 2 * HEAD_DIM:])
            q2s = [q_ref[:, 2 * HEAD_DIM * j:2 * HEAD_DIM * (j + 1)].astype(BF16) for j in range(N_Q_HEADS // 2)]
            do2s = [dcat_ref[:, POOL_WIDTH + 2 * HEAD_DIM * j:POOL_WIDTH + 2 * HEAD_DIM * (j + 1)].astype(BF16)
                    for j in range(N_Q_HEADS // 2)]
            slot = _head_slot
            for hq in range(N_Q_HEADS):
                j, half, h = hq // 2, hq % 2, hq // GQA_GROUP
                dp_all[hq] = _dot_nt(do2s[j], v_var[h][half])
            sink_acc[...] = jnp.zeros_like(sink_acc)
            for hq in range(N_Q_HEADS):
                for r in range(0, BLOCK, ROW_CHUNK):
                    rows = slice(r, r + ROW_CHUNK)
                    probs = p_all[slot(hq), rows, :].astype(F32)
                    dp = dp_all[hq, rows, :]
                    delta = jnp.sum(probs * dp, axis=-1, keepdims=True)
                    ds = probs * (dp - delta)
                    dbias_ref[hq, rows, :] += ds
                    sink_acc[rows, :] += jnp.where(lane == hq, psink_ref[rows, :], 0.0) * delta
                    ds_all[slot(hq), rows, :] = (ds * ATTN_SCALE).astype(BF16)
            dsink_ref[...] -= jnp.sum(sink_acc[...], axis=0, keepdims=True)
            dq2 = [None] * (N_Q_HEADS // 2)
            for hq in range(N_Q_HEADS):
                j, half, h = hq // 2, hq % 2, hq // GQA_GROUP
                dq = _dot(ds_all[slot(hq)], k_var[h][half])
                dq2[j] = dq if dq2[j] is None else dq2[j] + dq
            low = lax.broadcasted_iota(jnp.int32, (2 * BLOCK, 2 * HEAD_DIM), 1) < HEAD_DIM
            dk_half, dv_half = [[None, None], [None, None]], [[None, None], [None, None]]
            for h in range(N_KV_HEADS):
                for half in range(2):
                    heads = [hq for hq in range(GQA_GROUP * h, GQA_GROUP * (h + 1)) if hq % 2 == half]
                    base = slot(heads[0])
                    q_rows = jnp.concatenate([q2s[hq // 2] for hq in heads], axis=0)
                    do_rows = jnp.concatenate([do2s[hq // 2] for hq in heads], axis=0)
                    dk_half[h][half] = _dot_tn(_merge_rows(ds_all[base:base + 2]), q_rows)
                    dv_half[h][half] = _dot_tn(_merge_rows(p_all[base:base + 2]), do_rows)

            def pair_of(halves):
                return jnp.where(low, halves[0][0] + pltpu.roll(halves[0][1], HEAD_DIM, 1),
                                 halves[1][1] + pltpu.roll(halves[1][0], HEAD_DIM, 1))

            dkv = jnp.concatenate([pair_of(dk_half), pair_of(dv_half)], axis=1)
            dproj_ref[:, POOL_WIDTH:2 * POOL_WIDTH] = c_q[...].astype(BF16)
            dproj_ref[:, 2 * POOL_WIDTH:] = (c_kv[...] + dkv[0:BLOCK]).astype(BF16)
            c_q[...] = jnp.concatenate(dq2, axis=1)
            c_kv[...] = dkv[BLOCK:]

        @pl.when(i == nb)
        def _():
            dbuf[BLOCK:, :] = jnp.zeros((HALO, POOL_WIDTH), F32)
            for g, w in enumerate(POOL_WINDOWS):
                cols = slice(g * POOL_GROUP_DIM, (g + 1) * POOL_GROUP_DIM)
                dproj_ref[:, cols] = (_window_sum(dbuf, g, w, lambda k: k) + c_u[:, cols]).astype(BF16)
            dproj_ref[:, POOL_WIDTH:2 * POOL_WIDTH] = c_q[...].astype(BF16)
            dproj_ref[:, 2 * POOL_WIDTH:] = c_kv[...].astype(BF16)
            bk = bk_ref[...]
            for h in range(N_Q_HEADS):
                db = dbias_ref[h]
                for b in range(N_BUCKETS):
                    drb_ref[h, b] = jnp.sum(jnp.where(bk == float(b), db, 0.0))
            finish()

    cur = lambda i: jnp.minimum(i, nb - 1)
    prv = lambda i: jnp.maximum(jnp.minimum(i, nb - 1) - 1, 0)
    out = pl.pallas_call(
        body, name="mixers_bwd", grid=(nb + 1,),
        out_shape=[jax.ShapeDtypeStruct((t, proj.shape[1]), BF16),
                   jax.ShapeDtypeStruct((N_Q_HEADS, BLOCK, 2 * BLOCK), F32),
                   jax.ShapeDtypeStruct((1, 128), F32),
                   jax.ShapeDtypeStruct((4, POOL_GROUP_DIM, POOL_GROUP_DIM), F32),
                   jax.ShapeDtypeStruct((len(POOL_WINDOWS), POOL_GROUP_DIM), F32),
                   jax.ShapeDtypeStruct((N_Q_HEADS, N_BUCKETS), F32)]
        + [jax.ShapeDtypeStruct(p.shape, p.dtype) for p in ffn_parts],
        in_specs=_mixer_in_specs(cur, prv)[2:]
        + [pl.BlockSpec((BLOCK, 2 * POOL_WIDTH), lambda i: (cur(i), 0)), pl.BlockSpec((BLOCK, POOL_WIDTH), lambda i: (cur(i), 0)),
           pl.BlockSpec((N_Q_HEADS, BLOCK, 2 * BLOCK), lambda i: (0, cur(i), 0)), pl.BlockSpec((BLOCK, 128), lambda i: (cur(i), 0))]
        + _mixer_param_specs()[2:] + [pl.BlockSpec((BLOCK, 2 * BLOCK), lambda i: (0, 0))] + [ANY] * na,
        out_specs=[pl.BlockSpec((BLOCK, proj.shape[1]), lambda i: (jnp.maximum(i - 1, 0), 0)),
                   pl.BlockSpec((N_Q_HEADS, BLOCK, 2 * BLOCK), lambda i: (0, 0, 0)),
                   pl.BlockSpec((1, 128), lambda i: (0, 0)),
                   pl.BlockSpec((4, POOL_GROUP_DIM, POOL_GROUP_DIM), lambda i: (0, 0, 0)),
                   pl.BlockSpec((len(POOL_WINDOWS), POOL_GROUP_DIM), lambda i: (0, 0)), SMEM] + [ANY] * na,
        scratch_shapes=[pltpu.VMEM((BLOCK + HALO, POOL_WIDTH), F32),
                        pltpu.VMEM((BLOCK, POOL_WIDTH), F32), pltpu.VMEM((BLOCK, POOL_WIDTH), F32),
                        pltpu.VMEM((BLOCK, 256), F32),
                        pltpu.VMEM((N_Q_HEADS, BLOCK, 2 * BLOCK), F32), pltpu.VMEM((N_Q_HEADS, BLOCK, 2 * BLOCK), BF16),
                        pltpu.VMEM((BLOCK, 128), F32)]
        + _chip_exchange_scratch(ffn_parts),
        compiler_params=_params(collective_id=CHIP_COLLECTIVE_ID),
    )(proj, proj, proj, dcat, pooled, probs, p_sinks, w_pool, pool_scale, bucket, *ffn_parts)
    return out[:6], out[6:]


def inproj_bwd(dproj, w_in_t, x, g, dx1):
    t, d = x.shape
    n = dproj.shape[1]
    tm = TOKEN_TILE

    def body(dp_ref, w_ref, x_ref, g_ref, dx1_ref, dx_ref, dg_ref):
        @pl.when(pl.program_id(0) == 0)
        def _():
            dg_ref[...] = jnp.zeros_like(dg_ref)

        dh = _dot(dp_ref[...], w_ref[...])
        xv = x_ref[...]
        dx, dg_rows = _norm_bwd(dh, xv, _rstd(xv), g_ref[...])
        dx_ref[...] = dx1_ref[...] + dx
        dg_ref[...] += _as_rows(jnp.sum(dg_rows, axis=0, keepdims=True))

    row = pl.BlockSpec((tm, d), lambda i: (i, 0))
    gain = pl.BlockSpec((1, d), lambda i: (0, 0))
    return pl.pallas_call(
        body, name="inproj_bwd", grid=(t // tm,),
        out_shape=[jax.ShapeDtypeStruct((t, d), F32), jax.ShapeDtypeStruct((d // 128, 128), F32)],
        in_specs=[pl.BlockSpec((tm, n), lambda i: (i, 0)), pl.BlockSpec(w_in_t.shape, lambda i: (0, 0)), row, gain, row],
        out_specs=[row, pl.BlockSpec((d // 128, 128), lambda i: (0, 0))],
        compiler_params=_params(),
    )(dproj, w_in_t, x, g, dx1)


def _bucket_band():
    qi = jnp.arange(BLOCK)[:, None]
    kj = jnp.arange(2 * BLOCK)[None, :]
    dist = qi + BLOCK - kj
    n = jnp.maximum(dist, 0)
    nf = jnp.maximum(n, 1).astype(F32)
    large = MAX_EXACT + (jnp.log(nf / MAX_EXACT) / np.float32(np.log(MAX_DISTANCE / MAX_EXACT))
                         * (N_BUCKETS - MAX_EXACT)).astype(jnp.int32)
    large = jnp.minimum(large, N_BUCKETS - 1)
    bucket = jnp.where(n < MAX_EXACT, n, large)
    in_window = (dist >= 0) & (dist < BLOCK)
    return bucket.astype(F32), in_window.astype(F32)


def kernel(x, g_pre_mix, w_in, w_pool, pool_scale, rel_bias, sinks, w_out, g_post_mix, g_pre_ffn, w_gate, w_up, w_down, g_post_ffn, loss_target, m_g_pre_mix, m_w_in, m_w_pool, m_pool_scale, m_rel_bias, m_sinks, m_w_out, m_g_post_mix, m_g_pre_ffn, m_w_gate, m_w_up, m_w_down, m_g_post_ffn, v_g_pre_mix, v_w_in, v_w_pool, v_pool_scale, v_rel_bias, v_sinks, v_w_out, v_g_post_mix, v_g_pre_ffn, v_w_gate, v_w_up, v_w_down, v_g_post_ffn):
    d = x.shape[-1]
    xs, target = x[0], loss_target[0]

    w_in_ts = w_in[0].T.astype(BF16)
    w_out_s = w_out[0].astype(BF16)
    gate_ts = w_gate[0].T.astype(BF16)
    up_ts = w_up[0].T.astype(BF16)
    w_down_s = w_down[0].astype(BF16)

    bucket, in_window = _bucket_band()
    biasm = bias_band(bucket, in_window, rel_bias)
    w_pool_b = w_pool[0].astype(BF16)
    half = up_ts.shape[0] // 2
    proj, h1, w_in_t, up_t = norm_inproj(xs, g_pre_mix, w_in_ts, up_ts[:half], up_ts.shape[0])
    w_in_t = w_in_t.reshape(-1, d)
    cat, pooled, probs, p_sinks, gate_t, w_out_f = mixers_fwd(proj, biasm, sinks, w_pool_b, pool_scale, [gate_ts, w_out_s])
    w_out_f = w_out_f.reshape(-1, d)
    mix, x1, h2, up_t = outproj_norm(cat, w_out_f, xs, g_post_mix, g_pre_ffn, up_ts[half:], up_t)
    gate, up, act, w_down_f = ffn_up(h2, gate_t, up_t, w_down_s)
    df, dy, dg_post_ffn, loss_part = ffn_down_loss(act, w_down_f, x1, g_post_ffn, target)

    def pair_sum(parts, tag):
        return pair_add(parts, pair_exchange(parts, "pair_exchange_" + tag), "pair_add_" + tag)

    dgate, dup = ffn_down_bwd(df, w_down_f, gate, up)
    (d_gate, d_up), _ = grad_ffn([dgate, dup], h2, "grad_w_gate_up")
    (d_down,), got_gate_up = grad_ffn([act], df, "grad_w_down", [d_gate, d_up])
    q_gate, q_up, q_down = pair_add(
        [d_gate, d_up, d_down], [*got_gate_up, *pair_exchange([d_down], "pair_exchange_down")], "pair_add_ffn")
    (dx1, dmix, dg_pre_ffn, dg_post_mix), (gate_slots, down_slots) = ffn_up_bwd(
        dgate, dup, gate_t, up_t, x1, g_pre_ffn, dy, mix, g_post_mix, [q_gate, q_down])
    dcat = outproj_bwd(dmix, w_out_f)
    d_out = grad_rows(cat, dmix, "grad_w_out", by_core=True)
    q_out, = pair_sum([d_out], "out")
    (dproj, _, dsinks, dw_pool, dpool_scale, drel_bias), (up_slots, out_slots) = mixers_bwd(
        proj, dcat, pooled, probs, p_sinks, w_pool_b, pool_scale, bucket, [q_up, q_out])
    grad_x, dg_pre_mix = inproj_bwd(dproj, w_in_t, xs, g_pre_mix, dx1)

    small_w = [g_pre_mix, g_post_mix, g_pre_ffn, g_post_ffn, pool_scale, sinks, w_pool, rel_bias.T]
    small_m = [m_g_pre_mix, m_g_post_mix, m_g_pre_ffn, m_g_post_ffn, m_pool_scale, m_sinks, m_w_pool, m_rel_bias.T]
    small_v = [v_g_pre_mix, v_g_post_mix, v_g_pre_ffn, v_g_post_ffn, v_pool_scale, v_sinks, v_w_pool, v_rel_bias.T]
    d_in_t, total, total_rb = grad_w_in_small_reduce(
        dproj, h1, [dg_pre_mix, dg_post_mix, dg_pre_ffn, dg_post_ffn], dpool_scale, dsinks, loss_part, dw_pool, drel_bias)
    g_in_t = reduce_w_in(d_in_t)
    loss_row, sm = small_adamw(total, total_rb, small_w, small_m, small_v)
    sm[7] = [r.T for r in sm[7]]
    big_w = [w_in[0].T, w_out[0], w_gate[0].T, w_up[0].T, w_down[0]]
    big_m = [m_w_in[0].T, m_w_out[0], m_w_gate[0].T, m_w_up[0].T, m_w_down[0]]
    big_v = [v_w_in[0].T, v_w_out[0], v_w_gate[0].T, v_w_up[0].T, v_w_down[0]]
    upd = sum_adamw([out_slots, gate_slots, up_slots, down_slots], big_w[1:], big_m[1:], big_v[1:], "sum_adamw")
    upd = [[g_in_t, *adamw_update(big_w[:1], [g_in_t], big_m[:1], big_v[:1], "adamw_in")[0]], *upd]
    back = lambda k, a: (a.T if k in (0, 2, 3) else a)[None]
    big = [[back(k, u) for u in upd[k]] for k in range(5)]

    def ordered(kind):
        s, b = [p[kind] for p in sm], [p[kind] for p in big]
        return [s[0], b[0], s[6], s[4], s[7], s[5], b[1], s[1], s[2], b[2], b[3], b[4], s[3]]

    return (loss_row[0, 0], grad_x[None], *ordered(0), *ordered(1), *ordered(2), *ordered(3))
```

```python
import numpy as np
import jax
import jax.numpy as jnp
from jax import lax
from jax.experimental import pallas as pl
from jax.experimental.pallas import tpu as pltpu

F32 = jnp.float32
BF16 = jnp.bfloat16

N_DEV = 8
N_CHIP = 4
POOL_WIDTH = 512
POOL_WINDOWS = (2, 4, 8, 16)
POOL_GROUP_DIM = 128
HEAD_DIM = 64
N_Q_HEADS = 8
N_KV_HEADS = 2
GQA_GROUP = 4
BLOCK = 128
HALO = 16
ROW_CHUNK = 32
N_BUCKETS = 32
MAX_EXACT = 16
MAX_DISTANCE = 128
EPS = 1e-6
NEG_INF = -1e30
ATTN_SCALE = float(1.0 / np.sqrt(np.float32(HEAD_DIM)))

ADAM_LR = 0.001
ADAM_B1 = 0.9
ADAM_B2 = 0.999
ADAM_EPS = 1e-08
ADAM_WD = 0.01
ADAM_STEP = 10

TOKEN_TILE = 1024
WIDE_K_TOKEN_TILE = 512
FFN_TOKEN_TILE = 1024
FF_SHARDS_PER_TILE = 4
VMEM_LIMIT = 56 * 1024 * 1024
MESH = pl.DeviceIdType.MESH
PAIR_COLLECTIVE_ID = 0
GATHER_COLLECTIVE_ID = 1
CHIP_COLLECTIVE_ID = 2
ANY = pl.BlockSpec(memory_space=pl.ANY)
VMEM = pl.BlockSpec(memory_space=pltpu.VMEM)
SMEM = pl.BlockSpec(memory_space=pltpu.SMEM)


def _params(**kw):
    return pltpu.CompilerParams(vmem_limit_bytes=VMEM_LIMIT, **kw)


def _dot(a, b):
    return jnp.dot(a, b, preferred_element_type=F32)


def _dot_nt(a, b):
    return lax.dot_general(a, b, (((1,), (1,)), ((), ())), preferred_element_type=F32)


def _dot_tn(a, b):
    return lax.dot_general(a, b, (((0,), (0,)), ((), ())), preferred_element_type=F32)


def _rstd(v):
    return lax.rsqrt(jnp.mean(v * v, axis=-1, keepdims=True) + EPS)


def _norm_bwd(dout, v, r, g):
    vn = v * r
    dn = dout * g
    dv = r * (dn - vn * jnp.mean(dn * vn, axis=-1, keepdims=True))
    return dv, dout * vn


def _as_rows(v):
    return jnp.concatenate([v[:, k:k + 128] for k in range(0, v.shape[1], 128)], axis=0)


def _as_lanes(rows):
    return jnp.concatenate([rows[k:k + 1, :] for k in range(rows.shape[0])], axis=1)


def _handshake(peers):
    barrier = pltpu.get_barrier_semaphore()
    for peer in peers:
        pl.semaphore_signal(barrier, inc=1, device_id=peer, device_id_type=MESH)
    pl.semaphore_wait(barrier, len(peers))


def _merge_rows(value):
    s, r, c_ = value.shape
    return value.reshape(s * r, c_)


def _gather_plan(srcs, outs, send_sems, recv_sems, local_sems=None, bounce=None, rows=None, shake=True):
    n = len(srcs)
    x, y, c = lax.axis_index("x"), lax.axis_index("y"), lax.axis_index("c")
    me, sibling = (x, y, c), (x, y, 1 - c)
    chips = [(1 - x, y), (x, 1 - y), (1 - x, 1 - y)]

    def slot(a, px, py, pc):
        whole = outs[a].at[4 * px + 2 * py + pc]
        return whole if rows is None or rows[a] is None else whole.at[pl.ds(*rows[a])]

    def copy(a, k, block, to, from_src=False):
        return pltpu.make_async_remote_copy(
            src_ref=srcs[a] if from_src else slot(a, *block), dst_ref=slot(a, *block),
            send_sem=send_sems.at[k * n + a], recv_sem=recv_sems.at[k * n + a], device_id=to, device_id_type=MESH)

    def own_in(a):
        return pltpu.make_async_copy(srcs[a], bounce[a], local_sems.at[a])

    def own_out(a):
        return pltpu.make_async_copy(bounce[a], slot(a, *me), local_sems.at[a])

    def first(a):
        return [copy(a, 0, me, sibling, True)] + [copy(a, 1 + j, me, (*chip, c), True) for j, chip in enumerate(chips)]

    def passed(a, j):
        return copy(a, 4 + j, (*chips[j], c), sibling)

    def start():
        if shake:
            _handshake([sibling] + [(*chip, c) for chip in chips])
        for a in range(n):
            if bounce is not None:
                own_in(a).start()
            for cp in first(a):
                cp.start()

    def forward():
        if bounce is not None:
            for a in range(n):
                own_in(a).wait()
                own_out(a).start()
        for j, chip in enumerate(chips):
            for a in range(n):
                copy(a, 1 + j, (*chip, c), me).wait_recv()
                passed(a, j).start()

    def complete():
        for a in range(n):
            copy(a, 0, sibling, me).wait_recv()
            for j, chip in enumerate(chips):
                copy(a, 4 + j, (*chip, 1 - c), me).wait_recv()
        for a in range(n):
            for cp in first(a) + [passed(a, j) for j in range(3)]:
                cp.wait_send()
            if bounce is not None:
                own_out(a).wait()

    def finish():
        forward()
        complete()

    finish.forward, finish.complete = forward, complete
    return start, finish


def _gather_scratch(shards):
    n = len(shards)
    return [pltpu.SemaphoreType.DMA((7 * n,)), pltpu.SemaphoreType.DMA((7 * n,)), pltpu.SemaphoreType.DMA((n,))] \
        + [pltpu.VMEM(s.shape, s.dtype) for s in shards]


def _chip_exchange_plan(srcs, outs, send_sems, recv_sems, local_sems, bounce):
    n = len(srcs)
    x, y, c = lax.axis_index("x"), lax.axis_index("y"), lax.axis_index("c")
    my_chip = 2 * x + y

    def copies():
        out = []
        for a in range(n):
            for k in range(1, N_CHIP):
                px, py = x ^ (k >> 1), y ^ (k & 1)
                out.append(pltpu.make_async_remote_copy(
                    src_ref=srcs[a].at[2 * px + py], dst_ref=outs[a].at[my_chip],
                    send_sem=send_sems.at[(k - 1) * n + a], recv_sem=recv_sems.at[(k - 1) * n + a],
                    device_id=(px, py, c), device_id_type=MESH))
        return out

    def own_in(a):
        return pltpu.make_async_copy(srcs[a].at[my_chip], bounce[a], local_sems.at[a])

    def own_out(a):
        return pltpu.make_async_copy(bounce[a], outs[a].at[my_chip], local_sems.at[a])

    def start():
        _handshake([(x ^ (k >> 1), y ^ (k & 1), c) for k in range(1, N_CHIP)])
        for a in range(n):
            own_in(a).start()
        for cp in copies():
            cp.start()

    def finish():
        for a in range(n):
            own_in(a).wait()
            own_out(a).start()
        for cp in copies():
            cp.wait()
        for a in range(n):
            own_out(a).wait()

    return start, finish


def _chip_exchange_scratch(parts):
    n = len(parts)
    return [pltpu.SemaphoreType.DMA((3 * n,)), pltpu.SemaphoreType.DMA((3 * n,)), pltpu.SemaphoreType.DMA((n,))] \
        + [pltpu.VMEM(p.shape[1:], p.dtype) for p in parts]


def _pair_plan(srcs, outs, send_sems, recv_sems):
    x, y, c = lax.axis_index("x"), lax.axis_index("y"), lax.axis_index("c")

    def copies():
        return [pltpu.make_async_remote_copy(
            src_ref=srcs[a].at[1 - c], dst_ref=outs[a], send_sem=send_sems.at[a], recv_sem=recv_sems.at[a],
            device_id=(x, y, 1 - c), device_id_type=MESH) for a in range(len(srcs))]

    def start():
        _handshake([(x, y, 1 - c)])
        for cp in copies():
            cp.start()

    def finish():
        for cp in copies():
            cp.wait()

    return start, finish


def pair_exchange(parts, name):
    n = len(parts)

    def body(*refs):
        start, finish = _pair_plan(refs[:n], refs[n:2 * n], *refs[2 * n:])
        start()
        finish()

    return pl.pallas_call(
        body, name=name, out_shape=[jax.ShapeDtypeStruct(p.shape[1:], p.dtype) for p in parts],
        in_specs=[ANY] * n, out_specs=[ANY] * n,
        scratch_shapes=[pltpu.SemaphoreType.DMA((n,)), pltpu.SemaphoreType.DMA((n,))],
        compiler_params=_params(collective_id=PAIR_COLLECTIVE_ID),
    )(*parts)


def pair_add(parts, got, name):
    n = len(parts)

    def body(core_ref, *refs):
        for a in range(n):
            refs[2 * n + a][...] = (refs[a][...].astype(F32) + refs[n + a][...].astype(F32)).astype(BF16)

    def own(p):
        zeros = (0,) * (p.ndim - 2)
        return pl.BlockSpec((None, 1, *p.shape[2:]), lambda i, core: (core[0], i, *zeros))

    def plain(p):
        zeros = (0,) * (p.ndim - 1)
        return pl.BlockSpec((1, *p.shape[1:]), lambda i, core: (i, *zeros))

    core = lax.axis_index("c").astype(jnp.int32).reshape(1)
    return pl.pallas_call(
        body, name=name,
        grid_spec=pltpu.PrefetchScalarGridSpec(
            num_scalar_prefetch=1, grid=(got[0].shape[0],),
            in_specs=[own(p) for p in parts] + [plain(p) for p in got], out_specs=[plain(p) for p in got]),
        out_shape=[jax.ShapeDtypeStruct(p.shape, BF16) for p in got],
        compiler_params=_params(),
    )(core, *parts, *got)


def _adamw(w, g, m, v):
    m2 = ADAM_B1 * m + (1.0 - ADAM_B1) * g
    v2 = ADAM_B2 * v + (1.0 - ADAM_B2) * (g * g)
    m_hat = m2 / (1.0 - ADAM_B1 ** ADAM_STEP)
    v_hat = v2 / (1.0 - ADAM_B2 ** ADAM_STEP)
    delta = -ADAM_LR * (m_hat / (jnp.sqrt(v_hat) + ADAM_EPS) + ADAM_WD * w)
    return delta, m2, v2


def sum_adamw(slots, ws, ms, vs, name):
    n = len(ws)
    halves = 2

    def body(*refs):
        for a in range(n):
            total = refs[a][0].astype(F32)
            for s in range(1, slots[a].shape[0]):
                total = total + refs[a][s].astype(F32)
            delta, m2, v2 = _adamw(refs[n + a][...], total, refs[2 * n + a][...], refs[3 * n + a][...])
            for q, val in enumerate((total, delta, m2, v2)):
                refs[4 * n + 4 * a + q][...] = val

    def rows(w):
        return pl.BlockSpec((w.shape[0] // halves, w.shape[1]), lambda i: (i, 0))

    def slot_rows(p):
        return pl.BlockSpec((p.shape[0], p.shape[1] // halves, p.shape[2]), lambda i: (0, i, 0))

    out = pl.pallas_call(
        body, name=name, grid=(halves,),
        out_shape=[jax.ShapeDtypeStruct(w.shape, F32) for w in ws for _ in range(4)],
        in_specs=[slot_rows(p) for p in slots] + [rows(w) for w in ws] * 3,
        out_specs=[rows(w) for w in ws for _ in range(4)],
        compiler_params=_params(),
    )(*slots, *ws, *ms, *vs)
    return [out[4 * a:4 * a + 4] for a in range(n)]


def adamw_update(ws, gs, ms, vs, name):
    n = len(ws)

    def body(*refs):
        for a in range(n):
            delta, m2, v2 = _adamw(refs[a][...], refs[n + a][...], refs[2 * n + a][...], refs[3 * n + a][...])
            refs[4 * n + 3 * a][...] = delta
            refs[4 * n + 3 * a + 1][...] = m2
            refs[4 * n + 3 * a + 2][...] = v2

    out = pl.pallas_call(
        body, name=name,
        out_shape=[jax.ShapeDtypeStruct(w.shape, F32) for w in ws for _ in range(3)],
        in_specs=[VMEM] * (4 * n), out_specs=[VMEM] * (3 * n),
        compiler_params=_params(),
    )(*ws, *gs, *ms, *vs)
    return [out[3 * a:3 * a + 3] for a in range(n)]


GAIN_ROWS = 8
ROW_POOL_SCALE = 4 * GAIN_ROWS
ROW_SINKS = ROW_POOL_SCALE + 4
ROW_LOSS = ROW_SINKS + 1
ROW_W_POOL = 40
SMALL_ROWS = ROW_W_POOL + 4 * POOL_GROUP_DIM


def grad_w_in_small_reduce(a, b, gains, dpool_scale, dsinks, loss_part, dw_pool, drel_bias):
    t, m = a.shape
    d = b.shape[1]
    r = m // N_DEV
    tt = TOKEN_TILE
    last = t // tt - 1

    def body(a_ref, b_ref, g0, g1, g2, g3, dsc_ref, dsink_ref, loss_ref, dwp_ref, drb_ref, out_ref, total_ref, total_rb_ref,
             acc, stage, gat, gat_rb, g_send, g_recv):
        k = pl.program_id(0)
        x, y, c = lax.axis_index("x"), lax.axis_index("y"), lax.axis_index("c")
        start, finish = _gather_plan([stage, drb_ref], [gat, gat_rb], g_send, g_recv)

        @pl.when(k == 0)
        def _():
            for q, g_ref in enumerate((g0, g1, g2, g3)):
                stage[GAIN_ROWS * q:GAIN_ROWS * (q + 1), :] = g_ref[...]
            stage[ROW_POOL_SCALE:ROW_SINKS, :] = dsc_ref[...]
            stage[ROW_SINKS:ROW_LOSS, :] = dsink_ref[...]
            stage[ROW_LOSS:ROW_LOSS + 1, :] = loss_ref[...]
            stage[ROW_LOSS + 1:ROW_W_POOL, :] = jnp.zeros((ROW_W_POOL - ROW_LOSS - 1, 128), F32)
            stage[ROW_W_POOL:, :] = dwp_ref[...].reshape(4 * POOL_GROUP_DIM, POOL_GROUP_DIM)
            gat[4 * x + 2 * y + c] = stage[...]
            gat_rb[4 * x + 2 * y + c] = drb_ref[...]
            start()
            acc[...] = jnp.zeros_like(acc)

        acc[...] += _dot_tn(a_ref[...], b_ref[...])

        @pl.when(k == last)
        def _():
            blocks = acc[...].reshape(N_CHIP, 2, r, d)
            for chip in range(N_CHIP):
                for core in range(2):
                    out_ref[core, chip] = blocks[chip, core].astype(BF16)
            finish()
            total, total_rb = gat[0], gat_rb[0]
            for s in range(1, N_DEV):
                total, total_rb = total + gat[s], total_rb + gat_rb[s]
            total_ref[...] = total
            total_rb_ref[...] = total_rb

    out_shape = (2, N_CHIP, r, d)
    return pl.pallas_call(
        body, name="grad_w_in", grid=(t // tt,),
        out_shape=[jax.ShapeDtypeStruct(out_shape, BF16), jax.ShapeDtypeStruct((SMALL_ROWS, 128), F32),
                   jax.ShapeDtypeStruct(drel_bias.shape, F32)],
        in_specs=[pl.BlockSpec((tt, m), lambda k: (k, 0)), pl.BlockSpec((tt, d), lambda k: (k, 0))] + [VMEM] * 9,
        out_specs=[pl.BlockSpec(out_shape, lambda k: (0,) * len(out_shape)), VMEM, VMEM],
        scratch_shapes=[pltpu.VMEM((m, d), F32), pltpu.VMEM((SMALL_ROWS, 128), F32),
                        pltpu.VMEM((N_DEV, SMALL_ROWS, 128), F32), pltpu.VMEM((N_DEV, *drel_bias.shape), F32),
                        pltpu.SemaphoreType.DMA((14,)), pltpu.SemaphoreType.DMA((14,))],
        compiler_params=_params(collective_id=GATHER_COLLECTIVE_ID),
    )(a, b, *gains, dpool_scale, dsinks, loss_part, dw_pool, drel_bias)


def reduce_w_in(d_in_t):
    def body(d_in_ref, g_in_ref, pair_got, chip_part, chip_got, p_send, p_recv, x_send, x_recv):
        x, y, c = lax.axis_index("x"), lax.axis_index("y"), lax.axis_index("c")
        my_chip = 2 * x + y
        _handshake([(x, y, 1 - c)] + [(x ^ (k >> 1), y ^ (k & 1), c) for k in range(1, N_CHIP)])
        pair = pltpu.make_async_remote_copy(
            src_ref=d_in_ref.at[1 - c], dst_ref=pair_got, send_sem=p_send, recv_sem=p_recv,
            device_id=(x, y, 1 - c), device_id_type=MESH)
        pair.start()
        pair.wait()
        chip_part[...] = (d_in_ref[c].astype(F32) + pair_got[...].astype(F32)).astype(BF16)
        copies = []
        for k in range(1, N_CHIP):
            px, py = x ^ (k >> 1), y ^ (k & 1)
            copies.append(pltpu.make_async_remote_copy(
                src_ref=chip_part.at[2 * px + py], dst_ref=chip_got.at[my_chip],
                send_sem=x_send.at[k - 1], recv_sem=x_recv.at[k - 1], device_id=(px, py, c), device_id_type=MESH))
        for cp in copies:
            cp.start()
        chip_got[my_chip] = chip_part[my_chip]
        for cp in copies:
            cp.wait()
        g_in = chip_got[0].astype(F32)
        for s in range(1, N_CHIP):
            g_in = g_in + chip_got[s].astype(F32)
        g_in_ref[...] = g_in

    per_core = d_in_t.shape[1:]
    return pl.pallas_call(
        body, name="reduce_w_in",
        out_shape=jax.ShapeDtypeStruct(d_in_t.shape[2:], F32),
        in_specs=[VMEM], out_specs=VMEM,
        scratch_shapes=[pltpu.VMEM(per_core, d_in_t.dtype), pltpu.VMEM(per_core, d_in_t.dtype),
                        pltpu.VMEM(per_core, d_in_t.dtype),
                        pltpu.SemaphoreType.DMA, pltpu.SemaphoreType.DMA,
                        pltpu.SemaphoreType.DMA((3,)), pltpu.SemaphoreType.DMA((3,))],
        compiler_params=_params(collective_id=GATHER_COLLECTIVE_ID),
    )(d_in_t)


def small_adamw(total, total_rb, small_w, small_m, small_v):
    n_small = len(small_w)

    def body(*refs):
        total_ref, rb_ref = refs[:2]
        w_refs, m_refs, v_refs = (refs[2 + k * n_small:2 + (k + 1) * n_small] for k in range(3))
        loss_out = refs[2 + 3 * n_small]
        result = refs[3 + 3 * n_small:]
        total = total_ref[...]
        loss_out[...] = total[ROW_LOSS:ROW_LOSS + 1, :]
        grads = [_as_lanes(total[GAIN_ROWS * k:GAIN_ROWS * (k + 1), :]) for k in range(4)]
        grads.append(_as_lanes(total[ROW_POOL_SCALE:ROW_SINKS, :]))
        grads.append(total[ROW_SINKS:ROW_LOSS, 0:N_Q_HEADS])
        grads.append(total[ROW_W_POOL:, :].reshape(w_refs[6].shape))
        grads.append(rb_ref[...])
        for k in range(n_small):
            delta, m2, v2 = _adamw(w_refs[k][...], grads[k], m_refs[k][...], v_refs[k][...])
            result[4 * k][...] = grads[k]
            result[4 * k + 1][...] = delta
            result[4 * k + 2][...] = m2
            result[4 * k + 3][...] = v2

    out = pl.pallas_call(
        body, name="small_adamw",
        out_shape=[jax.ShapeDtypeStruct((1, 128), F32)] + [jax.ShapeDtypeStruct(w.shape, F32) for w in small_w for _ in range(4)],
        in_specs=[VMEM] * (2 + 3 * n_small), out_specs=[VMEM] * (1 + 4 * n_small),
        compiler_params=_params(),
    )(total, total_rb, *small_w, *small_m, *small_v)
    return out[0], [out[1 + 4 * k:5 + 4 * k] for k in range(n_small)]


def norm_inproj(x, g, w_shard, shard, shard_rows):
    t, d = x.shape
    r = w_shard.shape[0]
    tm = TOKEN_TILE
    nt = t // tm

    def body(x_ref, g_ref, w_shard_ref, shard_ref, proj_ref, h_ref, w_ref, gathered_ref, h_all, w_all, w_sem,
             send_w, recv_w, local_w, bounce_w, send_sems, recv_sems, local_sems, bounce):
        i = pl.program_id(0)
        start_w, finish_w = _gather_plan([w_shard_ref], [w_ref], send_w, recv_w, local_w, [bounce_w])
        start, finish = _gather_plan([shard_ref], [gathered_ref], send_sems, recv_sems, local_sems, [bounce],
                                     [(0, shard.shape[0])], shake=False)

        @pl.when(i == 0)
        def _():
            start_w()
            start()

        @pl.when(i < nt)
        def _():
            xv = x_ref[...]
            h = ((xv * _rstd(xv)) * g_ref[...]).astype(BF16)
            h_ref[...] = h
            h_all[pl.ds(pl.multiple_of(i * tm, tm), tm), :] = h

        @pl.when(i == nt - 1)
        def _():
            finish_w()
            landed = pltpu.make_async_copy(w_ref, w_all, w_sem)
            landed.start()
            landed.wait()

        @pl.when(i >= nt)
        def _():
            rows = pl.ds(pl.multiple_of((i - nt) * tm, tm), tm)
            proj_ref[...] = _dot_nt(h_all[rows, :], _merge_rows(w_all[...]))

        pl.when(i == 2 * nt - 2)(finish.forward)
        pl.when(i == 2 * nt - 1)(finish.complete)

    first = lambda i: (jnp.minimum(i, nt - 1), 0)
    return pl.pallas_call(
        body, name="norm_inproj", grid=(2 * nt,),
        out_shape=[jax.ShapeDtypeStruct((t, N_DEV * r), F32), jax.ShapeDtypeStruct((t, d), BF16),
                   jax.ShapeDtypeStruct((N_DEV, r, d), w_shard.dtype),
                   jax.ShapeDtypeStruct((N_DEV, shard_rows, d), shard.dtype)],
        in_specs=[pl.BlockSpec((tm, d), first), pl.BlockSpec((1, d), lambda i: (0, 0)), ANY, ANY],
        out_specs=[pl.BlockSpec((tm, N_DEV * r), lambda i: (jnp.maximum(i - nt, 0), 0)), pl.BlockSpec((tm, d), first),
                   ANY, ANY],
        scratch_shapes=[pltpu.VMEM((t, d), BF16), pltpu.VMEM((N_DEV, r, d), w_shard.dtype), pltpu.SemaphoreType.DMA]
        + _gather_scratch([w_shard]) + _gather_scratch([shard]),
        compiler_params=_params(collective_id=GATHER_COLLECTIVE_ID),
    )(x, g, w_shard, shard)


def _fill_bias_band(bk_ref, win_ref, rb_ref, biasm_ref):
    bk = bk_ref[...]
    keep = win_ref[...] > 0.5
    for h in range(N_Q_HEADS):
        acc = jnp.zeros(bk.shape, F32)
        for b in range(N_BUCKETS):
            acc = jnp.where(bk == float(b), rb_ref[b, h], acc)
        biasm_ref[h] = jnp.where(keep, acc, NEG_INF)


def _window_sum(buf_ref, g, w, first):
    cols = slice(g * POOL_GROUP_DIM, (g + 1) * POOL_GROUP_DIM)
    acc = None
    for k in range(w):
        piece = buf_ref[first(k):first(k) + BLOCK, cols]
        acc = piece if acc is None else acc + piece
    return acc


def _inv_count(i, w):
    row = lax.broadcasted_iota(jnp.int32, (BLOCK, 1), 0)
    return 1.0 / jnp.minimum(i * BLOCK + row + 1, w).astype(F32)


def _fill_pool_input(i, ubuf, uc_ref, halo_ref):
    ubuf[0:HALO, :] = jnp.where(i > 0, halo_ref[...], 0.0)
    ubuf[HALO:, :] = uc_ref[...]


def _pooled(i, g, w, ubuf):
    cols = slice(g * POOL_GROUP_DIM, (g + 1) * POOL_GROUP_DIM)
    return _window_sum(ubuf, g, w, lambda k: HALO - k) * _inv_count(i, w) - ubuf[HALO:, cols]


def _head_variants(pair):
    low = lax.broadcasted_iota(jnp.int32, pair.shape, 1) < HEAD_DIM
    swapped = pltpu.roll(pair, HEAD_DIM, 1)
    zero = jnp.zeros_like(pair)
    pick = lambda c, a, b: jnp.where(c, a, b).astype(BF16)
    return [[pick(low, pair, zero), pick(low, zero, swapped)], [pick(low, swapped, zero), pick(low, zero, pair)]]


def _head_probs(i, hq, rows, s_ref, biasm_ref, sinks_ref):
    s = s_ref[hq, rows, :] * ATTN_SCALE + biasm_ref[hq, rows, :]
    col = lax.broadcasted_iota(jnp.int32, s.shape, 1)
    s = jnp.where((i == 0) & (col < BLOCK), NEG_INF, s)
    sink = sinks_ref[0, hq]
    m = jnp.maximum(jnp.max(s, axis=-1, keepdims=True), sink)
    p = jnp.exp(s - m)
    e_sink = jnp.exp(sink - m)
    inv = 1.0 / (jnp.sum(p, axis=-1, keepdims=True) + e_sink)
    return p * inv, e_sink * inv


def _head_slot(hq):
    return 4 * (hq // GQA_GROUP) + 2 * (hq % 2) + (hq % GQA_GROUP) // 2


def _mixer_in_specs(cur, prv):
    return [pl.BlockSpec((BLOCK, 512), lambda i: (cur(i), 0)),
            pl.BlockSpec((HALO, 512), lambda i: (jnp.maximum(cur(i) * (BLOCK // HALO) - 1, 0), 0)),
            pl.BlockSpec((BLOCK, 512), lambda i: (cur(i), 1)),
            pl.BlockSpec((BLOCK, 256), lambda i: (cur(i), 4)),
            pl.BlockSpec((BLOCK, 256), lambda i: (prv(i), 4))]


def _mixer_param_specs():
    return [pl.BlockSpec((4, POOL_GROUP_DIM, POOL_GROUP_DIM), lambda i: (0, 0, 0)),
            pl.BlockSpec((1, POOL_WIDTH), lambda i: (0, 0))]


def mixers_fwd(proj, bucket, in_window, rel_bias, sinks, w_pool, pool_scale, shards):
    t = proj.shape[0]
    nb = t // BLOCK
    ns = len(shards)

    def body(*refs):
        uc_ref, halo_ref, q_ref, kvc_ref, kvp_ref, bk_ref, win_ref, rb_ref, sinks_ref, wp_ref, sc_ref = refs[:11]
        shard_refs = refs[11:11 + ns]
        out_ref, pooled_ref, p_all, psink_ref = refs[11 + ns:15 + ns]
        gathered_refs = refs[15 + ns:15 + 2 * ns]
        ubuf, s_all, biasm_ref, send_sems, recv_sems, local_sems = refs[15 + 2 * ns:21 + 2 * ns]
        i = pl.program_id(0)
        start, finish = _gather_plan(shard_refs, gathered_refs, send_sems, recv_sems, local_sems, refs[21 + 2 * ns:])

        @pl.when(i == 0)
        def _():
            start()
            _fill_bias_band(bk_ref, win_ref, rb_ref, biasm_ref)

        _fill_pool_input(i, ubuf, uc_ref, halo_ref)
        for g, w in enumerate(POOL_WINDOWS):
            cols = slice(g * POOL_GROUP_DIM, (g + 1) * POOL_GROUP_DIM)
            pooled = _pooled(i, g, w, ubuf).astype(BF16)
            pooled_ref[:, cols] = pooled
            out_ref[:, cols] = (_dot(pooled, wp_ref[g]) * sc_ref[:, cols]).astype(BF16)
        kv = jnp.concatenate([kvp_ref[...], kvc_ref[...]], axis=0)
        k_var = _head_variants(kv[:, 0:2 * HEAD_DIM])
        v_var = _head_variants(kv[:, 2 * HEAD_DIM:])
        for hq in range(N_Q_HEADS):
            j, half, h = hq // 2, hq % 2, hq // GQA_GROUP
            q2 = q_ref[:, 2 * HEAD_DIM * j:2 * HEAD_DIM * (j + 1)].astype(BF16)
            s_all[hq] = _dot_nt(q2, k_var[h][half])
        psink_ref[...] = jnp.zeros_like(psink_ref)
        for hq in range(N_Q_HEADS):
            for r in range(0, BLOCK, ROW_CHUNK):
                rows = slice(r, r + ROW_CHUNK)
                probs, p_sink = _head_probs(i, hq, rows, s_all, biasm_ref, sinks_ref)
                p_all[_head_slot(hq), rows, :] = probs.astype(BF16)
                psink_ref[rows, hq:hq + 1] = p_sink
        for j in range(N_Q_HEADS // 2):
            h = 2 * j // GQA_GROUP
            acc = _dot(p_all[_head_slot(2 * j)], v_var[h][0]) + _dot(p_all[_head_slot(2 * j + 1)], v_var[h][1])
            out_ref[:, POOL_WIDTH + 2 * HEAD_DIM * j:POOL_WIDTH + 2 * HEAD_DIM * (j + 1)] = acc.astype(BF16)

        pl.when(i == max(nb - 4, 0))(finish.forward)
        pl.when(i == nb - 1)(finish.complete)

    return pl.pallas_call(
        body, name="mixers_fwd", grid=(nb,),
        out_shape=[jax.ShapeDtypeStruct((t, 2 * POOL_WIDTH), BF16), jax.ShapeDtypeStruct((t, POOL_WIDTH), BF16),
                   jax.ShapeDtypeStruct((N_Q_HEADS, t, 2 * BLOCK), BF16), jax.ShapeDtypeStruct((t, 128), F32)]
        + [jax.ShapeDtypeStruct((N_DEV, *sh.shape), sh.dtype) for sh in shards],
        in_specs=_mixer_in_specs(lambda i: i, lambda i: jnp.maximum(i - 1, 0))
        + [pl.BlockSpec((BLOCK, 2 * BLOCK), lambda i: (0, 0))] * 2 + [SMEM, SMEM] + _mixer_param_specs() + [ANY] * ns,
        out_specs=[pl.BlockSpec((BLOCK, 2 * POOL_WIDTH), lambda i: (i, 0)), pl.BlockSpec((BLOCK, POOL_WIDTH), lambda i: (i, 0)),
                   pl.BlockSpec((N_Q_HEADS, BLOCK, 2 * BLOCK), lambda i: (0, i, 0)), pl.BlockSpec((BLOCK, 128), lambda i: (i, 0))]
        + [ANY] * ns,
        scratch_shapes=[pltpu.VMEM((HALO + BLOCK, POOL_WIDTH), F32), pltpu.VMEM((N_Q_HEADS, BLOCK, 2 * BLOCK), F32),
                        pltpu.VMEM((N_Q_HEADS, BLOCK, 2 * BLOCK), F32)]
        + _gather_scratch(shards),
        compiler_params=_params(collective_id=GATHER_COLLECTIVE_ID),
    )(proj, proj, proj, proj, proj, bucket, in_window, rel_bias, sinks, w_pool, pool_scale, *shards)


def outproj_norm(cat, w, x, g, g_next, shard, partial):
    t, d = x.shape
    tm = TOKEN_TILE
    last = t // tm - 1
    rows = [(partial.shape[1] - shard.shape[0], shard.shape[0])]

    def body(c_ref, w_ref, x_ref, g_ref, gn_ref, shard_ref, partial_ref, mix_ref, x1_ref, h2_ref, gathered_ref,
             send_sems, recv_sems, local_sems, bounce):
        i = pl.program_id(0)
        start, finish = _gather_plan([shard_ref], [gathered_ref], send_sems, recv_sems, local_sems, [bounce], rows)
        pl.when(i == 0)(start)
        mix = _dot(c_ref[...], w_ref[...])
        mix_ref[...] = mix
        x1 = x_ref[...] + (mix * _rstd(mix)) * g_ref[...]
        x1_ref[...] = x1
        h2_ref[...] = ((x1 * _rstd(x1)) * gn_ref[...]).astype(BF16)
        pl.when(i == max(last - 1, 0))(finish.forward)
        pl.when(i == last)(finish.complete)

    row = pl.BlockSpec((tm, d), lambda i: (i, 0))
    gain = pl.BlockSpec((1, d), lambda i: (0, 0))
    return pl.pallas_call(
        body, name="outproj_norm", grid=(t // tm,),
        out_shape=[jax.ShapeDtypeStruct((t, d), F32), jax.ShapeDtypeStruct((t, d), F32), jax.ShapeDtypeStruct((t, d), BF16),
                   jax.ShapeDtypeStruct(partial.shape, partial.dtype)],
        in_specs=[pl.BlockSpec((tm, cat.shape[1]), lambda i: (i, 0)), pl.BlockSpec(w.shape, lambda i: (0, 0)), row, gain, gain,
                  ANY, ANY],
        out_specs=[row, row, row, ANY],
        input_output_aliases={6: 3},
        scratch_shapes=_gather_scratch([shard]),
        compiler_params=_params(collective_id=GATHER_COLLECTIVE_ID),
    )(cat, w, x, g, g_next, shard, partial)


def ffn_up(h, gate_t, up_t, down_shard):
    t, d = h.shape
    n = gate_t.shape[1]
    f = N_DEV * n
    tm, ts = FFN_TOKEN_TILE, FF_SHARDS_PER_TILE
    tn = ts * n
    steps = (f // tn, t // tm)

    def body(h_ref, wg_ref, wu_ref, shard_ref, gate_ref, up_ref, a_ref, gathered_ref,
             send_sems, recv_sems, local_sems, bounce):
        j, i = pl.program_id(0), pl.program_id(1)
        start, finish = _gather_plan([shard_ref], [gathered_ref], send_sems, recv_sems, local_sems, [bounce])
        pl.when((i == 0) & (j == 0))(start)

        hv = h_ref[...]
        gate = _dot_nt(hv, _merge_rows(wg_ref[...]))
        up = _dot_nt(hv, _merge_rows(wu_ref[...]))
        gate_ref[...] = gate.astype(BF16)
        up_ref[...] = up.astype(BF16)
        a_ref[...] = (gate * (1.0 / (1.0 + jnp.exp(-gate))) * up).astype(BF16)

        pl.when((j == steps[0] - 1) & (i == max(steps[1] - 2, 0)))(finish.forward)
        pl.when((j == steps[0] - 1) & (i == steps[1] - 1))(finish.complete)

    wide = pl.BlockSpec((tm, tn), lambda j, i: (i, j))
    return pl.pallas_call(
        body, name="ffn_up", grid=steps,
        out_shape=[jax.ShapeDtypeStruct((t, f), BF16)] * 3
        + [jax.ShapeDtypeStruct((N_DEV, *down_shard.shape), down_shard.dtype)],
        in_specs=[pl.BlockSpec((tm, d), lambda j, i: (i, 0)),
                  pl.BlockSpec((ts, n, d), lambda j, i: (j, 0, 0)),
                  pl.BlockSpec((ts, n, d), lambda j, i: (j, 0, 0)), ANY],
        out_specs=[wide, wide, wide, ANY],
        scratch_shapes=_gather_scratch([down_shard]),
        compiler_params=_params(collective_id=GATHER_COLLECTIVE_ID),
    )(h, gate_t, up_t, down_shard)


def ffn_down_loss(a, w_down, x1, g, target):
    t, d = x1.shape
    tm = WIDE_K_TOKEN_TILE

    def body(a_ref, w_ref, x_ref, g_ref, t_ref, df_ref, dy_ref, dg_ref, loss_ref):
        @pl.when(pl.program_id(0) == 0)
        def _():
            dg_ref[...] = jnp.zeros_like(dg_ref)
            loss_ref[...] = jnp.zeros_like(loss_ref)

        f = _dot(a_ref[...], _merge_rows(w_ref[...]))
        r = _rstd(f)
        g = g_ref[...]
        err = x_ref[...] + (f * r) * g - t_ref[...]
        loss_ref[...] += 0.5 * jnp.sum(jnp.mean(err * err, axis=-1, keepdims=True))
        dy = err * (1.0 / d)
        dy_ref[...] = dy
        df, dg_rows = _norm_bwd(dy, f, r, g)
        df_ref[...] = df.astype(BF16)
        dg_ref[...] += _as_rows(jnp.sum(dg_rows, axis=0, keepdims=True))

    row = pl.BlockSpec((tm, d), lambda i: (i, 0))
    gain = pl.BlockSpec((1, d), lambda i: (0, 0))
    return pl.pallas_call(
        body, name="ffn_down_loss", grid=(t // tm,),
        out_shape=[jax.ShapeDtypeStruct((t, d), BF16), jax.ShapeDtypeStruct((t, d), F32),
                   jax.ShapeDtypeStruct((d // 128, 128), F32), jax.ShapeDtypeStruct((1, 128), F32)],
        in_specs=[pl.BlockSpec((tm, a.shape[1]), lambda i: (i, 0)), pl.BlockSpec(w_down.shape, lambda i: (0, 0, 0)), row, gain, row],
        out_specs=[row, row, pl.BlockSpec((d // 128, 128), lambda i: (0, 0)), pl.BlockSpec((1, 128), lambda i: (0, 0))],
        compiler_params=_params(),
    )(a, w_down, x1, g, target)


def ffn_down_bwd(df, w_down, gate, up):
    t, d = df.shape
    n = w_down.shape[1]
    f = gate.shape[1]
    tm, ts = FFN_TOKEN_TILE, FF_SHARDS_PER_TILE
    tn = ts * n

    def body(df_ref, w_ref, gate_ref, up_ref, dgate_ref, dup_ref):
        da = _dot_nt(df_ref[...], _merge_rows(w_ref[...]))
        gate = gate_ref[...].astype(F32)
        sig = 1.0 / (1.0 + jnp.exp(-gate))
        dgate_ref[...] = (da * up_ref[...].astype(F32) * (sig * (1.0 + gate * (1.0 - sig)))).astype(BF16)
        dup_ref[...] = (da * (gate * sig)).astype(BF16)

    wide = pl.BlockSpec((tm, tn), lambda j, i: (i, j))
    return pl.pallas_call(
        body, name="ffn_down_bwd", grid=(f // tn, t // tm),
        out_shape=[jax.ShapeDtypeStruct((t, f), BF16)] * 2,
        in_specs=[pl.BlockSpec((tm, d), lambda j, i: (i, 0)), pl.BlockSpec((ts, n, d), lambda j, i: (j, 0, 0)), wide, wide],
        out_specs=[wide, wide],
        compiler_params=_params(),
    )(df, w_down, gate, up)


def grad_rows(a, b, name, by_core=False):
    t, m = a.shape
    d = b.shape[1]
    r = m // N_DEV
    tt = TOKEN_TILE
    last = t // tt - 1
    out_shape = (2, N_CHIP, r, d) if by_core else (N_DEV, r, d)

    def body(a_ref, b_ref, out_ref, acc):
        k = pl.program_id(0)

        @pl.when(k == 0)
        def _():
            acc[...] = jnp.zeros_like(acc)

        acc[...] += _dot_tn(a_ref[...], b_ref[...])

        @pl.when(k == last)
        def _():
            if by_core:
                blocks = acc[...].reshape(N_CHIP, 2, r, d)
                for chip in range(N_CHIP):
                    for core in range(2):
                        out_ref[core, chip] = blocks[chip, core].astype(BF16)
            else:
                out_ref[...] = acc[...].reshape(out_shape).astype(BF16)

    return pl.pallas_call(
        body, name=name, grid=(t // tt,),
        out_shape=jax.ShapeDtypeStruct(out_shape, BF16),
        in_specs=[pl.BlockSpec((tt, m), lambda k: (k, 0)), pl.BlockSpec((tt, d), lambda k: (k, 0))],
        out_specs=pl.BlockSpec(out_shape, lambda k: (0,) * len(out_shape)),
        scratch_shapes=[pltpu.VMEM((m, d), F32)],
        compiler_params=_params(),
    )(a, b)


def grad_ffn(lhs, b, name, pair_parts=()):
    t, f = lhs[0].shape
    d = b.shape[1]
    nw = len(lhs)
    na = len(pair_parts)
    n = f // N_DEV
    tt, ts = TOKEN_TILE, FF_SHARDS_PER_TILE
    tn = ts * n
    steps = (f // tn, t // tt)

    def body(*refs):
        a_refs, b_ref, part_refs = refs[:nw], refs[nw], refs[nw + 1:nw + 1 + na]
        out_refs = refs[nw + 1 + na:2 * nw + 1 + na]
        got_refs = refs[2 * nw + 1 + na:2 * nw + 1 + 2 * na]
        acc = refs[2 * nw + 1 + 2 * na]
        i, k = pl.program_id(0), pl.program_id(1)
        if na:
            start, finish = _pair_plan(part_refs, got_refs, *refs[2 * nw + 2 + 2 * na:])
            pl.when((i == 0) & (k == 0))(start)

        @pl.when(k == 0)
        def _():
            acc[...] = jnp.zeros_like(acc)

        for w in range(nw):
            acc[w] += _dot_tn(a_refs[w][...], b_ref[...])

        @pl.when(k == steps[1] - 1)
        def _():
            for w in range(nw):
                blocks = acc[w].reshape(ts // 2, 2, n, d)
                for chip in range(ts // 2):
                    for core in range(2):
                        out_refs[w][core, chip] = blocks[chip, core].astype(BF16)

        if na:
            pl.when((i == steps[0] - 1) & (k == steps[1] - 1))(finish)

    out = pl.pallas_call(
        body, name=name, grid=steps,
        out_shape=[jax.ShapeDtypeStruct((2, N_CHIP, n, d), BF16)] * nw
        + [jax.ShapeDtypeStruct(p.shape[1:], p.dtype) for p in pair_parts],
        in_specs=[pl.BlockSpec((tt, tn), lambda i, k: (k, i))] * nw + [pl.BlockSpec((tt, d), lambda i, k: (k, 0))] + [ANY] * na,
        out_specs=[pl.BlockSpec((2, ts // 2, n, d), lambda i, k: (0, i, 0, 0))] * nw + [ANY] * na,
        scratch_shapes=[pltpu.VMEM((nw, tn, d), F32)]
        + ([pltpu.SemaphoreType.DMA((na,)), pltpu.SemaphoreType.DMA((na,))] if na else []),
        compiler_params=_params(collective_id=PAIR_COLLECTIVE_ID) if na else _params(),
    )(*lhs, b, *pair_parts)
    return out[:nw], out[nw:]


def ffn_up_bwd(dgate, dup, gate_t, up_t, x1, g_ffn, dy, mix, g_mix, chip_parts):
    t, d = x1.shape
    n = gate_t.shape[1]
    f = N_DEV * n
    tm = WIDE_K_TOKEN_TILE
    na = len(chip_parts)
    last = t // tm - 1

    def body(*refs):
        dg_ref, du_ref, wg_ref, wu_ref, x_ref, gf_ref, dy_ref, mix_ref, gm_ref = refs[:9]
        part_refs = refs[9:9 + na]
        dx1_ref, dmix_ref, dgf_ref, dgm_ref = refs[9 + na:13 + na]
        slot_refs = refs[13 + na:13 + 2 * na]
        send_sems, recv_sems, local_sems = refs[13 + 2 * na:16 + 2 * na]
        i = pl.program_id(0)
        start, finish = _chip_exchange_plan(part_refs, slot_refs, send_sems, recv_sems, local_sems, refs[16 + 2 * na:])

        @pl.when(i == 0)
        def _():
            start()
            dgf_ref[...] = jnp.zeros_like(dgf_ref)
            dgm_ref[...] = jnp.zeros_like(dgm_ref)

        dh = _dot(dg_ref[...], _merge_rows(wg_ref[...])) + _dot(du_ref[...], _merge_rows(wu_ref[...]))
        x1 = x_ref[...]
        dx, dgf_rows = _norm_bwd(dh, x1, _rstd(x1), gf_ref[...])
        dx1 = dy_ref[...] + dx
        dx1_ref[...] = dx1
        dgf_ref[...] += _as_rows(jnp.sum(dgf_rows, axis=0, keepdims=True))
        mix = mix_ref[...]
        dmix, dgm_rows = _norm_bwd(dx1, mix, _rstd(mix), gm_ref[...])
        dmix_ref[...] = dmix.astype(BF16)
        dgm_ref[...] += _as_rows(jnp.sum(dgm_rows, axis=0, keepdims=True))
        pl.when(i == last)(finish)

    row = pl.BlockSpec((tm, d), lambda i: (i, 0))
    wide = pl.BlockSpec((tm, f), lambda i: (i, 0))
    gain = pl.BlockSpec((1, d), lambda i: (0, 0))
    gain_rows = pl.BlockSpec((d // 128, 128), lambda i: (0, 0))
    whole = pl.BlockSpec((N_DEV, n, d), lambda i: (0, 0, 0), pipeline_mode=pl.Buffered(1))
    out = pl.pallas_call(
        body, name="ffn_up_bwd", grid=(t // tm,),
        out_shape=[jax.ShapeDtypeStruct((t, d), F32), jax.ShapeDtypeStruct((t, d), BF16),
                   jax.ShapeDtypeStruct((d // 128, 128), F32), jax.ShapeDtypeStruct((d // 128, 128), F32)]
        + [jax.ShapeDtypeStruct(p.shape, p.dtype) for p in chip_parts],
        in_specs=[wide, wide, whole, whole, row, gain, row, row, gain] + [ANY] * na,
        out_specs=[row, row, gain_rows, gain_rows] + [ANY] * na,
        scratch_shapes=_chip_exchange_scratch(chip_parts),
        compiler_params=_params(collective_id=CHIP_COLLECTIVE_ID),
    )(dgate, dup, gate_t, up_t, x1, g_ffn, dy, mix, g_mix, *chip_parts)
    return out[:4], out[4:]


def outproj_bwd(dmix, w_out):
    t, d = dmix.shape
    tm = TOKEN_TILE

    def body(dm_ref, w_ref, out_ref):
        out_ref[...] = _dot_nt(dm_ref[...], w_ref[...])

    return pl.pallas_call(
        body, name="outproj_bwd", grid=(t // tm,),
        out_shape=jax.ShapeDtypeStruct((t, w_out.shape[0]), F32),
        in_specs=[pl.BlockSpec((tm, d), lambda i: (i, 0)), pl.BlockSpec(w_out.shape, lambda i: (0, 0))],
        out_specs=pl.BlockSpec((tm, w_out.shape[0]), lambda i: (i, 0)),
        compiler_params=_params(),
    )(dmix, w_out)


def mixers_bwd(proj, dcat, pooled, probs, p_sinks, w_pool, pool_scale, bucket, ffn_parts):
    t = proj.shape[0]
    nb = t // BLOCK
    na = len(ffn_parts)

    def body(*refs):
        (q_ref, kvc_ref, kvp_ref, dcat_ref, pooled_ref, p_all, psink_ref, wp_ref, sc_ref, bk_ref) = refs[:10]
        part_refs = refs[10:10 + na]
        dproj_ref, dbias_ref, dsink_ref, dwp_ref, dsc_ref, drb_ref = refs[10 + na:16 + na]
        slot_refs = refs[16 + na:16 + 2 * na]
        dbuf, c_u, c_q, c_kv, dp_all, ds_all, sink_acc = refs[16 + 2 * na:23 + 2 * na]
        send_sems, recv_sems, local_sems = refs[23 + 2 * na:26 + 2 * na]
        bounce = refs[26 + 2 * na:]
        i = pl.program_id(0)
        lane = lax.broadcasted_iota(jnp.int32, (1, 128), 1)
        start, finish = _chip_exchange_plan(part_refs, slot_refs, send_sems, recv_sems, local_sems, bounce)

        @pl.when(i == 0)
        def _():
            start()
            dbias_ref[...] = jnp.zeros_like(dbias_ref)
            dwp_ref[...] = jnp.zeros_like(dwp_ref)
            dsc_ref[...] = jnp.zeros_like(dsc_ref)
            dsink_ref[...] = jnp.zeros_like(dsink_ref)
            dbuf[...] = jnp.zeros_like(dbuf)
            c_u[...] = jnp.zeros_like(c_u)
            c_q[...] = jnp.zeros_like(c_q)
            c_kv[...] = jnp.zeros_like(c_kv)

        @pl.when(i < nb)
        def _():
            for g, w in enumerate(POOL_WINDOWS):
                cols = slice(g * POOL_GROUP_DIM, (g + 1) * POOL_GROUP_DIM)
                pooled = pooled_ref[:, cols]
                mixed = _dot(pooled, wp_ref[g])
                dout = dcat_ref[:, cols]
                dsc_ref[g:g + 1, :] += jnp.sum(dout * mixed, axis=0, keepdims=True)
                dmixed = (dout * sc_ref[:, cols]).astype(BF16)
                dwp_ref[g] += _dot_tn(pooled, dmixed)
                dpooled = _dot_nt(dmixed, wp_ref[g])
                scaled = dpooled * _inv_count(i, w)
                dbuf[BLOCK:, cols] = scaled[0:HALO]
                dproj_ref[:, cols] = (_window_sum(dbuf, g, w, lambda k: k) + c_u[:, cols]).astype(BF16)
                dbuf[0:BLOCK, cols] = scaled
                c_u[:, cols] = -dpooled

            kv = jnp.concatenate([kvp_ref[...], kvc_ref[...]], axis=0)
            k_var = _head_variants(kv[:, 0:2 * HEAD_DIM])
            v_var = _head_variants(kv[:, 2 * HEAD_DIM:])
            q2s = [q_ref[:, 2 * HEAD_DIM * j:2 * HEAD_DIM * (j + 1)].astype(BF16) for j in range(N_Q_HEADS // 2)]
            do2s = [dcat_ref[:, POOL_WIDTH + 2 * HEAD_DIM * j:POOL_WIDTH + 2 * HEAD_DIM * (j + 1)].astype(BF16)
                    for j in range(N_Q_HEADS // 2)]
            slot = _head_slot
            for hq in range(N_Q_HEADS):
                j, half, h = hq // 2, hq % 2, hq // GQA_GROUP
                dp_all[hq] = _dot_nt(do2s[j], v_var[h][half])
            sink_acc[...] = jnp.zeros_like(sink_acc)
            for hq in range(N_Q_HEADS):
                for r in range(0, BLOCK, ROW_CHUNK):
                    rows = slice(r, r + ROW_CHUNK)
                    probs = p_all[slot(hq), rows, :].astype(F32)
                    dp = dp_all[hq, rows, :]
                    delta = jnp.sum(probs * dp, axis=-1, keepdims=True)
                    ds = probs * (dp - delta)
                    dbias_ref[hq, rows, :] += ds
                    sink_acc[rows, :] += jnp.where(lane == hq, psink_ref[rows, :], 0.0) * delta
                    ds_all[slot(hq), rows, :] = (ds * ATTN_SCALE).astype(BF16)
            dsink_ref[...] -= jnp.sum(sink_acc[...], axis=0, keepdims=True)
            dq2 = [None] * (N_Q_HEADS // 2)
            for hq in range(N_Q_HEADS):
                j, half, h = hq // 2, hq % 2, hq // GQA_GROUP
                dq = _dot(ds_all[slot(hq)], k_var[h][half])
                dq2[j] = dq if dq2[j] is None else dq2[j] + dq
            low = lax.broadcasted_iota(jnp.int32, (2 * BLOCK, 2 * HEAD_DIM), 1) < HEAD_DIM
            dk_half, dv_half = [[None, None], [None, None]], [[None, None], [None, None]]
            for h in range(N_KV_HEADS):
                for half in range(2):
                    heads = [hq for hq in range(GQA_GROUP * h, GQA_GROUP * (h + 1)) if hq % 2 == half]
                    base = slot(heads[0])
                    q_rows = jnp.concatenate([q2s[hq // 2] for hq in heads], axis=0)
                    do_rows = jnp.concatenate([do2s[hq // 2] for hq in heads], axis=0)
                    dk_half[h][half] = _dot_tn(_merge_rows(ds_all[base:base + 2]), q_rows)
                    dv_half[h][half] = _dot_tn(_merge_rows(p_all[base:base + 2]), do_rows)

            def pair_of(halves):
                return jnp.where(low, halves[0][0] + pltpu.roll(halves[0][1], HEAD_DIM, 1),
                                 halves[1][1] + pltpu.roll(halves[1][0], HEAD_DIM, 1))

            dkv = jnp.concatenate([pair_of(dk_half), pair_of(dv_half)], axis=1)
            dproj_ref[:, POOL_WIDTH:2 * POOL_WIDTH] = c_q[...].astype(BF16)
            dproj_ref[:, 2 * POOL_WIDTH:] = (c_kv[...] + dkv[0:BLOCK]).astype(BF16)
            c_q[...] = jnp.concatenate(dq2, axis=1)
            c_kv[...] = dkv[BLOCK:]

        @pl.when(i == nb)
        def _():
            dbuf[BLOCK:, :] = jnp.zeros((HALO, POOL_WIDTH), F32)
            for g, w in enumerate(POOL_WINDOWS):
                cols = slice(g * POOL_GROUP_DIM, (g + 1) * POOL_GROUP_DIM)
                dproj_ref[:, cols] = (_window_sum(dbuf, g, w, lambda k: k) + c_u[:, cols]).astype(BF16)
            dproj_ref[:, POOL_WIDTH:2 * POOL_WIDTH] = c_q[...].astype(BF16)
            dproj_ref[:, 2 * POOL_WIDTH:] = c_kv[...].astype(BF16)
            bk = bk_ref[...]
            for h in range(N_Q_HEADS):
                db = dbias_ref[h]
                for b in range(N_BUCKETS):
                    drb_ref[h, b] = jnp.sum(jnp.where(bk == float(b), db, 0.0))
            finish()

    cur = lambda i: jnp.minimum(i, nb - 1)
    prv = lambda i: jnp.maximum(jnp.minimum(i, nb - 1) - 1, 0)
    out = pl.pallas_call(
        body, name="mixers_bwd", grid=(nb + 1,),
        out_shape=[jax.ShapeDtypeStruct((t, proj.shape[1]), BF16),
                   jax.ShapeDtypeStruct((N_Q_HEADS, BLOCK, 2 * BLOCK), F32),
                   jax.ShapeDtypeStruct((1, 128), F32),
                   jax.ShapeDtypeStruct((4, POOL_GROUP_DIM, POOL_GROUP_DIM), F32),
                   jax.ShapeDtypeStruct((len(POOL_WINDOWS), POOL_GROUP_DIM), F32),
                   jax.ShapeDtypeStruct((N_Q_HEADS, N_BUCKETS), F32)]
        + [jax.ShapeDtypeStruct(p.shape, p.dtype) for p in ffn_parts],
        in_specs=_mixer_in_specs(cur, prv)[2:]
        + [pl.BlockSpec((BLOCK, 2 * POOL_WIDTH), lambda i: (cur(i), 0)), pl.BlockSpec((BLOCK, POOL_WIDTH), lambda i: (cur(i), 0)),
           pl.BlockSpec((N_Q_HEADS, BLOCK, 2 * BLOCK), lambda i: (0, cur(i), 0)), pl.BlockSpec((BLOCK, 128), lambda i: (cur(i), 0))]
        + _mixer_param_specs() + [pl.BlockSpec((BLOCK, 2 * BLOCK), lambda i: (0, 0))] + [ANY] * na,
        out_specs=[pl.BlockSpec((BLOCK, proj.shape[1]), lambda i: (jnp.maximum(i - 1, 0), 0)),
                   pl.BlockSpec((N_Q_HEADS, BLOCK, 2 * BLOCK), lambda i: (0, 0, 0)),
                   pl.BlockSpec((1, 128), lambda i: (0, 0)),
                   pl.BlockSpec((4, POOL_GROUP_DIM, POOL_GROUP_DIM), lambda i: (0, 0, 0)),
                   pl.BlockSpec((len(POOL_WINDOWS), POOL_GROUP_DIM), lambda i: (0, 0)), SMEM] + [ANY] * na,
        scratch_shapes=[pltpu.VMEM((BLOCK + HALO, POOL_WIDTH), F32),
                        pltpu.VMEM((BLOCK, POOL_WIDTH), F32), pltpu.VMEM((BLOCK, POOL_WIDTH), F32),
                        pltpu.VMEM((BLOCK, 256), F32),
                        pltpu.VMEM((N_Q_HEADS, BLOCK, 2 * BLOCK), F32), pltpu.VMEM((N_Q_HEADS, BLOCK, 2 * BLOCK), BF16),
                        pltpu.VMEM((BLOCK, 128), F32)]
        + _chip_exchange_scratch(ffn_parts),
        compiler_params=_params(collective_id=CHIP_COLLECTIVE_ID),
    )(proj, proj, proj, dcat, pooled, probs, p_sinks, w_pool, pool_scale, bucket, *ffn_parts)
    return out[:6], out[6:]


def inproj_bwd(dproj, w_in_t, x, g, dx1):
    t, d = x.shape
    n = dproj.shape[1]
    tm = TOKEN_TILE

    def body(dp_ref, w_ref, x_ref, g_ref, dx1_ref, dx_ref, dg_ref):
        @pl.when(pl.program_id(0) == 0)
        def _():
            dg_ref[...] = jnp.zeros_like(dg_ref)

        dh = _dot(dp_ref[...], w_ref[...])
        xv = x_ref[...]
        dx, dg_rows = _norm_bwd(dh, xv, _rstd(xv), g_ref[...])
        dx_ref[...] = dx1_ref[...] + dx
        dg_ref[...] += _as_rows(jnp.sum(dg_rows, axis=0, keepdims=True))

    row = pl.BlockSpec((tm, d), lambda i: (i, 0))
    gain = pl.BlockSpec((1, d), lambda i: (0, 0))
    return pl.pallas_call(
        body, name="inproj_bwd", grid=(t // tm,),
        out_shape=[jax.ShapeDtypeStruct((t, d), F32), jax.ShapeDtypeStruct((d // 128, 128), F32)],
        in_specs=[pl.BlockSpec((tm, n), lambda i: (i, 0)), pl.BlockSpec(w_in_t.shape, lambda i: (0, 0)), row, gain, row],
        out_specs=[row, pl.BlockSpec((d // 128, 128), lambda i: (0, 0))],
        compiler_params=_params(),
    )(dproj, w_in_t, x, g, dx1)


def _bucket_band():
    qi = jnp.arange(BLOCK)[:, None]
    kj = jnp.arange(2 * BLOCK)[None, :]
    dist = qi + BLOCK - kj
    n = jnp.maximum(dist, 0)
    nf = jnp.maximum(n, 1).astype(F32)
    large = MAX_EXACT + (jnp.log(nf / MAX_EXACT) / np.float32(np.log(MAX_DISTANCE / MAX_EXACT))
                         * (N_BUCKETS - MAX_EXACT)).astype(jnp.int32)
    large = jnp.minimum(large, N_BUCKETS - 1)
    bucket = jnp.where(n < MAX_EXACT, n, large)
    in_window = (dist >= 0) & (dist < BLOCK)
    return bucket.astype(F32), in_window.astype(F32)


def kernel(x, g_pre_mix, w_in, w_pool, pool_scale, rel_bias, sinks, w_out, g_post_mix, g_pre_ffn, w_gate, w_up, w_down, g_post_ffn, loss_target, m_g_pre_mix, m_w_in, m_w_pool, m_pool_scale, m_rel_bias, m_sinks, m_w_out, m_g_post_mix, m_g_pre_ffn, m_w_gate, m_w_up, m_w_down, m_g_post_ffn, v_g_pre_mix, v_w_in, v_w_pool, v_pool_scale, v_rel_bias, v_sinks, v_w_out, v_g_post_mix, v_g_pre_ffn, v_w_gate, v_w_up, v_w_down, v_g_post_ffn):
    d = x.shape[-1]
    xs, target = x[0], loss_target[0]

    w_in_ts = w_in[0].T.astype(BF16)
    w_out_s = w_out[0].astype(BF16)
    gate_ts = w_gate[0].T.astype(BF16)
    up_ts = w_up[0].T.astype(BF16)
    w_down_s = w_down[0].astype(BF16)

    bucket, in_window = _bucket_band()
    w_pool_b = w_pool[0].astype(BF16)
    half = up_ts.shape[0] // 2
    proj, h1, w_in_t, up_t = norm_inproj(xs, g_pre_mix, w_in_ts, up_ts[:half], up_ts.shape[0])
    w_in_t = w_in_t.reshape(-1, d)
    cat, pooled, probs, p_sinks, gate_t, w_out_f = mixers_fwd(
        proj, bucket, in_window, rel_bias, sinks, w_pool_b, pool_scale, [gate_ts, w_out_s])
    w_out_f = w_out_f.reshape(-1, d)
    mix, x1, h2, up_t = outproj_norm(cat, w_out_f, xs, g_post_mix, g_pre_ffn, up_ts[half:], up_t)
    gate, up, act, w_down_f = ffn_up(h2, gate_t, up_t, w_down_s)
    df, dy, dg_post_ffn, loss_part = ffn_down_loss(act, w_down_f, x1, g_post_ffn, target)

    def pair_sum(parts, tag):
        return pair_add(parts, pair_exchange(parts, "pair_exchange_" + tag), "pair_add_" + tag)

    dgate, dup = ffn_down_bwd(df, w_down_f, gate, up)
    (d_gate, d_up), _ = grad_ffn([dgate, dup], h2, "grad_w_gate_up")
    (d_down,), got_gate_up = grad_ffn([act], df, "grad_w_down", [d_gate, d_up])
    q_gate, q_up, q_down = pair_add(
        [d_gate, d_up, d_down], [*got_gate_up, *pair_exchange([d_down], "pair_exchange_down")], "pair_add_ffn")
    (dx1, dmix, dg_pre_ffn, dg_post_mix), (gate_slots, down_slots) = ffn_up_bwd(
        dgate, dup, gate_t, up_t, x1, g_pre_ffn, dy, mix, g_post_mix, [q_gate, q_down])
    dcat = outproj_bwd(dmix, w_out_f)
    d_out = grad_rows(cat, dmix, "grad_w_out", by_core=True)
    q_out, = pair_sum([d_out], "out")
    (dproj, _, dsinks, dw_pool, dpool_scale, drel_bias), (up_slots, out_slots) = mixers_bwd(
        proj, dcat, pooled, probs, p_sinks, w_pool_b, pool_scale, bucket, [q_up, q_out])
    grad_x, dg_pre_mix = inproj_bwd(dproj, w_in_t, xs, g_pre_mix, dx1)

    small_w = [g_pre_mix, g_post_mix, g_pre_ffn, g_post_ffn, pool_scale, sinks, w_pool, rel_bias.T]
    small_m = [m_g_pre_mix, m_g_post_mix, m_g_pre_ffn, m_g_post_ffn, m_pool_scale, m_sinks, m_w_pool, m_rel_bias.T]
    small_v = [v_g_pre_mix, v_g_post_mix, v_g_pre_ffn, v_g_post_ffn, v_pool_scale, v_sinks, v_w_pool, v_rel_bias.T]
    d_in_t, total, total_rb = grad_w_in_small_reduce(
        dproj, h1, [dg_pre_mix, dg_post_mix, dg_pre_ffn, dg_post_ffn], dpool_scale, dsinks, loss_part, dw_pool, drel_bias)
    g_in_t = reduce_w_in(d_in_t)
    loss_row, sm = small_adamw(total, total_rb, small_w, small_m, small_v)
    sm[7] = [r.T for r in sm[7]]
    big_w = [w_in[0].T, w_out[0], w_gate[0].T, w_up[0].T, w_down[0]]
    big_m = [m_w_in[0].T, m_w_out[0], m_w_gate[0].T, m_w_up[0].T, m_w_down[0]]
    big_v = [v_w_in[0].T, v_w_out[0], v_w_gate[0].T, v_w_up[0].T, v_w_down[0]]
    upd = sum_adamw([out_slots, gate_slots, up_slots, down_slots], big_w[1:], big_m[1:], big_v[1:], "sum_adamw")
    upd = [[g_in_t, *adamw_update(big_w[:1], [g_in_t], big_m[:1], big_v[:1], "adamw_in")[0]], *upd]
    back = lambda k, a: (a.T if k in (0, 2, 3) else a)[None]
    big = [[back(k, u) for u in upd[k]] for k in range(5)]

    def ordered(kind):
        s, b = [p[kind] for p in sm], [p[kind] for p in big]
        return [s[0], b[0], s[6], s[4], s[7], s[5], b[1], s[1], s[2], b[2], b[3], b[4], s[3]]

    return (loss_row[0, 0], grad_x[None], *ordered(0), *ordered(1), *ordered(2), *ordered(3))
```

```python
import numpy as np
import jax
import jax.numpy as jnp
from jax import lax
from jax.experimental import pallas as pl
from jax.experimental.pallas import tpu as pltpu

F32 = jnp.float32
BF16 = jnp.bfloat16

N_DEV = 8
N_CHIP = 4
POOL_WIDTH = 512
POOL_WINDOWS = (2, 4, 8, 16)
POOL_GROUP_DIM = 128
HEAD_DIM = 64
N_Q_HEADS = 8
N_KV_HEADS = 2
GQA_GROUP = 4
BLOCK = 128
HALO = 16
ROW_CHUNK = 32
N_BUCKETS = 32
MAX_EXACT = 16
MAX_DISTANCE = 128
EPS = 1e-6
NEG_INF = -1e30
ATTN_SCALE = float(1.0 / np.sqrt(np.float32(HEAD_DIM)))

ADAM_LR = 0.001
ADAM_B1 = 0.9
ADAM_B2 = 0.999
ADAM_EPS = 1e-08
ADAM_WD = 0.01
ADAM_STEP = 10

TOKEN_TILE = 1024
WIDE_K_TOKEN_TILE = 512
FFN_TOKEN_TILE = 1024
FF_SHARDS_PER_TILE = 4
VMEM_LIMIT = 56 * 1024 * 1024
MESH = pl.DeviceIdType.MESH
PAIR_COLLECTIVE_ID = 0
GATHER_COLLECTIVE_ID = 1
CHIP_COLLECTIVE_ID = 2
ANY = pl.BlockSpec(memory_space=pl.ANY)
VMEM = pl.BlockSpec(memory_space=pltpu.VMEM)
SMEM = pl.BlockSpec(memory_space=pltpu.SMEM)


def _params(**kw):
    return pltpu.CompilerParams(vmem_limit_bytes=VMEM_LIMIT, **kw)


def _dot(a, b):
    return jnp.dot(a, b, preferred_element_type=F32)


def _dot_nt(a, b):
    return lax.dot_general(a, b, (((1,), (1,)), ((), ())), preferred_element_type=F32)


def _dot_tn(a, b):
    return lax.dot_general(a, b, (((0,), (0,)), ((), ())), preferred_element_type=F32)


def _rstd(v):
    return lax.rsqrt(jnp.mean(v * v, axis=-1, keepdims=True) + EPS)


def _norm_bwd(dout, v, r, g):
    vn = v * r
    dn = dout * g
    dv = r * (dn - vn * jnp.mean(dn * vn, axis=-1, keepdims=True))
    return dv, dout * vn


def _as_rows(v):
    return jnp.concatenate([v[:, k:k + 128] for k in range(0, v.shape[1], 128)], axis=0)


def _as_lanes(rows):
    return jnp.concatenate([rows[k:k + 1, :] for k in range(rows.shape[0])], axis=1)


def _handshake(peers):
    barrier = pltpu.get_barrier_semaphore()
    for peer in peers:
        pl.semaphore_signal(barrier, inc=1, device_id=peer, device_id_type=MESH)
    pl.semaphore_wait(barrier, len(peers))


def _merge_rows(value):
    s, r, c_ = value.shape
    return value.reshape(s * r, c_)


def _gather_plan(srcs, outs, send_sems, recv_sems, local_sems=None, bounce=None, rows=None, shake=True):
    n = len(srcs)
    x, y, c = lax.axis_index("x"), lax.axis_index("y"), lax.axis_index("c")
    me, sibling = (x, y, c), (x, y, 1 - c)
    chips = [(1 - x, y), (x, 1 - y), (1 - x, 1 - y)]

    def slot(a, px, py, pc):
        whole = outs[a].at[4 * px + 2 * py + pc]
        return whole if rows is None or rows[a] is None else whole.at[pl.ds(*rows[a])]

    def copy(a, k, block, to, from_src=False):
        return pltpu.make_async_remote_copy(
            src_ref=srcs[a] if from_src else slot(a, *block), dst_ref=slot(a, *block),
            send_sem=send_sems.at[k * n + a], recv_sem=recv_sems.at[k * n + a], device_id=to, device_id_type=MESH)

    def own_in(a):
        return pltpu.make_async_copy(srcs[a], bounce[a], local_sems.at[a])

    def own_out(a):
        return pltpu.make_async_copy(bounce[a], slot(a, *me), local_sems.at[a])

    def first(a):
        return [copy(a, 0, me, sibling, True)] + [copy(a, 1 + j, me, (*chip, c), True) for j, chip in enumerate(chips)]

    def passed(a, j):
        return copy(a, 4 + j, (*chips[j], c), sibling)

    def start():
        if shake:
            _handshake([sibling] + [(*chip, c) for chip in chips])
        for a in range(n):
            if bounce is not None:
                own_in(a).start()
            for cp in first(a):
                cp.start()

    def forward():
        if bounce is not None:
            for a in range(n):
                own_in(a).wait()
                own_out(a).start()
        for j, chip in enumerate(chips):
            for a in range(n):
                copy(a, 1 + j, (*chip, c), me).wait_recv()
                passed(a, j).start()

    def complete():
        for a in range(n):
            copy(a, 0, sibling, me).wait_recv()
            for j, chip in enumerate(chips):
                copy(a, 4 + j, (*chip, 1 - c), me).wait_recv()
        for a in range(n):
            for cp in first(a) + [passed(a, j) for j in range(3)]:
                cp.wait_send()
            if bounce is not None:
                own_out(a).wait()

    def finish():
        forward()
        complete()

    finish.forward, finish.complete = forward, complete
    return start, finish


def _gather_scratch(shards):
    n = len(shards)
    return [pltpu.SemaphoreType.DMA((7 * n,)), pltpu.SemaphoreType.DMA((7 * n,)), pltpu.SemaphoreType.DMA((n,))] \
        + [pltpu.VMEM(s.shape, s.dtype) for s in shards]


def _chip_exchange_plan(srcs, outs, send_sems, recv_sems, local_sems, bounce):
    n = len(srcs)
    x, y, c = lax.axis_index("x"), lax.axis_index("y"), lax.axis_index("c")
    my_chip = 2 * x + y

    def copies():
        out = []
        for a in range(n):
            for k in range(1, N_CHIP):
                px, py = x ^ (k >> 1), y ^ (k & 1)
                out.append(pltpu.make_async_remote_copy(
                    src_ref=srcs[a].at[2 * px + py], dst_ref=outs[a].at[my_chip],
                    send_sem=send_sems.at[(k - 1) * n + a], recv_sem=recv_sems.at[(k - 1) * n + a],
                    device_id=(px, py, c), device_id_type=MESH))
        return out

    def own_in(a):
        return pltpu.make_async_copy(srcs[a].at[my_chip], bounce[a], local_sems.at[a])

    def own_out(a):
        return pltpu.make_async_copy(bounce[a], outs[a].at[my_chip], local_sems.at[a])

    def start():
        _handshake([(x ^ (k >> 1), y ^ (k & 1), c) for k in range(1, N_CHIP)])
        for a in range(n):
            own_in(a).start()
        for cp in copies():
            cp.start()

    def finish():
        for a in range(n):
            own_in(a).wait()
            own_out(a).start()
        for cp in copies():
            cp.wait()
        for a in range(n):
            own_out(a).wait()

    return start, finish


def _chip_exchange_scratch(parts):
    n = len(parts)
    return [pltpu.SemaphoreType.DMA((3 * n,)), pltpu.SemaphoreType.DMA((3 * n,)), pltpu.SemaphoreType.DMA((n,))] \
        + [pltpu.VMEM(p.shape[1:], p.dtype) for p in parts]


def _pair_plan(srcs, outs, send_sems, recv_sems):
    x, y, c = lax.axis_index("x"), lax.axis_index("y"), lax.axis_index("c")

    def copies():
        return [pltpu.make_async_remote_copy(
            src_ref=srcs[a].at[1 - c], dst_ref=outs[a], send_sem=send_sems.at[a], recv_sem=recv_sems.at[a],
            device_id=(x, y, 1 - c), device_id_type=MESH) for a in range(len(srcs))]

    def start():
        _handshake([(x, y, 1 - c)])
        for cp in copies():
            cp.start()

    def finish():
        for cp in copies():
            cp.wait()

    return start, finish


def pair_exchange(parts, name):
    n = len(parts)

    def body(*refs):
        start, finish = _pair_plan(refs[:n], refs[n:2 * n], *refs[2 * n:])
        start()
        finish()

    return pl.pallas_call(
        body, name=name, out_shape=[jax.ShapeDtypeStruct(p.shape[1:], p.dtype) for p in parts],
        in_specs=[ANY] * n, out_specs=[ANY] * n,
        scratch_shapes=[pltpu.SemaphoreType.DMA((n,)), pltpu.SemaphoreType.DMA((n,))],
        compiler_params=_params(collective_id=PAIR_COLLECTIVE_ID),
    )(*parts)


def pair_add(parts, got, name):
    n = len(parts)

    def body(core_ref, *refs):
        for a in range(n):
            refs[2 * n + a][...] = (refs[a][...].astype(F32) + refs[n + a][...].astype(F32)).astype(BF16)

    def own(p):
        zeros = (0,) * (p.ndim - 2)
        return pl.BlockSpec((None, 1, *p.shape[2:]), lambda i, core: (core[0], i, *zeros))

    def plain(p):
        zeros = (0,) * (p.ndim - 1)
        return pl.BlockSpec((1, *p.shape[1:]), lambda i, core: (i, *zeros))

    core = lax.axis_index("c").astype(jnp.int32).reshape(1)
    return pl.pallas_call(
        body, name=name,
        grid_spec=pltpu.PrefetchScalarGridSpec(
            num_scalar_prefetch=1, grid=(got[0].shape[0],),
            in_specs=[own(p) for p in parts] + [plain(p) for p in got], out_specs=[plain(p) for p in got]),
        out_shape=[jax.ShapeDtypeStruct(p.shape, BF16) for p in got],
        compiler_params=_params(),
    )(core, *parts, *got)


def _adamw(w, g, m, v):
    m2 = ADAM_B1 * m + (1.0 - ADAM_B1) * g
    v2 = ADAM_B2 * v + (1.0 - ADAM_B2) * (g * g)
    m_hat = m2 / (1.0 - ADAM_B1 ** ADAM_STEP)
    v_hat = v2 / (1.0 - ADAM_B2 ** ADAM_STEP)
    delta = -ADAM_LR * (m_hat / (jnp.sqrt(v_hat) + ADAM_EPS) + ADAM_WD * w)
    return delta, m2, v2


def sum_adamw(slots, ws, ms, vs, name):
    n = len(ws)
    halves = 2

    def body(*refs):
        for a in range(n):
            total = refs[a][0].astype(F32)
            for s in range(1, slots[a].shape[0]):
                total = total + refs[a][s].astype(F32)
            delta, m2, v2 = _adamw(refs[n + a][...], total, refs[2 * n + a][...], refs[3 * n + a][...])
            for q, val in enumerate((total, delta, m2, v2)):
                refs[4 * n + 4 * a + q][...] = val

    def rows(w):
        return pl.BlockSpec((w.shape[0] // halves, w.shape[1]), lambda i: (i, 0))

    def slot_rows(p):
        return pl.BlockSpec((p.shape[0], p.shape[1] // halves, p.shape[2]), lambda i: (0, i, 0))

    out = pl.pallas_call(
        body, name=name, grid=(halves,),
        out_shape=[jax.ShapeDtypeStruct(w.shape, F32) for w in ws for _ in range(4)],
        in_specs=[slot_rows(p) for p in slots] + [rows(w) for w in ws] * 3,
        out_specs=[rows(w) for w in ws for _ in range(4)],
        compiler_params=_params(),
    )(*slots, *ws, *ms, *vs)
    return [out[4 * a:4 * a + 4] for a in range(n)]


def adamw_update(ws, gs, ms, vs, name):
    n = len(ws)

    def body(*refs):
        for a in range(n):
            delta, m2, v2 = _adamw(refs[a][...], refs[n + a][...], refs[2 * n + a][...], refs[3 * n + a][...])
            refs[4 * n + 3 * a][...] = delta
            refs[4 * n + 3 * a + 1][...] = m2
            refs[4 * n + 3 * a + 2][...] = v2

    out = pl.pallas_call(
        body, name=name,
        out_shape=[jax.ShapeDtypeStruct(w.shape, F32) for w in ws for _ in range(3)],
        in_specs=[VMEM] * (4 * n), out_specs=[VMEM] * (3 * n),
        compiler_params=_params(),
    )(*ws, *gs, *ms, *vs)
    return [out[3 * a:3 * a + 3] for a in range(n)]


GAIN_ROWS = 8
ROW_POOL_SCALE = 4 * GAIN_ROWS
ROW_SINKS = ROW_POOL_SCALE + 4
ROW_LOSS = ROW_SINKS + 1
ROW_W_POOL = 40
SMALL_ROWS = ROW_W_POOL + 4 * POOL_GROUP_DIM


def grad_w_in_small_reduce(a, b, gains, dpool_scale, dsinks, loss_part, dw_pool, drel_bias):
    t, m = a.shape
    d = b.shape[1]
    r = m // N_DEV
    tt = TOKEN_TILE
    last = t // tt - 1

    def body(a_ref, b_ref, g0, g1, g2, g3, dsc_ref, dsink_ref, loss_ref, dwp_ref, drb_ref, out_ref, total_ref, total_rb_ref,
             acc, stage, gat, gat_rb, g_send, g_recv):
        k = pl.program_id(0)
        x, y, c = lax.axis_index("x"), lax.axis_index("y"), lax.axis_index("c")
        start, finish = _gather_plan([stage, drb_ref], [gat, gat_rb], g_send, g_recv)

        @pl.when(k == 0)
        def _():
            for q, g_ref in enumerate((g0, g1, g2, g3)):
                stage[GAIN_ROWS * q:GAIN_ROWS * (q + 1), :] = g_ref[...]
            stage[ROW_POOL_SCALE:ROW_SINKS, :] = dsc_ref[...]
            stage[ROW_SINKS:ROW_LOSS, :] = dsink_ref[...]
            stage[ROW_LOSS:ROW_LOSS + 1, :] = loss_ref[...]
            stage[ROW_LOSS + 1:ROW_W_POOL, :] = jnp.zeros((ROW_W_POOL - ROW_LOSS - 1, 128), F32)
            stage[ROW_W_POOL:, :] = dwp_ref[...].reshape(4 * POOL_GROUP_DIM, POOL_GROUP_DIM)
            gat[4 * x + 2 * y + c] = stage[...]
            gat_rb[4 * x + 2 * y + c] = drb_ref[...]
            start()
            acc[...] = jnp.zeros_like(acc)

        acc[...] += _dot_tn(a_ref[...], b_ref[...])

        @pl.when(k == last)
        def _():
            blocks = acc[...].reshape(N_CHIP, 2, r, d)
            for chip in range(N_CHIP):
                for core in range(2):
                    out_ref[core, chip] = blocks[chip, core].astype(BF16)
            finish()
            total, total_rb = gat[0], gat_rb[0]
            for s in range(1, N_DEV):
                total, total_rb = total + gat[s], total_rb + gat_rb[s]
            total_ref[...] = total
            total_rb_ref[...] = total_rb

    out_shape = (2, N_CHIP, r, d)
    return pl.pallas_call(
        body, name="grad_w_in", grid=(t // tt,),
        out_shape=[jax.ShapeDtypeStruct(out_shape, BF16), jax.ShapeDtypeStruct((SMALL_ROWS, 128), F32),
                   jax.ShapeDtypeStruct(drel_bias.shape, F32)],
        in_specs=[pl.BlockSpec((tt, m), lambda k: (k, 0)), pl.BlockSpec((tt, d), lambda k: (k, 0))] + [VMEM] * 9,
        out_specs=[pl.BlockSpec(out_shape, lambda k: (0,) * len(out_shape)), VMEM, VMEM],
        scratch_shapes=[pltpu.VMEM((m, d), F32), pltpu.VMEM((SMALL_ROWS, 128), F32),
                        pltpu.VMEM((N_DEV, SMALL_ROWS, 128), F32), pltpu.VMEM((N_DEV, *drel_bias.shape), F32),
                        pltpu.SemaphoreType.DMA((14,)), pltpu.SemaphoreType.DMA((14,))],
        compiler_params=_params(collective_id=GATHER_COLLECTIVE_ID),
    )(a, b, *gains, dpool_scale, dsinks, loss_part, dw_pool, drel_bias)


def reduce_w_in(d_in_t):
    def body(d_in_ref, g_in_ref, pair_got, chip_part, chip_got, p_send, p_recv, x_send, x_recv):
        x, y, c = lax.axis_index("x"), lax.axis_index("y"), lax.axis_index("c")
        my_chip = 2 * x + y
        _handshake([(x, y, 1 - c)] + [(x ^ (k >> 1), y ^ (k & 1), c) for k in range(1, N_CHIP)])
        pair = pltpu.make_async_remote_copy(
            src_ref=d_in_ref.at[1 - c], dst_ref=pair_got, send_sem=p_send, recv_sem=p_recv,
            device_id=(x, y, 1 - c), device_id_type=MESH)
        pair.start()
        pair.wait()
        chip_part[...] = (d_in_ref[c].astype(F32) + pair_got[...].astype(F32)).astype(BF16)
        copies = []
        for k in range(1, N_CHIP):
            px, py = x ^ (k >> 1), y ^ (k & 1)
            copies.append(pltpu.make_async_remote_copy(
                src_ref=chip_part.at[2 * px + py], dst_ref=chip_got.at[my_chip],
                send_sem=x_send.at[k - 1], recv_sem=x_recv.at[k - 1], device_id=(px, py, c), device_id_type=MESH))
        for cp in copies:
            cp.start()
        chip_got[my_chip] = chip_part[my_chip]
        for cp in copies:
            cp.wait()
        g_in = chip_got[0].astype(F32)
        for s in range(1, N_CHIP):
            g_in = g_in + chip_got[s].astype(F32)
        g_in_ref[...] = g_in

    per_core = d_in_t.shape[1:]
    return pl.pallas_call(
        body, name="reduce_w_in",
        out_shape=jax.ShapeDtypeStruct(d_in_t.shape[2:], F32),
        in_specs=[VMEM], out_specs=VMEM,
        scratch_shapes=[pltpu.VMEM(per_core, d_in_t.dtype), pltpu.VMEM(per_core, d_in_t.dtype),
                        pltpu.VMEM(per_core, d_in_t.dtype),
                        pltpu.SemaphoreType.DMA, pltpu.SemaphoreType.DMA,
                        pltpu.SemaphoreType.DMA((3,)), pltpu.SemaphoreType.DMA((3,))],
        compiler_params=_params(collective_id=GATHER_COLLECTIVE_ID),
    )(d_in_t)


def small_adamw(total, total_rb, small_w, small_m, small_v):
    n_small = len(small_w)

    def body(*refs):
        total_ref, rb_ref = refs[:2]
        w_refs, m_refs, v_refs = (refs[2 + k * n_small:2 + (k + 1) * n_small] for k in range(3))
        loss_out = refs[2 + 3 * n_small]
        result = refs[3 + 3 * n_small:]
        total = total_ref[...]
        loss_out[...] = total[ROW_LOSS:ROW_LOSS + 1, :]
        grads = [_as_lanes(total[GAIN_ROWS * k:GAIN_ROWS * (k + 1), :]) for k in range(4)]
        grads.append(_as_lanes(total[ROW_POOL_SCALE:ROW_SINKS, :]))
        grads.append(total[ROW_SINKS:ROW_LOSS, 0:N_Q_HEADS])
        grads.append(total[ROW_W_POOL:, :].reshape(w_refs[6].shape))
        grads.append(rb_ref[...])
        for k in range(n_small):
            delta, m2, v2 = _adamw(w_refs[k][...], grads[k], m_refs[k][...], v_refs[k][...])
            result[4 * k][...] = grads[k]
            result[4 * k + 1][...] = delta
            result[4 * k + 2][...] = m2
            result[4 * k + 3][...] = v2

    out = pl.pallas_call(
        body, name="small_adamw",
        out_shape=[jax.ShapeDtypeStruct((1, 128), F32)] + [jax.ShapeDtypeStruct(w.shape, F32) for w in small_w for _ in range(4)],
        in_specs=[VMEM] * (2 + 3 * n_small), out_specs=[VMEM] * (1 + 4 * n_small),
        compiler_params=_params(),
    )(total, total_rb, *small_w, *small_m, *small_v)
    return out[0], [out[1 + 4 * k:5 + 4 * k] for k in range(n_small)]


def norm_inproj(x, g, w_shard, shard, shard_rows):
    t, d = x.shape
    r = w_shard.shape[0]
    tm = TOKEN_TILE
    nt = t // tm

    def body(x_ref, g_ref, w_shard_ref, shard_ref, proj_ref, h_ref, w_ref, gathered_ref, h_all, w_all, w_sem,
             send_w, recv_w, local_w, bounce_w, send_sems, recv_sems, local_sems, bounce):
        i = pl.program_id(0)
        start_w, finish_w = _gather_plan([w_shard_ref], [w_ref], send_w, recv_w, local_w, [bounce_w])
        start, finish = _gather_plan([shard_ref], [gathered_ref], send_sems, recv_sems, local_sems, [bounce],
                                     [(0, shard.shape[0])], shake=False)

        @pl.when(i == 0)
        def _():
            start_w()
            start()

        @pl.when(i < nt)
        def _():
            xv = x_ref[...]
            h = ((xv * _rstd(xv)) * g_ref[...]).astype(BF16)
            h_ref[...] = h
            h_all[pl.ds(pl.multiple_of(i * tm, tm), tm), :] = h

        @pl.when(i == nt - 1)
        def _():
            finish_w()
            landed = pltpu.make_async_copy(w_ref, w_all, w_sem)
            landed.start()
            landed.wait()

        @pl.when(i >= nt)
        def _():
            rows = pl.ds(pl.multiple_of((i - nt) * tm, tm), tm)
            proj_ref[...] = _dot_nt(h_all[rows, :], _merge_rows(w_all[...]))

        pl.when(i == 2 * nt - 2)(finish.forward)
        pl.when(i == 2 * nt - 1)(finish.complete)

    first = lambda i: (jnp.minimum(i, nt - 1), 0)
    return pl.pallas_call(
        body, name="norm_inproj", grid=(2 * nt,),
        out_shape=[jax.ShapeDtypeStruct((t, N_DEV * r), F32), jax.ShapeDtypeStruct((t, d), BF16),
                   jax.ShapeDtypeStruct((N_DEV, r, d), w_shard.dtype),
                   jax.ShapeDtypeStruct((N_DEV, shard_rows, d), shard.dtype)],
        in_specs=[pl.BlockSpec((tm, d), first), pl.BlockSpec((1, d), lambda i: (0, 0)), ANY, ANY],
        out_specs=[pl.BlockSpec((tm, N_DEV * r), lambda i: (jnp.maximum(i - nt, 0), 0)), pl.BlockSpec((tm, d), first),
                   ANY, ANY],
        scratch_shapes=[pltpu.VMEM((t, d), BF16), pltpu.VMEM((N_DEV, r, d), w_shard.dtype), pltpu.SemaphoreType.DMA]
        + _gather_scratch([w_shard]) + _gather_scratch([shard]),
        compiler_params=_params(collective_id=GATHER_COLLECTIVE_ID),
    )(x, g, w_shard, shard)


def _fill_bias_band(bk_ref, win_ref, rb_ref, biasm_ref):
    bk = bk_ref[...]
    keep = win_ref[...] > 0.5
    for h in range(N_Q_HEADS):
        acc = jnp.zeros(bk.shape, F32)
        for b in range(N_BUCKETS):
            acc = jnp.where(bk == float(b), rb_ref[b, h], acc)
        biasm_ref[h] = jnp.where(keep, acc, NEG_INF)


def _window_sum(buf_ref, g, w, first):
    cols = slice(g * POOL_GROUP_DIM, (g + 1) * POOL_GROUP_DIM)
    acc = None
    for k in range(w):
        piece = buf_ref[first(k):first(k) + BLOCK, cols]
        acc = piece if acc is None else acc + piece
    return acc


def _inv_count(i, w):
    row = lax.broadcasted_iota(jnp.int32, (BLOCK, 1), 0)
    return 1.0 / jnp.minimum(i * BLOCK + row + 1, w).astype(F32)


def _fill_pool_input(i, ubuf, uc_ref, halo_ref):
    ubuf[0:HALO, :] = jnp.where(i > 0, halo_ref[...], 0.0)
    ubuf[HALO:, :] = uc_ref[...]


def _pooled(i, g, w, ubuf):
    cols = slice(g * POOL_GROUP_DIM, (g + 1) * POOL_GROUP_DIM)
    return _window_sum(ubuf, g, w, lambda k: HALO - k) * _inv_count(i, w) - ubuf[HALO:, cols]


def _head_variants(pair):
    low = lax.broadcasted_iota(jnp.int32, pair.shape, 1) < HEAD_DIM
    swapped = pltpu.roll(pair, HEAD_DIM, 1)
    zero = jnp.zeros_like(pair)
    pick = lambda c, a, b: jnp.where(c, a, b).astype(BF16)
    return [[pick(low, pair, zero), pick(low, zero, swapped)], [pick(low, swapped, zero), pick(low, zero, pair)]]


def _head_probs(i, hq, rows, s_ref, biasm_ref, sinks_ref):
    s = s_ref[hq, rows, :] * ATTN_SCALE + biasm_ref[hq, rows, :]
    col = lax.broadcasted_iota(jnp.int32, s.shape, 1)
    s = jnp.where((i == 0) & (col < BLOCK), NEG_INF, s)
    sink = sinks_ref[0, hq]
    m = jnp.maximum(jnp.max(s, axis=-1, keepdims=True), sink)
    p = jnp.exp(s - m)
    e_sink = jnp.exp(sink - m)
    inv = 1.0 / (jnp.sum(p, axis=-1, keepdims=True) + e_sink)
    return p * inv, e_sink * inv


def _head_slot(hq):
    return 4 * (hq // GQA_GROUP) + 2 * (hq % 2) + (hq % GQA_GROUP) // 2


def _mixer_in_specs(cur, prv):
    return [pl.BlockSpec((BLOCK, 512), lambda i: (cur(i), 0)),
            pl.BlockSpec((HALO, 512), lambda i: (jnp.maximum(cur(i) * (BLOCK // HALO) - 1, 0), 0)),
            pl.BlockSpec((BLOCK, 512), lambda i: (cur(i), 1)),
            pl.BlockSpec((BLOCK, 256), lambda i: (cur(i), 4)),
            pl.BlockSpec((BLOCK, 256), lambda i: (prv(i), 4))]


def _mixer_param_specs():
    return [pl.BlockSpec((4, POOL_GROUP_DIM, POOL_GROUP_DIM), lambda i: (0, 0, 0)),
            pl.BlockSpec((1, POOL_WIDTH), lambda i: (0, 0))]


def mixers_fwd(proj, bucket, in_window, rel_bias, sinks, w_pool, pool_scale, shards):
    t = proj.shape[0]
    nb = t // BLOCK
    ns = len(shards)

    def body(*refs):
        uc_ref, halo_ref, q_ref, kvc_ref, kvp_ref, bk_ref, win_ref, rb_ref, sinks_ref, wp_ref, sc_ref = refs[:11]
        shard_refs = refs[11:11 + ns]
        out_ref, pooled_ref, p_all, psink_ref = refs[11 + ns:15 + ns]
        gathered_refs = refs[15 + ns:15 + 2 * ns]
        ubuf, s_all, biasm_ref, send_sems, recv_sems, local_sems = refs[15 + 2 * ns:21 + 2 * ns]
        i = pl.program_id(0)
        start, finish = _gather_plan(shard_refs, gathered_refs, send_sems, recv_sems, local_sems, refs[21 + 2 * ns:])

        @pl.when(i == 0)
        def _():
            start()
            _fill_bias_band(bk_ref, win_ref, rb_ref, biasm_ref)

        _fill_pool_input(i, ubuf, uc_ref, halo_ref)
        for g, w in enumerate(POOL_WINDOWS):
            cols = slice(g * POOL_GROUP_DIM, (g + 1) * POOL_GROUP_DIM)
            pooled = _pooled(i, g, w, ubuf).astype(BF16)
            pooled_ref[:, cols] = pooled
            out_ref[:, cols] = (_dot(pooled, wp_ref[g]) * sc_ref[:, cols]).astype(BF16)
        kv = jnp.concatenate([kvp_ref[...], kvc_ref[...]], axis=0)
        k_var = _head_variants(kv[:, 0:2 * HEAD_DIM])
        v_var = _head_variants(kv[:, 2 * HEAD_DIM:])
        for hq in range(N_Q_HEADS):
            j, half, h = hq // 2, hq % 2, hq // GQA_GROUP
            q2 = q_ref[:, 2 * HEAD_DIM * j:2 * HEAD_DIM * (j + 1)].astype(BF16)
            s_all[hq] = _dot_nt(q2, k_var[h][half])
        psink_ref[...] = jnp.zeros_like(psink_ref)
        for hq in range(N_Q_HEADS):
            for r in range(0, BLOCK, ROW_CHUNK):
                rows = slice(r, r + ROW_CHUNK)
                probs, p_sink = _head_probs(i, hq, rows, s_all, biasm_ref, sinks_ref)
                p_all[_head_slot(hq), rows, :] = probs.astype(BF16)
                psink_ref[rows, hq:hq + 1] = p_sink
        for j in range(N_Q_HEADS // 2):
            h = 2 * j // GQA_GROUP
            acc = _dot(p_all[_head_slot(2 * j)], v_var[h][0]) + _dot(p_all[_head_slot(2 * j + 1)], v_var[h][1])
            out_ref[:, POOL_WIDTH + 2 * HEAD_DIM * j:POOL_WIDTH + 2 * HEAD_DIM * (j + 1)] = acc.astype(BF16)

        pl.when(i == max(nb - 4, 0))(finish.forward)
        pl.when(i == nb - 1)(finish.complete)

    return pl.pallas_call(
        body, name="mixers_fwd", grid=(nb,),
        out_shape=[jax.ShapeDtypeStruct((t, 2 * POOL_WIDTH), BF16), jax.ShapeDtypeStruct((t, POOL_WIDTH), BF16),
                   jax.ShapeDtypeStruct((N_Q_HEADS, t, 2 * BLOCK), BF16), jax.ShapeDtypeStruct((t, 128), F32)]
        + [jax.ShapeDtypeStruct((N_DEV, *sh.shape), sh.dtype) for sh in shards],
        in_specs=_mixer_in_specs(lambda i: i, lambda i: jnp.maximum(i - 1, 0))
        + [pl.BlockSpec((BLOCK, 2 * BLOCK), lambda i: (0, 0))] * 2 + [SMEM, SMEM] + _mixer_param_specs() + [ANY] * ns,
        out_specs=[pl.BlockSpec((BLOCK, 2 * POOL_WIDTH), lambda i: (i, 0)), pl.BlockSpec((BLOCK, POOL_WIDTH), lambda i: (i, 0)),
                   pl.BlockSpec((N_Q_HEADS, BLOCK, 2 * BLOCK), lambda i: (0, i, 0)), pl.BlockSpec((BLOCK, 128), lambda i: (i, 0))]
        + [ANY] * ns,
        scratch_shapes=[pltpu.VMEM((HALO + BLOCK, POOL_WIDTH), F32), pltpu.VMEM((N_Q_HEADS, BLOCK, 2 * BLOCK), F32),
                        pltpu.VMEM((N_Q_HEADS, BLOCK, 2 * BLOCK), F32)]
        + _gather_scratch(shards),
        compiler_params=_params(collective_id=GATHER_COLLECTIVE_ID),
    )(proj, proj, proj, proj, proj, bucket, in_window, rel_bias, sinks, w_pool, pool_scale, *shards)


def outproj_norm(cat, w, x, g, g_next, shard, partial):
    t, d = x.shape
    tm = TOKEN_TILE
    last = t // tm - 1
    rows = [(partial.shape[1] - shard.shape[0], shard.shape[0])]

    def body(c_ref, w_ref, x_ref, g_ref, gn_ref, shard_ref, partial_ref, mix_ref, x1_ref, h2_ref, gathered_ref,
             send_sems, recv_sems, local_sems, bounce):
        i = pl.program_id(0)
        start, finish = _gather_plan([shard_ref], [gathered_ref], send_sems, recv_sems, local_sems, [bounce], rows)
        pl.when(i == 0)(start)
        mix = _dot(c_ref[...], w_ref[...])
        mix_ref[...] = mix
        x1 = x_ref[...] + (mix * _rstd(mix)) * g_ref[...]
        x1_ref[...] = x1
        h2_ref[...] = ((x1 * _rstd(x1)) * gn_ref[...]).astype(BF16)
        pl.when(i == max(last - 1, 0))(finish.forward)
        pl.when(i == last)(finish.complete)

    row = pl.BlockSpec((tm, d), lambda i: (i, 0))
    gain = pl.BlockSpec((1, d), lambda i: (0, 0))
    return pl.pallas_call(
        body, name="outproj_norm", grid=(t // tm,),
        out_shape=[jax.ShapeDtypeStruct((t, d), F32), jax.ShapeDtypeStruct((t, d), F32), jax.ShapeDtypeStruct((t, d), BF16),
                   jax.ShapeDtypeStruct(partial.shape, partial.dtype)],
        in_specs=[pl.BlockSpec((tm, cat.shape[1]), lambda i: (i, 0)), pl.BlockSpec(w.shape, lambda i: (0, 0)), row, gain, gain,
                  ANY, ANY],
        out_specs=[row, row, row, ANY],
        input_output_aliases={6: 3},
        scratch_shapes=_gather_scratch([shard]),
        compiler_params=_params(collective_id=GATHER_COLLECTIVE_ID),
    )(cat, w, x, g, g_next, shard, partial)


def ffn_up(h, gate_t, up_t, down_shard):
    t, d = h.shape
    n = gate_t.shape[1]
    f = N_DEV * n
    tm, ts = FFN_TOKEN_TILE, FF_SHARDS_PER_TILE
    tn = ts * n
    steps = (f // tn, t // tm)

    def body(h_ref, wg_ref, wu_ref, shard_ref, gate_ref, up_ref, a_ref, gathered_ref,
             send_sems, recv_sems, local_sems, bounce):
        j, i = pl.program_id(0), pl.program_id(1)
        start, finish = _gather_plan([shard_ref], [gathered_ref], send_sems, recv_sems, local_sems, [bounce])
        pl.when((i == 0) & (j == 0))(start)

        hv = h_ref[...]
        gate = _dot_nt(hv, _merge_rows(wg_ref[...]))
        up = _dot_nt(hv, _merge_rows(wu_ref[...]))
        gate_ref[...] = gate.astype(BF16)
        up_ref[...] = up.astype(BF16)
        a_ref[...] = (gate * (1.0 / (1.0 + jnp.exp(-gate))) * up).astype(BF16)

        pl.when((j == steps[0] - 1) & (i == max(steps[1] - 2, 0)))(finish.forward)
        pl.when((j == steps[0] - 1) & (i == steps[1] - 1))(finish.complete)

    wide = pl.BlockSpec((tm, tn), lambda j, i: (i, j))
    return pl.pallas_call(
        body, name="ffn_up", grid=steps,
        out_shape=[jax.ShapeDtypeStruct((t, f), BF16)] * 3
        + [jax.ShapeDtypeStruct((N_DEV, *down_shard.shape), down_shard.dtype)],
        in_specs=[pl.BlockSpec((tm, d), lambda j, i: (i, 0)),
                  pl.BlockSpec((ts, n, d), lambda j, i: (j, 0, 0)),
                  pl.BlockSpec((ts, n, d), lambda j, i: (j, 0, 0)), ANY],
        out_specs=[wide, wide, wide, ANY],
        scratch_shapes=_gather_scratch([down_shard]),
        compiler_params=_params(collective_id=GATHER_COLLECTIVE_ID),
    )(h, gate_t, up_t, down_shard)


def ffn_down_loss(a, w_down, x1, g, target):
    t, d = x1.shape
    tm = WIDE_K_TOKEN_TILE

    def body(a_ref, w_ref, x_ref, g_ref, t_ref, df_ref, dy_ref, dg_ref, loss_ref):
        @pl.when(pl.program_id(0) == 0)
        def _():
            dg_ref[...] = jnp.zeros_like(dg_ref)
            loss_ref[...] = jnp.zeros_like(loss_ref)

        f = _dot(a_ref[...], _merge_rows(w_ref[...]))
        r = _rstd(f)
        g = g_ref[...]
        err = x_ref[...] + (f * r) * g - t_ref[...]
        loss_ref[...] += 0.5 * jnp.sum(jnp.mean(err * err, axis=-1, keepdims=True))
        dy = err * (1.0 / d)
        dy_ref[...] = dy
        df, dg_rows = _norm_bwd(dy, f, r, g)
        df_ref[...] = df.astype(BF16)
        dg_ref[...] += _as_rows(jnp.sum(dg_rows, axis=0, keepdims=True))

    row = pl.BlockSpec((tm, d), lambda i: (i, 0))
    gain = pl.BlockSpec((1, d), lambda i: (0, 0))
    return pl.pallas_call(
        body, name="ffn_down_loss", grid=(t // tm,),
        out_shape=[jax.ShapeDtypeStruct((t, d), BF16), jax.ShapeDtypeStruct((t, d), F32),
                   jax.ShapeDtypeStruct((d // 128, 128), F32), jax.ShapeDtypeStruct((1, 128), F32)],
        in_specs=[pl.BlockSpec((tm, a.shape[1]), lambda i: (i, 0)), pl.BlockSpec(w_down.shape, lambda i: (0, 0, 0)), row, gain, row],
        out_specs=[row, row, pl.BlockSpec((d // 128, 128), lambda i: (0, 0)), pl.BlockSpec((1, 128), lambda i: (0, 0))],
        compiler_params=_params(),
    )(a, w_down, x1, g, target)


def ffn_down_bwd(df, w_down, gate, up):
    t, d = df.shape
    n = w_down.shape[1]
    f = gate.shape[1]
    tm, ts = FFN_TOKEN_TILE, FF_SHARDS_PER_TILE
    tn = ts * n

    def body(df_ref, w_ref, gate_ref, up_ref, dgate_ref, dup_ref):
        da = _dot_nt(df_ref[...], _merge_rows(w_ref[...]))
        gate = gate_ref[...].astype(F32)
        sig = 1.0 / (1.0 + jnp.exp(-gate))
        dgate_ref[...] = (da * up_ref[...].astype(F32) * (sig * (1.0 + gate * (1.0 - sig)))).astype(BF16)
        dup_ref[...] = (da * (gate * sig)).astype(BF16)

    wide = pl.BlockSpec((tm, tn), lambda j, i: (i, j))
    return pl.pallas_call(
        body, name="ffn_down_bwd", grid=(f // tn, t // tm),
        out_shape=[jax.ShapeDtypeStruct((t, f), BF16)] * 2,
        in_specs=[pl.BlockSpec((tm, d), lambda j, i: (i, 0)), pl.BlockSpec((ts, n, d), lambda j, i: (j, 0, 0)), wide, wide],
        out_specs=[wide, wide],
        compiler_params=_params(),
    )(df, w_down, gate, up)


def grad_ffn(lhs, b, name, pair_parts=()):
    t, f = lhs[0].shape
    d = b.shape[1]
    nw = len(lhs)
    na = len(pair_parts)
    n = f // N_DEV
    tt, ts = TOKEN_TILE, FF_SHARDS_PER_TILE
    tn = ts * n
    steps = (f // tn, t // tt)

    def body(*refs):
        a_refs, b_ref, part_refs = refs[:nw], refs[nw], refs[nw + 1:nw + 1 + na]
        out_refs = refs[nw + 1 + na:2 * nw + 1 + na]
        got_refs = refs[2 * nw + 1 + na:2 * nw + 1 + 2 * na]
        acc = refs[2 * nw + 1 + 2 * na]
        i, k = pl.program_id(0), pl.program_id(1)
        if na:
            start, finish = _pair_plan(part_refs, got_refs, *refs[2 * nw + 2 + 2 * na:])
            pl.when((i == 0) & (k == 0))(start)

        @pl.when(k == 0)
        def _():
            acc[...] = jnp.zeros_like(acc)

        for w in range(nw):
            acc[w] += _dot_tn(a_refs[w][...], b_ref[...])

        @pl.when(k == steps[1] - 1)
        def _():
            for w in range(nw):
                blocks = acc[w].reshape(ts // 2, 2, n, d)
                for chip in range(ts // 2):
                    for core in range(2):
                        out_refs[w][core, chip] = blocks[chip, core].astype(BF16)

        if na:
            pl.when((i == steps[0] - 1) & (k == steps[1] - 1))(finish)

    out = pl.pallas_call(
        body, name=name, grid=steps,
        out_shape=[jax.ShapeDtypeStruct((2, N_CHIP, n, d), BF16)] * nw
        + [jax.ShapeDtypeStruct(p.shape[1:], p.dtype) for p in pair_parts],
        in_specs=[pl.BlockSpec((tt, tn), lambda i, k: (k, i))] * nw + [pl.BlockSpec((tt, d), lambda i, k: (k, 0))] + [ANY] * na,
        out_specs=[pl.BlockSpec((2, ts // 2, n, d), lambda i, k: (0, i, 0, 0))] * nw + [ANY] * na,
        scratch_shapes=[pltpu.VMEM((nw, tn, d), F32)]
        + ([pltpu.SemaphoreType.DMA((na,)), pltpu.SemaphoreType.DMA((na,))] if na else []),
        compiler_params=_params(collective_id=PAIR_COLLECTIVE_ID) if na else _params(),
    )(*lhs, b, *pair_parts)
    return out[:nw], out[nw:]


def ffn_up_bwd(dgate, dup, gate_t, up_t, x1, g_ffn, dy, mix, g_mix, chip_parts):
    t, d = x1.shape
    n = gate_t.shape[1]
    f = N_DEV * n
    tm = WIDE_K_TOKEN_TILE
    na = len(chip_parts)
    last = t // tm - 1

    def body(*refs):
        dg_ref, du_ref, wg_ref, wu_ref, x_ref, gf_ref, dy_ref, mix_ref, gm_ref = refs[:9]
        part_refs = refs[9:9 + na]
        dx1_ref, dmix_ref, dgf_ref, dgm_ref = refs[9 + na:13 + na]
        slot_refs = refs[13 + na:13 + 2 * na]
        send_sems, recv_sems, local_sems = refs[13 + 2 * na:16 + 2 * na]
        i = pl.program_id(0)
        start, finish = _chip_exchange_plan(part_refs, slot_refs, send_sems, recv_sems, local_sems, refs[16 + 2 * na:])

        @pl.when(i == 0)
        def _():
            start()
            dgf_ref[...] = jnp.zeros_like(dgf_ref)
            dgm_ref[...] = jnp.zeros_like(dgm_ref)

        dh = _dot(dg_ref[...], _merge_rows(wg_ref[...])) + _dot(du_ref[...], _merge_rows(wu_ref[...]))
        x1 = x_ref[...]
        dx, dgf_rows = _norm_bwd(dh, x1, _rstd(x1), gf_ref[...])
        dx1 = dy_ref[...] + dx
        dx1_ref[...] = dx1
        dgf_ref[...] += _as_rows(jnp.sum(dgf_rows, axis=0, keepdims=True))
        mix = mix_ref[...]
        dmix, dgm_rows = _norm_bwd(dx1, mix, _rstd(mix), gm_ref[...])
        dmix_ref[...] = dmix.astype(BF16)
        dgm_ref[...] += _as_rows(jnp.sum(dgm_rows, axis=0, keepdims=True))
        pl.when(i == last)(finish)

    row = pl.BlockSpec((tm, d), lambda i: (i, 0))
    wide = pl.BlockSpec((tm, f), lambda i: (i, 0))
    gain = pl.BlockSpec((1, d), lambda i: (0, 0))
    gain_rows = pl.BlockSpec((d // 128, 128), lambda i: (0, 0))
    whole = pl.BlockSpec((N_DEV, n, d), lambda i: (0, 0, 0), pipeline_mode=pl.Buffered(1))
    out = pl.pallas_call(
        body, name="ffn_up_bwd", grid=(t // tm,),
        out_shape=[jax.ShapeDtypeStruct((t, d), F32), jax.ShapeDtypeStruct((t, d), BF16),
                   jax.ShapeDtypeStruct((d // 128, 128), F32), jax.ShapeDtypeStruct((d // 128, 128), F32)]
        + [jax.ShapeDtypeStruct(p.shape, p.dtype) for p in chip_parts],
        in_specs=[wide, wide, whole, whole, row, gain, row, row, gain] + [ANY] * na,
        out_specs=[row, row, gain_rows, gain_rows] + [ANY] * na,
        scratch_shapes=_chip_exchange_scratch(chip_parts),
        compiler_params=_params(collective_id=CHIP_COLLECTIVE_ID),
    )(dgate, dup, gate_t, up_t, x1, g_ffn, dy, mix, g_mix, *chip_parts)
    return out[:4], out[4:]


def outproj_bwd(dmix, w_out, cat):
    t, d = dmix.shape
    m = w_out.shape[0]
    r = m // N_DEV
    tm = TOKEN_TILE
    last = t // tm - 1

    def body(dm_ref, w_ref, cat_ref, dcat_ref, dw_ref, acc):
        k = pl.program_id(0)

        @pl.when(k == 0)
        def _():
            acc[...] = jnp.zeros_like(acc)

        dcat_ref[...] = _dot_nt(dm_ref[...], w_ref[...])
        acc[...] += _dot_tn(cat_ref[...], dm_ref[...])

        @pl.when(k == last)
        def _():
            blocks = acc[...].reshape(N_CHIP, 2, r, d)
            for chip in range(N_CHIP):
                for core in range(2):
                    dw_ref[core, chip] = blocks[chip, core].astype(BF16)

    tile = lambda width: pl.BlockSpec((tm, width), lambda k: (k, 0))
    return pl.pallas_call(
        body, name="outproj_bwd", grid=(t // tm,),
        out_shape=[jax.ShapeDtypeStruct((t, m), F32), jax.ShapeDtypeStruct((2, N_CHIP, r, d), BF16)],
        in_specs=[tile(d), pl.BlockSpec(w_out.shape, lambda k: (0, 0)), tile(m)],
        out_specs=[tile(m), pl.BlockSpec((2, N_CHIP, r, d), lambda k: (0, 0, 0, 0))],
        scratch_shapes=[pltpu.VMEM((m, d), F32)],
        compiler_params=_params(),
    )(dmix, w_out, cat)


def mixers_bwd(proj, dcat, pooled, probs, p_sinks, w_pool, pool_scale, bucket, ffn_parts):
    t = proj.shape[0]
    nb = t // BLOCK
    na = len(ffn_parts)

    def body(*refs):
        (q_ref, kvc_ref, kvp_ref, dcat_ref, pooled_ref, p_all, psink_ref, wp_ref, sc_ref, bk_ref) = refs[:10]
        part_refs = refs[10:10 + na]
        dproj_ref, dbias_ref, dsink_ref, dwp_ref, dsc_ref, drb_ref = refs[10 + na:16 + na]
        slot_refs = refs[16 + na:16 + 2 * na]
        dbuf, c_u, c_q, c_kv, dp_all, ds_all, sink_acc = refs[16 + 2 * na:23 + 2 * na]
        send_sems, recv_sems, local_sems = refs[23 + 2 * na:26 + 2 * na]
        bounce = refs[26 + 2 * na:]
        i = pl.program_id(0)
        lane = lax.broadcasted_iota(jnp.int32, (1, 128), 1)
        start, finish = _chip_exchange_plan(part_refs, slot_refs, send_sems, recv_sems, local_sems, bounce)

        @pl.when(i == 0)
        def _():
            start()
            dbias_ref[...] = jnp.zeros_like(dbias_ref)
            dwp_ref[...] = jnp.zeros_like(dwp_ref)
            dsc_ref[...] = jnp.zeros_like(dsc_ref)
            dsink_ref[...] = jnp.zeros_like(dsink_ref)
            dbuf[...] = jnp.zeros_like(dbuf)
            c_u[...] = jnp.zeros_like(c_u)
            c_q[...] = jnp.zeros_like(c_q)
            c_kv[...] = jnp.zeros_like(c_kv)

        @pl.when(i < nb)
        def _():
            for g, w in enumerate(POOL_WINDOWS):
                cols = slice(g * POOL_GROUP_DIM, (g + 1) * POOL_GROUP_DIM)
                pooled = pooled_ref[:, cols]
                mixed = _dot(pooled, wp_ref[g])
                dout = dcat_ref[:, cols]
                dsc_ref[g:g + 1, :] += jnp.sum(dout * mixed, axis=0, keepdims=True)
                dmixed = (dout * sc_ref[:, cols]).astype(BF16)
                dwp_ref[g] += _dot_tn(pooled, dmixed)
                dpooled = _dot_nt(dmixed, wp_ref[g])
                scaled = dpooled * _inv_count(i, w)
                dbuf[BLOCK:, cols] = scaled[0:HALO]
                dproj_ref[:, cols] = (_window_sum(dbuf, g, w, lambda k: k) + c_u[:, cols]).astype(BF16)
                dbuf[0:BLOCK, cols] = scaled
                c_u[:, cols] = -dpooled

            kv = jnp.concatenate([kvp_ref[...], kvc_ref[...]], axis=0)
            k_var = _head_variants(kv[:, 0:2 * HEAD_DIM])
            v_var = _head_variants(kv[:, 2 * HEAD_DIM:])
            q2s = [q_ref[:, 2 * HEAD_DIM * j:2 * HEAD_DIM * (j + 1)].astype(BF16) for j in range(N_Q_HEADS // 2)]
            do2s = [dcat_ref[:, POOL_WIDTH + 2 * HEAD_DIM * j:POOL_WIDTH + 2 * HEAD_DIM * (j + 1)].astype(BF16)
                    for j in range(N_Q_HEADS // 2)]
            slot = _head_slot
            for hq in range(N_Q_HEADS):
                j, half, h = hq // 2, hq % 2, hq // GQA_GROUP
                dp_all[hq] = _dot_nt(do2s[j], v_var[h][half])
            sink_acc[...] = jnp.zeros_like(sink_acc)
            for hq in range(N_Q_HEADS):
                for r in range(0, BLOCK, ROW_CHUNK):
                    rows = slice(r, r + ROW_CHUNK)
                    probs = p_all[slot(hq), rows, :].astype(F32)
                    dp = dp_all[hq, rows, :]
                    delta = jnp.sum(probs * dp, axis=-1, keepdims=True)
                    ds = probs * (dp - delta)
                    dbias_ref[hq, rows, :] += ds
                    sink_acc[rows, :] += jnp.where(lane == hq, psink_ref[rows, :], 0.0) * delta
                    ds_all[slot(hq), rows, :] = (ds * ATTN_SCALE).astype(BF16)
            dsink_ref[...] -= jnp.sum(sink_acc[...], axis=0, keepdims=True)
            dq2 = [None] * (N_Q_HEADS // 2)
            for hq in range(N_Q_HEADS):
                j, half, h = hq // 2, hq % 2, hq // GQA_GROUP
                dq = _dot(ds_all[slot(hq)], k_var[h][half])
                dq2[j] = dq if dq2[j] is None else dq2[j] + dq
            low = lax.broadcasted_iota(jnp.int32, (2 * BLOCK, 2 * HEAD_DIM), 1) < HEAD_DIM
            dk_half, dv_half = [[None, None], [None, None]], [[None, None], [None, None]]
            for h in range(N_KV_HEADS):
                for half in range(2):
                    heads = [hq for hq in range(GQA_GROUP * h, GQA_GROUP * (h + 1)) if hq % 2 == half]
                    base = slot(heads[0])
                    q_rows = jnp.concatenate([q2s[hq // 2] for hq in heads], axis=0)
                    do_rows = jnp.concatenate([do2s[hq // 2] for hq in heads], axis=0)
                    dk_half[h][half] = _dot_tn(_merge_rows(ds_all[base:base + 2]), q_rows)
                    dv_half[h][half] = _dot_tn(_merge_rows(p_all[base:base + 2]), do_rows)

            def pair_of(halves):
                return jnp.where(low, halves[0][0] + pltpu.roll(halves[0][1], HEAD_DIM, 1),
                                 halves[1][1] + pltpu.roll(halves[1][0], HEAD_DIM, 1))

            dkv = jnp.concatenate([pair_of(dk_half), pair_of(dv_half)], axis=1)
            dproj_ref[:, POOL_WIDTH:2 * POOL_WIDTH] = c_q[...].astype(BF16)
            dproj_ref[:, 2 * POOL_WIDTH:] = (c_kv[...] + dkv[0:BLOCK]).astype(BF16)
            c_q[...] = jnp.concatenate(dq2, axis=1)
            c_kv[...] = dkv[BLOCK:]

        @pl.when(i == nb)
        def _():
            dbuf[BLOCK:, :] = jnp.zeros((HALO, POOL_WIDTH), F32)
            for g, w in enumerate(POOL_WINDOWS):
                cols = slice(g * POOL_GROUP_DIM, (g + 1) * POOL_GROUP_DIM)
                dproj_ref[:, cols] = (_window_sum(dbuf, g, w, lambda k: k) + c_u[:, cols]).astype(BF16)
            dproj_ref[:, POOL_WIDTH:2 * POOL_WIDTH] = c_q[...].astype(BF16)
            dproj_ref[:, 2 * POOL_WIDTH:] = c_kv[...].astype(BF16)
            bk = bk_ref[...]
            for h in range(N_Q_HEADS):
                db = dbias_ref[h]
                for b in range(N_BUCKETS):
                    drb_ref[h, b] = jnp.sum(jnp.where(bk == float(b), db, 0.0))
            finish()

    cur = lambda i: jnp.minimum(i, nb - 1)
    prv = lambda i: jnp.maximum(jnp.minimum(i, nb - 1) - 1, 0)
    out = pl.pallas_call(
        body, name="mixers_bwd", grid=(nb + 1,),
        out_shape=[jax.ShapeDtypeStruct((t, proj.shape[1]), BF16),
                   jax.ShapeDtypeStruct((N_Q_HEADS, BLOCK, 2 * BLOCK), F32),
                   jax.ShapeDtypeStruct((1, 128), F32),
                   jax.ShapeDtypeStruct((4, POOL_GROUP_DIM, POOL_GROUP_DIM), F32),
                   jax.ShapeDtypeStruct((len(POOL_WINDOWS), POOL_GROUP_DIM), F32),
                   jax.ShapeDtypeStruct((N_Q_HEADS, N_BUCKETS), F32)]
        + [jax.ShapeDtypeStruct(p.shape, p.dtype) for p in ffn_parts],
        in_specs=_mixer_in_specs(cur, prv)[2:]
        + [pl.BlockSpec((BLOCK, 2 * POOL_WIDTH), lambda i: (cur(i), 0)), pl.BlockSpec((BLOCK, POOL_WIDTH), lambda i: (cur(i), 0)),
           pl.BlockSpec((N_Q_HEADS, BLOCK, 2 * BLOCK), lambda i: (0, cur(i), 0)), pl.BlockSpec((BLOCK, 128), lambda i: (cur(i), 0))]
        + _mixer_param_specs() + [pl.BlockSpec((BLOCK, 2 * BLOCK), lambda i: (0, 0))] + [ANY] * na,
        out_specs=[pl.BlockSpec((BLOCK, proj.shape[1]), lambda i: (jnp.maximum(i - 1, 0), 0)),
                   pl.BlockSpec((N_Q_HEADS, BLOCK, 2 * BLOCK), lambda i: (0, 0, 0)),
                   pl.BlockSpec((1, 128), lambda i: (0, 0)),
                   pl.BlockSpec((4, POOL_GROUP_DIM, POOL_GROUP_DIM), lambda i: (0, 0, 0)),
                   pl.BlockSpec((len(POOL_WINDOWS), POOL_GROUP_DIM), lambda i: (0, 0)), SMEM] + [ANY] * na,
        scratch_shapes=[pltpu.VMEM((BLOCK + HALO, POOL_WIDTH), F32),
                        pltpu.VMEM((BLOCK, POOL_WIDTH), F32), pltpu.VMEM((BLOCK, POOL_WIDTH), F32),
                        pltpu.VMEM((BLOCK, 256), F32),
                        pltpu.VMEM((N_Q_HEADS, BLOCK, 2 * BLOCK), F32), pltpu.VMEM((N_Q_HEADS, BLOCK, 2 * BLOCK), BF16),
                        pltpu.VMEM((BLOCK, 128), F32)]
        + _chip_exchange_scratch(ffn_parts),
        compiler_params=_params(collective_id=CHIP_COLLECTIVE_ID),
    )(proj, proj, proj, dcat, pooled, probs, p_sinks, w_pool, pool_scale, bucket, *ffn_parts)
    return out[:6], out[6:]


def inproj_bwd(dproj, w_in_t, x, g, dx1):
    t, d = x.shape
    n = dproj.shape[1]
    tm = TOKEN_TILE

    def body(dp_ref, w_ref, x_ref, g_ref, dx1_ref, dx_ref, dg_ref):
        @pl.when(pl.program_id(0) == 0)
        def _():
            dg_ref[...] = jnp.zeros_like(dg_ref)

        dh = _dot(dp_ref[...], w_ref[...])
        xv = x_ref[...]
        dx, dg_rows = _norm_bwd(dh, xv, _rstd(xv), g_ref[...])
        dx_ref[...] = dx1_ref[...] + dx
        dg_ref[...] += _as_rows(jnp.sum(dg_rows, axis=0, keepdims=True))

    row = pl.BlockSpec((tm, d), lambda i: (i, 0))
    gain = pl.BlockSpec((1, d), lambda i: (0, 0))
    return pl.pallas_call(
        body, name="inproj_bwd", grid=(t // tm,),
        out_shape=[jax.ShapeDtypeStruct((t, d), F32), jax.ShapeDtypeStruct((d // 128, 128), F32)],
        in_specs=[pl.BlockSpec((tm, n), lambda i: (i, 0)), pl.BlockSpec(w_in_t.shape, lambda i: (0, 0)), row, gain, row],
        out_specs=[row, pl.BlockSpec((d // 128, 128), lambda i: (0, 0))],
        compiler_params=_params(),
    )(dproj, w_in_t, x, g, dx1)


def _bucket_band():
    qi = jnp.arange(BLOCK)[:, None]
    kj = jnp.arange(2 * BLOCK)[None, :]
    dist = qi + BLOCK - kj
    n = jnp.maximum(dist, 0)
    nf = jnp.maximum(n, 1).astype(F32)
    large = MAX_EXACT + (jnp.log(nf / MAX_EXACT) / np.float32(np.log(MAX_DISTANCE / MAX_EXACT))
                         * (N_BUCKETS - MAX_EXACT)).astype(jnp.int32)
    large = jnp.minimum(large, N_BUCKETS - 1)
    bucket = jnp.where(n < MAX_EXACT, n, large)
    in_window = (dist >= 0) & (dist < BLOCK)
    return bucket.astype(F32), in_window.astype(F32)


def kernel(x, g_pre_mix, w_in, w_pool, pool_scale, rel_bias, sinks, w_out, g_post_mix, g_pre_ffn, w_gate, w_up, w_down, g_post_ffn, loss_target, m_g_pre_mix, m_w_in, m_w_pool, m_pool_scale, m_rel_bias, m_sinks, m_w_out, m_g_post_mix, m_g_pre_ffn, m_w_gate, m_w_up, m_w_down, m_g_post_ffn, v_g_pre_mix, v_w_in, v_w_pool, v_pool_scale, v_rel_bias, v_sinks, v_w_out, v_g_post_mix, v_g_pre_ffn, v_w_gate, v_w_up, v_w_down, v_g_post_ffn):
    d = x.shape[-1]
    xs, target = x[0], loss_target[0]

    w_in_ts = w_in[0].T.astype(BF16)
    w_out_s = w_out[0].astype(BF16)
    gate_ts = w_gate[0].T.astype(BF16)
    up_ts = w_up[0].T.astype(BF16)
    w_down_s = w_down[0].astype(BF16)

    bucket, in_window = _bucket_band()
    w_pool_b = w_pool[0].astype(BF16)
    half = up_ts.shape[0] // 2
    proj, h1, w_in_t, up_t = norm_inproj(xs, g_pre_mix, w_in_ts, up_ts[:half], up_ts.shape[0])
    w_in_t = w_in_t.reshape(-1, d)
    cat, pooled, probs, p_sinks, gate_t, w_out_f = mixers_fwd(
        proj, bucket, in_window, rel_bias, sinks, w_pool_b, pool_scale, [gate_ts, w_out_s])
    w_out_f = w_out_f.reshape(-1, d)
    mix, x1, h2, up_t = outproj_norm(cat, w_out_f, xs, g_post_mix, g_pre_ffn, up_ts[half:], up_t)
    gate, up, act, w_down_f = ffn_up(h2, gate_t, up_t, w_down_s)
    df, dy, dg_post_ffn, loss_part = ffn_down_loss(act, w_down_f, x1, g_post_ffn, target)

    def pair_sum(parts, tag):
        return pair_add(parts, pair_exchange(parts, "pair_exchange_" + tag), "pair_add_" + tag)

    dgate, dup = ffn_down_bwd(df, w_down_f, gate, up)
    (d_gate, d_up), _ = grad_ffn([dgate, dup], h2, "grad_w_gate_up")
    (d_down,), got_gate_up = grad_ffn([act], df, "grad_w_down", [d_gate, d_up])
    q_gate, q_up, q_down = pair_add(
        [d_gate, d_up, d_down], [*got_gate_up, *pair_exchange([d_down], "pair_exchange_down")], "pair_add_ffn")
    (dx1, dmix, dg_pre_ffn, dg_post_mix), (gate_slots, down_slots) = ffn_up_bwd(
        dgate, dup, gate_t, up_t, x1, g_pre_ffn, dy, mix, g_post_mix, [q_gate, q_down])
    dcat, d_out = outproj_bwd(dmix, w_out_f, cat)
    q_out, = pair_sum([d_out], "out")
    (dproj, _, dsinks, dw_pool, dpool_scale, drel_bias), (up_slots, out_slots) = mixers_bwd(
        proj, dcat, pooled, probs, p_sinks, w_pool_b, pool_scale, bucket, [q_up, q_out])
    grad_x, dg_pre_mix = inproj_bwd(dproj, w_in_t, xs, g_pre_mix, dx1)

    small_w = [g_pre_mix, g_post_mix, g_pre_ffn, g_post_ffn, pool_scale, sinks, w_pool, rel_bias.T]
    small_m = [m_g_pre_mix, m_g_post_mix, m_g_pre_ffn, m_g_post_ffn, m_pool_scale, m_sinks, m_w_pool, m_rel_bias.T]
    small_v = [v_g_pre_mix, v_g_post_mix, v_g_pre_ffn, v_g_post_ffn, v_pool_scale, v_sinks, v_w_pool, v_rel_bias.T]
    d_in_t, total, total_rb = grad_w_in_small_reduce(
        dproj, h1, [dg_pre_mix, dg_post_mix, dg_pre_ffn, dg_post_ffn], dpool_scale, dsinks, loss_part, dw_pool, drel_bias)
    g_in_t = reduce_w_in(d_in_t)
    loss_row, sm = small_adamw(total, total_rb, small_w, small_m, small_v)
    sm[7] = [r.T for r in sm[7]]
    big_w = [w_in[0].T, w_out[0], w_gate[0].T, w_up[0].T, w_down[0]]
    big_m = [m_w_in[0].T, m_w_out[0], m_w_gate[0].T, m_w_up[0].T, m_w_down[0]]
    big_v = [v_w_in[0].T, v_w_out[0], v_w_gate[0].T, v_w_up[0].T, v_w_down[0]]
    upd = sum_adamw([out_slots, gate_slots, up_slots, down_slots], big_w[1:], big_m[1:], big_v[1:], "sum_adamw")
    upd = [[g_in_t, *adamw_update(big_w[:1], [g_in_t], big_m[:1], big_v[:1], "adamw_in")[0]], *upd]
    back = lambda k, a: (a.T if k in (0, 2, 3) else a)[None]
    big = [[back(k, u) for u in upd[k]] for k in range(5)]

    def ordered(kind):
        s, b = [p[kind] for p in sm], [p[kind] for p in big]
        return [s[0], b[0], s[6], s[4], s[7], s[5], b[1], s[1], s[2], b[2], b[3], b[4], s[3]]

    return (loss_row[0, 0], grad_x[None], *ordered(0), *ordered(1), *ordered(2), *ordered(3))
```

```python
import numpy as np
import jax
import jax.numpy as jnp
from jax import lax
from jax.experimental import pallas as pl
from jax.experimental.pallas import tpu as pltpu

F32 = jnp.float32
BF16 = jnp.bfloat16

N_DEV = 8
N_CHIP = 4
POOL_WIDTH = 512
POOL_WINDOWS = (2, 4, 8, 16)
POOL_GROUP_DIM = 128
HEAD_DIM = 64
N_Q_HEADS = 8
N_KV_HEADS = 2
GQA_GROUP = 4
BLOCK = 128
HALO = 16
ROW_CHUNK = 32
N_BUCKETS = 32
MAX_EXACT = 16
MAX_DISTANCE = 128
EPS = 1e-6
NEG_INF = -1e30
ATTN_SCALE = float(1.0 / np.sqrt(np.float32(HEAD_DIM)))

ADAM_LR = 0.001
ADAM_B1 = 0.9
ADAM_B2 = 0.999
ADAM_EPS = 1e-08
ADAM_WD = 0.01
ADAM_STEP = 10

TOKEN_TILE = 1024
WIDE_K_TOKEN_TILE = 512
FFN_TOKEN_TILE = 1024
FF_SHARDS_PER_TILE = 4
VMEM_LIMIT = 56 * 1024 * 1024
MESH = pl.DeviceIdType.MESH
PAIR_COLLECTIVE_ID = 0
GATHER_COLLECTIVE_ID = 1
CHIP_COLLECTIVE_ID = 2
ANY = pl.BlockSpec(memory_space=pl.ANY)
VMEM = pl.BlockSpec(memory_space=pltpu.VMEM)
SMEM = pl.BlockSpec(memory_space=pltpu.SMEM)


def _params(**kw):
    return pltpu.CompilerParams(vmem_limit_bytes=VMEM_LIMIT, **kw)


def _dot(a, b):
    return jnp.dot(a, b, preferred_element_type=F32)


def _dot_nt(a, b):
    return lax.dot_general(a, b, (((1,), (1,)), ((), ())), preferred_element_type=F32)


def _dot_tn(a, b):
    return lax.dot_general(a, b, (((0,), (0,)), ((), ())), preferred_element_type=F32)


def _rstd(v):
    return lax.rsqrt(jnp.mean(v * v, axis=-1, keepdims=True) + EPS)


def _norm_bwd(dout, v, r, g):
    vn = v * r
    dn = dout * g
    dv = r * (dn - vn * jnp.mean(dn * vn, axis=-1, keepdims=True))
    return dv, dout * vn


def _as_rows(v):
    return jnp.concatenate([v[:, k:k + 128] for k in range(0, v.shape[1], 128)], axis=0)


def _as_lanes(rows):
    return jnp.concatenate([rows[k:k + 1, :] for k in range(rows.shape[0])], axis=1)


def _handshake(peers):
    barrier = pltpu.get_barrier_semaphore()
    for peer in peers:
        pl.semaphore_signal(barrier, inc=1, device_id=peer, device_id_type=MESH)
    pl.semaphore_wait(barrier, len(peers))


def _merge_rows(value):
    s, r, c_ = value.shape
    return value.reshape(s * r, c_)


def _gather_plan(srcs, outs, send_sems, recv_sems, local_sems=None, bounce=None, rows=None, shake=True):
    n = len(srcs)
    x, y, c = lax.axis_index("x"), lax.axis_index("y"), lax.axis_index("c")
    me, sibling = (x, y, c), (x, y, 1 - c)
    chips = [(1 - x, y), (x, 1 - y), (1 - x, 1 - y)]

    def slot(a, px, py, pc):
        whole = outs[a].at[4 * px + 2 * py + pc]
        return whole if rows is None or rows[a] is None else whole.at[pl.ds(*rows[a])]

    def copy(a, k, block, to, from_src=False):
        return pltpu.make_async_remote_copy(
            src_ref=srcs[a] if from_src else slot(a, *block), dst_ref=slot(a, *block),
            send_sem=send_sems.at[k * n + a], recv_sem=recv_sems.at[k * n + a], device_id=to, device_id_type=MESH)

    def own_in(a):
        return pltpu.make_async_copy(srcs[a], bounce[a], local_sems.at[a])

    def own_out(a):
        return pltpu.make_async_copy(bounce[a], slot(a, *me), local_sems.at[a])

    def first(a):
        return [copy(a, 0, me, sibling, True)] + [copy(a, 1 + j, me, (*chip, c), True) for j, chip in enumerate(chips)]

    def passed(a, j):
        return copy(a, 4 + j, (*chips[j], c), sibling)

    def start():
        if shake:
            _handshake([sibling] + [(*chip, c) for chip in chips])
        for a in range(n):
            if bounce is not None:
                own_in(a).start()
            for cp in first(a):
                cp.start()

    def forward():
        if bounce is not None:
            for a in range(n):
                own_in(a).wait()
                own_out(a).start()
        for j, chip in enumerate(chips):
            for a in range(n):
                copy(a, 1 + j, (*chip, c), me).wait_recv()
                passed(a, j).start()

    def complete():
        for a in range(n):
            copy(a, 0, sibling, me).wait_recv()
            for j, chip in enumerate(chips):
                copy(a, 4 + j, (*chip, 1 - c), me).wait_recv()
        for a in range(n):
            for cp in first(a) + [passed(a, j) for j in range(3)]:
                cp.wait_send()
            if bounce is not None:
                own_out(a).wait()

    def finish():
        forward()
        complete()

    finish.forward, finish.complete = forward, complete
    return start, finish


def _gather_scratch(shards):
    n = len(shards)
    return [pltpu.SemaphoreType.DMA((7 * n,)), pltpu.SemaphoreType.DMA((7 * n,)), pltpu.SemaphoreType.DMA((n,))] \
        + [pltpu.VMEM(s.shape, s.dtype) for s in shards]


def _chip_exchange_plan(srcs, outs, send_sems, recv_sems, local_sems, bounce):
    n = len(srcs)
    x, y, c = lax.axis_index("x"), lax.axis_index("y"), lax.axis_index("c")
    my_chip = 2 * x + y

    def copies():
        out = []
        for a in range(n):
            for k in range(1, N_CHIP):
                px, py = x ^ (k >> 1), y ^ (k & 1)
                out.append(pltpu.make_async_remote_copy(
                    src_ref=srcs[a].at[2 * px + py], dst_ref=outs[a].at[my_chip],
                    send_sem=send_sems.at[(k - 1) * n + a], recv_sem=recv_sems.at[(k - 1) * n + a],
                    device_id=(px, py, c), device_id_type=MESH))
        return out

    def own_in(a):
        return pltpu.make_async_copy(srcs[a].at[my_chip], bounce[a], local_sems.at[a])

    def own_out(a):
        return pltpu.make_async_copy(bounce[a], outs[a].at[my_chip], local_sems.at[a])

    def start():
        _handshake([(x ^ (k >> 1), y ^ (k & 1), c) for k in range(1, N_CHIP)])
        for a in range(n):
            own_in(a).start()
        for cp in copies():
            cp.start()

    def finish():
        for a in range(n):
            own_in(a).wait()
            own_out(a).start()
        for cp in copies():
            cp.wait()
        for a in range(n):
            own_out(a).wait()

    return start, finish


def _chip_exchange_scratch(parts):
    n = len(parts)
    return [pltpu.SemaphoreType.DMA((3 * n,)), pltpu.SemaphoreType.DMA((3 * n,)), pltpu.SemaphoreType.DMA((n,))] \
        + [pltpu.VMEM(p.shape[1:], p.dtype) for p in parts]


def _pair_plan(srcs, outs, send_sems, recv_sems):
    x, y, c = lax.axis_index("x"), lax.axis_index("y"), lax.axis_index("c")

    def copies():
        return [pltpu.make_async_remote_copy(
            src_ref=srcs[a].at[1 - c], dst_ref=outs[a], send_sem=send_sems.at[a], recv_sem=recv_sems.at[a],
            device_id=(x, y, 1 - c), device_id_type=MESH) for a in range(len(srcs))]

    def start():
        _handshake([(x, y, 1 - c)])
        for cp in copies():
            cp.start()

    def finish():
        for cp in copies():
            cp.wait()

    return start, finish


def pair_exchange(parts, name):
    n = len(parts)

    def body(*refs):
        start, finish = _pair_plan(refs[:n], refs[n:2 * n], *refs[2 * n:])
        start()
        finish()

    return pl.pallas_call(
        body, name=name, out_shape=[jax.ShapeDtypeStruct(p.shape[1:], p.dtype) for p in parts],
        in_specs=[ANY] * n, out_specs=[ANY] * n,
        scratch_shapes=[pltpu.SemaphoreType.DMA((n,)), pltpu.SemaphoreType.DMA((n,))],
        compiler_params=_params(collective_id=PAIR_COLLECTIVE_ID),
    )(*parts)


def pair_add(parts, got, name):
    n = len(parts)

    def body(core_ref, *refs):
        for a in range(n):
            refs[2 * n + a][...] = (refs[a][...].astype(F32) + refs[n + a][...].astype(F32)).astype(BF16)

    def own(p):
        zeros = (0,) * (p.ndim - 2)
        return pl.BlockSpec((None, 1, *p.shape[2:]), lambda i, core: (core[0], i, *zeros))

    def plain(p):
        zeros = (0,) * (p.ndim - 1)
        return pl.BlockSpec((1, *p.shape[1:]), lambda i, core: (i, *zeros))

    core = lax.axis_index("c").astype(jnp.int32).reshape(1)
    return pl.pallas_call(
        body, name=name,
        grid_spec=pltpu.PrefetchScalarGridSpec(
            num_scalar_prefetch=1, grid=(got[0].shape[0],),
            in_specs=[own(p) for p in parts] + [plain(p) for p in got], out_specs=[plain(p) for p in got]),
        out_shape=[jax.ShapeDtypeStruct(p.shape, BF16) for p in got],
        compiler_params=_params(),
    )(core, *parts, *got)


def _adamw(w, g, m, v):
    m2 = ADAM_B1 * m + (1.0 - ADAM_B1) * g
    v2 = ADAM_B2 * v + (1.0 - ADAM_B2) * (g * g)
    m_hat = m2 / (1.0 - ADAM_B1 ** ADAM_STEP)
    v_hat = v2 / (1.0 - ADAM_B2 ** ADAM_STEP)
    delta = -ADAM_LR * (m_hat / (jnp.sqrt(v_hat) + ADAM_EPS) + ADAM_WD * w)
    return delta, m2, v2


def sum_adamw(slots, ws, ms, vs, name):
    n = len(ws)
    halves = 2

    def body(*refs):
        for a in range(n):
            total = refs[a][0].astype(F32)
            for s in range(1, slots[a].shape[0]):
                total = total + refs[a][s].astype(F32)
            delta, m2, v2 = _adamw(refs[n + a][...], total, refs[2 * n + a][...], refs[3 * n + a][...])
            for q, val in enumerate((total, delta, m2, v2)):
                refs[4 * n + 4 * a + q][...] = val

    def rows(w):
        return pl.BlockSpec((w.shape[0] // halves, w.shape[1]), lambda i: (i, 0))

    def slot_rows(p):
        return pl.BlockSpec((p.shape[0], p.shape[1] // halves, p.shape[2]), lambda i: (0, i, 0))

    out = pl.pallas_call(
        body, name=name, grid=(halves,),
        out_shape=[jax.ShapeDtypeStruct(w.shape, F32) for w in ws for _ in range(4)],
        in_specs=[slot_rows(p) for p in slots] + [rows(w) for w in ws] * 3,
        out_specs=[rows(w) for w in ws for _ in range(4)],
        compiler_params=_params(),
    )(*slots, *ws, *ms, *vs)
    return [out[4 * a:4 * a + 4] for a in range(n)]


def adamw_update(ws, gs, ms, vs, name):
    n = len(ws)

    def body(*refs):
        for a in range(n):
            delta, m2, v2 = _adamw(refs[a][...], refs[n + a][...], refs[2 * n + a][...], refs[3 * n + a][...])
            refs[4 * n + 3 * a][...] = delta
            refs[4 * n + 3 * a + 1][...] = m2
            refs[4 * n + 3 * a + 2][...] = v2

    out = pl.pallas_call(
        body, name=name,
        out_shape=[jax.ShapeDtypeStruct(w.shape, F32) for w in ws for _ in range(3)],
        in_specs=[VMEM] * (4 * n), out_specs=[VMEM] * (3 * n),
        compiler_params=_params(),
    )(*ws, *gs, *ms, *vs)
    return [out[3 * a:3 * a + 3] for a in range(n)]


GAIN_ROWS = 8
ROW_POOL_SCALE = 4 * GAIN_ROWS
ROW_SINKS = ROW_POOL_SCALE + 4
ROW_LOSS = ROW_SINKS + 1
ROW_W_POOL = 40
SMALL_ROWS = ROW_W_POOL + 4 * POOL_GROUP_DIM


def grad_w_in_small_reduce(a, b, gains, dpool_scale, dsinks, loss_part, dw_pool, drel_bias):
    t, m = a.shape
    d = b.shape[1]
    r = m // N_DEV
    tt = TOKEN_TILE
    last = t // tt - 1

    def body(a_ref, b_ref, g0, g1, g2, g3, dsc_ref, dsink_ref, loss_ref, dwp_ref, drb_ref, out_ref, total_ref, total_rb_ref,
             acc, stage, gat, gat_rb, g_send, g_recv):
        k = pl.program_id(0)
        x, y, c = lax.axis_index("x"), lax.axis_index("y"), lax.axis_index("c")
        start, finish = _gather_plan([stage, drb_ref], [gat, gat_rb], g_send, g_recv)

        @pl.when(k == 0)
        def _():
            for q, g_ref in enumerate((g0, g1, g2, g3)):
                stage[GAIN_ROWS * q:GAIN_ROWS * (q + 1), :] = g_ref[...]
            stage[ROW_POOL_SCALE:ROW_SINKS, :] = dsc_ref[...]
            stage[ROW_SINKS:ROW_LOSS, :] = dsink_ref[...]
            stage[ROW_LOSS:ROW_LOSS + 1, :] = loss_ref[...]
            stage[ROW_LOSS + 1:ROW_W_POOL, :] = jnp.zeros((ROW_W_POOL - ROW_LOSS - 1, 128), F32)
            stage[ROW_W_POOL:, :] = dwp_ref[...].reshape(4 * POOL_GROUP_DIM, POOL_GROUP_DIM)
            gat[4 * x + 2 * y + c] = stage[...]
            gat_rb[4 * x + 2 * y + c] = drb_ref[...]
            start()
            acc[...] = jnp.zeros_like(acc)

        acc[...] += _dot_tn(a_ref[...], b_ref[...])

        @pl.when(k == last)
        def _():
            blocks = acc[...].reshape(N_CHIP, 2, r, d)
            for chip in range(N_CHIP):
                for core in range(2):
                    out_ref[core, chip] = blocks[chip, core].astype(BF16)
            finish()
            total, total_rb = gat[0], gat_rb[0]
            for s in range(1, N_DEV):
                total, total_rb = total + gat[s], total_rb + gat_rb[s]
            total_ref[...] = total
            total_rb_ref[...] = total_rb

    out_shape = (2, N_CHIP, r, d)
    return pl.pallas_call(
        body, name="grad_w_in", grid=(t // tt,),
        out_shape=[jax.ShapeDtypeStruct(out_shape, BF16), jax.ShapeDtypeStruct((SMALL_ROWS, 128), F32),
                   jax.ShapeDtypeStruct(drel_bias.shape, F32)],
        in_specs=[pl.BlockSpec((tt, m), lambda k: (k, 0)), pl.BlockSpec((tt, d), lambda k: (k, 0))] + [VMEM] * 9,
        out_specs=[pl.BlockSpec(out_shape, lambda k: (0,) * len(out_shape)), VMEM, VMEM],
        scratch_shapes=[pltpu.VMEM((m, d), F32), pltpu.VMEM((SMALL_ROWS, 128), F32),
                        pltpu.VMEM((N_DEV, SMALL_ROWS, 128), F32), pltpu.VMEM((N_DEV, *drel_bias.shape), F32),
                        pltpu.SemaphoreType.DMA((14,)), pltpu.SemaphoreType.DMA((14,))],
        compiler_params=_params(collective_id=GATHER_COLLECTIVE_ID),
    )(a, b, *gains, dpool_scale, dsinks, loss_part, dw_pool, drel_bias)


def reduce_w_in(d_in_t):
    def body(d_in_ref, g_in_ref, pair_got, chip_part, chip_got, p_send, p_recv, x_send, x_recv):
        x, y, c = lax.axis_index("x"), lax.axis_index("y"), lax.axis_index("c")
        my_chip = 2 * x + y
        _handshake([(x, y, 1 - c)] + [(x ^ (k >> 1), y ^ (k & 1), c) for k in range(1, N_CHIP)])
        pair = pltpu.make_async_remote_copy(
            src_ref=d_in_ref.at[1 - c], dst_ref=pair_got, send_sem=p_send, recv_sem=p_recv,
            device_id=(x, y, 1 - c), device_id_type=MESH)
        pair.start()
        pair.wait()
        chip_part[...] = (d_in_ref[c].astype(F32) + pair_got[...].astype(F32)).astype(BF16)
        copies = []
        for k in range(1, N_CHIP):
            px, py = x ^ (k >> 1), y ^ (k & 1)
            copies.append(pltpu.make_async_remote_copy(
                src_ref=chip_part.at[2 * px + py], dst_ref=chip_got.at[my_chip],
                send_sem=x_send.at[k - 1], recv_sem=x_recv.at[k - 1], device_id=(px, py, c), device_id_type=MESH))
        for cp in copies:
            cp.start()
        chip_got[my_chip] = chip_part[my_chip]
        for cp in copies:
            cp.wait()
        g_in = chip_got[0].astype(F32)
        for s in range(1, N_CHIP):
            g_in = g_in + chip_got[s].astype(F32)
        g_in_ref[...] = g_in

    per_core = d_in_t.shape[1:]
    return pl.pallas_call(
        body, name="reduce_w_in",
        out_shape=jax.ShapeDtypeStruct(d_in_t.shape[2:], F32),
        in_specs=[VMEM], out_specs=VMEM,
        scratch_shapes=[pltpu.VMEM(per_core, d_in_t.dtype), pltpu.VMEM(per_core, d_in_t.dtype),
                        pltpu.VMEM(per_core, d_in_t.dtype),
                        pltpu.SemaphoreType.DMA, pltpu.SemaphoreType.DMA,
                        pltpu.SemaphoreType.DMA((3,)), pltpu.SemaphoreType.DMA((3,))],
        compiler_params=_params(collective_id=GATHER_COLLECTIVE_ID),
    )(d_in_t)


def small_adamw(total, total_rb, small_w, small_m, small_v):
    n_small = len(small_w)

    def body(*refs):
        total_ref, rb_ref = refs[:2]
        w_refs, m_refs, v_refs = (refs[2 + k * n_small:2 + (k + 1) * n_small] for k in range(3))
        loss_out = refs[2 + 3 * n_small]
        result = refs[3 + 3 * n_small:]
        total = total_ref[...]
        loss_out[...] = total[ROW_LOSS:ROW_LOSS + 1, :]
        grads = [_as_lanes(total[GAIN_ROWS * k:GAIN_ROWS * (k + 1), :]) for k in range(4)]
        grads.append(_as_lanes(total[ROW_POOL_SCALE:ROW_SINKS, :]))
        grads.append(total[ROW_SINKS:ROW_LOSS, 0:N_Q_HEADS])
        grads.append(total[ROW_W_POOL:, :].reshape(w_refs[6].shape))
        grads.append(rb_ref[...])
        for k in range(n_small):
            delta, m2, v2 = _adamw(w_refs[k][...], grads[k], m_refs[k][...], v_refs[k][...])
            result[4 * k][...] = grads[k]
            result[4 * k + 1][...] = delta
            result[4 * k + 2][...] = m2
            result[4 * k + 3][...] = v2

    out = pl.pallas_call(
        body, name="small_adamw",
        out_shape=[jax.ShapeDtypeStruct((1, 128), F32)] + [jax.ShapeDtypeStruct(w.shape, F32) for w in small_w for _ in range(4)],
        in_specs=[VMEM] * (2 + 3 * n_small), out_specs=[VMEM] * (1 + 4 * n_small),
        compiler_params=_params(),
    )(total, total_rb, *small_w, *small_m, *small_v)
    return out[0], [out[1 + 4 * k:5 + 4 * k] for k in range(n_small)]


def norm_inproj(x, g, w_shard, shard, shard_rows):
    t, d = x.shape
    r = w_shard.shape[0]
    tm = TOKEN_TILE
    nt = t // tm

    def body(x_ref, g_ref, w_shard_ref, shard_ref, proj_ref, h_ref, w_ref, gathered_ref, h_all, w_all, w_sem,
             send_w, recv_w, local_w, bounce_w, send_sems, recv_sems, local_sems, bounce):
        i = pl.program_id(0)
        start_w, finish_w = _gather_plan([w_shard_ref], [w_ref], send_w, recv_w, local_w, [bounce_w])
        start, finish = _gather_plan([shard_ref], [gathered_ref], send_sems, recv_sems, local_sems, [bounce],
                                     [(0, shard.shape[0])], shake=False)

        @pl.when(i == 0)
        def _():
            start_w()
            start()

        @pl.when(i < nt)
        def _():
            xv = x_ref[...]
            h = ((xv * _rstd(xv)) * g_ref[...]).astype(BF16)
            h_ref[...] = h
            h_all[pl.ds(pl.multiple_of(i * tm, tm), tm), :] = h

        @pl.when(i == nt - 1)
        def _():
            finish_w()
            landed = pltpu.make_async_copy(w_ref, w_all, w_sem)
            landed.start()
            landed.wait()

        @pl.when(i >= nt)
        def _():
            rows = pl.ds(pl.multiple_of((i - nt) * tm, tm), tm)
            proj_ref[...] = _dot_nt(h_all[rows, :], _merge_rows(w_all[...]))

        pl.when(i == 2 * nt - 2)(finish.forward)
        pl.when(i == 2 * nt - 1)(finish.complete)

    first = lambda i: (jnp.minimum(i, nt - 1), 0)
    return pl.pallas_call(
        body, name="norm_inproj", grid=(2 * nt,),
        out_shape=[jax.ShapeDtypeStruct((t, N_DEV * r), F32), jax.ShapeDtypeStruct((t, d), BF16),
                   jax.ShapeDtypeStruct((N_DEV, r, d), w_shard.dtype),
                   jax.ShapeDtypeStruct((N_DEV, shard_rows, d), shard.dtype)],
        in_specs=[pl.BlockSpec((tm, d), first), pl.BlockSpec((1, d), lambda i: (0, 0)), ANY, ANY],
        out_specs=[pl.BlockSpec((tm, N_DEV * r), lambda i: (jnp.maximum(i - nt, 0), 0)), pl.BlockSpec((tm, d), first),
                   ANY, ANY],
        scratch_shapes=[pltpu.VMEM((t, d), BF16), pltpu.VMEM((N_DEV, r, d), w_shard.dtype), pltpu.SemaphoreType.DMA]
        + _gather_scratch([w_shard]) + _gather_scratch([shard]),
        compiler_params=_params(collective_id=GATHER_COLLECTIVE_ID),
    )(x, g, w_shard, shard)


def _fill_bias_band(bk_ref, win_ref, rb_ref, biasm_ref):
    bk = bk_ref[...]
    keep = win_ref[...] > 0.5
    for h in range(N_Q_HEADS):
        acc = jnp.zeros(bk.shape, F32)
        for b in range(N_BUCKETS):
            acc = jnp.where(bk == float(b), rb_ref[b, h], acc)
        biasm_ref[h] = jnp.where(keep, acc, NEG_INF)


def _window_sum(buf_ref, g, w, first):
    cols = slice(g * POOL_GROUP_DIM, (g + 1) * POOL_GROUP_DIM)
    acc = None
    for k in range(w):
        piece = buf_ref[first(k):first(k) + BLOCK, cols]
        acc = piece if acc is None else acc + piece
    return acc


def _inv_count(i, w):
    row = lax.broadcasted_iota(jnp.int32, (BLOCK, 1), 0)
    return 1.0 / jnp.minimum(i * BLOCK + row + 1, w).astype(F32)


def _fill_pool_input(i, ubuf, uc_ref, halo_ref):
    ubuf[0:HALO, :] = jnp.where(i > 0, halo_ref[...], 0.0)
    ubuf[HALO:, :] = uc_ref[...]


def _pooled(i, g, w, ubuf):
    cols = slice(g * POOL_GROUP_DIM, (g + 1) * POOL_GROUP_DIM)
    return _window_sum(ubuf, g, w, lambda k: HALO - k) * _inv_count(i, w) - ubuf[HALO:, cols]


def _head_variants(pair):
    low = lax.broadcasted_iota(jnp.int32, pair.shape, 1) < HEAD_DIM
    swapped = pltpu.roll(pair, HEAD_DIM, 1)
    zero = jnp.zeros_like(pair)
    pick = lambda c, a, b: jnp.where(c, a, b).astype(BF16)
    return [[pick(low, pair, zero), pick(low, zero, swapped)], [pick(low, swapped, zero), pick(low, zero, pair)]]


def _head_probs(i, hq, rows, s_ref, biasm_ref, sinks_ref):
    s = s_ref[hq, rows, :] * ATTN_SCALE + biasm_ref[hq, rows, :]
    col = lax.broadcasted_iota(jnp.int32, s.shape, 1)
    s = jnp.where((i == 0) & (col < BLOCK), NEG_INF, s)
    sink = sinks_ref[0, hq]
    m = jnp.maximum(jnp.max(s, axis=-1, keepdims=True), sink)
    p = jnp.exp(s - m)
    e_sink = jnp.exp(sink - m)
    inv = 1.0 / (jnp.sum(p, axis=-1, keepdims=True) + e_sink)
    return p * inv, e_sink * inv


def _head_slot(hq):
    return 4 * (hq // GQA_GROUP) + 2 * (hq % 2) + (hq % GQA_GROUP) // 2


def _mixer_in_specs(cur, prv):
    return [pl.BlockSpec((BLOCK, 512), lambda i: (cur(i), 0)),
            pl.BlockSpec((HALO, 512), lambda i: (jnp.maximum(cur(i) * (BLOCK // HALO) - 1, 0), 0)),
            pl.BlockSpec((BLOCK, 512), lambda i: (cur(i), 1)),
            pl.BlockSpec((BLOCK, 256), lambda i: (cur(i), 4)),
            pl.BlockSpec((BLOCK, 256), lambda i: (prv(i), 4))]


def _mixer_param_specs():
    return [pl.BlockSpec((4, POOL_GROUP_DIM, POOL_GROUP_DIM), lambda i: (0, 0, 0)),
            pl.BlockSpec((1, POOL_WIDTH), lambda i: (0, 0))]


def mixers_fwd(proj, bucket, in_window, rel_bias, sinks, w_pool, pool_scale, shards):
    t = proj.shape[0]
    nb = t // BLOCK
    ns = len(shards)

    def body(*refs):
        uc_ref, halo_ref, q_ref, kvc_ref, kvp_ref, bk_ref, win_ref, rb_ref, sinks_ref, wp_ref, sc_ref = refs[:11]
        shard_refs = refs[11:11 + ns]
        out_ref, pooled_ref, p_all, psink_ref = refs[11 + ns:15 + ns]
        gathered_refs = refs[15 + ns:15 + 2 * ns]
        ubuf, s_all, biasm_ref, send_sems, recv_sems, local_sems = refs[15 + 2 * ns:21 + 2 * ns]
        i = pl.program_id(0)
        start, finish = _gather_plan(shard_refs, gathered_refs, send_sems, recv_sems, local_sems, refs[21 + 2 * ns:])

        @pl.when(i == 0)
        def _():
            start()
            _fill_bias_band(bk_ref, win_ref, rb_ref, biasm_ref)

        _fill_pool_input(i, ubuf, uc_ref, halo_ref)
        for g, w in enumerate(POOL_WINDOWS):
            cols = slice(g * POOL_GROUP_DIM, (g + 1) * POOL_GROUP_DIM)
            pooled = _pooled(i, g, w, ubuf).astype(BF16)
            pooled_ref[:, cols] = pooled
            out_ref[:, cols] = (_dot(pooled, wp_ref[g]) * sc_ref[:, cols]).astype(BF16)
        kv = jnp.concatenate([kvp_ref[...], kvc_ref[...]], axis=0)
        k_var = _head_variants(kv[:, 0:2 * HEAD_DIM])
        v_var = _head_variants(kv[:, 2 * HEAD_DIM:])
        for hq in range(N_Q_HEADS):
            j, half, h = hq // 2, hq % 2, hq // GQA_GROUP
            q2 = q_ref[:, 2 * HEAD_DIM * j:2 * HEAD_DIM * (j + 1)].astype(BF16)
            s_all[hq] = _dot_nt(q2, k_var[h][half])
        psink_ref[...] = jnp.zeros_like(psink_ref)
        for hq in range(N_Q_HEADS):
            for r in range(0, BLOCK, ROW_CHUNK):
                rows = slice(r, r + ROW_CHUNK)
                probs, p_sink = _head_probs(i, hq, rows, s_all, biasm_ref, sinks_ref)
                p_all[_head_slot(hq), rows, :] = probs.astype(BF16)
                psink_ref[rows, hq:hq + 1] = p_sink
        for j in range(N_Q_HEADS // 2):
            h = 2 * j // GQA_GROUP
            acc = _dot(p_all[_head_slot(2 * j)], v_var[h][0]) + _dot(p_all[_head_slot(2 * j + 1)], v_var[h][1])
            out_ref[:, POOL_WIDTH + 2 * HEAD_DIM * j:POOL_WIDTH + 2 * HEAD_DIM * (j + 1)] = acc.astype(BF16)

        pl.when(i == max(nb - 4, 0))(finish.forward)
        pl.when(i == nb - 1)(finish.complete)

    return pl.pallas_call(
        body, name="mixers_fwd", grid=(nb,),
        out_shape=[jax.ShapeDtypeStruct((t, 2 * POOL_WIDTH), BF16), jax.ShapeDtypeStruct((t, POOL_WIDTH), BF16),
                   jax.ShapeDtypeStruct((N_Q_HEADS, t, 2 * BLOCK), BF16), jax.ShapeDtypeStruct((t, 128), F32)]
        + [jax.ShapeDtypeStruct((N_DEV, *sh.shape), sh.dtype) for sh in shards],
        in_specs=_mixer_in_specs(lambda i: i, lambda i: jnp.maximum(i - 1, 0))
        + [pl.BlockSpec((BLOCK, 2 * BLOCK), lambda i: (0, 0))] * 2 + [SMEM, SMEM] + _mixer_param_specs() + [ANY] * ns,
        out_specs=[pl.BlockSpec((BLOCK, 2 * POOL_WIDTH), lambda i: (i, 0)), pl.BlockSpec((BLOCK, POOL_WIDTH), lambda i: (i, 0)),
                   pl.BlockSpec((N_Q_HEADS, BLOCK, 2 * BLOCK), lambda i: (0, i, 0)), pl.BlockSpec((BLOCK, 128), lambda i: (i, 0))]
        + [ANY] * ns,
        scratch_shapes=[pltpu.VMEM((HALO + BLOCK, POOL_WIDTH), F32), pltpu.VMEM((N_Q_HEADS, BLOCK, 2 * BLOCK), F32),
                        pltpu.VMEM((N_Q_HEADS, BLOCK, 2 * BLOCK), F32)]
        + _gather_scratch(shards),
        compiler_params=_params(collective_id=GATHER_COLLECTIVE_ID),
    )(proj, proj, proj, proj, proj, bucket, in_window, rel_bias, sinks, w_pool, pool_scale, *shards)


def outproj_norm(cat, w, x, g, g_next, shard, partial):
    t, d = x.shape
    tm = TOKEN_TILE
    last = t // tm - 1
    rows = [(partial.shape[1] - shard.shape[0], shard.shape[0])]

    def body(c_ref, w_ref, x_ref, g_ref, gn_ref, shard_ref, partial_ref, mix_ref, x1_ref, h2_ref, gathered_ref,
             send_sems, recv_sems, local_sems, bounce):
        i = pl.program_id(0)
        start, finish = _gather_plan([shard_ref], [gathered_ref], send_sems, recv_sems, local_sems, [bounce], rows)
        pl.when(i == 0)(start)
        mix = _dot(c_ref[...], w_ref[...])
        mix_ref[...] = mix
        x1 = x_ref[...] + (mix * _rstd(mix)) * g_ref[...]
        x1_ref[...] = x1
        h2_ref[...] = ((x1 * _rstd(x1)) * gn_ref[...]).astype(BF16)
        pl.when(i == max(last - 1, 0))(finish.forward)
        pl.when(i == last)(finish.complete)

    row = pl.BlockSpec((tm, d), lambda i: (i, 0))
    gain = pl.BlockSpec((1, d), lambda i: (0, 0))
    return pl.pallas_call(
        body, name="outproj_norm", grid=(t // tm,),
        out_shape=[jax.ShapeDtypeStruct((t, d), F32), jax.ShapeDtypeStruct((t, d), F32), jax.ShapeDtypeStruct((t, d), BF16),
                   jax.ShapeDtypeStruct(partial.shape, partial.dtype)],
        in_specs=[pl.BlockSpec((tm, cat.shape[1]), lambda i: (i, 0)), pl.BlockSpec(w.shape, lambda i: (0, 0)), row, gain, gain,
                  ANY, ANY],
        out_specs=[row, row, row, ANY],
        input_output_aliases={6: 3},
        scratch_shapes=_gather_scratch([shard]),
        compiler_params=_params(collective_id=GATHER_COLLECTIVE_ID),
    )(cat, w, x, g, g_next, shard, partial)


def ffn_up(h, gate_t, up_t, down_shard):
    t, d = h.shape
    n = gate_t.shape[1]
    f = N_DEV * n
    tm, ts = FFN_TOKEN_TILE, FF_SHARDS_PER_TILE
    tn = ts * n
    steps = (f // tn, t // tm)

    def body(h_ref, wg_ref, wu_ref, shard_ref, gate_ref, up_ref, a_ref, gathered_ref,
             send_sems, recv_sems, local_sems, bounce):
        j, i = pl.program_id(0), pl.program_id(1)
        start, finish = _gather_plan([shard_ref], [gathered_ref], send_sems, recv_sems, local_sems, [bounce])
        pl.when((i == 0) & (j == 0))(start)

        hv = h_ref[...]
        gate = _dot_nt(hv, _merge_rows(wg_ref[...]))
        up = _dot_nt(hv, _merge_rows(wu_ref[...]))
        gate_ref[...] = gate.astype(BF16)
        up_ref[...] = up.astype(BF16)
        a_ref[...] = (gate * (1.0 / (1.0 + jnp.exp(-gate))) * up).astype(BF16)

        pl.when((j == steps[0] - 1) & (i == max(steps[1] - 2, 0)))(finish.forward)
        pl.when((j == steps[0] - 1) & (i == steps[1] - 1))(finish.complete)

    wide = pl.BlockSpec((tm, tn), lambda j, i: (i, j))
    return pl.pallas_call(
        body, name="ffn_up", grid=steps,
        out_shape=[jax.ShapeDtypeStruct((t, f), BF16)] * 3
        + [jax.ShapeDtypeStruct((N_DEV, *down_shard.shape), down_shard.dtype)],
        in_specs=[pl.BlockSpec((tm, d), lambda j, i: (i, 0)),
                  pl.BlockSpec((ts, n, d), lambda j, i: (j, 0, 0)),
                  pl.BlockSpec((ts, n, d), lambda j, i: (j, 0, 0)), ANY],
        out_specs=[wide, wide, wide, ANY],
        scratch_shapes=_gather_scratch([down_shard]),
        compiler_params=_params(collective_id=GATHER_COLLECTIVE_ID),
    )(h, gate_t, up_t, down_shard)


def ffn_down_loss(a, w_down, x1, g, target):
    t, d = x1.shape
    tm = WIDE_K_TOKEN_TILE

    def body(a_ref, w_ref, x_ref, g_ref, t_ref, df_ref, dy_ref, dg_ref, loss_ref):
        @pl.when(pl.program_id(0) == 0)
        def _():
            dg_ref[...] = jnp.zeros_like(dg_ref)
            loss_ref[...] = jnp.zeros_like(loss_ref)

        f = _dot(a_ref[...], _merge_rows(w_ref[...]))
        r = _rstd(f)
        g = g_ref[...]
        err = x_ref[...] + (f * r) * g - t_ref[...]
        loss_ref[...] += 0.5 * jnp.sum(jnp.mean(err * err, axis=-1, keepdims=True))
        dy = err * (1.0 / d)
        dy_ref[...] = dy
        df, dg_rows = _norm_bwd(dy, f, r, g)
        df_ref[...] = df.astype(BF16)
        dg_ref[...] += _as_rows(jnp.sum(dg_rows, axis=0, keepdims=True))

    row = pl.BlockSpec((tm, d), lambda i: (i, 0))
    gain = pl.BlockSpec((1, d), lambda i: (0, 0))
    return pl.pallas_call(
        body, name="ffn_down_loss", grid=(t // tm,),
        out_shape=[jax.ShapeDtypeStruct((t, d), BF16), jax.ShapeDtypeStruct((t, d), F32),
                   jax.ShapeDtypeStruct((d // 128, 128), F32), jax.ShapeDtypeStruct((1, 128), F32)],
        in_specs=[pl.BlockSpec((tm, a.shape[1]), lambda i: (i, 0)), pl.BlockSpec(w_down.shape, lambda i: (0, 0, 0)), row, gain, row],
        out_specs=[row, row, pl.BlockSpec((d // 128, 128), lambda i: (0, 0)), pl.BlockSpec((1, 128), lambda i: (0, 0))],
        compiler_params=_params(),
    )(a, w_down, x1, g, target)


def ffn_down_bwd(df, w_down, gate, up, h):
    t, d = df.shape
    n = w_down.shape[1]
    f = gate.shape[1]
    tm, ts = WIDE_K_TOKEN_TILE, FF_SHARDS_PER_TILE
    tn = ts * n
    last = t // tm - 1

    def body(df_ref, w_ref, gate_ref, up_ref, h_ref, dgate_ref, dup_ref, dwg_ref, dwu_ref, acc):
        i = pl.program_id(1)

        @pl.when(i == 0)
        def _():
            acc[...] = jnp.zeros_like(acc)

        da = _dot_nt(df_ref[...], _merge_rows(w_ref[...]))
        gate = gate_ref[...].astype(F32)
        sig = 1.0 / (1.0 + jnp.exp(-gate))
        dgate = (da * up_ref[...].astype(F32) * (sig * (1.0 + gate * (1.0 - sig)))).astype(BF16)
        dup = (da * (gate * sig)).astype(BF16)
        dgate_ref[...] = dgate
        dup_ref[...] = dup
        acc[0] += _dot_tn(dgate, h_ref[...])
        acc[1] += _dot_tn(dup, h_ref[...])

        @pl.when(i == last)
        def _():
            for w, out_ref in enumerate((dwg_ref, dwu_ref)):
                blocks = acc[w].reshape(ts // 2, 2, n, d)
                for chip in range(ts // 2):
                    for core in range(2):
                        out_ref[core, chip] = blocks[chip, core].astype(BF16)

    wide = pl.BlockSpec((tm, tn), lambda j, i: (i, j))
    rows = pl.BlockSpec((tm, d), lambda j, i: (i, 0))
    owned = pl.BlockSpec((2, ts // 2, n, d), lambda j, i: (0, j, 0, 0))
    out = pl.pallas_call(
        body, name="ffn_down_bwd", grid=(f // tn, t // tm),
        out_shape=[jax.ShapeDtypeStruct((t, f), BF16)] * 2 + [jax.ShapeDtypeStruct((2, N_CHIP, n, d), BF16)] * 2,
        in_specs=[rows, pl.BlockSpec((ts, n, d), lambda j, i: (j, 0, 0)), wide, wide, rows],
        out_specs=[wide, wide, owned, owned],
        scratch_shapes=[pltpu.VMEM((2, tn, d), F32)],
        compiler_params=_params(),
    )(df, w_down, gate, up, h)
    return out[:2], out[2:]


def grad_ffn(lhs, b, name, pair_parts=()):
    t, f = lhs[0].shape
    d = b.shape[1]
    nw = len(lhs)
    na = len(pair_parts)
    n = f // N_DEV
    tt, ts = TOKEN_TILE, FF_SHARDS_PER_TILE
    tn = ts * n
    steps = (f // tn, t // tt)

    def body(*refs):
        a_refs, b_ref, part_refs = refs[:nw], refs[nw], refs[nw + 1:nw + 1 + na]
        out_refs = refs[nw + 1 + na:2 * nw + 1 + na]
        got_refs = refs[2 * nw + 1 + na:2 * nw + 1 + 2 * na]
        acc = refs[2 * nw + 1 + 2 * na]
        i, k = pl.program_id(0), pl.program_id(1)
        if na:
            start, finish = _pair_plan(part_refs, got_refs, *refs[2 * nw + 2 + 2 * na:])
            pl.when((i == 0) & (k == 0))(start)

        @pl.when(k == 0)
        def _():
            acc[...] = jnp.zeros_like(acc)

        for w in range(nw):
            acc[w] += _dot_tn(a_refs[w][...], b_ref[...])

        @pl.when(k == steps[1] - 1)
        def _():
            for w in range(nw):
                blocks = acc[w].reshape(ts // 2, 2, n, d)
                for chip in range(ts // 2):
                    for core in range(2):
                        out_refs[w][core, chip] = blocks[chip, core].astype(BF16)

        if na:
            pl.when((i == steps[0] - 1) & (k == steps[1] - 1))(finish)

    out = pl.pallas_call(
        body, name=name, grid=steps,
        out_shape=[jax.ShapeDtypeStruct((2, N_CHIP, n, d), BF16)] * nw
        + [jax.ShapeDtypeStruct(p.shape[1:], p.dtype) for p in pair_parts],
        in_specs=[pl.BlockSpec((tt, tn), lambda i, k: (k, i))] * nw + [pl.BlockSpec((tt, d), lambda i, k: (k, 0))] + [ANY] * na,
        out_specs=[pl.BlockSpec((2, ts // 2, n, d), lambda i, k: (0, i, 0, 0))] * nw + [ANY] * na,
        scratch_shapes=[pltpu.VMEM((nw, tn, d), F32)]
        + ([pltpu.SemaphoreType.DMA((na,)), pltpu.SemaphoreType.DMA((na,))] if na else []),
        compiler_params=_params(collective_id=PAIR_COLLECTIVE_ID) if na else _params(),
    )(*lhs, b, *pair_parts)
    return out[:nw], out[nw:]


def ffn_up_bwd(dgate, dup, gate_t, up_t, x1, g_ffn, dy, mix, g_mix, chip_parts):
    t, d = x1.shape
    n = gate_t.shape[1]
    f = N_DEV * n
    tm = WIDE_K_TOKEN_TILE
    na = len(chip_parts)
    last = t // tm - 1

    def body(*refs):
        dg_ref, du_ref, wg_ref, wu_ref, x_ref, gf_ref, dy_ref, mix_ref, gm_ref = refs[:9]
        part_refs = refs[9:9 + na]
        dx1_ref, dmix_ref, dgf_ref, dgm_ref = refs[9 + na:13 + na]
        slot_refs = refs[13 + na:13 + 2 * na]
        send_sems, recv_sems, local_sems = refs[13 + 2 * na:16 + 2 * na]
        i = pl.program_id(0)
        start, finish = _chip_exchange_plan(part_refs, slot_refs, send_sems, recv_sems, local_sems, refs[16 + 2 * na:])

        @pl.when(i == 0)
        def _():
            start()
            dgf_ref[...] = jnp.zeros_like(dgf_ref)
            dgm_ref[...] = jnp.zeros_like(dgm_ref)

        dh = _dot(dg_ref[...], _merge_rows(wg_ref[...])) + _dot(du_ref[...], _merge_rows(wu_ref[...]))
        x1 = x_ref[...]
        dx, dgf_rows = _norm_bwd(dh, x1, _rstd(x1), gf_ref[...])
        dx1 = dy_ref[...] + dx
        dx1_ref[...] = dx1
        dgf_ref[...] += _as_rows(jnp.sum(dgf_rows, axis=0, keepdims=True))
        mix = mix_ref[...]
        dmix, dgm_rows = _norm_bwd(dx1, mix, _rstd(mix), gm_ref[...])
        dmix_ref[...] = dmix.astype(BF16)
        dgm_ref[...] += _as_rows(jnp.sum(dgm_rows, axis=0, keepdims=True))
        pl.when(i == last)(finish)

    row = pl.BlockSpec((tm, d), lambda i: (i, 0))
    wide = pl.BlockSpec((tm, f), lambda i: (i, 0))
    gain = pl.BlockSpec((1, d), lambda i: (0, 0))
    gain_rows = pl.BlockSpec((d // 128, 128), lambda i: (0, 0))
    whole = pl.BlockSpec((N_DEV, n, d), lambda i: (0, 0, 0), pipeline_mode=pl.Buffered(1))
    out = pl.pallas_call(
        body, name="ffn_up_bwd", grid=(t // tm,),
        out_shape=[jax.ShapeDtypeStruct((t, d), F32), jax.ShapeDtypeStruct((t, d), BF16),
                   jax.ShapeDtypeStruct((d // 128, 128), F32), jax.ShapeDtypeStruct((d // 128, 128), F32)]
        + [jax.ShapeDtypeStruct(p.shape, p.dtype) for p in chip_parts],
        in_specs=[wide, wide, whole, whole, row, gain, row, row, gain] + [ANY] * na,
        out_specs=[row, row, gain_rows, gain_rows] + [ANY] * na,
        scratch_shapes=_chip_exchange_scratch(chip_parts),
        compiler_params=_params(collective_id=CHIP_COLLECTIVE_ID),
    )(dgate, dup, gate_t, up_t, x1, g_ffn, dy, mix, g_mix, *chip_parts)
    return out[:4], out[4:]


def outproj_bwd(dmix, w_out, cat):
    t, d = dmix.shape
    m = w_out.shape[0]
    r = m // N_DEV
    tm = TOKEN_TILE
    last = t // tm - 1

    def body(dm_ref, w_ref, cat_ref, dcat_ref, dw_ref, acc):
        k = pl.program_id(0)

        @pl.when(k == 0)
        def _():
            acc[...] = jnp.zeros_like(acc)

        dcat_ref[...] = _dot_nt(dm_ref[...], w_ref[...])
        acc[...] += _dot_tn(cat_ref[...], dm_ref[...])

        @pl.when(k == last)
        def _():
            blocks = acc[...].reshape(N_CHIP, 2, r, d)
            for chip in range(N_CHIP):
                for core in range(2):
                    dw_ref[core, chip] = blocks[chip, core].astype(BF16)

    tile = lambda width: pl.BlockSpec((tm, width), lambda k: (k, 0))
    return pl.pallas_call(
        body, name="outproj_bwd", grid=(t // tm,),
        out_shape=[jax.ShapeDtypeStruct((t, m), F32), jax.ShapeDtypeStruct((2, N_CHIP, r, d), BF16)],
        in_specs=[tile(d), pl.BlockSpec(w_out.shape, lambda k: (0, 0)), tile(m)],
        out_specs=[tile(m), pl.BlockSpec((2, N_CHIP, r, d), lambda k: (0, 0, 0, 0))],
        scratch_shapes=[pltpu.VMEM((m, d), F32)],
        compiler_params=_params(),
    )(dmix, w_out, cat)


def mixers_bwd(proj, dcat, pooled, probs, p_sinks, w_pool, pool_scale, bucket, ffn_parts):
    t = proj.shape[0]
    nb = t // BLOCK
    na = len(ffn_parts)

    def body(*refs):
        (q_ref, kvc_ref, kvp_ref, dcat_ref, pooled_ref, p_all, psink_ref, wp_ref, sc_ref, bk_ref) = refs[:10]
        part_refs = refs[10:10 + na]
        dproj_ref, dbias_ref, dsink_ref, dwp_ref, dsc_ref, drb_ref = refs[10 + na:16 + na]
        slot_refs = refs[16 + na:16 + 2 * na]
        dbuf, c_u, c_q, c_kv, dp_all, ds_all, sink_acc = refs[16 + 2 * na:23 + 2 * na]
        send_sems, recv_sems, local_sems = refs[23 + 2 * na:26 + 2 * na]
        bounce = refs[26 + 2 * na:]
        i = pl.program_id(0)
        lane = lax.broadcasted_iota(jnp.int32, (1, 128), 1)
        start, finish = _chip_exchange_plan(part_refs, slot_refs, send_sems, recv_sems, local_sems, bounce)

        @pl.when(i == 0)
        def _():
            start()
            dbias_ref[...] = jnp.zeros_like(dbias_ref)
            dwp_ref[...] = jnp.zeros_like(dwp_ref)
            dsc_ref[...] = jnp.zeros_like(dsc_ref)
            dsink_ref[...] = jnp.zeros_like(dsink_ref)
            dbuf[...] = jnp.zeros_like(dbuf)
            c_u[...] = jnp.zeros_like(c_u)
            c_q[...] = jnp.zeros_like(c_q)
            c_kv[...] = jnp.zeros_like(c_kv)

        @pl.when(i < nb)
        def _():
            for g, w in enumerate(POOL_WINDOWS):
                cols = slice(g * POOL_GROUP_DIM, (g + 1) * POOL_GROUP_DIM)
                pooled = pooled_ref[:, cols]
                mixed = _dot(pooled, wp_ref[g])
                dout = dcat_ref[:, cols]
                dsc_ref[g:g + 1, :] += jnp.sum(dout * mixed, axis=0, keepdims=True)
                dmixed = (dout * sc_ref[:, cols]).astype(BF16)
                dwp_ref[g] += _dot_tn(pooled, dmixed)
                dpooled = _dot_nt(dmixed, wp_ref[g])
                scaled = dpooled * _inv_count(i, w)
                dbuf[BLOCK:, cols] = scaled[0:HALO]
                dproj_ref[:, cols] = (_window_sum(dbuf, g, w, lambda k: k) + c_u[:, cols]).astype(BF16)
                dbuf[0:BLOCK, cols] = scaled
                c_u[:, cols] = -dpooled

            kv = jnp.concatenate([kvp_ref[...], kvc_ref[...]], axis=0)
            k_var = _head_variants(kv[:, 0:2 * HEAD_DIM])
            v_var = _head_variants(kv[:, 2 * HEAD_DIM:])
            q2s = [q_ref[:, 2 * HEAD_DIM * j:2 * HEAD_DIM * (j + 1)].astype(BF16) for j in range(N_Q_HEADS // 2)]
            do2s = [dcat_ref[:, POOL_WIDTH + 2 * HEAD_DIM * j:POOL_WIDTH + 2 * HEAD_DIM * (j + 1)].astype(BF16)
                    for j in range(N_Q_HEADS // 2)]
            slot = _head_slot
            for hq in range(N_Q_HEADS):
                j, half, h = hq // 2, hq % 2, hq // GQA_GROUP
                dp_all[hq] = _dot_nt(do2s[j], v_var[h][half])
            sink_acc[...] = jnp.zeros_like(sink_acc)
            for hq in range(N_Q_HEADS):
                for r in range(0, BLOCK, ROW_CHUNK):
                    rows = slice(r, r + ROW_CHUNK)
                    probs = p_all[slot(hq), rows, :].astype(F32)
                    dp = dp_all[hq, rows, :]
                    delta = jnp.sum(probs * dp, axis=-1, keepdims=True)
                    ds = probs * (dp - delta)
                    dbias_ref[hq, rows, :] += ds
                    sink_acc[rows, :] += jnp.where(lane == hq, psink_ref[rows, :], 0.0) * delta
                    ds_all[slot(hq), rows, :] = (ds * ATTN_SCALE).astype(BF16)
            dsink_ref[...] -= jnp.sum(sink_acc[...], axis=0, keepdims=True)
            dq2 = [None] * (N_Q_HEADS // 2)
            for hq in range(N_Q_HEADS):
                j, half, h = hq // 2, hq % 2, hq // GQA_GROUP
                dq = _dot(ds_all[slot(hq)], k_var[h][half])
                dq2[j] = dq if dq2[j] is None else dq2[j] + dq
            low = lax.broadcasted_iota(jnp.int32, (2 * BLOCK, 2 * HEAD_DIM), 1) < HEAD_DIM
            dk_half, dv_half = [[None, None], [None, None]], [[None, None], [None, None]]
            for h in range(N_KV_HEADS):
                for half in range(2):
                    heads = [hq for hq in range(GQA_GROUP * h, GQA_GROUP * (h + 1)) if hq % 2 == half]
                    base = slot(heads[0])
                    q_rows = jnp.concatenate([q2s[hq // 2] for hq in heads], axis=0)
                    do_rows = jnp.concatenate([do2s[hq // 2] for hq in heads], axis=0)
                    dk_half[h][half] = _dot_tn(_merge_rows(ds_all[base:base + 2]), q_rows)
                    dv_half[h][half] = _dot_tn(_merge_rows(p_all[base:base + 2]), do_rows)

            def pair_of(halves):
                return jnp.where(low, halves[0][0] + pltpu.roll(halves[0][1], HEAD_DIM, 1),
                                 halves[1][1] + pltpu.roll(halves[1][0], HEAD_DIM, 1))

            dkv = jnp.concatenate([pair_of(dk_half), pair_of(dv_half)], axis=1)
            dproj_ref[:, POOL_WIDTH:2 * POOL_WIDTH] = c_q[...].astype(BF16)
            dproj_ref[:, 2 * POOL_WIDTH:] = (c_kv[...] + dkv[0:BLOCK]).astype(BF16)
            c_q[...] = jnp.concatenate(dq2, axis=1)
            c_kv[...] = dkv[BLOCK:]

        @pl.when(i == nb)
        def _():
            dbuf[BLOCK:, :] = jnp.zeros((HALO, POOL_WIDTH), F32)
            for g, w in enumerate(POOL_WINDOWS):
                cols = slice(g * POOL_GROUP_DIM, (g + 1) * POOL_GROUP_DIM)
                dproj_ref[:, cols] = (_window_sum(dbuf, g, w, lambda k: k) + c_u[:, cols]).astype(BF16)
            dproj_ref[:, POOL_WIDTH:2 * POOL_WIDTH] = c_q[...].astype(BF16)
            dproj_ref[:, 2 * POOL_WIDTH:] = c_kv[...].astype(BF16)
            bk = bk_ref[...]
            for h in range(N_Q_HEADS):
                db = dbias_ref[h]
                for b in range(N_BUCKETS):
                    drb_ref[h, b] = jnp.sum(jnp.where(bk == float(b), db, 0.0))
            finish()

    cur = lambda i: jnp.minimum(i, nb - 1)
    prv = lambda i: jnp.maximum(jnp.minimum(i, nb - 1) - 1, 0)
    out = pl.pallas_call(
        body, name="mixers_bwd", grid=(nb + 1,),
        out_shape=[jax.ShapeDtypeStruct((t, proj.shape[1]), BF16),
                   jax.ShapeDtypeStruct((N_Q_HEADS, BLOCK, 2 * BLOCK), F32),
                   jax.ShapeDtypeStruct((1, 128), F32),
                   jax.ShapeDtypeStruct((4, POOL_GROUP_DIM, POOL_GROUP_DIM), F32),
                   jax.ShapeDtypeStruct((len(POOL_WINDOWS), POOL_GROUP_DIM), F32),
                   jax.ShapeDtypeStruct((N_Q_HEADS, N_BUCKETS), F32)]
        + [jax.ShapeDtypeStruct(p.shape, p.dtype) for p in ffn_parts],
        in_specs=_mixer_in_specs(cur, prv)[2:]
        + [pl.BlockSpec((BLOCK, 2 * POOL_WIDTH), lambda i: (cur(i), 0)), pl.BlockSpec((BLOCK, POOL_WIDTH), lambda i: (cur(i), 0)),
           pl.BlockSpec((N_Q_HEADS, BLOCK, 2 * BLOCK), lambda i: (0, cur(i), 0)), pl.BlockSpec((BLOCK, 128), lambda i: (cur(i), 0))]
        + _mixer_param_specs() + [pl.BlockSpec((BLOCK, 2 * BLOCK), lambda i: (0, 0))] + [ANY] * na,
        out_specs=[pl.BlockSpec((BLOCK, proj.shape[1]), lambda i: (jnp.maximum(i - 1, 0), 0)),
                   pl.BlockSpec((N_Q_HEADS, BLOCK, 2 * BLOCK), lambda i: (0, 0, 0)),
                   pl.BlockSpec((1, 128), lambda i: (0, 0)),
                   pl.BlockSpec((4, POOL_GROUP_DIM, POOL_GROUP_DIM), lambda i: (0, 0, 0)),
                   pl.BlockSpec((len(POOL_WINDOWS), POOL_GROUP_DIM), lambda i: (0, 0)), SMEM] + [ANY] * na,
        scratch_shapes=[pltpu.VMEM((BLOCK + HALO, POOL_WIDTH), F32),
                        pltpu.VMEM((BLOCK, POOL_WIDTH), F32), pltpu.VMEM((BLOCK, POOL_WIDTH), F32),
                        pltpu.VMEM((BLOCK, 256), F32),
                        pltpu.VMEM((N_Q_HEADS, BLOCK, 2 * BLOCK), F32), pltpu.VMEM((N_Q_HEADS, BLOCK, 2 * BLOCK), BF16),
                        pltpu.VMEM((BLOCK, 128), F32)]
        + _chip_exchange_scratch(ffn_parts),
        compiler_params=_params(collective_id=CHIP_COLLECTIVE_ID),
    )(proj, proj, proj, dcat, pooled, probs, p_sinks, w_pool, pool_scale, bucket, *ffn_parts)
    return out[:6], out[6:]


def inproj_bwd(dproj, w_in_t, x, g, dx1):
    t, d = x.shape
    n = dproj.shape[1]
    tm = TOKEN_TILE

    def body(dp_ref, w_ref, x_ref, g_ref, dx1_ref, dx_ref, dg_ref):
        @pl.when(pl.program_id(0) == 0)
        def _():
            dg_ref[...] = jnp.zeros_like(dg_ref)

        dh = _dot(dp_ref[...], w_ref[...])
        xv = x_ref[...]
        dx, dg_rows = _norm_bwd(dh, xv, _rstd(xv), g_ref[...])
        dx_ref[...] = dx1_ref[...] + dx
        dg_ref[...] += _as_rows(jnp.sum(dg_rows, axis=0, keepdims=True))

    row = pl.BlockSpec((tm, d), lambda i: (i, 0))
    gain = pl.BlockSpec((1, d), lambda i: (0, 0))
    return pl.pallas_call(
        body, name="inproj_bwd", grid=(t // tm,),
        out_shape=[jax.ShapeDtypeStruct((t, d), F32), jax.ShapeDtypeStruct((d // 128, 128), F32)],
        in_specs=[pl.BlockSpec((tm, n), lambda i: (i, 0)), pl.BlockSpec(w_in_t.shape, lambda i: (0, 0)), row, gain, row],
        out_specs=[row, pl.BlockSpec((d // 128, 128), lambda i: (0, 0))],
        compiler_params=_params(),
    )(dproj, w_in_t, x, g, dx1)


def _bucket_band():
    qi = jnp.arange(BLOCK)[:, None]
    kj = jnp.arange(2 * BLOCK)[None, :]
    dist = qi + BLOCK - kj
    n = jnp.maximum(dist, 0)
    nf = jnp.maximum(n, 1).astype(F32)
    large = MAX_EXACT + (jnp.log(nf / MAX_EXACT) / np.float32(np.log(MAX_DISTANCE / MAX_EXACT))
                         * (N_BUCKETS - MAX_EXACT)).astype(jnp.int32)
    large = jnp.minimum(large, N_BUCKETS - 1)
    bucket = jnp.where(n < MAX_EXACT, n, large)
    in_window = (dist >= 0) & (dist < BLOCK)
    return bucket.astype(F32), in_window.astype(F32)


def kernel(x, g_pre_mix, w_in, w_pool, pool_scale, rel_bias, sinks, w_out, g_post_mix, g_pre_ffn, w_gate, w_up, w_down, g_post_ffn, loss_target, m_g_pre_mix, m_w_in, m_w_pool, m_pool_scale, m_rel_bias, m_sinks, m_w_out, m_g_post_mix, m_g_pre_ffn, m_w_gate, m_w_up, m_w_down, m_g_post_ffn, v_g_pre_mix, v_w_in, v_w_pool, v_pool_scale, v_rel_bias, v_sinks, v_w_out, v_g_post_mix, v_g_pre_ffn, v_w_gate, v_w_up, v_w_down, v_g_post_ffn):
    d = x.shape[-1]
    xs, target = x[0], loss_target[0]

    w_in_ts = w_in[0].T.astype(BF16)
    w_out_s = w_out[0].astype(BF16)
    gate_ts = w_gate[0].T.astype(BF16)
    up_ts = w_up[0].T.astype(BF16)
    w_down_s = w_down[0].astype(BF16)

    bucket, in_window = _bucket_band()
    w_pool_b = w_pool[0].astype(BF16)
    half = up_ts.shape[0] // 2
    proj, h1, w_in_t, up_t = norm_inproj(xs, g_pre_mix, w_in_ts, up_ts[:half], up_ts.shape[0])
    w_in_t = w_in_t.reshape(-1, d)
    cat, pooled, probs, p_sinks, gate_t, w_out_f = mixers_fwd(
        proj, bucket, in_window, rel_bias, sinks, w_pool_b, pool_scale, [gate_ts, w_out_s])
    w_out_f = w_out_f.reshape(-1, d)
    mix, x1, h2, up_t = outproj_norm(cat, w_out_f, xs, g_post_mix, g_pre_ffn, up_ts[half:], up_t)
    gate, up, act, w_down_f = ffn_up(h2, gate_t, up_t, w_down_s)
    df, dy, dg_post_ffn, loss_part = ffn_down_loss(act, w_down_f, x1, g_post_ffn, target)

    def pair_sum(parts, tag):
        return pair_add(parts, pair_exchange(parts, "pair_exchange_" + tag), "pair_add_" + tag)

    (dgate, dup), (d_gate, d_up) = ffn_down_bwd(df, w_down_f, gate, up, h2)
    (d_down,), got_gate_up = grad_ffn([act], df, "grad_w_down", [d_gate, d_up])
    q_gate, q_up, q_down = pair_add(
        [d_gate, d_up, d_down], [*got_gate_up, *pair_exchange([d_down], "pair_exchange_down")], "pair_add_ffn")
    (dx1, dmix, dg_pre_ffn, dg_post_mix), (gate_slots, down_slots) = ffn_up_bwd(
        dgate, dup, gate_t, up_t, x1, g_pre_ffn, dy, mix, g_post_mix, [q_gate, q_down])
    dcat, d_out = outproj_bwd(dmix, w_out_f, cat)
    q_out, = pair_sum([d_out], "out")
    (dproj, _, dsinks, dw_pool, dpool_scale, drel_bias), (up_slots, out_slots) = mixers_bwd(
        proj, dcat, pooled, probs, p_sinks, w_pool_b, pool_scale, bucket, [q_up, q_out])
    grad_x, dg_pre_mix = inproj_bwd(dproj, w_in_t, xs, g_pre_mix, dx1)

    small_w = [g_pre_mix, g_post_mix, g_pre_ffn, g_post_ffn, pool_scale, sinks, w_pool, rel_bias.T]
    small_m = [m_g_pre_mix, m_g_post_mix, m_g_pre_ffn, m_g_post_ffn, m_pool_scale, m_sinks, m_w_pool, m_rel_bias.T]
    small_v = [v_g_pre_mix, v_g_post_mix, v_g_pre_ffn, v_g_post_ffn, v_pool_scale, v_sinks, v_w_pool, v_rel_bias.T]
    d_in_t, total, total_rb = grad_w_in_small_reduce(
        dproj, h1, [dg_pre_mix, dg_post_mix, dg_pre_ffn, dg_post_ffn], dpool_scale, dsinks, loss_part, dw_pool, drel_bias)
    g_in_t = reduce_w_in(d_in_t)
    loss_row, sm = small_adamw(total, total_rb, small_w, small_m, small_v)
    sm[7] = [r.T for r in sm[7]]
    big_w = [w_in[0].T, w_out[0], w_gate[0].T, w_up[0].T, w_down[0]]
    big_m = [m_w_in[0].T, m_w_out[0], m_w_gate[0].T, m_w_up[0].T, m_w_down[0]]
    big_v = [v_w_in[0].T, v_w_out[0], v_w_gate[0].T, v_w_up[0].T, v_w_down[0]]
    upd = sum_adamw([out_slots, gate_slots, up_slots, down_slots], big_w[1:], big_m[1:], big_v[1:], "sum_adamw")
    upd = [[g_in_t, *adamw_update(big_w[:1], [g_in_t], big_m[:1], big_v[:1], "adamw_in")[0]], *upd]
    back = lambda k, a: (a.T if k in (0, 2, 3) else a)[None]
    big = [[back(k, u) for u in upd[k]] for k in range(5)]

    def ordered(kind):
        s, b = [p[kind] for p in sm], [p[kind] for p in big]
        return [s[0], b[0], s[6], s[4], s[7], s[5], b[1], s[1], s[2], b[2], b[3], b[4], s[3]]

    return (loss_row[0, 0], grad_x[None], *ordered(0), *ordered(1), *ordered(2), *ordered(3))
```

```python
import numpy as np
import jax
import jax.numpy as jnp
from jax import lax
from jax.experimental import pallas as pl
from jax.experimental.pallas import tpu as pltpu

F32 = jnp.float32
BF16 = jnp.bfloat16

N_DEV = 8
N_CHIP = 4
POOL_WIDTH = 512
POOL_WINDOWS = (2, 4, 8, 16)
POOL_GROUP_DIM = 128
HEAD_DIM = 64
N_Q_HEADS = 8
N_KV_HEADS = 2
GQA_GROUP = 4
BLOCK = 128
HALO = 16
ROW_CHUNK = 32
N_BUCKETS = 32
MAX_EXACT = 16
MAX_DISTANCE = 128
EPS = 1e-6
NEG_INF = -1e30
ATTN_SCALE = float(1.0 / np.sqrt(np.float32(HEAD_DIM)))

ADAM_LR = 0.001
ADAM_B1 = 0.9
ADAM_B2 = 0.999
ADAM_EPS = 1e-08
ADAM_WD = 0.01
ADAM_STEP = 10

TOKEN_TILE = 1024
WIDE_K_TOKEN_TILE = 512
FFN_TOKEN_TILE = 1024
FF_SHARDS_PER_TILE = 4
VMEM_LIMIT = 56 * 1024 * 1024
MESH = pl.DeviceIdType.MESH
PAIR_COLLECTIVE_ID = 0
GATHER_COLLECTIVE_ID = 1
CHIP_COLLECTIVE_ID = 2
ANY = pl.BlockSpec(memory_space=pl.ANY)
VMEM = pl.BlockSpec(memory_space=pltpu.VMEM)
SMEM = pl.BlockSpec(memory_space=pltpu.SMEM)


def _params(**kw):
    return pltpu.CompilerParams(vmem_limit_bytes=VMEM_LIMIT, **kw)


def _dot(a, b):
    return jnp.dot(a, b, preferred_element_type=F32)


def _dot_nt(a, b):
    return lax.dot_general(a, b, (((1,), (1,)), ((), ())), preferred_element_type=F32)


def _dot_tn(a, b):
    return lax.dot_general(a, b, (((0,), (0,)), ((), ())), preferred_element_type=F32)


def _rstd(v):
    return lax.rsqrt(jnp.mean(v * v, axis=-1, keepdims=True) + EPS)


def _norm_bwd(dout, v, r, g):
    vn = v * r
    dn = dout * g
    dv = r * (dn - vn * jnp.mean(dn * vn, axis=-1, keepdims=True))
    return dv, dout * vn


def _as_rows(v):
    return jnp.concatenate([v[:, k:k + 128] for k in range(0, v.shape[1], 128)], axis=0)


def _as_lanes(rows):
    return jnp.concatenate([rows[k:k + 1, :] for k in range(rows.shape[0])], axis=1)


def _handshake(peers):
    barrier = pltpu.get_barrier_semaphore()
    for peer in peers:
        pl.semaphore_signal(barrier, inc=1, device_id=peer, device_id_type=MESH)
    pl.semaphore_wait(barrier, len(peers))


def _merge_rows(value):
    s, r, c_ = value.shape
    return value.reshape(s * r, c_)


def _gather_plan(srcs, outs, send_sems, recv_sems, local_sems=None, bounce=None, rows=None, shake=True):
    n = len(srcs)
    x, y, c = lax.axis_index("x"), lax.axis_index("y"), lax.axis_index("c")
    me, sibling = (x, y, c), (x, y, 1 - c)
    chips = [(1 - x, y), (x, 1 - y), (1 - x, 1 - y)]

    def slot(a, px, py, pc):
        whole = outs[a].at[4 * px + 2 * py + pc]
        return whole if rows is None or rows[a] is None else whole.at[pl.ds(*rows[a])]

    def copy(a, k, block, to, from_src=False):
        return pltpu.make_async_remote_copy(
            src_ref=srcs[a] if from_src else slot(a, *block), dst_ref=slot(a, *block),
            send_sem=send_sems.at[k * n + a], recv_sem=recv_sems.at[k * n + a], device_id=to, device_id_type=MESH)

    def own_in(a):
        return pltpu.make_async_copy(srcs[a], bounce[a], local_sems.at[a])

    def own_out(a):
        return pltpu.make_async_copy(bounce[a], slot(a, *me), local_sems.at[a])

    def first(a):
        return [copy(a, 0, me, sibling, True)] + [copy(a, 1 + j, me, (*chip, c), True) for j, chip in enumerate(chips)]

    def passed(a, j):
        return copy(a, 4 + j, (*chips[j], c), sibling)

    def start():
        if shake:
            _handshake([sibling] + [(*chip, c) for chip in chips])
        for a in range(n):
            if bounce is not None:
                own_in(a).start()
            for cp in first(a):
                cp.start()

    def forward():
        if bounce is not None:
            for a in range(n):
                own_in(a).wait()
                own_out(a).start()
        for j, chip in enumerate(chips):
            for a in range(n):
                copy(a, 1 + j, (*chip, c), me).wait_recv()
                passed(a, j).start()

    def complete():
        for a in range(n):
            copy(a, 0, sibling, me).wait_recv()
            for j, chip in enumerate(chips):
                copy(a, 4 + j, (*chip, 1 - c), me).wait_recv()
        for a in range(n):
            for cp in first(a) + [passed(a, j) for j in range(3)]:
                cp.wait_send()
            if bounce is not None:
                own_out(a).wait()

    def finish():
        forward()
        complete()

    finish.forward, finish.complete = forward, complete
    return start, finish


def _gather_scratch(shards):
    n = len(shards)
    return [pltpu.SemaphoreType.DMA((7 * n,)), pltpu.SemaphoreType.DMA((7 * n,)), pltpu.SemaphoreType.DMA((n,))] \
        + [pltpu.VMEM(s.shape, s.dtype) for s in shards]


def _chip_exchange_plan(srcs, outs, send_sems, recv_sems, local_sems, bounce):
    n = len(srcs)
    x, y, c = lax.axis_index("x"), lax.axis_index("y"), lax.axis_index("c")
    my_chip = 2 * x + y

    def copies():
        out = []
        for a in range(n):
            for k in range(1, N_CHIP):
                px, py = x ^ (k >> 1), y ^ (k & 1)
                out.append(pltpu.make_async_remote_copy(
                    src_ref=srcs[a].at[2 * px + py], dst_ref=outs[a].at[my_chip],
                    send_sem=send_sems.at[(k - 1) * n + a], recv_sem=recv_sems.at[(k - 1) * n + a],
                    device_id=(px, py, c), device_id_type=MESH))
        return out

    def own_in(a):
        return pltpu.make_async_copy(srcs[a].at[my_chip], bounce[a], local_sems.at[a])

    def own_out(a):
        return pltpu.make_async_copy(bounce[a], outs[a].at[my_chip], local_sems.at[a])

    def start():
        _handshake([(x ^ (k >> 1), y ^ (k & 1), c) for k in range(1, N_CHIP)])
        for a in range(n):
            own_in(a).start()
        for cp in copies():
            cp.start()

    def finish():
        for a in range(n):
            own_in(a).wait()
            own_out(a).start()
        for cp in copies():
            cp.wait()
        for a in range(n):
            own_out(a).wait()

    return start, finish


def _chip_exchange_scratch(parts):
    n = len(parts)
    return [pltpu.SemaphoreType.DMA((3 * n,)), pltpu.SemaphoreType.DMA((3 * n,)), pltpu.SemaphoreType.DMA((n,))] \
        + [pltpu.VMEM(p.shape[1:], p.dtype) for p in parts]


def _pair_plan(srcs, outs, send_sems, recv_sems):
    x, y, c = lax.axis_index("x"), lax.axis_index("y"), lax.axis_index("c")

    def copies():
        return [pltpu.make_async_remote_copy(
            src_ref=srcs[a].at[1 - c], dst_ref=outs[a], send_sem=send_sems.at[a], recv_sem=recv_sems.at[a],
            device_id=(x, y, 1 - c), device_id_type=MESH) for a in range(len(srcs))]

    def start():
        _handshake([(x, y, 1 - c)])
        for cp in copies():
            cp.start()

    def finish():
        for cp in copies():
            cp.wait()

    return start, finish


def pair_add_exchange(parts, got, late, name):
    n = len(parts)
    chips = late.shape[1]

    def body(core_ref, *refs):
        late_own, late_all = refs[2 * n:2 * n + 2]
        late_out = refs[3 * n + 2]
        landed, send_sems, recv_sems = refs[3 * n + 3:]
        i = pl.program_id(0)
        start, finish = _pair_plan([late_all], [landed], send_sems, recv_sems)
        pl.when(i == 0)(start)
        for a in range(n):
            refs[2 * n + 2 + a][...] = (refs[a][...].astype(F32) + refs[n + a][...].astype(F32)).astype(BF16)

        @pl.when(i == chips - 1)
        def _():
            finish()
            late_out[...] = (late_own[...].astype(F32) + landed[...].astype(F32)).astype(BF16)

    def own(p):
        zeros = (0,) * (p.ndim - 2)
        return pl.BlockSpec((None, 1, *p.shape[2:]), lambda i, core: (core[0], i, *zeros))

    def plain(p):
        zeros = (0,) * (p.ndim - 1)
        return pl.BlockSpec((1, *p.shape[1:]), lambda i, core: (i, *zeros))

    zeros = (0,) * (late.ndim - 1)
    core = lax.axis_index("c").astype(jnp.int32).reshape(1)
    return pl.pallas_call(
        body, name=name,
        grid_spec=pltpu.PrefetchScalarGridSpec(
            num_scalar_prefetch=1, grid=(chips,),
            in_specs=[own(p) for p in parts] + [plain(p) for p in got]
            + [pl.BlockSpec((None, *late.shape[1:]), lambda i, core: (core[0], *zeros)), ANY],
            out_specs=[plain(p) for p in got] + [pl.BlockSpec(late.shape[1:], lambda i, core: zeros)],
            scratch_shapes=[pltpu.VMEM(late.shape[1:], late.dtype), pltpu.SemaphoreType.DMA((1,)), pltpu.SemaphoreType.DMA((1,))]),
        out_shape=[jax.ShapeDtypeStruct(p.shape, BF16) for p in got] + [jax.ShapeDtypeStruct(late.shape[1:], BF16)],
        compiler_params=_params(collective_id=PAIR_COLLECTIVE_ID),
    )(core, *parts, *got, late, late)


def _adamw(w, g, m, v):
    m2 = ADAM_B1 * m + (1.0 - ADAM_B1) * g
    v2 = ADAM_B2 * v + (1.0 - ADAM_B2) * (g * g)
    m_hat = m2 / (1.0 - ADAM_B1 ** ADAM_STEP)
    v_hat = v2 / (1.0 - ADAM_B2 ** ADAM_STEP)
    delta = -ADAM_LR * (m_hat / (jnp.sqrt(v_hat) + ADAM_EPS) + ADAM_WD * w)
    return delta, m2, v2


def sum_adamw(slots, ws, ms, vs, name):
    n = len(ws)
    halves = 2

    def body(*refs):
        for a in range(n):
            total = refs[a][0].astype(F32)
            for s in range(1, slots[a].shape[0]):
                total = total + refs[a][s].astype(F32)
            delta, m2, v2 = _adamw(refs[n + a][...], total, refs[2 * n + a][...], refs[3 * n + a][...])
            for q, val in enumerate((total, delta, m2, v2)):
                refs[4 * n + 4 * a + q][...] = val

    def rows(w):
        return pl.BlockSpec((w.shape[0] // halves, w.shape[1]), lambda i: (i, 0))

    def slot_rows(p):
        return pl.BlockSpec((p.shape[0], p.shape[1] // halves, p.shape[2]), lambda i: (0, i, 0))

    out = pl.pallas_call(
        body, name=name, grid=(halves,),
        out_shape=[jax.ShapeDtypeStruct(w.shape, F32) for w in ws for _ in range(4)],
        in_specs=[slot_rows(p) for p in slots] + [rows(w) for w in ws] * 3,
        out_specs=[rows(w) for w in ws for _ in range(4)],
        compiler_params=_params(),
    )(*slots, *ws, *ms, *vs)
    return [out[4 * a:4 * a + 4] for a in range(n)]


def adamw_update(ws, gs, ms, vs, name):
    n = len(ws)

    def body(*refs):
        for a in range(n):
            delta, m2, v2 = _adamw(refs[a][...], refs[n + a][...], refs[2 * n + a][...], refs[3 * n + a][...])
            refs[4 * n + 3 * a][...] = delta
            refs[4 * n + 3 * a + 1][...] = m2
            refs[4 * n + 3 * a + 2][...] = v2

    out = pl.pallas_call(
        body, name=name,
        out_shape=[jax.ShapeDtypeStruct(w.shape, F32) for w in ws for _ in range(3)],
        in_specs=[VMEM] * (4 * n), out_specs=[VMEM] * (3 * n),
        compiler_params=_params(),
    )(*ws, *gs, *ms, *vs)
    return [out[3 * a:3 * a + 3] for a in range(n)]


GAIN_ROWS = 8
ROW_POOL_SCALE = 4 * GAIN_ROWS
ROW_SINKS = ROW_POOL_SCALE + 4
ROW_LOSS = ROW_SINKS + 1
ROW_W_POOL = 40
SMALL_ROWS = ROW_W_POOL + 4 * POOL_GROUP_DIM


def grad_w_in_small_reduce(a, b, gains, dpool_scale, dsinks, loss_part, dw_pool, drel_bias):
    t, m = a.shape
    d = b.shape[1]
    r = m // N_DEV
    tt = TOKEN_TILE
    last = t // tt - 1

    def body(a_ref, b_ref, g0, g1, g2, g3, dsc_ref, dsink_ref, loss_ref, dwp_ref, drb_ref, out_ref, total_ref, total_rb_ref,
             acc, stage, gat, gat_rb, g_send, g_recv):
        k = pl.program_id(0)
        x, y, c = lax.axis_index("x"), lax.axis_index("y"), lax.axis_index("c")
        start, finish = _gather_plan([stage, drb_ref], [gat, gat_rb], g_send, g_recv)

        @pl.when(k == 0)
        def _():
            for q, g_ref in enumerate((g0, g1, g2, g3)):
                stage[GAIN_ROWS * q:GAIN_ROWS * (q + 1), :] = g_ref[...]
            stage[ROW_POOL_SCALE:ROW_SINKS, :] = dsc_ref[...]
            stage[ROW_SINKS:ROW_LOSS, :] = dsink_ref[...]
            stage[ROW_LOSS:ROW_LOSS + 1, :] = loss_ref[...]
            stage[ROW_LOSS + 1:ROW_W_POOL, :] = jnp.zeros((ROW_W_POOL - ROW_LOSS - 1, 128), F32)
            stage[ROW_W_POOL:, :] = dwp_ref[...].reshape(4 * POOL_GROUP_DIM, POOL_GROUP_DIM)
            gat[4 * x + 2 * y + c] = stage[...]
            gat_rb[4 * x + 2 * y + c] = drb_ref[...]
            start()
            acc[...] = jnp.zeros_like(acc)

        acc[...] += _dot_tn(a_ref[...], b_ref[...])

        @pl.when(k == last)
        def _():
            blocks = acc[...].reshape(N_CHIP, 2, r, d)
            for chip in range(N_CHIP):
                for core in range(2):
                    out_ref[core, chip] = blocks[chip, core].astype(BF16)
            finish()
            total, total_rb = gat[0], gat_rb[0]
            for s in range(1, N_DEV):
                total, total_rb = total + gat[s], total_rb + gat_rb[s]
            total_ref[...] = total
            total_rb_ref[...] = total_rb

    out_shape = (2, N_CHIP, r, d)
    return pl.pallas_call(
        body, name="grad_w_in", grid=(t // tt,),
        out_shape=[jax.ShapeDtypeStruct(out_shape, BF16), jax.ShapeDtypeStruct((SMALL_ROWS, 128), F32),
                   jax.ShapeDtypeStruct(drel_bias.shape, F32)],
        in_specs=[pl.BlockSpec((tt, m), lambda k: (k, 0)), pl.BlockSpec((tt, d), lambda k: (k, 0))] + [VMEM] * 9,
        out_specs=[pl.BlockSpec(out_shape, lambda k: (0,) * len(out_shape)), VMEM, VMEM],
        scratch_shapes=[pltpu.VMEM((m, d), F32), pltpu.VMEM((SMALL_ROWS, 128), F32),
                        pltpu.VMEM((N_DEV, SMALL_ROWS, 128), F32), pltpu.VMEM((N_DEV, *drel_bias.shape), F32),
                        pltpu.SemaphoreType.DMA((14,)), pltpu.SemaphoreType.DMA((14,))],
        compiler_params=_params(collective_id=GATHER_COLLECTIVE_ID),
    )(a, b, *gains, dpool_scale, dsinks, loss_part, dw_pool, drel_bias)


def reduce_w_in(d_in_t):
    def body(d_in_ref, g_in_ref, pair_got, chip_part, chip_got, p_send, p_recv, x_send, x_recv):
        x, y, c = lax.axis_index("x"), lax.axis_index("y"), lax.axis_index("c")
        my_chip = 2 * x + y
        _handshake([(x, y, 1 - c)] + [(x ^ (k >> 1), y ^ (k & 1), c) for k in range(1, N_CHIP)])
        pair = pltpu.make_async_remote_copy(
            src_ref=d_in_ref.at[1 - c], dst_ref=pair_got, send_sem=p_send, recv_sem=p_recv,
            device_id=(x, y, 1 - c), device_id_type=MESH)
        pair.start()
        pair.wait()
        chip_part[...] = (d_in_ref[c].astype(F32) + pair_got[...].astype(F32)).astype(BF16)
        copies = []
        for k in range(1, N_CHIP):
            px, py = x ^ (k >> 1), y ^ (k & 1)
            copies.append(pltpu.make_async_remote_copy(
                src_ref=chip_part.at[2 * px + py], dst_ref=chip_got.at[my_chip],
                send_sem=x_send.at[k - 1], recv_sem=x_recv.at[k - 1], device_id=(px, py, c), device_id_type=MESH))
        for cp in copies:
            cp.start()
        chip_got[my_chip] = chip_part[my_chip]
        for cp in copies:
            cp.wait()
        g_in = chip_got[0].astype(F32)
        for s in range(1, N_CHIP):
            g_in = g_in + chip_got[s].astype(F32)
        g_in_ref[...] = g_in

    per_core = d_in_t.shape[1:]
    return pl.pallas_call(
        body, name="reduce_w_in",
        out_shape=jax.ShapeDtypeStruct(d_in_t.shape[2:], F32),
        in_specs=[VMEM], out_specs=VMEM,
        scratch_shapes=[pltpu.VMEM(per_core, d_in_t.dtype), pltpu.VMEM(per_core, d_in_t.dtype),
                        pltpu.VMEM(per_core, d_in_t.dtype),
                        pltpu.SemaphoreType.DMA, pltpu.SemaphoreType.DMA,
                        pltpu.SemaphoreType.DMA((3,)), pltpu.SemaphoreType.DMA((3,))],
        compiler_params=_params(collective_id=GATHER_COLLECTIVE_ID),
    )(d_in_t)


def small_adamw(total, total_rb, small_w, small_m, small_v):
    n_small = len(small_w)

    def body(*refs):
        total_ref, rb_ref = refs[:2]
        w_refs, m_refs, v_refs = (refs[2 + k * n_small:2 + (k + 1) * n_small] for k in range(3))
        loss_out = refs[2 + 3 * n_small]
        result = refs[3 + 3 * n_small:]
        total = total_ref[...]
        loss_out[...] = total[ROW_LOSS:ROW_LOSS + 1, :]
        grads = [_as_lanes(total[GAIN_ROWS * k:GAIN_ROWS * (k + 1), :]) for k in range(4)]
        grads.append(_as_lanes(total[ROW_POOL_SCALE:ROW_SINKS, :]))
        grads.append(total[ROW_SINKS:ROW_LOSS, 0:N_Q_HEADS])
        grads.append(total[ROW_W_POOL:, :].reshape(w_refs[6].shape))
        grads.append(rb_ref[...])
        for k in range(n_small):
            delta, m2, v2 = _adamw(w_refs[k][...], grads[k], m_refs[k][...], v_refs[k][...])
            result[4 * k][...] = grads[k]
            result[4 * k + 1][...] = delta
            result[4 * k + 2][...] = m2
            result[4 * k + 3][...] = v2

    out = pl.pallas_call(
        body, name="small_adamw",
        out_shape=[jax.ShapeDtypeStruct((1, 128), F32)] + [jax.ShapeDtypeStruct(w.shape, F32) for w in small_w for _ in range(4)],
        in_specs=[VMEM] * (2 + 3 * n_small), out_specs=[VMEM] * (1 + 4 * n_small),
        compiler_params=_params(),
    )(total, total_rb, *small_w, *small_m, *small_v)
    return out[0], [out[1 + 4 * k:5 + 4 * k] for k in range(n_small)]


def norm_inproj(x, g, w_shard, shard, shard_rows):
    t, d = x.shape
    r = w_shard.shape[0]
    tm = TOKEN_TILE
    nt = t // tm

    def body(x_ref, g_ref, w_shard_ref, shard_ref, proj_ref, h_ref, w_ref, gathered_ref, h_all, w_all, w_sem,
             send_w, recv_w, local_w, bounce_w, send_sems, recv_sems, local_sems, bounce):
        i = pl.program_id(0)
        start_w, finish_w = _gather_plan([w_shard_ref], [w_ref], send_w, recv_w, local_w, [bounce_w])
        start, finish = _gather_plan([shard_ref], [gathered_ref], send_sems, recv_sems, local_sems, [bounce],
                                     [(0, shard.shape[0])], shake=False)

        @pl.when(i == 0)
        def _():
            start_w()
            start()

        @pl.when(i < nt)
        def _():
            xv = x_ref[...]
            h = ((xv * _rstd(xv)) * g_ref[...]).astype(BF16)
            h_ref[...] = h
            h_all[pl.ds(pl.multiple_of(i * tm, tm), tm), :] = h

        @pl.when(i == nt - 1)
        def _():
            finish_w()
            landed = pltpu.make_async_copy(w_ref, w_all, w_sem)
            landed.start()
            landed.wait()

        @pl.when(i >= nt)
        def _():
            rows = pl.ds(pl.multiple_of((i - nt) * tm, tm), tm)
            proj_ref[...] = _dot_nt(h_all[rows, :], _merge_rows(w_all[...]))

        pl.when(i == 2 * nt - 2)(finish.forward)
        pl.when(i == 2 * nt - 1)(finish.complete)

    first = lambda i: (jnp.minimum(i, nt - 1), 0)
    return pl.pallas_call(
        body, name="norm_inproj", grid=(2 * nt,),
        out_shape=[jax.ShapeDtypeStruct((t, N_DEV * r), F32), jax.ShapeDtypeStruct((t, d), BF16),
                   jax.ShapeDtypeStruct((N_DEV, r, d), w_shard.dtype),
                   jax.ShapeDtypeStruct((N_DEV, shard_rows, d), shard.dtype)],
        in_specs=[pl.BlockSpec((tm, d), first), pl.BlockSpec((1, d), lambda i: (0, 0)), ANY, ANY],
        out_specs=[pl.BlockSpec((tm, N_DEV * r), lambda i: (jnp.maximum(i - nt, 0), 0)), pl.BlockSpec((tm, d), first),
                   ANY, ANY],
        scratch_shapes=[pltpu.VMEM((t, d), BF16), pltpu.VMEM((N_DEV, r, d), w_shard.dtype), pltpu.SemaphoreType.DMA]
        + _gather_scratch([w_shard]) + _gather_scratch([shard]),
        compiler_params=_params(collective_id=GATHER_COLLECTIVE_ID),
    )(x, g, w_shard, shard)


def _fill_bias_band(bk_ref, win_ref, rb_ref, biasm_ref):
    bk = bk_ref[...]
    keep = win_ref[...] > 0.5
    for h in range(N_Q_HEADS):
        acc = jnp.zeros(bk.shape, F32)
        for b in range(N_BUCKETS):
            acc = jnp.where(bk == float(b), rb_ref[b, h], acc)
        biasm_ref[h] = jnp.where(keep, acc, NEG_INF)


def _window_sum(buf_ref, g, w, first):
    cols = slice(g * POOL_GROUP_DIM, (g + 1) * POOL_GROUP_DIM)
    acc = None
    for k in range(w):
        piece = buf_ref[first(k):first(k) + BLOCK, cols]
        acc = piece if acc is None else acc + piece
    return acc


def _inv_count(i, w):
    row = lax.broadcasted_iota(jnp.int32, (BLOCK, 1), 0)
    return 1.0 / jnp.minimum(i * BLOCK + row + 1, w).astype(F32)


def _fill_pool_input(i, ubuf, uc_ref, halo_ref):
    ubuf[0:HALO, :] = jnp.where(i > 0, halo_ref[...], 0.0)
    ubuf[HALO:, :] = uc_ref[...]


def _pooled(i, g, w, ubuf):
    cols = slice(g * POOL_GROUP_DIM, (g + 1) * POOL_GROUP_DIM)
    return _window_sum(ubuf, g, w, lambda k: HALO - k) * _inv_count(i, w) - ubuf[HALO:, cols]


def _head_variants(pair):
    low = lax.broadcasted_iota(jnp.int32, pair.shape, 1) < HEAD_DIM
    swapped = pltpu.roll(pair, HEAD_DIM, 1)
    zero = jnp.zeros_like(pair)
    pick = lambda c, a, b: jnp.where(c, a, b).astype(BF16)
    return [[pick(low, pair, zero), pick(low, zero, swapped)], [pick(low, swapped, zero), pick(low, zero, pair)]]


def _head_probs(i, hq, rows, s_ref, biasm_ref, sinks_ref):
    s = s_ref[hq, rows, :] * ATTN_SCALE + biasm_ref[hq, rows, :]
    col = lax.broadcasted_iota(jnp.int32, s.shape, 1)
    s = jnp.where((i == 0) & (col < BLOCK), NEG_INF, s)
    sink = sinks_ref[0, hq]
    m = jnp.maximum(jnp.max(s, axis=-1, keepdims=True), sink)
    p = jnp.exp(s - m)
    e_sink = jnp.exp(sink - m)
    inv = 1.0 / (jnp.sum(p, axis=-1, keepdims=True) + e_sink)
    return p * inv, e_sink * inv


def _head_slot(hq):
    return 4 * (hq // GQA_GROUP) + 2 * (hq % 2) + (hq % GQA_GROUP) // 2


def _mixer_in_specs(cur, prv):
    return [pl.BlockSpec((BLOCK, 512), lambda i: (cur(i), 0)),
            pl.BlockSpec((HALO, 512), lambda i: (jnp.maximum(cur(i) * (BLOCK // HALO) - 1, 0), 0)),
            pl.BlockSpec((BLOCK, 512), lambda i: (cur(i), 1)),
            pl.BlockSpec((BLOCK, 256), lambda i: (cur(i), 4)),
            pl.BlockSpec((BLOCK, 256), lambda i: (prv(i), 4))]


def _mixer_param_specs():
    return [pl.BlockSpec((4, POOL_GROUP_DIM, POOL_GROUP_DIM), lambda i: (0, 0, 0)),
            pl.BlockSpec((1, POOL_WIDTH), lambda i: (0, 0))]


def mixers_fwd(proj, bucket, in_window, rel_bias, sinks, w_pool, pool_scale, shards):
    t = proj.shape[0]
    nb = t // BLOCK
    ns = len(shards)

    def body(*refs):
        uc_ref, halo_ref, q_ref, kvc_ref, kvp_ref, bk_ref, win_ref, rb_ref, sinks_ref, wp_ref, sc_ref = refs[:11]
        shard_refs = refs[11:11 + ns]
        out_ref, pooled_ref, p_all, psink_ref = refs[11 + ns:15 + ns]
        gathered_refs = refs[15 + ns:15 + 2 * ns]
        ubuf, s_all, biasm_ref, send_sems, recv_sems, local_sems = refs[15 + 2 * ns:21 + 2 * ns]
        i = pl.program_id(0)
        start, finish = _gather_plan(shard_refs, gathered_refs, send_sems, recv_sems, local_sems, refs[21 + 2 * ns:])

        @pl.when(i == 0)
        def _():
            start()
            _fill_bias_band(bk_ref, win_ref, rb_ref, biasm_ref)

        _fill_pool_input(i, ubuf, uc_ref, halo_ref)
        for g, w in enumerate(POOL_WINDOWS):
            cols = slice(g * POOL_GROUP_DIM, (g + 1) * POOL_GROUP_DIM)
            pooled = _pooled(i, g, w, ubuf).astype(BF16)
            pooled_ref[:, cols] = pooled
            out_ref[:, cols] = (_dot(pooled, wp_ref[g]) * sc_ref[:, cols]).astype(BF16)
        kv = jnp.concatenate([kvp_ref[...], kvc_ref[...]], axis=0)
        k_var = _head_variants(kv[:, 0:2 * HEAD_DIM])
        v_var = _head_variants(kv[:, 2 * HEAD_DIM:])
        for hq in range(N_Q_HEADS):
            j, half, h = hq // 2, hq % 2, hq // GQA_GROUP
            q2 = q_ref[:, 2 * HEAD_DIM * j:2 * HEAD_DIM * (j + 1)].astype(BF16)
            s_all[hq] = _dot_nt(q2, k_var[h][half])
        psink_ref[...] = jnp.zeros_like(psink_ref)
        for hq in range(N_Q_HEADS):
            for r in range(0, BLOCK, ROW_CHUNK):
                rows = slice(r, r + ROW_CHUNK)
                probs, p_sink = _head_probs(i, hq, rows, s_all, biasm_ref, sinks_ref)
                p_all[_head_slot(hq), rows, :] = probs.astype(BF16)
                psink_ref[rows, hq:hq + 1] = p_sink
        for j in range(N_Q_HEADS // 2):
            h = 2 * j // GQA_GROUP
            acc = _dot(p_all[_head_slot(2 * j)], v_var[h][0]) + _dot(p_all[_head_slot(2 * j + 1)], v_var[h][1])
            out_ref[:, POOL_WIDTH + 2 * HEAD_DIM * j:POOL_WIDTH + 2 * HEAD_DIM * (j + 1)] = acc.astype(BF16)

        pl.when(i == max(nb - 4, 0))(finish.forward)
        pl.when(i == nb - 1)(finish.complete)

    return pl.pallas_call(
        body, name="mixers_fwd", grid=(nb,),
        out_shape=[jax.ShapeDtypeStruct((t, 2 * POOL_WIDTH), BF16), jax.ShapeDtypeStruct((t, POOL_WIDTH), BF16),
                   jax.ShapeDtypeStruct((N_Q_HEADS, t, 2 * BLOCK), BF16), jax.ShapeDtypeStruct((t, 128), F32)]
        + [jax.ShapeDtypeStruct((N_DEV, *sh.shape), sh.dtype) for sh in shards],
        in_specs=_mixer_in_specs(lambda i: i, lambda i: jnp.maximum(i - 1, 0))
        + [pl.BlockSpec((BLOCK, 2 * BLOCK), lambda i: (0, 0))] * 2 + [SMEM, SMEM] + _mixer_param_specs() + [ANY] * ns,
        out_specs=[pl.BlockSpec((BLOCK, 2 * POOL_WIDTH), lambda i: (i, 0)), pl.BlockSpec((BLOCK, POOL_WIDTH), lambda i: (i, 0)),
                   pl.BlockSpec((N_Q_HEADS, BLOCK, 2 * BLOCK), lambda i: (0, i, 0)), pl.BlockSpec((BLOCK, 128), lambda i: (i, 0))]
        + [ANY] * ns,
        scratch_shapes=[pltpu.VMEM((HALO + BLOCK, POOL_WIDTH), F32), pltpu.VMEM((N_Q_HEADS, BLOCK, 2 * BLOCK), F32),
                        pltpu.VMEM((N_Q_HEADS, BLOCK, 2 * BLOCK), F32)]
        + _gather_scratch(shards),
        compiler_params=_params(collective_id=GATHER_COLLECTIVE_ID),
    )(proj, proj, proj, proj, proj, bucket, in_window, rel_bias, sinks, w_pool, pool_scale, *shards)


def outproj_norm(cat, w, x, g, g_next, shard, partial):
    t, d = x.shape
    tm = TOKEN_TILE
    last = t // tm - 1
    rows = [(partial.shape[1] - shard.shape[0], shard.shape[0])]

    def body(c_ref, w_ref, x_ref, g_ref, gn_ref, shard_ref, partial_ref, mix_ref, x1_ref, h2_ref, gathered_ref,
             send_sems, recv_sems, local_sems, bounce):
        i = pl.program_id(0)
        start, finish = _gather_plan([shard_ref], [gathered_ref], send_sems, recv_sems, local_sems, [bounce], rows)
        pl.when(i == 0)(start)
        mix = _dot(c_ref[...], w_ref[...])
        mix_ref[...] = mix
        x1 = x_ref[...] + (mix * _rstd(mix)) * g_ref[...]
        x1_ref[...] = x1
        h2_ref[...] = ((x1 * _rstd(x1)) * gn_ref[...]).astype(BF16)
        pl.when(i == max(last - 1, 0))(finish.forward)
        pl.when(i == last)(finish.complete)

    row = pl.BlockSpec((tm, d), lambda i: (i, 0))
    gain = pl.BlockSpec((1, d), lambda i: (0, 0))
    return pl.pallas_call(
        body, name="outproj_norm", grid=(t // tm,),
        out_shape=[jax.ShapeDtypeStruct((t, d), F32), jax.ShapeDtypeStruct((t, d), F32), jax.ShapeDtypeStruct((t, d), BF16),
                   jax.ShapeDtypeStruct(partial.shape, partial.dtype)],
        in_specs=[pl.BlockSpec((tm, cat.shape[1]), lambda i: (i, 0)), pl.BlockSpec(w.shape, lambda i: (0, 0)), row, gain, gain,
                  ANY, ANY],
        out_specs=[row, row, row, ANY],
        input_output_aliases={6: 3},
        scratch_shapes=_gather_scratch([shard]),
        compiler_params=_params(collective_id=GATHER_COLLECTIVE_ID),
    )(cat, w, x, g, g_next, shard, partial)


def ffn_up(h, gate_t, up_t, down_shard):
    t, d = h.shape
    n = gate_t.shape[1]
    f = N_DEV * n
    tm, ts = FFN_TOKEN_TILE, FF_SHARDS_PER_TILE
    tn = ts * n
    steps = (f // tn, t // tm)

    def body(h_ref, wg_ref, wu_ref, shard_ref, gate_ref, up_ref, a_ref, gathered_ref,
             send_sems, recv_sems, local_sems, bounce):
        j, i = pl.program_id(0), pl.program_id(1)
        start, finish = _gather_plan([shard_ref], [gathered_ref], send_sems, recv_sems, local_sems, [bounce])
        pl.when((i == 0) & (j == 0))(start)

        hv = h_ref[...]
        gate = _dot_nt(hv, _merge_rows(wg_ref[...]))
        up = _dot_nt(hv, _merge_rows(wu_ref[...]))
        gate_ref[...] = gate.astype(BF16)
        up_ref[...] = up.astype(BF16)
        a_ref[...] = (gate * (1.0 / (1.0 + jnp.exp(-gate))) * up).astype(BF16)

        pl.when((j == steps[0] - 1) & (i == max(steps[1] - 2, 0)))(finish.forward)
        pl.when((j == steps[0] - 1) & (i == steps[1] - 1))(finish.complete)

    wide = pl.BlockSpec((tm, tn), lambda j, i: (i, j))
    return pl.pallas_call(
        body, name="ffn_up", grid=steps,
        out_shape=[jax.ShapeDtypeStruct((t, f), BF16)] * 3
        + [jax.ShapeDtypeStruct((N_DEV, *down_shard.shape), down_shard.dtype)],
        in_specs=[pl.BlockSpec((tm, d), lambda j, i: (i, 0)),
                  pl.BlockSpec((ts, n, d), lambda j, i: (j, 0, 0)),
                  pl.BlockSpec((ts, n, d), lambda j, i: (j, 0, 0)), ANY],
        out_specs=[wide, wide, wide, ANY],
        scratch_shapes=_gather_scratch([down_shard]),
        compiler_params=_params(collective_id=GATHER_COLLECTIVE_ID),
    )(h, gate_t, up_t, down_shard)


def ffn_down_loss(a, w_down, x1, g, target):
    t, d = x1.shape
    tm = WIDE_K_TOKEN_TILE

    def body(a_ref, w_ref, x_ref, g_ref, t_ref, df_ref, dy_ref, dg_ref, loss_ref):
        @pl.when(pl.program_id(0) == 0)
        def _():
            dg_ref[...] = jnp.zeros_like(dg_ref)
            loss_ref[...] = jnp.zeros_like(loss_ref)

        f = _dot(a_ref[...], _merge_rows(w_ref[...]))
        r = _rstd(f)
        g = g_ref[...]
        err = x_ref[...] + (f * r) * g - t_ref[...]
        loss_ref[...] += 0.5 * jnp.sum(jnp.mean(err * err, axis=-1, keepdims=True))
        dy = err * (1.0 / d)
        dy_ref[...] = dy
        df, dg_rows = _norm_bwd(dy, f, r, g)
        df_ref[...] = df.astype(BF16)
        dg_ref[...] += _as_rows(jnp.sum(dg_rows, axis=0, keepdims=True))

    row = pl.BlockSpec((tm, d), lambda i: (i, 0))
    gain = pl.BlockSpec((1, d), lambda i: (0, 0))
    return pl.pallas_call(
        body, name="ffn_down_loss", grid=(t // tm,),
        out_shape=[jax.ShapeDtypeStruct((t, d), BF16), jax.ShapeDtypeStruct((t, d), F32),
                   jax.ShapeDtypeStruct((d // 128, 128), F32), jax.ShapeDtypeStruct((1, 128), F32)],
        in_specs=[pl.BlockSpec((tm, a.shape[1]), lambda i: (i, 0)), pl.BlockSpec(w_down.shape, lambda i: (0, 0, 0)), row, gain, row],
        out_specs=[row, row, pl.BlockSpec((d // 128, 128), lambda i: (0, 0)), pl.BlockSpec((1, 128), lambda i: (0, 0))],
        compiler_params=_params(),
    )(a, w_down, x1, g, target)


def ffn_down_bwd(df, w_down, gate, up, h):
    t, d = df.shape
    n = w_down.shape[1]
    f = gate.shape[1]
    tm, ts = WIDE_K_TOKEN_TILE, FF_SHARDS_PER_TILE
    tn = ts * n
    last = t // tm - 1

    def body(df_ref, w_ref, gate_ref, up_ref, h_ref, dgate_ref, dup_ref, dwg_ref, dwu_ref, acc):
        i = pl.program_id(1)

        @pl.when(i == 0)
        def _():
            acc[...] = jnp.zeros_like(acc)

        da = _dot_nt(df_ref[...], _merge_rows(w_ref[...]))
        gate = gate_ref[...].astype(F32)
        sig = 1.0 / (1.0 + jnp.exp(-gate))
        dgate = (da * up_ref[...].astype(F32) * (sig * (1.0 + gate * (1.0 - sig)))).astype(BF16)
        dup = (da * (gate * sig)).astype(BF16)
        dgate_ref[...] = dgate
        dup_ref[...] = dup
        acc[0] += _dot_tn(dgate, h_ref[...])
        acc[1] += _dot_tn(dup, h_ref[...])

        @pl.when(i == last)
        def _():
            for w, out_ref in enumerate((dwg_ref, dwu_ref)):
                blocks = acc[w].reshape(ts // 2, 2, n, d)
                for chip in range(ts // 2):
                    for core in range(2):
                        out_ref[core, chip] = blocks[chip, core].astype(BF16)

    wide = pl.BlockSpec((tm, tn), lambda j, i: (i, j))
    rows = pl.BlockSpec((tm, d), lambda j, i: (i, 0))
    owned = pl.BlockSpec((2, ts // 2, n, d), lambda j, i: (0, j, 0, 0))
    out = pl.pallas_call(
        body, name="ffn_down_bwd", grid=(f // tn, t // tm),
        out_shape=[jax.ShapeDtypeStruct((t, f), BF16)] * 2 + [jax.ShapeDtypeStruct((2, N_CHIP, n, d), BF16)] * 2,
        in_specs=[rows, pl.BlockSpec((ts, n, d), lambda j, i: (j, 0, 0)), wide, wide, rows],
        out_specs=[wide, wide, owned, owned],
        scratch_shapes=[pltpu.VMEM((2, tn, d), F32)],
        compiler_params=_params(),
    )(df, w_down, gate, up, h)
    return out[:2], out[2:]


def grad_ffn(lhs, b, name, pair_parts=()):
    t, f = lhs[0].shape
    d = b.shape[1]
    nw = len(lhs)
    na = len(pair_parts)
    n = f // N_DEV
    tt, ts = TOKEN_TILE, FF_SHARDS_PER_TILE
    tn = ts * n
    steps = (f // tn, t // tt)

    def body(*refs):
        a_refs, b_ref, part_refs = refs[:nw], refs[nw], refs[nw + 1:nw + 1 + na]
        out_refs = refs[nw + 1 + na:2 * nw + 1 + na]
        got_refs = refs[2 * nw + 1 + na:2 * nw + 1 + 2 * na]
        acc = refs[2 * nw + 1 + 2 * na]
        i, k = pl.program_id(0), pl.program_id(1)
        if na:
            start, finish = _pair_plan(part_refs, got_refs, *refs[2 * nw + 2 + 2 * na:])
            pl.when((i == 0) & (k == 0))(start)

        @pl.when(k == 0)
        def _():
            acc[...] = jnp.zeros_like(acc)

        for w in range(nw):
            acc[w] += _dot_tn(a_refs[w][...], b_ref[...])

        @pl.when(k == steps[1] - 1)
        def _():
            for w in range(nw):
                blocks = acc[w].reshape(ts // 2, 2, n, d)
                for chip in range(ts // 2):
                    for core in range(2):
                        out_refs[w][core, chip] = blocks[chip, core].astype(BF16)

        if na:
            pl.when((i == steps[0] - 1) & (k == steps[1] - 1))(finish)

    out = pl.pallas_call(
        body, name=name, grid=steps,
        out_shape=[jax.ShapeDtypeStruct((2, N_CHIP, n, d), BF16)] * nw
        + [jax.ShapeDtypeStruct(p.shape[1:], p.dtype) for p in pair_parts],
        in_specs=[pl.BlockSpec((tt, tn), lambda i, k: (k, i))] * nw + [pl.BlockSpec((tt, d), lambda i, k: (k, 0))] + [ANY] * na,
        out_specs=[pl.BlockSpec((2, ts // 2, n, d), lambda i, k: (0, i, 0, 0))] * nw + [ANY] * na,
        scratch_shapes=[pltpu.VMEM((nw, tn, d), F32)]
        + ([pltpu.SemaphoreType.DMA((na,)), pltpu.SemaphoreType.DMA((na,))] if na else []),
        compiler_params=_params(collective_id=PAIR_COLLECTIVE_ID) if na else _params(),
    )(*lhs, b, *pair_parts)
    return out[:nw], out[nw:]


def ffn_up_bwd(dgate, dup, gate_t, up_t, x1, g_ffn, dy, mix, g_mix, chip_parts):
    t, d = x1.shape
    n = gate_t.shape[1]
    f = N_DEV * n
    tm = WIDE_K_TOKEN_TILE
    na = len(chip_parts)
    last = t // tm - 1

    def body(*refs):
        dg_ref, du_ref, wg_ref, wu_ref, x_ref, gf_ref, dy_ref, mix_ref, gm_ref = refs[:9]
        part_refs = refs[9:9 + na]
        dx1_ref, dmix_ref, dgf_ref, dgm_ref = refs[9 + na:13 + na]
        slot_refs = refs[13 + na:13 + 2 * na]
        send_sems, recv_sems, local_sems = refs[13 + 2 * na:16 + 2 * na]
        i = pl.program_id(0)
        start, finish = _chip_exchange_plan(part_refs, slot_refs, send_sems, recv_sems, local_sems, refs[16 + 2 * na:])

        @pl.when(i == 0)
        def _():
            start()
            dgf_ref[...] = jnp.zeros_like(dgf_ref)
            dgm_ref[...] = jnp.zeros_like(dgm_ref)

        dh = _dot(dg_ref[...], _merge_rows(wg_ref[...])) + _dot(du_ref[...], _merge_rows(wu_ref[...]))
        x1 = x_ref[...]
        dx, dgf_rows = _norm_bwd(dh, x1, _rstd(x1), gf_ref[...])
        dx1 = dy_ref[...] + dx
        dx1_ref[...] = dx1
        dgf_ref[...] += _as_rows(jnp.sum(dgf_rows, axis=0, keepdims=True))
        mix = mix_ref[...]
        dmix, dgm_rows = _norm_bwd(dx1, mix, _rstd(mix), gm_ref[...])
        dmix_ref[...] = dmix.astype(BF16)
        dgm_ref[...] += _as_rows(jnp.sum(dgm_rows, axis=0, keepdims=True))
        pl.when(i == last)(finish)

    row = pl.BlockSpec((tm, d), lambda i: (i, 0))
    wide = pl.BlockSpec((tm, f), lambda i: (i, 0))
    gain = pl.BlockSpec((1, d), lambda i: (0, 0))
    gain_rows = pl.BlockSpec((d // 128, 128), lambda i: (0, 0))
    whole = pl.BlockSpec((N_DEV, n, d), lambda i: (0, 0, 0), pipeline_mode=pl.Buffered(1))
    out = pl.pallas_call(
        body, name="ffn_up_bwd", grid=(t // tm,),
        out_shape=[jax.ShapeDtypeStruct((t, d), F32), jax.ShapeDtypeStruct((t, d), BF16),
                   jax.ShapeDtypeStruct((d // 128, 128), F32), jax.ShapeDtypeStruct((d // 128, 128), F32)]
        + [jax.ShapeDtypeStruct(p.shape, p.dtype) for p in chip_parts],
        in_specs=[wide, wide, whole, whole, row, gain, row, row, gain] + [ANY] * na,
        out_specs=[row, row, gain_rows, gain_rows] + [ANY] * na,
        scratch_shapes=_chip_exchange_scratch(chip_parts),
        compiler_params=_params(collective_id=CHIP_COLLECTIVE_ID),
    )(dgate, dup, gate_t, up_t, x1, g_ffn, dy, mix, g_mix, *chip_parts)
    return out[:4], out[4:]


def outproj_bwd(dmix, w_out, cat):
    t, d = dmix.shape
    m = w_out.shape[0]
    r = m // N_DEV
    tm = TOKEN_TILE
    last = t // tm - 1

    def body(dm_ref, w_ref, cat_ref, dcat_ref, dw_ref, acc):
        k = pl.program_id(0)

        @pl.when(k == 0)
        def _():
            acc[...] = jnp.zeros_like(acc)

        dcat_ref[...] = _dot_nt(dm_ref[...], w_ref[...])
        acc[...] += _dot_tn(cat_ref[...], dm_ref[...])

        @pl.when(k == last)
        def _():
            blocks = acc[...].reshape(N_CHIP, 2, r, d)
            for chip in range(N_CHIP):
                for core in range(2):
                    dw_ref[core, chip] = blocks[chip, core].astype(BF16)

    tile = lambda width: pl.BlockSpec((tm, width), lambda k: (k, 0))
    return pl.pallas_call(
        body, name="outproj_bwd", grid=(t // tm,),
        out_shape=[jax.ShapeDtypeStruct((t, m), F32), jax.ShapeDtypeStruct((2, N_CHIP, r, d), BF16)],
        in_specs=[tile(d), pl.BlockSpec(w_out.shape, lambda k: (0, 0)), tile(m)],
        out_specs=[tile(m), pl.BlockSpec((2, N_CHIP, r, d), lambda k: (0, 0, 0, 0))],
        scratch_shapes=[pltpu.VMEM((m, d), F32)],
        compiler_params=_params(),
    )(dmix, w_out, cat)


def mixers_bwd(proj, dcat, pooled, probs, p_sinks, w_pool, pool_scale, bucket, ffn_parts):
    t = proj.shape[0]
    nb = t // BLOCK
    na = len(ffn_parts)

    def body(*refs):
        (q_ref, kvc_ref, kvp_ref, dcat_ref, pooled_ref, p_all, psink_ref, wp_ref, sc_ref, bk_ref) = refs[:10]
        part_refs = refs[10:10 + na]
        dproj_ref, dbias_ref, dsink_ref, dwp_ref, dsc_ref, drb_ref = refs[10 + na:16 + na]
        slot_refs = refs[16 + na:16 + 2 * na]
        dbuf, c_u, c_q, c_kv, dp_all, ds_all, sink_acc = refs[16 + 2 * na:23 + 2 * na]
        send_sems, recv_sems, local_sems = refs[23 + 2 * na:26 + 2 * na]
        bounce = refs[26 + 2 * na:]
        i = pl.program_id(0)
        lane = lax.broadcasted_iota(jnp.int32, (1, 128), 1)
        start, finish = _chip_exchange_plan(part_refs, slot_refs, send_sems, recv_sems, local_sems, bounce)

        @pl.when(i == 0)
        def _():
            start()
            dbias_ref[...] = jnp.zeros_like(dbias_ref)
            dwp_ref[...] = jnp.zeros_like(dwp_ref)
            dsc_ref[...] = jnp.zeros_like(dsc_ref)
            dsink_ref[...] = jnp.zeros_like(dsink_ref)
            dbuf[...] = jnp.zeros_like(dbuf)
            c_u[...] = jnp.zeros_like(c_u)
            c_q[...] = jnp.zeros_like(c_q)
            c_kv[...] = jnp.zeros_like(c_kv)

        @pl.when(i < nb)
        def _():
            for g, w in enumerate(POOL_WINDOWS):
                cols = slice(g * POOL_GROUP_DIM, (g + 1) * POOL_GROUP_DIM)
                pooled = pooled_ref[:, cols]
                mixed = _dot(pooled, wp_ref[g])
                dout = dcat_ref[:, cols]
                dsc_ref[g:g + 1, :] += jnp.sum(dout * mixed, axis=0, keepdims=True)
                dmixed = (dout * sc_ref[:, cols]).astype(BF16)
                dwp_ref[g] += _dot_tn(pooled, dmixed)
                dpooled = _dot_nt(dmixed, wp_ref[g])
                scaled = dpooled * _inv_count(i, w)
                dbuf[BLOCK:, cols] = scaled[0:HALO]
                dproj_ref[:, cols] = (_window_sum(dbuf, g, w, lambda k: k) + c_u[:, cols]).astype(BF16)
                dbuf[0:BLOCK, cols] = scaled
                c_u[:, cols] = -dpooled

            kv = jnp.concatenate([kvp_ref[...], kvc_ref[...]], axis=0)
            k_var = _head_variants(kv[:, 0:2 * HEAD_DIM])
            v_var = _head_variants(kv[:, 2 * HEAD_DIM:])
            q2s = [q_ref[:, 2 * HEAD_DIM * j:2 * HEAD_DIM * (j + 1)].astype(BF16) for j in range(N_Q_HEADS // 2)]
            do2s = [dcat_ref[:, POOL_WIDTH + 2 * HEAD_DIM * j:POOL_WIDTH + 2 * HEAD_DIM * (j + 1)].astype(BF16)
                    for j in range(N_Q_HEADS // 2)]
            slot = _head_slot
            for hq in range(N_Q_HEADS):
                j, half, h = hq // 2, hq % 2, hq // GQA_GROUP
                dp_all[hq] = _dot_nt(do2s[j], v_var[h][half])
            sink_acc[...] = jnp.zeros_like(sink_acc)
            for hq in range(N_Q_HEADS):
                for r in range(0, BLOCK, ROW_CHUNK):
                    rows = slice(r, r + ROW_CHUNK)
                    probs = p_all[slot(hq), rows, :].astype(F32)
                    dp = dp_all[hq, rows, :]
                    delta = jnp.sum(probs * dp, axis=-1, keepdims=True)
                    ds = probs * (dp - delta)
                    dbias_ref[hq, rows, :] += ds
                    sink_acc[rows, :] += jnp.where(lane == hq, psink_ref[rows, :], 0.0) * delta
                    ds_all[slot(hq), rows, :] = (ds * ATTN_SCALE).astype(BF16)
            dsink_ref[...] -= jnp.sum(sink_acc[...], axis=0, keepdims=True)
            dq2 = [None] * (N_Q_HEADS // 2)
            for hq in range(N_Q_HEADS):
                j, half, h = hq // 2, hq % 2, hq // GQA_GROUP
                dq = _dot(ds_all[slot(hq)], k_var[h][half])
                dq2[j] = dq if dq2[j] is None else dq2[j] + dq
            low = lax.broadcasted_iota(jnp.int32, (2 * BLOCK, 2 * HEAD_DIM), 1) < HEAD_DIM
            dk_half, dv_half = [[None, None], [None, None]], [[None, None], [None, None]]
            for h in range(N_KV_HEADS):
                for half in range(2):
                    heads = [hq for hq in range(GQA_GROUP * h, GQA_GROUP * (h + 1)) if hq % 2 == half]
                    base = slot(heads[0])
                    q_rows = jnp.concatenate([q2s[hq // 2] for hq in heads], axis=0)
                    do_rows = jnp.concatenate([do2s[hq // 2] for hq in heads], axis=0)
                    dk_half[h][half] = _dot_tn(_merge_rows(ds_all[base:base + 2]), q_rows)
                    dv_half[h][half] = _dot_tn(_merge_rows(p_all[base:base + 2]), do_rows)

            def pair_of(halves):
                return jnp.where(low, halves[0][0] + pltpu.roll(halves[0][1], HEAD_DIM, 1),
                                 halves[1][1] + pltpu.roll(halves[1][0], HEAD_DIM, 1))

            dkv = jnp.concatenate([pair_of(dk_half), pair_of(dv_half)], axis=1)
            dproj_ref[:, POOL_WIDTH:2 * POOL_WIDTH] = c_q[...].astype(BF16)
            dproj_ref[:, 2 * POOL_WIDTH:] = (c_kv[...] + dkv[0:BLOCK]).astype(BF16)
            c_q[...] = jnp.concatenate(dq2, axis=1)
            c_kv[...] = dkv[BLOCK:]

        @pl.when(i == nb)
        def _():
            dbuf[BLOCK:, :] = jnp.zeros((HALO, POOL_WIDTH), F32)
            for g, w in enumerate(POOL_WINDOWS):
                cols = slice(g * POOL_GROUP_DIM, (g + 1) * POOL_GROUP_DIM)
                dproj_ref[:, cols] = (_window_sum(dbuf, g, w, lambda k: k) + c_u[:, cols]).astype(BF16)
            dproj_ref[:, POOL_WIDTH:2 * POOL_WIDTH] = c_q[...].astype(BF16)
            dproj_ref[:, 2 * POOL_WIDTH:] = c_kv[...].astype(BF16)
            bk = bk_ref[...]
            for h in range(N_Q_HEADS):
                db = dbias_ref[h]
                for b in range(N_BUCKETS):
                    drb_ref[h, b] = jnp.sum(jnp.where(bk == float(b), db, 0.0))
            finish()

    cur = lambda i: jnp.minimum(i, nb - 1)
    prv = lambda i: jnp.maximum(jnp.minimum(i, nb - 1) - 1, 0)
    out = pl.pallas_call(
        body, name="mixers_bwd", grid=(nb + 1,),
        out_shape=[jax.ShapeDtypeStruct((t, proj.shape[1]), BF16),
                   jax.ShapeDtypeStruct((N_Q_HEADS, BLOCK, 2 * BLOCK), F32),
                   jax.ShapeDtypeStruct((1, 128), F32),
                   jax.ShapeDtypeStruct((4, POOL_GROUP_DIM, POOL_GROUP_DIM), F32),
                   jax.ShapeDtypeStruct((len(POOL_WINDOWS), POOL_GROUP_DIM), F32),
                   jax.ShapeDtypeStruct((N_Q_HEADS, N_BUCKETS), F32)]
        + [jax.ShapeDtypeStruct(p.shape, p.dtype) for p in ffn_parts],
        in_specs=_mixer_in_specs(cur, prv)[2:]
        + [pl.BlockSpec((BLOCK, 2 * POOL_WIDTH), lambda i: (cur(i), 0)), pl.BlockSpec((BLOCK, POOL_WIDTH), lambda i: (cur(i), 0)),
           pl.BlockSpec((N_Q_HEADS, BLOCK, 2 * BLOCK), lambda i: (0, cur(i), 0)), pl.BlockSpec((BLOCK, 128), lambda i: (cur(i), 0))]
        + _mixer_param_specs() + [pl.BlockSpec((BLOCK, 2 * BLOCK), lambda i: (0, 0))] + [ANY] * na,
        out_specs=[pl.BlockSpec((BLOCK, proj.shape[1]), lambda i: (jnp.maximum(i - 1, 0), 0)),
                   pl.BlockSpec((N_Q_HEADS, BLOCK, 2 * BLOCK), lambda i: (0, 0, 0)),
                   pl.BlockSpec((1, 128), lambda i: (0, 0)),
                   pl.BlockSpec((4, POOL_GROUP_DIM, POOL_GROUP_DIM), lambda i: (0, 0, 0)),
                   pl.BlockSpec((len(POOL_WINDOWS), POOL_GROUP_DIM), lambda i: (0, 0)), SMEM] + [ANY] * na,
        scratch_shapes=[pltpu.VMEM((BLOCK + HALO, POOL_WIDTH), F32),
                        pltpu.VMEM((BLOCK, POOL_WIDTH), F32), pltpu.VMEM((BLOCK, POOL_WIDTH), F32),
                        pltpu.VMEM((BLOCK, 256), F32),
                        pltpu.VMEM((N_Q_HEADS, BLOCK, 2 * BLOCK), F32), pltpu.VMEM((N_Q_HEADS, BLOCK, 2 * BLOCK), BF16),
                        pltpu.VMEM((BLOCK, 128), F32)]
        + _chip_exchange_scratch(ffn_parts),
        compiler_params=_params(collective_id=CHIP_COLLECTIVE_ID),
    )(proj, proj, proj, dcat, pooled, probs, p_sinks, w_pool, pool_scale, bucket, *ffn_parts)
    return out[:6], out[6:]


def inproj_bwd(dproj, w_in_t, x, g, dx1):
    t, d = x.shape
    n = dproj.shape[1]
    tm = TOKEN_TILE

    def body(dp_ref, w_ref, x_ref, g_ref, dx1_ref, dx_ref, dg_ref):
        @pl.when(pl.program_id(0) == 0)
        def _():
            dg_ref[...] = jnp.zeros_like(dg_ref)

        dh = _dot(dp_ref[...], w_ref[...])
        xv = x_ref[...]
        dx, dg_rows = _norm_bwd(dh, xv, _rstd(xv), g_ref[...])
        dx_ref[...] = dx1_ref[...] + dx
        dg_ref[...] += _as_rows(jnp.sum(dg_rows, axis=0, keepdims=True))

    row = pl.BlockSpec((tm, d), lambda i: (i, 0))
    gain = pl.BlockSpec((1, d), lambda i: (0, 0))
    return pl.pallas_call(
        body, name="inproj_bwd", grid=(t // tm,),
        out_shape=[jax.ShapeDtypeStruct((t, d), F32), jax.ShapeDtypeStruct((d // 128, 128), F32)],
        in_specs=[pl.BlockSpec((tm, n), lambda i: (i, 0)), pl.BlockSpec(w_in_t.shape, lambda i: (0, 0)), row, gain, row],
        out_specs=[row, pl.BlockSpec((d // 128, 128), lambda i: (0, 0))],
        compiler_params=_params(),
    )(dproj, w_in_t, x, g, dx1)


def _bucket_band():
    qi = jnp.arange(BLOCK)[:, None]
    kj = jnp.arange(2 * BLOCK)[None, :]
    dist = qi + BLOCK - kj
    n = jnp.maximum(dist, 0)
    nf = jnp.maximum(n, 1).astype(F32)
    large = MAX_EXACT + (jnp.log(nf / MAX_EXACT) / np.float32(np.log(MAX_DISTANCE / MAX_EXACT))
                         * (N_BUCKETS - MAX_EXACT)).astype(jnp.int32)
    large = jnp.minimum(large, N_BUCKETS - 1)
    bucket = jnp.where(n < MAX_EXACT, n, large)
    in_window = (dist >= 0) & (dist < BLOCK)
    return bucket.astype(F32), in_window.astype(F32)


def kernel(x, g_pre_mix, w_in, w_pool, pool_scale, rel_bias, sinks, w_out, g_post_mix, g_pre_ffn, w_gate, w_up, w_down, g_post_ffn, loss_target, m_g_pre_mix, m_w_in, m_w_pool, m_pool_scale, m_rel_bias, m_sinks, m_w_out, m_g_post_mix, m_g_pre_ffn, m_w_gate, m_w_up, m_w_down, m_g_post_ffn, v_g_pre_mix, v_w_in, v_w_pool, v_pool_scale, v_rel_bias, v_sinks, v_w_out, v_g_post_mix, v_g_pre_ffn, v_w_gate, v_w_up, v_w_down, v_g_post_ffn):
    d = x.shape[-1]
    xs, target = x[0], loss_target[0]

    w_in_ts = w_in[0].T.astype(BF16)
    w_out_s = w_out[0].astype(BF16)
    gate_ts = w_gate[0].T.astype(BF16)
    up_ts = w_up[0].T.astype(BF16)
    w_down_s = w_down[0].astype(BF16)

    bucket, in_window = _bucket_band()
    w_pool_b = w_pool[0].astype(BF16)
    half = up_ts.shape[0] // 2
    proj, h1, w_in_t, up_t = norm_inproj(xs, g_pre_mix, w_in_ts, up_ts[:half], up_ts.shape[0])
    w_in_t = w_in_t.reshape(-1, d)
    cat, pooled, probs, p_sinks, gate_t, w_out_f = mixers_fwd(
        proj, bucket, in_window, rel_bias, sinks, w_pool_b, pool_scale, [gate_ts, w_out_s])
    w_out_f = w_out_f.reshape(-1, d)
    mix, x1, h2, up_t = outproj_norm(cat, w_out_f, xs, g_post_mix, g_pre_ffn, up_ts[half:], up_t)
    gate, up, act, w_down_f = ffn_up(h2, gate_t, up_t, w_down_s)
    df, dy, dg_post_ffn, loss_part = ffn_down_loss(act, w_down_f, x1, g_post_ffn, target)

    (dgate, dup), (d_gate, d_up) = ffn_down_bwd(df, w_down_f, gate, up, h2)
    (d_down,), got_gate_up = grad_ffn([act], df, "grad_w_down", [d_gate, d_up])
    q_gate, q_up, q_down = pair_add_exchange([d_gate, d_up], got_gate_up, d_down, "pair_add_ffn")
    (dx1, dmix, dg_pre_ffn, dg_post_mix), (gate_slots, down_slots) = ffn_up_bwd(
        dgate, dup, gate_t, up_t, x1, g_pre_ffn, dy, mix, g_post_mix, [q_gate, q_down])
    dcat, d_out = outproj_bwd(dmix, w_out_f, cat)
    q_out, = pair_add_exchange([], [], d_out, "pair_add_out")
    (dproj, _, dsinks, dw_pool, dpool_scale, drel_bias), (up_slots, out_slots) = mixers_bwd(
        proj, dcat, pooled, probs, p_sinks, w_pool_b, pool_scale, bucket, [q_up, q_out])
    grad_x, dg_pre_mix = inproj_bwd(dproj, w_in_t, xs, g_pre_mix, dx1)

    small_w = [g_pre_mix, g_post_mix, g_pre_ffn, g_post_ffn, pool_scale, sinks, w_pool, rel_bias.T]
    small_m = [m_g_pre_mix, m_g_post_mix, m_g_pre_ffn, m_g_post_ffn, m_pool_scale, m_sinks, m_w_pool, m_rel_bias.T]
    small_v = [v_g_pre_mix, v_g_post_mix, v_g_pre_ffn, v_g_post_ffn, v_pool_scale, v_sinks, v_w_pool, v_rel_bias.T]
    d_in_t, total, total_rb = grad_w_in_small_reduce(
        dproj, h1, [dg_pre_mix, dg_post_mix, dg_pre_ffn, dg_post_ffn], dpool_scale, dsinks, loss_part, dw_pool, drel_bias)
    g_in_t = reduce_w_in(d_in_t)
    loss_row, sm = small_adamw(total, total_rb, small_w, small_m, small_v)
    sm[7] = [r.T for r in sm[7]]
    big_w = [w_in[0].T, w_out[0], w_gate[0].T, w_up[0].T, w_down[0]]
    big_m = [m_w_in[0].T, m_w_out[0], m_w_gate[0].T, m_w_up[0].T, m_w_down[0]]
    big_v = [v_w_in[0].T, v_w_out[0], v_w_gate[0].T, v_w_up[0].T, v_w_down[0]]
    upd = sum_adamw([out_slots, gate_slots, up_slots, down_slots], big_w[1:], big_m[1:], big_v[1:], "sum_adamw")
    upd = [[g_in_t, *adamw_update(big_w[:1], [g_in_t], big_m[:1], big_v[:1], "adamw_in")[0]], *upd]
    back = lambda k, a: (a.T if k in (0, 2, 3) else a)[None]
    big = [[back(k, u) for u in upd[k]] for k in range(5)]

    def ordered(kind):
        s, b = [p[kind] for p in sm], [p[kind] for p in big]
        return [s[0], b[0], s[6], s[4], s[7], s[5], b[1], s[1], s[2], b[2], b[3], b[4], s[3]]

    return (loss_row[0, 0], grad_x[None], *ordered(0), *ordered(1), *ordered(2), *ordered(3))
```

```python
import numpy as np
import jax
import jax.numpy as jnp
from jax import lax
from jax.experimental import pallas as pl
from jax.experimental.pallas import tpu as pltpu

F32 = jnp.float32
BF16 = jnp.bfloat16

N_DEV = 8
N_CHIP = 4
POOL_WIDTH = 512
POOL_WINDOWS = (2, 4, 8, 16)
POOL_GROUP_DIM = 128
HEAD_DIM = 64
N_Q_HEADS = 8
N_KV_HEADS = 2
GQA_GROUP = 4
BLOCK = 128
HALO = 16
ROW_CHUNK = 32
N_BUCKETS = 32
MAX_EXACT = 16
MAX_DISTANCE = 128
EPS = 1e-6
NEG_INF = -1e30
ATTN_SCALE = float(1.0 / np.sqrt(np.float32(HEAD_DIM)))

ADAM_LR = 0.001
ADAM_B1 = 0.9
ADAM_B2 = 0.999
ADAM_EPS = 1e-08
ADAM_WD = 0.01
ADAM_STEP = 10

TOKEN_TILE = 1024
WIDE_K_TOKEN_TILE = 512
FFN_TOKEN_TILE = 1024
FF_SHARDS_PER_TILE = 4
VMEM_LIMIT = 56 * 1024 * 1024
MESH = pl.DeviceIdType.MESH
PAIR_COLLECTIVE_ID = 0
GATHER_COLLECTIVE_ID = 1
CHIP_COLLECTIVE_ID = 2
ANY = pl.BlockSpec(memory_space=pl.ANY)
VMEM = pl.BlockSpec(memory_space=pltpu.VMEM)
SMEM = pl.BlockSpec(memory_space=pltpu.SMEM)


def _params(**kw):
    return pltpu.CompilerParams(vmem_limit_bytes=VMEM_LIMIT, **kw)


def _dot(a, b):
    return jnp.dot(a, b, preferred_element_type=F32)


def _dot_nt(a, b):
    return lax.dot_general(a, b, (((1,), (1,)), ((), ())), preferred_element_type=F32)


def _dot_tn(a, b):
    return lax.dot_general(a, b, (((0,), (0,)), ((), ())), preferred_element_type=F32)


def _rstd(v):
    return lax.rsqrt(jnp.mean(v * v, axis=-1, keepdims=True) + EPS)


def _norm_bwd(dout, v, r, g):
    vn = v * r
    dn = dout * g
    dv = r * (dn - vn * jnp.mean(dn * vn, axis=-1, keepdims=True))
    return dv, dout * vn


def _as_rows(v):
    return jnp.concatenate([v[:, k:k + 128] for k in range(0, v.shape[1], 128)], axis=0)


def _as_lanes(rows):
    return jnp.concatenate([rows[k:k + 1, :] for k in range(rows.shape[0])], axis=1)


def _handshake(peers):
    barrier = pltpu.get_barrier_semaphore()
    for peer in peers:
        pl.semaphore_signal(barrier, inc=1, device_id=peer, device_id_type=MESH)
    pl.semaphore_wait(barrier, len(peers))


def _merge_rows(value):
    s, r, c_ = value.shape
    return value.reshape(s * r, c_)


def _gather_plan(srcs, outs, send_sems, recv_sems, local_sems=None, bounce=None, rows=None, shake=True):
    n = len(srcs)
    x, y, c = lax.axis_index("x"), lax.axis_index("y"), lax.axis_index("c")
    me, sibling = (x, y, c), (x, y, 1 - c)
    chips = [(1 - x, y), (x, 1 - y), (1 - x, 1 - y)]

    def slot(a, px, py, pc):
        whole = outs[a].at[4 * px + 2 * py + pc]
        return whole if rows is None or rows[a] is None else whole.at[pl.ds(*rows[a])]

    def copy(a, k, block, to, from_src=False):
        return pltpu.make_async_remote_copy(
            src_ref=srcs[a] if from_src else slot(a, *block), dst_ref=slot(a, *block),
            send_sem=send_sems.at[k * n + a], recv_sem=recv_sems.at[k * n + a], device_id=to, device_id_type=MESH)

    def own_in(a):
        return pltpu.make_async_copy(srcs[a], bounce[a], local_sems.at[a])

    def own_out(a):
        return pltpu.make_async_copy(bounce[a], slot(a, *me), local_sems.at[a])

    def first(a):
        return [copy(a, 0, me, sibling, True)] + [copy(a, 1 + j, me, (*chip, c), True) for j, chip in enumerate(chips)]

    def passed(a, j):
        return copy(a, 4 + j, (*chips[j], c), sibling)

    def start():
        if shake:
            _handshake([sibling] + [(*chip, c) for chip in chips])
        for a in range(n):
            if bounce is not None:
                own_in(a).start()
            for cp in first(a):
                cp.start()

    def forward():
        if bounce is not None:
            for a in range(n):
                own_in(a).wait()
                own_out(a).start()
        for j, chip in enumerate(chips):
            for a in range(n):
                copy(a, 1 + j, (*chip, c), me).wait_recv()
                passed(a, j).start()

    def complete():
        for a in range(n):
            copy(a, 0, sibling, me).wait_recv()
            for j, chip in enumerate(chips):
                copy(a, 4 + j, (*chip, 1 - c), me).wait_recv()
        for a in range(n):
            for cp in first(a) + [passed(a, j) for j in range(3)]:
                cp.wait_send()
            if bounce is not None:
                own_out(a).wait()

    def finish():
        forward()
        complete()

    finish.forward, finish.complete = forward, complete
    return start, finish


def _gather_scratch(shards):
    n = len(shards)
    return [pltpu.SemaphoreType.DMA((7 * n,)), pltpu.SemaphoreType.DMA((7 * n,)), pltpu.SemaphoreType.DMA((n,))] \
        + [pltpu.VMEM(s.shape, s.dtype) for s in shards]


def _chip_exchange_plan(srcs, outs, send_sems, recv_sems, local_sems, bounce):
    n = len(srcs)
    x, y, c = lax.axis_index("x"), lax.axis_index("y"), lax.axis_index("c")
    my_chip = 2 * x + y

    def copies():
        out = []
        for a in range(n):
            for k in range(1, N_CHIP):
                px, py = x ^ (k >> 1), y ^ (k & 1)
                out.append(pltpu.make_async_remote_copy(
                    src_ref=srcs[a].at[2 * px + py], dst_ref=outs[a].at[my_chip],
                    send_sem=send_sems.at[(k - 1) * n + a], recv_sem=recv_sems.at[(k - 1) * n + a],
                    device_id=(px, py, c), device_id_type=MESH))
        return out

    def own_in(a):
        return pltpu.make_async_copy(srcs[a].at[my_chip], bounce[a], local_sems.at[a])

    def own_out(a):
        return pltpu.make_async_copy(bounce[a], outs[a].at[my_chip], local_sems.at[a])

    def start():
        _handshake([(x ^ (k >> 1), y ^ (k & 1), c) for k in range(1, N_CHIP)])
        for a in range(n):
            own_in(a).start()
        for cp in copies():
            cp.start()

    def finish():
        for a in range(n):
            own_in(a).wait()
            own_out(a).start()
        for cp in copies():
            cp.wait()
        for a in range(n):
            own_out(a).wait()

    return start, finish


def _chip_exchange_scratch(parts):
    n = len(parts)
    return [pltpu.SemaphoreType.DMA((3 * n,)), pltpu.SemaphoreType.DMA((3 * n,)), pltpu.SemaphoreType.DMA((n,))] \
        + [pltpu.VMEM(p.shape[1:], p.dtype) for p in parts]


def _pair_plan(srcs, outs, send_sems, recv_sems):
    x, y, c = lax.axis_index("x"), lax.axis_index("y"), lax.axis_index("c")

    def copies():
        return [pltpu.make_async_remote_copy(
            src_ref=srcs[a].at[1 - c], dst_ref=outs[a], send_sem=send_sems.at[a], recv_sem=recv_sems.at[a],
            device_id=(x, y, 1 - c), device_id_type=MESH) for a in range(len(srcs))]

    def start():
        _handshake([(x, y, 1 - c)])
        for cp in copies():
            cp.start()

    def finish():
        for cp in copies():
            cp.wait()

    return start, finish


def pair_add_exchange(parts, got, late, name):
    n, nl = len(parts), len(late)
    chips = late[0].shape[1]

    def body(core_ref, *refs):
        late_own, late_all = refs[2 * n:2 * n + nl], refs[2 * n + nl:2 * n + 2 * nl]
        outs = refs[2 * n + 2 * nl:3 * n + 2 * nl]
        late_outs = refs[3 * n + 2 * nl:3 * n + 3 * nl]
        landed = refs[3 * n + 3 * nl:3 * n + 4 * nl]
        send_sems, recv_sems = refs[3 * n + 4 * nl:]
        i = pl.program_id(0)
        start, finish = _pair_plan(late_all, landed, send_sems, recv_sems)
        pl.when(i == 0)(start)
        for a in range(n):
            outs[a][...] = (refs[a][...].astype(F32) + refs[n + a][...].astype(F32)).astype(BF16)

        @pl.when(i == chips - 1)
        def _():
            finish()
            for b in range(nl):
                late_outs[b][...] = (late_own[b][...].astype(F32) + landed[b][...].astype(F32)).astype(BF16)

    def own(p):
        zeros = (0,) * (p.ndim - 2)
        return pl.BlockSpec((None, 1, *p.shape[2:]), lambda i, core: (core[0], i, *zeros))

    def plain(p):
        zeros = (0,) * (p.ndim - 1)
        return pl.BlockSpec((1, *p.shape[1:]), lambda i, core: (i, *zeros))

    def own_whole(p):
        zeros = (0,) * (p.ndim - 1)
        return pl.BlockSpec((None, *p.shape[1:]), lambda i, core: (core[0], *zeros))

    def whole(p):
        zeros = (0,) * (p.ndim - 1)
        return pl.BlockSpec(p.shape[1:], lambda i, core: zeros)

    core = lax.axis_index("c").astype(jnp.int32).reshape(1)
    return pl.pallas_call(
        body, name=name,
        grid_spec=pltpu.PrefetchScalarGridSpec(
            num_scalar_prefetch=1, grid=(chips,),
            in_specs=[own(p) for p in parts] + [plain(p) for p in got] + [own_whole(p) for p in late] + [ANY] * nl,
            out_specs=[plain(p) for p in got] + [whole(p) for p in late],
            scratch_shapes=[pltpu.VMEM(p.shape[1:], p.dtype) for p in late]
            + [pltpu.SemaphoreType.DMA((nl,)), pltpu.SemaphoreType.DMA((nl,))]),
        out_shape=[jax.ShapeDtypeStruct(p.shape, BF16) for p in got] + [jax.ShapeDtypeStruct(p.shape[1:], BF16) for p in late],
        compiler_params=_params(collective_id=PAIR_COLLECTIVE_ID),
    )(core, *parts, *got, *late, *late)


def _adamw(w, g, m, v):
    m2 = ADAM_B1 * m + (1.0 - ADAM_B1) * g
    v2 = ADAM_B2 * v + (1.0 - ADAM_B2) * (g * g)
    m_hat = m2 / (1.0 - ADAM_B1 ** ADAM_STEP)
    v_hat = v2 / (1.0 - ADAM_B2 ** ADAM_STEP)
    delta = -ADAM_LR * (m_hat / (jnp.sqrt(v_hat) + ADAM_EPS) + ADAM_WD * w)
    return delta, m2, v2


def sum_adamw(slots, ws, ms, vs, name):
    n = len(ws)
    halves = 2

    def body(*refs):
        for a in range(n):
            total = refs[a][0].astype(F32)
            for s in range(1, slots[a].shape[0]):
                total = total + refs[a][s].astype(F32)
            delta, m2, v2 = _adamw(refs[n + a][...], total, refs[2 * n + a][...], refs[3 * n + a][...])
            for q, val in enumerate((total, delta, m2, v2)):
                refs[4 * n + 4 * a + q][...] = val

    def rows(w):
        return pl.BlockSpec((w.shape[0] // halves, w.shape[1]), lambda i: (i, 0))

    def slot_rows(p):
        return pl.BlockSpec((p.shape[0], p.shape[1] // halves, p.shape[2]), lambda i: (0, i, 0))

    out = pl.pallas_call(
        body, name=name, grid=(halves,),
        out_shape=[jax.ShapeDtypeStruct(w.shape, F32) for w in ws for _ in range(4)],
        in_specs=[slot_rows(p) for p in slots] + [rows(w) for w in ws] * 3,
        out_specs=[rows(w) for w in ws for _ in range(4)],
        compiler_params=_params(),
    )(*slots, *ws, *ms, *vs)
    return [out[4 * a:4 * a + 4] for a in range(n)]


def adamw_update(ws, gs, ms, vs, name):
    n = len(ws)

    def body(*refs):
        for a in range(n):
            delta, m2, v2 = _adamw(refs[a][...], refs[n + a][...], refs[2 * n + a][...], refs[3 * n + a][...])
            refs[4 * n + 3 * a][...] = delta
            refs[4 * n + 3 * a + 1][...] = m2
            refs[4 * n + 3 * a + 2][...] = v2

    out = pl.pallas_call(
        body, name=name,
        out_shape=[jax.ShapeDtypeStruct(w.shape, F32) for w in ws for _ in range(3)],
        in_specs=[VMEM] * (4 * n), out_specs=[VMEM] * (3 * n),
        compiler_params=_params(),
    )(*ws, *gs, *ms, *vs)
    return [out[3 * a:3 * a + 3] for a in range(n)]


GAIN_ROWS = 8
ROW_POOL_SCALE = 4 * GAIN_ROWS
ROW_SINKS = ROW_POOL_SCALE + 4
ROW_LOSS = ROW_SINKS + 1
ROW_W_POOL = 40
SMALL_ROWS = ROW_W_POOL + 4 * POOL_GROUP_DIM


def grad_w_in_small_reduce(a, b, gains, dpool_scale, dsinks, loss_part, dw_pool, drel_bias):
    t, m = a.shape
    d = b.shape[1]
    r = m // N_DEV
    tt = TOKEN_TILE
    last = t // tt - 1

    def body(a_ref, b_ref, g0, g1, g2, g3, dsc_ref, dsink_ref, loss_ref, dwp_ref, drb_ref, out_ref, total_ref, total_rb_ref,
             acc, stage, gat, gat_rb, g_send, g_recv):
        k = pl.program_id(0)
        x, y, c = lax.axis_index("x"), lax.axis_index("y"), lax.axis_index("c")
        start, finish = _gather_plan([stage, drb_ref], [gat, gat_rb], g_send, g_recv)

        @pl.when(k == 0)
        def _():
            for q, g_ref in enumerate((g0, g1, g2, g3)):
                stage[GAIN_ROWS * q:GAIN_ROWS * (q + 1), :] = g_ref[...]
            stage[ROW_POOL_SCALE:ROW_SINKS, :] = dsc_ref[...]
            stage[ROW_SINKS:ROW_LOSS, :] = dsink_ref[...]
            stage[ROW_LOSS:ROW_LOSS + 1, :] = loss_ref[...]
            stage[ROW_LOSS + 1:ROW_W_POOL, :] = jnp.zeros((ROW_W_POOL - ROW_LOSS - 1, 128), F32)
            stage[ROW_W_POOL:, :] = dwp_ref[...].reshape(4 * POOL_GROUP_DIM, POOL_GROUP_DIM)
            gat[4 * x + 2 * y + c] = stage[...]
            gat_rb[4 * x + 2 * y + c] = drb_ref[...]
            start()
            acc[...] = jnp.zeros_like(acc)

        acc[...] += _dot_tn(a_ref[...], b_ref[...])

        @pl.when(k == last)
        def _():
            blocks = acc[...].reshape(N_CHIP, 2, r, d)
            for chip in range(N_CHIP):
                for core in range(2):
                    out_ref[core, chip] = blocks[chip, core].astype(BF16)
            finish()
            total, total_rb = gat[0], gat_rb[0]
            for s in range(1, N_DEV):
                total, total_rb = total + gat[s], total_rb + gat_rb[s]
            total_ref[...] = total
            total_rb_ref[...] = total_rb

    out_shape = (2, N_CHIP, r, d)
    return pl.pallas_call(
        body, name="grad_w_in", grid=(t // tt,),
        out_shape=[jax.ShapeDtypeStruct(out_shape, BF16), jax.ShapeDtypeStruct((SMALL_ROWS, 128), F32),
                   jax.ShapeDtypeStruct(drel_bias.shape, F32)],
        in_specs=[pl.BlockSpec((tt, m), lambda k: (k, 0)), pl.BlockSpec((tt, d), lambda k: (k, 0))] + [VMEM] * 9,
        out_specs=[pl.BlockSpec(out_shape, lambda k: (0,) * len(out_shape)), VMEM, VMEM],
        scratch_shapes=[pltpu.VMEM((m, d), F32), pltpu.VMEM((SMALL_ROWS, 128), F32),
                        pltpu.VMEM((N_DEV, SMALL_ROWS, 128), F32), pltpu.VMEM((N_DEV, *drel_bias.shape), F32),
                        pltpu.SemaphoreType.DMA((14,)), pltpu.SemaphoreType.DMA((14,))],
        compiler_params=_params(collective_id=GATHER_COLLECTIVE_ID),
    )(a, b, *gains, dpool_scale, dsinks, loss_part, dw_pool, drel_bias)


def reduce_w_in(d_in_t):
    def body(d_in_ref, g_in_ref, pair_got, chip_part, chip_got, p_send, p_recv, x_send, x_recv):
        x, y, c = lax.axis_index("x"), lax.axis_index("y"), lax.axis_index("c")
        my_chip = 2 * x + y
        _handshake([(x, y, 1 - c)] + [(x ^ (k >> 1), y ^ (k & 1), c) for k in range(1, N_CHIP)])
        pair = pltpu.make_async_remote_copy(
            src_ref=d_in_ref.at[1 - c], dst_ref=pair_got, send_sem=p_send, recv_sem=p_recv,
            device_id=(x, y, 1 - c), device_id_type=MESH)
        pair.start()
        pair.wait()
        chip_part[...] = (d_in_ref[c].astype(F32) + pair_got[...].astype(F32)).astype(BF16)
        copies = []
        for k in range(1, N_CHIP):
            px, py = x ^ (k >> 1), y ^ (k & 1)
            copies.append(pltpu.make_async_remote_copy(
                src_ref=chip_part.at[2 * px + py], dst_ref=chip_got.at[my_chip],
                send_sem=x_send.at[k - 1], recv_sem=x_recv.at[k - 1], device_id=(px, py, c), device_id_type=MESH))
        for cp in copies:
            cp.start()
        chip_got[my_chip] = chip_part[my_chip]
        for cp in copies:
            cp.wait()
        g_in = chip_got[0].astype(F32)
        for s in range(1, N_CHIP):
            g_in = g_in + chip_got[s].astype(F32)
        g_in_ref[...] = g_in

    per_core = d_in_t.shape[1:]
    return pl.pallas_call(
        body, name="reduce_w_in",
        out_shape=jax.ShapeDtypeStruct(d_in_t.shape[2:], F32),
        in_specs=[VMEM], out_specs=VMEM,
        scratch_shapes=[pltpu.VMEM(per_core, d_in_t.dtype), pltpu.VMEM(per_core, d_in_t.dtype),
                        pltpu.VMEM(per_core, d_in_t.dtype),
                        pltpu.SemaphoreType.DMA, pltpu.SemaphoreType.DMA,
                        pltpu.SemaphoreType.DMA((3,)), pltpu.SemaphoreType.DMA((3,))],
        compiler_params=_params(collective_id=GATHER_COLLECTIVE_ID),
    )(d_in_t)


def small_adamw(total, total_rb, small_w, small_m, small_v):
    n_small = len(small_w)

    def body(*refs):
        total_ref, rb_ref = refs[:2]
        w_refs, m_refs, v_refs = (refs[2 + k * n_small:2 + (k + 1) * n_small] for k in range(3))
        loss_out = refs[2 + 3 * n_small]
        result = refs[3 + 3 * n_small:]
        total = total_ref[...]
        loss_out[...] = total[ROW_LOSS:ROW_LOSS + 1, :]
        grads = [_as_lanes(total[GAIN_ROWS * k:GAIN_ROWS * (k + 1), :]) for k in range(4)]
        grads.append(_as_lanes(total[ROW_POOL_SCALE:ROW_SINKS, :]))
        grads.append(total[ROW_SINKS:ROW_LOSS, 0:N_Q_HEADS])
        grads.append(total[ROW_W_POOL:, :].reshape(w_refs[6].shape))
        grads.append(rb_ref[...])
        for k in range(n_small):
            delta, m2, v2 = _adamw(w_refs[k][...], grads[k], m_refs[k][...], v_refs[k][...])
            result[4 * k][...] = grads[k]
            result[4 * k + 1][...] = delta
            result[4 * k + 2][...] = m2
            result[4 * k + 3][...] = v2

    out = pl.pallas_call(
        body, name="small_adamw",
        out_shape=[jax.ShapeDtypeStruct((1, 128), F32)] + [jax.ShapeDtypeStruct(w.shape, F32) for w in small_w for _ in range(4)],
        in_specs=[VMEM] * (2 + 3 * n_small), out_specs=[VMEM] * (1 + 4 * n_small),
        compiler_params=_params(),
    )(total, total_rb, *small_w, *small_m, *small_v)
    return out[0], [out[1 + 4 * k:5 + 4 * k] for k in range(n_small)]


def norm_inproj(x, g, w_shard, shard, shard_rows):
    t, d = x.shape
    r = w_shard.shape[0]
    tm = TOKEN_TILE
    nt = t // tm

    def body(x_ref, g_ref, w_shard_ref, shard_ref, proj_ref, h_ref, w_ref, gathered_ref, h_all, w_all, w_sem,
             send_w, recv_w, local_w, bounce_w, send_sems, recv_sems, local_sems, bounce):
        i = pl.program_id(0)
        start_w, finish_w = _gather_plan([w_shard_ref], [w_ref], send_w, recv_w, local_w, [bounce_w])
        start, finish = _gather_plan([shard_ref], [gathered_ref], send_sems, recv_sems, local_sems, [bounce],
                                     [(0, shard.shape[0])], shake=False)

        @pl.when(i == 0)
        def _():
            start_w()
            start()

        @pl.when(i < nt)
        def _():
            xv = x_ref[...]
            h = ((xv * _rstd(xv)) * g_ref[...]).astype(BF16)
            h_ref[...] = h
            h_all[pl.ds(pl.multiple_of(i * tm, tm), tm), :] = h

        @pl.when(i == nt - 1)
        def _():
            finish_w()
            landed = pltpu.make_async_copy(w_ref, w_all, w_sem)
            landed.start()
            landed.wait()

        @pl.when(i >= nt)
        def _():
            rows = pl.ds(pl.multiple_of((i - nt) * tm, tm), tm)
            proj_ref[...] = _dot_nt(h_all[rows, :], _merge_rows(w_all[...]))

        pl.when(i == 2 * nt - 2)(finish.forward)
        pl.when(i == 2 * nt - 1)(finish.complete)

    first = lambda i: (jnp.minimum(i, nt - 1), 0)
    return pl.pallas_call(
        body, name="norm_inproj", grid=(2 * nt,),
        out_shape=[jax.ShapeDtypeStruct((t, N_DEV * r), F32), jax.ShapeDtypeStruct((t, d), BF16),
                   jax.ShapeDtypeStruct((N_DEV, r, d), w_shard.dtype),
                   jax.ShapeDtypeStruct((N_DEV, shard_rows, d), shard.dtype)],
        in_specs=[pl.BlockSpec((tm, d), first), pl.BlockSpec((1, d), lambda i: (0, 0)), ANY, ANY],
        out_specs=[pl.BlockSpec((tm, N_DEV * r), lambda i: (jnp.maximum(i - nt, 0), 0)), pl.BlockSpec((tm, d), first),
                   ANY, ANY],
        scratch_shapes=[pltpu.VMEM((t, d), BF16), pltpu.VMEM((N_DEV, r, d), w_shard.dtype), pltpu.SemaphoreType.DMA]
        + _gather_scratch([w_shard]) + _gather_scratch([shard]),
        compiler_params=_params(collective_id=GATHER_COLLECTIVE_ID),
    )(x, g, w_shard, shard)


def _fill_bias_band(bk_ref, win_ref, rb_ref, biasm_ref):
    bk = bk_ref[...]
    keep = win_ref[...] > 0.5
    for h in range(N_Q_HEADS):
        acc = jnp.zeros(bk.shape, F32)
        for b in range(N_BUCKETS):
            acc = jnp.where(bk == float(b), rb_ref[b, h], acc)
        biasm_ref[h] = jnp.where(keep, acc, NEG_INF)


def _window_sum(buf_ref, g, w, first):
    cols = slice(g * POOL_GROUP_DIM, (g + 1) * POOL_GROUP_DIM)
    acc = None
    for k in range(w):
        piece = buf_ref[first(k):first(k) + BLOCK, cols]
        acc = piece if acc is None else acc + piece
    return acc


def _inv_count(i, w):
    row = lax.broadcasted_iota(jnp.int32, (BLOCK, 1), 0)
    return 1.0 / jnp.minimum(i * BLOCK + row + 1, w).astype(F32)


def _fill_pool_input(i, ubuf, uc_ref, halo_ref):
    ubuf[0:HALO, :] = jnp.where(i > 0, halo_ref[...], 0.0)
    ubuf[HALO:, :] = uc_ref[...]


def _pooled(i, g, w, ubuf):
    cols = slice(g * POOL_GROUP_DIM, (g + 1) * POOL_GROUP_DIM)
    return _window_sum(ubuf, g, w, lambda k: HALO - k) * _inv_count(i, w) - ubuf[HALO:, cols]


def _head_variants(pair):
    low = lax.broadcasted_iota(jnp.int32, pair.shape, 1) < HEAD_DIM
    swapped = pltpu.roll(pair, HEAD_DIM, 1)
    zero = jnp.zeros_like(pair)
    pick = lambda c, a, b: jnp.where(c, a, b).astype(BF16)
    return [[pick(low, pair, zero), pick(low, zero, swapped)], [pick(low, swapped, zero), pick(low, zero, pair)]]


def _head_probs(i, hq, rows, s_ref, biasm_ref, sinks_ref):
    s = s_ref[hq, rows, :] * ATTN_SCALE + biasm_ref[hq, rows, :]
    col = lax.broadcasted_iota(jnp.int32, s.shape, 1)
    s = jnp.where((i == 0) & (col < BLOCK), NEG_INF, s)
    sink = sinks_ref[0, hq]
    m = jnp.maximum(jnp.max(s, axis=-1, keepdims=True), sink)
    p = jnp.exp(s - m)
    e_sink = jnp.exp(sink - m)
    inv = 1.0 / (jnp.sum(p, axis=-1, keepdims=True) + e_sink)
    return p * inv, e_sink * inv


def _head_slot(hq):
    return 4 * (hq // GQA_GROUP) + 2 * (hq % 2) + (hq % GQA_GROUP) // 2


def _mixer_in_specs(cur, prv):
    return [pl.BlockSpec((BLOCK, 512), lambda i: (cur(i), 0)),
            pl.BlockSpec((HALO, 512), lambda i: (jnp.maximum(cur(i) * (BLOCK // HALO) - 1, 0), 0)),
            pl.BlockSpec((BLOCK, 512), lambda i: (cur(i), 1)),
            pl.BlockSpec((BLOCK, 256), lambda i: (cur(i), 4)),
            pl.BlockSpec((BLOCK, 256), lambda i: (prv(i), 4))]


def _mixer_param_specs():
    return [pl.BlockSpec((4, POOL_GROUP_DIM, POOL_GROUP_DIM), lambda i: (0, 0, 0)),
            pl.BlockSpec((1, POOL_WIDTH), lambda i: (0, 0))]


def mixers_fwd(proj, bucket, in_window, rel_bias, sinks, w_pool, pool_scale, shards):
    t = proj.shape[0]
    nb = t // BLOCK
    ns = len(shards)

    def body(*refs):
        uc_ref, halo_ref, q_ref, kvc_ref, kvp_ref, bk_ref, win_ref, rb_ref, sinks_ref, wp_ref, sc_ref = refs[:11]
        shard_refs = refs[11:11 + ns]
        out_ref, pooled_ref, p_all, psink_ref = refs[11 + ns:15 + ns]
        gathered_refs = refs[15 + ns:15 + 2 * ns]
        ubuf, s_all, biasm_ref, send_sems, recv_sems, local_sems = refs[15 + 2 * ns:21 + 2 * ns]
        i = pl.program_id(0)
        start, finish = _gather_plan(shard_refs, gathered_refs, send_sems, recv_sems, local_sems, refs[21 + 2 * ns:])

        @pl.when(i == 0)
        def _():
            start()
            _fill_bias_band(bk_ref, win_ref, rb_ref, biasm_ref)

        _fill_pool_input(i, ubuf, uc_ref, halo_ref)
        for g, w in enumerate(POOL_WINDOWS):
            cols = slice(g * POOL_GROUP_DIM, (g + 1) * POOL_GROUP_DIM)
            pooled = _pooled(i, g, w, ubuf).astype(BF16)
            pooled_ref[:, cols] = pooled
            out_ref[:, cols] = (_dot(pooled, wp_ref[g]) * sc_ref[:, cols]).astype(BF16)
        kv = jnp.concatenate([kvp_ref[...], kvc_ref[...]], axis=0)
        k_var = _head_variants(kv[:, 0:2 * HEAD_DIM])
        v_var = _head_variants(kv[:, 2 * HEAD_DIM:])
        for hq in range(N_Q_HEADS):
            j, half, h = hq // 2, hq % 2, hq // GQA_GROUP
            q2 = q_ref[:, 2 * HEAD_DIM * j:2 * HEAD_DIM * (j + 1)].astype(BF16)
            s_all[hq] = _dot_nt(q2, k_var[h][half])
        psink_ref[...] = jnp.zeros_like(psink_ref)
        for hq in range(N_Q_HEADS):
            for r in range(0, BLOCK, ROW_CHUNK):
                rows = slice(r, r + ROW_CHUNK)
                probs, p_sink = _head_probs(i, hq, rows, s_all, biasm_ref, sinks_ref)
                p_all[_head_slot(hq), rows, :] = probs.astype(BF16)
                psink_ref[rows, hq:hq + 1] = p_sink
        for j in range(N_Q_HEADS // 2):
            h = 2 * j // GQA_GROUP
            acc = _dot(p_all[_head_slot(2 * j)], v_var[h][0]) + _dot(p_all[_head_slot(2 * j + 1)], v_var[h][1])
            out_ref[:, POOL_WIDTH + 2 * HEAD_DIM * j:POOL_WIDTH + 2 * HEAD_DIM * (j + 1)] = acc.astype(BF16)

        pl.when(i == max(nb - 4, 0))(finish.forward)
        pl.when(i == nb - 1)(finish.complete)

    return pl.pallas_call(
        body, name="mixers_fwd", grid=(nb,),
        out_shape=[jax.ShapeDtypeStruct((t, 2 * POOL_WIDTH), BF16), jax.ShapeDtypeStruct((t, POOL_WIDTH), BF16),
                   jax.ShapeDtypeStruct((N_Q_HEADS, t, 2 * BLOCK), BF16), jax.ShapeDtypeStruct((t, 128), F32)]
        + [jax.ShapeDtypeStruct((N_DEV, *sh.shape), sh.dtype) for sh in shards],
        in_specs=_mixer_in_specs(lambda i: i, lambda i: jnp.maximum(i - 1, 0))
        + [pl.BlockSpec((BLOCK, 2 * BLOCK), lambda i: (0, 0))] * 2 + [SMEM, SMEM] + _mixer_param_specs() + [ANY] * ns,
        out_specs=[pl.BlockSpec((BLOCK, 2 * POOL_WIDTH), lambda i: (i, 0)), pl.BlockSpec((BLOCK, POOL_WIDTH), lambda i: (i, 0)),
                   pl.BlockSpec((N_Q_HEADS, BLOCK, 2 * BLOCK), lambda i: (0, i, 0)), pl.BlockSpec((BLOCK, 128), lambda i: (i, 0))]
        + [ANY] * ns,
        scratch_shapes=[pltpu.VMEM((HALO + BLOCK, POOL_WIDTH), F32), pltpu.VMEM((N_Q_HEADS, BLOCK, 2 * BLOCK), F32),
                        pltpu.VMEM((N_Q_HEADS, BLOCK, 2 * BLOCK), F32)]
        + _gather_scratch(shards),
        compiler_params=_params(collective_id=GATHER_COLLECTIVE_ID),
    )(proj, proj, proj, proj, proj, bucket, in_window, rel_bias, sinks, w_pool, pool_scale, *shards)


def outproj_norm(cat, w, x, g, g_next, shard, partial):
    t, d = x.shape
    tm = TOKEN_TILE
    last = t // tm - 1
    rows = [(partial.shape[1] - shard.shape[0], shard.shape[0])]

    def body(c_ref, w_ref, x_ref, g_ref, gn_ref, shard_ref, partial_ref, mix_ref, x1_ref, h2_ref, gathered_ref,
             send_sems, recv_sems, local_sems, bounce):
        i = pl.program_id(0)
        start, finish = _gather_plan([shard_ref], [gathered_ref], send_sems, recv_sems, local_sems, [bounce], rows)
        pl.when(i == 0)(start)
        mix = _dot(c_ref[...], w_ref[...])
        mix_ref[...] = mix
        x1 = x_ref[...] + (mix * _rstd(mix)) * g_ref[...]
        x1_ref[...] = x1
        h2_ref[...] = ((x1 * _rstd(x1)) * gn_ref[...]).astype(BF16)
        pl.when(i == max(last - 1, 0))(finish.forward)
        pl.when(i == last)(finish.complete)

    row = pl.BlockSpec((tm, d), lambda i: (i, 0))
    gain = pl.BlockSpec((1, d), lambda i: (0, 0))
    return pl.pallas_call(
        body, name="outproj_norm", grid=(t // tm,),
        out_shape=[jax.ShapeDtypeStruct((t, d), F32), jax.ShapeDtypeStruct((t, d), F32), jax.ShapeDtypeStruct((t, d), BF16),
                   jax.ShapeDtypeStruct(partial.shape, partial.dtype)],
        in_specs=[pl.BlockSpec((tm, cat.shape[1]), lambda i: (i, 0)), pl.BlockSpec(w.shape, lambda i: (0, 0)), row, gain, gain,
                  ANY, ANY],
        out_specs=[row, row, row, ANY],
        input_output_aliases={6: 3},
        scratch_shapes=_gather_scratch([shard]),
        compiler_params=_params(collective_id=GATHER_COLLECTIVE_ID),
    )(cat, w, x, g, g_next, shard, partial)


def ffn_up(h, gate_t, up_t, down_shard):
    t, d = h.shape
    n = gate_t.shape[1]
    f = N_DEV * n
    tm, ts = FFN_TOKEN_TILE, FF_SHARDS_PER_TILE
    tn = ts * n
    steps = (f // tn, t // tm)

    def body(h_ref, wg_ref, wu_ref, shard_ref, gate_ref, up_ref, a_ref, gathered_ref,
             send_sems, recv_sems, local_sems, bounce):
        j, i = pl.program_id(0), pl.program_id(1)
        start, finish = _gather_plan([shard_ref], [gathered_ref], send_sems, recv_sems, local_sems, [bounce])
        pl.when((i == 0) & (j == 0))(start)

        hv = h_ref[...]
        gate = _dot_nt(hv, _merge_rows(wg_ref[...]))
        up = _dot_nt(hv, _merge_rows(wu_ref[...]))
        gate_ref[...] = gate.astype(BF16)
        up_ref[...] = up.astype(BF16)
        a_ref[...] = (gate * (1.0 / (1.0 + jnp.exp(-gate))) * up).astype(BF16)

        pl.when((j == steps[0] - 1) & (i == max(steps[1] - 2, 0)))(finish.forward)
        pl.when((j == steps[0] - 1) & (i == steps[1] - 1))(finish.complete)

    wide = pl.BlockSpec((tm, tn), lambda j, i: (i, j))
    return pl.pallas_call(
        body, name="ffn_up", grid=steps,
        out_shape=[jax.ShapeDtypeStruct((t, f), BF16)] * 3
        + [jax.ShapeDtypeStruct((N_DEV, *down_shard.shape), down_shard.dtype)],
        in_specs=[pl.BlockSpec((tm, d), lambda j, i: (i, 0)),
                  pl.BlockSpec((ts, n, d), lambda j, i: (j, 0, 0)),
                  pl.BlockSpec((ts, n, d), lambda j, i: (j, 0, 0)), ANY],
        out_specs=[wide, wide, wide, ANY],
        scratch_shapes=_gather_scratch([down_shard]),
        compiler_params=_params(collective_id=GATHER_COLLECTIVE_ID),
    )(h, gate_t, up_t, down_shard)


def ffn_down_loss(a, w_down, x1, g, target):
    t, d = x1.shape
    tm = WIDE_K_TOKEN_TILE

    def body(a_ref, w_ref, x_ref, g_ref, t_ref, df_ref, dy_ref, dg_ref, loss_ref):
        @pl.when(pl.program_id(0) == 0)
        def _():
            dg_ref[...] = jnp.zeros_like(dg_ref)
            loss_ref[...] = jnp.zeros_like(loss_ref)

        f = _dot(a_ref[...], _merge_rows(w_ref[...]))
        r = _rstd(f)
        g = g_ref[...]
        err = x_ref[...] + (f * r) * g - t_ref[...]
        loss_ref[...] += 0.5 * jnp.sum(jnp.mean(err * err, axis=-1, keepdims=True))
        dy = err * (1.0 / d)
        dy_ref[...] = dy
        df, dg_rows = _norm_bwd(dy, f, r, g)
        df_ref[...] = df.astype(BF16)
        dg_ref[...] += _as_rows(jnp.sum(dg_rows, axis=0, keepdims=True))

    row = pl.BlockSpec((tm, d), lambda i: (i, 0))
    gain = pl.BlockSpec((1, d), lambda i: (0, 0))
    return pl.pallas_call(
        body, name="ffn_down_loss", grid=(t // tm,),
        out_shape=[jax.ShapeDtypeStruct((t, d), BF16), jax.ShapeDtypeStruct((t, d), F32),
                   jax.ShapeDtypeStruct((d // 128, 128), F32), jax.ShapeDtypeStruct((1, 128), F32)],
        in_specs=[pl.BlockSpec((tm, a.shape[1]), lambda i: (i, 0)), pl.BlockSpec(w_down.shape, lambda i: (0, 0, 0)), row, gain, row],
        out_specs=[row, row, pl.BlockSpec((d // 128, 128), lambda i: (0, 0)), pl.BlockSpec((1, 128), lambda i: (0, 0))],
        compiler_params=_params(),
    )(a, w_down, x1, g, target)


def ffn_down_bwd(df, w_down, gate, up, h, chip_parts):
    t, d = df.shape
    n = w_down.shape[1]
    f = gate.shape[1]
    na = len(chip_parts)
    tm, ts = WIDE_K_TOKEN_TILE, FF_SHARDS_PER_TILE
    tn = ts * n
    steps = (f // tn, t // tm)

    def body(*refs):
        df_ref, w_ref, gate_ref, up_ref, h_ref = refs[:5]
        part_refs = refs[5:5 + na]
        dgate_ref, dup_ref, dwg_ref, dwu_ref = refs[5 + na:9 + na]
        slot_refs = refs[9 + na:9 + 2 * na]
        acc, send_sems, recv_sems, local_sems = refs[9 + 2 * na:13 + 2 * na]
        j, i = pl.program_id(0), pl.program_id(1)
        start, finish = _chip_exchange_plan(part_refs, slot_refs, send_sems, recv_sems, local_sems, refs[13 + 2 * na:])
        pl.when((j == 0) & (i == 0))(start)

        @pl.when(i == 0)
        def _():
            acc[...] = jnp.zeros_like(acc)

        da = _dot_nt(df_ref[...], _merge_rows(w_ref[...]))
        gate = gate_ref[...].astype(F32)
        sig = 1.0 / (1.0 + jnp.exp(-gate))
        dgate = (da * up_ref[...].astype(F32) * (sig * (1.0 + gate * (1.0 - sig)))).astype(BF16)
        dup = (da * (gate * sig)).astype(BF16)
        dgate_ref[...] = dgate
        dup_ref[...] = dup
        acc[0] += _dot_tn(dgate, h_ref[...])
        acc[1] += _dot_tn(dup, h_ref[...])

        @pl.when(i == steps[1] - 1)
        def _():
            for w, out_ref in enumerate((dwg_ref, dwu_ref)):
                blocks = acc[w].reshape(ts // 2, 2, n, d)
                for chip in range(ts // 2):
                    for core in range(2):
                        out_ref[core, chip] = blocks[chip, core].astype(BF16)

        pl.when((j == steps[0] - 1) & (i == steps[1] - 1))(finish)

    wide = pl.BlockSpec((tm, tn), lambda j, i: (i, j))
    rows = pl.BlockSpec((tm, d), lambda j, i: (i, 0))
    owned = pl.BlockSpec((2, ts // 2, n, d), lambda j, i: (0, j, 0, 0))
    out = pl.pallas_call(
        body, name="ffn_down_bwd", grid=steps,
        out_shape=[jax.ShapeDtypeStruct((t, f), BF16)] * 2 + [jax.ShapeDtypeStruct((2, N_CHIP, n, d), BF16)] * 2
        + [jax.ShapeDtypeStruct(p.shape, p.dtype) for p in chip_parts],
        in_specs=[rows, pl.BlockSpec((ts, n, d), lambda j, i: (j, 0, 0)), wide, wide, rows] + [ANY] * na,
        out_specs=[wide, wide, owned, owned] + [ANY] * na,
        scratch_shapes=[pltpu.VMEM((2, tn, d), F32)] + _chip_exchange_scratch(chip_parts),
        compiler_params=_params(collective_id=CHIP_COLLECTIVE_ID),
    )(df, w_down, gate, up, h, *chip_parts)
    return out[:2], out[2:4], out[4:]


def grad_ffn(lhs, b, name, pair_parts=()):
    t, f = lhs[0].shape
    d = b.shape[1]
    nw = len(lhs)
    na = len(pair_parts)
    n = f // N_DEV
    tt, ts = TOKEN_TILE, FF_SHARDS_PER_TILE
    tn = ts * n
    steps = (f // tn, t // tt)

    def body(*refs):
        a_refs, b_ref, part_refs = refs[:nw], refs[nw], refs[nw + 1:nw + 1 + na]
        out_refs = refs[nw + 1 + na:2 * nw + 1 + na]
        got_refs = refs[2 * nw + 1 + na:2 * nw + 1 + 2 * na]
        acc = refs[2 * nw + 1 + 2 * na]
        i, k = pl.program_id(0), pl.program_id(1)
        if na:
            start, finish = _pair_plan(part_refs, got_refs, *refs[2 * nw + 2 + 2 * na:])
            pl.when((i == 0) & (k == 0))(start)

        @pl.when(k == 0)
        def _():
            acc[...] = jnp.zeros_like(acc)

        for w in range(nw):
            acc[w] += _dot_tn(a_refs[w][...], b_ref[...])

        @pl.when(k == steps[1] - 1)
        def _():
            for w in range(nw):
                blocks = acc[w].reshape(ts // 2, 2, n, d)
                for chip in range(ts // 2):
                    for core in range(2):
                        out_refs[w][core, chip] = blocks[chip, core].astype(BF16)

        if na:
            pl.when((i == steps[0] - 1) & (k == steps[1] - 1))(finish)

    out = pl.pallas_call(
        body, name=name, grid=steps,
        out_shape=[jax.ShapeDtypeStruct((2, N_CHIP, n, d), BF16)] * nw
        + [jax.ShapeDtypeStruct(p.shape[1:], p.dtype) for p in pair_parts],
        in_specs=[pl.BlockSpec((tt, tn), lambda i, k: (k, i))] * nw + [pl.BlockSpec((tt, d), lambda i, k: (k, 0))] + [ANY] * na,
        out_specs=[pl.BlockSpec((2, ts // 2, n, d), lambda i, k: (0, i, 0, 0))] * nw + [ANY] * na,
        scratch_shapes=[pltpu.VMEM((nw, tn, d), F32)]
        + ([pltpu.SemaphoreType.DMA((na,)), pltpu.SemaphoreType.DMA((na,))] if na else []),
        compiler_params=_params(collective_id=PAIR_COLLECTIVE_ID) if na else _params(),
    )(*lhs, b, *pair_parts)
    return out[:nw], out[nw:]


def ffn_up_bwd(dgate, dup, gate_t, up_t, x1, g_ffn, dy, mix, g_mix, chip_parts):
    t, d = x1.shape
    n = gate_t.shape[1]
    f = N_DEV * n
    tm = WIDE_K_TOKEN_TILE
    na = len(chip_parts)
    last = t // tm - 1

    def body(*refs):
        dg_ref, du_ref, wg_ref, wu_ref, x_ref, gf_ref, dy_ref, mix_ref, gm_ref = refs[:9]
        part_refs = refs[9:9 + na]
        dx1_ref, dmix_ref, dgf_ref, dgm_ref = refs[9 + na:13 + na]
        slot_refs = refs[13 + na:13 + 2 * na]
        send_sems, recv_sems, local_sems = refs[13 + 2 * na:16 + 2 * na]
        i = pl.program_id(0)
        start, finish = _chip_exchange_plan(part_refs, slot_refs, send_sems, recv_sems, local_sems, refs[16 + 2 * na:])

        @pl.when(i == 0)
        def _():
            start()
            dgf_ref[...] = jnp.zeros_like(dgf_ref)
            dgm_ref[...] = jnp.zeros_like(dgm_ref)

        dh = _dot(dg_ref[...], _merge_rows(wg_ref[...])) + _dot(du_ref[...], _merge_rows(wu_ref[...]))
        x1 = x_ref[...]
        dx, dgf_rows = _norm_bwd(dh, x1, _rstd(x1), gf_ref[...])
        dx1 = dy_ref[...] + dx
        dx1_ref[...] = dx1
        dgf_ref[...] += _as_rows(jnp.sum(dgf_rows, axis=0, keepdims=True))
        mix = mix_ref[...]
        dmix, dgm_rows = _norm_bwd(dx1, mix, _rstd(mix), gm_ref[...])
        dmix_ref[...] = dmix.astype(BF16)
        dgm_ref[...] += _as_rows(jnp.sum(dgm_rows, axis=0, keepdims=True))
        pl.when(i == last)(finish)

    row = pl.BlockSpec((tm, d), lambda i: (i, 0))
    wide = pl.BlockSpec((tm, f), lambda i: (i, 0))
    gain = pl.BlockSpec((1, d), lambda i: (0, 0))
    gain_rows = pl.BlockSpec((d // 128, 128), lambda i: (0, 0))
    whole = pl.BlockSpec((N_DEV, n, d), lambda i: (0, 0, 0), pipeline_mode=pl.Buffered(1))
    out = pl.pallas_call(
        body, name="ffn_up_bwd", grid=(t // tm,),
        out_shape=[jax.ShapeDtypeStruct((t, d), F32), jax.ShapeDtypeStruct((t, d), BF16),
                   jax.ShapeDtypeStruct((d // 128, 128), F32), jax.ShapeDtypeStruct((d // 128, 128), F32)]
        + [jax.ShapeDtypeStruct(p.shape, p.dtype) for p in chip_parts],
        in_specs=[wide, wide, whole, whole, row, gain, row, row, gain] + [ANY] * na,
        out_specs=[row, row, gain_rows, gain_rows] + [ANY] * na,
        scratch_shapes=_chip_exchange_scratch(chip_parts),
        compiler_params=_params(collective_id=CHIP_COLLECTIVE_ID),
    )(dgate, dup, gate_t, up_t, x1, g_ffn, dy, mix, g_mix, *chip_parts)
    return out[:4], out[4:]


def outproj_bwd(dmix, w_out, cat):
    t, d = dmix.shape
    m = w_out.shape[0]
    r = m // N_DEV
    tm = TOKEN_TILE
    last = t // tm - 1

    def body(dm_ref, w_ref, cat_ref, dcat_ref, dw_ref, acc):
        k = pl.program_id(0)

        @pl.when(k == 0)
        def _():
            acc[...] = jnp.zeros_like(acc)

        dcat_ref[...] = _dot_nt(dm_ref[...], w_ref[...])
        acc[...] += _dot_tn(cat_ref[...], dm_ref[...])

        @pl.when(k == last)
        def _():
            blocks = acc[...].reshape(N_CHIP, 2, r, d)
            for chip in range(N_CHIP):
                for core in range(2):
                    dw_ref[core, chip] = blocks[chip, core].astype(BF16)

    tile = lambda width: pl.BlockSpec((tm, width), lambda k: (k, 0))
    return pl.pallas_call(
        body, name="outproj_bwd", grid=(t // tm,),
        out_shape=[jax.ShapeDtypeStruct((t, m), F32), jax.ShapeDtypeStruct((2, N_CHIP, r, d), BF16)],
        in_specs=[tile(d), pl.BlockSpec(w_out.shape, lambda k: (0, 0)), tile(m)],
        out_specs=[tile(m), pl.BlockSpec((2, N_CHIP, r, d), lambda k: (0, 0, 0, 0))],
        scratch_shapes=[pltpu.VMEM((m, d), F32)],
        compiler_params=_params(),
    )(dmix, w_out, cat)


def mixers_bwd(proj, dcat, pooled, probs, p_sinks, w_pool, pool_scale, bucket, ffn_parts):
    t = proj.shape[0]
    nb = t // BLOCK
    na = len(ffn_parts)

    def body(*refs):
        (q_ref, kvc_ref, kvp_ref, dcat_ref, pooled_ref, p_all, psink_ref, wp_ref, sc_ref, bk_ref) = refs[:10]
        part_refs = refs[10:10 + na]
        dproj_ref, dbias_ref, dsink_ref, dwp_ref, dsc_ref, drb_ref = refs[10 + na:16 + na]
        slot_refs = refs[16 + na:16 + 2 * na]
        dbuf, c_u, c_q, c_kv, dp_all, ds_all, sink_acc = refs[16 + 2 * na:23 + 2 * na]
        send_sems, recv_sems, local_sems = refs[23 + 2 * na:26 + 2 * na]
        bounce = refs[26 + 2 * na:]
        i = pl.program_id(0)
        lane = lax.broadcasted_iota(jnp.int32, (1, 128), 1)
        start, finish = _chip_exchange_plan(part_refs, slot_refs, send_sems, recv_sems, local_sems, bounce)

        @pl.when(i == 0)
        def _():
            start()
            dbias_ref[...] = jnp.zeros_like(dbias_ref)
            dwp_ref[...] = jnp.zeros_like(dwp_ref)
            dsc_ref[...] = jnp.zeros_like(dsc_ref)
            dsink_ref[...] = jnp.zeros_like(dsink_ref)
            dbuf[...] = jnp.zeros_like(dbuf)
            c_u[...] = jnp.zeros_like(c_u)
            c_q[...] = jnp.zeros_like(c_q)
            c_kv[...] = jnp.zeros_like(c_kv)

        @pl.when(i < nb)
        def _():
            for g, w in enumerate(POOL_WINDOWS):
                cols = slice(g * POOL_GROUP_DIM, (g + 1) * POOL_GROUP_DIM)
                pooled = pooled_ref[:, cols]
                mixed = _dot(pooled, wp_ref[g])
                dout = dcat_ref[:, cols]
                dsc_ref[g:g + 1, :] += jnp.sum(dout * mixed, axis=0, keepdims=True)
                dmixed = (dout * sc_ref[:, cols]).astype(BF16)
                dwp_ref[g] += _dot_tn(pooled, dmixed)
                dpooled = _dot_nt(dmixed, wp_ref[g])
                scaled = dpooled * _inv_count(i, w)
                dbuf[BLOCK:, cols] = scaled[0:HALO]
                dproj_ref[:, cols] = (_window_sum(dbuf, g, w, lambda k: k) + c_u[:, cols]).astype(BF16)
                dbuf[0:BLOCK, cols] = scaled
                c_u[:, cols] = -dpooled

            kv = jnp.concatenate([kvp_ref[...], kvc_ref[...]], axis=0)
            k_var = _head_variants(kv[:, 0:2 * HEAD_DIM])
            v_var = _head_variants(kv[:, 2 * HEAD_DIM:])
            q2s = [q_ref[:, 2 * HEAD_DIM * j:2 * HEAD_DIM * (j + 1)].astype(BF16) for j in range(N_Q_HEADS // 2)]
            do2s = [dcat_ref[:, POOL_WIDTH + 2 * HEAD_DIM * j:POOL_WIDTH + 2 * HEAD_DIM * (j + 1)].astype(BF16)
                    for j in range(N_Q_HEADS // 2)]
            slot = _head_slot
            for hq in range(N_Q_HEADS):
                j, half, h = hq // 2, hq % 2, hq // GQA_GROUP
                dp_all[hq] = _dot_nt(do2s[j], v_var[h][half])
            sink_acc[...] = jnp.zeros_like(sink_acc)
            for hq in range(N_Q_HEADS):
                for r in range(0, BLOCK, ROW_CHUNK):
                    rows = slice(r, r + ROW_CHUNK)
                    probs = p_all[slot(hq), rows, :].astype(F32)
                    dp = dp_all[hq, rows, :]
                    delta = jnp.sum(probs * dp, axis=-1, keepdims=True)
                    ds = probs * (dp - delta)
                    dbias_ref[hq, rows, :] += ds
                    sink_acc[rows, :] += jnp.where(lane == hq, psink_ref[rows, :], 0.0) * delta
                    ds_all[slot(hq), rows, :] = (ds * ATTN_SCALE).astype(BF16)
            dsink_ref[...] -= jnp.sum(sink_acc[...], axis=0, keepdims=True)
            dq2 = [None] * (N_Q_HEADS // 2)
            for hq in range(N_Q_HEADS):
                j, half, h = hq // 2, hq % 2, hq // GQA_GROUP
                dq = _dot(ds_all[slot(hq)], k_var[h][half])
                dq2[j] = dq if dq2[j] is None else dq2[j] + dq
            low = lax.broadcasted_iota(jnp.int32, (2 * BLOCK, 2 * HEAD_DIM), 1) < HEAD_DIM
            dk_half, dv_half = [[None, None], [None, None]], [[None, None], [None, None]]
            for h in range(N_KV_HEADS):
                for half in range(2):
                    heads = [hq for hq in range(GQA_GROUP * h, GQA_GROUP * (h + 1)) if hq % 2 == half]
                    base = slot(heads[0])
                    q_rows = jnp.concatenate([q2s[hq // 2] for hq in heads], axis=0)
                    do_rows = jnp.concatenate([do2s[hq // 2] for hq in heads], axis=0)
                    dk_half[h][half] = _dot_tn(_merge_rows(ds_all[base:base + 2]), q_rows)
                    dv_half[h][half] = _dot_tn(_merge_rows(p_all[base:base + 2]), do_rows)

            def pair_of(halves):
                return jnp.where(low, halves[0][0] + pltpu.roll(halves[0][1], HEAD_DIM, 1),
                                 halves[1][1] + pltpu.roll(halves[1][0], HEAD_DIM, 1))

            dkv = jnp.concatenate([pair_of(dk_half), pair_of(dv_half)], axis=1)
            dproj_ref[:, POOL_WIDTH:2 * POOL_WIDTH] = c_q[...].astype(BF16)
            dproj_ref[:, 2 * POOL_WIDTH:] = (c_kv[...] + dkv[0:BLOCK]).astype(BF16)
            c_q[...] = jnp.concatenate(dq2, axis=1)
            c_kv[...] = dkv[BLOCK:]

        @pl.when(i == nb)
        def _():
            dbuf[BLOCK:, :] = jnp.zeros((HALO, POOL_WIDTH), F32)
            for g, w in enumerate(POOL_WINDOWS):
                cols = slice(g * POOL_GROUP_DIM, (g + 1) * POOL_GROUP_DIM)
                dproj_ref[:, cols] = (_window_sum(dbuf, g, w, lambda k: k) + c_u[:, cols]).astype(BF16)
            dproj_ref[:, POOL_WIDTH:2 * POOL_WIDTH] = c_q[...].astype(BF16)
            dproj_ref[:, 2 * POOL_WIDTH:] = c_kv[...].astype(BF16)
            bk = bk_ref[...]
            for h in range(N_Q_HEADS):
                db = dbias_ref[h]
                for b in range(N_BUCKETS):
                    drb_ref[h, b] = jnp.sum(jnp.where(bk == float(b), db, 0.0))
            finish()

    cur = lambda i: jnp.minimum(i, nb - 1)
    prv = lambda i: jnp.maximum(jnp.minimum(i, nb - 1) - 1, 0)
    out = pl.pallas_call(
        body, name="mixers_bwd", grid=(nb + 1,),
        out_shape=[jax.ShapeDtypeStruct((t, proj.shape[1]), BF16),
                   jax.ShapeDtypeStruct((N_Q_HEADS, BLOCK, 2 * BLOCK), F32),
                   jax.ShapeDtypeStruct((1, 128), F32),
                   jax.ShapeDtypeStruct((4, POOL_GROUP_DIM, POOL_GROUP_DIM), F32),
                   jax.ShapeDtypeStruct((len(POOL_WINDOWS), POOL_GROUP_DIM), F32),
                   jax.ShapeDtypeStruct((N_Q_HEADS, N_BUCKETS), F32)]
        + [jax.ShapeDtypeStruct(p.shape, p.dtype) for p in ffn_parts],
        in_specs=_mixer_in_specs(cur, prv)[2:]
        + [pl.BlockSpec((BLOCK, 2 * POOL_WIDTH), lambda i: (cur(i), 0)), pl.BlockSpec((BLOCK, POOL_WIDTH), lambda i: (cur(i), 0)),
           pl.BlockSpec((N_Q_HEADS, BLOCK, 2 * BLOCK), lambda i: (0, cur(i), 0)), pl.BlockSpec((BLOCK, 128), lambda i: (cur(i), 0))]
        + _mixer_param_specs() + [pl.BlockSpec((BLOCK, 2 * BLOCK), lambda i: (0, 0))] + [ANY] * na,
        out_specs=[pl.BlockSpec((BLOCK, proj.shape[1]), lambda i: (jnp.maximum(i - 1, 0), 0)),
                   pl.BlockSpec((N_Q_HEADS, BLOCK, 2 * BLOCK), lambda i: (0, 0, 0)),
                   pl.BlockSpec((1, 128), lambda i: (0, 0)),
                   pl.BlockSpec((4, POOL_GROUP_DIM, POOL_GROUP_DIM), lambda i: (0, 0, 0)),
                   pl.BlockSpec((len(POOL_WINDOWS), POOL_GROUP_DIM), lambda i: (0, 0)), SMEM] + [ANY] * na,
        scratch_shapes=[pltpu.VMEM((BLOCK + HALO, POOL_WIDTH), F32),
                        pltpu.VMEM((BLOCK, POOL_WIDTH), F32), pltpu.VMEM((BLOCK, POOL_WIDTH), F32),
                        pltpu.VMEM((BLOCK, 256), F32),
                        pltpu.VMEM((N_Q_HEADS, BLOCK, 2 * BLOCK), F32), pltpu.VMEM((N_Q_HEADS, BLOCK, 2 * BLOCK), BF16),
                        pltpu.VMEM((BLOCK, 128), F32)]
        + _chip_exchange_scratch(ffn_parts),
        compiler_params=_params(collective_id=CHIP_COLLECTIVE_ID),
    )(proj, proj, proj, dcat, pooled, probs, p_sinks, w_pool, pool_scale, bucket, *ffn_parts)
    return out[:6], out[6:]


def inproj_bwd(dproj, w_in_t, x, g, dx1):
    t, d = x.shape
    n = dproj.shape[1]
    tm = TOKEN_TILE

    def body(dp_ref, w_ref, x_ref, g_ref, dx1_ref, dx_ref, dg_ref):
        @pl.when(pl.program_id(0) == 0)
        def _():
            dg_ref[...] = jnp.zeros_like(dg_ref)

        dh = _dot(dp_ref[...], w_ref[...])
        xv = x_ref[...]
        dx, dg_rows = _norm_bwd(dh, xv, _rstd(xv), g_ref[...])
        dx_ref[...] = dx1_ref[...] + dx
        dg_ref[...] += _as_rows(jnp.sum(dg_rows, axis=0, keepdims=True))

    row = pl.BlockSpec((tm, d), lambda i: (i, 0))
    gain = pl.BlockSpec((1, d), lambda i: (0, 0))
    return pl.pallas_call(
        body, name="inproj_bwd", grid=(t // tm,),
        out_shape=[jax.ShapeDtypeStruct((t, d), F32), jax.ShapeDtypeStruct((d // 128, 128), F32)],
        in_specs=[pl.BlockSpec((tm, n), lambda i: (i, 0)), pl.BlockSpec(w_in_t.shape, lambda i: (0, 0)), row, gain, row],
        out_specs=[row, pl.BlockSpec((d // 128, 128), lambda i: (0, 0))],
        compiler_params=_params(),
    )(dproj, w_in_t, x, g, dx1)


def _bucket_band():
    qi = jnp.arange(BLOCK)[:, None]
    kj = jnp.arange(2 * BLOCK)[None, :]
    dist = qi + BLOCK - kj
    n = jnp.maximum(dist, 0)
    nf = jnp.maximum(n, 1).astype(F32)
    large = MAX_EXACT + (jnp.log(nf / MAX_EXACT) / np.float32(np.log(MAX_DISTANCE / MAX_EXACT))
                         * (N_BUCKETS - MAX_EXACT)).astype(jnp.int32)
    large = jnp.minimum(large, N_BUCKETS - 1)
    bucket = jnp.where(n < MAX_EXACT, n, large)
    in_window = (dist >= 0) & (dist < BLOCK)
    return bucket.astype(F32), in_window.astype(F32)


def kernel(x, g_pre_mix, w_in, w_pool, pool_scale, rel_bias, sinks, w_out, g_post_mix, g_pre_ffn, w_gate, w_up, w_down, g_post_ffn, loss_target, m_g_pre_mix, m_w_in, m_w_pool, m_pool_scale, m_rel_bias, m_sinks, m_w_out, m_g_post_mix, m_g_pre_ffn, m_w_gate, m_w_up, m_w_down, m_g_post_ffn, v_g_pre_mix, v_w_in, v_w_pool, v_pool_scale, v_rel_bias, v_sinks, v_w_out, v_g_post_mix, v_g_pre_ffn, v_w_gate, v_w_up, v_w_down, v_g_post_ffn):
    d = x.shape[-1]
    xs, target = x[0], loss_target[0]

    w_in_ts = w_in[0].T.astype(BF16)
    w_out_s = w_out[0].astype(BF16)
    gate_ts = w_gate[0].T.astype(BF16)
    up_ts = w_up[0].T.astype(BF16)
    w_down_s = w_down[0].astype(BF16)

    bucket, in_window = _bucket_band()
    w_pool_b = w_pool[0].astype(BF16)
    half = up_ts.shape[0] // 2
    proj, h1, w_in_t, up_t = norm_inproj(xs, g_pre_mix, w_in_ts, up_ts[:half], up_ts.shape[0])
    w_in_t = w_in_t.reshape(-1, d)
    cat, pooled, probs, p_sinks, gate_t, w_out_f = mixers_fwd(
        proj, bucket, in_window, rel_bias, sinks, w_pool_b, pool_scale, [gate_ts, w_out_s])
    w_out_f = w_out_f.reshape(-1, d)
    mix, x1, h2, up_t = outproj_norm(cat, w_out_f, xs, g_post_mix, g_pre_ffn, up_ts[half:], up_t)
    gate, up, act, w_down_f = ffn_up(h2, gate_t, up_t, w_down_s)
    df, dy, dg_post_ffn, loss_part = ffn_down_loss(act, w_down_f, x1, g_post_ffn, target)

    (d_down,), _ = grad_ffn([act], df, "grad_w_down")
    q_down, = pair_add_exchange([], [], [d_down], "pair_add_down")
    (dgate, dup), (d_gate, d_up), (down_slots,) = ffn_down_bwd(df, w_down_f, gate, up, h2, [q_down])
    q_gate, q_up = pair_add_exchange([], [], [d_gate, d_up], "pair_add_gate_up")
    (dx1, dmix, dg_pre_ffn, dg_post_mix), (gate_slots,) = ffn_up_bwd(
        dgate, dup, gate_t, up_t, x1, g_pre_ffn, dy, mix, g_post_mix, [q_gate])
    dcat, d_out = outproj_bwd(dmix, w_out_f, cat)
    q_out, = pair_add_exchange([], [], [d_out], "pair_add_out")
    (dproj, _, dsinks, dw_pool, dpool_scale, drel_bias), (up_slots, out_slots) = mixers_bwd(
        proj, dcat, pooled, probs, p_sinks, w_pool_b, pool_scale, bucket, [q_up, q_out])
    grad_x, dg_pre_mix = inproj_bwd(dproj, w_in_t, xs, g_pre_mix, dx1)

    small_w = [g_pre_mix, g_post_mix, g_pre_ffn, g_post_ffn, pool_scale, sinks, w_pool, rel_bias.T]
    small_m = [m_g_pre_mix, m_g_post_mix, m_g_pre_ffn, m_g_post_ffn, m_pool_scale, m_sinks, m_w_pool, m_rel_bias.T]
    small_v = [v_g_pre_mix, v_g_post_mix, v_g_pre_ffn, v_g_post_ffn, v_pool_scale, v_sinks, v_w_pool, v_rel_bias.T]
    d_in_t, total, total_rb = grad_w_in_small_reduce(
        dproj, h1, [dg_pre_mix, dg_post_mix, dg_pre_ffn, dg_post_ffn], dpool_scale, dsinks, loss_part, dw_pool, drel_bias)
    g_in_t = reduce_w_in(d_in_t)
    loss_row, sm = small_adamw(total, total_rb, small_w, small_m, small_v)
    sm[7] = [r.T for r in sm[7]]
    big_w = [w_in[0].T, w_out[0], w_gate[0].T, w_up[0].T, w_down[0]]
    big_m = [m_w_in[0].T, m_w_out[0], m_w_gate[0].T, m_w_up[0].T, m_w_down[0]]
    big_v = [v_w_in[0].T, v_w_out[0], v_w_gate[0].T, v_w_up[0].T, v_w_down[0]]
    upd = sum_adamw([out_slots, gate_slots, up_slots, down_slots], big_w[1:], big_m[1:], big_v[1:], "sum_adamw")
    upd = [[g_in_t, *adamw_update(big_w[:1], [g_in_t], big_m[:1], big_v[:1], "adamw_in")[0]], *upd]
    back = lambda k, a: (a.T if k in (0, 2, 3) else a)[None]
    big = [[back(k, u) for u in upd[k]] for k in range(5)]

    def ordered(kind):
        s, b = [p[kind] for p in sm], [p[kind] for p in big]
        return [s[0], b[0], s[6], s[4], s[7], s[5], b[1], s[1], s[2], b[2], b[3], b[4], s[3]]

    return (loss_row[0, 0], grad_x[None], *ordered(0), *ordered(1), *ordered(2), *ordered(3))
```

```python
import numpy as np
import jax
import jax.numpy as jnp
from jax import lax
from jax.experimental import pallas as pl
from jax.experimental.pallas import tpu as pltpu

F32 = jnp.float32
BF16 = jnp.bfloat16

N_DEV = 8
N_CHIP = 4
POOL_WIDTH = 512
POOL_WINDOWS = (2, 4, 8, 16)
POOL_GROUP_DIM = 128
HEAD_DIM = 64
N_Q_HEADS = 8
N_KV_HEADS = 2
GQA_GROUP = 4
BLOCK = 128
HALO = 16
ROW_CHUNK = 32
N_BUCKETS = 32
MAX_EXACT = 16
MAX_DISTANCE = 128
EPS = 1e-6
NEG_INF = -1e30
ATTN_SCALE = float(1.0 / np.sqrt(np.float32(HEAD_DIM)))

ADAM_LR = 0.001
ADAM_B1 = 0.9
ADAM_B2 = 0.999
ADAM_EPS = 1e-08
ADAM_WD = 0.01
ADAM_STEP = 10

TOKEN_TILE = 1024
WIDE_K_TOKEN_TILE = 512
FFN_TOKEN_TILE = 1024
FF_SHARDS_PER_TILE = 4
VMEM_LIMIT = 56 * 1024 * 1024
MESH = pl.DeviceIdType.MESH
PAIR_COLLECTIVE_ID = 0
GATHER_COLLECTIVE_ID = 1
CHIP_COLLECTIVE_ID = 2
ANY = pl.BlockSpec(memory_space=pl.ANY)
VMEM = pl.BlockSpec(memory_space=pltpu.VMEM)
SMEM = pl.BlockSpec(memory_space=pltpu.SMEM)


def _params(**kw):
    return pltpu.CompilerParams(vmem_limit_bytes=VMEM_LIMIT, **kw)


def _dot(a, b):
    return jnp.dot(a, b, preferred_element_type=F32)


def _dot_nt(a, b):
    return lax.dot_general(a, b, (((1,), (1,)), ((), ())), preferred_element_type=F32)


def _dot_tn(a, b):
    return lax.dot_general(a, b, (((0,), (0,)), ((), ())), preferred_element_type=F32)


def _rstd(v):
    return lax.rsqrt(jnp.mean(v * v, axis=-1, keepdims=True) + EPS)


def _norm_bwd(dout, v, r, g):
    vn = v * r
    dn = dout * g
    dv = r * (dn - vn * jnp.mean(dn * vn, axis=-1, keepdims=True))
    return dv, dout * vn


def _as_rows(v):
    return jnp.concatenate([v[:, k:k + 128] for k in range(0, v.shape[1], 128)], axis=0)


def _as_lanes(rows):
    return jnp.concatenate([rows[k:k + 1, :] for k in range(rows.shape[0])], axis=1)


def _handshake(peers):
    barrier = pltpu.get_barrier_semaphore()
    for peer in peers:
        pl.semaphore_signal(barrier, inc=1, device_id=peer, device_id_type=MESH)
    pl.semaphore_wait(barrier, len(peers))


def _merge_rows(value):
    s, r, c_ = value.shape
    return value.reshape(s * r, c_)


def _gather_plan(srcs, outs, send_sems, recv_sems, local_sems=None, bounce=None, rows=None, shake=True):
    n = len(srcs)
    x, y, c = lax.axis_index("x"), lax.axis_index("y"), lax.axis_index("c")
    me, sibling = (x, y, c), (x, y, 1 - c)
    chips = [(1 - x, y), (x, 1 - y), (1 - x, 1 - y)]

    def slot(a, px, py, pc):
        whole = outs[a].at[4 * px + 2 * py + pc]
        return whole if rows is None or rows[a] is None else whole.at[pl.ds(*rows[a])]

    def copy(a, k, block, to, from_src=False):
        return pltpu.make_async_remote_copy(
            src_ref=srcs[a] if from_src else slot(a, *block), dst_ref=slot(a, *block),
            send_sem=send_sems.at[k * n + a], recv_sem=recv_sems.at[k * n + a], device_id=to, device_id_type=MESH)

    def own_in(a):
        return pltpu.make_async_copy(srcs[a], bounce[a], local_sems.at[a])

    def own_out(a):
        return pltpu.make_async_copy(bounce[a], slot(a, *me), local_sems.at[a])

    def first(a):
        return [copy(a, 0, me, sibling, True)] + [copy(a, 1 + j, me, (*chip, c), True) for j, chip in enumerate(chips)]

    def passed(a, j):
        return copy(a, 4 + j, (*chips[j], c), sibling)

    def start():
        if shake:
            _handshake([sibling] + [(*chip, c) for chip in chips])
        for a in range(n):
            if bounce is not None:
                own_in(a).start()
            for cp in first(a):
                cp.start()

    def forward():
        if bounce is not None:
            for a in range(n):
                own_in(a).wait()
                own_out(a).start()
        for j, chip in enumerate(chips):
            for a in range(n):
                copy(a, 1 + j, (*chip, c), me).wait_recv()
                passed(a, j).start()

    def complete():
        for a in range(n):
            copy(a, 0, sibling, me).wait_recv()
            for j, chip in enumerate(chips):
                copy(a, 4 + j, (*chip, 1 - c), me).wait_recv()
        for a in range(n):
            for cp in first(a) + [passed(a, j) for j in range(3)]:
                cp.wait_send()
            if bounce is not None:
                own_out(a).wait()

    def finish():
        forward()
        complete()

    finish.forward, finish.complete = forward, complete
    return start, finish


def _gather_scratch(shards):
    n = len(shards)
    return [pltpu.SemaphoreType.DMA((7 * n,)), pltpu.SemaphoreType.DMA((7 * n,)), pltpu.SemaphoreType.DMA((n,))] \
        + [pltpu.VMEM(s.shape, s.dtype) for s in shards]


def _chip_exchange_plan(srcs, outs, send_sems, recv_sems, local_sems, bounce):
    n = len(srcs)
    x, y, c = lax.axis_index("x"), lax.axis_index("y"), lax.axis_index("c")
    my_chip = 2 * x + y

    def copies():
        out = []
        for a in range(n):
            for k in range(1, N_CHIP):
                px, py = x ^ (k >> 1), y ^ (k & 1)
                out.append(pltpu.make_async_remote_copy(
                    src_ref=srcs[a].at[2 * px + py], dst_ref=outs[a].at[my_chip],
                    send_sem=send_sems.at[(k - 1) * n + a], recv_sem=recv_sems.at[(k - 1) * n + a],
                    device_id=(px, py, c), device_id_type=MESH))
        return out

    def own_in(a):
        return pltpu.make_async_copy(srcs[a].at[my_chip], bounce[a], local_sems.at[a])

    def own_out(a):
        return pltpu.make_async_copy(bounce[a], outs[a].at[my_chip], local_sems.at[a])

    def start():
        _handshake([(x ^ (k >> 1), y ^ (k & 1), c) for k in range(1, N_CHIP)])
        for a in range(n):
            own_in(a).start()
        for cp in copies():
            cp.start()

    def finish():
        for a in range(n):
            own_in(a).wait()
            own_out(a).start()
        for cp in copies():
            cp.wait()
        for a in range(n):
            own_out(a).wait()

    return start, finish


def _chip_exchange_scratch(parts):
    n = len(parts)
    return [pltpu.SemaphoreType.DMA((3 * n,)), pltpu.SemaphoreType.DMA((3 * n,)), pltpu.SemaphoreType.DMA((n,))] \
        + [pltpu.VMEM(p.shape[1:], p.dtype) for p in parts]


def _pair_plan(srcs, outs, send_sems, recv_sems):
    x, y, c = lax.axis_index("x"), lax.axis_index("y"), lax.axis_index("c")

    def copies():
        return [pltpu.make_async_remote_copy(
            src_ref=srcs[a].at[1 - c], dst_ref=outs[a], send_sem=send_sems.at[a], recv_sem=recv_sems.at[a],
            device_id=(x, y, 1 - c), device_id_type=MESH) for a in range(len(srcs))]

    def start():
        _handshake([(x, y, 1 - c)])
        for cp in copies():
            cp.start()

    def finish():
        for cp in copies():
            cp.wait()

    return start, finish


def pair_add_exchange(parts, got, late, name):
    n = len(parts)
    chips = late.shape[1]

    def body(core_ref, *refs):
        late_own, late_all = refs[2 * n:2 * n + 2]
        late_out = refs[3 * n + 2]
        landed, send_sems, recv_sems = refs[3 * n + 3:]
        i = pl.program_id(0)
        start, finish = _pair_plan([late_all], [landed], send_sems, recv_sems)
        pl.when(i == 0)(start)
        for a in range(n):
            refs[2 * n + 2 + a][...] = (refs[a][...].astype(F32) + refs[n + a][...].astype(F32)).astype(BF16)

        @pl.when(i == chips - 1)
        def _():
            finish()
            late_out[...] = (late_own[...].astype(F32) + landed[...].astype(F32)).astype(BF16)

    def own(p):
        zeros = (0,) * (p.ndim - 2)
        return pl.BlockSpec((None, 1, *p.shape[2:]), lambda i, core: (core[0], i, *zeros))

    def plain(p):
        zeros = (0,) * (p.ndim - 1)
        return pl.BlockSpec((1, *p.shape[1:]), lambda i, core: (i, *zeros))

    zeros = (0,) * (late.ndim - 1)
    core = lax.axis_index("c").astype(jnp.int32).reshape(1)
    return pl.pallas_call(
        body, name=name,
        grid_spec=pltpu.PrefetchScalarGridSpec(
            num_scalar_prefetch=1, grid=(chips,),
            in_specs=[own(p) for p in parts] + [plain(p) for p in got]
            + [pl.BlockSpec((None, *late.shape[1:]), lambda i, core: (core[0], *zeros)), ANY],
            out_specs=[plain(p) for p in got] + [pl.BlockSpec(late.shape[1:], lambda i, core: zeros)],
            scratch_shapes=[pltpu.VMEM(late.shape[1:], late.dtype), pltpu.SemaphoreType.DMA((1,)), pltpu.SemaphoreType.DMA((1,))]),
        out_shape=[jax.ShapeDtypeStruct(p.shape, BF16) for p in got] + [jax.ShapeDtypeStruct(late.shape[1:], BF16)],
        compiler_params=_params(collective_id=PAIR_COLLECTIVE_ID),
    )(core, *parts, *got, late, late)


def _adamw(w, g, m, v):
    m2 = ADAM_B1 * m + (1.0 - ADAM_B1) * g
    v2 = ADAM_B2 * v + (1.0 - ADAM_B2) * (g * g)
    m_hat = m2 / (1.0 - ADAM_B1 ** ADAM_STEP)
    v_hat = v2 / (1.0 - ADAM_B2 ** ADAM_STEP)
    delta = -ADAM_LR * (m_hat / (jnp.sqrt(v_hat) + ADAM_EPS) + ADAM_WD * w)
    return delta, m2, v2


def sum_adamw(slots, ws, ms, vs, name):
    n = len(ws)
    halves = 2

    def body(*refs):
        for a in range(n):
            total = refs[a][0].astype(F32)
            for s in range(1, slots[a].shape[0]):
                total = total + refs[a][s].astype(F32)
            delta, m2, v2 = _adamw(refs[n + a][...], total, refs[2 * n + a][...], refs[3 * n + a][...])
            for q, val in enumerate((total, delta, m2, v2)):
                refs[4 * n + 4 * a + q][...] = val

    def rows(w):
        return pl.BlockSpec((w.shape[0] // halves, w.shape[1]), lambda i: (i, 0))

    def slot_rows(p):
        return pl.BlockSpec((p.shape[0], p.shape[1] // halves, p.shape[2]), lambda i: (0, i, 0))

    out = pl.pallas_call(
        body, name=name, grid=(halves,),
        out_shape=[jax.ShapeDtypeStruct(w.shape, F32) for w in ws for _ in range(4)],
        in_specs=[slot_rows(p) for p in slots] + [rows(w) for w in ws] * 3,
        out_specs=[rows(w) for w in ws for _ in range(4)],
        compiler_params=_params(),
    )(*slots, *ws, *ms, *vs)
    return [out[4 * a:4 * a + 4] for a in range(n)]


def adamw_update(ws, gs, ms, vs, name):
    n = len(ws)

    def body(*refs):
        for a in range(n):
            delta, m2, v2 = _adamw(refs[a][...], refs[n + a][...], refs[2 * n + a][...], refs[3 * n + a][...])
            refs[4 * n + 3 * a][...] = delta
            refs[4 * n + 3 * a + 1][...] = m2
            refs[4 * n + 3 * a + 2][...] = v2

    out = pl.pallas_call(
        body, name=name,
        out_shape=[jax.ShapeDtypeStruct(w.shape, F32) for w in ws for _ in range(3)],
        in_specs=[VMEM] * (4 * n), out_specs=[VMEM] * (3 * n),
        compiler_params=_params(),
    )(*ws, *gs, *ms, *vs)
    return [out[3 * a:3 * a + 3] for a in range(n)]


GAIN_ROWS = 8
ROW_POOL_SCALE = 4 * GAIN_ROWS
ROW_SINKS = ROW_POOL_SCALE + 4
ROW_LOSS = ROW_SINKS + 1
ROW_W_POOL = 40
SMALL_ROWS = ROW_W_POOL + 4 * POOL_GROUP_DIM


def grad_w_in_small_reduce(a, b, gains, dpool_scale, dsinks, loss_part, dw_pool, drel_bias):
    t, m = a.shape
    d = b.shape[1]
    r = m // N_DEV
    tt = TOKEN_TILE
    last = t // tt - 1

    def body(a_ref, b_ref, g0, g1, g2, g3, dsc_ref, dsink_ref, loss_ref, dwp_ref, drb_ref, out_ref, total_ref, total_rb_ref,
             acc, stage, gat, gat_rb, g_send, g_recv):
        k = pl.program_id(0)
        x, y, c = lax.axis_index("x"), lax.axis_index("y"), lax.axis_index("c")
        start, finish = _gather_plan([stage, drb_ref], [gat, gat_rb], g_send, g_recv)

        @pl.when(k == 0)
        def _():
            for q, g_ref in enumerate((g0, g1, g2, g3)):
                stage[GAIN_ROWS * q:GAIN_ROWS * (q + 1), :] = g_ref[...]
            stage[ROW_POOL_SCALE:ROW_SINKS, :] = dsc_ref[...]
            stage[ROW_SINKS:ROW_LOSS, :] = dsink_ref[...]
            stage[ROW_LOSS:ROW_LOSS + 1, :] = loss_ref[...]
            stage[ROW_LOSS + 1:ROW_W_POOL, :] = jnp.zeros((ROW_W_POOL - ROW_LOSS - 1, 128), F32)
            stage[ROW_W_POOL:, :] = dwp_ref[...].reshape(4 * POOL_GROUP_DIM, POOL_GROUP_DIM)
            gat[4 * x + 2 * y + c] = stage[...]
            gat_rb[4 * x + 2 * y + c] = drb_ref[...]
            start()
            acc[...] = jnp.zeros_like(acc)

        acc[...] += _dot_tn(a_ref[...], b_ref[...])

        @pl.when(k == last)
        def _():
            blocks = acc[...].reshape(N_CHIP, 2, r, d)
            for chip in range(N_CHIP):
                for core in range(2):
                    out_ref[core, chip] = blocks[chip, core].astype(BF16)
            finish()
            total, total_rb = gat[0], gat_rb[0]
            for s in range(1, N_DEV):
                total, total_rb = total + gat[s], total_rb + gat_rb[s]
            total_ref[...] = total
            total_rb_ref[...] = total_rb

    out_shape = (2, N_CHIP, r, d)
    return pl.pallas_call(
        body, name="grad_w_in", grid=(t // tt,),
        out_shape=[jax.ShapeDtypeStruct(out_shape, BF16), jax.ShapeDtypeStruct((SMALL_ROWS, 128), F32),
                   jax.ShapeDtypeStruct(drel_bias.shape, F32)],
        in_specs=[pl.BlockSpec((tt, m), lambda k: (k, 0)), pl.BlockSpec((tt, d), lambda k: (k, 0))] + [VMEM] * 9,
        out_specs=[pl.BlockSpec(out_shape, lambda k: (0,) * len(out_shape)), VMEM, VMEM],
        scratch_shapes=[pltpu.VMEM((m, d), F32), pltpu.VMEM((SMALL_ROWS, 128), F32),
                        pltpu.VMEM((N_DEV, SMALL_ROWS, 128), F32), pltpu.VMEM((N_DEV, *drel_bias.shape), F32),
                        pltpu.SemaphoreType.DMA((14,)), pltpu.SemaphoreType.DMA((14,))],
        compiler_params=_params(collective_id=GATHER_COLLECTIVE_ID),
    )(a, b, *gains, dpool_scale, dsinks, loss_part, dw_pool, drel_bias)


def reduce_w_in(d_in_t):
    def body(d_in_ref, g_in_ref, pair_got, chip_part, chip_got, p_send, p_recv, x_send, x_recv):
        x, y, c = lax.axis_index("x"), lax.axis_index("y"), lax.axis_index("c")
        my_chip = 2 * x + y
        _handshake([(x, y, 1 - c)] + [(x ^ (k >> 1), y ^ (k & 1), c) for k in range(1, N_CHIP)])
        pair = pltpu.make_async_remote_copy(
            src_ref=d_in_ref.at[1 - c], dst_ref=pair_got, send_sem=p_send, recv_sem=p_recv,
            device_id=(x, y, 1 - c), device_id_type=MESH)
        pair.start()
        pair.wait()
        chip_part[...] = (d_in_ref[c].astype(F32) + pair_got[...].astype(F32)).astype(BF16)
        copies = []
        for k in range(1, N_CHIP):
            px, py = x ^ (k >> 1), y ^ (k & 1)
            copies.append(pltpu.make_async_remote_copy(
                src_ref=chip_part.at[2 * px + py], dst_ref=chip_got.at[my_chip],
                send_sem=x_send.at[k - 1], recv_sem=x_recv.at[k - 1], device_id=(px, py, c), device_id_type=MESH))
        for cp in copies:
            cp.start()
        chip_got[my_chip] = chip_part[my_chip]
        for cp in copies:
            cp.wait()
        g_in = chip_got[0].astype(F32)
        for s in range(1, N_CHIP):
            g_in = g_in + chip_got[s].astype(F32)
        g_in_ref[...] = g_in

    per_core = d_in_t.shape[1:]
    return pl.pallas_call(
        body, name="reduce_w_in",
        out_shape=jax.ShapeDtypeStruct(d_in_t.shape[2:], F32),
        in_specs=[VMEM], out_specs=VMEM,
        scratch_shapes=[pltpu.VMEM(per_core, d_in_t.dtype), pltpu.VMEM(per_core, d_in_t.dtype),
                        pltpu.VMEM(per_core, d_in_t.dtype),
                        pltpu.SemaphoreType.DMA, pltpu.SemaphoreType.DMA,
                        pltpu.SemaphoreType.DMA((3,)), pltpu.SemaphoreType.DMA((3,))],
        compiler_params=_params(collective_id=GATHER_COLLECTIVE_ID),
    )(d_in_t)


def small_adamw(total, total_rb, small_w, small_m, small_v):
    n_small = len(small_w)

    def body(*refs):
        total_ref, rb_ref = refs[:2]
        w_refs, m_refs, v_refs = (refs[2 + k * n_small:2 + (k + 1) * n_small] for k in range(3))
        loss_out = refs[2 + 3 * n_small]
        result = refs[3 + 3 * n_small:]
        total = total_ref[...]
        loss_out[...] = total[ROW_LOSS:ROW_LOSS + 1, :]
        grads = [_as_lanes(total[GAIN_ROWS * k:GAIN_ROWS * (k + 1), :]) for k in range(4)]
        grads.append(_as_lanes(total[ROW_POOL_SCALE:ROW_SINKS, :]))
        grads.append(total[ROW_SINKS:ROW_LOSS, 0:N_Q_HEADS])
        grads.append(total[ROW_W_POOL:, :].reshape(w_refs[6].shape))
        grads.append(rb_ref[...])
        for k in range(n_small):
            delta, m2, v2 = _adamw(w_refs[k][...], grads[k], m_refs[k][...], v_refs[k][...])
            result[4 * k][...] = grads[k]
            result[4 * k + 1][...] = delta
            result[4 * k + 2][...] = m2
            result[4 * k + 3][...] = v2

    out = pl.pallas_call(
        body, name="small_adamw",
        out_shape=[jax.ShapeDtypeStruct((1, 128), F32)] + [jax.ShapeDtypeStruct(w.shape, F32) for w in small_w for _ in range(4)],
        in_specs=[VMEM] * (2 + 3 * n_small), out_specs=[VMEM] * (1 + 4 * n_small),
        compiler_params=_params(),
    )(total, total_rb, *small_w, *small_m, *small_v)
    return out[0], [out[1 + 4 * k:5 + 4 * k] for k in range(n_small)]


def norm_inproj(x, g, w_shard, shard, shard_rows):
    t, d = x.shape
    r = w_shard.shape[0]
    tm = TOKEN_TILE
    nt = t // tm

    def body(x_ref, g_ref, w_shard_ref, shard_ref, proj_ref, h_ref, w_ref, gathered_ref, h_all, w_all, w_sem,
             send_w, recv_w, local_w, bounce_w, send_sems, recv_sems, local_sems, bounce):
        i = pl.program_id(0)
        start_w, finish_w = _gather_plan([w_shard_ref], [w_ref], send_w, recv_w, local_w, [bounce_w])
        start, finish = _gather_plan([shard_ref], [gathered_ref], send_sems, recv_sems, local_sems, [bounce],
                                     [(0, shard.shape[0])], shake=False)

        @pl.when(i == 0)
        def _():
            start_w()
            start()

        @pl.when(i < nt)
        def _():
            xv = x_ref[...]
            h = ((xv * _rstd(xv)) * g_ref[...]).astype(BF16)
            h_ref[...] = h
            h_all[pl.ds(pl.multiple_of(i * tm, tm), tm), :] = h

        @pl.when(i == nt - 1)
        def _():
            finish_w()
            landed = pltpu.make_async_copy(w_ref, w_all, w_sem)
            landed.start()
            landed.wait()

        @pl.when(i >= nt)
        def _():
            rows = pl.ds(pl.multiple_of((i - nt) * tm, tm), tm)
            proj_ref[...] = _dot_nt(h_all[rows, :], _merge_rows(w_all[...]))

        pl.when(i == 2 * nt - 2)(finish.forward)
        pl.when(i == 2 * nt - 1)(finish.complete)

    first = lambda i: (jnp.minimum(i, nt - 1), 0)
    return pl.pallas_call(
        body, name="norm_inproj", grid=(2 * nt,),
        out_shape=[jax.ShapeDtypeStruct((t, N_DEV * r), F32), jax.ShapeDtypeStruct((t, d), BF16),
                   jax.ShapeDtypeStruct((N_DEV, r, d), w_shard.dtype),
                   jax.ShapeDtypeStruct((N_DEV, shard_rows, d), shard.dtype)],
        in_specs=[pl.BlockSpec((tm, d), first), pl.BlockSpec((1, d), lambda i: (0, 0)), ANY, ANY],
        out_specs=[pl.BlockSpec((tm, N_DEV * r), lambda i: (jnp.maximum(i - nt, 0), 0)), pl.BlockSpec((tm, d), first),
                   ANY, ANY],
        scratch_shapes=[pltpu.VMEM((t, d), BF16), pltpu.VMEM((N_DEV, r, d), w_shard.dtype), pltpu.SemaphoreType.DMA]
        + _gather_scratch([w_shard]) + _gather_scratch([shard]),
        compiler_params=_params(collective_id=GATHER_COLLECTIVE_ID),
    )(x, g, w_shard, shard)


def _fill_bias_band(bk_ref, win_ref, rb_ref, biasm_ref):
    bk = bk_ref[...]
    keep = win_ref[...] > 0.5
    for h in range(N_Q_HEADS):
        acc = jnp.zeros(bk.shape, F32)
        for b in range(N_BUCKETS):
            acc = jnp.where(bk == float(b), rb_ref[b, h], acc)
        biasm_ref[h] = jnp.where(keep, acc, NEG_INF)


def _window_sum(buf_ref, g, w, first):
    cols = slice(g * POOL_GROUP_DIM, (g + 1) * POOL_GROUP_DIM)
    acc = None
    for k in range(w):
        piece = buf_ref[first(k):first(k) + BLOCK, cols]
        acc = piece if acc is None else acc + piece
    return acc


def _inv_count(i, w):
    row = lax.broadcasted_iota(jnp.int32, (BLOCK, 1), 0)
    return 1.0 / jnp.minimum(i * BLOCK + row + 1, w).astype(F32)


def _fill_pool_input(i, ubuf, uc_ref, halo_ref):
    ubuf[0:HALO, :] = jnp.where(i > 0, halo_ref[...], 0.0)
    ubuf[HALO:, :] = uc_ref[...]


def _pooled(i, g, w, ubuf):
    cols = slice(g * POOL_GROUP_DIM, (g + 1) * POOL_GROUP_DIM)
    return _window_sum(ubuf, g, w, lambda k: HALO - k) * _inv_count(i, w) - ubuf[HALO:, cols]


def _head_variants(pair):
    low = lax.broadcasted_iota(jnp.int32, pair.shape, 1) < HEAD_DIM
    swapped = pltpu.roll(pair, HEAD_DIM, 1)
    zero = jnp.zeros_like(pair)
    pick = lambda c, a, b: jnp.where(c, a, b).astype(BF16)
    return [[pick(low, pair, zero), pick(low, zero, swapped)], [pick(low, swapped, zero), pick(low, zero, pair)]]


def _head_probs(i, hq, rows, s_ref, biasm_ref, sinks_ref):
    s = s_ref[hq, rows, :] * ATTN_SCALE + biasm_ref[hq, rows, :]
    col = lax.broadcasted_iota(jnp.int32, s.shape, 1)
    s = jnp.where((i == 0) & (col < BLOCK), NEG_INF, s)
    sink = sinks_ref[0, hq]
    m = jnp.maximum(jnp.max(s, axis=-1, keepdims=True), sink)
    p = jnp.exp(s - m)
    e_sink = jnp.exp(sink - m)
    inv = 1.0 / (jnp.sum(p, axis=-1, keepdims=True) + e_sink)
    return p * inv, e_sink * inv


def _head_slot(hq):
    return 4 * (hq // GQA_GROUP) + 2 * (hq % 2) + (hq % GQA_GROUP) // 2


def _mixer_in_specs(cur, prv):
    return [pl.BlockSpec((BLOCK, 512), lambda i: (cur(i), 0)),
            pl.BlockSpec((HALO, 512), lambda i: (jnp.maximum(cur(i) * (BLOCK // HALO) - 1, 0), 0)),
            pl.BlockSpec((BLOCK, 512), lambda i: (cur(i), 1)),
            pl.BlockSpec((BLOCK, 256), lambda i: (cur(i), 4)),
            pl.BlockSpec((BLOCK, 256), lambda i: (prv(i), 4))]


def _mixer_param_specs():
    return [pl.BlockSpec((4, POOL_GROUP_DIM, POOL_GROUP_DIM), lambda i: (0, 0, 0)),
            pl.BlockSpec((1, POOL_WIDTH), lambda i: (0, 0))]


def mixers_fwd(proj, bucket, in_window, rel_bias, sinks, w_pool, pool_scale, shards):
    t = proj.shape[0]
    nb = t // BLOCK
    ns = len(shards)

    def body(*refs):
        uc_ref, halo_ref, q_ref, kvc_ref, kvp_ref, bk_ref, win_ref, rb_ref, sinks_ref, wp_ref, sc_ref = refs[:11]
        shard_refs = refs[11:11 + ns]
        out_ref, pooled_ref, p_all, psink_ref = refs[11 + ns:15 + ns]
        gathered_refs = refs[15 + ns:15 + 2 * ns]
        ubuf, s_all, biasm_ref, send_sems, recv_sems, local_sems = refs[15 + 2 * ns:21 + 2 * ns]
        i = pl.program_id(0)
        start, finish = _gather_plan(shard_refs, gathered_refs, send_sems, recv_sems, local_sems, refs[21 + 2 * ns:])

        @pl.when(i == 0)
        def _():
            start()
            _fill_bias_band(bk_ref, win_ref, rb_ref, biasm_ref)

        _fill_pool_input(i, ubuf, uc_ref, halo_ref)
        for g, w in enumerate(POOL_WINDOWS):
            cols = slice(g * POOL_GROUP_DIM, (g + 1) * POOL_GROUP_DIM)
            pooled = _pooled(i, g, w, ubuf).astype(BF16)
            pooled_ref[:, cols] = pooled
            out_ref[:, cols] = (_dot(pooled, wp_ref[g]) * sc_ref[:, cols]).astype(BF16)
        kv = jnp.concatenate([kvp_ref[...], kvc_ref[...]], axis=0)
        k_var = _head_variants(kv[:, 0:2 * HEAD_DIM])
        v_var = _head_variants(kv[:, 2 * HEAD_DIM:])
        for hq in range(N_Q_HEADS):
            j, half, h = hq // 2, hq % 2, hq // GQA_GROUP
            q2 = q_ref[:, 2 * HEAD_DIM * j:2 * HEAD_DIM * (j + 1)].astype(BF16)
            s_all[hq] = _dot_nt(q2, k_var[h][half])
        psink_ref[...] = jnp.zeros_like(psink_ref)
        for hq in range(N_Q_HEADS):
            for r in range(0, BLOCK, ROW_CHUNK):
                rows = slice(r, r + ROW_CHUNK)
                probs, p_sink = _head_probs(i, hq, rows, s_all, biasm_ref, sinks_ref)
                p_all[_head_slot(hq), rows, :] = probs.astype(BF16)
                psink_ref[rows, hq:hq + 1] = p_sink
        for j in range(N_Q_HEADS // 2):
            h = 2 * j // GQA_GROUP
            acc = _dot(p_all[_head_slot(2 * j)], v_var[h][0]) + _dot(p_all[_head_slot(2 * j + 1)], v_var[h][1])
            out_ref[:, POOL_WIDTH + 2 * HEAD_DIM * j:POOL_WIDTH + 2 * HEAD_DIM * (j + 1)] = acc.astype(BF16)

        pl.when(i == max(nb - 4, 0))(finish.forward)
        pl.when(i == nb - 1)(finish.complete)

    return pl.pallas_call(
        body, name="mixers_fwd", grid=(nb,),
        out_shape=[jax.ShapeDtypeStruct((t, 2 * POOL_WIDTH), BF16), jax.ShapeDtypeStruct((t, POOL_WIDTH), BF16),
                   jax.ShapeDtypeStruct((N_Q_HEADS, t, 2 * BLOCK), BF16), jax.ShapeDtypeStruct((t, 128), F32)]
        + [jax.ShapeDtypeStruct((N_DEV, *sh.shape), sh.dtype) for sh in shards],
        in_specs=_mixer_in_specs(lambda i: i, lambda i: jnp.maximum(i - 1, 0))
        + [pl.BlockSpec((BLOCK, 2 * BLOCK), lambda i: (0, 0))] * 2 + [SMEM, SMEM] + _mixer_param_specs() + [ANY] * ns,
        out_specs=[pl.BlockSpec((BLOCK, 2 * POOL_WIDTH), lambda i: (i, 0)), pl.BlockSpec((BLOCK, POOL_WIDTH), lambda i: (i, 0)),
                   pl.BlockSpec((N_Q_HEADS, BLOCK, 2 * BLOCK), lambda i: (0, i, 0)), pl.BlockSpec((BLOCK, 128), lambda i: (i, 0))]
        + [ANY] * ns,
        scratch_shapes=[pltpu.VMEM((HALO + BLOCK, POOL_WIDTH), F32), pltpu.VMEM((N_Q_HEADS, BLOCK, 2 * BLOCK), F32),
                        pltpu.VMEM((N_Q_HEADS, BLOCK, 2 * BLOCK), F32)]
        + _gather_scratch(shards),
        compiler_params=_params(collective_id=GATHER_COLLECTIVE_ID),
    )(proj, proj, proj, proj, proj, bucket, in_window, rel_bias, sinks, w_pool, pool_scale, *shards)


def outproj_norm(cat, w, x, g, g_next, shard, partial):
    t, d = x.shape
    tm = TOKEN_TILE
    last = t // tm - 1
    rows = [(partial.shape[1] - shard.shape[0], shard.shape[0])]

    def body(c_ref, w_ref, x_ref, g_ref, gn_ref, shard_ref, partial_ref, mix_ref, x1_ref, h2_ref, gathered_ref,
             send_sems, recv_sems, local_sems, bounce):
        i = pl.program_id(0)
        start, finish = _gather_plan([shard_ref], [gathered_ref], send_sems, recv_sems, local_sems, [bounce], rows)
        pl.when(i == 0)(start)
        mix = _dot(c_ref[...], w_ref[...])
        mix_ref[...] = mix
        x1 = x_ref[...] + (mix * _rstd(mix)) * g_ref[...]
        x1_ref[...] = x1
        h2_ref[...] = ((x1 * _rstd(x1)) * gn_ref[...]).astype(BF16)
        pl.when(i == max(last - 1, 0))(finish.forward)
        pl.when(i == last)(finish.complete)

    row = pl.BlockSpec((tm, d), lambda i: (i, 0))
    gain = pl.BlockSpec((1, d), lambda i: (0, 0))
    return pl.pallas_call(
        body, name="outproj_norm", grid=(t // tm,),
        out_shape=[jax.ShapeDtypeStruct((t, d), F32), jax.ShapeDtypeStruct((t, d), F32), jax.ShapeDtypeStruct((t, d), BF16),
                   jax.ShapeDtypeStruct(partial.shape, partial.dtype)],
        in_specs=[pl.BlockSpec((tm, cat.shape[1]), lambda i: (i, 0)), pl.BlockSpec(w.shape, lambda i: (0, 0)), row, gain, gain,
                  ANY, ANY],
        out_specs=[row, row, row, ANY],
        input_output_aliases={6: 3},
        scratch_shapes=_gather_scratch([shard]),
        compiler_params=_params(collective_id=GATHER_COLLECTIVE_ID),
    )(cat, w, x, g, g_next, shard, partial)


def ffn_up(h, gate_t, up_t, down_shard):
    t, d = h.shape
    n = gate_t.shape[1]
    f = N_DEV * n
    tm, ts = FFN_TOKEN_TILE, FF_SHARDS_PER_TILE
    tn = ts * n
    steps = (f // tn, t // tm)

    def body(h_ref, wg_ref, wu_ref, shard_ref, gate_ref, up_ref, a_ref, gathered_ref,
             send_sems, recv_sems, local_sems, bounce):
        j, i = pl.program_id(0), pl.program_id(1)
        start, finish = _gather_plan([shard_ref], [gathered_ref], send_sems, recv_sems, local_sems, [bounce])
        pl.when((i == 0) & (j == 0))(start)

        hv = h_ref[...]
        gate = _dot_nt(hv, _merge_rows(wg_ref[...]))
        up = _dot_nt(hv, _merge_rows(wu_ref[...]))
        gate_ref[...] = gate.astype(BF16)
        up_ref[...] = up.astype(BF16)
        a_ref[...] = (gate * (1.0 / (1.0 + jnp.exp(-gate))) * up).astype(BF16)

        pl.when((j == steps[0] - 1) & (i == max(steps[1] - 2, 0)))(finish.forward)
        pl.when((j == steps[0] - 1) & (i == steps[1] - 1))(finish.complete)

    wide = pl.BlockSpec((tm, tn), lambda j, i: (i, j))
    return pl.pallas_call(
        body, name="ffn_up", grid=steps,
        out_shape=[jax.ShapeDtypeStruct((t, f), BF16)] * 3
        + [jax.ShapeDtypeStruct((N_DEV, *down_shard.shape), down_shard.dtype)],
        in_specs=[pl.BlockSpec((tm, d), lambda j, i: (i, 0)),
                  pl.BlockSpec((ts, n, d), lambda j, i: (j, 0, 0)),
                  pl.BlockSpec((ts, n, d), lambda j, i: (j, 0, 0)), ANY],
        out_specs=[wide, wide, wide, ANY],
        scratch_shapes=_gather_scratch([down_shard]),
        compiler_params=_params(collective_id=GATHER_COLLECTIVE_ID),
    )(h, gate_t, up_t, down_shard)


def ffn_down_loss(a, w_down, x1, g, target):
    t, d = x1.shape
    tm = WIDE_K_TOKEN_TILE

    def body(a_ref, w_ref, x_ref, g_ref, t_ref, df_ref, dy_ref, dg_ref, loss_ref):
        @pl.when(pl.program_id(0) == 0)
        def _():
            dg_ref[...] = jnp.zeros_like(dg_ref)
            loss_ref[...] = jnp.zeros_like(loss_ref)

        f = _dot(a_ref[...], _merge_rows(w_ref[...]))
        r = _rstd(f)
        g = g_ref[...]
        err = x_ref[...] + (f * r) * g - t_ref[...]
        loss_ref[...] += 0.5 * jnp.sum(jnp.mean(err * err, axis=-1, keepdims=True))
        dy = err * (1.0 / d)
        dy_ref[...] = dy
        df, dg_rows = _norm_bwd(dy, f, r, g)
        df_ref[...] = df.astype(BF16)
        dg_ref[...] += _as_rows(jnp.sum(dg_rows, axis=0, keepdims=True))

    row = pl.BlockSpec((tm, d), lambda i: (i, 0))
    gain = pl.BlockSpec((1, d), lambda i: (0, 0))
    return pl.pallas_call(
        body, name="ffn_down_loss", grid=(t // tm,),
        out_shape=[jax.ShapeDtypeStruct((t, d), BF16), jax.ShapeDtypeStruct((t, d), F32),
                   jax.ShapeDtypeStruct((d // 128, 128), F32), jax.ShapeDtypeStruct((1, 128), F32)],
        in_specs=[pl.BlockSpec((tm, a.shape[1]), lambda i: (i, 0)), pl.BlockSpec(w_down.shape, lambda i: (0, 0, 0)), row, gain, row],
        out_specs=[row, row, pl.BlockSpec((d // 128, 128), lambda i: (0, 0)), pl.BlockSpec((1, 128), lambda i: (0, 0))],
        compiler_params=_params(),
    )(a, w_down, x1, g, target)


def ffn_down_bwd(df, w_down, gate, up, h):
    t, d = df.shape
    n = w_down.shape[1]
    f = gate.shape[1]
    tm, ts = WIDE_K_TOKEN_TILE, FF_SHARDS_PER_TILE
    tn = ts * n
    last = t // tm - 1

    def body(df_ref, w_ref, gate_ref, up_ref, h_ref, dgate_ref, dup_ref, dwg_ref, dwu_ref, acc):
        i = pl.program_id(1)

        @pl.when(i == 0)
        def _():
            acc[...] = jnp.zeros_like(acc)

        da = _dot_nt(df_ref[...], _merge_rows(w_ref[...]))
        gate = gate_ref[...].astype(F32)
        sig = 1.0 / (1.0 + jnp.exp(-gate))
        dgate = (da * up_ref[...].astype(F32) * (sig * (1.0 + gate * (1.0 - sig)))).astype(BF16)
        dup = (da * (gate * sig)).astype(BF16)
        dgate_ref[...] = dgate
        dup_ref[...] = dup
        acc[0] += _dot_tn(dgate, h_ref[...])
        acc[1] += _dot_tn(dup, h_ref[...])

        @pl.when(i == last)
        def _():
            for w, out_ref in enumerate((dwg_ref, dwu_ref)):
                blocks = acc[w].reshape(ts // 2, 2, n, d)
                for chip in range(ts // 2):
                    for core in range(2):
                        out_ref[core, chip] = blocks[chip, core].astype(BF16)

    wide = pl.BlockSpec((tm, tn), lambda j, i: (i, j))
    rows = pl.BlockSpec((tm, d), lambda j, i: (i, 0))
    owned = pl.BlockSpec((2, ts // 2, n, d), lambda j, i: (0, j, 0, 0))
    out = pl.pallas_call(
        body, name="ffn_down_bwd", grid=(f // tn, t // tm),
        out_shape=[jax.ShapeDtypeStruct((t, f), BF16)] * 2 + [jax.ShapeDtypeStruct((2, N_CHIP, n, d), BF16)] * 2,
        in_specs=[rows, pl.BlockSpec((ts, n, d), lambda j, i: (j, 0, 0)), wide, wide, rows],
        out_specs=[wide, wide, owned, owned],
        scratch_shapes=[pltpu.VMEM((2, tn, d), F32)],
        compiler_params=_params(),
    )(df, w_down, gate, up, h)
    return out[:2], out[2:]


def grad_w_down(a, b, pair_parts):
    t, f = a.shape
    d = b.shape[1]
    na = len(pair_parts)
    n = f // N_DEV
    tt, ts = TOKEN_TILE, FF_SHARDS_PER_TILE
    tn = ts * n
    steps = (f // tn, t // tt)

    def body(a_ref, b_ref, *refs):
        part_refs, out_ref, got_refs = refs[:na], refs[na], refs[na + 1:2 * na + 1]
        acc, send_sems, recv_sems = refs[2 * na + 1:]
        i, k = pl.program_id(0), pl.program_id(1)
        start, finish = _pair_plan(part_refs, got_refs, send_sems, recv_sems)
        pl.when((i == 0) & (k == 0))(start)

        @pl.when(k == 0)
        def _():
            acc[...] = jnp.zeros_like(acc)

        acc[...] += _dot_tn(a_ref[...], b_ref[...])

        @pl.when(k == steps[1] - 1)
        def _():
            blocks = acc[...].reshape(ts // 2, 2, n, d)
            for chip in range(ts // 2):
                for core in range(2):
                    out_ref[core, chip] = blocks[chip, core].astype(BF16)

        pl.when((i == steps[0] - 1) & (k == steps[1] - 1))(finish)

    out = pl.pallas_call(
        body, name="grad_w_down", grid=steps,
        out_shape=[jax.ShapeDtypeStruct((2, N_CHIP, n, d), BF16)]
        + [jax.ShapeDtypeStruct(p.shape[1:], p.dtype) for p in pair_parts],
        in_specs=[pl.BlockSpec((tt, tn), lambda i, k: (k, i)), pl.BlockSpec((tt, d), lambda i, k: (k, 0))] + [ANY] * na,
        out_specs=[pl.BlockSpec((2, ts // 2, n, d), lambda i, k: (0, i, 0, 0))] + [ANY] * na,
        scratch_shapes=[pltpu.VMEM((tn, d), F32), pltpu.SemaphoreType.DMA((na,)), pltpu.SemaphoreType.DMA((na,))],
        compiler_params=_params(collective_id=PAIR_COLLECTIVE_ID),
    )(a, b, *pair_parts)
    return out[0], out[1:]


def ffn_up_bwd(dgate, dup, gate_t, up_t, x1, g_ffn, dy, mix, g_mix, chip_parts):
    t, d = x1.shape
    n = gate_t.shape[1]
    f = N_DEV * n
    tm = WIDE_K_TOKEN_TILE
    na = len(chip_parts)
    last = t // tm - 1

    def body(*refs):
        dg_ref, du_ref, wg_ref, wu_ref, x_ref, gf_ref, dy_ref, mix_ref, gm_ref = refs[:9]
        part_refs = refs[9:9 + na]
        dx1_ref, dmix_ref, dgf_ref, dgm_ref = refs[9 + na:13 + na]
        slot_refs = refs[13 + na:13 + 2 * na]
        send_sems, recv_sems, local_sems = refs[13 + 2 * na:16 + 2 * na]
        i = pl.program_id(0)
        start, finish = _chip_exchange_plan(part_refs, slot_refs, send_sems, recv_sems, local_sems, refs[16 + 2 * na:])

        @pl.when(i == 0)
        def _():
            start()
            dgf_ref[...] = jnp.zeros_like(dgf_ref)
            dgm_ref[...] = jnp.zeros_like(dgm_ref)

        dh = _dot(dg_ref[...], _merge_rows(wg_ref[...])) + _dot(du_ref[...], _merge_rows(wu_ref[...]))
        x1 = x_ref[...]
        dx, dgf_rows = _norm_bwd(dh, x1, _rstd(x1), gf_ref[...])
        dx1 = dy_ref[...] + dx
        dx1_ref[...] = dx1
        dgf_ref[...] += _as_rows(jnp.sum(dgf_rows, axis=0, keepdims=True))
        mix = mix_ref[...]
        dmix, dgm_rows = _norm_bwd(dx1, mix, _rstd(mix), gm_ref[...])
        dmix_ref[...] = dmix.astype(BF16)
        dgm_ref[...] += _as_rows(jnp.sum(dgm_rows, axis=0, keepdims=True))
        pl.when(i == last)(finish)

    row = pl.BlockSpec((tm, d), lambda i: (i, 0))
    wide = pl.BlockSpec((tm, f), lambda i: (i, 0))
    gain = pl.BlockSpec((1, d), lambda i: (0, 0))
    gain_rows = pl.BlockSpec((d // 128, 128), lambda i: (0, 0))
    whole = pl.BlockSpec((N_DEV, n, d), lambda i: (0, 0, 0), pipeline_mode=pl.Buffered(1))
    out = pl.pallas_call(
        body, name="ffn_up_bwd", grid=(t // tm,),
        out_shape=[jax.ShapeDtypeStruct((t, d), F32), jax.ShapeDtypeStruct((t, d), BF16),
                   jax.ShapeDtypeStruct((d // 128, 128), F32), jax.ShapeDtypeStruct((d // 128, 128), F32)]
        + [jax.ShapeDtypeStruct(p.shape, p.dtype) for p in chip_parts],
        in_specs=[wide, wide, whole, whole, row, gain, row, row, gain] + [ANY] * na,
        out_specs=[row, row, gain_rows, gain_rows] + [ANY] * na,
        scratch_shapes=_chip_exchange_scratch(chip_parts),
        compiler_params=_params(collective_id=CHIP_COLLECTIVE_ID),
    )(dgate, dup, gate_t, up_t, x1, g_ffn, dy, mix, g_mix, *chip_parts)
    return out[:4], out[4:]


def outproj_bwd(dmix, w_out, cat):
    t, d = dmix.shape
    m = w_out.shape[0]
    r = m // N_DEV
    tm = TOKEN_TILE
    last = t // tm - 1

    def body(dm_ref, w_ref, cat_ref, dcat_ref, dw_ref, acc):
        k = pl.program_id(0)

        @pl.when(k == 0)
        def _():
            acc[...] = jnp.zeros_like(acc)

        dcat_ref[...] = _dot_nt(dm_ref[...], w_ref[...])
        acc[...] += _dot_tn(cat_ref[...], dm_ref[...])

        @pl.when(k == last)
        def _():
            blocks = acc[...].reshape(N_CHIP, 2, r, d)
            for chip in range(N_CHIP):
                for core in range(2):
                    dw_ref[core, chip] = blocks[chip, core].astype(BF16)

    tile = lambda width: pl.BlockSpec((tm, width), lambda k: (k, 0))
    return pl.pallas_call(
        body, name="outproj_bwd", grid=(t // tm,),
        out_shape=[jax.ShapeDtypeStruct((t, m), F32), jax.ShapeDtypeStruct((2, N_CHIP, r, d), BF16)],
        in_specs=[tile(d), pl.BlockSpec(w_out.shape, lambda k: (0, 0)), tile(m)],
        out_specs=[tile(m), pl.BlockSpec((2, N_CHIP, r, d), lambda k: (0, 0, 0, 0))],
        scratch_shapes=[pltpu.VMEM((m, d), F32)],
        compiler_params=_params(),
    )(dmix, w_out, cat)


def mixers_bwd(proj, dcat, pooled, probs, p_sinks, w_pool, pool_scale, bucket, ffn_parts):
    t = proj.shape[0]
    nb = t // BLOCK
    na = len(ffn_parts)

    def body(*refs):
        (q_ref, kvc_ref, kvp_ref, dcat_ref, pooled_ref, p_all, psink_ref, wp_ref, sc_ref, bk_ref) = refs[:10]
        part_refs = refs[10:10 + na]
        dproj_ref, dbias_ref, dsink_ref, dwp_ref, dsc_ref, drb_ref = refs[10 + na:16 + na]
        slot_refs = refs[16 + na:16 + 2 * na]
        dbuf, c_u, c_q, c_kv, dp_all, ds_all, sink_acc = refs[16 + 2 * na:23 + 2 * na]
        send_sems, recv_sems, local_sems = refs[23 + 2 * na:26 + 2 * na]
        bounce = refs[26 + 2 * na:]
        i = pl.program_id(0)
        lane = lax.broadcasted_iota(jnp.int32, (1, 128), 1)
        start, finish = _chip_exchange_plan(part_refs, slot_refs, send_sems, recv_sems, local_sems, bounce)

        @pl.when(i == 0)
        def _():
            start()
            dbias_ref[...] = jnp.zeros_like(dbias_ref)
            dwp_ref[...] = jnp.zeros_like(dwp_ref)
            dsc_ref[...] = jnp.zeros_like(dsc_ref)
            dsink_ref[...] = jnp.zeros_like(dsink_ref)
            dbuf[...] = jnp.zeros_like(dbuf)
            c_u[...] = jnp.zeros_like(c_u)
            c_q[...] = jnp.zeros_like(c_q)
            c_kv[...] = jnp.zeros_like(c_kv)

        @pl.when(i < nb)
        def _():
            for g, w in enumerate(POOL_WINDOWS):
                cols = slice(g * POOL_GROUP_DIM, (g + 1) * POOL_GROUP_DIM)
                pooled = pooled_ref[:, cols]
                mixed = _dot(pooled, wp_ref[g])
                dout = dcat_ref[:, cols]
                dsc_ref[g:g + 1, :] += jnp.sum(dout * mixed, axis=0, keepdims=True)
                dmixed = (dout * sc_ref[:, cols]).astype(BF16)
                dwp_ref[g] += _dot_tn(pooled, dmixed)
                dpooled = _dot_nt(dmixed, wp_ref[g])
                scaled = dpooled * _inv_count(i, w)
                dbuf[BLOCK:, cols] = scaled[0:HALO]
                dproj_ref[:, cols] = (_window_sum(dbuf, g, w, lambda k: k) + c_u[:, cols]).astype(BF16)
                dbuf[0:BLOCK, cols] = scaled
                c_u[:, cols] = -dpooled

            kv = jnp.concatenate([kvp_ref[...], kvc_ref[...]], axis=0)
            k_var = _head_variants(kv[:, 0:2 * HEAD_DIM])
            v_var = _head_variants(kv[:, 2 * HEAD_DIM:])
            q2s = [q_ref[:, 2 * HEAD_DIM * j:2 * HEAD_DIM * (j + 1)].astype(BF16) for j in range(N_Q_HEADS // 2)]
            do2s = [dcat_ref[:, POOL_WIDTH + 2 * HEAD_DIM * j:POOL_WIDTH + 2 * HEAD_DIM * (j + 1)].astype(BF16)
                    for j in range(N_Q_HEADS // 2)]
            slot = _head_slot
            for hq in range(N_Q_HEADS):
                j, half, h = hq // 2, hq % 2, hq // GQA_GROUP
                dp_all[hq] = _dot_nt(do2s[j], v_var[h][half])
            sink_acc[...] = jnp.zeros_like(sink_acc)
            for hq in range(N_Q_HEADS):
                for r in range(0, BLOCK, ROW_CHUNK):
                    rows = slice(r, r + ROW_CHUNK)
                    probs = p_all[slot(hq), rows, :].astype(F32)
                    dp = dp_all[hq, rows, :]
                    delta = jnp.sum(probs * dp, axis=-1, keepdims=True)
                    ds = probs * (dp - delta)
                    dbias_ref[hq, rows, :] += ds
                    sink_acc[rows, :] += jnp.where(lane == hq, psink_ref[rows, :], 0.0) * delta
                    ds_all[slot(hq), rows, :] = (ds * ATTN_SCALE).astype(BF16)
            dsink_ref[...] -= jnp.sum(sink_acc[...], axis=0, keepdims=True)
            dq2 = [None] * (N_Q_HEADS // 2)
            for hq in range(N_Q_HEADS):
                j, half, h = hq // 2, hq % 2, hq // GQA_GROUP
                dq = _dot(ds_all[slot(hq)], k_var[h][half])
                dq2[j] = dq if dq2[j] is None else dq2[j] + dq
            low = lax.broadcasted_iota(jnp.int32, (2 * BLOCK, 2 * HEAD_DIM), 1) < HEAD_DIM
            dk_half, dv_half = [[None, None], [None, None]], [[None, None], [None, None]]
            for h in range(N_KV_HEADS):
                for half in range(2):
                    heads = [hq for hq in range(GQA_GROUP * h, GQA_GROUP * (h + 1)) if hq % 2 == half]
                    base = slot(heads[0])
                    q_rows = jnp.concatenate([q2s[hq // 2] for hq in heads], axis=0)
                    do_rows = jnp.concatenate([do2s[hq // 2] for hq in heads], axis=0)
                    dk_half[h][half] = _dot_tn(_merge_rows(ds_all[base:base + 2]), q_rows)
                    dv_half[h][half] = _dot_tn(_merge_rows(p_all[base:base + 2]), do_rows)

            def pair_of(halves):
                return jnp.where(low, halves[0][0] + pltpu.roll(halves[0][1], HEAD_DIM, 1),
                                 halves[1][1] + pltpu.roll(halves[1][0], HEAD_DIM, 1))

            dkv = jnp.concatenate([pair_of(dk_half), pair_of(dv_half)], axis=1)
            dproj_ref[:, POOL_WIDTH:2 * POOL_WIDTH] = c_q[...].astype(BF16)
            dproj_ref[:, 2 * POOL_WIDTH:] = (c_kv[...] + dkv[0:BLOCK]).astype(BF16)
            c_q[...] = jnp.concatenate(dq2, axis=1)
            c_kv[...] = dkv[BLOCK:]

        @pl.when(i == nb)
        def _():
            dbuf[BLOCK:, :] = jnp.zeros((HALO, POOL_WIDTH), F32)
            for g, w in enumerate(POOL_WINDOWS):
                cols = slice(g * POOL_GROUP_DIM, (g + 1) * POOL_GROUP_DIM)
                dproj_ref[:, cols] = (_window_sum(dbuf, g, w, lambda k: k) + c_u[:, cols]).astype(BF16)
            dproj_ref[:, POOL_WIDTH:2 * POOL_WIDTH] = c_q[...].astype(BF16)
            dproj_ref[:, 2 * POOL_WIDTH:] = c_kv[...].astype(BF16)
            bk = bk_ref[...]
            for h in range(N_Q_HEADS):
                db = dbias_ref[h]
                for b in range(N_BUCKETS):
                    drb_ref[h, b] = jnp.sum(jnp.where(bk == float(b), db, 0.0))
            finish()

    cur = lambda i: jnp.minimum(i, nb - 1)
    prv = lambda i: jnp.maximum(jnp.minimum(i, nb - 1) - 1, 0)
    out = pl.pallas_call(
        body, name="mixers_bwd", grid=(nb + 1,),
        out_shape=[jax.ShapeDtypeStruct((t, proj.shape[1]), BF16),
                   jax.ShapeDtypeStruct((N_Q_HEADS, BLOCK, 2 * BLOCK), F32),
                   jax.ShapeDtypeStruct((1, 128), F32),
                   jax.ShapeDtypeStruct((4, POOL_GROUP_DIM, POOL_GROUP_DIM), F32),
                   jax.ShapeDtypeStruct((len(POOL_WINDOWS), POOL_GROUP_DIM), F32),
                   jax.ShapeDtypeStruct((N_Q_HEADS, N_BUCKETS), F32)]
        + [jax.ShapeDtypeStruct(p.shape, p.dtype) for p in ffn_parts],
        in_specs=_mixer_in_specs(cur, prv)[2:]
        + [pl.BlockSpec((BLOCK, 2 * POOL_WIDTH), lambda i: (cur(i), 0)), pl.BlockSpec((BLOCK, POOL_WIDTH), lambda i: (cur(i), 0)),
           pl.BlockSpec((N_Q_HEADS, BLOCK, 2 * BLOCK), lambda i: (0, cur(i), 0)), pl.BlockSpec((BLOCK, 128), lambda i: (cur(i), 0))]
        + _mixer_param_specs() + [pl.BlockSpec((BLOCK, 2 * BLOCK), lambda i: (0, 0))] + [ANY] * na,
        out_specs=[pl.BlockSpec((BLOCK, proj.shape[1]), lambda i: (jnp.maximum(i - 1, 0), 0)),
                   pl.BlockSpec((N_Q_HEADS, BLOCK, 2 * BLOCK), lambda i: (0, 0, 0)),
                   pl.BlockSpec((1, 128), lambda i: (0, 0)),
                   pl.BlockSpec((4, POOL_GROUP_DIM, POOL_GROUP_DIM), lambda i: (0, 0, 0)),
                   pl.BlockSpec((len(POOL_WINDOWS), POOL_GROUP_DIM), lambda i: (0, 0)), SMEM] + [ANY] * na,
        scratch_shapes=[pltpu.VMEM((BLOCK + HALO, POOL_WIDTH), F32),
                        pltpu.VMEM((BLOCK, POOL_WIDTH), F32), pltpu.VMEM((BLOCK, POOL_WIDTH), F32),
                        pltpu.VMEM((BLOCK, 256), F32),
                        pltpu.VMEM((N_Q_HEADS, BLOCK, 2 * BLOCK), F32), pltpu.VMEM((N_Q_HEADS, BLOCK, 2 * BLOCK), BF16),
                        pltpu.VMEM((BLOCK, 128), F32)]
        + _chip_exchange_scratch(ffn_parts),
        compiler_params=_params(collective_id=CHIP_COLLECTIVE_ID),
    )(proj, proj, proj, dcat, pooled, probs, p_sinks, w_pool, pool_scale, bucket, *ffn_parts)
    return out[:6], out[6:]


def inproj_bwd(dproj, w_in_t, x, g, dx1):
    t, d = x.shape
    n = dproj.shape[1]
    tm = TOKEN_TILE

    def body(dp_ref, w_ref, x_ref, g_ref, dx1_ref, dx_ref, dg_ref):
        @pl.when(pl.program_id(0) == 0)
        def _():
            dg_ref[...] = jnp.zeros_like(dg_ref)

        dh = _dot(dp_ref[...], w_ref[...])
        xv = x_ref[...]
        dx, dg_rows = _norm_bwd(dh, xv, _rstd(xv), g_ref[...])
        dx_ref[...] = dx1_ref[...] + dx
        dg_ref[...] += _as_rows(jnp.sum(dg_rows, axis=0, keepdims=True))

    row = pl.BlockSpec((tm, d), lambda i: (i, 0))
    gain = pl.BlockSpec((1, d), lambda i: (0, 0))
    return pl.pallas_call(
        body, name="inproj_bwd", grid=(t // tm,),
        out_shape=[jax.ShapeDtypeStruct((t, d), F32), jax.ShapeDtypeStruct((d // 128, 128), F32)],
        in_specs=[pl.BlockSpec((tm, n), lambda i: (i, 0)), pl.BlockSpec(w_in_t.shape, lambda i: (0, 0)), row, gain, row],
        out_specs=[row, pl.BlockSpec((d // 128, 128), lambda i: (0, 0))],
        compiler_params=_params(),
    )(dproj, w_in_t, x, g, dx1)


def _bucket_band():
    qi = jnp.arange(BLOCK)[:, None]
    kj = jnp.arange(2 * BLOCK)[None, :]
    dist = qi + BLOCK - kj
    n = jnp.maximum(dist, 0)
    nf = jnp.maximum(n, 1).astype(F32)
    large = MAX_EXACT + (jnp.log(nf / MAX_EXACT) / np.float32(np.log(MAX_DISTANCE / MAX_EXACT))
                         * (N_BUCKETS - MAX_EXACT)).astype(jnp.int32)
    large = jnp.minimum(large, N_BUCKETS - 1)
    bucket = jnp.where(n < MAX_EXACT, n, large)
    in_window = (dist >= 0) & (dist < BLOCK)
    return bucket.astype(F32), in_window.astype(F32)


def kernel(x, g_pre_mix, w_in, w_pool, pool_scale, rel_bias, sinks, w_out, g_post_mix, g_pre_ffn, w_gate, w_up, w_down, g_post_ffn, loss_target, m_g_pre_mix, m_w_in, m_w_pool, m_pool_scale, m_rel_bias, m_sinks, m_w_out, m_g_post_mix, m_g_pre_ffn, m_w_gate, m_w_up, m_w_down, m_g_post_ffn, v_g_pre_mix, v_w_in, v_w_pool, v_pool_scale, v_rel_bias, v_sinks, v_w_out, v_g_post_mix, v_g_pre_ffn, v_w_gate, v_w_up, v_w_down, v_g_post_ffn):
    d = x.shape[-1]
    xs, target = x[0], loss_target[0]

    w_in_ts = w_in[0].T.astype(BF16)
    w_out_s = w_out[0].astype(BF16)
    gate_ts = w_gate[0].T.astype(BF16)
    up_ts = w_up[0].T.astype(BF16)
    w_down_s = w_down[0].astype(BF16)

    bucket, in_window = _bucket_band()
    w_pool_b = w_pool[0].astype(BF16)
    half = up_ts.shape[0] // 2
    proj, h1, w_in_t, up_t = norm_inproj(xs, g_pre_mix, w_in_ts, up_ts[:half], up_ts.shape[0])
    w_in_t = w_in_t.reshape(-1, d)
    cat, pooled, probs, p_sinks, gate_t, w_out_f = mixers_fwd(
        proj, bucket, in_window, rel_bias, sinks, w_pool_b, pool_scale, [gate_ts, w_out_s])
    w_out_f = w_out_f.reshape(-1, d)
    mix, x1, h2, up_t = outproj_norm(cat, w_out_f, xs, g_post_mix, g_pre_ffn, up_ts[half:], up_t)
    gate, up, act, w_down_f = ffn_up(h2, gate_t, up_t, w_down_s)
    df, dy, dg_post_ffn, loss_part = ffn_down_loss(act, w_down_f, x1, g_post_ffn, target)

    (dgate, dup), (d_gate, d_up) = ffn_down_bwd(df, w_down_f, gate, up, h2)
    d_down, got_gate_up = grad_w_down(act, df, [d_gate, d_up])
    q_gate, q_up, q_down = pair_add_exchange([d_gate, d_up], got_gate_up, d_down, "pair_add_ffn")
    (dx1, dmix, dg_pre_ffn, dg_post_mix), (gate_slots, down_slots) = ffn_up_bwd(
        dgate, dup, gate_t, up_t, x1, g_pre_ffn, dy, mix, g_post_mix, [q_gate, q_down])
    dcat, d_out = outproj_bwd(dmix, w_out_f, cat)
    q_out, = pair_add_exchange([], [], d_out, "pair_add_out")
    (dproj, _, dsinks, dw_pool, dpool_scale, drel_bias), (up_slots, out_slots) = mixers_bwd(
        proj, dcat, pooled, probs, p_sinks, w_pool_b, pool_scale, bucket, [q_up, q_out])
    grad_x, dg_pre_mix = inproj_bwd(dproj, w_in_t, xs, g_pre_mix, dx1)

    small_w = [g_pre_mix, g_post_mix, g_pre_ffn, g_post_ffn, pool_scale, sinks, w_pool, rel_bias.T]
    small_m = [m_g_pre_mix, m_g_post_mix, m_g_pre_ffn, m_g_post_ffn, m_pool_scale, m_sinks, m_w_pool, m_rel_bias.T]
    small_v = [v_g_pre_mix, v_g_post_mix, v_g_pre_ffn, v_g_post_ffn, v_pool_scale, v_sinks, v_w_pool, v_rel_bias.T]
    d_in_t, total, total_rb = grad_w_in_small_reduce(
        dproj, h1, [dg_pre_mix, dg_post_mix, dg_pre_ffn, dg_post_ffn], dpool_scale, dsinks, loss_part, dw_pool, drel_bias)
    g_in_t = reduce_w_in(d_in_t)
    loss_row, sm = small_adamw(total, total_rb, small_w, small_m, small_v)
    sm[7] = [r.T for r in sm[7]]
    big_w = [w_in[0].T, w_out[0], w_gate[0].T, w_up[0].T, w_down[0]]
    big_m = [m_w_in[0].T, m_w_out[0], m_w_gate[0].T, m_w_up[0].T, m_w_down[0]]
    big_v = [v_w_in[0].T, v_w_out[0], v_w_gate[0].T, v_w_up[0].T, v_w_down[0]]
    upd = sum_adamw([out_slots, gate_slots, up_slots, down_slots], big_w[1:], big_m[1:], big_v[1:], "sum_adamw")
    upd = [[g_in_t, *adamw_update(big_w[:1], [g_in_t], big_m[:1], big_v[:1], "adamw_in")[0]], *upd]
    back = lambda k, a: (a.T if k in (0, 2, 3) else a)[None]
    big = [[back(k, u) for u in upd[k]] for k in range(5)]

    def ordered(kind):
        s, b = [p[kind] for p in sm], [p[kind] for p in big]
        return [s[0], b[0], s[6], s[4], s[7], s[5], b[1], s[1], s[2], b[2], b[3], b[4], s[3]]

    return (loss_row[0, 0], grad_x[None], *ordered(0), *ordered(1), *ordered(2), *ordered(3))
```

```python
import numpy as np
import jax
import jax.numpy as jnp
from jax import lax
from jax.experimental import pallas as pl
from jax.experimental.pallas import tpu as pltpu

F32 = jnp.float32
BF16 = jnp.bfloat16

N_DEV = 8
N_CHIP = 4
POOL_WIDTH = 512
POOL_WINDOWS = (2, 4, 8, 16)
POOL_GROUP_DIM = 128
HEAD_DIM = 64
N_Q_HEADS = 8
N_KV_HEADS = 2
GQA_GROUP = 4
BLOCK = 128
HALO = 16
ROW_CHUNK = 32
N_BUCKETS = 32
MAX_EXACT = 16
MAX_DISTANCE = 128
EPS = 1e-6
NEG_INF = -1e30
ATTN_SCALE = float(1.0 / np.sqrt(np.float32(HEAD_DIM)))

ADAM_LR = 0.001
ADAM_B1 = 0.9
ADAM_B2 = 0.999
ADAM_EPS = 1e-08
ADAM_WD = 0.01
ADAM_STEP = 10

TOKEN_TILE = 1024
WIDE_K_TOKEN_TILE = 512
FFN_TOKEN_TILE = 1024
FF_SHARDS_PER_TILE = 4
VMEM_LIMIT = 56 * 1024 * 1024
MESH = pl.DeviceIdType.MESH
PAIR_COLLECTIVE_ID = 0
GATHER_COLLECTIVE_ID = 1
CHIP_COLLECTIVE_ID = 2
ANY = pl.BlockSpec(memory_space=pl.ANY)
VMEM = pl.BlockSpec(memory_space=pltpu.VMEM)
SMEM = pl.BlockSpec(memory_space=pltpu.SMEM)


def _params(**kw):
    return pltpu.CompilerParams(vmem_limit_bytes=VMEM_LIMIT, **kw)


def _dot(a, b):
    return jnp.dot(a, b, preferred_element_type=F32)


def _dot_nt(a, b):
    return lax.dot_general(a, b, (((1,), (1,)), ((), ())), preferred_element_type=F32)


def _dot_tn(a, b):
    return lax.dot_general(a, b, (((0,), (0,)), ((), ())), preferred_element_type=F32)


def _rstd(v):
    return lax.rsqrt(jnp.mean(v * v, axis=-1, keepdims=True) + EPS)


def _norm_bwd(dout, v, r, g):
    vn = v * r
    dn = dout * g
    dv = r * (dn - vn * jnp.mean(dn * vn, axis=-1, keepdims=True))
    return dv, dout * vn


def _as_rows(v):
    return jnp.concatenate([v[:, k:k + 128] for k in range(0, v.shape[1], 128)], axis=0)


def _as_lanes(rows):
    return jnp.concatenate([rows[k:k + 1, :] for k in range(rows.shape[0])], axis=1)


def _handshake(peers):
    barrier = pltpu.get_barrier_semaphore()
    for peer in peers:
        pl.semaphore_signal(barrier, inc=1, device_id=peer, device_id_type=MESH)
    pl.semaphore_wait(barrier, len(peers))


def _merge_rows(value):
    s, r, c_ = value.shape
    return value.reshape(s * r, c_)


def _gather_plan(srcs, outs, send_sems, recv_sems, local_sems=None, bounce=None, rows=None, shake=True):
    n = len(srcs)
    x, y, c = lax.axis_index("x"), lax.axis_index("y"), lax.axis_index("c")
    me, sibling = (x, y, c), (x, y, 1 - c)
    chips = [(1 - x, y), (x, 1 - y), (1 - x, 1 - y)]

    def slot(a, px, py, pc):
        whole = outs[a].at[4 * px + 2 * py + pc]
        return whole if rows is None or rows[a] is None else whole.at[pl.ds(*rows[a])]

    def copy(a, k, block, to, from_src=False):
        return pltpu.make_async_remote_copy(
            src_ref=srcs[a] if from_src else slot(a, *block), dst_ref=slot(a, *block),
            send_sem=send_sems.at[k * n + a], recv_sem=recv_sems.at[k * n + a], device_id=to, device_id_type=MESH)

    def own_in(a):
        return pltpu.make_async_copy(srcs[a], bounce[a], local_sems.at[a])

    def own_out(a):
        return pltpu.make_async_copy(bounce[a], slot(a, *me), local_sems.at[a])

    def first(a):
        return [copy(a, 0, me, sibling, True)] + [copy(a, 1 + j, me, (*chip, c), True) for j, chip in enumerate(chips)]

    def passed(a, j):
        return copy(a, 4 + j, (*chips[j], c), sibling)

    def start():
        if shake:
            _handshake([sibling] + [(*chip, c) for chip in chips])
        for a in range(n):
            if bounce is not None:
                own_in(a).start()
            for cp in first(a):
                cp.start()

    def forward():
        if bounce is not None:
            for a in range(n):
                own_in(a).wait()
                own_out(a).start()
        for j, chip in enumerate(chips):
            for a in range(n):
                copy(a, 1 + j, (*chip, c), me).wait_recv()
                passed(a, j).start()

    def complete():
        for a in range(n):
            copy(a, 0, sibling, me).wait_recv()
            for j, chip in enumerate(chips):
                copy(a, 4 + j, (*chip, 1 - c), me).wait_recv()
        for a in range(n):
            for cp in first(a) + [passed(a, j) for j in range(3)]:
                cp.wait_send()
            if bounce is not None:
                own_out(a).wait()

    def finish():
        forward()
        complete()

    finish.forward, finish.complete = forward, complete
    return start, finish


def _gather_scratch(shards):
    n = len(shards)
    return [pltpu.SemaphoreType.DMA((7 * n,)), pltpu.SemaphoreType.DMA((7 * n,)), pltpu.SemaphoreType.DMA((n,))] \
        + [pltpu.VMEM(s.shape, s.dtype) for s in shards]


def _chip_exchange_plan(srcs, outs, send_sems, recv_sems, local_sems, bounce):
    n = len(srcs)
    x, y, c = lax.axis_index("x"), lax.axis_index("y"), lax.axis_index("c")
    my_chip = 2 * x + y

    def copies():
        out = []
        for a in range(n):
            for k in range(1, N_CHIP):
                px, py = x ^ (k >> 1), y ^ (k & 1)
                out.append(pltpu.make_async_remote_copy(
                    src_ref=srcs[a].at[2 * px + py], dst_ref=outs[a].at[my_chip],
                    send_sem=send_sems.at[(k - 1) * n + a], recv_sem=recv_sems.at[(k - 1) * n + a],
                    device_id=(px, py, c), device_id_type=MESH))
        return out

    def own_in(a):
        return pltpu.make_async_copy(srcs[a].at[my_chip], bounce[a], local_sems.at[a])

    def own_out(a):
        return pltpu.make_async_copy(bounce[a], outs[a].at[my_chip], local_sems.at[a])

    def start():
        _handshake([(x ^ (k >> 1), y ^ (k & 1), c) for k in range(1, N_CHIP)])
        for a in range(n):
            own_in(a).start()
        for cp in copies():
            cp.start()

    def finish():
        for a in range(n):
            own_in(a).wait()
            own_out(a).start()
        for cp in copies():
            cp.wait()
        for a in range(n):
            own_out(a).wait()

    return start, finish


def _chip_exchange_scratch(parts):
    n = len(parts)
    return [pltpu.SemaphoreType.DMA((3 * n,)), pltpu.SemaphoreType.DMA((3 * n,)), pltpu.SemaphoreType.DMA((n,))] \
        + [pltpu.VMEM(p.shape[1:], p.dtype) for p in parts]


def _pair_plan(srcs, outs, send_sems, recv_sems):
    x, y, c = lax.axis_index("x"), lax.axis_index("y"), lax.axis_index("c")

    def copies():
        return [pltpu.make_async_remote_copy(
            src_ref=srcs[a].at[1 - c], dst_ref=outs[a], send_sem=send_sems.at[a], recv_sem=recv_sems.at[a],
            device_id=(x, y, 1 - c), device_id_type=MESH) for a in range(len(srcs))]

    def start():
        _handshake([(x, y, 1 - c)])
        for cp in copies():
            cp.start()

    def finish():
        for cp in copies():
            cp.wait()

    return start, finish


def pair_add_exchange(parts, got, late, name):
    n = len(parts)
    chips = late.shape[1]

    def body(core_ref, *refs):
        late_own, late_all = refs[2 * n:2 * n + 2]
        late_out = refs[3 * n + 2]
        landed, send_sems, recv_sems = refs[3 * n + 3:]
        i = pl.program_id(0)
        start, finish = _pair_plan([late_all], [landed], send_sems, recv_sems)
        pl.when(i == 0)(start)
        for a in range(n):
            refs[2 * n + 2 + a][...] = (refs[a][...].astype(F32) + refs[n + a][...].astype(F32)).astype(BF16)

        @pl.when(i == chips - 1)
        def _():
            finish()
            late_out[...] = (late_own[...].astype(F32) + landed[...].astype(F32)).astype(BF16)

    def own(p):
        zeros = (0,) * (p.ndim - 2)
        return pl.BlockSpec((None, 1, *p.shape[2:]), lambda i, core: (core[0], i, *zeros))

    def plain(p):
        zeros = (0,) * (p.ndim - 1)
        return pl.BlockSpec((1, *p.shape[1:]), lambda i, core: (i, *zeros))

    zeros = (0,) * (late.ndim - 1)
    core = lax.axis_index("c").astype(jnp.int32).reshape(1)
    return pl.pallas_call(
        body, name=name,
        grid_spec=pltpu.PrefetchScalarGridSpec(
            num_scalar_prefetch=1, grid=(chips,),
            in_specs=[own(p) for p in parts] + [plain(p) for p in got]
            + [pl.BlockSpec((None, *late.shape[1:]), lambda i, core: (core[0], *zeros)), ANY],
            out_specs=[plain(p) for p in got] + [pl.BlockSpec(late.shape[1:], lambda i, core: zeros)],
            scratch_shapes=[pltpu.VMEM(late.shape[1:], late.dtype), pltpu.SemaphoreType.DMA((1,)), pltpu.SemaphoreType.DMA((1,))]),
        out_shape=[jax.ShapeDtypeStruct(p.shape, BF16) for p in got] + [jax.ShapeDtypeStruct(late.shape[1:], BF16)],
        compiler_params=_params(collective_id=PAIR_COLLECTIVE_ID),
    )(core, *parts, *got, late, late)


def _adamw(w, g, m, v):
    m2 = ADAM_B1 * m + (1.0 - ADAM_B1) * g
    v2 = ADAM_B2 * v + (1.0 - ADAM_B2) * (g * g)
    m_hat = m2 / (1.0 - ADAM_B1 ** ADAM_STEP)
    v_hat = v2 / (1.0 - ADAM_B2 ** ADAM_STEP)
    delta = -ADAM_LR * (m_hat / (jnp.sqrt(v_hat) + ADAM_EPS) + ADAM_WD * w)
    return delta, m2, v2


def sum_adamw(slots, ws, ms, vs, name):
    n = len(ws)
    halves = 2

    def body(*refs):
        for a in range(n):
            total = refs[a][0].astype(F32)
            for s in range(1, slots[a].shape[0]):
                total = total + refs[a][s].astype(F32)
            delta, m2, v2 = _adamw(refs[n + a][...], total, refs[2 * n + a][...], refs[3 * n + a][...])
            for q, val in enumerate((total, delta, m2, v2)):
                refs[4 * n + 4 * a + q][...] = val

    def rows(w):
        return pl.BlockSpec((w.shape[0] // halves, w.shape[1]), lambda i: (i, 0))

    def slot_rows(p):
        return pl.BlockSpec((p.shape[0], p.shape[1] // halves, p.shape[2]), lambda i: (0, i, 0))

    out = pl.pallas_call(
        body, name=name, grid=(halves,),
        out_shape=[jax.ShapeDtypeStruct(w.shape, F32) for w in ws for _ in range(4)],
        in_specs=[slot_rows(p) for p in slots] + [rows(w) for w in ws] * 3,
        out_specs=[rows(w) for w in ws for _ in range(4)],
        compiler_params=_params(),
    )(*slots, *ws, *ms, *vs)
    return [out[4 * a:4 * a + 4] for a in range(n)]


def adamw_update(ws, gs, ms, vs, name):
    n = len(ws)

    def body(*refs):
        for a in range(n):
            delta, m2, v2 = _adamw(refs[a][...], refs[n + a][...], refs[2 * n + a][...], refs[3 * n + a][...])
            refs[4 * n + 3 * a][...] = delta
            refs[4 * n + 3 * a + 1][...] = m2
            refs[4 * n + 3 * a + 2][...] = v2

    out = pl.pallas_call(
        body, name=name,
        out_shape=[jax.ShapeDtypeStruct(w.shape, F32) for w in ws for _ in range(3)],
        in_specs=[VMEM] * (4 * n), out_specs=[VMEM] * (3 * n),
        compiler_params=_params(),
    )(*ws, *gs, *ms, *vs)
    return [out[3 * a:3 * a + 3] for a in range(n)]


GAIN_ROWS = 8
ROW_POOL_SCALE = 4 * GAIN_ROWS
ROW_SINKS = ROW_POOL_SCALE + 4
ROW_LOSS = ROW_SINKS + 1
ROW_W_POOL = 40
SMALL_ROWS = ROW_W_POOL + 4 * POOL_GROUP_DIM


def grad_w_in_small_reduce(a, b, gains, dpool_scale, dsinks, loss_part, dw_pool, drel_bias):
    t, m = a.shape
    d = b.shape[1]
    r = m // N_DEV
    tt = TOKEN_TILE
    last = t // tt - 1

    def body(a_ref, b_ref, g0, g1, g2, g3, dsc_ref, dsink_ref, loss_ref, dwp_ref, drb_ref, out_ref, total_ref, total_rb_ref,
             acc, stage, gat, gat_rb, g_send, g_recv):
        k = pl.program_id(0)
        x, y, c = lax.axis_index("x"), lax.axis_index("y"), lax.axis_index("c")
        start, finish = _gather_plan([stage, drb_ref], [gat, gat_rb], g_send, g_recv)

        @pl.when(k == 0)
        def _():
            for q, g_ref in enumerate((g0, g1, g2, g3)):
                stage[GAIN_ROWS * q:GAIN_ROWS * (q + 1), :] = g_ref[...]
            stage[ROW_POOL_SCALE:ROW_SINKS, :] = dsc_ref[...]
            stage[ROW_SINKS:ROW_LOSS, :] = dsink_ref[...]
            stage[ROW_LOSS:ROW_LOSS + 1, :] = loss_ref[...]
            stage[ROW_LOSS + 1:ROW_W_POOL, :] = jnp.zeros((ROW_W_POOL - ROW_LOSS - 1, 128), F32)
            stage[ROW_W_POOL:, :] = dwp_ref[...].reshape(4 * POOL_GROUP_DIM, POOL_GROUP_DIM)
            gat[4 * x + 2 * y + c] = stage[...]
            gat_rb[4 * x + 2 * y + c] = drb_ref[...]
            start()
            acc[...] = jnp.zeros_like(acc)

        acc[...] += _dot_tn(a_ref[...], b_ref[...])

        @pl.when(k == last)
        def _():
            blocks = acc[...].reshape(N_CHIP, 2, r, d)
            for chip in range(N_CHIP):
                for core in range(2):
                    out_ref[core, chip] = blocks[chip, core].astype(BF16)
            finish()
            total, total_rb = gat[0], gat_rb[0]
            for s in range(1, N_DEV):
                total, total_rb = total + gat[s], total_rb + gat_rb[s]
            total_ref[...] = total
            total_rb_ref[...] = total_rb

    out_shape = (2, N_CHIP, r, d)
    return pl.pallas_call(
        body, name="grad_w_in", grid=(t // tt,),
        out_shape=[jax.ShapeDtypeStruct(out_shape, BF16), jax.ShapeDtypeStruct((SMALL_ROWS, 128), F32),
                   jax.ShapeDtypeStruct(drel_bias.shape, F32)],
        in_specs=[pl.BlockSpec((tt, m), lambda k: (k, 0)), pl.BlockSpec((tt, d), lambda k: (k, 0))] + [VMEM] * 9,
        out_specs=[pl.BlockSpec(out_shape, lambda k: (0,) * len(out_shape)), VMEM, VMEM],
        scratch_shapes=[pltpu.VMEM((m, d), F32), pltpu.VMEM((SMALL_ROWS, 128), F32),
                        pltpu.VMEM((N_DEV, SMALL_ROWS, 128), F32), pltpu.VMEM((N_DEV, *drel_bias.shape), F32),
                        pltpu.SemaphoreType.DMA((14,)), pltpu.SemaphoreType.DMA((14,))],
        compiler_params=_params(collective_id=GATHER_COLLECTIVE_ID),
    )(a, b, *gains, dpool_scale, dsinks, loss_part, dw_pool, drel_bias)


def reduce_w_in(d_in_t):
    def body(d_in_ref, g_in_ref, pair_got, chip_part, chip_got, p_send, p_recv, x_send, x_recv):
        x, y, c = lax.axis_index("x"), lax.axis_index("y"), lax.axis_index("c")
        my_chip = 2 * x + y
        _handshake([(x, y, 1 - c)] + [(x ^ (k >> 1), y ^ (k & 1), c) for k in range(1, N_CHIP)])
        pair = pltpu.make_async_remote_copy(
            src_ref=d_in_ref.at[1 - c], dst_ref=pair_got, send_sem=p_send, recv_sem=p_recv,
            device_id=(x, y, 1 - c), device_id_type=MESH)
        pair.start()
        pair.wait()
        chip_part[...] = (d_in_ref[c].astype(F32) + pair_got[...].astype(F32)).astype(BF16)
        copies = []
        for k in range(1, N_CHIP):
            px, py = x ^ (k >> 1), y ^ (k & 1)
            copies.append(pltpu.make_async_remote_copy(
                src_ref=chip_part.at[2 * px + py], dst_ref=chip_got.at[my_chip],
                send_sem=x_send.at[k - 1], recv_sem=x_recv.at[k - 1], device_id=(px, py, c), device_id_type=MESH))
        for cp in copies:
            cp.start()
        chip_got[my_chip] = chip_part[my_chip]
        for cp in copies:
            cp.wait()
        g_in = chip_got[0].astype(F32)
        for s in range(1, N_CHIP):
            g_in = g_in + chip_got[s].astype(F32)
        g_in_ref[...] = g_in

    per_core = d_in_t.shape[1:]
    return pl.pallas_call(
        body, name="reduce_w_in",
        out_shape=jax.ShapeDtypeStruct(d_in_t.shape[2:], F32),
        in_specs=[VMEM], out_specs=VMEM,
        scratch_shapes=[pltpu.VMEM(per_core, d_in_t.dtype), pltpu.VMEM(per_core, d_in_t.dtype),
                        pltpu.VMEM(per_core, d_in_t.dtype),
                        pltpu.SemaphoreType.DMA, pltpu.SemaphoreType.DMA,
                        pltpu.SemaphoreType.DMA((3,)), pltpu.SemaphoreType.DMA((3,))],
        compiler_params=_params(collective_id=GATHER_COLLECTIVE_ID),
    )(d_in_t)


def small_adamw(total, total_rb, small_w, small_m, small_v):
    n_small = len(small_w)

    def body(*refs):
        total_ref, rb_ref = refs[:2]
        w_refs, m_refs, v_refs = (refs[2 + k * n_small:2 + (k + 1) * n_small] for k in range(3))
        loss_out = refs[2 + 3 * n_small]
        result = refs[3 + 3 * n_small:]
        total = total_ref[...]
        loss_out[...] = total[ROW_LOSS:ROW_LOSS + 1, :]
        grads = [_as_lanes(total[GAIN_ROWS * k:GAIN_ROWS * (k + 1), :]) for k in range(4)]
        grads.append(_as_lanes(total[ROW_POOL_SCALE:ROW_SINKS, :]))
        grads.append(total[ROW_SINKS:ROW_LOSS, 0:N_Q_HEADS])
        grads.append(total[ROW_W_POOL:, :].reshape(w_refs[6].shape))
        grads.append(rb_ref[...])
        for k in range(n_small):
            delta, m2, v2 = _adamw(w_refs[k][...], grads[k], m_refs[k][...], v_refs[k][...])
            result[4 * k][...] = grads[k]
            result[4 * k + 1][...] = delta
            result[4 * k + 2][...] = m2
            result[4 * k + 3][...] = v2

    out = pl.pallas_call(
        body, name="small_adamw",
        out_shape=[jax.ShapeDtypeStruct((1, 128), F32)] + [jax.ShapeDtypeStruct(w.shape, F32) for w in small_w for _ in range(4)],
        in_specs=[VMEM] * (2 + 3 * n_small), out_specs=[VMEM] * (1 + 4 * n_small),
        compiler_params=_params(),
    )(total, total_rb, *small_w, *small_m, *small_v)
    return out[0], [out[1 + 4 * k:5 + 4 * k] for k in range(n_small)]


def norm_inproj(x, g, w_shard, shard, shard_rows):
    t, d = x.shape
    r = w_shard.shape[0]
    tm = TOKEN_TILE
    nt = t // tm

    def body(x_ref, g_ref, w_shard_ref, shard_ref, proj_ref, h_ref, w_ref, gathered_ref, h_all, w_all, w_sem,
             send_w, recv_w, local_w, bounce_w, send_sems, recv_sems, local_sems, bounce):
        i = pl.program_id(0)
        start_w, finish_w = _gather_plan([w_shard_ref], [w_ref], send_w, recv_w, local_w, [bounce_w])
        start, finish = _gather_plan([shard_ref], [gathered_ref], send_sems, recv_sems, local_sems, [bounce],
                                     [(0, shard.shape[0])], shake=False)

        @pl.when(i == 0)
        def _():
            start_w()
            start()

        @pl.when(i < nt)
        def _():
            xv = x_ref[...]
            h = ((xv * _rstd(xv)) * g_ref[...]).astype(BF16)
            h_ref[...] = h
            h_all[pl.ds(pl.multiple_of(i * tm, tm), tm), :] = h

        @pl.when(i == nt - 1)
        def _():
            finish_w()
            landed = pltpu.make_async_copy(w_ref, w_all, w_sem)
            landed.start()
            landed.wait()

        @pl.when(i >= nt)
        def _():
            rows = pl.ds(pl.multiple_of((i - nt) * tm, tm), tm)
            proj_ref[...] = _dot_nt(h_all[rows, :], _merge_rows(w_all[...]))

        pl.when(i == 2 * nt - 2)(finish.forward)
        pl.when(i == 2 * nt - 1)(finish.complete)

    first = lambda i: (jnp.minimum(i, nt - 1), 0)
    return pl.pallas_call(
        body, name="norm_inproj", grid=(2 * nt,),
        out_shape=[jax.ShapeDtypeStruct((t, N_DEV * r), F32), jax.ShapeDtypeStruct((t, d), BF16),
                   jax.ShapeDtypeStruct((N_DEV, r, d), w_shard.dtype),
                   jax.ShapeDtypeStruct((N_DEV, shard_rows, d), shard.dtype)],
        in_specs=[pl.BlockSpec((tm, d), first), pl.BlockSpec((1, d), lambda i: (0, 0)), ANY, ANY],
        out_specs=[pl.BlockSpec((tm, N_DEV * r), lambda i: (jnp.maximum(i - nt, 0), 0)), pl.BlockSpec((tm, d), first),
                   ANY, ANY],
        scratch_shapes=[pltpu.VMEM((t, d), BF16), pltpu.VMEM((N_DEV, r, d), w_shard.dtype), pltpu.SemaphoreType.DMA]
        + _gather_scratch([w_shard]) + _gather_scratch([shard]),
        compiler_params=_params(collective_id=GATHER_COLLECTIVE_ID),
    )(x, g, w_shard, shard)


def _fill_bias_band(bk_ref, win_ref, rb_ref, biasm_ref):
    bk = bk_ref[...]
    keep = win_ref[...] > 0.5
    for h in range(N_Q_HEADS):
        acc = jnp.zeros(bk.shape, F32)
        for b in range(N_BUCKETS):
            acc = jnp.where(bk == float(b), rb_ref[b, h], acc)
        biasm_ref[h] = jnp.where(keep, acc, NEG_INF)


def _window_sum(buf_ref, g, w, first):
    cols = slice(g * POOL_GROUP_DIM, (g + 1) * POOL_GROUP_DIM)
    acc = None
    for k in range(w):
        piece = buf_ref[first(k):first(k) + BLOCK, cols]
        acc = piece if acc is None else acc + piece
    return acc


def _inv_count(i, w):
    row = lax.broadcasted_iota(jnp.int32, (BLOCK, 1), 0)
    return 1.0 / jnp.minimum(i * BLOCK + row + 1, w).astype(F32)


def _fill_pool_input(i, ubuf, uc_ref, halo_ref):
    ubuf[0:HALO, :] = jnp.where(i > 0, halo_ref[...], 0.0)
    ubuf[HALO:, :] = uc_ref[...]


def _pooled(i, g, w, ubuf):
    cols = slice(g * POOL_GROUP_DIM, (g + 1) * POOL_GROUP_DIM)
    return _window_sum(ubuf, g, w, lambda k: HALO - k) * _inv_count(i, w) - ubuf[HALO:, cols]


def _head_variants(pair):
    low = lax.broadcasted_iota(jnp.int32, pair.shape, 1) < HEAD_DIM
    swapped = pltpu.roll(pair, HEAD_DIM, 1)
    zero = jnp.zeros_like(pair)
    pick = lambda c, a, b: jnp.where(c, a, b).astype(BF16)
    return [[pick(low, pair, zero), pick(low, zero, swapped)], [pick(low, swapped, zero), pick(low, zero, pair)]]


def _head_probs(i, hq, rows, s_ref, biasm_ref, sinks_ref):
    s = s_ref[hq, rows, :] * ATTN_SCALE + biasm_ref[hq, rows, :]
    col = lax.broadcasted_iota(jnp.int32, s.shape, 1)
    s = jnp.where((i == 0) & (col < BLOCK), NEG_INF, s)
    sink = sinks_ref[0, hq]
    m = jnp.maximum(jnp.max(s, axis=-1, keepdims=True), sink)
    p = jnp.exp(s - m)
    e_sink = jnp.exp(sink - m)
    inv = 1.0 / (jnp.sum(p, axis=-1, keepdims=True) + e_sink)
    return p * inv, e_sink * inv


def _head_slot(hq):
    return 4 * (hq // GQA_GROUP) + 2 * (hq % 2) + (hq % GQA_GROUP) // 2


def _mixer_in_specs(cur, prv):
    return [pl.BlockSpec((BLOCK, 512), lambda i: (cur(i), 0)),
            pl.BlockSpec((HALO, 512), lambda i: (jnp.maximum(cur(i) * (BLOCK // HALO) - 1, 0), 0)),
            pl.BlockSpec((BLOCK, 512), lambda i: (cur(i), 1)),
            pl.BlockSpec((BLOCK, 256), lambda i: (cur(i), 4)),
            pl.BlockSpec((BLOCK, 256), lambda i: (prv(i), 4))]


def _mixer_param_specs():
    return [pl.BlockSpec((4, POOL_GROUP_DIM, POOL_GROUP_DIM), lambda i: (0, 0, 0)),
            pl.BlockSpec((1, POOL_WIDTH), lambda i: (0, 0))]


def mixers_fwd(proj, bucket, in_window, rel_bias, sinks, w_pool, pool_scale, shards):
    t = proj.shape[0]
    nb = t // BLOCK
    ns = len(shards)

    def body(*refs):
        uc_ref, halo_ref, q_ref, kvc_ref, kvp_ref, bk_ref, win_ref, rb_ref, sinks_ref, wp_ref, sc_ref = refs[:11]
        shard_refs = refs[11:11 + ns]
        out_ref, pooled_ref, p_all, psink_ref = refs[11 + ns:15 + ns]
        gathered_refs = refs[15 + ns:15 + 2 * ns]
        ubuf, s_all, biasm_ref, send_sems, recv_sems, local_sems = refs[15 + 2 * ns:21 + 2 * ns]
        i = pl.program_id(0)
        start, finish = _gather_plan(shard_refs, gathered_refs, send_sems, recv_sems, local_sems, refs[21 + 2 * ns:])

        @pl.when(i == 0)
        def _():
            start()
            _fill_bias_band(bk_ref, win_ref, rb_ref, biasm_ref)

        _fill_pool_input(i, ubuf, uc_ref, halo_ref)
        for g, w in enumerate(POOL_WINDOWS):
            cols = slice(g * POOL_GROUP_DIM, (g + 1) * POOL_GROUP_DIM)
            pooled = _pooled(i, g, w, ubuf).astype(BF16)
            pooled_ref[:, cols] = pooled
            out_ref[:, cols] = (_dot(pooled, wp_ref[g]) * sc_ref[:, cols]).astype(BF16)
        kv = jnp.concatenate([kvp_ref[...], kvc_ref[...]], axis=0)
        k_var = _head_variants(kv[:, 0:2 * HEAD_DIM])
        v_var = _head_variants(kv[:, 2 * HEAD_DIM:])
        for hq in range(N_Q_HEADS):
            j, half, h = hq // 2, hq % 2, hq // GQA_GROUP
            q2 = q_ref[:, 2 * HEAD_DIM * j:2 * HEAD_DIM * (j + 1)].astype(BF16)
            s_all[hq] = _dot_nt(q2, k_var[h][half])
        psink_ref[...] = jnp.zeros_like(psink_ref)
        for hq in range(N_Q_HEADS):
            for r in range(0, BLOCK, ROW_CHUNK):
                rows = slice(r, r + ROW_CHUNK)
                probs, p_sink = _head_probs(i, hq, rows, s_all, biasm_ref, sinks_ref)
                p_all[_head_slot(hq), rows, :] = probs.astype(BF16)
                psink_ref[rows, hq:hq + 1] = p_sink
        for j in range(N_Q_HEADS // 2):
            h = 2 * j // GQA_GROUP
            acc = _dot(p_all[_head_slot(2 * j)], v_var[h][0]) + _dot(p_all[_head_slot(2 * j + 1)], v_var[h][1])
            out_ref[:, POOL_WIDTH + 2 * HEAD_DIM * j:POOL_WIDTH + 2 * HEAD_DIM * (j + 1)] = acc.astype(BF16)

        pl.when(i == max(nb - 4, 0))(finish.forward)
        pl.when(i == nb - 1)(finish.complete)

    return pl.pallas_call(
        body, name="mixers_fwd", grid=(nb,),
        out_shape=[jax.ShapeDtypeStruct((t, 2 * POOL_WIDTH), BF16), jax.ShapeDtypeStruct((t, POOL_WIDTH), BF16),
                   jax.ShapeDtypeStruct((N_Q_HEADS, t, 2 * BLOCK), BF16), jax.ShapeDtypeStruct((t, 128), F32)]
        + [jax.ShapeDtypeStruct((N_DEV, *sh.shape), sh.dtype) for sh in shards],
        in_specs=_mixer_in_specs(lambda i: i, lambda i: jnp.maximum(i - 1, 0))
        + [pl.BlockSpec((BLOCK, 2 * BLOCK), lambda i: (0, 0))] * 2 + [SMEM, SMEM] + _mixer_param_specs() + [ANY] * ns,
        out_specs=[pl.BlockSpec((BLOCK, 2 * POOL_WIDTH), lambda i: (i, 0)), pl.BlockSpec((BLOCK, POOL_WIDTH), lambda i: (i, 0)),
                   pl.BlockSpec((N_Q_HEADS, BLOCK, 2 * BLOCK), lambda i: (0, i, 0)), pl.BlockSpec((BLOCK, 128), lambda i: (i, 0))]
        + [ANY] * ns,
        scratch_shapes=[pltpu.VMEM((HALO + BLOCK, POOL_WIDTH), F32), pltpu.VMEM((N_Q_HEADS, BLOCK, 2 * BLOCK), F32),
                        pltpu.VMEM((N_Q_HEADS, BLOCK, 2 * BLOCK), F32)]
        + _gather_scratch(shards),
        compiler_params=_params(collective_id=GATHER_COLLECTIVE_ID),
    )(proj, proj, proj, proj, proj, bucket, in_window, rel_bias, sinks, w_pool, pool_scale, *shards)


def outproj_norm(cat, w, x, g, g_next, shard, partial):
    t, d = x.shape
    tm = TOKEN_TILE
    last = t // tm - 1
    rows = [(partial.shape[1] - shard.shape[0], shard.shape[0])]

    def body(c_ref, w_ref, x_ref, g_ref, gn_ref, shard_ref, partial_ref, mix_ref, x1_ref, h2_ref, gathered_ref,
             send_sems, recv_sems, local_sems, bounce):
        i = pl.program_id(0)
        start, finish = _gather_plan([shard_ref], [gathered_ref], send_sems, recv_sems, local_sems, [bounce], rows)
        pl.when(i == 0)(start)
        mix = _dot(c_ref[...], w_ref[...])
        mix_ref[...] = mix
        x1 = x_ref[...] + (mix * _rstd(mix)) * g_ref[...]
        x1_ref[...] = x1
        h2_ref[...] = ((x1 * _rstd(x1)) * gn_ref[...]).astype(BF16)
        pl.when(i == max(last - 1, 0))(finish.forward)
        pl.when(i == last)(finish.complete)

    row = pl.BlockSpec((tm, d), lambda i: (i, 0))
    gain = pl.BlockSpec((1, d), lambda i: (0, 0))
    return pl.pallas_call(
        body, name="outproj_norm", grid=(t // tm,),
        out_shape=[jax.ShapeDtypeStruct((t, d), F32), jax.ShapeDtypeStruct((t, d), F32), jax.ShapeDtypeStruct((t, d), BF16),
                   jax.ShapeDtypeStruct(partial.shape, partial.dtype)],
        in_specs=[pl.BlockSpec((tm, cat.shape[1]), lambda i: (i, 0)), pl.BlockSpec(w.shape, lambda i: (0, 0)), row, gain, gain,
                  ANY, ANY],
        out_specs=[row, row, row, ANY],
        input_output_aliases={6: 3},
        scratch_shapes=_gather_scratch([shard]),
        compiler_params=_params(collective_id=GATHER_COLLECTIVE_ID),
    )(cat, w, x, g, g_next, shard, partial)


def ffn_up(h, gate_t, up_t, down_shard):
    t, d = h.shape
    n = gate_t.shape[1]
    f = N_DEV * n
    tm, ts = FFN_TOKEN_TILE, FF_SHARDS_PER_TILE
    tn = ts * n
    steps = (f // tn, t // tm)

    def body(h_ref, wg_ref, wu_ref, shard_ref, gate_ref, up_ref, a_ref, gathered_ref,
             send_sems, recv_sems, local_sems, bounce):
        j, i = pl.program_id(0), pl.program_id(1)
        start, finish = _gather_plan([shard_ref], [gathered_ref], send_sems, recv_sems, local_sems, [bounce])
        pl.when((i == 0) & (j == 0))(start)

        hv = h_ref[...]
        gate = _dot_nt(hv, _merge_rows(wg_ref[...]))
        up = _dot_nt(hv, _merge_rows(wu_ref[...]))
        gate_ref[...] = gate.astype(BF16)
        up_ref[...] = up.astype(BF16)
        a_ref[...] = (gate * (1.0 / (1.0 + jnp.exp(-gate))) * up).astype(BF16)

        pl.when((j == steps[0] - 1) & (i == max(steps[1] - 2, 0)))(finish.forward)
        pl.when((j == steps[0] - 1) & (i == steps[1] - 1))(finish.complete)

    wide = pl.BlockSpec((tm, tn), lambda j, i: (i, j))
    return pl.pallas_call(
        body, name="ffn_up", grid=steps,
        out_shape=[jax.ShapeDtypeStruct((t, f), BF16)] * 3
        + [jax.ShapeDtypeStruct((N_DEV, *down_shard.shape), down_shard.dtype)],
        in_specs=[pl.BlockSpec((tm, d), lambda j, i: (i, 0)),
                  pl.BlockSpec((ts, n, d), lambda j, i: (j, 0, 0)),
                  pl.BlockSpec((ts, n, d), lambda j, i: (j, 0, 0)), ANY],
        out_specs=[wide, wide, wide, ANY],
        scratch_shapes=_gather_scratch([down_shard]),
        compiler_params=_params(collective_id=GATHER_COLLECTIVE_ID),
    )(h, gate_t, up_t, down_shard)


def ffn_down_loss(a, w_down, x1, g, target):
    t, d = x1.shape
    tm = WIDE_K_TOKEN_TILE

    def body(a_ref, w_ref, x_ref, g_ref, t_ref, df_ref, dy_ref, dg_ref, loss_ref):
        @pl.when(pl.program_id(0) == 0)
        def _():
            dg_ref[...] = jnp.zeros_like(dg_ref)
            loss_ref[...] = jnp.zeros_like(loss_ref)

        f = _dot(a_ref[...], _merge_rows(w_ref[...]))
        r = _rstd(f)
        g = g_ref[...]
        err = x_ref[...] + (f * r) * g - t_ref[...]
        loss_ref[...] += 0.5 * jnp.sum(jnp.mean(err * err, axis=-1, keepdims=True))
        dy = err * (1.0 / d)
        dy_ref[...] = dy
        df, dg_rows = _norm_bwd(dy, f, r, g)
        df_ref[...] = df.astype(BF16)
        dg_ref[...] += _as_rows(jnp.sum(dg_rows, axis=0, keepdims=True))

    row = pl.BlockSpec((tm, d), lambda i: (i, 0))
    gain = pl.BlockSpec((1, d), lambda i: (0, 0))
    return pl.pallas_call(
        body, name="ffn_down_loss", grid=(t // tm,),
        out_shape=[jax.ShapeDtypeStruct((t, d), BF16), jax.ShapeDtypeStruct((t, d), F32),
                   jax.ShapeDtypeStruct((d // 128, 128), F32), jax.ShapeDtypeStruct((1, 128), F32)],
        in_specs=[pl.BlockSpec((tm, a.shape[1]), lambda i: (i, 0)), pl.BlockSpec(w_down.shape, lambda i: (0, 0, 0)), row, gain, row],
        out_specs=[row, row, pl.BlockSpec((d // 128, 128), lambda i: (0, 0)), pl.BlockSpec((1, 128), lambda i: (0, 0))],
        compiler_params=_params(),
    )(a, w_down, x1, g, target)


def ffn_down_bwd(df, w_down, gate, up, h):
    t, d = df.shape
    n = w_down.shape[1]
    f = gate.shape[1]
    tm, ts = WIDE_K_TOKEN_TILE, FF_SHARDS_PER_TILE
    tn = ts * n
    last = t // tm - 1

    def body(df_ref, w_ref, gate_ref, up_ref, h_ref, dgate_ref, dup_ref, dwg_ref, dwu_ref, acc):
        i = pl.program_id(1)

        @pl.when(i == 0)
        def _():
            acc[...] = jnp.zeros_like(acc)

        da = _dot_nt(df_ref[...], _merge_rows(w_ref[...]))
        gate = gate_ref[...].astype(F32)
        sig = 1.0 / (1.0 + jnp.exp(-gate))
        dgate = (da * up_ref[...].astype(F32) * (sig * (1.0 + gate * (1.0 - sig)))).astype(BF16)
        dup = (da * (gate * sig)).astype(BF16)
        dgate_ref[...] = dgate
        dup_ref[...] = dup
        acc[0] += _dot_tn(dgate, h_ref[...])
        acc[1] += _dot_tn(dup, h_ref[...])

        @pl.when(i == last)
        def _():
            for w, out_ref in enumerate((dwg_ref, dwu_ref)):
                blocks = acc[w].reshape(ts // 2, 2, n, d)
                for chip in range(ts // 2):
                    for core in range(2):
                        out_ref[core, chip] = blocks[chip, core].astype(BF16)

    wide = pl.BlockSpec((tm, tn), lambda j, i: (i, j))
    rows = pl.BlockSpec((tm, d), lambda j, i: (i, 0))
    owned = pl.BlockSpec((2, ts // 2, n, d), lambda j, i: (0, j, 0, 0))
    out = pl.pallas_call(
        body, name="ffn_down_bwd", grid=(f // tn, t // tm),
        out_shape=[jax.ShapeDtypeStruct((t, f), BF16)] * 2 + [jax.ShapeDtypeStruct((2, N_CHIP, n, d), BF16)] * 2,
        in_specs=[rows, pl.BlockSpec((ts, n, d), lambda j, i: (j, 0, 0)), wide, wide, rows],
        out_specs=[wide, wide, owned, owned],
        scratch_shapes=[pltpu.VMEM((2, tn, d), F32)],
        compiler_params=_params(),
    )(df, w_down, gate, up, h)
    return out[:2], out[2:]


def grad_w_down(a, b, pair_parts):
    t, f = a.shape
    d = b.shape[1]
    na = len(pair_parts)
    n = f // N_DEV
    tt, ts = TOKEN_TILE, FF_SHARDS_PER_TILE
    tn = ts * n
    steps = (f // tn, t // tt)

    def body(a_ref, b_ref, *refs):
        part_refs, out_ref, got_refs = refs[:na], refs[na], refs[na + 1:2 * na + 1]
        acc, send_sems, recv_sems = refs[2 * na + 1:]
        i, k = pl.program_id(0), pl.program_id(1)
        start, finish = _pair_plan(part_refs, got_refs, send_sems, recv_sems)
        pl.when((i == 0) & (k == 0))(start)

        @pl.when(k == 0)
        def _():
            acc[...] = jnp.zeros_like(acc)

        acc[...] += _dot_tn(a_ref[...], b_ref[...])

        @pl.when(k == steps[1] - 1)
        def _():
            blocks = acc[...].reshape(ts // 2, 2, n, d)
            for chip in range(ts // 2):
                for core in range(2):
                    out_ref[core, chip] = blocks[chip, core].astype(BF16)

        pl.when((i == steps[0] - 1) & (k == steps[1] - 1))(finish)

    out = pl.pallas_call(
        body, name="grad_w_down", grid=steps,
        out_shape=[jax.ShapeDtypeStruct((2, N_CHIP, n, d), BF16)]
        + [jax.ShapeDtypeStruct(p.shape[1:], p.dtype) for p in pair_parts],
        in_specs=[pl.BlockSpec((tt, tn), lambda i, k: (k, i)), pl.BlockSpec((tt, d), lambda i, k: (k, 0))] + [ANY] * na,
        out_specs=[pl.BlockSpec((2, ts // 2, n, d), lambda i, k: (0, i, 0, 0))] + [ANY] * na,
        scratch_shapes=[pltpu.VMEM((tn, d), F32), pltpu.SemaphoreType.DMA((na,)), pltpu.SemaphoreType.DMA((na,))],
        compiler_params=_params(collective_id=PAIR_COLLECTIVE_ID),
    )(a, b, *pair_parts)
    return out[0], out[1:]


def ffn_up_bwd(dgate, dup, gate_t, up_t, x1, g_ffn, dy, mix, g_mix, chip_parts):
    t, d = x1.shape
    n = gate_t.shape[1]
    f = N_DEV * n
    tm = WIDE_K_TOKEN_TILE
    na = len(chip_parts)
    last = t // tm - 1

    def body(*refs):
        dg_ref, du_ref, wg_ref, wu_ref, x_ref, gf_ref, dy_ref, mix_ref, gm_ref = refs[:9]
        part_refs = refs[9:9 + na]
        dx1_ref, dmix_ref, dgf_ref, dgm_ref = refs[9 + na:13 + na]
        slot_refs = refs[13 + na:13 + 2 * na]
        send_sems, recv_sems, local_sems = refs[13 + 2 * na:16 + 2 * na]
        i = pl.program_id(0)
        start, finish = _chip_exchange_plan(part_refs, slot_refs, send_sems, recv_sems, local_sems, refs[16 + 2 * na:])

        @pl.when(i == 0)
        def _():
            start()
            dgf_ref[...] = jnp.zeros_like(dgf_ref)
            dgm_ref[...] = jnp.zeros_like(dgm_ref)

        dh = _dot(dg_ref[...], _merge_rows(wg_ref[...])) + _dot(du_ref[...], _merge_rows(wu_ref[...]))
        x1 = x_ref[...]
        dx, dgf_rows = _norm_bwd(dh, x1, _rstd(x1), gf_ref[...])
        dx1 = dy_ref[...] + dx
        dx1_ref[...] = dx1
        dgf_ref[...] += _as_rows(jnp.sum(dgf_rows, axis=0, keepdims=True))
        mix = mix_ref[...]
        dmix, dgm_rows = _norm_bwd(dx1, mix, _rstd(mix), gm_ref[...])
        dmix_ref[...] = dmix.astype(BF16)
        dgm_ref[...] += _as_rows(jnp.sum(dgm_rows, axis=0, keepdims=True))
        pl.when(i == last)(finish)

    row = pl.BlockSpec((tm, d), lambda i: (i, 0))
    wide = pl.BlockSpec((tm, f), lambda i: (i, 0))
    gain = pl.BlockSpec((1, d), lambda i: (0, 0))
    gain_rows = pl.BlockSpec((d // 128, 128), lambda i: (0, 0))
    whole = pl.BlockSpec((N_DEV, n, d), lambda i: (0, 0, 0), pipeline_mode=pl.Buffered(1))
    out = pl.pallas_call(
        body, name="ffn_up_bwd", grid=(t // tm,),
        out_shape=[jax.ShapeDtypeStruct((t, d), F32), jax.ShapeDtypeStruct((t, d), BF16),
                   jax.ShapeDtypeStruct((d // 128, 128), F32), jax.ShapeDtypeStruct((d // 128, 128), F32)]
        + [jax.ShapeDtypeStruct(p.shape, p.dtype) for p in chip_parts],
        in_specs=[wide, wide, whole, whole, row, gain, row, row, gain] + [ANY] * na,
        out_specs=[row, row, gain_rows, gain_rows] + [ANY] * na,
        scratch_shapes=_chip_exchange_scratch(chip_parts),
        compiler_params=_params(collective_id=CHIP_COLLECTIVE_ID),
    )(dgate, dup, gate_t, up_t, x1, g_ffn, dy, mix, g_mix, *chip_parts)
    return out[:4], out[4:]


def outproj_bwd(dmix, w_out, cat):
    t, d = dmix.shape
    m = w_out.shape[0]
    r = m // N_DEV
    tm = TOKEN_TILE
    last = t // tm - 1

    def body(dm_ref, w_ref, cat_ref, dcat_ref, dw_ref, acc, stage, landed, send_sems, recv_sems):
        k = pl.program_id(0)

        @pl.when(k == 0)
        def _():
            acc[...] = jnp.zeros_like(acc)

        dcat_ref[...] = _dot_nt(dm_ref[...], w_ref[...])
        acc[...] += _dot_tn(cat_ref[...], dm_ref[...])

        @pl.when(k == last)
        def _():
            blocks = acc[...].reshape(N_CHIP, 2, r, d)
            for chip in range(N_CHIP):
                for core in range(2):
                    stage[core, chip] = blocks[chip, core].astype(BF16)
            start, finish = _pair_plan([stage], [landed], send_sems, recv_sems)
            start()
            finish()
            mine = stage[lax.axis_index("c")]
            dw_ref[...] = (mine.astype(F32) + landed[...].astype(F32)).astype(BF16)

    tile = lambda width: pl.BlockSpec((tm, width), lambda k: (k, 0))
    return pl.pallas_call(
        body, name="outproj_bwd", grid=(t // tm,),
        out_shape=[jax.ShapeDtypeStruct((t, m), F32), jax.ShapeDtypeStruct((N_CHIP, r, d), BF16)],
        in_specs=[tile(d), pl.BlockSpec(w_out.shape, lambda k: (0, 0)), tile(m)],
        out_specs=[tile(m), pl.BlockSpec((N_CHIP, r, d), lambda k: (0, 0, 0))],
        scratch_shapes=[pltpu.VMEM((m, d), F32), pltpu.VMEM((2, N_CHIP, r, d), BF16), pltpu.VMEM((N_CHIP, r, d), BF16),
                        pltpu.SemaphoreType.DMA((1,)), pltpu.SemaphoreType.DMA((1,))],
        compiler_params=_params(collective_id=PAIR_COLLECTIVE_ID),
    )(dmix, w_out, cat)


def mixers_bwd(proj, dcat, pooled, probs, p_sinks, w_pool, pool_scale, bucket, ffn_parts):
    t = proj.shape[0]
    nb = t // BLOCK
    na = len(ffn_parts)

    def body(*refs):
        (q_ref, kvc_ref, kvp_ref, dcat_ref, pooled_ref, p_all, psink_ref, wp_ref, sc_ref, bk_ref) = refs[:10]
        part_refs = refs[10:10 + na]
        dproj_ref, dbias_ref, dsink_ref, dwp_ref, dsc_ref, drb_ref = refs[10 + na:16 + na]
        slot_refs = refs[16 + na:16 + 2 * na]
        dbuf, c_u, c_q, c_kv, dp_all, ds_all, sink_acc = refs[16 + 2 * na:23 + 2 * na]
        send_sems, recv_sems, local_sems = refs[23 + 2 * na:26 + 2 * na]
        bounce = refs[26 + 2 * na:]
        i = pl.program_id(0)
        lane = lax.broadcasted_iota(jnp.int32, (1, 128), 1)
        start, finish = _chip_exchange_plan(part_refs, slot_refs, send_sems, recv_sems, local_sems, bounce)

        @pl.when(i == 0)
        def _():
            start()
            dbias_ref[...] = jnp.zeros_like(dbias_ref)
            dwp_ref[...] = jnp.zeros_like(dwp_ref)
            dsc_ref[...] = jnp.zeros_like(dsc_ref)
            dsink_ref[...] = jnp.zeros_like(dsink_ref)
            dbuf[...] = jnp.zeros_like(dbuf)
            c_u[...] = jnp.zeros_like(c_u)
            c_q[...] = jnp.zeros_like(c_q)
            c_kv[...] = jnp.zeros_like(c_kv)

        @pl.when(i < nb)
        def _():
            for g, w in enumerate(POOL_WINDOWS):
                cols = slice(g * POOL_GROUP_DIM, (g + 1) * POOL_GROUP_DIM)
                pooled = pooled_ref[:, cols]
                mixed = _dot(pooled, wp_ref[g])
                dout = dcat_ref[:, cols]
                dsc_ref[g:g + 1, :] += jnp.sum(dout * mixed, axis=0, keepdims=True)
                dmixed = (dout * sc_ref[:, cols]).astype(BF16)
                dwp_ref[g] += _dot_tn(pooled, dmixed)
                dpooled = _dot_nt(dmixed, wp_ref[g])
                scaled = dpooled * _inv_count(i, w)
                dbuf[BLOCK:, cols] = scaled[0:HALO]
                dproj_ref[:, cols] = (_window_sum(dbuf, g, w, lambda k: k) + c_u[:, cols]).astype(BF16)
                dbuf[0:BLOCK, cols] = scaled
                c_u[:, cols] = -dpooled

            kv = jnp.concatenate([kvp_ref[...], kvc_ref[...]], axis=0)
            k_var = _head_variants(kv[:, 0:2 * HEAD_DIM])
            v_var = _head_variants(kv[:, 2 * HEAD_DIM:])
            q2s = [q_ref[:, 2 * HEAD_DIM * j:2 * HEAD_DIM * (j + 1)].astype(BF16) for j in range(N_Q_HEADS // 2)]
            do2s = [dcat_ref[:, POOL_WIDTH + 2 * HEAD_DIM * j:POOL_WIDTH + 2 * HEAD_DIM * (j + 1)].astype(BF16)
                    for j in range(N_Q_HEADS // 2)]
            slot = _head_slot
            for hq in range(N_Q_HEADS):
                j, half, h = hq // 2, hq % 2, hq // GQA_GROUP
                dp_all[hq] = _dot_nt(do2s[j], v_var[h][half])
            sink_acc[...] = jnp.zeros_like(sink_acc)
            for hq in range(N_Q_HEADS):
                for r in range(0, BLOCK, ROW_CHUNK):
                    rows = slice(r, r + ROW_CHUNK)
                    probs = p_all[slot(hq), rows, :].astype(F32)
                    dp = dp_all[hq, rows, :]
                    delta = jnp.sum(probs * dp, axis=-1, keepdims=True)
                    ds = probs * (dp - delta)
                    dbias_ref[hq, rows, :] += ds
                    sink_acc[rows, :] += jnp.where(lane == hq, psink_ref[rows, :], 0.0) * delta
                    ds_all[slot(hq), rows, :] = (ds * ATTN_SCALE).astype(BF16)
            dsink_ref[...] -= jnp.sum(sink_acc[...], axis=0, keepdims=True)
            dq2 = [None] * (N_Q_HEADS // 2)
            for hq in range(N_Q_HEADS):
                j, half, h = hq // 2, hq % 2, hq // GQA_GROUP
                dq = _dot(ds_all[slot(hq)], k_var[h][half])
                dq2[j] = dq if dq2[j] is None else dq2[j] + dq
            low = lax.broadcasted_iota(jnp.int32, (2 * BLOCK, 2 * HEAD_DIM), 1) < HEAD_DIM
            dk_half, dv_half = [[None, None], [None, None]], [[None, None], [None, None]]
            for h in range(N_KV_HEADS):
                for half in range(2):
                    heads = [hq for hq in range(GQA_GROUP * h, GQA_GROUP * (h + 1)) if hq % 2 == half]
                    base = slot(heads[0])
                    q_rows = jnp.concatenate([q2s[hq // 2] for hq in heads], axis=0)
                    do_rows = jnp.concatenate([do2s[hq // 2] for hq in heads], axis=0)
                    dk_half[h][half] = _dot_tn(_merge_rows(ds_all[base:base + 2]), q_rows)
                    dv_half[h][half] = _dot_tn(_merge_rows(p_all[base:base + 2]), do_rows)

            def pair_of(halves):
                return jnp.where(low, halves[0][0] + pltpu.roll(halves[0][1], HEAD_DIM, 1),
                                 halves[1][1] + pltpu.roll(halves[1][0], HEAD_DIM, 1))

            dkv = jnp.concatenate([pair_of(dk_half), pair_of(dv_half)], axis=1)
            dproj_ref[:, POOL_WIDTH:2 * POOL_WIDTH] = c_q[...].astype(BF16)
            dproj_ref[:, 2 * POOL_WIDTH:] = (c_kv[...] + dkv[0:BLOCK]).astype(BF16)
            c_q[...] = jnp.concatenate(dq2, axis=1)
            c_kv[...] = dkv[BLOCK:]

        @pl.when(i == nb)
        def _():
            dbuf[BLOCK:, :] = jnp.zeros((HALO, POOL_WIDTH), F32)
            for g, w in enumerate(POOL_WINDOWS):
                cols = slice(g * POOL_GROUP_DIM, (g + 1) * POOL_GROUP_DIM)
                dproj_ref[:, cols] = (_window_sum(dbuf, g, w, lambda k: k) + c_u[:, cols]).astype(BF16)
            dproj_ref[:, POOL_WIDTH:2 * POOL_WIDTH] = c_q[...].astype(BF16)
            dproj_ref[:, 2 * POOL_WIDTH:] = c_kv[...].astype(BF16)
            bk = bk_ref[...]
            for h in range(N_Q_HEADS):
                db = dbias_ref[h]
                for b in range(N_BUCKETS):
                    drb_ref[h, b] = jnp.sum(jnp.where(bk == float(b), db, 0.0))
            finish()

    cur = lambda i: jnp.minimum(i, nb - 1)
    prv = lambda i: jnp.maximum(jnp.minimum(i, nb - 1) - 1, 0)
    out = pl.pallas_call(
        body, name="mixers_bwd", grid=(nb + 1,),
        out_shape=[jax.ShapeDtypeStruct((t, proj.shape[1]), BF16),
                   jax.ShapeDtypeStruct((N_Q_HEADS, BLOCK, 2 * BLOCK), F32),
                   jax.ShapeDtypeStruct((1, 128), F32),
                   jax.ShapeDtypeStruct((4, POOL_GROUP_DIM, POOL_GROUP_DIM), F32),
                   jax.ShapeDtypeStruct((len(POOL_WINDOWS), POOL_GROUP_DIM), F32),
                   jax.ShapeDtypeStruct((N_Q_HEADS, N_BUCKETS), F32)]
        + [jax.ShapeDtypeStruct(p.shape, p.dtype) for p in ffn_parts],
        in_specs=_mixer_in_specs(cur, prv)[2:]
        + [pl.BlockSpec((BLOCK, 2 * POOL_WIDTH), lambda i: (cur(i), 0)), pl.BlockSpec((BLOCK, POOL_WIDTH), lambda i: (cur(i), 0)),
           pl.BlockSpec((N_Q_HEADS, BLOCK, 2 * BLOCK), lambda i: (0, cur(i), 0)), pl.BlockSpec((BLOCK, 128), lambda i: (cur(i), 0))]
        + _mixer_param_specs() + [pl.BlockSpec((BLOCK, 2 * BLOCK), lambda i: (0, 0))] + [ANY] * na,
        out_specs=[pl.BlockSpec((BLOCK, proj.shape[1]), lambda i: (jnp.maximum(i - 1, 0), 0)),
                   pl.BlockSpec((N_Q_HEADS, BLOCK, 2 * BLOCK), lambda i: (0, 0, 0)),
                   pl.BlockSpec((1, 128), lambda i: (0, 0)),
                   pl.BlockSpec((4, POOL_GROUP_DIM, POOL_GROUP_DIM), lambda i: (0, 0, 0)),
                   pl.BlockSpec((len(POOL_WINDOWS), POOL_GROUP_DIM), lambda i: (0, 0)), SMEM] + [ANY] * na,
        scratch_shapes=[pltpu.VMEM((BLOCK + HALO, POOL_WIDTH), F32),
                        pltpu.VMEM((BLOCK, POOL_WIDTH), F32), pltpu.VMEM((BLOCK, POOL_WIDTH), F32),
                        pltpu.VMEM((BLOCK, 256), F32),
                        pltpu.VMEM((N_Q_HEADS, BLOCK, 2 * BLOCK), F32), pltpu.VMEM((N_Q_HEADS, BLOCK, 2 * BLOCK), BF16),
                        pltpu.VMEM((BLOCK, 128), F32)]
        + _chip_exchange_scratch(ffn_parts),
        compiler_params=_params(collective_id=CHIP_COLLECTIVE_ID),
    )(proj, proj, proj, dcat, pooled, probs, p_sinks, w_pool, pool_scale, bucket, *ffn_parts)
    return out[:6], out[6:]


def inproj_bwd(dproj, w_in_t, x, g, dx1):
    t, d = x.shape
    n = dproj.shape[1]
    tm = TOKEN_TILE

    def body(dp_ref, w_ref, x_ref, g_ref, dx1_ref, dx_ref, dg_ref):
        @pl.when(pl.program_id(0) == 0)
        def _():
            dg_ref[...] = jnp.zeros_like(dg_ref)

        dh = _dot(dp_ref[...], w_ref[...])
        xv = x_ref[...]
        dx, dg_rows = _norm_bwd(dh, xv, _rstd(xv), g_ref[...])
        dx_ref[...] = dx1_ref[...] + dx
        dg_ref[...] += _as_rows(jnp.sum(dg_rows, axis=0, keepdims=True))

    row = pl.BlockSpec((tm, d), lambda i: (i, 0))
    gain = pl.BlockSpec((1, d), lambda i: (0, 0))
    return pl.pallas_call(
        body, name="inproj_bwd", grid=(t // tm,),
        out_shape=[jax.ShapeDtypeStruct((t, d), F32), jax.ShapeDtypeStruct((d // 128, 128), F32)],
        in_specs=[pl.BlockSpec((tm, n), lambda i: (i, 0)), pl.BlockSpec(w_in_t.shape, lambda i: (0, 0)), row, gain, row],
        out_specs=[row, pl.BlockSpec((d // 128, 128), lambda i: (0, 0))],
        compiler_params=_params(),
    )(dproj, w_in_t, x, g, dx1)


def _bucket_band():
    qi = jnp.arange(BLOCK)[:, None]
    kj = jnp.arange(2 * BLOCK)[None, :]
    dist = qi + BLOCK - kj
    n = jnp.maximum(dist, 0)
    nf = jnp.maximum(n, 1).astype(F32)
    large = MAX_EXACT + (jnp.log(nf / MAX_EXACT) / np.float32(np.log(MAX_DISTANCE / MAX_EXACT))
                         * (N_BUCKETS - MAX_EXACT)).astype(jnp.int32)
    large = jnp.minimum(large, N_BUCKETS - 1)
    bucket = jnp.where(n < MAX_EXACT, n, large)
    in_window = (dist >= 0) & (dist < BLOCK)
    return bucket.astype(F32), in_window.astype(F32)


def kernel(x, g_pre_mix, w_in, w_pool, pool_scale, rel_bias, sinks, w_out, g_post_mix, g_pre_ffn, w_gate, w_up, w_down, g_post_ffn, loss_target, m_g_pre_mix, m_w_in, m_w_pool, m_pool_scale, m_rel_bias, m_sinks, m_w_out, m_g_post_mix, m_g_pre_ffn, m_w_gate, m_w_up, m_w_down, m_g_post_ffn, v_g_pre_mix, v_w_in, v_w_pool, v_pool_scale, v_rel_bias, v_sinks, v_w_out, v_g_post_mix, v_g_pre_ffn, v_w_gate, v_w_up, v_w_down, v_g_post_ffn):
    d = x.shape[-1]
    xs, target = x[0], loss_target[0]

    w_in_ts = w_in[0].T.astype(BF16)
    w_out_s = w_out[0].astype(BF16)
    gate_ts = w_gate[0].T.astype(BF16)
    up_ts = w_up[0].T.astype(BF16)
    w_down_s = w_down[0].astype(BF16)

    bucket, in_window = _bucket_band()
    w_pool_b = w_pool[0].astype(BF16)
    half = up_ts.shape[0] // 2
    proj, h1, w_in_t, up_t = norm_inproj(xs, g_pre_mix, w_in_ts, up_ts[:half], up_ts.shape[0])
    w_in_t = w_in_t.reshape(-1, d)
    cat, pooled, probs, p_sinks, gate_t, w_out_f = mixers_fwd(
        proj, bucket, in_window, rel_bias, sinks, w_pool_b, pool_scale, [gate_ts, w_out_s])
    w_out_f = w_out_f.reshape(-1, d)
    mix, x1, h2, up_t = outproj_norm(cat, w_out_f, xs, g_post_mix, g_pre_ffn, up_ts[half:], up_t)
    gate, up, act, w_down_f = ffn_up(h2, gate_t, up_t, w_down_s)
    df, dy, dg_post_ffn, loss_part = ffn_down_loss(act, w_down_f, x1, g_post_ffn, target)

    (dgate, dup), (d_gate, d_up) = ffn_down_bwd(df, w_down_f, gate, up, h2)
    d_down, got_gate_up = grad_w_down(act, df, [d_gate, d_up])
    q_gate, q_up, q_down = pair_add_exchange([d_gate, d_up], got_gate_up, d_down, "pair_add_ffn")
    (dx1, dmix, dg_pre_ffn, dg_post_mix), (gate_slots, down_slots) = ffn_up_bwd(
        dgate, dup, gate_t, up_t, x1, g_pre_ffn, dy, mix, g_post_mix, [q_gate, q_down])
    dcat, q_out = outproj_bwd(dmix, w_out_f, cat)
    (dproj, _, dsinks, dw_pool, dpool_scale, drel_bias), (up_slots, out_slots) = mixers_bwd(
        proj, dcat, pooled, probs, p_sinks, w_pool_b, pool_scale, bucket, [q_up, q_out])
    grad_x, dg_pre_mix = inproj_bwd(dproj, w_in_t, xs, g_pre_mix, dx1)

    small_w = [g_pre_mix, g_post_mix, g_pre_ffn, g_post_ffn, pool_scale, sinks, w_pool, rel_bias.T]
    small_m = [m_g_pre_mix, m_g_post_mix, m_g_pre_ffn, m_g_post_ffn, m_pool_scale, m_sinks, m_w_pool, m_rel_bias.T]
    small_v = [v_g_pre_mix, v_g_post_mix, v_g_pre_ffn, v_g_post_ffn, v_pool_scale, v_sinks, v_w_pool, v_rel_bias.T]
    d_in_t, total, total_rb = grad_w_in_small_reduce(
        dproj, h1, [dg_pre_mix, dg_post_mix, dg_pre_ffn, dg_post_ffn], dpool_scale, dsinks, loss_part, dw_pool, drel_bias)
    g_in_t = reduce_w_in(d_in_t)
    loss_row, sm = small_adamw(total, total_rb, small_w, small_m, small_v)
    sm[7] = [r.T for r in sm[7]]
    big_w = [w_in[0].T, w_out[0], w_gate[0].T, w_up[0].T, w_down[0]]
    big_m = [m_w_in[0].T, m_w_out[0], m_w_gate[0].T, m_w_up[0].T, m_w_down[0]]
    big_v = [v_w_in[0].T, v_w_out[0], v_w_gate[0].T, v_w_up[0].T, v_w_down[0]]
    upd = sum_adamw([out_slots, gate_slots, up_slots, down_slots], big_w[1:], big_m[1:], big_v[1:], "sum_adamw")
    upd = [[g_in_t, *adamw_update(big_w[:1], [g_in_t], big_m[:1], big_v[:1], "adamw_in")[0]], *upd]
    back = lambda k, a: (a.T if k in (0, 2, 3) else a)[None]
    big = [[back(k, u) for u in upd[k]] for k in range(5)]

    def ordered(kind):
        s, b = [p[kind] for p in sm], [p[kind] for p in big]
        return [s[0], b[0], s[6], s[4], s[7], s[5], b[1], s[1], s[2], b[2], b[3], b[4], s[3]]

    return (loss_row[0, 0], grad_x[None], *ordered(0), *ordered(1), *ordered(2), *ordered(3))
```

```python
import numpy as np
import jax
import jax.numpy as jnp
from jax import lax
from jax.experimental import pallas as pl
from jax.experimental.pallas import tpu as pltpu

F32 = jnp.float32
BF16 = jnp.bfloat16

N_DEV = 8
N_CHIP = 4
POOL_WIDTH = 512
POOL_WINDOWS = (2, 4, 8, 16)
POOL_GROUP_DIM = 128
HEAD_DIM = 64
N_Q_HEADS = 8
N_KV_HEADS = 2
GQA_GROUP = 4
BLOCK = 128
HALO = 16
ROW_CHUNK = 32
N_BUCKETS = 32
MAX_EXACT = 16
MAX_DISTANCE = 128
EPS = 1e-6
NEG_INF = -1e30
ATTN_SCALE = float(1.0 / np.sqrt(np.float32(HEAD_DIM)))

ADAM_LR = 0.001
ADAM_B1 = 0.9
ADAM_B2 = 0.999
ADAM_EPS = 1e-08
ADAM_WD = 0.01
ADAM_STEP = 10

TOKEN_TILE = 1024
WIDE_K_TOKEN_TILE = 512
FFN_TOKEN_TILE = 1024
FF_SHARDS_PER_TILE = 4
VMEM_LIMIT = 56 * 1024 * 1024
MESH = pl.DeviceIdType.MESH
PAIR_COLLECTIVE_ID = 0
GATHER_COLLECTIVE_ID = 1
CHIP_COLLECTIVE_ID = 2
ANY = pl.BlockSpec(memory_space=pl.ANY)
VMEM = pl.BlockSpec(memory_space=pltpu.VMEM)
SMEM = pl.BlockSpec(memory_space=pltpu.SMEM)


def _params(**kw):
    return pltpu.CompilerParams(vmem_limit_bytes=VMEM_LIMIT, **kw)


def _dot(a, b):
    return jnp.dot(a, b, preferred_element_type=F32)


def _dot_nt(a, b):
    return lax.dot_general(a, b, (((1,), (1,)), ((), ())), preferred_element_type=F32)


def _dot_tn(a, b):
    return lax.dot_general(a, b, (((0,), (0,)), ((), ())), preferred_element_type=F32)


def _rstd(v):
    return lax.rsqrt(jnp.mean(v * v, axis=-1, keepdims=True) + EPS)


def _norm_bwd(dout, v, r, g):
    vn = v * r
    dn = dout * g
    dv = r * (dn - vn * jnp.mean(dn * vn, axis=-1, keepdims=True))
    return dv, dout * vn


def _as_rows(v):
    return jnp.concatenate([v[:, k:k + 128] for k in range(0, v.shape[1], 128)], axis=0)


def _as_lanes(rows):
    return jnp.concatenate([rows[k:k + 1, :] for k in range(rows.shape[0])], axis=1)


def _handshake(peers):
    barrier = pltpu.get_barrier_semaphore()
    for peer in peers:
        pl.semaphore_signal(barrier, inc=1, device_id=peer, device_id_type=MESH)
    pl.semaphore_wait(barrier, len(peers))


def _merge_rows(value):
    s, r, c_ = value.shape
    return value.reshape(s * r, c_)


def _gather_plan(srcs, outs, send_sems, recv_sems, local_sems=None, bounce=None, rows=None, shake=True):
    n = len(srcs)
    x, y, c = lax.axis_index("x"), lax.axis_index("y"), lax.axis_index("c")
    me, sibling = (x, y, c), (x, y, 1 - c)
    chips = [(1 - x, y), (x, 1 - y), (1 - x, 1 - y)]

    def slot(a, px, py, pc):
        whole = outs[a].at[4 * px + 2 * py + pc]
        return whole if rows is None or rows[a] is None else whole.at[pl.ds(*rows[a])]

    def copy(a, k, block, to, from_src=False):
        return pltpu.make_async_remote_copy(
            src_ref=srcs[a] if from_src else slot(a, *block), dst_ref=slot(a, *block),
            send_sem=send_sems.at[k * n + a], recv_sem=recv_sems.at[k * n + a], device_id=to, device_id_type=MESH)

    def own_in(a):
        return pltpu.make_async_copy(srcs[a], bounce[a], local_sems.at[a])

    def own_out(a):
        return pltpu.make_async_copy(bounce[a], slot(a, *me), local_sems.at[a])

    def first(a):
        return [copy(a, 0, me, sibling, True)] + [copy(a, 1 + j, me, (*chip, c), True) for j, chip in enumerate(chips)]

    def passed(a, j):
        return copy(a, 4 + j, (*chips[j], c), sibling)

    def start():
        if shake:
            _handshake([sibling] + [(*chip, c) for chip in chips])
        for a in range(n):
            if bounce is not None:
                own_in(a).start()
            for cp in first(a):
                cp.start()

    def forward():
        if bounce is not None:
            for a in range(n):
                own_in(a).wait()
                own_out(a).start()
        for j, chip in enumerate(chips):
            for a in range(n):
                copy(a, 1 + j, (*chip, c), me).wait_recv()
                passed(a, j).start()

    def complete():
        for a in range(n):
            copy(a, 0, sibling, me).wait_recv()
            for j, chip in enumerate(chips):
                copy(a, 4 + j, (*chip, 1 - c), me).wait_recv()
        for a in range(n):
            for cp in first(a) + [passed(a, j) for j in range(3)]:
                cp.wait_send()
            if bounce is not None:
                own_out(a).wait()

    def finish():
        forward()
        complete()

    finish.forward, finish.complete = forward, complete
    return start, finish


def _gather_scratch(shards):
    n = len(shards)
    return [pltpu.SemaphoreType.DMA((7 * n,)), pltpu.SemaphoreType.DMA((7 * n,)), pltpu.SemaphoreType.DMA((n,))] \
        + [pltpu.VMEM(s.shape, s.dtype) for s in shards]


def _chip_exchange_plan(srcs, outs, send_sems, recv_sems, local_sems, bounce):
    n = len(srcs)
    x, y, c = lax.axis_index("x"), lax.axis_index("y"), lax.axis_index("c")
    my_chip = 2 * x + y

    def copies():
        out = []
        for a in range(n):
            for k in range(1, N_CHIP):
                px, py = x ^ (k >> 1), y ^ (k & 1)
                out.append(pltpu.make_async_remote_copy(
                    src_ref=srcs[a].at[2 * px + py], dst_ref=outs[a].at[my_chip],
                    send_sem=send_sems.at[(k - 1) * n + a], recv_sem=recv_sems.at[(k - 1) * n + a],
                    device_id=(px, py, c), device_id_type=MESH))
        return out

    def own_in(a):
        return pltpu.make_async_copy(srcs[a].at[my_chip], bounce[a], local_sems.at[a])

    def own_out(a):
        return pltpu.make_async_copy(bounce[a], outs[a].at[my_chip], local_sems.at[a])

    def start():
        _handshake([(x ^ (k >> 1), y ^ (k & 1), c) for k in range(1, N_CHIP)])
        for a in range(n):
            own_in(a).start()
        for cp in copies():
            cp.start()

    def finish():
        for a in range(n):
            own_in(a).wait()
            own_out(a).start()
        for cp in copies():
            cp.wait()
        for a in range(n):
            own_out(a).wait()

    return start, finish


def _chip_exchange_scratch(parts):
    n = len(parts)
    return [pltpu.SemaphoreType.DMA((3 * n,)), pltpu.SemaphoreType.DMA((3 * n,)), pltpu.SemaphoreType.DMA((n,))] \
        + [pltpu.VMEM(p.shape[1:], p.dtype) for p in parts]


def _pair_plan(srcs, outs, send_sems, recv_sems):
    x, y, c = lax.axis_index("x"), lax.axis_index("y"), lax.axis_index("c")

    def copies():
        return [pltpu.make_async_remote_copy(
            src_ref=srcs[a].at[1 - c], dst_ref=outs[a], send_sem=send_sems.at[a], recv_sem=recv_sems.at[a],
            device_id=(x, y, 1 - c), device_id_type=MESH) for a in range(len(srcs))]

    def start():
        _handshake([(x, y, 1 - c)])
        for cp in copies():
            cp.start()

    def finish():
        for cp in copies():
            cp.wait()

    return start, finish


def pair_add_exchange(parts, got, late, name):
    n = len(parts)
    chips = late.shape[1]

    def body(core_ref, *refs):
        late_own, late_all = refs[2 * n:2 * n + 2]
        late_out = refs[3 * n + 2]
        landed, send_sems, recv_sems = refs[3 * n + 3:]
        i = pl.program_id(0)
        start, finish = _pair_plan([late_all], [landed], send_sems, recv_sems)
        pl.when(i == 0)(start)
        for a in range(n):
            refs[2 * n + 2 + a][...] = (refs[a][...].astype(F32) + refs[n + a][...].astype(F32)).astype(BF16)

        @pl.when(i == chips - 1)
        def _():
            finish()
            late_out[...] = (late_own[...].astype(F32) + landed[...].astype(F32)).astype(BF16)

    def own(p):
        zeros = (0,) * (p.ndim - 2)
        return pl.BlockSpec((None, 1, *p.shape[2:]), lambda i, core: (core[0], i, *zeros))

    def plain(p):
        zeros = (0,) * (p.ndim - 1)
        return pl.BlockSpec((1, *p.shape[1:]), lambda i, core: (i, *zeros))

    zeros = (0,) * (late.ndim - 1)
    core = lax.axis_index("c").astype(jnp.int32).reshape(1)
    return pl.pallas_call(
        body, name=name,
        grid_spec=pltpu.PrefetchScalarGridSpec(
            num_scalar_prefetch=1, grid=(chips,),
            in_specs=[own(p) for p in parts] + [plain(p) for p in got]
            + [pl.BlockSpec((None, *late.shape[1:]), lambda i, core: (core[0], *zeros)), ANY],
            out_specs=[plain(p) for p in got] + [pl.BlockSpec(late.shape[1:], lambda i, core: zeros)],
            scratch_shapes=[pltpu.VMEM(late.shape[1:], late.dtype), pltpu.SemaphoreType.DMA((1,)), pltpu.SemaphoreType.DMA((1,))]),
        out_shape=[jax.ShapeDtypeStruct(p.shape, BF16) for p in got] + [jax.ShapeDtypeStruct(late.shape[1:], BF16)],
        compiler_params=_params(collective_id=PAIR_COLLECTIVE_ID),
    )(core, *parts, *got, late, late)


def _adamw(w, g, m, v):
    m2 = ADAM_B1 * m + (1.0 - ADAM_B1) * g
    v2 = ADAM_B2 * v + (1.0 - ADAM_B2) * (g * g)
    m_hat = m2 / (1.0 - ADAM_B1 ** ADAM_STEP)
    v_hat = v2 / (1.0 - ADAM_B2 ** ADAM_STEP)
    delta = -ADAM_LR * (m_hat / (jnp.sqrt(v_hat) + ADAM_EPS) + ADAM_WD * w)
    return delta, m2, v2


def sum_adamw(slots, ws, ms, vs, name):
    n = len(ws)
    halves = 2

    def body(*refs):
        for a in range(n):
            total = refs[a][0].astype(F32)
            for s in range(1, slots[a].shape[0]):
                total = total + refs[a][s].astype(F32)
            delta, m2, v2 = _adamw(refs[n + a][...], total, refs[2 * n + a][...], refs[3 * n + a][...])
            for q, val in enumerate((total, delta, m2, v2)):
                refs[4 * n + 4 * a + q][...] = val

    def rows(w):
        return pl.BlockSpec((w.shape[0] // halves, w.shape[1]), lambda i: (i, 0))

    def slot_rows(p):
        return pl.BlockSpec((p.shape[0], p.shape[1] // halves, p.shape[2]), lambda i: (0, i, 0))

    out = pl.pallas_call(
        body, name=name, grid=(halves,),
        out_shape=[jax.ShapeDtypeStruct(w.shape, F32) for w in ws for _ in range(4)],
        in_specs=[slot_rows(p) for p in slots] + [rows(w) for w in ws] * 3,
        out_specs=[rows(w) for w in ws for _ in range(4)],
        compiler_params=_params(),
    )(*slots, *ws, *ms, *vs)
    return [out[4 * a:4 * a + 4] for a in range(n)]


GAIN_ROWS = 8
ROW_POOL_SCALE = 4 * GAIN_ROWS
ROW_SINKS = ROW_POOL_SCALE + 4
ROW_LOSS = ROW_SINKS + 1
ROW_W_POOL = 40
SMALL_ROWS = ROW_W_POOL + 4 * POOL_GROUP_DIM


def grad_w_in_small_reduce(a, b, gains, dpool_scale, dsinks, loss_part, dw_pool, drel_bias):
    t, m = a.shape
    d = b.shape[1]
    r = m // N_DEV
    tt = TOKEN_TILE
    last = t // tt - 1

    def body(a_ref, b_ref, g0, g1, g2, g3, dsc_ref, dsink_ref, loss_ref, dwp_ref, drb_ref, out_ref, total_ref, total_rb_ref,
             acc, stage, gat, gat_rb, g_send, g_recv):
        k = pl.program_id(0)
        x, y, c = lax.axis_index("x"), lax.axis_index("y"), lax.axis_index("c")
        start, finish = _gather_plan([stage, drb_ref], [gat, gat_rb], g_send, g_recv)

        @pl.when(k == 0)
        def _():
            for q, g_ref in enumerate((g0, g1, g2, g3)):
                stage[GAIN_ROWS * q:GAIN_ROWS * (q + 1), :] = g_ref[...]
            stage[ROW_POOL_SCALE:ROW_SINKS, :] = dsc_ref[...]
            stage[ROW_SINKS:ROW_LOSS, :] = dsink_ref[...]
            stage[ROW_LOSS:ROW_LOSS + 1, :] = loss_ref[...]
            stage[ROW_LOSS + 1:ROW_W_POOL, :] = jnp.zeros((ROW_W_POOL - ROW_LOSS - 1, 128), F32)
            stage[ROW_W_POOL:, :] = dwp_ref[...].reshape(4 * POOL_GROUP_DIM, POOL_GROUP_DIM)
            gat[4 * x + 2 * y + c] = stage[...]
            gat_rb[4 * x + 2 * y + c] = drb_ref[...]
            start()
            acc[...] = jnp.zeros_like(acc)

        acc[...] += _dot_tn(a_ref[...], b_ref[...])

        @pl.when(k == last)
        def _():
            blocks = acc[...].reshape(N_CHIP, 2, r, d)
            for chip in range(N_CHIP):
                for core in range(2):
                    out_ref[core, chip] = blocks[chip, core].astype(BF16)
            finish()
            total, total_rb = gat[0], gat_rb[0]
            for s in range(1, N_DEV):
                total, total_rb = total + gat[s], total_rb + gat_rb[s]
            total_ref[...] = total
            total_rb_ref[...] = total_rb

    out_shape = (2, N_CHIP, r, d)
    return pl.pallas_call(
        body, name="grad_w_in", grid=(t // tt,),
        out_shape=[jax.ShapeDtypeStruct(out_shape, BF16), jax.ShapeDtypeStruct((SMALL_ROWS, 128), F32),
                   jax.ShapeDtypeStruct(drel_bias.shape, F32)],
        in_specs=[pl.BlockSpec((tt, m), lambda k: (k, 0)), pl.BlockSpec((tt, d), lambda k: (k, 0))] + [VMEM] * 9,
        out_specs=[pl.BlockSpec(out_shape, lambda k: (0,) * len(out_shape)), VMEM, VMEM],
        scratch_shapes=[pltpu.VMEM((m, d), F32), pltpu.VMEM((SMALL_ROWS, 128), F32),
                        pltpu.VMEM((N_DEV, SMALL_ROWS, 128), F32), pltpu.VMEM((N_DEV, *drel_bias.shape), F32),
                        pltpu.SemaphoreType.DMA((14,)), pltpu.SemaphoreType.DMA((14,))],
        compiler_params=_params(collective_id=GATHER_COLLECTIVE_ID),
    )(a, b, *gains, dpool_scale, dsinks, loss_part, dw_pool, drel_bias)


def reduce_w_in(d_in_t):
    def body(d_in_ref, g_in_ref, pair_got, chip_part, chip_got, p_send, p_recv, x_send, x_recv):
        x, y, c = lax.axis_index("x"), lax.axis_index("y"), lax.axis_index("c")
        my_chip = 2 * x + y
        _handshake([(x, y, 1 - c)] + [(x ^ (k >> 1), y ^ (k & 1), c) for k in range(1, N_CHIP)])
        pair = pltpu.make_async_remote_copy(
            src_ref=d_in_ref.at[1 - c], dst_ref=pair_got, send_sem=p_send, recv_sem=p_recv,
            device_id=(x, y, 1 - c), device_id_type=MESH)
        pair.start()
        pair.wait()
        chip_part[...] = (d_in_ref[c].astype(F32) + pair_got[...].astype(F32)).astype(BF16)
        copies = []
        for k in range(1, N_CHIP):
            px, py = x ^ (k >> 1), y ^ (k & 1)
            copies.append(pltpu.make_async_remote_copy(
                src_ref=chip_part.at[2 * px + py], dst_ref=chip_got.at[my_chip],
                send_sem=x_send.at[k - 1], recv_sem=x_recv.at[k - 1], device_id=(px, py, c), device_id_type=MESH))
        for cp in copies:
            cp.start()
        chip_got[my_chip] = chip_part[my_chip]
        for cp in copies:
            cp.wait()
        g_in = chip_got[0].astype(F32)
        for s in range(1, N_CHIP):
            g_in = g_in + chip_got[s].astype(F32)
        g_in_ref[...] = g_in

    per_core = d_in_t.shape[1:]
    return pl.pallas_call(
        body, name="reduce_w_in",
        out_shape=jax.ShapeDtypeStruct(d_in_t.shape[2:], F32),
        in_specs=[VMEM], out_specs=VMEM,
        scratch_shapes=[pltpu.VMEM(per_core, d_in_t.dtype), pltpu.VMEM(per_core, d_in_t.dtype),
                        pltpu.VMEM(per_core, d_in_t.dtype),
                        pltpu.SemaphoreType.DMA, pltpu.SemaphoreType.DMA,
                        pltpu.SemaphoreType.DMA((3,)), pltpu.SemaphoreType.DMA((3,))],
        compiler_params=_params(collective_id=GATHER_COLLECTIVE_ID),
    )(d_in_t)


def small_adamw(total, total_rb, small_w, small_m, small_v):
    n_small = len(small_w)

    def body(*refs):
        total_ref, rb_ref = refs[:2]
        w_refs, m_refs, v_refs = (refs[2 + k * n_small:2 + (k + 1) * n_small] for k in range(3))
        loss_out = refs[2 + 3 * n_small]
        result = refs[3 + 3 * n_small:]
        total = total_ref[...]
        loss_out[...] = total[ROW_LOSS:ROW_LOSS + 1, :]
        grads = [_as_lanes(total[GAIN_ROWS * k:GAIN_ROWS * (k + 1), :]) for k in range(4)]
        grads.append(_as_lanes(total[ROW_POOL_SCALE:ROW_SINKS, :]))
        grads.append(total[ROW_SINKS:ROW_LOSS, 0:N_Q_HEADS])
        grads.append(total[ROW_W_POOL:, :].reshape(w_refs[6].shape))
        grads.append(rb_ref[...])
        for k in range(n_small):
            delta, m2, v2 = _adamw(w_refs[k][...], grads[k], m_refs[k][...], v_refs[k][...])
            result[4 * k][...] = grads[k]
            result[4 * k + 1][...] = delta
            result[4 * k + 2][...] = m2
            result[4 * k + 3][...] = v2

    out = pl.pallas_call(
        body, name="small_adamw",
        out_shape=[jax.ShapeDtypeStruct((1, 128), F32)] + [jax.ShapeDtypeStruct(w.shape, F32) for w in small_w for _ in range(4)],
        in_specs=[VMEM] * (2 + 3 * n_small), out_specs=[VMEM] * (1 + 4 * n_small),
        compiler_params=_params(),
    )(total, total_rb, *small_w, *small_m, *small_v)
    return out[0], [out[1 + 4 * k:5 + 4 * k] for k in range(n_small)]


def norm_inproj(x, g, w_shard, shard, shard_rows):
    t, d = x.shape
    r = w_shard.shape[0]
    tm = TOKEN_TILE
    nt = t // tm

    def body(x_ref, g_ref, w_shard_ref, shard_ref, proj_ref, h_ref, w_ref, gathered_ref, h_all, w_all, w_sem,
             send_w, recv_w, local_w, bounce_w, send_sems, recv_sems, local_sems, bounce):
        i = pl.program_id(0)
        start_w, finish_w = _gather_plan([w_shard_ref], [w_ref], send_w, recv_w, local_w, [bounce_w])
        start, finish = _gather_plan([shard_ref], [gathered_ref], send_sems, recv_sems, local_sems, [bounce],
                                     [(0, shard.shape[0])], shake=False)

        @pl.when(i == 0)
        def _():
            start_w()
            start()

        @pl.when(i < nt)
        def _():
            xv = x_ref[...]
            h = ((xv * _rstd(xv)) * g_ref[...]).astype(BF16)
            h_ref[...] = h
            h_all[pl.ds(pl.multiple_of(i * tm, tm), tm), :] = h

        @pl.when(i == nt - 1)
        def _():
            finish_w()
            landed = pltpu.make_async_copy(w_ref, w_all, w_sem)
            landed.start()
            landed.wait()

        @pl.when(i >= nt)
        def _():
            rows = pl.ds(pl.multiple_of((i - nt) * tm, tm), tm)
            proj_ref[...] = _dot_nt(h_all[rows, :], _merge_rows(w_all[...]))

        pl.when(i == 2 * nt - 2)(finish.forward)
        pl.when(i == 2 * nt - 1)(finish.complete)

    first = lambda i: (jnp.minimum(i, nt - 1), 0)
    return pl.pallas_call(
        body, name="norm_inproj", grid=(2 * nt,),
        out_shape=[jax.ShapeDtypeStruct((t, N_DEV * r), F32), jax.ShapeDtypeStruct((t, d), BF16),
                   jax.ShapeDtypeStruct((N_DEV, r, d), w_shard.dtype),
                   jax.ShapeDtypeStruct((N_DEV, shard_rows, d), shard.dtype)],
        in_specs=[pl.BlockSpec((tm, d), first), pl.BlockSpec((1, d), lambda i: (0, 0)), ANY, ANY],
        out_specs=[pl.BlockSpec((tm, N_DEV * r), lambda i: (jnp.maximum(i - nt, 0), 0)), pl.BlockSpec((tm, d), first),
                   ANY, ANY],
        scratch_shapes=[pltpu.VMEM((t, d), BF16), pltpu.VMEM((N_DEV, r, d), w_shard.dtype), pltpu.SemaphoreType.DMA]
        + _gather_scratch([w_shard]) + _gather_scratch([shard]),
        compiler_params=_params(collective_id=GATHER_COLLECTIVE_ID),
    )(x, g, w_shard, shard)


def _fill_bias_band(bk_ref, win_ref, rb_ref, biasm_ref):
    bk = bk_ref[...]
    keep = win_ref[...] > 0.5
    for h in range(N_Q_HEADS):
        acc = jnp.zeros(bk.shape, F32)
        for b in range(N_BUCKETS):
            acc = jnp.where(bk == float(b), rb_ref[b, h], acc)
        biasm_ref[h] = jnp.where(keep, acc, NEG_INF)


def _window_sum(buf_ref, g, w, first):
    cols = slice(g * POOL_GROUP_DIM, (g + 1) * POOL_GROUP_DIM)
    acc = None
    for k in range(w):
        piece = buf_ref[first(k):first(k) + BLOCK, cols]
        acc = piece if acc is None else acc + piece
    return acc


def _inv_count(i, w):
    row = lax.broadcasted_iota(jnp.int32, (BLOCK, 1), 0)
    return 1.0 / jnp.minimum(i * BLOCK + row + 1, w).astype(F32)


def _fill_pool_input(i, ubuf, uc_ref, halo_ref):
    ubuf[0:HALO, :] = jnp.where(i > 0, halo_ref[...], 0.0)
    ubuf[HALO:, :] = uc_ref[...]


def _pooled(i, g, w, ubuf):
    cols = slice(g * POOL_GROUP_DIM, (g + 1) * POOL_GROUP_DIM)
    return _window_sum(ubuf, g, w, lambda k: HALO - k) * _inv_count(i, w) - ubuf[HALO:, cols]


def _head_variants(pair):
    low = lax.broadcasted_iota(jnp.int32, pair.shape, 1) < HEAD_DIM
    swapped = pltpu.roll(pair, HEAD_DIM, 1)
    zero = jnp.zeros_like(pair)
    pick = lambda c, a, b: jnp.where(c, a, b).astype(BF16)
    return [[pick(low, pair, zero), pick(low, zero, swapped)], [pick(low, swapped, zero), pick(low, zero, pair)]]


def _head_probs(i, hq, rows, s_ref, biasm_ref, sinks_ref):
    s = s_ref[hq, rows, :] * ATTN_SCALE + biasm_ref[hq, rows, :]
    col = lax.broadcasted_iota(jnp.int32, s.shape, 1)
    s = jnp.where((i == 0) & (col < BLOCK), NEG_INF, s)
    sink = sinks_ref[0, hq]
    m = jnp.maximum(jnp.max(s, axis=-1, keepdims=True), sink)
    p = jnp.exp(s - m)
    e_sink = jnp.exp(sink - m)
    inv = 1.0 / (jnp.sum(p, axis=-1, keepdims=True) + e_sink)
    return p * inv, e_sink * inv


def _head_slot(hq):
    return 4 * (hq // GQA_GROUP) + 2 * (hq % 2) + (hq % GQA_GROUP) // 2


def _mixer_in_specs(cur, prv):
    return [pl.BlockSpec((BLOCK, 512), lambda i: (cur(i), 0)),
            pl.BlockSpec((HALO, 512), lambda i: (jnp.maximum(cur(i) * (BLOCK // HALO) - 1, 0), 0)),
            pl.BlockSpec((BLOCK, 512), lambda i: (cur(i), 1)),
            pl.BlockSpec((BLOCK, 256), lambda i: (cur(i), 4)),
            pl.BlockSpec((BLOCK, 256), lambda i: (prv(i), 4))]


def _mixer_param_specs():
    return [pl.BlockSpec((4, POOL_GROUP_DIM, POOL_GROUP_DIM), lambda i: (0, 0, 0)),
            pl.BlockSpec((1, POOL_WIDTH), lambda i: (0, 0))]


def mixers_fwd(proj, bucket, in_window, rel_bias, sinks, w_pool, pool_scale, shards):
    t = proj.shape[0]
    nb = t // BLOCK
    ns = len(shards)

    def body(*refs):
        uc_ref, halo_ref, q_ref, kvc_ref, kvp_ref, bk_ref, win_ref, rb_ref, sinks_ref, wp_ref, sc_ref = refs[:11]
        shard_refs = refs[11:11 + ns]
        out_ref, pooled_ref, p_all, psink_ref = refs[11 + ns:15 + ns]
        gathered_refs = refs[15 + ns:15 + 2 * ns]
        ubuf, s_all, biasm_ref, send_sems, recv_sems, local_sems = refs[15 + 2 * ns:21 + 2 * ns]
        i = pl.program_id(0)
        start, finish = _gather_plan(shard_refs, gathered_refs, send_sems, recv_sems, local_sems, refs[21 + 2 * ns:])

        @pl.when(i == 0)
        def _():
            start()
            _fill_bias_band(bk_ref, win_ref, rb_ref, biasm_ref)

        _fill_pool_input(i, ubuf, uc_ref, halo_ref)
        for g, w in enumerate(POOL_WINDOWS):
            cols = slice(g * POOL_GROUP_DIM, (g + 1) * POOL_GROUP_DIM)
            pooled = _pooled(i, g, w, ubuf).astype(BF16)
            pooled_ref[:, cols] = pooled
            out_ref[:, cols] = (_dot(pooled, wp_ref[g]) * sc_ref[:, cols]).astype(BF16)
        kv = jnp.concatenate([kvp_ref[...], kvc_ref[...]], axis=0)
        k_var = _head_variants(kv[:, 0:2 * HEAD_DIM])
        v_var = _head_variants(kv[:, 2 * HEAD_DIM:])
        for hq in range(N_Q_HEADS):
            j, half, h = hq // 2, hq % 2, hq // GQA_GROUP
            q2 = q_ref[:, 2 * HEAD_DIM * j:2 * HEAD_DIM * (j + 1)].astype(BF16)
            s_all[hq] = _dot_nt(q2, k_var[h][half])
        psink_ref[...] = jnp.zeros_like(psink_ref)
        for hq in range(N_Q_HEADS):
            for r in range(0, BLOCK, ROW_CHUNK):
                rows = slice(r, r + ROW_CHUNK)
                probs, p_sink = _head_probs(i, hq, rows, s_all, biasm_ref, sinks_ref)
                p_all[_head_slot(hq), rows, :] = probs.astype(BF16)
                psink_ref[rows, hq:hq + 1] = p_sink
        for j in range(N_Q_HEADS // 2):
            h = 2 * j // GQA_GROUP
            acc = _dot(p_all[_head_slot(2 * j)], v_var[h][0]) + _dot(p_all[_head_slot(2 * j + 1)], v_var[h][1])
            out_ref[:, POOL_WIDTH + 2 * HEAD_DIM * j:POOL_WIDTH + 2 * HEAD_DIM * (j + 1)] = acc.astype(BF16)

        pl.when(i == max(nb - 4, 0))(finish.forward)
        pl.when(i == nb - 1)(finish.complete)

    return pl.pallas_call(
        body, name="mixers_fwd", grid=(nb,),
        out_shape=[jax.ShapeDtypeStruct((t, 2 * POOL_WIDTH), BF16), jax.ShapeDtypeStruct((t, POOL_WIDTH), BF16),
                   jax.ShapeDtypeStruct((N_Q_HEADS, t, 2 * BLOCK), BF16), jax.ShapeDtypeStruct((t, 128), F32)]
        + [jax.ShapeDtypeStruct((N_DEV, *sh.shape), sh.dtype) for sh in shards],
        in_specs=_mixer_in_specs(lambda i: i, lambda i: jnp.maximum(i - 1, 0))
        + [pl.BlockSpec((BLOCK, 2 * BLOCK), lambda i: (0, 0))] * 2 + [SMEM, SMEM] + _mixer_param_specs() + [ANY] * ns,
        out_specs=[pl.BlockSpec((BLOCK, 2 * POOL_WIDTH), lambda i: (i, 0)), pl.BlockSpec((BLOCK, POOL_WIDTH), lambda i: (i, 0)),
                   pl.BlockSpec((N_Q_HEADS, BLOCK, 2 * BLOCK), lambda i: (0, i, 0)), pl.BlockSpec((BLOCK, 128), lambda i: (i, 0))]
        + [ANY] * ns,
        scratch_shapes=[pltpu.VMEM((HALO + BLOCK, POOL_WIDTH), F32), pltpu.VMEM((N_Q_HEADS, BLOCK, 2 * BLOCK), F32),
                        pltpu.VMEM((N_Q_HEADS, BLOCK, 2 * BLOCK), F32)]
        + _gather_scratch(shards),
        compiler_params=_params(collective_id=GATHER_COLLECTIVE_ID),
    )(proj, proj, proj, proj, proj, bucket, in_window, rel_bias, sinks, w_pool, pool_scale, *shards)


def outproj_norm(cat, w, x, g, g_next, shard, partial):
    t, d = x.shape
    tm = TOKEN_TILE
    last = t // tm - 1
    rows = [(partial.shape[1] - shard.shape[0], shard.shape[0])]

    def body(c_ref, w_ref, x_ref, g_ref, gn_ref, shard_ref, partial_ref, mix_ref, x1_ref, h2_ref, gathered_ref,
             send_sems, recv_sems, local_sems, bounce):
        i = pl.program_id(0)
        start, finish = _gather_plan([shard_ref], [gathered_ref], send_sems, recv_sems, local_sems, [bounce], rows)
        pl.when(i == 0)(start)
        mix = _dot(c_ref[...], w_ref[...])
        mix_ref[...] = mix
        x1 = x_ref[...] + (mix * _rstd(mix)) * g_ref[...]
        x1_ref[...] = x1
        h2_ref[...] = ((x1 * _rstd(x1)) * gn_ref[...]).astype(BF16)
        pl.when(i == max(last - 1, 0))(finish.forward)
        pl.when(i == last)(finish.complete)

    row = pl.BlockSpec((tm, d), lambda i: (i, 0))
    gain = pl.BlockSpec((1, d), lambda i: (0, 0))
    return pl.pallas_call(
        body, name="outproj_norm", grid=(t // tm,),
        out_shape=[jax.ShapeDtypeStruct((t, d), F32), jax.ShapeDtypeStruct((t, d), F32), jax.ShapeDtypeStruct((t, d), BF16),
                   jax.ShapeDtypeStruct(partial.shape, partial.dtype)],
        in_specs=[pl.BlockSpec((tm, cat.shape[1]), lambda i: (i, 0)), pl.BlockSpec(w.shape, lambda i: (0, 0)), row, gain, gain,
                  ANY, ANY],
        out_specs=[row, row, row, ANY],
        input_output_aliases={6: 3},
        scratch_shapes=_gather_scratch([shard]),
        compiler_params=_params(collective_id=GATHER_COLLECTIVE_ID),
    )(cat, w, x, g, g_next, shard, partial)


def ffn_up(h, gate_t, up_t, down_shard):
    t, d = h.shape
    n = gate_t.shape[1]
    f = N_DEV * n
    tm, ts = FFN_TOKEN_TILE, FF_SHARDS_PER_TILE
    tn = ts * n
    steps = (f // tn, t // tm)

    def body(h_ref, wg_ref, wu_ref, shard_ref, gate_ref, up_ref, a_ref, gathered_ref,
             send_sems, recv_sems, local_sems, bounce):
        j, i = pl.program_id(0), pl.program_id(1)
        start, finish = _gather_plan([shard_ref], [gathered_ref], send_sems, recv_sems, local_sems, [bounce])
        pl.when((i == 0) & (j == 0))(start)

        hv = h_ref[...]
        gate = _dot_nt(hv, _merge_rows(wg_ref[...]))
        up = _dot_nt(hv, _merge_rows(wu_ref[...]))
        gate_ref[...] = gate.astype(BF16)
        up_ref[...] = up.astype(BF16)
        a_ref[...] = (gate * (1.0 / (1.0 + jnp.exp(-gate))) * up).astype(BF16)

        pl.when((j == steps[0] - 1) & (i == max(steps[1] - 2, 0)))(finish.forward)
        pl.when((j == steps[0] - 1) & (i == steps[1] - 1))(finish.complete)

    wide = pl.BlockSpec((tm, tn), lambda j, i: (i, j))
    return pl.pallas_call(
        body, name="ffn_up", grid=steps,
        out_shape=[jax.ShapeDtypeStruct((t, f), BF16)] * 3
        + [jax.ShapeDtypeStruct((N_DEV, *down_shard.shape), down_shard.dtype)],
        in_specs=[pl.BlockSpec((tm, d), lambda j, i: (i, 0)),
                  pl.BlockSpec((ts, n, d), lambda j, i: (j, 0, 0)),
                  pl.BlockSpec((ts, n, d), lambda j, i: (j, 0, 0)), ANY],
        out_specs=[wide, wide, wide, ANY],
        scratch_shapes=_gather_scratch([down_shard]),
        compiler_params=_params(collective_id=GATHER_COLLECTIVE_ID),
    )(h, gate_t, up_t, down_shard)


def ffn_down_loss(a, w_down, x1, g, target):
    t, d = x1.shape
    tm = WIDE_K_TOKEN_TILE

    def body(a_ref, w_ref, x_ref, g_ref, t_ref, df_ref, dy_ref, dg_ref, loss_ref):
        @pl.when(pl.program_id(0) == 0)
        def _():
            dg_ref[...] = jnp.zeros_like(dg_ref)
            loss_ref[...] = jnp.zeros_like(loss_ref)

        f = _dot(a_ref[...], _merge_rows(w_ref[...]))
        r = _rstd(f)
        g = g_ref[...]
        err = x_ref[...] + (f * r) * g - t_ref[...]
        loss_ref[...] += 0.5 * jnp.sum(jnp.mean(err * err, axis=-1, keepdims=True))
        dy = err * (1.0 / d)
        dy_ref[...] = dy
        df, dg_rows = _norm_bwd(dy, f, r, g)
        df_ref[...] = df.astype(BF16)
        dg_ref[...] += _as_rows(jnp.sum(dg_rows, axis=0, keepdims=True))

    row = pl.BlockSpec((tm, d), lambda i: (i, 0))
    gain = pl.BlockSpec((1, d), lambda i: (0, 0))
    return pl.pallas_call(
        body, name="ffn_down_loss", grid=(t // tm,),
        out_shape=[jax.ShapeDtypeStruct((t, d), BF16), jax.ShapeDtypeStruct((t, d), F32),
                   jax.ShapeDtypeStruct((d // 128, 128), F32), jax.ShapeDtypeStruct((1, 128), F32)],
        in_specs=[pl.BlockSpec((tm, a.shape[1]), lambda i: (i, 0)), pl.BlockSpec(w_down.shape, lambda i: (0, 0, 0)), row, gain, row],
        out_specs=[row, row, pl.BlockSpec((d // 128, 128), lambda i: (0, 0)), pl.BlockSpec((1, 128), lambda i: (0, 0))],
        compiler_params=_params(),
    )(a, w_down, x1, g, target)


def ffn_down_bwd(df, w_down, gate, up, h):
    t, d = df.shape
    n = w_down.shape[1]
    f = gate.shape[1]
    tm, ts = WIDE_K_TOKEN_TILE, FF_SHARDS_PER_TILE
    tn = ts * n
    last = t // tm - 1

    def body(df_ref, w_ref, gate_ref, up_ref, h_ref, dgate_ref, dup_ref, dwg_ref, dwu_ref, acc):
        i = pl.program_id(1)

        @pl.when(i == 0)
        def _():
            acc[...] = jnp.zeros_like(acc)

        da = _dot_nt(df_ref[...], _merge_rows(w_ref[...]))
        gate = gate_ref[...].astype(F32)
        sig = 1.0 / (1.0 + jnp.exp(-gate))
        dgate = (da * up_ref[...].astype(F32) * (sig * (1.0 + gate * (1.0 - sig)))).astype(BF16)
        dup = (da * (gate * sig)).astype(BF16)
        dgate_ref[...] = dgate
        dup_ref[...] = dup
        acc[0] += _dot_tn(dgate, h_ref[...])
        acc[1] += _dot_tn(dup, h_ref[...])

        @pl.when(i == last)
        def _():
            for w, out_ref in enumerate((dwg_ref, dwu_ref)):
                blocks = acc[w].reshape(ts // 2, 2, n, d)
                for chip in range(ts // 2):
                    for core in range(2):
                        out_ref[core, chip] = blocks[chip, core].astype(BF16)

    wide = pl.BlockSpec((tm, tn), lambda j, i: (i, j))
    rows = pl.BlockSpec((tm, d), lambda j, i: (i, 0))
    owned = pl.BlockSpec((2, ts // 2, n, d), lambda j, i: (0, j, 0, 0))
    out = pl.pallas_call(
        body, name="ffn_down_bwd", grid=(f // tn, t // tm),
        out_shape=[jax.ShapeDtypeStruct((t, f), BF16)] * 2 + [jax.ShapeDtypeStruct((2, N_CHIP, n, d), BF16)] * 2,
        in_specs=[rows, pl.BlockSpec((ts, n, d), lambda j, i: (j, 0, 0)), wide, wide, rows],
        out_specs=[wide, wide, owned, owned],
        scratch_shapes=[pltpu.VMEM((2, tn, d), F32)],
        compiler_params=_params(),
    )(df, w_down, gate, up, h)
    return out[:2], out[2:]


def grad_w_down(a, b, pair_parts):
    t, f = a.shape
    d = b.shape[1]
    na = len(pair_parts)
    n = f // N_DEV
    tt, ts = TOKEN_TILE, FF_SHARDS_PER_TILE
    tn = ts * n
    steps = (f // tn, t // tt)

    def body(a_ref, b_ref, *refs):
        part_refs, out_ref, got_refs = refs[:na], refs[na], refs[na + 1:2 * na + 1]
        acc, send_sems, recv_sems = refs[2 * na + 1:]
        i, k = pl.program_id(0), pl.program_id(1)
        start, finish = _pair_plan(part_refs, got_refs, send_sems, recv_sems)
        pl.when((i == 0) & (k == 0))(start)

        @pl.when(k == 0)
        def _():
            acc[...] = jnp.zeros_like(acc)

        acc[...] += _dot_tn(a_ref[...], b_ref[...])

        @pl.when(k == steps[1] - 1)
        def _():
            blocks = acc[...].reshape(ts // 2, 2, n, d)
            for chip in range(ts // 2):
                for core in range(2):
                    out_ref[core, chip] = blocks[chip, core].astype(BF16)

        pl.when((i == steps[0] - 1) & (k == steps[1] - 1))(finish)

    out = pl.pallas_call(
        body, name="grad_w_down", grid=steps,
        out_shape=[jax.ShapeDtypeStruct((2, N_CHIP, n, d), BF16)]
        + [jax.ShapeDtypeStruct(p.shape[1:], p.dtype) for p in pair_parts],
        in_specs=[pl.BlockSpec((tt, tn), lambda i, k: (k, i)), pl.BlockSpec((tt, d), lambda i, k: (k, 0))] + [ANY] * na,
        out_specs=[pl.BlockSpec((2, ts // 2, n, d), lambda i, k: (0, i, 0, 0))] + [ANY] * na,
        scratch_shapes=[pltpu.VMEM((tn, d), F32), pltpu.SemaphoreType.DMA((na,)), pltpu.SemaphoreType.DMA((na,))],
        compiler_params=_params(collective_id=PAIR_COLLECTIVE_ID),
    )(a, b, *pair_parts)
    return out[0], out[1:]


def ffn_up_bwd(dgate, dup, gate_t, up_t, x1, g_ffn, dy, mix, g_mix, chip_parts):
    t, d = x1.shape
    n = gate_t.shape[1]
    f = N_DEV * n
    tm = WIDE_K_TOKEN_TILE
    na = len(chip_parts)
    last = t // tm - 1

    def body(*refs):
        dg_ref, du_ref, wg_ref, wu_ref, x_ref, gf_ref, dy_ref, mix_ref, gm_ref = refs[:9]
        part_refs = refs[9:9 + na]
        dx1_ref, dmix_ref, dgf_ref, dgm_ref = refs[9 + na:13 + na]
        slot_refs = refs[13 + na:13 + 2 * na]
        send_sems, recv_sems, local_sems = refs[13 + 2 * na:16 + 2 * na]
        i = pl.program_id(0)
        start, finish = _chip_exchange_plan(part_refs, slot_refs, send_sems, recv_sems, local_sems, refs[16 + 2 * na:])

        @pl.when(i == 0)
        def _():
            start()
            dgf_ref[...] = jnp.zeros_like(dgf_ref)
            dgm_ref[...] = jnp.zeros_like(dgm_ref)

        dh = _dot(dg_ref[...], _merge_rows(wg_ref[...])) + _dot(du_ref[...], _merge_rows(wu_ref[...]))
        x1 = x_ref[...]
        dx, dgf_rows = _norm_bwd(dh, x1, _rstd(x1), gf_ref[...])
        dx1 = dy_ref[...] + dx
        dx1_ref[...] = dx1
        dgf_ref[...] += _as_rows(jnp.sum(dgf_rows, axis=0, keepdims=True))
        mix = mix_ref[...]
        dmix, dgm_rows = _norm_bwd(dx1, mix, _rstd(mix), gm_ref[...])
        dmix_ref[...] = dmix.astype(BF16)
        dgm_ref[...] += _as_rows(jnp.sum(dgm_rows, axis=0, keepdims=True))
        pl.when(i == last)(finish)

    row = pl.BlockSpec((tm, d), lambda i: (i, 0))
    wide = pl.BlockSpec((tm, f), lambda i: (i, 0))
    gain = pl.BlockSpec((1, d), lambda i: (0, 0))
    gain_rows = pl.BlockSpec((d // 128, 128), lambda i: (0, 0))
    whole = pl.BlockSpec((N_DEV, n, d), lambda i: (0, 0, 0), pipeline_mode=pl.Buffered(1))
    out = pl.pallas_call(
        body, name="ffn_up_bwd", grid=(t // tm,),
        out_shape=[jax.ShapeDtypeStruct((t, d), F32), jax.ShapeDtypeStruct((t, d), BF16),
                   jax.ShapeDtypeStruct((d // 128, 128), F32), jax.ShapeDtypeStruct((d // 128, 128), F32)]
        + [jax.ShapeDtypeStruct(p.shape, p.dtype) for p in chip_parts],
        in_specs=[wide, wide, whole, whole, row, gain, row, row, gain] + [ANY] * na,
        out_specs=[row, row, gain_rows, gain_rows] + [ANY] * na,
        scratch_shapes=_chip_exchange_scratch(chip_parts),
        compiler_params=_params(collective_id=CHIP_COLLECTIVE_ID),
    )(dgate, dup, gate_t, up_t, x1, g_ffn, dy, mix, g_mix, *chip_parts)
    return out[:4], out[4:]


def outproj_bwd(dmix, w_out, cat):
    t, d = dmix.shape
    m = w_out.shape[0]
    r = m // N_DEV
    tm = TOKEN_TILE
    last = t // tm - 1

    def body(dm_ref, w_ref, cat_ref, dcat_ref, dw_ref, acc, stage, landed, send_sems, recv_sems):
        k = pl.program_id(0)

        @pl.when(k == 0)
        def _():
            acc[...] = jnp.zeros_like(acc)

        dcat_ref[...] = _dot_nt(dm_ref[...], w_ref[...])
        acc[...] += _dot_tn(cat_ref[...], dm_ref[...])

        @pl.when(k == last)
        def _():
            blocks = acc[...].reshape(N_CHIP, 2, r, d)
            for chip in range(N_CHIP):
                for core in range(2):
                    stage[core, chip] = blocks[chip, core].astype(BF16)
            start, finish = _pair_plan([stage], [landed], send_sems, recv_sems)
            start()
            finish()
            mine = stage[lax.axis_index("c")]
            dw_ref[...] = (mine.astype(F32) + landed[...].astype(F32)).astype(BF16)

    tile = lambda width: pl.BlockSpec((tm, width), lambda k: (k, 0))
    return pl.pallas_call(
        body, name="outproj_bwd", grid=(t // tm,),
        out_shape=[jax.ShapeDtypeStruct((t, m), F32), jax.ShapeDtypeStruct((N_CHIP, r, d), BF16)],
        in_specs=[tile(d), pl.BlockSpec(w_out.shape, lambda k: (0, 0)), tile(m)],
        out_specs=[tile(m), pl.BlockSpec((N_CHIP, r, d), lambda k: (0, 0, 0))],
        scratch_shapes=[pltpu.VMEM((m, d), F32), pltpu.VMEM((2, N_CHIP, r, d), BF16), pltpu.VMEM((N_CHIP, r, d), BF16),
                        pltpu.SemaphoreType.DMA((1,)), pltpu.SemaphoreType.DMA((1,))],
        compiler_params=_params(collective_id=PAIR_COLLECTIVE_ID),
    )(dmix, w_out, cat)


def mixers_bwd(proj, dcat, pooled, probs, p_sinks, w_pool, pool_scale, bucket, ffn_parts):
    t = proj.shape[0]
    nb = t // BLOCK
    na = len(ffn_parts)

    def body(*refs):
        (q_ref, kvc_ref, kvp_ref, dcat_ref, pooled_ref, p_all, psink_ref, wp_ref, sc_ref, bk_ref) = refs[:10]
        part_refs = refs[10:10 + na]
        dproj_ref, dbias_ref, dsink_ref, dwp_ref, dsc_ref, drb_ref = refs[10 + na:16 + na]
        slot_refs = refs[16 + na:16 + 2 * na]
        dbuf, c_u, c_q, c_kv, dp_all, ds_all, sink_acc = refs[16 + 2 * na:23 + 2 * na]
        send_sems, recv_sems, local_sems = refs[23 + 2 * na:26 + 2 * na]
        bounce = refs[26 + 2 * na:]
        i = pl.program_id(0)
        lane = lax.broadcasted_iota(jnp.int32, (1, 128), 1)
        start, finish = _chip_exchange_plan(part_refs, slot_refs, send_sems, recv_sems, local_sems, bounce)

        @pl.when(i == 0)
        def _():
            start()
            dbias_ref[...] = jnp.zeros_like(dbias_ref)
            dwp_ref[...] = jnp.zeros_like(dwp_ref)
            dsc_ref[...] = jnp.zeros_like(dsc_ref)
            dsink_ref[...] = jnp.zeros_like(dsink_ref)
            dbuf[...] = jnp.zeros_like(dbuf)
            c_u[...] = jnp.zeros_like(c_u)
            c_q[...] = jnp.zeros_like(c_q)
            c_kv[...] = jnp.zeros_like(c_kv)

        @pl.when(i < nb)
        def _():
            for g, w in enumerate(POOL_WINDOWS):
                cols = slice(g * POOL_GROUP_DIM, (g + 1) * POOL_GROUP_DIM)
                pooled = pooled_ref[:, cols]
                mixed = _dot(pooled, wp_ref[g])
                dout = dcat_ref[:, cols]
                dsc_ref[g:g + 1, :] += jnp.sum(dout * mixed, axis=0, keepdims=True)
                dmixed = (dout * sc_ref[:, cols]).astype(BF16)
                dwp_ref[g] += _dot_tn(pooled, dmixed)
                dpooled = _dot_nt(dmixed, wp_ref[g])
                scaled = dpooled * _inv_count(i, w)
                dbuf[BLOCK:, cols] = scaled[0:HALO]
                dproj_ref[:, cols] = (_window_sum(dbuf, g, w, lambda k: k) + c_u[:, cols]).astype(BF16)
                dbuf[0:BLOCK, cols] = scaled
                c_u[:, cols] = -dpooled

            kv = jnp.concatenate([kvp_ref[...], kvc_ref[...]], axis=0)
            k_var = _head_variants(kv[:, 0:2 * HEAD_DIM])
            v_var = _head_variants(kv[:, 2 * HEAD_DIM:])
            q2s = [q_ref[:, 2 * HEAD_DIM * j:2 * HEAD_DIM * (j + 1)].astype(BF16) for j in range(N_Q_HEADS // 2)]
            do2s = [dcat_ref[:, POOL_WIDTH + 2 * HEAD_DIM * j:POOL_WIDTH + 2 * HEAD_DIM * (j + 1)].astype(BF16)
                    for j in range(N_Q_HEADS // 2)]
            slot = _head_slot
            for hq in range(N_Q_HEADS):
                j, half, h = hq // 2, hq % 2, hq // GQA_GROUP
                dp_all[hq] = _dot_nt(do2s[j], v_var[h][half])
            sink_acc[...] = jnp.zeros_like(sink_acc)
            for hq in range(N_Q_HEADS):
                for r in range(0, BLOCK, ROW_CHUNK):
                    rows = slice(r, r + ROW_CHUNK)
                    probs = p_all[slot(hq), rows, :].astype(F32)
                    dp = dp_all[hq, rows, :]
                    delta = jnp.sum(probs * dp, axis=-1, keepdims=True)
                    ds = probs * (dp - delta)
                    dbias_ref[hq, rows, :] += ds
                    sink_acc[rows, :] += jnp.where(lane == hq, psink_ref[rows, :], 0.0) * delta
                    ds_all[slot(hq), rows, :] = (ds * ATTN_SCALE).astype(BF16)
            dsink_ref[...] -= jnp.sum(sink_acc[...], axis=0, keepdims=True)
            dq2 = [None] * (N_Q_HEADS // 2)
            for hq in range(N_Q_HEADS):
                j, half, h = hq // 2, hq % 2, hq // GQA_GROUP
                dq = _dot(ds_all[slot(hq)], k_var[h][half])
                dq2[j] = dq if dq2[j] is None else dq2[j] + dq
            low = lax.broadcasted_iota(jnp.int32, (2 * BLOCK, 2 * HEAD_DIM), 1) < HEAD_DIM
            dk_half, dv_half = [[None, None], [None, None]], [[None, None], [None, None]]
            for h in range(N_KV_HEADS):
                for half in range(2):
                    heads = [hq for hq in range(GQA_GROUP * h, GQA_GROUP * (h + 1)) if hq % 2 == half]
                    base = slot(heads[0])
                    q_rows = jnp.concatenate([q2s[hq // 2] for hq in heads], axis=0)
                    do_rows = jnp.concatenate([do2s[hq // 2] for hq in heads], axis=0)
                    dk_half[h][half] = _dot_tn(_merge_rows(ds_all[base:base + 2]), q_rows)
                    dv_half[h][half] = _dot_tn(_merge_rows(p_all[base:base + 2]), do_rows)

            def pair_of(halves):
                return jnp.where(low, halves[0][0] + pltpu.roll(halves[0][1], HEAD_DIM, 1),
                                 halves[1][1] + pltpu.roll(halves[1][0], HEAD_DIM, 1))

            dkv = jnp.concatenate([pair_of(dk_half), pair_of(dv_half)], axis=1)
            dproj_ref[:, POOL_WIDTH:2 * POOL_WIDTH] = c_q[...].astype(BF16)
            dproj_ref[:, 2 * POOL_WIDTH:] = (c_kv[...] + dkv[0:BLOCK]).astype(BF16)
            c_q[...] = jnp.concatenate(dq2, axis=1)
            c_kv[...] = dkv[BLOCK:]

        @pl.when(i == nb)
        def _():
            dbuf[BLOCK:, :] = jnp.zeros((HALO, POOL_WIDTH), F32)
            for g, w in enumerate(POOL_WINDOWS):
                cols = slice(g * POOL_GROUP_DIM, (g + 1) * POOL_GROUP_DIM)
                dproj_ref[:, cols] = (_window_sum(dbuf, g, w, lambda k: k) + c_u[:, cols]).astype(BF16)
            dproj_ref[:, POOL_WIDTH:2 * POOL_WIDTH] = c_q[...].astype(BF16)
            dproj_ref[:, 2 * POOL_WIDTH:] = c_kv[...].astype(BF16)
            bk = bk_ref[...]
            for h in range(N_Q_HEADS):
                db = dbias_ref[h]
                for b in range(N_BUCKETS):
                    drb_ref[h, b] = jnp.sum(jnp.where(bk == float(b), db, 0.0))
            finish()

    cur = lambda i: jnp.minimum(i, nb - 1)
    prv = lambda i: jnp.maximum(jnp.minimum(i, nb - 1) - 1, 0)
    out = pl.pallas_call(
        body, name="mixers_bwd", grid=(nb + 1,),
        out_shape=[jax.ShapeDtypeStruct((t, proj.shape[1]), BF16),
                   jax.ShapeDtypeStruct((N_Q_HEADS, BLOCK, 2 * BLOCK), F32),
                   jax.ShapeDtypeStruct((1, 128), F32),
                   jax.ShapeDtypeStruct((4, POOL_GROUP_DIM, POOL_GROUP_DIM), F32),
                   jax.ShapeDtypeStruct((len(POOL_WINDOWS), POOL_GROUP_DIM), F32),
                   jax.ShapeDtypeStruct((N_Q_HEADS, N_BUCKETS), F32)]
        + [jax.ShapeDtypeStruct(p.shape, p.dtype) for p in ffn_parts],
        in_specs=_mixer_in_specs(cur, prv)[2:]
        + [pl.BlockSpec((BLOCK, 2 * POOL_WIDTH), lambda i: (cur(i), 0)), pl.BlockSpec((BLOCK, POOL_WIDTH), lambda i: (cur(i), 0)),
           pl.BlockSpec((N_Q_HEADS, BLOCK, 2 * BLOCK), lambda i: (0, cur(i), 0)), pl.BlockSpec((BLOCK, 128), lambda i: (cur(i), 0))]
        + _mixer_param_specs() + [pl.BlockSpec((BLOCK, 2 * BLOCK), lambda i: (0, 0))] + [ANY] * na,
        out_specs=[pl.BlockSpec((BLOCK, proj.shape[1]), lambda i: (jnp.maximum(i - 1, 0), 0)),
                   pl.BlockSpec((N_Q_HEADS, BLOCK, 2 * BLOCK), lambda i: (0, 0, 0)),
                   pl.BlockSpec((1, 128), lambda i: (0, 0)),
                   pl.BlockSpec((4, POOL_GROUP_DIM, POOL_GROUP_DIM), lambda i: (0, 0, 0)),
                   pl.BlockSpec((len(POOL_WINDOWS), POOL_GROUP_DIM), lambda i: (0, 0)), SMEM] + [ANY] * na,
        scratch_shapes=[pltpu.VMEM((BLOCK + HALO, POOL_WIDTH), F32),
                        pltpu.VMEM((BLOCK, POOL_WIDTH), F32), pltpu.VMEM((BLOCK, POOL_WIDTH), F32),
                        pltpu.VMEM((BLOCK, 256), F32),
                        pltpu.VMEM((N_Q_HEADS, BLOCK, 2 * BLOCK), F32), pltpu.VMEM((N_Q_HEADS, BLOCK, 2 * BLOCK), BF16),
                        pltpu.VMEM((BLOCK, 128), F32)]
        + _chip_exchange_scratch(ffn_parts),
        compiler_params=_params(collective_id=CHIP_COLLECTIVE_ID),
    )(proj, proj, proj, dcat, pooled, probs, p_sinks, w_pool, pool_scale, bucket, *ffn_parts)
    return out[:6], out[6:]


def inproj_bwd(dproj, w_in_t, x, g, dx1):
    t, d = x.shape
    n = dproj.shape[1]
    tm = TOKEN_TILE

    def body(dp_ref, w_ref, x_ref, g_ref, dx1_ref, dx_ref, dg_ref):
        @pl.when(pl.program_id(0) == 0)
        def _():
            dg_ref[...] = jnp.zeros_like(dg_ref)

        dh = _dot(dp_ref[...], w_ref[...])
        xv = x_ref[...]
        dx, dg_rows = _norm_bwd(dh, xv, _rstd(xv), g_ref[...])
        dx_ref[...] = dx1_ref[...] + dx
        dg_ref[...] += _as_rows(jnp.sum(dg_rows, axis=0, keepdims=True))

    row = pl.BlockSpec((tm, d), lambda i: (i, 0))
    gain = pl.BlockSpec((1, d), lambda i: (0, 0))
    return pl.pallas_call(
        body, name="inproj_bwd", grid=(t // tm,),
        out_shape=[jax.ShapeDtypeStruct((t, d), F32), jax.ShapeDtypeStruct((d // 128, 128), F32)],
        in_specs=[pl.BlockSpec((tm, n), lambda i: (i, 0)), pl.BlockSpec(w_in_t.shape, lambda i: (0, 0)), row, gain, row],
        out_specs=[row, pl.BlockSpec((d // 128, 128), lambda i: (0, 0))],
        compiler_params=_params(),
    )(dproj, w_in_t, x, g, dx1)


def _bucket_band():
    qi = jnp.arange(BLOCK)[:, None]
    kj = jnp.arange(2 * BLOCK)[None, :]
    dist = qi + BLOCK - kj
    n = jnp.maximum(dist, 0)
    nf = jnp.maximum(n, 1).astype(F32)
    large = MAX_EXACT + (jnp.log(nf / MAX_EXACT) / np.float32(np.log(MAX_DISTANCE / MAX_EXACT))
                         * (N_BUCKETS - MAX_EXACT)).astype(jnp.int32)
    large = jnp.minimum(large, N_BUCKETS - 1)
    bucket = jnp.where(n < MAX_EXACT, n, large)
    in_window = (dist >= 0) & (dist < BLOCK)
    return bucket.astype(F32), in_window.astype(F32)


def kernel(x, g_pre_mix, w_in, w_pool, pool_scale, rel_bias, sinks, w_out, g_post_mix, g_pre_ffn, w_gate, w_up, w_down, g_post_ffn, loss_target, m_g_pre_mix, m_w_in, m_w_pool, m_pool_scale, m_rel_bias, m_sinks, m_w_out, m_g_post_mix, m_g_pre_ffn, m_w_gate, m_w_up, m_w_down, m_g_post_ffn, v_g_pre_mix, v_w_in, v_w_pool, v_pool_scale, v_rel_bias, v_sinks, v_w_out, v_g_post_mix, v_g_pre_ffn, v_w_gate, v_w_up, v_w_down, v_g_post_ffn):
    d = x.shape[-1]
    xs, target = x[0], loss_target[0]

    w_in_ts = w_in[0].T.astype(BF16)
    w_out_s = w_out[0].astype(BF16)
    gate_ts = w_gate[0].T.astype(BF16)
    up_ts = w_up[0].T.astype(BF16)
    w_down_s = w_down[0].astype(BF16)

    bucket, in_window = _bucket_band()
    w_pool_b = w_pool[0].astype(BF16)
    half = up_ts.shape[0] // 2
    proj, h1, w_in_t, up_t = norm_inproj(xs, g_pre_mix, w_in_ts, up_ts[:half], up_ts.shape[0])
    w_in_t = w_in_t.reshape(-1, d)
    cat, pooled, probs, p_sinks, gate_t, w_out_f = mixers_fwd(
        proj, bucket, in_window, rel_bias, sinks, w_pool_b, pool_scale, [gate_ts, w_out_s])
    w_out_f = w_out_f.reshape(-1, d)
    mix, x1, h2, up_t = outproj_norm(cat, w_out_f, xs, g_post_mix, g_pre_ffn, up_ts[half:], up_t)
    gate, up, act, w_down_f = ffn_up(h2, gate_t, up_t, w_down_s)
    df, dy, dg_post_ffn, loss_part = ffn_down_loss(act, w_down_f, x1, g_post_ffn, target)

    (dgate, dup), (d_gate, d_up) = ffn_down_bwd(df, w_down_f, gate, up, h2)
    d_down, got_gate_up = grad_w_down(act, df, [d_gate, d_up])
    q_gate, q_up, q_down = pair_add_exchange([d_gate, d_up], got_gate_up, d_down, "pair_add_ffn")
    (dx1, dmix, dg_pre_ffn, dg_post_mix), (gate_slots, down_slots) = ffn_up_bwd(
        dgate, dup, gate_t, up_t, x1, g_pre_ffn, dy, mix, g_post_mix, [q_gate, q_down])
    dcat, q_out = outproj_bwd(dmix, w_out_f, cat)
    (dproj, _, dsinks, dw_pool, dpool_scale, drel_bias), (up_slots, out_slots) = mixers_bwd(
        proj, dcat, pooled, probs, p_sinks, w_pool_b, pool_scale, bucket, [q_up, q_out])
    grad_x, dg_pre_mix = inproj_bwd(dproj, w_in_t, xs, g_pre_mix, dx1)

    small_w = [g_pre_mix, g_post_mix, g_pre_ffn, g_post_ffn, pool_scale, sinks, w_pool, rel_bias.T]
    small_m = [m_g_pre_mix, m_g_post_mix, m_g_pre_ffn, m_g_post_ffn, m_pool_scale, m_sinks, m_w_pool, m_rel_bias.T]
    small_v = [v_g_pre_mix, v_g_post_mix, v_g_pre_ffn, v_g_post_ffn, v_pool_scale, v_sinks, v_w_pool, v_rel_bias.T]
    d_in_t, total, total_rb = grad_w_in_small_reduce(
        dproj, h1, [dg_pre_mix, dg_post_mix, dg_pre_ffn, dg_post_ffn], dpool_scale, dsinks, loss_part, dw_pool, drel_bias)
    g_in_t = reduce_w_in(d_in_t)
    loss_row, sm = small_adamw(total, total_rb, small_w, small_m, small_v)
    sm[7] = [r.T for r in sm[7]]
    big_w = [w_in[0].T, w_out[0], w_gate[0].T, w_up[0].T, w_down[0]]
    big_m = [m_w_in[0].T, m_w_out[0], m_w_gate[0].T, m_w_up[0].T, m_w_down[0]]
    big_v = [v_w_in[0].T, v_w_out[0], v_w_gate[0].T, v_w_up[0].T, v_w_down[0]]
    upd = sum_adamw([g_in_t[None], out_slots, gate_slots, up_slots, down_slots], big_w, big_m, big_v, "sum_adamw")
    back = lambda k, a: (a.T if k in (0, 2, 3) else a)[None]
    big = [[back(k, u) for u in upd[k]] for k in range(5)]

    def ordered(kind):
        s, b = [p[kind] for p in sm], [p[kind] for p in big]
        return [s[0], b[0], s[6], s[4], s[7], s[5], b[1], s[1], s[2], b[2], b[3], b[4], s[3]]

    return (loss_row[0, 0], grad_x[None], *ordered(0), *ordered(1), *ordered(2), *ordered(3))
```
